```python
import jax, jax.numpy as jnp
from jax import lax
import numpy as np

D_MODEL = 1024
BATCH = 8
SEQ = 8192
DEPTH = 1

D_CONV = D_MODEL
CONV_WIDTH = 3
HEAD_DIM = 64
HEADS_PER_GROUP = 8
GROUPS = ((128, 1), (512, 4), (2048, 16))
N_GROUPS = len(GROUPS)
N_ATT_HEADS = N_GROUPS * HEADS_PER_GROUP
ATT_GROUP_W = HEADS_PER_GROUP * HEAD_DIM
ATT_QKV_W = N_GROUPS * ATT_GROUP_W
D_FF = 2816
LN_EPS = 1e-5
ALPHA = (2.0 * DEPTH) ** 0.25
BETA = (8.0 * DEPTH) ** -0.25
MASK_VALUE = -1e30

OFF_B = 0
OFF_C = OFF_B + D_CONV
OFF_H = OFF_C + D_CONV
OFF_Q = OFF_H + D_CONV
OFF_K = OFF_Q + ATT_QKV_W
OFF_V = OFF_K + ATT_QKV_W
OFF_GA = OFF_V + ATT_QKV_W
OFF_GB = OFF_GA + D_MODEL
N_IN = OFF_GB + D_MODEL

kernel_name = "hybrid_shortconv_dilated_alibi_deepnorm_encoder"


def layer_norm(x, g, b):
    xf = x.astype(jnp.float32)
    mu = jnp.mean(xf, -1, keepdims=True)
    xc = xf - mu
    var = jnp.mean(xc * xc, -1, keepdims=True)
    return (xc * lax.rsqrt(var + LN_EPS) * g + b).astype(x.dtype)


def dwconv3(u, w):
    up = jnp.pad(u, ((0, 0), (1, 1), (0, 0)))
    return up[:, :-2] * w[0] + up[:, 1:-1] * w[1] + up[:, 2:] * w[2]


def alibi_slopes(n):
    return jnp.exp2(-8.0 * jnp.arange(1, n + 1, dtype=jnp.float32) / n)


def dilated_window_attention(q, k, v, dil, radius, slopes):
    bsz, seq, nh, hd = q.shape
    sub_len = seq // dil
    blk = radius
    nb = -(-sub_len // blk)
    lp = nb * blk

    def to_sub(a):
        a = a.reshape(bsz, sub_len, dil, nh, hd).transpose(0, 2, 1, 3, 4)
        return a.reshape(bsz * dil, sub_len, nh, hd)

    qs, ks, vs = to_sub(q), to_sub(k), to_sub(v)
    qb = jnp.pad(qs, ((0, 0), (0, lp - sub_len), (0, 0), (0, 0))).reshape(bsz * dil, nb, blk, nh, hd)

    def windows(a):
        ap = jnp.pad(a, ((0, 0), (blk, lp - sub_len + blk), (0, 0), (0, 0)))
        ap = ap.reshape(bsz * dil, nb + 2, blk, nh, hd)
        return jnp.concatenate([ap[:, :-2], ap[:, 1:-1], ap[:, 2:]], axis=2)

    kw, vw = windows(ks), windows(vs)
    qpos = jnp.arange(lp).reshape(nb, blk)
    kpos = (jnp.arange(nb)[:, None] - 1) * blk + jnp.arange(3 * blk)[None, :]
    rel = kpos[:, None, :] - qpos[:, :, None]
    valid = (jnp.abs(rel) <= radius) & (kpos[:, None, :] >= 0) & (kpos[:, None, :] < sub_len)
    dist = (jnp.abs(rel) * dil).astype(jnp.float32)
    bias = -slopes[None, :, None, None] * dist[:, None]
    s = jnp.einsum('bnqhd,bnkhd->bnhqk', qb, kw).astype(jnp.float32) * (hd ** -0.5) + bias
    s = jnp.where(valid[:, None], s, MASK_VALUE)
    m = jnp.max(s, -1, keepdims=True)
    p = jnp.exp(s - m)
    den = jnp.sum(p, -1, keepdims=True)
    o = jnp.einsum('bnhqk,bnkhd->bnqhd', (p / den).astype(v.dtype), vw)
    lse = jnp.transpose((m + jnp.log(den))[..., 0], (0, 1, 3, 2))
    o = o.reshape(bsz, dil, lp, nh, hd)[:, :, :sub_len].transpose(0, 2, 1, 3, 4).reshape(bsz, seq, nh, hd)
    lse = lse.reshape(bsz, dil, lp, nh)[:, :, :sub_len].transpose(0, 2, 1, 3).reshape(bsz, seq, nh)
    return o, lse


def _fwd_setup_inputs(seed: int = 0) -> dict:
    key = jax.random.key(seed)
    ks = jax.random.split(key, 24)

    def nrm(k, shape, scale):
        return jax.random.normal(k, shape, jnp.float32) * scale

    col_scale = np.ones((N_IN,), np.float32)
    col_scale[OFF_H:OFF_H + D_CONV] = BETA
    col_scale[OFF_V:OFF_V + ATT_QKV_W] = BETA
    L = DEPTH
    return {
        "x": nrm(ks[0], (BATCH, SEQ, D_MODEL), 1.0),
        "ln0_g": 1.0 + nrm(ks[1], (D_MODEL,), 0.02),
        "ln0_b": nrm(ks[2], (D_MODEL,), 0.02),
        "w_in": nrm(ks[3], (L, D_MODEL, N_IN), D_MODEL ** -0.5) * jnp.asarray(col_scale),
        "b_in": nrm(ks[4], (L, N_IN), 0.02),
        "conv_w": nrm(ks[5], (L, CONV_WIDTH, D_CONV), CONV_WIDTH ** -0.5),
        "w_a": nrm(ks[6], (L, D_CONV, D_MODEL), BETA * D_CONV ** -0.5),
        "w_b": nrm(ks[7], (L, ATT_GROUP_W, D_MODEL), BETA * ATT_GROUP_W ** -0.5),
        "w_o": nrm(ks[8], (L, D_MODEL, D_MODEL), BETA * D_MODEL ** -0.5),
        "b_o": nrm(ks[9], (L, D_MODEL), 0.02),
        "ln1_g": 1.0 + nrm(ks[10], (L, D_MODEL), 0.02),
        "ln1_b": nrm(ks[11], (L, D_MODEL), 0.02),
        "w_up": nrm(ks[12], (L, D_MODEL, 2 * D_FF), BETA * D_MODEL ** -0.5),
        "b_up": nrm(ks[13], (L, 2 * D_FF), 0.02),
        "ffn_conv_w": nrm(ks[14], (L, CONV_WIDTH, D_FF), CONV_WIDTH ** -0.5),
        "ffn_conv_b": nrm(ks[15], (L, D_FF), 0.02),
        "w_down": nrm(ks[16], (L, D_FF, D_MODEL), BETA * D_FF ** -0.5),
        "b_down": nrm(ks[17], (L, D_MODEL), 0.02),
        "ln2_g": 1.0 + nrm(ks[18], (L, D_MODEL), 0.02),
        "ln2_b": nrm(ks[19], (L, D_MODEL), 0.02),
    }


def _fwd_reference(x, ln0_g, ln0_b, w_in, b_in, conv_w, w_a, w_b, w_o, b_o, ln1_g, ln1_b,
              w_up, b_up, ffn_conv_w, ffn_conv_b, w_down, b_down, ln2_g, ln2_b):
    bsz, seq, _ = x.shape
    slopes = alibi_slopes(N_ATT_HEADS).reshape(N_GROUPS, HEADS_PER_GROUP)
    h = layer_norm(x, ln0_g, ln0_b)
    for l in range(DEPTH):
        proj = h @ w_in[l] + b_in[l]
        gate_b = proj[..., OFF_B:OFF_B + D_CONV]
        gate_c = proj[..., OFF_C:OFF_C + D_CONV]
        hin = proj[..., OFF_H:OFF_H + D_CONV]
        y_a = (gate_b * dwconv3(gate_c * hin, conv_w[l])) @ w_a[l]
        q = proj[..., OFF_Q:OFF_Q + ATT_QKV_W].reshape(bsz, seq, N_GROUPS, HEADS_PER_GROUP, HEAD_DIM)
        k = proj[..., OFF_K:OFF_K + ATT_QKV_W].reshape(bsz, seq, N_GROUPS, HEADS_PER_GROUP, HEAD_DIM)
        v = proj[..., OFF_V:OFF_V + ATT_QKV_W].reshape(bsz, seq, N_GROUPS, HEADS_PER_GROUP, HEAD_DIM)
        outs, lses = [], []
        for g, (window, dil) in enumerate(GROUPS):
            o, lse = dilated_window_attention(q[:, :, g], k[:, :, g], v[:, :, g], dil,
                                              window // (2 * dil), slopes[g])
            outs.append(o)
            lses.append(lse)
        wts = jax.nn.softmax(jnp.stack(lses, 0), axis=0)
        comb = jnp.sum(wts[..., None].astype(x.dtype) * jnp.stack(outs, 0), axis=0)
        y_b = comb.reshape(bsz, seq, ATT_GROUP_W) @ w_b[l]
        g_a = jax.nn.sigmoid(proj[..., OFF_GA:OFF_GA + D_MODEL])
        g_b = jax.nn.sigmoid(proj[..., OFF_GB:OFF_GB + D_MODEL])
        mix = (g_a * y_a + g_b * y_b) @ w_o[l] + b_o[l]
        h = layer_norm(ALPHA * h + mix, ln1_g[l], ln1_b[l])
        up = h @ w_up[l] + b_up[l]
        a, gte = up[..., :D_FF], up[..., D_FF:]
        f = jax.nn.gelu(dwconv3(a, ffn_conv_w[l]) + ffn_conv_b[l], approximate=False) * gte
        ffn = f @ w_down[l] + b_down[l]
        h = layer_norm(ALPHA * h + ffn, ln2_g[l], ln2_b[l])
    return h


import jax as _jax
import jax.numpy as _jnp

TWIN_FORMAT = 'train_step'
FWD_PARAMS = ['x', 'ln0_g', 'ln0_b', 'w_in', 'b_in', 'conv_w', 'w_a', 'w_b', 'w_o', 'b_o', 'ln1_g', 'ln1_b', 'w_up', 'b_up', 'ffn_conv_w', 'ffn_conv_b', 'w_down', 'b_down', 'ln2_g', 'ln2_b']
TWIN_WEIGHTS = ['ln0_g', 'ln0_b', 'w_in', 'b_in', 'conv_w', 'w_a', 'w_b', 'w_o', 'b_o', 'ln1_g', 'ln1_b', 'w_up', 'b_up', 'ffn_conv_w', 'ffn_conv_b', 'w_down', 'b_down', 'ln2_g', 'ln2_b']
TWIN_DIFF_INPUT = 'x'
TWIN_INPUTS = ['x', 'ln0_g', 'ln0_b', 'w_in', 'b_in', 'conv_w', 'w_a', 'w_b', 'w_o', 'b_o', 'ln1_g', 'ln1_b', 'w_up', 'b_up', 'ffn_conv_w', 'ffn_conv_b', 'w_down', 'b_down', 'ln2_g', 'ln2_b', 'loss_target', 'm_ln0_g', 'm_ln0_b', 'm_w_in', 'm_b_in', 'm_conv_w', 'm_w_a', 'm_w_b', 'm_w_o', 'm_b_o', 'm_ln1_g', 'm_ln1_b', 'm_w_up', 'm_b_up', 'm_ffn_conv_w', 'm_ffn_conv_b', 'm_w_down', 'm_b_down', 'm_ln2_g', 'm_ln2_b', 'v_ln0_g', 'v_ln0_b', 'v_w_in', 'v_b_in', 'v_conv_w', 'v_w_a', 'v_w_b', 'v_w_o', 'v_b_o', 'v_ln1_g', 'v_ln1_b', 'v_w_up', 'v_b_up', 'v_ffn_conv_w', 'v_ffn_conv_b', 'v_w_down', 'v_b_down', 'v_ln2_g', 'v_ln2_b']
TWIN_OUTPUTS = ['loss', 'grad_x', 'grad_ln0_g', 'grad_ln0_b', 'grad_w_in', 'grad_b_in', 'grad_conv_w', 'grad_w_a', 'grad_w_b', 'grad_w_o', 'grad_b_o', 'grad_ln1_g', 'grad_ln1_b', 'grad_w_up', 'grad_b_up', 'grad_ffn_conv_w', 'grad_ffn_conv_b', 'grad_w_down', 'grad_b_down', 'grad_ln2_g', 'grad_ln2_b', 'delta_ln0_g', 'delta_ln0_b', 'delta_w_in', 'delta_b_in', 'delta_conv_w', 'delta_w_a', 'delta_w_b', 'delta_w_o', 'delta_b_o', 'delta_ln1_g', 'delta_ln1_b', 'delta_w_up', 'delta_b_up', 'delta_ffn_conv_w', 'delta_ffn_conv_b', 'delta_w_down', 'delta_b_down', 'delta_ln2_g', 'delta_ln2_b', 'new_m_ln0_g', 'new_m_ln0_b', 'new_m_w_in', 'new_m_b_in', 'new_m_conv_w', 'new_m_w_a', 'new_m_w_b', 'new_m_w_o', 'new_m_b_o', 'new_m_ln1_g', 'new_m_ln1_b', 'new_m_w_up', 'new_m_b_up', 'new_m_ffn_conv_w', 'new_m_ffn_conv_b', 'new_m_w_down', 'new_m_b_down', 'new_m_ln2_g', 'new_m_ln2_b', 'new_v_ln0_g', 'new_v_ln0_b', 'new_v_w_in', 'new_v_b_in', 'new_v_conv_w', 'new_v_w_a', 'new_v_w_b', 'new_v_w_o', 'new_v_b_o', 'new_v_ln1_g', 'new_v_ln1_b', 'new_v_w_up', 'new_v_b_up', 'new_v_ffn_conv_w', 'new_v_ffn_conv_b', 'new_v_w_down', 'new_v_b_down', 'new_v_ln2_g', 'new_v_ln2_b']
TWIN_LEAF_KINDS = {'loss': 'loss', 'grad_x': 'grad_x', 'grad_ln0_g': 'grad_w', 'grad_ln0_b': 'grad_w', 'grad_w_in': 'grad_w', 'grad_b_in': 'grad_w', 'grad_conv_w': 'grad_w', 'grad_w_a': 'grad_w', 'grad_w_b': 'grad_w', 'grad_w_o': 'grad_w', 'grad_b_o': 'grad_w', 'grad_ln1_g': 'grad_w', 'grad_ln1_b': 'grad_w', 'grad_w_up': 'grad_w', 'grad_b_up': 'grad_w', 'grad_ffn_conv_w': 'grad_w', 'grad_ffn_conv_b': 'grad_w', 'grad_w_down': 'grad_w', 'grad_b_down': 'grad_w', 'grad_ln2_g': 'grad_w', 'grad_ln2_b': 'grad_w', 'delta_ln0_g': 'delta_w', 'delta_ln0_b': 'delta_w', 'delta_w_in': 'delta_w', 'delta_b_in': 'delta_w', 'delta_conv_w': 'delta_w', 'delta_w_a': 'delta_w', 'delta_w_b': 'delta_w', 'delta_w_o': 'delta_w', 'delta_b_o': 'delta_w', 'delta_ln1_g': 'delta_w', 'delta_ln1_b': 'delta_w', 'delta_w_up': 'delta_w', 'delta_b_up': 'delta_w', 'delta_ffn_conv_w': 'delta_w', 'delta_ffn_conv_b': 'delta_w', 'delta_w_down': 'delta_w', 'delta_b_down': 'delta_w', 'delta_ln2_g': 'delta_w', 'delta_ln2_b': 'delta_w', 'new_m_ln0_g': 'new_m', 'new_m_ln0_b': 'new_m', 'new_m_w_in': 'new_m', 'new_m_b_in': 'new_m', 'new_m_conv_w': 'new_m', 'new_m_w_a': 'new_m', 'new_m_w_b': 'new_m', 'new_m_w_o': 'new_m', 'new_m_b_o': 'new_m', 'new_m_ln1_g': 'new_m', 'new_m_ln1_b': 'new_m', 'new_m_w_up': 'new_m', 'new_m_b_up': 'new_m', 'new_m_ffn_conv_w': 'new_m', 'new_m_ffn_conv_b': 'new_m', 'new_m_w_down': 'new_m', 'new_m_b_down': 'new_m', 'new_m_ln2_g': 'new_m', 'new_m_ln2_b': 'new_m', 'new_v_ln0_g': 'new_v', 'new_v_ln0_b': 'new_v', 'new_v_w_in': 'new_v', 'new_v_b_in': 'new_v', 'new_v_conv_w': 'new_v', 'new_v_w_a': 'new_v', 'new_v_w_b': 'new_v', 'new_v_w_o': 'new_v', 'new_v_b_o': 'new_v', 'new_v_ln1_g': 'new_v', 'new_v_ln1_b': 'new_v', 'new_v_w_up': 'new_v', 'new_v_b_up': 'new_v', 'new_v_ffn_conv_w': 'new_v', 'new_v_ffn_conv_b': 'new_v', 'new_v_w_down': 'new_v', 'new_v_b_down': 'new_v', 'new_v_ln2_g': 'new_v', 'new_v_ln2_b': 'new_v'}


def _forward(args):
    return _fwd_reference(*[args[k] for k in FWD_PARAMS])


def _output_shape():
    def fwd():
        inp = _fwd_setup_inputs(0)
        return _fwd_reference(*[inp[k] for k in FWD_PARAMS])
    out = _jax.eval_shape(fwd)
    return out.shape, out.dtype

N_MICROBATCH = 1
ADAM_LR = 0.001
ADAM_B1 = 0.9
ADAM_B2 = 0.999
ADAM_EPS = 1e-08
ADAM_WD = 0.01
ADAM_STEP = 10
PER_EXAMPLE_BATCH_AXIS = {'x': 0, 'loss_target': 0}
SHARED_INPUTS = []
_WEIGHT_DTYPES = {'ln0_g': _jnp.float32, 'ln0_b': _jnp.float32, 'w_in': _jnp.float32, 'b_in': _jnp.float32, 'conv_w': _jnp.float32, 'w_a': _jnp.float32, 'w_b': _jnp.float32, 'w_o': _jnp.float32, 'b_o': _jnp.float32, 'ln1_g': _jnp.float32, 'ln1_b': _jnp.float32, 'w_up': _jnp.float32, 'b_up': _jnp.float32, 'ffn_conv_w': _jnp.float32, 'ffn_conv_b': _jnp.float32, 'w_down': _jnp.float32, 'b_down': _jnp.float32, 'ln2_g': _jnp.float32, 'ln2_b': _jnp.float32}
MOMENT_SCALE = {'ln0_g': 2.181933e+00, 'ln0_b': 1.056731e+00, 'w_in': 1.772320e-02, 'b_in': 2.852274e-02, 'conv_w': 2.362090e-02, 'w_a': 4.040961e-02, 'w_b': 1.018119e-02, 'w_o': 4.142663e-02, 'b_o': 8.898981e-01, 'ln1_g': 2.214496e+00, 'ln1_b': 1.065091e+00, 'w_up': 2.794948e-02, 'b_up': 3.543957e-02, 'ffn_conv_w': 1.740022e-02, 'ffn_conv_b': 2.708824e-02, 'w_down': 4.541286e-02, 'b_down': 8.889055e-01, 'ln2_g': 6.408164e+01, 'ln2_b': 2.277281e+00}


def _to_microbatches(a, axis):
    t = _jnp.moveaxis(a, axis, 0)
    t = t.reshape((N_MICROBATCH, t.shape[0] // N_MICROBATCH) + t.shape[1:])
    return _jnp.moveaxis(t, 1, axis + 1)


def setup_inputs(seed: int = 0) -> dict:
    inp = _fwd_setup_inputs(seed)
    key = _jax.random.fold_in(_jax.random.key(seed), 7919)
    shape, _ = _output_shape()
    out = dict(inp)
    out["loss_target"] = _jax.random.normal(_jax.random.fold_in(key, 0), shape, _jnp.float32)
    for i, name in enumerate(TWIN_WEIGHTS):
        w = inp[name].astype(_jnp.float32)
        if MOMENT_SCALE is None:
            s = _jnp.sqrt(_jnp.mean(_jnp.square(w)) + 1e-30)
        else:
            s = MOMENT_SCALE[name]
        km, kv = _jax.random.split(_jax.random.fold_in(key, i + 1))
        out[name] = w
        out["m_" + name] = s * _jax.random.normal(km, w.shape, _jnp.float32)
        out["v_" + name] = (s * s) * _jax.random.uniform(kv, w.shape, _jnp.float32, 0.5, 1.5)
    if N_MICROBATCH > 1:
        for name, axis in PER_EXAMPLE_BATCH_AXIS.items():
            out[name] = _to_microbatches(out[name], axis)
    return {'x': out['x'], 'ln0_g': out['ln0_g'], 'ln0_b': out['ln0_b'], 'w_in': out['w_in'], 'b_in': out['b_in'], 'conv_w': out['conv_w'], 'w_a': out['w_a'], 'w_b': out['w_b'], 'w_o': out['w_o'], 'b_o': out['b_o'], 'ln1_g': out['ln1_g'], 'ln1_b': out['ln1_b'], 'w_up': out['w_up'], 'b_up': out['b_up'], 'ffn_conv_w': out['ffn_conv_w'], 'ffn_conv_b': out['ffn_conv_b'], 'w_down': out['w_down'], 'b_down': out['b_down'], 'ln2_g': out['ln2_g'], 'ln2_b': out['ln2_b'], 'loss_target': out['loss_target'], 'm_ln0_g': out['m_ln0_g'], 'm_ln0_b': out['m_ln0_b'], 'm_w_in': out['m_w_in'], 'm_b_in': out['m_b_in'], 'm_conv_w': out['m_conv_w'], 'm_w_a': out['m_w_a'], 'm_w_b': out['m_w_b'], 'm_w_o': out['m_w_o'], 'm_b_o': out['m_b_o'], 'm_ln1_g': out['m_ln1_g'], 'm_ln1_b': out['m_ln1_b'], 'm_w_up': out['m_w_up'], 'm_b_up': out['m_b_up'], 'm_ffn_conv_w': out['m_ffn_conv_w'], 'm_ffn_conv_b': out['m_ffn_conv_b'], 'm_w_down': out['m_w_down'], 'm_b_down': out['m_b_down'], 'm_ln2_g': out['m_ln2_g'], 'm_ln2_b': out['m_ln2_b'], 'v_ln0_g': out['v_ln0_g'], 'v_ln0_b': out['v_ln0_b'], 'v_w_in': out['v_w_in'], 'v_b_in': out['v_b_in'], 'v_conv_w': out['v_conv_w'], 'v_w_a': out['v_w_a'], 'v_w_b': out['v_w_b'], 'v_w_o': out['v_w_o'], 'v_b_o': out['v_b_o'], 'v_ln1_g': out['v_ln1_g'], 'v_ln1_b': out['v_ln1_b'], 'v_w_up': out['v_w_up'], 'v_b_up': out['v_b_up'], 'v_ffn_conv_w': out['v_ffn_conv_w'], 'v_ffn_conv_b': out['v_ffn_conv_b'], 'v_w_down': out['v_w_down'], 'v_b_down': out['v_b_down'], 'v_ln2_g': out['v_ln2_g'], 'v_ln2_b': out['v_ln2_b']}


def _loss(weights, diff, rest, loss_target):
    with _jax.named_scope("forward"):
        args = {**rest, TWIN_DIFF_INPUT: diff, **{k: w.astype(_WEIGHT_DTYPES[k]) for k, w in weights.items()}}
        y = _forward(args)
    with _jax.named_scope("loss_head"):
        err = _jnp.square(y.astype(_jnp.float32) - loss_target)
        return 0.5 * _jnp.sum(_jnp.mean(err, axis=-1)) if err.ndim else 0.5 * err


def _adamw(w, g, m, v):
    m = ADAM_B1 * m + (1.0 - ADAM_B1) * g
    v = ADAM_B2 * v + (1.0 - ADAM_B2) * _jnp.square(g)
    m_hat = m / (1.0 - ADAM_B1 ** ADAM_STEP)
    v_hat = v / (1.0 - ADAM_B2 ** ADAM_STEP)
    delta = -ADAM_LR * (m_hat / (_jnp.sqrt(v_hat) + ADAM_EPS) + ADAM_WD * w)
    return delta, m, v


def reference(x, ln0_g, ln0_b, w_in, b_in, conv_w, w_a, w_b, w_o, b_o, ln1_g, ln1_b, w_up, b_up, ffn_conv_w, ffn_conv_b, w_down, b_down, ln2_g, ln2_b, loss_target, m_ln0_g, m_ln0_b, m_w_in, m_b_in, m_conv_w, m_w_a, m_w_b, m_w_o, m_b_o, m_ln1_g, m_ln1_b, m_w_up, m_b_up, m_ffn_conv_w, m_ffn_conv_b, m_w_down, m_b_down, m_ln2_g, m_ln2_b, v_ln0_g, v_ln0_b, v_w_in, v_b_in, v_conv_w, v_w_a, v_w_b, v_w_o, v_b_o, v_ln1_g, v_ln1_b, v_w_up, v_b_up, v_ffn_conv_w, v_ffn_conv_b, v_w_down, v_b_down, v_ln2_g, v_ln2_b):
    given = dict(x=x, ln0_g=ln0_g, ln0_b=ln0_b, w_in=w_in, b_in=b_in, conv_w=conv_w, w_a=w_a, w_b=w_b, w_o=w_o, b_o=b_o, ln1_g=ln1_g, ln1_b=ln1_b, w_up=w_up, b_up=b_up, ffn_conv_w=ffn_conv_w, ffn_conv_b=ffn_conv_b, w_down=w_down, b_down=b_down, ln2_g=ln2_g, ln2_b=ln2_b, loss_target=loss_target, m_ln0_g=m_ln0_g, m_ln0_b=m_ln0_b, m_w_in=m_w_in, m_b_in=m_b_in, m_conv_w=m_conv_w, m_w_a=m_w_a, m_w_b=m_w_b, m_w_o=m_w_o, m_b_o=m_b_o, m_ln1_g=m_ln1_g, m_ln1_b=m_ln1_b, m_w_up=m_w_up, m_b_up=m_b_up, m_ffn_conv_w=m_ffn_conv_w, m_ffn_conv_b=m_ffn_conv_b, m_w_down=m_w_down, m_b_down=m_b_down, m_ln2_g=m_ln2_g, m_ln2_b=m_ln2_b, v_ln0_g=v_ln0_g, v_ln0_b=v_ln0_b, v_w_in=v_w_in, v_b_in=v_b_in, v_conv_w=v_conv_w, v_w_a=v_w_a, v_w_b=v_w_b, v_w_o=v_w_o, v_b_o=v_b_o, v_ln1_g=v_ln1_g, v_ln1_b=v_ln1_b, v_w_up=v_w_up, v_b_up=v_b_up, v_ffn_conv_w=v_ffn_conv_w, v_ffn_conv_b=v_ffn_conv_b, v_w_down=v_w_down, v_b_down=v_b_down, v_ln2_g=v_ln2_g, v_ln2_b=v_ln2_b)
    weights = {n: given[n] for n in TWIN_WEIGHTS}
    shared = {n: given[n] for n in SHARED_INPUTS}
    per_example = {n: given[n] for n in ['x']}
    grad_fn = _jax.value_and_grad(_loss, argnums=(0, 1))

    def one_microbatch(ex, loss_target):
        ex = dict(ex)
        diff = ex.pop(TWIN_DIFF_INPUT)
        return grad_fn(weights, diff, {**shared, **ex}, loss_target)

    if N_MICROBATCH == 1:
        loss, (grad_w, grad_x) = one_microbatch(per_example, given["loss_target"])
    else:
        def body(carry, xs):
            loss_sum, grad_sum = carry
            l_k, (gw_k, gx_k) = one_microbatch(xs[0], xs[1])
            with _jax.named_scope("update"):
                return (loss_sum + l_k, _jax.tree.map(_jnp.add, grad_sum, gw_k)), gx_k

        init = (_jnp.zeros((), _jnp.float32), _jax.tree.map(_jnp.zeros_like, weights))
        (loss, grad_w), grad_x = _jax.lax.scan(body, init, (per_example, given["loss_target"]))
    with _jax.named_scope("update"):
        delta_w, new_m, new_v = {}, {}, {}
        for n in TWIN_WEIGHTS:
            delta_w[n], new_m[n], new_v[n] = _adamw(weights[n], grad_w[n], given["m_" + n], given["v_" + n])
    return (loss, grad_x, *[grad_w[n] for n in TWIN_WEIGHTS], *[delta_w[n] for n in TWIN_WEIGHTS],
            *[new_m[n] for n in TWIN_WEIGHTS], *[new_v[n] for n in TWIN_WEIGHTS])
```

```python
import functools
import math

import jax
import jax.numpy as jnp
from jax import lax
from jax.experimental import pallas as pl
from jax.experimental.pallas import tpu as pltpu

F32 = jnp.float32
BF16 = jnp.bfloat16

D_MODEL = 1024
D_CONV = D_MODEL
HEAD_DIM = 64
HEADS_PER_GROUP = 8
GROUPS = ((128, 1), (512, 4), (2048, 16))
N_GROUPS = len(GROUPS)
GROUP_W = HEADS_PER_GROUP * HEAD_DIM
QKV_W = N_GROUPS * GROUP_W
RADIUS = 64
D_FF = 2816
LN_EPS = 1e-5
ALPHA = 2.0 ** 0.25
MASK_VALUE = -1e30
ATT_SCALE = HEAD_DIM ** -0.5
OFF_B = 0
OFF_C = OFF_B + D_CONV
OFF_H = OFF_C + D_CONV
OFF_Q = OFF_H + D_CONV
OFF_K = OFF_Q + QKV_W
OFF_V = OFF_K + QKV_W
OFF_GA = OFF_V + QKV_W
OFF_GB = OFF_GA + D_MODEL
N_IN = OFF_GB + D_MODEL
ADAM_LR = 0.001
ADAM_B1 = 0.9
ADAM_B2 = 0.999
ADAM_EPS = 1e-08
ADAM_WD = 0.01
ADAM_STEP = 10
INV_SQRT2 = 0.7071067811865476
INV_SQRT_2PI = 0.3989422804014327

LANES = 128
SUBLANES = 8
VMEM_BYTES_V7X = 64 * 1024 * 1024
N_CHIPS = 4
N_CORES = 2
N_DEV = N_CHIPS * N_CORES
MESH = pl.DeviceIdType.MESH

SLAB = 128
CHUNK = 256
PAD = SUBLANES
TQ = 128


def _cparams(sem, vmem_mb):
    assert vmem_mb * 1024 * 1024 < VMEM_BYTES_V7X
    return pltpu.CompilerParams(dimension_semantics=sem, vmem_limit_bytes=vmem_mb * 1024 * 1024)


def _dot(a, b):
    return jnp.dot(a, b, preferred_element_type=F32)


def _dot_nt(a, b):
    return lax.dot_general(a, b, (((1,), (1,)), ((), ())), preferred_element_type=F32)


def _dot_tn(a, b):
    return lax.dot_general(a, b, (((0,), (0,)), ((), ())), preferred_element_type=F32)


def _ln_stats(z):
    mu = jnp.mean(z, -1, keepdims=True)
    zc = z - mu
    var = jnp.mean(zc * zc, -1, keepdims=True)
    rstd = lax.rsqrt(var + LN_EPS)
    return zc * rstd, rstd


def _ln_bwd(dh, xhat, rstd, g):
    dxh = dh * g
    m1 = jnp.mean(dxh, -1, keepdims=True)
    m2 = jnp.mean(dxh * xhat, -1, keepdims=True)
    return rstd * (dxh - m1 - xhat * m2)


def _rows8(rows, width):
    pad = [jnp.zeros((1, width), F32)] * (SUBLANES - len(rows))
    return jnp.concatenate(list(rows) + pad, axis=0)


def _mm_nn(a, w3, bias, *, tm, out_dtype, name, vmem_mb=40):
    M, K = a.shape
    ns, _, tn = w3.shape

    def body(a_ref, w_ref, b_ref, o_ref):
        o_ref[...] = (_dot(a_ref[...], w_ref[...]) + b_ref[...]).astype(o_ref.dtype)

    return pl.pallas_call(
        body, grid=(M // tm, ns),
        in_specs=[pl.BlockSpec((tm, K), lambda i, j: (i, 0)),
                  pl.BlockSpec((None, K, tn), lambda i, j: (j, 0, 0)),
                  pl.BlockSpec((1, tn), lambda i, j: (0, j))],
        out_specs=pl.BlockSpec((tm, tn), lambda i, j: (i, j)),
        out_shape=jax.ShapeDtypeStruct((M, ns * tn), out_dtype),
        name=name, compiler_params=_cparams(("parallel", "arbitrary"), vmem_mb))(a, w3, bias)


def _mm_nt(a, w, *, tm, a_col=0, name, vmem_mb=40):
    M = a.shape[0]
    N, K = w.shape

    def body(a_ref, w_ref, o_ref):
        o_ref[...] = _dot_nt(a_ref[...], w_ref[...]).astype(o_ref.dtype)

    return pl.pallas_call(
        body, grid=(M // tm,),
        in_specs=[pl.BlockSpec((tm, K), lambda i: (i, a_col)),
                  pl.BlockSpec((N, K), lambda i: (0, 0))],
        out_specs=pl.BlockSpec((tm, N), lambda i: (i, 0)),
        out_shape=jax.ShapeDtypeStruct((M, N), BF16),
        name=name, compiler_params=_cparams(("parallel",), vmem_mb))(a, w)


def _mm_tn(a, g, *, n_out, tn, ts, g_block, g_map, colsum=False, name, vmem_mb=48):
    S, K = a.shape
    n_s = S // ts

    def body(a_ref, g_ref, *rest):
        if colsum:
            o_ref, cs_ref, acc_ref, cacc_ref = rest
        else:
            o_ref, acc_ref = rest
        s = pl.program_id(1)

        @pl.when(s == 0)
        def _():
            acc_ref[...] = jnp.zeros_like(acc_ref)
            if colsum:
                cacc_ref[...] = jnp.zeros_like(cacc_ref)

        gv = g_ref[...]
        acc_ref[...] += _dot_tn(a_ref[...], gv)
        if colsum:
            cacc_ref[...] += jnp.broadcast_to(jnp.sum(gv.astype(F32), axis=0, keepdims=True), cacc_ref.shape)

        @pl.when(s == n_s - 1)
        def _():
            o_ref[...] = acc_ref[...].astype(o_ref.dtype)
            if colsum:
                cs_ref[...] = cacc_ref[...]

    out_specs = [pl.BlockSpec((None, K, tn), lambda j, s: (j, 0, 0))]
    out_shape = [jax.ShapeDtypeStruct((n_out, K, tn), BF16)]
    scratch = [pltpu.VMEM((K, tn), F32)]
    if colsum:
        out_specs.append(pl.BlockSpec((SUBLANES, tn), lambda j, s: (0, j)))
        out_shape.append(jax.ShapeDtypeStruct((SUBLANES, n_out * tn), F32))
        scratch.append(pltpu.VMEM((SUBLANES, tn), F32))
    res = pl.pallas_call(
        body, grid=(n_out, n_s),
        in_specs=[pl.BlockSpec((ts, K), lambda j, s: (s, 0)), pl.BlockSpec(g_block, g_map)],
        out_specs=out_specs, out_shape=out_shape, scratch_shapes=scratch,
        name=name, compiler_params=_cparams(("parallel", "arbitrary"), vmem_mb))(a, g)
    return res if colsum else res[0]


def _ln0_fwd(x, g, b, *, tm=512):
    S, Dm = x.shape

    def body(x_ref, g_ref, b_ref, h_ref, hb_ref):
        xhat, _ = _ln_stats(x_ref[...])
        h = xhat * g_ref[...] + b_ref[...]
        h_ref[...] = h
        hb_ref[...] = h.astype(BF16)

    row = pl.BlockSpec((tm, Dm), lambda i: (i, 0))
    vec = pl.BlockSpec((1, Dm), lambda i: (0, 0))
    return pl.pallas_call(
        body, grid=(S // tm,), in_specs=[row, vec, vec], out_specs=[row, row],
        out_shape=[jax.ShapeDtypeStruct((S, Dm), F32), jax.ShapeDtypeStruct((S, Dm), BF16)],
        name="ln0_fwd", compiler_params=_cparams(("parallel",), 32))(x, g, b)


def _slab_spec(S, col0):
    return pl.BlockSpec((S, SLAB), lambda j: (0, col0 // SLAB + j))


def _zero_pads(scr, S):
    scr[0:PAD, :] = jnp.zeros((PAD, SLAB), F32)
    scr[S + PAD:S + 2 * PAD, :] = jnp.zeros((PAD, SLAB), F32)


def _shifted(scr, t):
    return (scr[PAD - 1 + t:PAD - 1 + t + CHUNK, :], scr[PAD + t:PAD + t + CHUNK, :],
            scr[PAD + 1 + t:PAD + 1 + t + CHUNK, :])


def _conv_gate_fwd(proj, conv_w):
    S = proj.shape[0]

    def body(b_ref, c_ref, h_ref, w_ref, o_ref, u_scr):
        _zero_pads(u_scr, S)
        for t in range(0, S, CHUNK):
            u_scr[PAD + t:PAD + t + CHUNK, :] = c_ref[t:t + CHUNK, :].astype(F32) * h_ref[t:t + CHUNK, :].astype(F32)
        w0, w1, w2 = w_ref[0:1, :], w_ref[1:2, :], w_ref[2:3, :]
        for t in range(0, S, CHUNK):
            um, u0, up = _shifted(u_scr, t)
            cv = w0 * um + w1 * u0 + w2 * up
            o_ref[t:t + CHUNK, :] = (b_ref[t:t + CHUNK, :].astype(F32) * cv).astype(BF16)

    return pl.pallas_call(
        body, grid=(D_CONV // SLAB,),
        in_specs=[_slab_spec(S, OFF_B), _slab_spec(S, OFF_C), _slab_spec(S, OFF_H),
                  pl.BlockSpec((3, SLAB), lambda j: (0, j))],
        out_specs=pl.BlockSpec((S, SLAB), lambda j: (0, j)),
        out_shape=jax.ShapeDtypeStruct((S, D_CONV), BF16),
        scratch_shapes=[pltpu.VMEM((S + 2 * PAD, SLAB), F32)],
        name="conv_gate_fwd", compiler_params=_cparams(("parallel",), 40))(proj, proj, proj, conv_w)


def _attn_masks(i, sub, dil):
    a = lax.broadcasted_iota(jnp.int32, (TQ, 2 * TQ), 0)
    j = lax.broadcasted_iota(jnp.int32, (TQ, 2 * TQ), 1)
    rel = jnp.abs(j - RADIUS - a)
    kpos = i * TQ - RADIUS + j
    valid = (rel <= RADIUS) & (kpos >= 0) & (kpos < sub)
    return valid, -(rel * dil).astype(F32)


def _slope(g, h):
    return 2.0 ** (-8.0 * (g * HEADS_PER_GROUP + h + 1) / (N_GROUPS * HEADS_PER_GROUP))


def _window(p_ref, c_ref, n_ref):
    return jnp.concatenate([p_ref[TQ - RADIUS:, :], c_ref[...], n_ref[:RADIUS, :]], axis=0)


def _qkv_specs(nb, g):
    nblk = N_IN // GROUP_W

    def spec(off, shift):
        col = off // GROUP_W + g
        return pl.BlockSpec((TQ, GROUP_W), lambda r, i: (jnp.clip(i + shift, 0, nb - 1), r * nblk + col))

    return [spec(OFF_Q, 0), spec(OFF_K, -1), spec(OFF_K, 0), spec(OFF_K, 1),
            spec(OFF_V, -1), spec(OFF_V, 0), spec(OFF_V, 1)]


def _attn_fwd(proj, g):
    S = proj.shape[0]
    dil = GROUPS[g][1]
    sub = S // dil
    nb = sub // TQ
    pv = proj.reshape(sub, dil * N_IN)

    def body(q_ref, kp, kc, kn, vp, vc, vn, o_ref, lse_ref):
        valid, base = _attn_masks(pl.program_id(1), sub, dil)
        kwin = _window(kp, kc, kn)
        vwin = _window(vp, vc, vn)
        q = q_ref[...]
        for h in range(HEADS_PER_GROUP):
            sl = slice(h * HEAD_DIM, (h + 1) * HEAD_DIM)
            s = _dot_nt(q[:, sl], kwin[:, sl]) * ATT_SCALE + _slope(g, h) * base
            s = jnp.where(valid, s, MASK_VALUE)
            m = jnp.max(s, -1, keepdims=True)
            p = jnp.exp(s - m)
            den = jnp.sum(p, -1, keepdims=True)
            o_ref[:, sl] = _dot(p.astype(BF16), vwin[:, sl]) / den
            lse_ref[:, sl] = jnp.broadcast_to(m + jnp.log(den), (TQ, HEAD_DIM))

    out = pl.BlockSpec((TQ, GROUP_W), lambda r, i: (i, r))
    o, lse = pl.pallas_call(
        body, grid=(dil, nb), in_specs=_qkv_specs(nb, g), out_specs=[out, out],
        out_shape=[jax.ShapeDtypeStruct((sub, dil * GROUP_W), F32)] * 2,
        name=f"attn_fwd_g{g}", compiler_params=_cparams(("parallel", "arbitrary"), 32))(*([pv] * 7))
    return o.reshape(S, GROUP_W), lse.reshape(S, GROUP_W)


def _attn_combine(outs, lses, *, tm=512):
    S = outs[0].shape[0]

    def body(o0, o1, o2, l0, l1, l2, c_ref, cb_ref, lt_ref):
        la, lb, lc = l0[...], l1[...], l2[...]
        m = jnp.maximum(jnp.maximum(la, lb), lc)
        e0, e1, e2 = jnp.exp(la - m), jnp.exp(lb - m), jnp.exp(lc - m)
        den = e0 + e1 + e2
        comb = (e0 * o0[...] + e1 * o1[...] + e2 * o2[...]) / den
        c_ref[...] = comb
        cb_ref[...] = comb.astype(BF16)
        lt_ref[...] = m + jnp.log(den)

    row = pl.BlockSpec((tm, GROUP_W), lambda i: (i, 0))
    return pl.pallas_call(
        body, grid=(S // tm,), in_specs=[row] * 6, out_specs=[row] * 3,
        out_shape=[jax.ShapeDtypeStruct((S, GROUP_W), F32), jax.ShapeDtypeStruct((S, GROUP_W), BF16),
                   jax.ShapeDtypeStruct((S, GROUP_W), F32)],
        name="attn_combine", compiler_params=_cparams(("parallel",), 32))(*outs, *lses)


def _branch_mix(ya_in, comb_b, w_a, w_b, proj, *, tm=512):
    S = ya_in.shape[0]

    half = D_MODEL // 2
    gate0 = OFF_GA // half

    def body(ya_ref, cb_ref, wa_ref, wb_ref, ga0, ga1, gb0, gb1, yab_ref, mx_ref):
        y_a = _dot(ya_ref[...], wa_ref[...])
        y_b = _dot(cb_ref[...], wb_ref[...])
        yab_ref[:, 0:D_MODEL] = y_a.astype(BF16)
        yab_ref[:, D_MODEL:2 * D_MODEL] = y_b.astype(BF16)
        for q, (ga, gb) in enumerate(((ga0, gb0), (ga1, gb1))):
            sl = slice(q * half, (q + 1) * half)
            mx = (jax.nn.sigmoid(ga[...].astype(F32)) * y_a[:, sl] + jax.nn.sigmoid(gb[...].astype(F32)) * y_b[:, sl])
            mx_ref[:, sl] = mx.astype(BF16)

    gate = lambda q: pl.BlockSpec((tm, half), lambda i: (i, gate0 + q))
    return pl.pallas_call(
        body, grid=(S // tm,),
        in_specs=[pl.BlockSpec((tm, D_CONV), lambda i: (i, 0)), pl.BlockSpec((tm, GROUP_W), lambda i: (i, 0)),
                  pl.BlockSpec((D_CONV, D_MODEL), lambda i: (0, 0)), pl.BlockSpec((GROUP_W, D_MODEL), lambda i: (0, 0)),
                  gate(0), gate(1), gate(2), gate(3)],
        out_specs=[pl.BlockSpec((tm, 2 * D_MODEL), lambda i: (i, 0)), pl.BlockSpec((tm, D_MODEL), lambda i: (i, 0))],
        out_shape=[jax.ShapeDtypeStruct((S, 2 * D_MODEL), BF16), jax.ShapeDtypeStruct((S, D_MODEL), BF16)],
        name="branch_mix", compiler_params=_cparams(("parallel",), 40))(ya_in, comb_b, w_a, w_b, proj, proj, proj, proj)


def _mix_ln1(mixin, w_o, b_o, h0, g1, b1, *, tm=512):
    S = mixin.shape[0]

    def body(mx_ref, wo_ref, bo_ref, h0_ref, g_ref, b_ref, xh_ref, rs_ref, h1b_ref):
        z = ALPHA * h0_ref[...] + _dot(mx_ref[...], wo_ref[...]) + bo_ref[...]
        xhat, rstd = _ln_stats(z)
        xh_ref[...] = xhat
        rs_ref[...] = jnp.broadcast_to(rstd, (tm, LANES))
        h1b_ref[...] = (xhat * g_ref[...] + b_ref[...]).astype(BF16)

    row = pl.BlockSpec((tm, D_MODEL), lambda i: (i, 0))
    vec = pl.BlockSpec((1, D_MODEL), lambda i: (0, 0))
    return pl.pallas_call(
        body, grid=(S // tm,),
        in_specs=[row, pl.BlockSpec((D_MODEL, D_MODEL), lambda i: (0, 0)), vec, row, vec, vec],
        out_specs=[row, pl.BlockSpec((tm, LANES), lambda i: (i, 0)), row],
        out_shape=[jax.ShapeDtypeStruct((S, D_MODEL), F32), jax.ShapeDtypeStruct((S, LANES), F32),
                   jax.ShapeDtypeStruct((S, D_MODEL), BF16)],
        name="mix_ln1", compiler_params=_cparams(("parallel",), 40))(mixin, w_o, b_o, h0, g1, b1)


def _gelu_parts(cz):
    cdf = 0.5 * (1.0 + lax.erf(cz * INV_SQRT2))
    return cdf, cz * cdf


def _ffn_conv_fwd(up, cw, cb):
    S = up.shape[0]

    def body(a_ref, g_ref, w_ref, cb_ref, o_ref, a_scr):
        _zero_pads(a_scr, S)
        for t in range(0, S, CHUNK):
            a_scr[PAD + t:PAD + t + CHUNK, :] = a_ref[t:t + CHUNK, :].astype(F32)
        w0, w1, w2 = w_ref[0:1, :], w_ref[1:2, :], w_ref[2:3, :]
        for t in range(0, S, CHUNK):
            am, a0, ap = _shifted(a_scr, t)
            _, gel = _gelu_parts(w0 * am + w1 * a0 + w2 * ap + cb_ref[...])
            o_ref[t:t + CHUNK, :] = (gel * g_ref[t:t + CHUNK, :].astype(F32)).astype(BF16)

    return pl.pallas_call(
        body, grid=(D_FF // SLAB,),
        in_specs=[_slab_spec(S, 0), _slab_spec(S, D_FF), pl.BlockSpec((3, SLAB), lambda j: (0, j)),
                  pl.BlockSpec((1, SLAB), lambda j: (0, j))],
        out_specs=pl.BlockSpec((S, SLAB), lambda j: (0, j)),
        out_shape=jax.ShapeDtypeStruct((S, D_FF), BF16),
        scratch_shapes=[pltpu.VMEM((S + 2 * PAD, SLAB), F32)],
        name="ffn_conv_fwd", compiler_params=_cparams(("parallel",), 40))(up, up, cw, cb)


def _down_ln2_loss(f, w_down, b_down, xhat1, g1, b1, g2, b2, target, *, tm=256):
    S = f.shape[0]

    def body(f_ref, wd_ref, bd_ref, xh1_ref, g1_ref, b1_ref, g2_ref, b2_ref, t_ref, dz_ref, dzb_ref, st_ref):
        h1 = xh1_ref[...] * g1_ref[...] + b1_ref[...]
        z = ALPHA * h1 + _dot(f_ref[...], wd_ref[...]) + bd_ref[...]
        xhat, rstd = _ln_stats(z)
        err = xhat * g2_ref[...] + b2_ref[...] - t_ref[...]
        loss = (0.5 / D_MODEL) * jnp.sum(jnp.sum(err * err, axis=1, keepdims=True), axis=0, keepdims=True)
        dh2 = err * (1.0 / D_MODEL)
        dz = _ln_bwd(dh2, xhat, rstd, g2_ref[...])
        dz_ref[...] = dz
        dzb_ref[...] = dz.astype(BF16)
        upd = _rows8([jnp.sum(dh2 * xhat, axis=0, keepdims=True), jnp.sum(dh2, axis=0, keepdims=True),
                      jnp.broadcast_to(loss, (1, D_MODEL)), jnp.sum(dz, axis=0, keepdims=True)], D_MODEL)

        @pl.when(pl.program_id(0) == 0)
        def _():
            st_ref[...] = upd

        @pl.when(pl.program_id(0) != 0)
        def _():
            st_ref[...] += upd

    row = pl.BlockSpec((tm, D_MODEL), lambda i: (i, 0))
    vec = pl.BlockSpec((1, D_MODEL), lambda i: (0, 0))
    return pl.pallas_call(
        body, grid=(S // tm,),
        in_specs=[pl.BlockSpec((tm, D_FF), lambda i: (i, 0)), pl.BlockSpec((D_FF, D_MODEL), lambda i: (0, 0)),
                  vec, row, vec, vec, vec, vec, row],
        out_specs=[row, row, pl.BlockSpec((SUBLANES, D_MODEL), lambda i: (0, 0))],
        out_shape=[jax.ShapeDtypeStruct((S, D_MODEL), F32), jax.ShapeDtypeStruct((S, D_MODEL), BF16),
                   jax.ShapeDtypeStruct((SUBLANES, D_MODEL), F32)],
        name="down_ln2_loss", compiler_params=_cparams(("arbitrary",), 48))(
            f, w_down, b_down, xhat1, g1, b1, g2, b2, target)


def _ffn_conv_bwd(up, df, cw, cb):
    S = up.shape[0]

    def body(a_ref, g_ref, df_ref, w_ref, cb_ref, dup_ref, sm_ref, a_scr, d_scr):
        _zero_pads(a_scr, S)
        _zero_pads(d_scr, S)
        for t in range(0, S, CHUNK):
            a_scr[PAD + t:PAD + t + CHUNK, :] = a_ref[t:t + CHUNK, :].astype(F32)
        w0, w1, w2 = w_ref[0:1, :], w_ref[1:2, :], w_ref[2:3, :]
        zero = jnp.zeros((1, SLAB), F32)
        s_dg, s_dcz, s_w0, s_w1, s_w2 = zero, zero, zero, zero, zero
        for t in range(0, S, CHUNK):
            am, a0, ap = _shifted(a_scr, t)
            cz = w0 * am + w1 * a0 + w2 * ap + cb_ref[...]
            cdf, gel = _gelu_parts(cz)
            dfv = df_ref[t:t + CHUNK, :].astype(F32)
            dgte = dfv * gel
            dcz = dfv * g_ref[t:t + CHUNK, :].astype(F32) * (cdf + cz * jnp.exp(-0.5 * cz * cz) * INV_SQRT_2PI)
            dup_ref[1, t:t + CHUNK, :] = dgte.astype(BF16)
            d_scr[PAD + t:PAD + t + CHUNK, :] = dcz
            s_dg = s_dg + jnp.sum(dgte, axis=0, keepdims=True)
            s_dcz = s_dcz + jnp.sum(dcz, axis=0, keepdims=True)
            s_w0 = s_w0 + jnp.sum(dcz * am, axis=0, keepdims=True)
            s_w1 = s_w1 + jnp.sum(dcz * a0, axis=0, keepdims=True)
            s_w2 = s_w2 + jnp.sum(dcz * ap, axis=0, keepdims=True)
        s_da = zero
        for t in range(0, S, CHUNK):
            dm, d0, dp = _shifted(d_scr, t)
            da = w0 * dp + w1 * d0 + w2 * dm
            dup_ref[0, t:t + CHUNK, :] = da.astype(BF16)
            s_da = s_da + jnp.sum(da, axis=0, keepdims=True)
        sm_ref[...] = _rows8([s_da, s_dg, s_dcz, s_w0, s_w1, s_w2], SLAB)

    return pl.pallas_call(
        body, grid=(D_FF // SLAB,),
        in_specs=[_slab_spec(S, 0), _slab_spec(S, D_FF), pl.BlockSpec((S, SLAB), lambda j: (0, j)),
                  pl.BlockSpec((3, SLAB), lambda j: (0, j)), pl.BlockSpec((1, SLAB), lambda j: (0, j))],
        out_specs=[pl.BlockSpec((2, S, SLAB), lambda j: (0, 0, j)), pl.BlockSpec((SUBLANES, SLAB), lambda j: (0, j))],
        out_shape=[jax.ShapeDtypeStruct((2, S, D_FF), BF16), jax.ShapeDtypeStruct((SUBLANES, D_FF), F32)],
        scratch_shapes=[pltpu.VMEM((S + 2 * PAD, SLAB), F32)] * 2,
        name="ffn_conv_bwd", compiler_params=_cparams(("parallel",), 48))(up, up, df, cw, cb)


def _up_bwd_ln1(dup, w_up3, dz2, xhat1, rstd1, g1, *, tm=512):
    S = dz2.shape[0]
    ns, _, tk = w_up3.shape
    per_plane = D_FF // tk

    def body(du_ref, w_ref, dz2_ref, xh_ref, rs_ref, g_ref, dz_ref, dzb_ref, st_ref, acc_ref):
        i, k = pl.program_id(0), pl.program_id(1)

        @pl.when(k == 0)
        def _():
            acc_ref[...] = ALPHA * dz2_ref[...]

        acc_ref[...] += _dot_nt(du_ref[...], w_ref[...])

        @pl.when(k == ns - 1)
        def _():
            dh = acc_ref[...]
            xhat = xh_ref[...]
            dz = _ln_bwd(dh, xhat, rs_ref[:, 0:1], g_ref[...])
            dz_ref[...] = dz
            dzb_ref[...] = dz.astype(BF16)
            upd = _rows8([jnp.sum(dh * xhat, axis=0, keepdims=True), jnp.sum(dh, axis=0, keepdims=True),
                          jnp.sum(dz, axis=0, keepdims=True)], D_MODEL)

            @pl.when(i == 0)
            def _():
                st_ref[...] = upd

            @pl.when(i != 0)
            def _():
                st_ref[...] += upd

    row = pl.BlockSpec((tm, D_MODEL), lambda i, k: (i, 0))
    return pl.pallas_call(
        body, grid=(S // tm, ns),
        in_specs=[pl.BlockSpec((None, tm, tk), lambda i, k: (k // per_plane, i, k % per_plane)),
                  pl.BlockSpec((None, D_MODEL, tk), lambda i, k: (k, 0, 0)),
                  row, row, pl.BlockSpec((tm, LANES), lambda i, k: (i, 0)),
                  pl.BlockSpec((1, D_MODEL), lambda i, k: (0, 0))],
        out_specs=[row, row, pl.BlockSpec((SUBLANES, D_MODEL), lambda i, k: (0, 0))],
        out_shape=[jax.ShapeDtypeStruct((S, D_MODEL), F32), jax.ShapeDtypeStruct((S, D_MODEL), BF16),
                   jax.ShapeDtypeStruct((SUBLANES, D_MODEL), F32)],
        scratch_shapes=[pltpu.VMEM((tm, D_MODEL), F32)],
        name="up_bwd_ln1", compiler_params=_cparams(("arbitrary", "arbitrary"), 40))(
            dup, w_up3, dz2, xhat1, rstd1, g1)


def _mix_bwd(dz1b, w_o, proj, yab, *, tm=512):
    S = dz1b.shape[0]
    half = D_MODEL // 2
    gate0 = OFF_GA // half

    def body(dz_ref, wo_ref, gt_ref, y_ref, dy_ref, dg_ref):
        dmx = _dot_nt(dz_ref[...], wo_ref[...])
        sg = jax.nn.sigmoid(gt_ref[...].astype(F32))
        dy_ref[...] = (dmx * sg).astype(BF16)
        dg_ref[...] = (dmx * y_ref[...].astype(F32) * sg * (1.0 - sg)).astype(BF16)

    blk = pl.BlockSpec((tm, half), lambda i, j: (i, j))
    return pl.pallas_call(
        body, grid=(S // tm, 4),
        in_specs=[pl.BlockSpec((tm, D_MODEL), lambda i, j: (i, 0)),
                  pl.BlockSpec((half, D_MODEL), lambda i, j: (j % 2, 0)),
                  pl.BlockSpec((tm, half), lambda i, j: (i, gate0 + j)), blk],
        out_specs=[blk, blk],
        out_shape=[jax.ShapeDtypeStruct((S, 2 * D_MODEL), BF16)] * 2,
        name="mix_bwd", compiler_params=_cparams(("parallel", "arbitrary"), 32))(dz1b, w_o, proj, yab)


def _conv_gate_bwd(proj, dya_in, conv_w):
    S = proj.shape[0]

    def body(b_ref, c_ref, h_ref, dy_ref, w_ref, o_ref, sm_ref, u_scr, d_scr):
        _zero_pads(u_scr, S)
        _zero_pads(d_scr, S)
        for t in range(0, S, CHUNK):
            u_scr[PAD + t:PAD + t + CHUNK, :] = c_ref[t:t + CHUNK, :].astype(F32) * h_ref[t:t + CHUNK, :].astype(F32)
        w0, w1, w2 = w_ref[0:1, :], w_ref[1:2, :], w_ref[2:3, :]
        zero = jnp.zeros((1, SLAB), F32)
        s_w0, s_w1, s_w2 = zero, zero, zero
        for t in range(0, S, CHUNK):
            um, u0, up = _shifted(u_scr, t)
            dy = dy_ref[t:t + CHUNK, :].astype(F32)
            o_ref[0, t:t + CHUNK, :] = (dy * (w0 * um + w1 * u0 + w2 * up)).astype(BF16)
            dcv = dy * b_ref[t:t + CHUNK, :].astype(F32)
            d_scr[PAD + t:PAD + t + CHUNK, :] = dcv
            s_w0 = s_w0 + jnp.sum(dcv * um, axis=0, keepdims=True)
            s_w1 = s_w1 + jnp.sum(dcv * u0, axis=0, keepdims=True)
            s_w2 = s_w2 + jnp.sum(dcv * up, axis=0, keepdims=True)
        for t in range(0, S, CHUNK):
            dm, d0, dp = _shifted(d_scr, t)
            du = w0 * dp + w1 * d0 + w2 * dm
            o_ref[1, t:t + CHUNK, :] = (du * h_ref[t:t + CHUNK, :].astype(F32)).astype(BF16)
            o_ref[2, t:t + CHUNK, :] = (du * c_ref[t:t + CHUNK, :].astype(F32)).astype(BF16)
        sm_ref[...] = _rows8([s_w0, s_w1, s_w2], SLAB)

    return pl.pallas_call(
        body, grid=(D_CONV // SLAB,),
        in_specs=[_slab_spec(S, OFF_B), _slab_spec(S, OFF_C), _slab_spec(S, OFF_H),
                  pl.BlockSpec((S, SLAB), lambda j: (0, j)), pl.BlockSpec((3, SLAB), lambda j: (0, j))],
        out_specs=[pl.BlockSpec((3, S, SLAB), lambda j: (0, 0, j)), pl.BlockSpec((SUBLANES, SLAB), lambda j: (0, j))],
        out_shape=[jax.ShapeDtypeStruct((3, S, D_CONV), BF16), jax.ShapeDtypeStruct((SUBLANES, D_CONV), F32)],
        scratch_shapes=[pltpu.VMEM((S + 2 * PAD, SLAB), F32)] * 2,
        name="conv_gate_bwd", compiler_params=_cparams(("parallel",), 48))(proj, proj, proj, dya_in, conv_w)


def _comb_bwd(dyab, w_b, comb, *, tm=512):
    S = comb.shape[0]

    def body(dy_ref, wb_ref, c_ref, dc_ref, dl_ref):
        dcb = _dot_nt(dy_ref[...], wb_ref[...]).astype(BF16)
        dc_ref[...] = dcb
        prod = dcb.astype(F32) * c_ref[...]
        for h in range(HEADS_PER_GROUP):
            sl = slice(h * HEAD_DIM, (h + 1) * HEAD_DIM)
            dl_ref[:, sl] = jnp.broadcast_to(jnp.sum(prod[:, sl], axis=1, keepdims=True), (tm, HEAD_DIM))

    row = pl.BlockSpec((tm, GROUP_W), lambda i: (i, 0))
    return pl.pallas_call(
        body, grid=(S // tm,),
        in_specs=[pl.BlockSpec((tm, D_MODEL), lambda i: (i, 1)), pl.BlockSpec((GROUP_W, D_MODEL), lambda i: (0, 0)), row],
        out_specs=[row, row],
        out_shape=[jax.ShapeDtypeStruct((S, GROUP_W), BF16), jax.ShapeDtypeStruct((S, GROUP_W), F32)],
        name="comb_bwd", compiler_params=_cparams(("parallel",), 32))(dyab, w_b, comb)


def _attn_bwd(proj, g, dcomb, lse_tot, delta, prev):
    S = proj.shape[0]
    dil = GROUPS[g][1]
    sub = S // dil
    nb = sub // TQ
    pv = proj.reshape(sub, dil * N_IN)
    view = lambda a: a.reshape(sub, dil * a.shape[1])

    def body(q_ref, kp, kc, kn, vp, vc, vn, do_ref, lse_ref, dl_ref, *rest):
        dq_ref, dk_ref, dv_ref, ak, av = rest[-5:]
        i = pl.program_id(1)

        @pl.when(i == 0)
        def _():
            ak[...] = jnp.zeros_like(ak)
            av[...] = jnp.zeros_like(av)

        @pl.when(i < nb)
        def _():
            valid, base = _attn_masks(i, sub, dil)
            kwin = _window(kp, kc, kn)
            vwin = _window(vp, vc, vn)
            q = q_ref[...]
            do = do_ref[...]
            for h in range(HEADS_PER_GROUP):
                sl = slice(h * HEAD_DIM, (h + 1) * HEAD_DIM)
                s = _dot_nt(q[:, sl], kwin[:, sl]) * ATT_SCALE + _slope(g, h) * base
                s = jnp.where(valid, s, MASK_VALUE)
                p = jnp.exp(s - lse_ref[:, h * HEAD_DIM:h * HEAD_DIM + 1])
                dp = _dot_nt(do[:, sl], vwin[:, sl])
                ds = (p * (dp - dl_ref[:, h * HEAD_DIM:h * HEAD_DIM + 1])).astype(BF16)
                dq_ref[:, sl] = (_dot(ds, kwin[:, sl]) * ATT_SCALE).astype(BF16)
                ak[RADIUS:RADIUS + 2 * TQ, sl] += _dot_tn(ds, q[:, sl]) * ATT_SCALE
                av[RADIUS:RADIUS + 2 * TQ, sl] += _dot_tn(p.astype(BF16), do[:, sl])

        dk_ref[...] = ak[0:TQ, :].astype(BF16)
        dv_ref[...] = av[0:TQ, :].astype(BF16)
        ak[0:2 * TQ, :] = ak[TQ:3 * TQ, :]
        av[0:2 * TQ, :] = av[TQ:3 * TQ, :]
        ak[2 * TQ:3 * TQ, :] = jnp.zeros((TQ, GROUP_W), F32)
        av[2 * TQ:3 * TQ, :] = jnp.zeros((TQ, GROUP_W), F32)

    tok = pl.BlockSpec((TQ, GROUP_W), lambda r, i: (jnp.minimum(i, nb - 1), r))
    any_spec = pl.BlockSpec(memory_space=pl.ANY)
    dq_spec = pl.BlockSpec((TQ, GROUP_W), lambda r, i: (jnp.minimum(i, nb - 1), r * N_GROUPS + g))
    dkv_spec = pl.BlockSpec((TQ, GROUP_W), lambda r, i: (jnp.maximum(i - 1, 0), r * N_GROUPS + g))
    operands = [pv] * 7 + [view(dcomb), view(lse_tot), view(delta)]
    in_specs = _qkv_specs(nb, g) + [tok, tok, tok]
    aliases = {}
    if prev is not None:
        operands += [view(a) for a in prev]
        in_specs += [any_spec] * 3
        aliases = {10: 0, 11: 1, 12: 2}
    outs = pl.pallas_call(
        body, grid=(dil, nb + 1), in_specs=in_specs, out_specs=[dq_spec, dkv_spec, dkv_spec],
        out_shape=[jax.ShapeDtypeStruct((sub, dil * QKV_W), BF16)] * 3,
        scratch_shapes=[pltpu.VMEM((3 * TQ, GROUP_W), F32)] * 2, input_output_aliases=aliases,
        name=f"attn_bwd_g{g}", compiler_params=_cparams(("arbitrary", "arbitrary"), 32))(*operands)
    return tuple(o.reshape(S, QKV_W) for o in outs)


def _in_bwd_ln0(dproj, w_in3, dz1, x, g0, *, tm=512):
    S = x.shape[0]
    ns, _, tk = w_in3.shape

    def body(dp_ref, w_ref, dz_ref, x_ref, g_ref, gx_ref, st_ref, acc_ref):
        i, k = pl.program_id(0), pl.program_id(1)

        @pl.when(k == 0)
        def _():
            acc_ref[...] = ALPHA * dz_ref[...]

        acc_ref[...] += _dot_nt(dp_ref[...], w_ref[...])

        @pl.when(k == ns - 1)
        def _():
            dh = acc_ref[...]
            xhat, rstd = _ln_stats(x_ref[...])
            gx_ref[...] = _ln_bwd(dh, xhat, rstd, g_ref[...])
            upd = _rows8([jnp.sum(dh * xhat, axis=0, keepdims=True), jnp.sum(dh, axis=0, keepdims=True)], D_MODEL)

            @pl.when(i == 0)
            def _():
                st_ref[...] = upd

            @pl.when(i != 0)
            def _():
                st_ref[...] += upd

    row = pl.BlockSpec((tm, D_MODEL), lambda i, k: (i, 0))
    return pl.pallas_call(
        body, grid=(S // tm, ns),
        in_specs=[pl.BlockSpec((tm, tk), lambda i, k: (i, k)), pl.BlockSpec((None, D_MODEL, tk), lambda i, k: (k, 0, 0)),
                  row, row, pl.BlockSpec((1, D_MODEL), lambda i, k: (0, 0))],
        out_specs=[row, pl.BlockSpec((SUBLANES, D_MODEL), lambda i, k: (0, 0))],
        out_shape=[jax.ShapeDtypeStruct((S, D_MODEL), F32), jax.ShapeDtypeStruct((SUBLANES, D_MODEL), F32)],
        scratch_shapes=[pltpu.VMEM((tm, D_MODEL), F32)],
        name="in_bwd_ln0", compiler_params=_cparams(("arbitrary", "arbitrary"), 48))(dproj, w_in3, dz1, x, g0)


HBM_SPEC = pl.BlockSpec(memory_space=pltpu.HBM)


def _place():
    x, y, c = lax.axis_index("x"), lax.axis_index("y"), lax.axis_index("c")
    chips = [(1 - x, y), (x, 1 - y), (1 - x, 1 - y)]
    return x, y, c, chips


def _allgather_shards(shards):
    n = len(shards)

    def body(*refs):
        ins, outs = refs[:n], refs[n:2 * n]
        send_sems, recv_sems, loc_sems = refs[2 * n:]
        x, y, c, chips = _place()
        me = 2 * x + y
        local, sends = [], []
        for w in range(n):
            cp = pltpu.make_async_copy(ins[w], outs[w].at[me], loc_sems.at[w])
            cp.start()
            local.append(cp)
            for j, (px, py) in enumerate(chips):
                cp = pltpu.make_async_remote_copy(
                    src_ref=ins[w], dst_ref=outs[w].at[me], send_sem=send_sems.at[3 * w + j],
                    recv_sem=recv_sems.at[3 * w + j], device_id=(px, py, c), device_id_type=MESH)
                cp.start()
                sends.append(cp)
        for w in range(n):
            for j, (px, py) in enumerate(chips):
                pltpu.make_async_remote_copy(
                    src_ref=ins[w], dst_ref=outs[w].at[2 * px + py], send_sem=send_sems.at[3 * w + j],
                    recv_sem=recv_sems.at[3 * w + j], device_id=(px, py, c), device_id_type=MESH).wait_recv()
        for cp in sends:
            cp.wait_send()
        for cp in local:
            cp.wait()

    return pl.pallas_call(
        body, in_specs=[HBM_SPEC] * n, out_specs=[HBM_SPEC] * n,
        out_shape=[jax.ShapeDtypeStruct((N_CHIPS,) + s.shape, s.dtype) for s in shards],
        scratch_shapes=[pltpu.SemaphoreType.DMA((3 * n,)), pltpu.SemaphoreType.DMA((3 * n,)),
                        pltpu.SemaphoreType.DMA((n,))],
        name="allgather_weights")(*shards)


def _exchange_grads(grads):
    n = len(grads)
    per = 7

    def body(*refs):
        ins, outs = refs[:n], refs[n:2 * n]
        send_sems, recv_sems, loc_sems = refs[2 * n:]
        x, y, c, chips = _place()
        me = 2 * x + y
        sib = (x, y, 1 - c)

        def rcopy(w, k, src, dst, to):
            return pltpu.make_async_remote_copy(src_ref=src, dst_ref=dst, send_sem=send_sems.at[per * w + k],
                                                recv_sem=recv_sems.at[per * w + k], device_id=to, device_id_type=MESH)

        local, sends = [], []
        for w in range(n):
            cp = pltpu.make_async_copy(ins[w].at[me], outs[w].at[c, me], loc_sems.at[w])
            cp.start()
            local.append(cp)
            cp = rcopy(w, 0, ins[w].at[me], outs[w].at[c, me], sib)
            cp.start()
            sends.append(cp)
            for j, (px, py) in enumerate(chips):
                cp = rcopy(w, 1 + j, ins[w].at[2 * px + py], outs[w].at[c, me], (px, py, c))
                cp.start()
                sends.append(cp)
        for w in range(n):
            for j, (px, py) in enumerate(chips):
                slot = outs[w].at[c, 2 * px + py]
                rcopy(w, 1 + j, slot, slot, (px, py, c)).wait_recv()
                cp = rcopy(w, 4 + j, slot, slot, sib)
                cp.start()
                sends.append(cp)
        for w in range(n):
            slot = outs[w].at[1 - c, me]
            rcopy(w, 0, slot, slot, sib).wait_recv()
            for j, (px, py) in enumerate(chips):
                slot = outs[w].at[1 - c, 2 * px + py]
                rcopy(w, 4 + j, slot, slot, sib).wait_recv()
        for cp in sends:
            cp.wait_send()
        for cp in local:
            cp.wait()

    return pl.pallas_call(
        body, in_specs=[HBM_SPEC] * n, out_specs=[HBM_SPEC] * n,
        out_shape=[jax.ShapeDtypeStruct((N_CORES,) + g.shape, g.dtype) for g in grads],
        scratch_shapes=[pltpu.SemaphoreType.DMA((per * n,)), pltpu.SemaphoreType.DMA((per * n,)),
                        pltpu.SemaphoreType.DMA((n,))],
        name="exchange_grads")(*grads)


def _allgather_small(vec):
    def body(v_ref, o_ref, send_sems, recv_sems, loc_sem):
        x, y, c = lax.axis_index("x"), lax.axis_index("y"), lax.axis_index("c")
        me = 4 * x + 2 * y + c

        def peer(k):
            flip = lambda v, bit: 1 - v if (k >> bit) & 1 else v
            return flip(x, 2), flip(y, 1), flip(c, 0)

        loc = pltpu.make_async_copy(v_ref, o_ref.at[me], loc_sem)
        loc.start()
        sends = []
        for k in range(1, N_DEV):
            cp = pltpu.make_async_remote_copy(src_ref=v_ref, dst_ref=o_ref.at[me], send_sem=send_sems.at[k - 1],
                                              recv_sem=recv_sems.at[k - 1], device_id=peer(k), device_id_type=MESH)
            cp.start()
            sends.append(cp)
        for k in range(1, N_DEV):
            px, py, pc = peer(k)
            pltpu.make_async_remote_copy(src_ref=v_ref, dst_ref=o_ref.at[4 * px + 2 * py + pc],
                                         send_sem=send_sems.at[k - 1], recv_sem=recv_sems.at[k - 1],
                                         device_id=(px, py, pc), device_id_type=MESH).wait_recv()
        for cp in sends:
            cp.wait_send()
        loc.wait()

    return pl.pallas_call(
        body, in_specs=[HBM_SPEC], out_specs=HBM_SPEC,
        out_shape=jax.ShapeDtypeStruct((N_DEV,) + vec.shape, vec.dtype),
        scratch_shapes=[pltpu.SemaphoreType.DMA((N_DEV - 1,)), pltpu.SemaphoreType.DMA((N_DEV - 1,)),
                        pltpu.SemaphoreType.DMA],
        name="allgather_small")(vec)


def _adamw(w, g, m, v):
    m = ADAM_B1 * m + (1.0 - ADAM_B1) * g
    v = ADAM_B2 * v + (1.0 - ADAM_B2) * (g * g)
    m_hat = m / (1.0 - ADAM_B1 ** ADAM_STEP)
    v_hat = v / (1.0 - ADAM_B2 ** ADAM_STEP)
    delta = -ADAM_LR * (m_hat / (jnp.sqrt(v_hat) + ADAM_EPS) + ADAM_WD * w)
    return delta, m, v


def _reduce_adamw(parts, w, m, v, *, tr, name):
    R, C = w.shape

    def body(p_ref, w_ref, m_ref, v_ref, g_ref, d_ref, nm_ref, nv_ref):
        def core_sum(cc):
            s = p_ref[cc, 0].astype(F32)
            for k in range(1, N_CHIPS):
                s = s + p_ref[cc, k].astype(F32)
            return s

        g = core_sum(0) + core_sum(1)
        delta, nm, nv = _adamw(w_ref[...], g, m_ref[...], v_ref[...])
        g_ref[...] = g
        d_ref[...] = delta
        nm_ref[...] = nm
        nv_ref[...] = nv

    blk = pl.BlockSpec((tr, C), lambda i: (i, 0))
    return pl.pallas_call(
        body, grid=(R // tr,),
        in_specs=[pl.BlockSpec((N_CORES, N_CHIPS, tr, C), lambda i: (0, 0, i, 0)), blk, blk, blk],
        out_specs=[blk] * 4, out_shape=[jax.ShapeDtypeStruct((R, C), F32)] * 4,
        name=name, compiler_params=_cparams(("parallel",), 40))(parts, w, m, v)


def _sum_devices(allv):
    _, R, _ = allv.shape

    def body(a_ref, o_ref):
        s = a_ref[0]
        for d in range(1, N_DEV):
            s = s + a_ref[d]
        o_ref[...] = s

    return pl.pallas_call(body, out_shape=jax.ShapeDtypeStruct((R, LANES), F32), name="sum_small")(allv)


def _adamw_small(w, g, m, v):
    def body(w_ref, g_ref, m_ref, v_ref, d_ref, nm_ref, nv_ref):
        delta, nm, nv = _adamw(w_ref[...], g_ref[...], m_ref[...], v_ref[...])
        d_ref[...] = delta
        nm_ref[...] = nm
        nv_ref[...] = nv

    return pl.pallas_call(body, out_shape=[jax.ShapeDtypeStruct(w.shape, F32)] * 3, name="adamw_small")(w, g, m, v)


def _pack(pieces):
    flat = [p.reshape(-1) for p in pieces]
    offs, n = [], 0
    for f in flat:
        offs.append(n)
        n += f.shape[0]
    total = -(-n // (SUBLANES * LANES)) * SUBLANES * LANES
    flat.append(jnp.zeros((total - n,), F32))
    return jnp.concatenate(flat).reshape(total // LANES, LANES), offs


def _local_step(x, target, p, wfull):
    S = x.shape[0]
    w_in3, w_up3 = wfull["w_in"], wfull["w_up"]
    w_a, w_o, w_down, w_b = wfull["w_a"], wfull["w_o"], wfull["w_down"], wfull["w_b"]
    conv_w, ffn_conv_w = wfull["conv_w"], wfull["ffn_conv_w"]

    h0, h0b = _ln0_fwd(x, p["ln0_g"], p["ln0_b"])
    proj = _mm_nn(h0b, w_in3, p["b_in"], tm=512, out_dtype=BF16, name="proj")
    ya_in = _conv_gate_fwd(proj, conv_w)
    att = [_attn_fwd(proj, g) for g in range(N_GROUPS)]
    comb, comb_b, lse_tot = _attn_combine([a[0] for a in att], [a[1] for a in att])
    yab, mixin = _branch_mix(ya_in, comb_b, w_a, w_b, proj)
    xhat1, rstd1, h1b = _mix_ln1(mixin, w_o, p["b_o"], h0, p["ln1_g"], p["ln1_b"])
    up = _mm_nn(h1b, w_up3, p["b_up"], tm=512, out_dtype=BF16, name="up")
    f = _ffn_conv_fwd(up, ffn_conv_w, p["ffn_conv_b"])
    dz2, dz2b, st2 = _down_ln2_loss(f, w_down, p["b_down"], xhat1, p["ln1_g"], p["ln1_b"],
                                    p["ln2_g"], p["ln2_b"], target)

    gw = {}
    gw["w_down"] = _mm_tn(f, dz2b, n_out=1, tn=D_MODEL, ts=512, g_block=(512, D_MODEL),
                          g_map=lambda j, s: (s, 0), name="grad_w_down").reshape(N_CHIPS, D_FF // N_CHIPS, D_MODEL)
    df = _mm_nt(dz2b, w_down, tm=512, name="df")
    dup, sm_ffn = _ffn_conv_bwd(up, df, ffn_conv_w, p["ffn_conv_b"])
    up_tn = w_up3.shape[2]
    up_pp = D_FF // up_tn
    gw["w_up"] = _mm_tn(h1b, dup, n_out=N_CHIPS, tn=up_tn, ts=512, g_block=(None, 512, up_tn),
                        g_map=lambda j, s: (j // up_pp, s, j % up_pp), name="grad_w_up")
    dz1, dz1b, st1 = _up_bwd_ln1(dup, w_up3, dz2, xhat1, rstd1, p["ln1_g"])

    gw["w_o"] = _mm_tn(mixin, dz1b, n_out=1, tn=D_MODEL, ts=512, g_block=(512, D_MODEL),
                       g_map=lambda j, s: (s, 0), name="grad_w_o").reshape(N_CHIPS, D_MODEL // N_CHIPS, D_MODEL)
    dyab, dgab = _mix_bwd(dz1b, w_o, proj, yab)
    gw["w_a"] = _mm_tn(ya_in, dyab, n_out=1, tn=D_MODEL, ts=512, g_block=(512, D_MODEL),
                       g_map=lambda j, s: (s, 0), name="grad_w_a").reshape(N_CHIPS, D_CONV // N_CHIPS, D_MODEL)
    b_tn = D_MODEL // N_CHIPS
    gw["w_b"] = _mm_tn(comb_b, dyab, n_out=N_CHIPS, tn=b_tn, ts=512, g_block=(512, b_tn),
                       g_map=lambda j, s: (s, D_MODEL // b_tn + j), name="grad_w_b")
    dya_in = _mm_nt(dyab, w_a, tm=512, a_col=0, name="dya_in")
    dbch, sm_conv = _conv_gate_bwd(proj, dya_in, conv_w)
    dcomb, delta = _comb_bwd(dyab, w_b, comb)
    dqkv = None
    for g in range(N_GROUPS):
        dqkv = _attn_bwd(proj, g, dcomb, lse_tot, delta, dqkv)
    dproj = jnp.concatenate([dbch[0], dbch[1], dbch[2], dqkv[0], dqkv[1], dqkv[2], dgab], axis=1)
    in_tn = w_in3.shape[2]
    gw["w_in"], cs_in = _mm_tn(h0b, dproj, n_out=N_CHIPS, tn=in_tn, ts=512, g_block=(512, in_tn),
                               g_map=lambda j, s: (s, j), colsum=True, name="grad_w_in")
    grad_x, st0 = _in_bwd_ln0(dproj, w_in3, dz1, x, p["ln0_g"])

    small = {
        "loss": st2[2:3, 0:1],
        "ln0_g": st0[0], "ln0_b": st0[1], "b_in": cs_in[0], "conv_w": sm_conv[0:3],
        "b_o": st1[2], "ln1_g": st1[0], "ln1_b": st1[1],
        "b_up": jnp.concatenate([sm_ffn[0], sm_ffn[1]]), "ffn_conv_w": sm_ffn[3:6], "ffn_conv_b": sm_ffn[2],
        "b_down": st2[3], "ln2_g": st2[0], "ln2_b": st2[1],
    }
    return grad_x, gw, small


BIG = ("w_in", "w_a", "w_b", "w_o", "w_up", "w_down")
CONV = ("conv_w", "ffn_conv_w")
VECS = ("ln0_g", "ln0_b", "b_in", "b_o", "ln1_g", "ln1_b", "b_up", "ffn_conv_b", "b_down", "ln2_g", "ln2_b")
ORDER = ("ln0_g", "ln0_b", "w_in", "b_in", "conv_w", "w_a", "w_b", "w_o", "b_o", "ln1_g", "ln1_b", "w_up", "b_up",
         "ffn_conv_w", "ffn_conv_b", "w_down", "b_down", "ln2_g", "ln2_b")
SMALL_ORDER = ("loss",) + VECS + CONV


def _step(x, target, W, Mo, Vo):
    x2, t2 = x[0], target[0]
    big2 = {n: W[n][0] for n in BIG}
    shards = [big2[n].astype(BF16) for n in BIG] + [W[n][0] for n in CONV]
    gathered = dict(zip(BIG + CONV, _allgather_shards(shards)))
    wfull = {
        "w_in": gathered["w_in"], "w_up": gathered["w_up"],
        "w_a": gathered["w_a"].reshape(D_CONV, D_MODEL), "w_o": gathered["w_o"].reshape(D_MODEL, D_MODEL),
        "w_down": gathered["w_down"].reshape(D_FF, D_MODEL),
        "w_b": gathered["w_b"].transpose(1, 0, 2).reshape(GROUP_W, D_MODEL),
        "conv_w": gathered["conv_w"].transpose(1, 0, 2).reshape(3, D_CONV),
        "ffn_conv_w": gathered["ffn_conv_w"].transpose(1, 0, 2).reshape(3, D_FF),
    }
    pvec = {n: W[n].reshape(1, -1) for n in VECS}
    grad_x, gw, small = _local_step(x2, t2, pvec, wfull)

    parts = dict(zip(BIG, _exchange_grads([gw[n] for n in BIG])))
    out = {}
    for n in BIG:
        tr = {"w_in": 128, "w_up": 128, "w_b": 128}.get(n, big2[n].shape[0] // 4)
        g, d, nm, nv = _reduce_adamw(parts[n], big2[n], Mo[n][0], Vo[n][0], tr=tr, name="adamw_" + n)
        out[n] = tuple(a[None] for a in (g, d, nm, nv))

    vec, offs = _pack([small[n] for n in SMALL_ORDER])
    tot = _sum_devices(_allgather_small(vec)).reshape(-1)
    off = dict(zip(SMALL_ORDER, offs))
    loss = tot[off["loss"]]
    chip = 2 * lax.axis_index("x") + lax.axis_index("y")
    gs = {}
    for n in VECS:
        gs[n] = lax.slice(tot, (off[n],), (off[n] + W[n].size,)).reshape(W[n].shape)
    for n in CONV:
        width = W[n].shape[2]
        full = lax.slice(tot, (off[n],), (off[n] + 3 * N_CHIPS * width,)).reshape(1, 3, N_CHIPS * width)
        gs[n] = lax.dynamic_slice_in_dim(full, chip * width, width, axis=2)
    names = VECS + CONV
    wp, _ = _pack([W[n] for n in names])
    gp, poffs = _pack([gs[n] for n in names])
    mp, _ = _pack([Mo[n] for n in names])
    vp, _ = _pack([Vo[n] for n in names])
    dl, nm, nv = (a.reshape(-1) for a in _adamw_small(wp, gp, mp, vp))
    for n, o in zip(names, poffs):
        cut = lambda a: lax.slice(a, (o,), (o + W[n].size,)).reshape(W[n].shape)
        out[n] = (gs[n], cut(dl), cut(nm), cut(nv))

    res = [loss, grad_x[None]]
    for k in range(4):
        res += [out[n][k] for n in ORDER]
    return tuple(res)


def kernel(x, ln0_g, ln0_b, w_in, b_in, conv_w, w_a, w_b, w_o, b_o, ln1_g, ln1_b, w_up, b_up, ffn_conv_w, ffn_conv_b, w_down, b_down, ln2_g, ln2_b, loss_target, m_ln0_g, m_ln0_b, m_w_in, m_b_in, m_conv_w, m_w_a, m_w_b, m_w_o, m_b_o, m_ln1_g, m_ln1_b, m_w_up, m_b_up, m_ffn_conv_w, m_ffn_conv_b, m_w_down, m_b_down, m_ln2_g, m_ln2_b, v_ln0_g, v_ln0_b, v_w_in, v_b_in, v_conv_w, v_w_a, v_w_b, v_w_o, v_b_o, v_ln1_g, v_ln1_b, v_w_up, v_b_up, v_ffn_conv_w, v_ffn_conv_b, v_w_down, v_b_down, v_ln2_g, v_ln2_b):
    W = dict(zip(ORDER, (ln0_g, ln0_b, w_in, b_in, conv_w, w_a, w_b, w_o, b_o, ln1_g, ln1_b, w_up, b_up,
                         ffn_conv_w, ffn_conv_b, w_down, b_down, ln2_g, ln2_b)))
    Mo = dict(zip(ORDER, (m_ln0_g, m_ln0_b, m_w_in, m_b_in, m_conv_w, m_w_a, m_w_b, m_w_o, m_b_o, m_ln1_g, m_ln1_b,
                          m_w_up, m_b_up, m_ffn_conv_w, m_ffn_conv_b, m_w_down, m_b_down, m_ln2_g, m_ln2_b)))
    Vo = dict(zip(ORDER, (v_ln0_g, v_ln0_b, v_w_in, v_b_in, v_conv_w, v_w_a, v_w_b, v_w_o, v_b_o, v_ln1_g, v_ln1_b,
                          v_w_up, v_b_up, v_ffn_conv_w, v_ffn_conv_b, v_w_down, v_b_down, v_ln2_g, v_ln2_b)))
    return _step(x, loss_target, W, Mo, Vo)
```

```python
import functools
import math

import jax
import jax.numpy as jnp
from jax import lax
from jax.experimental import pallas as pl
from jax.experimental.pallas import tpu as pltpu

F32 = jnp.float32
BF16 = jnp.bfloat16

D_MODEL = 1024
D_CONV = D_MODEL
HEAD_DIM = 64
HEADS_PER_GROUP = 8
GROUPS = ((128, 1), (512, 4), (2048, 16))
N_GROUPS = len(GROUPS)
GROUP_W = HEADS_PER_GROUP * HEAD_DIM
QKV_W = N_GROUPS * GROUP_W
RADIUS = 64
D_FF = 2816
LN_EPS = 1e-5
ALPHA = 2.0 ** 0.25
MASK_VALUE = -1e30
ATT_SCALE = HEAD_DIM ** -0.5
OFF_B = 0
OFF_C = OFF_B + D_CONV
OFF_H = OFF_C + D_CONV
OFF_Q = OFF_H + D_CONV
OFF_K = OFF_Q + QKV_W
OFF_V = OFF_K + QKV_W
OFF_GA = OFF_V + QKV_W
OFF_GB = OFF_GA + D_MODEL
N_IN = OFF_GB + D_MODEL
ADAM_LR = 0.001
ADAM_B1 = 0.9
ADAM_B2 = 0.999
ADAM_EPS = 1e-08
ADAM_WD = 0.01
ADAM_STEP = 10
INV_SQRT2 = 0.7071067811865476
INV_SQRT_2PI = 0.3989422804014327

LANES = 128
SUBLANES = 8
VMEM_BYTES_V7X = 64 * 1024 * 1024
N_CHIPS = 4
N_CORES = 2
N_DEV = N_CHIPS * N_CORES
MESH = pl.DeviceIdType.MESH

N_BLK = N_IN // GROUP_W
PERM = (0, 1, 2, 3, 4, 5, 15, 16, 17, 18, 6, 9, 12, 7, 10, 13, 8, 11, 14)
INV_PERM = tuple(PERM.index(b) for b in range(N_BLK))
P_B, P_C, P_H, P_GA, P_GB, P_Q0 = 0, 1024, 2048, 3072, 4096, 5120
N_NAT = P_Q0 + QKV_W // N_GROUPS * 3
N_GATED = P_Q0

SLAB = 128
CHUNK = 256
PAD = SUBLANES
TQ = 128


def _cparams(sem, vmem_mb):
    assert vmem_mb * 1024 * 1024 < VMEM_BYTES_V7X
    return pltpu.CompilerParams(dimension_semantics=sem, vmem_limit_bytes=vmem_mb * 1024 * 1024)


def _dot(a, b):
    return jnp.dot(a, b, preferred_element_type=F32)


def _dot_nt(a, b):
    return lax.dot_general(a, b, (((1,), (1,)), ((), ())), preferred_element_type=F32)


def _dot_tn(a, b):
    return lax.dot_general(a, b, (((0,), (0,)), ((), ())), preferred_element_type=F32)


def _ln_stats(z):
    mu = jnp.mean(z, -1, keepdims=True)
    zc = z - mu
    var = jnp.mean(zc * zc, -1, keepdims=True)
    rstd = lax.rsqrt(var + LN_EPS)
    return zc * rstd, rstd


def _ln_bwd(dh, xhat, rstd, g):
    dxh = dh * g
    m1 = jnp.mean(dxh, -1, keepdims=True)
    m2 = jnp.mean(dxh * xhat, -1, keepdims=True)
    return rstd * (dxh - m1 - xhat * m2)


def _rows8(rows, width):
    pad = [jnp.zeros((1, width), F32)] * (SUBLANES - len(rows))
    return jnp.concatenate(list(rows) + pad, axis=0)


def _mm_nn(a, w, bias, *, tm, tn, out_dtype, name, vmem_mb=40):
    M, K = a.shape
    if w.ndim == 3:
        assert w.shape[2] == tn
        n_tiles = w.shape[0]
        w_spec = pl.BlockSpec((None, K, tn), lambda i, j: (j, 0, 0))
    else:
        n_tiles = w.shape[1] // tn
        w_spec = pl.BlockSpec((K, tn), lambda i, j: (0, j))

    def body(a_ref, w_ref, b_ref, o_ref):
        o_ref[...] = (_dot(a_ref[...], w_ref[...]) + b_ref[...]).astype(o_ref.dtype)

    return pl.pallas_call(
        body, grid=(M // tm, n_tiles),
        in_specs=[pl.BlockSpec((tm, K), lambda i, j: (i, 0)), w_spec, pl.BlockSpec((1, tn), lambda i, j: (0, j))],
        out_specs=pl.BlockSpec((tm, tn), lambda i, j: (i, j)),
        out_shape=jax.ShapeDtypeStruct((M, n_tiles * tn), out_dtype),
        name=name, compiler_params=_cparams(("parallel", "arbitrary"), vmem_mb))(a, w, bias)


def _mm_nt(a, w, *, tm, a_col=0, name, vmem_mb=40):
    M = a.shape[0]
    N, K = w.shape

    def body(a_ref, w_ref, o_ref):
        o_ref[...] = _dot_nt(a_ref[...], w_ref[...]).astype(o_ref.dtype)

    return pl.pallas_call(
        body, grid=(M // tm,),
        in_specs=[pl.BlockSpec((tm, K), lambda i: (i, a_col)),
                  pl.BlockSpec((N, K), lambda i: (0, 0))],
        out_specs=pl.BlockSpec((tm, N), lambda i: (i, 0)),
        out_shape=jax.ShapeDtypeStruct((M, N), BF16),
        name=name, compiler_params=_cparams(("parallel",), vmem_mb))(a, w)


def _mm_tn(a, g, *, n_out, tn, ts, g_block, g_map, colsum=False, name, vmem_mb=48):
    S, K = a.shape
    n_s = S // ts

    def body(a_ref, g_ref, *rest):
        if colsum:
            o_ref, cs_ref, acc_ref, cacc_ref = rest
        else:
            o_ref, acc_ref = rest
        s = pl.program_id(1)

        @pl.when(s == 0)
        def _():
            acc_ref[...] = jnp.zeros_like(acc_ref)
            if colsum:
                cacc_ref[...] = jnp.zeros_like(cacc_ref)

        gv = g_ref[...]
        acc_ref[...] += _dot_tn(a_ref[...], gv)
        if colsum:
            cacc_ref[...] += jnp.broadcast_to(jnp.sum(gv.astype(F32), axis=0, keepdims=True), cacc_ref.shape)

        @pl.when(s == n_s - 1)
        def _():
            o_ref[...] = acc_ref[...].astype(o_ref.dtype)
            if colsum:
                cs_ref[...] = cacc_ref[...]

    out_specs = [pl.BlockSpec((None, K, tn), lambda j, s: (j, 0, 0))]
    out_shape = [jax.ShapeDtypeStruct((n_out, K, tn), BF16)]
    scratch = [pltpu.VMEM((K, tn), F32)]
    if colsum:
        out_specs.append(pl.BlockSpec((SUBLANES, tn), lambda j, s: (0, j)))
        out_shape.append(jax.ShapeDtypeStruct((SUBLANES, n_out * tn), F32))
        scratch.append(pltpu.VMEM((SUBLANES, tn), F32))
    res = pl.pallas_call(
        body, grid=(n_out, n_s),
        in_specs=[pl.BlockSpec((ts, K), lambda j, s: (s, 0)), pl.BlockSpec(g_block, g_map)],
        out_specs=out_specs, out_shape=out_shape, scratch_shapes=scratch,
        name=name, compiler_params=_cparams(("parallel", "arbitrary"), vmem_mb))(a, g)
    return res if colsum else res[0]


DILS = tuple(d for _, d in GROUPS if d > 1)


def _res_spec(d, tm, width):
    return pl.BlockSpec((d, tm // d, width), lambda i: (0, i, 0))


def _lane_scratch(tm, width):
    return [pltpu.VMEM((tm, LANES), F32)] * (width // LANES)


def _to_residue(val, dst_refs, dils, tm, dtype, scr):
    for c, ref in enumerate(scr):
        ref[...] = val[:, c * LANES:(c + 1) * LANES]
    for dst_ref, d in zip(dst_refs, dils):
        for r in range(d):
            cols = [ref[pl.ds(r, tm // d, stride=d), :] for ref in scr]
            dst_ref[r] = jnp.concatenate(cols, axis=1).astype(dtype)


def _from_residue(rows_of, d, tm, scr):
    for r in range(d):
        v = rows_of(r).astype(F32)
        for c, ref in enumerate(scr):
            ref[pl.ds(r, tm // d, stride=d), :] = v[:, c * LANES:(c + 1) * LANES]
    return jnp.concatenate([ref[...] for ref in scr], axis=1)


def _ln0_fwd(x, g, b, *, tm=512):
    S, Dm = x.shape

    def body(x_ref, g_ref, b_ref, h_ref, hb_ref, *rest):
        xhat, _ = _ln_stats(x_ref[...])
        h = xhat * g_ref[...] + b_ref[...]
        h_ref[...] = h
        hb_ref[...] = h.astype(BF16)
        _to_residue(h, rest[:len(DILS)], DILS, tm, BF16, rest[len(DILS):])

    row = pl.BlockSpec((tm, Dm), lambda i: (i, 0))
    vec = pl.BlockSpec((1, Dm), lambda i: (0, 0))
    return pl.pallas_call(
        body, grid=(S // tm,), in_specs=[row, vec, vec], out_specs=[row, row] + [_res_spec(d, tm, Dm) for d in DILS],
        out_shape=[jax.ShapeDtypeStruct((S, Dm), F32), jax.ShapeDtypeStruct((S, Dm), BF16)]
        + [jax.ShapeDtypeStruct((d, S // d, Dm), BF16) for d in DILS],
        scratch_shapes=_lane_scratch(tm, Dm),
        name="ln0_fwd", compiler_params=_cparams(("parallel",), 32))(x, g, b)


def _slab_spec(S, col0):
    return pl.BlockSpec((S, SLAB), lambda j: (0, col0 // SLAB + j))


def _zero_pads(scr, S):
    scr[0:PAD, :] = jnp.zeros((PAD, SLAB), F32)
    scr[S + PAD:S + 2 * PAD, :] = jnp.zeros((PAD, SLAB), F32)


def _shifted(scr, t):
    return (scr[PAD - 1 + t:PAD - 1 + t + CHUNK, :], scr[PAD + t:PAD + t + CHUNK, :],
            scr[PAD + 1 + t:PAD + 1 + t + CHUNK, :])


def _conv_gate_fwd(proj, conv_w):
    S = proj.shape[0]

    def body(b_ref, c_ref, h_ref, w_ref, o_ref, u_scr):
        _zero_pads(u_scr, S)
        for t in range(0, S, CHUNK):
            u_scr[PAD + t:PAD + t + CHUNK, :] = c_ref[t:t + CHUNK, :].astype(F32) * h_ref[t:t + CHUNK, :].astype(F32)
        w0, w1, w2 = w_ref[0:1, :], w_ref[1:2, :], w_ref[2:3, :]
        for t in range(0, S, CHUNK):
            um, u0, up = _shifted(u_scr, t)
            cv = w0 * um + w1 * u0 + w2 * up
            o_ref[t:t + CHUNK, :] = (b_ref[t:t + CHUNK, :].astype(F32) * cv).astype(BF16)

    return pl.pallas_call(
        body, grid=(D_CONV // SLAB,),
        in_specs=[_slab_spec(S, P_B), _slab_spec(S, P_C), _slab_spec(S, P_H),
                  pl.BlockSpec((3, SLAB), lambda j: (0, j))],
        out_specs=pl.BlockSpec((S, SLAB), lambda j: (0, j)),
        out_shape=jax.ShapeDtypeStruct((S, D_CONV), BF16),
        scratch_shapes=[pltpu.VMEM((S + 2 * PAD, SLAB), F32)],
        name="conv_gate_fwd", compiler_params=_cparams(("parallel",), 40))(proj, proj, proj, conv_w)


def _attn_masks(i, sub, dil):
    a = lax.broadcasted_iota(jnp.int32, (TQ, 2 * TQ), 0)
    j = lax.broadcasted_iota(jnp.int32, (TQ, 2 * TQ), 1)
    rel = jnp.abs(j - RADIUS - a)
    kpos = i * TQ - RADIUS + j
    valid = (rel <= RADIUS) & (kpos >= 0) & (kpos < sub)
    return valid, -(rel * dil).astype(F32)


def _slope(g, h):
    return 2.0 ** (-8.0 * (g * HEADS_PER_GROUP + h + 1) / (N_GROUPS * HEADS_PER_GROUP))


def _window(p_ref, c_ref, n_ref):
    return jnp.concatenate([p_ref[TQ - RADIUS:, :], c_ref[...], n_ref[:RADIUS, :]], axis=0)


def _qkv_specs(nb, col0):
    def spec(col, shift):
        return pl.BlockSpec((None, TQ, GROUP_W), lambda r, i: (r, jnp.clip(i + shift, 0, nb - 1), col))

    return [spec(col0, 0), spec(col0 + 1, -1), spec(col0 + 1, 0), spec(col0 + 1, 1),
            spec(col0 + 2, -1), spec(col0 + 2, 0), spec(col0 + 2, 1)]


def _attn_fwd(qkv, col0, g):
    dil, sub, _ = qkv.shape
    nb = sub // TQ

    def body(q_ref, kp, kc, kn, vp, vc, vn, o_ref, lse_ref):
        valid, base = _attn_masks(pl.program_id(1), sub, dil)
        kwin = _window(kp, kc, kn)
        vwin = _window(vp, vc, vn)
        q = q_ref[...]
        for h in range(HEADS_PER_GROUP):
            sl = slice(h * HEAD_DIM, (h + 1) * HEAD_DIM)
            s = _dot_nt(q[:, sl], kwin[:, sl]) * ATT_SCALE + _slope(g, h) * base
            s = jnp.where(valid, s, MASK_VALUE)
            m = jnp.max(s, -1, keepdims=True)
            p = jnp.exp(s - m)
            den = jnp.sum(p, -1, keepdims=True)
            o_ref[:, sl] = _dot(p.astype(BF16), vwin[:, sl]) / den
            lse_ref[:, sl] = jnp.broadcast_to(m + jnp.log(den), (TQ, HEAD_DIM))

    out = pl.BlockSpec((None, TQ, GROUP_W), lambda r, i: (r, i, 0))
    return pl.pallas_call(
        body, grid=(dil, nb), in_specs=_qkv_specs(nb, col0), out_specs=[out, out],
        out_shape=[jax.ShapeDtypeStruct((dil, sub, GROUP_W), F32)] * 2,
        name=f"attn_fwd_g{g}", compiler_params=_cparams(("parallel", "arbitrary"), 32))(*([qkv] * 7))


def _attn_combine(outs, lses, *, tm=512):
    S = outs[0].shape[1]
    n_col = GROUP_W // LANES

    def body(*refs):
        ins, (c_ref, cb_ref, lt_ref) = refs[:2 * N_GROUPS], refs[2 * N_GROUPS:2 * N_GROUPS + 3]
        scr = refs[2 * N_GROUPS + 3:]
        o, l = [ins[0][0]], [ins[N_GROUPS][0]]
        for k, d in enumerate(DILS):
            o_ref, l_ref = ins[1 + k], ins[N_GROUPS + 1 + k]
            o.append(_from_residue(lambda r: o_ref[r], d, tm, scr[2 * k * n_col:(2 * k + 1) * n_col]))
            l.append(_from_residue(lambda r: l_ref[r], d, tm, scr[(2 * k + 1) * n_col:(2 * k + 2) * n_col]))
        m = jnp.maximum(jnp.maximum(l[0], l[1]), l[2])
        e = [jnp.exp(v - m) for v in l]
        den = e[0] + e[1] + e[2]
        comb = (e[0] * o[0] + e[1] * o[1] + e[2] * o[2]) / den
        c_ref[...] = comb
        cb_ref[...] = comb.astype(BF16)
        lt_ref[...] = m + jnp.log(den)

    row = pl.BlockSpec((tm, GROUP_W), lambda i: (i, 0))
    specs = [_res_spec(GROUPS[g][1], tm, GROUP_W) for g in range(N_GROUPS)]
    return pl.pallas_call(
        body, grid=(S // tm,), in_specs=specs * 2, out_specs=[row] * 3,
        out_shape=[jax.ShapeDtypeStruct((S, GROUP_W), F32), jax.ShapeDtypeStruct((S, GROUP_W), BF16),
                   jax.ShapeDtypeStruct((S, GROUP_W), F32)],
        scratch_shapes=_lane_scratch(tm, GROUP_W) * (2 * len(DILS)),
        name="attn_combine", compiler_params=_cparams(("parallel",), 32))(*outs, *lses)


def _branch_mix(ya_in, comb_b, w_a, w_b, proj, *, tm=512):
    S = ya_in.shape[0]

    def body(ya_ref, cb_ref, wa_ref, wb_ref, ga_ref, gb_ref, yab_ref, mx_ref):
        y_a = _dot(ya_ref[...], wa_ref[...])
        y_b = _dot(cb_ref[...], wb_ref[...])
        yab_ref[:, 0:D_MODEL] = y_a.astype(BF16)
        yab_ref[:, D_MODEL:2 * D_MODEL] = y_b.astype(BF16)
        mx = jax.nn.sigmoid(ga_ref[...].astype(F32)) * y_a + jax.nn.sigmoid(gb_ref[...].astype(F32)) * y_b
        mx_ref[...] = mx.astype(BF16)

    return pl.pallas_call(
        body, grid=(S // tm,),
        in_specs=[pl.BlockSpec((tm, D_CONV), lambda i: (i, 0)), pl.BlockSpec((tm, GROUP_W), lambda i: (i, 0)),
                  pl.BlockSpec((D_CONV, D_MODEL), lambda i: (0, 0)), pl.BlockSpec((GROUP_W, D_MODEL), lambda i: (0, 0)),
                  pl.BlockSpec((tm, D_MODEL), lambda i: (i, P_GA // D_MODEL)),
                  pl.BlockSpec((tm, D_MODEL), lambda i: (i, P_GB // D_MODEL))],
        out_specs=[pl.BlockSpec((tm, 2 * D_MODEL), lambda i: (i, 0)), pl.BlockSpec((tm, D_MODEL), lambda i: (i, 0))],
        out_shape=[jax.ShapeDtypeStruct((S, 2 * D_MODEL), BF16), jax.ShapeDtypeStruct((S, D_MODEL), BF16)],
        name="branch_mix", compiler_params=_cparams(("parallel",), 40))(ya_in, comb_b, w_a, w_b, proj, proj)


def _mix_ln1(mixin, w_o, b_o, h0, g1, b1, *, tm=512):
    S = mixin.shape[0]

    def body(mx_ref, wo_ref, bo_ref, h0_ref, g_ref, b_ref, xh_ref, rs_ref, h1b_ref):
        z = ALPHA * h0_ref[...] + _dot(mx_ref[...], wo_ref[...]) + bo_ref[...]
        xhat, rstd = _ln_stats(z)
        xh_ref[...] = xhat
        rs_ref[...] = jnp.broadcast_to(rstd, (tm, LANES))
        h1b_ref[...] = (xhat * g_ref[...] + b_ref[...]).astype(BF16)

    row = pl.BlockSpec((tm, D_MODEL), lambda i: (i, 0))
    vec = pl.BlockSpec((1, D_MODEL), lambda i: (0, 0))
    return pl.pallas_call(
        body, grid=(S // tm,),
        in_specs=[row, pl.BlockSpec((D_MODEL, D_MODEL), lambda i: (0, 0)), vec, row, vec, vec],
        out_specs=[row, pl.BlockSpec((tm, LANES), lambda i: (i, 0)), row],
        out_shape=[jax.ShapeDtypeStruct((S, D_MODEL), F32), jax.ShapeDtypeStruct((S, LANES), F32),
                   jax.ShapeDtypeStruct((S, D_MODEL), BF16)],
        name="mix_ln1", compiler_params=_cparams(("parallel",), 40))(mixin, w_o, b_o, h0, g1, b1)


def _gelu_parts(cz):
    cdf = 0.5 * (1.0 + lax.erf(cz * INV_SQRT2))
    return cdf, cz * cdf


def _ffn_conv_fwd(up, cw, cb):
    S = up.shape[0]

    def body(a_ref, g_ref, w_ref, cb_ref, o_ref, a_scr):
        _zero_pads(a_scr, S)
        for t in range(0, S, CHUNK):
            a_scr[PAD + t:PAD + t + CHUNK, :] = a_ref[t:t + CHUNK, :].astype(F32)
        w0, w1, w2 = w_ref[0:1, :], w_ref[1:2, :], w_ref[2:3, :]
        for t in range(0, S, CHUNK):
            am, a0, ap = _shifted(a_scr, t)
            _, gel = _gelu_parts(w0 * am + w1 * a0 + w2 * ap + cb_ref[...])
            o_ref[t:t + CHUNK, :] = (gel * g_ref[t:t + CHUNK, :].astype(F32)).astype(BF16)

    return pl.pallas_call(
        body, grid=(D_FF // SLAB,),
        in_specs=[_slab_spec(S, 0), _slab_spec(S, D_FF), pl.BlockSpec((3, SLAB), lambda j: (0, j)),
                  pl.BlockSpec((1, SLAB), lambda j: (0, j))],
        out_specs=pl.BlockSpec((S, SLAB), lambda j: (0, j)),
        out_shape=jax.ShapeDtypeStruct((S, D_FF), BF16),
        scratch_shapes=[pltpu.VMEM((S + 2 * PAD, SLAB), F32)],
        name="ffn_conv_fwd", compiler_params=_cparams(("parallel",), 40))(up, up, cw, cb)


def _down_ln2_loss(f, w_down, b_down, xhat1, g1, b1, g2, b2, target, *, tm=256):
    S = f.shape[0]

    def body(f_ref, wd_ref, bd_ref, xh1_ref, g1_ref, b1_ref, g2_ref, b2_ref, t_ref, dz_ref, dzb_ref, st_ref):
        h1 = xh1_ref[...] * g1_ref[...] + b1_ref[...]
        z = ALPHA * h1 + _dot(f_ref[...], wd_ref[...]) + bd_ref[...]
        xhat, rstd = _ln_stats(z)
        err = xhat * g2_ref[...] + b2_ref[...] - t_ref[...]
        loss = (0.5 / D_MODEL) * jnp.sum(jnp.sum(err * err, axis=1, keepdims=True), axis=0, keepdims=True)
        dh2 = err * (1.0 / D_MODEL)
        dz = _ln_bwd(dh2, xhat, rstd, g2_ref[...])
        dz_ref[...] = dz
        dzb_ref[...] = dz.astype(BF16)
        upd = _rows8([jnp.sum(dh2 * xhat, axis=0, keepdims=True), jnp.sum(dh2, axis=0, keepdims=True),
                      jnp.broadcast_to(loss, (1, D_MODEL)), jnp.sum(dz, axis=0, keepdims=True)], D_MODEL)

        @pl.when(pl.program_id(0) == 0)
        def _():
            st_ref[...] = upd

        @pl.when(pl.program_id(0) != 0)
        def _():
            st_ref[...] += upd

    row = pl.BlockSpec((tm, D_MODEL), lambda i: (i, 0))
    vec = pl.BlockSpec((1, D_MODEL), lambda i: (0, 0))
    return pl.pallas_call(
        body, grid=(S // tm,),
        in_specs=[pl.BlockSpec((tm, D_FF), lambda i: (i, 0)), pl.BlockSpec((D_FF, D_MODEL), lambda i: (0, 0)),
                  vec, row, vec, vec, vec, vec, row],
        out_specs=[row, row, pl.BlockSpec((SUBLANES, D_MODEL), lambda i: (0, 0))],
        out_shape=[jax.ShapeDtypeStruct((S, D_MODEL), F32), jax.ShapeDtypeStruct((S, D_MODEL), BF16),
                   jax.ShapeDtypeStruct((SUBLANES, D_MODEL), F32)],
        name="down_ln2_loss", compiler_params=_cparams(("arbitrary",), 48))(
            f, w_down, b_down, xhat1, g1, b1, g2, b2, target)


def _ffn_conv_bwd(up, df, cw, cb):
    S = up.shape[0]

    def body(a_ref, g_ref, df_ref, w_ref, cb_ref, dup_ref, sm_ref, a_scr, d_scr):
        _zero_pads(a_scr, S)
        _zero_pads(d_scr, S)
        for t in range(0, S, CHUNK):
            a_scr[PAD + t:PAD + t + CHUNK, :] = a_ref[t:t + CHUNK, :].astype(F32)
        w0, w1, w2 = w_ref[0:1, :], w_ref[1:2, :], w_ref[2:3, :]
        zero = jnp.zeros((1, SLAB), F32)
        s_dg, s_dcz, s_w0, s_w1, s_w2 = zero, zero, zero, zero, zero
        for t in range(0, S, CHUNK):
            am, a0, ap = _shifted(a_scr, t)
            cz = w0 * am + w1 * a0 + w2 * ap + cb_ref[...]
            cdf, gel = _gelu_parts(cz)
            dfv = df_ref[t:t + CHUNK, :].astype(F32)
            dgte = dfv * gel
            dcz = dfv * g_ref[t:t + CHUNK, :].astype(F32) * (cdf + cz * jnp.exp(-0.5 * cz * cz) * INV_SQRT_2PI)
            dup_ref[1, t:t + CHUNK, :] = dgte.astype(BF16)
            d_scr[PAD + t:PAD + t + CHUNK, :] = dcz
            s_dg = s_dg + jnp.sum(dgte, axis=0, keepdims=True)
            s_dcz = s_dcz + jnp.sum(dcz, axis=0, keepdims=True)
            s_w0 = s_w0 + jnp.sum(dcz * am, axis=0, keepdims=True)
            s_w1 = s_w1 + jnp.sum(dcz * a0, axis=0, keepdims=True)
            s_w2 = s_w2 + jnp.sum(dcz * ap, axis=0, keepdims=True)
        s_da = zero
        for t in range(0, S, CHUNK):
            dm, d0, dp = _shifted(d_scr, t)
            da = w0 * dp + w1 * d0 + w2 * dm
            dup_ref[0, t:t + CHUNK, :] = da.astype(BF16)
            s_da = s_da + jnp.sum(da, axis=0, keepdims=True)
        sm_ref[...] = _rows8([s_da, s_dg, s_dcz, s_w0, s_w1, s_w2], SLAB)

    return pl.pallas_call(
        body, grid=(D_FF // SLAB,),
        in_specs=[_slab_spec(S, 0), _slab_spec(S, D_FF), pl.BlockSpec((S, SLAB), lambda j: (0, j)),
                  pl.BlockSpec((3, SLAB), lambda j: (0, j)), pl.BlockSpec((1, SLAB), lambda j: (0, j))],
        out_specs=[pl.BlockSpec((2, S, SLAB), lambda j: (0, 0, j)), pl.BlockSpec((SUBLANES, SLAB), lambda j: (0, j))],
        out_shape=[jax.ShapeDtypeStruct((2, S, D_FF), BF16), jax.ShapeDtypeStruct((SUBLANES, D_FF), F32)],
        scratch_shapes=[pltpu.VMEM((S + 2 * PAD, SLAB), F32)] * 2,
        name="ffn_conv_bwd", compiler_params=_cparams(("parallel",), 48))(up, up, df, cw, cb)


def _up_bwd_ln1(dup, w_up3, dz2, xhat1, rstd1, g1, *, tm=512):
    S = dz2.shape[0]
    ns, _, tk = w_up3.shape
    per_plane = D_FF // tk

    def body(du_ref, w_ref, dz2_ref, xh_ref, rs_ref, g_ref, dz_ref, dzb_ref, st_ref, acc_ref):
        i, k = pl.program_id(0), pl.program_id(1)

        @pl.when(k == 0)
        def _():
            acc_ref[...] = ALPHA * dz2_ref[...]

        acc_ref[...] += _dot_nt(du_ref[...], w_ref[...])

        @pl.when(k == ns - 1)
        def _():
            dh = acc_ref[...]
            xhat = xh_ref[...]
            dz = _ln_bwd(dh, xhat, rs_ref[:, 0:1], g_ref[...])
            dz_ref[...] = dz
            dzb_ref[...] = dz.astype(BF16)
            upd = _rows8([jnp.sum(dh * xhat, axis=0, keepdims=True), jnp.sum(dh, axis=0, keepdims=True),
                          jnp.sum(dz, axis=0, keepdims=True)], D_MODEL)

            @pl.when(i == 0)
            def _():
                st_ref[...] = upd

            @pl.when(i != 0)
            def _():
                st_ref[...] += upd

    row = pl.BlockSpec((tm, D_MODEL), lambda i, k: (i, 0))
    return pl.pallas_call(
        body, grid=(S // tm, ns),
        in_specs=[pl.BlockSpec((None, tm, tk), lambda i, k: (k // per_plane, i, k % per_plane)),
                  pl.BlockSpec((None, D_MODEL, tk), lambda i, k: (k, 0, 0)),
                  row, row, pl.BlockSpec((tm, LANES), lambda i, k: (i, 0)),
                  pl.BlockSpec((1, D_MODEL), lambda i, k: (0, 0))],
        out_specs=[row, row, pl.BlockSpec((SUBLANES, D_MODEL), lambda i, k: (0, 0))],
        out_shape=[jax.ShapeDtypeStruct((S, D_MODEL), F32), jax.ShapeDtypeStruct((S, D_MODEL), BF16),
                   jax.ShapeDtypeStruct((SUBLANES, D_MODEL), F32)],
        scratch_shapes=[pltpu.VMEM((tm, D_MODEL), F32)],
        name="up_bwd_ln1", compiler_params=_cparams(("arbitrary", "arbitrary"), 40))(
            dup, w_up3, dz2, xhat1, rstd1, g1)


def _mix_bwd(dz1b, w_o, proj, yab, *, tm=512):
    S = dz1b.shape[0]
    half = D_MODEL // 2
    gate0 = P_GA // half

    def body(dz_ref, wo_ref, gt_ref, y_ref, dy_ref, dg_ref):
        dmx = _dot_nt(dz_ref[...], wo_ref[...])
        sg = jax.nn.sigmoid(gt_ref[...].astype(F32))
        dy_ref[...] = (dmx * sg).astype(BF16)
        dg_ref[...] = (dmx * y_ref[...].astype(F32) * sg * (1.0 - sg)).astype(BF16)

    blk = pl.BlockSpec((tm, half), lambda i, j: (i, j))
    return pl.pallas_call(
        body, grid=(S // tm, 4),
        in_specs=[pl.BlockSpec((tm, D_MODEL), lambda i, j: (i, 0)),
                  pl.BlockSpec((half, D_MODEL), lambda i, j: (j % 2, 0)),
                  pl.BlockSpec((tm, half), lambda i, j: (i, gate0 + j)), blk],
        out_specs=[blk, pl.BlockSpec((None, tm, half), lambda i, j: (3 + j // 2, i, j % 2))],
        out_shape=[jax.ShapeDtypeStruct((S, 2 * D_MODEL), BF16),
                   jax.ShapeDtypeStruct((N_GATED // D_MODEL, S, D_MODEL), BF16)],
        name="mix_bwd", compiler_params=_cparams(("parallel", "arbitrary"), 32))(dz1b, w_o, proj, yab)


def _conv_gate_bwd(proj, dya_in, conv_w, dgated):
    S = proj.shape[0]

    def body(b_ref, c_ref, h_ref, dy_ref, w_ref, _, o_ref, sm_ref, u_scr, d_scr):
        _zero_pads(u_scr, S)
        _zero_pads(d_scr, S)
        for t in range(0, S, CHUNK):
            u_scr[PAD + t:PAD + t + CHUNK, :] = c_ref[t:t + CHUNK, :].astype(F32) * h_ref[t:t + CHUNK, :].astype(F32)
        w0, w1, w2 = w_ref[0:1, :], w_ref[1:2, :], w_ref[2:3, :]
        zero = jnp.zeros((1, SLAB), F32)
        s_w0, s_w1, s_w2 = zero, zero, zero
        for t in range(0, S, CHUNK):
            um, u0, up = _shifted(u_scr, t)
            dy = dy_ref[t:t + CHUNK, :].astype(F32)
            o_ref[0, t:t + CHUNK, :] = (dy * (w0 * um + w1 * u0 + w2 * up)).astype(BF16)
            dcv = dy * b_ref[t:t + CHUNK, :].astype(F32)
            d_scr[PAD + t:PAD + t + CHUNK, :] = dcv
            s_w0 = s_w0 + jnp.sum(dcv * um, axis=0, keepdims=True)
            s_w1 = s_w1 + jnp.sum(dcv * u0, axis=0, keepdims=True)
            s_w2 = s_w2 + jnp.sum(dcv * up, axis=0, keepdims=True)
        for t in range(0, S, CHUNK):
            dm, d0, dp = _shifted(d_scr, t)
            du = w0 * dp + w1 * d0 + w2 * dm
            o_ref[1, t:t + CHUNK, :] = (du * h_ref[t:t + CHUNK, :].astype(F32)).astype(BF16)
            o_ref[2, t:t + CHUNK, :] = (du * c_ref[t:t + CHUNK, :].astype(F32)).astype(BF16)
        sm_ref[...] = _rows8([s_w0, s_w1, s_w2], SLAB)

    return pl.pallas_call(
        body, grid=(D_CONV // SLAB,),
        in_specs=[_slab_spec(S, P_B), _slab_spec(S, P_C), _slab_spec(S, P_H),
                  pl.BlockSpec((S, SLAB), lambda j: (0, j)), pl.BlockSpec((3, SLAB), lambda j: (0, j)),
                  pl.BlockSpec(memory_space=pl.ANY)],
        out_specs=[pl.BlockSpec((3, S, SLAB), lambda j: (0, 0, j)), pl.BlockSpec((SUBLANES, SLAB), lambda j: (0, j))],
        out_shape=[jax.ShapeDtypeStruct(dgated.shape, BF16), jax.ShapeDtypeStruct((SUBLANES, D_CONV), F32)],
        scratch_shapes=[pltpu.VMEM((S + 2 * PAD, SLAB), F32)] * 2, input_output_aliases={5: 0},
        name="conv_gate_bwd", compiler_params=_cparams(("parallel",), 48))(proj, proj, proj, dya_in, conv_w, dgated)


def _comb_bwd(dyab, w_b, comb, lse_tot, *, tm=512):
    S = comb.shape[0]

    def body(dy_ref, wb_ref, c_ref, lt_ref, *rest):
        outs, scr = rest[:3 * N_GROUPS], rest[3 * N_GROUPS:]
        dcb = _dot_nt(dy_ref[...], wb_ref[...]).astype(BF16)
        dc = dcb.astype(F32)
        prod = dc * c_ref[...]
        heads = [jnp.broadcast_to(jnp.sum(prod[:, h * HEAD_DIM:(h + 1) * HEAD_DIM], axis=1, keepdims=True),
                                  (tm, HEAD_DIM)) for h in range(HEADS_PER_GROUP)]
        vals = (dc, lt_ref[...], jnp.concatenate(heads, axis=1))
        for k, (val, dtype) in enumerate(zip(vals, (BF16, F32, F32))):
            outs[k][0] = val.astype(dtype)
            _to_residue(val, [outs[3 * (1 + j) + k] for j in range(len(DILS))], DILS, tm, dtype, scr)

    row = pl.BlockSpec((tm, GROUP_W), lambda i: (i, 0))
    out_specs, out_shape = [], []
    for _, d in GROUPS:
        out_specs += [_res_spec(d, tm, GROUP_W)] * 3
        out_shape += [jax.ShapeDtypeStruct((d, S // d, GROUP_W), t) for t in (BF16, F32, F32)]
    res = pl.pallas_call(
        body, grid=(S // tm,),
        in_specs=[pl.BlockSpec((tm, D_MODEL), lambda i: (i, 1)), pl.BlockSpec((GROUP_W, D_MODEL), lambda i: (0, 0)),
                  row, row],
        out_specs=out_specs, out_shape=out_shape, scratch_shapes=_lane_scratch(tm, GROUP_W),
        name="comb_bwd", compiler_params=_cparams(("parallel",), 32))(dyab, w_b, comb, lse_tot)
    return [tuple(res[3 * g:3 * g + 3]) for g in range(N_GROUPS)]


def _attn_bwd(qkv, col0, g, dcomb, lse_tot, delta):
    dil, sub, _ = qkv.shape
    nb = sub // TQ

    def body(q_ref, kp, kc, kn, vp, vc, vn, do_ref, lse_ref, dl_ref, dq_ref, dk_ref, dv_ref, ak, av):
        i = pl.program_id(1)

        @pl.when(i == 0)
        def _():
            ak[...] = jnp.zeros_like(ak)
            av[...] = jnp.zeros_like(av)

        @pl.when(i < nb)
        def _():
            valid, base = _attn_masks(i, sub, dil)
            kwin = _window(kp, kc, kn)
            vwin = _window(vp, vc, vn)
            q = q_ref[...]
            do = do_ref[...]
            for h in range(HEADS_PER_GROUP):
                sl = slice(h * HEAD_DIM, (h + 1) * HEAD_DIM)
                s = _dot_nt(q[:, sl], kwin[:, sl]) * ATT_SCALE + _slope(g, h) * base
                s = jnp.where(valid, s, MASK_VALUE)
                p = jnp.exp(s - lse_ref[:, h * HEAD_DIM:h * HEAD_DIM + 1])
                dp = _dot_nt(do[:, sl], vwin[:, sl])
                ds = (p * (dp - dl_ref[:, h * HEAD_DIM:h * HEAD_DIM + 1])).astype(BF16)
                dq_ref[:, sl] = (_dot(ds, kwin[:, sl]) * ATT_SCALE).astype(BF16)
                ak[RADIUS:RADIUS + 2 * TQ, sl] += _dot_tn(ds, q[:, sl]) * ATT_SCALE
                av[RADIUS:RADIUS + 2 * TQ, sl] += _dot_tn(p.astype(BF16), do[:, sl])

        dk_ref[...] = ak[0:TQ, :].astype(BF16)
        dv_ref[...] = av[0:TQ, :].astype(BF16)
        ak[0:2 * TQ, :] = ak[TQ:3 * TQ, :]
        av[0:2 * TQ, :] = av[TQ:3 * TQ, :]
        ak[2 * TQ:3 * TQ, :] = jnp.zeros((TQ, GROUP_W), F32)
        av[2 * TQ:3 * TQ, :] = jnp.zeros((TQ, GROUP_W), F32)

    tok = pl.BlockSpec((None, TQ, GROUP_W), lambda r, i: (r, jnp.minimum(i, nb - 1), 0))
    dkv_spec = pl.BlockSpec((None, TQ, GROUP_W), lambda r, i: (r, jnp.maximum(i - 1, 0), 0))
    return pl.pallas_call(
        body, grid=(dil, nb + 1), in_specs=_qkv_specs(nb, col0) + [tok, tok, tok],
        out_specs=[tok, dkv_spec, dkv_spec], out_shape=[jax.ShapeDtypeStruct((dil, sub, GROUP_W), BF16)] * 3,
        scratch_shapes=[pltpu.VMEM((3 * TQ, GROUP_W), F32)] * 2,
        name=f"attn_bwd_g{g}", compiler_params=_cparams(("arbitrary", "arbitrary"), 32))(
            *([qkv] * 7), dcomb, lse_tot, delta)


def _gated_bwd(dgated, w_nat, dz1, *, tm=512):
    n_planes, S, _ = dgated.shape

    def body(dg_ref, w_ref, dz_ref, o_ref, acc_ref):
        k = pl.program_id(1)

        @pl.when(k == 0)
        def _():
            acc_ref[...] = ALPHA * dz_ref[...]

        acc_ref[...] += _dot_nt(dg_ref[...], w_ref[...])

        @pl.when(k == n_planes - 1)
        def _():
            o_ref[...] = acc_ref[...]

    row = pl.BlockSpec((tm, D_MODEL), lambda i, k: (i, 0))
    return pl.pallas_call(
        body, grid=(S // tm, n_planes),
        in_specs=[pl.BlockSpec((None, tm, D_MODEL), lambda i, k: (k, i, 0)),
                  pl.BlockSpec((D_MODEL, D_MODEL), lambda i, k: (0, k)), row],
        out_specs=row, out_shape=jax.ShapeDtypeStruct((S, D_MODEL), F32),
        scratch_shapes=[pltpu.VMEM((tm, D_MODEL), F32)],
        name="gated_bwd", compiler_params=_cparams(("parallel", "arbitrary"), 32))(dgated, w_nat, dz1)


def _in_bwd_ln0(dh0_part, dqkv, w_qkv, x, g0, *, tm=512):
    S = x.shape[0]
    n_in = 3 * N_GROUPS

    def body(*refs):
        dh_ref, d_refs, w_refs = refs[0], refs[1:1 + n_in], refs[1 + n_in:1 + n_in + N_GROUPS]
        x_ref, g_ref, gx_ref, st_ref, acc_ref, *tmp_ref = refs[1 + n_in + N_GROUPS:]
        acc_ref[...] = dh_ref[...]
        for g, (_, d) in enumerate(GROUPS):
            rows = [jnp.concatenate([d_refs[3 * g + k][r] for k in range(3)], axis=1) for r in range(d)]
            res = _dot_nt(jnp.concatenate(rows, axis=0), w_refs[g][...])
            if d == 1:
                acc_ref[...] += res
            else:
                n = tm // d
                acc_ref[...] += _from_residue(lambda r: res[r * n:(r + 1) * n, :], d, tm, tmp_ref)
        dh = acc_ref[...]
        xhat, rstd = _ln_stats(x_ref[...])
        gx_ref[...] = _ln_bwd(dh, xhat, rstd, g_ref[...])
        upd = _rows8([jnp.sum(dh * xhat, axis=0, keepdims=True), jnp.sum(dh, axis=0, keepdims=True)], D_MODEL)

        @pl.when(pl.program_id(0) == 0)
        def _():
            st_ref[...] = upd

        @pl.when(pl.program_id(0) != 0)
        def _():
            st_ref[...] += upd

    row = pl.BlockSpec((tm, D_MODEL), lambda i: (i, 0))
    d_specs = []
    for _, d in GROUPS:
        d_specs += [_res_spec(d, tm, GROUP_W)] * 3
    operands = [dh0_part] + [a for grp in dqkv for a in grp] + list(w_qkv) + [x, g0]
    return pl.pallas_call(
        body, grid=(S // tm,),
        in_specs=[row] + d_specs + [pl.BlockSpec((D_MODEL, QKV_W), lambda i: (0, 0))] * N_GROUPS
        + [row, pl.BlockSpec((1, D_MODEL), lambda i: (0, 0))],
        out_specs=[row, pl.BlockSpec((SUBLANES, D_MODEL), lambda i: (0, 0))],
        out_shape=[jax.ShapeDtypeStruct((S, D_MODEL), F32), jax.ShapeDtypeStruct((SUBLANES, D_MODEL), F32)],
        scratch_shapes=[pltpu.VMEM((tm, D_MODEL), F32)] + _lane_scratch(tm, D_MODEL),
        name="in_bwd_ln0", compiler_params=_cparams(("arbitrary",), 48))(*operands)


HBM_SPEC = pl.BlockSpec(memory_space=pltpu.HBM)


def _place():
    x, y, c = lax.axis_index("x"), lax.axis_index("y"), lax.axis_index("c")
    chips = [(1 - x, y), (x, 1 - y), (1 - x, 1 - y)]
    return x, y, c, chips


def _allgather_shards(shards):
    n = len(shards)

    def body(*refs):
        ins, outs = refs[:n], refs[n:2 * n]
        send_sems, recv_sems, loc_sems = refs[2 * n:]
        x, y, c, chips = _place()
        me = 2 * x + y
        local, sends = [], []
        for w in range(n):
            cp = pltpu.make_async_copy(ins[w], outs[w].at[me], loc_sems.at[w])
            cp.start()
            local.append(cp)
            for j, (px, py) in enumerate(chips):
                cp = pltpu.make_async_remote_copy(
                    src_ref=ins[w], dst_ref=outs[w].at[me], send_sem=send_sems.at[3 * w + j],
                    recv_sem=recv_sems.at[3 * w + j], device_id=(px, py, c), device_id_type=MESH)
                cp.start()
                sends.append(cp)
        for w in range(n):
            for j, (px, py) in enumerate(chips):
                pltpu.make_async_remote_copy(
                    src_ref=ins[w], dst_ref=outs[w].at[2 * px + py], send_sem=send_sems.at[3 * w + j],
                    recv_sem=recv_sems.at[3 * w + j], device_id=(px, py, c), device_id_type=MESH).wait_recv()
        for cp in sends:
            cp.wait_send()
        for cp in local:
            cp.wait()

    return pl.pallas_call(
        body, in_specs=[HBM_SPEC] * n, out_specs=[HBM_SPEC] * n,
        out_shape=[jax.ShapeDtypeStruct((N_CHIPS,) + s.shape, s.dtype) for s in shards],
        scratch_shapes=[pltpu.SemaphoreType.DMA((3 * n,)), pltpu.SemaphoreType.DMA((3 * n,)),
                        pltpu.SemaphoreType.DMA((n,))],
        name="allgather_weights")(*shards)


def _exchange_grads(grads):
    n = len(grads)
    per = 7

    def body(*refs):
        ins, outs = refs[:n], refs[n:2 * n]
        send_sems, recv_sems, loc_sems = refs[2 * n:]
        x, y, c, chips = _place()
        me = 2 * x + y
        sib = (x, y, 1 - c)

        def rcopy(w, k, src, dst, to):
            return pltpu.make_async_remote_copy(src_ref=src, dst_ref=dst, send_sem=send_sems.at[per * w + k],
                                                recv_sem=recv_sems.at[per * w + k], device_id=to, device_id_type=MESH)

        local, sends = [], []
        for w in range(n):
            cp = pltpu.make_async_copy(ins[w].at[me], outs[w].at[c, me], loc_sems.at[w])
            cp.start()
            local.append(cp)
            cp = rcopy(w, 0, ins[w].at[me], outs[w].at[c, me], sib)
            cp.start()
            sends.append(cp)
            for j, (px, py) in enumerate(chips):
                cp = rcopy(w, 1 + j, ins[w].at[2 * px + py], outs[w].at[c, me], (px, py, c))
                cp.start()
                sends.append(cp)
        for w in range(n):
            for j, (px, py) in enumerate(chips):
                slot = outs[w].at[c, 2 * px + py]
                rcopy(w, 1 + j, slot, slot, (px, py, c)).wait_recv()
                cp = rcopy(w, 4 + j, slot, slot, sib)
                cp.start()
                sends.append(cp)
        for w in range(n):
            slot = outs[w].at[1 - c, me]
            rcopy(w, 0, slot, slot, sib).wait_recv()
            for j, (px, py) in enumerate(chips):
                slot = outs[w].at[1 - c, 2 * px + py]
                rcopy(w, 4 + j, slot, slot, sib).wait_recv()
        for cp in sends:
            cp.wait_send()
        for cp in local:
            cp.wait()

    return pl.pallas_call(
        body, in_specs=[HBM_SPEC] * n, out_specs=[HBM_SPEC] * n,
        out_shape=[jax.ShapeDtypeStruct((N_CORES,) + g.shape, g.dtype) for g in grads],
        scratch_shapes=[pltpu.SemaphoreType.DMA((per * n,)), pltpu.SemaphoreType.DMA((per * n,)),
                        pltpu.SemaphoreType.DMA((n,))],
        name="exchange_grads")(*grads)


def _allgather_small(vec):
    def body(v_ref, o_ref, send_sems, recv_sems, loc_sem):
        x, y, c = lax.axis_index("x"), lax.axis_index("y"), lax.axis_index("c")
        me = 4 * x + 2 * y + c

        def peer(k):
            flip = lambda v, bit: 1 - v if (k >> bit) & 1 else v
            return flip(x, 2), flip(y, 1), flip(c, 0)

        loc = pltpu.make_async_copy(v_ref, o_ref.at[me], loc_sem)
        loc.start()
        sends = []
        for k in range(1, N_DEV):
            cp = pltpu.make_async_remote_copy(src_ref=v_ref, dst_ref=o_ref.at[me], send_sem=send_sems.at[k - 1],
                                              recv_sem=recv_sems.at[k - 1], device_id=peer(k), device_id_type=MESH)
            cp.start()
            sends.append(cp)
        for k in range(1, N_DEV):
            px, py, pc = peer(k)
            pltpu.make_async_remote_copy(src_ref=v_ref, dst_ref=o_ref.at[4 * px + 2 * py + pc],
                                         send_sem=send_sems.at[k - 1], recv_sem=recv_sems.at[k - 1],
                                         device_id=(px, py, pc), device_id_type=MESH).wait_recv()
        for cp in sends:
            cp.wait_send()
        loc.wait()

    return pl.pallas_call(
        body, in_specs=[HBM_SPEC], out_specs=HBM_SPEC,
        out_shape=jax.ShapeDtypeStruct((N_DEV,) + vec.shape, vec.dtype),
        scratch_shapes=[pltpu.SemaphoreType.DMA((N_DEV - 1,)), pltpu.SemaphoreType.DMA((N_DEV - 1,)),
                        pltpu.SemaphoreType.DMA],
        name="allgather_small")(vec)


def _adamw(w, g, m, v):
    m = ADAM_B1 * m + (1.0 - ADAM_B1) * g
    v = ADAM_B2 * v + (1.0 - ADAM_B2) * (g * g)
    m_hat = m / (1.0 - ADAM_B1 ** ADAM_STEP)
    v_hat = v / (1.0 - ADAM_B2 ** ADAM_STEP)
    delta = -ADAM_LR * (m_hat / (jnp.sqrt(v_hat) + ADAM_EPS) + ADAM_WD * w)
    return delta, m, v


def _reduce_adamw(parts, w, m, v, *, tr, name):
    R, C = w.shape

    def body(p_ref, w_ref, m_ref, v_ref, g_ref, d_ref, nm_ref, nv_ref):
        def core_sum(cc):
            s = p_ref[cc, 0].astype(F32)
            for k in range(1, N_CHIPS):
                s = s + p_ref[cc, k].astype(F32)
            return s

        g = core_sum(0) + core_sum(1)
        delta, nm, nv = _adamw(w_ref[...], g, m_ref[...], v_ref[...])
        g_ref[...] = g
        d_ref[...] = delta
        nm_ref[...] = nm
        nv_ref[...] = nv

    blk = pl.BlockSpec((tr, C), lambda i: (i, 0))
    return pl.pallas_call(
        body, grid=(R // tr,),
        in_specs=[pl.BlockSpec((N_CORES, N_CHIPS, tr, C), lambda i: (0, 0, i, 0)), blk, blk, blk],
        out_specs=[blk] * 4, out_shape=[jax.ShapeDtypeStruct((R, C), F32)] * 4,
        name=name, compiler_params=_cparams(("parallel",), 40))(parts, w, m, v)


def _sum_devices(allv):
    _, R, _ = allv.shape

    def body(a_ref, o_ref):
        s = a_ref[0]
        for d in range(1, N_DEV):
            s = s + a_ref[d]
        o_ref[...] = s

    return pl.pallas_call(body, out_shape=jax.ShapeDtypeStruct((R, LANES), F32), name="sum_small")(allv)


def _adamw_small(w, g, m, v):
    def body(w_ref, g_ref, m_ref, v_ref, d_ref, nm_ref, nv_ref):
        delta, nm, nv = _adamw(w_ref[...], g_ref[...], m_ref[...], v_ref[...])
        d_ref[...] = delta
        nm_ref[...] = nm
        nv_ref[...] = nv

    return pl.pallas_call(body, out_shape=[jax.ShapeDtypeStruct(w.shape, F32)] * 3, name="adamw_small")(w, g, m, v)


def _pack(pieces):
    flat = [p.reshape(-1) for p in pieces]
    offs, n = [], 0
    for f in flat:
        offs.append(n)
        n += f.shape[0]
    total = -(-n // (SUBLANES * LANES)) * SUBLANES * LANES
    flat.append(jnp.zeros((total - n,), F32))
    return jnp.concatenate(flat).reshape(total // LANES, LANES), offs


def _local_step(x, target, p, wfull):
    S = x.shape[0]
    w_in3, w_up3 = wfull["w_in"], wfull["w_up"]
    w_a, w_o, w_down, w_b = wfull["w_a"], wfull["w_o"], wfull["w_down"], wfull["w_b"]
    conv_w, ffn_conv_w = wfull["conv_w"], wfull["ffn_conv_w"]
    dils = [d for _, d in GROUPS]

    w_blocks = w_in3.transpose(1, 0, 2).reshape(D_MODEL, N_BLK, GROUP_W)
    w_perm = jnp.concatenate([w_blocks[:, b] for b in PERM], axis=1)
    b_blocks = p["b_in"].reshape(N_BLK, GROUP_W)
    b_perm = jnp.concatenate([b_blocks[b] for b in PERM]).reshape(1, N_IN)
    w_nat, b_nat = w_perm[:, :N_NAT], b_perm[:, :N_NAT]
    qkv_cols = [slice(P_Q0 + g * QKV_W, P_Q0 + (g + 1) * QKV_W) for g in range(N_GROUPS)]
    w_qkv = [w_perm[:, c] for c in qkv_cols]

    h0, h0b, *h0_res = _ln0_fwd(x, p["ln0_g"], p["ln0_b"])
    h0_rows = [h0b] + [h.reshape(S, D_MODEL) for h in h0_res]
    proj = _mm_nn(h0b, w_nat, b_nat, tm=512, tn=N_NAT // 2, out_dtype=BF16, name="proj")
    qkv = [proj[None]]
    for g in range(1, N_GROUPS):
        t = _mm_nn(h0_rows[g], w_qkv[g], b_perm[:, qkv_cols[g]], tm=512, tn=QKV_W, out_dtype=BF16, name=f"proj_qkv{g}")
        qkv.append(t.reshape(dils[g], S // dils[g], QKV_W))
    col0 = [P_Q0 // GROUP_W] + [0] * (N_GROUPS - 1)
    ya_in = _conv_gate_fwd(proj, conv_w)
    att = [_attn_fwd(qkv[g], col0[g], g) for g in range(N_GROUPS)]
    comb, comb_b, lse_tot = _attn_combine([a[0] for a in att], [a[1] for a in att])
    yab, mixin = _branch_mix(ya_in, comb_b, w_a, w_b, proj)
    xhat1, rstd1, h1b = _mix_ln1(mixin, w_o, p["b_o"], h0, p["ln1_g"], p["ln1_b"])
    up = _mm_nn(h1b, w_up3, p["b_up"], tm=512, tn=w_up3.shape[2], out_dtype=BF16, name="up")
    f = _ffn_conv_fwd(up, ffn_conv_w, p["ffn_conv_b"])
    dz2, dz2b, st2 = _down_ln2_loss(f, w_down, p["b_down"], xhat1, p["ln1_g"], p["ln1_b"],
                                    p["ln2_g"], p["ln2_b"], target)

    gw = {}
    gw["w_down"] = _mm_tn(f, dz2b, n_out=1, tn=D_MODEL, ts=512, g_block=(512, D_MODEL),
                          g_map=lambda j, s: (s, 0), name="grad_w_down").reshape(N_CHIPS, D_FF // N_CHIPS, D_MODEL)
    df = _mm_nt(dz2b, w_down, tm=512, name="df")
    dup, sm_ffn = _ffn_conv_bwd(up, df, ffn_conv_w, p["ffn_conv_b"])
    up_tn = w_up3.shape[2]
    up_pp = D_FF // up_tn
    gw["w_up"] = _mm_tn(h1b, dup, n_out=N_CHIPS, tn=up_tn, ts=512, g_block=(None, 512, up_tn),
                        g_map=lambda j, s: (j // up_pp, s, j % up_pp), name="grad_w_up")
    dz1, dz1b, st1 = _up_bwd_ln1(dup, w_up3, dz2, xhat1, rstd1, p["ln1_g"])

    gw["w_o"] = _mm_tn(mixin, dz1b, n_out=1, tn=D_MODEL, ts=512, g_block=(512, D_MODEL),
                       g_map=lambda j, s: (s, 0), name="grad_w_o").reshape(N_CHIPS, D_MODEL // N_CHIPS, D_MODEL)
    dyab, dgated = _mix_bwd(dz1b, w_o, proj, yab)
    gw["w_a"] = _mm_tn(ya_in, dyab, n_out=1, tn=D_MODEL, ts=512, g_block=(512, D_MODEL),
                       g_map=lambda j, s: (s, 0), name="grad_w_a").reshape(N_CHIPS, D_CONV // N_CHIPS, D_MODEL)
    b_tn = D_MODEL // N_CHIPS
    gw["w_b"] = _mm_tn(comb_b, dyab, n_out=N_CHIPS, tn=b_tn, ts=512, g_block=(512, b_tn),
                       g_map=lambda j, s: (s, D_MODEL // b_tn + j), name="grad_w_b")
    dya_in = _mm_nt(dyab, w_a, tm=512, a_col=0, name="dya_in")
    dgated, sm_conv = _conv_gate_bwd(proj, dya_in, conv_w, dgated)
    att_stats = _comb_bwd(dyab, w_b, comb, lse_tot)
    dqkv = [_attn_bwd(qkv[g], col0[g], g, *att_stats[g]) for g in range(N_GROUPS)]

    n_planes = dgated.shape[0]
    gated_w, gated_cs = _mm_tn(h0b, dgated, n_out=n_planes, tn=D_MODEL, ts=512, g_block=(None, 512, D_MODEL),
                               g_map=lambda j, s: (j, s, 0), colsum=True, name="grad_w_in_gated")
    w_pieces, b_pieces = [gated_w.transpose(1, 0, 2).reshape(D_MODEL, N_GATED)], [gated_cs[0]]
    for g in range(N_GROUPS):
        for k, nm in enumerate("qkv"):
            pw, pc = _mm_tn(h0_rows[g], dqkv[g][k].reshape(S, GROUP_W), n_out=1, tn=GROUP_W, ts=512,
                            g_block=(512, GROUP_W), g_map=lambda j, s: (s, 0), colsum=True, name=f"grad_w_in_{nm}{g}")
            w_pieces.append(pw[0])
            b_pieces.append(pc[0])
    dw_blocks = jnp.concatenate(w_pieces, axis=1).reshape(D_MODEL, N_BLK, GROUP_W)
    dw_ref = jnp.concatenate([dw_blocks[:, b] for b in INV_PERM], axis=1)
    gw["w_in"] = dw_ref.reshape(D_MODEL, N_CHIPS, N_IN // N_CHIPS).transpose(1, 0, 2)
    db_blocks = jnp.concatenate(b_pieces).reshape(N_BLK, GROUP_W)
    grad_b_in = jnp.concatenate([db_blocks[b] for b in INV_PERM])

    dh0_part = _gated_bwd(dgated, w_nat, dz1)
    grad_x, st0 = _in_bwd_ln0(dh0_part, dqkv, w_qkv, x, p["ln0_g"])

    small = {
        "loss": st2[2:3, 0:1],
        "ln0_g": st0[0], "ln0_b": st0[1], "b_in": grad_b_in, "conv_w": sm_conv[0:3],
        "b_o": st1[2], "ln1_g": st1[0], "ln1_b": st1[1],
        "b_up": jnp.concatenate([sm_ffn[0], sm_ffn[1]]), "ffn_conv_w": sm_ffn[3:6], "ffn_conv_b": sm_ffn[2],
        "b_down": st2[3], "ln2_g": st2[0], "ln2_b": st2[1],
    }
    return grad_x, gw, small


BIG = ("w_in", "w_a", "w_b", "w_o", "w_up", "w_down")
CONV = ("conv_w", "ffn_conv_w")
VECS = ("ln0_g", "ln0_b", "b_in", "b_o", "ln1_g", "ln1_b", "b_up", "ffn_conv_b", "b_down", "ln2_g", "ln2_b")
ORDER = ("ln0_g", "ln0_b", "w_in", "b_in", "conv_w", "w_a", "w_b", "w_o", "b_o", "ln1_g", "ln1_b", "w_up", "b_up",
         "ffn_conv_w", "ffn_conv_b", "w_down", "b_down", "ln2_g", "ln2_b")
SMALL_ORDER = ("loss",) + VECS + CONV


def _step(x, target, W, Mo, Vo):
    x2, t2 = x[0], target[0]
    big2 = {n: W[n][0] for n in BIG}
    shards = [big2[n].astype(BF16) for n in BIG] + [W[n][0] for n in CONV]
    gathered = dict(zip(BIG + CONV, _allgather_shards(shards)))
    wfull = {
        "w_in": gathered["w_in"], "w_up": gathered["w_up"],
        "w_a": gathered["w_a"].reshape(D_CONV, D_MODEL), "w_o": gathered["w_o"].reshape(D_MODEL, D_MODEL),
        "w_down": gathered["w_down"].reshape(D_FF, D_MODEL),
        "w_b": gathered["w_b"].transpose(1, 0, 2).reshape(GROUP_W, D_MODEL),
        "conv_w": gathered["conv_w"].transpose(1, 0, 2).reshape(3, D_CONV),
        "ffn_conv_w": gathered["ffn_conv_w"].transpose(1, 0, 2).reshape(3, D_FF),
    }
    pvec = {n: W[n].reshape(1, -1) for n in VECS}
    grad_x, gw, small = _local_step(x2, t2, pvec, wfull)

    parts = dict(zip(BIG, _exchange_grads([gw[n] for n in BIG])))
    out = {}
    for n in BIG:
        tr = {"w_in": 128, "w_up": 128, "w_b": 128}.get(n, big2[n].shape[0] // 4)
        g, d, nm, nv = _reduce_adamw(parts[n], big2[n], Mo[n][0], Vo[n][0], tr=tr, name="adamw_" + n)
        out[n] = tuple(a[None] for a in (g, d, nm, nv))

    vec, offs = _pack([small[n] for n in SMALL_ORDER])
    tot = _sum_devices(_allgather_small(vec)).reshape(-1)
    off = dict(zip(SMALL_ORDER, offs))
    loss = tot[off["loss"]]
    chip = 2 * lax.axis_index("x") + lax.axis_index("y")
    gs = {}
    for n in VECS:
        gs[n] = lax.slice(tot, (off[n],), (off[n] + W[n].size,)).reshape(W[n].shape)
    for n in CONV:
        width = W[n].shape[2]
        full = lax.slice(tot, (off[n],), (off[n] + 3 * N_CHIPS * width,)).reshape(1, 3, N_CHIPS * width)
        gs[n] = lax.dynamic_slice_in_dim(full, chip * width, width, axis=2)
    names = VECS + CONV
    wp, _ = _pack([W[n] for n in names])
    gp, poffs = _pack([gs[n] for n in names])
    mp, _ = _pack([Mo[n] for n in names])
    vp, _ = _pack([Vo[n] for n in names])
    dl, nm, nv = (a.reshape(-1) for a in _adamw_small(wp, gp, mp, vp))
    for n, o in zip(names, poffs):
        cut = lambda a: lax.slice(a, (o,), (o + W[n].size,)).reshape(W[n].shape)
        out[n] = (gs[n], cut(dl), cut(nm), cut(nv))

    res = [loss, grad_x[None]]
    for k in range(4):
        res += [out[n][k] for n in ORDER]
    return tuple(res)


def kernel(x, ln0_g, ln0_b, w_in, b_in, conv_w, w_a, w_b, w_o, b_o, ln1_g, ln1_b, w_up, b_up, ffn_conv_w, ffn_conv_b, w_down, b_down, ln2_g, ln2_b, loss_target, m_ln0_g, m_ln0_b, m_w_in, m_b_in, m_conv_w, m_w_a, m_w_b, m_w_o, m_b_o, m_ln1_g, m_ln1_b, m_w_up, m_b_up, m_ffn_conv_w, m_ffn_conv_b, m_w_down, m_b_down, m_ln2_g, m_ln2_b, v_ln0_g, v_ln0_b, v_w_in, v_b_in, v_conv_w, v_w_a, v_w_b, v_w_o, v_b_o, v_ln1_g, v_ln1_b, v_w_up, v_b_up, v_ffn_conv_w, v_ffn_conv_b, v_w_down, v_b_down, v_ln2_g, v_ln2_b):
    W = dict(zip(ORDER, (ln0_g, ln0_b, w_in, b_in, conv_w, w_a, w_b, w_o, b_o, ln1_g, ln1_b, w_up, b_up,
                         ffn_conv_w, ffn_conv_b, w_down, b_down, ln2_g, ln2_b)))
    Mo = dict(zip(ORDER, (m_ln0_g, m_ln0_b, m_w_in, m_b_in, m_conv_w, m_w_a, m_w_b, m_w_o, m_b_o, m_ln1_g, m_ln1_b,
                          m_w_up, m_b_up, m_ffn_conv_w, m_ffn_conv_b, m_w_down, m_b_down, m_ln2_g, m_ln2_b)))
    Vo = dict(zip(ORDER, (v_ln0_g, v_ln0_b, v_w_in, v_b_in, v_conv_w, v_w_a, v_w_b, v_w_o, v_b_o, v_ln1_g, v_ln1_b,
                          v_w_up, v_b_up, v_ffn_conv_w, v_ffn_conv_b, v_w_down, v_b_down, v_ln2_g, v_ln2_b)))
    return _step(x, loss_target, W, Mo, Vo)
```

```python
import functools
import math

import jax
import jax.numpy as jnp
from jax import lax
from jax.experimental import pallas as pl
from jax.experimental.pallas import tpu as pltpu
from jax.experimental.pallas import tpu_sc as plsc

F32 = jnp.float32
BF16 = jnp.bfloat16

D_MODEL = 1024
D_CONV = D_MODEL
HEAD_DIM = 64
HEADS_PER_GROUP = 8
GROUPS = ((128, 1), (512, 4), (2048, 16))
N_GROUPS = len(GROUPS)
GROUP_W = HEADS_PER_GROUP * HEAD_DIM
QKV_W = N_GROUPS * GROUP_W
RADIUS = 64
D_FF = 2816
LN_EPS = 1e-5
ALPHA = 2.0 ** 0.25
MASK_VALUE = -1e30
ATT_SCALE = HEAD_DIM ** -0.5
OFF_B = 0
OFF_C = OFF_B + D_CONV
OFF_H = OFF_C + D_CONV
OFF_Q = OFF_H + D_CONV
OFF_K = OFF_Q + QKV_W
OFF_V = OFF_K + QKV_W
OFF_GA = OFF_V + QKV_W
OFF_GB = OFF_GA + D_MODEL
N_IN = OFF_GB + D_MODEL
ADAM_LR = 0.001
ADAM_B1 = 0.9
ADAM_B2 = 0.999
ADAM_EPS = 1e-08
ADAM_WD = 0.01
ADAM_STEP = 10
INV_SQRT2 = 0.7071067811865476
INV_SQRT_2PI = 0.3989422804014327

LANES = 128
SUBLANES = 8
VMEM_BYTES_V7X = 64 * 1024 * 1024
N_CHIPS = 4
N_CORES = 2
N_DEV = N_CHIPS * N_CORES
MESH = pl.DeviceIdType.MESH

N_BLK = N_IN // GROUP_W
PERM = (0, 1, 2, 3, 4, 5, 15, 16, 17, 18, 6, 9, 12, 7, 10, 13, 8, 11, 14)
INV_PERM = tuple(PERM.index(b) for b in range(N_BLK))
P_B, P_C, P_H, P_GA, P_GB, P_Q0 = 0, 1024, 2048, 3072, 4096, 5120
N_NAT = P_Q0 + QKV_W // N_GROUPS * 3
N_GATED = P_Q0

SLAB = 128
CHUNK = 256
PAD = SUBLANES
TQ = 128


def _cparams(sem, vmem_mb):
    assert vmem_mb * 1024 * 1024 < VMEM_BYTES_V7X
    return pltpu.CompilerParams(dimension_semantics=sem, vmem_limit_bytes=vmem_mb * 1024 * 1024)


def _dot(a, b):
    return jnp.dot(a, b, preferred_element_type=F32)


def _dot_nt(a, b):
    return lax.dot_general(a, b, (((1,), (1,)), ((), ())), preferred_element_type=F32)


def _dot_tn(a, b):
    return lax.dot_general(a, b, (((0,), (0,)), ((), ())), preferred_element_type=F32)


def _ln_stats(z):
    mu = jnp.mean(z, -1, keepdims=True)
    zc = z - mu
    var = jnp.mean(zc * zc, -1, keepdims=True)
    rstd = lax.rsqrt(var + LN_EPS)
    return zc * rstd, rstd


def _ln_bwd(dh, xhat, rstd, g):
    dxh = dh * g
    m1 = jnp.mean(dxh, -1, keepdims=True)
    m2 = jnp.mean(dxh * xhat, -1, keepdims=True)
    return rstd * (dxh - m1 - xhat * m2)


def _rows8(rows, width):
    pad = [jnp.zeros((1, width), F32)] * (SUBLANES - len(rows))
    return jnp.concatenate(list(rows) + pad, axis=0)


def _mm_nn(a, w, bias, *, tm, tn, out_dtype, name, vmem_mb=40):
    M, K = a.shape
    if w.ndim == 3:
        assert w.shape[2] == tn
        n_tiles = w.shape[0]
        w_spec = pl.BlockSpec((None, K, tn), lambda i, j: (j, 0, 0))
    else:
        n_tiles = w.shape[1] // tn
        w_spec = pl.BlockSpec((K, tn), lambda i, j: (0, j))

    def body(a_ref, w_ref, b_ref, o_ref):
        o_ref[...] = (_dot(a_ref[...], w_ref[...]) + b_ref[...]).astype(o_ref.dtype)

    return pl.pallas_call(
        body, grid=(M // tm, n_tiles),
        in_specs=[pl.BlockSpec((tm, K), lambda i, j: (i, 0)), w_spec, pl.BlockSpec((1, tn), lambda i, j: (0, j))],
        out_specs=pl.BlockSpec((tm, tn), lambda i, j: (i, j)),
        out_shape=jax.ShapeDtypeStruct((M, n_tiles * tn), out_dtype),
        name=name, compiler_params=_cparams(("parallel", "arbitrary"), vmem_mb))(a, w, bias)


def _mm_nt(a, w, *, tm, a_col=0, name, vmem_mb=40):
    M = a.shape[0]
    N, K = w.shape

    def body(a_ref, w_ref, o_ref):
        o_ref[...] = _dot_nt(a_ref[...], w_ref[...]).astype(o_ref.dtype)

    return pl.pallas_call(
        body, grid=(M // tm,),
        in_specs=[pl.BlockSpec((tm, K), lambda i: (i, a_col)),
                  pl.BlockSpec((N, K), lambda i: (0, 0))],
        out_specs=pl.BlockSpec((tm, N), lambda i: (i, 0)),
        out_shape=jax.ShapeDtypeStruct((M, N), BF16),
        name=name, compiler_params=_cparams(("parallel",), vmem_mb))(a, w)


def _mm_tn(a, g, *, n_out, tn, ts, g_block, g_map, colsum=False, name, vmem_mb=48):
    S, K = a.shape
    n_s = S // ts

    def body(a_ref, g_ref, *rest):
        if colsum:
            o_ref, cs_ref, acc_ref, cacc_ref = rest
        else:
            o_ref, acc_ref = rest
        s = pl.program_id(1)

        @pl.when(s == 0)
        def _():
            acc_ref[...] = jnp.zeros_like(acc_ref)
            if colsum:
                cacc_ref[...] = jnp.zeros_like(cacc_ref)

        gv = g_ref[...]
        acc_ref[...] += _dot_tn(a_ref[...], gv)
        if colsum:
            cacc_ref[...] += jnp.broadcast_to(jnp.sum(gv.astype(F32), axis=0, keepdims=True), cacc_ref.shape)

        @pl.when(s == n_s - 1)
        def _():
            o_ref[...] = acc_ref[...].astype(o_ref.dtype)
            if colsum:
                cs_ref[...] = cacc_ref[...]

    out_specs = [pl.BlockSpec((None, K, tn), lambda j, s: (j, 0, 0))]
    out_shape = [jax.ShapeDtypeStruct((n_out, K, tn), BF16)]
    scratch = [pltpu.VMEM((K, tn), F32)]
    if colsum:
        out_specs.append(pl.BlockSpec((SUBLANES, tn), lambda j, s: (0, j)))
        out_shape.append(jax.ShapeDtypeStruct((SUBLANES, n_out * tn), F32))
        scratch.append(pltpu.VMEM((SUBLANES, tn), F32))
    res = pl.pallas_call(
        body, grid=(n_out, n_s),
        in_specs=[pl.BlockSpec((ts, K), lambda j, s: (s, 0)), pl.BlockSpec(g_block, g_map)],
        out_specs=out_specs, out_shape=out_shape, scratch_shapes=scratch,
        name=name, compiler_params=_cparams(("parallel", "arbitrary"), vmem_mb))(a, g)
    return res if colsum else res[0]


DILS = tuple(d for _, d in GROUPS if d > 1)


def _res_spec(d, tm, width):
    return pl.BlockSpec((d, tm // d, width), lambda i: (0, i, 0))


def _lane_scratch(tm, width):
    return [pltpu.VMEM((tm, LANES), F32)] * (width // LANES)


def _to_residue(val, dst_refs, dils, tm, dtype, scr):
    for c, ref in enumerate(scr):
        ref[...] = val[:, c * LANES:(c + 1) * LANES]
    for dst_ref, d in zip(dst_refs, dils):
        for r in range(d):
            cols = [ref[pl.ds(r, tm // d, stride=d), :] for ref in scr]
            dst_ref[r] = jnp.concatenate(cols, axis=1).astype(dtype)


def _from_residue(rows_of, d, tm, scr):
    for r in range(d):
        v = rows_of(r).astype(F32)
        for c, ref in enumerate(scr):
            ref[pl.ds(r, tm // d, stride=d), :] = v[:, c * LANES:(c + 1) * LANES]
    return jnp.concatenate([ref[...] for ref in scr], axis=1)


def _ln0_fwd(x, g, b, *, tm=512):
    S, Dm = x.shape

    def body(x_ref, g_ref, b_ref, h_ref, hb_ref, *rest):
        xhat, _ = _ln_stats(x_ref[...])
        h = xhat * g_ref[...] + b_ref[...]
        h_ref[...] = h
        hb_ref[...] = h.astype(BF16)
        _to_residue(h, rest[:len(DILS)], DILS, tm, BF16, rest[len(DILS):])

    row = pl.BlockSpec((tm, Dm), lambda i: (i, 0))
    vec = pl.BlockSpec((1, Dm), lambda i: (0, 0))
    return pl.pallas_call(
        body, grid=(S // tm,), in_specs=[row, vec, vec], out_specs=[row, row] + [_res_spec(d, tm, Dm) for d in DILS],
        out_shape=[jax.ShapeDtypeStruct((S, Dm), F32), jax.ShapeDtypeStruct((S, Dm), BF16)]
        + [jax.ShapeDtypeStruct((d, S // d, Dm), BF16) for d in DILS],
        scratch_shapes=_lane_scratch(tm, Dm),
        name="ln0_fwd", compiler_params=_cparams(("parallel",), 32))(x, g, b)


def _slab_spec(S, col0):
    return pl.BlockSpec((S, SLAB), lambda j: (0, col0 // SLAB + j))


def _zero_pads(scr, S):
    scr[0:PAD, :] = jnp.zeros((PAD, SLAB), F32)
    scr[S + PAD:S + 2 * PAD, :] = jnp.zeros((PAD, SLAB), F32)


def _shifted(scr, t):
    return (scr[PAD - 1 + t:PAD - 1 + t + CHUNK, :], scr[PAD + t:PAD + t + CHUNK, :],
            scr[PAD + 1 + t:PAD + 1 + t + CHUNK, :])


def _conv_gate_fwd(proj, conv_w):
    S = proj.shape[0]

    def body(b_ref, c_ref, h_ref, w_ref, o_ref, u_scr):
        _zero_pads(u_scr, S)
        for t in range(0, S, CHUNK):
            u_scr[PAD + t:PAD + t + CHUNK, :] = c_ref[t:t + CHUNK, :].astype(F32) * h_ref[t:t + CHUNK, :].astype(F32)
        w0, w1, w2 = w_ref[0:1, :], w_ref[1:2, :], w_ref[2:3, :]
        for t in range(0, S, CHUNK):
            um, u0, up = _shifted(u_scr, t)
            cv = w0 * um + w1 * u0 + w2 * up
            o_ref[t:t + CHUNK, :] = (b_ref[t:t + CHUNK, :].astype(F32) * cv).astype(BF16)

    return pl.pallas_call(
        body, grid=(D_CONV // SLAB,),
        in_specs=[_slab_spec(S, P_B), _slab_spec(S, P_C), _slab_spec(S, P_H),
                  pl.BlockSpec((3, SLAB), lambda j: (0, j))],
        out_specs=pl.BlockSpec((S, SLAB), lambda j: (0, j)),
        out_shape=jax.ShapeDtypeStruct((S, D_CONV), BF16),
        scratch_shapes=[pltpu.VMEM((S + 2 * PAD, SLAB), F32)],
        name="conv_gate_fwd", compiler_params=_cparams(("parallel",), 40))(proj, proj, proj, conv_w)


def _attn_masks(i, sub, dil):
    a = lax.broadcasted_iota(jnp.int32, (TQ, 2 * TQ), 0)
    j = lax.broadcasted_iota(jnp.int32, (TQ, 2 * TQ), 1)
    rel = jnp.abs(j - RADIUS - a)
    kpos = i * TQ - RADIUS + j
    valid = (rel <= RADIUS) & (kpos >= 0) & (kpos < sub)
    return valid, -(rel * dil).astype(F32)


def _slope(g, h):
    return 2.0 ** (-8.0 * (g * HEADS_PER_GROUP + h + 1) / (N_GROUPS * HEADS_PER_GROUP))


def _window(p_ref, c_ref, n_ref):
    return jnp.concatenate([p_ref[TQ - RADIUS:, :], c_ref[...], n_ref[:RADIUS, :]], axis=0)


def _qkv_specs(nb, col0):
    def spec(col, shift):
        return pl.BlockSpec((None, TQ, GROUP_W), lambda r, i: (r, jnp.clip(i + shift, 0, nb - 1), col))

    return [spec(col0, 0), spec(col0 + 1, -1), spec(col0 + 1, 0), spec(col0 + 1, 1),
            spec(col0 + 2, -1), spec(col0 + 2, 0), spec(col0 + 2, 1)]


def _attn_fwd(qkv, col0, g):
    dil, sub, _ = qkv.shape
    nb = sub // TQ

    def body(q_ref, kp, kc, kn, vp, vc, vn, o_ref, lse_ref):
        valid, base = _attn_masks(pl.program_id(1), sub, dil)
        kwin = _window(kp, kc, kn)
        vwin = _window(vp, vc, vn)
        q = q_ref[...]
        for h in range(HEADS_PER_GROUP):
            sl = slice(h * HEAD_DIM, (h + 1) * HEAD_DIM)
            s = _dot_nt(q[:, sl], kwin[:, sl]) * ATT_SCALE + _slope(g, h) * base
            s = jnp.where(valid, s, MASK_VALUE)
            m = jnp.max(s, -1, keepdims=True)
            p = jnp.exp(s - m)
            den = jnp.sum(p, -1, keepdims=True)
            o_ref[:, sl] = _dot(p.astype(BF16), vwin[:, sl]) / den
            lse_ref[:, sl] = jnp.broadcast_to(m + jnp.log(den), (TQ, HEAD_DIM))

    out = pl.BlockSpec((None, TQ, GROUP_W), lambda r, i: (r, i, 0))
    return pl.pallas_call(
        body, grid=(dil, nb), in_specs=_qkv_specs(nb, col0), out_specs=[out, out],
        out_shape=[jax.ShapeDtypeStruct((dil, sub, GROUP_W), F32)] * 2,
        name=f"attn_fwd_g{g}", compiler_params=_cparams(("parallel", "arbitrary"), 32))(*([qkv] * 7))


def _attn_combine(outs, lses, *, tm=512):
    S = outs[0].shape[1]
    n_col = GROUP_W // LANES

    def body(*refs):
        ins, (c_ref, cb_ref, lt_ref) = refs[:2 * N_GROUPS], refs[2 * N_GROUPS:2 * N_GROUPS + 3]
        scr = refs[2 * N_GROUPS + 3:]
        o, l = [ins[0][0]], [ins[N_GROUPS][0]]
        for k, d in enumerate(DILS):
            o_ref, l_ref = ins[1 + k], ins[N_GROUPS + 1 + k]
            o.append(_from_residue(lambda r: o_ref[r], d, tm, scr[2 * k * n_col:(2 * k + 1) * n_col]))
            l.append(_from_residue(lambda r: l_ref[r], d, tm, scr[(2 * k + 1) * n_col:(2 * k + 2) * n_col]))
        m = jnp.maximum(jnp.maximum(l[0], l[1]), l[2])
        e = [jnp.exp(v - m) for v in l]
        den = e[0] + e[1] + e[2]
        comb = (e[0] * o[0] + e[1] * o[1] + e[2] * o[2]) / den
        c_ref[...] = comb
        cb_ref[...] = comb.astype(BF16)
        lt_ref[...] = m + jnp.log(den)

    row = pl.BlockSpec((tm, GROUP_W), lambda i: (i, 0))
    specs = [_res_spec(GROUPS[g][1], tm, GROUP_W) for g in range(N_GROUPS)]
    return pl.pallas_call(
        body, grid=(S // tm,), in_specs=specs * 2, out_specs=[row] * 3,
        out_shape=[jax.ShapeDtypeStruct((S, GROUP_W), F32), jax.ShapeDtypeStruct((S, GROUP_W), BF16),
                   jax.ShapeDtypeStruct((S, GROUP_W), F32)],
        scratch_shapes=_lane_scratch(tm, GROUP_W) * (2 * len(DILS)),
        name="attn_combine", compiler_params=_cparams(("parallel",), 32))(*outs, *lses)


def _branch_mix(ya_in, comb_b, w_a, w_b, proj, *, tm=512):
    S = ya_in.shape[0]

    def body(ya_ref, cb_ref, wa_ref, wb_ref, ga_ref, gb_ref, yab_ref, mx_ref):
        y_a = _dot(ya_ref[...], wa_ref[...])
        y_b = _dot(cb_ref[...], wb_ref[...])
        yab_ref[:, 0:D_MODEL] = y_a.astype(BF16)
        yab_ref[:, D_MODEL:2 * D_MODEL] = y_b.astype(BF16)
        mx = jax.nn.sigmoid(ga_ref[...].astype(F32)) * y_a + jax.nn.sigmoid(gb_ref[...].astype(F32)) * y_b
        mx_ref[...] = mx.astype(BF16)

    return pl.pallas_call(
        body, grid=(S // tm,),
        in_specs=[pl.BlockSpec((tm, D_CONV), lambda i: (i, 0)), pl.BlockSpec((tm, GROUP_W), lambda i: (i, 0)),
                  pl.BlockSpec((D_CONV, D_MODEL), lambda i: (0, 0)), pl.BlockSpec((GROUP_W, D_MODEL), lambda i: (0, 0)),
                  pl.BlockSpec((tm, D_MODEL), lambda i: (i, P_GA // D_MODEL)),
                  pl.BlockSpec((tm, D_MODEL), lambda i: (i, P_GB // D_MODEL))],
        out_specs=[pl.BlockSpec((tm, 2 * D_MODEL), lambda i: (i, 0)), pl.BlockSpec((tm, D_MODEL), lambda i: (i, 0))],
        out_shape=[jax.ShapeDtypeStruct((S, 2 * D_MODEL), BF16), jax.ShapeDtypeStruct((S, D_MODEL), BF16)],
        name="branch_mix", compiler_params=_cparams(("parallel",), 40))(ya_in, comb_b, w_a, w_b, proj, proj)


def _mix_ln1(mixin, w_o, b_o, h0, g1, b1, *, tm=512):
    S = mixin.shape[0]

    def body(mx_ref, wo_ref, bo_ref, h0_ref, g_ref, b_ref, xh_ref, rs_ref, h1b_ref):
        z = ALPHA * h0_ref[...] + _dot(mx_ref[...], wo_ref[...]) + bo_ref[...]
        xhat, rstd = _ln_stats(z)
        xh_ref[...] = xhat
        rs_ref[...] = jnp.broadcast_to(rstd, (tm, LANES))
        h1b_ref[...] = (xhat * g_ref[...] + b_ref[...]).astype(BF16)

    row = pl.BlockSpec((tm, D_MODEL), lambda i: (i, 0))
    vec = pl.BlockSpec((1, D_MODEL), lambda i: (0, 0))
    return pl.pallas_call(
        body, grid=(S // tm,),
        in_specs=[row, pl.BlockSpec((D_MODEL, D_MODEL), lambda i: (0, 0)), vec, row, vec, vec],
        out_specs=[row, pl.BlockSpec((tm, LANES), lambda i: (i, 0)), row],
        out_shape=[jax.ShapeDtypeStruct((S, D_MODEL), F32), jax.ShapeDtypeStruct((S, LANES), F32),
                   jax.ShapeDtypeStruct((S, D_MODEL), BF16)],
        name="mix_ln1", compiler_params=_cparams(("parallel",), 40))(mixin, w_o, b_o, h0, g1, b1)


def _gelu_parts(cz):
    cdf = 0.5 * (1.0 + lax.erf(cz * INV_SQRT2))
    return cdf, cz * cdf


def _ffn_conv_fwd(up, cw, cb):
    S = up.shape[0]

    def body(a_ref, g_ref, w_ref, cb_ref, o_ref, a_scr):
        _zero_pads(a_scr, S)
        for t in range(0, S, CHUNK):
            a_scr[PAD + t:PAD + t + CHUNK, :] = a_ref[t:t + CHUNK, :].astype(F32)
        w0, w1, w2 = w_ref[0:1, :], w_ref[1:2, :], w_ref[2:3, :]
        for t in range(0, S, CHUNK):
            am, a0, ap = _shifted(a_scr, t)
            _, gel = _gelu_parts(w0 * am + w1 * a0 + w2 * ap + cb_ref[...])
            o_ref[t:t + CHUNK, :] = (gel * g_ref[t:t + CHUNK, :].astype(F32)).astype(BF16)

    return pl.pallas_call(
        body, grid=(D_FF // SLAB,),
        in_specs=[_slab_spec(S, 0), _slab_spec(S, D_FF), pl.BlockSpec((3, SLAB), lambda j: (0, j)),
                  pl.BlockSpec((1, SLAB), lambda j: (0, j))],
        out_specs=pl.BlockSpec((S, SLAB), lambda j: (0, j)),
        out_shape=jax.ShapeDtypeStruct((S, D_FF), BF16),
        scratch_shapes=[pltpu.VMEM((S + 2 * PAD, SLAB), F32)],
        name="ffn_conv_fwd", compiler_params=_cparams(("parallel",), 40))(up, up, cw, cb)


def _down_ln2_loss(f, w_down, b_down, xhat1, g1, b1, g2, b2, target, *, tm=256):
    S = f.shape[0]

    def body(f_ref, wd_ref, bd_ref, xh1_ref, g1_ref, b1_ref, g2_ref, b2_ref, t_ref, dz_ref, dzb_ref, st_ref):
        h1 = xh1_ref[...] * g1_ref[...] + b1_ref[...]
        z = ALPHA * h1 + _dot(f_ref[...], wd_ref[...]) + bd_ref[...]
        xhat, rstd = _ln_stats(z)
        err = xhat * g2_ref[...] + b2_ref[...] - t_ref[...]
        loss = (0.5 / D_MODEL) * jnp.sum(jnp.sum(err * err, axis=1, keepdims=True), axis=0, keepdims=True)
        dh2 = err * (1.0 / D_MODEL)
        dz = _ln_bwd(dh2, xhat, rstd, g2_ref[...])
        dz_ref[...] = dz
        dzb_ref[...] = dz.astype(BF16)
        upd = _rows8([jnp.sum(dh2 * xhat, axis=0, keepdims=True), jnp.sum(dh2, axis=0, keepdims=True),
                      jnp.broadcast_to(loss, (1, D_MODEL)), jnp.sum(dz, axis=0, keepdims=True)], D_MODEL)

        @pl.when(pl.program_id(0) == 0)
        def _():
            st_ref[...] = upd

        @pl.when(pl.program_id(0) != 0)
        def _():
            st_ref[...] += upd

    row = pl.BlockSpec((tm, D_MODEL), lambda i: (i, 0))
    vec = pl.BlockSpec((1, D_MODEL), lambda i: (0, 0))
    return pl.pallas_call(
        body, grid=(S // tm,),
        in_specs=[pl.BlockSpec((tm, D_FF), lambda i: (i, 0)), pl.BlockSpec((D_FF, D_MODEL), lambda i: (0, 0)),
                  vec, row, vec, vec, vec, vec, row],
        out_specs=[row, row, pl.BlockSpec((SUBLANES, D_MODEL), lambda i: (0, 0))],
        out_shape=[jax.ShapeDtypeStruct((S, D_MODEL), F32), jax.ShapeDtypeStruct((S, D_MODEL), BF16),
                   jax.ShapeDtypeStruct((SUBLANES, D_MODEL), F32)],
        name="down_ln2_loss", compiler_params=_cparams(("arbitrary",), 48))(
            f, w_down, b_down, xhat1, g1, b1, g2, b2, target)


def _ffn_conv_bwd(up, df, cw, cb):
    S = up.shape[0]

    def body(a_ref, g_ref, df_ref, w_ref, cb_ref, dup_ref, sm_ref, a_scr, d_scr):
        _zero_pads(a_scr, S)
        _zero_pads(d_scr, S)
        for t in range(0, S, CHUNK):
            a_scr[PAD + t:PAD + t + CHUNK, :] = a_ref[t:t + CHUNK, :].astype(F32)
        w0, w1, w2 = w_ref[0:1, :], w_ref[1:2, :], w_ref[2:3, :]
        zero = jnp.zeros((1, SLAB), F32)
        s_dg, s_dcz, s_w0, s_w1, s_w2 = zero, zero, zero, zero, zero
        for t in range(0, S, CHUNK):
            am, a0, ap = _shifted(a_scr, t)
            cz = w0 * am + w1 * a0 + w2 * ap + cb_ref[...]
            cdf, gel = _gelu_parts(cz)
            dfv = df_ref[t:t + CHUNK, :].astype(F32)
            dgte = dfv * gel
            dcz = dfv * g_ref[t:t + CHUNK, :].astype(F32) * (cdf + cz * jnp.exp(-0.5 * cz * cz) * INV_SQRT_2PI)
            dup_ref[1, t:t + CHUNK, :] = dgte.astype(BF16)
            d_scr[PAD + t:PAD + t + CHUNK, :] = dcz
            s_dg = s_dg + jnp.sum(dgte, axis=0, keepdims=True)
            s_dcz = s_dcz + jnp.sum(dcz, axis=0, keepdims=True)
            s_w0 = s_w0 + jnp.sum(dcz * am, axis=0, keepdims=True)
            s_w1 = s_w1 + jnp.sum(dcz * a0, axis=0, keepdims=True)
            s_w2 = s_w2 + jnp.sum(dcz * ap, axis=0, keepdims=True)
        s_da = zero
        for t in range(0, S, CHUNK):
            dm, d0, dp = _shifted(d_scr, t)
            da = w0 * dp + w1 * d0 + w2 * dm
            dup_ref[0, t:t + CHUNK, :] = da.astype(BF16)
            s_da = s_da + jnp.sum(da, axis=0, keepdims=True)
        sm_ref[...] = _rows8([s_da, s_dg, s_dcz, s_w0, s_w1, s_w2], SLAB)

    return pl.pallas_call(
        body, grid=(D_FF // SLAB,),
        in_specs=[_slab_spec(S, 0), _slab_spec(S, D_FF), pl.BlockSpec((S, SLAB), lambda j: (0, j)),
                  pl.BlockSpec((3, SLAB), lambda j: (0, j)), pl.BlockSpec((1, SLAB), lambda j: (0, j))],
        out_specs=[pl.BlockSpec((2, S, SLAB), lambda j: (0, 0, j)), pl.BlockSpec((SUBLANES, SLAB), lambda j: (0, j))],
        out_shape=[jax.ShapeDtypeStruct((2, S, D_FF), BF16), jax.ShapeDtypeStruct((SUBLANES, D_FF), F32)],
        scratch_shapes=[pltpu.VMEM((S + 2 * PAD, SLAB), F32)] * 2,
        name="ffn_conv_bwd", compiler_params=_cparams(("parallel",), 48))(up, up, df, cw, cb)


def _up_bwd_ln1(dup, w_up3, dz2, xhat1, rstd1, g1, *, tm=512):
    S = dz2.shape[0]
    ns, _, tk = w_up3.shape
    per_plane = D_FF // tk

    def body(du_ref, w_ref, dz2_ref, xh_ref, rs_ref, g_ref, dz_ref, dzb_ref, st_ref, acc_ref):
        i, k = pl.program_id(0), pl.program_id(1)

        @pl.when(k == 0)
        def _():
            acc_ref[...] = ALPHA * dz2_ref[...]

        acc_ref[...] += _dot_nt(du_ref[...], w_ref[...])

        @pl.when(k == ns - 1)
        def _():
            dh = acc_ref[...]
            xhat = xh_ref[...]
            dz = _ln_bwd(dh, xhat, rs_ref[:, 0:1], g_ref[...])
            dz_ref[...] = dz
            dzb_ref[...] = dz.astype(BF16)
            upd = _rows8([jnp.sum(dh * xhat, axis=0, keepdims=True), jnp.sum(dh, axis=0, keepdims=True),
                          jnp.sum(dz, axis=0, keepdims=True)], D_MODEL)

            @pl.when(i == 0)
            def _():
                st_ref[...] = upd

            @pl.when(i != 0)
            def _():
                st_ref[...] += upd

    row = pl.BlockSpec((tm, D_MODEL), lambda i, k: (i, 0))
    return pl.pallas_call(
        body, grid=(S // tm, ns),
        in_specs=[pl.BlockSpec((None, tm, tk), lambda i, k: (k // per_plane, i, k % per_plane)),
                  pl.BlockSpec((None, D_MODEL, tk), lambda i, k: (k, 0, 0)),
                  row, row, pl.BlockSpec((tm, LANES), lambda i, k: (i, 0)),
                  pl.BlockSpec((1, D_MODEL), lambda i, k: (0, 0))],
        out_specs=[row, row, pl.BlockSpec((SUBLANES, D_MODEL), lambda i, k: (0, 0))],
        out_shape=[jax.ShapeDtypeStruct((S, D_MODEL), F32), jax.ShapeDtypeStruct((S, D_MODEL), BF16),
                   jax.ShapeDtypeStruct((SUBLANES, D_MODEL), F32)],
        scratch_shapes=[pltpu.VMEM((tm, D_MODEL), F32)],
        name="up_bwd_ln1", compiler_params=_cparams(("arbitrary", "arbitrary"), 40))(
            dup, w_up3, dz2, xhat1, rstd1, g1)


def _mix_bwd(dz1b, w_o, proj, yab, *, tm=512):
    S = dz1b.shape[0]
    half = D_MODEL // 2
    gate0 = P_GA // half

    def body(dz_ref, wo_ref, gt_ref, y_ref, dy_ref, dg_ref):
        dmx = _dot_nt(dz_ref[...], wo_ref[...])
        sg = jax.nn.sigmoid(gt_ref[...].astype(F32))
        dy_ref[...] = (dmx * sg).astype(BF16)
        dg_ref[...] = (dmx * y_ref[...].astype(F32) * sg * (1.0 - sg)).astype(BF16)

    blk = pl.BlockSpec((tm, half), lambda i, j: (i, j))
    return pl.pallas_call(
        body, grid=(S // tm, 4),
        in_specs=[pl.BlockSpec((tm, D_MODEL), lambda i, j: (i, 0)),
                  pl.BlockSpec((half, D_MODEL), lambda i, j: (j % 2, 0)),
                  pl.BlockSpec((tm, half), lambda i, j: (i, gate0 + j)), blk],
        out_specs=[blk, pl.BlockSpec((None, tm, half), lambda i, j: (3 + j // 2, i, j % 2))],
        out_shape=[jax.ShapeDtypeStruct((S, 2 * D_MODEL), BF16),
                   jax.ShapeDtypeStruct((N_GATED // D_MODEL, S, D_MODEL), BF16)],
        name="mix_bwd", compiler_params=_cparams(("parallel", "arbitrary"), 32))(dz1b, w_o, proj, yab)


def _conv_gate_bwd(proj, dya_in, conv_w, dgated):
    S = proj.shape[0]

    def body(b_ref, c_ref, h_ref, dy_ref, w_ref, _, o_ref, sm_ref, u_scr, d_scr):
        _zero_pads(u_scr, S)
        _zero_pads(d_scr, S)
        for t in range(0, S, CHUNK):
            u_scr[PAD + t:PAD + t + CHUNK, :] = c_ref[t:t + CHUNK, :].astype(F32) * h_ref[t:t + CHUNK, :].astype(F32)
        w0, w1, w2 = w_ref[0:1, :], w_ref[1:2, :], w_ref[2:3, :]
        zero = jnp.zeros((1, SLAB), F32)
        s_w0, s_w1, s_w2 = zero, zero, zero
        for t in range(0, S, CHUNK):
            um, u0, up = _shifted(u_scr, t)
            dy = dy_ref[t:t + CHUNK, :].astype(F32)
            o_ref[0, t:t + CHUNK, :] = (dy * (w0 * um + w1 * u0 + w2 * up)).astype(BF16)
            dcv = dy * b_ref[t:t + CHUNK, :].astype(F32)
            d_scr[PAD + t:PAD + t + CHUNK, :] = dcv
            s_w0 = s_w0 + jnp.sum(dcv * um, axis=0, keepdims=True)
            s_w1 = s_w1 + jnp.sum(dcv * u0, axis=0, keepdims=True)
            s_w2 = s_w2 + jnp.sum(dcv * up, axis=0, keepdims=True)
        for t in range(0, S, CHUNK):
            dm, d0, dp = _shifted(d_scr, t)
            du = w0 * dp + w1 * d0 + w2 * dm
            o_ref[1, t:t + CHUNK, :] = (du * h_ref[t:t + CHUNK, :].astype(F32)).astype(BF16)
            o_ref[2, t:t + CHUNK, :] = (du * c_ref[t:t + CHUNK, :].astype(F32)).astype(BF16)
        sm_ref[...] = _rows8([s_w0, s_w1, s_w2], SLAB)

    return pl.pallas_call(
        body, grid=(D_CONV // SLAB,),
        in_specs=[_slab_spec(S, P_B), _slab_spec(S, P_C), _slab_spec(S, P_H),
                  pl.BlockSpec((S, SLAB), lambda j: (0, j)), pl.BlockSpec((3, SLAB), lambda j: (0, j)),
                  pl.BlockSpec(memory_space=pl.ANY)],
        out_specs=[pl.BlockSpec((3, S, SLAB), lambda j: (0, 0, j)), pl.BlockSpec((SUBLANES, SLAB), lambda j: (0, j))],
        out_shape=[jax.ShapeDtypeStruct(dgated.shape, BF16), jax.ShapeDtypeStruct((SUBLANES, D_CONV), F32)],
        scratch_shapes=[pltpu.VMEM((S + 2 * PAD, SLAB), F32)] * 2, input_output_aliases={5: 0},
        name="conv_gate_bwd", compiler_params=_cparams(("parallel",), 48))(proj, proj, proj, dya_in, conv_w, dgated)


def _comb_bwd(dyab, w_b, comb, lse_tot, *, tm=512):
    S = comb.shape[0]

    def body(dy_ref, wb_ref, c_ref, lt_ref, *rest):
        outs, scr = rest[:3 * N_GROUPS], rest[3 * N_GROUPS:]
        dcb = _dot_nt(dy_ref[...], wb_ref[...]).astype(BF16)
        dc = dcb.astype(F32)
        prod = dc * c_ref[...]
        heads = [jnp.broadcast_to(jnp.sum(prod[:, h * HEAD_DIM:(h + 1) * HEAD_DIM], axis=1, keepdims=True),
                                  (tm, HEAD_DIM)) for h in range(HEADS_PER_GROUP)]
        vals = (dc, lt_ref[...], jnp.concatenate(heads, axis=1))
        for k, (val, dtype) in enumerate(zip(vals, (BF16, F32, F32))):
            outs[k][0] = val.astype(dtype)
            _to_residue(val, [outs[3 * (1 + j) + k] for j in range(len(DILS))], DILS, tm, dtype, scr)

    row = pl.BlockSpec((tm, GROUP_W), lambda i: (i, 0))
    out_specs, out_shape = [], []
    for _, d in GROUPS:
        out_specs += [_res_spec(d, tm, GROUP_W)] * 3
        out_shape += [jax.ShapeDtypeStruct((d, S // d, GROUP_W), t) for t in (BF16, F32, F32)]
    res = pl.pallas_call(
        body, grid=(S // tm,),
        in_specs=[pl.BlockSpec((tm, D_MODEL), lambda i: (i, 1)), pl.BlockSpec((GROUP_W, D_MODEL), lambda i: (0, 0)),
                  row, row],
        out_specs=out_specs, out_shape=out_shape, scratch_shapes=_lane_scratch(tm, GROUP_W),
        name="comb_bwd", compiler_params=_cparams(("parallel",), 32))(dyab, w_b, comb, lse_tot)
    return [tuple(res[3 * g:3 * g + 3]) for g in range(N_GROUPS)]


def _attn_bwd(qkv, col0, g, dcomb, lse_tot, delta):
    dil, sub, _ = qkv.shape
    nb = sub // TQ

    def body(q_ref, kp, kc, kn, vp, vc, vn, do_ref, lse_ref, dl_ref, dq_ref, dk_ref, dv_ref, ak, av):
        i = pl.program_id(1)

        @pl.when(i == 0)
        def _():
            ak[...] = jnp.zeros_like(ak)
            av[...] = jnp.zeros_like(av)

        @pl.when(i < nb)
        def _():
            valid, base = _attn_masks(i, sub, dil)
            kwin = _window(kp, kc, kn)
            vwin = _window(vp, vc, vn)
            q = q_ref[...]
            do = do_ref[...]
            for h in range(HEADS_PER_GROUP):
                sl = slice(h * HEAD_DIM, (h + 1) * HEAD_DIM)
                s = _dot_nt(q[:, sl], kwin[:, sl]) * ATT_SCALE + _slope(g, h) * base
                s = jnp.where(valid, s, MASK_VALUE)
                p = jnp.exp(s - lse_ref[:, h * HEAD_DIM:h * HEAD_DIM + 1])
                dp = _dot_nt(do[:, sl], vwin[:, sl])
                ds = (p * (dp - dl_ref[:, h * HEAD_DIM:h * HEAD_DIM + 1])).astype(BF16)
                dq_ref[:, sl] = (_dot(ds, kwin[:, sl]) * ATT_SCALE).astype(BF16)
                ak[RADIUS:RADIUS + 2 * TQ, sl] += _dot_tn(ds, q[:, sl]) * ATT_SCALE
                av[RADIUS:RADIUS + 2 * TQ, sl] += _dot_tn(p.astype(BF16), do[:, sl])

        dk_ref[...] = ak[0:TQ, :].astype(BF16)
        dv_ref[...] = av[0:TQ, :].astype(BF16)
        ak[0:2 * TQ, :] = ak[TQ:3 * TQ, :]
        av[0:2 * TQ, :] = av[TQ:3 * TQ, :]
        ak[2 * TQ:3 * TQ, :] = jnp.zeros((TQ, GROUP_W), F32)
        av[2 * TQ:3 * TQ, :] = jnp.zeros((TQ, GROUP_W), F32)

    tok = pl.BlockSpec((None, TQ, GROUP_W), lambda r, i: (r, jnp.minimum(i, nb - 1), 0))
    dkv_spec = pl.BlockSpec((None, TQ, GROUP_W), lambda r, i: (r, jnp.maximum(i - 1, 0), 0))
    return pl.pallas_call(
        body, grid=(dil, nb + 1), in_specs=_qkv_specs(nb, col0) + [tok, tok, tok],
        out_specs=[tok, dkv_spec, dkv_spec], out_shape=[jax.ShapeDtypeStruct((dil, sub, GROUP_W), BF16)] * 3,
        scratch_shapes=[pltpu.VMEM((3 * TQ, GROUP_W), F32)] * 2,
        name=f"attn_bwd_g{g}", compiler_params=_cparams(("arbitrary", "arbitrary"), 32))(
            *([qkv] * 7), dcomb, lse_tot, delta)


def _gated_bwd(dgated, w_nat, dz1, *, tm=512):
    n_planes, S, _ = dgated.shape

    def body(dg_ref, w_ref, dz_ref, o_ref, acc_ref):
        k = pl.program_id(1)

        @pl.when(k == 0)
        def _():
            acc_ref[...] = ALPHA * dz_ref[...]

        acc_ref[...] += _dot_nt(dg_ref[...], w_ref[...])

        @pl.when(k == n_planes - 1)
        def _():
            o_ref[...] = acc_ref[...]

    row = pl.BlockSpec((tm, D_MODEL), lambda i, k: (i, 0))
    return pl.pallas_call(
        body, grid=(S // tm, n_planes),
        in_specs=[pl.BlockSpec((None, tm, D_MODEL), lambda i, k: (k, i, 0)),
                  pl.BlockSpec((D_MODEL, D_MODEL), lambda i, k: (0, k)), row],
        out_specs=row, out_shape=jax.ShapeDtypeStruct((S, D_MODEL), F32),
        scratch_shapes=[pltpu.VMEM((tm, D_MODEL), F32)],
        name="gated_bwd", compiler_params=_cparams(("parallel", "arbitrary"), 32))(dgated, w_nat, dz1)


def _in_bwd_ln0(dh0_part, dqkv, w_qkv, x, g0, *, tm=512):
    S = x.shape[0]
    n_in = 3 * N_GROUPS

    def body(*refs):
        dh_ref, d_refs, w_refs = refs[0], refs[1:1 + n_in], refs[1 + n_in:1 + n_in + N_GROUPS]
        x_ref, g_ref, gx_ref, st_ref, acc_ref, *tmp_ref = refs[1 + n_in + N_GROUPS:]
        acc_ref[...] = dh_ref[...]
        for g, (_, d) in enumerate(GROUPS):
            rows = [jnp.concatenate([d_refs[3 * g + k][r] for k in range(3)], axis=1) for r in range(d)]
            res = _dot_nt(jnp.concatenate(rows, axis=0), w_refs[g][...])
            if d == 1:
                acc_ref[...] += res
            else:
                n = tm // d
                acc_ref[...] += _from_residue(lambda r: res[r * n:(r + 1) * n, :], d, tm, tmp_ref)
        dh = acc_ref[...]
        xhat, rstd = _ln_stats(x_ref[...])
        gx_ref[...] = _ln_bwd(dh, xhat, rstd, g_ref[...])
        upd = _rows8([jnp.sum(dh * xhat, axis=0, keepdims=True), jnp.sum(dh, axis=0, keepdims=True)], D_MODEL)

        @pl.when(pl.program_id(0) == 0)
        def _():
            st_ref[...] = upd

        @pl.when(pl.program_id(0) != 0)
        def _():
            st_ref[...] += upd

    row = pl.BlockSpec((tm, D_MODEL), lambda i: (i, 0))
    d_specs = []
    for _, d in GROUPS:
        d_specs += [_res_spec(d, tm, GROUP_W)] * 3
    operands = [dh0_part] + [a for grp in dqkv for a in grp] + list(w_qkv) + [x, g0]
    return pl.pallas_call(
        body, grid=(S // tm,),
        in_specs=[row] + d_specs + [pl.BlockSpec((D_MODEL, QKV_W), lambda i: (0, 0))] * N_GROUPS
        + [row, pl.BlockSpec((1, D_MODEL), lambda i: (0, 0))],
        out_specs=[row, pl.BlockSpec((SUBLANES, D_MODEL), lambda i: (0, 0))],
        out_shape=[jax.ShapeDtypeStruct((S, D_MODEL), F32), jax.ShapeDtypeStruct((SUBLANES, D_MODEL), F32)],
        scratch_shapes=[pltpu.VMEM((tm, D_MODEL), F32)] + _lane_scratch(tm, D_MODEL),
        name="in_bwd_ln0", compiler_params=_cparams(("arbitrary",), 48))(*operands)


HBM_SPEC = pl.BlockSpec(memory_space=pltpu.HBM)


def _place():
    x, y, c = lax.axis_index("x"), lax.axis_index("y"), lax.axis_index("c")
    chips = [(1 - x, y), (x, 1 - y), (1 - x, 1 - y)]
    return x, y, c, chips


def _allgather_shards(shards):
    n = len(shards)

    def body(*refs):
        ins, outs = refs[:n], refs[n:2 * n]
        send_sems, recv_sems, loc_sems = refs[2 * n:]
        x, y, c, chips = _place()
        me = 2 * x + y
        local, sends = [], []
        for w in range(n):
            cp = pltpu.make_async_copy(ins[w], outs[w].at[me], loc_sems.at[w])
            cp.start()
            local.append(cp)
            for j, (px, py) in enumerate(chips):
                cp = pltpu.make_async_remote_copy(
                    src_ref=ins[w], dst_ref=outs[w].at[me], send_sem=send_sems.at[3 * w + j],
                    recv_sem=recv_sems.at[3 * w + j], device_id=(px, py, c), device_id_type=MESH)
                cp.start()
                sends.append(cp)
        for w in range(n):
            for j, (px, py) in enumerate(chips):
                pltpu.make_async_remote_copy(
                    src_ref=ins[w], dst_ref=outs[w].at[2 * px + py], send_sem=send_sems.at[3 * w + j],
                    recv_sem=recv_sems.at[3 * w + j], device_id=(px, py, c), device_id_type=MESH).wait_recv()
        for cp in sends:
            cp.wait_send()
        for cp in local:
            cp.wait()

    return pl.pallas_call(
        body, in_specs=[HBM_SPEC] * n, out_specs=[HBM_SPEC] * n,
        out_shape=[jax.ShapeDtypeStruct((N_CHIPS,) + s.shape, s.dtype) for s in shards],
        scratch_shapes=[pltpu.SemaphoreType.DMA((3 * n,)), pltpu.SemaphoreType.DMA((3 * n,)),
                        pltpu.SemaphoreType.DMA((n,))],
        name="allgather_weights")(*shards)


def _exchange_grads(grads, *, name, collective_id):
    n = len(grads)
    per = 7

    def body(*refs):
        ins, outs = refs[:n], refs[n:2 * n]
        send_sems, recv_sems, loc_sems = refs[2 * n:]
        x, y, c, chips = _place()
        me = 2 * x + y
        sib = (x, y, 1 - c)
        peers = [sib] + [(px, py, c) for px, py in chips]
        barrier = pltpu.get_barrier_semaphore()
        for peer in peers:
            pl.semaphore_signal(barrier, inc=1, device_id=peer, device_id_type=MESH)
        pl.semaphore_wait(barrier, len(peers))

        def rcopy(w, k, src, dst, to):
            return pltpu.make_async_remote_copy(src_ref=src, dst_ref=dst, send_sem=send_sems.at[per * w + k],
                                                recv_sem=recv_sems.at[per * w + k], device_id=to, device_id_type=MESH)

        local, sends = [], []
        for w in range(n):
            cp = pltpu.make_async_copy(ins[w].at[me], outs[w].at[c, me], loc_sems.at[w])
            cp.start()
            local.append(cp)
            cp = rcopy(w, 0, ins[w].at[me], outs[w].at[c, me], sib)
            cp.start()
            sends.append(cp)
            for j, (px, py) in enumerate(chips):
                cp = rcopy(w, 1 + j, ins[w].at[2 * px + py], outs[w].at[c, me], (px, py, c))
                cp.start()
                sends.append(cp)
        for w in range(n):
            for j, (px, py) in enumerate(chips):
                slot = outs[w].at[c, 2 * px + py]
                rcopy(w, 1 + j, slot, slot, (px, py, c)).wait_recv()
                cp = rcopy(w, 4 + j, slot, slot, sib)
                cp.start()
                sends.append(cp)
        for w in range(n):
            slot = outs[w].at[1 - c, me]
            rcopy(w, 0, slot, slot, sib).wait_recv()
            for j, (px, py) in enumerate(chips):
                slot = outs[w].at[1 - c, 2 * px + py]
                rcopy(w, 4 + j, slot, slot, sib).wait_recv()
        for cp in sends:
            cp.wait_send()
        for cp in local:
            cp.wait()

    return pl.kernel(
        body, out_type=[jax.ShapeDtypeStruct((N_CORES,) + g.shape, g.dtype) for g in grads],
        mesh=plsc.ScalarSubcoreMesh(axis_name="sequencer", num_cores=1),
        scratch_types=[pltpu.SemaphoreType.DMA((per * n,)), pltpu.SemaphoreType.DMA((per * n,)),
                       pltpu.SemaphoreType.DMA((n,))],
        name=name, compiler_params=pltpu.CompilerParams(collective_id=collective_id))(*grads)


def _allgather_small(vec):
    def body(v_ref, o_ref, send_sems, recv_sems, loc_sem):
        x, y, c = lax.axis_index("x"), lax.axis_index("y"), lax.axis_index("c")
        me = 4 * x + 2 * y + c

        def peer(k):
            flip = lambda v, bit: 1 - v if (k >> bit) & 1 else v
            return flip(x, 2), flip(y, 1), flip(c, 0)

        loc = pltpu.make_async_copy(v_ref, o_ref.at[me], loc_sem)
        loc.start()
        sends = []
        for k in range(1, N_DEV):
            cp = pltpu.make_async_remote_copy(src_ref=v_ref, dst_ref=o_ref.at[me], send_sem=send_sems.at[k - 1],
                                              recv_sem=recv_sems.at[k - 1], device_id=peer(k), device_id_type=MESH)
            cp.start()
            sends.append(cp)
        for k in range(1, N_DEV):
            px, py, pc = peer(k)
            pltpu.make_async_remote_copy(src_ref=v_ref, dst_ref=o_ref.at[4 * px + 2 * py + pc],
                                         send_sem=send_sems.at[k - 1], recv_sem=recv_sems.at[k - 1],
                                         device_id=(px, py, pc), device_id_type=MESH).wait_recv()
        for cp in sends:
            cp.wait_send()
        loc.wait()

    return pl.pallas_call(
        body, in_specs=[HBM_SPEC], out_specs=HBM_SPEC,
        out_shape=jax.ShapeDtypeStruct((N_DEV,) + vec.shape, vec.dtype),
        scratch_shapes=[pltpu.SemaphoreType.DMA((N_DEV - 1,)), pltpu.SemaphoreType.DMA((N_DEV - 1,)),
                        pltpu.SemaphoreType.DMA],
        name="allgather_small")(vec)


def _adamw(w, g, m, v):
    m = ADAM_B1 * m + (1.0 - ADAM_B1) * g
    v = ADAM_B2 * v + (1.0 - ADAM_B2) * (g * g)
    m_hat = m / (1.0 - ADAM_B1 ** ADAM_STEP)
    v_hat = v / (1.0 - ADAM_B2 ** ADAM_STEP)
    delta = -ADAM_LR * (m_hat / (jnp.sqrt(v_hat) + ADAM_EPS) + ADAM_WD * w)
    return delta, m, v


def _reduce_adamw(parts, w, m, v, *, tr, name):
    R, C = w.shape

    def body(p_ref, w_ref, m_ref, v_ref, g_ref, d_ref, nm_ref, nv_ref):
        def core_sum(cc):
            s = p_ref[cc, 0].astype(F32)
            for k in range(1, N_CHIPS):
                s = s + p_ref[cc, k].astype(F32)
            return s

        g = core_sum(0) + core_sum(1)
        delta, nm, nv = _adamw(w_ref[...], g, m_ref[...], v_ref[...])
        g_ref[...] = g
        d_ref[...] = delta
        nm_ref[...] = nm
        nv_ref[...] = nv

    blk = pl.BlockSpec((tr, C), lambda i: (i, 0))
    return pl.pallas_call(
        body, grid=(R // tr,),
        in_specs=[pl.BlockSpec((N_CORES, N_CHIPS, tr, C), lambda i: (0, 0, i, 0)), blk, blk, blk],
        out_specs=[blk] * 4, out_shape=[jax.ShapeDtypeStruct((R, C), F32)] * 4,
        name=name, compiler_params=_cparams(("parallel",), 40))(parts, w, m, v)


def _sum_devices(allv):
    _, R, _ = allv.shape

    def body(a_ref, o_ref):
        s = a_ref[0]
        for d in range(1, N_DEV):
            s = s + a_ref[d]
        o_ref[...] = s

    return pl.pallas_call(body, out_shape=jax.ShapeDtypeStruct((R, LANES), F32), name="sum_small")(allv)


def _adamw_small(w, g, m, v):
    def body(w_ref, g_ref, m_ref, v_ref, d_ref, nm_ref, nv_ref):
        delta, nm, nv = _adamw(w_ref[...], g_ref[...], m_ref[...], v_ref[...])
        d_ref[...] = delta
        nm_ref[...] = nm
        nv_ref[...] = nv

    return pl.pallas_call(body, out_shape=[jax.ShapeDtypeStruct(w.shape, F32)] * 3, name="adamw_small")(w, g, m, v)


def _pack(pieces):
    flat = [p.reshape(-1) for p in pieces]
    offs, n = [], 0
    for f in flat:
        offs.append(n)
        n += f.shape[0]
    total = -(-n // (SUBLANES * LANES)) * SUBLANES * LANES
    flat.append(jnp.zeros((total - n,), F32))
    return jnp.concatenate(flat).reshape(total // LANES, LANES), offs


def _local_step(x, target, p, wfull, on_ready=lambda group: None):
    S = x.shape[0]
    w_in3, w_up3 = wfull["w_in"], wfull["w_up"]
    w_a, w_o, w_down, w_b = wfull["w_a"], wfull["w_o"], wfull["w_down"], wfull["w_b"]
    conv_w, ffn_conv_w = wfull["conv_w"], wfull["ffn_conv_w"]
    dils = [d for _, d in GROUPS]

    w_blocks = w_in3.transpose(1, 0, 2).reshape(D_MODEL, N_BLK, GROUP_W)
    w_perm = jnp.concatenate([w_blocks[:, b] for b in PERM], axis=1)
    b_blocks = p["b_in"].reshape(N_BLK, GROUP_W)
    b_perm = jnp.concatenate([b_blocks[b] for b in PERM]).reshape(1, N_IN)
    w_nat, b_nat = w_perm[:, :N_NAT], b_perm[:, :N_NAT]
    qkv_cols = [slice(P_Q0 + g * QKV_W, P_Q0 + (g + 1) * QKV_W) for g in range(N_GROUPS)]
    w_qkv = [w_perm[:, c] for c in qkv_cols]

    h0, h0b, *h0_res = _ln0_fwd(x, p["ln0_g"], p["ln0_b"])
    h0_rows = [h0b] + [h.reshape(S, D_MODEL) for h in h0_res]
    proj = _mm_nn(h0b, w_nat, b_nat, tm=512, tn=N_NAT // 2, out_dtype=BF16, name="proj")
    qkv = [proj[None]]
    for g in range(1, N_GROUPS):
        t = _mm_nn(h0_rows[g], w_qkv[g], b_perm[:, qkv_cols[g]], tm=512, tn=QKV_W, out_dtype=BF16, name=f"proj_qkv{g}")
        qkv.append(t.reshape(dils[g], S // dils[g], QKV_W))
    col0 = [P_Q0 // GROUP_W] + [0] * (N_GROUPS - 1)
    ya_in = _conv_gate_fwd(proj, conv_w)
    att = [_attn_fwd(qkv[g], col0[g], g) for g in range(N_GROUPS)]
    comb, comb_b, lse_tot = _attn_combine([a[0] for a in att], [a[1] for a in att])
    yab, mixin = _branch_mix(ya_in, comb_b, w_a, w_b, proj)
    xhat1, rstd1, h1b = _mix_ln1(mixin, w_o, p["b_o"], h0, p["ln1_g"], p["ln1_b"])
    up = _mm_nn(h1b, w_up3, p["b_up"], tm=512, tn=w_up3.shape[2], out_dtype=BF16, name="up")
    f = _ffn_conv_fwd(up, ffn_conv_w, p["ffn_conv_b"])
    dz2, dz2b, st2 = _down_ln2_loss(f, w_down, p["b_down"], xhat1, p["ln1_g"], p["ln1_b"],
                                    p["ln2_g"], p["ln2_b"], target)

    gw = {}
    gw["w_down"] = _mm_tn(f, dz2b, n_out=1, tn=D_MODEL, ts=512, g_block=(512, D_MODEL),
                          g_map=lambda j, s: (s, 0), name="grad_w_down").reshape(N_CHIPS, D_FF // N_CHIPS, D_MODEL)
    df = _mm_nt(dz2b, w_down, tm=512, name="df")
    dup, sm_ffn = _ffn_conv_bwd(up, df, ffn_conv_w, p["ffn_conv_b"])
    up_tn = w_up3.shape[2]
    up_pp = D_FF // up_tn
    gw["w_up"] = _mm_tn(h1b, dup, n_out=N_CHIPS, tn=up_tn, ts=512, g_block=(None, 512, up_tn),
                        g_map=lambda j, s: (j // up_pp, s, j % up_pp), name="grad_w_up")
    on_ready({n: gw[n] for n in ("w_down", "w_up")})
    dz1, dz1b, st1 = _up_bwd_ln1(dup, w_up3, dz2, xhat1, rstd1, p["ln1_g"])

    gw["w_o"] = _mm_tn(mixin, dz1b, n_out=1, tn=D_MODEL, ts=512, g_block=(512, D_MODEL),
                       g_map=lambda j, s: (s, 0), name="grad_w_o").reshape(N_CHIPS, D_MODEL // N_CHIPS, D_MODEL)
    dyab, dgated = _mix_bwd(dz1b, w_o, proj, yab)
    gw["w_a"] = _mm_tn(ya_in, dyab, n_out=1, tn=D_MODEL, ts=512, g_block=(512, D_MODEL),
                       g_map=lambda j, s: (s, 0), name="grad_w_a").reshape(N_CHIPS, D_CONV // N_CHIPS, D_MODEL)
    b_tn = D_MODEL // N_CHIPS
    gw["w_b"] = _mm_tn(comb_b, dyab, n_out=N_CHIPS, tn=b_tn, ts=512, g_block=(512, b_tn),
                       g_map=lambda j, s: (s, D_MODEL // b_tn + j), name="grad_w_b")
    on_ready({n: gw[n] for n in ("w_o", "w_a", "w_b")})
    dya_in = _mm_nt(dyab, w_a, tm=512, a_col=0, name="dya_in")
    dgated, sm_conv = _conv_gate_bwd(proj, dya_in, conv_w, dgated)
    att_stats = _comb_bwd(dyab, w_b, comb, lse_tot)
    dqkv = [_attn_bwd(qkv[g], col0[g], g, *att_stats[g]) for g in range(N_GROUPS)]

    n_planes = dgated.shape[0]
    gated_w, gated_cs = _mm_tn(h0b, dgated, n_out=n_planes, tn=D_MODEL, ts=512, g_block=(None, 512, D_MODEL),
                               g_map=lambda j, s: (j, s, 0), colsum=True, name="grad_w_in_gated")
    w_pieces, b_pieces = [gated_w.transpose(1, 0, 2).reshape(D_MODEL, N_GATED)], [gated_cs[0]]
    for g in range(N_GROUPS):
        for k, nm in enumerate("qkv"):
            pw, pc = _mm_tn(h0_rows[g], dqkv[g][k].reshape(S, GROUP_W), n_out=1, tn=GROUP_W, ts=512,
                            g_block=(512, GROUP_W), g_map=lambda j, s: (s, 0), colsum=True, name=f"grad_w_in_{nm}{g}")
            w_pieces.append(pw[0])
            b_pieces.append(pc[0])
    dw_blocks = jnp.concatenate(w_pieces, axis=1).reshape(D_MODEL, N_BLK, GROUP_W)
    dw_ref = jnp.concatenate([dw_blocks[:, b] for b in INV_PERM], axis=1)
    gw["w_in"] = dw_ref.reshape(D_MODEL, N_CHIPS, N_IN // N_CHIPS).transpose(1, 0, 2)
    on_ready({"w_in": gw["w_in"]})
    db_blocks = jnp.concatenate(b_pieces).reshape(N_BLK, GROUP_W)
    grad_b_in = jnp.concatenate([db_blocks[b] for b in INV_PERM])

    dh0_part = _gated_bwd(dgated, w_nat, dz1)
    grad_x, st0 = _in_bwd_ln0(dh0_part, dqkv, w_qkv, x, p["ln0_g"])

    small = {
        "loss": st2[2:3, 0:1],
        "ln0_g": st0[0], "ln0_b": st0[1], "b_in": grad_b_in, "conv_w": sm_conv[0:3],
        "b_o": st1[2], "ln1_g": st1[0], "ln1_b": st1[1],
        "b_up": jnp.concatenate([sm_ffn[0], sm_ffn[1]]), "ffn_conv_w": sm_ffn[3:6], "ffn_conv_b": sm_ffn[2],
        "b_down": st2[3], "ln2_g": st2[0], "ln2_b": st2[1],
    }
    return grad_x, gw, small


BIG = ("w_in", "w_a", "w_b", "w_o", "w_up", "w_down")
CONV = ("conv_w", "ffn_conv_w")
VECS = ("ln0_g", "ln0_b", "b_in", "b_o", "ln1_g", "ln1_b", "b_up", "ffn_conv_b", "b_down", "ln2_g", "ln2_b")
ORDER = ("ln0_g", "ln0_b", "w_in", "b_in", "conv_w", "w_a", "w_b", "w_o", "b_o", "ln1_g", "ln1_b", "w_up", "b_up",
         "ffn_conv_w", "ffn_conv_b", "w_down", "b_down", "ln2_g", "ln2_b")
SMALL_ORDER = ("loss",) + VECS + CONV


def _step(x, target, W, Mo, Vo):
    x2, t2 = x[0], target[0]
    big2 = {n: W[n][0] for n in BIG}
    shards = [big2[n].astype(BF16) for n in BIG] + [W[n][0] for n in CONV]
    gathered = dict(zip(BIG + CONV, _allgather_shards(shards)))
    wfull = {
        "w_in": gathered["w_in"], "w_up": gathered["w_up"],
        "w_a": gathered["w_a"].reshape(D_CONV, D_MODEL), "w_o": gathered["w_o"].reshape(D_MODEL, D_MODEL),
        "w_down": gathered["w_down"].reshape(D_FF, D_MODEL),
        "w_b": gathered["w_b"].transpose(1, 0, 2).reshape(GROUP_W, D_MODEL),
        "conv_w": gathered["conv_w"].transpose(1, 0, 2).reshape(3, D_CONV),
        "ffn_conv_w": gathered["ffn_conv_w"].transpose(1, 0, 2).reshape(3, D_FF),
    }
    pvec = {n: W[n].reshape(1, -1) for n in VECS}

    parts = {}

    def exchange(group):
        names = tuple(group)
        res = _exchange_grads([group[n] for n in names], name="exchange_" + "_".join(names),
                              collective_id=1 + len(parts))
        parts.update(zip(names, res))

    grad_x, _, small = _local_step(x2, t2, pvec, wfull, exchange)
    out = {}
    for n in BIG:
        tr = {"w_in": 128, "w_up": 128, "w_b": 128}.get(n, big2[n].shape[0] // 4)
        g, d, nm, nv = _reduce_adamw(parts[n], big2[n], Mo[n][0], Vo[n][0], tr=tr, name="adamw_" + n)
        out[n] = tuple(a[None] for a in (g, d, nm, nv))

    vec, offs = _pack([small[n] for n in SMALL_ORDER])
    tot = _sum_devices(_allgather_small(vec)).reshape(-1)
    off = dict(zip(SMALL_ORDER, offs))
    loss = tot[off["loss"]]
    chip = 2 * lax.axis_index("x") + lax.axis_index("y")
    gs = {}
    for n in VECS:
        gs[n] = lax.slice(tot, (off[n],), (off[n] + W[n].size,)).reshape(W[n].shape)
    for n in CONV:
        width = W[n].shape[2]
        full = lax.slice(tot, (off[n],), (off[n] + 3 * N_CHIPS * width,)).reshape(1, 3, N_CHIPS * width)
        gs[n] = lax.dynamic_slice_in_dim(full, chip * width, width, axis=2)
    names = VECS + CONV
    wp, _ = _pack([W[n] for n in names])
    gp, poffs = _pack([gs[n] for n in names])
    mp, _ = _pack([Mo[n] for n in names])
    vp, _ = _pack([Vo[n] for n in names])
    dl, nm, nv = (a.reshape(-1) for a in _adamw_small(wp, gp, mp, vp))
    for n, o in zip(names, poffs):
        cut = lambda a: lax.slice(a, (o,), (o + W[n].size,)).reshape(W[n].shape)
        out[n] = (gs[n], cut(dl), cut(nm), cut(nv))

    res = [loss, grad_x[None]]
    for k in range(4):
        res += [out[n][k] for n in ORDER]
    return tuple(res)


def kernel(x, ln0_g, ln0_b, w_in, b_in, conv_w, w_a, w_b, w_o, b_o, ln1_g, ln1_b, w_up, b_up, ffn_conv_w, ffn_conv_b, w_down, b_down, ln2_g, ln2_b, loss_target, m_ln0_g, m_ln0_b, m_w_in, m_b_in, m_conv_w, m_w_a, m_w_b, m_w_o, m_b_o, m_ln1_g, m_ln1_b, m_w_up, m_b_up, m_ffn_conv_w, m_ffn_conv_b, m_w_down, m_b_down, m_ln2_g, m_ln2_b, v_ln0_g, v_ln0_b, v_w_in, v_b_in, v_conv_w, v_w_a, v_w_b, v_w_o, v_b_o, v_ln1_g, v_ln1_b, v_w_up, v_b_up, v_ffn_conv_w, v_ffn_conv_b, v_w_down, v_b_down, v_ln2_g, v_ln2_b):
    W = dict(zip(ORDER, (ln0_g, ln0_b, w_in, b_in, conv_w, w_a, w_b, w_o, b_o, ln1_g, ln1_b, w_up, b_up,
                         ffn_conv_w, ffn_conv_b, w_down, b_down, ln2_g, ln2_b)))
    Mo = dict(zip(ORDER, (m_ln0_g, m_ln0_b, m_w_in, m_b_in, m_conv_w, m_w_a, m_w_b, m_w_o, m_b_o, m_ln1_g, m_ln1_b,
                          m_w_up, m_b_up, m_ffn_conv_w, m_ffn_conv_b, m_w_down, m_b_down, m_ln2_g, m_ln2_b)))
    Vo = dict(zip(ORDER, (v_ln0_g, v_ln0_b, v_w_in, v_b_in, v_conv_w, v_w_a, v_w_b, v_w_o, v_b_o, v_ln1_g, v_ln1_b,
                          v_w_up, v_b_up, v_ffn_conv_w, v_ffn_conv_b, v_w_down, v_b_down, v_ln2_g, v_ln2_b)))
    return _step(x, loss_target, W, Mo, Vo)
```

```python
import functools
import math

import jax
import jax.numpy as jnp
from jax import lax
from jax.experimental import pallas as pl
from jax.experimental.pallas import tpu as pltpu
from jax.experimental.pallas import tpu_sc as plsc

F32 = jnp.float32
BF16 = jnp.bfloat16

D_MODEL = 1024
D_CONV = D_MODEL
HEAD_DIM = 64
HEADS_PER_GROUP = 8
GROUPS = ((128, 1), (512, 4), (2048, 16))
N_GROUPS = len(GROUPS)
GROUP_W = HEADS_PER_GROUP * HEAD_DIM
QKV_W = N_GROUPS * GROUP_W
RADIUS = 64
D_FF = 2816
LN_EPS = 1e-5
ALPHA = 2.0 ** 0.25
MASK_VALUE = -1e30
ATT_SCALE = HEAD_DIM ** -0.5
OFF_B = 0
OFF_C = OFF_B + D_CONV
OFF_H = OFF_C + D_CONV
OFF_Q = OFF_H + D_CONV
OFF_K = OFF_Q + QKV_W
OFF_V = OFF_K + QKV_W
OFF_GA = OFF_V + QKV_W
OFF_GB = OFF_GA + D_MODEL
N_IN = OFF_GB + D_MODEL
ADAM_LR = 0.001
ADAM_B1 = 0.9
ADAM_B2 = 0.999
ADAM_EPS = 1e-08
ADAM_WD = 0.01
ADAM_STEP = 10
INV_SQRT2 = 0.7071067811865476
INV_SQRT_2PI = 0.3989422804014327

LANES = 128
SUBLANES = 8
VMEM_BYTES_V7X = 64 * 1024 * 1024
N_CHIPS = 4
N_CORES = 2
N_DEV = N_CHIPS * N_CORES
MESH = pl.DeviceIdType.MESH

N_BLK = N_IN // GROUP_W
PERM = (0, 1, 2, 3, 4, 5, 15, 16, 17, 18, 6, 9, 12, 7, 10, 13, 8, 11, 14)
INV_PERM = tuple(PERM.index(b) for b in range(N_BLK))
P_B, P_C, P_H, P_GA, P_GB, P_Q0 = 0, 1024, 2048, 3072, 4096, 5120
N_NAT = P_Q0 + QKV_W // N_GROUPS * 3
N_GATED = P_Q0

SLAB = 128
CHUNK = 256
PAD = SUBLANES
TQ = 128


def _cparams(sem, vmem_mb):
    assert vmem_mb * 1024 * 1024 < VMEM_BYTES_V7X
    return pltpu.CompilerParams(dimension_semantics=sem, vmem_limit_bytes=vmem_mb * 1024 * 1024)


def _dot(a, b):
    return jnp.dot(a, b, preferred_element_type=F32)


def _dot_nt(a, b):
    return lax.dot_general(a, b, (((1,), (1,)), ((), ())), preferred_element_type=F32)


def _dot_tn(a, b):
    return lax.dot_general(a, b, (((0,), (0,)), ((), ())), preferred_element_type=F32)


def _ln_stats(z):
    mu = jnp.mean(z, -1, keepdims=True)
    zc = z - mu
    var = jnp.mean(zc * zc, -1, keepdims=True)
    rstd = lax.rsqrt(var + LN_EPS)
    return zc * rstd, rstd


def _ln_bwd(dh, xhat, rstd, g):
    dxh = dh * g
    m1 = jnp.mean(dxh, -1, keepdims=True)
    m2 = jnp.mean(dxh * xhat, -1, keepdims=True)
    return rstd * (dxh - m1 - xhat * m2)


def _rows8(rows, width):
    pad = [jnp.zeros((1, width), F32)] * (SUBLANES - len(rows))
    return jnp.concatenate(list(rows) + pad, axis=0)


def _mm_nn(a, w, bias, *, tm, tn, out_dtype, name, vmem_mb=40):
    M, K = a.shape
    if w.ndim == 3:
        assert w.shape[2] == tn
        n_tiles = w.shape[0]
        w_spec = pl.BlockSpec((None, K, tn), lambda i, j: (j, 0, 0))
    else:
        n_tiles = w.shape[1] // tn
        w_spec = pl.BlockSpec((K, tn), lambda i, j: (0, j))

    def body(a_ref, w_ref, b_ref, o_ref):
        o_ref[...] = (_dot(a_ref[...], w_ref[...]) + b_ref[...]).astype(o_ref.dtype)

    return pl.pallas_call(
        body, grid=(M // tm, n_tiles),
        in_specs=[pl.BlockSpec((tm, K), lambda i, j: (i, 0)), w_spec, pl.BlockSpec((1, tn), lambda i, j: (0, j))],
        out_specs=pl.BlockSpec((tm, tn), lambda i, j: (i, j)),
        out_shape=jax.ShapeDtypeStruct((M, n_tiles * tn), out_dtype),
        name=name, compiler_params=_cparams(("parallel", "arbitrary"), vmem_mb))(a, w, bias)


def _mm_nt(a, w, *, tm, a_col=0, name, vmem_mb=40):
    M = a.shape[0]
    N, K = w.shape

    def body(a_ref, w_ref, o_ref):
        o_ref[...] = _dot_nt(a_ref[...], w_ref[...]).astype(o_ref.dtype)

    return pl.pallas_call(
        body, grid=(M // tm,),
        in_specs=[pl.BlockSpec((tm, K), lambda i: (i, a_col)),
                  pl.BlockSpec((N, K), lambda i: (0, 0))],
        out_specs=pl.BlockSpec((tm, N), lambda i: (i, 0)),
        out_shape=jax.ShapeDtypeStruct((M, N), BF16),
        name=name, compiler_params=_cparams(("parallel",), vmem_mb))(a, w)


def _mm_tn(a, g, *, n_out, tn, ts, g_block, g_map, colsum=False, name, vmem_mb=48):
    S, K = a.shape
    n_s = S // ts

    def body(a_ref, g_ref, *rest):
        if colsum:
            o_ref, cs_ref, acc_ref, cacc_ref = rest
        else:
            o_ref, acc_ref = rest
        s = pl.program_id(1)

        @pl.when(s == 0)
        def _():
            acc_ref[...] = jnp.zeros_like(acc_ref)
            if colsum:
                cacc_ref[...] = jnp.zeros_like(cacc_ref)

        gv = g_ref[...]
        acc_ref[...] += _dot_tn(a_ref[...], gv)
        if colsum:
            cacc_ref[...] += jnp.broadcast_to(jnp.sum(gv.astype(F32), axis=0, keepdims=True), cacc_ref.shape)

        @pl.when(s == n_s - 1)
        def _():
            o_ref[...] = acc_ref[...].astype(o_ref.dtype)
            if colsum:
                cs_ref[...] = cacc_ref[...]

    out_specs = [pl.BlockSpec((None, K, tn), lambda j, s: (j, 0, 0))]
    out_shape = [jax.ShapeDtypeStruct((n_out, K, tn), BF16)]
    scratch = [pltpu.VMEM((K, tn), F32)]
    if colsum:
        out_specs.append(pl.BlockSpec((SUBLANES, tn), lambda j, s: (0, j)))
        out_shape.append(jax.ShapeDtypeStruct((SUBLANES, n_out * tn), F32))
        scratch.append(pltpu.VMEM((SUBLANES, tn), F32))
    res = pl.pallas_call(
        body, grid=(n_out, n_s),
        in_specs=[pl.BlockSpec((ts, K), lambda j, s: (s, 0)), pl.BlockSpec(g_block, g_map)],
        out_specs=out_specs, out_shape=out_shape, scratch_shapes=scratch,
        name=name, compiler_params=_cparams(("parallel", "arbitrary"), vmem_mb))(a, g)
    return res if colsum else res[0]


DILS = tuple(d for _, d in GROUPS if d > 1)


def _res_spec(d, tm, width):
    return pl.BlockSpec((d, tm // d, width), lambda i: (0, i, 0))


def _lane_scratch(tm, width):
    return [pltpu.VMEM((tm, LANES), F32)] * (width // LANES)


def _to_residue(val, dst_refs, dils, tm, dtype, scr):
    for c, ref in enumerate(scr):
        ref[...] = val[:, c * LANES:(c + 1) * LANES]
    for dst_ref, d in zip(dst_refs, dils):
        for r in range(d):
            cols = [ref[pl.ds(r, tm // d, stride=d), :] for ref in scr]
            dst_ref[r] = jnp.concatenate(cols, axis=1).astype(dtype)


def _from_residue(rows_of, d, tm, scr):
    for r in range(d):
        v = rows_of(r).astype(F32)
        for c, ref in enumerate(scr):
            ref[pl.ds(r, tm // d, stride=d), :] = v[:, c * LANES:(c + 1) * LANES]
    return jnp.concatenate([ref[...] for ref in scr], axis=1)


def _ln0_fwd(x, g, b, *, tm=512):
    S, Dm = x.shape

    def body(x_ref, g_ref, b_ref, h_ref, hb_ref, *rest):
        xhat, _ = _ln_stats(x_ref[...])
        h = xhat * g_ref[...] + b_ref[...]
        h_ref[...] = h
        hb_ref[...] = h.astype(BF16)
        _to_residue(h, rest[:len(DILS)], DILS, tm, BF16, rest[len(DILS):])

    row = pl.BlockSpec((tm, Dm), lambda i: (i, 0))
    vec = pl.BlockSpec((1, Dm), lambda i: (0, 0))
    return pl.pallas_call(
        body, grid=(S // tm,), in_specs=[row, vec, vec], out_specs=[row, row] + [_res_spec(d, tm, Dm) for d in DILS],
        out_shape=[jax.ShapeDtypeStruct((S, Dm), F32), jax.ShapeDtypeStruct((S, Dm), BF16)]
        + [jax.ShapeDtypeStruct((d, S // d, Dm), BF16) for d in DILS],
        scratch_shapes=_lane_scratch(tm, Dm),
        name="ln0_fwd", compiler_params=_cparams(("parallel",), 32))(x, g, b)


def _slab_spec(S, col0):
    return pl.BlockSpec((S, SLAB), lambda j: (0, col0 // SLAB + j))


def _zero_pads(scr, S):
    scr[0:PAD, :] = jnp.zeros((PAD, SLAB), F32)
    scr[S + PAD:S + 2 * PAD, :] = jnp.zeros((PAD, SLAB), F32)


def _shifted(scr, t):
    return (scr[PAD - 1 + t:PAD - 1 + t + CHUNK, :], scr[PAD + t:PAD + t + CHUNK, :],
            scr[PAD + 1 + t:PAD + 1 + t + CHUNK, :])


def _conv_gate_fwd(proj, conv_w):
    S = proj.shape[0]

    def body(b_ref, c_ref, h_ref, w_ref, o_ref, u_scr):
        _zero_pads(u_scr, S)
        for t in range(0, S, CHUNK):
            u_scr[PAD + t:PAD + t + CHUNK, :] = c_ref[t:t + CHUNK, :].astype(F32) * h_ref[t:t + CHUNK, :].astype(F32)
        w0, w1, w2 = w_ref[0:1, :], w_ref[1:2, :], w_ref[2:3, :]
        for t in range(0, S, CHUNK):
            um, u0, up = _shifted(u_scr, t)
            cv = w0 * um + w1 * u0 + w2 * up
            o_ref[t:t + CHUNK, :] = (b_ref[t:t + CHUNK, :].astype(F32) * cv).astype(BF16)

    return pl.pallas_call(
        body, grid=(D_CONV // SLAB,),
        in_specs=[_slab_spec(S, P_B), _slab_spec(S, P_C), _slab_spec(S, P_H),
                  pl.BlockSpec((3, SLAB), lambda j: (0, j))],
        out_specs=pl.BlockSpec((S, SLAB), lambda j: (0, j)),
        out_shape=jax.ShapeDtypeStruct((S, D_CONV), BF16),
        scratch_shapes=[pltpu.VMEM((S + 2 * PAD, SLAB), F32)],
        name="conv_gate_fwd", compiler_params=_cparams(("parallel",), 40))(proj, proj, proj, conv_w)


def _attn_masks(i, sub, dil):
    a = lax.broadcasted_iota(jnp.int32, (TQ, 2 * TQ), 0)
    j = lax.broadcasted_iota(jnp.int32, (TQ, 2 * TQ), 1)
    rel = jnp.abs(j - RADIUS - a)
    kpos = i * TQ - RADIUS + j
    valid = (rel <= RADIUS) & (kpos >= 0) & (kpos < sub)
    return valid, -(rel * dil).astype(F32)


def _slope(g, h):
    return 2.0 ** (-8.0 * (g * HEADS_PER_GROUP + h + 1) / (N_GROUPS * HEADS_PER_GROUP))


def _window(p_ref, c_ref, n_ref):
    return jnp.concatenate([p_ref[TQ - RADIUS:, :], c_ref[...], n_ref[:RADIUS, :]], axis=0)


def _qkv_specs(nb, col0):
    def spec(col, shift):
        return pl.BlockSpec((None, TQ, GROUP_W), lambda r, i: (r, jnp.clip(i + shift, 0, nb - 1), col))

    return [spec(col0, 0), spec(col0 + 1, -1), spec(col0 + 1, 0), spec(col0 + 1, 1),
            spec(col0 + 2, -1), spec(col0 + 2, 0), spec(col0 + 2, 1)]


def _attn_fwd(qkv, col0, g):
    dil, sub, _ = qkv.shape
    nb = sub // TQ

    def body(q_ref, kp, kc, kn, vp, vc, vn, o_ref, lse_ref):
        valid, base = _attn_masks(pl.program_id(1), sub, dil)
        kwin = _window(kp, kc, kn)
        vwin = _window(vp, vc, vn)
        q = q_ref[...]
        for h in range(HEADS_PER_GROUP):
            sl = slice(h * HEAD_DIM, (h + 1) * HEAD_DIM)
            s = _dot_nt(q[:, sl], kwin[:, sl]) * ATT_SCALE + _slope(g, h) * base
            s = jnp.where(valid, s, MASK_VALUE)
            m = jnp.max(s, -1, keepdims=True)
            p = jnp.exp(s - m)
            den = jnp.sum(p, -1, keepdims=True)
            o_ref[:, sl] = _dot(p.astype(BF16), vwin[:, sl]) / den
            lse_ref[:, sl] = jnp.broadcast_to(m + jnp.log(den), (TQ, HEAD_DIM))

    out = pl.BlockSpec((None, TQ, GROUP_W), lambda r, i: (r, i, 0))
    return pl.pallas_call(
        body, grid=(dil, nb), in_specs=_qkv_specs(nb, col0), out_specs=[out, out],
        out_shape=[jax.ShapeDtypeStruct((dil, sub, GROUP_W), F32)] * 2,
        name=f"attn_fwd_g{g}", compiler_params=_cparams(("parallel", "arbitrary"), 32))(*([qkv] * 7))


def _attn_combine(outs, lses, *, tm=512):
    S = outs[0].shape[1]
    n_col = GROUP_W // LANES

    def body(*refs):
        ins, (c_ref, cb_ref, lt_ref) = refs[:2 * N_GROUPS], refs[2 * N_GROUPS:2 * N_GROUPS + 3]
        scr = refs[2 * N_GROUPS + 3:]
        o, l = [ins[0][0]], [ins[N_GROUPS][0]]
        for k, d in enumerate(DILS):
            o_ref, l_ref = ins[1 + k], ins[N_GROUPS + 1 + k]
            o.append(_from_residue(lambda r: o_ref[r], d, tm, scr[2 * k * n_col:(2 * k + 1) * n_col]))
            l.append(_from_residue(lambda r: l_ref[r], d, tm, scr[(2 * k + 1) * n_col:(2 * k + 2) * n_col]))
        m = jnp.maximum(jnp.maximum(l[0], l[1]), l[2])
        e = [jnp.exp(v - m) for v in l]
        den = e[0] + e[1] + e[2]
        comb = (e[0] * o[0] + e[1] * o[1] + e[2] * o[2]) / den
        c_ref[...] = comb
        cb_ref[...] = comb.astype(BF16)
        lt_ref[...] = m + jnp.log(den)

    row = pl.BlockSpec((tm, GROUP_W), lambda i: (i, 0))
    specs = [_res_spec(GROUPS[g][1], tm, GROUP_W) for g in range(N_GROUPS)]
    return pl.pallas_call(
        body, grid=(S // tm,), in_specs=specs * 2, out_specs=[row] * 3,
        out_shape=[jax.ShapeDtypeStruct((S, GROUP_W), F32), jax.ShapeDtypeStruct((S, GROUP_W), BF16),
                   jax.ShapeDtypeStruct((S, GROUP_W), F32)],
        scratch_shapes=_lane_scratch(tm, GROUP_W) * (2 * len(DILS)),
        name="attn_combine", compiler_params=_cparams(("parallel",), 32))(*outs, *lses)


def _branch_mix(ya_in, comb_b, w_a, w_b, proj, *, tm=512):
    S = ya_in.shape[0]

    def body(ya_ref, cb_ref, wa_ref, wb_ref, ga_ref, gb_ref, yab_ref, mx_ref):
        y_a = _dot(ya_ref[...], wa_ref[...])
        y_b = _dot(cb_ref[...], wb_ref[...])
        yab_ref[:, 0:D_MODEL] = y_a.astype(BF16)
        yab_ref[:, D_MODEL:2 * D_MODEL] = y_b.astype(BF16)
        mx = jax.nn.sigmoid(ga_ref[...].astype(F32)) * y_a + jax.nn.sigmoid(gb_ref[...].astype(F32)) * y_b
        mx_ref[...] = mx.astype(BF16)

    return pl.pallas_call(
        body, grid=(S // tm,),
        in_specs=[pl.BlockSpec((tm, D_CONV), lambda i: (i, 0)), pl.BlockSpec((tm, GROUP_W), lambda i: (i, 0)),
                  pl.BlockSpec((D_CONV, D_MODEL), lambda i: (0, 0)), pl.BlockSpec((GROUP_W, D_MODEL), lambda i: (0, 0)),
                  pl.BlockSpec((tm, D_MODEL), lambda i: (i, P_GA // D_MODEL)),
                  pl.BlockSpec((tm, D_MODEL), lambda i: (i, P_GB // D_MODEL))],
        out_specs=[pl.BlockSpec((tm, 2 * D_MODEL), lambda i: (i, 0)), pl.BlockSpec((tm, D_MODEL), lambda i: (i, 0))],
        out_shape=[jax.ShapeDtypeStruct((S, 2 * D_MODEL), BF16), jax.ShapeDtypeStruct((S, D_MODEL), BF16)],
        name="branch_mix", compiler_params=_cparams(("parallel",), 40))(ya_in, comb_b, w_a, w_b, proj, proj)


def _mix_ln1(mixin, w_o, b_o, h0, g1, b1, *, tm=512):
    S = mixin.shape[0]

    def body(mx_ref, wo_ref, bo_ref, h0_ref, g_ref, b_ref, xh_ref, rs_ref, h1b_ref):
        z = ALPHA * h0_ref[...] + _dot(mx_ref[...], wo_ref[...]) + bo_ref[...]
        xhat, rstd = _ln_stats(z)
        xh_ref[...] = xhat
        rs_ref[...] = jnp.broadcast_to(rstd, (tm, LANES))
        h1b_ref[...] = (xhat * g_ref[...] + b_ref[...]).astype(BF16)

    row = pl.BlockSpec((tm, D_MODEL), lambda i: (i, 0))
    vec = pl.BlockSpec((1, D_MODEL), lambda i: (0, 0))
    return pl.pallas_call(
        body, grid=(S // tm,),
        in_specs=[row, pl.BlockSpec((D_MODEL, D_MODEL), lambda i: (0, 0)), vec, row, vec, vec],
        out_specs=[row, pl.BlockSpec((tm, LANES), lambda i: (i, 0)), row],
        out_shape=[jax.ShapeDtypeStruct((S, D_MODEL), F32), jax.ShapeDtypeStruct((S, LANES), F32),
                   jax.ShapeDtypeStruct((S, D_MODEL), BF16)],
        name="mix_ln1", compiler_params=_cparams(("parallel",), 40))(mixin, w_o, b_o, h0, g1, b1)


def _gelu_parts(cz):
    cdf = 0.5 * (1.0 + lax.erf(cz * INV_SQRT2))
    return cdf, cz * cdf


def _ffn_conv_fwd(up, cw, cb):
    S = up.shape[0]

    def body(a_ref, g_ref, w_ref, cb_ref, o_ref, a_scr):
        _zero_pads(a_scr, S)
        for t in range(0, S, CHUNK):
            a_scr[PAD + t:PAD + t + CHUNK, :] = a_ref[t:t + CHUNK, :].astype(F32)
        w0, w1, w2 = w_ref[0:1, :], w_ref[1:2, :], w_ref[2:3, :]
        for t in range(0, S, CHUNK):
            am, a0, ap = _shifted(a_scr, t)
            _, gel = _gelu_parts(w0 * am + w1 * a0 + w2 * ap + cb_ref[...])
            o_ref[t:t + CHUNK, :] = (gel * g_ref[t:t + CHUNK, :].astype(F32)).astype(BF16)

    return pl.pallas_call(
        body, grid=(D_FF // SLAB,),
        in_specs=[_slab_spec(S, 0), _slab_spec(S, D_FF), pl.BlockSpec((3, SLAB), lambda j: (0, j)),
                  pl.BlockSpec((1, SLAB), lambda j: (0, j))],
        out_specs=pl.BlockSpec((S, SLAB), lambda j: (0, j)),
        out_shape=jax.ShapeDtypeStruct((S, D_FF), BF16),
        scratch_shapes=[pltpu.VMEM((S + 2 * PAD, SLAB), F32)],
        name="ffn_conv_fwd", compiler_params=_cparams(("parallel",), 40))(up, up, cw, cb)


def _down_ln2_loss(f, w_down, b_down, xhat1, g1, b1, g2, b2, target, *, tm=256):
    S = f.shape[0]

    def body(f_ref, wd_ref, bd_ref, xh1_ref, g1_ref, b1_ref, g2_ref, b2_ref, t_ref, dz_ref, dzb_ref, st_ref):
        h1 = xh1_ref[...] * g1_ref[...] + b1_ref[...]
        z = ALPHA * h1 + _dot(f_ref[...], wd_ref[...]) + bd_ref[...]
        xhat, rstd = _ln_stats(z)
        err = xhat * g2_ref[...] + b2_ref[...] - t_ref[...]
        loss = (0.5 / D_MODEL) * jnp.sum(jnp.sum(err * err, axis=1, keepdims=True), axis=0, keepdims=True)
        dh2 = err * (1.0 / D_MODEL)
        dz = _ln_bwd(dh2, xhat, rstd, g2_ref[...])
        dz_ref[...] = dz
        dzb_ref[...] = dz.astype(BF16)
        upd = _rows8([jnp.sum(dh2 * xhat, axis=0, keepdims=True), jnp.sum(dh2, axis=0, keepdims=True),
                      jnp.broadcast_to(loss, (1, D_MODEL)), jnp.sum(dz, axis=0, keepdims=True)], D_MODEL)

        @pl.when(pl.program_id(0) == 0)
        def _():
            st_ref[...] = upd

        @pl.when(pl.program_id(0) != 0)
        def _():
            st_ref[...] += upd

    row = pl.BlockSpec((tm, D_MODEL), lambda i: (i, 0))
    vec = pl.BlockSpec((1, D_MODEL), lambda i: (0, 0))
    return pl.pallas_call(
        body, grid=(S // tm,),
        in_specs=[pl.BlockSpec((tm, D_FF), lambda i: (i, 0)), pl.BlockSpec((D_FF, D_MODEL), lambda i: (0, 0)),
                  vec, row, vec, vec, vec, vec, row],
        out_specs=[row, row, pl.BlockSpec((SUBLANES, D_MODEL), lambda i: (0, 0))],
        out_shape=[jax.ShapeDtypeStruct((S, D_MODEL), F32), jax.ShapeDtypeStruct((S, D_MODEL), BF16),
                   jax.ShapeDtypeStruct((SUBLANES, D_MODEL), F32)],
        name="down_ln2_loss", compiler_params=_cparams(("arbitrary",), 48))(
            f, w_down, b_down, xhat1, g1, b1, g2, b2, target)


def _ffn_conv_bwd(up, df, cw, cb):
    S = up.shape[0]

    def body(a_ref, g_ref, df_ref, w_ref, cb_ref, dup_ref, sm_ref, a_scr, d_scr):
        _zero_pads(a_scr, S)
        _zero_pads(d_scr, S)
        for t in range(0, S, CHUNK):
            a_scr[PAD + t:PAD + t + CHUNK, :] = a_ref[t:t + CHUNK, :].astype(F32)
        w0, w1, w2 = w_ref[0:1, :], w_ref[1:2, :], w_ref[2:3, :]
        zero = jnp.zeros((1, SLAB), F32)
        s_dg, s_dcz, s_w0, s_w1, s_w2 = zero, zero, zero, zero, zero
        for t in range(0, S, CHUNK):
            am, a0, ap = _shifted(a_scr, t)
            cz = w0 * am + w1 * a0 + w2 * ap + cb_ref[...]
            cdf, gel = _gelu_parts(cz)
            dfv = df_ref[t:t + CHUNK, :].astype(F32)
            dgte = dfv * gel
            dcz = dfv * g_ref[t:t + CHUNK, :].astype(F32) * (cdf + cz * jnp.exp(-0.5 * cz * cz) * INV_SQRT_2PI)
            dup_ref[1, t:t + CHUNK, :] = dgte.astype(BF16)
            d_scr[PAD + t:PAD + t + CHUNK, :] = dcz
            s_dg = s_dg + jnp.sum(dgte, axis=0, keepdims=True)
            s_dcz = s_dcz + jnp.sum(dcz, axis=0, keepdims=True)
            s_w0 = s_w0 + jnp.sum(dcz * am, axis=0, keepdims=True)
            s_w1 = s_w1 + jnp.sum(dcz * a0, axis=0, keepdims=True)
            s_w2 = s_w2 + jnp.sum(dcz * ap, axis=0, keepdims=True)
        s_da = zero
        for t in range(0, S, CHUNK):
            dm, d0, dp = _shifted(d_scr, t)
            da = w0 * dp + w1 * d0 + w2 * dm
            dup_ref[0, t:t + CHUNK, :] = da.astype(BF16)
            s_da = s_da + jnp.sum(da, axis=0, keepdims=True)
        sm_ref[...] = _rows8([s_da, s_dg, s_dcz, s_w0, s_w1, s_w2], SLAB)

    return pl.pallas_call(
        body, grid=(D_FF // SLAB,),
        in_specs=[_slab_spec(S, 0), _slab_spec(S, D_FF), pl.BlockSpec((S, SLAB), lambda j: (0, j)),
                  pl.BlockSpec((3, SLAB), lambda j: (0, j)), pl.BlockSpec((1, SLAB), lambda j: (0, j))],
        out_specs=[pl.BlockSpec((2, S, SLAB), lambda j: (0, 0, j)), pl.BlockSpec((SUBLANES, SLAB), lambda j: (0, j))],
        out_shape=[jax.ShapeDtypeStruct((2, S, D_FF), BF16), jax.ShapeDtypeStruct((SUBLANES, D_FF), F32)],
        scratch_shapes=[pltpu.VMEM((S + 2 * PAD, SLAB), F32)] * 2,
        name="ffn_conv_bwd", compiler_params=_cparams(("parallel",), 48))(up, up, df, cw, cb)


def _up_bwd_ln1(dup, w_up3, dz2, xhat1, rstd1, g1, *, tm=512):
    S = dz2.shape[0]
    ns, _, tk = w_up3.shape
    per_plane = D_FF // tk

    def body(du_ref, w_ref, dz2_ref, xh_ref, rs_ref, g_ref, dz_ref, dzb_ref, st_ref, acc_ref):
        i, k = pl.program_id(0), pl.program_id(1)

        @pl.when(k == 0)
        def _():
            acc_ref[...] = ALPHA * dz2_ref[...]

        acc_ref[...] += _dot_nt(du_ref[...], w_ref[...])

        @pl.when(k == ns - 1)
        def _():
            dh = acc_ref[...]
            xhat = xh_ref[...]
            dz = _ln_bwd(dh, xhat, rs_ref[:, 0:1], g_ref[...])
            dz_ref[...] = dz
            dzb_ref[...] = dz.astype(BF16)
            upd = _rows8([jnp.sum(dh * xhat, axis=0, keepdims=True), jnp.sum(dh, axis=0, keepdims=True),
                          jnp.sum(dz, axis=0, keepdims=True)], D_MODEL)

            @pl.when(i == 0)
            def _():
                st_ref[...] = upd

            @pl.when(i != 0)
            def _():
                st_ref[...] += upd

    row = pl.BlockSpec((tm, D_MODEL), lambda i, k: (i, 0))
    return pl.pallas_call(
        body, grid=(S // tm, ns),
        in_specs=[pl.BlockSpec((None, tm, tk), lambda i, k: (k // per_plane, i, k % per_plane)),
                  pl.BlockSpec((None, D_MODEL, tk), lambda i, k: (k, 0, 0)),
                  row, row, pl.BlockSpec((tm, LANES), lambda i, k: (i, 0)),
                  pl.BlockSpec((1, D_MODEL), lambda i, k: (0, 0))],
        out_specs=[row, row, pl.BlockSpec((SUBLANES, D_MODEL), lambda i, k: (0, 0))],
        out_shape=[jax.ShapeDtypeStruct((S, D_MODEL), F32), jax.ShapeDtypeStruct((S, D_MODEL), BF16),
                   jax.ShapeDtypeStruct((SUBLANES, D_MODEL), F32)],
        scratch_shapes=[pltpu.VMEM((tm, D_MODEL), F32)],
        name="up_bwd_ln1", compiler_params=_cparams(("arbitrary", "arbitrary"), 40))(
            dup, w_up3, dz2, xhat1, rstd1, g1)


def _mix_bwd(dz1b, w_o, proj, yab, *, tm=512):
    S = dz1b.shape[0]
    half = D_MODEL // 2
    gate0 = P_GA // half

    def body(dz_ref, wo_ref, gt_ref, y_ref, dy_ref, dg_ref):
        dmx = _dot_nt(dz_ref[...], wo_ref[...])
        sg = jax.nn.sigmoid(gt_ref[...].astype(F32))
        dy_ref[...] = (dmx * sg).astype(BF16)
        dg_ref[...] = (dmx * y_ref[...].astype(F32) * sg * (1.0 - sg)).astype(BF16)

    blk = pl.BlockSpec((tm, half), lambda i, j: (i, j))
    return pl.pallas_call(
        body, grid=(S // tm, 4),
        in_specs=[pl.BlockSpec((tm, D_MODEL), lambda i, j: (i, 0)),
                  pl.BlockSpec((half, D_MODEL), lambda i, j: (j % 2, 0)),
                  pl.BlockSpec((tm, half), lambda i, j: (i, gate0 + j)), blk],
        out_specs=[blk, pl.BlockSpec((None, tm, half), lambda i, j: (3 + j // 2, i, j % 2))],
        out_shape=[jax.ShapeDtypeStruct((S, 2 * D_MODEL), BF16),
                   jax.ShapeDtypeStruct((N_GATED // D_MODEL, S, D_MODEL), BF16)],
        name="mix_bwd", compiler_params=_cparams(("parallel", "arbitrary"), 32))(dz1b, w_o, proj, yab)


def _conv_gate_bwd(proj, dya_in, conv_w, dgated):
    S = proj.shape[0]

    def body(b_ref, c_ref, h_ref, dy_ref, w_ref, _, o_ref, sm_ref, u_scr, d_scr):
        _zero_pads(u_scr, S)
        _zero_pads(d_scr, S)
        for t in range(0, S, CHUNK):
            u_scr[PAD + t:PAD + t + CHUNK, :] = c_ref[t:t + CHUNK, :].astype(F32) * h_ref[t:t + CHUNK, :].astype(F32)
        w0, w1, w2 = w_ref[0:1, :], w_ref[1:2, :], w_ref[2:3, :]
        zero = jnp.zeros((1, SLAB), F32)
        s_w0, s_w1, s_w2 = zero, zero, zero
        for t in range(0, S, CHUNK):
            um, u0, up = _shifted(u_scr, t)
            dy = dy_ref[t:t + CHUNK, :].astype(F32)
            o_ref[0, t:t + CHUNK, :] = (dy * (w0 * um + w1 * u0 + w2 * up)).astype(BF16)
            dcv = dy * b_ref[t:t + CHUNK, :].astype(F32)
            d_scr[PAD + t:PAD + t + CHUNK, :] = dcv
            s_w0 = s_w0 + jnp.sum(dcv * um, axis=0, keepdims=True)
            s_w1 = s_w1 + jnp.sum(dcv * u0, axis=0, keepdims=True)
            s_w2 = s_w2 + jnp.sum(dcv * up, axis=0, keepdims=True)
        for t in range(0, S, CHUNK):
            dm, d0, dp = _shifted(d_scr, t)
            du = w0 * dp + w1 * d0 + w2 * dm
            o_ref[1, t:t + CHUNK, :] = (du * h_ref[t:t + CHUNK, :].astype(F32)).astype(BF16)
            o_ref[2, t:t + CHUNK, :] = (du * c_ref[t:t + CHUNK, :].astype(F32)).astype(BF16)
        sm_ref[...] = _rows8([s_w0, s_w1, s_w2], SLAB)

    return pl.pallas_call(
        body, grid=(D_CONV // SLAB,),
        in_specs=[_slab_spec(S, P_B), _slab_spec(S, P_C), _slab_spec(S, P_H),
                  pl.BlockSpec((S, SLAB), lambda j: (0, j)), pl.BlockSpec((3, SLAB), lambda j: (0, j)),
                  pl.BlockSpec(memory_space=pl.ANY)],
        out_specs=[pl.BlockSpec((3, S, SLAB), lambda j: (0, 0, j)), pl.BlockSpec((SUBLANES, SLAB), lambda j: (0, j))],
        out_shape=[jax.ShapeDtypeStruct(dgated.shape, BF16), jax.ShapeDtypeStruct((SUBLANES, D_CONV), F32)],
        scratch_shapes=[pltpu.VMEM((S + 2 * PAD, SLAB), F32)] * 2, input_output_aliases={5: 0},
        name="conv_gate_bwd", compiler_params=_cparams(("parallel",), 48))(proj, proj, proj, dya_in, conv_w, dgated)


def _comb_bwd(dyab, w_b, comb, lse_tot, *, tm=512):
    S = comb.shape[0]

    def body(dy_ref, wb_ref, c_ref, lt_ref, *rest):
        outs, scr = rest[:3 * N_GROUPS], rest[3 * N_GROUPS:]
        dcb = _dot_nt(dy_ref[...], wb_ref[...]).astype(BF16)
        dc = dcb.astype(F32)
        prod = dc * c_ref[...]
        heads = [jnp.broadcast_to(jnp.sum(prod[:, h * HEAD_DIM:(h + 1) * HEAD_DIM], axis=1, keepdims=True),
                                  (tm, HEAD_DIM)) for h in range(HEADS_PER_GROUP)]
        vals = (dc, lt_ref[...], jnp.concatenate(heads, axis=1))
        for k, (val, dtype) in enumerate(zip(vals, (BF16, F32, F32))):
            outs[k][0] = val.astype(dtype)
            _to_residue(val, [outs[3 * (1 + j) + k] for j in range(len(DILS))], DILS, tm, dtype, scr)

    row = pl.BlockSpec((tm, GROUP_W), lambda i: (i, 0))
    out_specs, out_shape = [], []
    for _, d in GROUPS:
        out_specs += [_res_spec(d, tm, GROUP_W)] * 3
        out_shape += [jax.ShapeDtypeStruct((d, S // d, GROUP_W), t) for t in (BF16, F32, F32)]
    res = pl.pallas_call(
        body, grid=(S // tm,),
        in_specs=[pl.BlockSpec((tm, D_MODEL), lambda i: (i, 1)), pl.BlockSpec((GROUP_W, D_MODEL), lambda i: (0, 0)),
                  row, row],
        out_specs=out_specs, out_shape=out_shape, scratch_shapes=_lane_scratch(tm, GROUP_W),
        name="comb_bwd", compiler_params=_cparams(("parallel",), 32))(dyab, w_b, comb, lse_tot)
    return [tuple(res[3 * g:3 * g + 3]) for g in range(N_GROUPS)]


def _attn_bwd(qkv, col0, g, dcomb, lse_tot, delta):
    dil, sub, _ = qkv.shape
    nb = sub // TQ

    def body(q_ref, kp, kc, kn, vp, vc, vn, do_ref, lse_ref, dl_ref, dq_ref, dk_ref, dv_ref, ak, av):
        i = pl.program_id(1)

        @pl.when(i == 0)
        def _():
            ak[...] = jnp.zeros_like(ak)
            av[...] = jnp.zeros_like(av)

        @pl.when(i < nb)
        def _():
            valid, base = _attn_masks(i, sub, dil)
            kwin = _window(kp, kc, kn)
            vwin = _window(vp, vc, vn)
            q = q_ref[...]
            do = do_ref[...]
            for h in range(HEADS_PER_GROUP):
                sl = slice(h * HEAD_DIM, (h + 1) * HEAD_DIM)
                s = _dot_nt(q[:, sl], kwin[:, sl]) * ATT_SCALE + _slope(g, h) * base
                s = jnp.where(valid, s, MASK_VALUE)
                p = jnp.exp(s - lse_ref[:, h * HEAD_DIM:h * HEAD_DIM + 1])
                dp = _dot_nt(do[:, sl], vwin[:, sl])
                ds = (p * (dp - dl_ref[:, h * HEAD_DIM:h * HEAD_DIM + 1])).astype(BF16)
                dq_ref[:, sl] = (_dot(ds, kwin[:, sl]) * ATT_SCALE).astype(BF16)
                ak[RADIUS:RADIUS + 2 * TQ, sl] += _dot_tn(ds, q[:, sl]) * ATT_SCALE
                av[RADIUS:RADIUS + 2 * TQ, sl] += _dot_tn(p.astype(BF16), do[:, sl])

        dk_ref[...] = ak[0:TQ, :].astype(BF16)
        dv_ref[...] = av[0:TQ, :].astype(BF16)
        ak[0:2 * TQ, :] = ak[TQ:3 * TQ, :]
        av[0:2 * TQ, :] = av[TQ:3 * TQ, :]
        ak[2 * TQ:3 * TQ, :] = jnp.zeros((TQ, GROUP_W), F32)
        av[2 * TQ:3 * TQ, :] = jnp.zeros((TQ, GROUP_W), F32)

    tok = pl.BlockSpec((None, TQ, GROUP_W), lambda r, i: (r, jnp.minimum(i, nb - 1), 0))
    dkv_spec = pl.BlockSpec((None, TQ, GROUP_W), lambda r, i: (r, jnp.maximum(i - 1, 0), 0))
    return pl.pallas_call(
        body, grid=(dil, nb + 1), in_specs=_qkv_specs(nb, col0) + [tok, tok, tok],
        out_specs=[tok, dkv_spec, dkv_spec], out_shape=[jax.ShapeDtypeStruct((dil, sub, GROUP_W), BF16)] * 3,
        scratch_shapes=[pltpu.VMEM((3 * TQ, GROUP_W), F32)] * 2,
        name=f"attn_bwd_g{g}", compiler_params=_cparams(("arbitrary", "arbitrary"), 32))(
            *([qkv] * 7), dcomb, lse_tot, delta)


def _gated_bwd(dgated, w_nat, dz1, *, tm=512):
    n_planes, S, _ = dgated.shape

    def body(dg_ref, w_ref, dz_ref, o_ref, acc_ref):
        k = pl.program_id(1)

        @pl.when(k == 0)
        def _():
            acc_ref[...] = ALPHA * dz_ref[...]

        acc_ref[...] += _dot_nt(dg_ref[...], w_ref[...])

        @pl.when(k == n_planes - 1)
        def _():
            o_ref[...] = acc_ref[...]

    row = pl.BlockSpec((tm, D_MODEL), lambda i, k: (i, 0))
    return pl.pallas_call(
        body, grid=(S // tm, n_planes),
        in_specs=[pl.BlockSpec((None, tm, D_MODEL), lambda i, k: (k, i, 0)),
                  pl.BlockSpec((D_MODEL, D_MODEL), lambda i, k: (0, k)), row],
        out_specs=row, out_shape=jax.ShapeDtypeStruct((S, D_MODEL), F32),
        scratch_shapes=[pltpu.VMEM((tm, D_MODEL), F32)],
        name="gated_bwd", compiler_params=_cparams(("parallel", "arbitrary"), 32))(dgated, w_nat, dz1)


def _in_bwd_ln0(dh0_part, dqkv, w_qkv, x, g0, *, tm=512):
    S = x.shape[0]
    n_in = 3 * N_GROUPS

    def body(*refs):
        dh_ref, d_refs, w_refs = refs[0], refs[1:1 + n_in], refs[1 + n_in:1 + n_in + N_GROUPS]
        x_ref, g_ref, gx_ref, st_ref, acc_ref, *tmp_ref = refs[1 + n_in + N_GROUPS:]
        acc_ref[...] = dh_ref[...]
        for g, (_, d) in enumerate(GROUPS):
            rows = [jnp.concatenate([d_refs[3 * g + k][r] for k in range(3)], axis=1) for r in range(d)]
            res = _dot_nt(jnp.concatenate(rows, axis=0), w_refs[g][...])
            if d == 1:
                acc_ref[...] += res
            else:
                n = tm // d
                acc_ref[...] += _from_residue(lambda r: res[r * n:(r + 1) * n, :], d, tm, tmp_ref)
        dh = acc_ref[...]
        xhat, rstd = _ln_stats(x_ref[...])
        gx_ref[...] = _ln_bwd(dh, xhat, rstd, g_ref[...])
        upd = _rows8([jnp.sum(dh * xhat, axis=0, keepdims=True), jnp.sum(dh, axis=0, keepdims=True)], D_MODEL)

        @pl.when(pl.program_id(0) == 0)
        def _():
            st_ref[...] = upd

        @pl.when(pl.program_id(0) != 0)
        def _():
            st_ref[...] += upd

    row = pl.BlockSpec((tm, D_MODEL), lambda i: (i, 0))
    d_specs = []
    for _, d in GROUPS:
        d_specs += [_res_spec(d, tm, GROUP_W)] * 3
    operands = [dh0_part] + [a for grp in dqkv for a in grp] + list(w_qkv) + [x, g0]
    return pl.pallas_call(
        body, grid=(S // tm,),
        in_specs=[row] + d_specs + [pl.BlockSpec((D_MODEL, QKV_W), lambda i: (0, 0))] * N_GROUPS
        + [row, pl.BlockSpec((1, D_MODEL), lambda i: (0, 0))],
        out_specs=[row, pl.BlockSpec((SUBLANES, D_MODEL), lambda i: (0, 0))],
        out_shape=[jax.ShapeDtypeStruct((S, D_MODEL), F32), jax.ShapeDtypeStruct((SUBLANES, D_MODEL), F32)],
        scratch_shapes=[pltpu.VMEM((tm, D_MODEL), F32)] + _lane_scratch(tm, D_MODEL),
        name="in_bwd_ln0", compiler_params=_cparams(("arbitrary",), 48))(*operands)


HBM_SPEC = pl.BlockSpec(memory_space=pltpu.HBM)


def _place():
    x, y, c = lax.axis_index("x"), lax.axis_index("y"), lax.axis_index("c")
    chips = [(1 - x, y), (x, 1 - y), (1 - x, 1 - y)]
    return x, y, c, chips


def _allgather_shards(shards, after, *, name, collective_id):
    n = len(shards)
    per = 6

    def body(*refs):
        ins, outs = refs[:n], refs[n + len(after):2 * n + len(after)]
        send_sems, recv_sems, loc_sems = refs[2 * n + len(after):]
        x, y, c, chips = _place()
        me = 2 * x + y
        sib = (x, y, 1 - c)
        peers = [sib] + [(px, py, c) for px, py in chips]
        barrier = pltpu.get_barrier_semaphore()
        for peer in peers:
            pl.semaphore_signal(barrier, inc=1, device_id=peer, device_id_type=MESH)
        pl.semaphore_wait(barrier, len(peers))

        def rcopy(w, k, src, dst, to):
            return pltpu.make_async_remote_copy(src_ref=src, dst_ref=dst, send_sem=send_sems.at[per * w + k],
                                                recv_sem=recv_sems.at[per * w + k], device_id=to, device_id_type=MESH)

        split = [s.shape[0] == N_CORES for s in shards]
        half = lambda w: c if split[w] else 0
        local, sends = [], []
        for w in range(n):
            cp = pltpu.make_async_copy(ins[w], outs[w].at[me], loc_sems.at[w])
            cp.start()
            local.append(cp)
            for j, (px, py) in enumerate(chips):
                cp = rcopy(w, j, ins[w].at[half(w)], outs[w].at[me, half(w)], (px, py, c))
                cp.start()
                sends.append(cp)
        for w in range(n):
            for j, (px, py) in enumerate(chips):
                slot = outs[w].at[2 * px + py, half(w)]
                rcopy(w, j, slot, slot, (px, py, c)).wait_recv()
                if split[w]:
                    cp = rcopy(w, 3 + j, slot, slot, sib)
                    cp.start()
                    sends.append(cp)
        for w in range(n):
            if split[w]:
                for j, (px, py) in enumerate(chips):
                    slot = outs[w].at[2 * px + py, 1 - c]
                    rcopy(w, 3 + j, slot, slot, sib).wait_recv()
        for cp in sends:
            cp.wait_send()
        for cp in local:
            cp.wait()

    return pl.kernel(
        body, out_type=[jax.ShapeDtypeStruct((N_CHIPS,) + s.shape, s.dtype) for s in shards],
        mesh=plsc.ScalarSubcoreMesh(axis_name="sequencer", num_cores=1),
        scratch_types=[pltpu.SemaphoreType.DMA((per * n,)), pltpu.SemaphoreType.DMA((per * n,)),
                       pltpu.SemaphoreType.DMA((n,))],
        name=name, compiler_params=pltpu.CompilerParams(collective_id=collective_id))(*shards, *after)


def _exchange_grads(grads, *, name, collective_id):
    n = len(grads)
    per = 7

    def body(*refs):
        ins, outs = refs[:n], refs[n:2 * n]
        send_sems, recv_sems, loc_sems = refs[2 * n:]
        x, y, c, chips = _place()
        me = 2 * x + y
        sib = (x, y, 1 - c)
        peers = [sib] + [(px, py, c) for px, py in chips]
        barrier = pltpu.get_barrier_semaphore()
        for peer in peers:
            pl.semaphore_signal(barrier, inc=1, device_id=peer, device_id_type=MESH)
        pl.semaphore_wait(barrier, len(peers))

        def rcopy(w, k, src, dst, to):
            return pltpu.make_async_remote_copy(src_ref=src, dst_ref=dst, send_sem=send_sems.at[per * w + k],
                                                recv_sem=recv_sems.at[per * w + k], device_id=to, device_id_type=MESH)

        local, sends = [], []
        for w in range(n):
            cp = pltpu.make_async_copy(ins[w].at[me], outs[w].at[c, me], loc_sems.at[w])
            cp.start()
            local.append(cp)
            cp = rcopy(w, 0, ins[w].at[me], outs[w].at[c, me], sib)
            cp.start()
            sends.append(cp)
            for j, (px, py) in enumerate(chips):
                cp = rcopy(w, 1 + j, ins[w].at[2 * px + py], outs[w].at[c, me], (px, py, c))
                cp.start()
                sends.append(cp)
        for w in range(n):
            for j, (px, py) in enumerate(chips):
                slot = outs[w].at[c, 2 * px + py]
                rcopy(w, 1 + j, slot, slot, (px, py, c)).wait_recv()
                cp = rcopy(w, 4 + j, slot, slot, sib)
                cp.start()
                sends.append(cp)
        for w in range(n):
            slot = outs[w].at[1 - c, me]
            rcopy(w, 0, slot, slot, sib).wait_recv()
            for j, (px, py) in enumerate(chips):
                slot = outs[w].at[1 - c, 2 * px + py]
                rcopy(w, 4 + j, slot, slot, sib).wait_recv()
        for cp in sends:
            cp.wait_send()
        for cp in local:
            cp.wait()

    return pl.kernel(
        body, out_type=[jax.ShapeDtypeStruct((N_CORES,) + g.shape, g.dtype) for g in grads],
        mesh=plsc.ScalarSubcoreMesh(axis_name="sequencer", num_cores=1),
        scratch_types=[pltpu.SemaphoreType.DMA((per * n,)), pltpu.SemaphoreType.DMA((per * n,)),
                       pltpu.SemaphoreType.DMA((n,))],
        name=name, compiler_params=pltpu.CompilerParams(collective_id=collective_id))(*grads)


def _allgather_small(vec):
    def body(v_ref, o_ref, send_sems, recv_sems, loc_sem):
        x, y, c = lax.axis_index("x"), lax.axis_index("y"), lax.axis_index("c")
        me = 4 * x + 2 * y + c

        def peer(k):
            flip = lambda v, bit: 1 - v if (k >> bit) & 1 else v
            return flip(x, 2), flip(y, 1), flip(c, 0)

        loc = pltpu.make_async_copy(v_ref, o_ref.at[me], loc_sem)
        loc.start()
        sends = []
        for k in range(1, N_DEV):
            cp = pltpu.make_async_remote_copy(src_ref=v_ref, dst_ref=o_ref.at[me], send_sem=send_sems.at[k - 1],
                                              recv_sem=recv_sems.at[k - 1], device_id=peer(k), device_id_type=MESH)
            cp.start()
            sends.append(cp)
        for k in range(1, N_DEV):
            px, py, pc = peer(k)
            pltpu.make_async_remote_copy(src_ref=v_ref, dst_ref=o_ref.at[4 * px + 2 * py + pc],
                                         send_sem=send_sems.at[k - 1], recv_sem=recv_sems.at[k - 1],
                                         device_id=(px, py, pc), device_id_type=MESH).wait_recv()
        for cp in sends:
            cp.wait_send()
        loc.wait()

    return pl.pallas_call(
        body, in_specs=[HBM_SPEC], out_specs=HBM_SPEC,
        out_shape=jax.ShapeDtypeStruct((N_DEV,) + vec.shape, vec.dtype),
        scratch_shapes=[pltpu.SemaphoreType.DMA((N_DEV - 1,)), pltpu.SemaphoreType.DMA((N_DEV - 1,)),
                        pltpu.SemaphoreType.DMA],
        name="allgather_small")(vec)


def _adamw(w, g, m, v):
    m = ADAM_B1 * m + (1.0 - ADAM_B1) * g
    v = ADAM_B2 * v + (1.0 - ADAM_B2) * (g * g)
    m_hat = m / (1.0 - ADAM_B1 ** ADAM_STEP)
    v_hat = v / (1.0 - ADAM_B2 ** ADAM_STEP)
    delta = -ADAM_LR * (m_hat / (jnp.sqrt(v_hat) + ADAM_EPS) + ADAM_WD * w)
    return delta, m, v


def _reduce_adamw(parts, w, m, v, *, tr, name):
    R, C = w.shape

    def body(p_ref, w_ref, m_ref, v_ref, g_ref, d_ref, nm_ref, nv_ref):
        def core_sum(cc):
            s = p_ref[cc, 0].astype(F32)
            for k in range(1, N_CHIPS):
                s = s + p_ref[cc, k].astype(F32)
            return s

        g = core_sum(0) + core_sum(1)
        delta, nm, nv = _adamw(w_ref[...], g, m_ref[...], v_ref[...])
        g_ref[...] = g
        d_ref[...] = delta
        nm_ref[...] = nm
        nv_ref[...] = nv

    blk = pl.BlockSpec((tr, C), lambda i: (i, 0))
    return pl.pallas_call(
        body, grid=(R // tr,),
        in_specs=[pl.BlockSpec((N_CORES, N_CHIPS, tr, C), lambda i: (0, 0, i, 0)), blk, blk, blk],
        out_specs=[blk] * 4, out_shape=[jax.ShapeDtypeStruct((R, C), F32)] * 4,
        name=name, compiler_params=_cparams(("parallel",), 40))(parts, w, m, v)


def _sum_devices(allv):
    _, R, _ = allv.shape

    def body(a_ref, o_ref):
        s = a_ref[0]
        for d in range(1, N_DEV):
            s = s + a_ref[d]
        o_ref[...] = s

    return pl.pallas_call(body, out_shape=jax.ShapeDtypeStruct((R, LANES), F32), name="sum_small")(allv)


def _adamw_small(w, g, m, v):
    def body(w_ref, g_ref, m_ref, v_ref, d_ref, nm_ref, nv_ref):
        delta, nm, nv = _adamw(w_ref[...], g_ref[...], m_ref[...], v_ref[...])
        d_ref[...] = delta
        nm_ref[...] = nm
        nv_ref[...] = nv

    return pl.pallas_call(body, out_shape=[jax.ShapeDtypeStruct(w.shape, F32)] * 3, name="adamw_small")(w, g, m, v)


def _pack(pieces):
    flat = [p.reshape(-1) for p in pieces]
    offs, n = [], 0
    for f in flat:
        offs.append(n)
        n += f.shape[0]
    total = -(-n // (SUBLANES * LANES)) * SUBLANES * LANES
    flat.append(jnp.zeros((total - n,), F32))
    return jnp.concatenate(flat).reshape(total // LANES, LANES), offs


def _local_step(x, target, p, wfull, on_ready=lambda group: None):
    S = x.shape[0]
    w_in3, w_up3 = wfull["w_in"], wfull["w_up"]
    w_a, w_o, w_down, w_b = wfull["w_a"], wfull["w_o"], wfull["w_down"], wfull["w_b"]
    conv_w, ffn_conv_w = wfull["conv_w"], wfull["ffn_conv_w"]
    dils = [d for _, d in GROUPS]

    w_blocks = w_in3.transpose(1, 0, 2).reshape(D_MODEL, N_BLK, GROUP_W)
    w_perm = jnp.concatenate([w_blocks[:, b] for b in PERM], axis=1)
    b_blocks = p["b_in"].reshape(N_BLK, GROUP_W)
    b_perm = jnp.concatenate([b_blocks[b] for b in PERM]).reshape(1, N_IN)
    w_nat, b_nat = w_perm[:, :N_NAT], b_perm[:, :N_NAT]
    qkv_cols = [slice(P_Q0 + g * QKV_W, P_Q0 + (g + 1) * QKV_W) for g in range(N_GROUPS)]
    w_qkv = [w_perm[:, c] for c in qkv_cols]

    h0, h0b, *h0_res = _ln0_fwd(x, p["ln0_g"], p["ln0_b"])
    h0_rows = [h0b] + [h.reshape(S, D_MODEL) for h in h0_res]
    proj = _mm_nn(h0b, w_nat, b_nat, tm=512, tn=N_NAT // 2, out_dtype=BF16, name="proj")
    qkv = [proj[None]]
    for g in range(1, N_GROUPS):
        t = _mm_nn(h0_rows[g], w_qkv[g], b_perm[:, qkv_cols[g]], tm=512, tn=QKV_W, out_dtype=BF16, name=f"proj_qkv{g}")
        qkv.append(t.reshape(dils[g], S // dils[g], QKV_W))
    col0 = [P_Q0 // GROUP_W] + [0] * (N_GROUPS - 1)
    ya_in = _conv_gate_fwd(proj, conv_w)
    att = [_attn_fwd(qkv[g], col0[g], g) for g in range(N_GROUPS)]
    comb, comb_b, lse_tot = _attn_combine([a[0] for a in att], [a[1] for a in att])
    yab, mixin = _branch_mix(ya_in, comb_b, w_a, w_b, proj)
    xhat1, rstd1, h1b = _mix_ln1(mixin, w_o, p["b_o"], h0, p["ln1_g"], p["ln1_b"])
    up = _mm_nn(h1b, w_up3, p["b_up"], tm=512, tn=w_up3.shape[2], out_dtype=BF16, name="up")
    f = _ffn_conv_fwd(up, ffn_conv_w, p["ffn_conv_b"])
    dz2, dz2b, st2 = _down_ln2_loss(f, w_down, p["b_down"], xhat1, p["ln1_g"], p["ln1_b"],
                                    p["ln2_g"], p["ln2_b"], target)

    gw = {}
    gw["w_down"] = _mm_tn(f, dz2b, n_out=1, tn=D_MODEL, ts=512, g_block=(512, D_MODEL),
                          g_map=lambda j, s: (s, 0), name="grad_w_down").reshape(N_CHIPS, D_FF // N_CHIPS, D_MODEL)
    df = _mm_nt(dz2b, w_down, tm=512, name="df")
    dup, sm_ffn = _ffn_conv_bwd(up, df, ffn_conv_w, p["ffn_conv_b"])
    up_tn = w_up3.shape[2]
    up_pp = D_FF // up_tn
    gw["w_up"] = _mm_tn(h1b, dup, n_out=N_CHIPS, tn=up_tn, ts=512, g_block=(None, 512, up_tn),
                        g_map=lambda j, s: (j // up_pp, s, j % up_pp), name="grad_w_up")
    on_ready({n: gw[n] for n in ("w_down", "w_up")})
    dz1, dz1b, st1 = _up_bwd_ln1(dup, w_up3, dz2, xhat1, rstd1, p["ln1_g"])

    gw["w_o"] = _mm_tn(mixin, dz1b, n_out=1, tn=D_MODEL, ts=512, g_block=(512, D_MODEL),
                       g_map=lambda j, s: (s, 0), name="grad_w_o").reshape(N_CHIPS, D_MODEL // N_CHIPS, D_MODEL)
    dyab, dgated = _mix_bwd(dz1b, w_o, proj, yab)
    gw["w_a"] = _mm_tn(ya_in, dyab, n_out=1, tn=D_MODEL, ts=512, g_block=(512, D_MODEL),
                       g_map=lambda j, s: (s, 0), name="grad_w_a").reshape(N_CHIPS, D_CONV // N_CHIPS, D_MODEL)
    b_tn = D_MODEL // N_CHIPS
    gw["w_b"] = _mm_tn(comb_b, dyab, n_out=N_CHIPS, tn=b_tn, ts=512, g_block=(512, b_tn),
                       g_map=lambda j, s: (s, D_MODEL // b_tn + j), name="grad_w_b")
    on_ready({n: gw[n] for n in ("w_o", "w_a", "w_b")})
    dya_in = _mm_nt(dyab, w_a, tm=512, a_col=0, name="dya_in")
    dgated, sm_conv = _conv_gate_bwd(proj, dya_in, conv_w, dgated)
    att_stats = _comb_bwd(dyab, w_b, comb, lse_tot)
    dqkv = [_attn_bwd(qkv[g], col0[g], g, *att_stats[g]) for g in range(N_GROUPS)]

    n_planes = dgated.shape[0]
    gated_w, gated_cs = _mm_tn(h0b, dgated, n_out=n_planes, tn=D_MODEL, ts=512, g_block=(None, 512, D_MODEL),
                               g_map=lambda j, s: (j, s, 0), colsum=True, name="grad_w_in_gated")
    w_pieces, b_pieces = [gated_w.transpose(1, 0, 2).reshape(D_MODEL, N_GATED)], [gated_cs[0]]
    for g in range(N_GROUPS):
        for k, nm in enumerate("qkv"):
            pw, pc = _mm_tn(h0_rows[g], dqkv[g][k].reshape(S, GROUP_W), n_out=1, tn=GROUP_W, ts=512,
                            g_block=(512, GROUP_W), g_map=lambda j, s: (s, 0), colsum=True, name=f"grad_w_in_{nm}{g}")
            w_pieces.append(pw[0])
            b_pieces.append(pc[0])
    dw_blocks = jnp.concatenate(w_pieces, axis=1).reshape(D_MODEL, N_BLK, GROUP_W)
    dw_ref = jnp.concatenate([dw_blocks[:, b] for b in INV_PERM], axis=1)
    gw["w_in"] = dw_ref.reshape(D_MODEL, N_CHIPS, N_IN // N_CHIPS).transpose(1, 0, 2)
    on_ready({"w_in": gw["w_in"]})
    db_blocks = jnp.concatenate(b_pieces).reshape(N_BLK, GROUP_W)
    grad_b_in = jnp.concatenate([db_blocks[b] for b in INV_PERM])

    dh0_part = _gated_bwd(dgated, w_nat, dz1)
    grad_x, st0 = _in_bwd_ln0(dh0_part, dqkv, w_qkv, x, p["ln0_g"])

    small = {
        "loss": st2[2:3, 0:1],
        "ln0_g": st0[0], "ln0_b": st0[1], "b_in": grad_b_in, "conv_w": sm_conv[0:3],
        "b_o": st1[2], "ln1_g": st1[0], "ln1_b": st1[1],
        "b_up": jnp.concatenate([sm_ffn[0], sm_ffn[1]]), "ffn_conv_w": sm_ffn[3:6], "ffn_conv_b": sm_ffn[2],
        "b_down": st2[3], "ln2_g": st2[0], "ln2_b": st2[1],
    }
    return grad_x, gw, small


BIG = ("w_in", "w_a", "w_b", "w_o", "w_up", "w_down")
CONV = ("conv_w", "ffn_conv_w")
VECS = ("ln0_g", "ln0_b", "b_in", "b_o", "ln1_g", "ln1_b", "b_up", "ffn_conv_b", "b_down", "ln2_g", "ln2_b")
ORDER = ("ln0_g", "ln0_b", "w_in", "b_in", "conv_w", "w_a", "w_b", "w_o", "b_o", "ln1_g", "ln1_b", "w_up", "b_up",
         "ffn_conv_w", "ffn_conv_b", "w_down", "b_down", "ln2_g", "ln2_b")
SMALL_ORDER = ("loss",) + VECS + CONV


def _step(x, target, W, Mo, Vo):
    x2, t2 = x[0], target[0]
    big2 = {n: W[n][0] for n in BIG}
    halves = lambda a: a.astype(BF16).reshape(N_CORES, a.shape[0] // N_CORES, a.shape[1])
    whole = lambda g: g.reshape(N_CHIPS, g.shape[1] * g.shape[2], g.shape[3])
    later = tuple(n for n in BIG if n != "w_in")
    first = _allgather_shards([halves(big2["w_in"])], [], name="allgather_w_in", collective_id=1)
    rest = _allgather_shards([halves(big2[n]) for n in later] + [W[n] for n in CONV], first,
                             name="allgather_rest", collective_id=2)
    gathered = {n: whole(g) for n, g in zip(("w_in",) + later + CONV, first + rest)}
    wfull = {
        "w_in": gathered["w_in"], "w_up": gathered["w_up"],
        "w_a": gathered["w_a"].reshape(D_CONV, D_MODEL), "w_o": gathered["w_o"].reshape(D_MODEL, D_MODEL),
        "w_down": gathered["w_down"].reshape(D_FF, D_MODEL),
        "w_b": gathered["w_b"].transpose(1, 0, 2).reshape(GROUP_W, D_MODEL),
        "conv_w": gathered["conv_w"].transpose(1, 0, 2).reshape(3, D_CONV),
        "ffn_conv_w": gathered["ffn_conv_w"].transpose(1, 0, 2).reshape(3, D_FF),
    }
    pvec = {n: W[n].reshape(1, -1) for n in VECS}

    parts = {}
    exchange_ids = iter((3, 4, 5))

    def exchange(group):
        names = tuple(group)
        res = _exchange_grads([group[n] for n in names], name="exchange_" + "_".join(names),
                              collective_id=next(exchange_ids))
        parts.update(zip(names, res))

    grad_x, _, small = _local_step(x2, t2, pvec, wfull, exchange)
    out = {}
    for n in BIG:
        tr = {"w_in": 128, "w_up": 128, "w_b": 128}.get(n, big2[n].shape[0] // 4)
        g, d, nm, nv = _reduce_adamw(parts[n], big2[n], Mo[n][0], Vo[n][0], tr=tr, name="adamw_" + n)
        out[n] = tuple(a[None] for a in (g, d, nm, nv))

    vec, offs = _pack([small[n] for n in SMALL_ORDER])
    tot = _sum_devices(_allgather_small(vec)).reshape(-1)
    off = dict(zip(SMALL_ORDER, offs))
    loss = tot[off["loss"]]
    chip = 2 * lax.axis_index("x") + lax.axis_index("y")
    gs = {}
    for n in VECS:
        gs[n] = lax.slice(tot, (off[n],), (off[n] + W[n].size,)).reshape(W[n].shape)
    for n in CONV:
        width = W[n].shape[2]
        full = lax.slice(tot, (off[n],), (off[n] + 3 * N_CHIPS * width,)).reshape(1, 3, N_CHIPS * width)
        gs[n] = lax.dynamic_slice_in_dim(full, chip * width, width, axis=2)
    names = VECS + CONV
    wp, _ = _pack([W[n] for n in names])
    gp, poffs = _pack([gs[n] for n in names])
    mp, _ = _pack([Mo[n] for n in names])
    vp, _ = _pack([Vo[n] for n in names])
    dl, nm, nv = (a.reshape(-1) for a in _adamw_small(wp, gp, mp, vp))
    for n, o in zip(names, poffs):
        cut = lambda a: lax.slice(a, (o,), (o + W[n].size,)).reshape(W[n].shape)
        out[n] = (gs[n], cut(dl), cut(nm), cut(nv))

    res = [loss, grad_x[None]]
    for k in range(4):
        res += [out[n][k] for n in ORDER]
    return tuple(res)


def kernel(x, ln0_g, ln0_b, w_in, b_in, conv_w, w_a, w_b, w_o, b_o, ln1_g, ln1_b, w_up, b_up, ffn_conv_w, ffn_conv_b, w_down, b_down, ln2_g, ln2_b, loss_target, m_ln0_g, m_ln0_b, m_w_in, m_b_in, m_conv_w, m_w_a, m_w_b, m_w_o, m_b_o, m_ln1_g, m_ln1_b, m_w_up, m_b_up, m_ffn_conv_w, m_ffn_conv_b, m_w_down, m_b_down, m_ln2_g, m_ln2_b, v_ln0_g, v_ln0_b, v_w_in, v_b_in, v_conv_w, v_w_a, v_w_b, v_w_o, v_b_o, v_ln1_g, v_ln1_b, v_w_up, v_b_up, v_ffn_conv_w, v_ffn_conv_b, v_w_down, v_b_down, v_ln2_g, v_ln2_b):
    W = dict(zip(ORDER, (ln0_g, ln0_b, w_in, b_in, conv_w, w_a, w_b, w_o, b_o, ln1_g, ln1_b, w_up, b_up,
                         ffn_conv_w, ffn_conv_b, w_down, b_down, ln2_g, ln2_b)))
    Mo = dict(zip(ORDER, (m_ln0_g, m_ln0_b, m_w_in, m_b_in, m_conv_w, m_w_a, m_w_b, m_w_o, m_b_o, m_ln1_g, m_ln1_b,
                          m_w_up, m_b_up, m_ffn_conv_w, m_ffn_conv_b, m_w_down, m_b_down, m_ln2_g, m_ln2_b)))
    Vo = dict(zip(ORDER, (v_ln0_g, v_ln0_b, v_w_in, v_b_in, v_conv_w, v_w_a, v_w_b, v_w_o, v_b_o, v_ln1_g, v_ln1_b,
                          v_w_up, v_b_up, v_ffn_conv_w, v_ffn_conv_b, v_w_down, v_b_down, v_ln2_g, v_ln2_b)))
    return _step(x, loss_target, W, Mo, Vo)
```

```python
import functools
import math

import jax
import jax.numpy as jnp
from jax import lax
from jax.experimental import pallas as pl
from jax.experimental.pallas import tpu as pltpu
from jax.experimental.pallas import tpu_sc as plsc

F32 = jnp.float32
BF16 = jnp.bfloat16

D_MODEL = 1024
D_CONV = D_MODEL
HEAD_DIM = 64
HEADS_PER_GROUP = 8
GROUPS = ((128, 1), (512, 4), (2048, 16))
N_GROUPS = len(GROUPS)
GROUP_W = HEADS_PER_GROUP * HEAD_DIM
QKV_W = N_GROUPS * GROUP_W
RADIUS = 64
D_FF = 2816
LN_EPS = 1e-5
ALPHA = 2.0 ** 0.25
MASK_VALUE = -1e30
ATT_SCALE = HEAD_DIM ** -0.5
OFF_B = 0
OFF_C = OFF_B + D_CONV
OFF_H = OFF_C + D_CONV
OFF_Q = OFF_H + D_CONV
OFF_K = OFF_Q + QKV_W
OFF_V = OFF_K + QKV_W
OFF_GA = OFF_V + QKV_W
OFF_GB = OFF_GA + D_MODEL
N_IN = OFF_GB + D_MODEL
ADAM_LR = 0.001
ADAM_B1 = 0.9
ADAM_B2 = 0.999
ADAM_EPS = 1e-08
ADAM_WD = 0.01
ADAM_STEP = 10
INV_SQRT2 = 0.7071067811865476
INV_SQRT_2PI = 0.3989422804014327

LANES = 128
SUBLANES = 8
VMEM_BYTES_V7X = 64 * 1024 * 1024
N_CHIPS = 4
N_CORES = 2
N_DEV = N_CHIPS * N_CORES
MESH = pl.DeviceIdType.MESH

N_BLK = N_IN // GROUP_W
PERM = (0, 1, 2, 3, 4, 5, 15, 16, 17, 18, 6, 9, 12, 7, 10, 13, 8, 11, 14)
INV_PERM = tuple(PERM.index(b) for b in range(N_BLK))
P_B, P_C, P_H, P_GA, P_GB, P_Q0 = 0, 1024, 2048, 3072, 4096, 5120
N_NAT = P_Q0 + QKV_W // N_GROUPS * 3
N_GATED = P_Q0

SLAB = 128
CHUNK = 256
PAD = SUBLANES
TQ = 128


def _cparams(sem, vmem_mb):
    assert vmem_mb * 1024 * 1024 < VMEM_BYTES_V7X
    return pltpu.CompilerParams(dimension_semantics=sem, vmem_limit_bytes=vmem_mb * 1024 * 1024)


def _dot(a, b):
    return jnp.dot(a, b, preferred_element_type=F32)


def _dot_nt(a, b):
    return lax.dot_general(a, b, (((1,), (1,)), ((), ())), preferred_element_type=F32)


def _dot_tn(a, b):
    return lax.dot_general(a, b, (((0,), (0,)), ((), ())), preferred_element_type=F32)


def _ln_stats(z):
    mu = jnp.mean(z, -1, keepdims=True)
    zc = z - mu
    var = jnp.mean(zc * zc, -1, keepdims=True)
    rstd = lax.rsqrt(var + LN_EPS)
    return zc * rstd, rstd


def _ln_bwd(dh, xhat, rstd, g):
    dxh = dh * g
    m1 = jnp.mean(dxh, -1, keepdims=True)
    m2 = jnp.mean(dxh * xhat, -1, keepdims=True)
    return rstd * (dxh - m1 - xhat * m2)


def _rows8(rows, width):
    pad = [jnp.zeros((1, width), F32)] * (SUBLANES - len(rows))
    return jnp.concatenate(list(rows) + pad, axis=0)


def _mm_nn(a, w, bias, *, tm, tn, out_dtype, name, vmem_mb=40):
    M, K = a.shape
    if w.ndim == 3:
        assert w.shape[2] == tn
        n_tiles = w.shape[0]
        w_spec = pl.BlockSpec((None, K, tn), lambda i, j: (j, 0, 0))
    else:
        n_tiles = w.shape[1] // tn
        w_spec = pl.BlockSpec((K, tn), lambda i, j: (0, j))

    def body(a_ref, w_ref, b_ref, o_ref):
        o_ref[...] = (_dot(a_ref[...], w_ref[...]) + b_ref[...]).astype(o_ref.dtype)

    return pl.pallas_call(
        body, grid=(M // tm, n_tiles),
        in_specs=[pl.BlockSpec((tm, K), lambda i, j: (i, 0)), w_spec, pl.BlockSpec((1, tn), lambda i, j: (0, j))],
        out_specs=pl.BlockSpec((tm, tn), lambda i, j: (i, j)),
        out_shape=jax.ShapeDtypeStruct((M, n_tiles * tn), out_dtype),
        name=name, compiler_params=_cparams(("parallel", "arbitrary"), vmem_mb))(a, w, bias)


def _mm_nt(a, w, *, tm, a_col=0, name, vmem_mb=40):
    M = a.shape[0]
    N, K = w.shape

    def body(a_ref, w_ref, o_ref):
        o_ref[...] = _dot_nt(a_ref[...], w_ref[...]).astype(o_ref.dtype)

    return pl.pallas_call(
        body, grid=(M // tm,),
        in_specs=[pl.BlockSpec((tm, K), lambda i: (i, a_col)),
                  pl.BlockSpec((N, K), lambda i: (0, 0))],
        out_specs=pl.BlockSpec((tm, N), lambda i: (i, 0)),
        out_shape=jax.ShapeDtypeStruct((M, N), BF16),
        name=name, compiler_params=_cparams(("parallel",), vmem_mb))(a, w)


def _mm_tn(a, g, *, n_out, tn, ts, g_block, g_map, colsum=False, name, vmem_mb=48):
    S, K = a.shape
    n_s = S // ts

    def body(a_ref, g_ref, *rest):
        if colsum:
            o_ref, cs_ref, acc_ref, cacc_ref = rest
        else:
            o_ref, acc_ref = rest
        s = pl.program_id(1)

        @pl.when(s == 0)
        def _():
            acc_ref[...] = jnp.zeros_like(acc_ref)
            if colsum:
                cacc_ref[...] = jnp.zeros_like(cacc_ref)

        gv = g_ref[...]
        acc_ref[...] += _dot_tn(a_ref[...], gv)
        if colsum:
            cacc_ref[...] += jnp.broadcast_to(jnp.sum(gv.astype(F32), axis=0, keepdims=True), cacc_ref.shape)

        @pl.when(s == n_s - 1)
        def _():
            o_ref[...] = acc_ref[...].astype(o_ref.dtype)
            if colsum:
                cs_ref[...] = cacc_ref[...]

    out_specs = [pl.BlockSpec((None, K, tn), lambda j, s: (j, 0, 0))]
    out_shape = [jax.ShapeDtypeStruct((n_out, K, tn), BF16)]
    scratch = [pltpu.VMEM((K, tn), F32)]
    if colsum:
        out_specs.append(pl.BlockSpec((SUBLANES, tn), lambda j, s: (0, j)))
        out_shape.append(jax.ShapeDtypeStruct((SUBLANES, n_out * tn), F32))
        scratch.append(pltpu.VMEM((SUBLANES, tn), F32))
    res = pl.pallas_call(
        body, grid=(n_out, n_s),
        in_specs=[pl.BlockSpec((ts, K), lambda j, s: (s, 0)), pl.BlockSpec(g_block, g_map)],
        out_specs=out_specs, out_shape=out_shape, scratch_shapes=scratch,
        name=name, compiler_params=_cparams(("parallel", "arbitrary"), vmem_mb))(a, g)
    return res if colsum else res[0]


def _mm_tn_cat(a, gs, *, ts, name, vmem_mb=40):
    S, K = a.shape
    widths = [g.shape[1] for g in gs]
    n_s, total = S // ts, sum(widths)

    def body(*refs):
        a_ref, g_refs = refs[0], refs[1:1 + len(gs)]
        o_ref, cs_ref, acc_ref, cacc_ref = refs[1 + len(gs):]
        s = pl.program_id(0)

        @pl.when(s == 0)
        def _():
            acc_ref[...] = jnp.zeros_like(acc_ref)
            cacc_ref[...] = jnp.zeros_like(cacc_ref)

        av, col = a_ref[...], 0
        for g_ref, w in zip(g_refs, widths):
            gv = g_ref[...]
            acc_ref[:, col:col + w] += _dot_tn(av, gv)
            cacc_ref[:, col:col + w] += jnp.broadcast_to(jnp.sum(gv.astype(F32), axis=0, keepdims=True), (SUBLANES, w))
            col += w

        @pl.when(s == n_s - 1)
        def _():
            o_ref[...] = acc_ref[...].astype(BF16)
            cs_ref[...] = cacc_ref[...]

    return pl.pallas_call(
        body, grid=(n_s,),
        in_specs=[pl.BlockSpec((ts, K), lambda s: (s, 0))] + [pl.BlockSpec((ts, w), lambda s: (s, 0)) for w in widths],
        out_specs=[pl.BlockSpec((K, total), lambda s: (0, 0)), pl.BlockSpec((SUBLANES, total), lambda s: (0, 0))],
        out_shape=[jax.ShapeDtypeStruct((K, total), BF16), jax.ShapeDtypeStruct((SUBLANES, total), F32)],
        scratch_shapes=[pltpu.VMEM((K, total), F32), pltpu.VMEM((SUBLANES, total), F32)],
        name=name, compiler_params=_cparams(("arbitrary",), vmem_mb))(a, *gs)


DILS = tuple(d for _, d in GROUPS if d > 1)


def _res_spec(d, tm, width):
    return pl.BlockSpec((d, tm // d, width), lambda i: (0, i, 0))


def _lane_scratch(tm, width):
    return [pltpu.VMEM((tm, LANES), F32)] * (width // LANES)


def _to_residue(val, dst_refs, dils, tm, dtype, scr):
    for c, ref in enumerate(scr):
        ref[...] = val[:, c * LANES:(c + 1) * LANES]
    for dst_ref, d in zip(dst_refs, dils):
        for r in range(d):
            cols = [ref[pl.ds(r, tm // d, stride=d), :] for ref in scr]
            dst_ref[r] = jnp.concatenate(cols, axis=1).astype(dtype)


def _from_residue(rows_of, d, tm, scr):
    for r in range(d):
        v = rows_of(r).astype(F32)
        for c, ref in enumerate(scr):
            ref[pl.ds(r, tm // d, stride=d), :] = v[:, c * LANES:(c + 1) * LANES]
    return jnp.concatenate([ref[...] for ref in scr], axis=1)


def _ln0_fwd(x, g, b, *, tm=512):
    S, Dm = x.shape

    def body(x_ref, g_ref, b_ref, h_ref, hb_ref, *rest):
        xhat, _ = _ln_stats(x_ref[...])
        h = xhat * g_ref[...] + b_ref[...]
        h_ref[...] = h
        hb_ref[...] = h.astype(BF16)
        _to_residue(h, rest[:len(DILS)], DILS, tm, BF16, rest[len(DILS):])

    row = pl.BlockSpec((tm, Dm), lambda i: (i, 0))
    vec = pl.BlockSpec((1, Dm), lambda i: (0, 0))
    return pl.pallas_call(
        body, grid=(S // tm,), in_specs=[row, vec, vec], out_specs=[row, row] + [_res_spec(d, tm, Dm) for d in DILS],
        out_shape=[jax.ShapeDtypeStruct((S, Dm), F32), jax.ShapeDtypeStruct((S, Dm), BF16)]
        + [jax.ShapeDtypeStruct((d, S // d, Dm), BF16) for d in DILS],
        scratch_shapes=_lane_scratch(tm, Dm),
        name="ln0_fwd", compiler_params=_cparams(("parallel",), 32))(x, g, b)


def _slab_spec(S, col0):
    return pl.BlockSpec((S, SLAB), lambda j: (0, col0 // SLAB + j))


def _zero_pads(scr, S):
    scr[0:PAD, :] = jnp.zeros((PAD, SLAB), F32)
    scr[S + PAD:S + 2 * PAD, :] = jnp.zeros((PAD, SLAB), F32)


def _shifted(scr, t):
    return (scr[PAD - 1 + t:PAD - 1 + t + CHUNK, :], scr[PAD + t:PAD + t + CHUNK, :],
            scr[PAD + 1 + t:PAD + 1 + t + CHUNK, :])


def _conv_gate_fwd(proj, conv_w):
    S = proj.shape[0]

    def body(b_ref, c_ref, h_ref, w_ref, o_ref, u_scr):
        _zero_pads(u_scr, S)
        for t in range(0, S, CHUNK):
            u_scr[PAD + t:PAD + t + CHUNK, :] = c_ref[t:t + CHUNK, :].astype(F32) * h_ref[t:t + CHUNK, :].astype(F32)
        w0, w1, w2 = w_ref[0:1, :], w_ref[1:2, :], w_ref[2:3, :]
        for t in range(0, S, CHUNK):
            um, u0, up = _shifted(u_scr, t)
            cv = w0 * um + w1 * u0 + w2 * up
            o_ref[t:t + CHUNK, :] = (b_ref[t:t + CHUNK, :].astype(F32) * cv).astype(BF16)

    return pl.pallas_call(
        body, grid=(D_CONV // SLAB,),
        in_specs=[_slab_spec(S, P_B), _slab_spec(S, P_C), _slab_spec(S, P_H),
                  pl.BlockSpec((3, SLAB), lambda j: (0, j))],
        out_specs=pl.BlockSpec((S, SLAB), lambda j: (0, j)),
        out_shape=jax.ShapeDtypeStruct((S, D_CONV), BF16),
        scratch_shapes=[pltpu.VMEM((S + 2 * PAD, SLAB), F32)],
        name="conv_gate_fwd", compiler_params=_cparams(("parallel",), 40))(proj, proj, proj, conv_w)


def _attn_masks(i, sub, dil):
    a = lax.broadcasted_iota(jnp.int32, (TQ, 2 * TQ), 0)
    j = lax.broadcasted_iota(jnp.int32, (TQ, 2 * TQ), 1)
    rel = jnp.abs(j - RADIUS - a)
    kpos = i * TQ - RADIUS + j
    valid = (rel <= RADIUS) & (kpos >= 0) & (kpos < sub)
    return valid, -(rel * dil).astype(F32)


def _slope(g, h):
    return 2.0 ** (-8.0 * (g * HEADS_PER_GROUP + h + 1) / (N_GROUPS * HEADS_PER_GROUP))


def _window(p_ref, c_ref, n_ref):
    return jnp.concatenate([p_ref[TQ - RADIUS:, :], c_ref[...], n_ref[:RADIUS, :]], axis=0)


def _qkv_specs(nb, col0):
    def spec(col, shift):
        return pl.BlockSpec((None, TQ, GROUP_W), lambda r, i: (r, jnp.clip(i + shift, 0, nb - 1), col))

    return [spec(col0, 0), spec(col0 + 1, -1), spec(col0 + 1, 0), spec(col0 + 1, 1),
            spec(col0 + 2, -1), spec(col0 + 2, 0), spec(col0 + 2, 1)]


def _attn_fwd(qkv, col0, g):
    dil, sub, _ = qkv.shape
    nb = sub // TQ

    def body(q_ref, kp, kc, kn, vp, vc, vn, o_ref, lse_ref):
        valid, base = _attn_masks(pl.program_id(1), sub, dil)
        kwin = _window(kp, kc, kn)
        vwin = _window(vp, vc, vn)
        q = q_ref[...]
        for h in range(HEADS_PER_GROUP):
            sl = slice(h * HEAD_DIM, (h + 1) * HEAD_DIM)
            s = _dot_nt(q[:, sl], kwin[:, sl]) * ATT_SCALE + _slope(g, h) * base
            s = jnp.where(valid, s, MASK_VALUE)
            m = jnp.max(s, -1, keepdims=True)
            p = jnp.exp(s - m)
            den = jnp.sum(p, -1, keepdims=True)
            o_ref[:, sl] = _dot(p.astype(BF16), vwin[:, sl]) / den
            lse_ref[:, sl] = jnp.broadcast_to(m + jnp.log(den), (TQ, HEAD_DIM))

    out = pl.BlockSpec((None, TQ, GROUP_W), lambda r, i: (r, i, 0))
    return pl.pallas_call(
        body, grid=(dil, nb), in_specs=_qkv_specs(nb, col0), out_specs=[out, out],
        out_shape=[jax.ShapeDtypeStruct((dil, sub, GROUP_W), F32)] * 2,
        name=f"attn_fwd_g{g}", compiler_params=_cparams(("parallel", "arbitrary"), 32))(*([qkv] * 7))


def _attn_combine(outs, lses, *, tm=512):
    S = outs[0].shape[1]
    n_col = GROUP_W // LANES

    def body(*refs):
        ins, (c_ref, cb_ref, lt_ref) = refs[:2 * N_GROUPS], refs[2 * N_GROUPS:2 * N_GROUPS + 3]
        scr = refs[2 * N_GROUPS + 3:]
        o, l = [ins[0][0]], [ins[N_GROUPS][0]]
        for k, d in enumerate(DILS):
            o_ref, l_ref = ins[1 + k], ins[N_GROUPS + 1 + k]
            o.append(_from_residue(lambda r: o_ref[r], d, tm, scr[2 * k * n_col:(2 * k + 1) * n_col]))
            l.append(_from_residue(lambda r: l_ref[r], d, tm, scr[(2 * k + 1) * n_col:(2 * k + 2) * n_col]))
        m = jnp.maximum(jnp.maximum(l[0], l[1]), l[2])
        e = [jnp.exp(v - m) for v in l]
        den = e[0] + e[1] + e[2]
        comb = (e[0] * o[0] + e[1] * o[1] + e[2] * o[2]) / den
        c_ref[...] = comb
        cb_ref[...] = comb.astype(BF16)
        lt_ref[...] = m + jnp.log(den)

    row = pl.BlockSpec((tm, GROUP_W), lambda i: (i, 0))
    specs = [_res_spec(GROUPS[g][1], tm, GROUP_W) for g in range(N_GROUPS)]
    return pl.pallas_call(
        body, grid=(S // tm,), in_specs=specs * 2, out_specs=[row] * 3,
        out_shape=[jax.ShapeDtypeStruct((S, GROUP_W), F32), jax.ShapeDtypeStruct((S, GROUP_W), BF16),
                   jax.ShapeDtypeStruct((S, GROUP_W), F32)],
        scratch_shapes=_lane_scratch(tm, GROUP_W) * (2 * len(DILS)),
        name="attn_combine", compiler_params=_cparams(("parallel",), 32))(*outs, *lses)


def _branch_mix(ya_in, comb_b, w_a, w_b, proj, *, tm=512):
    S = ya_in.shape[0]

    def body(ya_ref, cb_ref, wa_ref, wb_ref, ga_ref, gb_ref, yab_ref, mx_ref):
        y_a = _dot(ya_ref[...], wa_ref[...])
        y_b = _dot(cb_ref[...], wb_ref[...])
        yab_ref[:, 0:D_MODEL] = y_a.astype(BF16)
        yab_ref[:, D_MODEL:2 * D_MODEL] = y_b.astype(BF16)
        mx = jax.nn.sigmoid(ga_ref[...].astype(F32)) * y_a + jax.nn.sigmoid(gb_ref[...].astype(F32)) * y_b
        mx_ref[...] = mx.astype(BF16)

    return pl.pallas_call(
        body, grid=(S // tm,),
        in_specs=[pl.BlockSpec((tm, D_CONV), lambda i: (i, 0)), pl.BlockSpec((tm, GROUP_W), lambda i: (i, 0)),
                  pl.BlockSpec((D_CONV, D_MODEL), lambda i: (0, 0)), pl.BlockSpec((GROUP_W, D_MODEL), lambda i: (0, 0)),
                  pl.BlockSpec((tm, D_MODEL), lambda i: (i, P_GA // D_MODEL)),
                  pl.BlockSpec((tm, D_MODEL), lambda i: (i, P_GB // D_MODEL))],
        out_specs=[pl.BlockSpec((tm, 2 * D_MODEL), lambda i: (i, 0)), pl.BlockSpec((tm, D_MODEL), lambda i: (i, 0))],
        out_shape=[jax.ShapeDtypeStruct((S, 2 * D_MODEL), BF16), jax.ShapeDtypeStruct((S, D_MODEL), BF16)],
        name="branch_mix", compiler_params=_cparams(("parallel",), 40))(ya_in, comb_b, w_a, w_b, proj, proj)


def _mix_ln1(mixin, w_o, b_o, h0, g1, b1, *, tm=512):
    S = mixin.shape[0]

    def body(mx_ref, wo_ref, bo_ref, h0_ref, g_ref, b_ref, xh_ref, rs_ref, h1b_ref):
        z = ALPHA * h0_ref[...] + _dot(mx_ref[...], wo_ref[...]) + bo_ref[...]
        xhat, rstd = _ln_stats(z)
        xh_ref[...] = xhat
        rs_ref[...] = jnp.broadcast_to(rstd, (tm, LANES))
        h1b_ref[...] = (xhat * g_ref[...] + b_ref[...]).astype(BF16)

    row = pl.BlockSpec((tm, D_MODEL), lambda i: (i, 0))
    vec = pl.BlockSpec((1, D_MODEL), lambda i: (0, 0))
    return pl.pallas_call(
        body, grid=(S // tm,),
        in_specs=[row, pl.BlockSpec((D_MODEL, D_MODEL), lambda i: (0, 0)), vec, row, vec, vec],
        out_specs=[row, pl.BlockSpec((tm, LANES), lambda i: (i, 0)), row],
        out_shape=[jax.ShapeDtypeStruct((S, D_MODEL), F32), jax.ShapeDtypeStruct((S, LANES), F32),
                   jax.ShapeDtypeStruct((S, D_MODEL), BF16)],
        name="mix_ln1", compiler_params=_cparams(("parallel",), 40))(mixin, w_o, b_o, h0, g1, b1)


def _gelu_parts(cz):
    cdf = 0.5 * (1.0 + lax.erf(cz * INV_SQRT2))
    return cdf, cz * cdf


def _ffn_conv_fwd(up, cw, cb):
    S = up.shape[0]

    def body(a_ref, g_ref, w_ref, cb_ref, o_ref, a_scr):
        _zero_pads(a_scr, S)
        for t in range(0, S, CHUNK):
            a_scr[PAD + t:PAD + t + CHUNK, :] = a_ref[t:t + CHUNK, :].astype(F32)
        w0, w1, w2 = w_ref[0:1, :], w_ref[1:2, :], w_ref[2:3, :]
        for t in range(0, S, CHUNK):
            am, a0, ap = _shifted(a_scr, t)
            _, gel = _gelu_parts(w0 * am + w1 * a0 + w2 * ap + cb_ref[...])
            o_ref[t:t + CHUNK, :] = (gel * g_ref[t:t + CHUNK, :].astype(F32)).astype(BF16)

    return pl.pallas_call(
        body, grid=(D_FF // SLAB,),
        in_specs=[_slab_spec(S, 0), _slab_spec(S, D_FF), pl.BlockSpec((3, SLAB), lambda j: (0, j)),
                  pl.BlockSpec((1, SLAB), lambda j: (0, j))],
        out_specs=pl.BlockSpec((S, SLAB), lambda j: (0, j)),
        out_shape=jax.ShapeDtypeStruct((S, D_FF), BF16),
        scratch_shapes=[pltpu.VMEM((S + 2 * PAD, SLAB), F32)],
        name="ffn_conv_fwd", compiler_params=_cparams(("parallel",), 40))(up, up, cw, cb)


def _down_ln2_loss(f, w_down, b_down, xhat1, g1, b1, g2, b2, target, *, tm=256):
    S = f.shape[0]

    def body(f_ref, wd_ref, bd_ref, xh1_ref, g1_ref, b1_ref, g2_ref, b2_ref, t_ref, dz_ref, dzb_ref, st_ref):
        h1 = xh1_ref[...] * g1_ref[...] + b1_ref[...]
        z = ALPHA * h1 + _dot(f_ref[...], wd_ref[...]) + bd_ref[...]
        xhat, rstd = _ln_stats(z)
        err = xhat * g2_ref[...] + b2_ref[...] - t_ref[...]
        loss = (0.5 / D_MODEL) * jnp.sum(jnp.sum(err * err, axis=1, keepdims=True), axis=0, keepdims=True)
        dh2 = err * (1.0 / D_MODEL)
        dz = _ln_bwd(dh2, xhat, rstd, g2_ref[...])
        dz_ref[...] = dz
        dzb_ref[...] = dz.astype(BF16)
        upd = _rows8([jnp.sum(dh2 * xhat, axis=0, keepdims=True), jnp.sum(dh2, axis=0, keepdims=True),
                      jnp.broadcast_to(loss, (1, D_MODEL)), jnp.sum(dz, axis=0, keepdims=True)], D_MODEL)

        @pl.when(pl.program_id(0) == 0)
        def _():
            st_ref[...] = upd

        @pl.when(pl.program_id(0) != 0)
        def _():
            st_ref[...] += upd

    row = pl.BlockSpec((tm, D_MODEL), lambda i: (i, 0))
    vec = pl.BlockSpec((1, D_MODEL), lambda i: (0, 0))
    return pl.pallas_call(
        body, grid=(S // tm,),
        in_specs=[pl.BlockSpec((tm, D_FF), lambda i: (i, 0)), pl.BlockSpec((D_FF, D_MODEL), lambda i: (0, 0)),
                  vec, row, vec, vec, vec, vec, row],
        out_specs=[row, row, pl.BlockSpec((SUBLANES, D_MODEL), lambda i: (0, 0))],
        out_shape=[jax.ShapeDtypeStruct((S, D_MODEL), F32), jax.ShapeDtypeStruct((S, D_MODEL), BF16),
                   jax.ShapeDtypeStruct((SUBLANES, D_MODEL), F32)],
        name="down_ln2_loss", compiler_params=_cparams(("arbitrary",), 48))(
            f, w_down, b_down, xhat1, g1, b1, g2, b2, target)


def _ffn_conv_bwd(up, df, cw, cb):
    S = up.shape[0]

    def body(a_ref, g_ref, df_ref, w_ref, cb_ref, dup_ref, sm_ref, a_scr, d_scr):
        _zero_pads(a_scr, S)
        _zero_pads(d_scr, S)
        for t in range(0, S, CHUNK):
            a_scr[PAD + t:PAD + t + CHUNK, :] = a_ref[t:t + CHUNK, :].astype(F32)
        w0, w1, w2 = w_ref[0:1, :], w_ref[1:2, :], w_ref[2:3, :]
        zero = jnp.zeros((1, SLAB), F32)
        s_dg, s_dcz, s_w0, s_w1, s_w2 = zero, zero, zero, zero, zero
        for t in range(0, S, CHUNK):
            am, a0, ap = _shifted(a_scr, t)
            cz = w0 * am + w1 * a0 + w2 * ap + cb_ref[...]
            cdf, gel = _gelu_parts(cz)
            dfv = df_ref[t:t + CHUNK, :].astype(F32)
            dgte = dfv * gel
            dcz = dfv * g_ref[t:t + CHUNK, :].astype(F32) * (cdf + cz * jnp.exp(-0.5 * cz * cz) * INV_SQRT_2PI)
            dup_ref[1, t:t + CHUNK, :] = dgte.astype(BF16)
            d_scr[PAD + t:PAD + t + CHUNK, :] = dcz
            s_dg = s_dg + jnp.sum(dgte, axis=0, keepdims=True)
            s_dcz = s_dcz + jnp.sum(dcz, axis=0, keepdims=True)
            s_w0 = s_w0 + jnp.sum(dcz * am, axis=0, keepdims=True)
            s_w1 = s_w1 + jnp.sum(dcz * a0, axis=0, keepdims=True)
            s_w2 = s_w2 + jnp.sum(dcz * ap, axis=0, keepdims=True)
        s_da = zero
        for t in range(0, S, CHUNK):
            dm, d0, dp = _shifted(d_scr, t)
            da = w0 * dp + w1 * d0 + w2 * dm
            dup_ref[0, t:t + CHUNK, :] = da.astype(BF16)
            s_da = s_da + jnp.sum(da, axis=0, keepdims=True)
        sm_ref[...] = _rows8([s_da, s_dg, s_dcz, s_w0, s_w1, s_w2], SLAB)

    return pl.pallas_call(
        body, grid=(D_FF // SLAB,),
        in_specs=[_slab_spec(S, 0), _slab_spec(S, D_FF), pl.BlockSpec((S, SLAB), lambda j: (0, j)),
                  pl.BlockSpec((3, SLAB), lambda j: (0, j)), pl.BlockSpec((1, SLAB), lambda j: (0, j))],
        out_specs=[pl.BlockSpec((2, S, SLAB), lambda j: (0, 0, j)), pl.BlockSpec((SUBLANES, SLAB), lambda j: (0, j))],
        out_shape=[jax.ShapeDtypeStruct((2, S, D_FF), BF16), jax.ShapeDtypeStruct((SUBLANES, D_FF), F32)],
        scratch_shapes=[pltpu.VMEM((S + 2 * PAD, SLAB), F32)] * 2,
        name="ffn_conv_bwd", compiler_params=_cparams(("parallel",), 48))(up, up, df, cw, cb)


def _resident(shape):
    nd = len(shape)
    return pl.BlockSpec(shape, lambda *_: (0,) * nd, pipeline_mode=pl.Buffered(1))


def _up_bwd_ln1(dup, w_up3, dz2, xhat1, rstd1, g1, *, tm=256):
    S = dz2.shape[0]
    ns, _, tk = w_up3.shape
    per_plane = D_FF // tk

    def body(du_ref, w_ref, dz2_ref, xh_ref, rs_ref, g_ref, dz_ref, dzb_ref, st_ref):
        dh = ALPHA * dz2_ref[...]
        for k in range(ns):
            col = (k % per_plane) * tk
            dh = dh + _dot_nt(du_ref[k // per_plane, :, col:col + tk], w_ref[k])
        xhat = xh_ref[...]
        dz = _ln_bwd(dh, xhat, rs_ref[:, 0:1], g_ref[...])
        dz_ref[...] = dz
        dzb_ref[...] = dz.astype(BF16)
        upd = _rows8([jnp.sum(dh * xhat, axis=0, keepdims=True), jnp.sum(dh, axis=0, keepdims=True),
                      jnp.sum(dz, axis=0, keepdims=True)], D_MODEL)

        @pl.when(pl.program_id(0) == 0)
        def _():
            st_ref[...] = upd

        @pl.when(pl.program_id(0) != 0)
        def _():
            st_ref[...] += upd

    row = pl.BlockSpec((tm, D_MODEL), lambda i: (i, 0))
    return pl.pallas_call(
        body, grid=(S // tm,),
        in_specs=[pl.BlockSpec((dup.shape[0], tm, D_FF), lambda i: (0, i, 0)), _resident(w_up3.shape),
                  row, row, pl.BlockSpec((tm, LANES), lambda i: (i, 0)), pl.BlockSpec((1, D_MODEL), lambda i: (0, 0))],
        out_specs=[row, row, pl.BlockSpec((SUBLANES, D_MODEL), lambda i: (0, 0))],
        out_shape=[jax.ShapeDtypeStruct((S, D_MODEL), F32), jax.ShapeDtypeStruct((S, D_MODEL), BF16),
                   jax.ShapeDtypeStruct((SUBLANES, D_MODEL), F32)],
        name="up_bwd_ln1", compiler_params=_cparams(("arbitrary",), 48))(dup, w_up3, dz2, xhat1, rstd1, g1)


def _mix_bwd(dz1b, w_o, proj, yab, *, tm=512):
    S = dz1b.shape[0]

    def body(dz_ref, wo_ref, ga_ref, gb_ref, y_ref, dy_ref, dg_ref):
        dmx = _dot_nt(dz_ref[...], wo_ref[...])
        for k, gt_ref in enumerate((ga_ref, gb_ref)):
            sl = slice(k * D_MODEL, (k + 1) * D_MODEL)
            sg = jax.nn.sigmoid(gt_ref[...].astype(F32))
            dy_ref[:, sl] = (dmx * sg).astype(BF16)
            dg_ref[k] = (dmx * y_ref[:, sl].astype(F32) * sg * (1.0 - sg)).astype(BF16)

    row = pl.BlockSpec((tm, D_MODEL), lambda i: (i, 0))
    wide = pl.BlockSpec((tm, 2 * D_MODEL), lambda i: (i, 0))
    return pl.pallas_call(
        body, grid=(S // tm,),
        in_specs=[row, _resident(w_o.shape), pl.BlockSpec((tm, D_MODEL), lambda i: (i, P_GA // D_MODEL)),
                  pl.BlockSpec((tm, D_MODEL), lambda i: (i, P_GB // D_MODEL)), wide],
        out_specs=[wide, pl.BlockSpec((2, tm, D_MODEL), lambda i: (0, i, 0))],
        out_shape=[jax.ShapeDtypeStruct((S, 2 * D_MODEL), BF16), jax.ShapeDtypeStruct((2, S, D_MODEL), BF16)],
        name="mix_bwd", compiler_params=_cparams(("parallel",), 40))(dz1b, w_o, proj, proj, yab)


def _conv_gate_bwd(proj, dya_in, conv_w):
    S = proj.shape[0]

    def body(b_ref, c_ref, h_ref, dy_ref, w_ref, o_ref, sm_ref, u_scr, d_scr):
        _zero_pads(u_scr, S)
        _zero_pads(d_scr, S)
        for t in range(0, S, CHUNK):
            u_scr[PAD + t:PAD + t + CHUNK, :] = c_ref[t:t + CHUNK, :].astype(F32) * h_ref[t:t + CHUNK, :].astype(F32)
        w0, w1, w2 = w_ref[0:1, :], w_ref[1:2, :], w_ref[2:3, :]
        zero = jnp.zeros((1, SLAB), F32)
        s_w0, s_w1, s_w2 = zero, zero, zero
        for t in range(0, S, CHUNK):
            um, u0, up = _shifted(u_scr, t)
            dy = dy_ref[t:t + CHUNK, :].astype(F32)
            o_ref[0, t:t + CHUNK, :] = (dy * (w0 * um + w1 * u0 + w2 * up)).astype(BF16)
            dcv = dy * b_ref[t:t + CHUNK, :].astype(F32)
            d_scr[PAD + t:PAD + t + CHUNK, :] = dcv
            s_w0 = s_w0 + jnp.sum(dcv * um, axis=0, keepdims=True)
            s_w1 = s_w1 + jnp.sum(dcv * u0, axis=0, keepdims=True)
            s_w2 = s_w2 + jnp.sum(dcv * up, axis=0, keepdims=True)
        for t in range(0, S, CHUNK):
            dm, d0, dp = _shifted(d_scr, t)
            du = w0 * dp + w1 * d0 + w2 * dm
            o_ref[1, t:t + CHUNK, :] = (du * h_ref[t:t + CHUNK, :].astype(F32)).astype(BF16)
            o_ref[2, t:t + CHUNK, :] = (du * c_ref[t:t + CHUNK, :].astype(F32)).astype(BF16)
        sm_ref[...] = _rows8([s_w0, s_w1, s_w2], SLAB)

    return pl.pallas_call(
        body, grid=(D_CONV // SLAB,),
        in_specs=[_slab_spec(S, P_B), _slab_spec(S, P_C), _slab_spec(S, P_H),
                  pl.BlockSpec((S, SLAB), lambda j: (0, j)), pl.BlockSpec((3, SLAB), lambda j: (0, j))],
        out_specs=[pl.BlockSpec((3, S, SLAB), lambda j: (0, 0, j)), pl.BlockSpec((SUBLANES, SLAB), lambda j: (0, j))],
        out_shape=[jax.ShapeDtypeStruct((3, S, D_CONV), BF16), jax.ShapeDtypeStruct((SUBLANES, D_CONV), F32)],
        scratch_shapes=[pltpu.VMEM((S + 2 * PAD, SLAB), F32)] * 2,
        name="conv_gate_bwd", compiler_params=_cparams(("parallel",), 48))(proj, proj, proj, dya_in, conv_w)


def _comb_bwd(dyab, w_b, comb, lse_tot, *, tm=512):
    S = comb.shape[0]

    def body(dy_ref, wb_ref, c_ref, lt_ref, *rest):
        outs, scr = rest[:3 * N_GROUPS], rest[3 * N_GROUPS:]
        dcb = _dot_nt(dy_ref[...], wb_ref[...]).astype(BF16)
        dc = dcb.astype(F32)
        prod = dc * c_ref[...]
        heads = [jnp.broadcast_to(jnp.sum(prod[:, h * HEAD_DIM:(h + 1) * HEAD_DIM], axis=1, keepdims=True),
                                  (tm, HEAD_DIM)) for h in range(HEADS_PER_GROUP)]
        vals = (dc, lt_ref[...], jnp.concatenate(heads, axis=1))
        for k, (val, dtype) in enumerate(zip(vals, (BF16, F32, F32))):
            outs[k][0] = val.astype(dtype)
            _to_residue(val, [outs[3 * (1 + j) + k] for j in range(len(DILS))], DILS, tm, dtype, scr)

    row = pl.BlockSpec((tm, GROUP_W), lambda i: (i, 0))
    out_specs, out_shape = [], []
    for _, d in GROUPS:
        out_specs += [_res_spec(d, tm, GROUP_W)] * 3
        out_shape += [jax.ShapeDtypeStruct((d, S // d, GROUP_W), t) for t in (BF16, F32, F32)]
    res = pl.pallas_call(
        body, grid=(S // tm,),
        in_specs=[pl.BlockSpec((tm, D_MODEL), lambda i: (i, 1)), pl.BlockSpec((GROUP_W, D_MODEL), lambda i: (0, 0)),
                  row, row],
        out_specs=out_specs, out_shape=out_shape, scratch_shapes=_lane_scratch(tm, GROUP_W),
        name="comb_bwd", compiler_params=_cparams(("parallel",), 32))(dyab, w_b, comb, lse_tot)
    return [tuple(res[3 * g:3 * g + 3]) for g in range(N_GROUPS)]


def _attn_bwd(qkv, col0, g, dcomb, lse_tot, delta):
    dil, sub, _ = qkv.shape
    nb = sub // TQ

    def body(q_ref, kp, kc, kn, vp, vc, vn, do_ref, lse_ref, dl_ref, dq_ref, dk_ref, dv_ref, ak, av):
        i = pl.program_id(1)

        @pl.when(i == 0)
        def _():
            ak[...] = jnp.zeros_like(ak)
            av[...] = jnp.zeros_like(av)

        @pl.when(i < nb)
        def _():
            valid, base = _attn_masks(i, sub, dil)
            kwin = _window(kp, kc, kn)
            vwin = _window(vp, vc, vn)
            q = q_ref[...]
            do = do_ref[...]
            for h in range(HEADS_PER_GROUP):
                sl = slice(h * HEAD_DIM, (h + 1) * HEAD_DIM)
                s = _dot_nt(q[:, sl], kwin[:, sl]) * ATT_SCALE + _slope(g, h) * base
                s = jnp.where(valid, s, MASK_VALUE)
                p = jnp.exp(s - lse_ref[:, h * HEAD_DIM:h * HEAD_DIM + 1])
                dp = _dot_nt(do[:, sl], vwin[:, sl])
                ds = (p * (dp - dl_ref[:, h * HEAD_DIM:h * HEAD_DIM + 1])).astype(BF16)
                dq_ref[:, sl] = (_dot(ds, kwin[:, sl]) * ATT_SCALE).astype(BF16)
                ak[RADIUS:RADIUS + 2 * TQ, sl] += _dot_tn(ds, q[:, sl]) * ATT_SCALE
                av[RADIUS:RADIUS + 2 * TQ, sl] += _dot_tn(p.astype(BF16), do[:, sl])

        dk_ref[...] = ak[0:TQ, :].astype(BF16)
        dv_ref[...] = av[0:TQ, :].astype(BF16)
        ak[0:2 * TQ, :] = ak[TQ:3 * TQ, :]
        av[0:2 * TQ, :] = av[TQ:3 * TQ, :]
        ak[2 * TQ:3 * TQ, :] = jnp.zeros((TQ, GROUP_W), F32)
        av[2 * TQ:3 * TQ, :] = jnp.zeros((TQ, GROUP_W), F32)

    tok = pl.BlockSpec((None, TQ, GROUP_W), lambda r, i: (r, jnp.minimum(i, nb - 1), 0))
    dkv_spec = pl.BlockSpec((None, TQ, GROUP_W), lambda r, i: (r, jnp.maximum(i - 1, 0), 0))
    return pl.pallas_call(
        body, grid=(dil, nb + 1), in_specs=_qkv_specs(nb, col0) + [tok, tok, tok],
        out_specs=[tok, dkv_spec, dkv_spec], out_shape=[jax.ShapeDtypeStruct((dil, sub, GROUP_W), BF16)] * 3,
        scratch_shapes=[pltpu.VMEM((3 * TQ, GROUP_W), F32)] * 2,
        name=f"attn_bwd_g{g}", compiler_params=_cparams(("arbitrary", "arbitrary"), 32))(
            *([qkv] * 7), dcomb, lse_tot, delta)


def _in_bwd_ln0(dgated, dqkv, w_nat, w_dil, dz1, x, g0, *, tm=256):
    S = x.shape[0]
    n_gated, n_in = len(dgated), 3 * N_GROUPS

    def body(*refs):
        g_refs, d_refs = refs[:n_gated], refs[n_gated:n_gated + n_in]
        wn_ref, *wd_refs = refs[n_gated + n_in:n_gated + n_in + N_GROUPS]
        dz_ref, x_ref, g_ref, gx_ref, st_ref, *tmp_ref = refs[n_gated + n_in + N_GROUPS:]
        dh = ALPHA * dz_ref[...]
        col = 0
        for ref in g_refs:
            for k in range(ref.shape[0]):
                dh = dh + _dot_nt(ref[k], wn_ref[:, col:col + D_MODEL])
                col += D_MODEL
        for g, (_, d) in enumerate(GROUPS):
            rows = [jnp.concatenate([d_refs[3 * g + k][r] for k in range(3)], axis=1) for r in range(d)]
            w = wn_ref[:, col:col + QKV_W] if d == 1 else wd_refs[g - 1][...]
            res = _dot_nt(jnp.concatenate(rows, axis=0), w)
            if d == 1:
                dh = dh + res
            else:
                n = tm // d
                dh = dh + _from_residue(lambda r: res[r * n:(r + 1) * n, :], d, tm, tmp_ref)
        xhat, rstd = _ln_stats(x_ref[...])
        gx_ref[...] = _ln_bwd(dh, xhat, rstd, g_ref[...])
        upd = _rows8([jnp.sum(dh * xhat, axis=0, keepdims=True), jnp.sum(dh, axis=0, keepdims=True)], D_MODEL)

        @pl.when(pl.program_id(0) == 0)
        def _():
            st_ref[...] = upd

        @pl.when(pl.program_id(0) != 0)
        def _():
            st_ref[...] += upd

    row = pl.BlockSpec((tm, D_MODEL), lambda i: (i, 0))
    g_specs = [pl.BlockSpec((a.shape[0], tm, D_MODEL), lambda i: (0, i, 0)) for a in dgated]
    d_specs = []
    for _, d in GROUPS:
        d_specs += [_res_spec(d, tm, GROUP_W)] * 3
    operands = list(dgated) + [a for grp in dqkv for a in grp] + [w_nat] + list(w_dil) + [dz1, x, g0]
    return pl.pallas_call(
        body, grid=(S // tm,),
        in_specs=g_specs + d_specs + [_resident(w_nat.shape)] + [_resident(w.shape) for w in w_dil]
        + [row, row, pl.BlockSpec((1, D_MODEL), lambda i: (0, 0))],
        out_specs=[row, pl.BlockSpec((SUBLANES, D_MODEL), lambda i: (0, 0))],
        out_shape=[jax.ShapeDtypeStruct((S, D_MODEL), F32), jax.ShapeDtypeStruct((SUBLANES, D_MODEL), F32)],
        scratch_shapes=_lane_scratch(tm, D_MODEL),
        name="in_bwd_ln0", compiler_params=_cparams(("arbitrary",), 52))(*operands)


HBM_SPEC = pl.BlockSpec(memory_space=pltpu.HBM)


def _place():
    x, y, c = lax.axis_index("x"), lax.axis_index("y"), lax.axis_index("c")
    chips = [(1 - x, y), (x, 1 - y), (1 - x, 1 - y)]
    return x, y, c, chips


def _allgather_shards(shards, after, *, name, collective_id):
    n = len(shards)
    per = 6

    def body(*refs):
        ins, outs = refs[:n], refs[n + len(after):2 * n + len(after)]
        send_sems, recv_sems, loc_sems = refs[2 * n + len(after):]
        x, y, c, chips = _place()
        me = 2 * x + y
        sib = (x, y, 1 - c)
        peers = [sib] + [(px, py, c) for px, py in chips]
        barrier = pltpu.get_barrier_semaphore()
        for peer in peers:
            pl.semaphore_signal(barrier, inc=1, device_id=peer, device_id_type=MESH)
        pl.semaphore_wait(barrier, len(peers))

        def rcopy(w, k, src, dst, to):
            return pltpu.make_async_remote_copy(src_ref=src, dst_ref=dst, send_sem=send_sems.at[per * w + k],
                                                recv_sem=recv_sems.at[per * w + k], device_id=to, device_id_type=MESH)

        split = [s.shape[0] == N_CORES for s in shards]
        half = lambda w: c if split[w] else 0
        local, sends = [], []
        for w in range(n):
            cp = pltpu.make_async_copy(ins[w], outs[w].at[me], loc_sems.at[w])
            cp.start()
            local.append(cp)
            for j, (px, py) in enumerate(chips):
                cp = rcopy(w, j, ins[w].at[half(w)], outs[w].at[me, half(w)], (px, py, c))
                cp.start()
                sends.append(cp)
        for w in range(n):
            for j, (px, py) in enumerate(chips):
                slot = outs[w].at[2 * px + py, half(w)]
                rcopy(w, j, slot, slot, (px, py, c)).wait_recv()
                if split[w]:
                    cp = rcopy(w, 3 + j, slot, slot, sib)
                    cp.start()
                    sends.append(cp)
        for w in range(n):
            if split[w]:
                for j, (px, py) in enumerate(chips):
                    slot = outs[w].at[2 * px + py, 1 - c]
                    rcopy(w, 3 + j, slot, slot, sib).wait_recv()
        for cp in sends:
            cp.wait_send()
        for cp in local:
            cp.wait()

    return pl.kernel(
        body, out_type=[jax.ShapeDtypeStruct((N_CHIPS,) + s.shape, s.dtype) for s in shards],
        mesh=plsc.ScalarSubcoreMesh(axis_name="sequencer", num_cores=1),
        scratch_types=[pltpu.SemaphoreType.DMA((per * n,)), pltpu.SemaphoreType.DMA((per * n,)),
                       pltpu.SemaphoreType.DMA((n,))],
        name=name, compiler_params=pltpu.CompilerParams(collective_id=collective_id))(*shards, *after)


def _exchange_grads(grads, *, name, collective_id):
    n = len(grads)
    per = 7

    def body(*refs):
        ins, outs = refs[:n], refs[n:2 * n]
        send_sems, recv_sems, loc_sems = refs[2 * n:]
        x, y, c, chips = _place()
        me = 2 * x + y
        sib = (x, y, 1 - c)
        peers = [sib] + [(px, py, c) for px, py in chips]
        barrier = pltpu.get_barrier_semaphore()
        for peer in peers:
            pl.semaphore_signal(barrier, inc=1, device_id=peer, device_id_type=MESH)
        pl.semaphore_wait(barrier, len(peers))

        def rcopy(w, k, src, dst, to):
            return pltpu.make_async_remote_copy(src_ref=src, dst_ref=dst, send_sem=send_sems.at[per * w + k],
                                                recv_sem=recv_sems.at[per * w + k], device_id=to, device_id_type=MESH)

        local, sends = [], []
        for w in range(n):
            cp = pltpu.make_async_copy(ins[w].at[me], outs[w].at[c, me], loc_sems.at[w])
            cp.start()
            local.append(cp)
            cp = rcopy(w, 0, ins[w].at[me], outs[w].at[c, me], sib)
            cp.start()
            sends.append(cp)
            for j, (px, py) in enumerate(chips):
                cp = rcopy(w, 1 + j, ins[w].at[2 * px + py], outs[w].at[c, me], (px, py, c))
                cp.start()
                sends.append(cp)
        for w in range(n):
            for j, (px, py) in enumerate(chips):
                slot = outs[w].at[c, 2 * px + py]
                rcopy(w, 1 + j, slot, slot, (px, py, c)).wait_recv()
                cp = rcopy(w, 4 + j, slot, slot, sib)
                cp.start()
                sends.append(cp)
        for w in range(n):
            slot = outs[w].at[1 - c, me]
            rcopy(w, 0, slot, slot, sib).wait_recv()
            for j, (px, py) in enumerate(chips):
                slot = outs[w].at[1 - c, 2 * px + py]
                rcopy(w, 4 + j, slot, slot, sib).wait_recv()
        for cp in sends:
            cp.wait_send()
        for cp in local:
            cp.wait()

    return pl.kernel(
        body, out_type=[jax.ShapeDtypeStruct((N_CORES,) + g.shape, g.dtype) for g in grads],
        mesh=plsc.ScalarSubcoreMesh(axis_name="sequencer", num_cores=1),
        scratch_types=[pltpu.SemaphoreType.DMA((per * n,)), pltpu.SemaphoreType.DMA((per * n,)),
                       pltpu.SemaphoreType.DMA((n,))],
        name=name, compiler_params=pltpu.CompilerParams(collective_id=collective_id))(*grads)


def _allgather_small(vec):
    def body(v_ref, o_ref, send_sems, recv_sems, loc_sem):
        x, y, c = lax.axis_index("x"), lax.axis_index("y"), lax.axis_index("c")
        me = 4 * x + 2 * y + c

        def peer(k):
            flip = lambda v, bit: 1 - v if (k >> bit) & 1 else v
            return flip(x, 2), flip(y, 1), flip(c, 0)

        loc = pltpu.make_async_copy(v_ref, o_ref.at[me], loc_sem)
        loc.start()
        sends = []
        for k in range(1, N_DEV):
            cp = pltpu.make_async_remote_copy(src_ref=v_ref, dst_ref=o_ref.at[me], send_sem=send_sems.at[k - 1],
                                              recv_sem=recv_sems.at[k - 1], device_id=peer(k), device_id_type=MESH)
            cp.start()
            sends.append(cp)
        for k in range(1, N_DEV):
            px, py, pc = peer(k)
            pltpu.make_async_remote_copy(src_ref=v_ref, dst_ref=o_ref.at[4 * px + 2 * py + pc],
                                         send_sem=send_sems.at[k - 1], recv_sem=recv_sems.at[k - 1],
                                         device_id=(px, py, pc), device_id_type=MESH).wait_recv()
        for cp in sends:
            cp.wait_send()
        loc.wait()

    return pl.pallas_call(
        body, in_specs=[HBM_SPEC], out_specs=HBM_SPEC,
        out_shape=jax.ShapeDtypeStruct((N_DEV,) + vec.shape, vec.dtype),
        scratch_shapes=[pltpu.SemaphoreType.DMA((N_DEV - 1,)), pltpu.SemaphoreType.DMA((N_DEV - 1,)),
                        pltpu.SemaphoreType.DMA],
        name="allgather_small")(vec)


def _adamw(w, g, m, v):
    m = ADAM_B1 * m + (1.0 - ADAM_B1) * g
    v = ADAM_B2 * v + (1.0 - ADAM_B2) * (g * g)
    m_hat = m / (1.0 - ADAM_B1 ** ADAM_STEP)
    v_hat = v / (1.0 - ADAM_B2 ** ADAM_STEP)
    delta = -ADAM_LR * (m_hat / (jnp.sqrt(v_hat) + ADAM_EPS) + ADAM_WD * w)
    return delta, m, v


def _reduce_adamw(parts, w, m, v, *, tr, name):
    R, C = w.shape

    def body(p_ref, w_ref, m_ref, v_ref, g_ref, d_ref, nm_ref, nv_ref):
        def core_sum(cc):
            s = p_ref[cc, 0].astype(F32)
            for k in range(1, N_CHIPS):
                s = s + p_ref[cc, k].astype(F32)
            return s

        g = core_sum(0) + core_sum(1)
        delta, nm, nv = _adamw(w_ref[...], g, m_ref[...], v_ref[...])
        g_ref[...] = g
        d_ref[...] = delta
        nm_ref[...] = nm
        nv_ref[...] = nv

    blk = pl.BlockSpec((tr, C), lambda i: (i, 0))
    return pl.pallas_call(
        body, grid=(R // tr,),
        in_specs=[pl.BlockSpec((N_CORES, N_CHIPS, tr, C), lambda i: (0, 0, i, 0)), blk, blk, blk],
        out_specs=[blk] * 4, out_shape=[jax.ShapeDtypeStruct((R, C), F32)] * 4,
        name=name, compiler_params=_cparams(("parallel",), 40))(parts, w, m, v)


def _sum_devices(allv):
    _, R, _ = allv.shape

    def body(a_ref, o_ref):
        s = a_ref[0]
        for d in range(1, N_DEV):
            s = s + a_ref[d]
        o_ref[...] = s

    return pl.pallas_call(body, out_shape=jax.ShapeDtypeStruct((R, LANES), F32), name="sum_small")(allv)


def _adamw_small(w, g, m, v):
    def body(w_ref, g_ref, m_ref, v_ref, d_ref, nm_ref, nv_ref):
        delta, nm, nv = _adamw(w_ref[...], g_ref[...], m_ref[...], v_ref[...])
        d_ref[...] = delta
        nm_ref[...] = nm
        nv_ref[...] = nv

    return pl.pallas_call(body, out_shape=[jax.ShapeDtypeStruct(w.shape, F32)] * 3, name="adamw_small")(w, g, m, v)


def _pack(pieces):
    flat = [p.reshape(-1) for p in pieces]
    offs, n = [], 0
    for f in flat:
        offs.append(n)
        n += f.shape[0]
    total = -(-n // (SUBLANES * LANES)) * SUBLANES * LANES
    flat.append(jnp.zeros((total - n,), F32))
    return jnp.concatenate(flat).reshape(total // LANES, LANES), offs


def _local_step(x, target, p, wfull, on_ready=lambda group: None):
    S = x.shape[0]
    w_in3, w_up3 = wfull["w_in"], wfull["w_up"]
    w_a, w_o, w_down, w_b = wfull["w_a"], wfull["w_o"], wfull["w_down"], wfull["w_b"]
    conv_w, ffn_conv_w = wfull["conv_w"], wfull["ffn_conv_w"]
    dils = [d for _, d in GROUPS]

    h0, h0b, *h0_res = _ln0_fwd(x, p["ln0_g"], p["ln0_b"])
    h0_rows = [h0b] + [h.reshape(S, D_MODEL) for h in h0_res]

    w_blocks = w_in3.transpose(1, 0, 2).reshape(D_MODEL, N_BLK, GROUP_W)
    w_perm = jnp.concatenate([w_blocks[:, b] for b in PERM], axis=1)
    b_blocks = p["b_in"].reshape(N_BLK, GROUP_W)
    b_perm = jnp.concatenate([b_blocks[b] for b in PERM]).reshape(1, N_IN)
    w_nat, b_nat = w_perm[:, :N_NAT], b_perm[:, :N_NAT]
    qkv_cols = [slice(P_Q0 + g * QKV_W, P_Q0 + (g + 1) * QKV_W) for g in range(N_GROUPS)]
    w_qkv = [w_perm[:, c] for c in qkv_cols]

    proj = _mm_nn(h0b, w_nat, b_nat, tm=512, tn=N_NAT // 2, out_dtype=BF16, name="proj")
    qkv = [proj[None]]
    for g in range(1, N_GROUPS):
        t = _mm_nn(h0_rows[g], w_qkv[g], b_perm[:, qkv_cols[g]], tm=512, tn=QKV_W, out_dtype=BF16, name=f"proj_qkv{g}")
        qkv.append(t.reshape(dils[g], S // dils[g], QKV_W))
    col0 = [P_Q0 // GROUP_W] + [0] * (N_GROUPS - 1)
    ya_in = _conv_gate_fwd(proj, conv_w)
    att = [_attn_fwd(qkv[g], col0[g], g) for g in range(N_GROUPS)]
    comb, comb_b, lse_tot = _attn_combine([a[0] for a in att], [a[1] for a in att])
    yab, mixin = _branch_mix(ya_in, comb_b, w_a, w_b, proj)
    xhat1, rstd1, h1b = _mix_ln1(mixin, w_o, p["b_o"], h0, p["ln1_g"], p["ln1_b"])
    up = _mm_nn(h1b, w_up3, p["b_up"], tm=512, tn=w_up3.shape[2], out_dtype=BF16, name="up")
    f = _ffn_conv_fwd(up, ffn_conv_w, p["ffn_conv_b"])
    dz2, dz2b, st2 = _down_ln2_loss(f, w_down, p["b_down"], xhat1, p["ln1_g"], p["ln1_b"],
                                    p["ln2_g"], p["ln2_b"], target)

    gw = {}
    gw["w_down"] = _mm_tn(f, dz2b, n_out=1, tn=D_MODEL, ts=1024, g_block=(1024, D_MODEL),
                          g_map=lambda j, s: (s, 0), name="grad_w_down").reshape(N_CHIPS, D_FF // N_CHIPS, D_MODEL)
    df = _mm_nt(dz2b, w_down, tm=512, name="df")
    dup, sm_ffn = _ffn_conv_bwd(up, df, ffn_conv_w, p["ffn_conv_b"])
    up_tn = w_up3.shape[2]
    up_pp = D_FF // up_tn
    gw["w_up"] = _mm_tn(h1b, dup, n_out=N_CHIPS, tn=up_tn, ts=1024, g_block=(None, 1024, up_tn),
                        g_map=lambda j, s: (j // up_pp, s, j % up_pp), name="grad_w_up")
    on_ready({n: gw[n] for n in ("w_down", "w_up")})
    dz1, dz1b, st1 = _up_bwd_ln1(dup, w_up3, dz2, xhat1, rstd1, p["ln1_g"])

    gw["w_o"] = _mm_tn(mixin, dz1b, n_out=1, tn=D_MODEL, ts=512, g_block=(512, D_MODEL),
                       g_map=lambda j, s: (s, 0), name="grad_w_o").reshape(N_CHIPS, D_MODEL // N_CHIPS, D_MODEL)
    dyab, dgab = _mix_bwd(dz1b, w_o, proj, yab)
    gw["w_a"] = _mm_tn(ya_in, dyab, n_out=1, tn=D_MODEL, ts=512, g_block=(512, D_MODEL),
                       g_map=lambda j, s: (s, 0), name="grad_w_a").reshape(N_CHIPS, D_CONV // N_CHIPS, D_MODEL)
    gw_b = _mm_tn(comb_b, dyab, n_out=1, tn=D_MODEL, ts=1024, g_block=(1024, D_MODEL),
                  g_map=lambda j, s: (s, 1), name="grad_w_b")
    gw["w_b"] = gw_b.reshape(GROUP_W, N_CHIPS, D_MODEL // N_CHIPS).transpose(1, 0, 2)
    on_ready({n: gw[n] for n in ("w_o", "w_a", "w_b")})
    dya_in = _mm_nt(dyab, w_a, tm=512, a_col=0, name="dya_in")
    dbch, sm_conv = _conv_gate_bwd(proj, dya_in, conv_w)
    att_stats = _comb_bwd(dyab, w_b, comb, lse_tot)
    dqkv = [_attn_bwd(qkv[g], col0[g], g, *att_stats[g]) for g in range(N_GROUPS)]

    w_pieces, b_pieces = [], []
    for nm, planes in (("bch", dbch), ("gab", dgab)):
        pw, pc = _mm_tn(h0b, planes, n_out=planes.shape[0], tn=D_MODEL, ts=1024, g_block=(None, 1024, D_MODEL),
                        g_map=lambda j, s: (j, s, 0), colsum=True, name="grad_w_in_" + nm)
        w_pieces.append(pw.transpose(1, 0, 2).reshape(D_MODEL, planes.shape[0] * D_MODEL))
        b_pieces.append(pc[0])
    for g in range(N_GROUPS):
        pw, pc = _mm_tn_cat(h0_rows[g], [a.reshape(S, GROUP_W) for a in dqkv[g]], ts=1024, name=f"grad_w_in_qkv{g}")
        w_pieces.append(pw)
        b_pieces.append(pc[0])
    dw_blocks = jnp.concatenate(w_pieces, axis=1).reshape(D_MODEL, N_BLK, GROUP_W)
    dw_ref = jnp.concatenate([dw_blocks[:, b] for b in INV_PERM], axis=1)
    gw["w_in"] = dw_ref.reshape(D_MODEL, N_CHIPS, N_IN // N_CHIPS).transpose(1, 0, 2)
    on_ready({"w_in": gw["w_in"]})
    db_blocks = jnp.concatenate(b_pieces).reshape(N_BLK, GROUP_W)
    grad_b_in = jnp.concatenate([db_blocks[b] for b in INV_PERM])

    grad_x, st0 = _in_bwd_ln0([dbch, dgab], dqkv, w_nat, w_qkv[1:], dz1, x, p["ln0_g"])

    small = {
        "loss": st2[2:3, 0:1],
        "ln0_g": st0[0], "ln0_b": st0[1], "b_in": grad_b_in, "conv_w": sm_conv[0:3],
        "b_o": st1[2], "ln1_g": st1[0], "ln1_b": st1[1],
        "b_up": jnp.concatenate([sm_ffn[0], sm_ffn[1]]), "ffn_conv_w": sm_ffn[3:6], "ffn_conv_b": sm_ffn[2],
        "b_down": st2[3], "ln2_g": st2[0], "ln2_b": st2[1],
    }
    return grad_x, gw, small


BIG = ("w_in", "w_a", "w_b", "w_o", "w_up", "w_down")
CONV = ("conv_w", "ffn_conv_w")
VECS = ("ln0_g", "ln0_b", "b_in", "b_o", "ln1_g", "ln1_b", "b_up", "ffn_conv_b", "b_down", "ln2_g", "ln2_b")
ORDER = ("ln0_g", "ln0_b", "w_in", "b_in", "conv_w", "w_a", "w_b", "w_o", "b_o", "ln1_g", "ln1_b", "w_up", "b_up",
         "ffn_conv_w", "ffn_conv_b", "w_down", "b_down", "ln2_g", "ln2_b")
SMALL_ORDER = ("loss",) + VECS + CONV


def _step(x, target, W, Mo, Vo):
    x2, t2 = x[0], target[0]
    big2 = {n: W[n][0] for n in BIG}
    halves = lambda a: a.astype(BF16).reshape(N_CORES, a.shape[0] // N_CORES, a.shape[1])
    whole = lambda g: g.reshape(N_CHIPS, g.shape[1] * g.shape[2], g.shape[3])
    later = tuple(n for n in BIG if n != "w_in")
    first = _allgather_shards([halves(big2["w_in"])], [], name="allgather_w_in", collective_id=1)
    rest = _allgather_shards([halves(big2[n]) for n in later] + [W[n] for n in CONV], first,
                             name="allgather_rest", collective_id=2)
    gathered = {n: whole(g) for n, g in zip(("w_in",) + later + CONV, first + rest)}
    wfull = {
        "w_in": gathered["w_in"], "w_up": gathered["w_up"],
        "w_a": gathered["w_a"].reshape(D_CONV, D_MODEL), "w_o": gathered["w_o"].reshape(D_MODEL, D_MODEL),
        "w_down": gathered["w_down"].reshape(D_FF, D_MODEL),
        "w_b": gathered["w_b"].transpose(1, 0, 2).reshape(GROUP_W, D_MODEL),
        "conv_w": gathered["conv_w"].transpose(1, 0, 2).reshape(3, D_CONV),
        "ffn_conv_w": gathered["ffn_conv_w"].transpose(1, 0, 2).reshape(3, D_FF),
    }
    pvec = {n: W[n].reshape(1, -1) for n in VECS}

    parts = {}
    exchange_ids = iter((3, 4, 5))

    def exchange(group):
        names = tuple(group)
        res = _exchange_grads([group[n] for n in names], name="exchange_" + "_".join(names),
                              collective_id=next(exchange_ids))
        parts.update(zip(names, res))

    grad_x, _, small = _local_step(x2, t2, pvec, wfull, exchange)
    out = {}
    for n in BIG:
        tr = {"w_in": 128, "w_up": 128, "w_b": 128}.get(n, big2[n].shape[0] // 4)
        g, d, nm, nv = _reduce_adamw(parts[n], big2[n], Mo[n][0], Vo[n][0], tr=tr, name="adamw_" + n)
        out[n] = tuple(a[None] for a in (g, d, nm, nv))

    vec, offs = _pack([small[n] for n in SMALL_ORDER])
    tot = _sum_devices(_allgather_small(vec)).reshape(-1)
    off = dict(zip(SMALL_ORDER, offs))
    loss = tot[off["loss"]]
    chip = 2 * lax.axis_index("x") + lax.axis_index("y")
    gs = {}
    for n in VECS:
        gs[n] = lax.slice(tot, (off[n],), (off[n] + W[n].size,)).reshape(W[n].shape)
    for n in CONV:
        width = W[n].shape[2]
        full = lax.slice(tot, (off[n],), (off[n] + 3 * N_CHIPS * width,)).reshape(1, 3, N_CHIPS * width)
        gs[n] = lax.dynamic_slice_in_dim(full, chip * width, width, axis=2)
    names = VECS + CONV
    wp, _ = _pack([W[n] for n in names])
    gp, poffs = _pack([gs[n] for n in names])
    mp, _ = _pack([Mo[n] for n in names])
    vp, _ = _pack([Vo[n] for n in names])
    dl, nm, nv = (a.reshape(-1) for a in _adamw_small(wp, gp, mp, vp))
    for n, o in zip(names, poffs):
        cut = lambda a: lax.slice(a, (o,), (o + W[n].size,)).reshape(W[n].shape)
        out[n] = (gs[n], cut(dl), cut(nm), cut(nv))

    res = [loss, grad_x[None]]
    for k in range(4):
        res += [out[n][k] for n in ORDER]
    return tuple(res)


def kernel(x, ln0_g, ln0_b, w_in, b_in, conv_w, w_a, w_b, w_o, b_o, ln1_g, ln1_b, w_up, b_up, ffn_conv_w, ffn_conv_b, w_down, b_down, ln2_g, ln2_b, loss_target, m_ln0_g, m_ln0_b, m_w_in, m_b_in, m_conv_w, m_w_a, m_w_b, m_w_o, m_b_o, m_ln1_g, m_ln1_b, m_w_up, m_b_up, m_ffn_conv_w, m_ffn_conv_b, m_w_down, m_b_down, m_ln2_g, m_ln2_b, v_ln0_g, v_ln0_b, v_w_in, v_b_in, v_conv_w, v_w_a, v_w_b, v_w_o, v_b_o, v_ln1_g, v_ln1_b, v_w_up, v_b_up, v_ffn_conv_w, v_ffn_conv_b, v_w_down, v_b_down, v_ln2_g, v_ln2_b):
    W = dict(zip(ORDER, (ln0_g, ln0_b, w_in, b_in, conv_w, w_a, w_b, w_o, b_o, ln1_g, ln1_b, w_up, b_up,
                         ffn_conv_w, ffn_conv_b, w_down, b_down, ln2_g, ln2_b)))
    Mo = dict(zip(ORDER, (m_ln0_g, m_ln0_b, m_w_in, m_b_in, m_conv_w, m_w_a, m_w_b, m_w_o, m_b_o, m_ln1_g, m_ln1_b,
                          m_w_up, m_b_up, m_ffn_conv_w, m_ffn_conv_b, m_w_down, m_b_down, m_ln2_g, m_ln2_b)))
    Vo = dict(zip(ORDER, (v_ln0_g, v_ln0_b, v_w_in, v_b_in, v_conv_w, v_w_a, v_w_b, v_w_o, v_b_o, v_ln1_g, v_ln1_b,
                          v_w_up, v_b_up, v_ffn_conv_w, v_ffn_conv_b, v_w_down, v_b_down, v_ln2_g, v_ln2_b)))
    return _step(x, loss_target, W, Mo, Vo)
```

```python
import functools
import math

import jax
import jax.numpy as jnp
from jax import lax
from jax.experimental import pallas as pl
from jax.experimental.pallas import tpu as pltpu
from jax.experimental.pallas import tpu_sc as plsc

F32 = jnp.float32
BF16 = jnp.bfloat16

D_MODEL = 1024
D_CONV = D_MODEL
HEAD_DIM = 64
HEADS_PER_GROUP = 8
GROUPS = ((128, 1), (512, 4), (2048, 16))
N_GROUPS = len(GROUPS)
GROUP_W = HEADS_PER_GROUP * HEAD_DIM
QKV_W = N_GROUPS * GROUP_W
RADIUS = 64
D_FF = 2816
LN_EPS = 1e-5
ALPHA = 2.0 ** 0.25
MASK_VALUE = -1e30
ATT_SCALE = HEAD_DIM ** -0.5
OFF_B = 0
OFF_C = OFF_B + D_CONV
OFF_H = OFF_C + D_CONV
OFF_Q = OFF_H + D_CONV
OFF_K = OFF_Q + QKV_W
OFF_V = OFF_K + QKV_W
OFF_GA = OFF_V + QKV_W
OFF_GB = OFF_GA + D_MODEL
N_IN = OFF_GB + D_MODEL
ADAM_LR = 0.001
ADAM_B1 = 0.9
ADAM_B2 = 0.999
ADAM_EPS = 1e-08
ADAM_WD = 0.01
ADAM_STEP = 10
INV_SQRT2 = 0.7071067811865476
INV_SQRT_2PI = 0.3989422804014327

LANES = 128
SUBLANES = 8
VMEM_BYTES_V7X = 64 * 1024 * 1024
N_CHIPS = 4
N_CORES = 2
N_DEV = N_CHIPS * N_CORES
MESH = pl.DeviceIdType.MESH

N_BLK = N_IN // GROUP_W
PERM = (0, 1, 2, 3, 4, 5, 15, 16, 17, 18, 6, 9, 12, 7, 10, 13, 8, 11, 14)
INV_PERM = tuple(PERM.index(b) for b in range(N_BLK))
P_B, P_C, P_H, P_GA, P_GB, P_Q0 = 0, 1024, 2048, 3072, 4096, 5120
N_NAT = P_Q0 + QKV_W // N_GROUPS * 3
N_GATED = P_Q0

SLAB = 128
CHUNK = 256
PAD = SUBLANES
TQ = 128


def _cparams(sem, vmem_mb):
    assert vmem_mb * 1024 * 1024 < VMEM_BYTES_V7X
    return pltpu.CompilerParams(dimension_semantics=sem, vmem_limit_bytes=vmem_mb * 1024 * 1024)


def _dot(a, b):
    return jnp.dot(a, b, preferred_element_type=F32)


def _dot_nt(a, b):
    return lax.dot_general(a, b, (((1,), (1,)), ((), ())), preferred_element_type=F32)


def _dot_tn(a, b):
    return lax.dot_general(a, b, (((0,), (0,)), ((), ())), preferred_element_type=F32)


def _ln_stats(z):
    mu = jnp.mean(z, -1, keepdims=True)
    zc = z - mu
    var = jnp.mean(zc * zc, -1, keepdims=True)
    rstd = lax.rsqrt(var + LN_EPS)
    return zc * rstd, rstd


def _ln_bwd(dh, xhat, rstd, g):
    dxh = dh * g
    m1 = jnp.mean(dxh, -1, keepdims=True)
    m2 = jnp.mean(dxh * xhat, -1, keepdims=True)
    return rstd * (dxh - m1 - xhat * m2)


def _rows8(rows, width):
    pad = [jnp.zeros((1, width), F32)] * (SUBLANES - len(rows))
    return jnp.concatenate(list(rows) + pad, axis=0)


def _mm_nn(a, w, bias, *, tm, tn, out_dtype, name, vmem_mb=40):
    M, K = a.shape
    if w.ndim == 3:
        assert w.shape[2] == tn
        n_tiles = w.shape[0]
        w_spec = pl.BlockSpec((None, K, tn), lambda i, j: (j, 0, 0))
    else:
        n_tiles = w.shape[1] // tn
        w_spec = pl.BlockSpec((K, tn), lambda i, j: (0, j))

    def body(a_ref, w_ref, b_ref, o_ref):
        o_ref[...] = (_dot(a_ref[...], w_ref[...]) + b_ref[...]).astype(o_ref.dtype)

    return pl.pallas_call(
        body, grid=(M // tm, n_tiles),
        in_specs=[pl.BlockSpec((tm, K), lambda i, j: (i, 0)), w_spec, pl.BlockSpec((1, tn), lambda i, j: (0, j))],
        out_specs=pl.BlockSpec((tm, tn), lambda i, j: (i, j)),
        out_shape=jax.ShapeDtypeStruct((M, n_tiles * tn), out_dtype),
        name=name, compiler_params=_cparams(("parallel", "arbitrary"), vmem_mb))(a, w, bias)


def _mm_nt(a, w, *, tm, a_col=0, name, vmem_mb=40):
    M = a.shape[0]
    N, K = w.shape

    def body(a_ref, w_ref, o_ref):
        o_ref[...] = _dot_nt(a_ref[...], w_ref[...]).astype(o_ref.dtype)

    return pl.pallas_call(
        body, grid=(M // tm,),
        in_specs=[pl.BlockSpec((tm, K), lambda i: (i, a_col)),
                  pl.BlockSpec((N, K), lambda i: (0, 0))],
        out_specs=pl.BlockSpec((tm, N), lambda i: (i, 0)),
        out_shape=jax.ShapeDtypeStruct((M, N), BF16),
        name=name, compiler_params=_cparams(("parallel",), vmem_mb))(a, w)


def _mm_tn(a, g, *, n_out, tn, ts, g_block, g_map, colsum=False, name, vmem_mb=48):
    S, K = a.shape
    n_s = S // ts

    def body(a_ref, g_ref, *rest):
        if colsum:
            o_ref, cs_ref, acc_ref, cacc_ref = rest
        else:
            o_ref, acc_ref = rest
        s = pl.program_id(1)

        @pl.when(s == 0)
        def _():
            acc_ref[...] = jnp.zeros_like(acc_ref)
            if colsum:
                cacc_ref[...] = jnp.zeros_like(cacc_ref)

        gv = g_ref[...]
        acc_ref[...] += _dot_tn(a_ref[...], gv)
        if colsum:
            cacc_ref[...] += jnp.broadcast_to(jnp.sum(gv.astype(F32), axis=0, keepdims=True), cacc_ref.shape)

        @pl.when(s == n_s - 1)
        def _():
            o_ref[...] = acc_ref[...].astype(o_ref.dtype)
            if colsum:
                cs_ref[...] = cacc_ref[...]

    out_specs = [pl.BlockSpec((None, K, tn), lambda j, s: (j, 0, 0))]
    out_shape = [jax.ShapeDtypeStruct((n_out, K, tn), BF16)]
    scratch = [pltpu.VMEM((K, tn), F32)]
    if colsum:
        out_specs.append(pl.BlockSpec((SUBLANES, tn), lambda j, s: (0, j)))
        out_shape.append(jax.ShapeDtypeStruct((SUBLANES, n_out * tn), F32))
        scratch.append(pltpu.VMEM((SUBLANES, tn), F32))
    res = pl.pallas_call(
        body, grid=(n_out, n_s),
        in_specs=[pl.BlockSpec((ts, K), lambda j, s: (s, 0)), pl.BlockSpec(g_block, g_map)],
        out_specs=out_specs, out_shape=out_shape, scratch_shapes=scratch,
        name=name, compiler_params=_cparams(("parallel", "arbitrary"), vmem_mb))(a, g)
    return res if colsum else res[0]


def _mm_tn_cat(a, gs, *, ts, name, vmem_mb=40):
    S, K = a.shape
    widths = [g.shape[1] for g in gs]
    n_s, total = S // ts, sum(widths)

    def body(*refs):
        a_ref, g_refs = refs[0], refs[1:1 + len(gs)]
        o_ref, cs_ref, acc_ref, cacc_ref = refs[1 + len(gs):]
        s = pl.program_id(0)

        @pl.when(s == 0)
        def _():
            acc_ref[...] = jnp.zeros_like(acc_ref)
            cacc_ref[...] = jnp.zeros_like(cacc_ref)

        av, col = a_ref[...], 0
        for g_ref, w in zip(g_refs, widths):
            gv = g_ref[...]
            acc_ref[:, col:col + w] += _dot_tn(av, gv)
            cacc_ref[:, col:col + w] += jnp.broadcast_to(jnp.sum(gv.astype(F32), axis=0, keepdims=True), (SUBLANES, w))
            col += w

        @pl.when(s == n_s - 1)
        def _():
            o_ref[...] = acc_ref[...].astype(BF16)
            cs_ref[...] = cacc_ref[...]

    return pl.pallas_call(
        body, grid=(n_s,),
        in_specs=[pl.BlockSpec((ts, K), lambda s: (s, 0))] + [pl.BlockSpec((ts, w), lambda s: (s, 0)) for w in widths],
        out_specs=[pl.BlockSpec((K, total), lambda s: (0, 0)), pl.BlockSpec((SUBLANES, total), lambda s: (0, 0))],
        out_shape=[jax.ShapeDtypeStruct((K, total), BF16), jax.ShapeDtypeStruct((SUBLANES, total), F32)],
        scratch_shapes=[pltpu.VMEM((K, total), F32), pltpu.VMEM((SUBLANES, total), F32)],
        name=name, compiler_params=_cparams(("arbitrary",), vmem_mb))(a, *gs)


DILS = tuple(d for _, d in GROUPS if d > 1)


def _res_spec(d, tm, width):
    return pl.BlockSpec((d, tm // d, width), lambda i: (0, i, 0))


def _lane_scratch(tm, width):
    return [pltpu.VMEM((tm, LANES), F32)] * (width // LANES)


def _to_residue(val, dst_refs, dils, tm, dtype, scr):
    for c, ref in enumerate(scr):
        ref[...] = val[:, c * LANES:(c + 1) * LANES]
    for dst_ref, d in zip(dst_refs, dils):
        for r in range(d):
            cols = [ref[pl.ds(r, tm // d, stride=d), :] for ref in scr]
            dst_ref[r] = jnp.concatenate(cols, axis=1).astype(dtype)


def _from_residue(rows_of, d, tm, scr):
    for r in range(d):
        v = rows_of(r).astype(F32)
        for c, ref in enumerate(scr):
            ref[pl.ds(r, tm // d, stride=d), :] = v[:, c * LANES:(c + 1) * LANES]
    return jnp.concatenate([ref[...] for ref in scr], axis=1)


def _ln0_fwd(x, g, b, *, tm=512):
    S, Dm = x.shape

    def body(x_ref, g_ref, b_ref, h_ref, hb_ref, *rest):
        xhat, _ = _ln_stats(x_ref[...])
        h = xhat * g_ref[...] + b_ref[...]
        h_ref[...] = h
        hb_ref[...] = h.astype(BF16)
        _to_residue(h, rest[:len(DILS)], DILS, tm, BF16, rest[len(DILS):])

    row = pl.BlockSpec((tm, Dm), lambda i: (i, 0))
    vec = pl.BlockSpec((1, Dm), lambda i: (0, 0))
    return pl.pallas_call(
        body, grid=(S // tm,), in_specs=[row, vec, vec], out_specs=[row, row] + [_res_spec(d, tm, Dm) for d in DILS],
        out_shape=[jax.ShapeDtypeStruct((S, Dm), F32), jax.ShapeDtypeStruct((S, Dm), BF16)]
        + [jax.ShapeDtypeStruct((d, S // d, Dm), BF16) for d in DILS],
        scratch_shapes=_lane_scratch(tm, Dm),
        name="ln0_fwd", compiler_params=_cparams(("parallel",), 32))(x, g, b)


def _slab_spec(S, col0):
    return pl.BlockSpec((S, SLAB), lambda j: (0, col0 // SLAB + j))


def _zero_pads(scr, S):
    scr[0:PAD, :] = jnp.zeros((PAD, SLAB), F32)
    scr[S + PAD:S + 2 * PAD, :] = jnp.zeros((PAD, SLAB), F32)


def _shifted(scr, t):
    return (scr[PAD - 1 + t:PAD - 1 + t + CHUNK, :], scr[PAD + t:PAD + t + CHUNK, :],
            scr[PAD + 1 + t:PAD + 1 + t + CHUNK, :])


def _conv_gate_fwd(proj, conv_w):
    S = proj.shape[0]

    def body(b_ref, c_ref, h_ref, w_ref, o_ref, u_scr):
        _zero_pads(u_scr, S)
        for t in range(0, S, CHUNK):
            u_scr[PAD + t:PAD + t + CHUNK, :] = c_ref[t:t + CHUNK, :].astype(F32) * h_ref[t:t + CHUNK, :].astype(F32)
        w0, w1, w2 = w_ref[0:1, :], w_ref[1:2, :], w_ref[2:3, :]
        for t in range(0, S, CHUNK):
            um, u0, up = _shifted(u_scr, t)
            cv = w0 * um + w1 * u0 + w2 * up
            o_ref[t:t + CHUNK, :] = (b_ref[t:t + CHUNK, :].astype(F32) * cv).astype(BF16)

    return pl.pallas_call(
        body, grid=(D_CONV // SLAB,),
        in_specs=[_slab_spec(S, P_B), _slab_spec(S, P_C), _slab_spec(S, P_H),
                  pl.BlockSpec((3, SLAB), lambda j: (0, j))],
        out_specs=pl.BlockSpec((S, SLAB), lambda j: (0, j)),
        out_shape=jax.ShapeDtypeStruct((S, D_CONV), BF16),
        scratch_shapes=[pltpu.VMEM((S + 2 * PAD, SLAB), F32)],
        name="conv_gate_fwd", compiler_params=_cparams(("parallel",), 40))(proj, proj, proj, conv_w)


MASKED_DISTANCE = -1e34


def _attn_bias(i, sub, dil):
    j = lax.broadcasted_iota(jnp.int32, (2 * TQ, TQ), 0)
    a = lax.broadcasted_iota(jnp.int32, (2 * TQ, TQ), 1)
    rel = jnp.abs(j - RADIUS - a)
    kpos = i * TQ - RADIUS + j
    valid = (rel <= RADIUS) & (kpos >= 0) & (kpos < sub)
    return jnp.where(valid, -(rel * dil).astype(F32), MASKED_DISTANCE)


def _head_stats(rows):
    pad = jnp.zeros((LANES - len(rows), TQ), F32)
    return jnp.concatenate(list(rows) + [pad], axis=0).T


def _slope(g, h):
    return 2.0 ** (-8.0 * (g * HEADS_PER_GROUP + h + 1) / (N_GROUPS * HEADS_PER_GROUP))


def _window(p_ref, c_ref, n_ref):
    return jnp.concatenate([p_ref[TQ - RADIUS:, :], c_ref[...], n_ref[:RADIUS, :]], axis=0)


def _pair(a, h):
    return a[:, (h // 2) * LANES:(h // 2 + 1) * LANES]


def _own_lanes(a, h):
    lane = lax.broadcasted_iota(jnp.int32, a.shape, 1)
    return jnp.where((lane >= HEAD_DIM) == (h % 2 == 1), a, jnp.zeros_like(a))


def _own_rows(a, h):
    return a[(h % 2) * HEAD_DIM:(h % 2 + 1) * HEAD_DIM, :]


def _qkv_specs(nb, col0):
    def spec(col, shift):
        return pl.BlockSpec((None, TQ, GROUP_W), lambda r, i: (r, jnp.clip(i + shift, 0, nb - 1), col))

    return [spec(col0, 0), spec(col0 + 1, -1), spec(col0 + 1, 0), spec(col0 + 1, 1),
            spec(col0 + 2, -1), spec(col0 + 2, 0), spec(col0 + 2, 1)]


def _attn_fwd(qkv, col0, g):
    dil, sub, _ = qkv.shape
    nb = sub // TQ

    def body(q_ref, kp, kc, kn, vp, vc, vn, o_ref, lse_ref, ot_scr, s_scr, p_scr):
        bias = _attn_bias(pl.program_id(1), sub, dil)
        kwin = _window(kp, kc, kn)
        vwin = _window(vp, vc, vn)
        q = q_ref[...] * ATT_SCALE
        for h in range(HEADS_PER_GROUP):
            s_scr[h] = _dot_nt(_pair(kwin, h), _own_lanes(_pair(q, h), h))
        lse, inv_den = [], []
        for h in range(HEADS_PER_GROUP):
            s = s_scr[h] + _slope(g, h) * bias
            m = jnp.max(s, axis=0, keepdims=True)
            p = jnp.exp(s - m)
            den = jnp.sum(p, axis=0, keepdims=True)
            p_scr[h] = p.astype(BF16)
            inv_den.append(1.0 / den)
            lse.append(m + jnp.log(den))
        for h in range(HEADS_PER_GROUP):
            ot = _dot_tn(_pair(vwin, h), p_scr[h])
            ot_scr[h * HEAD_DIM:(h + 1) * HEAD_DIM, :] = _own_rows(ot, h) * inv_den[h]
        o_ref[...] = ot_scr[...].T
        lse_ref[...] = _head_stats(lse)

    return pl.pallas_call(
        body, grid=(dil, nb), in_specs=_qkv_specs(nb, col0),
        out_specs=[pl.BlockSpec((None, TQ, GROUP_W), lambda r, i: (r, i, 0)),
                   pl.BlockSpec((None, TQ, LANES), lambda r, i: (r, i, 0))],
        out_shape=[jax.ShapeDtypeStruct((dil, sub, GROUP_W), F32), jax.ShapeDtypeStruct((dil, sub, LANES), F32)],
        scratch_shapes=[pltpu.VMEM((GROUP_W, TQ), F32), pltpu.VMEM((HEADS_PER_GROUP, 2 * TQ, TQ), F32),
                        pltpu.VMEM((HEADS_PER_GROUP, 2 * TQ, TQ), BF16)],
        name=f"attn_fwd_g{g}", compiler_params=_cparams(("parallel", "arbitrary"), 32))(*([qkv] * 7))


def _expand_heads():
    h = lax.broadcasted_iota(jnp.int32, (LANES, GROUP_W), 0)
    c = lax.broadcasted_iota(jnp.int32, (LANES, GROUP_W), 1)
    return (c // HEAD_DIM == h).astype(F32)


def _dot_f32(a, b):
    return jnp.dot(a, b, preferred_element_type=F32, precision=lax.Precision.HIGHEST)


def _attn_combine(outs, lses, *, tm=512):
    S = outs[0].shape[1]
    n_col = GROUP_W // LANES

    def body(*refs):
        ins, e_ref = refs[:2 * N_GROUPS], refs[2 * N_GROUPS]
        c_ref, cb_ref, lt_ref = refs[2 * N_GROUPS + 1:2 * N_GROUPS + 4]
        scr = refs[2 * N_GROUPS + 4:]
        o, l = [ins[0][0]], [ins[N_GROUPS][0]]
        for k, d in enumerate(DILS):
            o_ref, l_ref = ins[1 + k], ins[N_GROUPS + 1 + k]
            o.append(_from_residue(lambda r: o_ref[r], d, tm, scr[k * (n_col + 1):k * (n_col + 1) + n_col]))
            l.append(_from_residue(lambda r: l_ref[r], d, tm, scr[k * (n_col + 1) + n_col:(k + 1) * (n_col + 1)]))
        m = jnp.maximum(jnp.maximum(l[0], l[1]), l[2])
        e = [jnp.exp(v - m) for v in l]
        den = e[0] + e[1] + e[2]
        comb = sum(_dot_f32(ev / den, e_ref[...]) * ov for ev, ov in zip(e, o))
        c_ref[...] = comb
        cb_ref[...] = comb.astype(BF16)
        lt_ref[...] = m + jnp.log(den)

    row = pl.BlockSpec((tm, GROUP_W), lambda i: (i, 0))
    dils = [d for _, d in GROUPS]
    return pl.pallas_call(
        body, grid=(S // tm,),
        in_specs=[_res_spec(d, tm, GROUP_W) for d in dils] + [_res_spec(d, tm, LANES) for d in dils]
        + [_resident((LANES, GROUP_W))],
        out_specs=[row, row, pl.BlockSpec((tm, LANES), lambda i: (i, 0))],
        out_shape=[jax.ShapeDtypeStruct((S, GROUP_W), F32), jax.ShapeDtypeStruct((S, GROUP_W), BF16),
                   jax.ShapeDtypeStruct((S, LANES), F32)],
        scratch_shapes=_lane_scratch(tm, GROUP_W + LANES) * len(DILS),
        name="attn_combine", compiler_params=_cparams(("parallel",), 32))(*outs, *lses, _expand_heads())


def _branch_mix(ya_in, comb_b, w_a, w_b, proj, *, tm=512):
    S = ya_in.shape[0]

    def body(ya_ref, cb_ref, wa_ref, wb_ref, ga_ref, gb_ref, yab_ref, mx_ref):
        y_a = _dot(ya_ref[...], wa_ref[...])
        y_b = _dot(cb_ref[...], wb_ref[...])
        yab_ref[:, 0:D_MODEL] = y_a.astype(BF16)
        yab_ref[:, D_MODEL:2 * D_MODEL] = y_b.astype(BF16)
        mx = jax.nn.sigmoid(ga_ref[...].astype(F32)) * y_a + jax.nn.sigmoid(gb_ref[...].astype(F32)) * y_b
        mx_ref[...] = mx.astype(BF16)

    return pl.pallas_call(
        body, grid=(S // tm,),
        in_specs=[pl.BlockSpec((tm, D_CONV), lambda i: (i, 0)), pl.BlockSpec((tm, GROUP_W), lambda i: (i, 0)),
                  pl.BlockSpec((D_CONV, D_MODEL), lambda i: (0, 0)), pl.BlockSpec((GROUP_W, D_MODEL), lambda i: (0, 0)),
                  pl.BlockSpec((tm, D_MODEL), lambda i: (i, P_GA // D_MODEL)),
                  pl.BlockSpec((tm, D_MODEL), lambda i: (i, P_GB // D_MODEL))],
        out_specs=[pl.BlockSpec((tm, 2 * D_MODEL), lambda i: (i, 0)), pl.BlockSpec((tm, D_MODEL), lambda i: (i, 0))],
        out_shape=[jax.ShapeDtypeStruct((S, 2 * D_MODEL), BF16), jax.ShapeDtypeStruct((S, D_MODEL), BF16)],
        name="branch_mix", compiler_params=_cparams(("parallel",), 40))(ya_in, comb_b, w_a, w_b, proj, proj)


def _mix_ln1(mixin, w_o, b_o, h0, g1, b1, *, tm=512):
    S = mixin.shape[0]

    def body(mx_ref, wo_ref, bo_ref, h0_ref, g_ref, b_ref, xh_ref, rs_ref, h1b_ref):
        z = ALPHA * h0_ref[...] + _dot(mx_ref[...], wo_ref[...]) + bo_ref[...]
        xhat, rstd = _ln_stats(z)
        xh_ref[...] = xhat
        rs_ref[...] = jnp.broadcast_to(rstd, (tm, LANES))
        h1b_ref[...] = (xhat * g_ref[...] + b_ref[...]).astype(BF16)

    row = pl.BlockSpec((tm, D_MODEL), lambda i: (i, 0))
    vec = pl.BlockSpec((1, D_MODEL), lambda i: (0, 0))
    return pl.pallas_call(
        body, grid=(S // tm,),
        in_specs=[row, pl.BlockSpec((D_MODEL, D_MODEL), lambda i: (0, 0)), vec, row, vec, vec],
        out_specs=[row, pl.BlockSpec((tm, LANES), lambda i: (i, 0)), row],
        out_shape=[jax.ShapeDtypeStruct((S, D_MODEL), F32), jax.ShapeDtypeStruct((S, LANES), F32),
                   jax.ShapeDtypeStruct((S, D_MODEL), BF16)],
        name="mix_ln1", compiler_params=_cparams(("parallel",), 40))(mixin, w_o, b_o, h0, g1, b1)


def _gelu_parts(cz):
    cdf = 0.5 * (1.0 + lax.erf(cz * INV_SQRT2))
    return cdf, cz * cdf


def _ffn_conv_fwd(up, cw, cb):
    S = up.shape[0]

    def body(a_ref, g_ref, w_ref, cb_ref, o_ref, a_scr):
        _zero_pads(a_scr, S)
        for t in range(0, S, CHUNK):
            a_scr[PAD + t:PAD + t + CHUNK, :] = a_ref[t:t + CHUNK, :].astype(F32)
        w0, w1, w2 = w_ref[0:1, :], w_ref[1:2, :], w_ref[2:3, :]
        for t in range(0, S, CHUNK):
            am, a0, ap = _shifted(a_scr, t)
            _, gel = _gelu_parts(w0 * am + w1 * a0 + w2 * ap + cb_ref[...])
            o_ref[t:t + CHUNK, :] = (gel * g_ref[t:t + CHUNK, :].astype(F32)).astype(BF16)

    return pl.pallas_call(
        body, grid=(D_FF // SLAB,),
        in_specs=[_slab_spec(S, 0), _slab_spec(S, D_FF), pl.BlockSpec((3, SLAB), lambda j: (0, j)),
                  pl.BlockSpec((1, SLAB), lambda j: (0, j))],
        out_specs=pl.BlockSpec((S, SLAB), lambda j: (0, j)),
        out_shape=jax.ShapeDtypeStruct((S, D_FF), BF16),
        scratch_shapes=[pltpu.VMEM((S + 2 * PAD, SLAB), F32)],
        name="ffn_conv_fwd", compiler_params=_cparams(("parallel",), 40))(up, up, cw, cb)


def _down_ln2_loss(f, w_down, b_down, xhat1, g1, b1, g2, b2, target, *, tm=256):
    S = f.shape[0]

    def body(f_ref, wd_ref, bd_ref, xh1_ref, g1_ref, b1_ref, g2_ref, b2_ref, t_ref, dz_ref, dzb_ref, st_ref):
        h1 = xh1_ref[...] * g1_ref[...] + b1_ref[...]
        z = ALPHA * h1 + _dot(f_ref[...], wd_ref[...]) + bd_ref[...]
        xhat, rstd = _ln_stats(z)
        err = xhat * g2_ref[...] + b2_ref[...] - t_ref[...]
        loss = (0.5 / D_MODEL) * jnp.sum(jnp.sum(err * err, axis=1, keepdims=True), axis=0, keepdims=True)
        dh2 = err * (1.0 / D_MODEL)
        dz = _ln_bwd(dh2, xhat, rstd, g2_ref[...])
        dz_ref[...] = dz
        dzb_ref[...] = dz.astype(BF16)
        upd = _rows8([jnp.sum(dh2 * xhat, axis=0, keepdims=True), jnp.sum(dh2, axis=0, keepdims=True),
                      jnp.broadcast_to(loss, (1, D_MODEL)), jnp.sum(dz, axis=0, keepdims=True)], D_MODEL)

        @pl.when(pl.program_id(0) == 0)
        def _():
            st_ref[...] = upd

        @pl.when(pl.program_id(0) != 0)
        def _():
            st_ref[...] += upd

    row = pl.BlockSpec((tm, D_MODEL), lambda i: (i, 0))
    vec = pl.BlockSpec((1, D_MODEL), lambda i: (0, 0))
    return pl.pallas_call(
        body, grid=(S // tm,),
        in_specs=[pl.BlockSpec((tm, D_FF), lambda i: (i, 0)), pl.BlockSpec((D_FF, D_MODEL), lambda i: (0, 0)),
                  vec, row, vec, vec, vec, vec, row],
        out_specs=[row, row, pl.BlockSpec((SUBLANES, D_MODEL), lambda i: (0, 0))],
        out_shape=[jax.ShapeDtypeStruct((S, D_MODEL), F32), jax.ShapeDtypeStruct((S, D_MODEL), BF16),
                   jax.ShapeDtypeStruct((SUBLANES, D_MODEL), F32)],
        name="down_ln2_loss", compiler_params=_cparams(("arbitrary",), 48))(
            f, w_down, b_down, xhat1, g1, b1, g2, b2, target)


def _ffn_conv_bwd(up, df, cw, cb):
    S = up.shape[0]

    def body(a_ref, g_ref, df_ref, w_ref, cb_ref, dup_ref, sm_ref, a_scr, d_scr):
        _zero_pads(a_scr, S)
        _zero_pads(d_scr, S)
        for t in range(0, S, CHUNK):
            a_scr[PAD + t:PAD + t + CHUNK, :] = a_ref[t:t + CHUNK, :].astype(F32)
        w0, w1, w2 = w_ref[0:1, :], w_ref[1:2, :], w_ref[2:3, :]
        zero = jnp.zeros((1, SLAB), F32)
        s_dg, s_dcz, s_w0, s_w1, s_w2 = zero, zero, zero, zero, zero
        for t in range(0, S, CHUNK):
            am, a0, ap = _shifted(a_scr, t)
            cz = w0 * am + w1 * a0 + w2 * ap + cb_ref[...]
            cdf, gel = _gelu_parts(cz)
            dfv = df_ref[t:t + CHUNK, :].astype(F32)
            dgte = dfv * gel
            dcz = dfv * g_ref[t:t + CHUNK, :].astype(F32) * (cdf + cz * jnp.exp(-0.5 * cz * cz) * INV_SQRT_2PI)
            dup_ref[1, t:t + CHUNK, :] = dgte.astype(BF16)
            d_scr[PAD + t:PAD + t + CHUNK, :] = dcz
            s_dg = s_dg + jnp.sum(dgte, axis=0, keepdims=True)
            s_dcz = s_dcz + jnp.sum(dcz, axis=0, keepdims=True)
            s_w0 = s_w0 + jnp.sum(dcz * am, axis=0, keepdims=True)
            s_w1 = s_w1 + jnp.sum(dcz * a0, axis=0, keepdims=True)
            s_w2 = s_w2 + jnp.sum(dcz * ap, axis=0, keepdims=True)
        s_da = zero
        for t in range(0, S, CHUNK):
            dm, d0, dp = _shifted(d_scr, t)
            da = w0 * dp + w1 * d0 + w2 * dm
            dup_ref[0, t:t + CHUNK, :] = da.astype(BF16)
            s_da = s_da + jnp.sum(da, axis=0, keepdims=True)
        sm_ref[...] = _rows8([s_da, s_dg, s_dcz, s_w0, s_w1, s_w2], SLAB)

    return pl.pallas_call(
        body, grid=(D_FF // SLAB,),
        in_specs=[_slab_spec(S, 0), _slab_spec(S, D_FF), pl.BlockSpec((S, SLAB), lambda j: (0, j)),
                  pl.BlockSpec((3, SLAB), lambda j: (0, j)), pl.BlockSpec((1, SLAB), lambda j: (0, j))],
        out_specs=[pl.BlockSpec((2, S, SLAB), lambda j: (0, 0, j)), pl.BlockSpec((SUBLANES, SLAB), lambda j: (0, j))],
        out_shape=[jax.ShapeDtypeStruct((2, S, D_FF), BF16), jax.ShapeDtypeStruct((SUBLANES, D_FF), F32)],
        scratch_shapes=[pltpu.VMEM((S + 2 * PAD, SLAB), F32)] * 2,
        name="ffn_conv_bwd", compiler_params=_cparams(("parallel",), 48))(up, up, df, cw, cb)


def _resident(shape):
    nd = len(shape)
    return pl.BlockSpec(shape, lambda *_: (0,) * nd, pipeline_mode=pl.Buffered(1))


def _up_bwd_ln1(dup, w_up3, dz2, xhat1, rstd1, g1, *, tm=256):
    S = dz2.shape[0]
    ns, _, tk = w_up3.shape
    per_plane = D_FF // tk

    def body(du_ref, w_ref, dz2_ref, xh_ref, rs_ref, g_ref, dz_ref, dzb_ref, st_ref):
        dh = ALPHA * dz2_ref[...]
        for k in range(ns):
            col = (k % per_plane) * tk
            dh = dh + _dot_nt(du_ref[k // per_plane, :, col:col + tk], w_ref[k])
        xhat = xh_ref[...]
        dz = _ln_bwd(dh, xhat, rs_ref[:, 0:1], g_ref[...])
        dz_ref[...] = dz
        dzb_ref[...] = dz.astype(BF16)
        upd = _rows8([jnp.sum(dh * xhat, axis=0, keepdims=True), jnp.sum(dh, axis=0, keepdims=True),
                      jnp.sum(dz, axis=0, keepdims=True)], D_MODEL)

        @pl.when(pl.program_id(0) == 0)
        def _():
            st_ref[...] = upd

        @pl.when(pl.program_id(0) != 0)
        def _():
            st_ref[...] += upd

    row = pl.BlockSpec((tm, D_MODEL), lambda i: (i, 0))
    return pl.pallas_call(
        body, grid=(S // tm,),
        in_specs=[pl.BlockSpec((dup.shape[0], tm, D_FF), lambda i: (0, i, 0)), _resident(w_up3.shape),
                  row, row, pl.BlockSpec((tm, LANES), lambda i: (i, 0)), pl.BlockSpec((1, D_MODEL), lambda i: (0, 0))],
        out_specs=[row, row, pl.BlockSpec((SUBLANES, D_MODEL), lambda i: (0, 0))],
        out_shape=[jax.ShapeDtypeStruct((S, D_MODEL), F32), jax.ShapeDtypeStruct((S, D_MODEL), BF16),
                   jax.ShapeDtypeStruct((SUBLANES, D_MODEL), F32)],
        name="up_bwd_ln1", compiler_params=_cparams(("arbitrary",), 48))(dup, w_up3, dz2, xhat1, rstd1, g1)


def _mix_bwd(dz1b, w_o, proj, yab, *, tm=512):
    S = dz1b.shape[0]

    def body(dz_ref, wo_ref, ga_ref, gb_ref, y_ref, dy_ref, dg_ref):
        dmx = _dot_nt(dz_ref[...], wo_ref[...])
        for k, gt_ref in enumerate((ga_ref, gb_ref)):
            sl = slice(k * D_MODEL, (k + 1) * D_MODEL)
            sg = jax.nn.sigmoid(gt_ref[...].astype(F32))
            dy_ref[:, sl] = (dmx * sg).astype(BF16)
            dg_ref[k] = (dmx * y_ref[:, sl].astype(F32) * sg * (1.0 - sg)).astype(BF16)

    row = pl.BlockSpec((tm, D_MODEL), lambda i: (i, 0))
    wide = pl.BlockSpec((tm, 2 * D_MODEL), lambda i: (i, 0))
    return pl.pallas_call(
        body, grid=(S // tm,),
        in_specs=[row, _resident(w_o.shape), pl.BlockSpec((tm, D_MODEL), lambda i: (i, P_GA // D_MODEL)),
                  pl.BlockSpec((tm, D_MODEL), lambda i: (i, P_GB // D_MODEL)), wide],
        out_specs=[wide, pl.BlockSpec((2, tm, D_MODEL), lambda i: (0, i, 0))],
        out_shape=[jax.ShapeDtypeStruct((S, 2 * D_MODEL), BF16), jax.ShapeDtypeStruct((2, S, D_MODEL), BF16)],
        name="mix_bwd", compiler_params=_cparams(("parallel",), 40))(dz1b, w_o, proj, proj, yab)


def _conv_gate_bwd(proj, dya_in, conv_w):
    S = proj.shape[0]

    def body(b_ref, c_ref, h_ref, dy_ref, w_ref, o_ref, sm_ref, u_scr, d_scr):
        _zero_pads(u_scr, S)
        _zero_pads(d_scr, S)
        for t in range(0, S, CHUNK):
            u_scr[PAD + t:PAD + t + CHUNK, :] = c_ref[t:t + CHUNK, :].astype(F32) * h_ref[t:t + CHUNK, :].astype(F32)
        w0, w1, w2 = w_ref[0:1, :], w_ref[1:2, :], w_ref[2:3, :]
        zero = jnp.zeros((1, SLAB), F32)
        s_w0, s_w1, s_w2 = zero, zero, zero
        for t in range(0, S, CHUNK):
            um, u0, up = _shifted(u_scr, t)
            dy = dy_ref[t:t + CHUNK, :].astype(F32)
            o_ref[0, t:t + CHUNK, :] = (dy * (w0 * um + w1 * u0 + w2 * up)).astype(BF16)
            dcv = dy * b_ref[t:t + CHUNK, :].astype(F32)
            d_scr[PAD + t:PAD + t + CHUNK, :] = dcv
            s_w0 = s_w0 + jnp.sum(dcv * um, axis=0, keepdims=True)
            s_w1 = s_w1 + jnp.sum(dcv * u0, axis=0, keepdims=True)
            s_w2 = s_w2 + jnp.sum(dcv * up, axis=0, keepdims=True)
        for t in range(0, S, CHUNK):
            dm, d0, dp = _shifted(d_scr, t)
            du = w0 * dp + w1 * d0 + w2 * dm
            o_ref[1, t:t + CHUNK, :] = (du * h_ref[t:t + CHUNK, :].astype(F32)).astype(BF16)
            o_ref[2, t:t + CHUNK, :] = (du * c_ref[t:t + CHUNK, :].astype(F32)).astype(BF16)
        sm_ref[...] = _rows8([s_w0, s_w1, s_w2], SLAB)

    return pl.pallas_call(
        body, grid=(D_CONV // SLAB,),
        in_specs=[_slab_spec(S, P_B), _slab_spec(S, P_C), _slab_spec(S, P_H),
                  pl.BlockSpec((S, SLAB), lambda j: (0, j)), pl.BlockSpec((3, SLAB), lambda j: (0, j))],
        out_specs=[pl.BlockSpec((3, S, SLAB), lambda j: (0, 0, j)), pl.BlockSpec((SUBLANES, SLAB), lambda j: (0, j))],
        out_shape=[jax.ShapeDtypeStruct((3, S, D_CONV), BF16), jax.ShapeDtypeStruct((SUBLANES, D_CONV), F32)],
        scratch_shapes=[pltpu.VMEM((S + 2 * PAD, SLAB), F32)] * 2,
        name="conv_gate_bwd", compiler_params=_cparams(("parallel",), 48))(proj, proj, proj, dya_in, conv_w)


def _comb_bwd(dyab, w_b, comb, lse_tot, *, tm=512):
    S = comb.shape[0]
    widths, dtypes = (GROUP_W, LANES, LANES), (BF16, F32, F32)

    def body(dy_ref, wb_ref, c_ref, lt_ref, e_ref, *rest):
        outs, scr = rest[:3 * N_GROUPS], rest[3 * N_GROUPS:]
        dcb = _dot_nt(dy_ref[...], wb_ref[...]).astype(BF16)
        dc = dcb.astype(F32)
        delta = lax.dot_general(dc * c_ref[...], e_ref[...], (((1,), (1,)), ((), ())),
                                preferred_element_type=F32, precision=lax.Precision.HIGHEST)
        for k, (val, dtype) in enumerate(zip((dc, lt_ref[...], delta), dtypes)):
            outs[k][0] = val.astype(dtype)
            _to_residue(val, [outs[3 * (1 + j) + k] for j in range(len(DILS))], DILS, tm, dtype,
                        scr[:val.shape[1] // LANES])

    out_specs, out_shape = [], []
    for _, d in GROUPS:
        out_specs += [_res_spec(d, tm, w) for w in widths]
        out_shape += [jax.ShapeDtypeStruct((d, S // d, w), t) for w, t in zip(widths, dtypes)]
    res = pl.pallas_call(
        body, grid=(S // tm,),
        in_specs=[pl.BlockSpec((tm, D_MODEL), lambda i: (i, 1)), _resident(w_b.shape),
                  pl.BlockSpec((tm, GROUP_W), lambda i: (i, 0)), pl.BlockSpec((tm, LANES), lambda i: (i, 0)),
                  _resident((LANES, GROUP_W))],
        out_specs=out_specs, out_shape=out_shape, scratch_shapes=_lane_scratch(tm, GROUP_W),
        name="comb_bwd", compiler_params=_cparams(("parallel",), 32))(dyab, w_b, comb, lse_tot, _expand_heads())
    return [tuple(res[3 * g:3 * g + 3]) for g in range(N_GROUPS)]


def _attn_bwd(qkv, col0, g, dcomb, lse_tot, delta):
    dil, sub, _ = qkv.shape
    nb = sub // TQ

    def body(q_ref, kp, kc, kn, vp, vc, vn, do_ref, lse_ref, dl_ref, dq_ref, dk_ref, dv_ref,
             ak, av, dqt_scr, s_scr, dp_scr, ds_scr, p_scr):
        i = pl.program_id(1)

        @pl.when(i == 0)
        def _():
            ak[...] = jnp.zeros_like(ak)
            av[...] = jnp.zeros_like(av)

        @pl.when(i < nb)
        def _():
            bias = _attn_bias(i, sub, dil)
            kwin = _window(kp, kc, kn)
            vwin = _window(vp, vc, vn)
            q = q_ref[...] * ATT_SCALE
            do = do_ref[...]
            lse_t, dl_t = lse_ref[...].T, dl_ref[...].T
            for h in range(HEADS_PER_GROUP):
                s_scr[h] = _dot_nt(_pair(kwin, h), _own_lanes(_pair(q, h), h))
                dp_scr[h] = _dot_nt(_pair(vwin, h), _own_lanes(_pair(do, h), h))
            for h in range(HEADS_PER_GROUP):
                p = jnp.exp(s_scr[h] + _slope(g, h) * bias - lse_t[h:h + 1, :])
                ds_scr[h] = (p * (dp_scr[h] - dl_t[h:h + 1, :])).astype(BF16)
                p_scr[h] = p.astype(BF16)
            for h in range(HEADS_PER_GROUP):
                dqt_scr[h * HEAD_DIM:(h + 1) * HEAD_DIM, :] = _own_rows(_dot_tn(_pair(kwin, h), ds_scr[h]), h)
            for h in range(0, HEADS_PER_GROUP, 2):
                cols = slice(h * HEAD_DIM, (h + 2) * HEAD_DIM)
                q2 = jnp.concatenate([_own_lanes(_pair(q, h), h), _own_lanes(_pair(q, h), h + 1)], axis=0)
                do2 = jnp.concatenate([_own_lanes(_pair(do, h), h), _own_lanes(_pair(do, h), h + 1)], axis=0)
                ak[RADIUS:RADIUS + 2 * TQ, cols] += _dot(jnp.concatenate([ds_scr[h], ds_scr[h + 1]], axis=1), q2)
                av[RADIUS:RADIUS + 2 * TQ, cols] += _dot(jnp.concatenate([p_scr[h], p_scr[h + 1]], axis=1), do2)
            dq_ref[...] = (dqt_scr[...].T * ATT_SCALE).astype(BF16)

        dk_ref[...] = ak[0:TQ, :].astype(BF16)
        dv_ref[...] = av[0:TQ, :].astype(BF16)
        ak[0:2 * TQ, :] = ak[TQ:3 * TQ, :]
        av[0:2 * TQ, :] = av[TQ:3 * TQ, :]
        ak[2 * TQ:3 * TQ, :] = jnp.zeros((TQ, GROUP_W), F32)
        av[2 * TQ:3 * TQ, :] = jnp.zeros((TQ, GROUP_W), F32)

    tok = pl.BlockSpec((None, TQ, GROUP_W), lambda r, i: (r, jnp.minimum(i, nb - 1), 0))
    stat = pl.BlockSpec((None, TQ, LANES), lambda r, i: (r, jnp.minimum(i, nb - 1), 0))
    dkv_spec = pl.BlockSpec((None, TQ, GROUP_W), lambda r, i: (r, jnp.maximum(i - 1, 0), 0))
    return pl.pallas_call(
        body, grid=(dil, nb + 1), in_specs=_qkv_specs(nb, col0) + [tok, stat, stat],
        out_specs=[tok, dkv_spec, dkv_spec], out_shape=[jax.ShapeDtypeStruct((dil, sub, GROUP_W), BF16)] * 3,
        scratch_shapes=[pltpu.VMEM((3 * TQ, GROUP_W), F32)] * 2 + [pltpu.VMEM((GROUP_W, TQ), F32)]
        + [pltpu.VMEM((HEADS_PER_GROUP, 2 * TQ, TQ), F32)] * 2 + [pltpu.VMEM((HEADS_PER_GROUP, 2 * TQ, TQ), BF16)] * 2,
        name=f"attn_bwd_g{g}", compiler_params=_cparams(("arbitrary", "arbitrary"), 32))(
            *([qkv] * 7), dcomb, lse_tot, delta)


def _in_bwd_ln0(dgated, dqkv, w_nat, w_dil, dz1, x, g0, *, tm=256):
    S = x.shape[0]
    n_gated, n_in = len(dgated), 3 * N_GROUPS

    def body(*refs):
        g_refs, d_refs = refs[:n_gated], refs[n_gated:n_gated + n_in]
        wn_ref, *wd_refs = refs[n_gated + n_in:n_gated + n_in + N_GROUPS]
        dz_ref, x_ref, g_ref, gx_ref, st_ref, *tmp_ref = refs[n_gated + n_in + N_GROUPS:]
        dh = ALPHA * dz_ref[...]
        col = 0
        for ref in g_refs:
            for k in range(ref.shape[0]):
                dh = dh + _dot_nt(ref[k], wn_ref[:, col:col + D_MODEL])
                col += D_MODEL
        for g, (_, d) in enumerate(GROUPS):
            rows = [jnp.concatenate([d_refs[3 * g + k][r] for k in range(3)], axis=1) for r in range(d)]
            w = wn_ref[:, col:col + QKV_W] if d == 1 else wd_refs[g - 1][...]
            res = _dot_nt(jnp.concatenate(rows, axis=0), w)
            if d == 1:
                dh = dh + res
            else:
                n = tm // d
                dh = dh + _from_residue(lambda r: res[r * n:(r + 1) * n, :], d, tm, tmp_ref)
        xhat, rstd = _ln_stats(x_ref[...])
        gx_ref[...] = _ln_bwd(dh, xhat, rstd, g_ref[...])
        upd = _rows8([jnp.sum(dh * xhat, axis=0, keepdims=True), jnp.sum(dh, axis=0, keepdims=True)], D_MODEL)

        @pl.when(pl.program_id(0) == 0)
        def _():
            st_ref[...] = upd

        @pl.when(pl.program_id(0) != 0)
        def _():
            st_ref[...] += upd

    row = pl.BlockSpec((tm, D_MODEL), lambda i: (i, 0))
    g_specs = [pl.BlockSpec((a.shape[0], tm, D_MODEL), lambda i: (0, i, 0)) for a in dgated]
    d_specs = []
    for _, d in GROUPS:
        d_specs += [_res_spec(d, tm, GROUP_W)] * 3
    operands = list(dgated) + [a for grp in dqkv for a in grp] + [w_nat] + list(w_dil) + [dz1, x, g0]
    return pl.pallas_call(
        body, grid=(S // tm,),
        in_specs=g_specs + d_specs + [_resident(w_nat.shape)] + [_resident(w.shape) for w in w_dil]
        + [row, row, pl.BlockSpec((1, D_MODEL), lambda i: (0, 0))],
        out_specs=[row, pl.BlockSpec((SUBLANES, D_MODEL), lambda i: (0, 0))],
        out_shape=[jax.ShapeDtypeStruct((S, D_MODEL), F32), jax.ShapeDtypeStruct((SUBLANES, D_MODEL), F32)],
        scratch_shapes=_lane_scratch(tm, D_MODEL),
        name="in_bwd_ln0", compiler_params=_cparams(("arbitrary",), 52))(*operands)


HBM_SPEC = pl.BlockSpec(memory_space=pltpu.HBM)


def _place():
    x, y, c = lax.axis_index("x"), lax.axis_index("y"), lax.axis_index("c")
    chips = [(1 - x, y), (x, 1 - y), (1 - x, 1 - y)]
    return x, y, c, chips


def _allgather_shards(shards, after, *, name, collective_id):
    n = len(shards)
    per = 6

    def body(*refs):
        ins, outs = refs[:n], refs[n + len(after):2 * n + len(after)]
        send_sems, recv_sems, loc_sems = refs[2 * n + len(after):]
        x, y, c, chips = _place()
        me = 2 * x + y
        sib = (x, y, 1 - c)
        peers = [sib] + [(px, py, c) for px, py in chips]
        barrier = pltpu.get_barrier_semaphore()
        for peer in peers:
            pl.semaphore_signal(barrier, inc=1, device_id=peer, device_id_type=MESH)
        pl.semaphore_wait(barrier, len(peers))

        def rcopy(w, k, src, dst, to):
            return pltpu.make_async_remote_copy(src_ref=src, dst_ref=dst, send_sem=send_sems.at[per * w + k],
                                                recv_sem=recv_sems.at[per * w + k], device_id=to, device_id_type=MESH)

        split = [s.shape[0] == N_CORES for s in shards]
        half = lambda w: c if split[w] else 0
        local, sends = [], []
        for w in range(n):
            cp = pltpu.make_async_copy(ins[w], outs[w].at[me], loc_sems.at[w])
            cp.start()
            local.append(cp)
            for j, (px, py) in enumerate(chips):
                cp = rcopy(w, j, ins[w].at[half(w)], outs[w].at[me, half(w)], (px, py, c))
                cp.start()
                sends.append(cp)
        for w in range(n):
            for j, (px, py) in enumerate(chips):
                slot = outs[w].at[2 * px + py, half(w)]
                rcopy(w, j, slot, slot, (px, py, c)).wait_recv()
                if split[w]:
                    cp = rcopy(w, 3 + j, slot, slot, sib)
                    cp.start()
                    sends.append(cp)
        for w in range(n):
            if split[w]:
                for j, (px, py) in enumerate(chips):
                    slot = outs[w].at[2 * px + py, 1 - c]
                    rcopy(w, 3 + j, slot, slot, sib).wait_recv()
        for cp in sends:
            cp.wait_send()
        for cp in local:
            cp.wait()

    return pl.kernel(
        body, out_type=[jax.ShapeDtypeStruct((N_CHIPS,) + s.shape, s.dtype) for s in shards],
        mesh=plsc.ScalarSubcoreMesh(axis_name="sequencer", num_cores=1),
        scratch_types=[pltpu.SemaphoreType.DMA((per * n,)), pltpu.SemaphoreType.DMA((per * n,)),
                       pltpu.SemaphoreType.DMA((n,))],
        name=name, compiler_params=pltpu.CompilerParams(collective_id=collective_id))(*shards, *after)


def _exchange_grads(grads, *, name, collective_id):
    n = len(grads)
    per = 7

    def body(*refs):
        ins, outs = refs[:n], refs[n:2 * n]
        send_sems, recv_sems, loc_sems = refs[2 * n:]
        x, y, c, chips = _place()
        me = 2 * x + y
        sib = (x, y, 1 - c)
        peers = [sib] + [(px, py, c) for px, py in chips]
        barrier = pltpu.get_barrier_semaphore()
        for peer in peers:
            pl.semaphore_signal(barrier, inc=1, device_id=peer, device_id_type=MESH)
        pl.semaphore_wait(barrier, len(peers))

        def rcopy(w, k, src, dst, to):
            return pltpu.make_async_remote_copy(src_ref=src, dst_ref=dst, send_sem=send_sems.at[per * w + k],
                                                recv_sem=recv_sems.at[per * w + k], device_id=to, device_id_type=MESH)

        local, sends = [], []
        for w in range(n):
            cp = pltpu.make_async_copy(ins[w].at[me], outs[w].at[c, me], loc_sems.at[w])
            cp.start()
            local.append(cp)
            cp = rcopy(w, 0, ins[w].at[me], outs[w].at[c, me], sib)
            cp.start()
            sends.append(cp)
            for j, (px, py) in enumerate(chips):
                cp = rcopy(w, 1 + j, ins[w].at[2 * px + py], outs[w].at[c, me], (px, py, c))
                cp.start()
                sends.append(cp)
        for w in range(n):
            for j, (px, py) in enumerate(chips):
                slot = outs[w].at[c, 2 * px + py]
                rcopy(w, 1 + j, slot, slot, (px, py, c)).wait_recv()
                cp = rcopy(w, 4 + j, slot, slot, sib)
                cp.start()
                sends.append(cp)
        for w in range(n):
            slot = outs[w].at[1 - c, me]
            rcopy(w, 0, slot, slot, sib).wait_recv()
            for j, (px, py) in enumerate(chips):
                slot = outs[w].at[1 - c, 2 * px + py]
                rcopy(w, 4 + j, slot, slot, sib).wait_recv()
        for cp in sends:
            cp.wait_send()
        for cp in local:
            cp.wait()

    return pl.kernel(
        body, out_type=[jax.ShapeDtypeStruct((N_CORES,) + g.shape, g.dtype) for g in grads],
        mesh=plsc.ScalarSubcoreMesh(axis_name="sequencer", num_cores=1),
        scratch_types=[pltpu.SemaphoreType.DMA((per * n,)), pltpu.SemaphoreType.DMA((per * n,)),
                       pltpu.SemaphoreType.DMA((n,))],
        name=name, compiler_params=pltpu.CompilerParams(collective_id=collective_id))(*grads)


def _allgather_small(vec):
    def body(v_ref, o_ref, send_sems, recv_sems, loc_sem):
        x, y, c = lax.axis_index("x"), lax.axis_index("y"), lax.axis_index("c")
        me = 4 * x + 2 * y + c

        def peer(k):
            flip = lambda v, bit: 1 - v if (k >> bit) & 1 else v
            return flip(x, 2), flip(y, 1), flip(c, 0)

        loc = pltpu.make_async_copy(v_ref, o_ref.at[me], loc_sem)
        loc.start()
        sends = []
        for k in range(1, N_DEV):
            cp = pltpu.make_async_remote_copy(src_ref=v_ref, dst_ref=o_ref.at[me], send_sem=send_sems.at[k - 1],
                                              recv_sem=recv_sems.at[k - 1], device_id=peer(k), device_id_type=MESH)
            cp.start()
            sends.append(cp)
        for k in range(1, N_DEV):
            px, py, pc = peer(k)
            pltpu.make_async_remote_copy(src_ref=v_ref, dst_ref=o_ref.at[4 * px + 2 * py + pc],
                                         send_sem=send_sems.at[k - 1], recv_sem=recv_sems.at[k - 1],
                                         device_id=(px, py, pc), device_id_type=MESH).wait_recv()
        for cp in sends:
            cp.wait_send()
        loc.wait()

    return pl.pallas_call(
        body, in_specs=[HBM_SPEC], out_specs=HBM_SPEC,
        out_shape=jax.ShapeDtypeStruct((N_DEV,) + vec.shape, vec.dtype),
        scratch_shapes=[pltpu.SemaphoreType.DMA((N_DEV - 1,)), pltpu.SemaphoreType.DMA((N_DEV - 1,)),
                        pltpu.SemaphoreType.DMA],
        name="allgather_small")(vec)


def _adamw(w, g, m, v):
    m = ADAM_B1 * m + (1.0 - ADAM_B1) * g
    v = ADAM_B2 * v + (1.0 - ADAM_B2) * (g * g)
    m_hat = m / (1.0 - ADAM_B1 ** ADAM_STEP)
    v_hat = v / (1.0 - ADAM_B2 ** ADAM_STEP)
    delta = -ADAM_LR * (m_hat / (jnp.sqrt(v_hat) + ADAM_EPS) + ADAM_WD * w)
    return delta, m, v


def _reduce_adamw(parts, w, m, v, *, tr, name):
    R, C = w.shape

    def body(p_ref, w_ref, m_ref, v_ref, g_ref, d_ref, nm_ref, nv_ref):
        def core_sum(cc):
            s = p_ref[cc, 0].astype(F32)
            for k in range(1, N_CHIPS):
                s = s + p_ref[cc, k].astype(F32)
            return s

        g = core_sum(0) + core_sum(1)
        delta, nm, nv = _adamw(w_ref[...], g, m_ref[...], v_ref[...])
        g_ref[...] = g
        d_ref[...] = delta
        nm_ref[...] = nm
        nv_ref[...] = nv

    blk = pl.BlockSpec((tr, C), lambda i: (i, 0))
    return pl.pallas_call(
        body, grid=(R // tr,),
        in_specs=[pl.BlockSpec((N_CORES, N_CHIPS, tr, C), lambda i: (0, 0, i, 0)), blk, blk, blk],
        out_specs=[blk] * 4, out_shape=[jax.ShapeDtypeStruct((R, C), F32)] * 4,
        name=name, compiler_params=_cparams(("parallel",), 40))(parts, w, m, v)


def _sum_devices(allv):
    _, R, _ = allv.shape

    def body(a_ref, o_ref):
        s = a_ref[0]
        for d in range(1, N_DEV):
            s = s + a_ref[d]
        o_ref[...] = s

    return pl.pallas_call(body, out_shape=jax.ShapeDtypeStruct((R, LANES), F32), name="sum_small")(allv)


def _adamw_small(w, g, m, v):
    def body(w_ref, g_ref, m_ref, v_ref, d_ref, nm_ref, nv_ref):
        delta, nm, nv = _adamw(w_ref[...], g_ref[...], m_ref[...], v_ref[...])
        d_ref[...] = delta
        nm_ref[...] = nm
        nv_ref[...] = nv

    return pl.pallas_call(body, out_shape=[jax.ShapeDtypeStruct(w.shape, F32)] * 3, name="adamw_small")(w, g, m, v)


def _pack(pieces):
    flat = [p.reshape(-1) for p in pieces]
    offs, n = [], 0
    for f in flat:
        offs.append(n)
        n += f.shape[0]
    total = -(-n // (SUBLANES * LANES)) * SUBLANES * LANES
    flat.append(jnp.zeros((total - n,), F32))
    return jnp.concatenate(flat).reshape(total // LANES, LANES), offs


def _local_step(x, target, p, wfull, on_ready=lambda group: None):
    S = x.shape[0]
    w_in3, w_up3 = wfull["w_in"], wfull["w_up"]
    w_a, w_o, w_down, w_b = wfull["w_a"], wfull["w_o"], wfull["w_down"], wfull["w_b"]
    conv_w, ffn_conv_w = wfull["conv_w"], wfull["ffn_conv_w"]
    dils = [d for _, d in GROUPS]

    h0, h0b, *h0_res = _ln0_fwd(x, p["ln0_g"], p["ln0_b"])
    h0_rows = [h0b] + [h.reshape(S, D_MODEL) for h in h0_res]

    w_blocks = w_in3.transpose(1, 0, 2).reshape(D_MODEL, N_BLK, GROUP_W)
    w_perm = jnp.concatenate([w_blocks[:, b] for b in PERM], axis=1)
    b_blocks = p["b_in"].reshape(N_BLK, GROUP_W)
    b_perm = jnp.concatenate([b_blocks[b] for b in PERM]).reshape(1, N_IN)
    w_nat, b_nat = w_perm[:, :N_NAT], b_perm[:, :N_NAT]
    qkv_cols = [slice(P_Q0 + g * QKV_W, P_Q0 + (g + 1) * QKV_W) for g in range(N_GROUPS)]
    w_qkv = [w_perm[:, c] for c in qkv_cols]

    proj = _mm_nn(h0b, w_nat, b_nat, tm=512, tn=N_NAT // 2, out_dtype=BF16, name="proj")
    qkv = [proj[None]]
    for g in range(1, N_GROUPS):
        t = _mm_nn(h0_rows[g], w_qkv[g], b_perm[:, qkv_cols[g]], tm=512, tn=QKV_W, out_dtype=BF16, name=f"proj_qkv{g}")
        qkv.append(t.reshape(dils[g], S // dils[g], QKV_W))
    col0 = [P_Q0 // GROUP_W] + [0] * (N_GROUPS - 1)
    ya_in = _conv_gate_fwd(proj, conv_w)
    att = [_attn_fwd(qkv[g], col0[g], g) for g in range(N_GROUPS)]
    comb, comb_b, lse_tot = _attn_combine([a[0] for a in att], [a[1] for a in att])
    yab, mixin = _branch_mix(ya_in, comb_b, w_a, w_b, proj)
    xhat1, rstd1, h1b = _mix_ln1(mixin, w_o, p["b_o"], h0, p["ln1_g"], p["ln1_b"])
    up = _mm_nn(h1b, w_up3, p["b_up"], tm=512, tn=w_up3.shape[2], out_dtype=BF16, name="up")
    f = _ffn_conv_fwd(up, ffn_conv_w, p["ffn_conv_b"])
    dz2, dz2b, st2 = _down_ln2_loss(f, w_down, p["b_down"], xhat1, p["ln1_g"], p["ln1_b"],
                                    p["ln2_g"], p["ln2_b"], target)

    gw = {}
    gw["w_down"] = _mm_tn(f, dz2b, n_out=1, tn=D_MODEL, ts=1024, g_block=(1024, D_MODEL),
                          g_map=lambda j, s: (s, 0), name="grad_w_down").reshape(N_CHIPS, D_FF // N_CHIPS, D_MODEL)
    df = _mm_nt(dz2b, w_down, tm=512, name="df")
    dup, sm_ffn = _ffn_conv_bwd(up, df, ffn_conv_w, p["ffn_conv_b"])
    up_tn = w_up3.shape[2]
    up_pp = D_FF // up_tn
    gw["w_up"] = _mm_tn(h1b, dup, n_out=N_CHIPS, tn=up_tn, ts=1024, g_block=(None, 1024, up_tn),
                        g_map=lambda j, s: (j // up_pp, s, j % up_pp), name="grad_w_up")
    on_ready({n: gw[n] for n in ("w_down", "w_up")})
    dz1, dz1b, st1 = _up_bwd_ln1(dup, w_up3, dz2, xhat1, rstd1, p["ln1_g"])

    gw["w_o"] = _mm_tn(mixin, dz1b, n_out=1, tn=D_MODEL, ts=512, g_block=(512, D_MODEL),
                       g_map=lambda j, s: (s, 0), name="grad_w_o").reshape(N_CHIPS, D_MODEL // N_CHIPS, D_MODEL)
    dyab, dgab = _mix_bwd(dz1b, w_o, proj, yab)
    gw["w_a"] = _mm_tn(ya_in, dyab, n_out=1, tn=D_MODEL, ts=512, g_block=(512, D_MODEL),
                       g_map=lambda j, s: (s, 0), name="grad_w_a").reshape(N_CHIPS, D_CONV // N_CHIPS, D_MODEL)
    gw_b = _mm_tn(comb_b, dyab, n_out=1, tn=D_MODEL, ts=1024, g_block=(1024, D_MODEL),
                  g_map=lambda j, s: (s, 1), name="grad_w_b")
    gw["w_b"] = gw_b.reshape(GROUP_W, N_CHIPS, D_MODEL // N_CHIPS).transpose(1, 0, 2)
    on_ready({n: gw[n] for n in ("w_o", "w_a", "w_b")})
    dya_in = _mm_nt(dyab, w_a, tm=512, a_col=0, name="dya_in")
    dbch, sm_conv = _conv_gate_bwd(proj, dya_in, conv_w)
    att_stats = _comb_bwd(dyab, w_b, comb, lse_tot)
    dqkv = [_attn_bwd(qkv[g], col0[g], g, *att_stats[g]) for g in range(N_GROUPS)]

    w_pieces, b_pieces = [], []
    for nm, planes in (("bch", dbch), ("gab", dgab)):
        pw, pc = _mm_tn(h0b, planes, n_out=planes.shape[0], tn=D_MODEL, ts=1024, g_block=(None, 1024, D_MODEL),
                        g_map=lambda j, s: (j, s, 0), colsum=True, name="grad_w_in_" + nm)
        w_pieces.append(pw.transpose(1, 0, 2).reshape(D_MODEL, planes.shape[0] * D_MODEL))
        b_pieces.append(pc[0])
    for g in range(N_GROUPS):
        pw, pc = _mm_tn_cat(h0_rows[g], [a.reshape(S, GROUP_W) for a in dqkv[g]], ts=1024, name=f"grad_w_in_qkv{g}")
        w_pieces.append(pw)
        b_pieces.append(pc[0])
    dw_blocks = jnp.concatenate(w_pieces, axis=1).reshape(D_MODEL, N_BLK, GROUP_W)
    dw_ref = jnp.concatenate([dw_blocks[:, b] for b in INV_PERM], axis=1)
    gw["w_in"] = dw_ref.reshape(D_MODEL, N_CHIPS, N_IN // N_CHIPS).transpose(1, 0, 2)
    on_ready({"w_in": gw["w_in"]})
    db_blocks = jnp.concatenate(b_pieces).reshape(N_BLK, GROUP_W)
    grad_b_in = jnp.concatenate([db_blocks[b] for b in INV_PERM])

    grad_x, st0 = _in_bwd_ln0([dbch, dgab], dqkv, w_nat, w_qkv[1:], dz1, x, p["ln0_g"])

    small = {
        "loss": st2[2:3, 0:1],
        "ln0_g": st0[0], "ln0_b": st0[1], "b_in": grad_b_in, "conv_w": sm_conv[0:3],
        "b_o": st1[2], "ln1_g": st1[0], "ln1_b": st1[1],
        "b_up": jnp.concatenate([sm_ffn[0], sm_ffn[1]]), "ffn_conv_w": sm_ffn[3:6], "ffn_conv_b": sm_ffn[2],
        "b_down": st2[3], "ln2_g": st2[0], "ln2_b": st2[1],
    }
    return grad_x, gw, small


BIG = ("w_in", "w_a", "w_b", "w_o", "w_up", "w_down")
CONV = ("conv_w", "ffn_conv_w")
VECS = ("ln0_g", "ln0_b", "b_in", "b_o", "ln1_g", "ln1_b", "b_up", "ffn_conv_b", "b_down", "ln2_g", "ln2_b")
ORDER = ("ln0_g", "ln0_b", "w_in", "b_in", "conv_w", "w_a", "w_b", "w_o", "b_o", "ln1_g", "ln1_b", "w_up", "b_up",
         "ffn_conv_w", "ffn_conv_b", "w_down", "b_down", "ln2_g", "ln2_b")
SMALL_ORDER = ("loss",) + VECS + CONV


def _step(x, target, W, Mo, Vo):
    x2, t2 = x[0], target[0]
    big2 = {n: W[n][0] for n in BIG}
    halves = lambda a: a.astype(BF16).reshape(N_CORES, a.shape[0] // N_CORES, a.shape[1])
    whole = lambda g: g.reshape(N_CHIPS, g.shape[1] * g.shape[2], g.shape[3])
    later = tuple(n for n in BIG if n != "w_in")
    first = _allgather_shards([halves(big2["w_in"])], [], name="allgather_w_in", collective_id=1)
    rest = _allgather_shards([halves(big2[n]) for n in later] + [W[n] for n in CONV], first,
                             name="allgather_rest", collective_id=2)
    gathered = {n: whole(g) for n, g in zip(("w_in",) + later + CONV, first + rest)}
    wfull = {
        "w_in": gathered["w_in"], "w_up": gathered["w_up"],
        "w_a": gathered["w_a"].reshape(D_CONV, D_MODEL), "w_o": gathered["w_o"].reshape(D_MODEL, D_MODEL),
        "w_down": gathered["w_down"].reshape(D_FF, D_MODEL),
        "w_b": gathered["w_b"].transpose(1, 0, 2).reshape(GROUP_W, D_MODEL),
        "conv_w": gathered["conv_w"].transpose(1, 0, 2).reshape(3, D_CONV),
        "ffn_conv_w": gathered["ffn_conv_w"].transpose(1, 0, 2).reshape(3, D_FF),
    }
    pvec = {n: W[n].reshape(1, -1) for n in VECS}

    parts = {}
    exchange_ids = iter((3, 4, 5))

    def exchange(group):
        names = tuple(group)
        res = _exchange_grads([group[n] for n in names], name="exchange_" + "_".join(names),
                              collective_id=next(exchange_ids))
        parts.update(zip(names, res))

    grad_x, _, small = _local_step(x2, t2, pvec, wfull, exchange)
    out = {}
    for n in BIG:
        tr = {"w_in": 128, "w_up": 128, "w_b": 128}.get(n, big2[n].shape[0] // 4)
        g, d, nm, nv = _reduce_adamw(parts[n], big2[n], Mo[n][0], Vo[n][0], tr=tr, name="adamw_" + n)
        out[n] = tuple(a[None] for a in (g, d, nm, nv))

    vec, offs = _pack([small[n] for n in SMALL_ORDER])
    tot = _sum_devices(_allgather_small(vec)).reshape(-1)
    off = dict(zip(SMALL_ORDER, offs))
    loss = tot[off["loss"]]
    chip = 2 * lax.axis_index("x") + lax.axis_index("y")
    gs = {}
    for n in VECS:
        gs[n] = lax.slice(tot, (off[n],), (off[n] + W[n].size,)).reshape(W[n].shape)
    for n in CONV:
        width = W[n].shape[2]
        full = lax.slice(tot, (off[n],), (off[n] + 3 * N_CHIPS * width,)).reshape(1, 3, N_CHIPS * width)
        gs[n] = lax.dynamic_slice_in_dim(full, chip * width, width, axis=2)
    names = VECS + CONV
    wp, _ = _pack([W[n] for n in names])
    gp, poffs = _pack([gs[n] for n in names])
    mp, _ = _pack([Mo[n] for n in names])
    vp, _ = _pack([Vo[n] for n in names])
    dl, nm, nv = (a.reshape(-1) for a in _adamw_small(wp, gp, mp, vp))
    for n, o in zip(names, poffs):
        cut = lambda a: lax.slice(a, (o,), (o + W[n].size,)).reshape(W[n].shape)
        out[n] = (gs[n], cut(dl), cut(nm), cut(nv))

    res = [loss, grad_x[None]]
    for k in range(4):
        res += [out[n][k] for n in ORDER]
    return tuple(res)


def kernel(x, ln0_g, ln0_b, w_in, b_in, conv_w, w_a, w_b, w_o, b_o, ln1_g, ln1_b, w_up, b_up, ffn_conv_w, ffn_conv_b, w_down, b_down, ln2_g, ln2_b, loss_target, m_ln0_g, m_ln0_b, m_w_in, m_b_in, m_conv_w, m_w_a, m_w_b, m_w_o, m_b_o, m_ln1_g, m_ln1_b, m_w_up, m_b_up, m_ffn_conv_w, m_ffn_conv_b, m_w_down, m_b_down, m_ln2_g, m_ln2_b, v_ln0_g, v_ln0_b, v_w_in, v_b_in, v_conv_w, v_w_a, v_w_b, v_w_o, v_b_o, v_ln1_g, v_ln1_b, v_w_up, v_b_up, v_ffn_conv_w, v_ffn_conv_b, v_w_down, v_b_down, v_ln2_g, v_ln2_b):
    W = dict(zip(ORDER, (ln0_g, ln0_b, w_in, b_in, conv_w, w_a, w_b, w_o, b_o, ln1_g, ln1_b, w_up, b_up,
                         ffn_conv_w, ffn_conv_b, w_down, b_down, ln2_g, ln2_b)))
    Mo = dict(zip(ORDER, (m_ln0_g, m_ln0_b, m_w_in, m_b_in, m_conv_w, m_w_a, m_w_b, m_w_o, m_b_o, m_ln1_g, m_ln1_b,
                          m_w_up, m_b_up, m_ffn_conv_w, m_ffn_conv_b, m_w_down, m_b_down, m_ln2_g, m_ln2_b)))
    Vo = dict(zip(ORDER, (v_ln0_g, v_ln0_b, v_w_in, v_b_in, v_conv_w, v_w_a, v_w_b, v_w_o, v_b_o, v_ln1_g, v_ln1_b,
                          v_w_up, v_b_up, v_ffn_conv_w, v_ffn_conv_b, v_w_down, v_b_down, v_ln2_g, v_ln2_b)))
    return _step(x, loss_target, W, Mo, Vo)
```

```python
import functools
import math

import jax
import jax.numpy as jnp
from jax import lax
from jax.experimental import pallas as pl
from jax.experimental.pallas import tpu as pltpu
from jax.experimental.pallas import tpu_sc as plsc

F32 = jnp.float32
BF16 = jnp.bfloat16

D_MODEL = 1024
D_CONV = D_MODEL
HEAD_DIM = 64
HEADS_PER_GROUP = 8
GROUPS = ((128, 1), (512, 4), (2048, 16))
N_GROUPS = len(GROUPS)
GROUP_W = HEADS_PER_GROUP * HEAD_DIM
QKV_W = N_GROUPS * GROUP_W
RADIUS = 64
D_FF = 2816
LN_EPS = 1e-5
ALPHA = 2.0 ** 0.25
MASK_VALUE = -1e30
ATT_SCALE = HEAD_DIM ** -0.5
OFF_B = 0
OFF_C = OFF_B + D_CONV
OFF_H = OFF_C + D_CONV
OFF_Q = OFF_H + D_CONV
OFF_K = OFF_Q + QKV_W
OFF_V = OFF_K + QKV_W
OFF_GA = OFF_V + QKV_W
OFF_GB = OFF_GA + D_MODEL
N_IN = OFF_GB + D_MODEL
ADAM_LR = 0.001
ADAM_B1 = 0.9
ADAM_B2 = 0.999
ADAM_EPS = 1e-08
ADAM_WD = 0.01
ADAM_STEP = 10
INV_SQRT2 = 0.7071067811865476
INV_SQRT_2PI = 0.3989422804014327

LANES = 128
SUBLANES = 8
VMEM_BYTES_V7X = 64 * 1024 * 1024
N_CHIPS = 4
N_CORES = 2
N_DEV = N_CHIPS * N_CORES
MESH = pl.DeviceIdType.MESH

N_BLK = N_IN // GROUP_W
PERM = (0, 1, 2, 3, 4, 5, 15, 16, 17, 18, 6, 9, 12, 7, 10, 13, 8, 11, 14)
INV_PERM = tuple(PERM.index(b) for b in range(N_BLK))
P_B, P_C, P_H, P_GA, P_GB, P_Q0 = 0, 1024, 2048, 3072, 4096, 5120
N_NAT = P_Q0 + QKV_W // N_GROUPS * 3
N_GATED = P_Q0

SLAB = 128
CHUNK = 256
PAD = SUBLANES
TQ = 128


def _cparams(sem, vmem_mb):
    assert vmem_mb * 1024 * 1024 < VMEM_BYTES_V7X
    return pltpu.CompilerParams(dimension_semantics=sem, vmem_limit_bytes=vmem_mb * 1024 * 1024)


def _dot(a, b):
    return jnp.dot(a, b, preferred_element_type=F32)


def _dot_nt(a, b):
    return lax.dot_general(a, b, (((1,), (1,)), ((), ())), preferred_element_type=F32)


def _dot_tn(a, b):
    return lax.dot_general(a, b, (((0,), (0,)), ((), ())), preferred_element_type=F32)


def _ln_stats(z):
    mu = jnp.mean(z, -1, keepdims=True)
    zc = z - mu
    var = jnp.mean(zc * zc, -1, keepdims=True)
    rstd = lax.rsqrt(var + LN_EPS)
    return zc * rstd, rstd


def _ln_bwd(dh, xhat, rstd, g):
    dxh = dh * g
    m1 = jnp.mean(dxh, -1, keepdims=True)
    m2 = jnp.mean(dxh * xhat, -1, keepdims=True)
    return rstd * (dxh - m1 - xhat * m2)


def _rows8(rows, width):
    pad = [jnp.zeros((1, width), F32)] * (SUBLANES - len(rows))
    return jnp.concatenate(list(rows) + pad, axis=0)


def _mm_nn(a, w, bias, *, tm, tn, out_dtype, name, vmem_mb=40):
    M, K = a.shape
    if w.ndim == 3:
        assert w.shape[2] == tn
        n_tiles = w.shape[0]
        w_spec = pl.BlockSpec((None, K, tn), lambda i, j: (j, 0, 0))
    else:
        n_tiles = w.shape[1] // tn
        w_spec = pl.BlockSpec((K, tn), lambda i, j: (0, j))

    def body(a_ref, w_ref, b_ref, o_ref):
        o_ref[...] = (_dot(a_ref[...], w_ref[...]) + b_ref[...]).astype(o_ref.dtype)

    return pl.pallas_call(
        body, grid=(M // tm, n_tiles),
        in_specs=[pl.BlockSpec((tm, K), lambda i, j: (i, 0)), w_spec, pl.BlockSpec((1, tn), lambda i, j: (0, j))],
        out_specs=pl.BlockSpec((tm, tn), lambda i, j: (i, j)),
        out_shape=jax.ShapeDtypeStruct((M, n_tiles * tn), out_dtype),
        name=name, compiler_params=_cparams(("parallel", "arbitrary"), vmem_mb))(a, w, bias)


def _mm_nt(a, w, *, tm, a_col=0, name, vmem_mb=40):
    M = a.shape[0]
    N, K = w.shape

    def body(a_ref, w_ref, o_ref):
        o_ref[...] = _dot_nt(a_ref[...], w_ref[...]).astype(o_ref.dtype)

    return pl.pallas_call(
        body, grid=(M // tm,),
        in_specs=[pl.BlockSpec((tm, K), lambda i: (i, a_col)),
                  pl.BlockSpec((N, K), lambda i: (0, 0))],
        out_specs=pl.BlockSpec((tm, N), lambda i: (i, 0)),
        out_shape=jax.ShapeDtypeStruct((M, N), BF16),
        name=name, compiler_params=_cparams(("parallel",), vmem_mb))(a, w)


def _mm_tn(a, g, *, n_out, tn, ts, g_block, g_map, colsum=False, name, vmem_mb=48):
    S, K = a.shape
    n_s = S // ts

    def body(a_ref, g_ref, *rest):
        if colsum:
            o_ref, cs_ref, acc_ref, cacc_ref = rest
        else:
            o_ref, acc_ref = rest
        s = pl.program_id(1)

        @pl.when(s == 0)
        def _():
            acc_ref[...] = jnp.zeros_like(acc_ref)
            if colsum:
                cacc_ref[...] = jnp.zeros_like(cacc_ref)

        gv = g_ref[...]
        acc_ref[...] += _dot_tn(a_ref[...], gv)
        if colsum:
            cacc_ref[...] += jnp.broadcast_to(jnp.sum(gv.astype(F32), axis=0, keepdims=True), cacc_ref.shape)

        @pl.when(s == n_s - 1)
        def _():
            o_ref[...] = acc_ref[...].astype(o_ref.dtype)
            if colsum:
                cs_ref[...] = cacc_ref[...]

    out_specs = [pl.BlockSpec((None, K, tn), lambda j, s: (j, 0, 0))]
    out_shape = [jax.ShapeDtypeStruct((n_out, K, tn), BF16)]
    scratch = [pltpu.VMEM((K, tn), F32)]
    if colsum:
        out_specs.append(pl.BlockSpec((SUBLANES, tn), lambda j, s: (0, j)))
        out_shape.append(jax.ShapeDtypeStruct((SUBLANES, n_out * tn), F32))
        scratch.append(pltpu.VMEM((SUBLANES, tn), F32))
    res = pl.pallas_call(
        body, grid=(n_out, n_s),
        in_specs=[pl.BlockSpec((ts, K), lambda j, s: (s, 0)), pl.BlockSpec(g_block, g_map)],
        out_specs=out_specs, out_shape=out_shape, scratch_shapes=scratch,
        name=name, compiler_params=_cparams(("parallel", "arbitrary"), vmem_mb))(a, g)
    return res if colsum else res[0]


def _mm_tn_cat(a, gs, *, ts, name, vmem_mb=40):
    S, K = a.shape
    widths = [g.shape[1] for g in gs]
    n_s, total = S // ts, sum(widths)

    def body(*refs):
        a_ref, g_refs = refs[0], refs[1:1 + len(gs)]
        o_ref, cs_ref, acc_ref, cacc_ref = refs[1 + len(gs):]
        s = pl.program_id(0)

        @pl.when(s == 0)
        def _():
            acc_ref[...] = jnp.zeros_like(acc_ref)
            cacc_ref[...] = jnp.zeros_like(cacc_ref)

        av, col = a_ref[...], 0
        for g_ref, w in zip(g_refs, widths):
            gv = g_ref[...]
            acc_ref[:, col:col + w] += _dot_tn(av, gv)
            cacc_ref[:, col:col + w] += jnp.broadcast_to(jnp.sum(gv.astype(F32), axis=0, keepdims=True), (SUBLANES, w))
            col += w

        @pl.when(s == n_s - 1)
        def _():
            o_ref[...] = acc_ref[...].astype(BF16)
            cs_ref[...] = cacc_ref[...]

    return pl.pallas_call(
        body, grid=(n_s,),
        in_specs=[pl.BlockSpec((ts, K), lambda s: (s, 0))] + [pl.BlockSpec((ts, w), lambda s: (s, 0)) for w in widths],
        out_specs=[pl.BlockSpec((K, total), lambda s: (0, 0)), pl.BlockSpec((SUBLANES, total), lambda s: (0, 0))],
        out_shape=[jax.ShapeDtypeStruct((K, total), BF16), jax.ShapeDtypeStruct((SUBLANES, total), F32)],
        scratch_shapes=[pltpu.VMEM((K, total), F32), pltpu.VMEM((SUBLANES, total), F32)],
        name=name, compiler_params=_cparams(("arbitrary",), vmem_mb))(a, *gs)


DILS = tuple(d for _, d in GROUPS if d > 1)


def _res_spec(d, tm, width):
    return pl.BlockSpec((d, tm // d, width), lambda i: (0, i, 0))


def _lane_scratch(tm, width):
    return [pltpu.VMEM((tm, LANES), F32)] * (width // LANES)


def _to_residue(val, dst_refs, dils, tm, dtype, scr):
    for c, ref in enumerate(scr):
        ref[...] = val[:, c * LANES:(c + 1) * LANES]
    for dst_ref, d in zip(dst_refs, dils):
        for r in range(d):
            cols = [ref[pl.ds(r, tm // d, stride=d), :] for ref in scr]
            dst_ref[r] = jnp.concatenate(cols, axis=1).astype(dtype)


def _from_residue(rows_of, d, tm, scr):
    for r in range(d):
        v = rows_of(r).astype(F32)
        for c, ref in enumerate(scr):
            ref[pl.ds(r, tm // d, stride=d), :] = v[:, c * LANES:(c + 1) * LANES]
    return jnp.concatenate([ref[...] for ref in scr], axis=1)


def _ln0_fwd(x, g, b, *, tm=512):
    S, Dm = x.shape

    def body(x_ref, g_ref, b_ref, h_ref, hb_ref, *rest):
        xhat, _ = _ln_stats(x_ref[...])
        h = xhat * g_ref[...] + b_ref[...]
        h_ref[...] = h
        hb_ref[...] = h.astype(BF16)
        _to_residue(h, rest[:len(DILS)], DILS, tm, BF16, rest[len(DILS):])

    row = pl.BlockSpec((tm, Dm), lambda i: (i, 0))
    vec = pl.BlockSpec((1, Dm), lambda i: (0, 0))
    return pl.pallas_call(
        body, grid=(S // tm,), in_specs=[row, vec, vec], out_specs=[row, row] + [_res_spec(d, tm, Dm) for d in DILS],
        out_shape=[jax.ShapeDtypeStruct((S, Dm), F32), jax.ShapeDtypeStruct((S, Dm), BF16)]
        + [jax.ShapeDtypeStruct((d, S // d, Dm), BF16) for d in DILS],
        scratch_shapes=_lane_scratch(tm, Dm),
        name="ln0_fwd", compiler_params=_cparams(("parallel",), 32))(x, g, b)


def _slab_spec(S, col0):
    return pl.BlockSpec((S, SLAB), lambda j: (0, col0 // SLAB + j))


def _zero_pads(scr, S):
    scr[0:PAD, :] = jnp.zeros((PAD, SLAB), F32)
    scr[S + PAD:S + 2 * PAD, :] = jnp.zeros((PAD, SLAB), F32)


def _shifted(scr, t):
    return (scr[PAD - 1 + t:PAD - 1 + t + CHUNK, :], scr[PAD + t:PAD + t + CHUNK, :],
            scr[PAD + 1 + t:PAD + 1 + t + CHUNK, :])


def _conv_gate_fwd(proj, conv_w):
    S = proj.shape[0]

    def body(b_ref, c_ref, h_ref, w_ref, o_ref, u_scr):
        _zero_pads(u_scr, S)
        for t in range(0, S, CHUNK):
            u_scr[PAD + t:PAD + t + CHUNK, :] = c_ref[t:t + CHUNK, :].astype(F32) * h_ref[t:t + CHUNK, :].astype(F32)
        w0, w1, w2 = w_ref[0:1, :], w_ref[1:2, :], w_ref[2:3, :]
        for t in range(0, S, CHUNK):
            um, u0, up = _shifted(u_scr, t)
            cv = w0 * um + w1 * u0 + w2 * up
            o_ref[t:t + CHUNK, :] = (b_ref[t:t + CHUNK, :].astype(F32) * cv).astype(BF16)

    return pl.pallas_call(
        body, grid=(D_CONV // SLAB,),
        in_specs=[_slab_spec(S, P_B), _slab_spec(S, P_C), _slab_spec(S, P_H),
                  pl.BlockSpec((3, SLAB), lambda j: (0, j))],
        out_specs=pl.BlockSpec((S, SLAB), lambda j: (0, j)),
        out_shape=jax.ShapeDtypeStruct((S, D_CONV), BF16),
        scratch_shapes=[pltpu.VMEM((S + 2 * PAD, SLAB), F32)],
        name="conv_gate_fwd", compiler_params=_cparams(("parallel",), 40))(proj, proj, proj, conv_w)


MASKED_DISTANCE = -1e34


def _attn_bias(i, sub, dil):
    j = lax.broadcasted_iota(jnp.int32, (2 * TQ, TQ), 0)
    a = lax.broadcasted_iota(jnp.int32, (2 * TQ, TQ), 1)
    rel = jnp.abs(j - RADIUS - a)
    kpos = i * TQ - RADIUS + j
    valid = (rel <= RADIUS) & (kpos >= 0) & (kpos < sub)
    return jnp.where(valid, -(rel * dil).astype(F32), MASKED_DISTANCE)


def _head_stats(rows):
    pad = jnp.zeros((LANES - len(rows), TQ), F32)
    return jnp.concatenate(list(rows) + [pad], axis=0).T


def _slope(g, h):
    return 2.0 ** (-8.0 * (g * HEADS_PER_GROUP + h + 1) / (N_GROUPS * HEADS_PER_GROUP))


def _window(p_ref, c_ref, n_ref):
    return jnp.concatenate([p_ref[TQ - RADIUS:, :], c_ref[...], n_ref[:RADIUS, :]], axis=0)


def _pair(a, h):
    return a[:, (h // 2) * LANES:(h // 2 + 1) * LANES]


def _own_lanes(a, h):
    lane = lax.broadcasted_iota(jnp.int32, a.shape, 1)
    return jnp.where((lane >= HEAD_DIM) == (h % 2 == 1), a, jnp.zeros_like(a))


def _own_rows(a, h):
    return a[(h % 2) * HEAD_DIM:(h % 2 + 1) * HEAD_DIM, :]


def _qkv_specs(nb, col0):
    def spec(col, shift):
        return pl.BlockSpec((None, TQ, GROUP_W), lambda r, i: (r, jnp.clip(i + shift, 0, nb - 1), col))

    return [spec(col0, 0), spec(col0 + 1, -1), spec(col0 + 1, 0), spec(col0 + 1, 1),
            spec(col0 + 2, -1), spec(col0 + 2, 0), spec(col0 + 2, 1)]


def _attn_fwd(qkv, col0, g):
    dil, sub, _ = qkv.shape
    nb = sub // TQ

    def body(q_ref, kp, kc, kn, vp, vc, vn, o_ref, lse_ref, ot_scr, s_scr, p_scr):
        bias = _attn_bias(pl.program_id(1), sub, dil)
        kwin = _window(kp, kc, kn)
        vwin = _window(vp, vc, vn)
        q = q_ref[...] * ATT_SCALE
        for h in range(HEADS_PER_GROUP):
            s_scr[h] = _dot_nt(_pair(kwin, h), _own_lanes(_pair(q, h), h))
        lse, inv_den = [], []
        for h in range(HEADS_PER_GROUP):
            s = s_scr[h] + _slope(g, h) * bias
            m = jnp.max(s, axis=0, keepdims=True)
            p = jnp.exp(s - m)
            den = jnp.sum(p, axis=0, keepdims=True)
            p_scr[h] = p.astype(BF16)
            inv_den.append(1.0 / den)
            lse.append(m + jnp.log(den))
        for h in range(HEADS_PER_GROUP):
            ot = _dot_tn(_pair(vwin, h), p_scr[h])
            ot_scr[h * HEAD_DIM:(h + 1) * HEAD_DIM, :] = _own_rows(ot, h) * inv_den[h]
        o_ref[...] = ot_scr[...].T
        lse_ref[...] = _head_stats(lse)

    return pl.pallas_call(
        body, grid=(dil, nb), in_specs=_qkv_specs(nb, col0),
        out_specs=[pl.BlockSpec((None, TQ, GROUP_W), lambda r, i: (r, i, 0)),
                   pl.BlockSpec((None, TQ, LANES), lambda r, i: (r, i, 0))],
        out_shape=[jax.ShapeDtypeStruct((dil, sub, GROUP_W), F32), jax.ShapeDtypeStruct((dil, sub, LANES), F32)],
        scratch_shapes=[pltpu.VMEM((GROUP_W, TQ), F32), pltpu.VMEM((HEADS_PER_GROUP, 2 * TQ, TQ), F32),
                        pltpu.VMEM((HEADS_PER_GROUP, 2 * TQ, TQ), BF16)],
        name=f"attn_fwd_g{g}", compiler_params=_cparams(("parallel", "arbitrary"), 32))(*([qkv] * 7))


def _expand_heads():
    h = lax.broadcasted_iota(jnp.int32, (LANES, GROUP_W), 0)
    c = lax.broadcasted_iota(jnp.int32, (LANES, GROUP_W), 1)
    return (c // HEAD_DIM == h).astype(F32)


def _dot_f32(a, b):
    return jnp.dot(a, b, preferred_element_type=F32, precision=lax.Precision.HIGHEST)


def _attn_combine(outs, lses, *, tm=512):
    S = outs[0].shape[1]
    n_col = GROUP_W // LANES

    def body(*refs):
        ins, e_ref = refs[:2 * N_GROUPS], refs[2 * N_GROUPS]
        c_ref, cb_ref, lt_ref = refs[2 * N_GROUPS + 1:2 * N_GROUPS + 4]
        scr = refs[2 * N_GROUPS + 4:]
        o, l = [ins[0][0]], [ins[N_GROUPS][0]]
        for k, d in enumerate(DILS):
            o_ref, l_ref = ins[1 + k], ins[N_GROUPS + 1 + k]
            o.append(_from_residue(lambda r: o_ref[r], d, tm, scr[k * (n_col + 1):k * (n_col + 1) + n_col]))
            l.append(_from_residue(lambda r: l_ref[r], d, tm, scr[k * (n_col + 1) + n_col:(k + 1) * (n_col + 1)]))
        m = jnp.maximum(jnp.maximum(l[0], l[1]), l[2])
        e = [jnp.exp(v - m) for v in l]
        den = e[0] + e[1] + e[2]
        comb = sum(_dot_f32(ev / den, e_ref[...]) * ov for ev, ov in zip(e, o))
        c_ref[...] = comb
        cb_ref[...] = comb.astype(BF16)
        lt_ref[...] = m + jnp.log(den)

    row = pl.BlockSpec((tm, GROUP_W), lambda i: (i, 0))
    dils = [d for _, d in GROUPS]
    return pl.pallas_call(
        body, grid=(S // tm,),
        in_specs=[_res_spec(d, tm, GROUP_W) for d in dils] + [_res_spec(d, tm, LANES) for d in dils]
        + [_resident((LANES, GROUP_W))],
        out_specs=[row, row, pl.BlockSpec((tm, LANES), lambda i: (i, 0))],
        out_shape=[jax.ShapeDtypeStruct((S, GROUP_W), F32), jax.ShapeDtypeStruct((S, GROUP_W), BF16),
                   jax.ShapeDtypeStruct((S, LANES), F32)],
        scratch_shapes=_lane_scratch(tm, GROUP_W + LANES) * len(DILS),
        name="attn_combine", compiler_params=_cparams(("parallel",), 32))(*outs, *lses, _expand_heads())


def _branch_mix(ya_in, comb_b, w_a, w_b, proj, *, tm=512):
    S = ya_in.shape[0]

    def body(ya_ref, cb_ref, wa_ref, wb_ref, ga_ref, gb_ref, yab_ref, mx_ref):
        y_a = _dot(ya_ref[...], wa_ref[...])
        y_b = _dot(cb_ref[...], wb_ref[...])
        yab_ref[:, 0:D_MODEL] = y_a.astype(BF16)
        yab_ref[:, D_MODEL:2 * D_MODEL] = y_b.astype(BF16)
        mx = jax.nn.sigmoid(ga_ref[...].astype(F32)) * y_a + jax.nn.sigmoid(gb_ref[...].astype(F32)) * y_b
        mx_ref[...] = mx.astype(BF16)

    return pl.pallas_call(
        body, grid=(S // tm,),
        in_specs=[pl.BlockSpec((tm, D_CONV), lambda i: (i, 0)), pl.BlockSpec((tm, GROUP_W), lambda i: (i, 0)),
                  pl.BlockSpec((D_CONV, D_MODEL), lambda i: (0, 0)), pl.BlockSpec((GROUP_W, D_MODEL), lambda i: (0, 0)),
                  pl.BlockSpec((tm, D_MODEL), lambda i: (i, P_GA // D_MODEL)),
                  pl.BlockSpec((tm, D_MODEL), lambda i: (i, P_GB // D_MODEL))],
        out_specs=[pl.BlockSpec((tm, 2 * D_MODEL), lambda i: (i, 0)), pl.BlockSpec((tm, D_MODEL), lambda i: (i, 0))],
        out_shape=[jax.ShapeDtypeStruct((S, 2 * D_MODEL), BF16), jax.ShapeDtypeStruct((S, D_MODEL), BF16)],
        name="branch_mix", compiler_params=_cparams(("parallel",), 40))(ya_in, comb_b, w_a, w_b, proj, proj)


def _mix_ln1(mixin, w_o, b_o, h0, g1, b1, *, tm=512):
    S = mixin.shape[0]

    def body(mx_ref, wo_ref, bo_ref, h0_ref, g_ref, b_ref, xh_ref, rs_ref, h1b_ref):
        z = ALPHA * h0_ref[...] + _dot(mx_ref[...], wo_ref[...]) + bo_ref[...]
        xhat, rstd = _ln_stats(z)
        xh_ref[...] = xhat
        rs_ref[...] = jnp.broadcast_to(rstd, (tm, LANES))
        h1b_ref[...] = (xhat * g_ref[...] + b_ref[...]).astype(BF16)

    row = pl.BlockSpec((tm, D_MODEL), lambda i: (i, 0))
    vec = pl.BlockSpec((1, D_MODEL), lambda i: (0, 0))
    return pl.pallas_call(
        body, grid=(S // tm,),
        in_specs=[row, pl.BlockSpec((D_MODEL, D_MODEL), lambda i: (0, 0)), vec, row, vec, vec],
        out_specs=[row, pl.BlockSpec((tm, LANES), lambda i: (i, 0)), row],
        out_shape=[jax.ShapeDtypeStruct((S, D_MODEL), F32), jax.ShapeDtypeStruct((S, LANES), F32),
                   jax.ShapeDtypeStruct((S, D_MODEL), BF16)],
        name="mix_ln1", compiler_params=_cparams(("parallel",), 40))(mixin, w_o, b_o, h0, g1, b1)


def _gelu_parts(cz):
    cdf = 0.5 * (1.0 + lax.erf(cz * INV_SQRT2))
    return cdf, cz * cdf


def _ffn_conv_fwd(up, cw, cb):
    S = up.shape[0]

    def body(a_ref, g_ref, w_ref, cb_ref, o_ref, a_scr):
        _zero_pads(a_scr, S)
        for t in range(0, S, CHUNK):
            a_scr[PAD + t:PAD + t + CHUNK, :] = a_ref[t:t + CHUNK, :].astype(F32)
        w0, w1, w2 = w_ref[0:1, :], w_ref[1:2, :], w_ref[2:3, :]
        for t in range(0, S, CHUNK):
            am, a0, ap = _shifted(a_scr, t)
            _, gel = _gelu_parts(w0 * am + w1 * a0 + w2 * ap + cb_ref[...])
            o_ref[t:t + CHUNK, :] = (gel * g_ref[t:t + CHUNK, :].astype(F32)).astype(BF16)

    return pl.pallas_call(
        body, grid=(D_FF // SLAB,),
        in_specs=[_slab_spec(S, 0), _slab_spec(S, D_FF), pl.BlockSpec((3, SLAB), lambda j: (0, j)),
                  pl.BlockSpec((1, SLAB), lambda j: (0, j))],
        out_specs=pl.BlockSpec((S, SLAB), lambda j: (0, j)),
        out_shape=jax.ShapeDtypeStruct((S, D_FF), BF16),
        scratch_shapes=[pltpu.VMEM((S + 2 * PAD, SLAB), F32)],
        name="ffn_conv_fwd", compiler_params=_cparams(("parallel",), 40))(up, up, cw, cb)


def _down_ln2_loss(f, w_down, b_down, xhat1, g1, b1, g2, b2, target, *, tm=256):
    S = f.shape[0]

    def body(f_ref, wd_ref, bd_ref, xh1_ref, g1_ref, b1_ref, g2_ref, b2_ref, t_ref, dz_ref, dzb_ref, st_ref):
        h1 = xh1_ref[...] * g1_ref[...] + b1_ref[...]
        z = ALPHA * h1 + _dot(f_ref[...], wd_ref[...]) + bd_ref[...]
        xhat, rstd = _ln_stats(z)
        err = xhat * g2_ref[...] + b2_ref[...] - t_ref[...]
        loss = (0.5 / D_MODEL) * jnp.sum(jnp.sum(err * err, axis=1, keepdims=True), axis=0, keepdims=True)
        dh2 = err * (1.0 / D_MODEL)
        dz = _ln_bwd(dh2, xhat, rstd, g2_ref[...])
        dz_ref[...] = dz
        dzb_ref[...] = dz.astype(BF16)
        upd = _rows8([jnp.sum(dh2 * xhat, axis=0, keepdims=True), jnp.sum(dh2, axis=0, keepdims=True),
                      jnp.broadcast_to(loss, (1, D_MODEL)), jnp.sum(dz, axis=0, keepdims=True)], D_MODEL)

        @pl.when(pl.program_id(0) == 0)
        def _():
            st_ref[...] = upd

        @pl.when(pl.program_id(0) != 0)
        def _():
            st_ref[...] += upd

    row = pl.BlockSpec((tm, D_MODEL), lambda i: (i, 0))
    vec = pl.BlockSpec((1, D_MODEL), lambda i: (0, 0))
    return pl.pallas_call(
        body, grid=(S // tm,),
        in_specs=[pl.BlockSpec((tm, D_FF), lambda i: (i, 0)), pl.BlockSpec((D_FF, D_MODEL), lambda i: (0, 0)),
                  vec, row, vec, vec, vec, vec, row],
        out_specs=[row, row, pl.BlockSpec((SUBLANES, D_MODEL), lambda i: (0, 0))],
        out_shape=[jax.ShapeDtypeStruct((S, D_MODEL), F32), jax.ShapeDtypeStruct((S, D_MODEL), BF16),
                   jax.ShapeDtypeStruct((SUBLANES, D_MODEL), F32)],
        name="down_ln2_loss", compiler_params=_cparams(("arbitrary",), 48))(
            f, w_down, b_down, xhat1, g1, b1, g2, b2, target)


def _ffn_conv_bwd(up, df, cw, cb):
    S = up.shape[0]

    def body(a_ref, g_ref, df_ref, w_ref, cb_ref, dup_ref, sm_ref, a_scr, d_scr):
        _zero_pads(a_scr, S)
        _zero_pads(d_scr, S)
        for t in range(0, S, CHUNK):
            a_scr[PAD + t:PAD + t + CHUNK, :] = a_ref[t:t + CHUNK, :].astype(F32)
        w0, w1, w2 = w_ref[0:1, :], w_ref[1:2, :], w_ref[2:3, :]
        zero = jnp.zeros((1, SLAB), F32)
        s_dg, s_dcz, s_w0, s_w1, s_w2 = zero, zero, zero, zero, zero
        for t in range(0, S, CHUNK):
            am, a0, ap = _shifted(a_scr, t)
            cz = w0 * am + w1 * a0 + w2 * ap + cb_ref[...]
            cdf, gel = _gelu_parts(cz)
            dfv = df_ref[t:t + CHUNK, :].astype(F32)
            dgte = dfv * gel
            dcz = dfv * g_ref[t:t + CHUNK, :].astype(F32) * (cdf + cz * jnp.exp(-0.5 * cz * cz) * INV_SQRT_2PI)
            dup_ref[1, t:t + CHUNK, :] = dgte.astype(BF16)
            d_scr[PAD + t:PAD + t + CHUNK, :] = dcz
            s_dg = s_dg + jnp.sum(dgte, axis=0, keepdims=True)
            s_dcz = s_dcz + jnp.sum(dcz, axis=0, keepdims=True)
            s_w0 = s_w0 + jnp.sum(dcz * am, axis=0, keepdims=True)
            s_w1 = s_w1 + jnp.sum(dcz * a0, axis=0, keepdims=True)
            s_w2 = s_w2 + jnp.sum(dcz * ap, axis=0, keepdims=True)
        s_da = zero
        for t in range(0, S, CHUNK):
            dm, d0, dp = _shifted(d_scr, t)
            da = w0 * dp + w1 * d0 + w2 * dm
            dup_ref[0, t:t + CHUNK, :] = da.astype(BF16)
            s_da = s_da + jnp.sum(da, axis=0, keepdims=True)
        sm_ref[...] = _rows8([s_da, s_dg, s_dcz, s_w0, s_w1, s_w2], SLAB)

    return pl.pallas_call(
        body, grid=(D_FF // SLAB,),
        in_specs=[_slab_spec(S, 0), _slab_spec(S, D_FF), pl.BlockSpec((S, SLAB), lambda j: (0, j)),
                  pl.BlockSpec((3, SLAB), lambda j: (0, j)), pl.BlockSpec((1, SLAB), lambda j: (0, j))],
        out_specs=[pl.BlockSpec((2, S, SLAB), lambda j: (0, 0, j)), pl.BlockSpec((SUBLANES, SLAB), lambda j: (0, j))],
        out_shape=[jax.ShapeDtypeStruct((2, S, D_FF), BF16), jax.ShapeDtypeStruct((SUBLANES, D_FF), F32)],
        scratch_shapes=[pltpu.VMEM((S + 2 * PAD, SLAB), F32)] * 2,
        name="ffn_conv_bwd", compiler_params=_cparams(("parallel",), 48))(up, up, df, cw, cb)


def _resident(shape):
    nd = len(shape)
    return pl.BlockSpec(shape, lambda *_: (0,) * nd, pipeline_mode=pl.Buffered(1))


def _up_bwd_ln1(dup, w_up3, dz2, xhat1, rstd1, g1, *, tm=256):
    S = dz2.shape[0]
    ns, _, tk = w_up3.shape
    per_plane = D_FF // tk

    def body(du_ref, w_ref, dz2_ref, xh_ref, rs_ref, g_ref, dz_ref, dzb_ref, st_ref):
        dh = ALPHA * dz2_ref[...]
        for k in range(ns):
            col = (k % per_plane) * tk
            dh = dh + _dot_nt(du_ref[k // per_plane, :, col:col + tk], w_ref[k])
        xhat = xh_ref[...]
        dz = _ln_bwd(dh, xhat, rs_ref[:, 0:1], g_ref[...])
        dz_ref[...] = dz
        dzb_ref[...] = dz.astype(BF16)
        upd = _rows8([jnp.sum(dh * xhat, axis=0, keepdims=True), jnp.sum(dh, axis=0, keepdims=True),
                      jnp.sum(dz, axis=0, keepdims=True)], D_MODEL)

        @pl.when(pl.program_id(0) == 0)
        def _():
            st_ref[...] = upd

        @pl.when(pl.program_id(0) != 0)
        def _():
            st_ref[...] += upd

    row = pl.BlockSpec((tm, D_MODEL), lambda i: (i, 0))
    return pl.pallas_call(
        body, grid=(S // tm,),
        in_specs=[pl.BlockSpec((dup.shape[0], tm, D_FF), lambda i: (0, i, 0)), _resident(w_up3.shape),
                  row, row, pl.BlockSpec((tm, LANES), lambda i: (i, 0)), pl.BlockSpec((1, D_MODEL), lambda i: (0, 0))],
        out_specs=[row, row, pl.BlockSpec((SUBLANES, D_MODEL), lambda i: (0, 0))],
        out_shape=[jax.ShapeDtypeStruct((S, D_MODEL), F32), jax.ShapeDtypeStruct((S, D_MODEL), BF16),
                   jax.ShapeDtypeStruct((SUBLANES, D_MODEL), F32)],
        name="up_bwd_ln1", compiler_params=_cparams(("arbitrary",), 48))(dup, w_up3, dz2, xhat1, rstd1, g1)


def _mix_bwd(dz1b, w_o, proj, yab, *, tm=512):
    S = dz1b.shape[0]

    def body(dz_ref, wo_ref, ga_ref, gb_ref, y_ref, dy_ref, dg_ref):
        dmx = _dot_nt(dz_ref[...], wo_ref[...])
        for k, gt_ref in enumerate((ga_ref, gb_ref)):
            sl = slice(k * D_MODEL, (k + 1) * D_MODEL)
            sg = jax.nn.sigmoid(gt_ref[...].astype(F32))
            dy_ref[:, sl] = (dmx * sg).astype(BF16)
            dg_ref[k] = (dmx * y_ref[:, sl].astype(F32) * sg * (1.0 - sg)).astype(BF16)

    row = pl.BlockSpec((tm, D_MODEL), lambda i: (i, 0))
    wide = pl.BlockSpec((tm, 2 * D_MODEL), lambda i: (i, 0))
    return pl.pallas_call(
        body, grid=(S // tm,),
        in_specs=[row, _resident(w_o.shape), pl.BlockSpec((tm, D_MODEL), lambda i: (i, P_GA // D_MODEL)),
                  pl.BlockSpec((tm, D_MODEL), lambda i: (i, P_GB // D_MODEL)), wide],
        out_specs=[wide, pl.BlockSpec((2, tm, D_MODEL), lambda i: (0, i, 0))],
        out_shape=[jax.ShapeDtypeStruct((S, 2 * D_MODEL), BF16), jax.ShapeDtypeStruct((2, S, D_MODEL), BF16)],
        name="mix_bwd", compiler_params=_cparams(("parallel",), 40))(dz1b, w_o, proj, proj, yab)


def _conv_gate_bwd(proj, dya_in, conv_w):
    S = proj.shape[0]

    def body(b_ref, c_ref, h_ref, dy_ref, w_ref, o_ref, sm_ref, u_scr, d_scr):
        _zero_pads(u_scr, S)
        _zero_pads(d_scr, S)
        for t in range(0, S, CHUNK):
            u_scr[PAD + t:PAD + t + CHUNK, :] = c_ref[t:t + CHUNK, :].astype(F32) * h_ref[t:t + CHUNK, :].astype(F32)
        w0, w1, w2 = w_ref[0:1, :], w_ref[1:2, :], w_ref[2:3, :]
        zero = jnp.zeros((1, SLAB), F32)
        s_w0, s_w1, s_w2 = zero, zero, zero
        for t in range(0, S, CHUNK):
            um, u0, up = _shifted(u_scr, t)
            dy = dy_ref[t:t + CHUNK, :].astype(F32)
            o_ref[0, t:t + CHUNK, :] = (dy * (w0 * um + w1 * u0 + w2 * up)).astype(BF16)
            dcv = dy * b_ref[t:t + CHUNK, :].astype(F32)
            d_scr[PAD + t:PAD + t + CHUNK, :] = dcv
            s_w0 = s_w0 + jnp.sum(dcv * um, axis=0, keepdims=True)
            s_w1 = s_w1 + jnp.sum(dcv * u0, axis=0, keepdims=True)
            s_w2 = s_w2 + jnp.sum(dcv * up, axis=0, keepdims=True)
        for t in range(0, S, CHUNK):
            dm, d0, dp = _shifted(d_scr, t)
            du = w0 * dp + w1 * d0 + w2 * dm
            o_ref[1, t:t + CHUNK, :] = (du * h_ref[t:t + CHUNK, :].astype(F32)).astype(BF16)
            o_ref[2, t:t + CHUNK, :] = (du * c_ref[t:t + CHUNK, :].astype(F32)).astype(BF16)
        sm_ref[...] = _rows8([s_w0, s_w1, s_w2], SLAB)

    return pl.pallas_call(
        body, grid=(D_CONV // SLAB,),
        in_specs=[_slab_spec(S, P_B), _slab_spec(S, P_C), _slab_spec(S, P_H),
                  pl.BlockSpec((S, SLAB), lambda j: (0, j)), pl.BlockSpec((3, SLAB), lambda j: (0, j))],
        out_specs=[pl.BlockSpec((3, S, SLAB), lambda j: (0, 0, j)), pl.BlockSpec((SUBLANES, SLAB), lambda j: (0, j))],
        out_shape=[jax.ShapeDtypeStruct((3, S, D_CONV), BF16), jax.ShapeDtypeStruct((SUBLANES, D_CONV), F32)],
        scratch_shapes=[pltpu.VMEM((S + 2 * PAD, SLAB), F32)] * 2,
        name="conv_gate_bwd", compiler_params=_cparams(("parallel",), 48))(proj, proj, proj, dya_in, conv_w)


def _comb_bwd(dyab, w_b, comb, lse_tot, *, tm=512):
    S = comb.shape[0]
    widths, dtypes = (GROUP_W, LANES, LANES), (BF16, F32, F32)

    def body(dy_ref, wb_ref, c_ref, lt_ref, e_ref, *rest):
        outs, scr = rest[:3 * N_GROUPS], rest[3 * N_GROUPS:]
        dcb = _dot_nt(dy_ref[...], wb_ref[...]).astype(BF16)
        dc = dcb.astype(F32)
        delta = lax.dot_general(dc * c_ref[...], e_ref[...], (((1,), (1,)), ((), ())),
                                preferred_element_type=F32, precision=lax.Precision.HIGHEST)
        for k, (val, dtype) in enumerate(zip((dc, lt_ref[...], delta), dtypes)):
            outs[k][0] = val.astype(dtype)
            _to_residue(val, [outs[3 * (1 + j) + k] for j in range(len(DILS))], DILS, tm, dtype,
                        scr[:val.shape[1] // LANES])

    out_specs, out_shape = [], []
    for _, d in GROUPS:
        out_specs += [_res_spec(d, tm, w) for w in widths]
        out_shape += [jax.ShapeDtypeStruct((d, S // d, w), t) for w, t in zip(widths, dtypes)]
    res = pl.pallas_call(
        body, grid=(S // tm,),
        in_specs=[pl.BlockSpec((tm, D_MODEL), lambda i: (i, 1)), _resident(w_b.shape),
                  pl.BlockSpec((tm, GROUP_W), lambda i: (i, 0)), pl.BlockSpec((tm, LANES), lambda i: (i, 0)),
                  _resident((LANES, GROUP_W))],
        out_specs=out_specs, out_shape=out_shape, scratch_shapes=_lane_scratch(tm, GROUP_W),
        name="comb_bwd", compiler_params=_cparams(("parallel",), 32))(dyab, w_b, comb, lse_tot, _expand_heads())
    return [tuple(res[3 * g:3 * g + 3]) for g in range(N_GROUPS)]


def _attn_bwd(qkv, col0, g, dcomb, lse_tot, delta):
    dil, sub, _ = qkv.shape
    nb = sub // TQ

    def body(q_ref, kp, kc, kn, vp, vc, vn, do_ref, lse_ref, dl_ref, dq_ref, dk_ref, dv_ref,
             ak, av, dqt_scr, s_scr, dp_scr, ds_scr, p_scr):
        i = pl.program_id(1)

        @pl.when(i == 0)
        def _():
            ak[...] = jnp.zeros_like(ak)
            av[...] = jnp.zeros_like(av)

        @pl.when(i < nb)
        def _():
            bias = _attn_bias(i, sub, dil)
            kwin = _window(kp, kc, kn)
            vwin = _window(vp, vc, vn)
            q = q_ref[...] * ATT_SCALE
            do = do_ref[...]
            lse_t, dl_t = lse_ref[...].T, dl_ref[...].T
            for h in range(HEADS_PER_GROUP):
                s_scr[h] = _dot_nt(_pair(kwin, h), _own_lanes(_pair(q, h), h))
                dp_scr[h] = _dot_nt(_pair(vwin, h), _own_lanes(_pair(do, h), h))
            for h in range(HEADS_PER_GROUP):
                p = jnp.exp(s_scr[h] + _slope(g, h) * bias - lse_t[h:h + 1, :])
                ds_scr[h] = (p * (dp_scr[h] - dl_t[h:h + 1, :])).astype(BF16)
                p_scr[h] = p.astype(BF16)
            for h in range(HEADS_PER_GROUP):
                dqt_scr[h * HEAD_DIM:(h + 1) * HEAD_DIM, :] = _own_rows(_dot_tn(_pair(kwin, h), ds_scr[h]), h)
            for h in range(0, HEADS_PER_GROUP, 2):
                cols = slice(h * HEAD_DIM, (h + 2) * HEAD_DIM)
                q2 = jnp.concatenate([_own_lanes(_pair(q, h), h), _own_lanes(_pair(q, h), h + 1)], axis=0)
                do2 = jnp.concatenate([_own_lanes(_pair(do, h), h), _own_lanes(_pair(do, h), h + 1)], axis=0)
                ak[RADIUS:RADIUS + 2 * TQ, cols] += _dot(jnp.concatenate([ds_scr[h], ds_scr[h + 1]], axis=1), q2)
                av[RADIUS:RADIUS + 2 * TQ, cols] += _dot(jnp.concatenate([p_scr[h], p_scr[h + 1]], axis=1), do2)
            dq_ref[...] = (dqt_scr[...].T * ATT_SCALE).astype(BF16)

        dk_ref[...] = ak[0:TQ, :].astype(BF16)
        dv_ref[...] = av[0:TQ, :].astype(BF16)
        ak[0:2 * TQ, :] = ak[TQ:3 * TQ, :]
        av[0:2 * TQ, :] = av[TQ:3 * TQ, :]
        ak[2 * TQ:3 * TQ, :] = jnp.zeros((TQ, GROUP_W), F32)
        av[2 * TQ:3 * TQ, :] = jnp.zeros((TQ, GROUP_W), F32)

    tok = pl.BlockSpec((None, TQ, GROUP_W), lambda r, i: (r, jnp.minimum(i, nb - 1), 0))
    stat = pl.BlockSpec((None, TQ, LANES), lambda r, i: (r, jnp.minimum(i, nb - 1), 0))
    dkv_spec = pl.BlockSpec((None, TQ, GROUP_W), lambda r, i: (r, jnp.maximum(i - 1, 0), 0))
    return pl.pallas_call(
        body, grid=(dil, nb + 1), in_specs=_qkv_specs(nb, col0) + [tok, stat, stat],
        out_specs=[tok, dkv_spec, dkv_spec], out_shape=[jax.ShapeDtypeStruct((dil, sub, GROUP_W), BF16)] * 3,
        scratch_shapes=[pltpu.VMEM((3 * TQ, GROUP_W), F32)] * 2 + [pltpu.VMEM((GROUP_W, TQ), F32)]
        + [pltpu.VMEM((HEADS_PER_GROUP, 2 * TQ, TQ), F32)] * 2 + [pltpu.VMEM((HEADS_PER_GROUP, 2 * TQ, TQ), BF16)] * 2,
        name=f"attn_bwd_g{g}", compiler_params=_cparams(("arbitrary", "arbitrary"), 32))(
            *([qkv] * 7), dcomb, lse_tot, delta)


def _in_bwd_ln0(dgated, dqkv, w_nat, w_dil, dz1, x, g0, *, tm=256):
    S = x.shape[0]
    n_gated, n_in = len(dgated), 3 * N_GROUPS

    def body(*refs):
        g_refs, d_refs = refs[:n_gated], refs[n_gated:n_gated + n_in]
        wn_ref, *wd_refs = refs[n_gated + n_in:n_gated + n_in + N_GROUPS]
        dz_ref, x_ref, g_ref, gx_ref, st_ref, *tmp_ref = refs[n_gated + n_in + N_GROUPS:]
        dh = ALPHA * dz_ref[...]
        col = 0
        for ref in g_refs:
            for k in range(ref.shape[0]):
                dh = dh + _dot_nt(ref[k], wn_ref[:, col:col + D_MODEL])
                col += D_MODEL
        for g, (_, d) in enumerate(GROUPS):
            rows = [jnp.concatenate([d_refs[3 * g + k][r] for k in range(3)], axis=1) for r in range(d)]
            w = wn_ref[:, col:col + QKV_W] if d == 1 else wd_refs[g - 1][...]
            res = _dot_nt(jnp.concatenate(rows, axis=0), w)
            if d == 1:
                dh = dh + res
            else:
                n = tm // d
                dh = dh + _from_residue(lambda r: res[r * n:(r + 1) * n, :], d, tm, tmp_ref)
        xhat, rstd = _ln_stats(x_ref[...])
        gx_ref[...] = _ln_bwd(dh, xhat, rstd, g_ref[...])
        upd = _rows8([jnp.sum(dh * xhat, axis=0, keepdims=True), jnp.sum(dh, axis=0, keepdims=True)], D_MODEL)

        @pl.when(pl.program_id(0) == 0)
        def _():
            st_ref[...] = upd

        @pl.when(pl.program_id(0) != 0)
        def _():
            st_ref[...] += upd

    row = pl.BlockSpec((tm, D_MODEL), lambda i: (i, 0))
    g_specs = [pl.BlockSpec((a.shape[0], tm, D_MODEL), lambda i: (0, i, 0)) for a in dgated]
    d_specs = []
    for _, d in GROUPS:
        d_specs += [_res_spec(d, tm, GROUP_W)] * 3
    operands = list(dgated) + [a for grp in dqkv for a in grp] + [w_nat] + list(w_dil) + [dz1, x, g0]
    return pl.pallas_call(
        body, grid=(S // tm,),
        in_specs=g_specs + d_specs + [_resident(w_nat.shape)] + [_resident(w.shape) for w in w_dil]
        + [row, row, pl.BlockSpec((1, D_MODEL), lambda i: (0, 0))],
        out_specs=[row, pl.BlockSpec((SUBLANES, D_MODEL), lambda i: (0, 0))],
        out_shape=[jax.ShapeDtypeStruct((S, D_MODEL), F32), jax.ShapeDtypeStruct((SUBLANES, D_MODEL), F32)],
        scratch_shapes=_lane_scratch(tm, D_MODEL),
        name="in_bwd_ln0", compiler_params=_cparams(("arbitrary",), 52))(*operands)


HBM_SPEC = pl.BlockSpec(memory_space=pltpu.HBM)


def _place():
    x, y, c = lax.axis_index("x"), lax.axis_index("y"), lax.axis_index("c")
    chips = [(1 - x, y), (x, 1 - y), (1 - x, 1 - y)]
    return x, y, c, chips


def _allgather_shards(shards, after, *, name, collective_id):
    n = len(shards)
    per = 6

    def body(*refs):
        ins, outs = refs[:n], refs[n + len(after):2 * n + len(after)]
        send_sems, recv_sems, loc_sems = refs[2 * n + len(after):]
        x, y, c, chips = _place()
        me = 2 * x + y
        sib = (x, y, 1 - c)
        peers = [sib] + [(px, py, c) for px, py in chips]
        barrier = pltpu.get_barrier_semaphore()
        for peer in peers:
            pl.semaphore_signal(barrier, inc=1, device_id=peer, device_id_type=MESH)
        pl.semaphore_wait(barrier, len(peers))

        def rcopy(w, k, src, dst, to):
            return pltpu.make_async_remote_copy(src_ref=src, dst_ref=dst, send_sem=send_sems.at[per * w + k],
                                                recv_sem=recv_sems.at[per * w + k], device_id=to, device_id_type=MESH)

        split = [s.shape[0] == N_CORES for s in shards]
        half = lambda w: c if split[w] else 0
        local, sends = [], []
        for w in range(n):
            cp = pltpu.make_async_copy(ins[w], outs[w].at[me], loc_sems.at[w])
            cp.start()
            local.append(cp)
            for j, (px, py) in enumerate(chips):
                cp = rcopy(w, j, ins[w].at[half(w)], outs[w].at[me, half(w)], (px, py, c))
                cp.start()
                sends.append(cp)
        for w in range(n):
            for j, (px, py) in enumerate(chips):
                slot = outs[w].at[2 * px + py, half(w)]
                rcopy(w, j, slot, slot, (px, py, c)).wait_recv()
                if split[w]:
                    cp = rcopy(w, 3 + j, slot, slot, sib)
                    cp.start()
                    sends.append(cp)
        for w in range(n):
            if split[w]:
                for j, (px, py) in enumerate(chips):
                    slot = outs[w].at[2 * px + py, 1 - c]
                    rcopy(w, 3 + j, slot, slot, sib).wait_recv()
        for cp in sends:
            cp.wait_send()
        for cp in local:
            cp.wait()

    return pl.kernel(
        body, out_type=[jax.ShapeDtypeStruct((N_CHIPS,) + s.shape, s.dtype) for s in shards],
        mesh=plsc.ScalarSubcoreMesh(axis_name="sequencer", num_cores=1),
        scratch_types=[pltpu.SemaphoreType.DMA((per * n,)), pltpu.SemaphoreType.DMA((per * n,)),
                       pltpu.SemaphoreType.DMA((n,))],
        name=name, compiler_params=pltpu.CompilerParams(collective_id=collective_id))(*shards, *after)


def _exchange_grads(grads, *, name, collective_id):
    n = len(grads)
    per = 7

    def body(*refs):
        ins, outs = refs[:n], refs[n:2 * n]
        send_sems, recv_sems, loc_sems = refs[2 * n:]
        x, y, c, chips = _place()
        me = 2 * x + y
        sib = (x, y, 1 - c)
        peers = [sib] + [(px, py, c) for px, py in chips]
        barrier = pltpu.get_barrier_semaphore()
        for peer in peers:
            pl.semaphore_signal(barrier, inc=1, device_id=peer, device_id_type=MESH)
        pl.semaphore_wait(barrier, len(peers))

        def rcopy(w, k, src, dst, to):
            return pltpu.make_async_remote_copy(src_ref=src, dst_ref=dst, send_sem=send_sems.at[per * w + k],
                                                recv_sem=recv_sems.at[per * w + k], device_id=to, device_id_type=MESH)

        local, sends = [], []
        for w in range(n):
            cp = pltpu.make_async_copy(ins[w].at[me], outs[w].at[c, me], loc_sems.at[w])
            cp.start()
            local.append(cp)
            cp = rcopy(w, 0, ins[w].at[me], outs[w].at[c, me], sib)
            cp.start()
            sends.append(cp)
            for j, (px, py) in enumerate(chips):
                cp = rcopy(w, 1 + j, ins[w].at[2 * px + py], outs[w].at[c, me], (px, py, c))
                cp.start()
                sends.append(cp)
        for w in range(n):
            for j, (px, py) in enumerate(chips):
                slot = outs[w].at[c, 2 * px + py]
                rcopy(w, 1 + j, slot, slot, (px, py, c)).wait_recv()
                cp = rcopy(w, 4 + j, slot, slot, sib)
                cp.start()
                sends.append(cp)
        for w in range(n):
            slot = outs[w].at[1 - c, me]
            rcopy(w, 0, slot, slot, sib).wait_recv()
            for j, (px, py) in enumerate(chips):
                slot = outs[w].at[1 - c, 2 * px + py]
                rcopy(w, 4 + j, slot, slot, sib).wait_recv()
        for cp in sends:
            cp.wait_send()
        for cp in local:
            cp.wait()

    return pl.kernel(
        body, out_type=[jax.ShapeDtypeStruct((N_CORES,) + g.shape, g.dtype) for g in grads],
        mesh=plsc.ScalarSubcoreMesh(axis_name="sequencer", num_cores=1),
        scratch_types=[pltpu.SemaphoreType.DMA((per * n,)), pltpu.SemaphoreType.DMA((per * n,)),
                       pltpu.SemaphoreType.DMA((n,))],
        name=name, compiler_params=pltpu.CompilerParams(collective_id=collective_id))(*grads)


def _allgather_small(vec):
    def body(v_ref, o_ref, send_sems, recv_sems, loc_sem):
        x, y, c = lax.axis_index("x"), lax.axis_index("y"), lax.axis_index("c")
        me = 4 * x + 2 * y + c

        def peer(k):
            flip = lambda v, bit: 1 - v if (k >> bit) & 1 else v
            return flip(x, 2), flip(y, 1), flip(c, 0)

        loc = pltpu.make_async_copy(v_ref, o_ref.at[me], loc_sem)
        loc.start()
        sends = []
        for k in range(1, N_DEV):
            cp = pltpu.make_async_remote_copy(src_ref=v_ref, dst_ref=o_ref.at[me], send_sem=send_sems.at[k - 1],
                                              recv_sem=recv_sems.at[k - 1], device_id=peer(k), device_id_type=MESH)
            cp.start()
            sends.append(cp)
        for k in range(1, N_DEV):
            px, py, pc = peer(k)
            pltpu.make_async_remote_copy(src_ref=v_ref, dst_ref=o_ref.at[4 * px + 2 * py + pc],
                                         send_sem=send_sems.at[k - 1], recv_sem=recv_sems.at[k - 1],
                                         device_id=(px, py, pc), device_id_type=MESH).wait_recv()
        for cp in sends:
            cp.wait_send()
        loc.wait()

    return pl.pallas_call(
        body, in_specs=[HBM_SPEC], out_specs=HBM_SPEC,
        out_shape=jax.ShapeDtypeStruct((N_DEV,) + vec.shape, vec.dtype),
        scratch_shapes=[pltpu.SemaphoreType.DMA((N_DEV - 1,)), pltpu.SemaphoreType.DMA((N_DEV - 1,)),
                        pltpu.SemaphoreType.DMA],
        name="allgather_small")(vec)


def _adamw(w, g, m, v):
    m = ADAM_B1 * m + (1.0 - ADAM_B1) * g
    v = ADAM_B2 * v + (1.0 - ADAM_B2) * (g * g)
    m_hat = m / (1.0 - ADAM_B1 ** ADAM_STEP)
    v_hat = v / (1.0 - ADAM_B2 ** ADAM_STEP)
    delta = -ADAM_LR * (m_hat / (jnp.sqrt(v_hat) + ADAM_EPS) + ADAM_WD * w)
    return delta, m, v


def _reduce_adamw(parts, w, m, v, *, tr, name):
    R, C = w.shape

    def body(p_ref, w_ref, m_ref, v_ref, g_ref, d_ref, nm_ref, nv_ref):
        def core_sum(cc):
            s = p_ref[cc, 0].astype(F32)
            for k in range(1, N_CHIPS):
                s = s + p_ref[cc, k].astype(F32)
            return s

        g = core_sum(0) + core_sum(1)
        delta, nm, nv = _adamw(w_ref[...], g, m_ref[...], v_ref[...])
        g_ref[...] = g
        d_ref[...] = delta
        nm_ref[...] = nm
        nv_ref[...] = nv

    blk = pl.BlockSpec((tr, C), lambda i: (i, 0))
    return pl.pallas_call(
        body, grid=(R // tr,),
        in_specs=[pl.BlockSpec((N_CORES, N_CHIPS, tr, C), lambda i: (0, 0, i, 0)), blk, blk, blk],
        out_specs=[blk] * 4, out_shape=[jax.ShapeDtypeStruct((R, C), F32)] * 4,
        name=name, compiler_params=_cparams(("parallel",), 40))(parts, w, m, v)


def _reduce_adamw_vectors(allv, offs, ws, ms, vs):
    n = len(ws)

    def body(a_ref, *refs):
        w_refs, m_refs, v_refs = refs[:n], refs[n:2 * n], refs[2 * n:3 * n]
        tot_ref, outs = refs[3 * n], refs[3 * n + 1:]
        s = a_ref[0]
        for d in range(1, N_DEV):
            s = s + a_ref[d]
        tot_ref[...] = s
        for k in range(n):
            g = s[:, offs[k]:offs[k] + w_refs[k].shape[1]]
            delta, nm, nv = _adamw(w_refs[k][...], g, m_refs[k][...], v_refs[k][...])
            for ref, val in zip(outs[4 * k:4 * k + 4], (g, delta, nm, nv)):
                ref[...] = val

    out_shape = [jax.ShapeDtypeStruct(allv.shape[1:], F32)]
    for w in ws:
        out_shape += [jax.ShapeDtypeStruct(w.shape, F32)] * 4
    res = pl.pallas_call(body, out_shape=out_shape, name="reduce_adamw_vectors",
                         compiler_params=_cparams((), 40))(allv, *ws, *ms, *vs)
    return res[0], [tuple(res[1 + 4 * k:5 + 4 * k]) for k in range(n)]


def _adamw_taps(ws, gs, ms, vs):
    n = len(ws)

    def body(*refs):
        outs = refs[4 * n:]
        for k in range(n):
            res = _adamw(refs[k][...], refs[n + k][...], refs[2 * n + k][...], refs[3 * n + k][...])
            for ref, val in zip(outs[3 * k:3 * k + 3], res):
                ref[...] = val

    out_shape = []
    for w in ws:
        out_shape += [jax.ShapeDtypeStruct(w.shape, F32)] * 3
    res = pl.pallas_call(body, out_shape=out_shape, name="adamw_taps")(*ws, *gs, *ms, *vs)
    return [tuple(res[3 * k:3 * k + 3]) for k in range(n)]


def _pack(pieces):
    flat, offs, n = [], [], 0
    for p in pieces:
        size = -(-p.size // LANES) * LANES
        flat.append(jnp.pad(p.reshape(-1), (0, size - p.size)))
        offs.append(n)
        n += size
    return jnp.concatenate(flat).reshape(1, n), offs


def _local_step(x, target, p, wfull, on_ready=lambda group: None):
    S = x.shape[0]
    dils = [d for _, d in GROUPS]

    h0, h0b, *h0_res = _ln0_fwd(x, p["ln0_g"], p["ln0_b"])
    h0_rows = [h0b] + [h.reshape(S, D_MODEL) for h in h0_res]

    if callable(wfull):
        wfull = wfull(h0b)
    w_in3, w_up3 = wfull["w_in"], wfull["w_up"]
    w_a, w_o, w_down, w_b = wfull["w_a"], wfull["w_o"], wfull["w_down"], wfull["w_b"]
    conv_w, ffn_conv_w = wfull["conv_w"], wfull["ffn_conv_w"]

    w_blocks = w_in3.transpose(1, 0, 2).reshape(D_MODEL, N_BLK, GROUP_W)
    w_perm = jnp.concatenate([w_blocks[:, b] for b in PERM], axis=1)
    b_blocks = p["b_in"].reshape(N_BLK, GROUP_W)
    b_perm = jnp.concatenate([b_blocks[b] for b in PERM]).reshape(1, N_IN)
    w_nat, b_nat = w_perm[:, :N_NAT], b_perm[:, :N_NAT]
    qkv_cols = [slice(P_Q0 + g * QKV_W, P_Q0 + (g + 1) * QKV_W) for g in range(N_GROUPS)]
    w_qkv = [w_perm[:, c] for c in qkv_cols]

    proj = _mm_nn(h0b, w_nat, b_nat, tm=512, tn=N_NAT // 2, out_dtype=BF16, name="proj")
    qkv = [proj[None]]
    for g in range(1, N_GROUPS):
        t = _mm_nn(h0_rows[g], w_qkv[g], b_perm[:, qkv_cols[g]], tm=512, tn=QKV_W, out_dtype=BF16, name=f"proj_qkv{g}")
        qkv.append(t.reshape(dils[g], S // dils[g], QKV_W))
    col0 = [P_Q0 // GROUP_W] + [0] * (N_GROUPS - 1)
    ya_in = _conv_gate_fwd(proj, conv_w)
    att = [_attn_fwd(qkv[g], col0[g], g) for g in range(N_GROUPS)]
    comb, comb_b, lse_tot = _attn_combine([a[0] for a in att], [a[1] for a in att])
    yab, mixin = _branch_mix(ya_in, comb_b, w_a, w_b, proj)
    xhat1, rstd1, h1b = _mix_ln1(mixin, w_o, p["b_o"], h0, p["ln1_g"], p["ln1_b"])
    up = _mm_nn(h1b, w_up3, p["b_up"], tm=512, tn=w_up3.shape[2], out_dtype=BF16, name="up")
    f = _ffn_conv_fwd(up, ffn_conv_w, p["ffn_conv_b"])
    dz2, dz2b, st2 = _down_ln2_loss(f, w_down, p["b_down"], xhat1, p["ln1_g"], p["ln1_b"],
                                    p["ln2_g"], p["ln2_b"], target)

    gw = {}
    gw["w_down"] = _mm_tn(f, dz2b, n_out=1, tn=D_MODEL, ts=1024, g_block=(1024, D_MODEL),
                          g_map=lambda j, s: (s, 0), name="grad_w_down").reshape(N_CHIPS, D_FF // N_CHIPS, D_MODEL)
    df = _mm_nt(dz2b, w_down, tm=512, name="df")
    dup, sm_ffn = _ffn_conv_bwd(up, df, ffn_conv_w, p["ffn_conv_b"])
    up_tn = w_up3.shape[2]
    up_pp = D_FF // up_tn
    gw["w_up"] = _mm_tn(h1b, dup, n_out=N_CHIPS, tn=up_tn, ts=1024, g_block=(None, 1024, up_tn),
                        g_map=lambda j, s: (j // up_pp, s, j % up_pp), name="grad_w_up")
    on_ready({n: gw[n] for n in ("w_down", "w_up")})
    dz1, dz1b, st1 = _up_bwd_ln1(dup, w_up3, dz2, xhat1, rstd1, p["ln1_g"])

    gw["w_o"] = _mm_tn(mixin, dz1b, n_out=1, tn=D_MODEL, ts=512, g_block=(512, D_MODEL),
                       g_map=lambda j, s: (s, 0), name="grad_w_o").reshape(N_CHIPS, D_MODEL // N_CHIPS, D_MODEL)
    dyab, dgab = _mix_bwd(dz1b, w_o, proj, yab)
    gw["w_a"] = _mm_tn(ya_in, dyab, n_out=1, tn=D_MODEL, ts=512, g_block=(512, D_MODEL),
                       g_map=lambda j, s: (s, 0), name="grad_w_a").reshape(N_CHIPS, D_CONV // N_CHIPS, D_MODEL)
    gw_b = _mm_tn(comb_b, dyab, n_out=1, tn=D_MODEL, ts=1024, g_block=(1024, D_MODEL),
                  g_map=lambda j, s: (s, 1), name="grad_w_b")
    gw["w_b"] = gw_b.reshape(GROUP_W, N_CHIPS, D_MODEL // N_CHIPS).transpose(1, 0, 2)
    on_ready({n: gw[n] for n in ("w_o", "w_a", "w_b")})
    dya_in = _mm_nt(dyab, w_a, tm=512, a_col=0, name="dya_in")
    dbch, sm_conv = _conv_gate_bwd(proj, dya_in, conv_w)
    att_stats = _comb_bwd(dyab, w_b, comb, lse_tot)
    dqkv = [_attn_bwd(qkv[g], col0[g], g, *att_stats[g]) for g in range(N_GROUPS)]

    w_pieces, b_pieces = [], []
    for nm, planes in (("bch", dbch), ("gab", dgab)):
        pw, pc = _mm_tn(h0b, planes, n_out=planes.shape[0], tn=D_MODEL, ts=1024, g_block=(None, 1024, D_MODEL),
                        g_map=lambda j, s: (j, s, 0), colsum=True, name="grad_w_in_" + nm)
        w_pieces.append(pw.transpose(1, 0, 2).reshape(D_MODEL, planes.shape[0] * D_MODEL))
        b_pieces.append(pc[0])
    for g in range(N_GROUPS):
        pw, pc = _mm_tn_cat(h0_rows[g], [a.reshape(S, GROUP_W) for a in dqkv[g]], ts=1024, name=f"grad_w_in_qkv{g}")
        w_pieces.append(pw)
        b_pieces.append(pc[0])
    dw_blocks = jnp.concatenate(w_pieces, axis=1).reshape(D_MODEL, N_BLK, GROUP_W)
    dw_ref = jnp.concatenate([dw_blocks[:, b] for b in INV_PERM], axis=1)
    gw["w_in"] = dw_ref.reshape(D_MODEL, N_CHIPS, N_IN // N_CHIPS).transpose(1, 0, 2)
    on_ready({"w_in": gw["w_in"]})
    db_blocks = jnp.concatenate(b_pieces).reshape(N_BLK, GROUP_W)
    grad_b_in = jnp.concatenate([db_blocks[b] for b in INV_PERM])

    grad_x, st0 = _in_bwd_ln0([dbch, dgab], dqkv, w_nat, w_qkv[1:], dz1, x, p["ln0_g"])

    small = {
        "loss": st2[2:3, 0:1],
        "ln0_g": st0[0], "ln0_b": st0[1], "b_in": grad_b_in, "conv_w": sm_conv[0:3],
        "b_o": st1[2], "ln1_g": st1[0], "ln1_b": st1[1],
        "b_up": jnp.concatenate([sm_ffn[0], sm_ffn[1]]), "ffn_conv_w": sm_ffn[3:6], "ffn_conv_b": sm_ffn[2],
        "b_down": st2[3], "ln2_g": st2[0], "ln2_b": st2[1],
    }
    return grad_x, gw, small


BIG = ("w_in", "w_a", "w_b", "w_o", "w_up", "w_down")
CONV = ("conv_w", "ffn_conv_w")
VECS = ("ln0_g", "ln0_b", "b_in", "b_o", "ln1_g", "ln1_b", "b_up", "ffn_conv_b", "b_down", "ln2_g", "ln2_b")
ORDER = ("ln0_g", "ln0_b", "w_in", "b_in", "conv_w", "w_a", "w_b", "w_o", "b_o", "ln1_g", "ln1_b", "w_up", "b_up",
         "ffn_conv_w", "ffn_conv_b", "w_down", "b_down", "ln2_g", "ln2_b")
SMALL_ORDER = ("loss",) + VECS + CONV


def _step(x, target, W, Mo, Vo):
    x2, t2 = x[0], target[0]
    big2 = {n: W[n][0] for n in BIG}
    halves = lambda a: a.astype(BF16).reshape(N_CORES, a.shape[0] // N_CORES, a.shape[1])
    whole = lambda g: g.reshape(N_CHIPS, g.shape[1] * g.shape[2], g.shape[3])
    later = tuple(n for n in BIG if n != "w_in")
    first = _allgather_shards([halves(big2["w_in"])], [], name="allgather_w_in", collective_id=1)

    def gather_rest(h0b):
        rest = _allgather_shards([halves(big2[n]) for n in later] + [W[n] for n in CONV], first + [h0b],
                                 name="allgather_rest", collective_id=2)
        gathered = {n: whole(g) for n, g in zip(("w_in",) + later + CONV, first + rest)}
        return {
            "w_in": gathered["w_in"], "w_up": gathered["w_up"],
            "w_a": gathered["w_a"].reshape(D_CONV, D_MODEL), "w_o": gathered["w_o"].reshape(D_MODEL, D_MODEL),
            "w_down": gathered["w_down"].reshape(D_FF, D_MODEL),
            "w_b": gathered["w_b"].transpose(1, 0, 2).reshape(GROUP_W, D_MODEL),
            "conv_w": gathered["conv_w"].transpose(1, 0, 2).reshape(3, D_CONV),
            "ffn_conv_w": gathered["ffn_conv_w"].transpose(1, 0, 2).reshape(3, D_FF),
        }

    pvec = {n: W[n].reshape(1, -1) for n in VECS}

    parts = {}
    exchange_ids = iter((3, 4, 5))

    def exchange(group):
        names = tuple(group)
        res = _exchange_grads([group[n] for n in names], name="exchange_" + "_".join(names),
                              collective_id=next(exchange_ids))
        parts.update(zip(names, res))

    grad_x, _, small = _local_step(x2, t2, pvec, gather_rest, exchange)
    out = {}
    for n in BIG:
        tr = {"w_in": 128, "w_up": 128, "w_b": 128}.get(n, big2[n].shape[0] // 4)
        g, d, nm, nv = _reduce_adamw(parts[n], big2[n], Mo[n][0], Vo[n][0], tr=tr, name="adamw_" + n)
        out[n] = tuple(a[None] for a in (g, d, nm, nv))

    vec, offs = _pack([small[n] for n in SMALL_ORDER])
    off = dict(zip(SMALL_ORDER, offs))
    row = lambda a: a.reshape(1, -1)
    tot, vec_out = _reduce_adamw_vectors(_allgather_small(vec), [off[n] for n in VECS], [row(W[n]) for n in VECS],
                                         [row(Mo[n]) for n in VECS], [row(Vo[n]) for n in VECS])
    for n, res in zip(VECS, vec_out):
        out[n] = tuple(a.reshape(W[n].shape) for a in res)
    loss = tot[0, off["loss"]]
    chip = 2 * lax.axis_index("x") + lax.axis_index("y")
    taps_g = []
    for n in CONV:
        width = W[n].shape[2]
        full = lax.slice(tot, (0, off[n]), (1, off[n] + 3 * N_CHIPS * width)).reshape(3, N_CHIPS * width)
        taps_g.append(lax.dynamic_slice_in_dim(full, chip * width, width, axis=1))
    taps_out = _adamw_taps([W[n][0] for n in CONV], taps_g, [Mo[n][0] for n in CONV], [Vo[n][0] for n in CONV])
    for n, g, res in zip(CONV, taps_g, taps_out):
        out[n] = tuple(a[None] for a in (g,) + res)

    res = [loss, grad_x[None]]
    for k in range(4):
        res += [out[n][k] for n in ORDER]
    return tuple(res)


def kernel(x, ln0_g, ln0_b, w_in, b_in, conv_w, w_a, w_b, w_o, b_o, ln1_g, ln1_b, w_up, b_up, ffn_conv_w, ffn_conv_b, w_down, b_down, ln2_g, ln2_b, loss_target, m_ln0_g, m_ln0_b, m_w_in, m_b_in, m_conv_w, m_w_a, m_w_b, m_w_o, m_b_o, m_ln1_g, m_ln1_b, m_w_up, m_b_up, m_ffn_conv_w, m_ffn_conv_b, m_w_down, m_b_down, m_ln2_g, m_ln2_b, v_ln0_g, v_ln0_b, v_w_in, v_b_in, v_conv_w, v_w_a, v_w_b, v_w_o, v_b_o, v_ln1_g, v_ln1_b, v_w_up, v_b_up, v_ffn_conv_w, v_ffn_conv_b, v_w_down, v_b_down, v_ln2_g, v_ln2_b):
    W = dict(zip(ORDER, (ln0_g, ln0_b, w_in, b_in, conv_w, w_a, w_b, w_o, b_o, ln1_g, ln1_b, w_up, b_up,
                         ffn_conv_w, ffn_conv_b, w_down, b_down, ln2_g, ln2_b)))
    Mo = dict(zip(ORDER, (m_ln0_g, m_ln0_b, m_w_in, m_b_in, m_conv_w, m_w_a, m_w_b, m_w_o, m_b_o, m_ln1_g, m_ln1_b,
                          m_w_up, m_b_up, m_ffn_conv_w, m_ffn_conv_b, m_w_down, m_b_down, m_ln2_g, m_ln2_b)))
    Vo = dict(zip(ORDER, (v_ln0_g, v_ln0_b, v_w_in, v_b_in, v_conv_w, v_w_a, v_w_b, v_w_o, v_b_o, v_ln1_g, v_ln1_b,
                          v_w_up, v_b_up, v_ffn_conv_w, v_ffn_conv_b, v_w_down, v_b_down, v_ln2_g, v_ln2_b)))
    return _step(x, loss_target, W, Mo, Vo)
```

```python
import functools
import math

import jax
import jax.numpy as jnp
from jax import lax
from jax.experimental import pallas as pl
from jax.experimental.pallas import tpu as pltpu
from jax.experimental.pallas import tpu_sc as plsc

F32 = jnp.float32
BF16 = jnp.bfloat16

D_MODEL = 1024
D_CONV = D_MODEL
HEAD_DIM = 64
HEADS_PER_GROUP = 8
GROUPS = ((128, 1), (512, 4), (2048, 16))
N_GROUPS = len(GROUPS)
GROUP_W = HEADS_PER_GROUP * HEAD_DIM
QKV_W = N_GROUPS * GROUP_W
RADIUS = 64
D_FF = 2816
LN_EPS = 1e-5
ALPHA = 2.0 ** 0.25
MASK_VALUE = -1e30
ATT_SCALE = HEAD_DIM ** -0.5
OFF_B = 0
OFF_C = OFF_B + D_CONV
OFF_H = OFF_C + D_CONV
OFF_Q = OFF_H + D_CONV
OFF_K = OFF_Q + QKV_W
OFF_V = OFF_K + QKV_W
OFF_GA = OFF_V + QKV_W
OFF_GB = OFF_GA + D_MODEL
N_IN = OFF_GB + D_MODEL
ADAM_LR = 0.001
ADAM_B1 = 0.9
ADAM_B2 = 0.999
ADAM_EPS = 1e-08
ADAM_WD = 0.01
ADAM_STEP = 10
INV_SQRT2 = 0.7071067811865476
INV_SQRT_2PI = 0.3989422804014327

LANES = 128
SUBLANES = 8
VMEM_BYTES_V7X = 64 * 1024 * 1024
N_CHIPS = 4
N_CORES = 2
N_DEV = N_CHIPS * N_CORES
MESH = pl.DeviceIdType.MESH

N_BLK = N_IN // GROUP_W
PERM = (0, 1, 2, 3, 4, 5, 15, 16, 17, 18, 6, 9, 12, 7, 10, 13, 8, 11, 14)
INV_PERM = tuple(PERM.index(b) for b in range(N_BLK))
P_B, P_C, P_H, P_GA, P_GB, P_Q0 = 0, 1024, 2048, 3072, 4096, 5120
N_NAT = P_Q0 + QKV_W // N_GROUPS * 3
N_GATED = P_Q0

SLAB = 128
CHUNK = 256
PAD = SUBLANES
TQ = 128


def _cparams(sem, vmem_mb):
    assert vmem_mb * 1024 * 1024 < VMEM_BYTES_V7X
    return pltpu.CompilerParams(dimension_semantics=sem, vmem_limit_bytes=vmem_mb * 1024 * 1024)


def _dot(a, b):
    return jnp.dot(a, b, preferred_element_type=F32)


def _dot_nt(a, b):
    return lax.dot_general(a, b, (((1,), (1,)), ((), ())), preferred_element_type=F32)


def _dot_tn(a, b):
    return lax.dot_general(a, b, (((0,), (0,)), ((), ())), preferred_element_type=F32)


def _ln_stats(z):
    mu = jnp.mean(z, -1, keepdims=True)
    zc = z - mu
    var = jnp.mean(zc * zc, -1, keepdims=True)
    rstd = lax.rsqrt(var + LN_EPS)
    return zc * rstd, rstd


def _ln_bwd(dh, xhat, rstd, g):
    dxh = dh * g
    m1 = jnp.mean(dxh, -1, keepdims=True)
    m2 = jnp.mean(dxh * xhat, -1, keepdims=True)
    return rstd * (dxh - m1 - xhat * m2)


def _rows8(rows, width):
    pad = [jnp.zeros((1, width), F32)] * (SUBLANES - len(rows))
    return jnp.concatenate(list(rows) + pad, axis=0)


def _mm_nn(a, w, bias, *, tm, tn, out_dtype, name, vmem_mb=40):
    M, K = a.shape
    if w.ndim == 3:
        assert w.shape[2] == tn
        n_tiles = w.shape[0]
        w_spec = pl.BlockSpec((None, K, tn), lambda i, j: (j, 0, 0))
    else:
        n_tiles = w.shape[1] // tn
        w_spec = pl.BlockSpec((K, tn), lambda i, j: (0, j))

    def body(a_ref, w_ref, b_ref, o_ref):
        o_ref[...] = (_dot(a_ref[...], w_ref[...]) + b_ref[...]).astype(o_ref.dtype)

    return pl.pallas_call(
        body, grid=(M // tm, n_tiles),
        in_specs=[pl.BlockSpec((tm, K), lambda i, j: (i, 0)), w_spec, pl.BlockSpec((1, tn), lambda i, j: (0, j))],
        out_specs=pl.BlockSpec((tm, tn), lambda i, j: (i, j)),
        out_shape=jax.ShapeDtypeStruct((M, n_tiles * tn), out_dtype),
        name=name, compiler_params=_cparams(("parallel", "arbitrary"), vmem_mb))(a, w, bias)


def _mm_nt(a, w, *, tm, a_col=0, name, vmem_mb=40):
    M = a.shape[0]
    N, K = w.shape

    def body(a_ref, w_ref, o_ref):
        o_ref[...] = _dot_nt(a_ref[...], w_ref[...]).astype(o_ref.dtype)

    return pl.pallas_call(
        body, grid=(M // tm,),
        in_specs=[pl.BlockSpec((tm, K), lambda i: (i, a_col)),
                  pl.BlockSpec((N, K), lambda i: (0, 0))],
        out_specs=pl.BlockSpec((tm, N), lambda i: (i, 0)),
        out_shape=jax.ShapeDtypeStruct((M, N), BF16),
        name=name, compiler_params=_cparams(("parallel",), vmem_mb))(a, w)


def _mm_tn(a, g, *, n_out, tn, ts, g_block, g_map, colsum=False, name, vmem_mb=48):
    S, K = a.shape
    n_s = S // ts

    def body(a_ref, g_ref, *rest):
        if colsum:
            o_ref, cs_ref, acc_ref, cacc_ref = rest
        else:
            o_ref, acc_ref = rest
        s = pl.program_id(1)

        @pl.when(s == 0)
        def _():
            acc_ref[...] = jnp.zeros_like(acc_ref)
            if colsum:
                cacc_ref[...] = jnp.zeros_like(cacc_ref)

        gv = g_ref[...]
        acc_ref[...] += _dot_tn(a_ref[...], gv)
        if colsum:
            cacc_ref[...] += jnp.broadcast_to(jnp.sum(gv.astype(F32), axis=0, keepdims=True), cacc_ref.shape)

        @pl.when(s == n_s - 1)
        def _():
            o_ref[...] = acc_ref[...].astype(o_ref.dtype)
            if colsum:
                cs_ref[...] = cacc_ref[...]

    out_specs = [pl.BlockSpec((None, K, tn), lambda j, s: (j, 0, 0))]
    out_shape = [jax.ShapeDtypeStruct((n_out, K, tn), BF16)]
    scratch = [pltpu.VMEM((K, tn), F32)]
    if colsum:
        out_specs.append(pl.BlockSpec((SUBLANES, tn), lambda j, s: (0, j)))
        out_shape.append(jax.ShapeDtypeStruct((SUBLANES, n_out * tn), F32))
        scratch.append(pltpu.VMEM((SUBLANES, tn), F32))
    res = pl.pallas_call(
        body, grid=(n_out, n_s),
        in_specs=[pl.BlockSpec((ts, K), lambda j, s: (s, 0)), pl.BlockSpec(g_block, g_map)],
        out_specs=out_specs, out_shape=out_shape, scratch_shapes=scratch,
        name=name, compiler_params=_cparams(("parallel", "arbitrary"), vmem_mb))(a, g)
    return res if colsum else res[0]


def _mm_tn_cat(a, gs, *, ts, name, vmem_mb=40):
    S, K = a.shape
    widths = [g.shape[1] for g in gs]
    n_s, total = S // ts, sum(widths)

    def body(*refs):
        a_ref, g_refs = refs[0], refs[1:1 + len(gs)]
        o_ref, cs_ref, acc_ref, cacc_ref = refs[1 + len(gs):]
        s = pl.program_id(0)

        @pl.when(s == 0)
        def _():
            acc_ref[...] = jnp.zeros_like(acc_ref)
            cacc_ref[...] = jnp.zeros_like(cacc_ref)

        av, col = a_ref[...], 0
        for g_ref, w in zip(g_refs, widths):
            gv = g_ref[...]
            acc_ref[:, col:col + w] += _dot_tn(av, gv)
            cacc_ref[:, col:col + w] += jnp.broadcast_to(jnp.sum(gv.astype(F32), axis=0, keepdims=True), (SUBLANES, w))
            col += w

        @pl.when(s == n_s - 1)
        def _():
            o_ref[...] = acc_ref[...].astype(BF16)
            cs_ref[...] = cacc_ref[...]

    return pl.pallas_call(
        body, grid=(n_s,),
        in_specs=[pl.BlockSpec((ts, K), lambda s: (s, 0))] + [pl.BlockSpec((ts, w), lambda s: (s, 0)) for w in widths],
        out_specs=[pl.BlockSpec((K, total), lambda s: (0, 0)), pl.BlockSpec((SUBLANES, total), lambda s: (0, 0))],
        out_shape=[jax.ShapeDtypeStruct((K, total), BF16), jax.ShapeDtypeStruct((SUBLANES, total), F32)],
        scratch_shapes=[pltpu.VMEM((K, total), F32), pltpu.VMEM((SUBLANES, total), F32)],
        name=name, compiler_params=_cparams(("arbitrary",), vmem_mb))(a, *gs)


def _col_runs():
    shard_w = N_IN // N_CHIPS
    runs = []
    for pos, blk in enumerate(PERM):
        c, end = blk * GROUP_W, (blk + 1) * GROUP_W
        while c < end:
            stop = min(end, (c // shard_w + 1) * shard_w)
            runs.append((c // shard_w, c % shard_w, pos * GROUP_W + c - blk * GROUP_W, stop - c))
            c = stop
    return runs


def _dma_copies(srcs, out_shapes, copies, *, name):
    n_in = len(srcs)

    def body(*refs):
        ins, outs, sems = refs[:n_in], refs[n_in:-1], refs[-1]
        started = []
        for k, (s, src_idx, d, dst_idx) in enumerate(copies):
            cp = pltpu.make_async_copy(ins[s].at[src_idx], outs[d].at[dst_idx], sems.at[k])
            cp.start()
            started.append(cp)
        for cp in started:
            cp.wait()

    any_spec = pl.BlockSpec(memory_space=pl.ANY)
    return pl.pallas_call(
        body, in_specs=[any_spec] * n_in, out_specs=[any_spec] * len(out_shapes), out_shape=out_shapes,
        scratch_shapes=[pltpu.SemaphoreType.DMA((len(copies),))], name=name)(*srcs)


def _here_piece(pcol, widths):
    for k, w in enumerate(widths):
        if pcol < w:
            return k, pcol
        pcol -= w
    raise ValueError(pcol)


DILS = tuple(d for _, d in GROUPS if d > 1)


def _res_spec(d, tm, width):
    return pl.BlockSpec((d, tm // d, width), lambda i: (0, i, 0))


def _lane_scratch(tm, width):
    return [pltpu.VMEM((tm, LANES), F32)] * (width // LANES)


def _to_residue(val, dst_refs, dils, tm, dtype, scr):
    for c, ref in enumerate(scr):
        ref[...] = val[:, c * LANES:(c + 1) * LANES]
    for dst_ref, d in zip(dst_refs, dils):
        for r in range(d):
            cols = [ref[pl.ds(r, tm // d, stride=d), :] for ref in scr]
            dst_ref[r] = jnp.concatenate(cols, axis=1).astype(dtype)


def _from_residue(rows_of, d, tm, scr):
    for r in range(d):
        v = rows_of(r).astype(F32)
        for c, ref in enumerate(scr):
            ref[pl.ds(r, tm // d, stride=d), :] = v[:, c * LANES:(c + 1) * LANES]
    return jnp.concatenate([ref[...] for ref in scr], axis=1)


def _ln0_fwd(x, g, b, after=(), *, tm=512):
    S, Dm = x.shape
    n_after = len(after)

    def body(x_ref, g_ref, b_ref, *rest):
        h_ref, hb_ref, *rest = rest[n_after:]
        xhat, _ = _ln_stats(x_ref[...])
        h = xhat * g_ref[...] + b_ref[...]
        h_ref[...] = h
        hb_ref[...] = h.astype(BF16)
        _to_residue(h, rest[:len(DILS)], DILS, tm, BF16, rest[len(DILS):])

    row = pl.BlockSpec((tm, Dm), lambda i: (i, 0))
    vec = pl.BlockSpec((1, Dm), lambda i: (0, 0))
    return pl.pallas_call(
        body, grid=(S // tm,), in_specs=[row, vec, vec] + [pl.BlockSpec(memory_space=pl.ANY)] * n_after,
        out_specs=[row, row] + [_res_spec(d, tm, Dm) for d in DILS],
        out_shape=[jax.ShapeDtypeStruct((S, Dm), F32), jax.ShapeDtypeStruct((S, Dm), BF16)]
        + [jax.ShapeDtypeStruct((d, S // d, Dm), BF16) for d in DILS],
        scratch_shapes=_lane_scratch(tm, Dm),
        name="ln0_fwd", compiler_params=_cparams(("parallel",), 32))(x, g, b, *after)


def _slab_spec(S, col0):
    return pl.BlockSpec((S, SLAB), lambda j: (0, col0 // SLAB + j))


def _zero_pads(scr, S):
    scr[0:PAD, :] = jnp.zeros((PAD, SLAB), F32)
    scr[S + PAD:S + 2 * PAD, :] = jnp.zeros((PAD, SLAB), F32)


def _shifted(scr, t):
    return (scr[PAD - 1 + t:PAD - 1 + t + CHUNK, :], scr[PAD + t:PAD + t + CHUNK, :],
            scr[PAD + 1 + t:PAD + 1 + t + CHUNK, :])


def _conv_gate_fwd(proj, conv_w):
    S = proj.shape[0]

    def body(b_ref, c_ref, h_ref, w_ref, o_ref, u_scr):
        _zero_pads(u_scr, S)
        for t in range(0, S, CHUNK):
            u_scr[PAD + t:PAD + t + CHUNK, :] = c_ref[t:t + CHUNK, :].astype(F32) * h_ref[t:t + CHUNK, :].astype(F32)
        w0, w1, w2 = w_ref[0:1, :], w_ref[1:2, :], w_ref[2:3, :]
        for t in range(0, S, CHUNK):
            um, u0, up = _shifted(u_scr, t)
            cv = w0 * um + w1 * u0 + w2 * up
            o_ref[t:t + CHUNK, :] = (b_ref[t:t + CHUNK, :].astype(F32) * cv).astype(BF16)

    return pl.pallas_call(
        body, grid=(D_CONV // SLAB,),
        in_specs=[_slab_spec(S, P_B), _slab_spec(S, P_C), _slab_spec(S, P_H),
                  pl.BlockSpec((3, SLAB), lambda j: (0, j))],
        out_specs=pl.BlockSpec((S, SLAB), lambda j: (0, j)),
        out_shape=jax.ShapeDtypeStruct((S, D_CONV), BF16),
        scratch_shapes=[pltpu.VMEM((S + 2 * PAD, SLAB), F32)],
        name="conv_gate_fwd", compiler_params=_cparams(("parallel",), 40))(proj, proj, proj, conv_w)


MASKED_DISTANCE = -1e34


def _attn_bias(i, sub, dil):
    j = lax.broadcasted_iota(jnp.int32, (2 * TQ, TQ), 0)
    a = lax.broadcasted_iota(jnp.int32, (2 * TQ, TQ), 1)
    rel = jnp.abs(j - RADIUS - a)
    kpos = i * TQ - RADIUS + j
    valid = (rel <= RADIUS) & (kpos >= 0) & (kpos < sub)
    return jnp.where(valid, -(rel * dil).astype(F32), MASKED_DISTANCE)


def _head_stats(rows):
    pad = jnp.zeros((LANES - len(rows), TQ), F32)
    return jnp.concatenate(list(rows) + [pad], axis=0).T


def _slope(g, h):
    return 2.0 ** (-8.0 * (g * HEADS_PER_GROUP + h + 1) / (N_GROUPS * HEADS_PER_GROUP))


def _window(p_ref, c_ref, n_ref):
    return jnp.concatenate([p_ref[TQ - RADIUS:, :], c_ref[...], n_ref[:RADIUS, :]], axis=0)


def _pair(a, h):
    return a[:, (h // 2) * LANES:(h // 2 + 1) * LANES]


def _own_lanes(a, h):
    lane = lax.broadcasted_iota(jnp.int32, a.shape, 1)
    return jnp.where((lane >= HEAD_DIM) == (h % 2 == 1), a, jnp.zeros_like(a))


def _own_rows(a, h):
    return a[(h % 2) * HEAD_DIM:(h % 2 + 1) * HEAD_DIM, :]


def _qkv_specs(nb, col0):
    def spec(col, shift):
        return pl.BlockSpec((None, TQ, GROUP_W), lambda r, i: (r, jnp.clip(i + shift, 0, nb - 1), col))

    return [spec(col0, 0), spec(col0 + 1, -1), spec(col0 + 1, 0), spec(col0 + 1, 1),
            spec(col0 + 2, -1), spec(col0 + 2, 0), spec(col0 + 2, 1)]


def _attn_fwd(qkv, col0, g):
    dil, sub, _ = qkv.shape
    nb = sub // TQ

    def body(q_ref, kp, kc, kn, vp, vc, vn, o_ref, lse_ref, ot_scr, s_scr, p_scr):
        bias = _attn_bias(pl.program_id(1), sub, dil)
        kwin = _window(kp, kc, kn)
        vwin = _window(vp, vc, vn)
        q = q_ref[...] * ATT_SCALE
        for h in range(HEADS_PER_GROUP):
            s_scr[h] = _dot_nt(_pair(kwin, h), _own_lanes(_pair(q, h), h))
        lse, inv_den = [], []
        for h in range(HEADS_PER_GROUP):
            s = s_scr[h] + _slope(g, h) * bias
            m = jnp.max(s, axis=0, keepdims=True)
            p = jnp.exp(s - m)
            den = jnp.sum(p, axis=0, keepdims=True)
            p_scr[h] = p.astype(BF16)
            inv_den.append(1.0 / den)
            lse.append(m + jnp.log(den))
        for h in range(HEADS_PER_GROUP):
            ot = _dot_tn(_pair(vwin, h), p_scr[h])
            ot_scr[h * HEAD_DIM:(h + 1) * HEAD_DIM, :] = _own_rows(ot, h) * inv_den[h]
        o_ref[...] = ot_scr[...].T
        lse_ref[...] = _head_stats(lse)

    return pl.pallas_call(
        body, grid=(dil, nb), in_specs=_qkv_specs(nb, col0),
        out_specs=[pl.BlockSpec((None, TQ, GROUP_W), lambda r, i: (r, i, 0)),
                   pl.BlockSpec((None, TQ, LANES), lambda r, i: (r, i, 0))],
        out_shape=[jax.ShapeDtypeStruct((dil, sub, GROUP_W), F32), jax.ShapeDtypeStruct((dil, sub, LANES), F32)],
        scratch_shapes=[pltpu.VMEM((GROUP_W, TQ), F32), pltpu.VMEM((HEADS_PER_GROUP, 2 * TQ, TQ), F32),
                        pltpu.VMEM((HEADS_PER_GROUP, 2 * TQ, TQ), BF16)],
        name=f"attn_fwd_g{g}", compiler_params=_cparams(("parallel", "arbitrary"), 32))(*([qkv] * 7))


def _expand_heads():
    h = lax.broadcasted_iota(jnp.int32, (LANES, GROUP_W), 0)
    c = lax.broadcasted_iota(jnp.int32, (LANES, GROUP_W), 1)
    return (c // HEAD_DIM == h).astype(F32)


def _dot_f32(a, b):
    return jnp.dot(a, b, preferred_element_type=F32, precision=lax.Precision.HIGHEST)


def _attn_combine(outs, lses, *, tm=512):
    S = outs[0].shape[1]
    n_col = GROUP_W // LANES

    def body(*refs):
        ins, e_ref = refs[:2 * N_GROUPS], refs[2 * N_GROUPS]
        c_ref, cb_ref, lt_ref = refs[2 * N_GROUPS + 1:2 * N_GROUPS + 4]
        scr = refs[2 * N_GROUPS + 4:]
        o, l = [ins[0][0]], [ins[N_GROUPS][0]]
        for k, d in enumerate(DILS):
            o_ref, l_ref = ins[1 + k], ins[N_GROUPS + 1 + k]
            o.append(_from_residue(lambda r: o_ref[r], d, tm, scr[k * (n_col + 1):k * (n_col + 1) + n_col]))
            l.append(_from_residue(lambda r: l_ref[r], d, tm, scr[k * (n_col + 1) + n_col:(k + 1) * (n_col + 1)]))
        m = jnp.maximum(jnp.maximum(l[0], l[1]), l[2])
        e = [jnp.exp(v - m) for v in l]
        den = e[0] + e[1] + e[2]
        comb = sum(_dot_f32(ev / den, e_ref[...]) * ov for ev, ov in zip(e, o))
        c_ref[...] = comb
        cb_ref[...] = comb.astype(BF16)
        lt_ref[...] = m + jnp.log(den)

    row = pl.BlockSpec((tm, GROUP_W), lambda i: (i, 0))
    dils = [d for _, d in GROUPS]
    return pl.pallas_call(
        body, grid=(S // tm,),
        in_specs=[_res_spec(d, tm, GROUP_W) for d in dils] + [_res_spec(d, tm, LANES) for d in dils]
        + [_resident((LANES, GROUP_W))],
        out_specs=[row, row, pl.BlockSpec((tm, LANES), lambda i: (i, 0))],
        out_shape=[jax.ShapeDtypeStruct((S, GROUP_W), F32), jax.ShapeDtypeStruct((S, GROUP_W), BF16),
                   jax.ShapeDtypeStruct((S, LANES), F32)],
        scratch_shapes=_lane_scratch(tm, GROUP_W + LANES) * len(DILS),
        name="attn_combine", compiler_params=_cparams(("parallel",), 32))(*outs, *lses, _expand_heads())


def _branch_mix(ya_in, comb_b, w_a, w_b, proj, *, tm=512):
    S = ya_in.shape[0]

    def body(ya_ref, cb_ref, wa_ref, wb_ref, ga_ref, gb_ref, yab_ref, mx_ref):
        y_a = _dot(ya_ref[...], wa_ref[...])
        y_b = _dot(cb_ref[...], wb_ref[...])
        yab_ref[:, 0:D_MODEL] = y_a.astype(BF16)
        yab_ref[:, D_MODEL:2 * D_MODEL] = y_b.astype(BF16)
        mx = jax.nn.sigmoid(ga_ref[...].astype(F32)) * y_a + jax.nn.sigmoid(gb_ref[...].astype(F32)) * y_b
        mx_ref[...] = mx.astype(BF16)

    return pl.pallas_call(
        body, grid=(S // tm,),
        in_specs=[pl.BlockSpec((tm, D_CONV), lambda i: (i, 0)), pl.BlockSpec((tm, GROUP_W), lambda i: (i, 0)),
                  pl.BlockSpec((D_CONV, D_MODEL), lambda i: (0, 0)), pl.BlockSpec((GROUP_W, D_MODEL), lambda i: (0, 0)),
                  pl.BlockSpec((tm, D_MODEL), lambda i: (i, P_GA // D_MODEL)),
                  pl.BlockSpec((tm, D_MODEL), lambda i: (i, P_GB // D_MODEL))],
        out_specs=[pl.BlockSpec((tm, 2 * D_MODEL), lambda i: (i, 0)), pl.BlockSpec((tm, D_MODEL), lambda i: (i, 0))],
        out_shape=[jax.ShapeDtypeStruct((S, 2 * D_MODEL), BF16), jax.ShapeDtypeStruct((S, D_MODEL), BF16)],
        name="branch_mix", compiler_params=_cparams(("parallel",), 40))(ya_in, comb_b, w_a, w_b, proj, proj)


def _mix_ln1(mixin, w_o, b_o, h0, g1, b1, *, tm=512):
    S = mixin.shape[0]

    def body(mx_ref, wo_ref, bo_ref, h0_ref, g_ref, b_ref, xh_ref, rs_ref, h1b_ref):
        z = ALPHA * h0_ref[...] + _dot(mx_ref[...], wo_ref[...]) + bo_ref[...]
        xhat, rstd = _ln_stats(z)
        xh_ref[...] = xhat
        rs_ref[...] = jnp.broadcast_to(rstd, (tm, LANES))
        h1b_ref[...] = (xhat * g_ref[...] + b_ref[...]).astype(BF16)

    row = pl.BlockSpec((tm, D_MODEL), lambda i: (i, 0))
    vec = pl.BlockSpec((1, D_MODEL), lambda i: (0, 0))
    return pl.pallas_call(
        body, grid=(S // tm,),
        in_specs=[row, pl.BlockSpec((D_MODEL, D_MODEL), lambda i: (0, 0)), vec, row, vec, vec],
        out_specs=[row, pl.BlockSpec((tm, LANES), lambda i: (i, 0)), row],
        out_shape=[jax.ShapeDtypeStruct((S, D_MODEL), F32), jax.ShapeDtypeStruct((S, LANES), F32),
                   jax.ShapeDtypeStruct((S, D_MODEL), BF16)],
        name="mix_ln1", compiler_params=_cparams(("parallel",), 40))(mixin, w_o, b_o, h0, g1, b1)


def _gelu_parts(cz):
    cdf = 0.5 * (1.0 + lax.erf(cz * INV_SQRT2))
    return cdf, cz * cdf


def _ffn_conv_fwd(up, cw, cb):
    S = up.shape[0]

    def body(a_ref, g_ref, w_ref, cb_ref, o_ref, a_scr):
        _zero_pads(a_scr, S)
        for t in range(0, S, CHUNK):
            a_scr[PAD + t:PAD + t + CHUNK, :] = a_ref[t:t + CHUNK, :].astype(F32)
        w0, w1, w2 = w_ref[0:1, :], w_ref[1:2, :], w_ref[2:3, :]
        for t in range(0, S, CHUNK):
            am, a0, ap = _shifted(a_scr, t)
            _, gel = _gelu_parts(w0 * am + w1 * a0 + w2 * ap + cb_ref[...])
            o_ref[t:t + CHUNK, :] = (gel * g_ref[t:t + CHUNK, :].astype(F32)).astype(BF16)

    return pl.pallas_call(
        body, grid=(D_FF // SLAB,),
        in_specs=[_slab_spec(S, 0), _slab_spec(S, D_FF), pl.BlockSpec((3, SLAB), lambda j: (0, j)),
                  pl.BlockSpec((1, SLAB), lambda j: (0, j))],
        out_specs=pl.BlockSpec((S, SLAB), lambda j: (0, j)),
        out_shape=jax.ShapeDtypeStruct((S, D_FF), BF16),
        scratch_shapes=[pltpu.VMEM((S + 2 * PAD, SLAB), F32)],
        name="ffn_conv_fwd", compiler_params=_cparams(("parallel",), 40))(up, up, cw, cb)


def _down_ln2_loss(f, w_down, b_down, xhat1, g1, b1, g2, b2, target, *, tm=256):
    S = f.shape[0]

    def body(f_ref, wd_ref, bd_ref, xh1_ref, g1_ref, b1_ref, g2_ref, b2_ref, t_ref, dz_ref, dzb_ref, st_ref):
        h1 = xh1_ref[...] * g1_ref[...] + b1_ref[...]
        z = ALPHA * h1 + _dot(f_ref[...], wd_ref[...]) + bd_ref[...]
        xhat, rstd = _ln_stats(z)
        err = xhat * g2_ref[...] + b2_ref[...] - t_ref[...]
        loss = (0.5 / D_MODEL) * jnp.sum(jnp.sum(err * err, axis=1, keepdims=True), axis=0, keepdims=True)
        dh2 = err * (1.0 / D_MODEL)
        dz = _ln_bwd(dh2, xhat, rstd, g2_ref[...])
        dz_ref[...] = dz
        dzb_ref[...] = dz.astype(BF16)
        upd = _rows8([jnp.sum(dh2 * xhat, axis=0, keepdims=True), jnp.sum(dh2, axis=0, keepdims=True),
                      jnp.broadcast_to(loss, (1, D_MODEL)), jnp.sum(dz, axis=0, keepdims=True)], D_MODEL)

        @pl.when(pl.program_id(0) == 0)
        def _():
            st_ref[...] = upd

        @pl.when(pl.program_id(0) != 0)
        def _():
            st_ref[...] += upd

    row = pl.BlockSpec((tm, D_MODEL), lambda i: (i, 0))
    vec = pl.BlockSpec((1, D_MODEL), lambda i: (0, 0))
    return pl.pallas_call(
        body, grid=(S // tm,),
        in_specs=[pl.BlockSpec((tm, D_FF), lambda i: (i, 0)), _resident((D_FF, D_MODEL)),
                  vec, row, vec, vec, vec, vec, row],
        out_specs=[row, row, pl.BlockSpec((SUBLANES, D_MODEL), lambda i: (0, 0))],
        out_shape=[jax.ShapeDtypeStruct((S, D_MODEL), F32), jax.ShapeDtypeStruct((S, D_MODEL), BF16),
                   jax.ShapeDtypeStruct((SUBLANES, D_MODEL), F32)],
        name="down_ln2_loss", compiler_params=_cparams(("arbitrary",), 48))(
            f, w_down, b_down, xhat1, g1, b1, g2, b2, target)


def _ffn_conv_bwd(up, df, cw, cb):
    S = up.shape[0]

    def body(a_ref, g_ref, df_ref, w_ref, cb_ref, dup_ref, sm_ref, a_scr, d_scr):
        _zero_pads(a_scr, S)
        _zero_pads(d_scr, S)
        for t in range(0, S, CHUNK):
            a_scr[PAD + t:PAD + t + CHUNK, :] = a_ref[t:t + CHUNK, :].astype(F32)
        w0, w1, w2 = w_ref[0:1, :], w_ref[1:2, :], w_ref[2:3, :]
        zero = jnp.zeros((1, SLAB), F32)
        s_dg, s_dcz, s_w0, s_w1, s_w2 = zero, zero, zero, zero, zero
        for t in range(0, S, CHUNK):
            am, a0, ap = _shifted(a_scr, t)
            cz = w0 * am + w1 * a0 + w2 * ap + cb_ref[...]
            cdf, gel = _gelu_parts(cz)
            dfv = df_ref[t:t + CHUNK, :].astype(F32)
            dgte = dfv * gel
            dcz = dfv * g_ref[t:t + CHUNK, :].astype(F32) * (cdf + cz * jnp.exp(-0.5 * cz * cz) * INV_SQRT_2PI)
            dup_ref[1, t:t + CHUNK, :] = dgte.astype(BF16)
            d_scr[PAD + t:PAD + t + CHUNK, :] = dcz
            s_dg = s_dg + jnp.sum(dgte, axis=0, keepdims=True)
            s_dcz = s_dcz + jnp.sum(dcz, axis=0, keepdims=True)
            s_w0 = s_w0 + jnp.sum(dcz * am, axis=0, keepdims=True)
            s_w1 = s_w1 + jnp.sum(dcz * a0, axis=0, keepdims=True)
            s_w2 = s_w2 + jnp.sum(dcz * ap, axis=0, keepdims=True)
        s_da = zero
        for t in range(0, S, CHUNK):
            dm, d0, dp = _shifted(d_scr, t)
            da = w0 * dp + w1 * d0 + w2 * dm
            dup_ref[0, t:t + CHUNK, :] = da.astype(BF16)
            s_da = s_da + jnp.sum(da, axis=0, keepdims=True)
        sm_ref[...] = _rows8([s_da, s_dg, s_dcz, s_w0, s_w1, s_w2], SLAB)

    return pl.pallas_call(
        body, grid=(D_FF // SLAB,),
        in_specs=[_slab_spec(S, 0), _slab_spec(S, D_FF), pl.BlockSpec((S, SLAB), lambda j: (0, j)),
                  pl.BlockSpec((3, SLAB), lambda j: (0, j)), pl.BlockSpec((1, SLAB), lambda j: (0, j))],
        out_specs=[pl.BlockSpec((2, S, SLAB), lambda j: (0, 0, j)), pl.BlockSpec((SUBLANES, SLAB), lambda j: (0, j))],
        out_shape=[jax.ShapeDtypeStruct((2, S, D_FF), BF16), jax.ShapeDtypeStruct((SUBLANES, D_FF), F32)],
        scratch_shapes=[pltpu.VMEM((S + 2 * PAD, SLAB), F32)] * 2,
        name="ffn_conv_bwd", compiler_params=_cparams(("parallel",), 48))(up, up, df, cw, cb)


def _resident(shape):
    nd = len(shape)
    return pl.BlockSpec(shape, lambda *_: (0,) * nd, pipeline_mode=pl.Buffered(1))


def _up_bwd_ln1(dup, w_up3, dz2, xhat1, rstd1, g1, *, tm=256):
    S = dz2.shape[0]
    ns, _, tk = w_up3.shape
    per_plane = D_FF // tk

    def body(du_ref, w_ref, dz2_ref, xh_ref, rs_ref, g_ref, dz_ref, dzb_ref, st_ref):
        dh = ALPHA * dz2_ref[...]
        for k in range(ns):
            col = (k % per_plane) * tk
            dh = dh + _dot_nt(du_ref[k // per_plane, :, col:col + tk], w_ref[k])
        xhat = xh_ref[...]
        dz = _ln_bwd(dh, xhat, rs_ref[:, 0:1], g_ref[...])
        dz_ref[...] = dz
        dzb_ref[...] = dz.astype(BF16)
        upd = _rows8([jnp.sum(dh * xhat, axis=0, keepdims=True), jnp.sum(dh, axis=0, keepdims=True),
                      jnp.sum(dz, axis=0, keepdims=True)], D_MODEL)

        @pl.when(pl.program_id(0) == 0)
        def _():
            st_ref[...] = upd

        @pl.when(pl.program_id(0) != 0)
        def _():
            st_ref[...] += upd

    row = pl.BlockSpec((tm, D_MODEL), lambda i: (i, 0))
    return pl.pallas_call(
        body, grid=(S // tm,),
        in_specs=[pl.BlockSpec((dup.shape[0], tm, D_FF), lambda i: (0, i, 0)), _resident(w_up3.shape),
                  row, row, pl.BlockSpec((tm, LANES), lambda i: (i, 0)), pl.BlockSpec((1, D_MODEL), lambda i: (0, 0))],
        out_specs=[row, row, pl.BlockSpec((SUBLANES, D_MODEL), lambda i: (0, 0))],
        out_shape=[jax.ShapeDtypeStruct((S, D_MODEL), F32), jax.ShapeDtypeStruct((S, D_MODEL), BF16),
                   jax.ShapeDtypeStruct((SUBLANES, D_MODEL), F32)],
        name="up_bwd_ln1", compiler_params=_cparams(("arbitrary",), 48))(dup, w_up3, dz2, xhat1, rstd1, g1)


def _mix_bwd(dz1b, w_o, proj, yab, *, tm=512):
    S = dz1b.shape[0]

    def body(dz_ref, wo_ref, ga_ref, gb_ref, y_ref, dy_ref, dg_ref):
        dmx = _dot_nt(dz_ref[...], wo_ref[...])
        for k, gt_ref in enumerate((ga_ref, gb_ref)):
            sl = slice(k * D_MODEL, (k + 1) * D_MODEL)
            sg = jax.nn.sigmoid(gt_ref[...].astype(F32))
            dy_ref[:, sl] = (dmx * sg).astype(BF16)
            dg_ref[k] = (dmx * y_ref[:, sl].astype(F32) * sg * (1.0 - sg)).astype(BF16)

    row = pl.BlockSpec((tm, D_MODEL), lambda i: (i, 0))
    wide = pl.BlockSpec((tm, 2 * D_MODEL), lambda i: (i, 0))
    return pl.pallas_call(
        body, grid=(S // tm,),
        in_specs=[row, _resident(w_o.shape), pl.BlockSpec((tm, D_MODEL), lambda i: (i, P_GA // D_MODEL)),
                  pl.BlockSpec((tm, D_MODEL), lambda i: (i, P_GB // D_MODEL)), wide],
        out_specs=[wide, pl.BlockSpec((2, tm, D_MODEL), lambda i: (0, i, 0))],
        out_shape=[jax.ShapeDtypeStruct((S, 2 * D_MODEL), BF16), jax.ShapeDtypeStruct((2, S, D_MODEL), BF16)],
        name="mix_bwd", compiler_params=_cparams(("parallel",), 40))(dz1b, w_o, proj, proj, yab)


def _conv_gate_bwd(proj, dya_in, conv_w):
    S = proj.shape[0]

    def body(b_ref, c_ref, h_ref, dy_ref, w_ref, o_ref, sm_ref, u_scr, d_scr):
        _zero_pads(u_scr, S)
        _zero_pads(d_scr, S)
        for t in range(0, S, CHUNK):
            u_scr[PAD + t:PAD + t + CHUNK, :] = c_ref[t:t + CHUNK, :].astype(F32) * h_ref[t:t + CHUNK, :].astype(F32)
        w0, w1, w2 = w_ref[0:1, :], w_ref[1:2, :], w_ref[2:3, :]
        zero = jnp.zeros((1, SLAB), F32)
        s_w0, s_w1, s_w2 = zero, zero, zero
        for t in range(0, S, CHUNK):
            um, u0, up = _shifted(u_scr, t)
            dy = dy_ref[t:t + CHUNK, :].astype(F32)
            o_ref[0, t:t + CHUNK, :] = (dy * (w0 * um + w1 * u0 + w2 * up)).astype(BF16)
            dcv = dy * b_ref[t:t + CHUNK, :].astype(F32)
            d_scr[PAD + t:PAD + t + CHUNK, :] = dcv
            s_w0 = s_w0 + jnp.sum(dcv * um, axis=0, keepdims=True)
            s_w1 = s_w1 + jnp.sum(dcv * u0, axis=0, keepdims=True)
            s_w2 = s_w2 + jnp.sum(dcv * up, axis=0, keepdims=True)
        for t in range(0, S, CHUNK):
            dm, d0, dp = _shifted(d_scr, t)
            du = w0 * dp + w1 * d0 + w2 * dm
            o_ref[1, t:t + CHUNK, :] = (du * h_ref[t:t + CHUNK, :].astype(F32)).astype(BF16)
            o_ref[2, t:t + CHUNK, :] = (du * c_ref[t:t + CHUNK, :].astype(F32)).astype(BF16)
        sm_ref[...] = _rows8([s_w0, s_w1, s_w2], SLAB)

    return pl.pallas_call(
        body, grid=(D_CONV // SLAB,),
        in_specs=[_slab_spec(S, P_B), _slab_spec(S, P_C), _slab_spec(S, P_H),
                  pl.BlockSpec((S, SLAB), lambda j: (0, j)), pl.BlockSpec((3, SLAB), lambda j: (0, j))],
        out_specs=[pl.BlockSpec((3, S, SLAB), lambda j: (0, 0, j)), pl.BlockSpec((SUBLANES, SLAB), lambda j: (0, j))],
        out_shape=[jax.ShapeDtypeStruct((3, S, D_CONV), BF16), jax.ShapeDtypeStruct((SUBLANES, D_CONV), F32)],
        scratch_shapes=[pltpu.VMEM((S + 2 * PAD, SLAB), F32)] * 2,
        name="conv_gate_bwd", compiler_params=_cparams(("parallel",), 48))(proj, proj, proj, dya_in, conv_w)


def _comb_bwd(dyab, w_b, comb, lse_tot, *, tm=512):
    S = comb.shape[0]
    widths, dtypes = (GROUP_W, LANES, LANES), (BF16, F32, F32)

    def body(dy_ref, wb_ref, c_ref, lt_ref, e_ref, *rest):
        outs, scr = rest[:3 * N_GROUPS], rest[3 * N_GROUPS:]
        dcb = _dot_nt(dy_ref[...], wb_ref[...]).astype(BF16)
        dc = dcb.astype(F32)
        delta = lax.dot_general(dc * c_ref[...], e_ref[...], (((1,), (1,)), ((), ())),
                                preferred_element_type=F32, precision=lax.Precision.HIGHEST)
        for k, (val, dtype) in enumerate(zip((dc, lt_ref[...], delta), dtypes)):
            outs[k][0] = val.astype(dtype)
            _to_residue(val, [outs[3 * (1 + j) + k] for j in range(len(DILS))], DILS, tm, dtype,
                        scr[:val.shape[1] // LANES])

    out_specs, out_shape = [], []
    for _, d in GROUPS:
        out_specs += [_res_spec(d, tm, w) for w in widths]
        out_shape += [jax.ShapeDtypeStruct((d, S // d, w), t) for w, t in zip(widths, dtypes)]
    res = pl.pallas_call(
        body, grid=(S // tm,),
        in_specs=[pl.BlockSpec((tm, D_MODEL), lambda i: (i, 1)), _resident(w_b.shape),
                  pl.BlockSpec((tm, GROUP_W), lambda i: (i, 0)), pl.BlockSpec((tm, LANES), lambda i: (i, 0)),
                  _resident((LANES, GROUP_W))],
        out_specs=out_specs, out_shape=out_shape, scratch_shapes=_lane_scratch(tm, GROUP_W),
        name="comb_bwd", compiler_params=_cparams(("parallel",), 32))(dyab, w_b, comb, lse_tot, _expand_heads())
    return [tuple(res[3 * g:3 * g + 3]) for g in range(N_GROUPS)]


def _attn_bwd(qkv, col0, g, dcomb, lse_tot, delta):
    dil, sub, _ = qkv.shape
    nb = sub // TQ

    def body(q_ref, kp, kc, kn, vp, vc, vn, do_ref, lse_ref, dl_ref, dq_ref, dk_ref, dv_ref,
             ak, av, dqt_scr, s_scr, dp_scr, ds_scr, p_scr):
        i = pl.program_id(1)

        @pl.when(i == 0)
        def _():
            ak[...] = jnp.zeros_like(ak)
            av[...] = jnp.zeros_like(av)

        @pl.when(i < nb)
        def _():
            bias = _attn_bias(i, sub, dil)
            kwin = _window(kp, kc, kn)
            vwin = _window(vp, vc, vn)
            q = q_ref[...] * ATT_SCALE
            do = do_ref[...]
            lse_t, dl_t = lse_ref[...].T, dl_ref[...].T
            for h in range(HEADS_PER_GROUP):
                s_scr[h] = _dot_nt(_pair(kwin, h), _own_lanes(_pair(q, h), h))
                dp_scr[h] = _dot_nt(_pair(vwin, h), _own_lanes(_pair(do, h), h))
            for h in range(HEADS_PER_GROUP):
                p = jnp.exp(s_scr[h] + _slope(g, h) * bias - lse_t[h:h + 1, :])
                ds_scr[h] = (p * (dp_scr[h] - dl_t[h:h + 1, :])).astype(BF16)
                p_scr[h] = p.astype(BF16)
            for h in range(HEADS_PER_GROUP):
                dqt_scr[h * HEAD_DIM:(h + 1) * HEAD_DIM, :] = _own_rows(_dot_tn(_pair(kwin, h), ds_scr[h]), h)
            for h in range(0, HEADS_PER_GROUP, 2):
                cols = slice(h * HEAD_DIM, (h + 2) * HEAD_DIM)
                q2 = jnp.concatenate([_own_lanes(_pair(q, h), h), _own_lanes(_pair(q, h), h + 1)], axis=0)
                do2 = jnp.concatenate([_own_lanes(_pair(do, h), h), _own_lanes(_pair(do, h), h + 1)], axis=0)
                ak[RADIUS:RADIUS + 2 * TQ, cols] += _dot(jnp.concatenate([ds_scr[h], ds_scr[h + 1]], axis=1), q2)
                av[RADIUS:RADIUS + 2 * TQ, cols] += _dot(jnp.concatenate([p_scr[h], p_scr[h + 1]], axis=1), do2)
            dq_ref[...] = (dqt_scr[...].T * ATT_SCALE).astype(BF16)

        dk_ref[...] = ak[0:TQ, :].astype(BF16)
        dv_ref[...] = av[0:TQ, :].astype(BF16)
        ak[0:2 * TQ, :] = ak[TQ:3 * TQ, :]
        av[0:2 * TQ, :] = av[TQ:3 * TQ, :]
        ak[2 * TQ:3 * TQ, :] = jnp.zeros((TQ, GROUP_W), F32)
        av[2 * TQ:3 * TQ, :] = jnp.zeros((TQ, GROUP_W), F32)

    tok = pl.BlockSpec((None, TQ, GROUP_W), lambda r, i: (r, jnp.minimum(i, nb - 1), 0))
    stat = pl.BlockSpec((None, TQ, LANES), lambda r, i: (r, jnp.minimum(i, nb - 1), 0))
    dkv_spec = pl.BlockSpec((None, TQ, GROUP_W), lambda r, i: (r, jnp.maximum(i - 1, 0), 0))
    return pl.pallas_call(
        body, grid=(dil, nb + 1), in_specs=_qkv_specs(nb, col0) + [tok, stat, stat],
        out_specs=[tok, dkv_spec, dkv_spec], out_shape=[jax.ShapeDtypeStruct((dil, sub, GROUP_W), BF16)] * 3,
        scratch_shapes=[pltpu.VMEM((3 * TQ, GROUP_W), F32)] * 2 + [pltpu.VMEM((GROUP_W, TQ), F32)]
        + [pltpu.VMEM((HEADS_PER_GROUP, 2 * TQ, TQ), F32)] * 2 + [pltpu.VMEM((HEADS_PER_GROUP, 2 * TQ, TQ), BF16)] * 2,
        name=f"attn_bwd_g{g}", compiler_params=_cparams(("arbitrary", "arbitrary"), 32))(
            *([qkv] * 7), dcomb, lse_tot, delta)


def _in_bwd_ln0(dgated, dqkv, w_nat, w_dil, dz1, x, g0, *, tm=256):
    S = x.shape[0]
    n_gated, n_in = len(dgated), 3 * N_GROUPS

    def body(*refs):
        g_refs, d_refs = refs[:n_gated], refs[n_gated:n_gated + n_in]
        wn_ref, *wd_refs = refs[n_gated + n_in:n_gated + n_in + N_GROUPS]
        dz_ref, x_ref, g_ref, gx_ref, st_ref, *tmp_ref = refs[n_gated + n_in + N_GROUPS:]
        dh = ALPHA * dz_ref[...]
        col = 0
        for ref in g_refs:
            for k in range(ref.shape[0]):
                dh = dh + _dot_nt(ref[k], wn_ref[:, col:col + D_MODEL])
                col += D_MODEL
        for g, (_, d) in enumerate(GROUPS):
            rows = [jnp.concatenate([d_refs[3 * g + k][r] for k in range(3)], axis=1) for r in range(d)]
            w = wn_ref[:, col:col + QKV_W] if d == 1 else wd_refs[g - 1][...]
            res = _dot_nt(jnp.concatenate(rows, axis=0), w)
            if d == 1:
                dh = dh + res
            else:
                n = tm // d
                dh = dh + _from_residue(lambda r: res[r * n:(r + 1) * n, :], d, tm, tmp_ref)
        xhat, rstd = _ln_stats(x_ref[...])
        gx_ref[...] = _ln_bwd(dh, xhat, rstd, g_ref[...])
        upd = _rows8([jnp.sum(dh * xhat, axis=0, keepdims=True), jnp.sum(dh, axis=0, keepdims=True)], D_MODEL)

        @pl.when(pl.program_id(0) == 0)
        def _():
            st_ref[...] = upd

        @pl.when(pl.program_id(0) != 0)
        def _():
            st_ref[...] += upd

    row = pl.BlockSpec((tm, D_MODEL), lambda i: (i, 0))
    g_specs = [pl.BlockSpec((a.shape[0], tm, D_MODEL), lambda i: (0, i, 0)) for a in dgated]
    d_specs = []
    for _, d in GROUPS:
        d_specs += [_res_spec(d, tm, GROUP_W)] * 3
    operands = list(dgated) + [a for grp in dqkv for a in grp] + [w_nat] + list(w_dil) + [dz1, x, g0]
    return pl.pallas_call(
        body, grid=(S // tm,),
        in_specs=g_specs + d_specs + [_resident(w_nat.shape)] + [_resident(w.shape) for w in w_dil]
        + [row, row, pl.BlockSpec((1, D_MODEL), lambda i: (0, 0))],
        out_specs=[row, pl.BlockSpec((SUBLANES, D_MODEL), lambda i: (0, 0))],
        out_shape=[jax.ShapeDtypeStruct((S, D_MODEL), F32), jax.ShapeDtypeStruct((SUBLANES, D_MODEL), F32)],
        scratch_shapes=_lane_scratch(tm, D_MODEL),
        name="in_bwd_ln0", compiler_params=_cparams(("arbitrary",), 52))(*operands)


HBM_SPEC = pl.BlockSpec(memory_space=pltpu.HBM)


def _place():
    x, y, c = lax.axis_index("x"), lax.axis_index("y"), lax.axis_index("c")
    chips = [(1 - x, y), (x, 1 - y), (1 - x, 1 - y)]
    return x, y, c, chips


def _allgather_shards(shards, after, *, name, collective_id):
    n = len(shards)
    per = 6

    def body(*refs):
        ins, outs = refs[:n], refs[n + len(after):2 * n + len(after)]
        send_sems, recv_sems, loc_sems = refs[2 * n + len(after):]
        x, y, c, chips = _place()
        me = 2 * x + y
        sib = (x, y, 1 - c)
        peers = [sib] + [(px, py, c) for px, py in chips]
        barrier = pltpu.get_barrier_semaphore()
        for peer in peers:
            pl.semaphore_signal(barrier, inc=1, device_id=peer, device_id_type=MESH)
        pl.semaphore_wait(barrier, len(peers))

        def rcopy(w, k, src, dst, to):
            return pltpu.make_async_remote_copy(src_ref=src, dst_ref=dst, send_sem=send_sems.at[per * w + k],
                                                recv_sem=recv_sems.at[per * w + k], device_id=to, device_id_type=MESH)

        split = [s.shape[0] == N_CORES for s in shards]
        half = lambda w: c if split[w] else 0
        local, sends = [], []
        for w in range(n):
            cp = pltpu.make_async_copy(ins[w], outs[w].at[me], loc_sems.at[w])
            cp.start()
            local.append(cp)
            for j, (px, py) in enumerate(chips):
                cp = rcopy(w, j, ins[w].at[half(w)], outs[w].at[me, half(w)], (px, py, c))
                cp.start()
                sends.append(cp)
        for w in range(n):
            for j, (px, py) in enumerate(chips):
                slot = outs[w].at[2 * px + py, half(w)]
                rcopy(w, j, slot, slot, (px, py, c)).wait_recv()
                if split[w]:
                    cp = rcopy(w, 3 + j, slot, slot, sib)
                    cp.start()
                    sends.append(cp)
        for w in range(n):
            if split[w]:
                for j, (px, py) in enumerate(chips):
                    slot = outs[w].at[2 * px + py, 1 - c]
                    rcopy(w, 3 + j, slot, slot, sib).wait_recv()
        for cp in sends:
            cp.wait_send()
        for cp in local:
            cp.wait()

    return pl.kernel(
        body, out_type=[jax.ShapeDtypeStruct((N_CHIPS,) + s.shape, s.dtype) for s in shards],
        mesh=plsc.ScalarSubcoreMesh(axis_name="sequencer", num_cores=1),
        scratch_types=[pltpu.SemaphoreType.DMA((per * n,)), pltpu.SemaphoreType.DMA((per * n,)),
                       pltpu.SemaphoreType.DMA((n,))],
        name=name, compiler_params=pltpu.CompilerParams(collective_id=collective_id))(*shards, *after)


def _exchange_grads(grads, *, name, collective_id):
    n = len(grads)
    per = 7

    def body(*refs):
        ins, outs = refs[:n], refs[n:2 * n]
        send_sems, recv_sems, loc_sems = refs[2 * n:]
        x, y, c, chips = _place()
        me = 2 * x + y
        sib = (x, y, 1 - c)
        peers = [sib] + [(px, py, c) for px, py in chips]
        barrier = pltpu.get_barrier_semaphore()
        for peer in peers:
            pl.semaphore_signal(barrier, inc=1, device_id=peer, device_id_type=MESH)
        pl.semaphore_wait(barrier, len(peers))

        def rcopy(w, k, src, dst, to):
            return pltpu.make_async_remote_copy(src_ref=src, dst_ref=dst, send_sem=send_sems.at[per * w + k],
                                                recv_sem=recv_sems.at[per * w + k], device_id=to, device_id_type=MESH)

        local, sends = [], []
        for w in range(n):
            cp = pltpu.make_async_copy(ins[w].at[me], outs[w].at[c, me], loc_sems.at[w])
            cp.start()
            local.append(cp)
            cp = rcopy(w, 0, ins[w].at[me], outs[w].at[c, me], sib)
            cp.start()
            sends.append(cp)
            for j, (px, py) in enumerate(chips):
                cp = rcopy(w, 1 + j, ins[w].at[2 * px + py], outs[w].at[c, me], (px, py, c))
                cp.start()
                sends.append(cp)
        for w in range(n):
            for j, (px, py) in enumerate(chips):
                slot = outs[w].at[c, 2 * px + py]
                rcopy(w, 1 + j, slot, slot, (px, py, c)).wait_recv()
                cp = rcopy(w, 4 + j, slot, slot, sib)
                cp.start()
                sends.append(cp)
        for w in range(n):
            slot = outs[w].at[1 - c, me]
            rcopy(w, 0, slot, slot, sib).wait_recv()
            for j, (px, py) in enumerate(chips):
                slot = outs[w].at[1 - c, 2 * px + py]
                rcopy(w, 4 + j, slot, slot, sib).wait_recv()
        for cp in sends:
            cp.wait_send()
        for cp in local:
            cp.wait()

    return pl.kernel(
        body, out_type=[jax.ShapeDtypeStruct((N_CORES,) + g.shape, g.dtype) for g in grads],
        mesh=plsc.ScalarSubcoreMesh(axis_name="sequencer", num_cores=1),
        scratch_types=[pltpu.SemaphoreType.DMA((per * n,)), pltpu.SemaphoreType.DMA((per * n,)),
                       pltpu.SemaphoreType.DMA((n,))],
        name=name, compiler_params=pltpu.CompilerParams(collective_id=collective_id))(*grads)


def _allgather_small(vec):
    def body(v_ref, o_ref, send_sems, recv_sems, loc_sem):
        x, y, c = lax.axis_index("x"), lax.axis_index("y"), lax.axis_index("c")
        me = 4 * x + 2 * y + c

        def peer(k):
            flip = lambda v, bit: 1 - v if (k >> bit) & 1 else v
            return flip(x, 2), flip(y, 1), flip(c, 0)

        loc = pltpu.make_async_copy(v_ref, o_ref.at[me], loc_sem)
        loc.start()
        sends = []
        for k in range(1, N_DEV):
            cp = pltpu.make_async_remote_copy(src_ref=v_ref, dst_ref=o_ref.at[me], send_sem=send_sems.at[k - 1],
                                              recv_sem=recv_sems.at[k - 1], device_id=peer(k), device_id_type=MESH)
            cp.start()
            sends.append(cp)
        for k in range(1, N_DEV):
            px, py, pc = peer(k)
            pltpu.make_async_remote_copy(src_ref=v_ref, dst_ref=o_ref.at[4 * px + 2 * py + pc],
                                         send_sem=send_sems.at[k - 1], recv_sem=recv_sems.at[k - 1],
                                         device_id=(px, py, pc), device_id_type=MESH).wait_recv()
        for cp in sends:
            cp.wait_send()
        loc.wait()

    return pl.pallas_call(
        body, in_specs=[HBM_SPEC], out_specs=HBM_SPEC,
        out_shape=jax.ShapeDtypeStruct((N_DEV,) + vec.shape, vec.dtype),
        scratch_shapes=[pltpu.SemaphoreType.DMA((N_DEV - 1,)), pltpu.SemaphoreType.DMA((N_DEV - 1,)),
                        pltpu.SemaphoreType.DMA],
        name="allgather_small")(vec)


def _adamw(w, g, m, v):
    m = ADAM_B1 * m + (1.0 - ADAM_B1) * g
    v = ADAM_B2 * v + (1.0 - ADAM_B2) * (g * g)
    m_hat = m / (1.0 - ADAM_B1 ** ADAM_STEP)
    v_hat = v / (1.0 - ADAM_B2 ** ADAM_STEP)
    delta = -ADAM_LR * (m_hat / (jnp.sqrt(v_hat) + ADAM_EPS) + ADAM_WD * w)
    return delta, m, v


def _reduce_adamw(parts, w, m, v, *, tr, name):
    R, C = w.shape

    def body(p_ref, w_ref, m_ref, v_ref, g_ref, d_ref, nm_ref, nv_ref):
        def core_sum(cc):
            s = p_ref[cc, 0].astype(F32)
            for k in range(1, N_CHIPS):
                s = s + p_ref[cc, k].astype(F32)
            return s

        g = core_sum(0) + core_sum(1)
        delta, nm, nv = _adamw(w_ref[...], g, m_ref[...], v_ref[...])
        g_ref[...] = g
        d_ref[...] = delta
        nm_ref[...] = nm
        nv_ref[...] = nv

    blk = pl.BlockSpec((tr, C), lambda i: (i, 0))
    return pl.pallas_call(
        body, grid=(R // tr,),
        in_specs=[pl.BlockSpec((N_CORES, N_CHIPS, tr, C), lambda i: (0, 0, i, 0)), blk, blk, blk],
        out_specs=[blk] * 4, out_shape=[jax.ShapeDtypeStruct((R, C), F32)] * 4,
        name=name, compiler_params=_cparams(("parallel",), 40))(parts, w, m, v)


def _reduce_adamw_vectors(allv, offs, ws, ms, vs):
    n = len(ws)

    def body(a_ref, *refs):
        w_refs, m_refs, v_refs = refs[:n], refs[n:2 * n], refs[2 * n:3 * n]
        tot_ref, outs = refs[3 * n], refs[3 * n + 1:]
        s = a_ref[0]
        for d in range(1, N_DEV):
            s = s + a_ref[d]
        tot_ref[...] = s
        for k in range(n):
            g = s[:, offs[k]:offs[k] + w_refs[k].shape[1]]
            delta, nm, nv = _adamw(w_refs[k][...], g, m_refs[k][...], v_refs[k][...])
            for ref, val in zip(outs[4 * k:4 * k + 4], (g, delta, nm, nv)):
                ref[...] = val

    out_shape = [jax.ShapeDtypeStruct(allv.shape[1:], F32)]
    for w in ws:
        out_shape += [jax.ShapeDtypeStruct(w.shape, F32)] * 4
    res = pl.pallas_call(body, out_shape=out_shape, name="reduce_adamw_vectors",
                         compiler_params=_cparams((), 40))(allv, *ws, *ms, *vs)
    return res[0], [tuple(res[1 + 4 * k:5 + 4 * k]) for k in range(n)]


def _adamw_taps(ws, gs, ms, vs):
    n = len(ws)

    def body(*refs):
        outs = refs[4 * n:]
        for k in range(n):
            res = _adamw(refs[k][...], refs[n + k][...], refs[2 * n + k][...], refs[3 * n + k][...])
            for ref, val in zip(outs[3 * k:3 * k + 3], res):
                ref[...] = val

    out_shape = []
    for w in ws:
        out_shape += [jax.ShapeDtypeStruct(w.shape, F32)] * 3
    res = pl.pallas_call(body, out_shape=out_shape, name="adamw_taps")(*ws, *gs, *ms, *vs)
    return [tuple(res[3 * k:3 * k + 3]) for k in range(n)]


def _pack(pieces):
    flat, offs, n = [], [], 0
    for p in pieces:
        size = -(-p.size // LANES) * LANES
        flat.append(jnp.pad(p.reshape(-1), (0, size - p.size)))
        offs.append(n)
        n += size
    return jnp.concatenate(flat).reshape(1, n), offs


def _local_step(x, target, p, wfull, on_ready=lambda group: None, before_ln0=()):
    S = x.shape[0]
    dils = [d for _, d in GROUPS]

    h0, h0b, *h0_res = _ln0_fwd(x, p["ln0_g"], p["ln0_b"], before_ln0)
    h0_rows = [h0b] + [h.reshape(S, D_MODEL) for h in h0_res]

    if callable(wfull):
        wfull = wfull(h0b)
    w_in3, w_up3 = wfull["w_in"], wfull["w_up"]
    w_a, w_o, w_down, w_b = wfull["w_a"], wfull["w_o"], wfull["w_down"], wfull["w_b"]
    conv_w, ffn_conv_w = wfull["conv_w"], wfull["ffn_conv_w"]

    runs = _col_runs()
    here_widths = (N_NAT,) + (QKV_W,) * len(DILS)
    copies = []
    for shard, col, pcol, width in runs:
        d, dcol = _here_piece(pcol, here_widths)
        copies.append((0, (shard, slice(None), pl.ds(col, width)), d, (slice(None), pl.ds(dcol, width))))
    w_nat, *w_dil = _dma_copies([w_in3], [jax.ShapeDtypeStruct((D_MODEL, w), BF16) for w in here_widths], copies,
                                name="permute_w_in")
    w_qkv = [None] + w_dil
    b_blocks = p["b_in"].reshape(N_BLK, GROUP_W)
    b_perm = jnp.concatenate([b_blocks[b] for b in PERM]).reshape(1, N_IN)
    b_nat = b_perm[:, :N_NAT]
    qkv_cols = [slice(P_Q0 + g * QKV_W, P_Q0 + (g + 1) * QKV_W) for g in range(N_GROUPS)]

    proj = _mm_nn(h0b, w_nat, b_nat, tm=512, tn=N_NAT // 2, out_dtype=BF16, name="proj")
    qkv = [proj[None]]
    for g in range(1, N_GROUPS):
        t = _mm_nn(h0_rows[g], w_qkv[g], b_perm[:, qkv_cols[g]], tm=512, tn=QKV_W, out_dtype=BF16, name=f"proj_qkv{g}")
        qkv.append(t.reshape(dils[g], S // dils[g], QKV_W))
    col0 = [P_Q0 // GROUP_W] + [0] * (N_GROUPS - 1)
    ya_in = _conv_gate_fwd(proj, conv_w)
    att = [_attn_fwd(qkv[g], col0[g], g) for g in range(N_GROUPS)]
    comb, comb_b, lse_tot = _attn_combine([a[0] for a in att], [a[1] for a in att])
    yab, mixin = _branch_mix(ya_in, comb_b, w_a, w_b, proj)
    xhat1, rstd1, h1b = _mix_ln1(mixin, w_o, p["b_o"], h0, p["ln1_g"], p["ln1_b"])
    up = _mm_nn(h1b, w_up3, p["b_up"], tm=512, tn=w_up3.shape[2], out_dtype=BF16, name="up")
    f = _ffn_conv_fwd(up, ffn_conv_w, p["ffn_conv_b"])
    dz2, dz2b, st2 = _down_ln2_loss(f, w_down, p["b_down"], xhat1, p["ln1_g"], p["ln1_b"],
                                    p["ln2_g"], p["ln2_b"], target)

    gw = {}
    gw["w_down"] = _mm_tn(f, dz2b, n_out=1, tn=D_MODEL, ts=1024, g_block=(1024, D_MODEL),
                          g_map=lambda j, s: (s, 0), name="grad_w_down").reshape(N_CHIPS, D_FF // N_CHIPS, D_MODEL)
    df = _mm_nt(dz2b, w_down, tm=512, name="df")
    dup, sm_ffn = _ffn_conv_bwd(up, df, ffn_conv_w, p["ffn_conv_b"])
    up_tn = w_up3.shape[2]
    up_pp = D_FF // up_tn
    gw["w_up"] = _mm_tn(h1b, dup, n_out=N_CHIPS, tn=up_tn, ts=1024, g_block=(None, 1024, up_tn),
                        g_map=lambda j, s: (j // up_pp, s, j % up_pp), name="grad_w_up")
    on_ready({n: gw[n] for n in ("w_down", "w_up")})
    dz1, dz1b, st1 = _up_bwd_ln1(dup, w_up3, dz2, xhat1, rstd1, p["ln1_g"])

    gw["w_o"] = _mm_tn(mixin, dz1b, n_out=1, tn=D_MODEL, ts=512, g_block=(512, D_MODEL),
                       g_map=lambda j, s: (s, 0), name="grad_w_o").reshape(N_CHIPS, D_MODEL // N_CHIPS, D_MODEL)
    dyab, dgab = _mix_bwd(dz1b, w_o, proj, yab)
    gw["w_a"] = _mm_tn(ya_in, dyab, n_out=1, tn=D_MODEL, ts=512, g_block=(512, D_MODEL),
                       g_map=lambda j, s: (s, 0), name="grad_w_a").reshape(N_CHIPS, D_CONV // N_CHIPS, D_MODEL)
    gw_b = _mm_tn(comb_b, dyab, n_out=1, tn=D_MODEL, ts=1024, g_block=(1024, D_MODEL),
                  g_map=lambda j, s: (s, 1), name="grad_w_b")
    gw["w_b"] = gw_b.reshape(GROUP_W, N_CHIPS, D_MODEL // N_CHIPS).transpose(1, 0, 2)
    on_ready({n: gw[n] for n in ("w_o", "w_a", "w_b")})
    dya_in = _mm_nt(dyab, w_a, tm=512, a_col=0, name="dya_in")
    dbch, sm_conv = _conv_gate_bwd(proj, dya_in, conv_w)
    att_stats = _comb_bwd(dyab, w_b, comb, lse_tot)
    dqkv = [_attn_bwd(qkv[g], col0[g], g, *att_stats[g]) for g in range(N_GROUPS)]

    w_pieces, b_pieces = [], []
    for nm, planes in (("bch", dbch), ("gab", dgab)):
        pw, pc = _mm_tn(h0b, planes, n_out=planes.shape[0], tn=D_MODEL, ts=1024, g_block=(None, 1024, D_MODEL),
                        g_map=lambda j, s: (j, s, 0), colsum=True, name="grad_w_in_" + nm)
        w_pieces.append(pw)
        b_pieces.append(pc[0])
    for g in range(N_GROUPS):
        pw, pc = _mm_tn_cat(h0_rows[g], [a.reshape(S, GROUP_W) for a in dqkv[g]], ts=1024, name=f"grad_w_in_qkv{g}")
        w_pieces.append(pw)
        b_pieces.append(pc[0])
    piece_widths = [a.shape[-1] * (a.shape[0] if a.ndim == 3 else 1) for a in w_pieces]
    copies = []
    for shard, col, pcol, width in runs:
        s, scol = _here_piece(pcol, piece_widths)
        if w_pieces[s].ndim == 3:
            src = (scol // D_MODEL, slice(None), pl.ds(scol % D_MODEL, width))
        else:
            src = (slice(None), pl.ds(scol, width))
        copies.append((s, src, 0, (shard, slice(None), pl.ds(col, width))))
    gw["w_in"], = _dma_copies(w_pieces, [jax.ShapeDtypeStruct((N_CHIPS, D_MODEL, N_IN // N_CHIPS), BF16)], copies,
                              name="unpermute_grad_w_in")
    on_ready({"w_in": gw["w_in"]})
    db_blocks = jnp.concatenate(b_pieces).reshape(N_BLK, GROUP_W)
    grad_b_in = jnp.concatenate([db_blocks[b] for b in INV_PERM])

    grad_x, st0 = _in_bwd_ln0([dbch, dgab], dqkv, w_nat, w_qkv[1:], dz1, x, p["ln0_g"])

    small = {
        "loss": st2[2:3, 0:1],
        "ln0_g": st0[0], "ln0_b": st0[1], "b_in": grad_b_in, "conv_w": sm_conv[0:3],
        "b_o": st1[2], "ln1_g": st1[0], "ln1_b": st1[1],
        "b_up": jnp.concatenate([sm_ffn[0], sm_ffn[1]]), "ffn_conv_w": sm_ffn[3:6], "ffn_conv_b": sm_ffn[2],
        "b_down": st2[3], "ln2_g": st2[0], "ln2_b": st2[1],
    }
    return grad_x, gw, small


BIG = ("w_in", "w_a", "w_b", "w_o", "w_up", "w_down")
CONV = ("conv_w", "ffn_conv_w")
VECS = ("ln0_g", "ln0_b", "b_in", "b_o", "ln1_g", "ln1_b", "b_up", "ffn_conv_b", "b_down", "ln2_g", "ln2_b")
ORDER = ("ln0_g", "ln0_b", "w_in", "b_in", "conv_w", "w_a", "w_b", "w_o", "b_o", "ln1_g", "ln1_b", "w_up", "b_up",
         "ffn_conv_w", "ffn_conv_b", "w_down", "b_down", "ln2_g", "ln2_b")
SMALL_ORDER = ("loss",) + VECS + CONV


def _step(x, target, W, Mo, Vo):
    x2, t2 = x[0], target[0]
    big2 = {n: W[n][0] for n in BIG}
    halves = lambda a: a.astype(BF16).reshape(N_CORES, a.shape[0] // N_CORES, a.shape[1])
    whole = lambda g: g.reshape(N_CHIPS, g.shape[1] * g.shape[2], g.shape[3])
    later = tuple(n for n in BIG if n != "w_in")
    w_in_halves = halves(big2["w_in"])
    first = _allgather_shards([w_in_halves], [], name="allgather_w_in", collective_id=1)

    def gather_rest(h0b):
        rest = _allgather_shards([halves(big2[n]) for n in later] + [W[n] for n in CONV], [h0b],
                                 name="allgather_rest", collective_id=2)
        gathered = {n: whole(g) for n, g in zip(("w_in",) + later + CONV, first + rest)}
        return {
            "w_in": gathered["w_in"], "w_up": gathered["w_up"],
            "w_a": gathered["w_a"].reshape(D_CONV, D_MODEL), "w_o": gathered["w_o"].reshape(D_MODEL, D_MODEL),
            "w_down": gathered["w_down"].reshape(D_FF, D_MODEL),
            "w_b": gathered["w_b"].transpose(1, 0, 2).reshape(GROUP_W, D_MODEL),
            "conv_w": gathered["conv_w"].transpose(1, 0, 2).reshape(3, D_CONV),
            "ffn_conv_w": gathered["ffn_conv_w"].transpose(1, 0, 2).reshape(3, D_FF),
        }

    pvec = {n: W[n].reshape(1, -1) for n in VECS}

    parts = {}
    exchange_ids = iter((3, 4, 5))

    def exchange(group):
        names = tuple(group)
        res = _exchange_grads([group[n] for n in names], name="exchange_" + "_".join(names),
                              collective_id=next(exchange_ids))
        parts.update(zip(names, res))

    grad_x, _, small = _local_step(x2, t2, pvec, gather_rest, exchange, before_ln0=[w_in_halves])
    out = {}
    for n in BIG:
        tr = {"w_in": 128, "w_up": 128, "w_b": 128}.get(n, big2[n].shape[0] // 4)
        g, d, nm, nv = _reduce_adamw(parts[n], big2[n], Mo[n][0], Vo[n][0], tr=tr, name="adamw_" + n)
        out[n] = tuple(a[None] for a in (g, d, nm, nv))

    vec, offs = _pack([small[n] for n in SMALL_ORDER])
    off = dict(zip(SMALL_ORDER, offs))
    row = lambda a: a.reshape(1, -1)
    tot, vec_out = _reduce_adamw_vectors(_allgather_small(vec), [off[n] for n in VECS], [row(W[n]) for n in VECS],
                                         [row(Mo[n]) for n in VECS], [row(Vo[n]) for n in VECS])
    for n, res in zip(VECS, vec_out):
        out[n] = tuple(a.reshape(W[n].shape) for a in res)
    loss = tot[0, off["loss"]]
    chip = 2 * lax.axis_index("x") + lax.axis_index("y")
    taps_g = []
    for n in CONV:
        width = W[n].shape[2]
        full = lax.slice(tot, (0, off[n]), (1, off[n] + 3 * N_CHIPS * width)).reshape(3, N_CHIPS * width)
        taps_g.append(lax.dynamic_slice_in_dim(full, chip * width, width, axis=1))
    taps_out = _adamw_taps([W[n][0] for n in CONV], taps_g, [Mo[n][0] for n in CONV], [Vo[n][0] for n in CONV])
    for n, g, res in zip(CONV, taps_g, taps_out):
        out[n] = tuple(a[None] for a in (g,) + res)

    res = [loss, grad_x[None]]
    for k in range(4):
        res += [out[n][k] for n in ORDER]
    return tuple(res)


def kernel(x, ln0_g, ln0_b, w_in, b_in, conv_w, w_a, w_b, w_o, b_o, ln1_g, ln1_b, w_up, b_up, ffn_conv_w, ffn_conv_b, w_down, b_down, ln2_g, ln2_b, loss_target, m_ln0_g, m_ln0_b, m_w_in, m_b_in, m_conv_w, m_w_a, m_w_b, m_w_o, m_b_o, m_ln1_g, m_ln1_b, m_w_up, m_b_up, m_ffn_conv_w, m_ffn_conv_b, m_w_down, m_b_down, m_ln2_g, m_ln2_b, v_ln0_g, v_ln0_b, v_w_in, v_b_in, v_conv_w, v_w_a, v_w_b, v_w_o, v_b_o, v_ln1_g, v_ln1_b, v_w_up, v_b_up, v_ffn_conv_w, v_ffn_conv_b, v_w_down, v_b_down, v_ln2_g, v_ln2_b):
    W = dict(zip(ORDER, (ln0_g, ln0_b, w_in, b_in, conv_w, w_a, w_b, w_o, b_o, ln1_g, ln1_b, w_up, b_up,
                         ffn_conv_w, ffn_conv_b, w_down, b_down, ln2_g, ln2_b)))
    Mo = dict(zip(ORDER, (m_ln0_g, m_ln0_b, m_w_in, m_b_in, m_conv_w, m_w_a, m_w_b, m_w_o, m_b_o, m_ln1_g, m_ln1_b,
                          m_w_up, m_b_up, m_ffn_conv_w, m_ffn_conv_b, m_w_down, m_b_down, m_ln2_g, m_ln2_b)))
    Vo = dict(zip(ORDER, (v_ln0_g, v_ln0_b, v_w_in, v_b_in, v_conv_w, v_w_a, v_w_b, v_w_o, v_b_o, v_ln1_g, v_ln1_b,
                          v_w_up, v_b_up, v_ffn_conv_w, v_ffn_conv_b, v_w_down, v_b_down, v_ln2_g, v_ln2_b)))
    return _step(x, loss_target, W, Mo, Vo)
```

```python
import functools
import math

import jax
import jax.numpy as jnp
from jax import lax
from jax.experimental import pallas as pl
from jax.experimental.pallas import tpu as pltpu
from jax.experimental.pallas import tpu_sc as plsc

F32 = jnp.float32
BF16 = jnp.bfloat16

D_MODEL = 1024
D_CONV = D_MODEL
HEAD_DIM = 64
HEADS_PER_GROUP = 8
GROUPS = ((128, 1), (512, 4), (2048, 16))
N_GROUPS = len(GROUPS)
GROUP_W = HEADS_PER_GROUP * HEAD_DIM
QKV_W = N_GROUPS * GROUP_W
RADIUS = 64
D_FF = 2816
LN_EPS = 1e-5
ALPHA = 2.0 ** 0.25
MASK_VALUE = -1e30
ATT_SCALE = HEAD_DIM ** -0.5
OFF_B = 0
OFF_C = OFF_B + D_CONV
OFF_H = OFF_C + D_CONV
OFF_Q = OFF_H + D_CONV
OFF_K = OFF_Q + QKV_W
OFF_V = OFF_K + QKV_W
OFF_GA = OFF_V + QKV_W
OFF_GB = OFF_GA + D_MODEL
N_IN = OFF_GB + D_MODEL
ADAM_LR = 0.001
ADAM_B1 = 0.9
ADAM_B2 = 0.999
ADAM_EPS = 1e-08
ADAM_WD = 0.01
ADAM_STEP = 10
INV_SQRT2 = 0.7071067811865476
INV_SQRT_2PI = 0.3989422804014327

LANES = 128
SUBLANES = 8
VMEM_BYTES_V7X = 64 * 1024 * 1024
N_CHIPS = 4
N_CORES = 2
N_DEV = N_CHIPS * N_CORES
MESH = pl.DeviceIdType.MESH

N_BLK = N_IN // GROUP_W
PERM = (0, 1, 2, 3, 4, 5, 15, 16, 17, 18, 6, 9, 12, 7, 10, 13, 8, 11, 14)
INV_PERM = tuple(PERM.index(b) for b in range(N_BLK))
P_B, P_C, P_H, P_GA, P_GB, P_Q0 = 0, 1024, 2048, 3072, 4096, 5120
N_NAT = P_Q0 + QKV_W // N_GROUPS * 3
N_GATED = P_Q0

SLAB = 128
CHUNK = 256
PAD = SUBLANES
TQ = 128


def _cparams(sem, vmem_mb):
    assert vmem_mb * 1024 * 1024 < VMEM_BYTES_V7X
    return pltpu.CompilerParams(dimension_semantics=sem, vmem_limit_bytes=vmem_mb * 1024 * 1024)


def _dot(a, b):
    return jnp.dot(a, b, preferred_element_type=F32)


def _dot_nt(a, b):
    return lax.dot_general(a, b, (((1,), (1,)), ((), ())), preferred_element_type=F32)


def _dot_tn(a, b):
    return lax.dot_general(a, b, (((0,), (0,)), ((), ())), preferred_element_type=F32)


def _ln_stats(z):
    mu = jnp.mean(z, -1, keepdims=True)
    zc = z - mu
    var = jnp.mean(zc * zc, -1, keepdims=True)
    rstd = lax.rsqrt(var + LN_EPS)
    return zc * rstd, rstd


def _ln_bwd(dh, xhat, rstd, g):
    dxh = dh * g
    m1 = jnp.mean(dxh, -1, keepdims=True)
    m2 = jnp.mean(dxh * xhat, -1, keepdims=True)
    return rstd * (dxh - m1 - xhat * m2)


def _rows8(rows, width):
    pad = [jnp.zeros((1, width), F32)] * (SUBLANES - len(rows))
    return jnp.concatenate(list(rows) + pad, axis=0)


def _mm_nn(a, w, bias, *, tm, tn, out_dtype, name, vmem_mb=40):
    M, K = a.shape
    if w.ndim == 3:
        assert w.shape[2] == tn
        n_tiles = w.shape[0]
        w_spec = pl.BlockSpec((None, K, tn), lambda j, i: (j, 0, 0))
    else:
        n_tiles = w.shape[1] // tn
        w_spec = pl.BlockSpec((K, tn), lambda j, i: (0, j))

    def body(a_ref, w_ref, b_ref, o_ref):
        o_ref[...] = (_dot(a_ref[...], w_ref[...]) + b_ref[...]).astype(o_ref.dtype)

    return pl.pallas_call(
        body, grid=(n_tiles, M // tm),
        in_specs=[pl.BlockSpec((tm, K), lambda j, i: (i, 0)), w_spec, pl.BlockSpec((1, tn), lambda j, i: (0, j))],
        out_specs=pl.BlockSpec((tm, tn), lambda j, i: (i, j)),
        out_shape=jax.ShapeDtypeStruct((M, n_tiles * tn), out_dtype),
        name=name, compiler_params=_cparams(("arbitrary", "parallel"), vmem_mb))(a, w, bias)


def _mm_nt(a, w, *, tm, a_col=0, name, vmem_mb=40):
    M = a.shape[0]
    N, K = w.shape

    def body(a_ref, w_ref, o_ref):
        o_ref[...] = _dot_nt(a_ref[...], w_ref[...]).astype(o_ref.dtype)

    return pl.pallas_call(
        body, grid=(M // tm,),
        in_specs=[pl.BlockSpec((tm, K), lambda i: (i, a_col)),
                  pl.BlockSpec((N, K), lambda i: (0, 0))],
        out_specs=pl.BlockSpec((tm, N), lambda i: (i, 0)),
        out_shape=jax.ShapeDtypeStruct((M, N), BF16),
        name=name, compiler_params=_cparams(("parallel",), vmem_mb))(a, w)


def _mm_tn(a, g, *, n_out, tn, ts, g_block, g_map, colsum=False, name, vmem_mb=48):
    S, K = a.shape
    n_s = S // ts

    def body(a_ref, g_ref, *rest):
        if colsum:
            o_ref, cs_ref, acc_ref, cacc_ref = rest
        else:
            o_ref, acc_ref = rest
        s = pl.program_id(1)

        @pl.when(s == 0)
        def _():
            acc_ref[...] = jnp.zeros_like(acc_ref)
            if colsum:
                cacc_ref[...] = jnp.zeros_like(cacc_ref)

        gv = g_ref[...]
        acc_ref[...] += _dot_tn(a_ref[...], gv)
        if colsum:
            cacc_ref[...] += jnp.broadcast_to(jnp.sum(gv.astype(F32), axis=0, keepdims=True), cacc_ref.shape)

        @pl.when(s == n_s - 1)
        def _():
            o_ref[...] = acc_ref[...].astype(o_ref.dtype)
            if colsum:
                cs_ref[...] = cacc_ref[...]

    out_specs = [pl.BlockSpec((None, K, tn), lambda j, s: (j, 0, 0))]
    out_shape = [jax.ShapeDtypeStruct((n_out, K, tn), BF16)]
    scratch = [pltpu.VMEM((K, tn), F32)]
    if colsum:
        out_specs.append(pl.BlockSpec((SUBLANES, tn), lambda j, s: (0, j)))
        out_shape.append(jax.ShapeDtypeStruct((SUBLANES, n_out * tn), F32))
        scratch.append(pltpu.VMEM((SUBLANES, tn), F32))
    res = pl.pallas_call(
        body, grid=(n_out, n_s),
        in_specs=[pl.BlockSpec((ts, K), lambda j, s: (s, 0)), pl.BlockSpec(g_block, g_map)],
        out_specs=out_specs, out_shape=out_shape, scratch_shapes=scratch,
        name=name, compiler_params=_cparams(("parallel", "arbitrary"), vmem_mb))(a, g)
    return res if colsum else res[0]


def _mm_tn_cat(a, gs, *, ts, name, vmem_mb=40):
    S, K = a.shape
    widths = [g.shape[1] for g in gs]
    n_s, total = S // ts, sum(widths)

    def body(*refs):
        a_ref, g_refs = refs[0], refs[1:1 + len(gs)]
        o_ref, cs_ref, acc_ref, cacc_ref = refs[1 + len(gs):]
        s = pl.program_id(0)

        @pl.when(s == 0)
        def _():
            acc_ref[...] = jnp.zeros_like(acc_ref)
            cacc_ref[...] = jnp.zeros_like(cacc_ref)

        av, col = a_ref[...], 0
        for g_ref, w in zip(g_refs, widths):
            gv = g_ref[...]
            acc_ref[:, col:col + w] += _dot_tn(av, gv)
            cacc_ref[:, col:col + w] += jnp.broadcast_to(jnp.sum(gv.astype(F32), axis=0, keepdims=True), (SUBLANES, w))
            col += w

        @pl.when(s == n_s - 1)
        def _():
            o_ref[...] = acc_ref[...].astype(BF16)
            cs_ref[...] = cacc_ref[...]

    return pl.pallas_call(
        body, grid=(n_s,),
        in_specs=[pl.BlockSpec((ts, K), lambda s: (s, 0))] + [pl.BlockSpec((ts, w), lambda s: (s, 0)) for w in widths],
        out_specs=[pl.BlockSpec((K, total), lambda s: (0, 0)), pl.BlockSpec((SUBLANES, total), lambda s: (0, 0))],
        out_shape=[jax.ShapeDtypeStruct((K, total), BF16), jax.ShapeDtypeStruct((SUBLANES, total), F32)],
        scratch_shapes=[pltpu.VMEM((K, total), F32), pltpu.VMEM((SUBLANES, total), F32)],
        name=name, compiler_params=_cparams(("arbitrary",), vmem_mb))(a, *gs)


DILS = tuple(d for _, d in GROUPS if d > 1)


def _res_spec(d, tm, width):
    return pl.BlockSpec((d, tm // d, width), lambda i: (0, i, 0))


def _lane_scratch(tm, width):
    return [pltpu.VMEM((tm, LANES), F32)] * (width // LANES)


def _to_residue(val, dst_refs, dils, tm, dtype, scr):
    for c, ref in enumerate(scr):
        ref[...] = val[:, c * LANES:(c + 1) * LANES]
    for dst_ref, d in zip(dst_refs, dils):
        for r in range(d):
            cols = [ref[pl.ds(r, tm // d, stride=d), :] for ref in scr]
            dst_ref[r] = jnp.concatenate(cols, axis=1).astype(dtype)


def _from_residue(rows_of, d, tm, scr):
    for r in range(d):
        v = rows_of(r).astype(F32)
        for c, ref in enumerate(scr):
            ref[pl.ds(r, tm // d, stride=d), :] = v[:, c * LANES:(c + 1) * LANES]
    return jnp.concatenate([ref[...] for ref in scr], axis=1)


def _ln0_fwd(x, g, b, after=(), *, tm=512):
    S, Dm = x.shape
    n_after = len(after)

    def body(x_ref, g_ref, b_ref, *rest):
        h_ref, hb_ref, *rest = rest[n_after:]
        xhat, _ = _ln_stats(x_ref[...])
        h = xhat * g_ref[...] + b_ref[...]
        h_ref[...] = h
        hb_ref[...] = h.astype(BF16)
        _to_residue(h, rest[:len(DILS)], DILS, tm, BF16, rest[len(DILS):])

    row = pl.BlockSpec((tm, Dm), lambda i: (i, 0))
    vec = pl.BlockSpec((1, Dm), lambda i: (0, 0))
    return pl.pallas_call(
        body, grid=(S // tm,), in_specs=[row, vec, vec] + [pl.BlockSpec(memory_space=pl.ANY)] * n_after,
        out_specs=[row, row] + [_res_spec(d, tm, Dm) for d in DILS],
        out_shape=[jax.ShapeDtypeStruct((S, Dm), F32), jax.ShapeDtypeStruct((S, Dm), BF16)]
        + [jax.ShapeDtypeStruct((d, S // d, Dm), BF16) for d in DILS],
        scratch_shapes=_lane_scratch(tm, Dm),
        name="ln0_fwd", compiler_params=_cparams(("parallel",), 32))(x, g, b, *after)


def _slab_spec(S, col0):
    return pl.BlockSpec((S, SLAB), lambda j: (0, col0 // SLAB + j))


def _zero_pads(scr, S):
    scr[0:PAD, :] = jnp.zeros((PAD, SLAB), F32)
    scr[S + PAD:S + 2 * PAD, :] = jnp.zeros((PAD, SLAB), F32)


def _shifted(scr, t):
    return (scr[PAD - 1 + t:PAD - 1 + t + CHUNK, :], scr[PAD + t:PAD + t + CHUNK, :],
            scr[PAD + 1 + t:PAD + 1 + t + CHUNK, :])


def _conv_gate_fwd(proj, conv_w):
    S = proj.shape[0]

    def body(b_ref, c_ref, h_ref, w_ref, o_ref, u_scr):
        _zero_pads(u_scr, S)
        for t in range(0, S, CHUNK):
            u_scr[PAD + t:PAD + t + CHUNK, :] = c_ref[t:t + CHUNK, :].astype(F32) * h_ref[t:t + CHUNK, :].astype(F32)
        w0, w1, w2 = w_ref[0:1, :], w_ref[1:2, :], w_ref[2:3, :]
        for t in range(0, S, CHUNK):
            um, u0, up = _shifted(u_scr, t)
            cv = w0 * um + w1 * u0 + w2 * up
            o_ref[t:t + CHUNK, :] = (b_ref[t:t + CHUNK, :].astype(F32) * cv).astype(BF16)

    return pl.pallas_call(
        body, grid=(D_CONV // SLAB,),
        in_specs=[_slab_spec(S, P_B), _slab_spec(S, P_C), _slab_spec(S, P_H),
                  pl.BlockSpec((3, SLAB), lambda j: (0, j))],
        out_specs=pl.BlockSpec((S, SLAB), lambda j: (0, j)),
        out_shape=jax.ShapeDtypeStruct((S, D_CONV), BF16),
        scratch_shapes=[pltpu.VMEM((S + 2 * PAD, SLAB), F32)],
        name="conv_gate_fwd", compiler_params=_cparams(("parallel",), 40))(proj, proj, proj, conv_w)


MASKED_DISTANCE = -1e34


def _attn_bias(i, sub, dil):
    j = lax.broadcasted_iota(jnp.int32, (2 * TQ, TQ), 0)
    a = lax.broadcasted_iota(jnp.int32, (2 * TQ, TQ), 1)
    rel = jnp.abs(j - RADIUS - a)
    kpos = i * TQ - RADIUS + j
    valid = (rel <= RADIUS) & (kpos >= 0) & (kpos < sub)
    return jnp.where(valid, -(rel * dil).astype(F32), MASKED_DISTANCE)


def _head_stats(rows):
    pad = jnp.zeros((LANES - len(rows), TQ), F32)
    return jnp.concatenate(list(rows) + [pad], axis=0).T


def _slope(g, h):
    return 2.0 ** (-8.0 * (g * HEADS_PER_GROUP + h + 1) / (N_GROUPS * HEADS_PER_GROUP))


def _window(p_ref, c_ref, n_ref):
    return jnp.concatenate([p_ref[TQ - RADIUS:, :], c_ref[...], n_ref[:RADIUS, :]], axis=0)


def _pair(a, h):
    return a[:, (h // 2) * LANES:(h // 2 + 1) * LANES]


def _own_lanes(a, h):
    lane = lax.broadcasted_iota(jnp.int32, a.shape, 1)
    return jnp.where((lane >= HEAD_DIM) == (h % 2 == 1), a, jnp.zeros_like(a))


def _own_rows(a, h):
    return a[(h % 2) * HEAD_DIM:(h % 2 + 1) * HEAD_DIM, :]


def _qkv_specs(nb, col0):
    def spec(col, shift):
        return pl.BlockSpec((None, TQ, GROUP_W), lambda r, i: (r, jnp.clip(i + shift, 0, nb - 1), col))

    return [spec(col0, 0), spec(col0 + 1, -1), spec(col0 + 1, 0), spec(col0 + 1, 1),
            spec(col0 + 2, -1), spec(col0 + 2, 0), spec(col0 + 2, 1)]


def _attn_fwd(qkv, col0, g):
    dil, sub, _ = qkv.shape
    nb = sub // TQ

    def body(q_ref, kp, kc, kn, vp, vc, vn, o_ref, lse_ref, ot_scr, s_scr, p_scr):
        bias = _attn_bias(pl.program_id(1), sub, dil)
        kwin = _window(kp, kc, kn)
        vwin = _window(vp, vc, vn)
        q = q_ref[...] * ATT_SCALE
        for h in range(HEADS_PER_GROUP):
            s_scr[h] = _dot_nt(_pair(kwin, h), _own_lanes(_pair(q, h), h))
        lse, inv_den = [], []
        for h in range(HEADS_PER_GROUP):
            s = s_scr[h] + _slope(g, h) * bias
            m = jnp.max(s, axis=0, keepdims=True)
            p = jnp.exp(s - m)
            den = jnp.sum(p, axis=0, keepdims=True)
            p_scr[h] = p.astype(BF16)
            inv_den.append(1.0 / den)
            lse.append(m + jnp.log(den))
        for h in range(HEADS_PER_GROUP):
            ot = _dot_tn(_pair(vwin, h), p_scr[h])
            ot_scr[h * HEAD_DIM:(h + 1) * HEAD_DIM, :] = _own_rows(ot, h) * inv_den[h]
        o_ref[...] = ot_scr[...].T
        lse_ref[...] = _head_stats(lse)

    return pl.pallas_call(
        body, grid=(dil, nb), in_specs=_qkv_specs(nb, col0),
        out_specs=[pl.BlockSpec((None, TQ, GROUP_W), lambda r, i: (r, i, 0)),
                   pl.BlockSpec((None, TQ, LANES), lambda r, i: (r, i, 0))],
        out_shape=[jax.ShapeDtypeStruct((dil, sub, GROUP_W), F32), jax.ShapeDtypeStruct((dil, sub, LANES), F32)],
        scratch_shapes=[pltpu.VMEM((GROUP_W, TQ), F32), pltpu.VMEM((HEADS_PER_GROUP, 2 * TQ, TQ), F32),
                        pltpu.VMEM((HEADS_PER_GROUP, 2 * TQ, TQ), BF16)],
        name=f"attn_fwd_g{g}", compiler_params=_cparams(("parallel", "arbitrary"), 32))(*([qkv] * 7))


def _expand_heads():
    h = lax.broadcasted_iota(jnp.int32, (LANES, GROUP_W), 0)
    c = lax.broadcasted_iota(jnp.int32, (LANES, GROUP_W), 1)
    return (c // HEAD_DIM == h).astype(F32)


def _dot_f32(a, b):
    return jnp.dot(a, b, preferred_element_type=F32, precision=lax.Precision.HIGHEST)


def _attn_combine(outs, lses, *, tm=512):
    S = outs[0].shape[1]
    n_col = GROUP_W // LANES

    def body(*refs):
        ins, e_ref = refs[:2 * N_GROUPS], refs[2 * N_GROUPS]
        c_ref, cb_ref, lt_ref = refs[2 * N_GROUPS + 1:2 * N_GROUPS + 4]
        scr = refs[2 * N_GROUPS + 4:]
        o, l = [ins[0][0]], [ins[N_GROUPS][0]]
        for k, d in enumerate(DILS):
            o_ref, l_ref = ins[1 + k], ins[N_GROUPS + 1 + k]
            o.append(_from_residue(lambda r: o_ref[r], d, tm, scr[k * (n_col + 1):k * (n_col + 1) + n_col]))
            l.append(_from_residue(lambda r: l_ref[r], d, tm, scr[k * (n_col + 1) + n_col:(k + 1) * (n_col + 1)]))
        m = jnp.maximum(jnp.maximum(l[0], l[1]), l[2])
        e = [jnp.exp(v - m) for v in l]
        den = e[0] + e[1] + e[2]
        comb = sum(_dot_f32(ev / den, e_ref[...]) * ov for ev, ov in zip(e, o))
        c_ref[...] = comb
        cb_ref[...] = comb.astype(BF16)
        lt_ref[...] = m + jnp.log(den)

    row = pl.BlockSpec((tm, GROUP_W), lambda i: (i, 0))
    dils = [d for _, d in GROUPS]
    return pl.pallas_call(
        body, grid=(S // tm,),
        in_specs=[_res_spec(d, tm, GROUP_W) for d in dils] + [_res_spec(d, tm, LANES) for d in dils]
        + [_resident((LANES, GROUP_W))],
        out_specs=[row, row, pl.BlockSpec((tm, LANES), lambda i: (i, 0))],
        out_shape=[jax.ShapeDtypeStruct((S, GROUP_W), F32), jax.ShapeDtypeStruct((S, GROUP_W), BF16),
                   jax.ShapeDtypeStruct((S, LANES), F32)],
        scratch_shapes=_lane_scratch(tm, GROUP_W + LANES) * len(DILS),
        name="attn_combine", compiler_params=_cparams(("parallel",), 32))(*outs, *lses, _expand_heads())


def _branch_mix(ya_in, comb_b, w_a, w_b, proj, *, tm=512):
    S = ya_in.shape[0]

    def body(ya_ref, cb_ref, wa_ref, wb_ref, ga_ref, gb_ref, yab_ref, mx_ref):
        y_a = _dot(ya_ref[...], wa_ref[...])
        y_b = _dot(cb_ref[...], wb_ref[...])
        yab_ref[:, 0:D_MODEL] = y_a.astype(BF16)
        yab_ref[:, D_MODEL:2 * D_MODEL] = y_b.astype(BF16)
        mx = jax.nn.sigmoid(ga_ref[...].astype(F32)) * y_a + jax.nn.sigmoid(gb_ref[...].astype(F32)) * y_b
        mx_ref[...] = mx.astype(BF16)

    return pl.pallas_call(
        body, grid=(S // tm,),
        in_specs=[pl.BlockSpec((tm, D_CONV), lambda i: (i, 0)), pl.BlockSpec((tm, GROUP_W), lambda i: (i, 0)),
                  pl.BlockSpec((D_CONV, D_MODEL), lambda i: (0, 0)), pl.BlockSpec((GROUP_W, D_MODEL), lambda i: (0, 0)),
                  pl.BlockSpec((tm, D_MODEL), lambda i: (i, P_GA // D_MODEL)),
                  pl.BlockSpec((tm, D_MODEL), lambda i: (i, P_GB // D_MODEL))],
        out_specs=[pl.BlockSpec((tm, 2 * D_MODEL), lambda i: (i, 0)), pl.BlockSpec((tm, D_MODEL), lambda i: (i, 0))],
        out_shape=[jax.ShapeDtypeStruct((S, 2 * D_MODEL), BF16), jax.ShapeDtypeStruct((S, D_MODEL), BF16)],
        name="branch_mix", compiler_params=_cparams(("parallel",), 40))(ya_in, comb_b, w_a, w_b, proj, proj)


def _mix_ln1(mixin, w_o, b_o, h0, g1, b1, *, tm=512):
    S = mixin.shape[0]

    def body(mx_ref, wo_ref, bo_ref, h0_ref, g_ref, b_ref, xh_ref, rs_ref, h1b_ref):
        z = ALPHA * h0_ref[...] + _dot(mx_ref[...], wo_ref[...]) + bo_ref[...]
        xhat, rstd = _ln_stats(z)
        xh_ref[...] = xhat
        rs_ref[...] = jnp.broadcast_to(rstd, (tm, LANES))
        h1b_ref[...] = (xhat * g_ref[...] + b_ref[...]).astype(BF16)

    row = pl.BlockSpec((tm, D_MODEL), lambda i: (i, 0))
    vec = pl.BlockSpec((1, D_MODEL), lambda i: (0, 0))
    return pl.pallas_call(
        body, grid=(S // tm,),
        in_specs=[row, pl.BlockSpec((D_MODEL, D_MODEL), lambda i: (0, 0)), vec, row, vec, vec],
        out_specs=[row, pl.BlockSpec((tm, LANES), lambda i: (i, 0)), row],
        out_shape=[jax.ShapeDtypeStruct((S, D_MODEL), F32), jax.ShapeDtypeStruct((S, LANES), F32),
                   jax.ShapeDtypeStruct((S, D_MODEL), BF16)],
        name="mix_ln1", compiler_params=_cparams(("parallel",), 40))(mixin, w_o, b_o, h0, g1, b1)


def _gelu_parts(cz):
    cdf = 0.5 * (1.0 + lax.erf(cz * INV_SQRT2))
    return cdf, cz * cdf


def _ffn_conv_fwd(up, cw, cb):
    S = up.shape[0]

    def body(a_ref, g_ref, w_ref, cb_ref, o_ref, a_scr):
        _zero_pads(a_scr, S)
        for t in range(0, S, CHUNK):
            a_scr[PAD + t:PAD + t + CHUNK, :] = a_ref[t:t + CHUNK, :].astype(F32)
        w0, w1, w2 = w_ref[0:1, :], w_ref[1:2, :], w_ref[2:3, :]
        for t in range(0, S, CHUNK):
            am, a0, ap = _shifted(a_scr, t)
            _, gel = _gelu_parts(w0 * am + w1 * a0 + w2 * ap + cb_ref[...])
            o_ref[t:t + CHUNK, :] = (gel * g_ref[t:t + CHUNK, :].astype(F32)).astype(BF16)

    return pl.pallas_call(
        body, grid=(D_FF // SLAB,),
        in_specs=[_slab_spec(S, 0), _slab_spec(S, D_FF), pl.BlockSpec((3, SLAB), lambda j: (0, j)),
                  pl.BlockSpec((1, SLAB), lambda j: (0, j))],
        out_specs=pl.BlockSpec((S, SLAB), lambda j: (0, j)),
        out_shape=jax.ShapeDtypeStruct((S, D_FF), BF16),
        scratch_shapes=[pltpu.VMEM((S + 2 * PAD, SLAB), F32)],
        name="ffn_conv_fwd", compiler_params=_cparams(("parallel",), 40))(up, up, cw, cb)


def _down_ln2_loss(f, w_down, b_down, xhat1, g1, b1, g2, b2, target, *, tm=256):
    S = f.shape[0]

    def body(f_ref, wd_ref, bd_ref, xh1_ref, g1_ref, b1_ref, g2_ref, b2_ref, t_ref, dz_ref, dzb_ref, st_ref):
        h1 = xh1_ref[...] * g1_ref[...] + b1_ref[...]
        z = ALPHA * h1 + _dot(f_ref[...], wd_ref[...]) + bd_ref[...]
        xhat, rstd = _ln_stats(z)
        err = xhat * g2_ref[...] + b2_ref[...] - t_ref[...]
        loss = (0.5 / D_MODEL) * jnp.sum(jnp.sum(err * err, axis=1, keepdims=True), axis=0, keepdims=True)
        dh2 = err * (1.0 / D_MODEL)
        dz = _ln_bwd(dh2, xhat, rstd, g2_ref[...])
        dz_ref[...] = dz
        dzb_ref[...] = dz.astype(BF16)
        upd = _rows8([jnp.sum(dh2 * xhat, axis=0, keepdims=True), jnp.sum(dh2, axis=0, keepdims=True),
                      jnp.broadcast_to(loss, (1, D_MODEL)), jnp.sum(dz, axis=0, keepdims=True)], D_MODEL)

        @pl.when(pl.program_id(0) == 0)
        def _():
            st_ref[...] = upd

        @pl.when(pl.program_id(0) != 0)
        def _():
            st_ref[...] += upd

    row = pl.BlockSpec((tm, D_MODEL), lambda i: (i, 0))
    vec = pl.BlockSpec((1, D_MODEL), lambda i: (0, 0))
    return pl.pallas_call(
        body, grid=(S // tm,),
        in_specs=[pl.BlockSpec((tm, D_FF), lambda i: (i, 0)), _resident((D_FF, D_MODEL)),
                  vec, row, vec, vec, vec, vec, row],
        out_specs=[row, row, pl.BlockSpec((SUBLANES, D_MODEL), lambda i: (0, 0))],
        out_shape=[jax.ShapeDtypeStruct((S, D_MODEL), F32), jax.ShapeDtypeStruct((S, D_MODEL), BF16),
                   jax.ShapeDtypeStruct((SUBLANES, D_MODEL), F32)],
        name="down_ln2_loss", compiler_params=_cparams(("arbitrary",), 48))(
            f, w_down, b_down, xhat1, g1, b1, g2, b2, target)


def _ffn_conv_bwd(up, df, cw, cb):
    S = up.shape[0]

    def body(a_ref, g_ref, df_ref, w_ref, cb_ref, dup_ref, sm_ref, a_scr, d_scr):
        _zero_pads(a_scr, S)
        _zero_pads(d_scr, S)
        for t in range(0, S, CHUNK):
            a_scr[PAD + t:PAD + t + CHUNK, :] = a_ref[t:t + CHUNK, :].astype(F32)
        w0, w1, w2 = w_ref[0:1, :], w_ref[1:2, :], w_ref[2:3, :]
        zero = jnp.zeros((1, SLAB), F32)
        s_dg, s_dcz, s_w0, s_w1, s_w2 = zero, zero, zero, zero, zero
        for t in range(0, S, CHUNK):
            am, a0, ap = _shifted(a_scr, t)
            cz = w0 * am + w1 * a0 + w2 * ap + cb_ref[...]
            cdf, gel = _gelu_parts(cz)
            dfv = df_ref[t:t + CHUNK, :].astype(F32)
            dgte = dfv * gel
            dcz = dfv * g_ref[t:t + CHUNK, :].astype(F32) * (cdf + cz * jnp.exp(-0.5 * cz * cz) * INV_SQRT_2PI)
            dup_ref[1, t:t + CHUNK, :] = dgte.astype(BF16)
            d_scr[PAD + t:PAD + t + CHUNK, :] = dcz
            s_dg = s_dg + jnp.sum(dgte, axis=0, keepdims=True)
            s_dcz = s_dcz + jnp.sum(dcz, axis=0, keepdims=True)
            s_w0 = s_w0 + jnp.sum(dcz * am, axis=0, keepdims=True)
            s_w1 = s_w1 + jnp.sum(dcz * a0, axis=0, keepdims=True)
            s_w2 = s_w2 + jnp.sum(dcz * ap, axis=0, keepdims=True)
        s_da = zero
        for t in range(0, S, CHUNK):
            dm, d0, dp = _shifted(d_scr, t)
            da = w0 * dp + w1 * d0 + w2 * dm
            dup_ref[0, t:t + CHUNK, :] = da.astype(BF16)
            s_da = s_da + jnp.sum(da, axis=0, keepdims=True)
        sm_ref[...] = _rows8([s_da, s_dg, s_dcz, s_w0, s_w1, s_w2], SLAB)

    return pl.pallas_call(
        body, grid=(D_FF // SLAB,),
        in_specs=[_slab_spec(S, 0), _slab_spec(S, D_FF), pl.BlockSpec((S, SLAB), lambda j: (0, j)),
                  pl.BlockSpec((3, SLAB), lambda j: (0, j)), pl.BlockSpec((1, SLAB), lambda j: (0, j))],
        out_specs=[pl.BlockSpec((2, S, SLAB), lambda j: (0, 0, j)), pl.BlockSpec((SUBLANES, SLAB), lambda j: (0, j))],
        out_shape=[jax.ShapeDtypeStruct((2, S, D_FF), BF16), jax.ShapeDtypeStruct((SUBLANES, D_FF), F32)],
        scratch_shapes=[pltpu.VMEM((S + 2 * PAD, SLAB), F32)] * 2,
        name="ffn_conv_bwd", compiler_params=_cparams(("parallel",), 48))(up, up, df, cw, cb)


def _resident(shape):
    nd = len(shape)
    return pl.BlockSpec(shape, lambda *_: (0,) * nd, pipeline_mode=pl.Buffered(1))


def _up_bwd_ln1(dup, w_up3, dz2, xhat1, rstd1, g1, *, tm=256):
    S = dz2.shape[0]
    ns, _, tk = w_up3.shape
    per_plane = D_FF // tk

    def body(du_ref, w_ref, dz2_ref, xh_ref, rs_ref, g_ref, dz_ref, dzb_ref, st_ref):
        dh = ALPHA * dz2_ref[...]
        for k in range(ns):
            col = (k % per_plane) * tk
            dh = dh + _dot_nt(du_ref[k // per_plane, :, col:col + tk], w_ref[k])
        xhat = xh_ref[...]
        dz = _ln_bwd(dh, xhat, rs_ref[:, 0:1], g_ref[...])
        dz_ref[...] = dz
        dzb_ref[...] = dz.astype(BF16)
        upd = _rows8([jnp.sum(dh * xhat, axis=0, keepdims=True), jnp.sum(dh, axis=0, keepdims=True),
                      jnp.sum(dz, axis=0, keepdims=True)], D_MODEL)

        @pl.when(pl.program_id(0) == 0)
        def _():
            st_ref[...] = upd

        @pl.when(pl.program_id(0) != 0)
        def _():
            st_ref[...] += upd

    row = pl.BlockSpec((tm, D_MODEL), lambda i: (i, 0))
    return pl.pallas_call(
        body, grid=(S // tm,),
        in_specs=[pl.BlockSpec((dup.shape[0], tm, D_FF), lambda i: (0, i, 0)), _resident(w_up3.shape),
                  row, row, pl.BlockSpec((tm, LANES), lambda i: (i, 0)), pl.BlockSpec((1, D_MODEL), lambda i: (0, 0))],
        out_specs=[row, row, pl.BlockSpec((SUBLANES, D_MODEL), lambda i: (0, 0))],
        out_shape=[jax.ShapeDtypeStruct((S, D_MODEL), F32), jax.ShapeDtypeStruct((S, D_MODEL), BF16),
                   jax.ShapeDtypeStruct((SUBLANES, D_MODEL), F32)],
        name="up_bwd_ln1", compiler_params=_cparams(("arbitrary",), 48))(dup, w_up3, dz2, xhat1, rstd1, g1)


def _mix_bwd(dz1b, w_o, proj, yab, *, tm=512):
    S = dz1b.shape[0]

    def body(dz_ref, wo_ref, ga_ref, gb_ref, y_ref, dy_ref, dg_ref):
        dmx = _dot_nt(dz_ref[...], wo_ref[...])
        for k, gt_ref in enumerate((ga_ref, gb_ref)):
            sl = slice(k * D_MODEL, (k + 1) * D_MODEL)
            sg = jax.nn.sigmoid(gt_ref[...].astype(F32))
            dy_ref[:, sl] = (dmx * sg).astype(BF16)
            dg_ref[k] = (dmx * y_ref[:, sl].astype(F32) * sg * (1.0 - sg)).astype(BF16)

    row = pl.BlockSpec((tm, D_MODEL), lambda i: (i, 0))
    wide = pl.BlockSpec((tm, 2 * D_MODEL), lambda i: (i, 0))
    return pl.pallas_call(
        body, grid=(S // tm,),
        in_specs=[row, _resident(w_o.shape), pl.BlockSpec((tm, D_MODEL), lambda i: (i, P_GA // D_MODEL)),
                  pl.BlockSpec((tm, D_MODEL), lambda i: (i, P_GB // D_MODEL)), wide],
        out_specs=[wide, pl.BlockSpec((2, tm, D_MODEL), lambda i: (0, i, 0))],
        out_shape=[jax.ShapeDtypeStruct((S, 2 * D_MODEL), BF16), jax.ShapeDtypeStruct((2, S, D_MODEL), BF16)],
        name="mix_bwd", compiler_params=_cparams(("parallel",), 40))(dz1b, w_o, proj, proj, yab)


def _conv_gate_bwd(proj, dya_in, conv_w):
    S = proj.shape[0]

    def body(b_ref, c_ref, h_ref, dy_ref, w_ref, o_ref, sm_ref, u_scr, d_scr):
        _zero_pads(u_scr, S)
        _zero_pads(d_scr, S)
        for t in range(0, S, CHUNK):
            u_scr[PAD + t:PAD + t + CHUNK, :] = c_ref[t:t + CHUNK, :].astype(F32) * h_ref[t:t + CHUNK, :].astype(F32)
        w0, w1, w2 = w_ref[0:1, :], w_ref[1:2, :], w_ref[2:3, :]
        zero = jnp.zeros((1, SLAB), F32)
        s_w0, s_w1, s_w2 = zero, zero, zero
        for t in range(0, S, CHUNK):
            um, u0, up = _shifted(u_scr, t)
            dy = dy_ref[t:t + CHUNK, :].astype(F32)
            o_ref[0, t:t + CHUNK, :] = (dy * (w0 * um + w1 * u0 + w2 * up)).astype(BF16)
            dcv = dy * b_ref[t:t + CHUNK, :].astype(F32)
            d_scr[PAD + t:PAD + t + CHUNK, :] = dcv
            s_w0 = s_w0 + jnp.sum(dcv * um, axis=0, keepdims=True)
            s_w1 = s_w1 + jnp.sum(dcv * u0, axis=0, keepdims=True)
            s_w2 = s_w2 + jnp.sum(dcv * up, axis=0, keepdims=True)
        for t in range(0, S, CHUNK):
            dm, d0, dp = _shifted(d_scr, t)
            du = w0 * dp + w1 * d0 + w2 * dm
            o_ref[1, t:t + CHUNK, :] = (du * h_ref[t:t + CHUNK, :].astype(F32)).astype(BF16)
            o_ref[2, t:t + CHUNK, :] = (du * c_ref[t:t + CHUNK, :].astype(F32)).astype(BF16)
        sm_ref[...] = _rows8([s_w0, s_w1, s_w2], SLAB)

    return pl.pallas_call(
        body, grid=(D_CONV // SLAB,),
        in_specs=[_slab_spec(S, P_B), _slab_spec(S, P_C), _slab_spec(S, P_H),
                  pl.BlockSpec((S, SLAB), lambda j: (0, j)), pl.BlockSpec((3, SLAB), lambda j: (0, j))],
        out_specs=[pl.BlockSpec((3, S, SLAB), lambda j: (0, 0, j)), pl.BlockSpec((SUBLANES, SLAB), lambda j: (0, j))],
        out_shape=[jax.ShapeDtypeStruct((3, S, D_CONV), BF16), jax.ShapeDtypeStruct((SUBLANES, D_CONV), F32)],
        scratch_shapes=[pltpu.VMEM((S + 2 * PAD, SLAB), F32)] * 2,
        name="conv_gate_bwd", compiler_params=_cparams(("parallel",), 48))(proj, proj, proj, dya_in, conv_w)


def _comb_bwd(dyab, w_b, comb, lse_tot, *, tm=512):
    S = comb.shape[0]
    widths, dtypes = (GROUP_W, LANES, LANES), (BF16, F32, F32)

    def body(dy_ref, wb_ref, c_ref, lt_ref, e_ref, *rest):
        outs, scr = rest[:3 * N_GROUPS], rest[3 * N_GROUPS:]
        dcb = _dot_nt(dy_ref[...], wb_ref[...]).astype(BF16)
        dc = dcb.astype(F32)
        delta = lax.dot_general(dc * c_ref[...], e_ref[...], (((1,), (1,)), ((), ())),
                                preferred_element_type=F32, precision=lax.Precision.HIGHEST)
        for k, (val, dtype) in enumerate(zip((dc, lt_ref[...], delta), dtypes)):
            outs[k][0] = val.astype(dtype)
            _to_residue(val, [outs[3 * (1 + j) + k] for j in range(len(DILS))], DILS, tm, dtype,
                        scr[:val.shape[1] // LANES])

    out_specs, out_shape = [], []
    for _, d in GROUPS:
        out_specs += [_res_spec(d, tm, w) for w in widths]
        out_shape += [jax.ShapeDtypeStruct((d, S // d, w), t) for w, t in zip(widths, dtypes)]
    res = pl.pallas_call(
        body, grid=(S // tm,),
        in_specs=[pl.BlockSpec((tm, D_MODEL), lambda i: (i, 1)), _resident(w_b.shape),
                  pl.BlockSpec((tm, GROUP_W), lambda i: (i, 0)), pl.BlockSpec((tm, LANES), lambda i: (i, 0)),
                  _resident((LANES, GROUP_W))],
        out_specs=out_specs, out_shape=out_shape, scratch_shapes=_lane_scratch(tm, GROUP_W),
        name="comb_bwd", compiler_params=_cparams(("parallel",), 32))(dyab, w_b, comb, lse_tot, _expand_heads())
    return [tuple(res[3 * g:3 * g + 3]) for g in range(N_GROUPS)]


def _attn_bwd(qkv, col0, g, dcomb, lse_tot, delta):
    dil, sub, _ = qkv.shape
    nb = sub // TQ

    def body(q_ref, kp, kc, kn, vp, vc, vn, do_ref, lse_ref, dl_ref, dq_ref, dk_ref, dv_ref,
             ak, av, dqt_scr, s_scr, dp_scr, ds_scr, p_scr):
        i = pl.program_id(1)

        @pl.when(i == 0)
        def _():
            ak[...] = jnp.zeros_like(ak)
            av[...] = jnp.zeros_like(av)

        @pl.when(i < nb)
        def _():
            bias = _attn_bias(i, sub, dil)
            kwin = _window(kp, kc, kn)
            vwin = _window(vp, vc, vn)
            q = q_ref[...] * ATT_SCALE
            do = do_ref[...]
            lse_t, dl_t = lse_ref[...].T, dl_ref[...].T
            for h in range(HEADS_PER_GROUP):
                s_scr[h] = _dot_nt(_pair(kwin, h), _own_lanes(_pair(q, h), h))
                dp_scr[h] = _dot_nt(_pair(vwin, h), _own_lanes(_pair(do, h), h))
            for h in range(HEADS_PER_GROUP):
                p = jnp.exp(s_scr[h] + _slope(g, h) * bias - lse_t[h:h + 1, :])
                ds_scr[h] = (p * (dp_scr[h] - dl_t[h:h + 1, :])).astype(BF16)
                p_scr[h] = p.astype(BF16)
            for h in range(HEADS_PER_GROUP):
                dqt_scr[h * HEAD_DIM:(h + 1) * HEAD_DIM, :] = _own_rows(_dot_tn(_pair(kwin, h), ds_scr[h]), h)
            for h in range(0, HEADS_PER_GROUP, 2):
                cols = slice(h * HEAD_DIM, (h + 2) * HEAD_DIM)
                q2 = jnp.concatenate([_own_lanes(_pair(q, h), h), _own_lanes(_pair(q, h), h + 1)], axis=0)
                do2 = jnp.concatenate([_own_lanes(_pair(do, h), h), _own_lanes(_pair(do, h), h + 1)], axis=0)
                ak[RADIUS:RADIUS + 2 * TQ, cols] += _dot(jnp.concatenate([ds_scr[h], ds_scr[h + 1]], axis=1), q2)
                av[RADIUS:RADIUS + 2 * TQ, cols] += _dot(jnp.concatenate([p_scr[h], p_scr[h + 1]], axis=1), do2)
            dq_ref[...] = (dqt_scr[...].T * ATT_SCALE).astype(BF16)

        dk_ref[...] = ak[0:TQ, :].astype(BF16)
        dv_ref[...] = av[0:TQ, :].astype(BF16)
        ak[0:2 * TQ, :] = ak[TQ:3 * TQ, :]
        av[0:2 * TQ, :] = av[TQ:3 * TQ, :]
        ak[2 * TQ:3 * TQ, :] = jnp.zeros((TQ, GROUP_W), F32)
        av[2 * TQ:3 * TQ, :] = jnp.zeros((TQ, GROUP_W), F32)

    tok = pl.BlockSpec((None, TQ, GROUP_W), lambda r, i: (r, jnp.minimum(i, nb - 1), 0))
    stat = pl.BlockSpec((None, TQ, LANES), lambda r, i: (r, jnp.minimum(i, nb - 1), 0))
    dkv_spec = pl.BlockSpec((None, TQ, GROUP_W), lambda r, i: (r, jnp.maximum(i - 1, 0), 0))
    return pl.pallas_call(
        body, grid=(dil, nb + 1), in_specs=_qkv_specs(nb, col0) + [tok, stat, stat],
        out_specs=[tok, dkv_spec, dkv_spec], out_shape=[jax.ShapeDtypeStruct((dil, sub, GROUP_W), BF16)] * 3,
        scratch_shapes=[pltpu.VMEM((3 * TQ, GROUP_W), F32)] * 2 + [pltpu.VMEM((GROUP_W, TQ), F32)]
        + [pltpu.VMEM((HEADS_PER_GROUP, 2 * TQ, TQ), F32)] * 2 + [pltpu.VMEM((HEADS_PER_GROUP, 2 * TQ, TQ), BF16)] * 2,
        name=f"attn_bwd_g{g}", compiler_params=_cparams(("arbitrary", "arbitrary"), 32))(
            *([qkv] * 7), dcomb, lse_tot, delta)


def _in_bwd_ln0(dgated, dqkv, w_nat, w_dil, dz1, x, g0, *, tm=256):
    S = x.shape[0]
    n_gated, n_in = len(dgated), 3 * N_GROUPS

    def body(*refs):
        g_refs, d_refs = refs[:n_gated], refs[n_gated:n_gated + n_in]
        wn_ref, *wd_refs = refs[n_gated + n_in:n_gated + n_in + N_GROUPS]
        dz_ref, x_ref, g_ref, gx_ref, st_ref, *tmp_ref = refs[n_gated + n_in + N_GROUPS:]
        dh = ALPHA * dz_ref[...]
        col = 0
        for ref in g_refs:
            for k in range(ref.shape[0]):
                dh = dh + _dot_nt(ref[k], wn_ref[:, col:col + D_MODEL])
                col += D_MODEL
        for g, (_, d) in enumerate(GROUPS):
            rows = [jnp.concatenate([d_refs[3 * g + k][r] for k in range(3)], axis=1) for r in range(d)]
            w = wn_ref[:, col:col + QKV_W] if d == 1 else wd_refs[g - 1][...]
            res = _dot_nt(jnp.concatenate(rows, axis=0), w)
            if d == 1:
                dh = dh + res
            else:
                n = tm // d
                dh = dh + _from_residue(lambda r: res[r * n:(r + 1) * n, :], d, tm, tmp_ref)
        xhat, rstd = _ln_stats(x_ref[...])
        gx_ref[...] = _ln_bwd(dh, xhat, rstd, g_ref[...])
        upd = _rows8([jnp.sum(dh * xhat, axis=0, keepdims=True), jnp.sum(dh, axis=0, keepdims=True)], D_MODEL)

        @pl.when(pl.program_id(0) == 0)
        def _():
            st_ref[...] = upd

        @pl.when(pl.program_id(0) != 0)
        def _():
            st_ref[...] += upd

    row = pl.BlockSpec((tm, D_MODEL), lambda i: (i, 0))
    g_specs = [pl.BlockSpec((a.shape[0], tm, D_MODEL), lambda i: (0, i, 0)) for a in dgated]
    d_specs = []
    for _, d in GROUPS:
        d_specs += [_res_spec(d, tm, GROUP_W)] * 3
    operands = list(dgated) + [a for grp in dqkv for a in grp] + [w_nat] + list(w_dil) + [dz1, x, g0]
    return pl.pallas_call(
        body, grid=(S // tm,),
        in_specs=g_specs + d_specs + [_resident(w_nat.shape)] + [_resident(w.shape) for w in w_dil]
        + [row, row, pl.BlockSpec((1, D_MODEL), lambda i: (0, 0))],
        out_specs=[row, pl.BlockSpec((SUBLANES, D_MODEL), lambda i: (0, 0))],
        out_shape=[jax.ShapeDtypeStruct((S, D_MODEL), F32), jax.ShapeDtypeStruct((SUBLANES, D_MODEL), F32)],
        scratch_shapes=_lane_scratch(tm, D_MODEL),
        name="in_bwd_ln0", compiler_params=_cparams(("arbitrary",), 52))(*operands)


HBM_SPEC = pl.BlockSpec(memory_space=pltpu.HBM)


def _place():
    x, y, c = lax.axis_index("x"), lax.axis_index("y"), lax.axis_index("c")
    chips = [(1 - x, y), (x, 1 - y), (1 - x, 1 - y)]
    return x, y, c, chips


def _allgather_shards(shards, after, *, name, collective_id):
    n = len(shards)
    per = 6

    def body(*refs):
        ins, outs = refs[:n], refs[n + len(after):2 * n + len(after)]
        send_sems, recv_sems, loc_sems = refs[2 * n + len(after):]
        x, y, c, chips = _place()
        me = 2 * x + y
        sib = (x, y, 1 - c)
        peers = [sib] + [(px, py, c) for px, py in chips]
        barrier = pltpu.get_barrier_semaphore()
        for peer in peers:
            pl.semaphore_signal(barrier, inc=1, device_id=peer, device_id_type=MESH)
        pl.semaphore_wait(barrier, len(peers))

        def rcopy(w, k, src, dst, to):
            return pltpu.make_async_remote_copy(src_ref=src, dst_ref=dst, send_sem=send_sems.at[per * w + k],
                                                recv_sem=recv_sems.at[per * w + k], device_id=to, device_id_type=MESH)

        split = [s.shape[0] == N_CORES for s in shards]
        half = lambda w: c if split[w] else 0
        local, sends = [], []
        for w in range(n):
            cp = pltpu.make_async_copy(ins[w], outs[w].at[me], loc_sems.at[w])
            cp.start()
            local.append(cp)
            for j, (px, py) in enumerate(chips):
                cp = rcopy(w, j, ins[w].at[half(w)], outs[w].at[me, half(w)], (px, py, c))
                cp.start()
                sends.append(cp)
        for w in range(n):
            for j, (px, py) in enumerate(chips):
                slot = outs[w].at[2 * px + py, half(w)]
                rcopy(w, j, slot, slot, (px, py, c)).wait_recv()
                if split[w]:
                    cp = rcopy(w, 3 + j, slot, slot, sib)
                    cp.start()
                    sends.append(cp)
        for w in range(n):
            if split[w]:
                for j, (px, py) in enumerate(chips):
                    slot = outs[w].at[2 * px + py, 1 - c]
                    rcopy(w, 3 + j, slot, slot, sib).wait_recv()
        for cp in sends:
            cp.wait_send()
        for cp in local:
            cp.wait()

    return pl.kernel(
        body, out_type=[jax.ShapeDtypeStruct((N_CHIPS,) + s.shape, s.dtype) for s in shards],
        mesh=plsc.ScalarSubcoreMesh(axis_name="sequencer", num_cores=1),
        scratch_types=[pltpu.SemaphoreType.DMA((per * n,)), pltpu.SemaphoreType.DMA((per * n,)),
                       pltpu.SemaphoreType.DMA((n,))],
        name=name, compiler_params=pltpu.CompilerParams(collective_id=collective_id))(*shards, *after)


def _exchange_grads(grads, *, name, collective_id):
    n = len(grads)
    per = 7

    def body(*refs):
        ins, outs = refs[:n], refs[n:2 * n]
        send_sems, recv_sems, loc_sems = refs[2 * n:]
        x, y, c, chips = _place()
        me = 2 * x + y
        sib = (x, y, 1 - c)
        peers = [sib] + [(px, py, c) for px, py in chips]
        barrier = pltpu.get_barrier_semaphore()
        for peer in peers:
            pl.semaphore_signal(barrier, inc=1, device_id=peer, device_id_type=MESH)
        pl.semaphore_wait(barrier, len(peers))

        def rcopy(w, k, src, dst, to):
            return pltpu.make_async_remote_copy(src_ref=src, dst_ref=dst, send_sem=send_sems.at[per * w + k],
                                                recv_sem=recv_sems.at[per * w + k], device_id=to, device_id_type=MESH)

        local, sends = [], []
        for w in range(n):
            cp = pltpu.make_async_copy(ins[w].at[me], outs[w].at[c, me], loc_sems.at[w])
            cp.start()
            local.append(cp)
            cp = rcopy(w, 0, ins[w].at[me], outs[w].at[c, me], sib)
            cp.start()
            sends.append(cp)
            for j, (px, py) in enumerate(chips):
                cp = rcopy(w, 1 + j, ins[w].at[2 * px + py], outs[w].at[c, me], (px, py, c))
                cp.start()
                sends.append(cp)
        for w in range(n):
            for j, (px, py) in enumerate(chips):
                slot = outs[w].at[c, 2 * px + py]
                rcopy(w, 1 + j, slot, slot, (px, py, c)).wait_recv()
                cp = rcopy(w, 4 + j, slot, slot, sib)
                cp.start()
                sends.append(cp)
        for w in range(n):
            slot = outs[w].at[1 - c, me]
            rcopy(w, 0, slot, slot, sib).wait_recv()
            for j, (px, py) in enumerate(chips):
                slot = outs[w].at[1 - c, 2 * px + py]
                rcopy(w, 4 + j, slot, slot, sib).wait_recv()
        for cp in sends:
            cp.wait_send()
        for cp in local:
            cp.wait()

    return pl.kernel(
        body, out_type=[jax.ShapeDtypeStruct((N_CORES,) + g.shape, g.dtype) for g in grads],
        mesh=plsc.ScalarSubcoreMesh(axis_name="sequencer", num_cores=1),
        scratch_types=[pltpu.SemaphoreType.DMA((per * n,)), pltpu.SemaphoreType.DMA((per * n,)),
                       pltpu.SemaphoreType.DMA((n,))],
        name=name, compiler_params=pltpu.CompilerParams(collective_id=collective_id))(*grads)


def _allgather_small(vec):
    def body(v_ref, o_ref, send_sems, recv_sems, loc_sem):
        x, y, c = lax.axis_index("x"), lax.axis_index("y"), lax.axis_index("c")
        me = 4 * x + 2 * y + c

        def peer(k):
            flip = lambda v, bit: 1 - v if (k >> bit) & 1 else v
            return flip(x, 2), flip(y, 1), flip(c, 0)

        loc = pltpu.make_async_copy(v_ref, o_ref.at[me], loc_sem)
        loc.start()
        sends = []
        for k in range(1, N_DEV):
            cp = pltpu.make_async_remote_copy(src_ref=v_ref, dst_ref=o_ref.at[me], send_sem=send_sems.at[k - 1],
                                              recv_sem=recv_sems.at[k - 1], device_id=peer(k), device_id_type=MESH)
            cp.start()
            sends.append(cp)
        for k in range(1, N_DEV):
            px, py, pc = peer(k)
            pltpu.make_async_remote_copy(src_ref=v_ref, dst_ref=o_ref.at[4 * px + 2 * py + pc],
                                         send_sem=send_sems.at[k - 1], recv_sem=recv_sems.at[k - 1],
                                         device_id=(px, py, pc), device_id_type=MESH).wait_recv()
        for cp in sends:
            cp.wait_send()
        loc.wait()

    return pl.pallas_call(
        body, in_specs=[HBM_SPEC], out_specs=HBM_SPEC,
        out_shape=jax.ShapeDtypeStruct((N_DEV,) + vec.shape, vec.dtype),
        scratch_shapes=[pltpu.SemaphoreType.DMA((N_DEV - 1,)), pltpu.SemaphoreType.DMA((N_DEV - 1,)),
                        pltpu.SemaphoreType.DMA],
        name="allgather_small")(vec)


def _adamw(w, g, m, v):
    m = ADAM_B1 * m + (1.0 - ADAM_B1) * g
    v = ADAM_B2 * v + (1.0 - ADAM_B2) * (g * g)
    m_hat = m / (1.0 - ADAM_B1 ** ADAM_STEP)
    v_hat = v / (1.0 - ADAM_B2 ** ADAM_STEP)
    delta = -ADAM_LR * (m_hat / (jnp.sqrt(v_hat) + ADAM_EPS) + ADAM_WD * w)
    return delta, m, v


def _reduce_adamw(parts, w, m, v, *, tr, name):
    R, C = w.shape

    def body(p_ref, w_ref, m_ref, v_ref, g_ref, d_ref, nm_ref, nv_ref):
        def core_sum(cc):
            s = p_ref[cc, 0].astype(F32)
            for k in range(1, N_CHIPS):
                s = s + p_ref[cc, k].astype(F32)
            return s

        g = core_sum(0) + core_sum(1)
        delta, nm, nv = _adamw(w_ref[...], g, m_ref[...], v_ref[...])
        g_ref[...] = g
        d_ref[...] = delta
        nm_ref[...] = nm
        nv_ref[...] = nv

    blk = pl.BlockSpec((tr, C), lambda i: (i, 0))
    return pl.pallas_call(
        body, grid=(R // tr,),
        in_specs=[pl.BlockSpec((N_CORES, N_CHIPS, tr, C), lambda i: (0, 0, i, 0)), blk, blk, blk],
        out_specs=[blk] * 4, out_shape=[jax.ShapeDtypeStruct((R, C), F32)] * 4,
        name=name, compiler_params=_cparams(("parallel",), 40))(parts, w, m, v)


def _reduce_adamw_vectors(allv, offs, ws, ms, vs):
    n = len(ws)

    def body(a_ref, *refs):
        w_refs, m_refs, v_refs = refs[:n], refs[n:2 * n], refs[2 * n:3 * n]
        tot_ref, outs = refs[3 * n], refs[3 * n + 1:]
        s = a_ref[0]
        for d in range(1, N_DEV):
            s = s + a_ref[d]
        tot_ref[...] = s
        for k in range(n):
            g = s[:, offs[k]:offs[k] + w_refs[k].shape[1]]
            delta, nm, nv = _adamw(w_refs[k][...], g, m_refs[k][...], v_refs[k][...])
            for ref, val in zip(outs[4 * k:4 * k + 4], (g, delta, nm, nv)):
                ref[...] = val

    out_shape = [jax.ShapeDtypeStruct(allv.shape[1:], F32)]
    for w in ws:
        out_shape += [jax.ShapeDtypeStruct(w.shape, F32)] * 4
    res = pl.pallas_call(body, out_shape=out_shape, name="reduce_adamw_vectors",
                         compiler_params=_cparams((), 40))(allv, *ws, *ms, *vs)
    return res[0], [tuple(res[1 + 4 * k:5 + 4 * k]) for k in range(n)]


def _adamw_taps(ws, gs, ms, vs):
    n = len(ws)

    def body(*refs):
        outs = refs[4 * n:]
        for k in range(n):
            res = _adamw(refs[k][...], refs[n + k][...], refs[2 * n + k][...], refs[3 * n + k][...])
            for ref, val in zip(outs[3 * k:3 * k + 3], res):
                ref[...] = val

    out_shape = []
    for w in ws:
        out_shape += [jax.ShapeDtypeStruct(w.shape, F32)] * 3
    res = pl.pallas_call(body, out_shape=out_shape, name="adamw_taps")(*ws, *gs, *ms, *vs)
    return [tuple(res[3 * k:3 * k + 3]) for k in range(n)]


def _pack(pieces):
    flat, offs, n = [], [], 0
    for p in pieces:
        size = -(-p.size // LANES) * LANES
        flat.append(jnp.pad(p.reshape(-1), (0, size - p.size)))
        offs.append(n)
        n += size
    return jnp.concatenate(flat).reshape(1, n), offs


def _local_step(x, target, p, wfull, on_ready=lambda group: None, before_ln0=()):
    S = x.shape[0]
    dils = [d for _, d in GROUPS]

    h0, h0b, *h0_res = _ln0_fwd(x, p["ln0_g"], p["ln0_b"], before_ln0)
    h0_rows = [h0b] + [h.reshape(S, D_MODEL) for h in h0_res]

    if callable(wfull):
        wfull = wfull(h0b)
    w_in3, w_up3 = wfull["w_in"], wfull["w_up"]
    w_a, w_o, w_down, w_b = wfull["w_a"], wfull["w_o"], wfull["w_down"], wfull["w_b"]
    conv_w, ffn_conv_w = wfull["conv_w"], wfull["ffn_conv_w"]

    w_blocks = w_in3.transpose(1, 0, 2).reshape(D_MODEL, N_BLK, GROUP_W)
    w_perm = jnp.concatenate([w_blocks[:, b] for b in PERM], axis=1)
    b_blocks = p["b_in"].reshape(N_BLK, GROUP_W)
    b_perm = jnp.concatenate([b_blocks[b] for b in PERM]).reshape(1, N_IN)
    w_nat, b_nat = w_perm[:, :N_NAT], b_perm[:, :N_NAT]
    qkv_cols = [slice(P_Q0 + g * QKV_W, P_Q0 + (g + 1) * QKV_W) for g in range(N_GROUPS)]
    w_qkv = [w_perm[:, c] for c in qkv_cols]

    proj = _mm_nn(h0b, w_nat, b_nat, tm=512, tn=N_NAT // 2, out_dtype=BF16, name="proj")
    qkv = [proj[None]]
    for g in range(1, N_GROUPS):
        t = _mm_nn(h0_rows[g], w_qkv[g], b_perm[:, qkv_cols[g]], tm=512, tn=QKV_W, out_dtype=BF16, name=f"proj_qkv{g}")
        qkv.append(t.reshape(dils[g], S // dils[g], QKV_W))
    col0 = [P_Q0 // GROUP_W] + [0] * (N_GROUPS - 1)
    ya_in = _conv_gate_fwd(proj, conv_w)
    att = [_attn_fwd(qkv[g], col0[g], g) for g in range(N_GROUPS)]
    comb, comb_b, lse_tot = _attn_combine([a[0] for a in att], [a[1] for a in att])
    yab, mixin = _branch_mix(ya_in, comb_b, w_a, w_b, proj)
    xhat1, rstd1, h1b = _mix_ln1(mixin, w_o, p["b_o"], h0, p["ln1_g"], p["ln1_b"])
    up = _mm_nn(h1b, w_up3, p["b_up"], tm=512, tn=w_up3.shape[2], out_dtype=BF16, name="up")
    f = _ffn_conv_fwd(up, ffn_conv_w, p["ffn_conv_b"])
    dz2, dz2b, st2 = _down_ln2_loss(f, w_down, p["b_down"], xhat1, p["ln1_g"], p["ln1_b"],
                                    p["ln2_g"], p["ln2_b"], target)

    gw = {}
    gw["w_down"] = _mm_tn(f, dz2b, n_out=1, tn=D_MODEL, ts=1024, g_block=(1024, D_MODEL),
                          g_map=lambda j, s: (s, 0), name="grad_w_down").reshape(N_CHIPS, D_FF // N_CHIPS, D_MODEL)
    df = _mm_nt(dz2b, w_down, tm=512, name="df")
    dup, sm_ffn = _ffn_conv_bwd(up, df, ffn_conv_w, p["ffn_conv_b"])
    up_tn = w_up3.shape[2]
    up_pp = D_FF // up_tn
    gw["w_up"] = _mm_tn(h1b, dup, n_out=N_CHIPS, tn=up_tn, ts=1024, g_block=(None, 1024, up_tn),
                        g_map=lambda j, s: (j // up_pp, s, j % up_pp), name="grad_w_up")
    on_ready({n: gw[n] for n in ("w_down", "w_up")})
    dz1, dz1b, st1 = _up_bwd_ln1(dup, w_up3, dz2, xhat1, rstd1, p["ln1_g"])

    gw["w_o"] = _mm_tn(mixin, dz1b, n_out=1, tn=D_MODEL, ts=512, g_block=(512, D_MODEL),
                       g_map=lambda j, s: (s, 0), name="grad_w_o").reshape(N_CHIPS, D_MODEL // N_CHIPS, D_MODEL)
    dyab, dgab = _mix_bwd(dz1b, w_o, proj, yab)
    gw["w_a"] = _mm_tn(ya_in, dyab, n_out=1, tn=D_MODEL, ts=512, g_block=(512, D_MODEL),
                       g_map=lambda j, s: (s, 0), name="grad_w_a").reshape(N_CHIPS, D_CONV // N_CHIPS, D_MODEL)
    gw_b = _mm_tn(comb_b, dyab, n_out=1, tn=D_MODEL, ts=1024, g_block=(1024, D_MODEL),
                  g_map=lambda j, s: (s, 1), name="grad_w_b")
    gw["w_b"] = gw_b.reshape(GROUP_W, N_CHIPS, D_MODEL // N_CHIPS).transpose(1, 0, 2)
    on_ready({n: gw[n] for n in ("w_o", "w_a", "w_b")})
    dya_in = _mm_nt(dyab, w_a, tm=512, a_col=0, name="dya_in")
    dbch, sm_conv = _conv_gate_bwd(proj, dya_in, conv_w)
    att_stats = _comb_bwd(dyab, w_b, comb, lse_tot)
    dqkv = [_attn_bwd(qkv[g], col0[g], g, *att_stats[g]) for g in range(N_GROUPS)]

    w_pieces, b_pieces = [], []
    for nm, planes in (("bch", dbch), ("gab", dgab)):
        pw, pc = _mm_tn(h0b, planes, n_out=planes.shape[0], tn=D_MODEL, ts=1024, g_block=(None, 1024, D_MODEL),
                        g_map=lambda j, s: (j, s, 0), colsum=True, name="grad_w_in_" + nm)
        w_pieces.append(pw.transpose(1, 0, 2).reshape(D_MODEL, planes.shape[0] * D_MODEL))
        b_pieces.append(pc[0])
    for g in range(N_GROUPS):
        pw, pc = _mm_tn_cat(h0_rows[g], [a.reshape(S, GROUP_W) for a in dqkv[g]], ts=1024, name=f"grad_w_in_qkv{g}")
        w_pieces.append(pw)
        b_pieces.append(pc[0])
    dw_blocks = jnp.concatenate(w_pieces, axis=1).reshape(D_MODEL, N_BLK, GROUP_W)
    dw_ref = jnp.concatenate([dw_blocks[:, b] for b in INV_PERM], axis=1)
    gw["w_in"] = dw_ref.reshape(D_MODEL, N_CHIPS, N_IN // N_CHIPS).transpose(1, 0, 2)
    on_ready({"w_in": gw["w_in"]})
    db_blocks = jnp.concatenate(b_pieces).reshape(N_BLK, GROUP_W)
    grad_b_in = jnp.concatenate([db_blocks[b] for b in INV_PERM])

    grad_x, st0 = _in_bwd_ln0([dbch, dgab], dqkv, w_nat, w_qkv[1:], dz1, x, p["ln0_g"])

    small = {
        "loss": st2[2:3, 0:1],
        "ln0_g": st0[0], "ln0_b": st0[1], "b_in": grad_b_in, "conv_w": sm_conv[0:3],
        "b_o": st1[2], "ln1_g": st1[0], "ln1_b": st1[1],
        "b_up": jnp.concatenate([sm_ffn[0], sm_ffn[1]]), "ffn_conv_w": sm_ffn[3:6], "ffn_conv_b": sm_ffn[2],
        "b_down": st2[3], "ln2_g": st2[0], "ln2_b": st2[1],
    }
    return grad_x, gw, small


BIG = ("w_in", "w_a", "w_b", "w_o", "w_up", "w_down")
CONV = ("conv_w", "ffn_conv_w")
VECS = ("ln0_g", "ln0_b", "b_in", "b_o", "ln1_g", "ln1_b", "b_up", "ffn_conv_b", "b_down", "ln2_g", "ln2_b")
ORDER = ("ln0_g", "ln0_b", "w_in", "b_in", "conv_w", "w_a", "w_b", "w_o", "b_o", "ln1_g", "ln1_b", "w_up", "b_up",
         "ffn_conv_w", "ffn_conv_b", "w_down", "b_down", "ln2_g", "ln2_b")
SMALL_ORDER = ("loss",) + VECS + CONV


def _step(x, target, W, Mo, Vo):
    x2, t2 = x[0], target[0]
    big2 = {n: W[n][0] for n in BIG}
    halves = lambda a: a.astype(BF16).reshape(N_CORES, a.shape[0] // N_CORES, a.shape[1])
    whole = lambda g: g.reshape(N_CHIPS, g.shape[1] * g.shape[2], g.shape[3])
    later = tuple(n for n in BIG if n != "w_in")
    w_in_halves = halves(big2["w_in"])
    first = _allgather_shards([w_in_halves], [], name="allgather_w_in", collective_id=1)

    def gather_rest(h0b):
        rest = _allgather_shards([halves(big2[n]) for n in later] + [W[n] for n in CONV], [h0b],
                                 name="allgather_rest", collective_id=2)
        gathered = {n: whole(g) for n, g in zip(("w_in",) + later + CONV, first + rest)}
        return {
            "w_in": gathered["w_in"], "w_up": gathered["w_up"],
            "w_a": gathered["w_a"].reshape(D_CONV, D_MODEL), "w_o": gathered["w_o"].reshape(D_MODEL, D_MODEL),
            "w_down": gathered["w_down"].reshape(D_FF, D_MODEL),
            "w_b": gathered["w_b"].transpose(1, 0, 2).reshape(GROUP_W, D_MODEL),
            "conv_w": gathered["conv_w"].transpose(1, 0, 2).reshape(3, D_CONV),
            "ffn_conv_w": gathered["ffn_conv_w"].transpose(1, 0, 2).reshape(3, D_FF),
        }

    pvec = {n: W[n].reshape(1, -1) for n in VECS}

    parts = {}
    exchange_ids = iter((3, 4, 5))

    def exchange(group):
        names = tuple(group)
        res = _exchange_grads([group[n] for n in names], name="exchange_" + "_".join(names),
                              collective_id=next(exchange_ids))
        parts.update(zip(names, res))

    grad_x, _, small = _local_step(x2, t2, pvec, gather_rest, exchange, before_ln0=[w_in_halves])
    out = {}
    for n in BIG:
        tr = {"w_in": 128, "w_up": 128, "w_b": 128}.get(n, big2[n].shape[0] // 4)
        g, d, nm, nv = _reduce_adamw(parts[n], big2[n], Mo[n][0], Vo[n][0], tr=tr, name="adamw_" + n)
        out[n] = tuple(a[None] for a in (g, d, nm, nv))

    vec, offs = _pack([small[n] for n in SMALL_ORDER])
    off = dict(zip(SMALL_ORDER, offs))
    row = lambda a: a.reshape(1, -1)
    tot, vec_out = _reduce_adamw_vectors(_allgather_small(vec), [off[n] for n in VECS], [row(W[n]) for n in VECS],
                                         [row(Mo[n]) for n in VECS], [row(Vo[n]) for n in VECS])
    for n, res in zip(VECS, vec_out):
        out[n] = tuple(a.reshape(W[n].shape) for a in res)
    loss = tot[0, off["loss"]]
    chip = 2 * lax.axis_index("x") + lax.axis_index("y")
    taps_g = []
    for n in CONV:
        width = W[n].shape[2]
        full = lax.slice(tot, (0, off[n]), (1, off[n] + 3 * N_CHIPS * width)).reshape(3, N_CHIPS * width)
        taps_g.append(lax.dynamic_slice_in_dim(full, chip * width, width, axis=1))
    taps_out = _adamw_taps([W[n][0] for n in CONV], taps_g, [Mo[n][0] for n in CONV], [Vo[n][0] for n in CONV])
    for n, g, res in zip(CONV, taps_g, taps_out):
        out[n] = tuple(a[None] for a in (g,) + res)

    res = [loss, grad_x[None]]
    for k in range(4):
        res += [out[n][k] for n in ORDER]
    return tuple(res)


def kernel(x, ln0_g, ln0_b, w_in, b_in, conv_w, w_a, w_b, w_o, b_o, ln1_g, ln1_b, w_up, b_up, ffn_conv_w, ffn_conv_b, w_down, b_down, ln2_g, ln2_b, loss_target, m_ln0_g, m_ln0_b, m_w_in, m_b_in, m_conv_w, m_w_a, m_w_b, m_w_o, m_b_o, m_ln1_g, m_ln1_b, m_w_up, m_b_up, m_ffn_conv_w, m_ffn_conv_b, m_w_down, m_b_down, m_ln2_g, m_ln2_b, v_ln0_g, v_ln0_b, v_w_in, v_b_in, v_conv_w, v_w_a, v_w_b, v_w_o, v_b_o, v_ln1_g, v_ln1_b, v_w_up, v_b_up, v_ffn_conv_w, v_ffn_conv_b, v_w_down, v_b_down, v_ln2_g, v_ln2_b):
    W = dict(zip(ORDER, (ln0_g, ln0_b, w_in, b_in, conv_w, w_a, w_b, w_o, b_o, ln1_g, ln1_b, w_up, b_up,
                         ffn_conv_w, ffn_conv_b, w_down, b_down, ln2_g, ln2_b)))
    Mo = dict(zip(ORDER, (m_ln0_g, m_ln0_b, m_w_in, m_b_in, m_conv_w, m_w_a, m_w_b, m_w_o, m_b_o, m_ln1_g, m_ln1_b,
                          m_w_up, m_b_up, m_ffn_conv_w, m_ffn_conv_b, m_w_down, m_b_down, m_ln2_g, m_ln2_b)))
    Vo = dict(zip(ORDER, (v_ln0_g, v_ln0_b, v_w_in, v_b_in, v_conv_w, v_w_a, v_w_b, v_w_o, v_b_o, v_ln1_g, v_ln1_b,
                          v_w_up, v_b_up, v_ffn_conv_w, v_ffn_conv_b, v_w_down, v_b_down, v_ln2_g, v_ln2_b)))
    return _step(x, loss_target, W, Mo, Vo)
```

```python
import functools
import math

import jax
import jax.numpy as jnp
from jax import lax
from jax.experimental import pallas as pl
from jax.experimental.pallas import tpu as pltpu
from jax.experimental.pallas import tpu_sc as plsc

F32 = jnp.float32
BF16 = jnp.bfloat16

D_MODEL = 1024
D_CONV = D_MODEL
HEAD_DIM = 64
HEADS_PER_GROUP = 8
GROUPS = ((128, 1), (512, 4), (2048, 16))
N_GROUPS = len(GROUPS)
GROUP_W = HEADS_PER_GROUP * HEAD_DIM
QKV_W = N_GROUPS * GROUP_W
RADIUS = 64
D_FF = 2816
LN_EPS = 1e-5
ALPHA = 2.0 ** 0.25
MASK_VALUE = -1e30
ATT_SCALE = HEAD_DIM ** -0.5
OFF_B = 0
OFF_C = OFF_B + D_CONV
OFF_H = OFF_C + D_CONV
OFF_Q = OFF_H + D_CONV
OFF_K = OFF_Q + QKV_W
OFF_V = OFF_K + QKV_W
OFF_GA = OFF_V + QKV_W
OFF_GB = OFF_GA + D_MODEL
N_IN = OFF_GB + D_MODEL
ADAM_LR = 0.001
ADAM_B1 = 0.9
ADAM_B2 = 0.999
ADAM_EPS = 1e-08
ADAM_WD = 0.01
ADAM_STEP = 10
INV_SQRT2 = 0.7071067811865476
INV_SQRT_2PI = 0.3989422804014327

LANES = 128
SUBLANES = 8
VMEM_BYTES_V7X = 64 * 1024 * 1024
N_CHIPS = 4
N_CORES = 2
N_DEV = N_CHIPS * N_CORES
MESH = pl.DeviceIdType.MESH

N_BLK = N_IN // GROUP_W
PERM = (0, 1, 2, 3, 4, 5, 15, 16, 17, 18, 6, 9, 12, 7, 10, 13, 8, 11, 14)
INV_PERM = tuple(PERM.index(b) for b in range(N_BLK))
P_B, P_C, P_H, P_GA, P_GB, P_Q0 = 0, 1024, 2048, 3072, 4096, 5120
N_NAT = P_Q0 + QKV_W // N_GROUPS * 3
N_GATED = P_Q0

SLAB = 128
CHUNK = 256
PAD = SUBLANES
TQ = 128


def _cparams(sem, vmem_mb):
    assert vmem_mb * 1024 * 1024 < VMEM_BYTES_V7X
    return pltpu.CompilerParams(dimension_semantics=sem, vmem_limit_bytes=vmem_mb * 1024 * 1024)


def _dot(a, b):
    return jnp.dot(a, b, preferred_element_type=F32)


def _dot_nt(a, b):
    return lax.dot_general(a, b, (((1,), (1,)), ((), ())), preferred_element_type=F32)


def _dot_tn(a, b):
    return lax.dot_general(a, b, (((0,), (0,)), ((), ())), preferred_element_type=F32)


def _ln_stats(z):
    mu = jnp.mean(z, -1, keepdims=True)
    zc = z - mu
    var = jnp.mean(zc * zc, -1, keepdims=True)
    rstd = lax.rsqrt(var + LN_EPS)
    return zc * rstd, rstd


def _ln_bwd(dh, xhat, rstd, g):
    dxh = dh * g
    m1 = jnp.mean(dxh, -1, keepdims=True)
    m2 = jnp.mean(dxh * xhat, -1, keepdims=True)
    return rstd * (dxh - m1 - xhat * m2)


def _rows8(rows, width):
    pad = [jnp.zeros((1, width), F32)] * (SUBLANES - len(rows))
    return jnp.concatenate(list(rows) + pad, axis=0)


def _mm_nn(a, w, bias, *, tm, tn, out_dtype, name, vmem_mb=40):
    M, K = a.shape
    if w.ndim == 3:
        assert w.shape[2] == tn
        n_tiles = w.shape[0]
        w_spec = pl.BlockSpec((None, K, tn), lambda j, i: (j, 0, 0))
    else:
        n_tiles = w.shape[1] // tn
        w_spec = pl.BlockSpec((K, tn), lambda j, i: (0, j))

    def body(a_ref, w_ref, b_ref, o_ref):
        o_ref[...] = (_dot(a_ref[...], w_ref[...]) + b_ref[...]).astype(o_ref.dtype)

    return pl.pallas_call(
        body, grid=(n_tiles, M // tm),
        in_specs=[pl.BlockSpec((tm, K), lambda j, i: (i, 0)), w_spec, pl.BlockSpec((1, tn), lambda j, i: (0, j))],
        out_specs=pl.BlockSpec((tm, tn), lambda j, i: (i, j)),
        out_shape=jax.ShapeDtypeStruct((M, n_tiles * tn), out_dtype),
        name=name, compiler_params=_cparams(("arbitrary", "parallel"), vmem_mb))(a, w, bias)


def _mm_nt(a, w, *, tm, a_col=0, name, vmem_mb=40):
    M = a.shape[0]
    N, K = w.shape

    def body(a_ref, w_ref, o_ref):
        o_ref[...] = _dot_nt(a_ref[...], w_ref[...]).astype(o_ref.dtype)

    return pl.pallas_call(
        body, grid=(M // tm,),
        in_specs=[pl.BlockSpec((tm, K), lambda i: (i, a_col)),
                  pl.BlockSpec((N, K), lambda i: (0, 0))],
        out_specs=pl.BlockSpec((tm, N), lambda i: (i, 0)),
        out_shape=jax.ShapeDtypeStruct((M, N), BF16),
        name=name, compiler_params=_cparams(("parallel",), vmem_mb))(a, w)


def _mm_tn(a, g, *, n_out, tn, ts, g_block, g_map, colsum=False, name, vmem_mb=48):
    S, K = a.shape
    n_s = S // ts

    def body(a_ref, g_ref, *rest):
        if colsum:
            o_ref, cs_ref, acc_ref, cacc_ref = rest
        else:
            o_ref, acc_ref = rest
        s = pl.program_id(1)

        @pl.when(s == 0)
        def _():
            acc_ref[...] = jnp.zeros_like(acc_ref)
            if colsum:
                cacc_ref[...] = jnp.zeros_like(cacc_ref)

        gv = g_ref[...]
        acc_ref[...] += _dot_tn(a_ref[...], gv)
        if colsum:
            cacc_ref[...] += jnp.broadcast_to(jnp.sum(gv.astype(F32), axis=0, keepdims=True), cacc_ref.shape)

        @pl.when(s == n_s - 1)
        def _():
            o_ref[...] = acc_ref[...].astype(o_ref.dtype)
            if colsum:
                cs_ref[...] = cacc_ref[...]

    out_specs = [pl.BlockSpec((None, K, tn), lambda j, s: (j, 0, 0))]
    out_shape = [jax.ShapeDtypeStruct((n_out, K, tn), BF16)]
    scratch = [pltpu.VMEM((K, tn), F32)]
    if colsum:
        out_specs.append(pl.BlockSpec((SUBLANES, tn), lambda j, s: (0, j)))
        out_shape.append(jax.ShapeDtypeStruct((SUBLANES, n_out * tn), F32))
        scratch.append(pltpu.VMEM((SUBLANES, tn), F32))
    res = pl.pallas_call(
        body, grid=(n_out, n_s),
        in_specs=[pl.BlockSpec((ts, K), lambda j, s: (s, 0)), pl.BlockSpec(g_block, g_map)],
        out_specs=out_specs, out_shape=out_shape, scratch_shapes=scratch,
        name=name, compiler_params=_cparams(("parallel", "arbitrary"), vmem_mb))(a, g)
    return res if colsum else res[0]


def _mm_tn_cat(a, gs, *, ts, name, vmem_mb=40):
    S, K = a.shape
    widths = [g.shape[1] for g in gs]
    n_s, total = S // ts, sum(widths)

    def body(*refs):
        a_ref, g_refs = refs[0], refs[1:1 + len(gs)]
        o_ref, cs_ref, acc_ref, cacc_ref = refs[1 + len(gs):]
        s = pl.program_id(0)

        @pl.when(s == 0)
        def _():
            acc_ref[...] = jnp.zeros_like(acc_ref)
            cacc_ref[...] = jnp.zeros_like(cacc_ref)

        av, col = a_ref[...], 0
        for g_ref, w in zip(g_refs, widths):
            gv = g_ref[...]
            acc_ref[:, col:col + w] += _dot_tn(av, gv)
            cacc_ref[:, col:col + w] += jnp.broadcast_to(jnp.sum(gv.astype(F32), axis=0, keepdims=True), (SUBLANES, w))
            col += w

        @pl.when(s == n_s - 1)
        def _():
            o_ref[...] = acc_ref[...].astype(BF16)
            cs_ref[...] = cacc_ref[...]

    return pl.pallas_call(
        body, grid=(n_s,),
        in_specs=[pl.BlockSpec((ts, K), lambda s: (s, 0))] + [pl.BlockSpec((ts, w), lambda s: (s, 0)) for w in widths],
        out_specs=[pl.BlockSpec((K, total), lambda s: (0, 0)), pl.BlockSpec((SUBLANES, total), lambda s: (0, 0))],
        out_shape=[jax.ShapeDtypeStruct((K, total), BF16), jax.ShapeDtypeStruct((SUBLANES, total), F32)],
        scratch_shapes=[pltpu.VMEM((K, total), F32), pltpu.VMEM((SUBLANES, total), F32)],
        name=name, compiler_params=_cparams(("arbitrary",), vmem_mb))(a, *gs)


DILS = tuple(d for _, d in GROUPS if d > 1)


def _res_spec(d, tm, width):
    return pl.BlockSpec((d, tm // d, width), lambda i: (0, i, 0))


def _lane_scratch(tm, width):
    return [pltpu.VMEM((tm, LANES), F32)] * (width // LANES)


def _to_residue(val, dst_refs, dils, tm, dtype, scr):
    for c, ref in enumerate(scr):
        ref[...] = val[:, c * LANES:(c + 1) * LANES]
    for dst_ref, d in zip(dst_refs, dils):
        for r in range(d):
            cols = [ref[pl.ds(r, tm // d, stride=d), :] for ref in scr]
            dst_ref[r] = jnp.concatenate(cols, axis=1).astype(dtype)


def _from_residue(rows_of, d, tm, scr):
    for r in range(d):
        v = rows_of(r).astype(F32)
        for c, ref in enumerate(scr):
            ref[pl.ds(r, tm // d, stride=d), :] = v[:, c * LANES:(c + 1) * LANES]
    return jnp.concatenate([ref[...] for ref in scr], axis=1)


def _ln0_fwd(x, g, b, after=(), *, tm=512):
    S, Dm = x.shape
    n_after = len(after)

    def body(x_ref, g_ref, b_ref, *rest):
        h_ref, hb_ref, *rest = rest[n_after:]
        xhat, _ = _ln_stats(x_ref[...])
        h = xhat * g_ref[...] + b_ref[...]
        h_ref[...] = h
        hb_ref[...] = h.astype(BF16)
        _to_residue(h, rest[:len(DILS)], DILS, tm, BF16, rest[len(DILS):])

    row = pl.BlockSpec((tm, Dm), lambda i: (i, 0))
    vec = pl.BlockSpec((1, Dm), lambda i: (0, 0))
    return pl.pallas_call(
        body, grid=(S // tm,), in_specs=[row, vec, vec] + [pl.BlockSpec(memory_space=pl.ANY)] * n_after,
        out_specs=[row, row] + [_res_spec(d, tm, Dm) for d in DILS],
        out_shape=[jax.ShapeDtypeStruct((S, Dm), F32), jax.ShapeDtypeStruct((S, Dm), BF16)]
        + [jax.ShapeDtypeStruct((d, S // d, Dm), BF16) for d in DILS],
        scratch_shapes=_lane_scratch(tm, Dm),
        name="ln0_fwd", compiler_params=_cparams(("parallel",), 32))(x, g, b, *after)


def _slab_spec(S, col0):
    return pl.BlockSpec((S, SLAB), lambda j: (0, col0 // SLAB + j))


def _zero_pads(scr, S):
    scr[0:PAD, :] = jnp.zeros((PAD, SLAB), F32)
    scr[S + PAD:S + 2 * PAD, :] = jnp.zeros((PAD, SLAB), F32)


def _shifted(scr, t):
    return (scr[PAD - 1 + t:PAD - 1 + t + CHUNK, :], scr[PAD + t:PAD + t + CHUNK, :],
            scr[PAD + 1 + t:PAD + 1 + t + CHUNK, :])


def _conv_gate_fwd(proj, conv_w):
    S = proj.shape[0]

    def body(b_ref, c_ref, h_ref, w_ref, o_ref, u_scr):
        _zero_pads(u_scr, S)
        for t in range(0, S, CHUNK):
            u_scr[PAD + t:PAD + t + CHUNK, :] = c_ref[t:t + CHUNK, :].astype(F32) * h_ref[t:t + CHUNK, :].astype(F32)
        w0, w1, w2 = w_ref[0:1, :], w_ref[1:2, :], w_ref[2:3, :]
        for t in range(0, S, CHUNK):
            um, u0, up = _shifted(u_scr, t)
            cv = w0 * um + w1 * u0 + w2 * up
            o_ref[t:t + CHUNK, :] = (b_ref[t:t + CHUNK, :].astype(F32) * cv).astype(BF16)

    return pl.pallas_call(
        body, grid=(D_CONV // SLAB,),
        in_specs=[_slab_spec(S, P_B), _slab_spec(S, P_C), _slab_spec(S, P_H),
                  pl.BlockSpec((3, SLAB), lambda j: (0, j))],
        out_specs=pl.BlockSpec((S, SLAB), lambda j: (0, j)),
        out_shape=jax.ShapeDtypeStruct((S, D_CONV), BF16),
        scratch_shapes=[pltpu.VMEM((S + 2 * PAD, SLAB), F32)],
        name="conv_gate_fwd", compiler_params=_cparams(("parallel",), 40))(proj, proj, proj, conv_w)


MASKED_DISTANCE = -1e34


def _attn_bias(i, sub, dil):
    j = lax.broadcasted_iota(jnp.int32, (2 * TQ, TQ), 0)
    a = lax.broadcasted_iota(jnp.int32, (2 * TQ, TQ), 1)
    rel = jnp.abs(j - RADIUS - a)
    kpos = i * TQ - RADIUS + j
    valid = (rel <= RADIUS) & (kpos >= 0) & (kpos < sub)
    return jnp.where(valid, -(rel * dil).astype(F32), MASKED_DISTANCE)


def _head_stats(rows):
    pad = jnp.zeros((LANES - len(rows), TQ), F32)
    return jnp.concatenate(list(rows) + [pad], axis=0).T


def _slope(g, h):
    return 2.0 ** (-8.0 * (g * HEADS_PER_GROUP + h + 1) / (N_GROUPS * HEADS_PER_GROUP))


def _window(p_ref, c_ref, n_ref):
    return jnp.concatenate([p_ref[TQ - RADIUS:, :], c_ref[...], n_ref[:RADIUS, :]], axis=0)


def _pair(a, h):
    return a[:, (h // 2) * LANES:(h // 2 + 1) * LANES]


def _own_lanes(a, h):
    lane = lax.broadcasted_iota(jnp.int32, a.shape, 1)
    return jnp.where((lane >= HEAD_DIM) == (h % 2 == 1), a, jnp.zeros_like(a))


def _own_rows(a, h):
    return a[(h % 2) * HEAD_DIM:(h % 2 + 1) * HEAD_DIM, :]


def _qkv_specs(nb, col0):
    def spec(col, shift):
        return pl.BlockSpec((None, TQ, GROUP_W), lambda r, i: (r, jnp.clip(i + shift, 0, nb - 1), col))

    return [spec(col0, 0), spec(col0 + 1, -1), spec(col0 + 1, 0), spec(col0 + 1, 1),
            spec(col0 + 2, -1), spec(col0 + 2, 0), spec(col0 + 2, 1)]


def _attn_fwd(qkv, col0, g):
    dil, sub, _ = qkv.shape
    nb = sub // TQ

    def body(q_ref, kp, kc, kn, vp, vc, vn, o_ref, lse_ref, ot_scr, s_scr, p_scr):
        bias = _attn_bias(pl.program_id(1), sub, dil)
        kwin = _window(kp, kc, kn)
        vwin = _window(vp, vc, vn)
        q = q_ref[...] * ATT_SCALE
        for h in range(HEADS_PER_GROUP):
            s_scr[h] = _dot_nt(_pair(kwin, h), _own_lanes(_pair(q, h), h))
        lse, inv_den = [], []
        for h in range(HEADS_PER_GROUP):
            s = s_scr[h] + _slope(g, h) * bias
            m = jnp.max(s, axis=0, keepdims=True)
            p = jnp.exp(s - m)
            den = jnp.sum(p, axis=0, keepdims=True)
            p_scr[h] = p.astype(BF16)
            inv_den.append(1.0 / den)
            lse.append(m + jnp.log(den))
        for h in range(HEADS_PER_GROUP):
            ot = _dot_tn(_pair(vwin, h), p_scr[h])
            ot_scr[h * HEAD_DIM:(h + 1) * HEAD_DIM, :] = _own_rows(ot, h) * inv_den[h]
        o_ref[...] = ot_scr[...].T
        lse_ref[...] = _head_stats(lse)

    return pl.pallas_call(
        body, grid=(dil, nb), in_specs=_qkv_specs(nb, col0),
        out_specs=[pl.BlockSpec((None, TQ, GROUP_W), lambda r, i: (r, i, 0)),
                   pl.BlockSpec((None, TQ, LANES), lambda r, i: (r, i, 0))],
        out_shape=[jax.ShapeDtypeStruct((dil, sub, GROUP_W), F32), jax.ShapeDtypeStruct((dil, sub, LANES), F32)],
        scratch_shapes=[pltpu.VMEM((GROUP_W, TQ), F32), pltpu.VMEM((HEADS_PER_GROUP, 2 * TQ, TQ), F32),
                        pltpu.VMEM((HEADS_PER_GROUP, 2 * TQ, TQ), BF16)],
        name=f"attn_fwd_g{g}", compiler_params=_cparams(("parallel", "arbitrary"), 32))(*([qkv] * 7))


def _expand_heads():
    h = lax.broadcasted_iota(jnp.int32, (LANES, GROUP_W), 0)
    c = lax.broadcasted_iota(jnp.int32, (LANES, GROUP_W), 1)
    return (c // HEAD_DIM == h).astype(F32)


def _dot_f32(a, b):
    return jnp.dot(a, b, preferred_element_type=F32, precision=lax.Precision.HIGHEST)


def _attn_combine(outs, lses, *, tm=512):
    S = outs[0].shape[1]
    n_col = GROUP_W // LANES

    def body(*refs):
        ins, e_ref = refs[:2 * N_GROUPS], refs[2 * N_GROUPS]
        c_ref, cb_ref, lt_ref = refs[2 * N_GROUPS + 1:2 * N_GROUPS + 4]
        scr = refs[2 * N_GROUPS + 4:]
        o, l = [ins[0][0]], [ins[N_GROUPS][0]]
        for k, d in enumerate(DILS):
            o_ref, l_ref = ins[1 + k], ins[N_GROUPS + 1 + k]
            o.append(_from_residue(lambda r: o_ref[r], d, tm, scr[k * (n_col + 1):k * (n_col + 1) + n_col]))
            l.append(_from_residue(lambda r: l_ref[r], d, tm, scr[k * (n_col + 1) + n_col:(k + 1) * (n_col + 1)]))
        m = jnp.maximum(jnp.maximum(l[0], l[1]), l[2])
        e = [jnp.exp(v - m) for v in l]
        den = e[0] + e[1] + e[2]
        comb = sum(_dot_f32(ev / den, e_ref[...]) * ov for ev, ov in zip(e, o))
        c_ref[...] = comb
        cb_ref[...] = comb.astype(BF16)
        lt_ref[...] = m + jnp.log(den)

    row = pl.BlockSpec((tm, GROUP_W), lambda i: (i, 0))
    dils = [d for _, d in GROUPS]
    return pl.pallas_call(
        body, grid=(S // tm,),
        in_specs=[_res_spec(d, tm, GROUP_W) for d in dils] + [_res_spec(d, tm, LANES) for d in dils]
        + [_resident((LANES, GROUP_W))],
        out_specs=[row, row, pl.BlockSpec((tm, LANES), lambda i: (i, 0))],
        out_shape=[jax.ShapeDtypeStruct((S, GROUP_W), F32), jax.ShapeDtypeStruct((S, GROUP_W), BF16),
                   jax.ShapeDtypeStruct((S, LANES), F32)],
        scratch_shapes=_lane_scratch(tm, GROUP_W + LANES) * len(DILS),
        name="attn_combine", compiler_params=_cparams(("parallel",), 32))(*outs, *lses, _expand_heads())


def _branch_mix(ya_in, comb_b, w_a, w_b, proj, *, tm=512):
    S = ya_in.shape[0]

    def body(ya_ref, cb_ref, wa_ref, wb_ref, ga_ref, gb_ref, yab_ref, mx_ref):
        y_a = _dot(ya_ref[...], wa_ref[...])
        y_b = _dot(cb_ref[...], wb_ref[...])
        yab_ref[:, 0:D_MODEL] = y_a.astype(BF16)
        yab_ref[:, D_MODEL:2 * D_MODEL] = y_b.astype(BF16)
        mx = jax.nn.sigmoid(ga_ref[...].astype(F32)) * y_a + jax.nn.sigmoid(gb_ref[...].astype(F32)) * y_b
        mx_ref[...] = mx.astype(BF16)

    return pl.pallas_call(
        body, grid=(S // tm,),
        in_specs=[pl.BlockSpec((tm, D_CONV), lambda i: (i, 0)), pl.BlockSpec((tm, GROUP_W), lambda i: (i, 0)),
                  pl.BlockSpec((D_CONV, D_MODEL), lambda i: (0, 0)), pl.BlockSpec((GROUP_W, D_MODEL), lambda i: (0, 0)),
                  pl.BlockSpec((tm, D_MODEL), lambda i: (i, P_GA // D_MODEL)),
                  pl.BlockSpec((tm, D_MODEL), lambda i: (i, P_GB // D_MODEL))],
        out_specs=[pl.BlockSpec((tm, 2 * D_MODEL), lambda i: (i, 0)), pl.BlockSpec((tm, D_MODEL), lambda i: (i, 0))],
        out_shape=[jax.ShapeDtypeStruct((S, 2 * D_MODEL), BF16), jax.ShapeDtypeStruct((S, D_MODEL), BF16)],
        name="branch_mix", compiler_params=_cparams(("parallel",), 40))(ya_in, comb_b, w_a, w_b, proj, proj)


def _mix_ln1(mixin, w_o, b_o, h0, g1, b1, *, tm=512):
    S = mixin.shape[0]

    def body(mx_ref, wo_ref, bo_ref, h0_ref, g_ref, b_ref, xh_ref, rs_ref, h1b_ref):
        z = ALPHA * h0_ref[...] + _dot(mx_ref[...], wo_ref[...]) + bo_ref[...]
        xhat, rstd = _ln_stats(z)
        xh_ref[...] = xhat
        rs_ref[...] = jnp.broadcast_to(rstd, (tm, LANES))
        h1b_ref[...] = (xhat * g_ref[...] + b_ref[...]).astype(BF16)

    row = pl.BlockSpec((tm, D_MODEL), lambda i: (i, 0))
    vec = pl.BlockSpec((1, D_MODEL), lambda i: (0, 0))
    return pl.pallas_call(
        body, grid=(S // tm,),
        in_specs=[row, pl.BlockSpec((D_MODEL, D_MODEL), lambda i: (0, 0)), vec, row, vec, vec],
        out_specs=[row, pl.BlockSpec((tm, LANES), lambda i: (i, 0)), row],
        out_shape=[jax.ShapeDtypeStruct((S, D_MODEL), F32), jax.ShapeDtypeStruct((S, LANES), F32),
                   jax.ShapeDtypeStruct((S, D_MODEL), BF16)],
        name="mix_ln1", compiler_params=_cparams(("parallel",), 40))(mixin, w_o, b_o, h0, g1, b1)


def _gelu_parts(cz):
    cdf = 0.5 * (1.0 + lax.erf(cz * INV_SQRT2))
    return cdf, cz * cdf


def _ffn_conv_fwd(up, cw, cb):
    S = up.shape[0]

    def body(a_ref, g_ref, w_ref, cb_ref, o_ref, a_scr):
        _zero_pads(a_scr, S)
        for t in range(0, S, CHUNK):
            a_scr[PAD + t:PAD + t + CHUNK, :] = a_ref[t:t + CHUNK, :].astype(F32)
        w0, w1, w2 = w_ref[0:1, :], w_ref[1:2, :], w_ref[2:3, :]
        for t in range(0, S, CHUNK):
            am, a0, ap = _shifted(a_scr, t)
            _, gel = _gelu_parts(w0 * am + w1 * a0 + w2 * ap + cb_ref[...])
            o_ref[t:t + CHUNK, :] = (gel * g_ref[t:t + CHUNK, :].astype(F32)).astype(BF16)

    return pl.pallas_call(
        body, grid=(D_FF // SLAB,),
        in_specs=[_slab_spec(S, 0), _slab_spec(S, D_FF), pl.BlockSpec((3, SLAB), lambda j: (0, j)),
                  pl.BlockSpec((1, SLAB), lambda j: (0, j))],
        out_specs=pl.BlockSpec((S, SLAB), lambda j: (0, j)),
        out_shape=jax.ShapeDtypeStruct((S, D_FF), BF16),
        scratch_shapes=[pltpu.VMEM((S + 2 * PAD, SLAB), F32)],
        name="ffn_conv_fwd", compiler_params=_cparams(("parallel",), 40))(up, up, cw, cb)


def _down_ln2_loss(f, w_down, b_down, xhat1, g1, b1, g2, b2, target, *, tm=256):
    S = f.shape[0]

    def body(f_ref, wd_ref, bd_ref, xh1_ref, g1_ref, b1_ref, g2_ref, b2_ref, t_ref, dz_ref, dzb_ref, st_ref):
        h1 = xh1_ref[...] * g1_ref[...] + b1_ref[...]
        z = ALPHA * h1 + _dot(f_ref[...], wd_ref[...]) + bd_ref[...]
        xhat, rstd = _ln_stats(z)
        err = xhat * g2_ref[...] + b2_ref[...] - t_ref[...]
        loss = (0.5 / D_MODEL) * jnp.sum(jnp.sum(err * err, axis=1, keepdims=True), axis=0, keepdims=True)
        dh2 = err * (1.0 / D_MODEL)
        dz = _ln_bwd(dh2, xhat, rstd, g2_ref[...])
        dz_ref[...] = dz
        dzb_ref[...] = dz.astype(BF16)
        upd = _rows8([jnp.sum(dh2 * xhat, axis=0, keepdims=True), jnp.sum(dh2, axis=0, keepdims=True),
                      jnp.broadcast_to(loss, (1, D_MODEL)), jnp.sum(dz, axis=0, keepdims=True)], D_MODEL)

        @pl.when(pl.program_id(0) == 0)
        def _():
            st_ref[...] = upd

        @pl.when(pl.program_id(0) != 0)
        def _():
            st_ref[...] += upd

    row = pl.BlockSpec((tm, D_MODEL), lambda i: (i, 0))
    vec = pl.BlockSpec((1, D_MODEL), lambda i: (0, 0))
    return pl.pallas_call(
        body, grid=(S // tm,),
        in_specs=[pl.BlockSpec((tm, D_FF), lambda i: (i, 0)), _resident((D_FF, D_MODEL)),
                  vec, row, vec, vec, vec, vec, row],
        out_specs=[row, row, pl.BlockSpec((SUBLANES, D_MODEL), lambda i: (0, 0))],
        out_shape=[jax.ShapeDtypeStruct((S, D_MODEL), F32), jax.ShapeDtypeStruct((S, D_MODEL), BF16),
                   jax.ShapeDtypeStruct((SUBLANES, D_MODEL), F32)],
        name="down_ln2_loss", compiler_params=_cparams(("arbitrary",), 48))(
            f, w_down, b_down, xhat1, g1, b1, g2, b2, target)


def _ffn_conv_bwd(up, df, cw, cb):
    S = up.shape[0]

    def body(a_ref, g_ref, df_ref, w_ref, cb_ref, dup_ref, sm_ref, a_scr, d_scr):
        _zero_pads(a_scr, S)
        _zero_pads(d_scr, S)
        for t in range(0, S, CHUNK):
            a_scr[PAD + t:PAD + t + CHUNK, :] = a_ref[t:t + CHUNK, :].astype(F32)
        w0, w1, w2 = w_ref[0:1, :], w_ref[1:2, :], w_ref[2:3, :]
        zero = jnp.zeros((1, SLAB), F32)
        s_dg, s_dcz, s_w0, s_w1, s_w2 = zero, zero, zero, zero, zero
        for t in range(0, S, CHUNK):
            am, a0, ap = _shifted(a_scr, t)
            cz = w0 * am + w1 * a0 + w2 * ap + cb_ref[...]
            cdf, gel = _gelu_parts(cz)
            dfv = df_ref[t:t + CHUNK, :].astype(F32)
            dgte = dfv * gel
            dcz = dfv * g_ref[t:t + CHUNK, :].astype(F32) * (cdf + cz * jnp.exp(-0.5 * cz * cz) * INV_SQRT_2PI)
            dup_ref[1, t:t + CHUNK, :] = dgte.astype(BF16)
            d_scr[PAD + t:PAD + t + CHUNK, :] = dcz
            s_dg = s_dg + jnp.sum(dgte, axis=0, keepdims=True)
            s_dcz = s_dcz + jnp.sum(dcz, axis=0, keepdims=True)
            s_w0 = s_w0 + jnp.sum(dcz * am, axis=0, keepdims=True)
            s_w1 = s_w1 + jnp.sum(dcz * a0, axis=0, keepdims=True)
            s_w2 = s_w2 + jnp.sum(dcz * ap, axis=0, keepdims=True)
        s_da = zero
        for t in range(0, S, CHUNK):
            dm, d0, dp = _shifted(d_scr, t)
            da = w0 * dp + w1 * d0 + w2 * dm
            dup_ref[0, t:t + CHUNK, :] = da.astype(BF16)
            s_da = s_da + jnp.sum(da, axis=0, keepdims=True)
        sm_ref[...] = _rows8([s_da, s_dg, s_dcz, s_w0, s_w1, s_w2], SLAB)

    return pl.pallas_call(
        body, grid=(D_FF // SLAB,),
        in_specs=[_slab_spec(S, 0), _slab_spec(S, D_FF), pl.BlockSpec((S, SLAB), lambda j: (0, j)),
                  pl.BlockSpec((3, SLAB), lambda j: (0, j)), pl.BlockSpec((1, SLAB), lambda j: (0, j))],
        out_specs=[pl.BlockSpec((2, S, SLAB), lambda j: (0, 0, j)), pl.BlockSpec((SUBLANES, SLAB), lambda j: (0, j))],
        out_shape=[jax.ShapeDtypeStruct((2, S, D_FF), BF16), jax.ShapeDtypeStruct((SUBLANES, D_FF), F32)],
        scratch_shapes=[pltpu.VMEM((S + 2 * PAD, SLAB), F32)] * 2,
        name="ffn_conv_bwd", compiler_params=_cparams(("parallel",), 48))(up, up, df, cw, cb)


def _resident(shape):
    nd = len(shape)
    return pl.BlockSpec(shape, lambda *_: (0,) * nd, pipeline_mode=pl.Buffered(1))


def _up_bwd_ln1(dup, w_up3, dz2, xhat1, rstd1, g1, *, tm=256):
    S = dz2.shape[0]
    ns, _, tk = w_up3.shape
    per_plane = D_FF // tk

    def body(du_ref, w_ref, dz2_ref, xh_ref, rs_ref, g_ref, dz_ref, dzb_ref, st_ref):
        dh = ALPHA * dz2_ref[...]
        for k in range(ns):
            col = (k % per_plane) * tk
            dh = dh + _dot_nt(du_ref[k // per_plane, :, col:col + tk], w_ref[k])
        xhat = xh_ref[...]
        dz = _ln_bwd(dh, xhat, rs_ref[:, 0:1], g_ref[...])
        dz_ref[...] = dz
        dzb_ref[...] = dz.astype(BF16)
        upd = _rows8([jnp.sum(dh * xhat, axis=0, keepdims=True), jnp.sum(dh, axis=0, keepdims=True),
                      jnp.sum(dz, axis=0, keepdims=True)], D_MODEL)

        @pl.when(pl.program_id(0) == 0)
        def _():
            st_ref[...] = upd

        @pl.when(pl.program_id(0) != 0)
        def _():
            st_ref[...] += upd

    row = pl.BlockSpec((tm, D_MODEL), lambda i: (i, 0))
    return pl.pallas_call(
        body, grid=(S // tm,),
        in_specs=[pl.BlockSpec((dup.shape[0], tm, D_FF), lambda i: (0, i, 0)), _resident(w_up3.shape),
                  row, row, pl.BlockSpec((tm, LANES), lambda i: (i, 0)), pl.BlockSpec((1, D_MODEL), lambda i: (0, 0))],
        out_specs=[row, row, pl.BlockSpec((SUBLANES, D_MODEL), lambda i: (0, 0))],
        out_shape=[jax.ShapeDtypeStruct((S, D_MODEL), F32), jax.ShapeDtypeStruct((S, D_MODEL), BF16),
                   jax.ShapeDtypeStruct((SUBLANES, D_MODEL), F32)],
        name="up_bwd_ln1", compiler_params=_cparams(("arbitrary",), 48))(dup, w_up3, dz2, xhat1, rstd1, g1)


def _mix_bwd(dz1b, w_o, proj, yab, *, tm=512):
    S = dz1b.shape[0]

    def body(dz_ref, wo_ref, ga_ref, gb_ref, y_ref, dy_ref, dg_ref):
        dmx = _dot_nt(dz_ref[...], wo_ref[...])
        for k, gt_ref in enumerate((ga_ref, gb_ref)):
            sl = slice(k * D_MODEL, (k + 1) * D_MODEL)
            sg = jax.nn.sigmoid(gt_ref[...].astype(F32))
            dy_ref[:, sl] = (dmx * sg).astype(BF16)
            dg_ref[k] = (dmx * y_ref[:, sl].astype(F32) * sg * (1.0 - sg)).astype(BF16)

    row = pl.BlockSpec((tm, D_MODEL), lambda i: (i, 0))
    wide = pl.BlockSpec((tm, 2 * D_MODEL), lambda i: (i, 0))
    return pl.pallas_call(
        body, grid=(S // tm,),
        in_specs=[row, _resident(w_o.shape), pl.BlockSpec((tm, D_MODEL), lambda i: (i, P_GA // D_MODEL)),
                  pl.BlockSpec((tm, D_MODEL), lambda i: (i, P_GB // D_MODEL)), wide],
        out_specs=[wide, pl.BlockSpec((2, tm, D_MODEL), lambda i: (0, i, 0))],
        out_shape=[jax.ShapeDtypeStruct((S, 2 * D_MODEL), BF16), jax.ShapeDtypeStruct((2, S, D_MODEL), BF16)],
        name="mix_bwd", compiler_params=_cparams(("parallel",), 40))(dz1b, w_o, proj, proj, yab)


def _conv_gate_bwd(proj, dya_in, conv_w):
    S = proj.shape[0]

    def body(b_ref, c_ref, h_ref, dy_ref, w_ref, o_ref, sm_ref, u_scr, d_scr):
        _zero_pads(u_scr, S)
        _zero_pads(d_scr, S)
        for t in range(0, S, CHUNK):
            u_scr[PAD + t:PAD + t + CHUNK, :] = c_ref[t:t + CHUNK, :].astype(F32) * h_ref[t:t + CHUNK, :].astype(F32)
        w0, w1, w2 = w_ref[0:1, :], w_ref[1:2, :], w_ref[2:3, :]
        zero = jnp.zeros((1, SLAB), F32)
        s_w0, s_w1, s_w2 = zero, zero, zero
        for t in range(0, S, CHUNK):
            um, u0, up = _shifted(u_scr, t)
            dy = dy_ref[t:t + CHUNK, :].astype(F32)
            o_ref[0, t:t + CHUNK, :] = (dy * (w0 * um + w1 * u0 + w2 * up)).astype(BF16)
            dcv = dy * b_ref[t:t + CHUNK, :].astype(F32)
            d_scr[PAD + t:PAD + t + CHUNK, :] = dcv
            s_w0 = s_w0 + jnp.sum(dcv * um, axis=0, keepdims=True)
            s_w1 = s_w1 + jnp.sum(dcv * u0, axis=0, keepdims=True)
            s_w2 = s_w2 + jnp.sum(dcv * up, axis=0, keepdims=True)
        for t in range(0, S, CHUNK):
            dm, d0, dp = _shifted(d_scr, t)
            du = w0 * dp + w1 * d0 + w2 * dm
            o_ref[1, t:t + CHUNK, :] = (du * h_ref[t:t + CHUNK, :].astype(F32)).astype(BF16)
            o_ref[2, t:t + CHUNK, :] = (du * c_ref[t:t + CHUNK, :].astype(F32)).astype(BF16)
        sm_ref[...] = _rows8([s_w0, s_w1, s_w2], SLAB)

    return pl.pallas_call(
        body, grid=(D_CONV // SLAB,),
        in_specs=[_slab_spec(S, P_B), _slab_spec(S, P_C), _slab_spec(S, P_H),
                  pl.BlockSpec((S, SLAB), lambda j: (0, j)), pl.BlockSpec((3, SLAB), lambda j: (0, j))],
        out_specs=[pl.BlockSpec((3, S, SLAB), lambda j: (0, 0, j)), pl.BlockSpec((SUBLANES, SLAB), lambda j: (0, j))],
        out_shape=[jax.ShapeDtypeStruct((3, S, D_CONV), BF16), jax.ShapeDtypeStruct((SUBLANES, D_CONV), F32)],
        scratch_shapes=[pltpu.VMEM((S + 2 * PAD, SLAB), F32)] * 2,
        name="conv_gate_bwd", compiler_params=_cparams(("parallel",), 48))(proj, proj, proj, dya_in, conv_w)


def _comb_bwd(dyab, w_b, comb, lse_tot, *, tm=512):
    S = comb.shape[0]
    widths, dtypes = (GROUP_W, LANES, LANES), (BF16, F32, F32)

    def body(dy_ref, wb_ref, c_ref, lt_ref, e_ref, *rest):
        outs, scr = rest[:3 * N_GROUPS], rest[3 * N_GROUPS:]
        dcb = _dot_nt(dy_ref[...], wb_ref[...]).astype(BF16)
        dc = dcb.astype(F32)
        delta = lax.dot_general(dc * c_ref[...], e_ref[...], (((1,), (1,)), ((), ())),
                                preferred_element_type=F32, precision=lax.Precision.HIGHEST)
        for k, (val, dtype) in enumerate(zip((dc, lt_ref[...], delta), dtypes)):
            outs[k][0] = val.astype(dtype)
            _to_residue(val, [outs[3 * (1 + j) + k] for j in range(len(DILS))], DILS, tm, dtype,
                        scr[:val.shape[1] // LANES])

    out_specs, out_shape = [], []
    for _, d in GROUPS:
        out_specs += [_res_spec(d, tm, w) for w in widths]
        out_shape += [jax.ShapeDtypeStruct((d, S // d, w), t) for w, t in zip(widths, dtypes)]
    res = pl.pallas_call(
        body, grid=(S // tm,),
        in_specs=[pl.BlockSpec((tm, D_MODEL), lambda i: (i, 1)), _resident(w_b.shape),
                  pl.BlockSpec((tm, GROUP_W), lambda i: (i, 0)), pl.BlockSpec((tm, LANES), lambda i: (i, 0)),
                  _resident((LANES, GROUP_W))],
        out_specs=out_specs, out_shape=out_shape, scratch_shapes=_lane_scratch(tm, GROUP_W),
        name="comb_bwd", compiler_params=_cparams(("parallel",), 32))(dyab, w_b, comb, lse_tot, _expand_heads())
    return [tuple(res[3 * g:3 * g + 3]) for g in range(N_GROUPS)]


def _attn_bwd(qkv, col0, g, dcomb, lse_tot, delta):
    dil, sub, _ = qkv.shape
    nb = sub // TQ

    def body(q_ref, kp, kc, kn, vp, vc, vn, do_ref, lse_ref, dl_ref, dq_ref, dk_ref, dv_ref,
             ak, av, dqt_scr, s_scr, dp_scr, ds_scr, p_scr):
        i = pl.program_id(1)

        @pl.when(i == 0)
        def _():
            ak[...] = jnp.zeros_like(ak)
            av[...] = jnp.zeros_like(av)

        @pl.when(i < nb)
        def _():
            bias = _attn_bias(i, sub, dil)
            kwin = _window(kp, kc, kn)
            vwin = _window(vp, vc, vn)
            q = q_ref[...] * ATT_SCALE
            do = do_ref[...]
            lse_t, dl_t = lse_ref[...].T, dl_ref[...].T
            for h in range(HEADS_PER_GROUP):
                s_scr[h] = _dot_nt(_pair(kwin, h), _own_lanes(_pair(q, h), h))
                dp_scr[h] = _dot_nt(_pair(vwin, h), _own_lanes(_pair(do, h), h))
            for h in range(HEADS_PER_GROUP):
                p = jnp.exp(s_scr[h] + _slope(g, h) * bias - lse_t[h:h + 1, :])
                ds_scr[h] = (p * (dp_scr[h] - dl_t[h:h + 1, :])).astype(BF16)
                p_scr[h] = p.astype(BF16)
            for h in range(HEADS_PER_GROUP):
                dqt_scr[h * HEAD_DIM:(h + 1) * HEAD_DIM, :] = _own_rows(_dot_tn(_pair(kwin, h), ds_scr[h]), h)
            for h in range(0, HEADS_PER_GROUP, 2):
                cols = slice(h * HEAD_DIM, (h + 2) * HEAD_DIM)
                q2 = jnp.concatenate([_own_lanes(_pair(q, h), h), _own_lanes(_pair(q, h), h + 1)], axis=0)
                do2 = jnp.concatenate([_own_lanes(_pair(do, h), h), _own_lanes(_pair(do, h), h + 1)], axis=0)
                ak[RADIUS:RADIUS + 2 * TQ, cols] += _dot(jnp.concatenate([ds_scr[h], ds_scr[h + 1]], axis=1), q2)
                av[RADIUS:RADIUS + 2 * TQ, cols] += _dot(jnp.concatenate([p_scr[h], p_scr[h + 1]], axis=1), do2)
            dq_ref[...] = (dqt_scr[...].T * ATT_SCALE).astype(BF16)

        dk_ref[...] = ak[0:TQ, :].astype(BF16)
        dv_ref[...] = av[0:TQ, :].astype(BF16)
        ak[0:2 * TQ, :] = ak[TQ:3 * TQ, :]
        av[0:2 * TQ, :] = av[TQ:3 * TQ, :]
        ak[2 * TQ:3 * TQ, :] = jnp.zeros((TQ, GROUP_W), F32)
        av[2 * TQ:3 * TQ, :] = jnp.zeros((TQ, GROUP_W), F32)

    tok = pl.BlockSpec((None, TQ, GROUP_W), lambda r, i: (r, jnp.minimum(i, nb - 1), 0))
    stat = pl.BlockSpec((None, TQ, LANES), lambda r, i: (r, jnp.minimum(i, nb - 1), 0))
    dkv_spec = pl.BlockSpec((None, TQ, GROUP_W), lambda r, i: (r, jnp.maximum(i - 1, 0), 0))
    return pl.pallas_call(
        body, grid=(dil, nb + 1), in_specs=_qkv_specs(nb, col0) + [tok, stat, stat],
        out_specs=[tok, dkv_spec, dkv_spec], out_shape=[jax.ShapeDtypeStruct((dil, sub, GROUP_W), BF16)] * 3,
        scratch_shapes=[pltpu.VMEM((3 * TQ, GROUP_W), F32)] * 2 + [pltpu.VMEM((GROUP_W, TQ), F32)]
        + [pltpu.VMEM((HEADS_PER_GROUP, 2 * TQ, TQ), F32)] * 2 + [pltpu.VMEM((HEADS_PER_GROUP, 2 * TQ, TQ), BF16)] * 2,
        name=f"attn_bwd_g{g}", compiler_params=_cparams(("arbitrary", "arbitrary"), 32))(
            *([qkv] * 7), dcomb, lse_tot, delta)


def _in_bwd_ln0(dgated, dqkv, w_nat, w_dil, dz1, x, g0, *, tm=256):
    S = x.shape[0]
    n_gated, n_in = len(dgated), 3 * N_GROUPS

    def body(*refs):
        g_refs, d_refs = refs[:n_gated], refs[n_gated:n_gated + n_in]
        wn_ref, *wd_refs = refs[n_gated + n_in:n_gated + n_in + N_GROUPS]
        dz_ref, x_ref, g_ref, gx_ref, st_ref, *tmp_ref = refs[n_gated + n_in + N_GROUPS:]
        dh = ALPHA * dz_ref[...]
        col = 0
        for ref in g_refs:
            for k in range(ref.shape[0]):
                dh = dh + _dot_nt(ref[k], wn_ref[:, col:col + D_MODEL])
                col += D_MODEL
        for g, (_, d) in enumerate(GROUPS):
            rows = [jnp.concatenate([d_refs[3 * g + k][r] for k in range(3)], axis=1) for r in range(d)]
            w = wn_ref[:, col:col + QKV_W] if d == 1 else wd_refs[g - 1][...]
            res = _dot_nt(jnp.concatenate(rows, axis=0), w)
            if d == 1:
                dh = dh + res
            else:
                n = tm // d
                dh = dh + _from_residue(lambda r: res[r * n:(r + 1) * n, :], d, tm, tmp_ref)
        xhat, rstd = _ln_stats(x_ref[...])
        gx_ref[...] = _ln_bwd(dh, xhat, rstd, g_ref[...])
        upd = _rows8([jnp.sum(dh * xhat, axis=0, keepdims=True), jnp.sum(dh, axis=0, keepdims=True)], D_MODEL)

        @pl.when(pl.program_id(0) == 0)
        def _():
            st_ref[...] = upd

        @pl.when(pl.program_id(0) != 0)
        def _():
            st_ref[...] += upd

    row = pl.BlockSpec((tm, D_MODEL), lambda i: (i, 0))
    g_specs = [pl.BlockSpec((a.shape[0], tm, D_MODEL), lambda i: (0, i, 0)) for a in dgated]
    d_specs = []
    for _, d in GROUPS:
        d_specs += [_res_spec(d, tm, GROUP_W)] * 3
    operands = list(dgated) + [a for grp in dqkv for a in grp] + [w_nat] + list(w_dil) + [dz1, x, g0]
    return pl.pallas_call(
        body, grid=(S // tm,),
        in_specs=g_specs + d_specs + [_resident(w_nat.shape)] + [_resident(w.shape) for w in w_dil]
        + [row, row, pl.BlockSpec((1, D_MODEL), lambda i: (0, 0))],
        out_specs=[row, pl.BlockSpec((SUBLANES, D_MODEL), lambda i: (0, 0))],
        out_shape=[jax.ShapeDtypeStruct((S, D_MODEL), F32), jax.ShapeDtypeStruct((SUBLANES, D_MODEL), F32)],
        scratch_shapes=_lane_scratch(tm, D_MODEL),
        name="in_bwd_ln0", compiler_params=_cparams(("arbitrary",), 52))(*operands)


HBM_SPEC = pl.BlockSpec(memory_space=pltpu.HBM)


def _place():
    x, y, c = lax.axis_index("x"), lax.axis_index("y"), lax.axis_index("c")
    chips = [(1 - x, y), (x, 1 - y), (1 - x, 1 - y)]
    return x, y, c, chips


def _allgather_shards(shards, after, *, name, collective_id):
    n = len(shards)
    per = 6

    def body(*refs):
        ins, outs = refs[:n], refs[n + len(after):2 * n + len(after)]
        send_sems, recv_sems, loc_sems = refs[2 * n + len(after):]
        x, y, c, chips = _place()
        me = 2 * x + y
        sib = (x, y, 1 - c)
        peers = [sib] + [(px, py, c) for px, py in chips]
        barrier = pltpu.get_barrier_semaphore()
        for peer in peers:
            pl.semaphore_signal(barrier, inc=1, device_id=peer, device_id_type=MESH)
        pl.semaphore_wait(barrier, len(peers))

        def rcopy(w, k, src, dst, to):
            return pltpu.make_async_remote_copy(src_ref=src, dst_ref=dst, send_sem=send_sems.at[per * w + k],
                                                recv_sem=recv_sems.at[per * w + k], device_id=to, device_id_type=MESH)

        split = [s.shape[0] == N_CORES for s in shards]
        half = lambda w: c if split[w] else 0
        local, sends = [], []
        for w in range(n):
            cp = pltpu.make_async_copy(ins[w], outs[w].at[me], loc_sems.at[w])
            cp.start()
            local.append(cp)
            for j, (px, py) in enumerate(chips):
                cp = rcopy(w, j, ins[w].at[half(w)], outs[w].at[me, half(w)], (px, py, c))
                cp.start()
                sends.append(cp)
        for w in range(n):
            for j, (px, py) in enumerate(chips):
                slot = outs[w].at[2 * px + py, half(w)]
                rcopy(w, j, slot, slot, (px, py, c)).wait_recv()
                if split[w]:
                    cp = rcopy(w, 3 + j, slot, slot, sib)
                    cp.start()
                    sends.append(cp)
        for w in range(n):
            if split[w]:
                for j, (px, py) in enumerate(chips):
                    slot = outs[w].at[2 * px + py, 1 - c]
                    rcopy(w, 3 + j, slot, slot, sib).wait_recv()
        for cp in sends:
            cp.wait_send()
        for cp in local:
            cp.wait()

    return pl.kernel(
        body, out_type=[jax.ShapeDtypeStruct((N_CHIPS,) + s.shape, s.dtype) for s in shards],
        mesh=plsc.ScalarSubcoreMesh(axis_name="sequencer", num_cores=1),
        scratch_types=[pltpu.SemaphoreType.DMA((per * n,)), pltpu.SemaphoreType.DMA((per * n,)),
                       pltpu.SemaphoreType.DMA((n,))],
        name=name, compiler_params=pltpu.CompilerParams(collective_id=collective_id))(*shards, *after)


def _exchange_grads(grads, *, name, collective_id):
    n = len(grads)
    per = 7

    def body(*refs):
        ins, outs = refs[:n], refs[n:2 * n]
        send_sems, recv_sems, loc_sems = refs[2 * n:]
        x, y, c, chips = _place()
        me = 2 * x + y
        sib = (x, y, 1 - c)
        peers = [sib] + [(px, py, c) for px, py in chips]
        barrier = pltpu.get_barrier_semaphore()
        for peer in peers:
            pl.semaphore_signal(barrier, inc=1, device_id=peer, device_id_type=MESH)
        pl.semaphore_wait(barrier, len(peers))

        def rcopy(w, k, src, dst, to):
            return pltpu.make_async_remote_copy(src_ref=src, dst_ref=dst, send_sem=send_sems.at[per * w + k],
                                                recv_sem=recv_sems.at[per * w + k], device_id=to, device_id_type=MESH)

        local, sends = [], []
        for w in range(n):
            cp = pltpu.make_async_copy(ins[w].at[me], outs[w].at[c, me], loc_sems.at[w])
            cp.start()
            local.append(cp)
            cp = rcopy(w, 0, ins[w].at[me], outs[w].at[c, me], sib)
            cp.start()
            sends.append(cp)
            for j, (px, py) in enumerate(chips):
                cp = rcopy(w, 1 + j, ins[w].at[2 * px + py], outs[w].at[c, me], (px, py, c))
                cp.start()
                sends.append(cp)
        for w in range(n):
            for j, (px, py) in enumerate(chips):
                slot = outs[w].at[c, 2 * px + py]
                rcopy(w, 1 + j, slot, slot, (px, py, c)).wait_recv()
                cp = rcopy(w, 4 + j, slot, slot, sib)
                cp.start()
                sends.append(cp)
        for w in range(n):
            slot = outs[w].at[1 - c, me]
            rcopy(w, 0, slot, slot, sib).wait_recv()
            for j, (px, py) in enumerate(chips):
                slot = outs[w].at[1 - c, 2 * px + py]
                rcopy(w, 4 + j, slot, slot, sib).wait_recv()
        for cp in sends:
            cp.wait_send()
        for cp in local:
            cp.wait()

    return pl.kernel(
        body, out_type=[jax.ShapeDtypeStruct((N_CORES,) + g.shape, g.dtype) for g in grads],
        mesh=plsc.ScalarSubcoreMesh(axis_name="sequencer", num_cores=1),
        scratch_types=[pltpu.SemaphoreType.DMA((per * n,)), pltpu.SemaphoreType.DMA((per * n,)),
                       pltpu.SemaphoreType.DMA((n,))],
        name=name, compiler_params=pltpu.CompilerParams(collective_id=collective_id))(*grads)


def _allgather_small(vec):
    def body(v_ref, o_ref, send_sems, recv_sems, loc_sem):
        x, y, c = lax.axis_index("x"), lax.axis_index("y"), lax.axis_index("c")
        me = 4 * x + 2 * y + c

        def peer(k):
            flip = lambda v, bit: 1 - v if (k >> bit) & 1 else v
            return flip(x, 2), flip(y, 1), flip(c, 0)

        loc = pltpu.make_async_copy(v_ref, o_ref.at[me], loc_sem)
        loc.start()
        sends = []
        for k in range(1, N_DEV):
            cp = pltpu.make_async_remote_copy(src_ref=v_ref, dst_ref=o_ref.at[me], send_sem=send_sems.at[k - 1],
                                              recv_sem=recv_sems.at[k - 1], device_id=peer(k), device_id_type=MESH)
            cp.start()
            sends.append(cp)
        for k in range(1, N_DEV):
            px, py, pc = peer(k)
            pltpu.make_async_remote_copy(src_ref=v_ref, dst_ref=o_ref.at[4 * px + 2 * py + pc],
                                         send_sem=send_sems.at[k - 1], recv_sem=recv_sems.at[k - 1],
                                         device_id=(px, py, pc), device_id_type=MESH).wait_recv()
        for cp in sends:
            cp.wait_send()
        loc.wait()

    return pl.pallas_call(
        body, in_specs=[HBM_SPEC], out_specs=HBM_SPEC,
        out_shape=jax.ShapeDtypeStruct((N_DEV,) + vec.shape, vec.dtype),
        scratch_shapes=[pltpu.SemaphoreType.DMA((N_DEV - 1,)), pltpu.SemaphoreType.DMA((N_DEV - 1,)),
                        pltpu.SemaphoreType.DMA],
        name="allgather_small")(vec)


def _adamw(w, g, m, v):
    m = ADAM_B1 * m + (1.0 - ADAM_B1) * g
    v = ADAM_B2 * v + (1.0 - ADAM_B2) * (g * g)
    m_hat = m / (1.0 - ADAM_B1 ** ADAM_STEP)
    v_hat = v / (1.0 - ADAM_B2 ** ADAM_STEP)
    delta = -ADAM_LR * (m_hat / (jnp.sqrt(v_hat) + ADAM_EPS) + ADAM_WD * w)
    return delta, m, v


def _reduce_adamw(parts, w, m, v, *, tr, name):
    R, C = w.shape

    def body(p_ref, w_ref, m_ref, v_ref, g_ref, d_ref, nm_ref, nv_ref):
        def core_sum(cc):
            s = p_ref[cc, 0].astype(F32)
            for k in range(1, N_CHIPS):
                s = s + p_ref[cc, k].astype(F32)
            return s

        g = core_sum(0) + core_sum(1)
        delta, nm, nv = _adamw(w_ref[...], g, m_ref[...], v_ref[...])
        g_ref[...] = g
        d_ref[...] = delta
        nm_ref[...] = nm
        nv_ref[...] = nv

    blk = pl.BlockSpec((tr, C), lambda i: (i, 0))
    return pl.pallas_call(
        body, grid=(R // tr,),
        in_specs=[pl.BlockSpec((N_CORES, N_CHIPS, tr, C), lambda i: (0, 0, i, 0)), blk, blk, blk],
        out_specs=[blk] * 4, out_shape=[jax.ShapeDtypeStruct((R, C), F32)] * 4,
        name=name, compiler_params=_cparams(("parallel",), 40))(parts, w, m, v)


def _reduce_adamw_vectors(allv, offs, ws, ms, vs):
    n = len(ws)

    def body(a_ref, *refs):
        w_refs, m_refs, v_refs = refs[:n], refs[n:2 * n], refs[2 * n:3 * n]
        tot_ref, outs = refs[3 * n], refs[3 * n + 1:]
        s = a_ref[0]
        for d in range(1, N_DEV):
            s = s + a_ref[d]
        tot_ref[...] = s
        for k in range(n):
            g = s[:, offs[k]:offs[k] + w_refs[k].shape[1]]
            delta, nm, nv = _adamw(w_refs[k][...], g, m_refs[k][...], v_refs[k][...])
            for ref, val in zip(outs[4 * k:4 * k + 4], (g, delta, nm, nv)):
                ref[...] = val

    out_shape = [jax.ShapeDtypeStruct(allv.shape[1:], F32)]
    for w in ws:
        out_shape += [jax.ShapeDtypeStruct(w.shape, F32)] * 4
    res = pl.pallas_call(body, out_shape=out_shape, name="reduce_adamw_vectors",
                         compiler_params=_cparams((), 40))(allv, *ws, *ms, *vs)
    return res[0], [tuple(res[1 + 4 * k:5 + 4 * k]) for k in range(n)]


def _adamw_taps(ws, gs, ms, vs):
    n = len(ws)

    def body(*refs):
        outs = refs[4 * n:]
        for k in range(n):
            res = _adamw(refs[k][...], refs[n + k][...], refs[2 * n + k][...], refs[3 * n + k][...])
            for ref, val in zip(outs[3 * k:3 * k + 3], res):
                ref[...] = val

    out_shape = []
    for w in ws:
        out_shape += [jax.ShapeDtypeStruct(w.shape, F32)] * 3
    res = pl.pallas_call(body, out_shape=out_shape, name="adamw_taps")(*ws, *gs, *ms, *vs)
    return [tuple(res[3 * k:3 * k + 3]) for k in range(n)]


def _pack(pieces):
    flat, offs, n = [], [], 0
    for p in pieces:
        size = -(-p.size // LANES) * LANES
        flat.append(jnp.pad(p.reshape(-1), (0, size - p.size)))
        offs.append(n)
        n += size
    return jnp.concatenate(flat).reshape(1, n), offs


def _local_step(x, target, p, wfull, on_ready=lambda group: None, before_ln0=()):
    S = x.shape[0]
    dils = [d for _, d in GROUPS]

    h0, h0b, *h0_res = _ln0_fwd(x, p["ln0_g"], p["ln0_b"], before_ln0)
    h0_rows = [h0b] + [h.reshape(S, D_MODEL) for h in h0_res]

    if isinstance(wfull, dict):
        w_in3, pending = wfull["w_in"], None
    else:
        w_in3, launch_rest, assemble = wfull
        pending = launch_rest(h0b)

    w_blocks = w_in3.transpose(1, 0, 2).reshape(D_MODEL, N_BLK, GROUP_W)
    w_perm = jnp.concatenate([w_blocks[:, b] for b in PERM], axis=1)
    b_blocks = p["b_in"].reshape(N_BLK, GROUP_W)
    b_perm = jnp.concatenate([b_blocks[b] for b in PERM]).reshape(1, N_IN)
    w_nat, b_nat = w_perm[:, :N_NAT], b_perm[:, :N_NAT]
    qkv_cols = [slice(P_Q0 + g * QKV_W, P_Q0 + (g + 1) * QKV_W) for g in range(N_GROUPS)]
    w_qkv = [w_perm[:, c] for c in qkv_cols]

    proj = _mm_nn(h0b, w_nat, b_nat, tm=512, tn=N_NAT // 2, out_dtype=BF16, name="proj")
    qkv = [proj[None]]
    for g in range(1, N_GROUPS):
        t = _mm_nn(h0_rows[g], w_qkv[g], b_perm[:, qkv_cols[g]], tm=512, tn=QKV_W, out_dtype=BF16, name=f"proj_qkv{g}")
        qkv.append(t.reshape(dils[g], S // dils[g], QKV_W))
    if pending is not None:
        pending, qkv = lax.optimization_barrier((pending, qkv))
        proj = qkv[0][0]
        wfull = assemble(pending)
    w_up3 = wfull["w_up"]
    w_a, w_o, w_down, w_b = wfull["w_a"], wfull["w_o"], wfull["w_down"], wfull["w_b"]
    conv_w, ffn_conv_w = wfull["conv_w"], wfull["ffn_conv_w"]
    col0 = [P_Q0 // GROUP_W] + [0] * (N_GROUPS - 1)
    ya_in = _conv_gate_fwd(proj, conv_w)
    att = [_attn_fwd(qkv[g], col0[g], g) for g in range(N_GROUPS)]
    comb, comb_b, lse_tot = _attn_combine([a[0] for a in att], [a[1] for a in att])
    yab, mixin = _branch_mix(ya_in, comb_b, w_a, w_b, proj)
    xhat1, rstd1, h1b = _mix_ln1(mixin, w_o, p["b_o"], h0, p["ln1_g"], p["ln1_b"])
    up = _mm_nn(h1b, w_up3, p["b_up"], tm=512, tn=w_up3.shape[2], out_dtype=BF16, name="up")
    f = _ffn_conv_fwd(up, ffn_conv_w, p["ffn_conv_b"])
    dz2, dz2b, st2 = _down_ln2_loss(f, w_down, p["b_down"], xhat1, p["ln1_g"], p["ln1_b"],
                                    p["ln2_g"], p["ln2_b"], target)

    gw = {}
    gw["w_down"] = _mm_tn(f, dz2b, n_out=1, tn=D_MODEL, ts=1024, g_block=(1024, D_MODEL),
                          g_map=lambda j, s: (s, 0), name="grad_w_down").reshape(N_CHIPS, D_FF // N_CHIPS, D_MODEL)
    df = _mm_nt(dz2b, w_down, tm=512, name="df")
    dup, sm_ffn = _ffn_conv_bwd(up, df, ffn_conv_w, p["ffn_conv_b"])
    up_tn = w_up3.shape[2]
    up_pp = D_FF // up_tn
    gw["w_up"] = _mm_tn(h1b, dup, n_out=N_CHIPS, tn=up_tn, ts=1024, g_block=(None, 1024, up_tn),
                        g_map=lambda j, s: (j // up_pp, s, j % up_pp), name="grad_w_up")
    on_ready({n: gw[n] for n in ("w_down", "w_up")})
    dz1, dz1b, st1 = _up_bwd_ln1(dup, w_up3, dz2, xhat1, rstd1, p["ln1_g"])

    gw["w_o"] = _mm_tn(mixin, dz1b, n_out=1, tn=D_MODEL, ts=512, g_block=(512, D_MODEL),
                       g_map=lambda j, s: (s, 0), name="grad_w_o").reshape(N_CHIPS, D_MODEL // N_CHIPS, D_MODEL)
    dyab, dgab = _mix_bwd(dz1b, w_o, proj, yab)
    gw["w_a"] = _mm_tn(ya_in, dyab, n_out=1, tn=D_MODEL, ts=512, g_block=(512, D_MODEL),
                       g_map=lambda j, s: (s, 0), name="grad_w_a").reshape(N_CHIPS, D_CONV // N_CHIPS, D_MODEL)
    gw_b = _mm_tn(comb_b, dyab, n_out=1, tn=D_MODEL, ts=1024, g_block=(1024, D_MODEL),
                  g_map=lambda j, s: (s, 1), name="grad_w_b")
    gw["w_b"] = gw_b.reshape(GROUP_W, N_CHIPS, D_MODEL // N_CHIPS).transpose(1, 0, 2)
    on_ready({n: gw[n] for n in ("w_o", "w_a", "w_b")})
    dya_in = _mm_nt(dyab, w_a, tm=512, a_col=0, name="dya_in")
    dbch, sm_conv = _conv_gate_bwd(proj, dya_in, conv_w)
    att_stats = _comb_bwd(dyab, w_b, comb, lse_tot)
    dqkv = [_attn_bwd(qkv[g], col0[g], g, *att_stats[g]) for g in range(N_GROUPS)]

    w_pieces, b_pieces = [], []
    for nm, planes in (("bch", dbch), ("gab", dgab)):
        pw, pc = _mm_tn(h0b, planes, n_out=planes.shape[0], tn=D_MODEL, ts=1024, g_block=(None, 1024, D_MODEL),
                        g_map=lambda j, s: (j, s, 0), colsum=True, name="grad_w_in_" + nm)
        w_pieces.append(pw.transpose(1, 0, 2).reshape(D_MODEL, planes.shape[0] * D_MODEL))
        b_pieces.append(pc[0])
    for g in range(N_GROUPS):
        pw, pc = _mm_tn_cat(h0_rows[g], [a.reshape(S, GROUP_W) for a in dqkv[g]], ts=1024, name=f"grad_w_in_qkv{g}")
        w_pieces.append(pw)
        b_pieces.append(pc[0])
    dw_blocks = jnp.concatenate(w_pieces, axis=1).reshape(D_MODEL, N_BLK, GROUP_W)
    dw_ref = jnp.concatenate([dw_blocks[:, b] for b in INV_PERM], axis=1)
    gw["w_in"] = dw_ref.reshape(D_MODEL, N_CHIPS, N_IN // N_CHIPS).transpose(1, 0, 2)
    on_ready({"w_in": gw["w_in"]})
    db_blocks = jnp.concatenate(b_pieces).reshape(N_BLK, GROUP_W)
    grad_b_in = jnp.concatenate([db_blocks[b] for b in INV_PERM])

    grad_x, st0 = _in_bwd_ln0([dbch, dgab], dqkv, w_nat, w_qkv[1:], dz1, x, p["ln0_g"])

    small = {
        "loss": st2[2:3, 0:1],
        "ln0_g": st0[0], "ln0_b": st0[1], "b_in": grad_b_in, "conv_w": sm_conv[0:3],
        "b_o": st1[2], "ln1_g": st1[0], "ln1_b": st1[1],
        "b_up": jnp.concatenate([sm_ffn[0], sm_ffn[1]]), "ffn_conv_w": sm_ffn[3:6], "ffn_conv_b": sm_ffn[2],
        "b_down": st2[3], "ln2_g": st2[0], "ln2_b": st2[1],
    }
    return grad_x, gw, small


BIG = ("w_in", "w_a", "w_b", "w_o", "w_up", "w_down")
CONV = ("conv_w", "ffn_conv_w")
VECS = ("ln0_g", "ln0_b", "b_in", "b_o", "ln1_g", "ln1_b", "b_up", "ffn_conv_b", "b_down", "ln2_g", "ln2_b")
ORDER = ("ln0_g", "ln0_b", "w_in", "b_in", "conv_w", "w_a", "w_b", "w_o", "b_o", "ln1_g", "ln1_b", "w_up", "b_up",
         "ffn_conv_w", "ffn_conv_b", "w_down", "b_down", "ln2_g", "ln2_b")
SMALL_ORDER = ("loss",) + VECS + CONV


def _step(x, target, W, Mo, Vo):
    x2, t2 = x[0], target[0]
    big2 = {n: W[n][0] for n in BIG}
    halves = lambda a: a.astype(BF16).reshape(N_CORES, a.shape[0] // N_CORES, a.shape[1])
    whole = lambda g: g.reshape(N_CHIPS, g.shape[1] * g.shape[2], g.shape[3])
    later = tuple(n for n in BIG if n != "w_in")
    w_in_halves = halves(big2["w_in"])
    first = _allgather_shards([w_in_halves], [], name="allgather_w_in", collective_id=1)

    def launch_rest(h0b):
        return _allgather_shards([halves(big2[n]) for n in later] + [W[n] for n in CONV], [h0b],
                                 name="allgather_rest", collective_id=2)

    def assemble(rest):
        gathered = {n: whole(g) for n, g in zip(later + CONV, rest)}
        return {
            "w_up": gathered["w_up"],
            "w_a": gathered["w_a"].reshape(D_CONV, D_MODEL), "w_o": gathered["w_o"].reshape(D_MODEL, D_MODEL),
            "w_down": gathered["w_down"].reshape(D_FF, D_MODEL),
            "w_b": gathered["w_b"].transpose(1, 0, 2).reshape(GROUP_W, D_MODEL),
            "conv_w": gathered["conv_w"].transpose(1, 0, 2).reshape(3, D_CONV),
            "ffn_conv_w": gathered["ffn_conv_w"].transpose(1, 0, 2).reshape(3, D_FF),
        }

    pvec = {n: W[n].reshape(1, -1) for n in VECS}

    parts = {}
    exchange_ids = iter((3, 4, 5))

    def exchange(group):
        names = tuple(group)
        res = _exchange_grads([group[n] for n in names], name="exchange_" + "_".join(names),
                              collective_id=next(exchange_ids))
        parts.update(zip(names, res))

    grad_x, _, small = _local_step(x2, t2, pvec, (whole(first[0]), launch_rest, assemble), exchange,
                                   before_ln0=[w_in_halves])
    out = {}
    for n in BIG:
        tr = {"w_in": 128, "w_up": 128, "w_b": 128}.get(n, big2[n].shape[0] // 4)
        g, d, nm, nv = _reduce_adamw(parts[n], big2[n], Mo[n][0], Vo[n][0], tr=tr, name="adamw_" + n)
        out[n] = tuple(a[None] for a in (g, d, nm, nv))

    vec, offs = _pack([small[n] for n in SMALL_ORDER])
    off = dict(zip(SMALL_ORDER, offs))
    row = lambda a: a.reshape(1, -1)
    tot, vec_out = _reduce_adamw_vectors(_allgather_small(vec), [off[n] for n in VECS], [row(W[n]) for n in VECS],
                                         [row(Mo[n]) for n in VECS], [row(Vo[n]) for n in VECS])
    for n, res in zip(VECS, vec_out):
        out[n] = tuple(a.reshape(W[n].shape) for a in res)
    loss = tot[0, off["loss"]]
    chip = 2 * lax.axis_index("x") + lax.axis_index("y")
    taps_g = []
    for n in CONV:
        width = W[n].shape[2]
        full = lax.slice(tot, (0, off[n]), (1, off[n] + 3 * N_CHIPS * width)).reshape(3, N_CHIPS * width)
        taps_g.append(lax.dynamic_slice_in_dim(full, chip * width, width, axis=1))
    taps_out = _adamw_taps([W[n][0] for n in CONV], taps_g, [Mo[n][0] for n in CONV], [Vo[n][0] for n in CONV])
    for n, g, res in zip(CONV, taps_g, taps_out):
        out[n] = tuple(a[None] for a in (g,) + res)

    res = [loss, grad_x[None]]
    for k in range(4):
        res += [out[n][k] for n in ORDER]
    return tuple(res)


def kernel(x, ln0_g, ln0_b, w_in, b_in, conv_w, w_a, w_b, w_o, b_o, ln1_g, ln1_b, w_up, b_up, ffn_conv_w, ffn_conv_b, w_down, b_down, ln2_g, ln2_b, loss_target, m_ln0_g, m_ln0_b, m_w_in, m_b_in, m_conv_w, m_w_a, m_w_b, m_w_o, m_b_o, m_ln1_g, m_ln1_b, m_w_up, m_b_up, m_ffn_conv_w, m_ffn_conv_b, m_w_down, m_b_down, m_ln2_g, m_ln2_b, v_ln0_g, v_ln0_b, v_w_in, v_b_in, v_conv_w, v_w_a, v_w_b, v_w_o, v_b_o, v_ln1_g, v_ln1_b, v_w_up, v_b_up, v_ffn_conv_w, v_ffn_conv_b, v_w_down, v_b_down, v_ln2_g, v_ln2_b):
    W = dict(zip(ORDER, (ln0_g, ln0_b, w_in, b_in, conv_w, w_a, w_b, w_o, b_o, ln1_g, ln1_b, w_up, b_up,
                         ffn_conv_w, ffn_conv_b, w_down, b_down, ln2_g, ln2_b)))
    Mo = dict(zip(ORDER, (m_ln0_g, m_ln0_b, m_w_in, m_b_in, m_conv_w, m_w_a, m_w_b, m_w_o, m_b_o, m_ln1_g, m_ln1_b,
                          m_w_up, m_b_up, m_ffn_conv_w, m_ffn_conv_b, m_w_down, m_b_down, m_ln2_g, m_ln2_b)))
    Vo = dict(zip(ORDER, (v_ln0_g, v_ln0_b, v_w_in, v_b_in, v_conv_w, v_w_a, v_w_b, v_w_o, v_b_o, v_ln1_g, v_ln1_b,
                          v_w_up, v_b_up, v_ffn_conv_w, v_ffn_conv_b, v_w_down, v_b_down, v_ln2_g, v_ln2_b)))
    return _step(x, loss_target, W, Mo, Vo)
```

```python
import functools
import math

import jax
import jax.numpy as jnp
from jax import lax
from jax.experimental import pallas as pl
from jax.experimental.pallas import tpu as pltpu
from jax.experimental.pallas import tpu_sc as plsc

F32 = jnp.float32
BF16 = jnp.bfloat16

D_MODEL = 1024
D_CONV = D_MODEL
HEAD_DIM = 64
HEADS_PER_GROUP = 8
GROUPS = ((128, 1), (512, 4), (2048, 16))
N_GROUPS = len(GROUPS)
GROUP_W = HEADS_PER_GROUP * HEAD_DIM
QKV_W = N_GROUPS * GROUP_W
RADIUS = 64
D_FF = 2816
LN_EPS = 1e-5
ALPHA = 2.0 ** 0.25
MASK_VALUE = -1e30
ATT_SCALE = HEAD_DIM ** -0.5
OFF_B = 0
OFF_C = OFF_B + D_CONV
OFF_H = OFF_C + D_CONV
OFF_Q = OFF_H + D_CONV
OFF_K = OFF_Q + QKV_W
OFF_V = OFF_K + QKV_W
OFF_GA = OFF_V + QKV_W
OFF_GB = OFF_GA + D_MODEL
N_IN = OFF_GB + D_MODEL
ADAM_LR = 0.001
ADAM_B1 = 0.9
ADAM_B2 = 0.999
ADAM_EPS = 1e-08
ADAM_WD = 0.01
ADAM_STEP = 10
INV_SQRT2 = 0.7071067811865476
INV_SQRT_2PI = 0.3989422804014327

LANES = 128
SUBLANES = 8
VMEM_BYTES_V7X = 64 * 1024 * 1024
N_CHIPS = 4
N_CORES = 2
N_DEV = N_CHIPS * N_CORES
MESH = pl.DeviceIdType.MESH

N_BLK = N_IN // GROUP_W
PERM = (0, 1, 2, 3, 4, 5, 15, 16, 17, 18, 6, 9, 12, 7, 10, 13, 8, 11, 14)
INV_PERM = tuple(PERM.index(b) for b in range(N_BLK))
P_B, P_C, P_H, P_GA, P_GB, P_Q0 = 0, 1024, 2048, 3072, 4096, 5120
N_NAT = P_Q0 + QKV_W // N_GROUPS * 3
N_GATED = P_Q0

SLAB = 128
CHUNK = 256
PAD = SUBLANES
TQ = 128


def _cparams(sem, vmem_mb):
    assert vmem_mb * 1024 * 1024 < VMEM_BYTES_V7X
    return pltpu.CompilerParams(dimension_semantics=sem, vmem_limit_bytes=vmem_mb * 1024 * 1024)


def _dot(a, b):
    return jnp.dot(a, b, preferred_element_type=F32)


def _dot_nt(a, b):
    return lax.dot_general(a, b, (((1,), (1,)), ((), ())), preferred_element_type=F32)


def _dot_tn(a, b):
    return lax.dot_general(a, b, (((0,), (0,)), ((), ())), preferred_element_type=F32)


def _ln_stats(z):
    mu = jnp.mean(z, -1, keepdims=True)
    zc = z - mu
    var = jnp.mean(zc * zc, -1, keepdims=True)
    rstd = lax.rsqrt(var + LN_EPS)
    return zc * rstd, rstd


def _ln_bwd(dh, xhat, rstd, g):
    dxh = dh * g
    m1 = jnp.mean(dxh, -1, keepdims=True)
    m2 = jnp.mean(dxh * xhat, -1, keepdims=True)
    return rstd * (dxh - m1 - xhat * m2)


def _rows8(rows, width):
    pad = [jnp.zeros((1, width), F32)] * (SUBLANES - len(rows))
    return jnp.concatenate(list(rows) + pad, axis=0)


def _mm_nn(a, w, bias, *, tm, tn, out_dtype, name, vmem_mb=40):
    M, K = a.shape
    if w.ndim == 3:
        assert w.shape[2] == tn
        n_tiles = w.shape[0]
        w_spec = pl.BlockSpec((None, K, tn), lambda j, i: (j, 0, 0))
    else:
        n_tiles = w.shape[1] // tn
        w_spec = pl.BlockSpec((K, tn), lambda j, i: (0, j))

    def body(a_ref, w_ref, b_ref, o_ref):
        o_ref[...] = (_dot(a_ref[...], w_ref[...]) + b_ref[...]).astype(o_ref.dtype)

    return pl.pallas_call(
        body, grid=(n_tiles, M // tm),
        in_specs=[pl.BlockSpec((tm, K), lambda j, i: (i, 0)), w_spec, pl.BlockSpec((1, tn), lambda j, i: (0, j))],
        out_specs=pl.BlockSpec((tm, tn), lambda j, i: (i, j)),
        out_shape=jax.ShapeDtypeStruct((M, n_tiles * tn), out_dtype),
        name=name, compiler_params=_cparams(("arbitrary", "parallel"), vmem_mb))(a, w, bias)


def _mm_nt(a, w, *, tm, a_col=0, name, vmem_mb=40):
    M = a.shape[0]
    N, K = w.shape

    def body(a_ref, w_ref, o_ref):
        o_ref[...] = _dot_nt(a_ref[...], w_ref[...]).astype(o_ref.dtype)

    return pl.pallas_call(
        body, grid=(M // tm,),
        in_specs=[pl.BlockSpec((tm, K), lambda i: (i, a_col)),
                  pl.BlockSpec((N, K), lambda i: (0, 0))],
        out_specs=pl.BlockSpec((tm, N), lambda i: (i, 0)),
        out_shape=jax.ShapeDtypeStruct((M, N), BF16),
        name=name, compiler_params=_cparams(("parallel",), vmem_mb))(a, w)


def _mm_tn(a, g, *, n_out, tn, ts, g_block, g_map, colsum=False, name, vmem_mb=48):
    S, K = a.shape
    n_s = S // ts

    def body(a_ref, g_ref, *rest):
        if colsum:
            o_ref, cs_ref, acc_ref, cacc_ref = rest
        else:
            o_ref, acc_ref = rest
        s = pl.program_id(1)

        @pl.when(s == 0)
        def _():
            acc_ref[...] = jnp.zeros_like(acc_ref)
            if colsum:
                cacc_ref[...] = jnp.zeros_like(cacc_ref)

        gv = g_ref[...]
        acc_ref[...] += _dot_tn(a_ref[...], gv)
        if colsum:
            cacc_ref[...] += jnp.broadcast_to(jnp.sum(gv.astype(F32), axis=0, keepdims=True), cacc_ref.shape)

        @pl.when(s == n_s - 1)
        def _():
            o_ref[...] = acc_ref[...].astype(o_ref.dtype)
            if colsum:
                cs_ref[...] = cacc_ref[...]

    out_specs = [pl.BlockSpec((None, K, tn), lambda j, s: (j, 0, 0))]
    out_shape = [jax.ShapeDtypeStruct((n_out, K, tn), BF16)]
    scratch = [pltpu.VMEM((K, tn), F32)]
    if colsum:
        out_specs.append(pl.BlockSpec((SUBLANES, tn), lambda j, s: (0, j)))
        out_shape.append(jax.ShapeDtypeStruct((SUBLANES, n_out * tn), F32))
        scratch.append(pltpu.VMEM((SUBLANES, tn), F32))
    res = pl.pallas_call(
        body, grid=(n_out, n_s),
        in_specs=[pl.BlockSpec((ts, K), lambda j, s: (s, 0)), pl.BlockSpec(g_block, g_map)],
        out_specs=out_specs, out_shape=out_shape, scratch_shapes=scratch,
        name=name, compiler_params=_cparams(("parallel", "arbitrary"), vmem_mb))(a, g)
    return res if colsum else res[0]


def _mm_tn_cat(a, gs, *, ts, name, vmem_mb=40):
    S, K = a.shape
    widths = [g.shape[1] for g in gs]
    n_s, total = S // ts, sum(widths)

    def body(*refs):
        a_ref, g_refs = refs[0], refs[1:1 + len(gs)]
        o_ref, cs_ref, acc_ref, cacc_ref = refs[1 + len(gs):]
        s = pl.program_id(0)

        @pl.when(s == 0)
        def _():
            acc_ref[...] = jnp.zeros_like(acc_ref)
            cacc_ref[...] = jnp.zeros_like(cacc_ref)

        av, col = a_ref[...], 0
        for g_ref, w in zip(g_refs, widths):
            gv = g_ref[...]
            acc_ref[:, col:col + w] += _dot_tn(av, gv)
            cacc_ref[:, col:col + w] += jnp.broadcast_to(jnp.sum(gv.astype(F32), axis=0, keepdims=True), (SUBLANES, w))
            col += w

        @pl.when(s == n_s - 1)
        def _():
            o_ref[...] = acc_ref[...].astype(BF16)
            cs_ref[...] = cacc_ref[...]

    return pl.pallas_call(
        body, grid=(n_s,),
        in_specs=[pl.BlockSpec((ts, K), lambda s: (s, 0))] + [pl.BlockSpec((ts, w), lambda s: (s, 0)) for w in widths],
        out_specs=[pl.BlockSpec((K, total), lambda s: (0, 0)), pl.BlockSpec((SUBLANES, total), lambda s: (0, 0))],
        out_shape=[jax.ShapeDtypeStruct((K, total), BF16), jax.ShapeDtypeStruct((SUBLANES, total), F32)],
        scratch_shapes=[pltpu.VMEM((K, total), F32), pltpu.VMEM((SUBLANES, total), F32)],
        name=name, compiler_params=_cparams(("arbitrary",), vmem_mb))(a, *gs)


DILS = tuple(d for _, d in GROUPS if d > 1)


def _res_spec(d, tm, width):
    return pl.BlockSpec((d, tm // d, width), lambda i: (0, i, 0))


def _lane_scratch(tm, width):
    return [pltpu.VMEM((tm, LANES), F32)] * (width // LANES)


def _to_residue(val, dst_refs, dils, tm, dtype, scr):
    for c, ref in enumerate(scr):
        ref[...] = val[:, c * LANES:(c + 1) * LANES]
    for dst_ref, d in zip(dst_refs, dils):
        for r in range(d):
            cols = [ref[pl.ds(r, tm // d, stride=d), :] for ref in scr]
            dst_ref[r] = jnp.concatenate(cols, axis=1).astype(dtype)


def _from_residue(rows_of, d, tm, scr):
    for r in range(d):
        v = rows_of(r).astype(F32)
        for c, ref in enumerate(scr):
            ref[pl.ds(r, tm // d, stride=d), :] = v[:, c * LANES:(c + 1) * LANES]
    return jnp.concatenate([ref[...] for ref in scr], axis=1)


def _ln0_fwd(x, g, b, after=(), *, tm=512):
    S, Dm = x.shape
    n_after = len(after)

    def body(x_ref, g_ref, b_ref, *rest):
        h_ref, hb_ref, *rest = rest[n_after:]
        xhat, _ = _ln_stats(x_ref[...])
        h = xhat * g_ref[...] + b_ref[...]
        h_ref[...] = h
        hb_ref[...] = h.astype(BF16)
        _to_residue(h, rest[:len(DILS)], DILS, tm, BF16, rest[len(DILS):])

    row = pl.BlockSpec((tm, Dm), lambda i: (i, 0))
    vec = pl.BlockSpec((1, Dm), lambda i: (0, 0))
    return pl.pallas_call(
        body, grid=(S // tm,), in_specs=[row, vec, vec] + [pl.BlockSpec(memory_space=pl.ANY)] * n_after,
        out_specs=[row, row] + [_res_spec(d, tm, Dm) for d in DILS],
        out_shape=[jax.ShapeDtypeStruct((S, Dm), F32), jax.ShapeDtypeStruct((S, Dm), BF16)]
        + [jax.ShapeDtypeStruct((d, S // d, Dm), BF16) for d in DILS],
        scratch_shapes=_lane_scratch(tm, Dm),
        name="ln0_fwd", compiler_params=_cparams(("parallel",), 32))(x, g, b, *after)


def _slab_spec(S, col0):
    return pl.BlockSpec((S, SLAB), lambda j: (0, col0 // SLAB + j))


def _zero_pads(scr, S):
    scr[0:PAD, :] = jnp.zeros((PAD, SLAB), F32)
    scr[S + PAD:S + 2 * PAD, :] = jnp.zeros((PAD, SLAB), F32)


def _shifted(scr, t):
    return (scr[PAD - 1 + t:PAD - 1 + t + CHUNK, :], scr[PAD + t:PAD + t + CHUNK, :],
            scr[PAD + 1 + t:PAD + 1 + t + CHUNK, :])


def _conv_gate_fwd(proj, conv_w):
    S = proj.shape[0]

    def body(b_ref, c_ref, h_ref, w_ref, o_ref, u_scr):
        _zero_pads(u_scr, S)
        for t in range(0, S, CHUNK):
            u_scr[PAD + t:PAD + t + CHUNK, :] = c_ref[t:t + CHUNK, :].astype(F32) * h_ref[t:t + CHUNK, :].astype(F32)
        w0, w1, w2 = w_ref[0:1, :], w_ref[1:2, :], w_ref[2:3, :]
        for t in range(0, S, CHUNK):
            um, u0, up = _shifted(u_scr, t)
            cv = w0 * um + w1 * u0 + w2 * up
            o_ref[t:t + CHUNK, :] = (b_ref[t:t + CHUNK, :].astype(F32) * cv).astype(BF16)

    return pl.pallas_call(
        body, grid=(D_CONV // SLAB,),
        in_specs=[_slab_spec(S, P_B), _slab_spec(S, P_C), _slab_spec(S, P_H),
                  pl.BlockSpec((3, SLAB), lambda j: (0, j))],
        out_specs=pl.BlockSpec((S, SLAB), lambda j: (0, j)),
        out_shape=jax.ShapeDtypeStruct((S, D_CONV), BF16),
        scratch_shapes=[pltpu.VMEM((S + 2 * PAD, SLAB), F32)],
        name="conv_gate_fwd", compiler_params=_cparams(("parallel",), 40))(proj, proj, proj, conv_w)


MASKED_DISTANCE = -1e34


def _attn_bias_table(g):
    dil = GROUPS[g][1]
    j = lax.broadcasted_iota(jnp.int32, (2 * TQ, TQ), 0)
    a = lax.broadcasted_iota(jnp.int32, (2 * TQ, TQ), 1)
    rel = jnp.abs(j - RADIUS - a)
    base = -(rel * dil).astype(F32)
    inside, after_start, before_end = rel <= RADIUS, j >= RADIUS, j < TQ + RADIUS
    slopes = jnp.asarray([_slope(g, h) for h in range(HEADS_PER_GROUP)], F32)[:, None, None]
    variants = []
    for first, last in ((False, False), (True, False), (False, True), (True, True)):
        valid = inside & (after_start if first else True) & (before_end if last else True)
        variants.append(slopes * jnp.where(valid, base, MASKED_DISTANCE)[None])
    return jnp.stack(variants)


def _bias_spec(nb):
    def variant(r, i):
        return (jnp.where(i == 0, 1, 0) + jnp.where(i == nb - 1, 2, 0), 0, 0, 0)
    return pl.BlockSpec((None, HEADS_PER_GROUP, 2 * TQ, TQ), variant)


def _head_stats(rows):
    pad = jnp.zeros((LANES - len(rows), TQ), F32)
    return jnp.concatenate(list(rows) + [pad], axis=0).T


def _slope(g, h):
    return 2.0 ** (-8.0 * (g * HEADS_PER_GROUP + h + 1) / (N_GROUPS * HEADS_PER_GROUP))


def _window(p_ref, c_ref, n_ref):
    return jnp.concatenate([p_ref[TQ - RADIUS:, :], c_ref[...], n_ref[:RADIUS, :]], axis=0)


def _pair(a, h):
    return a[:, (h // 2) * LANES:(h // 2 + 1) * LANES]


def _own_lanes(a, h):
    lane = lax.broadcasted_iota(jnp.int32, a.shape, 1)
    return jnp.where((lane >= HEAD_DIM) == (h % 2 == 1), a, jnp.zeros_like(a))


def _own_rows(a, h):
    return a[(h % 2) * HEAD_DIM:(h % 2 + 1) * HEAD_DIM, :]


def _qkv_specs(nb, col0):
    def spec(col, shift):
        return pl.BlockSpec((None, TQ, GROUP_W), lambda r, i: (r, jnp.clip(i + shift, 0, nb - 1), col))

    return [spec(col0, 0), spec(col0 + 1, -1), spec(col0 + 1, 0), spec(col0 + 1, 1),
            spec(col0 + 2, -1), spec(col0 + 2, 0), spec(col0 + 2, 1)]


def _attn_fwd(qkv, col0, g):
    dil, sub, _ = qkv.shape
    nb = sub // TQ

    def body(q_ref, kp, kc, kn, vp, vc, vn, bias_ref, o_ref, lse_ref, ot_scr, s_scr, p_scr):
        kwin = _window(kp, kc, kn)
        vwin = _window(vp, vc, vn)
        q = q_ref[...] * ATT_SCALE
        for h in range(HEADS_PER_GROUP):
            s_scr[h] = _dot_nt(_pair(kwin, h), _own_lanes(_pair(q, h), h))
        lse, inv_den = [], []
        for h in range(HEADS_PER_GROUP):
            s = s_scr[h] + bias_ref[h]
            m = jnp.max(s, axis=0, keepdims=True)
            p = jnp.exp(s - m)
            den = jnp.sum(p, axis=0, keepdims=True)
            p_scr[h] = p.astype(BF16)
            inv_den.append(1.0 / den)
            lse.append(m + jnp.log(den))
        for h in range(HEADS_PER_GROUP):
            ot = _dot_tn(_pair(vwin, h), p_scr[h])
            ot_scr[h * HEAD_DIM:(h + 1) * HEAD_DIM, :] = _own_rows(ot, h) * inv_den[h]
        o_ref[...] = ot_scr[...].T
        lse_ref[...] = _head_stats(lse)

    return pl.pallas_call(
        body, grid=(dil, nb), in_specs=_qkv_specs(nb, col0) + [_bias_spec(nb)],
        out_specs=[pl.BlockSpec((None, TQ, GROUP_W), lambda r, i: (r, i, 0)),
                   pl.BlockSpec((None, TQ, LANES), lambda r, i: (r, i, 0))],
        out_shape=[jax.ShapeDtypeStruct((dil, sub, GROUP_W), F32), jax.ShapeDtypeStruct((dil, sub, LANES), F32)],
        scratch_shapes=[pltpu.VMEM((GROUP_W, TQ), F32), pltpu.VMEM((HEADS_PER_GROUP, 2 * TQ, TQ), F32),
                        pltpu.VMEM((HEADS_PER_GROUP, 2 * TQ, TQ), BF16)],
        name=f"attn_fwd_g{g}", compiler_params=_cparams(("parallel", "arbitrary"), 32))(
            *([qkv] * 7), _attn_bias_table(g))


def _expand_heads():
    h = lax.broadcasted_iota(jnp.int32, (LANES, GROUP_W), 0)
    c = lax.broadcasted_iota(jnp.int32, (LANES, GROUP_W), 1)
    return (c // HEAD_DIM == h).astype(F32)


def _dot_f32(a, b):
    return jnp.dot(a, b, preferred_element_type=F32, precision=lax.Precision.HIGHEST)


def _attn_combine(outs, lses, *, tm=512):
    S = outs[0].shape[1]
    n_col = GROUP_W // LANES

    def body(*refs):
        ins, e_ref = refs[:2 * N_GROUPS], refs[2 * N_GROUPS]
        c_ref, cb_ref, lt_ref = refs[2 * N_GROUPS + 1:2 * N_GROUPS + 4]
        scr = refs[2 * N_GROUPS + 4:]
        o, l = [ins[0][0]], [ins[N_GROUPS][0]]
        for k, d in enumerate(DILS):
            o_ref, l_ref = ins[1 + k], ins[N_GROUPS + 1 + k]
            o.append(_from_residue(lambda r: o_ref[r], d, tm, scr[k * (n_col + 1):k * (n_col + 1) + n_col]))
            l.append(_from_residue(lambda r: l_ref[r], d, tm, scr[k * (n_col + 1) + n_col:(k + 1) * (n_col + 1)]))
        m = jnp.maximum(jnp.maximum(l[0], l[1]), l[2])
        e = [jnp.exp(v - m) for v in l]
        den = e[0] + e[1] + e[2]
        comb = sum(_dot_f32(ev / den, e_ref[...]) * ov for ev, ov in zip(e, o))
        c_ref[...] = comb
        cb_ref[...] = comb.astype(BF16)
        lt_ref[...] = m + jnp.log(den)

    row = pl.BlockSpec((tm, GROUP_W), lambda i: (i, 0))
    dils = [d for _, d in GROUPS]
    return pl.pallas_call(
        body, grid=(S // tm,),
        in_specs=[_res_spec(d, tm, GROUP_W) for d in dils] + [_res_spec(d, tm, LANES) for d in dils]
        + [_resident((LANES, GROUP_W))],
        out_specs=[row, row, pl.BlockSpec((tm, LANES), lambda i: (i, 0))],
        out_shape=[jax.ShapeDtypeStruct((S, GROUP_W), F32), jax.ShapeDtypeStruct((S, GROUP_W), BF16),
                   jax.ShapeDtypeStruct((S, LANES), F32)],
        scratch_shapes=_lane_scratch(tm, GROUP_W + LANES) * len(DILS),
        name="attn_combine", compiler_params=_cparams(("parallel",), 32))(*outs, *lses, _expand_heads())


def _branch_mix(ya_in, comb_b, w_a, w_b, proj, *, tm=512):
    S = ya_in.shape[0]

    def body(ya_ref, cb_ref, wa_ref, wb_ref, ga_ref, gb_ref, yab_ref, mx_ref):
        y_a = _dot(ya_ref[...], wa_ref[...])
        y_b = _dot(cb_ref[...], wb_ref[...])
        yab_ref[:, 0:D_MODEL] = y_a.astype(BF16)
        yab_ref[:, D_MODEL:2 * D_MODEL] = y_b.astype(BF16)
        mx = jax.nn.sigmoid(ga_ref[...].astype(F32)) * y_a + jax.nn.sigmoid(gb_ref[...].astype(F32)) * y_b
        mx_ref[...] = mx.astype(BF16)

    return pl.pallas_call(
        body, grid=(S // tm,),
        in_specs=[pl.BlockSpec((tm, D_CONV), lambda i: (i, 0)), pl.BlockSpec((tm, GROUP_W), lambda i: (i, 0)),
                  pl.BlockSpec((D_CONV, D_MODEL), lambda i: (0, 0)), pl.BlockSpec((GROUP_W, D_MODEL), lambda i: (0, 0)),
                  pl.BlockSpec((tm, D_MODEL), lambda i: (i, P_GA // D_MODEL)),
                  pl.BlockSpec((tm, D_MODEL), lambda i: (i, P_GB // D_MODEL))],
        out_specs=[pl.BlockSpec((tm, 2 * D_MODEL), lambda i: (i, 0)), pl.BlockSpec((tm, D_MODEL), lambda i: (i, 0))],
        out_shape=[jax.ShapeDtypeStruct((S, 2 * D_MODEL), BF16), jax.ShapeDtypeStruct((S, D_MODEL), BF16)],
        name="branch_mix", compiler_params=_cparams(("parallel",), 40))(ya_in, comb_b, w_a, w_b, proj, proj)


def _mix_ln1(mixin, w_o, b_o, h0, g1, b1, *, tm=512):
    S = mixin.shape[0]

    def body(mx_ref, wo_ref, bo_ref, h0_ref, g_ref, b_ref, xh_ref, rs_ref, h1b_ref):
        z = ALPHA * h0_ref[...] + _dot(mx_ref[...], wo_ref[...]) + bo_ref[...]
        xhat, rstd = _ln_stats(z)
        xh_ref[...] = xhat
        rs_ref[...] = jnp.broadcast_to(rstd, (tm, LANES))
        h1b_ref[...] = (xhat * g_ref[...] + b_ref[...]).astype(BF16)

    row = pl.BlockSpec((tm, D_MODEL), lambda i: (i, 0))
    vec = pl.BlockSpec((1, D_MODEL), lambda i: (0, 0))
    return pl.pallas_call(
        body, grid=(S // tm,),
        in_specs=[row, pl.BlockSpec((D_MODEL, D_MODEL), lambda i: (0, 0)), vec, row, vec, vec],
        out_specs=[row, pl.BlockSpec((tm, LANES), lambda i: (i, 0)), row],
        out_shape=[jax.ShapeDtypeStruct((S, D_MODEL), F32), jax.ShapeDtypeStruct((S, LANES), F32),
                   jax.ShapeDtypeStruct((S, D_MODEL), BF16)],
        name="mix_ln1", compiler_params=_cparams(("parallel",), 40))(mixin, w_o, b_o, h0, g1, b1)


def _gelu_parts(cz):
    cdf = 0.5 * (1.0 + lax.erf(cz * INV_SQRT2))
    return cdf, cz * cdf


def _ffn_conv_fwd(up, cw, cb):
    S = up.shape[0]

    def body(a_ref, g_ref, w_ref, cb_ref, o_ref, a_scr):
        _zero_pads(a_scr, S)
        for t in range(0, S, CHUNK):
            a_scr[PAD + t:PAD + t + CHUNK, :] = a_ref[t:t + CHUNK, :].astype(F32)
        w0, w1, w2 = w_ref[0:1, :], w_ref[1:2, :], w_ref[2:3, :]
        for t in range(0, S, CHUNK):
            am, a0, ap = _shifted(a_scr, t)
            _, gel = _gelu_parts(w0 * am + w1 * a0 + w2 * ap + cb_ref[...])
            o_ref[t:t + CHUNK, :] = (gel * g_ref[t:t + CHUNK, :].astype(F32)).astype(BF16)

    return pl.pallas_call(
        body, grid=(D_FF // SLAB,),
        in_specs=[_slab_spec(S, 0), _slab_spec(S, D_FF), pl.BlockSpec((3, SLAB), lambda j: (0, j)),
                  pl.BlockSpec((1, SLAB), lambda j: (0, j))],
        out_specs=pl.BlockSpec((S, SLAB), lambda j: (0, j)),
        out_shape=jax.ShapeDtypeStruct((S, D_FF), BF16),
        scratch_shapes=[pltpu.VMEM((S + 2 * PAD, SLAB), F32)],
        name="ffn_conv_fwd", compiler_params=_cparams(("parallel",), 40))(up, up, cw, cb)


def _down_ln2_loss(f, w_down, b_down, xhat1, g1, b1, g2, b2, target, *, tm=256):
    S = f.shape[0]

    def body(f_ref, wd_ref, bd_ref, xh1_ref, g1_ref, b1_ref, g2_ref, b2_ref, t_ref, dz_ref, dzb_ref, st_ref):
        h1 = xh1_ref[...] * g1_ref[...] + b1_ref[...]
        z = ALPHA * h1 + _dot(f_ref[...], wd_ref[...]) + bd_ref[...]
        xhat, rstd = _ln_stats(z)
        err = xhat * g2_ref[...] + b2_ref[...] - t_ref[...]
        loss = (0.5 / D_MODEL) * jnp.sum(jnp.sum(err * err, axis=1, keepdims=True), axis=0, keepdims=True)
        dh2 = err * (1.0 / D_MODEL)
        dz = _ln_bwd(dh2, xhat, rstd, g2_ref[...])
        dz_ref[...] = dz
        dzb_ref[...] = dz.astype(BF16)
        upd = _rows8([jnp.sum(dh2 * xhat, axis=0, keepdims=True), jnp.sum(dh2, axis=0, keepdims=True),
                      jnp.broadcast_to(loss, (1, D_MODEL)), jnp.sum(dz, axis=0, keepdims=True)], D_MODEL)

        @pl.when(pl.program_id(0) == 0)
        def _():
            st_ref[...] = upd

        @pl.when(pl.program_id(0) != 0)
        def _():
            st_ref[...] += upd

    row = pl.BlockSpec((tm, D_MODEL), lambda i: (i, 0))
    vec = pl.BlockSpec((1, D_MODEL), lambda i: (0, 0))
    return pl.pallas_call(
        body, grid=(S // tm,),
        in_specs=[pl.BlockSpec((tm, D_FF), lambda i: (i, 0)), _resident((D_FF, D_MODEL)),
                  vec, row, vec, vec, vec, vec, row],
        out_specs=[row, row, pl.BlockSpec((SUBLANES, D_MODEL), lambda i: (0, 0))],
        out_shape=[jax.ShapeDtypeStruct((S, D_MODEL), F32), jax.ShapeDtypeStruct((S, D_MODEL), BF16),
                   jax.ShapeDtypeStruct((SUBLANES, D_MODEL), F32)],
        name="down_ln2_loss", compiler_params=_cparams(("arbitrary",), 48))(
            f, w_down, b_down, xhat1, g1, b1, g2, b2, target)


def _ffn_conv_bwd(up, df, cw, cb):
    S = up.shape[0]

    def body(a_ref, g_ref, df_ref, w_ref, cb_ref, dup_ref, sm_ref, a_scr, d_scr):
        _zero_pads(a_scr, S)
        _zero_pads(d_scr, S)
        for t in range(0, S, CHUNK):
            a_scr[PAD + t:PAD + t + CHUNK, :] = a_ref[t:t + CHUNK, :].astype(F32)
        w0, w1, w2 = w_ref[0:1, :], w_ref[1:2, :], w_ref[2:3, :]
        zero = jnp.zeros((1, SLAB), F32)
        s_dg, s_dcz, s_w0, s_w1, s_w2 = zero, zero, zero, zero, zero
        for t in range(0, S, CHUNK):
            am, a0, ap = _shifted(a_scr, t)
            cz = w0 * am + w1 * a0 + w2 * ap + cb_ref[...]
            cdf, gel = _gelu_parts(cz)
            dfv = df_ref[t:t + CHUNK, :].astype(F32)
            dgte = dfv * gel
            dcz = dfv * g_ref[t:t + CHUNK, :].astype(F32) * (cdf + cz * jnp.exp(-0.5 * cz * cz) * INV_SQRT_2PI)
            dup_ref[1, t:t + CHUNK, :] = dgte.astype(BF16)
            d_scr[PAD + t:PAD + t + CHUNK, :] = dcz
            s_dg = s_dg + jnp.sum(dgte, axis=0, keepdims=True)
            s_dcz = s_dcz + jnp.sum(dcz, axis=0, keepdims=True)
            s_w0 = s_w0 + jnp.sum(dcz * am, axis=0, keepdims=True)
            s_w1 = s_w1 + jnp.sum(dcz * a0, axis=0, keepdims=True)
            s_w2 = s_w2 + jnp.sum(dcz * ap, axis=0, keepdims=True)
        s_da = zero
        for t in range(0, S, CHUNK):
            dm, d0, dp = _shifted(d_scr, t)
            da = w0 * dp + w1 * d0 + w2 * dm
            dup_ref[0, t:t + CHUNK, :] = da.astype(BF16)
            s_da = s_da + jnp.sum(da, axis=0, keepdims=True)
        sm_ref[...] = _rows8([s_da, s_dg, s_dcz, s_w0, s_w1, s_w2], SLAB)

    return pl.pallas_call(
        body, grid=(D_FF // SLAB,),
        in_specs=[_slab_spec(S, 0), _slab_spec(S, D_FF), pl.BlockSpec((S, SLAB), lambda j: (0, j)),
                  pl.BlockSpec((3, SLAB), lambda j: (0, j)), pl.BlockSpec((1, SLAB), lambda j: (0, j))],
        out_specs=[pl.BlockSpec((2, S, SLAB), lambda j: (0, 0, j)), pl.BlockSpec((SUBLANES, SLAB), lambda j: (0, j))],
        out_shape=[jax.ShapeDtypeStruct((2, S, D_FF), BF16), jax.ShapeDtypeStruct((SUBLANES, D_FF), F32)],
        scratch_shapes=[pltpu.VMEM((S + 2 * PAD, SLAB), F32)] * 2,
        name="ffn_conv_bwd", compiler_params=_cparams(("parallel",), 48))(up, up, df, cw, cb)


def _resident(shape):
    nd = len(shape)
    return pl.BlockSpec(shape, lambda *_: (0,) * nd, pipeline_mode=pl.Buffered(1))


def _up_bwd_ln1(dup, w_up3, dz2, xhat1, rstd1, g1, *, tm=256):
    S = dz2.shape[0]
    ns, _, tk = w_up3.shape
    per_plane = D_FF // tk

    def body(du_ref, w_ref, dz2_ref, xh_ref, rs_ref, g_ref, dz_ref, dzb_ref, st_ref):
        dh = ALPHA * dz2_ref[...]
        for k in range(ns):
            col = (k % per_plane) * tk
            dh = dh + _dot_nt(du_ref[k // per_plane, :, col:col + tk], w_ref[k])
        xhat = xh_ref[...]
        dz = _ln_bwd(dh, xhat, rs_ref[:, 0:1], g_ref[...])
        dz_ref[...] = dz
        dzb_ref[...] = dz.astype(BF16)
        upd = _rows8([jnp.sum(dh * xhat, axis=0, keepdims=True), jnp.sum(dh, axis=0, keepdims=True),
                      jnp.sum(dz, axis=0, keepdims=True)], D_MODEL)

        @pl.when(pl.program_id(0) == 0)
        def _():
            st_ref[...] = upd

        @pl.when(pl.program_id(0) != 0)
        def _():
            st_ref[...] += upd

    row = pl.BlockSpec((tm, D_MODEL), lambda i: (i, 0))
    return pl.pallas_call(
        body, grid=(S // tm,),
        in_specs=[pl.BlockSpec((dup.shape[0], tm, D_FF), lambda i: (0, i, 0)), _resident(w_up3.shape),
                  row, row, pl.BlockSpec((tm, LANES), lambda i: (i, 0)), pl.BlockSpec((1, D_MODEL), lambda i: (0, 0))],
        out_specs=[row, row, pl.BlockSpec((SUBLANES, D_MODEL), lambda i: (0, 0))],
        out_shape=[jax.ShapeDtypeStruct((S, D_MODEL), F32), jax.ShapeDtypeStruct((S, D_MODEL), BF16),
                   jax.ShapeDtypeStruct((SUBLANES, D_MODEL), F32)],
        name="up_bwd_ln1", compiler_params=_cparams(("arbitrary",), 48))(dup, w_up3, dz2, xhat1, rstd1, g1)


def _mix_bwd(dz1b, w_o, proj, yab, *, tm=512):
    S = dz1b.shape[0]

    def body(dz_ref, wo_ref, ga_ref, gb_ref, y_ref, dy_ref, dg_ref):
        dmx = _dot_nt(dz_ref[...], wo_ref[...])
        for k, gt_ref in enumerate((ga_ref, gb_ref)):
            sl = slice(k * D_MODEL, (k + 1) * D_MODEL)
            sg = jax.nn.sigmoid(gt_ref[...].astype(F32))
            dy_ref[:, sl] = (dmx * sg).astype(BF16)
            dg_ref[k] = (dmx * y_ref[:, sl].astype(F32) * sg * (1.0 - sg)).astype(BF16)

    row = pl.BlockSpec((tm, D_MODEL), lambda i: (i, 0))
    wide = pl.BlockSpec((tm, 2 * D_MODEL), lambda i: (i, 0))
    return pl.pallas_call(
        body, grid=(S // tm,),
        in_specs=[row, _resident(w_o.shape), pl.BlockSpec((tm, D_MODEL), lambda i: (i, P_GA // D_MODEL)),
                  pl.BlockSpec((tm, D_MODEL), lambda i: (i, P_GB // D_MODEL)), wide],
        out_specs=[wide, pl.BlockSpec((2, tm, D_MODEL), lambda i: (0, i, 0))],
        out_shape=[jax.ShapeDtypeStruct((S, 2 * D_MODEL), BF16), jax.ShapeDtypeStruct((2, S, D_MODEL), BF16)],
        name="mix_bwd", compiler_params=_cparams(("parallel",), 40))(dz1b, w_o, proj, proj, yab)


def _conv_gate_bwd(proj, dya_in, conv_w):
    S = proj.shape[0]

    def body(b_ref, c_ref, h_ref, dy_ref, w_ref, o_ref, sm_ref, u_scr, d_scr):
        _zero_pads(u_scr, S)
        _zero_pads(d_scr, S)
        for t in range(0, S, CHUNK):
            u_scr[PAD + t:PAD + t + CHUNK, :] = c_ref[t:t + CHUNK, :].astype(F32) * h_ref[t:t + CHUNK, :].astype(F32)
        w0, w1, w2 = w_ref[0:1, :], w_ref[1:2, :], w_ref[2:3, :]
        zero = jnp.zeros((1, SLAB), F32)
        s_w0, s_w1, s_w2 = zero, zero, zero
        for t in range(0, S, CHUNK):
            um, u0, up = _shifted(u_scr, t)
            dy = dy_ref[t:t + CHUNK, :].astype(F32)
            o_ref[0, t:t + CHUNK, :] = (dy * (w0 * um + w1 * u0 + w2 * up)).astype(BF16)
            dcv = dy * b_ref[t:t + CHUNK, :].astype(F32)
            d_scr[PAD + t:PAD + t + CHUNK, :] = dcv
            s_w0 = s_w0 + jnp.sum(dcv * um, axis=0, keepdims=True)
            s_w1 = s_w1 + jnp.sum(dcv * u0, axis=0, keepdims=True)
            s_w2 = s_w2 + jnp.sum(dcv * up, axis=0, keepdims=True)
        for t in range(0, S, CHUNK):
            dm, d0, dp = _shifted(d_scr, t)
            du = w0 * dp + w1 * d0 + w2 * dm
            o_ref[1, t:t + CHUNK, :] = (du * h_ref[t:t + CHUNK, :].astype(F32)).astype(BF16)
            o_ref[2, t:t + CHUNK, :] = (du * c_ref[t:t + CHUNK, :].astype(F32)).astype(BF16)
        sm_ref[...] = _rows8([s_w0, s_w1, s_w2], SLAB)

    return pl.pallas_call(
        body, grid=(D_CONV // SLAB,),
        in_specs=[_slab_spec(S, P_B), _slab_spec(S, P_C), _slab_spec(S, P_H),
                  pl.BlockSpec((S, SLAB), lambda j: (0, j)), pl.BlockSpec((3, SLAB), lambda j: (0, j))],
        out_specs=[pl.BlockSpec((3, S, SLAB), lambda j: (0, 0, j)), pl.BlockSpec((SUBLANES, SLAB), lambda j: (0, j))],
        out_shape=[jax.ShapeDtypeStruct((3, S, D_CONV), BF16), jax.ShapeDtypeStruct((SUBLANES, D_CONV), F32)],
        scratch_shapes=[pltpu.VMEM((S + 2 * PAD, SLAB), F32)] * 2,
        name="conv_gate_bwd", compiler_params=_cparams(("parallel",), 48))(proj, proj, proj, dya_in, conv_w)


def _comb_bwd(dyab, w_b, comb, lse_tot, *, tm=512):
    S = comb.shape[0]
    widths, dtypes = (GROUP_W, LANES, LANES), (BF16, F32, F32)

    def body(dy_ref, wb_ref, c_ref, lt_ref, e_ref, *rest):
        outs, scr = rest[:3 * N_GROUPS], rest[3 * N_GROUPS:]
        dcb = _dot_nt(dy_ref[...], wb_ref[...]).astype(BF16)
        dc = dcb.astype(F32)
        delta = lax.dot_general(dc * c_ref[...], e_ref[...], (((1,), (1,)), ((), ())),
                                preferred_element_type=F32, precision=lax.Precision.HIGHEST)
        for k, (val, dtype) in enumerate(zip((dc, lt_ref[...], delta), dtypes)):
            outs[k][0] = val.astype(dtype)
            _to_residue(val, [outs[3 * (1 + j) + k] for j in range(len(DILS))], DILS, tm, dtype,
                        scr[:val.shape[1] // LANES])

    out_specs, out_shape = [], []
    for _, d in GROUPS:
        out_specs += [_res_spec(d, tm, w) for w in widths]
        out_shape += [jax.ShapeDtypeStruct((d, S // d, w), t) for w, t in zip(widths, dtypes)]
    res = pl.pallas_call(
        body, grid=(S // tm,),
        in_specs=[pl.BlockSpec((tm, D_MODEL), lambda i: (i, 1)), _resident(w_b.shape),
                  pl.BlockSpec((tm, GROUP_W), lambda i: (i, 0)), pl.BlockSpec((tm, LANES), lambda i: (i, 0)),
                  _resident((LANES, GROUP_W))],
        out_specs=out_specs, out_shape=out_shape, scratch_shapes=_lane_scratch(tm, GROUP_W),
        name="comb_bwd", compiler_params=_cparams(("parallel",), 32))(dyab, w_b, comb, lse_tot, _expand_heads())
    return [tuple(res[3 * g:3 * g + 3]) for g in range(N_GROUPS)]


def _attn_bwd(qkv, col0, g, dcomb, lse_tot, delta):
    dil, sub, _ = qkv.shape
    nb = sub // TQ

    def body(q_ref, kp, kc, kn, vp, vc, vn, do_ref, lse_ref, dl_ref, bias_ref, dq_ref, dk_ref, dv_ref,
             ak, av, dqt_scr, s_scr, dp_scr, ds_scr, p_scr):
        i = pl.program_id(1)

        @pl.when(i == 0)
        def _():
            ak[...] = jnp.zeros_like(ak)
            av[...] = jnp.zeros_like(av)

        @pl.when(i < nb)
        def _():
            kwin = _window(kp, kc, kn)
            vwin = _window(vp, vc, vn)
            q = q_ref[...] * ATT_SCALE
            do = do_ref[...]
            lse_t, dl_t = lse_ref[...].T, dl_ref[...].T
            for h in range(HEADS_PER_GROUP):
                s_scr[h] = _dot_nt(_pair(kwin, h), _own_lanes(_pair(q, h), h))
                dp_scr[h] = _dot_nt(_pair(vwin, h), _own_lanes(_pair(do, h), h))
            for h in range(HEADS_PER_GROUP):
                p = jnp.exp(s_scr[h] + bias_ref[h] - lse_t[h:h + 1, :])
                ds_scr[h] = (p * (dp_scr[h] - dl_t[h:h + 1, :])).astype(BF16)
                p_scr[h] = p.astype(BF16)
            for h in range(HEADS_PER_GROUP):
                dqt_scr[h * HEAD_DIM:(h + 1) * HEAD_DIM, :] = _own_rows(_dot_tn(_pair(kwin, h), ds_scr[h]), h)
            for h in range(0, HEADS_PER_GROUP, 2):
                cols = slice(h * HEAD_DIM, (h + 2) * HEAD_DIM)
                q2 = jnp.concatenate([_own_lanes(_pair(q, h), h), _own_lanes(_pair(q, h), h + 1)], axis=0)
                do2 = jnp.concatenate([_own_lanes(_pair(do, h), h), _own_lanes(_pair(do, h), h + 1)], axis=0)
                ak[RADIUS:RADIUS + 2 * TQ, cols] += _dot(jnp.concatenate([ds_scr[h], ds_scr[h + 1]], axis=1), q2)
                av[RADIUS:RADIUS + 2 * TQ, cols] += _dot(jnp.concatenate([p_scr[h], p_scr[h + 1]], axis=1), do2)
            dq_ref[...] = (dqt_scr[...].T * ATT_SCALE).astype(BF16)

        dk_ref[...] = ak[0:TQ, :].astype(BF16)
        dv_ref[...] = av[0:TQ, :].astype(BF16)
        ak[0:2 * TQ, :] = ak[TQ:3 * TQ, :]
        av[0:2 * TQ, :] = av[TQ:3 * TQ, :]
        ak[2 * TQ:3 * TQ, :] = jnp.zeros((TQ, GROUP_W), F32)
        av[2 * TQ:3 * TQ, :] = jnp.zeros((TQ, GROUP_W), F32)

    tok = pl.BlockSpec((None, TQ, GROUP_W), lambda r, i: (r, jnp.minimum(i, nb - 1), 0))
    stat = pl.BlockSpec((None, TQ, LANES), lambda r, i: (r, jnp.minimum(i, nb - 1), 0))
    dkv_spec = pl.BlockSpec((None, TQ, GROUP_W), lambda r, i: (r, jnp.maximum(i - 1, 0), 0))
    return pl.pallas_call(
        body, grid=(dil, nb + 1), in_specs=_qkv_specs(nb, col0) + [tok, stat, stat, _bias_spec(nb)],
        out_specs=[tok, dkv_spec, dkv_spec], out_shape=[jax.ShapeDtypeStruct((dil, sub, GROUP_W), BF16)] * 3,
        scratch_shapes=[pltpu.VMEM((3 * TQ, GROUP_W), F32)] * 2 + [pltpu.VMEM((GROUP_W, TQ), F32)]
        + [pltpu.VMEM((HEADS_PER_GROUP, 2 * TQ, TQ), F32)] * 2 + [pltpu.VMEM((HEADS_PER_GROUP, 2 * TQ, TQ), BF16)] * 2,
        name=f"attn_bwd_g{g}", compiler_params=_cparams(("arbitrary", "arbitrary"), 32))(
            *([qkv] * 7), dcomb, lse_tot, delta, _attn_bias_table(g))


def _in_bwd_ln0(dgated, dqkv, w_nat, w_dil, dz1, x, g0, *, tm=256):
    S = x.shape[0]
    n_gated, n_in = len(dgated), 3 * N_GROUPS

    def body(*refs):
        g_refs, d_refs = refs[:n_gated], refs[n_gated:n_gated + n_in]
        wn_ref, *wd_refs = refs[n_gated + n_in:n_gated + n_in + N_GROUPS]
        dz_ref, x_ref, g_ref, gx_ref, st_ref, *tmp_ref = refs[n_gated + n_in + N_GROUPS:]
        dh = ALPHA * dz_ref[...]
        col = 0
        for ref in g_refs:
            for k in range(ref.shape[0]):
                dh = dh + _dot_nt(ref[k], wn_ref[:, col:col + D_MODEL])
                col += D_MODEL
        for g, (_, d) in enumerate(GROUPS):
            rows = [jnp.concatenate([d_refs[3 * g + k][r] for k in range(3)], axis=1) for r in range(d)]
            w = wn_ref[:, col:col + QKV_W] if d == 1 else wd_refs[g - 1][...]
            res = _dot_nt(jnp.concatenate(rows, axis=0), w)
            if d == 1:
                dh = dh + res
            else:
                n = tm // d
                dh = dh + _from_residue(lambda r: res[r * n:(r + 1) * n, :], d, tm, tmp_ref)
        xhat, rstd = _ln_stats(x_ref[...])
        gx_ref[...] = _ln_bwd(dh, xhat, rstd, g_ref[...])
        upd = _rows8([jnp.sum(dh * xhat, axis=0, keepdims=True), jnp.sum(dh, axis=0, keepdims=True)], D_MODEL)

        @pl.when(pl.program_id(0) == 0)
        def _():
            st_ref[...] = upd

        @pl.when(pl.program_id(0) != 0)
        def _():
            st_ref[...] += upd

    row = pl.BlockSpec((tm, D_MODEL), lambda i: (i, 0))
    g_specs = [pl.BlockSpec((a.shape[0], tm, D_MODEL), lambda i: (0, i, 0)) for a in dgated]
    d_specs = []
    for _, d in GROUPS:
        d_specs += [_res_spec(d, tm, GROUP_W)] * 3
    operands = list(dgated) + [a for grp in dqkv for a in grp] + [w_nat] + list(w_dil) + [dz1, x, g0]
    return pl.pallas_call(
        body, grid=(S // tm,),
        in_specs=g_specs + d_specs + [_resident(w_nat.shape)] + [_resident(w.shape) for w in w_dil]
        + [row, row, pl.BlockSpec((1, D_MODEL), lambda i: (0, 0))],
        out_specs=[row, pl.BlockSpec((SUBLANES, D_MODEL), lambda i: (0, 0))],
        out_shape=[jax.ShapeDtypeStruct((S, D_MODEL), F32), jax.ShapeDtypeStruct((SUBLANES, D_MODEL), F32)],
        scratch_shapes=_lane_scratch(tm, D_MODEL),
        name="in_bwd_ln0", compiler_params=_cparams(("arbitrary",), 52))(*operands)


HBM_SPEC = pl.BlockSpec(memory_space=pltpu.HBM)


def _place():
    x, y, c = lax.axis_index("x"), lax.axis_index("y"), lax.axis_index("c")
    chips = [(1 - x, y), (x, 1 - y), (1 - x, 1 - y)]
    return x, y, c, chips


def _allgather_shards(shards, after, *, name, collective_id):
    n = len(shards)
    per = 6

    def body(*refs):
        ins, outs = refs[:n], refs[n + len(after):2 * n + len(after)]
        send_sems, recv_sems, loc_sems = refs[2 * n + len(after):]
        x, y, c, chips = _place()
        me = 2 * x + y
        sib = (x, y, 1 - c)
        peers = [sib] + [(px, py, c) for px, py in chips]
        barrier = pltpu.get_barrier_semaphore()
        for peer in peers:
            pl.semaphore_signal(barrier, inc=1, device_id=peer, device_id_type=MESH)
        pl.semaphore_wait(barrier, len(peers))

        def rcopy(w, k, src, dst, to):
            return pltpu.make_async_remote_copy(src_ref=src, dst_ref=dst, send_sem=send_sems.at[per * w + k],
                                                recv_sem=recv_sems.at[per * w + k], device_id=to, device_id_type=MESH)

        split = [s.shape[0] == N_CORES for s in shards]
        half = lambda w: c if split[w] else 0
        local, sends = [], []
        for w in range(n):
            cp = pltpu.make_async_copy(ins[w], outs[w].at[me], loc_sems.at[w])
            cp.start()
            local.append(cp)
            for j, (px, py) in enumerate(chips):
                cp = rcopy(w, j, ins[w].at[half(w)], outs[w].at[me, half(w)], (px, py, c))
                cp.start()
                sends.append(cp)
        for w in range(n):
            for j, (px, py) in enumerate(chips):
                slot = outs[w].at[2 * px + py, half(w)]
                rcopy(w, j, slot, slot, (px, py, c)).wait_recv()
                if split[w]:
                    cp = rcopy(w, 3 + j, slot, slot, sib)
                    cp.start()
                    sends.append(cp)
        for w in range(n):
            if split[w]:
                for j, (px, py) in enumerate(chips):
                    slot = outs[w].at[2 * px + py, 1 - c]
                    rcopy(w, 3 + j, slot, slot, sib).wait_recv()
        for cp in sends:
            cp.wait_send()
        for cp in local:
            cp.wait()

    return pl.kernel(
        body, out_type=[jax.ShapeDtypeStruct((N_CHIPS,) + s.shape, s.dtype) for s in shards],
        mesh=plsc.ScalarSubcoreMesh(axis_name="sequencer", num_cores=1),
        scratch_types=[pltpu.SemaphoreType.DMA((per * n,)), pltpu.SemaphoreType.DMA((per * n,)),
                       pltpu.SemaphoreType.DMA((n,))],
        name=name, compiler_params=pltpu.CompilerParams(collective_id=collective_id))(*shards, *after)


def _exchange_grads(grads, *, name, collective_id):
    n = len(grads)
    per = 7

    def body(*refs):
        ins, outs = refs[:n], refs[n:2 * n]
        send_sems, recv_sems, loc_sems = refs[2 * n:]
        x, y, c, chips = _place()
        me = 2 * x + y
        sib = (x, y, 1 - c)
        peers = [sib] + [(px, py, c) for px, py in chips]
        barrier = pltpu.get_barrier_semaphore()
        for peer in peers:
            pl.semaphore_signal(barrier, inc=1, device_id=peer, device_id_type=MESH)
        pl.semaphore_wait(barrier, len(peers))

        def rcopy(w, k, src, dst, to):
            return pltpu.make_async_remote_copy(src_ref=src, dst_ref=dst, send_sem=send_sems.at[per * w + k],
                                                recv_sem=recv_sems.at[per * w + k], device_id=to, device_id_type=MESH)

        local, sends = [], []
        for w in range(n):
            cp = pltpu.make_async_copy(ins[w].at[me], outs[w].at[c, me], loc_sems.at[w])
            cp.start()
            local.append(cp)
            cp = rcopy(w, 0, ins[w].at[me], outs[w].at[c, me], sib)
            cp.start()
            sends.append(cp)
            for j, (px, py) in enumerate(chips):
                cp = rcopy(w, 1 + j, ins[w].at[2 * px + py], outs[w].at[c, me], (px, py, c))
                cp.start()
                sends.append(cp)
        for w in range(n):
            for j, (px, py) in enumerate(chips):
                slot = outs[w].at[c, 2 * px + py]
                rcopy(w, 1 + j, slot, slot, (px, py, c)).wait_recv()
                cp = rcopy(w, 4 + j, slot, slot, sib)
                cp.start()
                sends.append(cp)
        for w in range(n):
            slot = outs[w].at[1 - c, me]
            rcopy(w, 0, slot, slot, sib).wait_recv()
            for j, (px, py) in enumerate(chips):
                slot = outs[w].at[1 - c, 2 * px + py]
                rcopy(w, 4 + j, slot, slot, sib).wait_recv()
        for cp in sends:
            cp.wait_send()
        for cp in local:
            cp.wait()

    return pl.kernel(
        body, out_type=[jax.ShapeDtypeStruct((N_CORES,) + g.shape, g.dtype) for g in grads],
        mesh=plsc.ScalarSubcoreMesh(axis_name="sequencer", num_cores=1),
        scratch_types=[pltpu.SemaphoreType.DMA((per * n,)), pltpu.SemaphoreType.DMA((per * n,)),
                       pltpu.SemaphoreType.DMA((n,))],
        name=name, compiler_params=pltpu.CompilerParams(collective_id=collective_id))(*grads)


def _allgather_small(vec):
    def body(v_ref, o_ref, send_sems, recv_sems, loc_sem):
        x, y, c = lax.axis_index("x"), lax.axis_index("y"), lax.axis_index("c")
        me = 4 * x + 2 * y + c

        def peer(k):
            flip = lambda v, bit: 1 - v if (k >> bit) & 1 else v
            return flip(x, 2), flip(y, 1), flip(c, 0)

        loc = pltpu.make_async_copy(v_ref, o_ref.at[me], loc_sem)
        loc.start()
        sends = []
        for k in range(1, N_DEV):
            cp = pltpu.make_async_remote_copy(src_ref=v_ref, dst_ref=o_ref.at[me], send_sem=send_sems.at[k - 1],
                                              recv_sem=recv_sems.at[k - 1], device_id=peer(k), device_id_type=MESH)
            cp.start()
            sends.append(cp)
        for k in range(1, N_DEV):
            px, py, pc = peer(k)
            pltpu.make_async_remote_copy(src_ref=v_ref, dst_ref=o_ref.at[4 * px + 2 * py + pc],
                                         send_sem=send_sems.at[k - 1], recv_sem=recv_sems.at[k - 1],
                                         device_id=(px, py, pc), device_id_type=MESH).wait_recv()
        for cp in sends:
            cp.wait_send()
        loc.wait()

    return pl.pallas_call(
        body, in_specs=[HBM_SPEC], out_specs=HBM_SPEC,
        out_shape=jax.ShapeDtypeStruct((N_DEV,) + vec.shape, vec.dtype),
        scratch_shapes=[pltpu.SemaphoreType.DMA((N_DEV - 1,)), pltpu.SemaphoreType.DMA((N_DEV - 1,)),
                        pltpu.SemaphoreType.DMA],
        name="allgather_small")(vec)


def _adamw(w, g, m, v):
    m = ADAM_B1 * m + (1.0 - ADAM_B1) * g
    v = ADAM_B2 * v + (1.0 - ADAM_B2) * (g * g)
    m_hat = m / (1.0 - ADAM_B1 ** ADAM_STEP)
    v_hat = v / (1.0 - ADAM_B2 ** ADAM_STEP)
    delta = -ADAM_LR * (m_hat / (jnp.sqrt(v_hat) + ADAM_EPS) + ADAM_WD * w)
    return delta, m, v


def _reduce_adamw(parts, w, m, v, *, tr, name):
    R, C = w.shape

    def body(p_ref, w_ref, m_ref, v_ref, g_ref, d_ref, nm_ref, nv_ref):
        def core_sum(cc):
            s = p_ref[cc, 0].astype(F32)
            for k in range(1, N_CHIPS):
                s = s + p_ref[cc, k].astype(F32)
            return s

        g = core_sum(0) + core_sum(1)
        delta, nm, nv = _adamw(w_ref[...], g, m_ref[...], v_ref[...])
        g_ref[...] = g
        d_ref[...] = delta
        nm_ref[...] = nm
        nv_ref[...] = nv

    blk = pl.BlockSpec((tr, C), lambda i: (i, 0))
    return pl.pallas_call(
        body, grid=(R // tr,),
        in_specs=[pl.BlockSpec((N_CORES, N_CHIPS, tr, C), lambda i: (0, 0, i, 0)), blk, blk, blk],
        out_specs=[blk] * 4, out_shape=[jax.ShapeDtypeStruct((R, C), F32)] * 4,
        name=name, compiler_params=_cparams(("parallel",), 40))(parts, w, m, v)


def _reduce_adamw_vectors(allv, offs, ws, ms, vs):
    n = len(ws)

    def body(a_ref, *refs):
        w_refs, m_refs, v_refs = refs[:n], refs[n:2 * n], refs[2 * n:3 * n]
        tot_ref, outs = refs[3 * n], refs[3 * n + 1:]
        s = a_ref[0]
        for d in range(1, N_DEV):
            s = s + a_ref[d]
        tot_ref[...] = s
        for k in range(n):
            g = s[:, offs[k]:offs[k] + w_refs[k].shape[1]]
            delta, nm, nv = _adamw(w_refs[k][...], g, m_refs[k][...], v_refs[k][...])
            for ref, val in zip(outs[4 * k:4 * k + 4], (g, delta, nm, nv)):
                ref[...] = val

    out_shape = [jax.ShapeDtypeStruct(allv.shape[1:], F32)]
    for w in ws:
        out_shape += [jax.ShapeDtypeStruct(w.shape, F32)] * 4
    res = pl.pallas_call(body, out_shape=out_shape, name="reduce_adamw_vectors",
                         compiler_params=_cparams((), 40))(allv, *ws, *ms, *vs)
    return res[0], [tuple(res[1 + 4 * k:5 + 4 * k]) for k in range(n)]


def _adamw_taps(ws, gs, ms, vs):
    n = len(ws)

    def body(*refs):
        outs = refs[4 * n:]
        for k in range(n):
            res = _adamw(refs[k][...], refs[n + k][...], refs[2 * n + k][...], refs[3 * n + k][...])
            for ref, val in zip(outs[3 * k:3 * k + 3], res):
                ref[...] = val

    out_shape = []
    for w in ws:
        out_shape += [jax.ShapeDtypeStruct(w.shape, F32)] * 3
    res = pl.pallas_call(body, out_shape=out_shape, name="adamw_taps")(*ws, *gs, *ms, *vs)
    return [tuple(res[3 * k:3 * k + 3]) for k in range(n)]


def _pack(pieces):
    flat, offs, n = [], [], 0
    for p in pieces:
        size = -(-p.size // LANES) * LANES
        flat.append(jnp.pad(p.reshape(-1), (0, size - p.size)))
        offs.append(n)
        n += size
    return jnp.concatenate(flat).reshape(1, n), offs


def _local_step(x, target, p, wfull, on_ready=lambda group: None, before_ln0=()):
    S = x.shape[0]
    dils = [d for _, d in GROUPS]

    h0, h0b, *h0_res = _ln0_fwd(x, p["ln0_g"], p["ln0_b"], before_ln0)
    h0_rows = [h0b] + [h.reshape(S, D_MODEL) for h in h0_res]

    if isinstance(wfull, dict):
        w_in3, pending = wfull["w_in"], None
    else:
        w_in3, launch_rest, assemble = wfull
        w_in3, h0b = lax.optimization_barrier((w_in3, h0b))
        pending = launch_rest(h0b)

    w_blocks = w_in3.transpose(1, 0, 2).reshape(D_MODEL, N_BLK, GROUP_W)
    w_perm = jnp.concatenate([w_blocks[:, b] for b in PERM], axis=1)
    b_blocks = p["b_in"].reshape(N_BLK, GROUP_W)
    b_perm = jnp.concatenate([b_blocks[b] for b in PERM]).reshape(1, N_IN)
    w_nat, b_nat = w_perm[:, :N_NAT], b_perm[:, :N_NAT]
    qkv_cols = [slice(P_Q0 + g * QKV_W, P_Q0 + (g + 1) * QKV_W) for g in range(N_GROUPS)]
    w_qkv = [w_perm[:, c] for c in qkv_cols]

    proj = _mm_nn(h0b, w_nat, b_nat, tm=512, tn=N_NAT // 2, out_dtype=BF16, name="proj")
    qkv = [proj[None]]
    for g in range(1, N_GROUPS):
        t = _mm_nn(h0_rows[g], w_qkv[g], b_perm[:, qkv_cols[g]], tm=512, tn=QKV_W, out_dtype=BF16, name=f"proj_qkv{g}")
        qkv.append(t.reshape(dils[g], S // dils[g], QKV_W))
    if pending is not None:
        pending, qkv = lax.optimization_barrier((pending, qkv))
        proj = qkv[0][0]
        wfull = assemble(pending)
    w_up3 = wfull["w_up"]
    w_a, w_o, w_down, w_b = wfull["w_a"], wfull["w_o"], wfull["w_down"], wfull["w_b"]
    conv_w, ffn_conv_w = wfull["conv_w"], wfull["ffn_conv_w"]
    col0 = [P_Q0 // GROUP_W] + [0] * (N_GROUPS - 1)
    ya_in = _conv_gate_fwd(proj, conv_w)
    att = [_attn_fwd(qkv[g], col0[g], g) for g in range(N_GROUPS)]
    comb, comb_b, lse_tot = _attn_combine([a[0] for a in att], [a[1] for a in att])
    yab, mixin = _branch_mix(ya_in, comb_b, w_a, w_b, proj)
    xhat1, rstd1, h1b = _mix_ln1(mixin, w_o, p["b_o"], h0, p["ln1_g"], p["ln1_b"])
    up = _mm_nn(h1b, w_up3, p["b_up"], tm=512, tn=w_up3.shape[2], out_dtype=BF16, name="up")
    f = _ffn_conv_fwd(up, ffn_conv_w, p["ffn_conv_b"])
    dz2, dz2b, st2 = _down_ln2_loss(f, w_down, p["b_down"], xhat1, p["ln1_g"], p["ln1_b"],
                                    p["ln2_g"], p["ln2_b"], target)

    gw = {}
    gw["w_down"] = _mm_tn(f, dz2b, n_out=1, tn=D_MODEL, ts=1024, g_block=(1024, D_MODEL),
                          g_map=lambda j, s: (s, 0), name="grad_w_down").reshape(N_CHIPS, D_FF // N_CHIPS, D_MODEL)
    df = _mm_nt(dz2b, w_down, tm=512, name="df")
    dup, sm_ffn = _ffn_conv_bwd(up, df, ffn_conv_w, p["ffn_conv_b"])
    up_tn = w_up3.shape[2]
    up_pp = D_FF // up_tn
    gw["w_up"] = _mm_tn(h1b, dup, n_out=N_CHIPS, tn=up_tn, ts=1024, g_block=(None, 1024, up_tn),
                        g_map=lambda j, s: (j // up_pp, s, j % up_pp), name="grad_w_up")
    on_ready({n: gw[n] for n in ("w_down", "w_up")})
    dz1, dz1b, st1 = _up_bwd_ln1(dup, w_up3, dz2, xhat1, rstd1, p["ln1_g"])

    gw["w_o"] = _mm_tn(mixin, dz1b, n_out=1, tn=D_MODEL, ts=512, g_block=(512, D_MODEL),
                       g_map=lambda j, s: (s, 0), name="grad_w_o").reshape(N_CHIPS, D_MODEL // N_CHIPS, D_MODEL)
    dyab, dgab = _mix_bwd(dz1b, w_o, proj, yab)
    gw["w_a"] = _mm_tn(ya_in, dyab, n_out=1, tn=D_MODEL, ts=512, g_block=(512, D_MODEL),
                       g_map=lambda j, s: (s, 0), name="grad_w_a").reshape(N_CHIPS, D_CONV // N_CHIPS, D_MODEL)
    gw_b = _mm_tn(comb_b, dyab, n_out=1, tn=D_MODEL, ts=1024, g_block=(1024, D_MODEL),
                  g_map=lambda j, s: (s, 1), name="grad_w_b")
    gw["w_b"] = gw_b.reshape(GROUP_W, N_CHIPS, D_MODEL // N_CHIPS).transpose(1, 0, 2)
    on_ready({n: gw[n] for n in ("w_o", "w_a", "w_b")})
    dya_in = _mm_nt(dyab, w_a, tm=512, a_col=0, name="dya_in")
    dbch, sm_conv = _conv_gate_bwd(proj, dya_in, conv_w)
    att_stats = _comb_bwd(dyab, w_b, comb, lse_tot)
    dqkv = [_attn_bwd(qkv[g], col0[g], g, *att_stats[g]) for g in range(N_GROUPS)]

    w_pieces, b_pieces = [], []
    for nm, planes in (("bch", dbch), ("gab", dgab)):
        pw, pc = _mm_tn(h0b, planes, n_out=planes.shape[0], tn=D_MODEL, ts=1024, g_block=(None, 1024, D_MODEL),
                        g_map=lambda j, s: (j, s, 0), colsum=True, name="grad_w_in_" + nm)
        w_pieces.append(pw.transpose(1, 0, 2).reshape(D_MODEL, planes.shape[0] * D_MODEL))
        b_pieces.append(pc[0])
    for g in range(N_GROUPS):
        pw, pc = _mm_tn_cat(h0_rows[g], [a.reshape(S, GROUP_W) for a in dqkv[g]], ts=1024, name=f"grad_w_in_qkv{g}")
        w_pieces.append(pw)
        b_pieces.append(pc[0])
    dw_blocks = jnp.concatenate(w_pieces, axis=1).reshape(D_MODEL, N_BLK, GROUP_W)
    dw_ref = jnp.concatenate([dw_blocks[:, b] for b in INV_PERM], axis=1)
    gw["w_in"] = dw_ref.reshape(D_MODEL, N_CHIPS, N_IN // N_CHIPS).transpose(1, 0, 2)
    on_ready({"w_in": gw["w_in"]})
    db_blocks = jnp.concatenate(b_pieces).reshape(N_BLK, GROUP_W)
    grad_b_in = jnp.concatenate([db_blocks[b] for b in INV_PERM])

    grad_x, st0 = _in_bwd_ln0([dbch, dgab], dqkv, w_nat, w_qkv[1:], dz1, x, p["ln0_g"])

    small = {
        "loss": st2[2:3, 0:1],
        "ln0_g": st0[0], "ln0_b": st0[1], "b_in": grad_b_in, "conv_w": sm_conv[0:3],
        "b_o": st1[2], "ln1_g": st1[0], "ln1_b": st1[1],
        "b_up": jnp.concatenate([sm_ffn[0], sm_ffn[1]]), "ffn_conv_w": sm_ffn[3:6], "ffn_conv_b": sm_ffn[2],
        "b_down": st2[3], "ln2_g": st2[0], "ln2_b": st2[1],
    }
    return grad_x, gw, small


BIG = ("w_in", "w_a", "w_b", "w_o", "w_up", "w_down")
CONV = ("conv_w", "ffn_conv_w")
VECS = ("ln0_g", "ln0_b", "b_in", "b_o", "ln1_g", "ln1_b", "b_up", "ffn_conv_b", "b_down", "ln2_g", "ln2_b")
ORDER = ("ln0_g", "ln0_b", "w_in", "b_in", "conv_w", "w_a", "w_b", "w_o", "b_o", "ln1_g", "ln1_b", "w_up", "b_up",
         "ffn_conv_w", "ffn_conv_b", "w_down", "b_down", "ln2_g", "ln2_b")
SMALL_ORDER = ("loss",) + VECS + CONV


def _step(x, target, W, Mo, Vo):
    x2, t2 = x[0], target[0]
    big2 = {n: W[n][0] for n in BIG}
    halves = lambda a: a.astype(BF16).reshape(N_CORES, a.shape[0] // N_CORES, a.shape[1])
    whole = lambda g: g.reshape(N_CHIPS, g.shape[1] * g.shape[2], g.shape[3])
    later = tuple(n for n in BIG if n != "w_in")
    w_in_halves = halves(big2["w_in"])
    first = _allgather_shards([w_in_halves], [], name="allgather_w_in", collective_id=1)

    def launch_rest(h0b):
        return _allgather_shards([halves(big2[n]) for n in later] + [W[n] for n in CONV], [h0b],
                                 name="allgather_rest", collective_id=2)

    def assemble(rest):
        gathered = {n: whole(g) for n, g in zip(later + CONV, rest)}
        return {
            "w_up": gathered["w_up"],
            "w_a": gathered["w_a"].reshape(D_CONV, D_MODEL), "w_o": gathered["w_o"].reshape(D_MODEL, D_MODEL),
            "w_down": gathered["w_down"].reshape(D_FF, D_MODEL),
            "w_b": gathered["w_b"].transpose(1, 0, 2).reshape(GROUP_W, D_MODEL),
            "conv_w": gathered["conv_w"].transpose(1, 0, 2).reshape(3, D_CONV),
            "ffn_conv_w": gathered["ffn_conv_w"].transpose(1, 0, 2).reshape(3, D_FF),
        }

    pvec = {n: W[n].reshape(1, -1) for n in VECS}

    parts = {}
    exchange_ids = iter((3, 4, 5))

    def exchange(group):
        names = tuple(group)
        res = _exchange_grads([group[n] for n in names], name="exchange_" + "_".join(names),
                              collective_id=next(exchange_ids))
        parts.update(zip(names, res))

    grad_x, _, small = _local_step(x2, t2, pvec, (whole(first[0]), launch_rest, assemble), exchange,
                                   before_ln0=[w_in_halves])
    out = {}
    for n in BIG:
        tr = {"w_in": 128, "w_up": 128, "w_b": 128}.get(n, big2[n].shape[0] // 4)
        g, d, nm, nv = _reduce_adamw(parts[n], big2[n], Mo[n][0], Vo[n][0], tr=tr, name="adamw_" + n)
        out[n] = tuple(a[None] for a in (g, d, nm, nv))

    vec, offs = _pack([small[n] for n in SMALL_ORDER])
    off = dict(zip(SMALL_ORDER, offs))
    row = lambda a: a.reshape(1, -1)
    tot, vec_out = _reduce_adamw_vectors(_allgather_small(vec), [off[n] for n in VECS], [row(W[n]) for n in VECS],
                                         [row(Mo[n]) for n in VECS], [row(Vo[n]) for n in VECS])
    for n, res in zip(VECS, vec_out):
        out[n] = tuple(a.reshape(W[n].shape) for a in res)
    loss = tot[0, off["loss"]]
    chip = 2 * lax.axis_index("x") + lax.axis_index("y")
    taps_g = []
    for n in CONV:
        width = W[n].shape[2]
        full = lax.slice(tot, (0, off[n]), (1, off[n] + 3 * N_CHIPS * width)).reshape(3, N_CHIPS * width)
        taps_g.append(lax.dynamic_slice_in_dim(full, chip * width, width, axis=1))
    taps_out = _adamw_taps([W[n][0] for n in CONV], taps_g, [Mo[n][0] for n in CONV], [Vo[n][0] for n in CONV])
    for n, g, res in zip(CONV, taps_g, taps_out):
        out[n] = tuple(a[None] for a in (g,) + res)

    res = [loss, grad_x[None]]
    for k in range(4):
        res += [out[n][k] for n in ORDER]
    return tuple(res)


def kernel(x, ln0_g, ln0_b, w_in, b_in, conv_w, w_a, w_b, w_o, b_o, ln1_g, ln1_b, w_up, b_up, ffn_conv_w, ffn_conv_b, w_down, b_down, ln2_g, ln2_b, loss_target, m_ln0_g, m_ln0_b, m_w_in, m_b_in, m_conv_w, m_w_a, m_w_b, m_w_o, m_b_o, m_ln1_g, m_ln1_b, m_w_up, m_b_up, m_ffn_conv_w, m_ffn_conv_b, m_w_down, m_b_down, m_ln2_g, m_ln2_b, v_ln0_g, v_ln0_b, v_w_in, v_b_in, v_conv_w, v_w_a, v_w_b, v_w_o, v_b_o, v_ln1_g, v_ln1_b, v_w_up, v_b_up, v_ffn_conv_w, v_ffn_conv_b, v_w_down, v_b_down, v_ln2_g, v_ln2_b):
    W = dict(zip(ORDER, (ln0_g, ln0_b, w_in, b_in, conv_w, w_a, w_b, w_o, b_o, ln1_g, ln1_b, w_up, b_up,
                         ffn_conv_w, ffn_conv_b, w_down, b_down, ln2_g, ln2_b)))
    Mo = dict(zip(ORDER, (m_ln0_g, m_ln0_b, m_w_in, m_b_in, m_conv_w, m_w_a, m_w_b, m_w_o, m_b_o, m_ln1_g, m_ln1_b,
                          m_w_up, m_b_up, m_ffn_conv_w, m_ffn_conv_b, m_w_down, m_b_down, m_ln2_g, m_ln2_b)))
    Vo = dict(zip(ORDER, (v_ln0_g, v_ln0_b, v_w_in, v_b_in, v_conv_w, v_w_a, v_w_b, v_w_o, v_b_o, v_ln1_g, v_ln1_b,
                          v_w_up, v_b_up, v_ffn_conv_w, v_ffn_conv_b, v_w_down, v_b_down, v_ln2_g, v_ln2_b)))
    return _step(x, loss_target, W, Mo, Vo)
```

```python
import functools
import math

import jax
import jax.numpy as jnp
from jax import lax
from jax.experimental import pallas as pl
from jax.experimental.pallas import tpu as pltpu
from jax.experimental.pallas import tpu_sc as plsc

F32 = jnp.float32
BF16 = jnp.bfloat16

D_MODEL = 1024
D_CONV = D_MODEL
HEAD_DIM = 64
HEADS_PER_GROUP = 8
GROUPS = ((128, 1), (512, 4), (2048, 16))
N_GROUPS = len(GROUPS)
GROUP_W = HEADS_PER_GROUP * HEAD_DIM
QKV_W = N_GROUPS * GROUP_W
RADIUS = 64
D_FF = 2816
LN_EPS = 1e-5
ALPHA = 2.0 ** 0.25
MASK_VALUE = -1e30
ATT_SCALE = HEAD_DIM ** -0.5
OFF_B = 0
OFF_C = OFF_B + D_CONV
OFF_H = OFF_C + D_CONV
OFF_Q = OFF_H + D_CONV
OFF_K = OFF_Q + QKV_W
OFF_V = OFF_K + QKV_W
OFF_GA = OFF_V + QKV_W
OFF_GB = OFF_GA + D_MODEL
N_IN = OFF_GB + D_MODEL
ADAM_LR = 0.001
ADAM_B1 = 0.9
ADAM_B2 = 0.999
ADAM_EPS = 1e-08
ADAM_WD = 0.01
ADAM_STEP = 10
INV_SQRT2 = 0.7071067811865476
INV_SQRT_2PI = 0.3989422804014327

LANES = 128
SUBLANES = 8
VMEM_BYTES_V7X = 64 * 1024 * 1024
N_CHIPS = 4
N_CORES = 2
N_DEV = N_CHIPS * N_CORES
MESH = pl.DeviceIdType.MESH

N_BLK = N_IN // GROUP_W
PERM = (0, 1, 2, 3, 4, 5, 15, 16, 17, 18, 6, 9, 12, 7, 10, 13, 8, 11, 14)
INV_PERM = tuple(PERM.index(b) for b in range(N_BLK))
P_B, P_C, P_H, P_GA, P_GB, P_Q0 = 0, 1024, 2048, 3072, 4096, 5120
N_NAT = P_Q0 + QKV_W // N_GROUPS * 3
N_GATED = P_Q0

SLAB = 128
CHUNK = 256
PAD = SUBLANES
TQ = 128


def _cparams(sem, vmem_mb):
    assert vmem_mb * 1024 * 1024 < VMEM_BYTES_V7X
    return pltpu.CompilerParams(dimension_semantics=sem, vmem_limit_bytes=vmem_mb * 1024 * 1024)


def _dot(a, b):
    return jnp.dot(a, b, preferred_element_type=F32)


def _dot_nt(a, b):
    return lax.dot_general(a, b, (((1,), (1,)), ((), ())), preferred_element_type=F32)


def _dot_tn(a, b):
    return lax.dot_general(a, b, (((0,), (0,)), ((), ())), preferred_element_type=F32)


def _ln_stats(z):
    mu = jnp.mean(z, -1, keepdims=True)
    zc = z - mu
    var = jnp.mean(zc * zc, -1, keepdims=True)
    rstd = lax.rsqrt(var + LN_EPS)
    return zc * rstd, rstd


def _ln_bwd(dh, xhat, rstd, g):
    dxh = dh * g
    m1 = jnp.mean(dxh, -1, keepdims=True)
    m2 = jnp.mean(dxh * xhat, -1, keepdims=True)
    return rstd * (dxh - m1 - xhat * m2)


def _rows8(rows, width):
    pad = [jnp.zeros((1, width), F32)] * (SUBLANES - len(rows))
    return jnp.concatenate(list(rows) + pad, axis=0)


def _mm_nn(a, w, bias, *, tm, tn, out_dtype, name, vmem_mb=40):
    M, K = a.shape
    if w.ndim == 3:
        assert w.shape[2] == tn
        n_tiles = w.shape[0]
        w_spec = pl.BlockSpec((None, K, tn), lambda j, i: (j, 0, 0))
    else:
        n_tiles = w.shape[1] // tn
        w_spec = pl.BlockSpec((K, tn), lambda j, i: (0, j))

    def body(a_ref, w_ref, b_ref, o_ref):
        o_ref[...] = (_dot(a_ref[...], w_ref[...]) + b_ref[...]).astype(o_ref.dtype)

    return pl.pallas_call(
        body, grid=(n_tiles, M // tm),
        in_specs=[pl.BlockSpec((tm, K), lambda j, i: (i, 0)), w_spec, pl.BlockSpec((1, tn), lambda j, i: (0, j))],
        out_specs=pl.BlockSpec((tm, tn), lambda j, i: (i, j)),
        out_shape=jax.ShapeDtypeStruct((M, n_tiles * tn), out_dtype),
        name=name, compiler_params=_cparams(("arbitrary", "parallel"), vmem_mb))(a, w, bias)


def _mm_nt(a, w, *, tm, a_col=0, name, vmem_mb=40):
    M = a.shape[0]
    N, K = w.shape

    def body(a_ref, w_ref, o_ref):
        o_ref[...] = _dot_nt(a_ref[...], w_ref[...]).astype(o_ref.dtype)

    return pl.pallas_call(
        body, grid=(M // tm,),
        in_specs=[pl.BlockSpec((tm, K), lambda i: (i, a_col)),
                  pl.BlockSpec((N, K), lambda i: (0, 0))],
        out_specs=pl.BlockSpec((tm, N), lambda i: (i, 0)),
        out_shape=jax.ShapeDtypeStruct((M, N), BF16),
        name=name, compiler_params=_cparams(("parallel",), vmem_mb))(a, w)


def _mm_tn(a, g, *, n_out, tn, ts, g_block, g_map, colsum=False, name, vmem_mb=48):
    S, K = a.shape
    n_s = S // ts

    def body(a_ref, g_ref, *rest):
        if colsum:
            o_ref, cs_ref, acc_ref, cacc_ref = rest
        else:
            o_ref, acc_ref = rest
        s = pl.program_id(1)

        @pl.when(s == 0)
        def _():
            acc_ref[...] = jnp.zeros_like(acc_ref)
            if colsum:
                cacc_ref[...] = jnp.zeros_like(cacc_ref)

        gv = g_ref[...]
        acc_ref[...] += _dot_tn(a_ref[...], gv)
        if colsum:
            cacc_ref[...] += jnp.broadcast_to(jnp.sum(gv.astype(F32), axis=0, keepdims=True), cacc_ref.shape)

        @pl.when(s == n_s - 1)
        def _():
            o_ref[...] = acc_ref[...].astype(o_ref.dtype)
            if colsum:
                cs_ref[...] = cacc_ref[...]

    out_specs = [pl.BlockSpec((None, K, tn), lambda j, s: (j, 0, 0))]
    out_shape = [jax.ShapeDtypeStruct((n_out, K, tn), BF16)]
    scratch = [pltpu.VMEM((K, tn), F32)]
    if colsum:
        out_specs.append(pl.BlockSpec((SUBLANES, tn), lambda j, s: (0, j)))
        out_shape.append(jax.ShapeDtypeStruct((SUBLANES, n_out * tn), F32))
        scratch.append(pltpu.VMEM((SUBLANES, tn), F32))
    res = pl.pallas_call(
        body, grid=(n_out, n_s),
        in_specs=[pl.BlockSpec((ts, K), lambda j, s: (s, 0)), pl.BlockSpec(g_block, g_map)],
        out_specs=out_specs, out_shape=out_shape, scratch_shapes=scratch,
        name=name, compiler_params=_cparams(("parallel", "arbitrary"), vmem_mb))(a, g)
    return res if colsum else res[0]


def _mm_tn_cat(a, gs, *, ts, name, vmem_mb=40):
    S, K = a.shape
    widths = [g.shape[1] for g in gs]
    n_s, total = S // ts, sum(widths)

    def body(*refs):
        a_ref, g_refs = refs[0], refs[1:1 + len(gs)]
        o_ref, cs_ref, acc_ref, cacc_ref = refs[1 + len(gs):]
        s = pl.program_id(0)

        @pl.when(s == 0)
        def _():
            acc_ref[...] = jnp.zeros_like(acc_ref)
            cacc_ref[...] = jnp.zeros_like(cacc_ref)

        av, col = a_ref[...], 0
        for g_ref, w in zip(g_refs, widths):
            gv = g_ref[...]
            acc_ref[:, col:col + w] += _dot_tn(av, gv)
            cacc_ref[:, col:col + w] += jnp.broadcast_to(jnp.sum(gv.astype(F32), axis=0, keepdims=True), (SUBLANES, w))
            col += w

        @pl.when(s == n_s - 1)
        def _():
            o_ref[...] = acc_ref[...].astype(BF16)
            cs_ref[...] = cacc_ref[...]

    return pl.pallas_call(
        body, grid=(n_s,),
        in_specs=[pl.BlockSpec((ts, K), lambda s: (s, 0))] + [pl.BlockSpec((ts, w), lambda s: (s, 0)) for w in widths],
        out_specs=[pl.BlockSpec((K, total), lambda s: (0, 0)), pl.BlockSpec((SUBLANES, total), lambda s: (0, 0))],
        out_shape=[jax.ShapeDtypeStruct((K, total), BF16), jax.ShapeDtypeStruct((SUBLANES, total), F32)],
        scratch_shapes=[pltpu.VMEM((K, total), F32), pltpu.VMEM((SUBLANES, total), F32)],
        name=name, compiler_params=_cparams(("arbitrary",), vmem_mb))(a, *gs)


DILS = tuple(d for _, d in GROUPS if d > 1)


def _res_spec(d, tm, width):
    return pl.BlockSpec((d, tm // d, width), lambda i: (0, i, 0))


def _lane_scratch(tm, width):
    return [pltpu.VMEM((tm, LANES), F32)] * (width // LANES)


def _to_residue(val, dst_refs, dils, tm, dtype, scr):
    for c, ref in enumerate(scr):
        ref[...] = val[:, c * LANES:(c + 1) * LANES]
    for dst_ref, d in zip(dst_refs, dils):
        for r in range(d):
            cols = [ref[pl.ds(r, tm // d, stride=d), :] for ref in scr]
            dst_ref[r] = jnp.concatenate(cols, axis=1).astype(dtype)


def _from_residue(rows_of, d, tm, scr):
    for r in range(d):
        v = rows_of(r).astype(F32)
        for c, ref in enumerate(scr):
            ref[pl.ds(r, tm // d, stride=d), :] = v[:, c * LANES:(c + 1) * LANES]
    return jnp.concatenate([ref[...] for ref in scr], axis=1)


def _ln0_fwd(x, g, b, after=(), *, tm=512):
    S, Dm = x.shape
    n_after = len(after)

    def body(x_ref, g_ref, b_ref, *rest):
        h_ref, hb_ref, *rest = rest[n_after:]
        xhat, _ = _ln_stats(x_ref[...])
        h = xhat * g_ref[...] + b_ref[...]
        h_ref[...] = h
        hb_ref[...] = h.astype(BF16)
        _to_residue(h, rest[:len(DILS)], DILS, tm, BF16, rest[len(DILS):])

    row = pl.BlockSpec((tm, Dm), lambda i: (i, 0))
    vec = pl.BlockSpec((1, Dm), lambda i: (0, 0))
    return pl.pallas_call(
        body, grid=(S // tm,), in_specs=[row, vec, vec] + [pl.BlockSpec(memory_space=pl.ANY)] * n_after,
        out_specs=[row, row] + [_res_spec(d, tm, Dm) for d in DILS],
        out_shape=[jax.ShapeDtypeStruct((S, Dm), F32), jax.ShapeDtypeStruct((S, Dm), BF16)]
        + [jax.ShapeDtypeStruct((d, S // d, Dm), BF16) for d in DILS],
        scratch_shapes=_lane_scratch(tm, Dm),
        name="ln0_fwd", compiler_params=_cparams(("parallel",), 32))(x, g, b, *after)


def _slab_spec(S, col0):
    return pl.BlockSpec((S, SLAB), lambda j: (0, col0 // SLAB + j))


def _zero_pads(scr, S):
    scr[0:PAD, :] = jnp.zeros((PAD, SLAB), F32)
    scr[S + PAD:S + 2 * PAD, :] = jnp.zeros((PAD, SLAB), F32)


def _shifted(scr, t):
    return (scr[PAD - 1 + t:PAD - 1 + t + CHUNK, :], scr[PAD + t:PAD + t + CHUNK, :],
            scr[PAD + 1 + t:PAD + 1 + t + CHUNK, :])


def _conv_gate_fwd(proj, conv_w):
    S = proj.shape[0]

    def body(b_ref, c_ref, h_ref, w_ref, o_ref, u_scr):
        _zero_pads(u_scr, S)
        for t in range(0, S, CHUNK):
            u_scr[PAD + t:PAD + t + CHUNK, :] = c_ref[t:t + CHUNK, :].astype(F32) * h_ref[t:t + CHUNK, :].astype(F32)
        w0, w1, w2 = w_ref[0:1, :], w_ref[1:2, :], w_ref[2:3, :]
        for t in range(0, S, CHUNK):
            um, u0, up = _shifted(u_scr, t)
            cv = w0 * um + w1 * u0 + w2 * up
            o_ref[t:t + CHUNK, :] = (b_ref[t:t + CHUNK, :].astype(F32) * cv).astype(BF16)

    return pl.pallas_call(
        body, grid=(D_CONV // SLAB,),
        in_specs=[_slab_spec(S, P_B), _slab_spec(S, P_C), _slab_spec(S, P_H),
                  pl.BlockSpec((3, SLAB), lambda j: (0, j))],
        out_specs=pl.BlockSpec((S, SLAB), lambda j: (0, j)),
        out_shape=jax.ShapeDtypeStruct((S, D_CONV), BF16),
        scratch_shapes=[pltpu.VMEM((S + 2 * PAD, SLAB), F32)],
        name="conv_gate_fwd", compiler_params=_cparams(("parallel",), 40))(proj, proj, proj, conv_w)


MASKED_DISTANCE = -1e34


def _attn_bias_table(g):
    dil = GROUPS[g][1]
    j = lax.broadcasted_iota(jnp.int32, (2 * TQ, TQ), 0)
    a = lax.broadcasted_iota(jnp.int32, (2 * TQ, TQ), 1)
    rel = jnp.abs(j - RADIUS - a)
    base = -(rel * dil).astype(F32)
    inside, after_start, before_end = rel <= RADIUS, j >= RADIUS, j < TQ + RADIUS
    variants = []
    for first, last in ((False, False), (True, False), (False, True), (True, True)):
        valid = inside & (after_start if first else True) & (before_end if last else True)
        variants.append(jnp.where(valid, base, MASKED_DISTANCE))
    return jnp.stack(variants)


def _bias_spec(nb):
    def variant(r, i):
        return (jnp.where(i == 0, 1, 0) + jnp.where(i == nb - 1, 2, 0), 0, 0)
    return pl.BlockSpec((None, 2 * TQ, TQ), variant)


def _head_stats(rows):
    pad = jnp.zeros((LANES - len(rows), TQ), F32)
    return jnp.concatenate(list(rows) + [pad], axis=0).T


def _slope(g, h):
    return 2.0 ** (-8.0 * (g * HEADS_PER_GROUP + h + 1) / (N_GROUPS * HEADS_PER_GROUP))


def _window(p_ref, c_ref, n_ref):
    return jnp.concatenate([p_ref[TQ - RADIUS:, :], c_ref[...], n_ref[:RADIUS, :]], axis=0)


def _pair(a, h):
    return a[:, (h // 2) * LANES:(h // 2 + 1) * LANES]


def _own_lanes(a, h):
    lane = lax.broadcasted_iota(jnp.int32, a.shape, 1)
    return jnp.where((lane >= HEAD_DIM) == (h % 2 == 1), a, jnp.zeros_like(a))


def _own_rows(a, h):
    return a[(h % 2) * HEAD_DIM:(h % 2 + 1) * HEAD_DIM, :]


def _qkv_specs(nb, col0):
    def spec(col, shift):
        return pl.BlockSpec((None, TQ, GROUP_W), lambda r, i: (r, jnp.clip(i + shift, 0, nb - 1), col))

    return [spec(col0, 0), spec(col0 + 1, -1), spec(col0 + 1, 0), spec(col0 + 1, 1),
            spec(col0 + 2, -1), spec(col0 + 2, 0), spec(col0 + 2, 1)]


def _attn_fwd(qkv, col0, g):
    dil, sub, _ = qkv.shape
    nb = sub // TQ

    def body(q_ref, kp, kc, kn, vp, vc, vn, bias_ref, o_ref, lse_ref, ot_scr, s_scr, p_scr):
        kwin = _window(kp, kc, kn)
        vwin = _window(vp, vc, vn)
        q = q_ref[...] * ATT_SCALE
        for h in range(HEADS_PER_GROUP):
            s_scr[h] = _dot_nt(_pair(kwin, h), _own_lanes(_pair(q, h), h))
        lse, inv_den = [], []
        for h in range(HEADS_PER_GROUP):
            s = s_scr[h] + _slope(g, h) * bias_ref[...]
            m = jnp.max(s, axis=0, keepdims=True)
            p = jnp.exp(s - m)
            den = jnp.sum(p, axis=0, keepdims=True)
            p_scr[h] = p.astype(BF16)
            inv_den.append(1.0 / den)
            lse.append(m + jnp.log(den))
        for h in range(HEADS_PER_GROUP):
            ot = _dot_tn(_pair(vwin, h), p_scr[h])
            ot_scr[h * HEAD_DIM:(h + 1) * HEAD_DIM, :] = _own_rows(ot, h) * inv_den[h]
        o_ref[...] = ot_scr[...].T
        lse_ref[...] = _head_stats(lse)

    return pl.pallas_call(
        body, grid=(dil, nb), in_specs=_qkv_specs(nb, col0) + [_bias_spec(nb)],
        out_specs=[pl.BlockSpec((None, TQ, GROUP_W), lambda r, i: (r, i, 0)),
                   pl.BlockSpec((None, TQ, LANES), lambda r, i: (r, i, 0))],
        out_shape=[jax.ShapeDtypeStruct((dil, sub, GROUP_W), F32), jax.ShapeDtypeStruct((dil, sub, LANES), F32)],
        scratch_shapes=[pltpu.VMEM((GROUP_W, TQ), F32), pltpu.VMEM((HEADS_PER_GROUP, 2 * TQ, TQ), F32),
                        pltpu.VMEM((HEADS_PER_GROUP, 2 * TQ, TQ), BF16)],
        name=f"attn_fwd_g{g}", compiler_params=_cparams(("parallel", "arbitrary"), 32))(
            *([qkv] * 7), _attn_bias_table(g))


def _expand_heads():
    h = lax.broadcasted_iota(jnp.int32, (LANES, GROUP_W), 0)
    c = lax.broadcasted_iota(jnp.int32, (LANES, GROUP_W), 1)
    return (c // HEAD_DIM == h).astype(F32)


def _dot_f32(a, b):
    return jnp.dot(a, b, preferred_element_type=F32, precision=lax.Precision.HIGHEST)


def _attn_combine(outs, lses, *, tm=512):
    S = outs[0].shape[1]
    n_col = GROUP_W // LANES

    def body(*refs):
        ins, e_ref = refs[:2 * N_GROUPS], refs[2 * N_GROUPS]
        c_ref, cb_ref, lt_ref = refs[2 * N_GROUPS + 1:2 * N_GROUPS + 4]
        scr = refs[2 * N_GROUPS + 4:]
        o, l = [ins[0][0]], [ins[N_GROUPS][0]]
        for k, d in enumerate(DILS):
            o_ref, l_ref = ins[1 + k], ins[N_GROUPS + 1 + k]
            o.append(_from_residue(lambda r: o_ref[r], d, tm, scr[k * (n_col + 1):k * (n_col + 1) + n_col]))
            l.append(_from_residue(lambda r: l_ref[r], d, tm, scr[k * (n_col + 1) + n_col:(k + 1) * (n_col + 1)]))
        m = jnp.maximum(jnp.maximum(l[0], l[1]), l[2])
        e = [jnp.exp(v - m) for v in l]
        den = e[0] + e[1] + e[2]
        comb = sum(_dot_f32(ev / den, e_ref[...]) * ov for ev, ov in zip(e, o))
        c_ref[...] = comb
        cb_ref[...] = comb.astype(BF16)
        lt_ref[...] = m + jnp.log(den)

    row = pl.BlockSpec((tm, GROUP_W), lambda i: (i, 0))
    dils = [d for _, d in GROUPS]
    return pl.pallas_call(
        body, grid=(S // tm,),
        in_specs=[_res_spec(d, tm, GROUP_W) for d in dils] + [_res_spec(d, tm, LANES) for d in dils]
        + [_resident((LANES, GROUP_W))],
        out_specs=[row, row, pl.BlockSpec((tm, LANES), lambda i: (i, 0))],
        out_shape=[jax.ShapeDtypeStruct((S, GROUP_W), F32), jax.ShapeDtypeStruct((S, GROUP_W), BF16),
                   jax.ShapeDtypeStruct((S, LANES), F32)],
        scratch_shapes=_lane_scratch(tm, GROUP_W + LANES) * len(DILS),
        name="attn_combine", compiler_params=_cparams(("parallel",), 32))(*outs, *lses, _expand_heads())


def _branch_mix(ya_in, comb_b, w_a, w_b, proj, *, tm=512):
    S = ya_in.shape[0]

    def body(ya_ref, cb_ref, wa_ref, wb_ref, ga_ref, gb_ref, yab_ref, mx_ref):
        y_a = _dot(ya_ref[...], wa_ref[...])
        y_b = _dot(cb_ref[...], wb_ref[...])
        yab_ref[:, 0:D_MODEL] = y_a.astype(BF16)
        yab_ref[:, D_MODEL:2 * D_MODEL] = y_b.astype(BF16)
        mx = jax.nn.sigmoid(ga_ref[...].astype(F32)) * y_a + jax.nn.sigmoid(gb_ref[...].astype(F32)) * y_b
        mx_ref[...] = mx.astype(BF16)

    return pl.pallas_call(
        body, grid=(S // tm,),
        in_specs=[pl.BlockSpec((tm, D_CONV), lambda i: (i, 0)), pl.BlockSpec((tm, GROUP_W), lambda i: (i, 0)),
                  pl.BlockSpec((D_CONV, D_MODEL), lambda i: (0, 0)), pl.BlockSpec((GROUP_W, D_MODEL), lambda i: (0, 0)),
                  pl.BlockSpec((tm, D_MODEL), lambda i: (i, P_GA // D_MODEL)),
                  pl.BlockSpec((tm, D_MODEL), lambda i: (i, P_GB // D_MODEL))],
        out_specs=[pl.BlockSpec((tm, 2 * D_MODEL), lambda i: (i, 0)), pl.BlockSpec((tm, D_MODEL), lambda i: (i, 0))],
        out_shape=[jax.ShapeDtypeStruct((S, 2 * D_MODEL), BF16), jax.ShapeDtypeStruct((S, D_MODEL), BF16)],
        name="branch_mix", compiler_params=_cparams(("parallel",), 40))(ya_in, comb_b, w_a, w_b, proj, proj)


def _mix_ln1(mixin, w_o, b_o, h0, g1, b1, *, tm=512):
    S = mixin.shape[0]

    def body(mx_ref, wo_ref, bo_ref, h0_ref, g_ref, b_ref, xh_ref, rs_ref, h1b_ref):
        z = ALPHA * h0_ref[...] + _dot(mx_ref[...], wo_ref[...]) + bo_ref[...]
        xhat, rstd = _ln_stats(z)
        xh_ref[...] = xhat
        rs_ref[...] = jnp.broadcast_to(rstd, (tm, LANES))
        h1b_ref[...] = (xhat * g_ref[...] + b_ref[...]).astype(BF16)

    row = pl.BlockSpec((tm, D_MODEL), lambda i: (i, 0))
    vec = pl.BlockSpec((1, D_MODEL), lambda i: (0, 0))
    return pl.pallas_call(
        body, grid=(S // tm,),
        in_specs=[row, pl.BlockSpec((D_MODEL, D_MODEL), lambda i: (0, 0)), vec, row, vec, vec],
        out_specs=[row, pl.BlockSpec((tm, LANES), lambda i: (i, 0)), row],
        out_shape=[jax.ShapeDtypeStruct((S, D_MODEL), F32), jax.ShapeDtypeStruct((S, LANES), F32),
                   jax.ShapeDtypeStruct((S, D_MODEL), BF16)],
        name="mix_ln1", compiler_params=_cparams(("parallel",), 40))(mixin, w_o, b_o, h0, g1, b1)


def _gelu_parts(cz):
    cdf = 0.5 * (1.0 + lax.erf(cz * INV_SQRT2))
    return cdf, cz * cdf


def _ffn_conv_fwd(up, cw, cb):
    S = up.shape[0]

    def body(a_ref, g_ref, w_ref, cb_ref, o_ref, a_scr):
        _zero_pads(a_scr, S)
        for t in range(0, S, CHUNK):
            a_scr[PAD + t:PAD + t + CHUNK, :] = a_ref[t:t + CHUNK, :].astype(F32)
        w0, w1, w2 = w_ref[0:1, :], w_ref[1:2, :], w_ref[2:3, :]
        for t in range(0, S, CHUNK):
            am, a0, ap = _shifted(a_scr, t)
            _, gel = _gelu_parts(w0 * am + w1 * a0 + w2 * ap + cb_ref[...])
            o_ref[t:t + CHUNK, :] = (gel * g_ref[t:t + CHUNK, :].astype(F32)).astype(BF16)

    return pl.pallas_call(
        body, grid=(D_FF // SLAB,),
        in_specs=[_slab_spec(S, 0), _slab_spec(S, D_FF), pl.BlockSpec((3, SLAB), lambda j: (0, j)),
                  pl.BlockSpec((1, SLAB), lambda j: (0, j))],
        out_specs=pl.BlockSpec((S, SLAB), lambda j: (0, j)),
        out_shape=jax.ShapeDtypeStruct((S, D_FF), BF16),
        scratch_shapes=[pltpu.VMEM((S + 2 * PAD, SLAB), F32)],
        name="ffn_conv_fwd", compiler_params=_cparams(("parallel",), 40))(up, up, cw, cb)


def _down_ln2_loss(f, w_down, b_down, xhat1, g1, b1, g2, b2, target, *, tm=512):
    S = f.shape[0]

    def body(f_ref, wd_ref, bd_ref, xh1_ref, g1_ref, b1_ref, g2_ref, b2_ref, t_ref, dz_ref, dzb_ref, st_ref):
        h1 = xh1_ref[...] * g1_ref[...] + b1_ref[...]
        z = ALPHA * h1 + _dot(f_ref[...], wd_ref[...]) + bd_ref[...]
        xhat, rstd = _ln_stats(z)
        err = xhat * g2_ref[...] + b2_ref[...] - t_ref[...]
        loss = (0.5 / D_MODEL) * jnp.sum(jnp.sum(err * err, axis=1, keepdims=True), axis=0, keepdims=True)
        dh2 = err * (1.0 / D_MODEL)
        dz = _ln_bwd(dh2, xhat, rstd, g2_ref[...])
        dz_ref[...] = dz
        dzb_ref[...] = dz.astype(BF16)
        upd = _rows8([jnp.sum(dh2 * xhat, axis=0, keepdims=True), jnp.sum(dh2, axis=0, keepdims=True),
                      jnp.broadcast_to(loss, (1, D_MODEL)), jnp.sum(dz, axis=0, keepdims=True)], D_MODEL)

        @pl.when(pl.program_id(0) == 0)
        def _():
            st_ref[...] = upd

        @pl.when(pl.program_id(0) != 0)
        def _():
            st_ref[...] += upd

    row = pl.BlockSpec((tm, D_MODEL), lambda i: (i, 0))
    vec = pl.BlockSpec((1, D_MODEL), lambda i: (0, 0))
    return pl.pallas_call(
        body, grid=(S // tm,),
        in_specs=[pl.BlockSpec((tm, D_FF), lambda i: (i, 0)), _resident((D_FF, D_MODEL)),
                  vec, row, vec, vec, vec, vec, row],
        out_specs=[row, row, pl.BlockSpec((SUBLANES, D_MODEL), lambda i: (0, 0))],
        out_shape=[jax.ShapeDtypeStruct((S, D_MODEL), F32), jax.ShapeDtypeStruct((S, D_MODEL), BF16),
                   jax.ShapeDtypeStruct((SUBLANES, D_MODEL), F32)],
        name="down_ln2_loss", compiler_params=_cparams(("arbitrary",), 56))(
            f, w_down, b_down, xhat1, g1, b1, g2, b2, target)


def _ffn_conv_bwd(up, df, cw, cb):
    S = up.shape[0]

    def body(a_ref, g_ref, df_ref, w_ref, cb_ref, dup_ref, sm_ref, a_scr, d_scr):
        _zero_pads(a_scr, S)
        _zero_pads(d_scr, S)
        for t in range(0, S, CHUNK):
            a_scr[PAD + t:PAD + t + CHUNK, :] = a_ref[t:t + CHUNK, :].astype(F32)
        w0, w1, w2 = w_ref[0:1, :], w_ref[1:2, :], w_ref[2:3, :]
        zero = jnp.zeros((1, SLAB), F32)
        s_dg, s_dcz, s_w0, s_w1, s_w2 = zero, zero, zero, zero, zero
        for t in range(0, S, CHUNK):
            am, a0, ap = _shifted(a_scr, t)
            cz = w0 * am + w1 * a0 + w2 * ap + cb_ref[...]
            cdf, gel = _gelu_parts(cz)
            dfv = df_ref[t:t + CHUNK, :].astype(F32)
            dgte = dfv * gel
            dcz = dfv * g_ref[t:t + CHUNK, :].astype(F32) * (cdf + cz * jnp.exp(-0.5 * cz * cz) * INV_SQRT_2PI)
            dup_ref[1, t:t + CHUNK, :] = dgte.astype(BF16)
            d_scr[PAD + t:PAD + t + CHUNK, :] = dcz
            s_dg = s_dg + jnp.sum(dgte, axis=0, keepdims=True)
            s_dcz = s_dcz + jnp.sum(dcz, axis=0, keepdims=True)
            s_w0 = s_w0 + jnp.sum(dcz * am, axis=0, keepdims=True)
            s_w1 = s_w1 + jnp.sum(dcz * a0, axis=0, keepdims=True)
            s_w2 = s_w2 + jnp.sum(dcz * ap, axis=0, keepdims=True)
        s_da = zero
        for t in range(0, S, CHUNK):
            dm, d0, dp = _shifted(d_scr, t)
            da = w0 * dp + w1 * d0 + w2 * dm
            dup_ref[0, t:t + CHUNK, :] = da.astype(BF16)
            s_da = s_da + jnp.sum(da, axis=0, keepdims=True)
        sm_ref[...] = _rows8([s_da, s_dg, s_dcz, s_w0, s_w1, s_w2], SLAB)

    return pl.pallas_call(
        body, grid=(D_FF // SLAB,),
        in_specs=[_slab_spec(S, 0), _slab_spec(S, D_FF), pl.BlockSpec((S, SLAB), lambda j: (0, j)),
                  pl.BlockSpec((3, SLAB), lambda j: (0, j)), pl.BlockSpec((1, SLAB), lambda j: (0, j))],
        out_specs=[pl.BlockSpec((2, S, SLAB), lambda j: (0, 0, j)), pl.BlockSpec((SUBLANES, SLAB), lambda j: (0, j))],
        out_shape=[jax.ShapeDtypeStruct((2, S, D_FF), BF16), jax.ShapeDtypeStruct((SUBLANES, D_FF), F32)],
        scratch_shapes=[pltpu.VMEM((S + 2 * PAD, SLAB), F32)] * 2,
        name="ffn_conv_bwd", compiler_params=_cparams(("parallel",), 48))(up, up, df, cw, cb)


def _resident(shape):
    nd = len(shape)
    return pl.BlockSpec(shape, lambda *_: (0,) * nd, pipeline_mode=pl.Buffered(1))


def _up_bwd_ln1(dup, w_up3, dz2, xhat1, rstd1, g1, *, tm=512):
    S = dz2.shape[0]
    ns, _, tk = w_up3.shape
    per_plane = D_FF // tk

    def body(du_ref, w_ref, dz2_ref, xh_ref, rs_ref, g_ref, dz_ref, dzb_ref, st_ref):
        dh = ALPHA * dz2_ref[...]
        for k in range(ns):
            col = (k % per_plane) * tk
            dh = dh + _dot_nt(du_ref[k // per_plane, :, col:col + tk], w_ref[k])
        xhat = xh_ref[...]
        dz = _ln_bwd(dh, xhat, rs_ref[:, 0:1], g_ref[...])
        dz_ref[...] = dz
        dzb_ref[...] = dz.astype(BF16)
        upd = _rows8([jnp.sum(dh * xhat, axis=0, keepdims=True), jnp.sum(dh, axis=0, keepdims=True),
                      jnp.sum(dz, axis=0, keepdims=True)], D_MODEL)

        @pl.when(pl.program_id(0) == 0)
        def _():
            st_ref[...] = upd

        @pl.when(pl.program_id(0) != 0)
        def _():
            st_ref[...] += upd

    row = pl.BlockSpec((tm, D_MODEL), lambda i: (i, 0))
    return pl.pallas_call(
        body, grid=(S // tm,),
        in_specs=[pl.BlockSpec((dup.shape[0], tm, D_FF), lambda i: (0, i, 0)), _resident(w_up3.shape),
                  row, row, pl.BlockSpec((tm, LANES), lambda i: (i, 0)), pl.BlockSpec((1, D_MODEL), lambda i: (0, 0))],
        out_specs=[row, row, pl.BlockSpec((SUBLANES, D_MODEL), lambda i: (0, 0))],
        out_shape=[jax.ShapeDtypeStruct((S, D_MODEL), F32), jax.ShapeDtypeStruct((S, D_MODEL), BF16),
                   jax.ShapeDtypeStruct((SUBLANES, D_MODEL), F32)],
        name="up_bwd_ln1", compiler_params=_cparams(("arbitrary",), 56))(dup, w_up3, dz2, xhat1, rstd1, g1)


def _mix_bwd(dz1b, w_o, proj, yab, *, tm=512):
    S = dz1b.shape[0]

    def body(dz_ref, wo_ref, ga_ref, gb_ref, y_ref, dy_ref, dg_ref):
        dmx = _dot_nt(dz_ref[...], wo_ref[...])
        for k, gt_ref in enumerate((ga_ref, gb_ref)):
            sl = slice(k * D_MODEL, (k + 1) * D_MODEL)
            sg = jax.nn.sigmoid(gt_ref[...].astype(F32))
            dy_ref[:, sl] = (dmx * sg).astype(BF16)
            dg_ref[k] = (dmx * y_ref[:, sl].astype(F32) * sg * (1.0 - sg)).astype(BF16)

    row = pl.BlockSpec((tm, D_MODEL), lambda i: (i, 0))
    wide = pl.BlockSpec((tm, 2 * D_MODEL), lambda i: (i, 0))
    return pl.pallas_call(
        body, grid=(S // tm,),
        in_specs=[row, _resident(w_o.shape), pl.BlockSpec((tm, D_MODEL), lambda i: (i, P_GA // D_MODEL)),
                  pl.BlockSpec((tm, D_MODEL), lambda i: (i, P_GB // D_MODEL)), wide],
        out_specs=[wide, pl.BlockSpec((2, tm, D_MODEL), lambda i: (0, i, 0))],
        out_shape=[jax.ShapeDtypeStruct((S, 2 * D_MODEL), BF16), jax.ShapeDtypeStruct((2, S, D_MODEL), BF16)],
        name="mix_bwd", compiler_params=_cparams(("parallel",), 40))(dz1b, w_o, proj, proj, yab)


def _conv_gate_bwd(proj, dya_in, conv_w):
    S = proj.shape[0]

    def body(b_ref, c_ref, h_ref, dy_ref, w_ref, o_ref, sm_ref, u_scr, d_scr):
        _zero_pads(u_scr, S)
        _zero_pads(d_scr, S)
        for t in range(0, S, CHUNK):
            u_scr[PAD + t:PAD + t + CHUNK, :] = c_ref[t:t + CHUNK, :].astype(F32) * h_ref[t:t + CHUNK, :].astype(F32)
        w0, w1, w2 = w_ref[0:1, :], w_ref[1:2, :], w_ref[2:3, :]
        zero = jnp.zeros((1, SLAB), F32)
        s_w0, s_w1, s_w2 = zero, zero, zero
        for t in range(0, S, CHUNK):
            um, u0, up = _shifted(u_scr, t)
            dy = dy_ref[t:t + CHUNK, :].astype(F32)
            o_ref[0, t:t + CHUNK, :] = (dy * (w0 * um + w1 * u0 + w2 * up)).astype(BF16)
            dcv = dy * b_ref[t:t + CHUNK, :].astype(F32)
            d_scr[PAD + t:PAD + t + CHUNK, :] = dcv
            s_w0 = s_w0 + jnp.sum(dcv * um, axis=0, keepdims=True)
            s_w1 = s_w1 + jnp.sum(dcv * u0, axis=0, keepdims=True)
            s_w2 = s_w2 + jnp.sum(dcv * up, axis=0, keepdims=True)
        for t in range(0, S, CHUNK):
            dm, d0, dp = _shifted(d_scr, t)
            du = w0 * dp + w1 * d0 + w2 * dm
            o_ref[1, t:t + CHUNK, :] = (du * h_ref[t:t + CHUNK, :].astype(F32)).astype(BF16)
            o_ref[2, t:t + CHUNK, :] = (du * c_ref[t:t + CHUNK, :].astype(F32)).astype(BF16)
        sm_ref[...] = _rows8([s_w0, s_w1, s_w2], SLAB)

    return pl.pallas_call(
        body, grid=(D_CONV // SLAB,),
        in_specs=[_slab_spec(S, P_B), _slab_spec(S, P_C), _slab_spec(S, P_H),
                  pl.BlockSpec((S, SLAB), lambda j: (0, j)), pl.BlockSpec((3, SLAB), lambda j: (0, j))],
        out_specs=[pl.BlockSpec((3, S, SLAB), lambda j: (0, 0, j)), pl.BlockSpec((SUBLANES, SLAB), lambda j: (0, j))],
        out_shape=[jax.ShapeDtypeStruct((3, S, D_CONV), BF16), jax.ShapeDtypeStruct((SUBLANES, D_CONV), F32)],
        scratch_shapes=[pltpu.VMEM((S + 2 * PAD, SLAB), F32)] * 2,
        name="conv_gate_bwd", compiler_params=_cparams(("parallel",), 48))(proj, proj, proj, dya_in, conv_w)


def _comb_bwd(dyab, w_b, comb, lse_tot, *, tm=512):
    S = comb.shape[0]
    widths, dtypes = (GROUP_W, LANES, LANES), (BF16, F32, F32)

    def body(dy_ref, wb_ref, c_ref, lt_ref, e_ref, *rest):
        outs, scr = rest[:3 * N_GROUPS], rest[3 * N_GROUPS:]
        dcb = _dot_nt(dy_ref[...], wb_ref[...]).astype(BF16)
        dc = dcb.astype(F32)
        delta = lax.dot_general(dc * c_ref[...], e_ref[...], (((1,), (1,)), ((), ())),
                                preferred_element_type=F32, precision=lax.Precision.HIGHEST)
        for k, (val, dtype) in enumerate(zip((dc, lt_ref[...], delta), dtypes)):
            outs[k][0] = val.astype(dtype)
            _to_residue(val, [outs[3 * (1 + j) + k] for j in range(len(DILS))], DILS, tm, dtype,
                        scr[:val.shape[1] // LANES])

    out_specs, out_shape = [], []
    for _, d in GROUPS:
        out_specs += [_res_spec(d, tm, w) for w in widths]
        out_shape += [jax.ShapeDtypeStruct((d, S // d, w), t) for w, t in zip(widths, dtypes)]
    res = pl.pallas_call(
        body, grid=(S // tm,),
        in_specs=[pl.BlockSpec((tm, D_MODEL), lambda i: (i, 1)), _resident(w_b.shape),
                  pl.BlockSpec((tm, GROUP_W), lambda i: (i, 0)), pl.BlockSpec((tm, LANES), lambda i: (i, 0)),
                  _resident((LANES, GROUP_W))],
        out_specs=out_specs, out_shape=out_shape, scratch_shapes=_lane_scratch(tm, GROUP_W),
        name="comb_bwd", compiler_params=_cparams(("parallel",), 32))(dyab, w_b, comb, lse_tot, _expand_heads())
    return [tuple(res[3 * g:3 * g + 3]) for g in range(N_GROUPS)]


def _attn_bwd(qkv, col0, g, dcomb, lse_tot, delta):
    dil, sub, _ = qkv.shape
    nb = sub // TQ

    def body(q_ref, kp, kc, kn, vp, vc, vn, do_ref, lse_ref, dl_ref, bias_ref, dq_ref, dk_ref, dv_ref,
             ak, av, dqt_scr, s_scr, dp_scr, ds_scr, p_scr):
        i = pl.program_id(1)

        @pl.when(i == 0)
        def _():
            ak[...] = jnp.zeros_like(ak)
            av[...] = jnp.zeros_like(av)

        @pl.when(i < nb)
        def _():
            kwin = _window(kp, kc, kn)
            vwin = _window(vp, vc, vn)
            q = q_ref[...] * ATT_SCALE
            do = do_ref[...]
            lse_t, dl_t = lse_ref[...].T, dl_ref[...].T
            for h in range(HEADS_PER_GROUP):
                s_scr[h] = _dot_nt(_pair(kwin, h), _own_lanes(_pair(q, h), h))
                dp_scr[h] = _dot_nt(_pair(vwin, h), _own_lanes(_pair(do, h), h))
            for h in range(HEADS_PER_GROUP):
                p = jnp.exp(s_scr[h] + _slope(g, h) * bias_ref[...] - lse_t[h:h + 1, :])
                ds_scr[h] = (p * (dp_scr[h] - dl_t[h:h + 1, :])).astype(BF16)
                p_scr[h] = p.astype(BF16)
            for h in range(HEADS_PER_GROUP):
                dqt_scr[h * HEAD_DIM:(h + 1) * HEAD_DIM, :] = _own_rows(_dot_tn(_pair(kwin, h), ds_scr[h]), h)
            for h in range(0, HEADS_PER_GROUP, 2):
                cols = slice(h * HEAD_DIM, (h + 2) * HEAD_DIM)
                q2 = jnp.concatenate([_own_lanes(_pair(q, h), h), _own_lanes(_pair(q, h), h + 1)], axis=0)
                do2 = jnp.concatenate([_own_lanes(_pair(do, h), h), _own_lanes(_pair(do, h), h + 1)], axis=0)
                ak[RADIUS:RADIUS + 2 * TQ, cols] += _dot(jnp.concatenate([ds_scr[h], ds_scr[h + 1]], axis=1), q2)
                av[RADIUS:RADIUS + 2 * TQ, cols] += _dot(jnp.concatenate([p_scr[h], p_scr[h + 1]], axis=1), do2)
            dq_ref[...] = (dqt_scr[...].T * ATT_SCALE).astype(BF16)

        dk_ref[...] = ak[0:TQ, :].astype(BF16)
        dv_ref[...] = av[0:TQ, :].astype(BF16)
        ak[0:2 * TQ, :] = ak[TQ:3 * TQ, :]
        av[0:2 * TQ, :] = av[TQ:3 * TQ, :]
        ak[2 * TQ:3 * TQ, :] = jnp.zeros((TQ, GROUP_W), F32)
        av[2 * TQ:3 * TQ, :] = jnp.zeros((TQ, GROUP_W), F32)

    tok = pl.BlockSpec((None, TQ, GROUP_W), lambda r, i: (r, jnp.minimum(i, nb - 1), 0))
    stat = pl.BlockSpec((None, TQ, LANES), lambda r, i: (r, jnp.minimum(i, nb - 1), 0))
    dkv_spec = pl.BlockSpec((None, TQ, GROUP_W), lambda r, i: (r, jnp.maximum(i - 1, 0), 0))
    return pl.pallas_call(
        body, grid=(dil, nb + 1), in_specs=_qkv_specs(nb, col0) + [tok, stat, stat, _bias_spec(nb)],
        out_specs=[tok, dkv_spec, dkv_spec], out_shape=[jax.ShapeDtypeStruct((dil, sub, GROUP_W), BF16)] * 3,
        scratch_shapes=[pltpu.VMEM((3 * TQ, GROUP_W), F32)] * 2 + [pltpu.VMEM((GROUP_W, TQ), F32)]
        + [pltpu.VMEM((HEADS_PER_GROUP, 2 * TQ, TQ), F32)] * 2 + [pltpu.VMEM((HEADS_PER_GROUP, 2 * TQ, TQ), BF16)] * 2,
        name=f"attn_bwd_g{g}", compiler_params=_cparams(("arbitrary", "arbitrary"), 32))(
            *([qkv] * 7), dcomb, lse_tot, delta, _attn_bias_table(g))


def _in_bwd_ln0(dgated, dqkv, w_nat, w_dil, dz1, x, g0, *, tm=256):
    S = x.shape[0]
    n_gated, n_in = len(dgated), 3 * N_GROUPS

    def body(*refs):
        g_refs, d_refs = refs[:n_gated], refs[n_gated:n_gated + n_in]
        wn_ref, *wd_refs = refs[n_gated + n_in:n_gated + n_in + N_GROUPS]
        dz_ref, x_ref, g_ref, gx_ref, st_ref, *tmp_ref = refs[n_gated + n_in + N_GROUPS:]
        dh = ALPHA * dz_ref[...]
        col = 0
        for ref in g_refs:
            for k in range(ref.shape[0]):
                dh = dh + _dot_nt(ref[k], wn_ref[:, col:col + D_MODEL])
                col += D_MODEL
        for g, (_, d) in enumerate(GROUPS):
            rows = [jnp.concatenate([d_refs[3 * g + k][r] for k in range(3)], axis=1) for r in range(d)]
            w = wn_ref[:, col:col + QKV_W] if d == 1 else wd_refs[g - 1][...]
            res = _dot_nt(jnp.concatenate(rows, axis=0), w)
            if d == 1:
                dh = dh + res
            else:
                n = tm // d
                dh = dh + _from_residue(lambda r: res[r * n:(r + 1) * n, :], d, tm, tmp_ref)
        xhat, rstd = _ln_stats(x_ref[...])
        gx_ref[...] = _ln_bwd(dh, xhat, rstd, g_ref[...])
        upd = _rows8([jnp.sum(dh * xhat, axis=0, keepdims=True), jnp.sum(dh, axis=0, keepdims=True)], D_MODEL)

        @pl.when(pl.program_id(0) == 0)
        def _():
            st_ref[...] = upd

        @pl.when(pl.program_id(0) != 0)
        def _():
            st_ref[...] += upd

    row = pl.BlockSpec((tm, D_MODEL), lambda i: (i, 0))
    g_specs = [pl.BlockSpec((a.shape[0], tm, D_MODEL), lambda i: (0, i, 0)) for a in dgated]
    d_specs = []
    for _, d in GROUPS:
        d_specs += [_res_spec(d, tm, GROUP_W)] * 3
    operands = list(dgated) + [a for grp in dqkv for a in grp] + [w_nat] + list(w_dil) + [dz1, x, g0]
    return pl.pallas_call(
        body, grid=(S // tm,),
        in_specs=g_specs + d_specs + [_resident(w_nat.shape)] + [_resident(w.shape) for w in w_dil]
        + [row, row, pl.BlockSpec((1, D_MODEL), lambda i: (0, 0))],
        out_specs=[row, pl.BlockSpec((SUBLANES, D_MODEL), lambda i: (0, 0))],
        out_shape=[jax.ShapeDtypeStruct((S, D_MODEL), F32), jax.ShapeDtypeStruct((SUBLANES, D_MODEL), F32)],
        scratch_shapes=_lane_scratch(tm, D_MODEL),
        name="in_bwd_ln0", compiler_params=_cparams(("arbitrary",), 52))(*operands)


HBM_SPEC = pl.BlockSpec(memory_space=pltpu.HBM)


def _place():
    x, y, c = lax.axis_index("x"), lax.axis_index("y"), lax.axis_index("c")
    chips = [(1 - x, y), (x, 1 - y), (1 - x, 1 - y)]
    return x, y, c, chips


def _allgather_shards(shards, after, *, name, collective_id):
    n = len(shards)
    per = 6

    def body(*refs):
        ins, outs = refs[:n], refs[n + len(after):2 * n + len(after)]
        send_sems, recv_sems, loc_sems = refs[2 * n + len(after):]
        x, y, c, chips = _place()
        me = 2 * x + y
        sib = (x, y, 1 - c)
        peers = [sib] + [(px, py, c) for px, py in chips]
        barrier = pltpu.get_barrier_semaphore()
        for peer in peers:
            pl.semaphore_signal(barrier, inc=1, device_id=peer, device_id_type=MESH)
        pl.semaphore_wait(barrier, len(peers))

        def rcopy(w, k, src, dst, to):
            return pltpu.make_async_remote_copy(src_ref=src, dst_ref=dst, send_sem=send_sems.at[per * w + k],
                                                recv_sem=recv_sems.at[per * w + k], device_id=to, device_id_type=MESH)

        split = [s.shape[0] == N_CORES for s in shards]
        half = lambda w: c if split[w] else 0
        local, sends = [], []
        for w in range(n):
            cp = pltpu.make_async_copy(ins[w], outs[w].at[me], loc_sems.at[w])
            cp.start()
            local.append(cp)
            for j, (px, py) in enumerate(chips):
                cp = rcopy(w, j, ins[w].at[half(w)], outs[w].at[me, half(w)], (px, py, c))
                cp.start()
                sends.append(cp)
        for w in range(n):
            for j, (px, py) in enumerate(chips):
                slot = outs[w].at[2 * px + py, half(w)]
                rcopy(w, j, slot, slot, (px, py, c)).wait_recv()
                if split[w]:
                    cp = rcopy(w, 3 + j, slot, slot, sib)
                    cp.start()
                    sends.append(cp)
        for w in range(n):
            if split[w]:
                for j, (px, py) in enumerate(chips):
                    slot = outs[w].at[2 * px + py, 1 - c]
                    rcopy(w, 3 + j, slot, slot, sib).wait_recv()
        for cp in sends:
            cp.wait_send()
        for cp in local:
            cp.wait()

    return pl.kernel(
        body, out_type=[jax.ShapeDtypeStruct((N_CHIPS,) + s.shape, s.dtype) for s in shards],
        mesh=plsc.ScalarSubcoreMesh(axis_name="sequencer", num_cores=1),
        scratch_types=[pltpu.SemaphoreType.DMA((per * n,)), pltpu.SemaphoreType.DMA((per * n,)),
                       pltpu.SemaphoreType.DMA((n,))],
        name=name, compiler_params=pltpu.CompilerParams(collective_id=collective_id))(*shards, *after)


def _exchange_grads(grads, *, name, collective_id):
    n = len(grads)
    per = 7

    def body(*refs):
        ins, outs = refs[:n], refs[n:2 * n]
        send_sems, recv_sems, loc_sems = refs[2 * n:]
        x, y, c, chips = _place()
        me = 2 * x + y
        sib = (x, y, 1 - c)
        peers = [sib] + [(px, py, c) for px, py in chips]
        barrier = pltpu.get_barrier_semaphore()
        for peer in peers:
            pl.semaphore_signal(barrier, inc=1, device_id=peer, device_id_type=MESH)
        pl.semaphore_wait(barrier, len(peers))

        def rcopy(w, k, src, dst, to):
            return pltpu.make_async_remote_copy(src_ref=src, dst_ref=dst, send_sem=send_sems.at[per * w + k],
                                                recv_sem=recv_sems.at[per * w + k], device_id=to, device_id_type=MESH)

        local, sends = [], []
        for w in range(n):
            cp = pltpu.make_async_copy(ins[w].at[me], outs[w].at[c, me], loc_sems.at[w])
            cp.start()
            local.append(cp)
            cp = rcopy(w, 0, ins[w].at[me], outs[w].at[c, me], sib)
            cp.start()
            sends.append(cp)
            for j, (px, py) in enumerate(chips):
                cp = rcopy(w, 1 + j, ins[w].at[2 * px + py], outs[w].at[c, me], (px, py, c))
                cp.start()
                sends.append(cp)
        for w in range(n):
            for j, (px, py) in enumerate(chips):
                slot = outs[w].at[c, 2 * px + py]
                rcopy(w, 1 + j, slot, slot, (px, py, c)).wait_recv()
                cp = rcopy(w, 4 + j, slot, slot, sib)
                cp.start()
                sends.append(cp)
        for w in range(n):
            slot = outs[w].at[1 - c, me]
            rcopy(w, 0, slot, slot, sib).wait_recv()
            for j, (px, py) in enumerate(chips):
                slot = outs[w].at[1 - c, 2 * px + py]
                rcopy(w, 4 + j, slot, slot, sib).wait_recv()
        for cp in sends:
            cp.wait_send()
        for cp in local:
            cp.wait()

    return pl.kernel(
        body, out_type=[jax.ShapeDtypeStruct((N_CORES,) + g.shape, g.dtype) for g in grads],
        mesh=plsc.ScalarSubcoreMesh(axis_name="sequencer", num_cores=1),
        scratch_types=[pltpu.SemaphoreType.DMA((per * n,)), pltpu.SemaphoreType.DMA((per * n,)),
                       pltpu.SemaphoreType.DMA((n,))],
        name=name, compiler_params=pltpu.CompilerParams(collective_id=collective_id))(*grads)


def _allgather_small(vec, after):
    def body(v_ref, _, o_ref, send_sems, recv_sems, loc_sem):
        x, y, c = lax.axis_index("x"), lax.axis_index("y"), lax.axis_index("c")
        me = 4 * x + 2 * y + c

        def peer(k):
            flip = lambda v, bit: 1 - v if (k >> bit) & 1 else v
            return flip(x, 2), flip(y, 1), flip(c, 0)

        loc = pltpu.make_async_copy(v_ref, o_ref.at[me], loc_sem)
        loc.start()
        sends = []
        for k in range(1, N_DEV):
            cp = pltpu.make_async_remote_copy(src_ref=v_ref, dst_ref=o_ref.at[me], send_sem=send_sems.at[k - 1],
                                              recv_sem=recv_sems.at[k - 1], device_id=peer(k), device_id_type=MESH)
            cp.start()
            sends.append(cp)
        for k in range(1, N_DEV):
            px, py, pc = peer(k)
            pltpu.make_async_remote_copy(src_ref=v_ref, dst_ref=o_ref.at[4 * px + 2 * py + pc],
                                         send_sem=send_sems.at[k - 1], recv_sem=recv_sems.at[k - 1],
                                         device_id=(px, py, pc), device_id_type=MESH).wait_recv()
        for cp in sends:
            cp.wait_send()
        loc.wait()

    return pl.pallas_call(
        body, in_specs=[HBM_SPEC, HBM_SPEC], out_specs=HBM_SPEC,
        out_shape=jax.ShapeDtypeStruct((N_DEV,) + vec.shape, vec.dtype),
        scratch_shapes=[pltpu.SemaphoreType.DMA((N_DEV - 1,)), pltpu.SemaphoreType.DMA((N_DEV - 1,)),
                        pltpu.SemaphoreType.DMA],
        name="allgather_small")(vec, after)


def _adamw(w, g, m, v):
    m = ADAM_B1 * m + (1.0 - ADAM_B1) * g
    v = ADAM_B2 * v + (1.0 - ADAM_B2) * (g * g)
    m_hat = m / (1.0 - ADAM_B1 ** ADAM_STEP)
    v_hat = v / (1.0 - ADAM_B2 ** ADAM_STEP)
    delta = -ADAM_LR * (m_hat / (jnp.sqrt(v_hat) + ADAM_EPS) + ADAM_WD * w)
    return delta, m, v


def _reduce_adamw(parts, w, m, v, *, tr, name):
    R, C = w.shape

    def body(p_ref, w_ref, m_ref, v_ref, g_ref, d_ref, nm_ref, nv_ref):
        def core_sum(cc):
            s = p_ref[cc, 0].astype(F32)
            for k in range(1, N_CHIPS):
                s = s + p_ref[cc, k].astype(F32)
            return s

        g = core_sum(0) + core_sum(1)
        delta, nm, nv = _adamw(w_ref[...], g, m_ref[...], v_ref[...])
        g_ref[...] = g
        d_ref[...] = delta
        nm_ref[...] = nm
        nv_ref[...] = nv

    blk = pl.BlockSpec((tr, C), lambda i: (i, 0))
    return pl.pallas_call(
        body, grid=(R // tr,),
        in_specs=[pl.BlockSpec((N_CORES, N_CHIPS, tr, C), lambda i: (0, 0, i, 0)), blk, blk, blk],
        out_specs=[blk] * 4, out_shape=[jax.ShapeDtypeStruct((R, C), F32)] * 4,
        name=name, compiler_params=_cparams(("parallel",), 40))(parts, w, m, v)


def _reduce_adamw_vectors(allv, offs, ws, ms, vs):
    n = len(ws)

    def body(a_ref, *refs):
        w_refs, m_refs, v_refs = refs[:n], refs[n:2 * n], refs[2 * n:3 * n]
        tot_ref, outs = refs[3 * n], refs[3 * n + 1:]
        s = a_ref[0]
        for d in range(1, N_DEV):
            s = s + a_ref[d]
        tot_ref[...] = s
        for k in range(n):
            g = s[:, offs[k]:offs[k] + w_refs[k].shape[1]]
            delta, nm, nv = _adamw(w_refs[k][...], g, m_refs[k][...], v_refs[k][...])
            for ref, val in zip(outs[4 * k:4 * k + 4], (g, delta, nm, nv)):
                ref[...] = val

    out_shape = [jax.ShapeDtypeStruct(allv.shape[1:], F32)]
    for w in ws:
        out_shape += [jax.ShapeDtypeStruct(w.shape, F32)] * 4
    res = pl.pallas_call(body, out_shape=out_shape, name="reduce_adamw_vectors",
                         compiler_params=_cparams((), 40))(allv, *ws, *ms, *vs)
    return res[0], [tuple(res[1 + 4 * k:5 + 4 * k]) for k in range(n)]


def _adamw_taps(ws, gs, ms, vs):
    n = len(ws)

    def body(*refs):
        outs = refs[4 * n:]
        for k in range(n):
            res = _adamw(refs[k][...], refs[n + k][...], refs[2 * n + k][...], refs[3 * n + k][...])
            for ref, val in zip(outs[3 * k:3 * k + 3], res):
                ref[...] = val

    out_shape = []
    for w in ws:
        out_shape += [jax.ShapeDtypeStruct(w.shape, F32)] * 3
    res = pl.pallas_call(body, out_shape=out_shape, name="adamw_taps")(*ws, *gs, *ms, *vs)
    return [tuple(res[3 * k:3 * k + 3]) for k in range(n)]


def _pack(pieces):
    flat, offs, n = [], [], 0
    for p in pieces:
        size = -(-p.size // LANES) * LANES
        flat.append(jnp.pad(p.reshape(-1), (0, size - p.size)))
        offs.append(n)
        n += size
    return jnp.concatenate(flat).reshape(1, n), offs


def _local_step(x, target, p, wfull, on_ready=lambda group: None, before_ln0=()):
    S = x.shape[0]
    dils = [d for _, d in GROUPS]

    h0, h0b, *h0_res = _ln0_fwd(x, p["ln0_g"], p["ln0_b"], before_ln0)
    h0_rows = [h0b] + [h.reshape(S, D_MODEL) for h in h0_res]

    if isinstance(wfull, dict):
        w_in3, pending = wfull["w_in"], None
    else:
        w_in3, launch_rest, assemble = wfull
        w_in3, h0b = lax.optimization_barrier((w_in3, h0b))
        pending = launch_rest(h0b)

    w_blocks = w_in3.transpose(1, 0, 2).reshape(D_MODEL, N_BLK, GROUP_W)
    w_perm = jnp.concatenate([w_blocks[:, b] for b in PERM], axis=1)
    b_blocks = p["b_in"].reshape(N_BLK, GROUP_W)
    b_perm = jnp.concatenate([b_blocks[b] for b in PERM]).reshape(1, N_IN)
    w_nat, b_nat = w_perm[:, :N_NAT], b_perm[:, :N_NAT]
    qkv_cols = [slice(P_Q0 + g * QKV_W, P_Q0 + (g + 1) * QKV_W) for g in range(N_GROUPS)]
    w_qkv = [w_perm[:, c] for c in qkv_cols]

    proj = _mm_nn(h0b, w_nat, b_nat, tm=512, tn=N_NAT // 2, out_dtype=BF16, name="proj")
    qkv = [proj[None]]
    for g in range(1, N_GROUPS):
        t = _mm_nn(h0_rows[g], w_qkv[g], b_perm[:, qkv_cols[g]], tm=512, tn=QKV_W, out_dtype=BF16, name=f"proj_qkv{g}")
        qkv.append(t.reshape(dils[g], S // dils[g], QKV_W))
    if pending is not None:
        pending, qkv = lax.optimization_barrier((pending, qkv))
        proj = qkv[0][0]
        wfull = assemble(pending)
    w_up3 = wfull["w_up"]
    w_a, w_o, w_down, w_b = wfull["w_a"], wfull["w_o"], wfull["w_down"], wfull["w_b"]
    conv_w, ffn_conv_w = wfull["conv_w"], wfull["ffn_conv_w"]
    col0 = [P_Q0 // GROUP_W] + [0] * (N_GROUPS - 1)
    ya_in = _conv_gate_fwd(proj, conv_w)
    att = [_attn_fwd(qkv[g], col0[g], g) for g in range(N_GROUPS)]
    comb, comb_b, lse_tot = _attn_combine([a[0] for a in att], [a[1] for a in att])
    yab, mixin = _branch_mix(ya_in, comb_b, w_a, w_b, proj)
    xhat1, rstd1, h1b = _mix_ln1(mixin, w_o, p["b_o"], h0, p["ln1_g"], p["ln1_b"])
    up = _mm_nn(h1b, w_up3, p["b_up"], tm=512, tn=w_up3.shape[2], out_dtype=BF16, name="up")
    f = _ffn_conv_fwd(up, ffn_conv_w, p["ffn_conv_b"])
    dz2, dz2b, st2 = _down_ln2_loss(f, w_down, p["b_down"], xhat1, p["ln1_g"], p["ln1_b"],
                                    p["ln2_g"], p["ln2_b"], target)

    gw = {}
    gw["w_down"] = _mm_tn(f, dz2b, n_out=1, tn=D_MODEL, ts=1024, g_block=(1024, D_MODEL),
                          g_map=lambda j, s: (s, 0), name="grad_w_down").reshape(N_CHIPS, D_FF // N_CHIPS, D_MODEL)
    df = _mm_nt(dz2b, w_down, tm=512, name="df")
    dup, sm_ffn = _ffn_conv_bwd(up, df, ffn_conv_w, p["ffn_conv_b"])
    up_tn = w_up3.shape[2]
    up_pp = D_FF // up_tn
    gw["w_up"] = _mm_tn(h1b, dup, n_out=N_CHIPS, tn=up_tn, ts=1024, g_block=(None, 1024, up_tn),
                        g_map=lambda j, s: (j // up_pp, s, j % up_pp), name="grad_w_up")
    on_ready({n: gw[n] for n in ("w_down", "w_up")})
    dz1, dz1b, st1 = _up_bwd_ln1(dup, w_up3, dz2, xhat1, rstd1, p["ln1_g"])

    gw["w_o"] = _mm_tn(mixin, dz1b, n_out=1, tn=D_MODEL, ts=512, g_block=(512, D_MODEL),
                       g_map=lambda j, s: (s, 0), name="grad_w_o").reshape(N_CHIPS, D_MODEL // N_CHIPS, D_MODEL)
    dyab, dgab = _mix_bwd(dz1b, w_o, proj, yab)
    gw["w_a"] = _mm_tn(ya_in, dyab, n_out=1, tn=D_MODEL, ts=512, g_block=(512, D_MODEL),
                       g_map=lambda j, s: (s, 0), name="grad_w_a").reshape(N_CHIPS, D_CONV // N_CHIPS, D_MODEL)
    gw_b = _mm_tn(comb_b, dyab, n_out=1, tn=D_MODEL, ts=1024, g_block=(1024, D_MODEL),
                  g_map=lambda j, s: (s, 1), name="grad_w_b")
    gw["w_b"] = gw_b.reshape(GROUP_W, N_CHIPS, D_MODEL // N_CHIPS).transpose(1, 0, 2)
    on_ready({n: gw[n] for n in ("w_o", "w_a", "w_b")})
    dya_in = _mm_nt(dyab, w_a, tm=512, a_col=0, name="dya_in")
    dbch, sm_conv = _conv_gate_bwd(proj, dya_in, conv_w)
    att_stats = _comb_bwd(dyab, w_b, comb, lse_tot)
    dqkv = [_attn_bwd(qkv[g], col0[g], g, *att_stats[g]) for g in range(N_GROUPS)]

    w_pieces, b_pieces = [], []
    for nm, planes in (("bch", dbch), ("gab", dgab)):
        pw, pc = _mm_tn(h0b, planes, n_out=planes.shape[0], tn=D_MODEL, ts=1024, g_block=(None, 1024, D_MODEL),
                        g_map=lambda j, s: (j, s, 0), colsum=True, name="grad_w_in_" + nm)
        w_pieces.append(pw.transpose(1, 0, 2).reshape(D_MODEL, planes.shape[0] * D_MODEL))
        b_pieces.append(pc[0])
    for g in range(N_GROUPS):
        pw, pc = _mm_tn_cat(h0_rows[g], [a.reshape(S, GROUP_W) for a in dqkv[g]], ts=1024, name=f"grad_w_in_qkv{g}")
        w_pieces.append(pw)
        b_pieces.append(pc[0])
    dw_blocks = jnp.concatenate(w_pieces, axis=1).reshape(D_MODEL, N_BLK, GROUP_W)
    dw_ref = jnp.concatenate([dw_blocks[:, b] for b in INV_PERM], axis=1)
    gw["w_in"] = dw_ref.reshape(D_MODEL, N_CHIPS, N_IN // N_CHIPS).transpose(1, 0, 2)
    on_ready({"w_in": gw["w_in"]})
    db_blocks = jnp.concatenate(b_pieces).reshape(N_BLK, GROUP_W)
    grad_b_in = jnp.concatenate([db_blocks[b] for b in INV_PERM])

    grad_x, st0 = _in_bwd_ln0([dbch, dgab], dqkv, w_nat, w_qkv[1:], dz1, x, p["ln0_g"])

    small = {
        "loss": st2[2:3, 0:1],
        "ln0_g": st0[0], "ln0_b": st0[1], "b_in": grad_b_in, "conv_w": sm_conv[0:3],
        "b_o": st1[2], "ln1_g": st1[0], "ln1_b": st1[1],
        "b_up": jnp.concatenate([sm_ffn[0], sm_ffn[1]]), "ffn_conv_w": sm_ffn[3:6], "ffn_conv_b": sm_ffn[2],
        "b_down": st2[3], "ln2_g": st2[0], "ln2_b": st2[1],
    }
    return grad_x, gw, small


BIG = ("w_in", "w_a", "w_b", "w_o", "w_up", "w_down")
CONV = ("conv_w", "ffn_conv_w")
VECS = ("ln0_g", "ln0_b", "b_in", "b_o", "ln1_g", "ln1_b", "b_up", "ffn_conv_b", "b_down", "ln2_g", "ln2_b")
ORDER = ("ln0_g", "ln0_b", "w_in", "b_in", "conv_w", "w_a", "w_b", "w_o", "b_o", "ln1_g", "ln1_b", "w_up", "b_up",
         "ffn_conv_w", "ffn_conv_b", "w_down", "b_down", "ln2_g", "ln2_b")
SMALL_ORDER = ("loss",) + VECS + CONV


def _step(x, target, W, Mo, Vo):
    x2, t2 = x[0], target[0]
    big2 = {n: W[n][0] for n in BIG}
    halves = lambda a: a.astype(BF16).reshape(N_CORES, a.shape[0] // N_CORES, a.shape[1])
    whole = lambda g: g.reshape(N_CHIPS, g.shape[1] * g.shape[2], g.shape[3])
    later = tuple(n for n in BIG if n != "w_in")
    w_in_halves = halves(big2["w_in"])
    first = _allgather_shards([w_in_halves], [], name="allgather_w_in", collective_id=1)

    def launch_rest(h0b):
        return _allgather_shards([halves(big2[n]) for n in later] + [W[n] for n in CONV], [h0b],
                                 name="allgather_rest", collective_id=2)

    def assemble(rest):
        gathered = {n: whole(g) for n, g in zip(later + CONV, rest)}
        return {
            "w_up": gathered["w_up"],
            "w_a": gathered["w_a"].reshape(D_CONV, D_MODEL), "w_o": gathered["w_o"].reshape(D_MODEL, D_MODEL),
            "w_down": gathered["w_down"].reshape(D_FF, D_MODEL),
            "w_b": gathered["w_b"].transpose(1, 0, 2).reshape(GROUP_W, D_MODEL),
            "conv_w": gathered["conv_w"].transpose(1, 0, 2).reshape(3, D_CONV),
            "ffn_conv_w": gathered["ffn_conv_w"].transpose(1, 0, 2).reshape(3, D_FF),
        }

    pvec = {n: W[n].reshape(1, -1) for n in VECS}

    parts = {}
    exchange_ids = iter((3, 4, 5))

    def exchange(group):
        names = tuple(group)
        res = _exchange_grads([group[n] for n in names], name="exchange_" + "_".join(names),
                              collective_id=next(exchange_ids))
        parts.update(zip(names, res))

    grad_x, _, small = _local_step(x2, t2, pvec, (whole(first[0]), launch_rest, assemble), exchange,
                                   before_ln0=[w_in_halves])
    out = {}
    for n in BIG:
        tr = {"w_in": 128, "w_up": 128, "w_b": 128}.get(n, big2[n].shape[0] // 4)
        g, d, nm, nv = _reduce_adamw(parts[n], big2[n], Mo[n][0], Vo[n][0], tr=tr, name="adamw_" + n)
        out[n] = tuple(a[None] for a in (g, d, nm, nv))

    vec, offs = _pack([small[n] for n in SMALL_ORDER])
    off = dict(zip(SMALL_ORDER, offs))
    row = lambda a: a.reshape(1, -1)
    allv = _allgather_small(vec, parts["w_in"])
    tot, vec_out = _reduce_adamw_vectors(allv, [off[n] for n in VECS], [row(W[n]) for n in VECS],
                                         [row(Mo[n]) for n in VECS], [row(Vo[n]) for n in VECS])
    for n, res in zip(VECS, vec_out):
        out[n] = tuple(a.reshape(W[n].shape) for a in res)
    loss = tot[0, off["loss"]]
    chip = 2 * lax.axis_index("x") + lax.axis_index("y")
    taps_g = []
    for n in CONV:
        width = W[n].shape[2]
        full = lax.slice(tot, (0, off[n]), (1, off[n] + 3 * N_CHIPS * width)).reshape(3, N_CHIPS * width)
        taps_g.append(lax.dynamic_slice_in_dim(full, chip * width, width, axis=1))
    taps_out = _adamw_taps([W[n][0] for n in CONV], taps_g, [Mo[n][0] for n in CONV], [Vo[n][0] for n in CONV])
    for n, g, res in zip(CONV, taps_g, taps_out):
        out[n] = tuple(a[None] for a in (g,) + res)

    res = [loss, grad_x[None]]
    for k in range(4):
        res += [out[n][k] for n in ORDER]
    return tuple(res)


def kernel(x, ln0_g, ln0_b, w_in, b_in, conv_w, w_a, w_b, w_o, b_o, ln1_g, ln1_b, w_up, b_up, ffn_conv_w, ffn_conv_b, w_down, b_down, ln2_g, ln2_b, loss_target, m_ln0_g, m_ln0_b, m_w_in, m_b_in, m_conv_w, m_w_a, m_w_b, m_w_o, m_b_o, m_ln1_g, m_ln1_b, m_w_up, m_b_up, m_ffn_conv_w, m_ffn_conv_b, m_w_down, m_b_down, m_ln2_g, m_ln2_b, v_ln0_g, v_ln0_b, v_w_in, v_b_in, v_conv_w, v_w_a, v_w_b, v_w_o, v_b_o, v_ln1_g, v_ln1_b, v_w_up, v_b_up, v_ffn_conv_w, v_ffn_conv_b, v_w_down, v_b_down, v_ln2_g, v_ln2_b):
    W = dict(zip(ORDER, (ln0_g, ln0_b, w_in, b_in, conv_w, w_a, w_b, w_o, b_o, ln1_g, ln1_b, w_up, b_up,
                         ffn_conv_w, ffn_conv_b, w_down, b_down, ln2_g, ln2_b)))
    Mo = dict(zip(ORDER, (m_ln0_g, m_ln0_b, m_w_in, m_b_in, m_conv_w, m_w_a, m_w_b, m_w_o, m_b_o, m_ln1_g, m_ln1_b,
                          m_w_up, m_b_up, m_ffn_conv_w, m_ffn_conv_b, m_w_down, m_b_down, m_ln2_g, m_ln2_b)))
    Vo = dict(zip(ORDER, (v_ln0_g, v_ln0_b, v_w_in, v_b_in, v_conv_w, v_w_a, v_w_b, v_w_o, v_b_o, v_ln1_g, v_ln1_b,
                          v_w_up, v_b_up, v_ffn_conv_w, v_ffn_conv_b, v_w_down, v_b_down, v_ln2_g, v_ln2_b)))
    return _step(x, loss_target, W, Mo, Vo)
```

```python
import functools
import math

import jax
import jax.numpy as jnp
from jax import lax
from jax.experimental import pallas as pl
from jax.experimental.pallas import tpu as pltpu
from jax.experimental.pallas import tpu_sc as plsc

F32 = jnp.float32
BF16 = jnp.bfloat16

D_MODEL = 1024
D_CONV = D_MODEL
HEAD_DIM = 64
HEADS_PER_GROUP = 8
GROUPS = ((128, 1), (512, 4), (2048, 16))
N_GROUPS = len(GROUPS)
GROUP_W = HEADS_PER_GROUP * HEAD_DIM
QKV_W = N_GROUPS * GROUP_W
RADIUS = 64
D_FF = 2816
LN_EPS = 1e-5
ALPHA = 2.0 ** 0.25
MASK_VALUE = -1e30
ATT_SCALE = HEAD_DIM ** -0.5
OFF_B = 0
OFF_C = OFF_B + D_CONV
OFF_H = OFF_C + D_CONV
OFF_Q = OFF_H + D_CONV
OFF_K = OFF_Q + QKV_W
OFF_V = OFF_K + QKV_W
OFF_GA = OFF_V + QKV_W
OFF_GB = OFF_GA + D_MODEL
N_IN = OFF_GB + D_MODEL
ADAM_LR = 0.001
ADAM_B1 = 0.9
ADAM_B2 = 0.999
ADAM_EPS = 1e-08
ADAM_WD = 0.01
ADAM_STEP = 10
INV_SQRT2 = 0.7071067811865476
INV_SQRT_2PI = 0.3989422804014327

LANES = 128
SUBLANES = 8
VMEM_BYTES_V7X = 64 * 1024 * 1024
N_CHIPS = 4
N_CORES = 2
N_DEV = N_CHIPS * N_CORES
MESH = pl.DeviceIdType.MESH

N_BLK = N_IN // GROUP_W
PERM = (0, 1, 2, 3, 4, 5, 15, 16, 17, 18, 6, 9, 12, 7, 10, 13, 8, 11, 14)
INV_PERM = tuple(PERM.index(b) for b in range(N_BLK))
P_B, P_C, P_H, P_GA, P_GB, P_Q0 = 0, 1024, 2048, 3072, 4096, 5120
N_NAT = P_Q0 + QKV_W // N_GROUPS * 3
N_GATED = P_Q0

SLAB = 128
CHUNK = 256
PAD = SUBLANES
TQ = 128


def _cparams(sem, vmem_mb):
    assert vmem_mb * 1024 * 1024 < VMEM_BYTES_V7X
    return pltpu.CompilerParams(dimension_semantics=sem, vmem_limit_bytes=vmem_mb * 1024 * 1024)


def _dot(a, b):
    return jnp.dot(a, b, preferred_element_type=F32)


def _dot_nt(a, b):
    return lax.dot_general(a, b, (((1,), (1,)), ((), ())), preferred_element_type=F32)


def _dot_tn(a, b):
    return lax.dot_general(a, b, (((0,), (0,)), ((), ())), preferred_element_type=F32)


def _ln_stats(z):
    mu = jnp.mean(z, -1, keepdims=True)
    zc = z - mu
    var = jnp.mean(zc * zc, -1, keepdims=True)
    rstd = lax.rsqrt(var + LN_EPS)
    return zc * rstd, rstd


def _ln_bwd(dh, xhat, rstd, g):
    dxh = dh * g
    m1 = jnp.mean(dxh, -1, keepdims=True)
    m2 = jnp.mean(dxh * xhat, -1, keepdims=True)
    return rstd * (dxh - m1 - xhat * m2)


def _rows8(rows, width):
    pad = [jnp.zeros((1, width), F32)] * (SUBLANES - len(rows))
    return jnp.concatenate(list(rows) + pad, axis=0)


def _mm_nn(a, w, bias, *, tm, tn, out_dtype, name, vmem_mb=40):
    M, K = a.shape
    if w.ndim == 3:
        assert w.shape[2] == tn
        n_tiles = w.shape[0]
        w_spec = pl.BlockSpec((None, K, tn), lambda j, i: (j, 0, 0))
    else:
        n_tiles = w.shape[1] // tn
        w_spec = pl.BlockSpec((K, tn), lambda j, i: (0, j))

    def body(a_ref, w_ref, b_ref, o_ref):
        o_ref[...] = (_dot(a_ref[...], w_ref[...]) + b_ref[...]).astype(o_ref.dtype)

    return pl.pallas_call(
        body, grid=(n_tiles, M // tm),
        in_specs=[pl.BlockSpec((tm, K), lambda j, i: (i, 0)), w_spec, pl.BlockSpec((1, tn), lambda j, i: (0, j))],
        out_specs=pl.BlockSpec((tm, tn), lambda j, i: (i, j)),
        out_shape=jax.ShapeDtypeStruct((M, n_tiles * tn), out_dtype),
        name=name, compiler_params=_cparams(("arbitrary", "parallel"), vmem_mb))(a, w, bias)


def _mm_nt(a, w, *, tm, a_col=0, name, vmem_mb=40):
    M = a.shape[0]
    N, K = w.shape

    def body(a_ref, w_ref, o_ref):
        o_ref[...] = _dot_nt(a_ref[...], w_ref[...]).astype(o_ref.dtype)

    return pl.pallas_call(
        body, grid=(M // tm,),
        in_specs=[pl.BlockSpec((tm, K), lambda i: (i, a_col)),
                  pl.BlockSpec((N, K), lambda i: (0, 0))],
        out_specs=pl.BlockSpec((tm, N), lambda i: (i, 0)),
        out_shape=jax.ShapeDtypeStruct((M, N), BF16),
        name=name, compiler_params=_cparams(("parallel",), vmem_mb))(a, w)


def _mm_tn(a, g, *, n_out, tn, ts, g_block, g_map, colsum=False, name, vmem_mb=48):
    S, K = a.shape
    n_s = S // ts

    def body(a_ref, g_ref, *rest):
        if colsum:
            o_ref, cs_ref, acc_ref, cacc_ref = rest
        else:
            o_ref, acc_ref = rest
        s = pl.program_id(1)

        @pl.when(s == 0)
        def _():
            acc_ref[...] = jnp.zeros_like(acc_ref)
            if colsum:
                cacc_ref[...] = jnp.zeros_like(cacc_ref)

        gv = g_ref[...]
        acc_ref[...] += _dot_tn(a_ref[...], gv)
        if colsum:
            cacc_ref[...] += jnp.broadcast_to(jnp.sum(gv.astype(F32), axis=0, keepdims=True), cacc_ref.shape)

        @pl.when(s == n_s - 1)
        def _():
            o_ref[...] = acc_ref[...].astype(o_ref.dtype)
            if colsum:
                cs_ref[...] = cacc_ref[...]

    out_specs = [pl.BlockSpec((None, K, tn), lambda j, s: (j, 0, 0))]
    out_shape = [jax.ShapeDtypeStruct((n_out, K, tn), BF16)]
    scratch = [pltpu.VMEM((K, tn), F32)]
    if colsum:
        out_specs.append(pl.BlockSpec((SUBLANES, tn), lambda j, s: (0, j)))
        out_shape.append(jax.ShapeDtypeStruct((SUBLANES, n_out * tn), F32))
        scratch.append(pltpu.VMEM((SUBLANES, tn), F32))
    res = pl.pallas_call(
        body, grid=(n_out, n_s),
        in_specs=[pl.BlockSpec((ts, K), lambda j, s: (s, 0)), pl.BlockSpec(g_block, g_map)],
        out_specs=out_specs, out_shape=out_shape, scratch_shapes=scratch,
        name=name, compiler_params=_cparams(("parallel", "arbitrary"), vmem_mb))(a, g)
    return res if colsum else res[0]


def _mm_tn_cat(a, gs, *, ts, name, vmem_mb=40):
    S, K = a.shape
    widths = [g.shape[1] for g in gs]
    n_s, total = S // ts, sum(widths)

    def body(*refs):
        a_ref, g_refs = refs[0], refs[1:1 + len(gs)]
        o_ref, cs_ref, acc_ref, cacc_ref = refs[1 + len(gs):]
        s = pl.program_id(0)

        @pl.when(s == 0)
        def _():
            acc_ref[...] = jnp.zeros_like(acc_ref)
            cacc_ref[...] = jnp.zeros_like(cacc_ref)

        av, col = a_ref[...], 0
        for g_ref, w in zip(g_refs, widths):
            gv = g_ref[...]
            acc_ref[:, col:col + w] += _dot_tn(av, gv)
            cacc_ref[:, col:col + w] += jnp.broadcast_to(jnp.sum(gv.astype(F32), axis=0, keepdims=True), (SUBLANES, w))
            col += w

        @pl.when(s == n_s - 1)
        def _():
            o_ref[...] = acc_ref[...].astype(BF16)
            cs_ref[...] = cacc_ref[...]

    return pl.pallas_call(
        body, grid=(n_s,),
        in_specs=[pl.BlockSpec((ts, K), lambda s: (s, 0))] + [pl.BlockSpec((ts, w), lambda s: (s, 0)) for w in widths],
        out_specs=[pl.BlockSpec((K, total), lambda s: (0, 0)), pl.BlockSpec((SUBLANES, total), lambda s: (0, 0))],
        out_shape=[jax.ShapeDtypeStruct((K, total), BF16), jax.ShapeDtypeStruct((SUBLANES, total), F32)],
        scratch_shapes=[pltpu.VMEM((K, total), F32), pltpu.VMEM((SUBLANES, total), F32)],
        name=name, compiler_params=_cparams(("arbitrary",), vmem_mb))(a, *gs)


DILS = tuple(d for _, d in GROUPS if d > 1)


def _res_spec(d, tm, width):
    return pl.BlockSpec((d, tm // d, width), lambda i: (0, i, 0))


def _lane_scratch(tm, width):
    return [pltpu.VMEM((tm, LANES), F32)] * (width // LANES)


def _to_residue(val, dst_refs, dils, tm, dtype, scr):
    for c, ref in enumerate(scr):
        ref[...] = val[:, c * LANES:(c + 1) * LANES]
    for dst_ref, d in zip(dst_refs, dils):
        for r in range(d):
            cols = [ref[pl.ds(r, tm // d, stride=d), :] for ref in scr]
            dst_ref[r] = jnp.concatenate(cols, axis=1).astype(dtype)


def _from_residue(rows_of, d, tm, scr):
    for r in range(d):
        v = rows_of(r).astype(F32)
        for c, ref in enumerate(scr):
            ref[pl.ds(r, tm // d, stride=d), :] = v[:, c * LANES:(c + 1) * LANES]
    return jnp.concatenate([ref[...] for ref in scr], axis=1)


def _ln0_fwd(x, g, b, after=(), *, tm=512):
    S, Dm = x.shape
    n_after = len(after)

    def body(x_ref, g_ref, b_ref, *rest):
        h_ref, hb_ref, *rest = rest[n_after:]
        xhat, _ = _ln_stats(x_ref[...])
        h = xhat * g_ref[...] + b_ref[...]
        h_ref[...] = h
        hb_ref[...] = h.astype(BF16)
        _to_residue(h, rest[:len(DILS)], DILS, tm, BF16, rest[len(DILS):])

    row = pl.BlockSpec((tm, Dm), lambda i: (i, 0))
    vec = pl.BlockSpec((1, Dm), lambda i: (0, 0))
    return pl.pallas_call(
        body, grid=(S // tm,), in_specs=[row, vec, vec] + [pl.BlockSpec(memory_space=pl.ANY)] * n_after,
        out_specs=[row, row] + [_res_spec(d, tm, Dm) for d in DILS],
        out_shape=[jax.ShapeDtypeStruct((S, Dm), F32), jax.ShapeDtypeStruct((S, Dm), BF16)]
        + [jax.ShapeDtypeStruct((d, S // d, Dm), BF16) for d in DILS],
        scratch_shapes=_lane_scratch(tm, Dm),
        name="ln0_fwd", compiler_params=_cparams(("parallel",), 32))(x, g, b, *after)


def _slab_spec(S, col0):
    return pl.BlockSpec((S, SLAB), lambda j: (0, col0 // SLAB + j))


def _zero_pads(scr, S):
    scr[0:PAD, :] = jnp.zeros((PAD, SLAB), F32)
    scr[S + PAD:S + 2 * PAD, :] = jnp.zeros((PAD, SLAB), F32)


def _shifted(scr, t):
    return (scr[PAD - 1 + t:PAD - 1 + t + CHUNK, :], scr[PAD + t:PAD + t + CHUNK, :],
            scr[PAD + 1 + t:PAD + 1 + t + CHUNK, :])


def _conv_gate_fwd(proj, conv_w):
    S = proj.shape[0]

    def body(b_ref, c_ref, h_ref, w_ref, o_ref, u_scr):
        _zero_pads(u_scr, S)
        for t in range(0, S, CHUNK):
            u_scr[PAD + t:PAD + t + CHUNK, :] = c_ref[t:t + CHUNK, :].astype(F32) * h_ref[t:t + CHUNK, :].astype(F32)
        w0, w1, w2 = w_ref[0:1, :], w_ref[1:2, :], w_ref[2:3, :]
        for t in range(0, S, CHUNK):
            um, u0, up = _shifted(u_scr, t)
            cv = w0 * um + w1 * u0 + w2 * up
            o_ref[t:t + CHUNK, :] = (b_ref[t:t + CHUNK, :].astype(F32) * cv).astype(BF16)

    return pl.pallas_call(
        body, grid=(D_CONV // SLAB,),
        in_specs=[_slab_spec(S, P_B), _slab_spec(S, P_C), _slab_spec(S, P_H),
                  pl.BlockSpec((3, SLAB), lambda j: (0, j))],
        out_specs=pl.BlockSpec((S, SLAB), lambda j: (0, j)),
        out_shape=jax.ShapeDtypeStruct((S, D_CONV), BF16),
        scratch_shapes=[pltpu.VMEM((S + 2 * PAD, SLAB), F32)],
        name="conv_gate_fwd", compiler_params=_cparams(("parallel",), 40))(proj, proj, proj, conv_w)


MASKED_DISTANCE = -1e34


def _attn_bias_table(g):
    dil = GROUPS[g][1]
    j = lax.broadcasted_iota(jnp.int32, (2 * TQ, TQ), 0)
    a = lax.broadcasted_iota(jnp.int32, (2 * TQ, TQ), 1)
    rel = jnp.abs(j - RADIUS - a)
    base = -(rel * dil).astype(F32)
    inside, after_start, before_end = rel <= RADIUS, j >= RADIUS, j < TQ + RADIUS
    variants = []
    for first, last in ((False, False), (True, False), (False, True), (True, True)):
        valid = inside & (after_start if first else True) & (before_end if last else True)
        variants.append(jnp.where(valid, base, MASKED_DISTANCE))
    return jnp.stack(variants)


def _bias_spec(nb):
    def variant(r, i):
        return (jnp.where(i == 0, 1, 0) + jnp.where(i == nb - 1, 2, 0), 0, 0)
    return pl.BlockSpec((None, 2 * TQ, TQ), variant)


def _head_stats(rows):
    pad = jnp.zeros((LANES - len(rows), TQ), F32)
    return jnp.concatenate(list(rows) + [pad], axis=0).T


def _slope(g, h):
    return 2.0 ** (-8.0 * (g * HEADS_PER_GROUP + h + 1) / (N_GROUPS * HEADS_PER_GROUP))


def _window(p_ref, c_ref, n_ref):
    return jnp.concatenate([p_ref[TQ - RADIUS:, :], c_ref[...], n_ref[:RADIUS, :]], axis=0)


def _pair(a, h):
    return a[:, (h // 2) * LANES:(h // 2 + 1) * LANES]


def _own_lanes(a, h):
    lane = lax.broadcasted_iota(jnp.int32, a.shape, 1)
    return jnp.where((lane >= HEAD_DIM) == (h % 2 == 1), a, jnp.zeros_like(a))


def _own_rows(a, h):
    return a[(h % 2) * HEAD_DIM:(h % 2 + 1) * HEAD_DIM, :]


def _qkv_specs(nb, col0):
    def spec(col, shift):
        return pl.BlockSpec((None, TQ, GROUP_W), lambda r, i: (r, jnp.clip(i + shift, 0, nb - 1), col))

    return [spec(col0, 0), spec(col0 + 1, -1), spec(col0 + 1, 0), spec(col0 + 1, 1),
            spec(col0 + 2, -1), spec(col0 + 2, 0), spec(col0 + 2, 1)]


def _attn_fwd(qkv, col0, g):
    dil, sub, _ = qkv.shape
    nb = sub // TQ

    def body(q_ref, kp, kc, kn, vp, vc, vn, bias_ref, o_ref, lse_ref, ot_scr, s_scr, p_scr):
        kwin = _window(kp, kc, kn)
        vwin = _window(vp, vc, vn)
        q = q_ref[...] * ATT_SCALE
        for h in range(HEADS_PER_GROUP):
            s_scr[h] = _dot_nt(_pair(kwin, h), _own_lanes(_pair(q, h), h))
        lse, inv_den = [], []
        for h in range(HEADS_PER_GROUP):
            s = s_scr[h] + _slope(g, h) * bias_ref[...]
            m = jnp.max(s, axis=0, keepdims=True)
            p = jnp.exp(s - m)
            den = jnp.sum(p, axis=0, keepdims=True)
            p_scr[h] = p.astype(BF16)
            inv_den.append(1.0 / den)
            lse.append(m + jnp.log(den))
        for h in range(HEADS_PER_GROUP):
            ot = _dot_tn(_pair(vwin, h), p_scr[h])
            ot_scr[h * HEAD_DIM:(h + 1) * HEAD_DIM, :] = _own_rows(ot, h) * inv_den[h]
        o_ref[...] = ot_scr[...].T
        lse_ref[...] = _head_stats(lse)

    return pl.pallas_call(
        body, grid=(dil, nb), in_specs=_qkv_specs(nb, col0) + [_bias_spec(nb)],
        out_specs=[pl.BlockSpec((None, TQ, GROUP_W), lambda r, i: (r, i, 0)),
                   pl.BlockSpec((None, TQ, LANES), lambda r, i: (r, i, 0))],
        out_shape=[jax.ShapeDtypeStruct((dil, sub, GROUP_W), F32), jax.ShapeDtypeStruct((dil, sub, LANES), F32)],
        scratch_shapes=[pltpu.VMEM((GROUP_W, TQ), F32), pltpu.VMEM((HEADS_PER_GROUP, 2 * TQ, TQ), F32),
                        pltpu.VMEM((HEADS_PER_GROUP, 2 * TQ, TQ), BF16)],
        name=f"attn_fwd_g{g}", compiler_params=_cparams(("parallel", "arbitrary"), 32))(
            *([qkv] * 7), _attn_bias_table(g))


def _expand_heads():
    h = lax.broadcasted_iota(jnp.int32, (LANES, GROUP_W), 0)
    c = lax.broadcasted_iota(jnp.int32, (LANES, GROUP_W), 1)
    return (c // HEAD_DIM == h).astype(F32)


def _dot_f32(a, b):
    return jnp.dot(a, b, preferred_element_type=F32, precision=lax.Precision.HIGHEST)


def _attn_combine(outs, lses, *, tm=512):
    S = outs[0].shape[1]
    n_col = GROUP_W // LANES

    def body(*refs):
        ins, e_ref = refs[:2 * N_GROUPS], refs[2 * N_GROUPS]
        c_ref, cb_ref, lt_ref = refs[2 * N_GROUPS + 1:2 * N_GROUPS + 4]
        scr = refs[2 * N_GROUPS + 4:]
        o, l = [ins[0][0]], [ins[N_GROUPS][0]]
        for k, d in enumerate(DILS):
            o_ref, l_ref = ins[1 + k], ins[N_GROUPS + 1 + k]
            o.append(_from_residue(lambda r: o_ref[r], d, tm, scr[k * (n_col + 1):k * (n_col + 1) + n_col]))
            l.append(_from_residue(lambda r: l_ref[r], d, tm, scr[k * (n_col + 1) + n_col:(k + 1) * (n_col + 1)]))
        m = jnp.maximum(jnp.maximum(l[0], l[1]), l[2])
        e = [jnp.exp(v - m) for v in l]
        den = e[0] + e[1] + e[2]
        comb = sum(_dot_f32(ev / den, e_ref[...]) * ov for ev, ov in zip(e, o))
        c_ref[...] = comb
        cb_ref[...] = comb.astype(BF16)
        lt_ref[...] = m + jnp.log(den)

    row = pl.BlockSpec((tm, GROUP_W), lambda i: (i, 0))
    dils = [d for _, d in GROUPS]
    return pl.pallas_call(
        body, grid=(S // tm,),
        in_specs=[_res_spec(d, tm, GROUP_W) for d in dils] + [_res_spec(d, tm, LANES) for d in dils]
        + [_resident((LANES, GROUP_W))],
        out_specs=[row, row, pl.BlockSpec((tm, LANES), lambda i: (i, 0))],
        out_shape=[jax.ShapeDtypeStruct((S, GROUP_W), F32), jax.ShapeDtypeStruct((S, GROUP_W), BF16),
                   jax.ShapeDtypeStruct((S, LANES), F32)],
        scratch_shapes=_lane_scratch(tm, GROUP_W + LANES) * len(DILS),
        name="attn_combine", compiler_params=_cparams(("parallel",), 32))(*outs, *lses, _expand_heads())


def _branch_mix(ya_in, comb_b, w_a, w_b, proj, *, tm=512):
    S = ya_in.shape[0]

    def body(ya_ref, cb_ref, wa_ref, wb_ref, ga_ref, gb_ref, yab_ref, mx_ref):
        y_a = _dot(ya_ref[...], wa_ref[...])
        y_b = _dot(cb_ref[...], wb_ref[...])
        yab_ref[:, 0:D_MODEL] = y_a.astype(BF16)
        yab_ref[:, D_MODEL:2 * D_MODEL] = y_b.astype(BF16)
        mx = jax.nn.sigmoid(ga_ref[...].astype(F32)) * y_a + jax.nn.sigmoid(gb_ref[...].astype(F32)) * y_b
        mx_ref[...] = mx.astype(BF16)

    return pl.pallas_call(
        body, grid=(S // tm,),
        in_specs=[pl.BlockSpec((tm, D_CONV), lambda i: (i, 0)), pl.BlockSpec((tm, GROUP_W), lambda i: (i, 0)),
                  pl.BlockSpec((D_CONV, D_MODEL), lambda i: (0, 0)), pl.BlockSpec((GROUP_W, D_MODEL), lambda i: (0, 0)),
                  pl.BlockSpec((tm, D_MODEL), lambda i: (i, P_GA // D_MODEL)),
                  pl.BlockSpec((tm, D_MODEL), lambda i: (i, P_GB // D_MODEL))],
        out_specs=[pl.BlockSpec((tm, 2 * D_MODEL), lambda i: (i, 0)), pl.BlockSpec((tm, D_MODEL), lambda i: (i, 0))],
        out_shape=[jax.ShapeDtypeStruct((S, 2 * D_MODEL), BF16), jax.ShapeDtypeStruct((S, D_MODEL), BF16)],
        name="branch_mix", compiler_params=_cparams(("parallel",), 40))(ya_in, comb_b, w_a, w_b, proj, proj)


def _mix_ln1(mixin, w_o, b_o, h0, g1, b1, *, tm=512):
    S = mixin.shape[0]

    def body(mx_ref, wo_ref, bo_ref, h0_ref, g_ref, b_ref, xh_ref, rs_ref, h1b_ref):
        z = ALPHA * h0_ref[...] + _dot(mx_ref[...], wo_ref[...]) + bo_ref[...]
        xhat, rstd = _ln_stats(z)
        xh_ref[...] = xhat
        rs_ref[...] = jnp.broadcast_to(rstd, (tm, LANES))
        h1b_ref[...] = (xhat * g_ref[...] + b_ref[...]).astype(BF16)

    row = pl.BlockSpec((tm, D_MODEL), lambda i: (i, 0))
    vec = pl.BlockSpec((1, D_MODEL), lambda i: (0, 0))
    return pl.pallas_call(
        body, grid=(S // tm,),
        in_specs=[row, pl.BlockSpec((D_MODEL, D_MODEL), lambda i: (0, 0)), vec, row, vec, vec],
        out_specs=[row, pl.BlockSpec((tm, LANES), lambda i: (i, 0)), row],
        out_shape=[jax.ShapeDtypeStruct((S, D_MODEL), F32), jax.ShapeDtypeStruct((S, LANES), F32),
                   jax.ShapeDtypeStruct((S, D_MODEL), BF16)],
        name="mix_ln1", compiler_params=_cparams(("parallel",), 40))(mixin, w_o, b_o, h0, g1, b1)


def _gelu_parts(cz):
    cdf = 0.5 * (1.0 + lax.erf(cz * INV_SQRT2))
    return cdf, cz * cdf


def _ffn_conv_fwd(up, cw, cb):
    S = up.shape[0]

    def body(a_ref, g_ref, w_ref, cb_ref, o_ref, a_scr):
        _zero_pads(a_scr, S)
        for t in range(0, S, CHUNK):
            a_scr[PAD + t:PAD + t + CHUNK, :] = a_ref[t:t + CHUNK, :].astype(F32)
        w0, w1, w2 = w_ref[0:1, :], w_ref[1:2, :], w_ref[2:3, :]
        for t in range(0, S, CHUNK):
            am, a0, ap = _shifted(a_scr, t)
            _, gel = _gelu_parts(w0 * am + w1 * a0 + w2 * ap + cb_ref[...])
            o_ref[t:t + CHUNK, :] = (gel * g_ref[t:t + CHUNK, :].astype(F32)).astype(BF16)

    return pl.pallas_call(
        body, grid=(D_FF // SLAB,),
        in_specs=[_slab_spec(S, 0), _slab_spec(S, D_FF), pl.BlockSpec((3, SLAB), lambda j: (0, j)),
                  pl.BlockSpec((1, SLAB), lambda j: (0, j))],
        out_specs=pl.BlockSpec((S, SLAB), lambda j: (0, j)),
        out_shape=jax.ShapeDtypeStruct((S, D_FF), BF16),
        scratch_shapes=[pltpu.VMEM((S + 2 * PAD, SLAB), F32)],
        name="ffn_conv_fwd", compiler_params=_cparams(("parallel",), 40))(up, up, cw, cb)


def _down_ln2_loss(f, w_down, b_down, xhat1, g1, b1, g2, b2, target, *, tm=512):
    S = f.shape[0]

    def body(f_ref, wd_ref, bd_ref, xh1_ref, g1_ref, b1_ref, g2_ref, b2_ref, t_ref, dz_ref, dzb_ref, st_ref):
        h1 = xh1_ref[...] * g1_ref[...] + b1_ref[...]
        z = ALPHA * h1 + _dot(f_ref[...], wd_ref[...]) + bd_ref[...]
        xhat, rstd = _ln_stats(z)
        err = xhat * g2_ref[...] + b2_ref[...] - t_ref[...]
        loss = (0.5 / D_MODEL) * jnp.sum(jnp.sum(err * err, axis=1, keepdims=True), axis=0, keepdims=True)
        dh2 = err * (1.0 / D_MODEL)
        dz = _ln_bwd(dh2, xhat, rstd, g2_ref[...])
        dz_ref[...] = dz
        dzb_ref[...] = dz.astype(BF16)
        upd = _rows8([jnp.sum(dh2 * xhat, axis=0, keepdims=True), jnp.sum(dh2, axis=0, keepdims=True),
                      jnp.broadcast_to(loss, (1, D_MODEL)), jnp.sum(dz, axis=0, keepdims=True)], D_MODEL)

        @pl.when(pl.program_id(0) == 0)
        def _():
            st_ref[...] = upd

        @pl.when(pl.program_id(0) != 0)
        def _():
            st_ref[...] += upd

    row = pl.BlockSpec((tm, D_MODEL), lambda i: (i, 0))
    vec = pl.BlockSpec((1, D_MODEL), lambda i: (0, 0))
    return pl.pallas_call(
        body, grid=(S // tm,),
        in_specs=[pl.BlockSpec((tm, D_FF), lambda i: (i, 0)), _resident((D_FF, D_MODEL)),
                  vec, row, vec, vec, vec, vec, row],
        out_specs=[row, row, pl.BlockSpec((SUBLANES, D_MODEL), lambda i: (0, 0))],
        out_shape=[jax.ShapeDtypeStruct((S, D_MODEL), F32), jax.ShapeDtypeStruct((S, D_MODEL), BF16),
                   jax.ShapeDtypeStruct((SUBLANES, D_MODEL), F32)],
        name="down_ln2_loss", compiler_params=_cparams(("arbitrary",), 56))(
            f, w_down, b_down, xhat1, g1, b1, g2, b2, target)


def _ffn_conv_bwd(up, df, cw, cb):
    S = up.shape[0]

    def body(a_ref, g_ref, df_ref, w_ref, cb_ref, dup_ref, sm_ref, a_scr, d_scr):
        _zero_pads(a_scr, S)
        _zero_pads(d_scr, S)
        for t in range(0, S, CHUNK):
            a_scr[PAD + t:PAD + t + CHUNK, :] = a_ref[t:t + CHUNK, :].astype(F32)
        w0, w1, w2 = w_ref[0:1, :], w_ref[1:2, :], w_ref[2:3, :]
        zero = jnp.zeros((1, SLAB), F32)
        s_dg, s_dcz, s_w0, s_w1, s_w2 = zero, zero, zero, zero, zero
        for t in range(0, S, CHUNK):
            am, a0, ap = _shifted(a_scr, t)
            cz = w0 * am + w1 * a0 + w2 * ap + cb_ref[...]
            cdf, gel = _gelu_parts(cz)
            dfv = df_ref[t:t + CHUNK, :].astype(F32)
            dgte = dfv * gel
            dcz = dfv * g_ref[t:t + CHUNK, :].astype(F32) * (cdf + cz * jnp.exp(-0.5 * cz * cz) * INV_SQRT_2PI)
            dup_ref[1, t:t + CHUNK, :] = dgte.astype(BF16)
            d_scr[PAD + t:PAD + t + CHUNK, :] = dcz
            s_dg = s_dg + jnp.sum(dgte, axis=0, keepdims=True)
            s_dcz = s_dcz + jnp.sum(dcz, axis=0, keepdims=True)
            s_w0 = s_w0 + jnp.sum(dcz * am, axis=0, keepdims=True)
            s_w1 = s_w1 + jnp.sum(dcz * a0, axis=0, keepdims=True)
            s_w2 = s_w2 + jnp.sum(dcz * ap, axis=0, keepdims=True)
        s_da = zero
        for t in range(0, S, CHUNK):
            dm, d0, dp = _shifted(d_scr, t)
            da = w0 * dp + w1 * d0 + w2 * dm
            dup_ref[0, t:t + CHUNK, :] = da.astype(BF16)
            s_da = s_da + jnp.sum(da, axis=0, keepdims=True)
        sm_ref[...] = _rows8([s_da, s_dg, s_dcz, s_w0, s_w1, s_w2], SLAB)

    return pl.pallas_call(
        body, grid=(D_FF // SLAB,),
        in_specs=[_slab_spec(S, 0), _slab_spec(S, D_FF), pl.BlockSpec((S, SLAB), lambda j: (0, j)),
                  pl.BlockSpec((3, SLAB), lambda j: (0, j)), pl.BlockSpec((1, SLAB), lambda j: (0, j))],
        out_specs=[pl.BlockSpec((2, S, SLAB), lambda j: (0, 0, j)), pl.BlockSpec((SUBLANES, SLAB), lambda j: (0, j))],
        out_shape=[jax.ShapeDtypeStruct((2, S, D_FF), BF16), jax.ShapeDtypeStruct((SUBLANES, D_FF), F32)],
        scratch_shapes=[pltpu.VMEM((S + 2 * PAD, SLAB), F32)] * 2,
        name="ffn_conv_bwd", compiler_params=_cparams(("parallel",), 48))(up, up, df, cw, cb)


def _resident(shape):
    nd = len(shape)
    return pl.BlockSpec(shape, lambda *_: (0,) * nd, pipeline_mode=pl.Buffered(1))


def _up_bwd_ln1(dup, w_up3, dz2, xhat1, rstd1, g1, *, tm=512):
    S = dz2.shape[0]
    ns, _, tk = w_up3.shape
    per_plane = D_FF // tk

    def body(du_ref, w_ref, dz2_ref, xh_ref, rs_ref, g_ref, dz_ref, dzb_ref, st_ref):
        dh = ALPHA * dz2_ref[...]
        for k in range(ns):
            col = (k % per_plane) * tk
            dh = dh + _dot_nt(du_ref[k // per_plane, :, col:col + tk], w_ref[k])
        xhat = xh_ref[...]
        dz = _ln_bwd(dh, xhat, rs_ref[:, 0:1], g_ref[...])
        dz_ref[...] = dz
        dzb_ref[...] = dz.astype(BF16)
        upd = _rows8([jnp.sum(dh * xhat, axis=0, keepdims=True), jnp.sum(dh, axis=0, keepdims=True),
                      jnp.sum(dz, axis=0, keepdims=True)], D_MODEL)

        @pl.when(pl.program_id(0) == 0)
        def _():
            st_ref[...] = upd

        @pl.when(pl.program_id(0) != 0)
        def _():
            st_ref[...] += upd

    row = pl.BlockSpec((tm, D_MODEL), lambda i: (i, 0))
    return pl.pallas_call(
        body, grid=(S // tm,),
        in_specs=[pl.BlockSpec((dup.shape[0], tm, D_FF), lambda i: (0, i, 0)), _resident(w_up3.shape),
                  row, row, pl.BlockSpec((tm, LANES), lambda i: (i, 0)), pl.BlockSpec((1, D_MODEL), lambda i: (0, 0))],
        out_specs=[row, row, pl.BlockSpec((SUBLANES, D_MODEL), lambda i: (0, 0))],
        out_shape=[jax.ShapeDtypeStruct((S, D_MODEL), F32), jax.ShapeDtypeStruct((S, D_MODEL), BF16),
                   jax.ShapeDtypeStruct((SUBLANES, D_MODEL), F32)],
        name="up_bwd_ln1", compiler_params=_cparams(("arbitrary",), 56))(dup, w_up3, dz2, xhat1, rstd1, g1)


def _mix_bwd(dz1b, w_o, proj, yab, *, tm=512):
    S = dz1b.shape[0]

    def body(dz_ref, wo_ref, ga_ref, gb_ref, y_ref, dy_ref, dg_ref):
        dmx = _dot_nt(dz_ref[...], wo_ref[...])
        for k, gt_ref in enumerate((ga_ref, gb_ref)):
            sl = slice(k * D_MODEL, (k + 1) * D_MODEL)
            sg = jax.nn.sigmoid(gt_ref[...].astype(F32))
            dy_ref[:, sl] = (dmx * sg).astype(BF16)
            dg_ref[k] = (dmx * y_ref[:, sl].astype(F32) * sg * (1.0 - sg)).astype(BF16)

    row = pl.BlockSpec((tm, D_MODEL), lambda i: (i, 0))
    wide = pl.BlockSpec((tm, 2 * D_MODEL), lambda i: (i, 0))
    return pl.pallas_call(
        body, grid=(S // tm,),
        in_specs=[row, _resident(w_o.shape), pl.BlockSpec((tm, D_MODEL), lambda i: (i, P_GA // D_MODEL)),
                  pl.BlockSpec((tm, D_MODEL), lambda i: (i, P_GB // D_MODEL)), wide],
        out_specs=[wide, pl.BlockSpec((2, tm, D_MODEL), lambda i: (0, i, 0))],
        out_shape=[jax.ShapeDtypeStruct((S, 2 * D_MODEL), BF16), jax.ShapeDtypeStruct((2, S, D_MODEL), BF16)],
        name="mix_bwd", compiler_params=_cparams(("parallel",), 40))(dz1b, w_o, proj, proj, yab)


def _conv_gate_bwd(proj, dya_in, conv_w):
    S = proj.shape[0]

    def body(b_ref, c_ref, h_ref, dy_ref, w_ref, o_ref, sm_ref, u_scr, d_scr):
        _zero_pads(u_scr, S)
        _zero_pads(d_scr, S)
        for t in range(0, S, CHUNK):
            u_scr[PAD + t:PAD + t + CHUNK, :] = c_ref[t:t + CHUNK, :].astype(F32) * h_ref[t:t + CHUNK, :].astype(F32)
        w0, w1, w2 = w_ref[0:1, :], w_ref[1:2, :], w_ref[2:3, :]
        zero = jnp.zeros((1, SLAB), F32)
        s_w0, s_w1, s_w2 = zero, zero, zero
        for t in range(0, S, CHUNK):
            um, u0, up = _shifted(u_scr, t)
            dy = dy_ref[t:t + CHUNK, :].astype(F32)
            o_ref[0, t:t + CHUNK, :] = (dy * (w0 * um + w1 * u0 + w2 * up)).astype(BF16)
            dcv = dy * b_ref[t:t + CHUNK, :].astype(F32)
            d_scr[PAD + t:PAD + t + CHUNK, :] = dcv
            s_w0 = s_w0 + jnp.sum(dcv * um, axis=0, keepdims=True)
            s_w1 = s_w1 + jnp.sum(dcv * u0, axis=0, keepdims=True)
            s_w2 = s_w2 + jnp.sum(dcv * up, axis=0, keepdims=True)
        for t in range(0, S, CHUNK):
            dm, d0, dp = _shifted(d_scr, t)
            du = w0 * dp + w1 * d0 + w2 * dm
            o_ref[1, t:t + CHUNK, :] = (du * h_ref[t:t + CHUNK, :].astype(F32)).astype(BF16)
            o_ref[2, t:t + CHUNK, :] = (du * c_ref[t:t + CHUNK, :].astype(F32)).astype(BF16)
        sm_ref[...] = _rows8([s_w0, s_w1, s_w2], SLAB)

    return pl.pallas_call(
        body, grid=(D_CONV // SLAB,),
        in_specs=[_slab_spec(S, P_B), _slab_spec(S, P_C), _slab_spec(S, P_H),
                  pl.BlockSpec((S, SLAB), lambda j: (0, j)), pl.BlockSpec((3, SLAB), lambda j: (0, j))],
        out_specs=[pl.BlockSpec((3, S, SLAB), lambda j: (0, 0, j)), pl.BlockSpec((SUBLANES, SLAB), lambda j: (0, j))],
        out_shape=[jax.ShapeDtypeStruct((3, S, D_CONV), BF16), jax.ShapeDtypeStruct((SUBLANES, D_CONV), F32)],
        scratch_shapes=[pltpu.VMEM((S + 2 * PAD, SLAB), F32)] * 2,
        name="conv_gate_bwd", compiler_params=_cparams(("parallel",), 48))(proj, proj, proj, dya_in, conv_w)


def _comb_bwd(dyab, w_b, comb, lse_tot, *, tm=512):
    S = comb.shape[0]
    widths, dtypes = (GROUP_W, LANES, LANES), (BF16, F32, F32)

    def body(dy_ref, wb_ref, c_ref, lt_ref, e_ref, *rest):
        outs, scr = rest[:3 * N_GROUPS], rest[3 * N_GROUPS:]
        dcb = _dot_nt(dy_ref[...], wb_ref[...]).astype(BF16)
        dc = dcb.astype(F32)
        delta = lax.dot_general(dc * c_ref[...], e_ref[...], (((1,), (1,)), ((), ())),
                                preferred_element_type=F32, precision=lax.Precision.HIGHEST)
        for k, (val, dtype) in enumerate(zip((dc, lt_ref[...], delta), dtypes)):
            outs[k][0] = val.astype(dtype)
            _to_residue(val, [outs[3 * (1 + j) + k] for j in range(len(DILS))], DILS, tm, dtype,
                        scr[:val.shape[1] // LANES])

    out_specs, out_shape = [], []
    for _, d in GROUPS:
        out_specs += [_res_spec(d, tm, w) for w in widths]
        out_shape += [jax.ShapeDtypeStruct((d, S // d, w), t) for w, t in zip(widths, dtypes)]
    res = pl.pallas_call(
        body, grid=(S // tm,),
        in_specs=[pl.BlockSpec((tm, D_MODEL), lambda i: (i, 1)), _resident(w_b.shape),
                  pl.BlockSpec((tm, GROUP_W), lambda i: (i, 0)), pl.BlockSpec((tm, LANES), lambda i: (i, 0)),
                  _resident((LANES, GROUP_W))],
        out_specs=out_specs, out_shape=out_shape, scratch_shapes=_lane_scratch(tm, GROUP_W),
        name="comb_bwd", compiler_params=_cparams(("parallel",), 32))(dyab, w_b, comb, lse_tot, _expand_heads())
    return [tuple(res[3 * g:3 * g + 3]) for g in range(N_GROUPS)]


def _attn_bwd(qkv, col0, g, dcomb, lse_tot, delta):
    dil, sub, _ = qkv.shape
    nb = sub // TQ

    def body(q_ref, kp, kc, kn, vp, vc, vn, do_ref, lse_ref, dl_ref, bias_ref, dq_ref, dk_ref, dv_ref,
             ak, av, dqt_scr, s_scr, dp_scr, ds_scr, p_scr):
        i = pl.program_id(1)

        @pl.when(i == 0)
        def _():
            ak[...] = jnp.zeros_like(ak)
            av[...] = jnp.zeros_like(av)

        @pl.when(i < nb)
        def _():
            kwin = _window(kp, kc, kn)
            vwin = _window(vp, vc, vn)
            q = q_ref[...] * ATT_SCALE
            do = do_ref[...]
            lse_t, dl_t = lse_ref[...].T, dl_ref[...].T
            for h in range(HEADS_PER_GROUP):
                s_scr[h] = _dot_nt(_pair(kwin, h), _own_lanes(_pair(q, h), h))
                dp_scr[h] = _dot_nt(_pair(vwin, h), _own_lanes(_pair(do, h), h))
            for h in range(HEADS_PER_GROUP):
                p = jnp.exp(s_scr[h] + _slope(g, h) * bias_ref[...] - lse_t[h:h + 1, :])
                ds_scr[h] = (p * (dp_scr[h] - dl_t[h:h + 1, :])).astype(BF16)
                p_scr[h] = p.astype(BF16)
            for h in range(HEADS_PER_GROUP):
                dqt_scr[h * HEAD_DIM:(h + 1) * HEAD_DIM, :] = _own_rows(_dot_tn(_pair(kwin, h), ds_scr[h]), h)
            for h in range(0, HEADS_PER_GROUP, 2):
                cols = slice(h * HEAD_DIM, (h + 2) * HEAD_DIM)
                q2 = jnp.concatenate([_own_lanes(_pair(q, h), h), _own_lanes(_pair(q, h), h + 1)], axis=0)
                do2 = jnp.concatenate([_own_lanes(_pair(do, h), h), _own_lanes(_pair(do, h), h + 1)], axis=0)
                ak[RADIUS:RADIUS + 2 * TQ, cols] += _dot(jnp.concatenate([ds_scr[h], ds_scr[h + 1]], axis=1), q2)
                av[RADIUS:RADIUS + 2 * TQ, cols] += _dot(jnp.concatenate([p_scr[h], p_scr[h + 1]], axis=1), do2)
            dq_ref[...] = (dqt_scr[...].T * ATT_SCALE).astype(BF16)

        dk_ref[...] = ak[0:TQ, :].astype(BF16)
        dv_ref[...] = av[0:TQ, :].astype(BF16)
        ak[0:2 * TQ, :] = ak[TQ:3 * TQ, :]
        av[0:2 * TQ, :] = av[TQ:3 * TQ, :]
        ak[2 * TQ:3 * TQ, :] = jnp.zeros((TQ, GROUP_W), F32)
        av[2 * TQ:3 * TQ, :] = jnp.zeros((TQ, GROUP_W), F32)

    tok = pl.BlockSpec((None, TQ, GROUP_W), lambda r, i: (r, jnp.minimum(i, nb - 1), 0))
    stat = pl.BlockSpec((None, TQ, LANES), lambda r, i: (r, jnp.minimum(i, nb - 1), 0))
    dkv_spec = pl.BlockSpec((None, TQ, GROUP_W), lambda r, i: (r, jnp.maximum(i - 1, 0), 0))
    return pl.pallas_call(
        body, grid=(dil, nb + 1), in_specs=_qkv_specs(nb, col0) + [tok, stat, stat, _bias_spec(nb)],
        out_specs=[tok, dkv_spec, dkv_spec], out_shape=[jax.ShapeDtypeStruct((dil, sub, GROUP_W), BF16)] * 3,
        scratch_shapes=[pltpu.VMEM((3 * TQ, GROUP_W), F32)] * 2 + [pltpu.VMEM((GROUP_W, TQ), F32)]
        + [pltpu.VMEM((HEADS_PER_GROUP, 2 * TQ, TQ), F32)] * 2 + [pltpu.VMEM((HEADS_PER_GROUP, 2 * TQ, TQ), BF16)] * 2,
        name=f"attn_bwd_g{g}", compiler_params=_cparams(("arbitrary", "arbitrary"), 32))(
            *([qkv] * 7), dcomb, lse_tot, delta, _attn_bias_table(g))


def _in_bwd_ln0(dgated, dqkv, w_nat, w_dil, dz1, x, g0, *, tm=256):
    S = x.shape[0]
    n_gated, n_in = len(dgated), 3 * N_GROUPS

    def body(*refs):
        g_refs, d_refs = refs[:n_gated], refs[n_gated:n_gated + n_in]
        wn_ref, *wd_refs = refs[n_gated + n_in:n_gated + n_in + N_GROUPS]
        dz_ref, x_ref, g_ref, gx_ref, st_ref, *tmp_ref = refs[n_gated + n_in + N_GROUPS:]
        dh = ALPHA * dz_ref[...]
        col = 0
        for ref in g_refs:
            for k in range(ref.shape[0]):
                dh = dh + _dot_nt(ref[k], wn_ref[:, col:col + D_MODEL])
                col += D_MODEL
        for g, (_, d) in enumerate(GROUPS):
            rows = [jnp.concatenate([d_refs[3 * g + k][r] for k in range(3)], axis=1) for r in range(d)]
            w = wn_ref[:, col:col + QKV_W] if d == 1 else wd_refs[g - 1][...]
            res = _dot_nt(jnp.concatenate(rows, axis=0), w)
            if d == 1:
                dh = dh + res
            else:
                n = tm // d
                dh = dh + _from_residue(lambda r: res[r * n:(r + 1) * n, :], d, tm, tmp_ref)
        xhat, rstd = _ln_stats(x_ref[...])
        gx_ref[...] = _ln_bwd(dh, xhat, rstd, g_ref[...])
        upd = _rows8([jnp.sum(dh * xhat, axis=0, keepdims=True), jnp.sum(dh, axis=0, keepdims=True)], D_MODEL)

        @pl.when(pl.program_id(0) == 0)
        def _():
            st_ref[...] = upd

        @pl.when(pl.program_id(0) != 0)
        def _():
            st_ref[...] += upd

    row = pl.BlockSpec((tm, D_MODEL), lambda i: (i, 0))
    g_specs = [pl.BlockSpec((a.shape[0], tm, D_MODEL), lambda i: (0, i, 0)) for a in dgated]
    d_specs = []
    for _, d in GROUPS:
        d_specs += [_res_spec(d, tm, GROUP_W)] * 3
    operands = list(dgated) + [a for grp in dqkv for a in grp] + [w_nat] + list(w_dil) + [dz1, x, g0]
    return pl.pallas_call(
        body, grid=(S // tm,),
        in_specs=g_specs + d_specs + [_resident(w_nat.shape)] + [_resident(w.shape) for w in w_dil]
        + [row, row, pl.BlockSpec((1, D_MODEL), lambda i: (0, 0))],
        out_specs=[row, pl.BlockSpec((SUBLANES, D_MODEL), lambda i: (0, 0))],
        out_shape=[jax.ShapeDtypeStruct((S, D_MODEL), F32), jax.ShapeDtypeStruct((SUBLANES, D_MODEL), F32)],
        scratch_shapes=_lane_scratch(tm, D_MODEL),
        name="in_bwd_ln0", compiler_params=_cparams(("arbitrary",), 52))(*operands)


HBM_SPEC = pl.BlockSpec(memory_space=pltpu.HBM)


def _place():
    x, y, c = lax.axis_index("x"), lax.axis_index("y"), lax.axis_index("c")
    chips = [(1 - x, y), (x, 1 - y), (1 - x, 1 - y)]
    return x, y, c, chips


def _allgather_shards(shards, after, *, name, collective_id):
    n = len(shards)
    per = 6

    def body(*refs):
        ins, outs = refs[:n], refs[n + len(after):2 * n + len(after)]
        send_sems, recv_sems, loc_sems = refs[2 * n + len(after):]
        x, y, c, chips = _place()
        me = 2 * x + y
        sib = (x, y, 1 - c)
        peers = [sib] + [(px, py, c) for px, py in chips]
        barrier = pltpu.get_barrier_semaphore()
        for peer in peers:
            pl.semaphore_signal(barrier, inc=1, device_id=peer, device_id_type=MESH)
        pl.semaphore_wait(barrier, len(peers))

        def rcopy(w, k, src, dst, to):
            return pltpu.make_async_remote_copy(src_ref=src, dst_ref=dst, send_sem=send_sems.at[per * w + k],
                                                recv_sem=recv_sems.at[per * w + k], device_id=to, device_id_type=MESH)

        split = [s.shape[0] == N_CORES for s in shards]
        half = lambda w: c if split[w] else 0
        local, sends = [], []
        for w in range(n):
            cp = pltpu.make_async_copy(ins[w], outs[w].at[me], loc_sems.at[w])
            cp.start()
            local.append(cp)
            for j, (px, py) in enumerate(chips):
                cp = rcopy(w, j, ins[w].at[half(w)], outs[w].at[me, half(w)], (px, py, c))
                cp.start()
                sends.append(cp)
        for w in range(n):
            for j, (px, py) in enumerate(chips):
                slot = outs[w].at[2 * px + py, half(w)]
                rcopy(w, j, slot, slot, (px, py, c)).wait_recv()
                if split[w]:
                    cp = rcopy(w, 3 + j, slot, slot, sib)
                    cp.start()
                    sends.append(cp)
        for w in range(n):
            if split[w]:
                for j, (px, py) in enumerate(chips):
                    slot = outs[w].at[2 * px + py, 1 - c]
                    rcopy(w, 3 + j, slot, slot, sib).wait_recv()
        for cp in sends:
            cp.wait_send()
        for cp in local:
            cp.wait()

    return pl.kernel(
        body, out_type=[jax.ShapeDtypeStruct((N_CHIPS,) + s.shape, s.dtype) for s in shards],
        mesh=plsc.ScalarSubcoreMesh(axis_name="sequencer", num_cores=1),
        scratch_types=[pltpu.SemaphoreType.DMA((per * n,)), pltpu.SemaphoreType.DMA((per * n,)),
                       pltpu.SemaphoreType.DMA((n,))],
        name=name, compiler_params=pltpu.CompilerParams(collective_id=collective_id))(*shards, *after)


def _exchange_grads(grads, *, name, collective_id):
    n = len(grads)
    per = 7

    def body(*refs):
        ins, outs = refs[:n], refs[n:2 * n]
        send_sems, recv_sems, loc_sems = refs[2 * n:]
        x, y, c, chips = _place()
        me = 2 * x + y
        sib = (x, y, 1 - c)
        peers = [sib] + [(px, py, c) for px, py in chips]
        barrier = pltpu.get_barrier_semaphore()
        for peer in peers:
            pl.semaphore_signal(barrier, inc=1, device_id=peer, device_id_type=MESH)
        pl.semaphore_wait(barrier, len(peers))

        def rcopy(w, k, src, dst, to):
            return pltpu.make_async_remote_copy(src_ref=src, dst_ref=dst, send_sem=send_sems.at[per * w + k],
                                                recv_sem=recv_sems.at[per * w + k], device_id=to, device_id_type=MESH)

        local, sends = [], []
        for w in range(n):
            cp = pltpu.make_async_copy(ins[w].at[me], outs[w].at[c, me], loc_sems.at[w])
            cp.start()
            local.append(cp)
            cp = rcopy(w, 0, ins[w].at[me], outs[w].at[c, me], sib)
            cp.start()
            sends.append(cp)
            for j, (px, py) in enumerate(chips):
                cp = rcopy(w, 1 + j, ins[w].at[2 * px + py], outs[w].at[c, me], (px, py, c))
                cp.start()
                sends.append(cp)
        for w in range(n):
            for j, (px, py) in enumerate(chips):
                slot = outs[w].at[c, 2 * px + py]
                rcopy(w, 1 + j, slot, slot, (px, py, c)).wait_recv()
                cp = rcopy(w, 4 + j, slot, slot, sib)
                cp.start()
                sends.append(cp)
        for w in range(n):
            slot = outs[w].at[1 - c, me]
            rcopy(w, 0, slot, slot, sib).wait_recv()
            for j, (px, py) in enumerate(chips):
                slot = outs[w].at[1 - c, 2 * px + py]
                rcopy(w, 4 + j, slot, slot, sib).wait_recv()
        for cp in sends:
            cp.wait_send()
        for cp in local:
            cp.wait()

    return pl.kernel(
        body, out_type=[jax.ShapeDtypeStruct((N_CORES,) + g.shape, g.dtype) for g in grads],
        mesh=plsc.ScalarSubcoreMesh(axis_name="sequencer", num_cores=1),
        scratch_types=[pltpu.SemaphoreType.DMA((per * n,)), pltpu.SemaphoreType.DMA((per * n,)),
                       pltpu.SemaphoreType.DMA((n,))],
        name=name, compiler_params=pltpu.CompilerParams(collective_id=collective_id))(*grads)


def _allgather_small(vec, after):
    def body(v_ref, _, o_ref, send_sems, recv_sems, loc_sem):
        x, y, c = lax.axis_index("x"), lax.axis_index("y"), lax.axis_index("c")
        me = 4 * x + 2 * y + c

        def peer(k):
            flip = lambda v, bit: 1 - v if (k >> bit) & 1 else v
            return flip(x, 2), flip(y, 1), flip(c, 0)

        loc = pltpu.make_async_copy(v_ref, o_ref.at[me], loc_sem)
        loc.start()
        sends = []
        for k in range(1, N_DEV):
            cp = pltpu.make_async_remote_copy(src_ref=v_ref, dst_ref=o_ref.at[me], send_sem=send_sems.at[k - 1],
                                              recv_sem=recv_sems.at[k - 1], device_id=peer(k), device_id_type=MESH)
            cp.start()
            sends.append(cp)
        for k in range(1, N_DEV):
            px, py, pc = peer(k)
            pltpu.make_async_remote_copy(src_ref=v_ref, dst_ref=o_ref.at[4 * px + 2 * py + pc],
                                         send_sem=send_sems.at[k - 1], recv_sem=recv_sems.at[k - 1],
                                         device_id=(px, py, pc), device_id_type=MESH).wait_recv()
        for cp in sends:
            cp.wait_send()
        loc.wait()

    return pl.pallas_call(
        body, in_specs=[HBM_SPEC, HBM_SPEC], out_specs=HBM_SPEC,
        out_shape=jax.ShapeDtypeStruct((N_DEV,) + vec.shape, vec.dtype),
        scratch_shapes=[pltpu.SemaphoreType.DMA((N_DEV - 1,)), pltpu.SemaphoreType.DMA((N_DEV - 1,)),
                        pltpu.SemaphoreType.DMA],
        name="allgather_small")(vec, after)


def _adamw(w, g, m, v):
    m = ADAM_B1 * m + (1.0 - ADAM_B1) * g
    v = ADAM_B2 * v + (1.0 - ADAM_B2) * (g * g)
    m_hat = m / (1.0 - ADAM_B1 ** ADAM_STEP)
    v_hat = v / (1.0 - ADAM_B2 ** ADAM_STEP)
    delta = -ADAM_LR * (m_hat / (jnp.sqrt(v_hat) + ADAM_EPS) + ADAM_WD * w)
    return delta, m, v


def _reduce_adamw(parts, w, m, v, *, tr, name):
    R, C = w.shape

    def body(p_ref, w_ref, m_ref, v_ref, g_ref, d_ref, nm_ref, nv_ref):
        def core_sum(cc):
            s = p_ref[cc, 0].astype(F32)
            for k in range(1, N_CHIPS):
                s = s + p_ref[cc, k].astype(F32)
            return s

        g = core_sum(0) + core_sum(1)
        delta, nm, nv = _adamw(w_ref[...], g, m_ref[...], v_ref[...])
        g_ref[...] = g
        d_ref[...] = delta
        nm_ref[...] = nm
        nv_ref[...] = nv

    blk = pl.BlockSpec((tr, C), lambda i: (i, 0))
    return pl.pallas_call(
        body, grid=(R // tr,),
        in_specs=[pl.BlockSpec((N_CORES, N_CHIPS, tr, C), lambda i: (0, 0, i, 0)), blk, blk, blk],
        out_specs=[blk] * 4, out_shape=[jax.ShapeDtypeStruct((R, C), F32)] * 4,
        name=name, compiler_params=_cparams(("parallel",), 40))(parts, w, m, v)


def _reduce_adamw_vectors(allv, offs, ws, ms, vs):
    n = len(ws)

    def body(a_ref, *refs):
        w_refs, m_refs, v_refs = refs[:n], refs[n:2 * n], refs[2 * n:3 * n]
        tot_ref, outs = refs[3 * n], refs[3 * n + 1:]
        s = a_ref[0]
        for d in range(1, N_DEV):
            s = s + a_ref[d]
        tot_ref[...] = s
        for k in range(n):
            g = s[:, offs[k]:offs[k] + w_refs[k].shape[1]]
            delta, nm, nv = _adamw(w_refs[k][...], g, m_refs[k][...], v_refs[k][...])
            for ref, val in zip(outs[4 * k:4 * k + 4], (g, delta, nm, nv)):
                ref[...] = val

    out_shape = [jax.ShapeDtypeStruct(allv.shape[1:], F32)]
    for w in ws:
        out_shape += [jax.ShapeDtypeStruct(w.shape, F32)] * 4
    res = pl.pallas_call(body, out_shape=out_shape, name="reduce_adamw_vectors",
                         compiler_params=_cparams((), 40))(allv, *ws, *ms, *vs)
    return res[0], [tuple(res[1 + 4 * k:5 + 4 * k]) for k in range(n)]


def _adamw_taps(ws, gs, ms, vs):
    n = len(ws)

    def body(*refs):
        outs = refs[4 * n:]
        for k in range(n):
            res = _adamw(refs[k][...], refs[n + k][...], refs[2 * n + k][...], refs[3 * n + k][...])
            for ref, val in zip(outs[3 * k:3 * k + 3], res):
                ref[...] = val

    out_shape = []
    for w in ws:
        out_shape += [jax.ShapeDtypeStruct(w.shape, F32)] * 3
    res = pl.pallas_call(body, out_shape=out_shape, name="adamw_taps")(*ws, *gs, *ms, *vs)
    return [tuple(res[3 * k:3 * k + 3]) for k in range(n)]


def _pack(pieces):
    flat, offs, n = [], [], 0
    for p in pieces:
        size = -(-p.size // LANES) * LANES
        flat.append(jnp.pad(p.reshape(-1), (0, size - p.size)))
        offs.append(n)
        n += size
    return jnp.concatenate(flat).reshape(1, n), offs


def _local_step(x, target, p, wfull, on_ready=lambda group: None, before_ln0=()):
    S = x.shape[0]
    dils = [d for _, d in GROUPS]

    h0, h0b, *h0_res = _ln0_fwd(x, p["ln0_g"], p["ln0_b"], before_ln0)
    h0_rows = [h0b] + [h.reshape(S, D_MODEL) for h in h0_res]

    if isinstance(wfull, dict):
        w_in3, pending = wfull["w_in"], None
    else:
        w_in3, launch_rest, assemble = wfull
        w_in3, h0b = lax.optimization_barrier((w_in3, h0b))
        pending = launch_rest(h0b)

    w_blocks = w_in3.transpose(1, 0, 2).reshape(D_MODEL, N_BLK, GROUP_W)
    w_perm = jnp.concatenate([w_blocks[:, b] for b in PERM], axis=1)
    b_blocks = p["b_in"].reshape(N_BLK, GROUP_W)
    b_perm = jnp.concatenate([b_blocks[b] for b in PERM]).reshape(1, N_IN)
    w_nat, b_nat = w_perm[:, :N_NAT], b_perm[:, :N_NAT]
    qkv_cols = [slice(P_Q0 + g * QKV_W, P_Q0 + (g + 1) * QKV_W) for g in range(N_GROUPS)]
    w_qkv = [w_perm[:, c] for c in qkv_cols]

    proj = _mm_nn(h0b, w_nat, b_nat, tm=512, tn=N_NAT // 2, out_dtype=BF16, name="proj")
    qkv = [proj[None]]
    for g in range(1, N_GROUPS):
        t = _mm_nn(h0_rows[g], w_qkv[g], b_perm[:, qkv_cols[g]], tm=512, tn=QKV_W, out_dtype=BF16, name=f"proj_qkv{g}")
        qkv.append(t.reshape(dils[g], S // dils[g], QKV_W))
    if pending is not None:
        pending, qkv = lax.optimization_barrier((pending, qkv))
        proj = qkv[0][0]
        wfull = assemble(pending)
    w_up3 = wfull["w_up"]
    w_a, w_o, w_down, w_b = wfull["w_a"], wfull["w_o"], wfull["w_down"], wfull["w_b"]
    conv_w, ffn_conv_w = wfull["conv_w"], wfull["ffn_conv_w"]
    col0 = [P_Q0 // GROUP_W] + [0] * (N_GROUPS - 1)
    ya_in = _conv_gate_fwd(proj, conv_w)
    att = [_attn_fwd(qkv[g], col0[g], g) for g in range(N_GROUPS)]
    comb, comb_b, lse_tot = _attn_combine([a[0] for a in att], [a[1] for a in att])
    yab, mixin = _branch_mix(ya_in, comb_b, w_a, w_b, proj)
    xhat1, rstd1, h1b = _mix_ln1(mixin, w_o, p["b_o"], h0, p["ln1_g"], p["ln1_b"])
    up = _mm_nn(h1b, w_up3, p["b_up"], tm=512, tn=w_up3.shape[2], out_dtype=BF16, name="up")
    f = _ffn_conv_fwd(up, ffn_conv_w, p["ffn_conv_b"])
    dz2, dz2b, st2 = _down_ln2_loss(f, w_down, p["b_down"], xhat1, p["ln1_g"], p["ln1_b"],
                                    p["ln2_g"], p["ln2_b"], target)

    gw = {}
    gw["w_down"] = _mm_tn(f, dz2b, n_out=1, tn=D_MODEL, ts=1024, g_block=(1024, D_MODEL),
                          g_map=lambda j, s: (s, 0), name="grad_w_down").reshape(N_CHIPS, D_FF // N_CHIPS, D_MODEL)
    df = _mm_nt(dz2b, w_down, tm=512, name="df")
    dup, sm_ffn = _ffn_conv_bwd(up, df, ffn_conv_w, p["ffn_conv_b"])
    up_tn = w_up3.shape[2]
    up_pp = D_FF // up_tn
    gw["w_up"] = _mm_tn(h1b, dup, n_out=N_CHIPS, tn=up_tn, ts=1024, g_block=(None, 1024, up_tn),
                        g_map=lambda j, s: (j // up_pp, s, j % up_pp), name="grad_w_up")
    on_ready({n: gw[n] for n in ("w_down", "w_up")})
    dz1, dz1b, st1 = _up_bwd_ln1(dup, w_up3, dz2, xhat1, rstd1, p["ln1_g"])

    gw["w_o"] = _mm_tn(mixin, dz1b, n_out=1, tn=D_MODEL, ts=512, g_block=(512, D_MODEL),
                       g_map=lambda j, s: (s, 0), name="grad_w_o").reshape(N_CHIPS, D_MODEL // N_CHIPS, D_MODEL)
    dyab, dgab = _mix_bwd(dz1b, w_o, proj, yab)
    gw["w_a"] = _mm_tn(ya_in, dyab, n_out=1, tn=D_MODEL, ts=512, g_block=(512, D_MODEL),
                       g_map=lambda j, s: (s, 0), name="grad_w_a").reshape(N_CHIPS, D_CONV // N_CHIPS, D_MODEL)
    gw_b = _mm_tn(comb_b, dyab, n_out=1, tn=D_MODEL, ts=1024, g_block=(1024, D_MODEL),
                  g_map=lambda j, s: (s, 1), name="grad_w_b")
    gw["w_b"] = gw_b.reshape(GROUP_W, N_CHIPS, D_MODEL // N_CHIPS).transpose(1, 0, 2)
    on_ready({n: gw[n] for n in ("w_o", "w_a", "w_b")})
    dya_in = _mm_nt(dyab, w_a, tm=512, a_col=0, name="dya_in")
    dbch, sm_conv = _conv_gate_bwd(proj, dya_in, conv_w)
    att_stats = _comb_bwd(dyab, w_b, comb, lse_tot)
    dqkv = [_attn_bwd(qkv[g], col0[g], g, *att_stats[g]) for g in range(N_GROUPS)]

    w_pieces, b_pieces = [], []
    for nm, planes in (("bch", dbch), ("gab", dgab)):
        pw, pc = _mm_tn(h0b, planes, n_out=planes.shape[0], tn=D_MODEL, ts=1024, g_block=(None, 1024, D_MODEL),
                        g_map=lambda j, s: (j, s, 0), colsum=True, name="grad_w_in_" + nm)
        w_pieces.append(pw.transpose(1, 0, 2).reshape(D_MODEL, planes.shape[0] * D_MODEL))
        b_pieces.append(pc[0])
    for g in range(N_GROUPS):
        pw, pc = _mm_tn_cat(h0_rows[g], [a.reshape(S, GROUP_W) for a in dqkv[g]], ts=1024, name=f"grad_w_in_qkv{g}")
        w_pieces.append(pw)
        b_pieces.append(pc[0])
    dw_blocks = jnp.concatenate(w_pieces, axis=1).reshape(D_MODEL, N_BLK, GROUP_W)
    dw_ref = jnp.concatenate([dw_blocks[:, b] for b in INV_PERM], axis=1)
    gw["w_in"] = dw_ref.reshape(D_MODEL, N_CHIPS, N_IN // N_CHIPS).transpose(1, 0, 2)
    gw["w_in"], dz1 = lax.optimization_barrier((gw["w_in"], dz1))
    on_ready({"w_in": gw["w_in"]})
    db_blocks = jnp.concatenate(b_pieces).reshape(N_BLK, GROUP_W)
    grad_b_in = jnp.concatenate([db_blocks[b] for b in INV_PERM])

    grad_x, st0 = _in_bwd_ln0([dbch, dgab], dqkv, w_nat, w_qkv[1:], dz1, x, p["ln0_g"])

    small = {
        "loss": st2[2:3, 0:1],
        "ln0_g": st0[0], "ln0_b": st0[1], "b_in": grad_b_in, "conv_w": sm_conv[0:3],
        "b_o": st1[2], "ln1_g": st1[0], "ln1_b": st1[1],
        "b_up": jnp.concatenate([sm_ffn[0], sm_ffn[1]]), "ffn_conv_w": sm_ffn[3:6], "ffn_conv_b": sm_ffn[2],
        "b_down": st2[3], "ln2_g": st2[0], "ln2_b": st2[1],
    }
    return grad_x, gw, small


BIG = ("w_in", "w_a", "w_b", "w_o", "w_up", "w_down")
CONV = ("conv_w", "ffn_conv_w")
VECS = ("ln0_g", "ln0_b", "b_in", "b_o", "ln1_g", "ln1_b", "b_up", "ffn_conv_b", "b_down", "ln2_g", "ln2_b")
ORDER = ("ln0_g", "ln0_b", "w_in", "b_in", "conv_w", "w_a", "w_b", "w_o", "b_o", "ln1_g", "ln1_b", "w_up", "b_up",
         "ffn_conv_w", "ffn_conv_b", "w_down", "b_down", "ln2_g", "ln2_b")
SMALL_ORDER = ("loss",) + VECS + CONV


def _step(x, target, W, Mo, Vo):
    x2, t2 = x[0], target[0]
    big2 = {n: W[n][0] for n in BIG}
    halves = lambda a: a.astype(BF16).reshape(N_CORES, a.shape[0] // N_CORES, a.shape[1])
    whole = lambda g: g.reshape(N_CHIPS, g.shape[1] * g.shape[2], g.shape[3])
    later = tuple(n for n in BIG if n != "w_in")
    w_in_halves = halves(big2["w_in"])
    first = _allgather_shards([w_in_halves], [], name="allgather_w_in", collective_id=1)

    def launch_rest(h0b):
        return _allgather_shards([halves(big2[n]) for n in later] + [W[n] for n in CONV], [h0b],
                                 name="allgather_rest", collective_id=2)

    def assemble(rest):
        gathered = {n: whole(g) for n, g in zip(later + CONV, rest)}
        return {
            "w_up": gathered["w_up"],
            "w_a": gathered["w_a"].reshape(D_CONV, D_MODEL), "w_o": gathered["w_o"].reshape(D_MODEL, D_MODEL),
            "w_down": gathered["w_down"].reshape(D_FF, D_MODEL),
            "w_b": gathered["w_b"].transpose(1, 0, 2).reshape(GROUP_W, D_MODEL),
            "conv_w": gathered["conv_w"].transpose(1, 0, 2).reshape(3, D_CONV),
            "ffn_conv_w": gathered["ffn_conv_w"].transpose(1, 0, 2).reshape(3, D_FF),
        }

    pvec = {n: W[n].reshape(1, -1) for n in VECS}

    parts = {}
    exchange_ids = iter((3, 4, 5))

    def exchange(group):
        names = tuple(group)
        res = _exchange_grads([group[n] for n in names], name="exchange_" + "_".join(names),
                              collective_id=next(exchange_ids))
        parts.update(zip(names, res))

    grad_x, _, small = _local_step(x2, t2, pvec, (whole(first[0]), launch_rest, assemble), exchange,
                                   before_ln0=[w_in_halves])
    out = {}
    for n in BIG:
        tr = {"w_in": 128, "w_up": 128, "w_b": 128}.get(n, big2[n].shape[0] // 4)
        g, d, nm, nv = _reduce_adamw(parts[n], big2[n], Mo[n][0], Vo[n][0], tr=tr, name="adamw_" + n)
        out[n] = tuple(a[None] for a in (g, d, nm, nv))

    vec, offs = _pack([small[n] for n in SMALL_ORDER])
    off = dict(zip(SMALL_ORDER, offs))
    row = lambda a: a.reshape(1, -1)
    allv = _allgather_small(vec, parts["w_in"])
    tot, vec_out = _reduce_adamw_vectors(allv, [off[n] for n in VECS], [row(W[n]) for n in VECS],
                                         [row(Mo[n]) for n in VECS], [row(Vo[n]) for n in VECS])
    for n, res in zip(VECS, vec_out):
        out[n] = tuple(a.reshape(W[n].shape) for a in res)
    loss = tot[0, off["loss"]]
    chip = 2 * lax.axis_index("x") + lax.axis_index("y")
    taps_g = []
    for n in CONV:
        width = W[n].shape[2]
        full = lax.slice(tot, (0, off[n]), (1, off[n] + 3 * N_CHIPS * width)).reshape(3, N_CHIPS * width)
        taps_g.append(lax.dynamic_slice_in_dim(full, chip * width, width, axis=1))
    taps_out = _adamw_taps([W[n][0] for n in CONV], taps_g, [Mo[n][0] for n in CONV], [Vo[n][0] for n in CONV])
    for n, g, res in zip(CONV, taps_g, taps_out):
        out[n] = tuple(a[None] for a in (g,) + res)

    res = [loss, grad_x[None]]
    for k in range(4):
        res += [out[n][k] for n in ORDER]
    return tuple(res)


def kernel(x, ln0_g, ln0_b, w_in, b_in, conv_w, w_a, w_b, w_o, b_o, ln1_g, ln1_b, w_up, b_up, ffn_conv_w, ffn_conv_b, w_down, b_down, ln2_g, ln2_b, loss_target, m_ln0_g, m_ln0_b, m_w_in, m_b_in, m_conv_w, m_w_a, m_w_b, m_w_o, m_b_o, m_ln1_g, m_ln1_b, m_w_up, m_b_up, m_ffn_conv_w, m_ffn_conv_b, m_w_down, m_b_down, m_ln2_g, m_ln2_b, v_ln0_g, v_ln0_b, v_w_in, v_b_in, v_conv_w, v_w_a, v_w_b, v_w_o, v_b_o, v_ln1_g, v_ln1_b, v_w_up, v_b_up, v_ffn_conv_w, v_ffn_conv_b, v_w_down, v_b_down, v_ln2_g, v_ln2_b):
    W = dict(zip(ORDER, (ln0_g, ln0_b, w_in, b_in, conv_w, w_a, w_b, w_o, b_o, ln1_g, ln1_b, w_up, b_up,
                         ffn_conv_w, ffn_conv_b, w_down, b_down, ln2_g, ln2_b)))
    Mo = dict(zip(ORDER, (m_ln0_g, m_ln0_b, m_w_in, m_b_in, m_conv_w, m_w_a, m_w_b, m_w_o, m_b_o, m_ln1_g, m_ln1_b,
                          m_w_up, m_b_up, m_ffn_conv_w, m_ffn_conv_b, m_w_down, m_b_down, m_ln2_g, m_ln2_b)))
    Vo = dict(zip(ORDER, (v_ln0_g, v_ln0_b, v_w_in, v_b_in, v_conv_w, v_w_a, v_w_b, v_w_o, v_b_o, v_ln1_g, v_ln1_b,
                          v_w_up, v_b_up, v_ffn_conv_w, v_ffn_conv_b, v_w_down, v_b_down, v_ln2_g, v_ln2_b)))
    return _step(x, loss_target, W, Mo, Vo)
```

```python
import functools
import math

import jax
import jax.numpy as jnp
from jax import lax
from jax.experimental import pallas as pl
from jax.experimental.pallas import tpu as pltpu
from jax.experimental.pallas import tpu_sc as plsc

F32 = jnp.float32
BF16 = jnp.bfloat16

D_MODEL = 1024
D_CONV = D_MODEL
HEAD_DIM = 64
HEADS_PER_GROUP = 8
GROUPS = ((128, 1), (512, 4), (2048, 16))
N_GROUPS = len(GROUPS)
GROUP_W = HEADS_PER_GROUP * HEAD_DIM
QKV_W = N_GROUPS * GROUP_W
RADIUS = 64
D_FF = 2816
LN_EPS = 1e-5
ALPHA = 2.0 ** 0.25
MASK_VALUE = -1e30
ATT_SCALE = HEAD_DIM ** -0.5
OFF_B = 0
OFF_C = OFF_B + D_CONV
OFF_H = OFF_C + D_CONV
OFF_Q = OFF_H + D_CONV
OFF_K = OFF_Q + QKV_W
OFF_V = OFF_K + QKV_W
OFF_GA = OFF_V + QKV_W
OFF_GB = OFF_GA + D_MODEL
N_IN = OFF_GB + D_MODEL
ADAM_LR = 0.001
ADAM_B1 = 0.9
ADAM_B2 = 0.999
ADAM_EPS = 1e-08
ADAM_WD = 0.01
ADAM_STEP = 10
INV_SQRT2 = 0.7071067811865476
INV_SQRT_2PI = 0.3989422804014327

LANES = 128
SUBLANES = 8
VMEM_BYTES_V7X = 64 * 1024 * 1024
N_CHIPS = 4
N_CORES = 2
N_DEV = N_CHIPS * N_CORES
MESH = pl.DeviceIdType.MESH

N_BLK = N_IN // GROUP_W
PERM = (0, 1, 2, 3, 4, 5, 15, 16, 17, 18, 6, 9, 12, 7, 10, 13, 8, 11, 14)
INV_PERM = tuple(PERM.index(b) for b in range(N_BLK))
P_B, P_C, P_H, P_GA, P_GB, P_Q0 = 0, 1024, 2048, 3072, 4096, 5120
N_NAT = P_Q0 + QKV_W // N_GROUPS * 3
N_GATED = P_Q0

SLAB = 128
CHUNK = 256
PAD = SUBLANES
TQ = 128


def _cparams(sem, vmem_mb):
    assert vmem_mb * 1024 * 1024 < VMEM_BYTES_V7X
    return pltpu.CompilerParams(dimension_semantics=sem, vmem_limit_bytes=vmem_mb * 1024 * 1024)


def _dot(a, b):
    return jnp.dot(a, b, preferred_element_type=F32)


def _dot_nt(a, b):
    return lax.dot_general(a, b, (((1,), (1,)), ((), ())), preferred_element_type=F32)


def _dot_tn(a, b):
    return lax.dot_general(a, b, (((0,), (0,)), ((), ())), preferred_element_type=F32)


def _ln_stats(z):
    mu = jnp.mean(z, -1, keepdims=True)
    zc = z - mu
    var = jnp.mean(zc * zc, -1, keepdims=True)
    rstd = lax.rsqrt(var + LN_EPS)
    return zc * rstd, rstd


def _ln_bwd(dh, xhat, rstd, g):
    dxh = dh * g
    m1 = jnp.mean(dxh, -1, keepdims=True)
    m2 = jnp.mean(dxh * xhat, -1, keepdims=True)
    return rstd * (dxh - m1 - xhat * m2)


def _rows8(rows, width):
    pad = [jnp.zeros((1, width), F32)] * (SUBLANES - len(rows))
    return jnp.concatenate(list(rows) + pad, axis=0)


def _mm_nn(a, w, bias, *, tm, tn, out_dtype, name, vmem_mb=40):
    M, K = a.shape
    if w.ndim == 3:
        assert w.shape[2] == tn
        n_tiles = w.shape[0]
        w_spec = pl.BlockSpec((None, K, tn), lambda j, i: (j, 0, 0))
    else:
        n_tiles = w.shape[1] // tn
        w_spec = pl.BlockSpec((K, tn), lambda j, i: (0, j))

    def body(a_ref, w_ref, b_ref, o_ref):
        o_ref[...] = (_dot(a_ref[...], w_ref[...]) + b_ref[...]).astype(o_ref.dtype)

    return pl.pallas_call(
        body, grid=(n_tiles, M // tm),
        in_specs=[pl.BlockSpec((tm, K), lambda j, i: (i, 0)), w_spec, pl.BlockSpec((1, tn), lambda j, i: (0, j))],
        out_specs=pl.BlockSpec((tm, tn), lambda j, i: (i, j)),
        out_shape=jax.ShapeDtypeStruct((M, n_tiles * tn), out_dtype),
        name=name, compiler_params=_cparams(("arbitrary", "parallel"), vmem_mb))(a, w, bias)


def _mm_nt(a, w, *, tm, a_col=0, name, vmem_mb=40):
    M = a.shape[0]
    N, K = w.shape

    def body(a_ref, w_ref, o_ref):
        o_ref[...] = _dot_nt(a_ref[...], w_ref[...]).astype(o_ref.dtype)

    return pl.pallas_call(
        body, grid=(M // tm,),
        in_specs=[pl.BlockSpec((tm, K), lambda i: (i, a_col)),
                  pl.BlockSpec((N, K), lambda i: (0, 0))],
        out_specs=pl.BlockSpec((tm, N), lambda i: (i, 0)),
        out_shape=jax.ShapeDtypeStruct((M, N), BF16),
        name=name, compiler_params=_cparams(("parallel",), vmem_mb))(a, w)


def _mm_tn(a, g, *, n_out, tn, ts, g_block, g_map, colsum=False, name, vmem_mb=48):
    S, K = a.shape
    n_s = S // ts

    def body(a_ref, g_ref, *rest):
        if colsum:
            o_ref, cs_ref, acc_ref, cacc_ref = rest
        else:
            o_ref, acc_ref = rest
        s = pl.program_id(1)

        @pl.when(s == 0)
        def _():
            acc_ref[...] = jnp.zeros_like(acc_ref)
            if colsum:
                cacc_ref[...] = jnp.zeros_like(cacc_ref)

        gv = g_ref[...]
        acc_ref[...] += _dot_tn(a_ref[...], gv)
        if colsum:
            cacc_ref[...] += jnp.broadcast_to(jnp.sum(gv.astype(F32), axis=0, keepdims=True), cacc_ref.shape)

        @pl.when(s == n_s - 1)
        def _():
            o_ref[...] = acc_ref[...].astype(o_ref.dtype)
            if colsum:
                cs_ref[...] = cacc_ref[...]

    out_specs = [pl.BlockSpec((None, K, tn), lambda j, s: (j, 0, 0))]
    out_shape = [jax.ShapeDtypeStruct((n_out, K, tn), BF16)]
    scratch = [pltpu.VMEM((K, tn), F32)]
    if colsum:
        out_specs.append(pl.BlockSpec((SUBLANES, tn), lambda j, s: (0, j)))
        out_shape.append(jax.ShapeDtypeStruct((SUBLANES, n_out * tn), F32))
        scratch.append(pltpu.VMEM((SUBLANES, tn), F32))
    res = pl.pallas_call(
        body, grid=(n_out, n_s),
        in_specs=[pl.BlockSpec((ts, K), lambda j, s: (s, 0)), pl.BlockSpec(g_block, g_map)],
        out_specs=out_specs, out_shape=out_shape, scratch_shapes=scratch,
        name=name, compiler_params=_cparams(("parallel", "arbitrary"), vmem_mb))(a, g)
    return res if colsum else res[0]


def _mm_tn_cat(a, gs, *, ts, name, vmem_mb=40):
    S, K = a.shape
    widths = [g.shape[1] for g in gs]
    n_s, total = S // ts, sum(widths)

    def body(*refs):
        a_ref, g_refs = refs[0], refs[1:1 + len(gs)]
        o_ref, cs_ref, acc_ref, cacc_ref = refs[1 + len(gs):]
        s = pl.program_id(0)

        @pl.when(s == 0)
        def _():
            acc_ref[...] = jnp.zeros_like(acc_ref)
            cacc_ref[...] = jnp.zeros_like(cacc_ref)

        av, col = a_ref[...], 0
        for g_ref, w in zip(g_refs, widths):
            gv = g_ref[...]
            acc_ref[:, col:col + w] += _dot_tn(av, gv)
            cacc_ref[:, col:col + w] += jnp.broadcast_to(jnp.sum(gv.astype(F32), axis=0, keepdims=True), (SUBLANES, w))
            col += w

        @pl.when(s == n_s - 1)
        def _():
            o_ref[...] = acc_ref[...].astype(BF16)
            cs_ref[...] = cacc_ref[...]

    return pl.pallas_call(
        body, grid=(n_s,),
        in_specs=[pl.BlockSpec((ts, K), lambda s: (s, 0))] + [pl.BlockSpec((ts, w), lambda s: (s, 0)) for w in widths],
        out_specs=[pl.BlockSpec((K, total), lambda s: (0, 0)), pl.BlockSpec((SUBLANES, total), lambda s: (0, 0))],
        out_shape=[jax.ShapeDtypeStruct((K, total), BF16), jax.ShapeDtypeStruct((SUBLANES, total), F32)],
        scratch_shapes=[pltpu.VMEM((K, total), F32), pltpu.VMEM((SUBLANES, total), F32)],
        name=name, compiler_params=_cparams(("arbitrary",), vmem_mb))(a, *gs)


DILS = tuple(d for _, d in GROUPS if d > 1)


def _res_spec(d, tm, width):
    return pl.BlockSpec((d, tm // d, width), lambda i: (0, i, 0))


def _lane_scratch(tm, width):
    return [pltpu.VMEM((tm, LANES), F32)] * (width // LANES)


def _to_residue(val, dst_refs, dils, tm, dtype, scr):
    for c, ref in enumerate(scr):
        ref[...] = val[:, c * LANES:(c + 1) * LANES]
    for dst_ref, d in zip(dst_refs, dils):
        for r in range(d):
            cols = [ref[pl.ds(r, tm // d, stride=d), :] for ref in scr]
            dst_ref[r] = jnp.concatenate(cols, axis=1).astype(dtype)


def _from_residue(rows_of, d, tm, scr):
    for r in range(d):
        v = rows_of(r).astype(F32)
        for c, ref in enumerate(scr):
            ref[pl.ds(r, tm // d, stride=d), :] = v[:, c * LANES:(c + 1) * LANES]
    return jnp.concatenate([ref[...] for ref in scr], axis=1)


def _ln0_fwd(x, g, b, after=(), *, tm=512):
    S, Dm = x.shape
    n_after = len(after)

    def body(x_ref, g_ref, b_ref, *rest):
        h_ref, hb_ref, *rest = rest[n_after:]
        xhat, _ = _ln_stats(x_ref[...])
        h = xhat * g_ref[...] + b_ref[...]
        h_ref[...] = h
        hb_ref[...] = h.astype(BF16)
        _to_residue(h, rest[:len(DILS)], DILS, tm, BF16, rest[len(DILS):])

    row = pl.BlockSpec((tm, Dm), lambda i: (i, 0))
    vec = pl.BlockSpec((1, Dm), lambda i: (0, 0))
    return pl.pallas_call(
        body, grid=(S // tm,), in_specs=[row, vec, vec] + [pl.BlockSpec(memory_space=pl.ANY)] * n_after,
        out_specs=[row, row] + [_res_spec(d, tm, Dm) for d in DILS],
        out_shape=[jax.ShapeDtypeStruct((S, Dm), F32), jax.ShapeDtypeStruct((S, Dm), BF16)]
        + [jax.ShapeDtypeStruct((d, S // d, Dm), BF16) for d in DILS],
        scratch_shapes=_lane_scratch(tm, Dm),
        name="ln0_fwd", compiler_params=_cparams(("parallel",), 32))(x, g, b, *after)


def _slab_spec(S, col0):
    return pl.BlockSpec((S, SLAB), lambda j: (0, col0 // SLAB + j))


def _zero_pads(scr, S):
    scr[0:PAD, :] = jnp.zeros((PAD, SLAB), F32)
    scr[S + PAD:S + 2 * PAD, :] = jnp.zeros((PAD, SLAB), F32)


def _shifted(scr, t):
    return (scr[PAD - 1 + t:PAD - 1 + t + CHUNK, :], scr[PAD + t:PAD + t + CHUNK, :],
            scr[PAD + 1 + t:PAD + 1 + t + CHUNK, :])


def _conv_gate_fwd(proj, conv_w):
    S = proj.shape[0]

    def body(b_ref, c_ref, h_ref, w_ref, o_ref, u_scr):
        _zero_pads(u_scr, S)
        for t in range(0, S, CHUNK):
            u_scr[PAD + t:PAD + t + CHUNK, :] = c_ref[t:t + CHUNK, :].astype(F32) * h_ref[t:t + CHUNK, :].astype(F32)
        w0, w1, w2 = w_ref[0:1, :], w_ref[1:2, :], w_ref[2:3, :]
        for t in range(0, S, CHUNK):
            um, u0, up = _shifted(u_scr, t)
            cv = w0 * um + w1 * u0 + w2 * up
            o_ref[t:t + CHUNK, :] = (b_ref[t:t + CHUNK, :].astype(F32) * cv).astype(BF16)

    return pl.pallas_call(
        body, grid=(D_CONV // SLAB,),
        in_specs=[_slab_spec(S, P_B), _slab_spec(S, P_C), _slab_spec(S, P_H),
                  pl.BlockSpec((3, SLAB), lambda j: (0, j))],
        out_specs=pl.BlockSpec((S, SLAB), lambda j: (0, j)),
        out_shape=jax.ShapeDtypeStruct((S, D_CONV), BF16),
        scratch_shapes=[pltpu.VMEM((S + 2 * PAD, SLAB), F32)],
        name="conv_gate_fwd", compiler_params=_cparams(("parallel",), 40))(proj, proj, proj, conv_w)


MASKED_DISTANCE = -1e34


def _attn_bias_table(g):
    dil = GROUPS[g][1]
    j = lax.broadcasted_iota(jnp.int32, (2 * TQ, TQ), 0)
    a = lax.broadcasted_iota(jnp.int32, (2 * TQ, TQ), 1)
    rel = jnp.abs(j - RADIUS - a)
    base = -(rel * dil).astype(F32)
    inside, after_start, before_end = rel <= RADIUS, j >= RADIUS, j < TQ + RADIUS
    variants = []
    for first, last in ((False, False), (True, False), (False, True), (True, True)):
        valid = inside & (after_start if first else True) & (before_end if last else True)
        variants.append(jnp.where(valid, base, MASKED_DISTANCE))
    return jnp.stack(variants)


def _bias_spec(nb):
    def variant(r, i):
        return (jnp.where(i == 0, 1, 0) + jnp.where(i == nb - 1, 2, 0), 0, 0)
    return pl.BlockSpec((None, 2 * TQ, TQ), variant)


def _head_stats(rows):
    pad = jnp.zeros((LANES - len(rows), TQ), F32)
    return jnp.concatenate(list(rows) + [pad], axis=0).T


def _slope(g, h):
    return 2.0 ** (-8.0 * (g * HEADS_PER_GROUP + h + 1) / (N_GROUPS * HEADS_PER_GROUP))


def _window(p_ref, c_ref, n_ref):
    return jnp.concatenate([p_ref[TQ - RADIUS:, :], c_ref[...], n_ref[:RADIUS, :]], axis=0)


def _pair(a, h):
    return a[:, (h // 2) * LANES:(h // 2 + 1) * LANES]


def _own_lanes(a, h):
    lane = lax.broadcasted_iota(jnp.int32, a.shape, 1)
    return jnp.where((lane >= HEAD_DIM) == (h % 2 == 1), a, jnp.zeros_like(a))


def _own_rows(a, h):
    return a[(h % 2) * HEAD_DIM:(h % 2 + 1) * HEAD_DIM, :]


def _qkv_specs(nb, col0):
    def spec(col, shift):
        return pl.BlockSpec((None, TQ, GROUP_W), lambda r, i: (r, jnp.clip(i + shift, 0, nb - 1), col))

    return [spec(col0, 0), spec(col0 + 1, -1), spec(col0 + 1, 0), spec(col0 + 1, 1),
            spec(col0 + 2, -1), spec(col0 + 2, 0), spec(col0 + 2, 1)]


def _attn_fwd(qkv, col0, g):
    dil, sub, _ = qkv.shape
    nb = sub // TQ

    def body(q_ref, kp, kc, kn, vp, vc, vn, bias_ref, o_ref, lse_ref, ot_scr, s_scr, p_scr):
        kwin = _window(kp, kc, kn)
        vwin = _window(vp, vc, vn)
        q = q_ref[...] * ATT_SCALE
        for h in range(HEADS_PER_GROUP):
            s_scr[h] = _dot_nt(_pair(kwin, h), _own_lanes(_pair(q, h), h))
        lse, inv_den = [], []
        for h in range(HEADS_PER_GROUP):
            s = s_scr[h] + _slope(g, h) * bias_ref[...]
            m = jnp.max(s, axis=0, keepdims=True)
            p = jnp.exp(s - m)
            den = jnp.sum(p, axis=0, keepdims=True)
            p_scr[h] = p.astype(BF16)
            inv_den.append(1.0 / den)
            lse.append(m + jnp.log(den))
        for h in range(HEADS_PER_GROUP):
            ot = _dot_tn(_pair(vwin, h), p_scr[h])
            ot_scr[h * HEAD_DIM:(h + 1) * HEAD_DIM, :] = _own_rows(ot, h) * inv_den[h]
        o_ref[...] = ot_scr[...].T
        lse_ref[...] = _head_stats(lse)

    return pl.pallas_call(
        body, grid=(dil, nb), in_specs=_qkv_specs(nb, col0) + [_bias_spec(nb)],
        out_specs=[pl.BlockSpec((None, TQ, GROUP_W), lambda r, i: (r, i, 0)),
                   pl.BlockSpec((None, TQ, LANES), lambda r, i: (r, i, 0))],
        out_shape=[jax.ShapeDtypeStruct((dil, sub, GROUP_W), F32), jax.ShapeDtypeStruct((dil, sub, LANES), F32)],
        scratch_shapes=[pltpu.VMEM((GROUP_W, TQ), F32), pltpu.VMEM((HEADS_PER_GROUP, 2 * TQ, TQ), F32),
                        pltpu.VMEM((HEADS_PER_GROUP, 2 * TQ, TQ), BF16)],
        name=f"attn_fwd_g{g}", compiler_params=_cparams(("parallel", "arbitrary"), 32))(
            *([qkv] * 7), _attn_bias_table(g))


def _expand_heads():
    h = lax.broadcasted_iota(jnp.int32, (LANES, GROUP_W), 0)
    c = lax.broadcasted_iota(jnp.int32, (LANES, GROUP_W), 1)
    return (c // HEAD_DIM == h).astype(F32)


def _dot_f32(a, b):
    return jnp.dot(a, b, preferred_element_type=F32, precision=lax.Precision.HIGHEST)


def _attn_combine(outs, lses, *, tm=512):
    S = outs[0].shape[1]
    n_col = GROUP_W // LANES

    def body(*refs):
        ins, e_ref = refs[:2 * N_GROUPS], refs[2 * N_GROUPS]
        c_ref, cb_ref, lt_ref = refs[2 * N_GROUPS + 1:2 * N_GROUPS + 4]
        scr = refs[2 * N_GROUPS + 4:]
        o, l = [ins[0][0]], [ins[N_GROUPS][0]]
        for k, d in enumerate(DILS):
            o_ref, l_ref = ins[1 + k], ins[N_GROUPS + 1 + k]
            o.append(_from_residue(lambda r: o_ref[r], d, tm, scr[k * (n_col + 1):k * (n_col + 1) + n_col]))
            l.append(_from_residue(lambda r: l_ref[r], d, tm, scr[k * (n_col + 1) + n_col:(k + 1) * (n_col + 1)]))
        m = jnp.maximum(jnp.maximum(l[0], l[1]), l[2])
        e = [jnp.exp(v - m) for v in l]
        den = e[0] + e[1] + e[2]
        comb = sum(_dot_f32(ev / den, e_ref[...]) * ov for ev, ov in zip(e, o))
        c_ref[...] = comb
        cb_ref[...] = comb.astype(BF16)
        lt_ref[...] = m + jnp.log(den)

    row = pl.BlockSpec((tm, GROUP_W), lambda i: (i, 0))
    dils = [d for _, d in GROUPS]
    return pl.pallas_call(
        body, grid=(S // tm,),
        in_specs=[_res_spec(d, tm, GROUP_W) for d in dils] + [_res_spec(d, tm, LANES) for d in dils]
        + [_resident((LANES, GROUP_W))],
        out_specs=[row, row, pl.BlockSpec((tm, LANES), lambda i: (i, 0))],
        out_shape=[jax.ShapeDtypeStruct((S, GROUP_W), F32), jax.ShapeDtypeStruct((S, GROUP_W), BF16),
                   jax.ShapeDtypeStruct((S, LANES), F32)],
        scratch_shapes=_lane_scratch(tm, GROUP_W + LANES) * len(DILS),
        name="attn_combine", compiler_params=_cparams(("parallel",), 32))(*outs, *lses, _expand_heads())


def _branch_mix(ya_in, comb_b, w_a, w_b, proj, *, tm=512):
    S = ya_in.shape[0]

    def body(ya_ref, cb_ref, wa_ref, wb_ref, ga_ref, gb_ref, yab_ref, mx_ref):
        y_a = _dot(ya_ref[...], wa_ref[...])
        y_b = _dot(cb_ref[...], wb_ref[...])
        yab_ref[:, 0:D_MODEL] = y_a.astype(BF16)
        yab_ref[:, D_MODEL:2 * D_MODEL] = y_b.astype(BF16)
        mx = jax.nn.sigmoid(ga_ref[...].astype(F32)) * y_a + jax.nn.sigmoid(gb_ref[...].astype(F32)) * y_b
        mx_ref[...] = mx.astype(BF16)

    return pl.pallas_call(
        body, grid=(S // tm,),
        in_specs=[pl.BlockSpec((tm, D_CONV), lambda i: (i, 0)), pl.BlockSpec((tm, GROUP_W), lambda i: (i, 0)),
                  pl.BlockSpec((D_CONV, D_MODEL), lambda i: (0, 0)), pl.BlockSpec((GROUP_W, D_MODEL), lambda i: (0, 0)),
                  pl.BlockSpec((tm, D_MODEL), lambda i: (i, P_GA // D_MODEL)),
                  pl.BlockSpec((tm, D_MODEL), lambda i: (i, P_GB // D_MODEL))],
        out_specs=[pl.BlockSpec((tm, 2 * D_MODEL), lambda i: (i, 0)), pl.BlockSpec((tm, D_MODEL), lambda i: (i, 0))],
        out_shape=[jax.ShapeDtypeStruct((S, 2 * D_MODEL), BF16), jax.ShapeDtypeStruct((S, D_MODEL), BF16)],
        name="branch_mix", compiler_params=_cparams(("parallel",), 40))(ya_in, comb_b, w_a, w_b, proj, proj)


def _mix_ln1(mixin, w_o, b_o, h0, g1, b1, *, tm=512):
    S = mixin.shape[0]

    def body(mx_ref, wo_ref, bo_ref, h0_ref, g_ref, b_ref, xh_ref, rs_ref, h1b_ref):
        z = ALPHA * h0_ref[...] + _dot(mx_ref[...], wo_ref[...]) + bo_ref[...]
        xhat, rstd = _ln_stats(z)
        xh_ref[...] = xhat
        rs_ref[...] = jnp.broadcast_to(rstd, (tm, LANES))
        h1b_ref[...] = (xhat * g_ref[...] + b_ref[...]).astype(BF16)

    row = pl.BlockSpec((tm, D_MODEL), lambda i: (i, 0))
    vec = pl.BlockSpec((1, D_MODEL), lambda i: (0, 0))
    return pl.pallas_call(
        body, grid=(S // tm,),
        in_specs=[row, pl.BlockSpec((D_MODEL, D_MODEL), lambda i: (0, 0)), vec, row, vec, vec],
        out_specs=[row, pl.BlockSpec((tm, LANES), lambda i: (i, 0)), row],
        out_shape=[jax.ShapeDtypeStruct((S, D_MODEL), F32), jax.ShapeDtypeStruct((S, LANES), F32),
                   jax.ShapeDtypeStruct((S, D_MODEL), BF16)],
        name="mix_ln1", compiler_params=_cparams(("parallel",), 40))(mixin, w_o, b_o, h0, g1, b1)


def _gelu_parts(cz):
    cdf = 0.5 * (1.0 + lax.erf(cz * INV_SQRT2))
    return cdf, cz * cdf


def _ffn_conv_fwd(up, cw, cb):
    S = up.shape[0]

    def body(a_ref, g_ref, w_ref, cb_ref, o_ref, a_scr):
        _zero_pads(a_scr, S)
        for t in range(0, S, CHUNK):
            a_scr[PAD + t:PAD + t + CHUNK, :] = a_ref[t:t + CHUNK, :].astype(F32)
        w0, w1, w2 = w_ref[0:1, :], w_ref[1:2, :], w_ref[2:3, :]
        for t in range(0, S, CHUNK):
            am, a0, ap = _shifted(a_scr, t)
            _, gel = _gelu_parts(w0 * am + w1 * a0 + w2 * ap + cb_ref[...])
            o_ref[t:t + CHUNK, :] = (gel * g_ref[t:t + CHUNK, :].astype(F32)).astype(BF16)

    return pl.pallas_call(
        body, grid=(D_FF // SLAB,),
        in_specs=[_slab_spec(S, 0), _slab_spec(S, D_FF), pl.BlockSpec((3, SLAB), lambda j: (0, j)),
                  pl.BlockSpec((1, SLAB), lambda j: (0, j))],
        out_specs=pl.BlockSpec((S, SLAB), lambda j: (0, j)),
        out_shape=jax.ShapeDtypeStruct((S, D_FF), BF16),
        scratch_shapes=[pltpu.VMEM((S + 2 * PAD, SLAB), F32)],
        name="ffn_conv_fwd", compiler_params=_cparams(("parallel",), 40))(up, up, cw, cb)


def _down_ln2_loss(f, w_down, b_down, xhat1, g1, b1, g2, b2, target, *, tm=512):
    S = f.shape[0]

    def body(f_ref, wd_ref, bd_ref, xh1_ref, g1_ref, b1_ref, g2_ref, b2_ref, t_ref, dz_ref, dzb_ref, st_ref):
        h1 = xh1_ref[...] * g1_ref[...] + b1_ref[...]
        z = ALPHA * h1 + _dot(f_ref[...], wd_ref[...]) + bd_ref[...]
        xhat, rstd = _ln_stats(z)
        err = xhat * g2_ref[...] + b2_ref[...] - t_ref[...]
        loss = (0.5 / D_MODEL) * jnp.sum(jnp.sum(err * err, axis=1, keepdims=True), axis=0, keepdims=True)
        dh2 = err * (1.0 / D_MODEL)
        dz = _ln_bwd(dh2, xhat, rstd, g2_ref[...])
        dz_ref[...] = dz
        dzb_ref[...] = dz.astype(BF16)
        upd = _rows8([jnp.sum(dh2 * xhat, axis=0, keepdims=True), jnp.sum(dh2, axis=0, keepdims=True),
                      jnp.broadcast_to(loss, (1, D_MODEL)), jnp.sum(dz, axis=0, keepdims=True)], D_MODEL)

        @pl.when(pl.program_id(0) == 0)
        def _():
            st_ref[...] = upd

        @pl.when(pl.program_id(0) != 0)
        def _():
            st_ref[...] += upd

    row = pl.BlockSpec((tm, D_MODEL), lambda i: (i, 0))
    vec = pl.BlockSpec((1, D_MODEL), lambda i: (0, 0))
    return pl.pallas_call(
        body, grid=(S // tm,),
        in_specs=[pl.BlockSpec((tm, D_FF), lambda i: (i, 0)), _resident((D_FF, D_MODEL)),
                  vec, row, vec, vec, vec, vec, row],
        out_specs=[row, row, pl.BlockSpec((SUBLANES, D_MODEL), lambda i: (0, 0))],
        out_shape=[jax.ShapeDtypeStruct((S, D_MODEL), F32), jax.ShapeDtypeStruct((S, D_MODEL), BF16),
                   jax.ShapeDtypeStruct((SUBLANES, D_MODEL), F32)],
        name="down_ln2_loss", compiler_params=_cparams(("arbitrary",), 56))(
            f, w_down, b_down, xhat1, g1, b1, g2, b2, target)


def _ffn_conv_bwd(up, df, cw, cb):
    S = up.shape[0]

    def body(a_ref, g_ref, df_ref, w_ref, cb_ref, dup_ref, sm_ref, a_scr, d_scr):
        _zero_pads(a_scr, S)
        _zero_pads(d_scr, S)
        for t in range(0, S, CHUNK):
            a_scr[PAD + t:PAD + t + CHUNK, :] = a_ref[t:t + CHUNK, :].astype(F32)
        w0, w1, w2 = w_ref[0:1, :], w_ref[1:2, :], w_ref[2:3, :]
        zero = jnp.zeros((1, SLAB), F32)
        s_dg, s_dcz, s_w0, s_w1, s_w2 = zero, zero, zero, zero, zero
        for t in range(0, S, CHUNK):
            am, a0, ap = _shifted(a_scr, t)
            cz = w0 * am + w1 * a0 + w2 * ap + cb_ref[...]
            cdf, gel = _gelu_parts(cz)
            dfv = df_ref[t:t + CHUNK, :].astype(F32)
            dgte = dfv * gel
            dcz = dfv * g_ref[t:t + CHUNK, :].astype(F32) * (cdf + cz * jnp.exp(-0.5 * cz * cz) * INV_SQRT_2PI)
            dup_ref[1, t:t + CHUNK, :] = dgte.astype(BF16)
            d_scr[PAD + t:PAD + t + CHUNK, :] = dcz
            s_dg = s_dg + jnp.sum(dgte, axis=0, keepdims=True)
            s_dcz = s_dcz + jnp.sum(dcz, axis=0, keepdims=True)
            s_w0 = s_w0 + jnp.sum(dcz * am, axis=0, keepdims=True)
            s_w1 = s_w1 + jnp.sum(dcz * a0, axis=0, keepdims=True)
            s_w2 = s_w2 + jnp.sum(dcz * ap, axis=0, keepdims=True)
        s_da = zero
        for t in range(0, S, CHUNK):
            dm, d0, dp = _shifted(d_scr, t)
            da = w0 * dp + w1 * d0 + w2 * dm
            dup_ref[0, t:t + CHUNK, :] = da.astype(BF16)
            s_da = s_da + jnp.sum(da, axis=0, keepdims=True)
        sm_ref[...] = _rows8([s_da, s_dg, s_dcz, s_w0, s_w1, s_w2], SLAB)

    return pl.pallas_call(
        body, grid=(D_FF // SLAB,),
        in_specs=[_slab_spec(S, 0), _slab_spec(S, D_FF), pl.BlockSpec((S, SLAB), lambda j: (0, j)),
                  pl.BlockSpec((3, SLAB), lambda j: (0, j)), pl.BlockSpec((1, SLAB), lambda j: (0, j))],
        out_specs=[pl.BlockSpec((2, S, SLAB), lambda j: (0, 0, j)), pl.BlockSpec((SUBLANES, SLAB), lambda j: (0, j))],
        out_shape=[jax.ShapeDtypeStruct((2, S, D_FF), BF16), jax.ShapeDtypeStruct((SUBLANES, D_FF), F32)],
        scratch_shapes=[pltpu.VMEM((S + 2 * PAD, SLAB), F32)] * 2,
        name="ffn_conv_bwd", compiler_params=_cparams(("parallel",), 48))(up, up, df, cw, cb)


def _resident(shape):
    nd = len(shape)
    return pl.BlockSpec(shape, lambda *_: (0,) * nd, pipeline_mode=pl.Buffered(1))


def _up_bwd_ln1(dup, w_up3, dz2, xhat1, rstd1, g1, *, tm=512):
    S = dz2.shape[0]
    ns, _, tk = w_up3.shape
    per_plane = D_FF // tk

    def body(du_ref, w_ref, dz2_ref, xh_ref, rs_ref, g_ref, dz_ref, dzb_ref, st_ref):
        dh = ALPHA * dz2_ref[...]
        for k in range(ns):
            col = (k % per_plane) * tk
            dh = dh + _dot_nt(du_ref[k // per_plane, :, col:col + tk], w_ref[k])
        xhat = xh_ref[...]
        dz = _ln_bwd(dh, xhat, rs_ref[:, 0:1], g_ref[...])
        dz_ref[...] = dz
        dzb_ref[...] = dz.astype(BF16)
        upd = _rows8([jnp.sum(dh * xhat, axis=0, keepdims=True), jnp.sum(dh, axis=0, keepdims=True),
                      jnp.sum(dz, axis=0, keepdims=True)], D_MODEL)

        @pl.when(pl.program_id(0) == 0)
        def _():
            st_ref[...] = upd

        @pl.when(pl.program_id(0) != 0)
        def _():
            st_ref[...] += upd

    row = pl.BlockSpec((tm, D_MODEL), lambda i: (i, 0))
    return pl.pallas_call(
        body, grid=(S // tm,),
        in_specs=[pl.BlockSpec((dup.shape[0], tm, D_FF), lambda i: (0, i, 0)), _resident(w_up3.shape),
                  row, row, pl.BlockSpec((tm, LANES), lambda i: (i, 0)), pl.BlockSpec((1, D_MODEL), lambda i: (0, 0))],
        out_specs=[row, row, pl.BlockSpec((SUBLANES, D_MODEL), lambda i: (0, 0))],
        out_shape=[jax.ShapeDtypeStruct((S, D_MODEL), F32), jax.ShapeDtypeStruct((S, D_MODEL), BF16),
                   jax.ShapeDtypeStruct((SUBLANES, D_MODEL), F32)],
        name="up_bwd_ln1", compiler_params=_cparams(("arbitrary",), 56))(dup, w_up3, dz2, xhat1, rstd1, g1)


def _mix_bwd(dz1b, w_o, proj, yab, *, tm=512):
    S = dz1b.shape[0]

    def body(dz_ref, wo_ref, ga_ref, gb_ref, y_ref, dy_ref, dg_ref):
        dmx = _dot_nt(dz_ref[...], wo_ref[...])
        for k, gt_ref in enumerate((ga_ref, gb_ref)):
            sl = slice(k * D_MODEL, (k + 1) * D_MODEL)
            sg = jax.nn.sigmoid(gt_ref[...].astype(F32))
            dy_ref[:, sl] = (dmx * sg).astype(BF16)
            dg_ref[k] = (dmx * y_ref[:, sl].astype(F32) * sg * (1.0 - sg)).astype(BF16)

    row = pl.BlockSpec((tm, D_MODEL), lambda i: (i, 0))
    wide = pl.BlockSpec((tm, 2 * D_MODEL), lambda i: (i, 0))
    return pl.pallas_call(
        body, grid=(S // tm,),
        in_specs=[row, _resident(w_o.shape), pl.BlockSpec((tm, D_MODEL), lambda i: (i, P_GA // D_MODEL)),
                  pl.BlockSpec((tm, D_MODEL), lambda i: (i, P_GB // D_MODEL)), wide],
        out_specs=[wide, pl.BlockSpec((2, tm, D_MODEL), lambda i: (0, i, 0))],
        out_shape=[jax.ShapeDtypeStruct((S, 2 * D_MODEL), BF16), jax.ShapeDtypeStruct((2, S, D_MODEL), BF16)],
        name="mix_bwd", compiler_params=_cparams(("parallel",), 40))(dz1b, w_o, proj, proj, yab)


def _conv_gate_bwd(proj, dya_in, conv_w):
    S = proj.shape[0]

    def body(b_ref, c_ref, h_ref, dy_ref, w_ref, o_ref, sm_ref, u_scr, d_scr):
        _zero_pads(u_scr, S)
        _zero_pads(d_scr, S)
        for t in range(0, S, CHUNK):
            u_scr[PAD + t:PAD + t + CHUNK, :] = c_ref[t:t + CHUNK, :].astype(F32) * h_ref[t:t + CHUNK, :].astype(F32)
        w0, w1, w2 = w_ref[0:1, :], w_ref[1:2, :], w_ref[2:3, :]
        zero = jnp.zeros((1, SLAB), F32)
        s_w0, s_w1, s_w2 = zero, zero, zero
        for t in range(0, S, CHUNK):
            um, u0, up = _shifted(u_scr, t)
            dy = dy_ref[t:t + CHUNK, :].astype(F32)
            o_ref[0, t:t + CHUNK, :] = (dy * (w0 * um + w1 * u0 + w2 * up)).astype(BF16)
            dcv = dy * b_ref[t:t + CHUNK, :].astype(F32)
            d_scr[PAD + t:PAD + t + CHUNK, :] = dcv
            s_w0 = s_w0 + jnp.sum(dcv * um, axis=0, keepdims=True)
            s_w1 = s_w1 + jnp.sum(dcv * u0, axis=0, keepdims=True)
            s_w2 = s_w2 + jnp.sum(dcv * up, axis=0, keepdims=True)
        for t in range(0, S, CHUNK):
            dm, d0, dp = _shifted(d_scr, t)
            du = w0 * dp + w1 * d0 + w2 * dm
            o_ref[1, t:t + CHUNK, :] = (du * h_ref[t:t + CHUNK, :].astype(F32)).astype(BF16)
            o_ref[2, t:t + CHUNK, :] = (du * c_ref[t:t + CHUNK, :].astype(F32)).astype(BF16)
        sm_ref[...] = _rows8([s_w0, s_w1, s_w2], SLAB)

    return pl.pallas_call(
        body, grid=(D_CONV // SLAB,),
        in_specs=[_slab_spec(S, P_B), _slab_spec(S, P_C), _slab_spec(S, P_H),
                  pl.BlockSpec((S, SLAB), lambda j: (0, j)), pl.BlockSpec((3, SLAB), lambda j: (0, j))],
        out_specs=[pl.BlockSpec((3, S, SLAB), lambda j: (0, 0, j)), pl.BlockSpec((SUBLANES, SLAB), lambda j: (0, j))],
        out_shape=[jax.ShapeDtypeStruct((3, S, D_CONV), BF16), jax.ShapeDtypeStruct((SUBLANES, D_CONV), F32)],
        scratch_shapes=[pltpu.VMEM((S + 2 * PAD, SLAB), F32)] * 2,
        name="conv_gate_bwd", compiler_params=_cparams(("parallel",), 48))(proj, proj, proj, dya_in, conv_w)


def _comb_bwd(dyab, w_b, comb, lse_tot, *, tm=512):
    S = comb.shape[0]
    widths, dtypes = (GROUP_W, LANES, LANES), (BF16, F32, F32)

    def body(dy_ref, wb_ref, c_ref, lt_ref, e_ref, *rest):
        outs, scr = rest[:3 * N_GROUPS], rest[3 * N_GROUPS:]
        dcb = _dot_nt(dy_ref[...], wb_ref[...]).astype(BF16)
        dc = dcb.astype(F32)
        delta = lax.dot_general(dc * c_ref[...], e_ref[...], (((1,), (1,)), ((), ())),
                                preferred_element_type=F32, precision=lax.Precision.HIGHEST)
        for k, (val, dtype) in enumerate(zip((dc, lt_ref[...], delta), dtypes)):
            outs[k][0] = val.astype(dtype)
            _to_residue(val, [outs[3 * (1 + j) + k] for j in range(len(DILS))], DILS, tm, dtype,
                        scr[:val.shape[1] // LANES])

    out_specs, out_shape = [], []
    for _, d in GROUPS:
        out_specs += [_res_spec(d, tm, w) for w in widths]
        out_shape += [jax.ShapeDtypeStruct((d, S // d, w), t) for w, t in zip(widths, dtypes)]
    res = pl.pallas_call(
        body, grid=(S // tm,),
        in_specs=[pl.BlockSpec((tm, D_MODEL), lambda i: (i, 1)), _resident(w_b.shape),
                  pl.BlockSpec((tm, GROUP_W), lambda i: (i, 0)), pl.BlockSpec((tm, LANES), lambda i: (i, 0)),
                  _resident((LANES, GROUP_W))],
        out_specs=out_specs, out_shape=out_shape, scratch_shapes=_lane_scratch(tm, GROUP_W),
        name="comb_bwd", compiler_params=_cparams(("parallel",), 32))(dyab, w_b, comb, lse_tot, _expand_heads())
    return [tuple(res[3 * g:3 * g + 3]) for g in range(N_GROUPS)]


def _attn_bwd(qkv, col0, g, dcomb, lse_tot, delta):
    dil, sub, _ = qkv.shape
    nb = sub // TQ

    def body(q_ref, kp, kc, kn, vp, vc, vn, do_ref, lse_ref, dl_ref, bias_ref, dq_ref, dk_ref, dv_ref,
             ak, av, dqt_scr, s_scr, dp_scr, ds_scr, p_scr):
        i = pl.program_id(1)

        @pl.when(i == 0)
        def _():
            ak[...] = jnp.zeros_like(ak)
            av[...] = jnp.zeros_like(av)

        @pl.when(i < nb)
        def _():
            kwin = _window(kp, kc, kn)
            vwin = _window(vp, vc, vn)
            q = q_ref[...] * ATT_SCALE
            do = do_ref[...]
            lse_t, dl_t = lse_ref[...].T, dl_ref[...].T
            for h in range(HEADS_PER_GROUP):
                s_scr[h] = _dot_nt(_pair(kwin, h), _own_lanes(_pair(q, h), h))
                dp_scr[h] = _dot_nt(_pair(vwin, h), _own_lanes(_pair(do, h), h))
            for h in range(HEADS_PER_GROUP):
                p = jnp.exp(s_scr[h] + _slope(g, h) * bias_ref[...] - lse_t[h:h + 1, :])
                ds_scr[h] = (p * (dp_scr[h] - dl_t[h:h + 1, :])).astype(BF16)
                p_scr[h] = p.astype(BF16)
            for h in range(HEADS_PER_GROUP):
                dqt_scr[h * HEAD_DIM:(h + 1) * HEAD_DIM, :] = _own_rows(_dot_tn(_pair(kwin, h), ds_scr[h]), h)
            for h in range(0, HEADS_PER_GROUP, 2):
                cols = slice(h * HEAD_DIM, (h + 2) * HEAD_DIM)
                q2 = jnp.concatenate([_own_lanes(_pair(q, h), h), _own_lanes(_pair(q, h), h + 1)], axis=0)
                do2 = jnp.concatenate([_own_lanes(_pair(do, h), h), _own_lanes(_pair(do, h), h + 1)], axis=0)
                ak[RADIUS:RADIUS + 2 * TQ, cols] += _dot(jnp.concatenate([ds_scr[h], ds_scr[h + 1]], axis=1), q2)
                av[RADIUS:RADIUS + 2 * TQ, cols] += _dot(jnp.concatenate([p_scr[h], p_scr[h + 1]], axis=1), do2)
            dq_ref[...] = (dqt_scr[...].T * ATT_SCALE).astype(BF16)

        dk_ref[...] = ak[0:TQ, :].astype(BF16)
        dv_ref[...] = av[0:TQ, :].astype(BF16)
        ak[0:2 * TQ, :] = ak[TQ:3 * TQ, :]
        av[0:2 * TQ, :] = av[TQ:3 * TQ, :]
        ak[2 * TQ:3 * TQ, :] = jnp.zeros((TQ, GROUP_W), F32)
        av[2 * TQ:3 * TQ, :] = jnp.zeros((TQ, GROUP_W), F32)

    tok = pl.BlockSpec((None, TQ, GROUP_W), lambda r, i: (r, jnp.minimum(i, nb - 1), 0))
    stat = pl.BlockSpec((None, TQ, LANES), lambda r, i: (r, jnp.minimum(i, nb - 1), 0))
    dkv_spec = pl.BlockSpec((None, TQ, GROUP_W), lambda r, i: (r, jnp.maximum(i - 1, 0), 0))
    return pl.pallas_call(
        body, grid=(dil, nb + 1), in_specs=_qkv_specs(nb, col0) + [tok, stat, stat, _bias_spec(nb)],
        out_specs=[tok, dkv_spec, dkv_spec], out_shape=[jax.ShapeDtypeStruct((dil, sub, GROUP_W), BF16)] * 3,
        scratch_shapes=[pltpu.VMEM((3 * TQ, GROUP_W), F32)] * 2 + [pltpu.VMEM((GROUP_W, TQ), F32)]
        + [pltpu.VMEM((HEADS_PER_GROUP, 2 * TQ, TQ), F32)] * 2 + [pltpu.VMEM((HEADS_PER_GROUP, 2 * TQ, TQ), BF16)] * 2,
        name=f"attn_bwd_g{g}", compiler_params=_cparams(("arbitrary", "arbitrary"), 32))(
            *([qkv] * 7), dcomb, lse_tot, delta, _attn_bias_table(g))


def _in_bwd_ln0(dgated, dqkv, w_nat, w_dil, dz1, x, g0, *, tm=256):
    S = x.shape[0]
    n_gated, n_in = len(dgated), 3 * N_GROUPS

    def body(*refs):
        g_refs, d_refs = refs[:n_gated], refs[n_gated:n_gated + n_in]
        wn_ref, *wd_refs = refs[n_gated + n_in:n_gated + n_in + N_GROUPS]
        dz_ref, x_ref, g_ref, gx_ref, st_ref, *tmp_ref = refs[n_gated + n_in + N_GROUPS:]
        dh = ALPHA * dz_ref[...]
        col = 0
        for ref in g_refs:
            for k in range(ref.shape[0]):
                dh = dh + _dot_nt(ref[k], wn_ref[:, col:col + D_MODEL])
                col += D_MODEL
        for g, (_, d) in enumerate(GROUPS):
            rows = [jnp.concatenate([d_refs[3 * g + k][r] for k in range(3)], axis=1) for r in range(d)]
            w = wn_ref[:, col:col + QKV_W] if d == 1 else wd_refs[g - 1][...]
            res = _dot_nt(jnp.concatenate(rows, axis=0), w)
            if d == 1:
                dh = dh + res
            else:
                n = tm // d
                dh = dh + _from_residue(lambda r: res[r * n:(r + 1) * n, :], d, tm, tmp_ref)
        xhat, rstd = _ln_stats(x_ref[...])
        gx_ref[...] = _ln_bwd(dh, xhat, rstd, g_ref[...])
        upd = _rows8([jnp.sum(dh * xhat, axis=0, keepdims=True), jnp.sum(dh, axis=0, keepdims=True)], D_MODEL)

        @pl.when(pl.program_id(0) == 0)
        def _():
            st_ref[...] = upd

        @pl.when(pl.program_id(0) != 0)
        def _():
            st_ref[...] += upd

    row = pl.BlockSpec((tm, D_MODEL), lambda i: (i, 0))
    g_specs = [pl.BlockSpec((a.shape[0], tm, D_MODEL), lambda i: (0, i, 0)) for a in dgated]
    d_specs = []
    for _, d in GROUPS:
        d_specs += [_res_spec(d, tm, GROUP_W)] * 3
    operands = list(dgated) + [a for grp in dqkv for a in grp] + [w_nat] + list(w_dil) + [dz1, x, g0]
    return pl.pallas_call(
        body, grid=(S // tm,),
        in_specs=g_specs + d_specs + [_resident(w_nat.shape)] + [_resident(w.shape) for w in w_dil]
        + [row, row, pl.BlockSpec((1, D_MODEL), lambda i: (0, 0))],
        out_specs=[row, pl.BlockSpec((SUBLANES, D_MODEL), lambda i: (0, 0))],
        out_shape=[jax.ShapeDtypeStruct((S, D_MODEL), F32), jax.ShapeDtypeStruct((SUBLANES, D_MODEL), F32)],
        scratch_shapes=_lane_scratch(tm, D_MODEL),
        name="in_bwd_ln0", compiler_params=_cparams(("arbitrary",), 52))(*operands)


HBM_SPEC = pl.BlockSpec(memory_space=pltpu.HBM)


def _place():
    x, y, c = lax.axis_index("x"), lax.axis_index("y"), lax.axis_index("c")
    chips = [(1 - x, y), (x, 1 - y), (1 - x, 1 - y)]
    return x, y, c, chips


def _allgather_shards(shards, after, *, name, collective_id):
    n = len(shards)
    per = 6

    def body(*refs):
        ins, outs = refs[:n], refs[n + len(after):2 * n + len(after)]
        send_sems, recv_sems, loc_sems = refs[2 * n + len(after):]
        x, y, c, chips = _place()
        me = 2 * x + y
        sib = (x, y, 1 - c)
        peers = [sib] + [(px, py, c) for px, py in chips]
        barrier = pltpu.get_barrier_semaphore()
        for peer in peers:
            pl.semaphore_signal(barrier, inc=1, device_id=peer, device_id_type=MESH)
        pl.semaphore_wait(barrier, len(peers))

        def rcopy(w, k, src, dst, to):
            return pltpu.make_async_remote_copy(src_ref=src, dst_ref=dst, send_sem=send_sems.at[per * w + k],
                                                recv_sem=recv_sems.at[per * w + k], device_id=to, device_id_type=MESH)

        split = [s.shape[0] == N_CORES for s in shards]
        half = lambda w: c if split[w] else 0
        local, sends = [], []
        for w in range(n):
            cp = pltpu.make_async_copy(ins[w], outs[w].at[me], loc_sems.at[w])
            cp.start()
            local.append(cp)
            for j, (px, py) in enumerate(chips):
                cp = rcopy(w, j, ins[w].at[half(w)], outs[w].at[me, half(w)], (px, py, c))
                cp.start()
                sends.append(cp)
        for w in range(n):
            for j, (px, py) in enumerate(chips):
                slot = outs[w].at[2 * px + py, half(w)]
                rcopy(w, j, slot, slot, (px, py, c)).wait_recv()
                if split[w]:
                    cp = rcopy(w, 3 + j, slot, slot, sib)
                    cp.start()
                    sends.append(cp)
        for w in range(n):
            if split[w]:
                for j, (px, py) in enumerate(chips):
                    slot = outs[w].at[2 * px + py, 1 - c]
                    rcopy(w, 3 + j, slot, slot, sib).wait_recv()
        for cp in sends:
            cp.wait_send()
        for cp in local:
            cp.wait()

    return pl.kernel(
        body, out_type=[jax.ShapeDtypeStruct((N_CHIPS,) + s.shape, s.dtype) for s in shards],
        mesh=plsc.ScalarSubcoreMesh(axis_name="sequencer", num_cores=1),
        scratch_types=[pltpu.SemaphoreType.DMA((per * n,)), pltpu.SemaphoreType.DMA((per * n,)),
                       pltpu.SemaphoreType.DMA((n,))],
        name=name, compiler_params=pltpu.CompilerParams(collective_id=collective_id))(*shards, *after)


def _exchange_grads(grads, *, name, collective_id):
    n = len(grads)
    per = 7

    def body(*refs):
        ins, outs = refs[:n], refs[n:2 * n]
        send_sems, recv_sems, loc_sems = refs[2 * n:]
        x, y, c, chips = _place()
        me = 2 * x + y
        sib = (x, y, 1 - c)
        peers = [sib] + [(px, py, c) for px, py in chips]
        barrier = pltpu.get_barrier_semaphore()
        for peer in peers:
            pl.semaphore_signal(barrier, inc=1, device_id=peer, device_id_type=MESH)
        pl.semaphore_wait(barrier, len(peers))

        def rcopy(w, k, src, dst, to):
            return pltpu.make_async_remote_copy(src_ref=src, dst_ref=dst, send_sem=send_sems.at[per * w + k],
                                                recv_sem=recv_sems.at[per * w + k], device_id=to, device_id_type=MESH)

        local, sends = [], []
        for w in range(n):
            cp = pltpu.make_async_copy(ins[w].at[me], outs[w].at[c, me], loc_sems.at[w])
            cp.start()
            local.append(cp)
            cp = rcopy(w, 0, ins[w].at[me], outs[w].at[c, me], sib)
            cp.start()
            sends.append(cp)
            for j, (px, py) in enumerate(chips):
                cp = rcopy(w, 1 + j, ins[w].at[2 * px + py], outs[w].at[c, me], (px, py, c))
                cp.start()
                sends.append(cp)
        for w in range(n):
            for j, (px, py) in enumerate(chips):
                slot = outs[w].at[c, 2 * px + py]
                rcopy(w, 1 + j, slot, slot, (px, py, c)).wait_recv()
                cp = rcopy(w, 4 + j, slot, slot, sib)
                cp.start()
                sends.append(cp)
        for w in range(n):
            slot = outs[w].at[1 - c, me]
            rcopy(w, 0, slot, slot, sib).wait_recv()
            for j, (px, py) in enumerate(chips):
                slot = outs[w].at[1 - c, 2 * px + py]
                rcopy(w, 4 + j, slot, slot, sib).wait_recv()
        for cp in sends:
            cp.wait_send()
        for cp in local:
            cp.wait()

    return pl.kernel(
        body, out_type=[jax.ShapeDtypeStruct((N_CORES,) + g.shape, g.dtype) for g in grads],
        mesh=plsc.ScalarSubcoreMesh(axis_name="sequencer", num_cores=1),
        scratch_types=[pltpu.SemaphoreType.DMA((per * n,)), pltpu.SemaphoreType.DMA((per * n,)),
                       pltpu.SemaphoreType.DMA((n,))],
        name=name, compiler_params=pltpu.CompilerParams(collective_id=collective_id))(*grads)


def _allgather_small(vec, after):
    def body(v_ref, _, o_ref, send_sems, recv_sems, loc_sem):
        x, y, c = lax.axis_index("x"), lax.axis_index("y"), lax.axis_index("c")
        me = 4 * x + 2 * y + c

        def peer(k):
            flip = lambda v, bit: 1 - v if (k >> bit) & 1 else v
            return flip(x, 2), flip(y, 1), flip(c, 0)

        loc = pltpu.make_async_copy(v_ref, o_ref.at[me], loc_sem)
        loc.start()
        sends = []
        for k in range(1, N_DEV):
            cp = pltpu.make_async_remote_copy(src_ref=v_ref, dst_ref=o_ref.at[me], send_sem=send_sems.at[k - 1],
                                              recv_sem=recv_sems.at[k - 1], device_id=peer(k), device_id_type=MESH)
            cp.start()
            sends.append(cp)
        for k in range(1, N_DEV):
            px, py, pc = peer(k)
            pltpu.make_async_remote_copy(src_ref=v_ref, dst_ref=o_ref.at[4 * px + 2 * py + pc],
                                         send_sem=send_sems.at[k - 1], recv_sem=recv_sems.at[k - 1],
                                         device_id=(px, py, pc), device_id_type=MESH).wait_recv()
        for cp in sends:
            cp.wait_send()
        loc.wait()

    return pl.pallas_call(
        body, in_specs=[HBM_SPEC, HBM_SPEC], out_specs=HBM_SPEC,
        out_shape=jax.ShapeDtypeStruct((N_DEV,) + vec.shape, vec.dtype),
        scratch_shapes=[pltpu.SemaphoreType.DMA((N_DEV - 1,)), pltpu.SemaphoreType.DMA((N_DEV - 1,)),
                        pltpu.SemaphoreType.DMA],
        name="allgather_small")(vec, after)


def _adamw(w, g, m, v):
    m = ADAM_B1 * m + (1.0 - ADAM_B1) * g
    v = ADAM_B2 * v + (1.0 - ADAM_B2) * (g * g)
    m_hat = m / (1.0 - ADAM_B1 ** ADAM_STEP)
    v_hat = v / (1.0 - ADAM_B2 ** ADAM_STEP)
    delta = -ADAM_LR * (m_hat / (jnp.sqrt(v_hat) + ADAM_EPS) + ADAM_WD * w)
    return delta, m, v


def _reduce_adamw(parts, w, m, v, *, tr, name):
    R, C = w.shape

    def body(p_ref, w_ref, m_ref, v_ref, g_ref, d_ref, nm_ref, nv_ref):
        def core_sum(cc):
            s = p_ref[cc, 0].astype(F32)
            for k in range(1, N_CHIPS):
                s = s + p_ref[cc, k].astype(F32)
            return s

        g = core_sum(0) + core_sum(1)
        delta, nm, nv = _adamw(w_ref[...], g, m_ref[...], v_ref[...])
        g_ref[...] = g
        d_ref[...] = delta
        nm_ref[...] = nm
        nv_ref[...] = nv

    blk = pl.BlockSpec((tr, C), lambda i: (i, 0))
    return pl.pallas_call(
        body, grid=(R // tr,),
        in_specs=[pl.BlockSpec((N_CORES, N_CHIPS, tr, C), lambda i: (0, 0, i, 0)), blk, blk, blk],
        out_specs=[blk] * 4, out_shape=[jax.ShapeDtypeStruct((R, C), F32)] * 4,
        name=name, compiler_params=_cparams(("parallel",), 40))(parts, w, m, v)


def _reduce_adamw_vectors(allv, offs, ws, ms, vs):
    n = len(ws)

    def body(a_ref, *refs):
        w_refs, m_refs, v_refs = refs[:n], refs[n:2 * n], refs[2 * n:3 * n]
        tot_ref, outs = refs[3 * n], refs[3 * n + 1:]
        s = a_ref[0]
        for d in range(1, N_DEV):
            s = s + a_ref[d]
        tot_ref[...] = s
        for k in range(n):
            g = s[:, offs[k]:offs[k] + w_refs[k].shape[1]]
            delta, nm, nv = _adamw(w_refs[k][...], g, m_refs[k][...], v_refs[k][...])
            for ref, val in zip(outs[4 * k:4 * k + 4], (g, delta, nm, nv)):
                ref[...] = val

    out_shape = [jax.ShapeDtypeStruct(allv.shape[1:], F32)]
    for w in ws:
        out_shape += [jax.ShapeDtypeStruct(w.shape, F32)] * 4
    res = pl.pallas_call(body, out_shape=out_shape, name="reduce_adamw_vectors",
                         compiler_params=_cparams((), 40))(allv, *ws, *ms, *vs)
    return res[0], [tuple(res[1 + 4 * k:5 + 4 * k]) for k in range(n)]


def _adamw_taps(ws, gs, ms, vs):
    n = len(ws)

    def body(*refs):
        outs = refs[4 * n:]
        for k in range(n):
            res = _adamw(refs[k][...], refs[n + k][...], refs[2 * n + k][...], refs[3 * n + k][...])
            for ref, val in zip(outs[3 * k:3 * k + 3], res):
                ref[...] = val

    out_shape = []
    for w in ws:
        out_shape += [jax.ShapeDtypeStruct(w.shape, F32)] * 3
    res = pl.pallas_call(body, out_shape=out_shape, name="adamw_taps")(*ws, *gs, *ms, *vs)
    return [tuple(res[3 * k:3 * k + 3]) for k in range(n)]


def _pack(pieces):
    flat, offs, n = [], [], 0
    for p in pieces:
        size = -(-p.size // LANES) * LANES
        flat.append(jnp.pad(p.reshape(-1), (0, size - p.size)))
        offs.append(n)
        n += size
    return jnp.concatenate(flat).reshape(1, n), offs


def _local_step(x, target, p, wfull, on_ready=lambda group: None, before_ln0=()):
    S = x.shape[0]
    dils = [d for _, d in GROUPS]

    h0, h0b, *h0_res = _ln0_fwd(x, p["ln0_g"], p["ln0_b"], before_ln0)
    h0_rows = [h0b] + [h.reshape(S, D_MODEL) for h in h0_res]

    if isinstance(wfull, dict):
        w_in3, pending = wfull["w_in"], None
    else:
        w_in3, launch_rest, assemble = wfull
        w_in3, h0b = lax.optimization_barrier((w_in3, h0b))
        pending = launch_rest(h0b)

    w_blocks = w_in3.transpose(1, 0, 2).reshape(D_MODEL, N_BLK, GROUP_W)
    w_perm = jnp.concatenate([w_blocks[:, b] for b in PERM], axis=1)
    b_blocks = p["b_in"].reshape(N_BLK, GROUP_W)
    b_perm = jnp.concatenate([b_blocks[b] for b in PERM]).reshape(1, N_IN)
    w_nat, b_nat = w_perm[:, :N_NAT], b_perm[:, :N_NAT]
    qkv_cols = [slice(P_Q0 + g * QKV_W, P_Q0 + (g + 1) * QKV_W) for g in range(N_GROUPS)]
    w_qkv = [w_perm[:, c] for c in qkv_cols]

    proj = _mm_nn(h0b, w_nat, b_nat, tm=512, tn=N_NAT // 2, out_dtype=BF16, name="proj")
    qkv = [proj[None]]
    for g in range(1, N_GROUPS):
        t = _mm_nn(h0_rows[g], w_qkv[g], b_perm[:, qkv_cols[g]], tm=512, tn=QKV_W, out_dtype=BF16, name=f"proj_qkv{g}")
        qkv.append(t.reshape(dils[g], S // dils[g], QKV_W))
    if pending is not None:
        pending, qkv = lax.optimization_barrier((pending, qkv))
        proj = qkv[0][0]
        wfull = assemble(pending)
    w_up3 = wfull["w_up"]
    w_a, w_o, w_down, w_b = wfull["w_a"], wfull["w_o"], wfull["w_down"], wfull["w_b"]
    conv_w, ffn_conv_w = wfull["conv_w"], wfull["ffn_conv_w"]
    col0 = [P_Q0 // GROUP_W] + [0] * (N_GROUPS - 1)
    ya_in = _conv_gate_fwd(proj, conv_w)
    att = [_attn_fwd(qkv[g], col0[g], g) for g in range(N_GROUPS)]
    comb, comb_b, lse_tot = _attn_combine([a[0] for a in att], [a[1] for a in att])
    yab, mixin = _branch_mix(ya_in, comb_b, w_a, w_b, proj)
    xhat1, rstd1, h1b = _mix_ln1(mixin, w_o, p["b_o"], h0, p["ln1_g"], p["ln1_b"])
    up = _mm_nn(h1b, w_up3, p["b_up"], tm=512, tn=w_up3.shape[2], out_dtype=BF16, name="up")
    f = _ffn_conv_fwd(up, ffn_conv_w, p["ffn_conv_b"])
    dz2, dz2b, st2 = _down_ln2_loss(f, w_down, p["b_down"], xhat1, p["ln1_g"], p["ln1_b"],
                                    p["ln2_g"], p["ln2_b"], target)

    gw = {}
    gw["w_down"] = _mm_tn(f, dz2b, n_out=1, tn=D_MODEL, ts=1024, g_block=(1024, D_MODEL),
                          g_map=lambda j, s: (s, 0), name="grad_w_down").reshape(N_CHIPS, D_FF // N_CHIPS, D_MODEL)
    df = _mm_nt(dz2b, w_down, tm=512, name="df")
    dup, sm_ffn = _ffn_conv_bwd(up, df, ffn_conv_w, p["ffn_conv_b"])
    up_tn = w_up3.shape[2]
    up_pp = D_FF // up_tn
    gw["w_up"] = _mm_tn(h1b, dup, n_out=N_CHIPS, tn=up_tn, ts=1024, g_block=(None, 1024, up_tn),
                        g_map=lambda j, s: (j // up_pp, s, j % up_pp), name="grad_w_up")
    (gw["w_down"], gw["w_up"]), dup = lax.optimization_barrier(((gw["w_down"], gw["w_up"]), dup))
    on_ready({n: gw[n] for n in ("w_down", "w_up")})
    dz1, dz1b, st1 = _up_bwd_ln1(dup, w_up3, dz2, xhat1, rstd1, p["ln1_g"])

    gw["w_o"] = _mm_tn(mixin, dz1b, n_out=1, tn=D_MODEL, ts=512, g_block=(512, D_MODEL),
                       g_map=lambda j, s: (s, 0), name="grad_w_o").reshape(N_CHIPS, D_MODEL // N_CHIPS, D_MODEL)
    dyab, dgab = _mix_bwd(dz1b, w_o, proj, yab)
    gw["w_a"] = _mm_tn(ya_in, dyab, n_out=1, tn=D_MODEL, ts=512, g_block=(512, D_MODEL),
                       g_map=lambda j, s: (s, 0), name="grad_w_a").reshape(N_CHIPS, D_CONV // N_CHIPS, D_MODEL)
    gw_b = _mm_tn(comb_b, dyab, n_out=1, tn=D_MODEL, ts=1024, g_block=(1024, D_MODEL),
                  g_map=lambda j, s: (s, 1), name="grad_w_b")
    gw["w_b"] = gw_b.reshape(GROUP_W, N_CHIPS, D_MODEL // N_CHIPS).transpose(1, 0, 2)
    (gw["w_o"], gw["w_a"], gw["w_b"]), dyab = lax.optimization_barrier(((gw["w_o"], gw["w_a"], gw["w_b"]), dyab))
    on_ready({n: gw[n] for n in ("w_o", "w_a", "w_b")})
    dya_in = _mm_nt(dyab, w_a, tm=512, a_col=0, name="dya_in")
    dbch, sm_conv = _conv_gate_bwd(proj, dya_in, conv_w)
    att_stats = _comb_bwd(dyab, w_b, comb, lse_tot)
    dqkv = [_attn_bwd(qkv[g], col0[g], g, *att_stats[g]) for g in range(N_GROUPS)]

    w_pieces, b_pieces = [], []
    for nm, planes in (("bch", dbch), ("gab", dgab)):
        pw, pc = _mm_tn(h0b, planes, n_out=planes.shape[0], tn=D_MODEL, ts=1024, g_block=(None, 1024, D_MODEL),
                        g_map=lambda j, s: (j, s, 0), colsum=True, name="grad_w_in_" + nm)
        w_pieces.append(pw.transpose(1, 0, 2).reshape(D_MODEL, planes.shape[0] * D_MODEL))
        b_pieces.append(pc[0])
    for g in range(N_GROUPS):
        pw, pc = _mm_tn_cat(h0_rows[g], [a.reshape(S, GROUP_W) for a in dqkv[g]], ts=1024, name=f"grad_w_in_qkv{g}")
        w_pieces.append(pw)
        b_pieces.append(pc[0])
    dw_blocks = jnp.concatenate(w_pieces, axis=1).reshape(D_MODEL, N_BLK, GROUP_W)
    dw_ref = jnp.concatenate([dw_blocks[:, b] for b in INV_PERM], axis=1)
    gw["w_in"] = dw_ref.reshape(D_MODEL, N_CHIPS, N_IN // N_CHIPS).transpose(1, 0, 2)
    gw["w_in"], dz1 = lax.optimization_barrier((gw["w_in"], dz1))
    on_ready({"w_in": gw["w_in"]})
    db_blocks = jnp.concatenate(b_pieces).reshape(N_BLK, GROUP_W)
    grad_b_in = jnp.concatenate([db_blocks[b] for b in INV_PERM])

    grad_x, st0 = _in_bwd_ln0([dbch, dgab], dqkv, w_nat, w_qkv[1:], dz1, x, p["ln0_g"])

    small = {
        "loss": st2[2:3, 0:1],
        "ln0_g": st0[0], "ln0_b": st0[1], "b_in": grad_b_in, "conv_w": sm_conv[0:3],
        "b_o": st1[2], "ln1_g": st1[0], "ln1_b": st1[1],
        "b_up": jnp.concatenate([sm_ffn[0], sm_ffn[1]]), "ffn_conv_w": sm_ffn[3:6], "ffn_conv_b": sm_ffn[2],
        "b_down": st2[3], "ln2_g": st2[0], "ln2_b": st2[1],
    }
    return grad_x, gw, small


BIG = ("w_in", "w_a", "w_b", "w_o", "w_up", "w_down")
CONV = ("conv_w", "ffn_conv_w")
VECS = ("ln0_g", "ln0_b", "b_in", "b_o", "ln1_g", "ln1_b", "b_up", "ffn_conv_b", "b_down", "ln2_g", "ln2_b")
ORDER = ("ln0_g", "ln0_b", "w_in", "b_in", "conv_w", "w_a", "w_b", "w_o", "b_o", "ln1_g", "ln1_b", "w_up", "b_up",
         "ffn_conv_w", "ffn_conv_b", "w_down", "b_down", "ln2_g", "ln2_b")
SMALL_ORDER = ("loss",) + VECS + CONV


def _step(x, target, W, Mo, Vo):
    x2, t2 = x[0], target[0]
    big2 = {n: W[n][0] for n in BIG}
    halves = lambda a: a.astype(BF16).reshape(N_CORES, a.shape[0] // N_CORES, a.shape[1])
    whole = lambda g: g.reshape(N_CHIPS, g.shape[1] * g.shape[2], g.shape[3])
    later = tuple(n for n in BIG if n != "w_in")
    w_in_halves = halves(big2["w_in"])
    first = _allgather_shards([w_in_halves], [], name="allgather_w_in", collective_id=1)

    def launch_rest(h0b):
        return _allgather_shards([halves(big2[n]) for n in later] + [W[n] for n in CONV], [h0b],
                                 name="allgather_rest", collective_id=2)

    def assemble(rest):
        gathered = {n: whole(g) for n, g in zip(later + CONV, rest)}
        return {
            "w_up": gathered["w_up"],
            "w_a": gathered["w_a"].reshape(D_CONV, D_MODEL), "w_o": gathered["w_o"].reshape(D_MODEL, D_MODEL),
            "w_down": gathered["w_down"].reshape(D_FF, D_MODEL),
            "w_b": gathered["w_b"].transpose(1, 0, 2).reshape(GROUP_W, D_MODEL),
            "conv_w": gathered["conv_w"].transpose(1, 0, 2).reshape(3, D_CONV),
            "ffn_conv_w": gathered["ffn_conv_w"].transpose(1, 0, 2).reshape(3, D_FF),
        }

    pvec = {n: W[n].reshape(1, -1) for n in VECS}

    parts = {}
    exchange_ids = iter((3, 4, 5))

    def exchange(group):
        names = tuple(group)
        res = _exchange_grads([group[n] for n in names], name="exchange_" + "_".join(names),
                              collective_id=next(exchange_ids))
        parts.update(zip(names, res))

    grad_x, _, small = _local_step(x2, t2, pvec, (whole(first[0]), launch_rest, assemble), exchange,
                                   before_ln0=[w_in_halves])
    out = {}
    for n in BIG:
        tr = {"w_in": 128, "w_up": 128, "w_b": 128}.get(n, big2[n].shape[0] // 4)
        g, d, nm, nv = _reduce_adamw(parts[n], big2[n], Mo[n][0], Vo[n][0], tr=tr, name="adamw_" + n)
        out[n] = tuple(a[None] for a in (g, d, nm, nv))

    vec, offs = _pack([small[n] for n in SMALL_ORDER])
    off = dict(zip(SMALL_ORDER, offs))
    row = lambda a: a.reshape(1, -1)
    allv = _allgather_small(vec, parts["w_in"])
    tot, vec_out = _reduce_adamw_vectors(allv, [off[n] for n in VECS], [row(W[n]) for n in VECS],
                                         [row(Mo[n]) for n in VECS], [row(Vo[n]) for n in VECS])
    for n, res in zip(VECS, vec_out):
        out[n] = tuple(a.reshape(W[n].shape) for a in res)
    loss = tot[0, off["loss"]]
    chip = 2 * lax.axis_index("x") + lax.axis_index("y")
    taps_g = []
    for n in CONV:
        width = W[n].shape[2]
        full = lax.slice(tot, (0, off[n]), (1, off[n] + 3 * N_CHIPS * width)).reshape(3, N_CHIPS * width)
        taps_g.append(lax.dynamic_slice_in_dim(full, chip * width, width, axis=1))
    taps_out = _adamw_taps([W[n][0] for n in CONV], taps_g, [Mo[n][0] for n in CONV], [Vo[n][0] for n in CONV])
    for n, g, res in zip(CONV, taps_g, taps_out):
        out[n] = tuple(a[None] for a in (g,) + res)

    res = [loss, grad_x[None]]
    for k in range(4):
        res += [out[n][k] for n in ORDER]
    return tuple(res)


def kernel(x, ln0_g, ln0_b, w_in, b_in, conv_w, w_a, w_b, w_o, b_o, ln1_g, ln1_b, w_up, b_up, ffn_conv_w, ffn_conv_b, w_down, b_down, ln2_g, ln2_b, loss_target, m_ln0_g, m_ln0_b, m_w_in, m_b_in, m_conv_w, m_w_a, m_w_b, m_w_o, m_b_o, m_ln1_g, m_ln1_b, m_w_up, m_b_up, m_ffn_conv_w, m_ffn_conv_b, m_w_down, m_b_down, m_ln2_g, m_ln2_b, v_ln0_g, v_ln0_b, v_w_in, v_b_in, v_conv_w, v_w_a, v_w_b, v_w_o, v_b_o, v_ln1_g, v_ln1_b, v_w_up, v_b_up, v_ffn_conv_w, v_ffn_conv_b, v_w_down, v_b_down, v_ln2_g, v_ln2_b):
    W = dict(zip(ORDER, (ln0_g, ln0_b, w_in, b_in, conv_w, w_a, w_b, w_o, b_o, ln1_g, ln1_b, w_up, b_up,
                         ffn_conv_w, ffn_conv_b, w_down, b_down, ln2_g, ln2_b)))
    Mo = dict(zip(ORDER, (m_ln0_g, m_ln0_b, m_w_in, m_b_in, m_conv_w, m_w_a, m_w_b, m_w_o, m_b_o, m_ln1_g, m_ln1_b,
                          m_w_up, m_b_up, m_ffn_conv_w, m_ffn_conv_b, m_w_down, m_b_down, m_ln2_g, m_ln2_b)))
    Vo = dict(zip(ORDER, (v_ln0_g, v_ln0_b, v_w_in, v_b_in, v_conv_w, v_w_a, v_w_b, v_w_o, v_b_o, v_ln1_g, v_ln1_b,
                          v_w_up, v_b_up, v_ffn_conv_w, v_ffn_conv_b, v_w_down, v_b_down, v_ln2_g, v_ln2_b)))
    return _step(x, loss_target, W, Mo, Vo)
```

```python
import functools
import math

import jax
import jax.numpy as jnp
from jax import lax
from jax.experimental import pallas as pl
from jax.experimental.pallas import tpu as pltpu
from jax.experimental.pallas import tpu_sc as plsc

F32 = jnp.float32
BF16 = jnp.bfloat16

D_MODEL = 1024
D_CONV = D_MODEL
HEAD_DIM = 64
HEADS_PER_GROUP = 8
GROUPS = ((128, 1), (512, 4), (2048, 16))
N_GROUPS = len(GROUPS)
GROUP_W = HEADS_PER_GROUP * HEAD_DIM
QKV_W = N_GROUPS * GROUP_W
RADIUS = 64
D_FF = 2816
LN_EPS = 1e-5
ALPHA = 2.0 ** 0.25
MASK_VALUE = -1e30
ATT_SCALE = HEAD_DIM ** -0.5
OFF_B = 0
OFF_C = OFF_B + D_CONV
OFF_H = OFF_C + D_CONV
OFF_Q = OFF_H + D_CONV
OFF_K = OFF_Q + QKV_W
OFF_V = OFF_K + QKV_W
OFF_GA = OFF_V + QKV_W
OFF_GB = OFF_GA + D_MODEL
N_IN = OFF_GB + D_MODEL
ADAM_LR = 0.001
ADAM_B1 = 0.9
ADAM_B2 = 0.999
ADAM_EPS = 1e-08
ADAM_WD = 0.01
ADAM_STEP = 10
INV_SQRT2 = 0.7071067811865476
INV_SQRT_2PI = 0.3989422804014327

LANES = 128
SUBLANES = 8
VMEM_BYTES_V7X = 64 * 1024 * 1024
N_CHIPS = 4
N_CORES = 2
N_DEV = N_CHIPS * N_CORES
MESH = pl.DeviceIdType.MESH

N_BLK = N_IN // GROUP_W
PERM = (0, 1, 2, 3, 4, 5, 15, 16, 17, 18, 6, 9, 12, 7, 10, 13, 8, 11, 14)
INV_PERM = tuple(PERM.index(b) for b in range(N_BLK))
P_B, P_C, P_H, P_GA, P_GB, P_Q0 = 0, 1024, 2048, 3072, 4096, 5120
N_NAT = P_Q0 + QKV_W // N_GROUPS * 3
N_GATED = P_Q0

SLAB = 128
CHUNK = 256
PAD = SUBLANES
TQ = 128


def _cparams(sem, vmem_mb):
    assert vmem_mb * 1024 * 1024 < VMEM_BYTES_V7X
    return pltpu.CompilerParams(dimension_semantics=sem, vmem_limit_bytes=vmem_mb * 1024 * 1024)


def _resident(shape):
    nd = len(shape)
    return pl.BlockSpec(shape, lambda *_: (0,) * nd, pipeline_mode=pl.Buffered(1))


def _dot(a, b):
    return jnp.dot(a, b, preferred_element_type=F32)


def _dot_nt(a, b):
    return lax.dot_general(a, b, (((1,), (1,)), ((), ())), preferred_element_type=F32)


def _dot_tn(a, b):
    return lax.dot_general(a, b, (((0,), (0,)), ((), ())), preferred_element_type=F32)


def _ln_stats(z):
    mu = jnp.mean(z, -1, keepdims=True)
    zc = z - mu
    var = jnp.mean(zc * zc, -1, keepdims=True)
    rstd = lax.rsqrt(var + LN_EPS)
    return zc * rstd, rstd


def _ln_bwd(dh, xhat, rstd, g):
    dxh = dh * g
    m1 = jnp.mean(dxh, -1, keepdims=True)
    m2 = jnp.mean(dxh * xhat, -1, keepdims=True)
    return rstd * (dxh - m1 - xhat * m2)


def _rows8(rows, width):
    pad = [jnp.zeros((1, width), F32)] * (SUBLANES - len(rows))
    return jnp.concatenate(list(rows) + pad, axis=0)


def _mm_nn(a, w, bias, *, tm, tn, out_dtype, name, vmem_mb=40):
    M, K = a.shape
    if w.ndim == 3:
        assert w.shape[2] == tn
        n_tiles = w.shape[0]
        w_spec = pl.BlockSpec((None, K, tn), lambda j, i: (j, 0, 0))
    else:
        n_tiles = w.shape[1] // tn
        w_spec = pl.BlockSpec((K, tn), lambda j, i: (0, j))

    def body(a_ref, w_ref, b_ref, o_ref):
        o_ref[...] = (_dot(a_ref[...], w_ref[...]) + b_ref[...]).astype(o_ref.dtype)

    return pl.pallas_call(
        body, grid=(n_tiles, M // tm),
        in_specs=[pl.BlockSpec((tm, K), lambda j, i: (i, 0)), w_spec, pl.BlockSpec((1, tn), lambda j, i: (0, j))],
        out_specs=pl.BlockSpec((tm, tn), lambda j, i: (i, j)),
        out_shape=jax.ShapeDtypeStruct((M, n_tiles * tn), out_dtype),
        name=name, compiler_params=_cparams(("arbitrary", "parallel"), vmem_mb))(a, w, bias)


def _mm_tn(a, g, *, n_out, tn, ts, g_block, g_map, colsum=False, name, vmem_mb=48):
    S, K = a.shape
    n_s = S // ts

    def body(a_ref, g_ref, *rest):
        if colsum:
            o_ref, cs_ref, acc_ref, cacc_ref = rest
        else:
            o_ref, acc_ref = rest
        s = pl.program_id(1)

        @pl.when(s == 0)
        def _():
            acc_ref[...] = jnp.zeros_like(acc_ref)
            if colsum:
                cacc_ref[...] = jnp.zeros_like(cacc_ref)

        gv = g_ref[...]
        acc_ref[...] += _dot_tn(a_ref[...], gv)
        if colsum:
            cacc_ref[...] += jnp.broadcast_to(jnp.sum(gv.astype(F32), axis=0, keepdims=True), cacc_ref.shape)

        @pl.when(s == n_s - 1)
        def _():
            o_ref[...] = acc_ref[...].astype(o_ref.dtype)
            if colsum:
                cs_ref[...] = cacc_ref[...]

    out_specs = [pl.BlockSpec((None, K, tn), lambda j, s: (j, 0, 0))]
    out_shape = [jax.ShapeDtypeStruct((n_out, K, tn), BF16)]
    scratch = [pltpu.VMEM((K, tn), F32)]
    if colsum:
        out_specs.append(pl.BlockSpec((SUBLANES, tn), lambda j, s: (0, j)))
        out_shape.append(jax.ShapeDtypeStruct((SUBLANES, n_out * tn), F32))
        scratch.append(pltpu.VMEM((SUBLANES, tn), F32))
    res = pl.pallas_call(
        body, grid=(n_out, n_s),
        in_specs=[pl.BlockSpec((ts, K), lambda j, s: (s, 0)), pl.BlockSpec(g_block, g_map)],
        out_specs=out_specs, out_shape=out_shape, scratch_shapes=scratch,
        name=name, compiler_params=_cparams(("parallel", "arbitrary"), vmem_mb))(a, g)
    return res if colsum else res[0]


def _mm_tn_cat(a, gs, *, ts, name, vmem_mb=40):
    S, K = a.shape
    widths = [g.shape[1] for g in gs]
    n_s, total = S // ts, sum(widths)

    def body(*refs):
        a_ref, g_refs = refs[0], refs[1:1 + len(gs)]
        o_ref, cs_ref, acc_ref, cacc_ref = refs[1 + len(gs):]
        s = pl.program_id(0)

        @pl.when(s == 0)
        def _():
            acc_ref[...] = jnp.zeros_like(acc_ref)
            cacc_ref[...] = jnp.zeros_like(cacc_ref)

        av, col = a_ref[...], 0
        for g_ref, w in zip(g_refs, widths):
            gv = g_ref[...]
            acc_ref[:, col:col + w] += _dot_tn(av, gv)
            cacc_ref[:, col:col + w] += jnp.broadcast_to(jnp.sum(gv.astype(F32), axis=0, keepdims=True), (SUBLANES, w))
            col += w

        @pl.when(s == n_s - 1)
        def _():
            o_ref[...] = acc_ref[...].astype(BF16)
            cs_ref[...] = cacc_ref[...]

    return pl.pallas_call(
        body, grid=(n_s,),
        in_specs=[pl.BlockSpec((ts, K), lambda s: (s, 0))] + [pl.BlockSpec((ts, w), lambda s: (s, 0)) for w in widths],
        out_specs=[pl.BlockSpec((K, total), lambda s: (0, 0)), pl.BlockSpec((SUBLANES, total), lambda s: (0, 0))],
        out_shape=[jax.ShapeDtypeStruct((K, total), BF16), jax.ShapeDtypeStruct((SUBLANES, total), F32)],
        scratch_shapes=[pltpu.VMEM((K, total), F32), pltpu.VMEM((SUBLANES, total), F32)],
        name=name, compiler_params=_cparams(("arbitrary",), vmem_mb))(a, *gs)


DILS = tuple(d for _, d in GROUPS if d > 1)


def _res_spec(d, tm, width):
    return pl.BlockSpec((d, tm // d, width), lambda i: (0, i, 0))


def _lane_scratch(tm, width):
    return [pltpu.VMEM((tm, LANES), F32)] * (width // LANES)


def _to_residue(val, dst_refs, dils, tm, dtype, scr):
    for c, ref in enumerate(scr):
        ref[...] = val[:, c * LANES:(c + 1) * LANES]
    for dst_ref, d in zip(dst_refs, dils):
        for r in range(d):
            cols = [ref[pl.ds(r, tm // d, stride=d), :] for ref in scr]
            dst_ref[r] = jnp.concatenate(cols, axis=1).astype(dtype)


def _from_residue(rows_of, d, tm, scr):
    for r in range(d):
        v = rows_of(r).astype(F32)
        for c, ref in enumerate(scr):
            ref[pl.ds(r, tm // d, stride=d), :] = v[:, c * LANES:(c + 1) * LANES]
    return jnp.concatenate([ref[...] for ref in scr], axis=1)


def _ln0_fwd(x, g, b, after=(), *, tm=512):
    S, Dm = x.shape
    n_after = len(after)

    def body(x_ref, g_ref, b_ref, *rest):
        h_ref, hb_ref, *rest = rest[n_after:]
        xhat, _ = _ln_stats(x_ref[...])
        h = xhat * g_ref[...] + b_ref[...]
        h_ref[...] = h
        hb_ref[...] = h.astype(BF16)
        _to_residue(h, rest[:len(DILS)], DILS, tm, BF16, rest[len(DILS):])

    row = pl.BlockSpec((tm, Dm), lambda i: (i, 0))
    vec = pl.BlockSpec((1, Dm), lambda i: (0, 0))
    return pl.pallas_call(
        body, grid=(S // tm,), in_specs=[row, vec, vec] + [pl.BlockSpec(memory_space=pl.ANY)] * n_after,
        out_specs=[row, row] + [_res_spec(d, tm, Dm) for d in DILS],
        out_shape=[jax.ShapeDtypeStruct((S, Dm), F32), jax.ShapeDtypeStruct((S, Dm), BF16)]
        + [jax.ShapeDtypeStruct((d, S // d, Dm), BF16) for d in DILS],
        scratch_shapes=_lane_scratch(tm, Dm),
        name="ln0_fwd", compiler_params=_cparams(("parallel",), 32))(x, g, b, *after)


def _slab_spec(S, col0):
    return pl.BlockSpec((S, SLAB), lambda j: (0, col0 // SLAB + j))


def _zero_pads(scr, S):
    scr[0:PAD, :] = jnp.zeros((PAD, SLAB), F32)
    scr[S + PAD:S + 2 * PAD, :] = jnp.zeros((PAD, SLAB), F32)


def _shifted(scr, t):
    return (scr[PAD - 1 + t:PAD - 1 + t + CHUNK, :], scr[PAD + t:PAD + t + CHUNK, :],
            scr[PAD + 1 + t:PAD + 1 + t + CHUNK, :])


def _conv_gate_fwd(proj, conv_w):
    S = proj.shape[0]

    def body(b_ref, c_ref, h_ref, w_ref, o_ref, u_scr):
        _zero_pads(u_scr, S)
        for t in range(0, S, CHUNK):
            u_scr[PAD + t:PAD + t + CHUNK, :] = c_ref[t:t + CHUNK, :].astype(F32) * h_ref[t:t + CHUNK, :].astype(F32)
        w0, w1, w2 = w_ref[0:1, :], w_ref[1:2, :], w_ref[2:3, :]
        for t in range(0, S, CHUNK):
            um, u0, up = _shifted(u_scr, t)
            cv = w0 * um + w1 * u0 + w2 * up
            o_ref[t:t + CHUNK, :] = (b_ref[t:t + CHUNK, :].astype(F32) * cv).astype(BF16)

    return pl.pallas_call(
        body, grid=(D_CONV // SLAB,),
        in_specs=[_slab_spec(S, P_B), _slab_spec(S, P_C), _slab_spec(S, P_H),
                  pl.BlockSpec((3, SLAB), lambda j: (0, j))],
        out_specs=pl.BlockSpec((S, SLAB), lambda j: (0, j)),
        out_shape=jax.ShapeDtypeStruct((S, D_CONV), BF16),
        scratch_shapes=[pltpu.VMEM((S + 2 * PAD, SLAB), F32)],
        name="conv_gate_fwd", compiler_params=_cparams(("parallel",), 40))(proj, proj, proj, conv_w)


MASKED_DISTANCE = -1e34


def _attn_bias_table(g):
    dil = GROUPS[g][1]
    j = lax.broadcasted_iota(jnp.int32, (2 * TQ, TQ), 0)
    a = lax.broadcasted_iota(jnp.int32, (2 * TQ, TQ), 1)
    rel = jnp.abs(j - RADIUS - a)
    base = -(rel * dil).astype(F32)
    inside, after_start, before_end = rel <= RADIUS, j >= RADIUS, j < TQ + RADIUS
    variants = []
    for first, last in ((False, False), (True, False), (False, True), (True, True)):
        valid = inside & (after_start if first else True) & (before_end if last else True)
        variants.append(jnp.where(valid, base, MASKED_DISTANCE))
    return jnp.stack(variants)


def _bias_spec(nb):
    def variant(r, i):
        return (jnp.where(i == 0, 1, 0) + jnp.where(i == nb - 1, 2, 0), 0, 0)
    return pl.BlockSpec((None, 2 * TQ, TQ), variant)


def _head_stats(rows):
    pad = jnp.zeros((LANES - len(rows), TQ), F32)
    return jnp.concatenate(list(rows) + [pad], axis=0).T


def _slope(g, h):
    return 2.0 ** (-8.0 * (g * HEADS_PER_GROUP + h + 1) / (N_GROUPS * HEADS_PER_GROUP))


def _window(p_ref, c_ref, n_ref):
    return jnp.concatenate([p_ref[TQ - RADIUS:, :], c_ref[...], n_ref[:RADIUS, :]], axis=0)


def _pair(a, h):
    return a[:, (h // 2) * LANES:(h // 2 + 1) * LANES]


def _own_lanes(a, h):
    lane = lax.broadcasted_iota(jnp.int32, a.shape, 1)
    return jnp.where((lane >= HEAD_DIM) == (h % 2 == 1), a, jnp.zeros_like(a))


def _own_rows(a, h):
    return a[(h % 2) * HEAD_DIM:(h % 2 + 1) * HEAD_DIM, :]


def _qkv_specs(nb, col0):
    def spec(col, shift):
        return pl.BlockSpec((None, TQ, GROUP_W), lambda r, i: (r, jnp.clip(i + shift, 0, nb - 1), col))

    return [spec(col0, 0), spec(col0 + 1, -1), spec(col0 + 1, 0), spec(col0 + 1, 1),
            spec(col0 + 2, -1), spec(col0 + 2, 0), spec(col0 + 2, 1)]


def _attn_fwd(qkv, col0, g):
    dil, sub, _ = qkv.shape
    nb = sub // TQ

    def body(q_ref, kp, kc, kn, vp, vc, vn, bias_ref, o_ref, lse_ref, ot_scr, s_scr, p_scr):
        kwin = _window(kp, kc, kn)
        vwin = _window(vp, vc, vn)
        q = q_ref[...] * ATT_SCALE
        for h in range(HEADS_PER_GROUP):
            s_scr[h] = _dot_nt(_pair(kwin, h), _own_lanes(_pair(q, h), h))
        lse, inv_den = [], []
        for h in range(HEADS_PER_GROUP):
            s = s_scr[h] + _slope(g, h) * bias_ref[...]
            m = jnp.max(s, axis=0, keepdims=True)
            p = jnp.exp(s - m)
            den = jnp.sum(p, axis=0, keepdims=True)
            p_scr[h] = p.astype(BF16)
            inv_den.append(1.0 / den)
            lse.append(m + jnp.log(den))
        for h in range(HEADS_PER_GROUP):
            ot = _dot_tn(_pair(vwin, h), p_scr[h])
            ot_scr[h * HEAD_DIM:(h + 1) * HEAD_DIM, :] = _own_rows(ot, h) * inv_den[h]
        o_ref[...] = ot_scr[...].T
        lse_ref[...] = _head_stats(lse)

    return pl.pallas_call(
        body, grid=(dil, nb), in_specs=_qkv_specs(nb, col0) + [_bias_spec(nb)],
        out_specs=[pl.BlockSpec((None, TQ, GROUP_W), lambda r, i: (r, i, 0)),
                   pl.BlockSpec((None, TQ, LANES), lambda r, i: (r, i, 0))],
        out_shape=[jax.ShapeDtypeStruct((dil, sub, GROUP_W), F32), jax.ShapeDtypeStruct((dil, sub, LANES), F32)],
        scratch_shapes=[pltpu.VMEM((GROUP_W, TQ), F32), pltpu.VMEM((HEADS_PER_GROUP, 2 * TQ, TQ), F32),
                        pltpu.VMEM((HEADS_PER_GROUP, 2 * TQ, TQ), BF16)],
        name=f"attn_fwd_g{g}", compiler_params=_cparams(("parallel", "arbitrary"), 32))(
            *([qkv] * 7), _attn_bias_table(g))


def _expand_heads():
    h = lax.broadcasted_iota(jnp.int32, (LANES, GROUP_W), 0)
    c = lax.broadcasted_iota(jnp.int32, (LANES, GROUP_W), 1)
    return (c // HEAD_DIM == h).astype(F32)


def _dot_f32(a, b):
    return jnp.dot(a, b, preferred_element_type=F32, precision=lax.Precision.HIGHEST)


def _attn_combine(outs, lses, *, tm=512):
    S = outs[0].shape[1]
    n_col = GROUP_W // LANES

    def body(*refs):
        ins, e_ref = refs[:2 * N_GROUPS], refs[2 * N_GROUPS]
        c_ref, cb_ref, lt_ref = refs[2 * N_GROUPS + 1:2 * N_GROUPS + 4]
        scr = refs[2 * N_GROUPS + 4:]
        o, l = [ins[0][0]], [ins[N_GROUPS][0]]
        for k, d in enumerate(DILS):
            o_ref, l_ref = ins[1 + k], ins[N_GROUPS + 1 + k]
            o.append(_from_residue(lambda r: o_ref[r], d, tm, scr[k * (n_col + 1):k * (n_col + 1) + n_col]))
            l.append(_from_residue(lambda r: l_ref[r], d, tm, scr[k * (n_col + 1) + n_col:(k + 1) * (n_col + 1)]))
        m = jnp.maximum(jnp.maximum(l[0], l[1]), l[2])
        e = [jnp.exp(v - m) for v in l]
        den = e[0] + e[1] + e[2]
        comb = sum(_dot_f32(ev / den, e_ref[...]) * ov for ev, ov in zip(e, o))
        c_ref[...] = comb
        cb_ref[...] = comb.astype(BF16)
        lt_ref[...] = m + jnp.log(den)

    row = pl.BlockSpec((tm, GROUP_W), lambda i: (i, 0))
    dils = [d for _, d in GROUPS]
    return pl.pallas_call(
        body, grid=(S // tm,),
        in_specs=[_res_spec(d, tm, GROUP_W) for d in dils] + [_res_spec(d, tm, LANES) for d in dils]
        + [_resident((LANES, GROUP_W))],
        out_specs=[row, row, pl.BlockSpec((tm, LANES), lambda i: (i, 0))],
        out_shape=[jax.ShapeDtypeStruct((S, GROUP_W), F32), jax.ShapeDtypeStruct((S, GROUP_W), BF16),
                   jax.ShapeDtypeStruct((S, LANES), F32)],
        scratch_shapes=_lane_scratch(tm, GROUP_W + LANES) * len(DILS),
        name="attn_combine", compiler_params=_cparams(("parallel",), 32))(*outs, *lses, _expand_heads())


def _branch_mix(ya_in, comb_b, w_a, w_b, proj, *, tm=512):
    S = ya_in.shape[0]

    def body(ya_ref, cb_ref, wa_ref, wb_ref, ga_ref, gb_ref, yab_ref, mx_ref):
        y_a = _dot(ya_ref[...], wa_ref[...])
        y_b = _dot(cb_ref[...], wb_ref[...])
        yab_ref[:, 0:D_MODEL] = y_a.astype(BF16)
        yab_ref[:, D_MODEL:2 * D_MODEL] = y_b.astype(BF16)
        mx = jax.nn.sigmoid(ga_ref[...].astype(F32)) * y_a + jax.nn.sigmoid(gb_ref[...].astype(F32)) * y_b
        mx_ref[...] = mx.astype(BF16)

    return pl.pallas_call(
        body, grid=(S // tm,),
        in_specs=[pl.BlockSpec((tm, D_CONV), lambda i: (i, 0)), pl.BlockSpec((tm, GROUP_W), lambda i: (i, 0)),
                  pl.BlockSpec((D_CONV, D_MODEL), lambda i: (0, 0)), pl.BlockSpec((GROUP_W, D_MODEL), lambda i: (0, 0)),
                  pl.BlockSpec((tm, D_MODEL), lambda i: (i, P_GA // D_MODEL)),
                  pl.BlockSpec((tm, D_MODEL), lambda i: (i, P_GB // D_MODEL))],
        out_specs=[pl.BlockSpec((tm, 2 * D_MODEL), lambda i: (i, 0)), pl.BlockSpec((tm, D_MODEL), lambda i: (i, 0))],
        out_shape=[jax.ShapeDtypeStruct((S, 2 * D_MODEL), BF16), jax.ShapeDtypeStruct((S, D_MODEL), BF16)],
        name="branch_mix", compiler_params=_cparams(("parallel",), 40))(ya_in, comb_b, w_a, w_b, proj, proj)


def _mix_ln1(mixin, w_o, b_o, h0, g1, b1, *, tm=512):
    S = mixin.shape[0]

    def body(mx_ref, wo_ref, bo_ref, h0_ref, g_ref, b_ref, xh_ref, rs_ref, h1b_ref):
        z = ALPHA * h0_ref[...] + _dot(mx_ref[...], wo_ref[...]) + bo_ref[...]
        xhat, rstd = _ln_stats(z)
        xh_ref[...] = xhat
        rs_ref[...] = jnp.broadcast_to(rstd, (tm, LANES))
        h1b_ref[...] = (xhat * g_ref[...] + b_ref[...]).astype(BF16)

    row = pl.BlockSpec((tm, D_MODEL), lambda i: (i, 0))
    vec = pl.BlockSpec((1, D_MODEL), lambda i: (0, 0))
    return pl.pallas_call(
        body, grid=(S // tm,),
        in_specs=[row, pl.BlockSpec((D_MODEL, D_MODEL), lambda i: (0, 0)), vec, row, vec, vec],
        out_specs=[row, pl.BlockSpec((tm, LANES), lambda i: (i, 0)), row],
        out_shape=[jax.ShapeDtypeStruct((S, D_MODEL), F32), jax.ShapeDtypeStruct((S, LANES), F32),
                   jax.ShapeDtypeStruct((S, D_MODEL), BF16)],
        name="mix_ln1", compiler_params=_cparams(("parallel",), 40))(mixin, w_o, b_o, h0, g1, b1)


def _gelu_parts(cz):
    cdf = 0.5 * (1.0 + lax.erf(cz * INV_SQRT2))
    return cdf, cz * cdf


def _ffn_conv_fwd(up, cw, cb):
    S = up.shape[0]

    def body(a_ref, g_ref, w_ref, cb_ref, o_ref, a_scr):
        _zero_pads(a_scr, S)
        for t in range(0, S, CHUNK):
            a_scr[PAD + t:PAD + t + CHUNK, :] = a_ref[t:t + CHUNK, :].astype(F32)
        w0, w1, w2 = w_ref[0:1, :], w_ref[1:2, :], w_ref[2:3, :]
        for t in range(0, S, CHUNK):
            am, a0, ap = _shifted(a_scr, t)
            _, gel = _gelu_parts(w0 * am + w1 * a0 + w2 * ap + cb_ref[...])
            o_ref[t:t + CHUNK, :] = (gel * g_ref[t:t + CHUNK, :].astype(F32)).astype(BF16)

    return pl.pallas_call(
        body, grid=(D_FF // SLAB,),
        in_specs=[_slab_spec(S, 0), _slab_spec(S, D_FF), pl.BlockSpec((3, SLAB), lambda j: (0, j)),
                  pl.BlockSpec((1, SLAB), lambda j: (0, j))],
        out_specs=pl.BlockSpec((S, SLAB), lambda j: (0, j)),
        out_shape=jax.ShapeDtypeStruct((S, D_FF), BF16),
        scratch_shapes=[pltpu.VMEM((S + 2 * PAD, SLAB), F32)],
        name="ffn_conv_fwd", compiler_params=_cparams(("parallel",), 40))(up, up, cw, cb)


def _down_ln2_loss(f, w_down, b_down, xhat1, g1, b1, g2, b2, target, *, tm=512):
    S = f.shape[0]

    def body(f_ref, wd_ref, bd_ref, xh1_ref, g1_ref, b1_ref, g2_ref, b2_ref, t_ref, dz_ref, dzb_ref, st_ref):
        h1 = xh1_ref[...] * g1_ref[...] + b1_ref[...]
        z = ALPHA * h1 + _dot(f_ref[...], wd_ref[...]) + bd_ref[...]
        xhat, rstd = _ln_stats(z)
        err = xhat * g2_ref[...] + b2_ref[...] - t_ref[...]
        loss = (0.5 / D_MODEL) * jnp.sum(jnp.sum(err * err, axis=1, keepdims=True), axis=0, keepdims=True)
        dh2 = err * (1.0 / D_MODEL)
        dz = _ln_bwd(dh2, xhat, rstd, g2_ref[...])
        dz_ref[...] = dz
        dzb_ref[...] = dz.astype(BF16)
        upd = _rows8([jnp.sum(dh2 * xhat, axis=0, keepdims=True), jnp.sum(dh2, axis=0, keepdims=True),
                      jnp.broadcast_to(loss, (1, D_MODEL)), jnp.sum(dz, axis=0, keepdims=True)], D_MODEL)

        @pl.when(pl.program_id(0) == 0)
        def _():
            st_ref[...] = upd

        @pl.when(pl.program_id(0) != 0)
        def _():
            st_ref[...] += upd

    row = pl.BlockSpec((tm, D_MODEL), lambda i: (i, 0))
    vec = pl.BlockSpec((1, D_MODEL), lambda i: (0, 0))
    return pl.pallas_call(
        body, grid=(S // tm,),
        in_specs=[pl.BlockSpec((tm, D_FF), lambda i: (i, 0)), _resident((D_FF, D_MODEL)),
                  vec, row, vec, vec, vec, vec, row],
        out_specs=[row, row, pl.BlockSpec((SUBLANES, D_MODEL), lambda i: (0, 0))],
        out_shape=[jax.ShapeDtypeStruct((S, D_MODEL), F32), jax.ShapeDtypeStruct((S, D_MODEL), BF16),
                   jax.ShapeDtypeStruct((SUBLANES, D_MODEL), F32)],
        name="down_ln2_loss", compiler_params=_cparams(("arbitrary",), 56))(
            f, w_down, b_down, xhat1, g1, b1, g2, b2, target)


def _ffn_conv_bwd(up, dz2b, w_down, cw, cb):
    S = up.shape[0]

    def body(a_ref, g_ref, dz_ref, wd_ref, w_ref, cb_ref, dup_ref, sm_ref, a_scr, d_scr):
        _zero_pads(a_scr, S)
        _zero_pads(d_scr, S)
        for t in range(0, S, CHUNK):
            a_scr[PAD + t:PAD + t + CHUNK, :] = a_ref[t:t + CHUNK, :].astype(F32)
        w0, w1, w2 = w_ref[0:1, :], w_ref[1:2, :], w_ref[2:3, :]
        w_slab = wd_ref[...]
        zero = jnp.zeros((1, SLAB), F32)
        s_dg, s_dcz, s_w0, s_w1, s_w2 = zero, zero, zero, zero, zero
        for t in range(0, S, CHUNK):
            am, a0, ap = _shifted(a_scr, t)
            cz = w0 * am + w1 * a0 + w2 * ap + cb_ref[...]
            cdf, gel = _gelu_parts(cz)
            dfv = _dot_nt(dz_ref[t:t + CHUNK, :], w_slab)
            dgte = dfv * gel
            dcz = dfv * g_ref[t:t + CHUNK, :].astype(F32) * (cdf + cz * jnp.exp(-0.5 * cz * cz) * INV_SQRT_2PI)
            dup_ref[1, t:t + CHUNK, :] = dgte.astype(BF16)
            d_scr[PAD + t:PAD + t + CHUNK, :] = dcz
            s_dg = s_dg + jnp.sum(dgte, axis=0, keepdims=True)
            s_dcz = s_dcz + jnp.sum(dcz, axis=0, keepdims=True)
            s_w0 = s_w0 + jnp.sum(dcz * am, axis=0, keepdims=True)
            s_w1 = s_w1 + jnp.sum(dcz * a0, axis=0, keepdims=True)
            s_w2 = s_w2 + jnp.sum(dcz * ap, axis=0, keepdims=True)
        s_da = zero
        for t in range(0, S, CHUNK):
            dm, d0, dp = _shifted(d_scr, t)
            da = w0 * dp + w1 * d0 + w2 * dm
            dup_ref[0, t:t + CHUNK, :] = da.astype(BF16)
            s_da = s_da + jnp.sum(da, axis=0, keepdims=True)
        sm_ref[...] = _rows8([s_da, s_dg, s_dcz, s_w0, s_w1, s_w2], SLAB)

    return pl.pallas_call(
        body, grid=(D_FF // SLAB,),
        in_specs=[_slab_spec(S, 0), _slab_spec(S, D_FF), _resident(dz2b.shape),
                  pl.BlockSpec((SLAB, D_MODEL), lambda j: (j, 0)),
                  pl.BlockSpec((3, SLAB), lambda j: (0, j)), pl.BlockSpec((1, SLAB), lambda j: (0, j))],
        out_specs=[pl.BlockSpec((2, S, SLAB), lambda j: (0, 0, j)), pl.BlockSpec((SUBLANES, SLAB), lambda j: (0, j))],
        out_shape=[jax.ShapeDtypeStruct((2, S, D_FF), BF16), jax.ShapeDtypeStruct((SUBLANES, D_FF), F32)],
        scratch_shapes=[pltpu.VMEM((S + 2 * PAD, SLAB), F32)] * 2,
        name="ffn_conv_bwd", compiler_params=_cparams(("parallel",), 56))(up, up, dz2b, w_down, cw, cb)


def _up_bwd_ln1(dup, w_up3, dz2, xhat1, rstd1, g1, *, tm=512):
    S = dz2.shape[0]
    ns, _, tk = w_up3.shape
    per_plane = D_FF // tk

    def body(du_ref, w_ref, dz2_ref, xh_ref, rs_ref, g_ref, dz_ref, dzb_ref, st_ref):
        dh = ALPHA * dz2_ref[...]
        for k in range(ns):
            col = (k % per_plane) * tk
            dh = dh + _dot_nt(du_ref[k // per_plane, :, col:col + tk], w_ref[k])
        xhat = xh_ref[...]
        dz = _ln_bwd(dh, xhat, rs_ref[:, 0:1], g_ref[...])
        dz_ref[...] = dz
        dzb_ref[...] = dz.astype(BF16)
        upd = _rows8([jnp.sum(dh * xhat, axis=0, keepdims=True), jnp.sum(dh, axis=0, keepdims=True),
                      jnp.sum(dz, axis=0, keepdims=True)], D_MODEL)

        @pl.when(pl.program_id(0) == 0)
        def _():
            st_ref[...] = upd

        @pl.when(pl.program_id(0) != 0)
        def _():
            st_ref[...] += upd

    row = pl.BlockSpec((tm, D_MODEL), lambda i: (i, 0))
    return pl.pallas_call(
        body, grid=(S // tm,),
        in_specs=[pl.BlockSpec((dup.shape[0], tm, D_FF), lambda i: (0, i, 0)), _resident(w_up3.shape),
                  row, row, pl.BlockSpec((tm, LANES), lambda i: (i, 0)), pl.BlockSpec((1, D_MODEL), lambda i: (0, 0))],
        out_specs=[row, row, pl.BlockSpec((SUBLANES, D_MODEL), lambda i: (0, 0))],
        out_shape=[jax.ShapeDtypeStruct((S, D_MODEL), F32), jax.ShapeDtypeStruct((S, D_MODEL), BF16),
                   jax.ShapeDtypeStruct((SUBLANES, D_MODEL), F32)],
        name="up_bwd_ln1", compiler_params=_cparams(("arbitrary",), 56))(dup, w_up3, dz2, xhat1, rstd1, g1)


def _mix_bwd(dz1b, w_o, proj, yab, *, tm=512):
    S = dz1b.shape[0]

    def body(dz_ref, wo_ref, ga_ref, gb_ref, y_ref, dy_ref, dg_ref):
        dmx = _dot_nt(dz_ref[...], wo_ref[...])
        for k, gt_ref in enumerate((ga_ref, gb_ref)):
            sl = slice(k * D_MODEL, (k + 1) * D_MODEL)
            sg = jax.nn.sigmoid(gt_ref[...].astype(F32))
            dy_ref[:, sl] = (dmx * sg).astype(BF16)
            dg_ref[k] = (dmx * y_ref[:, sl].astype(F32) * sg * (1.0 - sg)).astype(BF16)

    row = pl.BlockSpec((tm, D_MODEL), lambda i: (i, 0))
    wide = pl.BlockSpec((tm, 2 * D_MODEL), lambda i: (i, 0))
    return pl.pallas_call(
        body, grid=(S // tm,),
        in_specs=[row, _resident(w_o.shape), pl.BlockSpec((tm, D_MODEL), lambda i: (i, P_GA // D_MODEL)),
                  pl.BlockSpec((tm, D_MODEL), lambda i: (i, P_GB // D_MODEL)), wide],
        out_specs=[wide, pl.BlockSpec((2, tm, D_MODEL), lambda i: (0, i, 0))],
        out_shape=[jax.ShapeDtypeStruct((S, 2 * D_MODEL), BF16), jax.ShapeDtypeStruct((2, S, D_MODEL), BF16)],
        name="mix_bwd", compiler_params=_cparams(("parallel",), 40))(dz1b, w_o, proj, proj, yab)


def _conv_gate_bwd(proj, dyab, w_a, conv_w):
    S = proj.shape[0]

    def body(b_ref, c_ref, h_ref, dy_ref, wa_ref, w_ref, o_ref, sm_ref, u_scr, d_scr):
        _zero_pads(u_scr, S)
        _zero_pads(d_scr, S)
        for t in range(0, S, CHUNK):
            u_scr[PAD + t:PAD + t + CHUNK, :] = c_ref[t:t + CHUNK, :].astype(F32) * h_ref[t:t + CHUNK, :].astype(F32)
        w0, w1, w2 = w_ref[0:1, :], w_ref[1:2, :], w_ref[2:3, :]
        w_slab = wa_ref[...]
        zero = jnp.zeros((1, SLAB), F32)
        s_w0, s_w1, s_w2 = zero, zero, zero
        for t in range(0, S, CHUNK):
            um, u0, up = _shifted(u_scr, t)
            dy = _dot_nt(dy_ref[t:t + CHUNK, :], w_slab)
            o_ref[0, t:t + CHUNK, :] = (dy * (w0 * um + w1 * u0 + w2 * up)).astype(BF16)
            dcv = dy * b_ref[t:t + CHUNK, :].astype(F32)
            d_scr[PAD + t:PAD + t + CHUNK, :] = dcv
            s_w0 = s_w0 + jnp.sum(dcv * um, axis=0, keepdims=True)
            s_w1 = s_w1 + jnp.sum(dcv * u0, axis=0, keepdims=True)
            s_w2 = s_w2 + jnp.sum(dcv * up, axis=0, keepdims=True)
        for t in range(0, S, CHUNK):
            dm, d0, dp = _shifted(d_scr, t)
            du = w0 * dp + w1 * d0 + w2 * dm
            o_ref[1, t:t + CHUNK, :] = (du * h_ref[t:t + CHUNK, :].astype(F32)).astype(BF16)
            o_ref[2, t:t + CHUNK, :] = (du * c_ref[t:t + CHUNK, :].astype(F32)).astype(BF16)
        sm_ref[...] = _rows8([s_w0, s_w1, s_w2], SLAB)

    return pl.pallas_call(
        body, grid=(D_CONV // SLAB,),
        in_specs=[_slab_spec(S, P_B), _slab_spec(S, P_C), _slab_spec(S, P_H),
                  pl.BlockSpec((S, D_MODEL), lambda j: (0, 0), pipeline_mode=pl.Buffered(1)),
                  pl.BlockSpec((SLAB, D_MODEL), lambda j: (j, 0)), pl.BlockSpec((3, SLAB), lambda j: (0, j))],
        out_specs=[pl.BlockSpec((3, S, SLAB), lambda j: (0, 0, j)), pl.BlockSpec((SUBLANES, SLAB), lambda j: (0, j))],
        out_shape=[jax.ShapeDtypeStruct((3, S, D_CONV), BF16), jax.ShapeDtypeStruct((SUBLANES, D_CONV), F32)],
        scratch_shapes=[pltpu.VMEM((S + 2 * PAD, SLAB), F32)] * 2,
        name="conv_gate_bwd", compiler_params=_cparams(("parallel",), 58))(proj, proj, proj, dyab, w_a, conv_w)


def _comb_bwd(dyab, w_b, comb, lse_tot, *, tm=512):
    S = comb.shape[0]
    widths, dtypes = (GROUP_W, LANES, LANES), (BF16, F32, F32)

    def body(dy_ref, wb_ref, c_ref, lt_ref, e_ref, *rest):
        outs, scr = rest[:3 * N_GROUPS], rest[3 * N_GROUPS:]
        dcb = _dot_nt(dy_ref[...], wb_ref[...]).astype(BF16)
        dc = dcb.astype(F32)
        delta = lax.dot_general(dc * c_ref[...], e_ref[...], (((1,), (1,)), ((), ())),
                                preferred_element_type=F32, precision=lax.Precision.HIGHEST)
        for k, (val, dtype) in enumerate(zip((dc, lt_ref[...], delta), dtypes)):
            outs[k][0] = val.astype(dtype)
            _to_residue(val, [outs[3 * (1 + j) + k] for j in range(len(DILS))], DILS, tm, dtype,
                        scr[:val.shape[1] // LANES])

    out_specs, out_shape = [], []
    for _, d in GROUPS:
        out_specs += [_res_spec(d, tm, w) for w in widths]
        out_shape += [jax.ShapeDtypeStruct((d, S // d, w), t) for w, t in zip(widths, dtypes)]
    res = pl.pallas_call(
        body, grid=(S // tm,),
        in_specs=[pl.BlockSpec((tm, D_MODEL), lambda i: (i, 1)), _resident(w_b.shape),
                  pl.BlockSpec((tm, GROUP_W), lambda i: (i, 0)), pl.BlockSpec((tm, LANES), lambda i: (i, 0)),
                  _resident((LANES, GROUP_W))],
        out_specs=out_specs, out_shape=out_shape, scratch_shapes=_lane_scratch(tm, GROUP_W),
        name="comb_bwd", compiler_params=_cparams(("parallel",), 32))(dyab, w_b, comb, lse_tot, _expand_heads())
    return [tuple(res[3 * g:3 * g + 3]) for g in range(N_GROUPS)]


def _attn_bwd(qkv, col0, g, dcomb, lse_tot, delta):
    dil, sub, _ = qkv.shape
    nb = sub // TQ

    def body(q_ref, kp, kc, kn, vp, vc, vn, do_ref, lse_ref, dl_ref, bias_ref, dq_ref, dk_ref, dv_ref,
             ak, av, dqt_scr, s_scr, dp_scr, ds_scr, p_scr):
        i = pl.program_id(1)

        @pl.when(i == 0)
        def _():
            ak[...] = jnp.zeros_like(ak)
            av[...] = jnp.zeros_like(av)

        @pl.when(i < nb)
        def _():
            kwin = _window(kp, kc, kn)
            vwin = _window(vp, vc, vn)
            q = q_ref[...] * ATT_SCALE
            do = do_ref[...]
            lse_t, dl_t = lse_ref[...].T, dl_ref[...].T
            for h in range(HEADS_PER_GROUP):
                s_scr[h] = _dot_nt(_pair(kwin, h), _own_lanes(_pair(q, h), h))
                dp_scr[h] = _dot_nt(_pair(vwin, h), _own_lanes(_pair(do, h), h))
            for h in range(HEADS_PER_GROUP):
                p = jnp.exp(s_scr[h] + _slope(g, h) * bias_ref[...] - lse_t[h:h + 1, :])
                ds_scr[h] = (p * (dp_scr[h] - dl_t[h:h + 1, :])).astype(BF16)
                p_scr[h] = p.astype(BF16)
            for h in range(HEADS_PER_GROUP):
                dqt_scr[h * HEAD_DIM:(h + 1) * HEAD_DIM, :] = _own_rows(_dot_tn(_pair(kwin, h), ds_scr[h]), h)
            for h in range(0, HEADS_PER_GROUP, 2):
                cols = slice(h * HEAD_DIM, (h + 2) * HEAD_DIM)
                q2 = jnp.concatenate([_own_lanes(_pair(q, h), h), _own_lanes(_pair(q, h), h + 1)], axis=0)
                do2 = jnp.concatenate([_own_lanes(_pair(do, h), h), _own_lanes(_pair(do, h), h + 1)], axis=0)
                ak[RADIUS:RADIUS + 2 * TQ, cols] += _dot(jnp.concatenate([ds_scr[h], ds_scr[h + 1]], axis=1), q2)
                av[RADIUS:RADIUS + 2 * TQ, cols] += _dot(jnp.concatenate([p_scr[h], p_scr[h + 1]], axis=1), do2)
            dq_ref[...] = (dqt_scr[...].T * ATT_SCALE).astype(BF16)

        dk_ref[...] = ak[0:TQ, :].astype(BF16)
        dv_ref[...] = av[0:TQ, :].astype(BF16)
        ak[0:2 * TQ, :] = ak[TQ:3 * TQ, :]
        av[0:2 * TQ, :] = av[TQ:3 * TQ, :]
        ak[2 * TQ:3 * TQ, :] = jnp.zeros((TQ, GROUP_W), F32)
        av[2 * TQ:3 * TQ, :] = jnp.zeros((TQ, GROUP_W), F32)

    tok = pl.BlockSpec((None, TQ, GROUP_W), lambda r, i: (r, jnp.minimum(i, nb - 1), 0))
    stat = pl.BlockSpec((None, TQ, LANES), lambda r, i: (r, jnp.minimum(i, nb - 1), 0))
    dkv_spec = pl.BlockSpec((None, TQ, GROUP_W), lambda r, i: (r, jnp.maximum(i - 1, 0), 0))
    return pl.pallas_call(
        body, grid=(dil, nb + 1), in_specs=_qkv_specs(nb, col0) + [tok, stat, stat, _bias_spec(nb)],
        out_specs=[tok, dkv_spec, dkv_spec], out_shape=[jax.ShapeDtypeStruct((dil, sub, GROUP_W), BF16)] * 3,
        scratch_shapes=[pltpu.VMEM((3 * TQ, GROUP_W), F32)] * 2 + [pltpu.VMEM((GROUP_W, TQ), F32)]
        + [pltpu.VMEM((HEADS_PER_GROUP, 2 * TQ, TQ), F32)] * 2 + [pltpu.VMEM((HEADS_PER_GROUP, 2 * TQ, TQ), BF16)] * 2,
        name=f"attn_bwd_g{g}", compiler_params=_cparams(("arbitrary", "arbitrary"), 32))(
            *([qkv] * 7), dcomb, lse_tot, delta, _attn_bias_table(g))


def _in_bwd_ln0(dgated, dqkv, w_nat, w_dil, dz1, x, g0, *, tm=256):
    S = x.shape[0]
    n_gated, n_in = len(dgated), 3 * N_GROUPS

    def body(*refs):
        g_refs, d_refs = refs[:n_gated], refs[n_gated:n_gated + n_in]
        wn_ref, *wd_refs = refs[n_gated + n_in:n_gated + n_in + N_GROUPS]
        dz_ref, x_ref, g_ref, gx_ref, st_ref, *tmp_ref = refs[n_gated + n_in + N_GROUPS:]
        dh = ALPHA * dz_ref[...]
        col = 0
        for ref in g_refs:
            for k in range(ref.shape[0]):
                dh = dh + _dot_nt(ref[k], wn_ref[:, col:col + D_MODEL])
                col += D_MODEL
        for g, (_, d) in enumerate(GROUPS):
            rows = [jnp.concatenate([d_refs[3 * g + k][r] for k in range(3)], axis=1) for r in range(d)]
            w = wn_ref[:, col:col + QKV_W] if d == 1 else wd_refs[g - 1][...]
            res = _dot_nt(jnp.concatenate(rows, axis=0), w)
            if d == 1:
                dh = dh + res
            else:
                n = tm // d
                dh = dh + _from_residue(lambda r: res[r * n:(r + 1) * n, :], d, tm, tmp_ref)
        xhat, rstd = _ln_stats(x_ref[...])
        gx_ref[...] = _ln_bwd(dh, xhat, rstd, g_ref[...])
        upd = _rows8([jnp.sum(dh * xhat, axis=0, keepdims=True), jnp.sum(dh, axis=0, keepdims=True)], D_MODEL)

        @pl.when(pl.program_id(0) == 0)
        def _():
            st_ref[...] = upd

        @pl.when(pl.program_id(0) != 0)
        def _():
            st_ref[...] += upd

    row = pl.BlockSpec((tm, D_MODEL), lambda i: (i, 0))
    g_specs = [pl.BlockSpec((a.shape[0], tm, D_MODEL), lambda i: (0, i, 0)) for a in dgated]
    d_specs = []
    for _, d in GROUPS:
        d_specs += [_res_spec(d, tm, GROUP_W)] * 3
    operands = list(dgated) + [a for grp in dqkv for a in grp] + [w_nat] + list(w_dil) + [dz1, x, g0]
    return pl.pallas_call(
        body, grid=(S // tm,),
        in_specs=g_specs + d_specs + [_resident(w_nat.shape)] + [_resident(w.shape) for w in w_dil]
        + [row, row, pl.BlockSpec((1, D_MODEL), lambda i: (0, 0))],
        out_specs=[row, pl.BlockSpec((SUBLANES, D_MODEL), lambda i: (0, 0))],
        out_shape=[jax.ShapeDtypeStruct((S, D_MODEL), F32), jax.ShapeDtypeStruct((SUBLANES, D_MODEL), F32)],
        scratch_shapes=_lane_scratch(tm, D_MODEL),
        name="in_bwd_ln0", compiler_params=_cparams(("arbitrary",), 52))(*operands)


HBM_SPEC = pl.BlockSpec(memory_space=pltpu.HBM)


def _place():
    x, y, c = lax.axis_index("x"), lax.axis_index("y"), lax.axis_index("c")
    chips = [(1 - x, y), (x, 1 - y), (1 - x, 1 - y)]
    return x, y, c, chips


def _allgather_shards(shards, after, *, name, collective_id):
    n = len(shards)
    per = 6

    def body(*refs):
        ins, outs = refs[:n], refs[n + len(after):2 * n + len(after)]
        send_sems, recv_sems, loc_sems = refs[2 * n + len(after):]
        x, y, c, chips = _place()
        me = 2 * x + y
        sib = (x, y, 1 - c)
        peers = [sib] + [(px, py, c) for px, py in chips]
        barrier = pltpu.get_barrier_semaphore()
        for peer in peers:
            pl.semaphore_signal(barrier, inc=1, device_id=peer, device_id_type=MESH)
        pl.semaphore_wait(barrier, len(peers))

        def rcopy(w, k, src, dst, to):
            return pltpu.make_async_remote_copy(src_ref=src, dst_ref=dst, send_sem=send_sems.at[per * w + k],
                                                recv_sem=recv_sems.at[per * w + k], device_id=to, device_id_type=MESH)

        split = [s.shape[0] == N_CORES for s in shards]
        half = lambda w: c if split[w] else 0
        local, sends = [], []
        for w in range(n):
            cp = pltpu.make_async_copy(ins[w], outs[w].at[me], loc_sems.at[w])
            cp.start()
            local.append(cp)
            for j, (px, py) in enumerate(chips):
                cp = rcopy(w, j, ins[w].at[half(w)], outs[w].at[me, half(w)], (px, py, c))
                cp.start()
                sends.append(cp)
        for w in range(n):
            for j, (px, py) in enumerate(chips):
                slot = outs[w].at[2 * px + py, half(w)]
                rcopy(w, j, slot, slot, (px, py, c)).wait_recv()
                if split[w]:
                    cp = rcopy(w, 3 + j, slot, slot, sib)
                    cp.start()
                    sends.append(cp)
        for w in range(n):
            if split[w]:
                for j, (px, py) in enumerate(chips):
                    slot = outs[w].at[2 * px + py, 1 - c]
                    rcopy(w, 3 + j, slot, slot, sib).wait_recv()
        for cp in sends:
            cp.wait_send()
        for cp in local:
            cp.wait()

    return pl.kernel(
        body, out_type=[jax.ShapeDtypeStruct((N_CHIPS,) + s.shape, s.dtype) for s in shards],
        mesh=plsc.ScalarSubcoreMesh(axis_name="sequencer", num_cores=1),
        scratch_types=[pltpu.SemaphoreType.DMA((per * n,)), pltpu.SemaphoreType.DMA((per * n,)),
                       pltpu.SemaphoreType.DMA((n,))],
        name=name, compiler_params=pltpu.CompilerParams(collective_id=collective_id))(*shards, *after)


def _exchange_grads(grads, *, name, collective_id):
    n = len(grads)
    per = 7

    def body(*refs):
        ins, outs = refs[:n], refs[n:2 * n]
        send_sems, recv_sems, loc_sems = refs[2 * n:]
        x, y, c, chips = _place()
        me = 2 * x + y
        sib = (x, y, 1 - c)
        peers = [sib] + [(px, py, c) for px, py in chips]
        barrier = pltpu.get_barrier_semaphore()
        for peer in peers:
            pl.semaphore_signal(barrier, inc=1, device_id=peer, device_id_type=MESH)
        pl.semaphore_wait(barrier, len(peers))

        def rcopy(w, k, src, dst, to):
            return pltpu.make_async_remote_copy(src_ref=src, dst_ref=dst, send_sem=send_sems.at[per * w + k],
                                                recv_sem=recv_sems.at[per * w + k], device_id=to, device_id_type=MESH)

        local, sends = [], []
        for w in range(n):
            cp = pltpu.make_async_copy(ins[w].at[me], outs[w].at[c, me], loc_sems.at[w])
            cp.start()
            local.append(cp)
            cp = rcopy(w, 0, ins[w].at[me], outs[w].at[c, me], sib)
            cp.start()
            sends.append(cp)
            for j, (px, py) in enumerate(chips):
                cp = rcopy(w, 1 + j, ins[w].at[2 * px + py], outs[w].at[c, me], (px, py, c))
                cp.start()
                sends.append(cp)
        for w in range(n):
            for j, (px, py) in enumerate(chips):
                slot = outs[w].at[c, 2 * px + py]
                rcopy(w, 1 + j, slot, slot, (px, py, c)).wait_recv()
                cp = rcopy(w, 4 + j, slot, slot, sib)
                cp.start()
                sends.append(cp)
        for w in range(n):
            slot = outs[w].at[1 - c, me]
            rcopy(w, 0, slot, slot, sib).wait_recv()
            for j, (px, py) in enumerate(chips):
                slot = outs[w].at[1 - c, 2 * px + py]
                rcopy(w, 4 + j, slot, slot, sib).wait_recv()
        for cp in sends:
            cp.wait_send()
        for cp in local:
            cp.wait()

    return pl.kernel(
        body, out_type=[jax.ShapeDtypeStruct((N_CORES,) + g.shape, g.dtype) for g in grads],
        mesh=plsc.ScalarSubcoreMesh(axis_name="sequencer", num_cores=1),
        scratch_types=[pltpu.SemaphoreType.DMA((per * n,)), pltpu.SemaphoreType.DMA((per * n,)),
                       pltpu.SemaphoreType.DMA((n,))],
        name=name, compiler_params=pltpu.CompilerParams(collective_id=collective_id))(*grads)


def _allgather_small(vec, after):
    def body(v_ref, _, o_ref, send_sems, recv_sems, loc_sem):
        x, y, c = lax.axis_index("x"), lax.axis_index("y"), lax.axis_index("c")
        me = 4 * x + 2 * y + c

        def peer(k):
            flip = lambda v, bit: 1 - v if (k >> bit) & 1 else v
            return flip(x, 2), flip(y, 1), flip(c, 0)

        loc = pltpu.make_async_copy(v_ref, o_ref.at[me], loc_sem)
        loc.start()
        sends = []
        for k in range(1, N_DEV):
            cp = pltpu.make_async_remote_copy(src_ref=v_ref, dst_ref=o_ref.at[me], send_sem=send_sems.at[k - 1],
                                              recv_sem=recv_sems.at[k - 1], device_id=peer(k), device_id_type=MESH)
            cp.start()
            sends.append(cp)
        for k in range(1, N_DEV):
            px, py, pc = peer(k)
            pltpu.make_async_remote_copy(src_ref=v_ref, dst_ref=o_ref.at[4 * px + 2 * py + pc],
                                         send_sem=send_sems.at[k - 1], recv_sem=recv_sems.at[k - 1],
                                         device_id=(px, py, pc), device_id_type=MESH).wait_recv()
        for cp in sends:
            cp.wait_send()
        loc.wait()

    return pl.pallas_call(
        body, in_specs=[HBM_SPEC, HBM_SPEC], out_specs=HBM_SPEC,
        out_shape=jax.ShapeDtypeStruct((N_DEV,) + vec.shape, vec.dtype),
        scratch_shapes=[pltpu.SemaphoreType.DMA((N_DEV - 1,)), pltpu.SemaphoreType.DMA((N_DEV - 1,)),
                        pltpu.SemaphoreType.DMA],
        name="allgather_small")(vec, after)


def _adamw(w, g, m, v):
    m = ADAM_B1 * m + (1.0 - ADAM_B1) * g
    v = ADAM_B2 * v + (1.0 - ADAM_B2) * (g * g)
    m_hat = m / (1.0 - ADAM_B1 ** ADAM_STEP)
    v_hat = v / (1.0 - ADAM_B2 ** ADAM_STEP)
    delta = -ADAM_LR * (m_hat / (jnp.sqrt(v_hat) + ADAM_EPS) + ADAM_WD * w)
    return delta, m, v


def _reduce_adamw(parts, w, m, v, *, tr, name):
    R, C = w.shape

    def body(p_ref, w_ref, m_ref, v_ref, g_ref, d_ref, nm_ref, nv_ref):
        def core_sum(cc):
            s = p_ref[cc, 0].astype(F32)
            for k in range(1, N_CHIPS):
                s = s + p_ref[cc, k].astype(F32)
            return s

        g = core_sum(0) + core_sum(1)
        delta, nm, nv = _adamw(w_ref[...], g, m_ref[...], v_ref[...])
        g_ref[...] = g
        d_ref[...] = delta
        nm_ref[...] = nm
        nv_ref[...] = nv

    blk = pl.BlockSpec((tr, C), lambda i: (i, 0))
    return pl.pallas_call(
        body, grid=(R // tr,),
        in_specs=[pl.BlockSpec((N_CORES, N_CHIPS, tr, C), lambda i: (0, 0, i, 0)), blk, blk, blk],
        out_specs=[blk] * 4, out_shape=[jax.ShapeDtypeStruct((R, C), F32)] * 4,
        name=name, compiler_params=_cparams(("parallel",), 40))(parts, w, m, v)


def _reduce_adamw_vectors(allv, offs, ws, ms, vs):
    n = len(ws)

    def body(a_ref, *refs):
        w_refs, m_refs, v_refs = refs[:n], refs[n:2 * n], refs[2 * n:3 * n]
        tot_ref, outs = refs[3 * n], refs[3 * n + 1:]
        s = a_ref[0]
        for d in range(1, N_DEV):
            s = s + a_ref[d]
        tot_ref[...] = s
        for k in range(n):
            g = s[:, offs[k]:offs[k] + w_refs[k].shape[1]]
            delta, nm, nv = _adamw(w_refs[k][...], g, m_refs[k][...], v_refs[k][...])
            for ref, val in zip(outs[4 * k:4 * k + 4], (g, delta, nm, nv)):
                ref[...] = val

    out_shape = [jax.ShapeDtypeStruct(allv.shape[1:], F32)]
    for w in ws:
        out_shape += [jax.ShapeDtypeStruct(w.shape, F32)] * 4
    res = pl.pallas_call(body, out_shape=out_shape, name="reduce_adamw_vectors",
                         compiler_params=_cparams((), 40))(allv, *ws, *ms, *vs)
    return res[0], [tuple(res[1 + 4 * k:5 + 4 * k]) for k in range(n)]


def _adamw_taps(ws, gs, ms, vs):
    n = len(ws)

    def body(*refs):
        outs = refs[4 * n:]
        for k in range(n):
            res = _adamw(refs[k][...], refs[n + k][...], refs[2 * n + k][...], refs[3 * n + k][...])
            for ref, val in zip(outs[3 * k:3 * k + 3], res):
                ref[...] = val

    out_shape = []
    for w in ws:
        out_shape += [jax.ShapeDtypeStruct(w.shape, F32)] * 3
    res = pl.pallas_call(body, out_shape=out_shape, name="adamw_taps")(*ws, *gs, *ms, *vs)
    return [tuple(res[3 * k:3 * k + 3]) for k in range(n)]


def _pack(pieces):
    flat, offs, n = [], [], 0
    for p in pieces:
        size = -(-p.size // LANES) * LANES
        flat.append(jnp.pad(p.reshape(-1), (0, size - p.size)))
        offs.append(n)
        n += size
    return jnp.concatenate(flat).reshape(1, n), offs


def _local_step(x, target, p, wfull, on_ready=lambda group: None, before_ln0=()):
    S = x.shape[0]
    dils = [d for _, d in GROUPS]

    h0, h0b, *h0_res = _ln0_fwd(x, p["ln0_g"], p["ln0_b"], before_ln0)
    h0_rows = [h0b] + [h.reshape(S, D_MODEL) for h in h0_res]

    if isinstance(wfull, dict):
        w_in3, pending = wfull["w_in"], None
    else:
        w_in3, launch_rest, assemble = wfull
        w_in3, h0b = lax.optimization_barrier((w_in3, h0b))
        pending = launch_rest(h0b)

    w_blocks = w_in3.transpose(1, 0, 2).reshape(D_MODEL, N_BLK, GROUP_W)
    w_perm = jnp.concatenate([w_blocks[:, b] for b in PERM], axis=1)
    b_blocks = p["b_in"].reshape(N_BLK, GROUP_W)
    b_perm = jnp.concatenate([b_blocks[b] for b in PERM]).reshape(1, N_IN)
    w_nat, b_nat = w_perm[:, :N_NAT], b_perm[:, :N_NAT]
    qkv_cols = [slice(P_Q0 + g * QKV_W, P_Q0 + (g + 1) * QKV_W) for g in range(N_GROUPS)]
    w_qkv = [w_perm[:, c] for c in qkv_cols]

    proj = _mm_nn(h0b, w_nat, b_nat, tm=512, tn=N_NAT // 2, out_dtype=BF16, name="proj")
    qkv = [proj[None]]
    for g in range(1, N_GROUPS):
        t = _mm_nn(h0_rows[g], w_qkv[g], b_perm[:, qkv_cols[g]], tm=512, tn=QKV_W, out_dtype=BF16, name=f"proj_qkv{g}")
        qkv.append(t.reshape(dils[g], S // dils[g], QKV_W))
    if pending is not None:
        pending, qkv = lax.optimization_barrier((pending, qkv))
        proj = qkv[0][0]
        wfull = assemble(pending)
    w_up3 = wfull["w_up"]
    w_a, w_o, w_down, w_b = wfull["w_a"], wfull["w_o"], wfull["w_down"], wfull["w_b"]
    conv_w, ffn_conv_w = wfull["conv_w"], wfull["ffn_conv_w"]
    col0 = [P_Q0 // GROUP_W] + [0] * (N_GROUPS - 1)
    ya_in = _conv_gate_fwd(proj, conv_w)
    att = [_attn_fwd(qkv[g], col0[g], g) for g in range(N_GROUPS)]
    comb, comb_b, lse_tot = _attn_combine([a[0] for a in att], [a[1] for a in att])
    yab, mixin = _branch_mix(ya_in, comb_b, w_a, w_b, proj)
    xhat1, rstd1, h1b = _mix_ln1(mixin, w_o, p["b_o"], h0, p["ln1_g"], p["ln1_b"])
    up = _mm_nn(h1b, w_up3, p["b_up"], tm=512, tn=w_up3.shape[2], out_dtype=BF16, name="up")
    f = _ffn_conv_fwd(up, ffn_conv_w, p["ffn_conv_b"])
    dz2, dz2b, st2 = _down_ln2_loss(f, w_down, p["b_down"], xhat1, p["ln1_g"], p["ln1_b"],
                                    p["ln2_g"], p["ln2_b"], target)

    gw = {}
    gw["w_down"] = _mm_tn(f, dz2b, n_out=1, tn=D_MODEL, ts=1024, g_block=(1024, D_MODEL),
                          g_map=lambda j, s: (s, 0), name="grad_w_down").reshape(N_CHIPS, D_FF // N_CHIPS, D_MODEL)
    dup, sm_ffn = _ffn_conv_bwd(up, dz2b, w_down, ffn_conv_w, p["ffn_conv_b"])
    up_tn = w_up3.shape[2]
    up_pp = D_FF // up_tn
    gw["w_up"] = _mm_tn(h1b, dup, n_out=N_CHIPS, tn=up_tn, ts=1024, g_block=(None, 1024, up_tn),
                        g_map=lambda j, s: (j // up_pp, s, j % up_pp), name="grad_w_up")
    on_ready({n: gw[n] for n in ("w_down", "w_up")})
    dz1, dz1b, st1 = _up_bwd_ln1(dup, w_up3, dz2, xhat1, rstd1, p["ln1_g"])

    gw["w_o"] = _mm_tn(mixin, dz1b, n_out=1, tn=D_MODEL, ts=512, g_block=(512, D_MODEL),
                       g_map=lambda j, s: (s, 0), name="grad_w_o").reshape(N_CHIPS, D_MODEL // N_CHIPS, D_MODEL)
    dyab, dgab = _mix_bwd(dz1b, w_o, proj, yab)
    gw["w_a"] = _mm_tn(ya_in, dyab, n_out=1, tn=D_MODEL, ts=512, g_block=(512, D_MODEL),
                       g_map=lambda j, s: (s, 0), name="grad_w_a").reshape(N_CHIPS, D_CONV // N_CHIPS, D_MODEL)
    gw_b = _mm_tn(comb_b, dyab, n_out=1, tn=D_MODEL, ts=1024, g_block=(1024, D_MODEL),
                  g_map=lambda j, s: (s, 1), name="grad_w_b")
    gw["w_b"] = gw_b.reshape(GROUP_W, N_CHIPS, D_MODEL // N_CHIPS).transpose(1, 0, 2)
    on_ready({n: gw[n] for n in ("w_o", "w_a", "w_b")})
    dbch, sm_conv = _conv_gate_bwd(proj, dyab, w_a, conv_w)
    att_stats = _comb_bwd(dyab, w_b, comb, lse_tot)
    dqkv = [_attn_bwd(qkv[g], col0[g], g, *att_stats[g]) for g in range(N_GROUPS)]

    w_pieces, b_pieces = [], []
    for nm, planes in (("bch", dbch), ("gab", dgab)):
        pw, pc = _mm_tn(h0b, planes, n_out=planes.shape[0], tn=D_MODEL, ts=1024, g_block=(None, 1024, D_MODEL),
                        g_map=lambda j, s: (j, s, 0), colsum=True, name="grad_w_in_" + nm)
        w_pieces.append(pw.transpose(1, 0, 2).reshape(D_MODEL, planes.shape[0] * D_MODEL))
        b_pieces.append(pc[0])
    for g in range(N_GROUPS):
        pw, pc = _mm_tn_cat(h0_rows[g], [a.reshape(S, GROUP_W) for a in dqkv[g]], ts=1024, name=f"grad_w_in_qkv{g}")
        w_pieces.append(pw)
        b_pieces.append(pc[0])
    dw_blocks = jnp.concatenate(w_pieces, axis=1).reshape(D_MODEL, N_BLK, GROUP_W)
    dw_ref = jnp.concatenate([dw_blocks[:, b] for b in INV_PERM], axis=1)
    gw["w_in"] = dw_ref.reshape(D_MODEL, N_CHIPS, N_IN // N_CHIPS).transpose(1, 0, 2)
    on_ready({"w_in": gw["w_in"]})
    db_blocks = jnp.concatenate(b_pieces).reshape(N_BLK, GROUP_W)
    grad_b_in = jnp.concatenate([db_blocks[b] for b in INV_PERM])

    grad_x, st0 = _in_bwd_ln0([dbch, dgab], dqkv, w_nat, w_qkv[1:], dz1, x, p["ln0_g"])

    small = {
        "loss": st2[2:3, 0:1],
        "ln0_g": st0[0], "ln0_b": st0[1], "b_in": grad_b_in, "conv_w": sm_conv[0:3],
        "b_o": st1[2], "ln1_g": st1[0], "ln1_b": st1[1],
        "b_up": jnp.concatenate([sm_ffn[0], sm_ffn[1]]), "ffn_conv_w": sm_ffn[3:6], "ffn_conv_b": sm_ffn[2],
        "b_down": st2[3], "ln2_g": st2[0], "ln2_b": st2[1],
    }
    return grad_x, gw, small


BIG = ("w_in", "w_a", "w_b", "w_o", "w_up", "w_down")
CONV = ("conv_w", "ffn_conv_w")
VECS = ("ln0_g", "ln0_b", "b_in", "b_o", "ln1_g", "ln1_b", "b_up", "ffn_conv_b", "b_down", "ln2_g", "ln2_b")
ORDER = ("ln0_g", "ln0_b", "w_in", "b_in", "conv_w", "w_a", "w_b", "w_o", "b_o", "ln1_g", "ln1_b", "w_up", "b_up",
         "ffn_conv_w", "ffn_conv_b", "w_down", "b_down", "ln2_g", "ln2_b")
SMALL_ORDER = ("loss",) + VECS + CONV


def _step(x, target, W, Mo, Vo):
    x2, t2 = x[0], target[0]
    big2 = {n: W[n][0] for n in BIG}
    halves = lambda a: a.astype(BF16).reshape(N_CORES, a.shape[0] // N_CORES, a.shape[1])
    whole = lambda g: g.reshape(N_CHIPS, g.shape[1] * g.shape[2], g.shape[3])
    later = tuple(n for n in BIG if n != "w_in")
    w_in_halves = halves(big2["w_in"])
    first = _allgather_shards([w_in_halves], [], name="allgather_w_in", collective_id=1)

    def launch_rest(h0b):
        return _allgather_shards([halves(big2[n]) for n in later] + [W[n] for n in CONV], [h0b],
                                 name="allgather_rest", collective_id=2)

    def assemble(rest):
        gathered = {n: whole(g) for n, g in zip(later + CONV, rest)}
        return {
            "w_up": gathered["w_up"],
            "w_a": gathered["w_a"].reshape(D_CONV, D_MODEL), "w_o": gathered["w_o"].reshape(D_MODEL, D_MODEL),
            "w_down": gathered["w_down"].reshape(D_FF, D_MODEL),
            "w_b": gathered["w_b"].transpose(1, 0, 2).reshape(GROUP_W, D_MODEL),
            "conv_w": gathered["conv_w"].transpose(1, 0, 2).reshape(3, D_CONV),
            "ffn_conv_w": gathered["ffn_conv_w"].transpose(1, 0, 2).reshape(3, D_FF),
        }

    pvec = {n: W[n].reshape(1, -1) for n in VECS}

    parts = {}
    exchange_ids = iter((3, 4, 5))

    def exchange(group):
        names = tuple(group)
        res = _exchange_grads([group[n] for n in names], name="exchange_" + "_".join(names),
                              collective_id=next(exchange_ids))
        parts.update(zip(names, res))

    grad_x, _, small = _local_step(x2, t2, pvec, (whole(first[0]), launch_rest, assemble), exchange,
                                   before_ln0=[w_in_halves])
    out = {}
    for n in BIG:
        tr = {"w_in": 128, "w_up": 128, "w_b": 128}.get(n, big2[n].shape[0] // 4)
        g, d, nm, nv = _reduce_adamw(parts[n], big2[n], Mo[n][0], Vo[n][0], tr=tr, name="adamw_" + n)
        out[n] = tuple(a[None] for a in (g, d, nm, nv))

    vec, offs = _pack([small[n] for n in SMALL_ORDER])
    off = dict(zip(SMALL_ORDER, offs))
    row = lambda a: a.reshape(1, -1)
    allv = _allgather_small(vec, parts["w_in"])
    tot, vec_out = _reduce_adamw_vectors(allv, [off[n] for n in VECS], [row(W[n]) for n in VECS],
                                         [row(Mo[n]) for n in VECS], [row(Vo[n]) for n in VECS])
    for n, res in zip(VECS, vec_out):
        out[n] = tuple(a.reshape(W[n].shape) for a in res)
    loss = tot[0, off["loss"]]
    chip = 2 * lax.axis_index("x") + lax.axis_index("y")
    taps_g = []
    for n in CONV:
        width = W[n].shape[2]
        full = lax.slice(tot, (0, off[n]), (1, off[n] + 3 * N_CHIPS * width)).reshape(3, N_CHIPS * width)
        taps_g.append(lax.dynamic_slice_in_dim(full, chip * width, width, axis=1))
    taps_out = _adamw_taps([W[n][0] for n in CONV], taps_g, [Mo[n][0] for n in CONV], [Vo[n][0] for n in CONV])
    for n, g, res in zip(CONV, taps_g, taps_out):
        out[n] = tuple(a[None] for a in (g,) + res)

    res = [loss, grad_x[None]]
    for k in range(4):
        res += [out[n][k] for n in ORDER]
    return tuple(res)


def kernel(x, ln0_g, ln0_b, w_in, b_in, conv_w, w_a, w_b, w_o, b_o, ln1_g, ln1_b, w_up, b_up, ffn_conv_w, ffn_conv_b, w_down, b_down, ln2_g, ln2_b, loss_target, m_ln0_g, m_ln0_b, m_w_in, m_b_in, m_conv_w, m_w_a, m_w_b, m_w_o, m_b_o, m_ln1_g, m_ln1_b, m_w_up, m_b_up, m_ffn_conv_w, m_ffn_conv_b, m_w_down, m_b_down, m_ln2_g, m_ln2_b, v_ln0_g, v_ln0_b, v_w_in, v_b_in, v_conv_w, v_w_a, v_w_b, v_w_o, v_b_o, v_ln1_g, v_ln1_b, v_w_up, v_b_up, v_ffn_conv_w, v_ffn_conv_b, v_w_down, v_b_down, v_ln2_g, v_ln2_b):
    W = dict(zip(ORDER, (ln0_g, ln0_b, w_in, b_in, conv_w, w_a, w_b, w_o, b_o, ln1_g, ln1_b, w_up, b_up,
                         ffn_conv_w, ffn_conv_b, w_down, b_down, ln2_g, ln2_b)))
    Mo = dict(zip(ORDER, (m_ln0_g, m_ln0_b, m_w_in, m_b_in, m_conv_w, m_w_a, m_w_b, m_w_o, m_b_o, m_ln1_g, m_ln1_b,
                          m_w_up, m_b_up, m_ffn_conv_w, m_ffn_conv_b, m_w_down, m_b_down, m_ln2_g, m_ln2_b)))
    Vo = dict(zip(ORDER, (v_ln0_g, v_ln0_b, v_w_in, v_b_in, v_conv_w, v_w_a, v_w_b, v_w_o, v_b_o, v_ln1_g, v_ln1_b,
                          v_w_up, v_b_up, v_ffn_conv_w, v_ffn_conv_b, v_w_down, v_b_down, v_ln2_g, v_ln2_b)))
    return _step(x, loss_target, W, Mo, Vo)
```

```python
import functools
import math

import jax
import jax.numpy as jnp
from jax import lax
from jax.experimental import pallas as pl
from jax.experimental.pallas import tpu as pltpu
from jax.experimental.pallas import tpu_sc as plsc

F32 = jnp.float32
BF16 = jnp.bfloat16

D_MODEL = 1024
D_CONV = D_MODEL
HEAD_DIM = 64
HEADS_PER_GROUP = 8
GROUPS = ((128, 1), (512, 4), (2048, 16))
N_GROUPS = len(GROUPS)
GROUP_W = HEADS_PER_GROUP * HEAD_DIM
QKV_W = N_GROUPS * GROUP_W
RADIUS = 64
D_FF = 2816
LN_EPS = 1e-5
ALPHA = 2.0 ** 0.25
MASK_VALUE = -1e30
ATT_SCALE = HEAD_DIM ** -0.5
OFF_B = 0
OFF_C = OFF_B + D_CONV
OFF_H = OFF_C + D_CONV
OFF_Q = OFF_H + D_CONV
OFF_K = OFF_Q + QKV_W
OFF_V = OFF_K + QKV_W
OFF_GA = OFF_V + QKV_W
OFF_GB = OFF_GA + D_MODEL
N_IN = OFF_GB + D_MODEL
ADAM_LR = 0.001
ADAM_B1 = 0.9
ADAM_B2 = 0.999
ADAM_EPS = 1e-08
ADAM_WD = 0.01
ADAM_STEP = 10
INV_SQRT2 = 0.7071067811865476
INV_SQRT_2PI = 0.3989422804014327

LANES = 128
SUBLANES = 8
VMEM_BYTES_V7X = 64 * 1024 * 1024
N_CHIPS = 4
N_CORES = 2
N_DEV = N_CHIPS * N_CORES
MESH = pl.DeviceIdType.MESH

N_BLK = N_IN // GROUP_W
PERM = (0, 1, 2, 3, 4, 5, 15, 16, 17, 18, 6, 9, 12, 7, 10, 13, 8, 11, 14)
INV_PERM = tuple(PERM.index(b) for b in range(N_BLK))
P_B, P_C, P_H, P_GA, P_GB, P_Q0 = 0, 1024, 2048, 3072, 4096, 5120
N_NAT = P_Q0 + QKV_W // N_GROUPS * 3
N_GATED = P_Q0

SLAB = 128
CHUNK = 256
PAD = SUBLANES
TQ = 128


def _cparams(sem, vmem_mb):
    assert vmem_mb * 1024 * 1024 < VMEM_BYTES_V7X
    return pltpu.CompilerParams(dimension_semantics=sem, vmem_limit_bytes=vmem_mb * 1024 * 1024)


def _resident(shape):
    nd = len(shape)
    return pl.BlockSpec(shape, lambda *_: (0,) * nd, pipeline_mode=pl.Buffered(1))


def _dot(a, b):
    return jnp.dot(a, b, preferred_element_type=F32)


def _dot_nt(a, b):
    return lax.dot_general(a, b, (((1,), (1,)), ((), ())), preferred_element_type=F32)


def _dot_tn(a, b):
    return lax.dot_general(a, b, (((0,), (0,)), ((), ())), preferred_element_type=F32)


def _ln_stats(z):
    mu = jnp.mean(z, -1, keepdims=True)
    zc = z - mu
    var = jnp.mean(zc * zc, -1, keepdims=True)
    rstd = lax.rsqrt(var + LN_EPS)
    return zc * rstd, rstd


def _ln_bwd(dh, xhat, rstd, g):
    dxh = dh * g
    m1 = jnp.mean(dxh, -1, keepdims=True)
    m2 = jnp.mean(dxh * xhat, -1, keepdims=True)
    return rstd * (dxh - m1 - xhat * m2)


def _rows8(rows, width):
    pad = [jnp.zeros((1, width), F32)] * (SUBLANES - len(rows))
    return jnp.concatenate(list(rows) + pad, axis=0)


def _mm_nn(a, w, bias, *, tm, tn, out_dtype, name, vmem_mb=40):
    M, K = a.shape
    if w.ndim == 3:
        assert w.shape[2] == tn
        n_tiles = w.shape[0]
        w_spec = pl.BlockSpec((None, K, tn), lambda j, i: (j, 0, 0))
    else:
        n_tiles = w.shape[1] // tn
        w_spec = pl.BlockSpec((K, tn), lambda j, i: (0, j))

    def body(a_ref, w_ref, b_ref, o_ref):
        o_ref[...] = (_dot(a_ref[...], w_ref[...]) + b_ref[...]).astype(o_ref.dtype)

    return pl.pallas_call(
        body, grid=(n_tiles, M // tm),
        in_specs=[pl.BlockSpec((tm, K), lambda j, i: (i, 0)), w_spec, pl.BlockSpec((1, tn), lambda j, i: (0, j))],
        out_specs=pl.BlockSpec((tm, tn), lambda j, i: (i, j)),
        out_shape=jax.ShapeDtypeStruct((M, n_tiles * tn), out_dtype),
        name=name, compiler_params=_cparams(("arbitrary", "parallel"), vmem_mb))(a, w, bias)


def _mm_nt(a, w, *, tm, a_col=0, name, vmem_mb=40):
    M = a.shape[0]
    N, K = w.shape

    def body(a_ref, w_ref, o_ref):
        o_ref[...] = _dot_nt(a_ref[...], w_ref[...]).astype(o_ref.dtype)

    return pl.pallas_call(
        body, grid=(M // tm,),
        in_specs=[pl.BlockSpec((tm, K), lambda i: (i, a_col)),
                  pl.BlockSpec((N, K), lambda i: (0, 0))],
        out_specs=pl.BlockSpec((tm, N), lambda i: (i, 0)),
        out_shape=jax.ShapeDtypeStruct((M, N), BF16),
        name=name, compiler_params=_cparams(("parallel",), vmem_mb))(a, w)


def _mm_tn(a, g, *, n_out, tn, ts, g_block, g_map, colsum=False, name, vmem_mb=48):
    S, K = a.shape
    n_s = S // ts

    def body(a_ref, g_ref, *rest):
        if colsum:
            o_ref, cs_ref, acc_ref, cacc_ref = rest
        else:
            o_ref, acc_ref = rest
        s = pl.program_id(1)

        @pl.when(s == 0)
        def _():
            acc_ref[...] = jnp.zeros_like(acc_ref)
            if colsum:
                cacc_ref[...] = jnp.zeros_like(cacc_ref)

        gv = g_ref[...]
        acc_ref[...] += _dot_tn(a_ref[...], gv)
        if colsum:
            cacc_ref[...] += jnp.broadcast_to(jnp.sum(gv.astype(F32), axis=0, keepdims=True), cacc_ref.shape)

        @pl.when(s == n_s - 1)
        def _():
            o_ref[...] = acc_ref[...].astype(o_ref.dtype)
            if colsum:
                cs_ref[...] = cacc_ref[...]

    out_specs = [pl.BlockSpec((None, K, tn), lambda j, s: (j, 0, 0))]
    out_shape = [jax.ShapeDtypeStruct((n_out, K, tn), BF16)]
    scratch = [pltpu.VMEM((K, tn), F32)]
    if colsum:
        out_specs.append(pl.BlockSpec((SUBLANES, tn), lambda j, s: (0, j)))
        out_shape.append(jax.ShapeDtypeStruct((SUBLANES, n_out * tn), F32))
        scratch.append(pltpu.VMEM((SUBLANES, tn), F32))
    res = pl.pallas_call(
        body, grid=(n_out, n_s),
        in_specs=[pl.BlockSpec((ts, K), lambda j, s: (s, 0)), pl.BlockSpec(g_block, g_map)],
        out_specs=out_specs, out_shape=out_shape, scratch_shapes=scratch,
        name=name, compiler_params=_cparams(("parallel", "arbitrary"), vmem_mb))(a, g)
    return res if colsum else res[0]


def _mm_tn_cat(a, gs, *, ts, name, vmem_mb=40):
    S, K = a.shape
    widths = [g.shape[1] for g in gs]
    n_s, total = S // ts, sum(widths)

    def body(*refs):
        a_ref, g_refs = refs[0], refs[1:1 + len(gs)]
        o_ref, cs_ref, acc_ref, cacc_ref = refs[1 + len(gs):]
        s = pl.program_id(0)

        @pl.when(s == 0)
        def _():
            acc_ref[...] = jnp.zeros_like(acc_ref)
            cacc_ref[...] = jnp.zeros_like(cacc_ref)

        av, col = a_ref[...], 0
        for g_ref, w in zip(g_refs, widths):
            gv = g_ref[...]
            acc_ref[:, col:col + w] += _dot_tn(av, gv)
            cacc_ref[:, col:col + w] += jnp.broadcast_to(jnp.sum(gv.astype(F32), axis=0, keepdims=True), (SUBLANES, w))
            col += w

        @pl.when(s == n_s - 1)
        def _():
            o_ref[...] = acc_ref[...].astype(BF16)
            cs_ref[...] = cacc_ref[...]

    return pl.pallas_call(
        body, grid=(n_s,),
        in_specs=[pl.BlockSpec((ts, K), lambda s: (s, 0))] + [pl.BlockSpec((ts, w), lambda s: (s, 0)) for w in widths],
        out_specs=[pl.BlockSpec((K, total), lambda s: (0, 0)), pl.BlockSpec((SUBLANES, total), lambda s: (0, 0))],
        out_shape=[jax.ShapeDtypeStruct((K, total), BF16), jax.ShapeDtypeStruct((SUBLANES, total), F32)],
        scratch_shapes=[pltpu.VMEM((K, total), F32), pltpu.VMEM((SUBLANES, total), F32)],
        name=name, compiler_params=_cparams(("arbitrary",), vmem_mb))(a, *gs)


DILS = tuple(d for _, d in GROUPS if d > 1)


def _res_spec(d, tm, width):
    return pl.BlockSpec((d, tm // d, width), lambda i: (0, i, 0))


def _lane_scratch(tm, width):
    return [pltpu.VMEM((tm, LANES), F32)] * (width // LANES)


def _to_residue(val, dst_refs, dils, tm, dtype, scr):
    for c, ref in enumerate(scr):
        ref[...] = val[:, c * LANES:(c + 1) * LANES]
    for dst_ref, d in zip(dst_refs, dils):
        for r in range(d):
            cols = [ref[pl.ds(r, tm // d, stride=d), :] for ref in scr]
            dst_ref[r] = jnp.concatenate(cols, axis=1).astype(dtype)


def _from_residue(rows_of, d, tm, scr):
    for r in range(d):
        v = rows_of(r).astype(F32)
        for c, ref in enumerate(scr):
            ref[pl.ds(r, tm // d, stride=d), :] = v[:, c * LANES:(c + 1) * LANES]
    return jnp.concatenate([ref[...] for ref in scr], axis=1)


def _ln0_fwd(x, g, b, after=(), *, tm=512):
    S, Dm = x.shape
    n_after = len(after)

    def body(x_ref, g_ref, b_ref, *rest):
        h_ref, hb_ref, *rest = rest[n_after:]
        xhat, _ = _ln_stats(x_ref[...])
        h = xhat * g_ref[...] + b_ref[...]
        h_ref[...] = h
        hb_ref[...] = h.astype(BF16)
        _to_residue(h, rest[:len(DILS)], DILS, tm, BF16, rest[len(DILS):])

    row = pl.BlockSpec((tm, Dm), lambda i: (i, 0))
    vec = pl.BlockSpec((1, Dm), lambda i: (0, 0))
    return pl.pallas_call(
        body, grid=(S // tm,), in_specs=[row, vec, vec] + [pl.BlockSpec(memory_space=pl.ANY)] * n_after,
        out_specs=[row, row] + [_res_spec(d, tm, Dm) for d in DILS],
        out_shape=[jax.ShapeDtypeStruct((S, Dm), F32), jax.ShapeDtypeStruct((S, Dm), BF16)]
        + [jax.ShapeDtypeStruct((d, S // d, Dm), BF16) for d in DILS],
        scratch_shapes=_lane_scratch(tm, Dm),
        name="ln0_fwd", compiler_params=_cparams(("parallel",), 32))(x, g, b, *after)


def _slab_spec(S, col0):
    return pl.BlockSpec((S, SLAB), lambda j: (0, col0 // SLAB + j))


def _zero_pads(scr, S):
    scr[0:PAD, :] = jnp.zeros((PAD, SLAB), F32)
    scr[S + PAD:S + 2 * PAD, :] = jnp.zeros((PAD, SLAB), F32)


def _shifted(scr, t):
    return (scr[PAD - 1 + t:PAD - 1 + t + CHUNK, :], scr[PAD + t:PAD + t + CHUNK, :],
            scr[PAD + 1 + t:PAD + 1 + t + CHUNK, :])


def _conv_gate_fwd(proj, conv_w):
    S = proj.shape[0]

    def body(b_ref, c_ref, h_ref, w_ref, o_ref, u_scr):
        _zero_pads(u_scr, S)
        for t in range(0, S, CHUNK):
            u_scr[PAD + t:PAD + t + CHUNK, :] = c_ref[t:t + CHUNK, :].astype(F32) * h_ref[t:t + CHUNK, :].astype(F32)
        w0, w1, w2 = w_ref[0:1, :], w_ref[1:2, :], w_ref[2:3, :]
        for t in range(0, S, CHUNK):
            um, u0, up = _shifted(u_scr, t)
            cv = w0 * um + w1 * u0 + w2 * up
            o_ref[t:t + CHUNK, :] = (b_ref[t:t + CHUNK, :].astype(F32) * cv).astype(BF16)

    return pl.pallas_call(
        body, grid=(D_CONV // SLAB,),
        in_specs=[_slab_spec(S, P_B), _slab_spec(S, P_C), _slab_spec(S, P_H),
                  pl.BlockSpec((3, SLAB), lambda j: (0, j))],
        out_specs=pl.BlockSpec((S, SLAB), lambda j: (0, j)),
        out_shape=jax.ShapeDtypeStruct((S, D_CONV), BF16),
        scratch_shapes=[pltpu.VMEM((S + 2 * PAD, SLAB), F32)],
        name="conv_gate_fwd", compiler_params=_cparams(("parallel",), 40))(proj, proj, proj, conv_w)


MASKED_DISTANCE = -1e34


def _attn_bias_table(g):
    dil = GROUPS[g][1]
    j = lax.broadcasted_iota(jnp.int32, (2 * TQ, TQ), 0)
    a = lax.broadcasted_iota(jnp.int32, (2 * TQ, TQ), 1)
    rel = jnp.abs(j - RADIUS - a)
    base = -(rel * dil).astype(F32)
    inside, after_start, before_end = rel <= RADIUS, j >= RADIUS, j < TQ + RADIUS
    variants = []
    for first, last in ((False, False), (True, False), (False, True), (True, True)):
        valid = inside & (after_start if first else True) & (before_end if last else True)
        variants.append(jnp.where(valid, base, MASKED_DISTANCE))
    return jnp.stack(variants)


def _bias_spec(nb):
    def variant(r, i):
        return (jnp.where(i == 0, 1, 0) + jnp.where(i == nb - 1, 2, 0), 0, 0)
    return pl.BlockSpec((None, 2 * TQ, TQ), variant)


def _head_stats(rows):
    pad = jnp.zeros((LANES - len(rows), TQ), F32)
    return jnp.concatenate(list(rows) + [pad], axis=0).T


def _slope(g, h):
    return 2.0 ** (-8.0 * (g * HEADS_PER_GROUP + h + 1) / (N_GROUPS * HEADS_PER_GROUP))


def _window(p_ref, c_ref, n_ref):
    return jnp.concatenate([p_ref[TQ - RADIUS:, :], c_ref[...], n_ref[:RADIUS, :]], axis=0)


def _pair(a, h):
    return a[:, (h // 2) * LANES:(h // 2 + 1) * LANES]


def _own_lanes(a, h):
    lane = lax.broadcasted_iota(jnp.int32, a.shape, 1)
    return jnp.where((lane >= HEAD_DIM) == (h % 2 == 1), a, jnp.zeros_like(a))


def _own_rows(a, h):
    return a[(h % 2) * HEAD_DIM:(h % 2 + 1) * HEAD_DIM, :]


def _qkv_specs(nb, col0):
    def spec(col, shift):
        return pl.BlockSpec((None, TQ, GROUP_W), lambda r, i: (r, jnp.clip(i + shift, 0, nb - 1), col))

    return [spec(col0, 0), spec(col0 + 1, -1), spec(col0 + 1, 0), spec(col0 + 1, 1),
            spec(col0 + 2, -1), spec(col0 + 2, 0), spec(col0 + 2, 1)]


def _attn_fwd(qkv, col0, g):
    dil, sub, _ = qkv.shape
    nb = sub // TQ

    def body(q_ref, kp, kc, kn, vp, vc, vn, bias_ref, o_ref, lse_ref, ot_scr, s_scr, p_scr):
        kwin = _window(kp, kc, kn)
        vwin = _window(vp, vc, vn)
        q = q_ref[...] * ATT_SCALE
        for h in range(HEADS_PER_GROUP):
            s_scr[h] = _dot_nt(_pair(kwin, h), _own_lanes(_pair(q, h), h))
        lse, inv_den = [], []
        for h in range(HEADS_PER_GROUP):
            s = s_scr[h] + _slope(g, h) * bias_ref[...]
            m = jnp.max(s, axis=0, keepdims=True)
            p = jnp.exp(s - m)
            den = jnp.sum(p, axis=0, keepdims=True)
            p_scr[h] = p.astype(BF16)
            inv_den.append(1.0 / den)
            lse.append(m + jnp.log(den))
        for h in range(HEADS_PER_GROUP):
            ot = _dot_tn(_pair(vwin, h), p_scr[h])
            ot_scr[h * HEAD_DIM:(h + 1) * HEAD_DIM, :] = _own_rows(ot, h) * inv_den[h]
        o_ref[...] = ot_scr[...].T
        lse_ref[...] = _head_stats(lse)

    return pl.pallas_call(
        body, grid=(dil, nb), in_specs=_qkv_specs(nb, col0) + [_bias_spec(nb)],
        out_specs=[pl.BlockSpec((None, TQ, GROUP_W), lambda r, i: (r, i, 0)),
                   pl.BlockSpec((None, TQ, LANES), lambda r, i: (r, i, 0))],
        out_shape=[jax.ShapeDtypeStruct((dil, sub, GROUP_W), F32), jax.ShapeDtypeStruct((dil, sub, LANES), F32)],
        scratch_shapes=[pltpu.VMEM((GROUP_W, TQ), F32), pltpu.VMEM((HEADS_PER_GROUP, 2 * TQ, TQ), F32),
                        pltpu.VMEM((HEADS_PER_GROUP, 2 * TQ, TQ), BF16)],
        name=f"attn_fwd_g{g}", compiler_params=_cparams(("parallel", "arbitrary"), 32))(
            *([qkv] * 7), _attn_bias_table(g))


def _expand_heads():
    h = lax.broadcasted_iota(jnp.int32, (LANES, GROUP_W), 0)
    c = lax.broadcasted_iota(jnp.int32, (LANES, GROUP_W), 1)
    return (c // HEAD_DIM == h).astype(F32)


def _dot_f32(a, b):
    return jnp.dot(a, b, preferred_element_type=F32, precision=lax.Precision.HIGHEST)


def _attn_combine(outs, lses, *, tm=512):
    S = outs[0].shape[1]
    n_col = GROUP_W // LANES

    def body(*refs):
        ins, e_ref = refs[:2 * N_GROUPS], refs[2 * N_GROUPS]
        c_ref, cb_ref, lt_ref = refs[2 * N_GROUPS + 1:2 * N_GROUPS + 4]
        scr = refs[2 * N_GROUPS + 4:]
        o, l = [ins[0][0]], [ins[N_GROUPS][0]]
        for k, d in enumerate(DILS):
            o_ref, l_ref = ins[1 + k], ins[N_GROUPS + 1 + k]
            o.append(_from_residue(lambda r: o_ref[r], d, tm, scr[k * (n_col + 1):k * (n_col + 1) + n_col]))
            l.append(_from_residue(lambda r: l_ref[r], d, tm, scr[k * (n_col + 1) + n_col:(k + 1) * (n_col + 1)]))
        m = jnp.maximum(jnp.maximum(l[0], l[1]), l[2])
        e = [jnp.exp(v - m) for v in l]
        den = e[0] + e[1] + e[2]
        comb = sum(_dot_f32(ev / den, e_ref[...]) * ov for ev, ov in zip(e, o))
        c_ref[...] = comb
        cb_ref[...] = comb.astype(BF16)
        lt_ref[...] = m + jnp.log(den)

    row = pl.BlockSpec((tm, GROUP_W), lambda i: (i, 0))
    dils = [d for _, d in GROUPS]
    return pl.pallas_call(
        body, grid=(S // tm,),
        in_specs=[_res_spec(d, tm, GROUP_W) for d in dils] + [_res_spec(d, tm, LANES) for d in dils]
        + [_resident((LANES, GROUP_W))],
        out_specs=[row, row, pl.BlockSpec((tm, LANES), lambda i: (i, 0))],
        out_shape=[jax.ShapeDtypeStruct((S, GROUP_W), F32), jax.ShapeDtypeStruct((S, GROUP_W), BF16),
                   jax.ShapeDtypeStruct((S, LANES), F32)],
        scratch_shapes=_lane_scratch(tm, GROUP_W + LANES) * len(DILS),
        name="attn_combine", compiler_params=_cparams(("parallel",), 32))(*outs, *lses, _expand_heads())


def _branch_mix(ya_in, comb_b, w_a, w_b, proj, *, tm=512):
    S = ya_in.shape[0]

    def body(ya_ref, cb_ref, wa_ref, wb_ref, ga_ref, gb_ref, yab_ref, mx_ref):
        y_a = _dot(ya_ref[...], wa_ref[...])
        y_b = _dot(cb_ref[...], wb_ref[...])
        yab_ref[:, 0:D_MODEL] = y_a.astype(BF16)
        yab_ref[:, D_MODEL:2 * D_MODEL] = y_b.astype(BF16)
        mx = jax.nn.sigmoid(ga_ref[...].astype(F32)) * y_a + jax.nn.sigmoid(gb_ref[...].astype(F32)) * y_b
        mx_ref[...] = mx.astype(BF16)

    return pl.pallas_call(
        body, grid=(S // tm,),
        in_specs=[pl.BlockSpec((tm, D_CONV), lambda i: (i, 0)), pl.BlockSpec((tm, GROUP_W), lambda i: (i, 0)),
                  pl.BlockSpec((D_CONV, D_MODEL), lambda i: (0, 0)), pl.BlockSpec((GROUP_W, D_MODEL), lambda i: (0, 0)),
                  pl.BlockSpec((tm, D_MODEL), lambda i: (i, P_GA // D_MODEL)),
                  pl.BlockSpec((tm, D_MODEL), lambda i: (i, P_GB // D_MODEL))],
        out_specs=[pl.BlockSpec((tm, 2 * D_MODEL), lambda i: (i, 0)), pl.BlockSpec((tm, D_MODEL), lambda i: (i, 0))],
        out_shape=[jax.ShapeDtypeStruct((S, 2 * D_MODEL), BF16), jax.ShapeDtypeStruct((S, D_MODEL), BF16)],
        name="branch_mix", compiler_params=_cparams(("parallel",), 40))(ya_in, comb_b, w_a, w_b, proj, proj)


def _mix_ln1(mixin, w_o, b_o, h0, g1, b1, *, tm=512):
    S = mixin.shape[0]

    def body(mx_ref, wo_ref, bo_ref, h0_ref, g_ref, b_ref, xh_ref, rs_ref, h1b_ref):
        z = ALPHA * h0_ref[...] + _dot(mx_ref[...], wo_ref[...]) + bo_ref[...]
        xhat, rstd = _ln_stats(z)
        xh_ref[...] = xhat
        rs_ref[...] = jnp.broadcast_to(rstd, (tm, LANES))
        h1b_ref[...] = (xhat * g_ref[...] + b_ref[...]).astype(BF16)

    row = pl.BlockSpec((tm, D_MODEL), lambda i: (i, 0))
    vec = pl.BlockSpec((1, D_MODEL), lambda i: (0, 0))
    return pl.pallas_call(
        body, grid=(S // tm,),
        in_specs=[row, pl.BlockSpec((D_MODEL, D_MODEL), lambda i: (0, 0)), vec, row, vec, vec],
        out_specs=[row, pl.BlockSpec((tm, LANES), lambda i: (i, 0)), row],
        out_shape=[jax.ShapeDtypeStruct((S, D_MODEL), F32), jax.ShapeDtypeStruct((S, LANES), F32),
                   jax.ShapeDtypeStruct((S, D_MODEL), BF16)],
        name="mix_ln1", compiler_params=_cparams(("parallel",), 40))(mixin, w_o, b_o, h0, g1, b1)


def _gelu_parts(cz):
    cdf = 0.5 * (1.0 + lax.erf(cz * INV_SQRT2))
    return cdf, cz * cdf


def _ffn_conv_fwd(up, cw, cb):
    S = up.shape[0]

    def body(a_ref, g_ref, w_ref, cb_ref, o_ref, a_scr):
        _zero_pads(a_scr, S)
        for t in range(0, S, CHUNK):
            a_scr[PAD + t:PAD + t + CHUNK, :] = a_ref[t:t + CHUNK, :].astype(F32)
        w0, w1, w2 = w_ref[0:1, :], w_ref[1:2, :], w_ref[2:3, :]
        for t in range(0, S, CHUNK):
            am, a0, ap = _shifted(a_scr, t)
            _, gel = _gelu_parts(w0 * am + w1 * a0 + w2 * ap + cb_ref[...])
            o_ref[t:t + CHUNK, :] = (gel * g_ref[t:t + CHUNK, :].astype(F32)).astype(BF16)

    return pl.pallas_call(
        body, grid=(D_FF // SLAB,),
        in_specs=[_slab_spec(S, 0), _slab_spec(S, D_FF), pl.BlockSpec((3, SLAB), lambda j: (0, j)),
                  pl.BlockSpec((1, SLAB), lambda j: (0, j))],
        out_specs=pl.BlockSpec((S, SLAB), lambda j: (0, j)),
        out_shape=jax.ShapeDtypeStruct((S, D_FF), BF16),
        scratch_shapes=[pltpu.VMEM((S + 2 * PAD, SLAB), F32)],
        name="ffn_conv_fwd", compiler_params=_cparams(("parallel",), 40))(up, up, cw, cb)


def _down_ln2_loss(f, w_down, b_down, xhat1, g1, b1, g2, b2, target, *, tm=512):
    S = f.shape[0]

    def body(f_ref, wd_ref, bd_ref, xh1_ref, g1_ref, b1_ref, g2_ref, b2_ref, t_ref, dz_ref, dzb_ref, st_ref):
        h1 = xh1_ref[...] * g1_ref[...] + b1_ref[...]
        z = ALPHA * h1 + _dot(f_ref[...], wd_ref[...]) + bd_ref[...]
        xhat, rstd = _ln_stats(z)
        err = xhat * g2_ref[...] + b2_ref[...] - t_ref[...]
        loss = (0.5 / D_MODEL) * jnp.sum(jnp.sum(err * err, axis=1, keepdims=True), axis=0, keepdims=True)
        dh2 = err * (1.0 / D_MODEL)
        dz = _ln_bwd(dh2, xhat, rstd, g2_ref[...])
        dz_ref[...] = dz
        dzb_ref[...] = dz.astype(BF16)
        upd = _rows8([jnp.sum(dh2 * xhat, axis=0, keepdims=True), jnp.sum(dh2, axis=0, keepdims=True),
                      jnp.broadcast_to(loss, (1, D_MODEL)), jnp.sum(dz, axis=0, keepdims=True)], D_MODEL)

        @pl.when(pl.program_id(0) == 0)
        def _():
            st_ref[...] = upd

        @pl.when(pl.program_id(0) != 0)
        def _():
            st_ref[...] += upd

    row = pl.BlockSpec((tm, D_MODEL), lambda i: (i, 0))
    vec = pl.BlockSpec((1, D_MODEL), lambda i: (0, 0))
    return pl.pallas_call(
        body, grid=(S // tm,),
        in_specs=[pl.BlockSpec((tm, D_FF), lambda i: (i, 0)), _resident((D_FF, D_MODEL)),
                  vec, row, vec, vec, vec, vec, row],
        out_specs=[row, row, pl.BlockSpec((SUBLANES, D_MODEL), lambda i: (0, 0))],
        out_shape=[jax.ShapeDtypeStruct((S, D_MODEL), F32), jax.ShapeDtypeStruct((S, D_MODEL), BF16),
                   jax.ShapeDtypeStruct((SUBLANES, D_MODEL), F32)],
        name="down_ln2_loss", compiler_params=_cparams(("arbitrary",), 56))(
            f, w_down, b_down, xhat1, g1, b1, g2, b2, target)


def _ffn_conv_bwd(up, df, cw, cb):
    S = up.shape[0]

    def body(a_ref, g_ref, df_ref, w_ref, cb_ref, dup_ref, sm_ref, a_scr, d_scr):
        _zero_pads(a_scr, S)
        _zero_pads(d_scr, S)
        for t in range(0, S, CHUNK):
            a_scr[PAD + t:PAD + t + CHUNK, :] = a_ref[t:t + CHUNK, :].astype(F32)
        w0, w1, w2 = w_ref[0:1, :], w_ref[1:2, :], w_ref[2:3, :]
        zero = jnp.zeros((1, SLAB), F32)
        s_dg, s_dcz, s_w0, s_w1, s_w2 = zero, zero, zero, zero, zero
        for t in range(0, S, CHUNK):
            am, a0, ap = _shifted(a_scr, t)
            cz = w0 * am + w1 * a0 + w2 * ap + cb_ref[...]
            cdf, gel = _gelu_parts(cz)
            dfv = df_ref[t:t + CHUNK, :].astype(F32)
            dgte = dfv * gel
            dcz = dfv * g_ref[t:t + CHUNK, :].astype(F32) * (cdf + cz * jnp.exp(-0.5 * cz * cz) * INV_SQRT_2PI)
            dup_ref[1, t:t + CHUNK, :] = dgte.astype(BF16)
            d_scr[PAD + t:PAD + t + CHUNK, :] = dcz
            s_dg = s_dg + jnp.sum(dgte, axis=0, keepdims=True)
            s_dcz = s_dcz + jnp.sum(dcz, axis=0, keepdims=True)
            s_w0 = s_w0 + jnp.sum(dcz * am, axis=0, keepdims=True)
            s_w1 = s_w1 + jnp.sum(dcz * a0, axis=0, keepdims=True)
            s_w2 = s_w2 + jnp.sum(dcz * ap, axis=0, keepdims=True)
        s_da = zero
        for t in range(0, S, CHUNK):
            dm, d0, dp = _shifted(d_scr, t)
            da = w0 * dp + w1 * d0 + w2 * dm
            dup_ref[0, t:t + CHUNK, :] = da.astype(BF16)
            s_da = s_da + jnp.sum(da, axis=0, keepdims=True)
        sm_ref[...] = _rows8([s_da, s_dg, s_dcz, s_w0, s_w1, s_w2], SLAB)

    return pl.pallas_call(
        body, grid=(D_FF // SLAB,),
        in_specs=[_slab_spec(S, 0), _slab_spec(S, D_FF), pl.BlockSpec((S, SLAB), lambda j: (0, j)),
                  pl.BlockSpec((3, SLAB), lambda j: (0, j)), pl.BlockSpec((1, SLAB), lambda j: (0, j))],
        out_specs=[pl.BlockSpec((2, S, SLAB), lambda j: (0, 0, j)), pl.BlockSpec((SUBLANES, SLAB), lambda j: (0, j))],
        out_shape=[jax.ShapeDtypeStruct((2, S, D_FF), BF16), jax.ShapeDtypeStruct((SUBLANES, D_FF), F32)],
        scratch_shapes=[pltpu.VMEM((S + 2 * PAD, SLAB), F32)] * 2,
        name="ffn_conv_bwd", compiler_params=_cparams(("parallel",), 48))(up, up, df, cw, cb)


def _up_bwd_ln1(dup, w_up3, dz2, xhat1, rstd1, g1, *, tm=512):
    S = dz2.shape[0]
    ns, _, tk = w_up3.shape
    per_plane = D_FF // tk

    def body(du_ref, w_ref, dz2_ref, xh_ref, rs_ref, g_ref, dz_ref, dzb_ref, st_ref):
        dh = ALPHA * dz2_ref[...]
        for k in range(ns):
            col = (k % per_plane) * tk
            dh = dh + _dot_nt(du_ref[k // per_plane, :, col:col + tk], w_ref[k])
        xhat = xh_ref[...]
        dz = _ln_bwd(dh, xhat, rs_ref[:, 0:1], g_ref[...])
        dz_ref[...] = dz
        dzb_ref[...] = dz.astype(BF16)
        upd = _rows8([jnp.sum(dh * xhat, axis=0, keepdims=True), jnp.sum(dh, axis=0, keepdims=True),
                      jnp.sum(dz, axis=0, keepdims=True)], D_MODEL)

        @pl.when(pl.program_id(0) == 0)
        def _():
            st_ref[...] = upd

        @pl.when(pl.program_id(0) != 0)
        def _():
            st_ref[...] += upd

    row = pl.BlockSpec((tm, D_MODEL), lambda i: (i, 0))
    return pl.pallas_call(
        body, grid=(S // tm,),
        in_specs=[pl.BlockSpec((dup.shape[0], tm, D_FF), lambda i: (0, i, 0)), _resident(w_up3.shape),
                  row, row, pl.BlockSpec((tm, LANES), lambda i: (i, 0)), pl.BlockSpec((1, D_MODEL), lambda i: (0, 0))],
        out_specs=[row, row, pl.BlockSpec((SUBLANES, D_MODEL), lambda i: (0, 0))],
        out_shape=[jax.ShapeDtypeStruct((S, D_MODEL), F32), jax.ShapeDtypeStruct((S, D_MODEL), BF16),
                   jax.ShapeDtypeStruct((SUBLANES, D_MODEL), F32)],
        name="up_bwd_ln1", compiler_params=_cparams(("arbitrary",), 56))(dup, w_up3, dz2, xhat1, rstd1, g1)


def _mix_bwd(dz1b, w_o, proj, yab, *, tm=512):
    S = dz1b.shape[0]

    def body(dz_ref, wo_ref, ga_ref, gb_ref, y_ref, dy_ref, dg_ref):
        dmx = _dot_nt(dz_ref[...], wo_ref[...])
        for k, gt_ref in enumerate((ga_ref, gb_ref)):
            sl = slice(k * D_MODEL, (k + 1) * D_MODEL)
            sg = jax.nn.sigmoid(gt_ref[...].astype(F32))
            dy_ref[:, sl] = (dmx * sg).astype(BF16)
            dg_ref[k] = (dmx * y_ref[:, sl].astype(F32) * sg * (1.0 - sg)).astype(BF16)

    row = pl.BlockSpec((tm, D_MODEL), lambda i: (i, 0))
    wide = pl.BlockSpec((tm, 2 * D_MODEL), lambda i: (i, 0))
    return pl.pallas_call(
        body, grid=(S // tm,),
        in_specs=[row, _resident(w_o.shape), pl.BlockSpec((tm, D_MODEL), lambda i: (i, P_GA // D_MODEL)),
                  pl.BlockSpec((tm, D_MODEL), lambda i: (i, P_GB // D_MODEL)), wide],
        out_specs=[wide, pl.BlockSpec((2, tm, D_MODEL), lambda i: (0, i, 0))],
        out_shape=[jax.ShapeDtypeStruct((S, 2 * D_MODEL), BF16), jax.ShapeDtypeStruct((2, S, D_MODEL), BF16)],
        name="mix_bwd", compiler_params=_cparams(("parallel",), 40))(dz1b, w_o, proj, proj, yab)


def _conv_gate_bwd(proj, dya_in, conv_w):
    S = proj.shape[0]

    def body(b_ref, c_ref, h_ref, dy_ref, w_ref, o_ref, sm_ref, u_scr, d_scr):
        _zero_pads(u_scr, S)
        _zero_pads(d_scr, S)
        for t in range(0, S, CHUNK):
            u_scr[PAD + t:PAD + t + CHUNK, :] = c_ref[t:t + CHUNK, :].astype(F32) * h_ref[t:t + CHUNK, :].astype(F32)
        w0, w1, w2 = w_ref[0:1, :], w_ref[1:2, :], w_ref[2:3, :]
        zero = jnp.zeros((1, SLAB), F32)
        s_w0, s_w1, s_w2 = zero, zero, zero
        for t in range(0, S, CHUNK):
            um, u0, up = _shifted(u_scr, t)
            dy = dy_ref[t:t + CHUNK, :].astype(F32)
            o_ref[0, t:t + CHUNK, :] = (dy * (w0 * um + w1 * u0 + w2 * up)).astype(BF16)
            dcv = dy * b_ref[t:t + CHUNK, :].astype(F32)
            d_scr[PAD + t:PAD + t + CHUNK, :] = dcv
            s_w0 = s_w0 + jnp.sum(dcv * um, axis=0, keepdims=True)
            s_w1 = s_w1 + jnp.sum(dcv * u0, axis=0, keepdims=True)
            s_w2 = s_w2 + jnp.sum(dcv * up, axis=0, keepdims=True)
        for t in range(0, S, CHUNK):
            dm, d0, dp = _shifted(d_scr, t)
            du = w0 * dp + w1 * d0 + w2 * dm
            o_ref[1, t:t + CHUNK, :] = (du * h_ref[t:t + CHUNK, :].astype(F32)).astype(BF16)
            o_ref[2, t:t + CHUNK, :] = (du * c_ref[t:t + CHUNK, :].astype(F32)).astype(BF16)
        sm_ref[...] = _rows8([s_w0, s_w1, s_w2], SLAB)

    return pl.pallas_call(
        body, grid=(D_CONV // SLAB,),
        in_specs=[_slab_spec(S, P_B), _slab_spec(S, P_C), _slab_spec(S, P_H),
                  pl.BlockSpec((S, SLAB), lambda j: (0, j)), pl.BlockSpec((3, SLAB), lambda j: (0, j))],
        out_specs=[pl.BlockSpec((3, S, SLAB), lambda j: (0, 0, j)), pl.BlockSpec((SUBLANES, SLAB), lambda j: (0, j))],
        out_shape=[jax.ShapeDtypeStruct((3, S, D_CONV), BF16), jax.ShapeDtypeStruct((SUBLANES, D_CONV), F32)],
        scratch_shapes=[pltpu.VMEM((S + 2 * PAD, SLAB), F32)] * 2,
        name="conv_gate_bwd", compiler_params=_cparams(("parallel",), 48))(proj, proj, proj, dya_in, conv_w)


def _comb_bwd(dyab, w_b, comb, lse_tot, *, tm=512):
    S = comb.shape[0]
    widths, dtypes = (GROUP_W, LANES, LANES), (BF16, F32, F32)

    def body(dy_ref, wb_ref, c_ref, lt_ref, e_ref, *rest):
        outs, scr = rest[:3 * N_GROUPS], rest[3 * N_GROUPS:]
        dcb = _dot_nt(dy_ref[...], wb_ref[...]).astype(BF16)
        dc = dcb.astype(F32)
        delta = lax.dot_general(dc * c_ref[...], e_ref[...], (((1,), (1,)), ((), ())),
                                preferred_element_type=F32, precision=lax.Precision.HIGHEST)
        for k, (val, dtype) in enumerate(zip((dc, lt_ref[...], delta), dtypes)):
            outs[k][0] = val.astype(dtype)
            _to_residue(val, [outs[3 * (1 + j) + k] for j in range(len(DILS))], DILS, tm, dtype,
                        scr[:val.shape[1] // LANES])

    out_specs, out_shape = [], []
    for _, d in GROUPS:
        out_specs += [_res_spec(d, tm, w) for w in widths]
        out_shape += [jax.ShapeDtypeStruct((d, S // d, w), t) for w, t in zip(widths, dtypes)]
    res = pl.pallas_call(
        body, grid=(S // tm,),
        in_specs=[pl.BlockSpec((tm, D_MODEL), lambda i: (i, 1)), _resident(w_b.shape),
                  pl.BlockSpec((tm, GROUP_W), lambda i: (i, 0)), pl.BlockSpec((tm, LANES), lambda i: (i, 0)),
                  _resident((LANES, GROUP_W))],
        out_specs=out_specs, out_shape=out_shape, scratch_shapes=_lane_scratch(tm, GROUP_W),
        name="comb_bwd", compiler_params=_cparams(("parallel",), 32))(dyab, w_b, comb, lse_tot, _expand_heads())
    return [tuple(res[3 * g:3 * g + 3]) for g in range(N_GROUPS)]


def _attn_bwd(qkv, col0, g, dcomb, lse_tot, delta):
    dil, sub, _ = qkv.shape
    nb = sub // TQ

    def body(q_ref, kp, kc, kn, vp, vc, vn, do_ref, lse_ref, dl_ref, bias_ref, dq_ref, dk_ref, dv_ref,
             ak, av, dqt_scr, s_scr, dp_scr, ds_scr, p_scr):
        i = pl.program_id(1)

        @pl.when(i == 0)
        def _():
            ak[...] = jnp.zeros_like(ak)
            av[...] = jnp.zeros_like(av)

        @pl.when(i < nb)
        def _():
            kwin = _window(kp, kc, kn)
            vwin = _window(vp, vc, vn)
            q = q_ref[...] * ATT_SCALE
            do = do_ref[...]
            lse_t, dl_t = lse_ref[...].T, dl_ref[...].T
            for h in range(HEADS_PER_GROUP):
                s_scr[h] = _dot_nt(_pair(kwin, h), _own_lanes(_pair(q, h), h))
                dp_scr[h] = _dot_nt(_pair(vwin, h), _own_lanes(_pair(do, h), h))
            for h in range(HEADS_PER_GROUP):
                p = jnp.exp(s_scr[h] + _slope(g, h) * bias_ref[...] - lse_t[h:h + 1, :])
                ds_scr[h] = (p * (dp_scr[h] - dl_t[h:h + 1, :])).astype(BF16)
                p_scr[h] = p.astype(BF16)
            for h in range(HEADS_PER_GROUP):
                dqt_scr[h * HEAD_DIM:(h + 1) * HEAD_DIM, :] = _own_rows(_dot_tn(_pair(kwin, h), ds_scr[h]), h)
            for h in range(0, HEADS_PER_GROUP, 2):
                cols = slice(h * HEAD_DIM, (h + 2) * HEAD_DIM)
                q2 = jnp.concatenate([_own_lanes(_pair(q, h), h), _own_lanes(_pair(q, h), h + 1)], axis=0)
                do2 = jnp.concatenate([_own_lanes(_pair(do, h), h), _own_lanes(_pair(do, h), h + 1)], axis=0)
                ak[RADIUS:RADIUS + 2 * TQ, cols] += _dot(jnp.concatenate([ds_scr[h], ds_scr[h + 1]], axis=1), q2)
                av[RADIUS:RADIUS + 2 * TQ, cols] += _dot(jnp.concatenate([p_scr[h], p_scr[h + 1]], axis=1), do2)
            dq_ref[...] = (dqt_scr[...].T * ATT_SCALE).astype(BF16)

        dk_ref[...] = ak[0:TQ, :].astype(BF16)
        dv_ref[...] = av[0:TQ, :].astype(BF16)
        ak[0:2 * TQ, :] = ak[TQ:3 * TQ, :]
        av[0:2 * TQ, :] = av[TQ:3 * TQ, :]
        ak[2 * TQ:3 * TQ, :] = jnp.zeros((TQ, GROUP_W), F32)
        av[2 * TQ:3 * TQ, :] = jnp.zeros((TQ, GROUP_W), F32)

    tok = pl.BlockSpec((None, TQ, GROUP_W), lambda r, i: (r, jnp.minimum(i, nb - 1), 0))
    stat = pl.BlockSpec((None, TQ, LANES), lambda r, i: (r, jnp.minimum(i, nb - 1), 0))
    dkv_spec = pl.BlockSpec((None, TQ, GROUP_W), lambda r, i: (r, jnp.maximum(i - 1, 0), 0))
    return pl.pallas_call(
        body, grid=(dil, nb + 1), in_specs=_qkv_specs(nb, col0) + [tok, stat, stat, _bias_spec(nb)],
        out_specs=[tok, dkv_spec, dkv_spec], out_shape=[jax.ShapeDtypeStruct((dil, sub, GROUP_W), BF16)] * 3,
        scratch_shapes=[pltpu.VMEM((3 * TQ, GROUP_W), F32)] * 2 + [pltpu.VMEM((GROUP_W, TQ), F32)]
        + [pltpu.VMEM((HEADS_PER_GROUP, 2 * TQ, TQ), F32)] * 2 + [pltpu.VMEM((HEADS_PER_GROUP, 2 * TQ, TQ), BF16)] * 2,
        name=f"attn_bwd_g{g}", compiler_params=_cparams(("arbitrary", "arbitrary"), 32))(
            *([qkv] * 7), dcomb, lse_tot, delta, _attn_bias_table(g))


def _in_bwd_ln0(dgated, dqkv, w_nat, w_dil, dz1, x, g0, *, tm=256):
    S = x.shape[0]
    n_gated, n_in = len(dgated), 3 * N_GROUPS

    def body(*refs):
        g_refs, d_refs = refs[:n_gated], refs[n_gated:n_gated + n_in]
        wn_ref, *wd_refs = refs[n_gated + n_in:n_gated + n_in + N_GROUPS]
        dz_ref, x_ref, g_ref, gx_ref, st_ref, *tmp_ref = refs[n_gated + n_in + N_GROUPS:]
        dh = ALPHA * dz_ref[...]
        col = 0
        for ref in g_refs:
            for k in range(ref.shape[0]):
                dh = dh + _dot_nt(ref[k], wn_ref[:, col:col + D_MODEL])
                col += D_MODEL
        for g, (_, d) in enumerate(GROUPS):
            rows = [jnp.concatenate([d_refs[3 * g + k][r] for k in range(3)], axis=1) for r in range(d)]
            w = wn_ref[:, col:col + QKV_W] if d == 1 else wd_refs[g - 1][...]
            res = _dot_nt(jnp.concatenate(rows, axis=0), w)
            if d == 1:
                dh = dh + res
            else:
                n = tm // d
                dh = dh + _from_residue(lambda r: res[r * n:(r + 1) * n, :], d, tm, tmp_ref)
        xhat, rstd = _ln_stats(x_ref[...])
        gx_ref[...] = _ln_bwd(dh, xhat, rstd, g_ref[...])
        upd = _rows8([jnp.sum(dh * xhat, axis=0, keepdims=True), jnp.sum(dh, axis=0, keepdims=True)], D_MODEL)

        @pl.when(pl.program_id(0) == 0)
        def _():
            st_ref[...] = upd

        @pl.when(pl.program_id(0) != 0)
        def _():
            st_ref[...] += upd

    row = pl.BlockSpec((tm, D_MODEL), lambda i: (i, 0))
    g_specs = [pl.BlockSpec((a.shape[0], tm, D_MODEL), lambda i: (0, i, 0)) for a in dgated]
    d_specs = []
    for _, d in GROUPS:
        d_specs += [_res_spec(d, tm, GROUP_W)] * 3
    operands = list(dgated) + [a for grp in dqkv for a in grp] + [w_nat] + list(w_dil) + [dz1, x, g0]
    return pl.pallas_call(
        body, grid=(S // tm,),
        in_specs=g_specs + d_specs + [_resident(w_nat.shape)] + [_resident(w.shape) for w in w_dil]
        + [row, row, pl.BlockSpec((1, D_MODEL), lambda i: (0, 0))],
        out_specs=[row, pl.BlockSpec((SUBLANES, D_MODEL), lambda i: (0, 0))],
        out_shape=[jax.ShapeDtypeStruct((S, D_MODEL), F32), jax.ShapeDtypeStruct((SUBLANES, D_MODEL), F32)],
        scratch_shapes=_lane_scratch(tm, D_MODEL),
        name="in_bwd_ln0", compiler_params=_cparams(("arbitrary",), 52))(*operands)


HBM_SPEC = pl.BlockSpec(memory_space=pltpu.HBM)


def _place():
    x, y, c = lax.axis_index("x"), lax.axis_index("y"), lax.axis_index("c")
    chips = [(1 - x, y), (x, 1 - y), (1 - x, 1 - y)]
    return x, y, c, chips


def _allgather_shards(shards, after, *, name, collective_id):
    n = len(shards)
    per = 6

    def body(*refs):
        ins, outs = refs[:n], refs[n + len(after):2 * n + len(after)]
        send_sems, recv_sems, loc_sems = refs[2 * n + len(after):]
        x, y, c, chips = _place()
        me = 2 * x + y
        sib = (x, y, 1 - c)
        peers = [sib] + [(px, py, c) for px, py in chips]
        barrier = pltpu.get_barrier_semaphore()
        for peer in peers:
            pl.semaphore_signal(barrier, inc=1, device_id=peer, device_id_type=MESH)
        pl.semaphore_wait(barrier, len(peers))

        def rcopy(w, k, src, dst, to):
            return pltpu.make_async_remote_copy(src_ref=src, dst_ref=dst, send_sem=send_sems.at[per * w + k],
                                                recv_sem=recv_sems.at[per * w + k], device_id=to, device_id_type=MESH)

        split = [s.shape[0] == N_CORES for s in shards]
        half = lambda w: c if split[w] else 0
        local, sends = [], []
        for w in range(n):
            cp = pltpu.make_async_copy(ins[w], outs[w].at[me], loc_sems.at[w])
            cp.start()
            local.append(cp)
            for j, (px, py) in enumerate(chips):
                cp = rcopy(w, j, ins[w].at[half(w)], outs[w].at[me, half(w)], (px, py, c))
                cp.start()
                sends.append(cp)
        for w in range(n):
            for j, (px, py) in enumerate(chips):
                slot = outs[w].at[2 * px + py, half(w)]
                rcopy(w, j, slot, slot, (px, py, c)).wait_recv()
                if split[w]:
                    cp = rcopy(w, 3 + j, slot, slot, sib)
                    cp.start()
                    sends.append(cp)
        for w in range(n):
            if split[w]:
                for j, (px, py) in enumerate(chips):
                    slot = outs[w].at[2 * px + py, 1 - c]
                    rcopy(w, 3 + j, slot, slot, sib).wait_recv()
        for cp in sends:
            cp.wait_send()
        for cp in local:
            cp.wait()

    return pl.kernel(
        body, out_type=[jax.ShapeDtypeStruct((N_CHIPS,) + s.shape, s.dtype) for s in shards],
        mesh=plsc.ScalarSubcoreMesh(axis_name="sequencer", num_cores=1),
        scratch_types=[pltpu.SemaphoreType.DMA((per * n,)), pltpu.SemaphoreType.DMA((per * n,)),
                       pltpu.SemaphoreType.DMA((n,))],
        name=name, compiler_params=pltpu.CompilerParams(collective_id=collective_id))(*shards, *after)


def _exchange_grads(grads, *, name, collective_id):
    n = len(grads)
    per = 7

    def body(*refs):
        ins, outs = refs[:n], refs[n:2 * n]
        send_sems, recv_sems, loc_sems = refs[2 * n:]
        x, y, c, chips = _place()
        me = 2 * x + y
        sib = (x, y, 1 - c)
        peers = [sib] + [(px, py, c) for px, py in chips]
        barrier = pltpu.get_barrier_semaphore()
        for peer in peers:
            pl.semaphore_signal(barrier, inc=1, device_id=peer, device_id_type=MESH)
        pl.semaphore_wait(barrier, len(peers))

        def rcopy(w, k, src, dst, to):
            return pltpu.make_async_remote_copy(src_ref=src, dst_ref=dst, send_sem=send_sems.at[per * w + k],
                                                recv_sem=recv_sems.at[per * w + k], device_id=to, device_id_type=MESH)

        local, sends = [], []
        for w in range(n):
            cp = pltpu.make_async_copy(ins[w].at[me], outs[w].at[c, me], loc_sems.at[w])
            cp.start()
            local.append(cp)
            cp = rcopy(w, 0, ins[w].at[me], outs[w].at[c, me], sib)
            cp.start()
            sends.append(cp)
            for j, (px, py) in enumerate(chips):
                cp = rcopy(w, 1 + j, ins[w].at[2 * px + py], outs[w].at[c, me], (px, py, c))
                cp.start()
                sends.append(cp)
        for w in range(n):
            for j, (px, py) in enumerate(chips):
                slot = outs[w].at[c, 2 * px + py]
                rcopy(w, 1 + j, slot, slot, (px, py, c)).wait_recv()
                cp = rcopy(w, 4 + j, slot, slot, sib)
                cp.start()
                sends.append(cp)
        for w in range(n):
            slot = outs[w].at[1 - c, me]
            rcopy(w, 0, slot, slot, sib).wait_recv()
            for j, (px, py) in enumerate(chips):
                slot = outs[w].at[1 - c, 2 * px + py]
                rcopy(w, 4 + j, slot, slot, sib).wait_recv()
        for cp in sends:
            cp.wait_send()
        for cp in local:
            cp.wait()

    return pl.kernel(
        body, out_type=[jax.ShapeDtypeStruct((N_CORES,) + g.shape, g.dtype) for g in grads],
        mesh=plsc.ScalarSubcoreMesh(axis_name="sequencer", num_cores=1),
        scratch_types=[pltpu.SemaphoreType.DMA((per * n,)), pltpu.SemaphoreType.DMA((per * n,)),
                       pltpu.SemaphoreType.DMA((n,))],
        name=name, compiler_params=pltpu.CompilerParams(collective_id=collective_id))(*grads)


def _allgather_small(vec, after):
    def body(v_ref, _, o_ref, send_sems, recv_sems, loc_sem):
        x, y, c = lax.axis_index("x"), lax.axis_index("y"), lax.axis_index("c")
        me = 4 * x + 2 * y + c

        def peer(k):
            flip = lambda v, bit: 1 - v if (k >> bit) & 1 else v
            return flip(x, 2), flip(y, 1), flip(c, 0)

        loc = pltpu.make_async_copy(v_ref, o_ref.at[me], loc_sem)
        loc.start()
        sends = []
        for k in range(1, N_DEV):
            cp = pltpu.make_async_remote_copy(src_ref=v_ref, dst_ref=o_ref.at[me], send_sem=send_sems.at[k - 1],
                                              recv_sem=recv_sems.at[k - 1], device_id=peer(k), device_id_type=MESH)
            cp.start()
            sends.append(cp)
        for k in range(1, N_DEV):
            px, py, pc = peer(k)
            pltpu.make_async_remote_copy(src_ref=v_ref, dst_ref=o_ref.at[4 * px + 2 * py + pc],
                                         send_sem=send_sems.at[k - 1], recv_sem=recv_sems.at[k - 1],
                                         device_id=(px, py, pc), device_id_type=MESH).wait_recv()
        for cp in sends:
            cp.wait_send()
        loc.wait()

    return pl.pallas_call(
        body, in_specs=[HBM_SPEC, HBM_SPEC], out_specs=HBM_SPEC,
        out_shape=jax.ShapeDtypeStruct((N_DEV,) + vec.shape, vec.dtype),
        scratch_shapes=[pltpu.SemaphoreType.DMA((N_DEV - 1,)), pltpu.SemaphoreType.DMA((N_DEV - 1,)),
                        pltpu.SemaphoreType.DMA],
        name="allgather_small")(vec, after)


def _adamw(w, g, m, v):
    m = ADAM_B1 * m + (1.0 - ADAM_B1) * g
    v = ADAM_B2 * v + (1.0 - ADAM_B2) * (g * g)
    m_hat = m / (1.0 - ADAM_B1 ** ADAM_STEP)
    v_hat = v / (1.0 - ADAM_B2 ** ADAM_STEP)
    delta = -ADAM_LR * (m_hat / (jnp.sqrt(v_hat) + ADAM_EPS) + ADAM_WD * w)
    return delta, m, v


def _reduce_adamw(parts, w, m, v, *, tr, name):
    R, C = w.shape

    def body(p_ref, w_ref, m_ref, v_ref, g_ref, d_ref, nm_ref, nv_ref):
        def core_sum(cc):
            s = p_ref[cc, 0].astype(F32)
            for k in range(1, N_CHIPS):
                s = s + p_ref[cc, k].astype(F32)
            return s

        g = core_sum(0) + core_sum(1)
        delta, nm, nv = _adamw(w_ref[...], g, m_ref[...], v_ref[...])
        g_ref[...] = g
        d_ref[...] = delta
        nm_ref[...] = nm
        nv_ref[...] = nv

    blk = pl.BlockSpec((tr, C), lambda i: (i, 0))
    return pl.pallas_call(
        body, grid=(R // tr,),
        in_specs=[pl.BlockSpec((N_CORES, N_CHIPS, tr, C), lambda i: (0, 0, i, 0)), blk, blk, blk],
        out_specs=[blk] * 4, out_shape=[jax.ShapeDtypeStruct((R, C), F32)] * 4,
        name=name, compiler_params=_cparams(("parallel",), 40))(parts, w, m, v)


def _reduce_adamw_vectors(allv, offs, ws, ms, vs):
    n = len(ws)

    def body(a_ref, *refs):
        w_refs, m_refs, v_refs = refs[:n], refs[n:2 * n], refs[2 * n:3 * n]
        tot_ref, outs = refs[3 * n], refs[3 * n + 1:]
        s = a_ref[0]
        for d in range(1, N_DEV):
            s = s + a_ref[d]
        tot_ref[...] = s
        for k in range(n):
            g = s[:, offs[k]:offs[k] + w_refs[k].shape[1]]
            delta, nm, nv = _adamw(w_refs[k][...], g, m_refs[k][...], v_refs[k][...])
            for ref, val in zip(outs[4 * k:4 * k + 4], (g, delta, nm, nv)):
                ref[...] = val

    out_shape = [jax.ShapeDtypeStruct(allv.shape[1:], F32)]
    for w in ws:
        out_shape += [jax.ShapeDtypeStruct(w.shape, F32)] * 4
    res = pl.pallas_call(body, out_shape=out_shape, name="reduce_adamw_vectors",
                         compiler_params=_cparams((), 40))(allv, *ws, *ms, *vs)
    return res[0], [tuple(res[1 + 4 * k:5 + 4 * k]) for k in range(n)]


def _adamw_taps(ws, gs, ms, vs):
    n = len(ws)

    def body(*refs):
        outs = refs[4 * n:]
        for k in range(n):
            res = _adamw(refs[k][...], refs[n + k][...], refs[2 * n + k][...], refs[3 * n + k][...])
            for ref, val in zip(outs[3 * k:3 * k + 3], res):
                ref[...] = val

    out_shape = []
    for w in ws:
        out_shape += [jax.ShapeDtypeStruct(w.shape, F32)] * 3
    res = pl.pallas_call(body, out_shape=out_shape, name="adamw_taps")(*ws, *gs, *ms, *vs)
    return [tuple(res[3 * k:3 * k + 3]) for k in range(n)]


def _pack(pieces):
    flat, offs, n = [], [], 0
    for p in pieces:
        size = -(-p.size // LANES) * LANES
        flat.append(jnp.pad(p.reshape(-1), (0, size - p.size)))
        offs.append(n)
        n += size
    return jnp.concatenate(flat).reshape(1, n), offs


def _local_step(x, target, p, wfull, on_ready=lambda group: None, before_ln0=()):
    S = x.shape[0]
    dils = [d for _, d in GROUPS]

    h0, h0b, *h0_res = _ln0_fwd(x, p["ln0_g"], p["ln0_b"], before_ln0)
    h0_rows = [h0b] + [h.reshape(S, D_MODEL) for h in h0_res]

    if isinstance(wfull, dict):
        w_in3, pending = wfull["w_in"], None
    else:
        w_in3, launch_rest, assemble = wfull
        w_in3, h0b = lax.optimization_barrier((w_in3, h0b))
        pending = launch_rest(h0b)

    w_blocks = w_in3.transpose(1, 0, 2).reshape(D_MODEL, N_BLK, GROUP_W)
    w_perm = jnp.concatenate([w_blocks[:, b] for b in PERM], axis=1)
    b_blocks = p["b_in"].reshape(N_BLK, GROUP_W)
    b_perm = jnp.concatenate([b_blocks[b] for b in PERM]).reshape(1, N_IN)
    w_nat, b_nat = w_perm[:, :N_NAT], b_perm[:, :N_NAT]
    qkv_cols = [slice(P_Q0 + g * QKV_W, P_Q0 + (g + 1) * QKV_W) for g in range(N_GROUPS)]
    w_qkv = [w_perm[:, c] for c in qkv_cols]

    proj = _mm_nn(h0b, w_nat, b_nat, tm=512, tn=N_NAT // 2, out_dtype=BF16, name="proj")
    qkv = [proj[None]]
    for g in range(1, N_GROUPS):
        t = _mm_nn(h0_rows[g], w_qkv[g], b_perm[:, qkv_cols[g]], tm=512, tn=QKV_W, out_dtype=BF16, name=f"proj_qkv{g}")
        qkv.append(t.reshape(dils[g], S // dils[g], QKV_W))
    if pending is not None:
        pending, qkv = lax.optimization_barrier((pending, qkv))
        proj = qkv[0][0]
        wfull = assemble(pending)
    w_up3 = wfull["w_up"]
    w_a, w_o, w_down, w_b = wfull["w_a"], wfull["w_o"], wfull["w_down"], wfull["w_b"]
    conv_w, ffn_conv_w = wfull["conv_w"], wfull["ffn_conv_w"]
    col0 = [P_Q0 // GROUP_W] + [0] * (N_GROUPS - 1)
    ya_in = _conv_gate_fwd(proj, conv_w)
    att = [_attn_fwd(qkv[g], col0[g], g) for g in range(N_GROUPS)]
    comb, comb_b, lse_tot = _attn_combine([a[0] for a in att], [a[1] for a in att])
    yab, mixin = _branch_mix(ya_in, comb_b, w_a, w_b, proj)
    xhat1, rstd1, h1b = _mix_ln1(mixin, w_o, p["b_o"], h0, p["ln1_g"], p["ln1_b"])
    up = _mm_nn(h1b, w_up3, p["b_up"], tm=512, tn=w_up3.shape[2], out_dtype=BF16, name="up")
    f = _ffn_conv_fwd(up, ffn_conv_w, p["ffn_conv_b"])
    dz2, dz2b, st2 = _down_ln2_loss(f, w_down, p["b_down"], xhat1, p["ln1_g"], p["ln1_b"],
                                    p["ln2_g"], p["ln2_b"], target)

    gw = {}
    gw["w_down"] = _mm_tn(f, dz2b, n_out=1, tn=D_MODEL, ts=1024, g_block=(1024, D_MODEL),
                          g_map=lambda j, s: (s, 0), name="grad_w_down").reshape(N_CHIPS, D_FF // N_CHIPS, D_MODEL)
    df = _mm_nt(dz2b, w_down, tm=512, name="df")
    dup, sm_ffn = _ffn_conv_bwd(up, df, ffn_conv_w, p["ffn_conv_b"])
    up_tn = w_up3.shape[2]
    up_pp = D_FF // up_tn
    gw["w_up"] = _mm_tn(h1b, dup, n_out=N_CHIPS, tn=up_tn, ts=1024, g_block=(None, 1024, up_tn),
                        g_map=lambda j, s: (j // up_pp, s, j % up_pp), name="grad_w_up")
    exchanged = on_ready({n: gw[n] for n in ("w_down", "w_up")}) or {}
    dz1, dz1b, st1 = _up_bwd_ln1(dup, w_up3, dz2, xhat1, rstd1, p["ln1_g"])

    gw["w_o"] = _mm_tn(mixin, dz1b, n_out=1, tn=D_MODEL, ts=512, g_block=(512, D_MODEL),
                       g_map=lambda j, s: (s, 0), name="grad_w_o").reshape(N_CHIPS, D_MODEL // N_CHIPS, D_MODEL)
    dyab, dgab = _mix_bwd(dz1b, w_o, proj, yab)
    gw["w_a"] =_mm_tn(ya_in, dyab, n_out=1, tn=D_MODEL, ts=512, g_block=(512, D_MODEL),
                       g_map=lambda j, s: (s, 0), name="grad_w_a").reshape(N_CHIPS, D_CONV // N_CHIPS, D_MODEL)
    gw_b = _mm_tn(comb_b, dyab, n_out=1, tn=D_MODEL, ts=1024, g_block=(1024, D_MODEL),
                  g_map=lambda j, s: (s, 1), name="grad_w_b")
    gw["w_b"] = gw_b.reshape(GROUP_W, N_CHIPS, D_MODEL // N_CHIPS).transpose(1, 0, 2)
    exchanged_mix = on_ready({n: gw[n] for n in ("w_o", "w_a", "w_b")}) or {}
    dya_in = _mm_nt(dyab, w_a, tm=512, a_col=0, name="dya_in")
    exchanged, dya_in = lax.optimization_barrier((exchanged, dya_in))
    dbch, sm_conv = _conv_gate_bwd(proj, dya_in, conv_w)
    att_stats = _comb_bwd(dyab, w_b, comb, lse_tot)
    exchanged_mix, att_stats = lax.optimization_barrier((exchanged_mix, att_stats))
    exchanged.update(exchanged_mix)
    dqkv = [_attn_bwd(qkv[g], col0[g], g, *att_stats[g]) for g in range(N_GROUPS)]

    w_pieces, b_pieces = [], []
    for nm, planes in (("bch", dbch), ("gab", dgab)):
        pw, pc = _mm_tn(h0b, planes, n_out=planes.shape[0], tn=D_MODEL, ts=1024, g_block=(None, 1024, D_MODEL),
                        g_map=lambda j, s: (j, s, 0), colsum=True, name="grad_w_in_" + nm)
        w_pieces.append(pw.transpose(1, 0, 2).reshape(D_MODEL, planes.shape[0] * D_MODEL))
        b_pieces.append(pc[0])
    for g in range(N_GROUPS):
        pw, pc = _mm_tn_cat(h0_rows[g], [a.reshape(S, GROUP_W) for a in dqkv[g]], ts=1024, name=f"grad_w_in_qkv{g}")
        w_pieces.append(pw)
        b_pieces.append(pc[0])
    dw_blocks = jnp.concatenate(w_pieces, axis=1).reshape(D_MODEL, N_BLK, GROUP_W)
    dw_ref = jnp.concatenate([dw_blocks[:, b] for b in INV_PERM], axis=1)
    gw["w_in"] = dw_ref.reshape(D_MODEL, N_CHIPS, N_IN // N_CHIPS).transpose(1, 0, 2)
    exchanged.update(on_ready({"w_in": gw["w_in"]}) or {})
    db_blocks = jnp.concatenate(b_pieces).reshape(N_BLK, GROUP_W)
    grad_b_in = jnp.concatenate([db_blocks[b] for b in INV_PERM])

    grad_x, st0 = _in_bwd_ln0([dbch, dgab], dqkv, w_nat, w_qkv[1:], dz1, x, p["ln0_g"])

    small = {
        "loss": st2[2:3, 0:1],
        "ln0_g": st0[0], "ln0_b": st0[1], "b_in": grad_b_in, "conv_w": sm_conv[0:3],
        "b_o": st1[2], "ln1_g": st1[0], "ln1_b": st1[1],
        "b_up": jnp.concatenate([sm_ffn[0], sm_ffn[1]]), "ffn_conv_w": sm_ffn[3:6], "ffn_conv_b": sm_ffn[2],
        "b_down": st2[3], "ln2_g": st2[0], "ln2_b": st2[1],
    }
    return grad_x, exchanged or gw, small


BIG =("w_in", "w_a", "w_b", "w_o", "w_up", "w_down")
CONV = ("conv_w", "ffn_conv_w")
VECS = ("ln0_g", "ln0_b", "b_in", "b_o", "ln1_g", "ln1_b", "b_up", "ffn_conv_b", "b_down", "ln2_g", "ln2_b")
ORDER = ("ln0_g", "ln0_b", "w_in", "b_in", "conv_w", "w_a", "w_b", "w_o", "b_o", "ln1_g", "ln1_b", "w_up", "b_up",
         "ffn_conv_w", "ffn_conv_b", "w_down", "b_down", "ln2_g", "ln2_b")
SMALL_ORDER = ("loss",) + VECS + CONV


def _step(x, target, W, Mo, Vo):
    x2, t2 = x[0], target[0]
    big2 = {n: W[n][0] for n in BIG}
    halves = lambda a: a.astype(BF16).reshape(N_CORES, a.shape[0] // N_CORES, a.shape[1])
    whole = lambda g: g.reshape(N_CHIPS, g.shape[1] * g.shape[2], g.shape[3])
    later = tuple(n for n in BIG if n != "w_in")
    w_in_halves = halves(big2["w_in"])
    first = _allgather_shards([w_in_halves], [], name="allgather_w_in", collective_id=1)

    def launch_rest(h0b):
        return _allgather_shards([halves(big2[n]) for n in later] + [W[n] for n in CONV], [h0b],
                                 name="allgather_rest", collective_id=2)

    def assemble(rest):
        gathered = {n: whole(g) for n, g in zip(later + CONV, rest)}
        return {
            "w_up": gathered["w_up"],
            "w_a": gathered["w_a"].reshape(D_CONV, D_MODEL), "w_o": gathered["w_o"].reshape(D_MODEL, D_MODEL),
            "w_down": gathered["w_down"].reshape(D_FF, D_MODEL),
            "w_b": gathered["w_b"].transpose(1, 0, 2).reshape(GROUP_W, D_MODEL),
            "conv_w": gathered["conv_w"].transpose(1, 0, 2).reshape(3, D_CONV),
            "ffn_conv_w": gathered["ffn_conv_w"].transpose(1, 0, 2).reshape(3, D_FF),
        }

    pvec = {n: W[n].reshape(1, -1) for n in VECS}

    exchange_ids = iter((3, 4, 5))

    def exchange(group):
        names = tuple(group)
        res = _exchange_grads([group[n] for n in names], name="exchange_" + "_".join(names),
                              collective_id=next(exchange_ids))
        return dict(zip(names, res))

    grad_x, parts, small = _local_step(x2, t2, pvec, (whole(first[0]), launch_rest, assemble), exchange,
                                       before_ln0=[w_in_halves])
    out = {}
    for n in BIG:
        tr = {"w_in": 128, "w_up": 128, "w_b": 128}.get(n, big2[n].shape[0] // 4)
        g, d, nm, nv = _reduce_adamw(parts[n], big2[n], Mo[n][0], Vo[n][0], tr=tr, name="adamw_" + n)
        out[n] = tuple(a[None] for a in (g, d, nm, nv))

    vec, offs = _pack([small[n] for n in SMALL_ORDER])
    off = dict(zip(SMALL_ORDER, offs))
    row = lambda a: a.reshape(1, -1)
    allv = _allgather_small(vec, parts["w_in"])
    tot, vec_out = _reduce_adamw_vectors(allv, [off[n] for n in VECS], [row(W[n]) for n in VECS],
                                         [row(Mo[n]) for n in VECS], [row(Vo[n]) for n in VECS])
    for n, res in zip(VECS, vec_out):
        out[n] = tuple(a.reshape(W[n].shape) for a in res)
    loss = tot[0, off["loss"]]
    chip = 2 * lax.axis_index("x") + lax.axis_index("y")
    taps_g = []
    for n in CONV:
        width = W[n].shape[2]
        full = lax.slice(tot, (0, off[n]), (1, off[n] + 3 * N_CHIPS * width)).reshape(3, N_CHIPS * width)
        taps_g.append(lax.dynamic_slice_in_dim(full, chip * width, width, axis=1))
    taps_out = _adamw_taps([W[n][0] for n in CONV], taps_g, [Mo[n][0] for n in CONV], [Vo[n][0] for n in CONV])
    for n, g, res in zip(CONV, taps_g, taps_out):
        out[n] = tuple(a[None] for a in (g,) + res)

    res = [loss, grad_x[None]]
    for k in range(4):
        res += [out[n][k] for n in ORDER]
    return tuple(res)


def kernel(x, ln0_g, ln0_b, w_in, b_in, conv_w, w_a, w_b, w_o, b_o, ln1_g, ln1_b, w_up, b_up, ffn_conv_w, ffn_conv_b, w_down, b_down, ln2_g, ln2_b, loss_target, m_ln0_g, m_ln0_b, m_w_in, m_b_in, m_conv_w, m_w_a, m_w_b, m_w_o, m_b_o, m_ln1_g, m_ln1_b, m_w_up, m_b_up, m_ffn_conv_w, m_ffn_conv_b, m_w_down, m_b_down, m_ln2_g, m_ln2_b, v_ln0_g, v_ln0_b, v_w_in, v_b_in, v_conv_w, v_w_a, v_w_b, v_w_o, v_b_o, v_ln1_g, v_ln1_b, v_w_up, v_b_up, v_ffn_conv_w, v_ffn_conv_b, v_w_down, v_b_down, v_ln2_g, v_ln2_b):
    W = dict(zip(ORDER, (ln0_g, ln0_b, w_in, b_in, conv_w, w_a, w_b, w_o, b_o, ln1_g, ln1_b, w_up, b_up,
                         ffn_conv_w, ffn_conv_b, w_down, b_down, ln2_g, ln2_b)))
    Mo = dict(zip(ORDER, (m_ln0_g, m_ln0_b, m_w_in, m_b_in, m_conv_w, m_w_a, m_w_b, m_w_o, m_b_o, m_ln1_g, m_ln1_b,
                          m_w_up, m_b_up, m_ffn_conv_w, m_ffn_conv_b, m_w_down, m_b_down, m_ln2_g, m_ln2_b)))
    Vo = dict(zip(ORDER, (v_ln0_g, v_ln0_b, v_w_in, v_b_in, v_conv_w, v_w_a, v_w_b, v_w_o, v_b_o, v_ln1_g, v_ln1_b,
                          v_w_up, v_b_up, v_ffn_conv_w, v_ffn_conv_b, v_w_down, v_b_down, v_ln2_g, v_ln2_b)))
    return _step(x, loss_target, W, Mo, Vo)
```

```python
import functools
import math

import jax
import jax.numpy as jnp
from jax import lax
from jax.experimental import pallas as pl
from jax.experimental.pallas import tpu as pltpu
from jax.experimental.pallas import tpu_sc as plsc

F32 = jnp.float32
BF16 = jnp.bfloat16

D_MODEL = 1024
D_CONV = D_MODEL
HEAD_DIM = 64
HEADS_PER_GROUP = 8
GROUPS = ((128, 1), (512, 4), (2048, 16))
N_GROUPS = len(GROUPS)
GROUP_W = HEADS_PER_GROUP * HEAD_DIM
QKV_W = N_GROUPS * GROUP_W
RADIUS = 64
D_FF = 2816
LN_EPS = 1e-5
ALPHA = 2.0 ** 0.25
MASK_VALUE = -1e30
ATT_SCALE = HEAD_DIM ** -0.5
OFF_B = 0
OFF_C = OFF_B + D_CONV
OFF_H = OFF_C + D_CONV
OFF_Q = OFF_H + D_CONV
OFF_K = OFF_Q + QKV_W
OFF_V = OFF_K + QKV_W
OFF_GA = OFF_V + QKV_W
OFF_GB = OFF_GA + D_MODEL
N_IN = OFF_GB + D_MODEL
ADAM_LR = 0.001
ADAM_B1 = 0.9
ADAM_B2 = 0.999
ADAM_EPS = 1e-08
ADAM_WD = 0.01
ADAM_STEP = 10
INV_SQRT2 = 0.7071067811865476
INV_SQRT_2PI = 0.3989422804014327

LANES = 128
SUBLANES = 8
VMEM_BYTES_V7X = 64 * 1024 * 1024
N_CHIPS = 4
N_CORES = 2
N_DEV = N_CHIPS * N_CORES
MESH = pl.DeviceIdType.MESH

N_BLK = N_IN // GROUP_W
PERM = (0, 1, 2, 3, 4, 5, 15, 16, 17, 18, 6, 9, 12, 7, 10, 13, 8, 11, 14)
INV_PERM = tuple(PERM.index(b) for b in range(N_BLK))
P_B, P_C, P_H, P_GA, P_GB, P_Q0 = 0, 1024, 2048, 3072, 4096, 5120
N_NAT = P_Q0 + QKV_W // N_GROUPS * 3
N_GATED = P_Q0

SLAB = 128
CHUNK = 256
PAD = SUBLANES
TQ = 128


def _cparams(sem, vmem_mb):
    assert vmem_mb * 1024 * 1024 < VMEM_BYTES_V7X
    return pltpu.CompilerParams(dimension_semantics=sem, vmem_limit_bytes=vmem_mb * 1024 * 1024)


def _resident(shape):
    nd = len(shape)
    return pl.BlockSpec(shape, lambda *_: (0,) * nd, pipeline_mode=pl.Buffered(1))


def _hbm(*arrays):
    return [pltpu.with_memory_space_constraint(a, pltpu.HBM) for a in arrays]


def _dot(a, b):
    return jnp.dot(a, b, preferred_element_type=F32)


def _dot_nt(a, b):
    return lax.dot_general(a, b, (((1,), (1,)), ((), ())), preferred_element_type=F32)


def _dot_tn(a, b):
    return lax.dot_general(a, b, (((0,), (0,)), ((), ())), preferred_element_type=F32)


def _ln_stats(z):
    mu = jnp.mean(z, -1, keepdims=True)
    zc = z - mu
    var = jnp.mean(zc * zc, -1, keepdims=True)
    rstd = lax.rsqrt(var + LN_EPS)
    return zc * rstd, rstd


def _ln_bwd(dh, xhat, rstd, g):
    dxh = dh * g
    m1 = jnp.mean(dxh, -1, keepdims=True)
    m2 = jnp.mean(dxh * xhat, -1, keepdims=True)
    return rstd * (dxh - m1 - xhat * m2)


def _rows8(rows, width):
    pad = [jnp.zeros((1, width), F32)] * (SUBLANES - len(rows))
    return jnp.concatenate(list(rows) + pad, axis=0)


def _mm_nn(a, w, bias, *, tm, tn, out_dtype, name, vmem_mb=40):
    M, K = a.shape
    if w.ndim == 3:
        assert w.shape[2] == tn
        n_tiles = w.shape[0]
        w_spec = pl.BlockSpec((None, K, tn), lambda j, i: (j, 0, 0))
    else:
        n_tiles = w.shape[1] // tn
        w_spec = pl.BlockSpec((K, tn), lambda j, i: (0, j))

    def body(a_ref, w_ref, b_ref, o_ref):
        o_ref[...] = (_dot(a_ref[...], w_ref[...]) + b_ref[...]).astype(o_ref.dtype)

    return pl.pallas_call(
        body, grid=(n_tiles, M // tm),
        in_specs=[pl.BlockSpec((tm, K), lambda j, i: (i, 0)), w_spec, pl.BlockSpec((1, tn), lambda j, i: (0, j))],
        out_specs=pl.BlockSpec((tm, tn), lambda j, i: (i, j)),
        out_shape=jax.ShapeDtypeStruct((M, n_tiles * tn), out_dtype),
        name=name, compiler_params=_cparams(("arbitrary", "parallel"), vmem_mb))(*_hbm(a, w, bias))


def _mm_nt(a, w, *, tm, a_col=0, name, vmem_mb=40):
    M = a.shape[0]
    N, K = w.shape

    def body(a_ref, w_ref, o_ref):
        o_ref[...] = _dot_nt(a_ref[...], w_ref[...]).astype(o_ref.dtype)

    return pl.pallas_call(
        body, grid=(M // tm,),
        in_specs=[pl.BlockSpec((tm, K), lambda i: (i, a_col)),
                  pl.BlockSpec((N, K), lambda i: (0, 0))],
        out_specs=pl.BlockSpec((tm, N), lambda i: (i, 0)),
        out_shape=jax.ShapeDtypeStruct((M, N), BF16),
        name=name, compiler_params=_cparams(("parallel",), vmem_mb))(*_hbm(a, w))


def _mm_tn(a, g, *, n_out, tn, ts, g_block, g_map, colsum=False, name, vmem_mb=48):
    S, K = a.shape
    n_s = S // ts

    def body(a_ref, g_ref, *rest):
        if colsum:
            o_ref, cs_ref, acc_ref, cacc_ref = rest
        else:
            o_ref, acc_ref = rest
        s = pl.program_id(1)

        @pl.when(s == 0)
        def _():
            acc_ref[...] = jnp.zeros_like(acc_ref)
            if colsum:
                cacc_ref[...] = jnp.zeros_like(cacc_ref)

        gv = g_ref[...]
        acc_ref[...] += _dot_tn(a_ref[...], gv)
        if colsum:
            cacc_ref[...] += jnp.broadcast_to(jnp.sum(gv.astype(F32), axis=0, keepdims=True), cacc_ref.shape)

        @pl.when(s == n_s - 1)
        def _():
            o_ref[...] = acc_ref[...].astype(o_ref.dtype)
            if colsum:
                cs_ref[...] = cacc_ref[...]

    out_specs = [pl.BlockSpec((None, K, tn), lambda j, s: (j, 0, 0))]
    out_shape = [jax.ShapeDtypeStruct((n_out, K, tn), BF16)]
    scratch = [pltpu.VMEM((K, tn), F32)]
    if colsum:
        out_specs.append(pl.BlockSpec((SUBLANES, tn), lambda j, s: (0, j)))
        out_shape.append(jax.ShapeDtypeStruct((SUBLANES, n_out * tn), F32))
        scratch.append(pltpu.VMEM((SUBLANES, tn), F32))
    res = pl.pallas_call(
        body, grid=(n_out, n_s),
        in_specs=[pl.BlockSpec((ts, K), lambda j, s: (s, 0)), pl.BlockSpec(g_block, g_map)],
        out_specs=out_specs, out_shape=out_shape, scratch_shapes=scratch,
        name=name, compiler_params=_cparams(("parallel", "arbitrary"), vmem_mb))(*_hbm(a, g))
    return res if colsum else res[0]


def _mm_tn_cat(a, gs, *, ts, name, vmem_mb=40):
    S, K = a.shape
    widths = [g.shape[1] for g in gs]
    n_s, total = S // ts, sum(widths)

    def body(*refs):
        a_ref, g_refs = refs[0], refs[1:1 + len(gs)]
        o_ref, cs_ref, acc_ref, cacc_ref = refs[1 + len(gs):]
        s = pl.program_id(0)

        @pl.when(s == 0)
        def _():
            acc_ref[...] = jnp.zeros_like(acc_ref)
            cacc_ref[...] = jnp.zeros_like(cacc_ref)

        av, col = a_ref[...], 0
        for g_ref, w in zip(g_refs, widths):
            gv = g_ref[...]
            acc_ref[:, col:col + w] += _dot_tn(av, gv)
            cacc_ref[:, col:col + w] += jnp.broadcast_to(jnp.sum(gv.astype(F32), axis=0, keepdims=True), (SUBLANES, w))
            col += w

        @pl.when(s == n_s - 1)
        def _():
            o_ref[...] = acc_ref[...].astype(BF16)
            cs_ref[...] = cacc_ref[...]

    return pl.pallas_call(
        body, grid=(n_s,),
        in_specs=[pl.BlockSpec((ts, K), lambda s: (s, 0))] + [pl.BlockSpec((ts, w), lambda s: (s, 0)) for w in widths],
        out_specs=[pl.BlockSpec((K, total), lambda s: (0, 0)), pl.BlockSpec((SUBLANES, total), lambda s: (0, 0))],
        out_shape=[jax.ShapeDtypeStruct((K, total), BF16), jax.ShapeDtypeStruct((SUBLANES, total), F32)],
        scratch_shapes=[pltpu.VMEM((K, total), F32), pltpu.VMEM((SUBLANES, total), F32)],
        name=name, compiler_params=_cparams(("arbitrary",), vmem_mb))(*_hbm(a, *gs))


DILS = tuple(d for _, d in GROUPS if d > 1)


def _res_spec(d, tm, width):
    return pl.BlockSpec((d, tm // d, width), lambda i: (0, i, 0))


def _lane_scratch(tm, width):
    return [pltpu.VMEM((tm, LANES), F32)] * (width // LANES)


def _to_residue(val, dst_refs, dils, tm, dtype, scr):
    for c, ref in enumerate(scr):
        ref[...] = val[:, c * LANES:(c + 1) * LANES]
    for dst_ref, d in zip(dst_refs, dils):
        for r in range(d):
            cols = [ref[pl.ds(r, tm // d, stride=d), :] for ref in scr]
            dst_ref[r] = jnp.concatenate(cols, axis=1).astype(dtype)


def _from_residue(rows_of, d, tm, scr):
    for r in range(d):
        v = rows_of(r).astype(F32)
        for c, ref in enumerate(scr):
            ref[pl.ds(r, tm // d, stride=d), :] = v[:, c * LANES:(c + 1) * LANES]
    return jnp.concatenate([ref[...] for ref in scr], axis=1)


def _ln0_fwd(x, g, b, after=(), *, tm=512):
    S, Dm = x.shape
    n_after = len(after)

    def body(x_ref, g_ref, b_ref, *rest):
        h_ref, hb_ref, *rest = rest[n_after:]
        xhat, _ = _ln_stats(x_ref[...])
        h = xhat * g_ref[...] + b_ref[...]
        h_ref[...] = h
        hb_ref[...] = h.astype(BF16)
        _to_residue(h, rest[:len(DILS)], DILS, tm, BF16, rest[len(DILS):])

    row = pl.BlockSpec((tm, Dm), lambda i: (i, 0))
    vec = pl.BlockSpec((1, Dm), lambda i: (0, 0))
    return pl.pallas_call(
        body, grid=(S // tm,), in_specs=[row, vec, vec] + [pl.BlockSpec(memory_space=pl.ANY)] * n_after,
        out_specs=[row, row] + [_res_spec(d, tm, Dm) for d in DILS],
        out_shape=[jax.ShapeDtypeStruct((S, Dm), F32), jax.ShapeDtypeStruct((S, Dm), BF16)]
        + [jax.ShapeDtypeStruct((d, S // d, Dm), BF16) for d in DILS],
        scratch_shapes=_lane_scratch(tm, Dm),
        name="ln0_fwd", compiler_params=_cparams(("parallel",), 32))(*_hbm(x, g, b), *after)


def _slab_spec(S, col0):
    return pl.BlockSpec((S, SLAB), lambda j: (0, col0 // SLAB + j))


def _zero_pads(scr, S):
    scr[0:PAD, :] = jnp.zeros((PAD, SLAB), F32)
    scr[S + PAD:S + 2 * PAD, :] = jnp.zeros((PAD, SLAB), F32)


def _shifted(scr, t):
    return (scr[PAD - 1 + t:PAD - 1 + t + CHUNK, :], scr[PAD + t:PAD + t + CHUNK, :],
            scr[PAD + 1 + t:PAD + 1 + t + CHUNK, :])


def _conv_gate_fwd(proj, conv_w):
    S = proj.shape[0]

    def body(b_ref, c_ref, h_ref, w_ref, o_ref, u_scr):
        _zero_pads(u_scr, S)
        for t in range(0, S, CHUNK):
            u_scr[PAD + t:PAD + t + CHUNK, :] = c_ref[t:t + CHUNK, :].astype(F32) * h_ref[t:t + CHUNK, :].astype(F32)
        w0, w1, w2 = w_ref[0:1, :], w_ref[1:2, :], w_ref[2:3, :]
        for t in range(0, S, CHUNK):
            um, u0, up = _shifted(u_scr, t)
            cv = w0 * um + w1 * u0 + w2 * up
            o_ref[t:t + CHUNK, :] = (b_ref[t:t + CHUNK, :].astype(F32) * cv).astype(BF16)

    return pl.pallas_call(
        body, grid=(D_CONV // SLAB,),
        in_specs=[_slab_spec(S, P_B), _slab_spec(S, P_C), _slab_spec(S, P_H),
                  pl.BlockSpec((3, SLAB), lambda j: (0, j))],
        out_specs=pl.BlockSpec((S, SLAB), lambda j: (0, j)),
        out_shape=jax.ShapeDtypeStruct((S, D_CONV), BF16),
        scratch_shapes=[pltpu.VMEM((S + 2 * PAD, SLAB), F32)],
        name="conv_gate_fwd", compiler_params=_cparams(("parallel",), 40))(*_hbm(proj, proj, proj, conv_w))


MASKED_DISTANCE = -1e34


def _attn_bias_table(g):
    dil = GROUPS[g][1]
    j = lax.broadcasted_iota(jnp.int32, (2 * TQ, TQ), 0)
    a = lax.broadcasted_iota(jnp.int32, (2 * TQ, TQ), 1)
    rel = jnp.abs(j - RADIUS - a)
    base = -(rel * dil).astype(F32)
    inside, after_start, before_end = rel <= RADIUS, j >= RADIUS, j < TQ + RADIUS
    variants = []
    for first, last in ((False, False), (True, False), (False, True), (True, True)):
        valid = inside & (after_start if first else True) & (before_end if last else True)
        variants.append(jnp.where(valid, base, MASKED_DISTANCE))
    return jnp.stack(variants)


def _bias_spec(nb):
    def variant(r, i):
        return (jnp.where(i == 0, 1, 0) + jnp.where(i == nb - 1, 2, 0), 0, 0)
    return pl.BlockSpec((None, 2 * TQ, TQ), variant)


def _head_stats(rows):
    pad = jnp.zeros((LANES - len(rows), TQ), F32)
    return jnp.concatenate(list(rows) + [pad], axis=0).T


def _slope(g, h):
    return 2.0 ** (-8.0 * (g * HEADS_PER_GROUP + h + 1) / (N_GROUPS * HEADS_PER_GROUP))


def _window(p_ref, c_ref, n_ref):
    return jnp.concatenate([p_ref[TQ - RADIUS:, :], c_ref[...], n_ref[:RADIUS, :]], axis=0)


def _pair(a, h):
    return a[:, (h // 2) * LANES:(h // 2 + 1) * LANES]


def _own_lanes(a, h):
    lane = lax.broadcasted_iota(jnp.int32, a.shape, 1)
    return jnp.where((lane >= HEAD_DIM) == (h % 2 == 1), a, jnp.zeros_like(a))


def _own_rows(a, h):
    return a[(h % 2) * HEAD_DIM:(h % 2 + 1) * HEAD_DIM, :]


def _qkv_specs(nb, col0):
    def spec(col, shift):
        return pl.BlockSpec((None, TQ, GROUP_W), lambda r, i: (r, jnp.clip(i + shift, 0, nb - 1), col))

    return [spec(col0, 0), spec(col0 + 1, -1), spec(col0 + 1, 0), spec(col0 + 1, 1),
            spec(col0 + 2, -1), spec(col0 + 2, 0), spec(col0 + 2, 1)]


def _attn_fwd(qkv, col0, g):
    dil, sub, _ = qkv.shape
    nb = sub // TQ

    def body(q_ref, kp, kc, kn, vp, vc, vn, bias_ref, o_ref, lse_ref, ot_scr, s_scr, p_scr):
        kwin = _window(kp, kc, kn)
        vwin = _window(vp, vc, vn)
        q = q_ref[...] * ATT_SCALE
        for h in range(HEADS_PER_GROUP):
            s_scr[h] = _dot_nt(_pair(kwin, h), _own_lanes(_pair(q, h), h))
        lse, inv_den = [], []
        for h in range(HEADS_PER_GROUP):
            s = s_scr[h] + _slope(g, h) * bias_ref[...]
            m = jnp.max(s, axis=0, keepdims=True)
            p = jnp.exp(s - m)
            den = jnp.sum(p, axis=0, keepdims=True)
            p_scr[h] = p.astype(BF16)
            inv_den.append(1.0 / den)
            lse.append(m + jnp.log(den))
        for h in range(HEADS_PER_GROUP):
            ot = _dot_tn(_pair(vwin, h), p_scr[h])
            ot_scr[h * HEAD_DIM:(h + 1) * HEAD_DIM, :] = _own_rows(ot, h) * inv_den[h]
        o_ref[...] = ot_scr[...].T
        lse_ref[...] = _head_stats(lse)

    return pl.pallas_call(
        body, grid=(dil, nb), in_specs=_qkv_specs(nb, col0) + [_bias_spec(nb)],
        out_specs=[pl.BlockSpec((None, TQ, GROUP_W), lambda r, i: (r, i, 0)),
                   pl.BlockSpec((None, TQ, LANES), lambda r, i: (r, i, 0))],
        out_shape=[jax.ShapeDtypeStruct((dil, sub, GROUP_W), F32), jax.ShapeDtypeStruct((dil, sub, LANES), F32)],
        scratch_shapes=[pltpu.VMEM((GROUP_W, TQ), F32), pltpu.VMEM((HEADS_PER_GROUP, 2 * TQ, TQ), F32),
                        pltpu.VMEM((HEADS_PER_GROUP, 2 * TQ, TQ), BF16)],
        name=f"attn_fwd_g{g}", compiler_params=_cparams(("parallel", "arbitrary"), 32))(
            *_hbm(*([qkv] * 7), _attn_bias_table(g)))


def _expand_heads():
    h = lax.broadcasted_iota(jnp.int32, (LANES, GROUP_W), 0)
    c = lax.broadcasted_iota(jnp.int32, (LANES, GROUP_W), 1)
    return (c // HEAD_DIM == h).astype(F32)


def _dot_f32(a, b):
    return jnp.dot(a, b, preferred_element_type=F32, precision=lax.Precision.HIGHEST)


def _attn_combine(outs, lses, *, tm=512):
    S = outs[0].shape[1]
    n_col = GROUP_W // LANES

    def body(*refs):
        ins, e_ref = refs[:2 * N_GROUPS], refs[2 * N_GROUPS]
        c_ref, cb_ref, lt_ref = refs[2 * N_GROUPS + 1:2 * N_GROUPS + 4]
        scr = refs[2 * N_GROUPS + 4:]
        o, l = [ins[0][0]], [ins[N_GROUPS][0]]
        for k, d in enumerate(DILS):
            o_ref, l_ref = ins[1 + k], ins[N_GROUPS + 1 + k]
            o.append(_from_residue(lambda r: o_ref[r], d, tm, scr[k * (n_col + 1):k * (n_col + 1) + n_col]))
            l.append(_from_residue(lambda r: l_ref[r], d, tm, scr[k * (n_col + 1) + n_col:(k + 1) * (n_col + 1)]))
        m = jnp.maximum(jnp.maximum(l[0], l[1]), l[2])
        e = [jnp.exp(v - m) for v in l]
        den = e[0] + e[1] + e[2]
        comb = sum(_dot_f32(ev / den, e_ref[...]) * ov for ev, ov in zip(e, o))
        c_ref[...] = comb
        cb_ref[...] = comb.astype(BF16)
        lt_ref[...] = m + jnp.log(den)

    row = pl.BlockSpec((tm, GROUP_W), lambda i: (i, 0))
    dils = [d for _, d in GROUPS]
    return pl.pallas_call(
        body, grid=(S // tm,),
        in_specs=[_res_spec(d, tm, GROUP_W) for d in dils] + [_res_spec(d, tm, LANES) for d in dils]
        + [_resident((LANES, GROUP_W))],
        out_specs=[row, row, pl.BlockSpec((tm, LANES), lambda i: (i, 0))],
        out_shape=[jax.ShapeDtypeStruct((S, GROUP_W), F32), jax.ShapeDtypeStruct((S, GROUP_W), BF16),
                   jax.ShapeDtypeStruct((S, LANES), F32)],
        scratch_shapes=_lane_scratch(tm, GROUP_W + LANES) * len(DILS),
        name="attn_combine", compiler_params=_cparams(("parallel",), 32))(*_hbm(*outs, *lses, _expand_heads()))


def _branch_mix(ya_in, comb_b, w_a, w_b, proj, *, tm=512):
    S = ya_in.shape[0]

    def body(ya_ref, cb_ref, wa_ref, wb_ref, ga_ref, gb_ref, yab_ref, mx_ref):
        y_a = _dot(ya_ref[...], wa_ref[...])
        y_b = _dot(cb_ref[...], wb_ref[...])
        yab_ref[:, 0:D_MODEL] = y_a.astype(BF16)
        yab_ref[:, D_MODEL:2 * D_MODEL] = y_b.astype(BF16)
        mx = jax.nn.sigmoid(ga_ref[...].astype(F32)) * y_a + jax.nn.sigmoid(gb_ref[...].astype(F32)) * y_b
        mx_ref[...] = mx.astype(BF16)

    return pl.pallas_call(
        body, grid=(S // tm,),
        in_specs=[pl.BlockSpec((tm, D_CONV), lambda i: (i, 0)), pl.BlockSpec((tm, GROUP_W), lambda i: (i, 0)),
                  pl.BlockSpec((D_CONV, D_MODEL), lambda i: (0, 0)), pl.BlockSpec((GROUP_W, D_MODEL), lambda i: (0, 0)),
                  pl.BlockSpec((tm, D_MODEL), lambda i: (i, P_GA // D_MODEL)),
                  pl.BlockSpec((tm, D_MODEL), lambda i: (i, P_GB // D_MODEL))],
        out_specs=[pl.BlockSpec((tm, 2 * D_MODEL), lambda i: (i, 0)), pl.BlockSpec((tm, D_MODEL), lambda i: (i, 0))],
        out_shape=[jax.ShapeDtypeStruct((S, 2 * D_MODEL), BF16), jax.ShapeDtypeStruct((S, D_MODEL), BF16)],
        name="branch_mix", compiler_params=_cparams(("parallel",), 40))(*_hbm(ya_in, comb_b, w_a, w_b, proj, proj))


def _mix_ln1(mixin, w_o, b_o, h0, g1, b1, *, tm=512):
    S = mixin.shape[0]

    def body(mx_ref, wo_ref, bo_ref, h0_ref, g_ref, b_ref, xh_ref, rs_ref, h1b_ref):
        z = ALPHA * h0_ref[...] + _dot(mx_ref[...], wo_ref[...]) + bo_ref[...]
        xhat, rstd = _ln_stats(z)
        xh_ref[...] = xhat
        rs_ref[...] = jnp.broadcast_to(rstd, (tm, LANES))
        h1b_ref[...] = (xhat * g_ref[...] + b_ref[...]).astype(BF16)

    row = pl.BlockSpec((tm, D_MODEL), lambda i: (i, 0))
    vec = pl.BlockSpec((1, D_MODEL), lambda i: (0, 0))
    return pl.pallas_call(
        body, grid=(S // tm,),
        in_specs=[row, pl.BlockSpec((D_MODEL, D_MODEL), lambda i: (0, 0)), vec, row, vec, vec],
        out_specs=[row, pl.BlockSpec((tm, LANES), lambda i: (i, 0)), row],
        out_shape=[jax.ShapeDtypeStruct((S, D_MODEL), F32), jax.ShapeDtypeStruct((S, LANES), F32),
                   jax.ShapeDtypeStruct((S, D_MODEL), BF16)],
        name="mix_ln1", compiler_params=_cparams(("parallel",), 40))(*_hbm(mixin, w_o, b_o, h0, g1, b1))


def _gelu_parts(cz):
    cdf = 0.5 * (1.0 + lax.erf(cz * INV_SQRT2))
    return cdf, cz * cdf


def _ffn_conv_fwd(up, cw, cb):
    S = up.shape[0]

    def body(a_ref, g_ref, w_ref, cb_ref, o_ref, a_scr):
        _zero_pads(a_scr, S)
        for t in range(0, S, CHUNK):
            a_scr[PAD + t:PAD + t + CHUNK, :] = a_ref[t:t + CHUNK, :].astype(F32)
        w0, w1, w2 = w_ref[0:1, :], w_ref[1:2, :], w_ref[2:3, :]
        for t in range(0, S, CHUNK):
            am, a0, ap = _shifted(a_scr, t)
            _, gel = _gelu_parts(w0 * am + w1 * a0 + w2 * ap + cb_ref[...])
            o_ref[t:t + CHUNK, :] = (gel * g_ref[t:t + CHUNK, :].astype(F32)).astype(BF16)

    return pl.pallas_call(
        body, grid=(D_FF // SLAB,),
        in_specs=[_slab_spec(S, 0), _slab_spec(S, D_FF), pl.BlockSpec((3, SLAB), lambda j: (0, j)),
                  pl.BlockSpec((1, SLAB), lambda j: (0, j))],
        out_specs=pl.BlockSpec((S, SLAB), lambda j: (0, j)),
        out_shape=jax.ShapeDtypeStruct((S, D_FF), BF16),
        scratch_shapes=[pltpu.VMEM((S + 2 * PAD, SLAB), F32)],
        name="ffn_conv_fwd", compiler_params=_cparams(("parallel",), 40))(*_hbm(up, up, cw, cb))


def _down_ln2_loss(f, w_down, b_down, xhat1, g1, b1, g2, b2, target, *, tm=512):
    S = f.shape[0]

    def body(f_ref, wd_ref, bd_ref, xh1_ref, g1_ref, b1_ref, g2_ref, b2_ref, t_ref, dz_ref, dzb_ref, st_ref):
        h1 = xh1_ref[...] * g1_ref[...] + b1_ref[...]
        z = ALPHA * h1 + _dot(f_ref[...], wd_ref[...]) + bd_ref[...]
        xhat, rstd = _ln_stats(z)
        err = xhat * g2_ref[...] + b2_ref[...] - t_ref[...]
        loss = (0.5 / D_MODEL) * jnp.sum(jnp.sum(err * err, axis=1, keepdims=True), axis=0, keepdims=True)
        dh2 = err * (1.0 / D_MODEL)
        dz = _ln_bwd(dh2, xhat, rstd, g2_ref[...])
        dz_ref[...] = dz
        dzb_ref[...] = dz.astype(BF16)
        upd = _rows8([jnp.sum(dh2 * xhat, axis=0, keepdims=True), jnp.sum(dh2, axis=0, keepdims=True),
                      jnp.broadcast_to(loss, (1, D_MODEL)), jnp.sum(dz, axis=0, keepdims=True)], D_MODEL)

        @pl.when(pl.program_id(0) == 0)
        def _():
            st_ref[...] = upd

        @pl.when(pl.program_id(0) != 0)
        def _():
            st_ref[...] += upd

    row = pl.BlockSpec((tm, D_MODEL), lambda i: (i, 0))
    vec = pl.BlockSpec((1, D_MODEL), lambda i: (0, 0))
    return pl.pallas_call(
        body, grid=(S // tm,),
        in_specs=[pl.BlockSpec((tm, D_FF), lambda i: (i, 0)), _resident((D_FF, D_MODEL)),
                  vec, row, vec, vec, vec, vec, row],
        out_specs=[row, row, pl.BlockSpec((SUBLANES, D_MODEL), lambda i: (0, 0))],
        out_shape=[jax.ShapeDtypeStruct((S, D_MODEL), F32), jax.ShapeDtypeStruct((S, D_MODEL), BF16),
                   jax.ShapeDtypeStruct((SUBLANES, D_MODEL), F32)],
        name="down_ln2_loss", compiler_params=_cparams(("arbitrary",), 56))(
            *_hbm(f, w_down, b_down, xhat1, g1, b1, g2, b2, target))


def _ffn_conv_bwd(up, df, cw, cb):
    S = up.shape[0]

    def body(a_ref, g_ref, df_ref, w_ref, cb_ref, dup_ref, sm_ref, a_scr, d_scr):
        _zero_pads(a_scr, S)
        _zero_pads(d_scr, S)
        for t in range(0, S, CHUNK):
            a_scr[PAD + t:PAD + t + CHUNK, :] = a_ref[t:t + CHUNK, :].astype(F32)
        w0, w1, w2 = w_ref[0:1, :], w_ref[1:2, :], w_ref[2:3, :]
        zero = jnp.zeros((1, SLAB), F32)
        s_dg, s_dcz, s_w0, s_w1, s_w2 = zero, zero, zero, zero, zero
        for t in range(0, S, CHUNK):
            am, a0, ap = _shifted(a_scr, t)
            cz = w0 * am + w1 * a0 + w2 * ap + cb_ref[...]
            cdf, gel = _gelu_parts(cz)
            dfv = df_ref[t:t + CHUNK, :].astype(F32)
            dgte = dfv * gel
            dcz = dfv * g_ref[t:t + CHUNK, :].astype(F32) * (cdf + cz * jnp.exp(-0.5 * cz * cz) * INV_SQRT_2PI)
            dup_ref[1, t:t + CHUNK, :] = dgte.astype(BF16)
            d_scr[PAD + t:PAD + t + CHUNK, :] = dcz
            s_dg = s_dg + jnp.sum(dgte, axis=0, keepdims=True)
            s_dcz = s_dcz + jnp.sum(dcz, axis=0, keepdims=True)
            s_w0 = s_w0 + jnp.sum(dcz * am, axis=0, keepdims=True)
            s_w1 = s_w1 + jnp.sum(dcz * a0, axis=0, keepdims=True)
            s_w2 = s_w2 + jnp.sum(dcz * ap, axis=0, keepdims=True)
        s_da = zero
        for t in range(0, S, CHUNK):
            dm, d0, dp = _shifted(d_scr, t)
            da = w0 * dp + w1 * d0 + w2 * dm
            dup_ref[0, t:t + CHUNK, :] = da.astype(BF16)
            s_da = s_da + jnp.sum(da, axis=0, keepdims=True)
        sm_ref[...] = _rows8([s_da, s_dg, s_dcz, s_w0, s_w1, s_w2], SLAB)

    return pl.pallas_call(
        body, grid=(D_FF // SLAB,),
        in_specs=[_slab_spec(S, 0), _slab_spec(S, D_FF), pl.BlockSpec((S, SLAB), lambda j: (0, j)),
                  pl.BlockSpec((3, SLAB), lambda j: (0, j)), pl.BlockSpec((1, SLAB), lambda j: (0, j))],
        out_specs=[pl.BlockSpec((2, S, SLAB), lambda j: (0, 0, j)), pl.BlockSpec((SUBLANES, SLAB), lambda j: (0, j))],
        out_shape=[jax.ShapeDtypeStruct((2, S, D_FF), BF16), jax.ShapeDtypeStruct((SUBLANES, D_FF), F32)],
        scratch_shapes=[pltpu.VMEM((S + 2 * PAD, SLAB), F32)] * 2,
        name="ffn_conv_bwd", compiler_params=_cparams(("parallel",), 48))(*_hbm(up, up, df, cw, cb))


def _up_bwd_ln1(dup, w_up3, dz2, xhat1, rstd1, g1, *, tm=512):
    S = dz2.shape[0]
    ns, _, tk = w_up3.shape
    per_plane = D_FF // tk

    def body(du_ref, w_ref, dz2_ref, xh_ref, rs_ref, g_ref, dz_ref, dzb_ref, st_ref):
        dh = ALPHA * dz2_ref[...]
        for k in range(ns):
            col = (k % per_plane) * tk
            dh = dh + _dot_nt(du_ref[k // per_plane, :, col:col + tk], w_ref[k])
        xhat = xh_ref[...]
        dz = _ln_bwd(dh, xhat, rs_ref[:, 0:1], g_ref[...])
        dz_ref[...] = dz
        dzb_ref[...] = dz.astype(BF16)
        upd = _rows8([jnp.sum(dh * xhat, axis=0, keepdims=True), jnp.sum(dh, axis=0, keepdims=True),
                      jnp.sum(dz, axis=0, keepdims=True)], D_MODEL)

        @pl.when(pl.program_id(0) == 0)
        def _():
            st_ref[...] = upd

        @pl.when(pl.program_id(0) != 0)
        def _():
            st_ref[...] += upd

    row = pl.BlockSpec((tm, D_MODEL), lambda i: (i, 0))
    return pl.pallas_call(
        body, grid=(S // tm,),
        in_specs=[pl.BlockSpec((dup.shape[0], tm, D_FF), lambda i: (0, i, 0)), _resident(w_up3.shape),
                  row, row, pl.BlockSpec((tm, LANES), lambda i: (i, 0)), pl.BlockSpec((1, D_MODEL), lambda i: (0, 0))],
        out_specs=[row, row, pl.BlockSpec((SUBLANES, D_MODEL), lambda i: (0, 0))],
        out_shape=[jax.ShapeDtypeStruct((S, D_MODEL), F32), jax.ShapeDtypeStruct((S, D_MODEL), BF16),
                   jax.ShapeDtypeStruct((SUBLANES, D_MODEL), F32)],
        name="up_bwd_ln1", compiler_params=_cparams(("arbitrary",), 56))(*_hbm(dup, w_up3, dz2, xhat1, rstd1, g1))


def _mix_bwd(dz1b, w_o, proj, yab, *, tm=512):
    S = dz1b.shape[0]

    def body(dz_ref, wo_ref, ga_ref, gb_ref, y_ref, dy_ref, dg_ref):
        dmx = _dot_nt(dz_ref[...], wo_ref[...])
        for k, gt_ref in enumerate((ga_ref, gb_ref)):
            sl = slice(k * D_MODEL, (k + 1) * D_MODEL)
            sg = jax.nn.sigmoid(gt_ref[...].astype(F32))
            dy_ref[:, sl] = (dmx * sg).astype(BF16)
            dg_ref[k] = (dmx * y_ref[:, sl].astype(F32) * sg * (1.0 - sg)).astype(BF16)

    row = pl.BlockSpec((tm, D_MODEL), lambda i: (i, 0))
    wide = pl.BlockSpec((tm, 2 * D_MODEL), lambda i: (i, 0))
    return pl.pallas_call(
        body, grid=(S // tm,),
        in_specs=[row, _resident(w_o.shape), pl.BlockSpec((tm, D_MODEL), lambda i: (i, P_GA // D_MODEL)),
                  pl.BlockSpec((tm, D_MODEL), lambda i: (i, P_GB // D_MODEL)), wide],
        out_specs=[wide, pl.BlockSpec((2, tm, D_MODEL), lambda i: (0, i, 0))],
        out_shape=[jax.ShapeDtypeStruct((S, 2 * D_MODEL), BF16), jax.ShapeDtypeStruct((2, S, D_MODEL), BF16)],
        name="mix_bwd", compiler_params=_cparams(("parallel",), 40))(*_hbm(dz1b, w_o, proj, proj, yab))


def _conv_gate_bwd(proj, dya_in, conv_w):
    S = proj.shape[0]

    def body(b_ref, c_ref, h_ref, dy_ref, w_ref, o_ref, sm_ref, u_scr, d_scr):
        _zero_pads(u_scr, S)
        _zero_pads(d_scr, S)
        for t in range(0, S, CHUNK):
            u_scr[PAD + t:PAD + t + CHUNK, :] = c_ref[t:t + CHUNK, :].astype(F32) * h_ref[t:t + CHUNK, :].astype(F32)
        w0, w1, w2 = w_ref[0:1, :], w_ref[1:2, :], w_ref[2:3, :]
        zero = jnp.zeros((1, SLAB), F32)
        s_w0, s_w1, s_w2 = zero, zero, zero
        for t in range(0, S, CHUNK):
            um, u0, up = _shifted(u_scr, t)
            dy = dy_ref[t:t + CHUNK, :].astype(F32)
            o_ref[0, t:t + CHUNK, :] = (dy * (w0 * um + w1 * u0 + w2 * up)).astype(BF16)
            dcv = dy * b_ref[t:t + CHUNK, :].astype(F32)
            d_scr[PAD + t:PAD + t + CHUNK, :] = dcv
            s_w0 = s_w0 + jnp.sum(dcv * um, axis=0, keepdims=True)
            s_w1 = s_w1 + jnp.sum(dcv * u0, axis=0, keepdims=True)
            s_w2 = s_w2 + jnp.sum(dcv * up, axis=0, keepdims=True)
        for t in range(0, S, CHUNK):
            dm, d0, dp = _shifted(d_scr, t)
            du = w0 * dp + w1 * d0 + w2 * dm
            o_ref[1, t:t + CHUNK, :] = (du * h_ref[t:t + CHUNK, :].astype(F32)).astype(BF16)
            o_ref[2, t:t + CHUNK, :] = (du * c_ref[t:t + CHUNK, :].astype(F32)).astype(BF16)
        sm_ref[...] = _rows8([s_w0, s_w1, s_w2], SLAB)

    return pl.pallas_call(
        body, grid=(D_CONV // SLAB,),
        in_specs=[_slab_spec(S, P_B), _slab_spec(S, P_C), _slab_spec(S, P_H),
                  pl.BlockSpec((S, SLAB), lambda j: (0, j)), pl.BlockSpec((3, SLAB), lambda j: (0, j))],
        out_specs=[pl.BlockSpec((3, S, SLAB), lambda j: (0, 0, j)), pl.BlockSpec((SUBLANES, SLAB), lambda j: (0, j))],
        out_shape=[jax.ShapeDtypeStruct((3, S, D_CONV), BF16), jax.ShapeDtypeStruct((SUBLANES, D_CONV), F32)],
        scratch_shapes=[pltpu.VMEM((S + 2 * PAD, SLAB), F32)] * 2,
        name="conv_gate_bwd", compiler_params=_cparams(("parallel",), 48))(*_hbm(proj, proj, proj, dya_in, conv_w))


def _comb_bwd(dyab, w_b, comb, lse_tot, *, tm=512):
    S = comb.shape[0]
    widths, dtypes = (GROUP_W, LANES, LANES), (BF16, F32, F32)

    def body(dy_ref, wb_ref, c_ref, lt_ref, e_ref, *rest):
        outs, scr = rest[:3 * N_GROUPS], rest[3 * N_GROUPS:]
        dcb = _dot_nt(dy_ref[...], wb_ref[...]).astype(BF16)
        dc = dcb.astype(F32)
        delta = lax.dot_general(dc * c_ref[...], e_ref[...], (((1,), (1,)), ((), ())),
                                preferred_element_type=F32, precision=lax.Precision.HIGHEST)
        for k, (val, dtype) in enumerate(zip((dc, lt_ref[...], delta), dtypes)):
            outs[k][0] = val.astype(dtype)
            _to_residue(val, [outs[3 * (1 + j) + k] for j in range(len(DILS))], DILS, tm, dtype,
                        scr[:val.shape[1] // LANES])

    out_specs, out_shape = [], []
    for _, d in GROUPS:
        out_specs += [_res_spec(d, tm, w) for w in widths]
        out_shape += [jax.ShapeDtypeStruct((d, S // d, w), t) for w, t in zip(widths, dtypes)]
    res = pl.pallas_call(
        body, grid=(S // tm,),
        in_specs=[pl.BlockSpec((tm, D_MODEL), lambda i: (i, 1)), _resident(w_b.shape),
                  pl.BlockSpec((tm, GROUP_W), lambda i: (i, 0)), pl.BlockSpec((tm, LANES), lambda i: (i, 0)),
                  _resident((LANES, GROUP_W))],
        out_specs=out_specs, out_shape=out_shape, scratch_shapes=_lane_scratch(tm, GROUP_W),
        name="comb_bwd", compiler_params=_cparams(("parallel",), 32))(*_hbm(dyab, w_b, comb, lse_tot, _expand_heads()))
    return [tuple(res[3 * g:3 * g + 3]) for g in range(N_GROUPS)]


def _attn_bwd(qkv, col0, g, dcomb, lse_tot, delta):
    dil, sub, _ = qkv.shape
    nb = sub // TQ

    def body(q_ref, kp, kc, kn, vp, vc, vn, do_ref, lse_ref, dl_ref, bias_ref, dq_ref, dk_ref, dv_ref,
             ak, av, dqt_scr, s_scr, dp_scr, ds_scr, p_scr):
        i = pl.program_id(1)

        @pl.when(i == 0)
        def _():
            ak[...] = jnp.zeros_like(ak)
            av[...] = jnp.zeros_like(av)

        @pl.when(i < nb)
        def _():
            kwin = _window(kp, kc, kn)
            vwin = _window(vp, vc, vn)
            q = q_ref[...] * ATT_SCALE
            do = do_ref[...]
            lse_t, dl_t = lse_ref[...].T, dl_ref[...].T
            for h in range(HEADS_PER_GROUP):
                s_scr[h] = _dot_nt(_pair(kwin, h), _own_lanes(_pair(q, h), h))
                dp_scr[h] = _dot_nt(_pair(vwin, h), _own_lanes(_pair(do, h), h))
            for h in range(HEADS_PER_GROUP):
                p = jnp.exp(s_scr[h] + _slope(g, h) * bias_ref[...] - lse_t[h:h + 1, :])
                ds_scr[h] = (p * (dp_scr[h] - dl_t[h:h + 1, :])).astype(BF16)
                p_scr[h] = p.astype(BF16)
            for h in range(HEADS_PER_GROUP):
                dqt_scr[h * HEAD_DIM:(h + 1) * HEAD_DIM, :] = _own_rows(_dot_tn(_pair(kwin, h), ds_scr[h]), h)
            for h in range(0, HEADS_PER_GROUP, 2):
                cols = slice(h * HEAD_DIM, (h + 2) * HEAD_DIM)
                q2 = jnp.concatenate([_own_lanes(_pair(q, h), h), _own_lanes(_pair(q, h), h + 1)], axis=0)
                do2 = jnp.concatenate([_own_lanes(_pair(do, h), h), _own_lanes(_pair(do, h), h + 1)], axis=0)
                ak[RADIUS:RADIUS + 2 * TQ, cols] += _dot(jnp.concatenate([ds_scr[h], ds_scr[h + 1]], axis=1), q2)
                av[RADIUS:RADIUS + 2 * TQ, cols] += _dot(jnp.concatenate([p_scr[h], p_scr[h + 1]], axis=1), do2)
            dq_ref[...] = (dqt_scr[...].T * ATT_SCALE).astype(BF16)

        dk_ref[...] = ak[0:TQ, :].astype(BF16)
        dv_ref[...] = av[0:TQ, :].astype(BF16)
        ak[0:2 * TQ, :] = ak[TQ:3 * TQ, :]
        av[0:2 * TQ, :] = av[TQ:3 * TQ, :]
        ak[2 * TQ:3 * TQ, :] = jnp.zeros((TQ, GROUP_W), F32)
        av[2 * TQ:3 * TQ, :] = jnp.zeros((TQ, GROUP_W), F32)

    tok = pl.BlockSpec((None, TQ, GROUP_W), lambda r, i: (r, jnp.minimum(i, nb - 1), 0))
    stat = pl.BlockSpec((None, TQ, LANES), lambda r, i: (r, jnp.minimum(i, nb - 1), 0))
    dkv_spec = pl.BlockSpec((None, TQ, GROUP_W), lambda r, i: (r, jnp.maximum(i - 1, 0), 0))
    return pl.pallas_call(
        body, grid=(dil, nb + 1), in_specs=_qkv_specs(nb, col0) + [tok, stat, stat, _bias_spec(nb)],
        out_specs=[tok, dkv_spec, dkv_spec], out_shape=[jax.ShapeDtypeStruct((dil, sub, GROUP_W), BF16)] * 3,
        scratch_shapes=[pltpu.VMEM((3 * TQ, GROUP_W), F32)] * 2 + [pltpu.VMEM((GROUP_W, TQ), F32)]
        + [pltpu.VMEM((HEADS_PER_GROUP, 2 * TQ, TQ), F32)] * 2 + [pltpu.VMEM((HEADS_PER_GROUP, 2 * TQ, TQ), BF16)] * 2,
        name=f"attn_bwd_g{g}", compiler_params=_cparams(("arbitrary", "arbitrary"), 32))(
            *_hbm(*([qkv] * 7), dcomb, lse_tot, delta, _attn_bias_table(g)))


def _in_bwd_ln0(dgated, dqkv, w_nat, w_dil, dz1, x, g0, *, tm=256):
    S = x.shape[0]
    n_gated, n_in = len(dgated), 3 * N_GROUPS

    def body(*refs):
        g_refs, d_refs = refs[:n_gated], refs[n_gated:n_gated + n_in]
        wn_ref, *wd_refs = refs[n_gated + n_in:n_gated + n_in + N_GROUPS]
        dz_ref, x_ref, g_ref, gx_ref, st_ref, *tmp_ref = refs[n_gated + n_in + N_GROUPS:]
        dh = ALPHA * dz_ref[...]
        col = 0
        for ref in g_refs:
            for k in range(ref.shape[0]):
                dh = dh + _dot_nt(ref[k], wn_ref[:, col:col + D_MODEL])
                col += D_MODEL
        for g, (_, d) in enumerate(GROUPS):
            rows = [jnp.concatenate([d_refs[3 * g + k][r] for k in range(3)], axis=1) for r in range(d)]
            w = wn_ref[:, col:col + QKV_W] if d == 1 else wd_refs[g - 1][...]
            res = _dot_nt(jnp.concatenate(rows, axis=0), w)
            if d == 1:
                dh = dh + res
            else:
                n = tm // d
                dh = dh + _from_residue(lambda r: res[r * n:(r + 1) * n, :], d, tm, tmp_ref)
        xhat, rstd = _ln_stats(x_ref[...])
        gx_ref[...] = _ln_bwd(dh, xhat, rstd, g_ref[...])
        upd = _rows8([jnp.sum(dh * xhat, axis=0, keepdims=True), jnp.sum(dh, axis=0, keepdims=True)], D_MODEL)

        @pl.when(pl.program_id(0) == 0)
        def _():
            st_ref[...] = upd

        @pl.when(pl.program_id(0) != 0)
        def _():
            st_ref[...] += upd

    row = pl.BlockSpec((tm, D_MODEL), lambda i: (i, 0))
    g_specs = [pl.BlockSpec((a.shape[0], tm, D_MODEL), lambda i: (0, i, 0)) for a in dgated]
    d_specs = []
    for _, d in GROUPS:
        d_specs += [_res_spec(d, tm, GROUP_W)] * 3
    operands = list(dgated) + [a for grp in dqkv for a in grp] + [w_nat] + list(w_dil) + [dz1, x, g0]
    return pl.pallas_call(
        body, grid=(S // tm,),
        in_specs=g_specs + d_specs + [_resident(w_nat.shape)] + [_resident(w.shape) for w in w_dil]
        + [row, row, pl.BlockSpec((1, D_MODEL), lambda i: (0, 0))],
        out_specs=[row, pl.BlockSpec((SUBLANES, D_MODEL), lambda i: (0, 0))],
        out_shape=[jax.ShapeDtypeStruct((S, D_MODEL), F32), jax.ShapeDtypeStruct((SUBLANES, D_MODEL), F32)],
        scratch_shapes=_lane_scratch(tm, D_MODEL),
        name="in_bwd_ln0", compiler_params=_cparams(("arbitrary",), 52))(*_hbm(*operands))


HBM_SPEC = pl.BlockSpec(memory_space=pltpu.HBM)


def _place():
    x, y, c = lax.axis_index("x"), lax.axis_index("y"), lax.axis_index("c")
    chips = [(1 - x, y), (x, 1 - y), (1 - x, 1 - y)]
    return x, y, c, chips


def _allgather_shards(shards, after, *, name, collective_id):
    n = len(shards)
    per = 6

    def body(*refs):
        ins, outs = refs[:n], refs[n + len(after):2 * n + len(after)]
        send_sems, recv_sems, loc_sems = refs[2 * n + len(after):]
        x, y, c, chips = _place()
        me = 2 * x + y
        sib = (x, y, 1 - c)
        peers = [sib] + [(px, py, c) for px, py in chips]
        barrier = pltpu.get_barrier_semaphore()
        for peer in peers:
            pl.semaphore_signal(barrier, inc=1, device_id=peer, device_id_type=MESH)
        pl.semaphore_wait(barrier, len(peers))

        def rcopy(w, k, src, dst, to):
            return pltpu.make_async_remote_copy(src_ref=src, dst_ref=dst, send_sem=send_sems.at[per * w + k],
                                                recv_sem=recv_sems.at[per * w + k], device_id=to, device_id_type=MESH)

        split = [s.shape[0] == N_CORES for s in shards]
        half = lambda w: c if split[w] else 0
        local, sends = [], []
        for w in range(n):
            cp = pltpu.make_async_copy(ins[w], outs[w].at[me], loc_sems.at[w])
            cp.start()
            local.append(cp)
            for j, (px, py) in enumerate(chips):
                cp = rcopy(w, j, ins[w].at[half(w)], outs[w].at[me, half(w)], (px, py, c))
                cp.start()
                sends.append(cp)
        for w in range(n):
            for j, (px, py) in enumerate(chips):
                slot = outs[w].at[2 * px + py, half(w)]
                rcopy(w, j, slot, slot, (px, py, c)).wait_recv()
                if split[w]:
                    cp = rcopy(w, 3 + j, slot, slot, sib)
                    cp.start()
                    sends.append(cp)
        for w in range(n):
            if split[w]:
                for j, (px, py) in enumerate(chips):
                    slot = outs[w].at[2 * px + py, 1 - c]
                    rcopy(w, 3 + j, slot, slot, sib).wait_recv()
        for cp in sends:
            cp.wait_send()
        for cp in local:
            cp.wait()

    return pl.kernel(
        body, out_type=[jax.ShapeDtypeStruct((N_CHIPS,) + s.shape, s.dtype) for s in shards],
        mesh=plsc.ScalarSubcoreMesh(axis_name="sequencer", num_cores=1),
        scratch_types=[pltpu.SemaphoreType.DMA((per * n,)), pltpu.SemaphoreType.DMA((per * n,)),
                       pltpu.SemaphoreType.DMA((n,))],
        name=name, compiler_params=pltpu.CompilerParams(collective_id=collective_id))(*shards, *after)


def _exchange_grads(grads, *, name, collective_id):
    n = len(grads)
    per = 7

    def body(*refs):
        ins, outs = refs[:n], refs[n:2 * n]
        send_sems, recv_sems, loc_sems = refs[2 * n:]
        x, y, c, chips = _place()
        me = 2 * x + y
        sib = (x, y, 1 - c)
        peers = [sib] + [(px, py, c) for px, py in chips]
        barrier = pltpu.get_barrier_semaphore()
        for peer in peers:
            pl.semaphore_signal(barrier, inc=1, device_id=peer, device_id_type=MESH)
        pl.semaphore_wait(barrier, len(peers))

        def rcopy(w, k, src, dst, to):
            return pltpu.make_async_remote_copy(src_ref=src, dst_ref=dst, send_sem=send_sems.at[per * w + k],
                                                recv_sem=recv_sems.at[per * w + k], device_id=to, device_id_type=MESH)

        local, sends = [], []
        for w in range(n):
            cp = pltpu.make_async_copy(ins[w].at[me], outs[w].at[c, me], loc_sems.at[w])
            cp.start()
            local.append(cp)
            cp = rcopy(w, 0, ins[w].at[me], outs[w].at[c, me], sib)
            cp.start()
            sends.append(cp)
            for j, (px, py) in enumerate(chips):
                cp = rcopy(w, 1 + j, ins[w].at[2 * px + py], outs[w].at[c, me], (px, py, c))
                cp.start()
                sends.append(cp)
        for w in range(n):
            for j, (px, py) in enumerate(chips):
                slot = outs[w].at[c, 2 * px + py]
                rcopy(w, 1 + j, slot, slot, (px, py, c)).wait_recv()
                cp = rcopy(w, 4 + j, slot, slot, sib)
                cp.start()
                sends.append(cp)
        for w in range(n):
            slot = outs[w].at[1 - c, me]
            rcopy(w, 0, slot, slot, sib).wait_recv()
            for j, (px, py) in enumerate(chips):
                slot = outs[w].at[1 - c, 2 * px + py]
                rcopy(w, 4 + j, slot, slot, sib).wait_recv()
        for cp in sends:
            cp.wait_send()
        for cp in local:
            cp.wait()

    return pl.kernel(
        body, out_type=[jax.ShapeDtypeStruct((N_CORES,) + g.shape, g.dtype) for g in grads],
        mesh=plsc.ScalarSubcoreMesh(axis_name="sequencer", num_cores=1),
        scratch_types=[pltpu.SemaphoreType.DMA((per * n,)), pltpu.SemaphoreType.DMA((per * n,)),
                       pltpu.SemaphoreType.DMA((n,))],
        name=name, compiler_params=pltpu.CompilerParams(collective_id=collective_id))(*grads)


def _allgather_small(vec, after):
    def body(v_ref, _, o_ref, send_sems, recv_sems, loc_sem):
        x, y, c = lax.axis_index("x"), lax.axis_index("y"), lax.axis_index("c")
        me = 4 * x + 2 * y + c

        def peer(k):
            flip = lambda v, bit: 1 - v if (k >> bit) & 1 else v
            return flip(x, 2), flip(y, 1), flip(c, 0)

        loc = pltpu.make_async_copy(v_ref, o_ref.at[me], loc_sem)
        loc.start()
        sends = []
        for k in range(1, N_DEV):
            cp = pltpu.make_async_remote_copy(src_ref=v_ref, dst_ref=o_ref.at[me], send_sem=send_sems.at[k - 1],
                                              recv_sem=recv_sems.at[k - 1], device_id=peer(k), device_id_type=MESH)
            cp.start()
            sends.append(cp)
        for k in range(1, N_DEV):
            px, py, pc = peer(k)
            pltpu.make_async_remote_copy(src_ref=v_ref, dst_ref=o_ref.at[4 * px + 2 * py + pc],
                                         send_sem=send_sems.at[k - 1], recv_sem=recv_sems.at[k - 1],
                                         device_id=(px, py, pc), device_id_type=MESH).wait_recv()
        for cp in sends:
            cp.wait_send()
        loc.wait()

    return pl.pallas_call(
        body, in_specs=[HBM_SPEC, HBM_SPEC], out_specs=HBM_SPEC,
        out_shape=jax.ShapeDtypeStruct((N_DEV,) + vec.shape, vec.dtype),
        scratch_shapes=[pltpu.SemaphoreType.DMA((N_DEV - 1,)), pltpu.SemaphoreType.DMA((N_DEV - 1,)),
                        pltpu.SemaphoreType.DMA],
        name="allgather_small")(vec, after)


def _adamw(w, g, m, v):
    m = ADAM_B1 * m + (1.0 - ADAM_B1) * g
    v = ADAM_B2 * v + (1.0 - ADAM_B2) * (g * g)
    m_hat = m / (1.0 - ADAM_B1 ** ADAM_STEP)
    v_hat = v / (1.0 - ADAM_B2 ** ADAM_STEP)
    delta = -ADAM_LR * (m_hat / (jnp.sqrt(v_hat) + ADAM_EPS) + ADAM_WD * w)
    return delta, m, v


def _reduce_adamw(parts, w, m, v, *, tr, name):
    R, C = w.shape

    def body(p_ref, w_ref, m_ref, v_ref, g_ref, d_ref, nm_ref, nv_ref):
        def core_sum(cc):
            s = p_ref[cc, 0].astype(F32)
            for k in range(1, N_CHIPS):
                s = s + p_ref[cc, k].astype(F32)
            return s

        g = core_sum(0) + core_sum(1)
        delta, nm, nv = _adamw(w_ref[...], g, m_ref[...], v_ref[...])
        g_ref[...] = g
        d_ref[...] = delta
        nm_ref[...] = nm
        nv_ref[...] = nv

    blk = pl.BlockSpec((tr, C), lambda i: (i, 0))
    return pl.pallas_call(
        body, grid=(R // tr,),
        in_specs=[pl.BlockSpec((N_CORES, N_CHIPS, tr, C), lambda i: (0, 0, i, 0)), blk, blk, blk],
        out_specs=[blk] * 4, out_shape=[jax.ShapeDtypeStruct((R, C), F32)] * 4,
        name=name, compiler_params=_cparams(("parallel",), 40))(*_hbm(parts, w, m, v))


def _reduce_adamw_vectors(allv, offs, ws, ms, vs):
    n = len(ws)

    def body(a_ref, *refs):
        w_refs, m_refs, v_refs = refs[:n], refs[n:2 * n], refs[2 * n:3 * n]
        tot_ref, outs = refs[3 * n], refs[3 * n + 1:]
        s = a_ref[0]
        for d in range(1, N_DEV):
            s = s + a_ref[d]
        tot_ref[...] = s
        for k in range(n):
            g = s[:, offs[k]:offs[k] + w_refs[k].shape[1]]
            delta, nm, nv = _adamw(w_refs[k][...], g, m_refs[k][...], v_refs[k][...])
            for ref, val in zip(outs[4 * k:4 * k + 4], (g, delta, nm, nv)):
                ref[...] = val

    out_shape = [jax.ShapeDtypeStruct(allv.shape[1:], F32)]
    for w in ws:
        out_shape += [jax.ShapeDtypeStruct(w.shape, F32)] * 4
    res = pl.pallas_call(body, out_shape=out_shape, name="reduce_adamw_vectors",
                         compiler_params=_cparams((), 40))(allv, *ws, *ms, *vs)
    return res[0], [tuple(res[1 + 4 * k:5 + 4 * k]) for k in range(n)]


def _adamw_taps(ws, gs, ms, vs):
    n = len(ws)

    def body(*refs):
        outs = refs[4 * n:]
        for k in range(n):
            res = _adamw(refs[k][...], refs[n + k][...], refs[2 * n + k][...], refs[3 * n + k][...])
            for ref, val in zip(outs[3 * k:3 * k + 3], res):
                ref[...] = val

    out_shape = []
    for w in ws:
        out_shape += [jax.ShapeDtypeStruct(w.shape, F32)] * 3
    res = pl.pallas_call(body, out_shape=out_shape, name="adamw_taps")(*ws, *gs, *ms, *vs)
    return [tuple(res[3 * k:3 * k + 3]) for k in range(n)]


def _pack(pieces):
    flat, offs, n = [], [], 0
    for p in pieces:
        size = -(-p.size // LANES) * LANES
        flat.append(jnp.pad(p.reshape(-1), (0, size - p.size)))
        offs.append(n)
        n += size
    return jnp.concatenate(flat).reshape(1, n), offs


def _local_step(x, target, p, wfull, on_ready=lambda group: None, before_ln0=()):
    S = x.shape[0]
    dils = [d for _, d in GROUPS]

    h0, h0b, *h0_res = _ln0_fwd(x, p["ln0_g"], p["ln0_b"], before_ln0)
    h0_rows = [h0b] + [h.reshape(S, D_MODEL) for h in h0_res]

    if isinstance(wfull, dict):
        w_in3, pending = wfull["w_in"], None
    else:
        w_in3, launch_rest, assemble = wfull
        w_in3, h0b = lax.optimization_barrier((w_in3, h0b))
        pending = launch_rest(h0b)

    w_blocks = w_in3.transpose(1, 0, 2).reshape(D_MODEL, N_BLK, GROUP_W)
    w_perm = jnp.concatenate([w_blocks[:, b] for b in PERM], axis=1)
    b_blocks = p["b_in"].reshape(N_BLK, GROUP_W)
    b_perm = jnp.concatenate([b_blocks[b] for b in PERM]).reshape(1, N_IN)
    w_nat, b_nat = w_perm[:, :N_NAT], b_perm[:, :N_NAT]
    qkv_cols = [slice(P_Q0 + g * QKV_W, P_Q0 + (g + 1) * QKV_W) for g in range(N_GROUPS)]
    w_qkv = [w_perm[:, c] for c in qkv_cols]

    proj = _mm_nn(h0b, w_nat, b_nat, tm=512, tn=N_NAT // 2, out_dtype=BF16, name="proj")
    qkv = [proj[None]]
    for g in range(1, N_GROUPS):
        t = _mm_nn(h0_rows[g], w_qkv[g], b_perm[:, qkv_cols[g]], tm=512, tn=QKV_W, out_dtype=BF16, name=f"proj_qkv{g}")
        qkv.append(t.reshape(dils[g], S // dils[g], QKV_W))
    if pending is not None:
        pending, qkv = lax.optimization_barrier((pending, qkv))
        proj = qkv[0][0]
        wfull = assemble(pending)
    w_up3 = wfull["w_up"]
    w_a, w_o, w_down, w_b = wfull["w_a"], wfull["w_o"], wfull["w_down"], wfull["w_b"]
    conv_w, ffn_conv_w = wfull["conv_w"], wfull["ffn_conv_w"]
    col0 = [P_Q0 // GROUP_W] + [0] * (N_GROUPS - 1)
    ya_in = _conv_gate_fwd(proj, conv_w)
    att = [_attn_fwd(qkv[g], col0[g], g) for g in range(N_GROUPS)]
    comb, comb_b, lse_tot = _attn_combine([a[0] for a in att], [a[1] for a in att])
    yab, mixin = _branch_mix(ya_in, comb_b, w_a, w_b, proj)
    xhat1, rstd1, h1b = _mix_ln1(mixin, w_o, p["b_o"], h0, p["ln1_g"], p["ln1_b"])
    up = _mm_nn(h1b, w_up3, p["b_up"], tm=512, tn=w_up3.shape[2], out_dtype=BF16, name="up")
    f = _ffn_conv_fwd(up, ffn_conv_w, p["ffn_conv_b"])
    dz2, dz2b, st2 = _down_ln2_loss(f, w_down, p["b_down"], xhat1, p["ln1_g"], p["ln1_b"],
                                    p["ln2_g"], p["ln2_b"], target)

    gw = {}
    gw["w_down"] = _mm_tn(f, dz2b, n_out=1, tn=D_MODEL, ts=1024, g_block=(1024, D_MODEL),
                          g_map=lambda j, s: (s, 0), name="grad_w_down").reshape(N_CHIPS, D_FF // N_CHIPS, D_MODEL)
    df = _mm_nt(dz2b, w_down, tm=512, name="df")
    dup, sm_ffn = _ffn_conv_bwd(up, df, ffn_conv_w, p["ffn_conv_b"])
    up_tn = w_up3.shape[2]
    up_pp = D_FF // up_tn
    gw["w_up"] = _mm_tn(h1b, dup, n_out=N_CHIPS, tn=up_tn, ts=1024, g_block=(None, 1024, up_tn),
                        g_map=lambda j, s: (j // up_pp, s, j % up_pp), name="grad_w_up")
    exchanged = on_ready({n: gw[n] for n in ("w_down", "w_up")}) or {}
    dz1, dz1b, st1 = _up_bwd_ln1(dup, w_up3, dz2, xhat1, rstd1, p["ln1_g"])

    gw["w_o"] = _mm_tn(mixin, dz1b, n_out=1, tn=D_MODEL, ts=512, g_block=(512, D_MODEL),
                       g_map=lambda j, s: (s, 0), name="grad_w_o").reshape(N_CHIPS, D_MODEL // N_CHIPS, D_MODEL)
    dyab, dgab = _mix_bwd(dz1b, w_o, proj, yab)
    gw["w_a"] =_mm_tn(ya_in, dyab, n_out=1, tn=D_MODEL, ts=512, g_block=(512, D_MODEL),
                       g_map=lambda j, s: (s, 0), name="grad_w_a").reshape(N_CHIPS, D_CONV // N_CHIPS, D_MODEL)
    gw_b = _mm_tn(comb_b, dyab, n_out=1, tn=D_MODEL, ts=1024, g_block=(1024, D_MODEL),
                  g_map=lambda j, s: (s, 1), name="grad_w_b")
    gw["w_b"] = gw_b.reshape(GROUP_W, N_CHIPS, D_MODEL // N_CHIPS).transpose(1, 0, 2)
    exchanged_mix = on_ready({n: gw[n] for n in ("w_o", "w_a", "w_b")}) or {}
    dya_in = _mm_nt(dyab, w_a, tm=512, a_col=0, name="dya_in")
    exchanged, dya_in = lax.optimization_barrier((exchanged, dya_in))
    dbch, sm_conv = _conv_gate_bwd(proj, dya_in, conv_w)
    att_stats = _comb_bwd(dyab, w_b, comb, lse_tot)
    exchanged_mix, att_stats = lax.optimization_barrier((exchanged_mix, att_stats))
    exchanged.update(exchanged_mix)
    dqkv = [_attn_bwd(qkv[g], col0[g], g, *att_stats[g]) for g in range(N_GROUPS)]

    w_pieces, b_pieces = [], []
    for nm, planes in (("bch", dbch), ("gab", dgab)):
        pw, pc = _mm_tn(h0b, planes, n_out=planes.shape[0], tn=D_MODEL, ts=1024, g_block=(None, 1024, D_MODEL),
                        g_map=lambda j, s: (j, s, 0), colsum=True, name="grad_w_in_" + nm)
        w_pieces.append(pw.transpose(1, 0, 2).reshape(D_MODEL, planes.shape[0] * D_MODEL))
        b_pieces.append(pc[0])
    for g in range(N_GROUPS):
        pw, pc = _mm_tn_cat(h0_rows[g], [a.reshape(S, GROUP_W) for a in dqkv[g]], ts=1024, name=f"grad_w_in_qkv{g}")
        w_pieces.append(pw)
        b_pieces.append(pc[0])
    dw_blocks = jnp.concatenate(w_pieces, axis=1).reshape(D_MODEL, N_BLK, GROUP_W)
    dw_ref = jnp.concatenate([dw_blocks[:, b] for b in INV_PERM], axis=1)
    gw["w_in"] = dw_ref.reshape(D_MODEL, N_CHIPS, N_IN // N_CHIPS).transpose(1, 0, 2)
    exchanged.update(on_ready({"w_in": gw["w_in"]}) or {})
    db_blocks = jnp.concatenate(b_pieces).reshape(N_BLK, GROUP_W)
    grad_b_in = jnp.concatenate([db_blocks[b] for b in INV_PERM])

    grad_x, st0 = _in_bwd_ln0([dbch, dgab], dqkv, w_nat, w_qkv[1:], dz1, x, p["ln0_g"])

    small = {
        "loss": st2[2:3, 0:1],
        "ln0_g": st0[0], "ln0_b": st0[1], "b_in": grad_b_in, "conv_w": sm_conv[0:3],
        "b_o": st1[2], "ln1_g": st1[0], "ln1_b": st1[1],
        "b_up": jnp.concatenate([sm_ffn[0], sm_ffn[1]]), "ffn_conv_w": sm_ffn[3:6], "ffn_conv_b": sm_ffn[2],
        "b_down": st2[3], "ln2_g": st2[0], "ln2_b": st2[1],
    }
    return grad_x, exchanged or gw, small


BIG =("w_in", "w_a", "w_b", "w_o", "w_up", "w_down")
CONV = ("conv_w", "ffn_conv_w")
VECS = ("ln0_g", "ln0_b", "b_in", "b_o", "ln1_g", "ln1_b", "b_up", "ffn_conv_b", "b_down", "ln2_g", "ln2_b")
ORDER = ("ln0_g", "ln0_b", "w_in", "b_in", "conv_w", "w_a", "w_b", "w_o", "b_o", "ln1_g", "ln1_b", "w_up", "b_up",
         "ffn_conv_w", "ffn_conv_b", "w_down", "b_down", "ln2_g", "ln2_b")
SMALL_ORDER = ("loss",) + VECS + CONV


def _step(x, target, W, Mo, Vo):
    x2, t2 = x[0], target[0]
    big2 = {n: W[n][0] for n in BIG}
    halves = lambda a: a.astype(BF16).reshape(N_CORES, a.shape[0] // N_CORES, a.shape[1])
    whole = lambda g: g.reshape(N_CHIPS, g.shape[1] * g.shape[2], g.shape[3])
    later = tuple(n for n in BIG if n != "w_in")
    w_in_halves = halves(big2["w_in"])
    first = _allgather_shards([w_in_halves], [], name="allgather_w_in", collective_id=1)

    def launch_rest(h0b):
        return _allgather_shards([halves(big2[n]) for n in later] + [W[n] for n in CONV], [h0b],
                                 name="allgather_rest", collective_id=2)

    def assemble(rest):
        gathered = {n: whole(g) for n, g in zip(later + CONV, rest)}
        return {
            "w_up": gathered["w_up"],
            "w_a": gathered["w_a"].reshape(D_CONV, D_MODEL), "w_o": gathered["w_o"].reshape(D_MODEL, D_MODEL),
            "w_down": gathered["w_down"].reshape(D_FF, D_MODEL),
            "w_b": gathered["w_b"].transpose(1, 0, 2).reshape(GROUP_W, D_MODEL),
            "conv_w": gathered["conv_w"].transpose(1, 0, 2).reshape(3, D_CONV),
            "ffn_conv_w": gathered["ffn_conv_w"].transpose(1, 0, 2).reshape(3, D_FF),
        }

    pvec = {n: W[n].reshape(1, -1) for n in VECS}

    exchange_ids = iter((3, 4, 5))

    def exchange(group):
        names = tuple(group)
        res = _exchange_grads([group[n] for n in names], name="exchange_" + "_".join(names),
                              collective_id=next(exchange_ids))
        return dict(zip(names, res))

    grad_x, parts, small = _local_step(x2, t2, pvec, (whole(first[0]), launch_rest, assemble), exchange,
                                       before_ln0=[w_in_halves])
    out = {}
    for n in BIG:
        tr = {"w_in": 128, "w_up": 128, "w_b": 128}.get(n, big2[n].shape[0] // 4)
        g, d, nm, nv = _reduce_adamw(parts[n], big2[n], Mo[n][0], Vo[n][0], tr=tr, name="adamw_" + n)
        out[n] = tuple(a[None] for a in (g, d, nm, nv))

    vec, offs = _pack([small[n] for n in SMALL_ORDER])
    off = dict(zip(SMALL_ORDER, offs))
    row = lambda a: a.reshape(1, -1)
    allv = _allgather_small(vec, parts["w_in"])
    tot, vec_out = _reduce_adamw_vectors(allv, [off[n] for n in VECS], [row(W[n]) for n in VECS],
                                         [row(Mo[n]) for n in VECS], [row(Vo[n]) for n in VECS])
    for n, res in zip(VECS, vec_out):
        out[n] = tuple(a.reshape(W[n].shape) for a in res)
    loss = tot[0, off["loss"]]
    chip = 2 * lax.axis_index("x") + lax.axis_index("y")
    taps_g = []
    for n in CONV:
        width = W[n].shape[2]
        full = lax.slice(tot, (0, off[n]), (1, off[n] + 3 * N_CHIPS * width)).reshape(3, N_CHIPS * width)
        taps_g.append(lax.dynamic_slice_in_dim(full, chip * width, width, axis=1))
    taps_out = _adamw_taps([W[n][0] for n in CONV], taps_g, [Mo[n][0] for n in CONV], [Vo[n][0] for n in CONV])
    for n, g, res in zip(CONV, taps_g, taps_out):
        out[n] = tuple(a[None] for a in (g,) + res)

    res = [loss, grad_x[None]]
    for k in range(4):
        res += [out[n][k] for n in ORDER]
    return tuple(res)


def kernel(x, ln0_g, ln0_b, w_in, b_in, conv_w, w_a, w_b, w_o, b_o, ln1_g, ln1_b, w_up, b_up, ffn_conv_w, ffn_conv_b, w_down, b_down, ln2_g, ln2_b, loss_target, m_ln0_g, m_ln0_b, m_w_in, m_b_in, m_conv_w, m_w_a, m_w_b, m_w_o, m_b_o, m_ln1_g, m_ln1_b, m_w_up, m_b_up, m_ffn_conv_w, m_ffn_conv_b, m_w_down, m_b_down, m_ln2_g, m_ln2_b, v_ln0_g, v_ln0_b, v_w_in, v_b_in, v_conv_w, v_w_a, v_w_b, v_w_o, v_b_o, v_ln1_g, v_ln1_b, v_w_up, v_b_up, v_ffn_conv_w, v_ffn_conv_b, v_w_down, v_b_down, v_ln2_g, v_ln2_b):
    W = dict(zip(ORDER, (ln0_g, ln0_b, w_in, b_in, conv_w, w_a, w_b, w_o, b_o, ln1_g, ln1_b, w_up, b_up,
                         ffn_conv_w, ffn_conv_b, w_down, b_down, ln2_g, ln2_b)))
    Mo = dict(zip(ORDER, (m_ln0_g, m_ln0_b, m_w_in, m_b_in, m_conv_w, m_w_a, m_w_b, m_w_o, m_b_o, m_ln1_g, m_ln1_b,
                          m_w_up, m_b_up, m_ffn_conv_w, m_ffn_conv_b, m_w_down, m_b_down, m_ln2_g, m_ln2_b)))
    Vo = dict(zip(ORDER, (v_ln0_g, v_ln0_b, v_w_in, v_b_in, v_conv_w, v_w_a, v_w_b, v_w_o, v_b_o, v_ln1_g, v_ln1_b,
                          v_w_up, v_b_up, v_ffn_conv_w, v_ffn_conv_b, v_w_down, v_b_down, v_ln2_g, v_ln2_b)))
    return _step(x, loss_target, W, Mo, Vo)
```

```python
import functools
import math

import jax
import jax.numpy as jnp
from jax import lax
from jax.experimental import pallas as pl
from jax.experimental.pallas import tpu as pltpu
from jax.experimental.pallas import tpu_sc as plsc

F32 = jnp.float32
BF16 = jnp.bfloat16

D_MODEL = 1024
D_CONV = D_MODEL
HEAD_DIM = 64
HEADS_PER_GROUP = 8
GROUPS = ((128, 1), (512, 4), (2048, 16))
N_GROUPS = len(GROUPS)
GROUP_W = HEADS_PER_GROUP * HEAD_DIM
QKV_W = N_GROUPS * GROUP_W
RADIUS = 64
D_FF = 2816
LN_EPS = 1e-5
ALPHA = 2.0 ** 0.25
MASK_VALUE = -1e30
ATT_SCALE = HEAD_DIM ** -0.5
OFF_B = 0
OFF_C = OFF_B + D_CONV
OFF_H = OFF_C + D_CONV
OFF_Q = OFF_H + D_CONV
OFF_K = OFF_Q + QKV_W
OFF_V = OFF_K + QKV_W
OFF_GA = OFF_V + QKV_W
OFF_GB = OFF_GA + D_MODEL
N_IN = OFF_GB + D_MODEL
ADAM_LR = 0.001
ADAM_B1 = 0.9
ADAM_B2 = 0.999
ADAM_EPS = 1e-08
ADAM_WD = 0.01
ADAM_STEP = 10
INV_SQRT2 = 0.7071067811865476
INV_SQRT_2PI = 0.3989422804014327

LANES = 128
SUBLANES = 8
VMEM_BYTES_V7X = 64 * 1024 * 1024
N_CHIPS = 4
N_CORES = 2
N_DEV = N_CHIPS * N_CORES
MESH = pl.DeviceIdType.MESH

N_BLK = N_IN // GROUP_W
PERM = (0, 1, 2, 3, 4, 5, 15, 16, 17, 18, 6, 9, 12, 7, 10, 13, 8, 11, 14)
INV_PERM = tuple(PERM.index(b) for b in range(N_BLK))
P_B, P_C, P_H, P_GA, P_GB, P_Q0 = 0, 1024, 2048, 3072, 4096, 5120
N_NAT = P_Q0 + QKV_W // N_GROUPS * 3
N_GATED = P_Q0

def _col_runs():
    shard_w = N_IN // N_CHIPS
    runs = []
    for pos, blk in enumerate(PERM):
        c, end = blk * GROUP_W, (blk + 1) * GROUP_W
        while c < end:
            stop = min(end, (c // shard_w + 1) * shard_w)
            runs.append((c // shard_w, c % shard_w, pos * GROUP_W + c - blk * GROUP_W, stop - c))
            c = stop
    return runs


SLAB = 128
CHUNK = 256
PAD = SUBLANES
TQ = 128


def _cparams(sem, vmem_mb):
    assert vmem_mb * 1024 * 1024 < VMEM_BYTES_V7X
    return pltpu.CompilerParams(dimension_semantics=sem, vmem_limit_bytes=vmem_mb * 1024 * 1024)


def _resident(shape):
    nd = len(shape)
    return pl.BlockSpec(shape, lambda *_: (0,) * nd, pipeline_mode=pl.Buffered(1))


def _hbm(*arrays):
    return [pltpu.with_memory_space_constraint(a, pltpu.HBM) for a in arrays]


def _dot(a, b):
    return jnp.dot(a, b, preferred_element_type=F32)


def _dot_nt(a, b):
    return lax.dot_general(a, b, (((1,), (1,)), ((), ())), preferred_element_type=F32)


def _dot_tn(a, b):
    return lax.dot_general(a, b, (((0,), (0,)), ((), ())), preferred_element_type=F32)


def _ln_stats(z):
    mu = jnp.mean(z, -1, keepdims=True)
    zc = z - mu
    var = jnp.mean(zc * zc, -1, keepdims=True)
    rstd = lax.rsqrt(var + LN_EPS)
    return zc * rstd, rstd


def _ln_bwd(dh, xhat, rstd, g):
    dxh = dh * g
    m1 = jnp.mean(dxh, -1, keepdims=True)
    m2 = jnp.mean(dxh * xhat, -1, keepdims=True)
    return rstd * (dxh - m1 - xhat * m2)


def _rows8(rows, width):
    pad = [jnp.zeros((1, width), F32)] * (SUBLANES - len(rows))
    return jnp.concatenate(list(rows) + pad, axis=0)


def _mm_nn(a, w, bias, *, tm, tn, out_dtype, name, vmem_mb=40):
    M, K = a.shape
    if w.ndim == 3:
        assert w.shape[2] == tn
        n_tiles = w.shape[0]
        w_spec = pl.BlockSpec((None, K, tn), lambda j, i: (j, 0, 0))
    else:
        n_tiles = w.shape[1] // tn
        w_spec = pl.BlockSpec((K, tn), lambda j, i: (0, j))

    def body(a_ref, w_ref, b_ref, o_ref):
        o_ref[...] = (_dot(a_ref[...], w_ref[...]) + b_ref[...]).astype(o_ref.dtype)

    return pl.pallas_call(
        body, grid=(n_tiles, M // tm),
        in_specs=[pl.BlockSpec((tm, K), lambda j, i: (i, 0)), w_spec, pl.BlockSpec((1, tn), lambda j, i: (0, j))],
        out_specs=pl.BlockSpec((tm, tn), lambda j, i: (i, j)),
        out_shape=jax.ShapeDtypeStruct((M, n_tiles * tn), out_dtype),
        name=name, compiler_params=_cparams(("arbitrary", "parallel"), vmem_mb))(*_hbm(a, w, bias))


def _mm_nt(a, w, *, tm, a_col=0, name, vmem_mb=40):
    M = a.shape[0]
    N, K = w.shape

    def body(a_ref, w_ref, o_ref):
        o_ref[...] = _dot_nt(a_ref[...], w_ref[...]).astype(o_ref.dtype)

    return pl.pallas_call(
        body, grid=(M // tm,),
        in_specs=[pl.BlockSpec((tm, K), lambda i: (i, a_col)),
                  pl.BlockSpec((N, K), lambda i: (0, 0))],
        out_specs=pl.BlockSpec((tm, N), lambda i: (i, 0)),
        out_shape=jax.ShapeDtypeStruct((M, N), BF16),
        name=name, compiler_params=_cparams(("parallel",), vmem_mb))(*_hbm(a, w))


def _mm_tn(a, g, *, n_out, tn, ts, g_block, g_map, colsum=False, name, vmem_mb=48):
    S, K = a.shape
    n_s = S // ts

    def body(a_ref, g_ref, *rest):
        if colsum:
            o_ref, cs_ref, acc_ref, cacc_ref = rest
        else:
            o_ref, acc_ref = rest
        s = pl.program_id(1)

        @pl.when(s == 0)
        def _():
            acc_ref[...] = jnp.zeros_like(acc_ref)
            if colsum:
                cacc_ref[...] = jnp.zeros_like(cacc_ref)

        gv = g_ref[...]
        acc_ref[...] += _dot_tn(a_ref[...], gv)
        if colsum:
            cacc_ref[...] += jnp.broadcast_to(jnp.sum(gv.astype(F32), axis=0, keepdims=True), cacc_ref.shape)

        @pl.when(s == n_s - 1)
        def _():
            o_ref[...] = acc_ref[...].astype(o_ref.dtype)
            if colsum:
                cs_ref[...] = cacc_ref[...]

    out_specs = [pl.BlockSpec((None, K, tn), lambda j, s: (j, 0, 0))]
    out_shape = [jax.ShapeDtypeStruct((n_out, K, tn), BF16)]
    scratch = [pltpu.VMEM((K, tn), F32)]
    if colsum:
        out_specs.append(pl.BlockSpec((SUBLANES, tn), lambda j, s: (0, j)))
        out_shape.append(jax.ShapeDtypeStruct((SUBLANES, n_out * tn), F32))
        scratch.append(pltpu.VMEM((SUBLANES, tn), F32))
    res = pl.pallas_call(
        body, grid=(n_out, n_s),
        in_specs=[pl.BlockSpec((ts, K), lambda j, s: (s, 0)), pl.BlockSpec(g_block, g_map)],
        out_specs=out_specs, out_shape=out_shape, scratch_shapes=scratch,
        name=name, compiler_params=_cparams(("parallel", "arbitrary"), vmem_mb))(*_hbm(a, g))
    return res if colsum else res[0]


def _mm_tn_cat(a, gs, *, ts, name, vmem_mb=40):
    S, K = a.shape
    widths = [g.shape[1] for g in gs]
    n_s, total = S // ts, sum(widths)

    def body(*refs):
        a_ref, g_refs = refs[0], refs[1:1 + len(gs)]
        o_ref, cs_ref, acc_ref, cacc_ref = refs[1 + len(gs):]
        s = pl.program_id(0)

        @pl.when(s == 0)
        def _():
            acc_ref[...] = jnp.zeros_like(acc_ref)
            cacc_ref[...] = jnp.zeros_like(cacc_ref)

        av, col = a_ref[...], 0
        for g_ref, w in zip(g_refs, widths):
            gv = g_ref[...]
            acc_ref[:, col:col + w] += _dot_tn(av, gv)
            cacc_ref[:, col:col + w] += jnp.broadcast_to(jnp.sum(gv.astype(F32), axis=0, keepdims=True), (SUBLANES, w))
            col += w

        @pl.when(s == n_s - 1)
        def _():
            o_ref[...] = acc_ref[...].astype(BF16)
            cs_ref[...] = cacc_ref[...]

    return pl.pallas_call(
        body, grid=(n_s,),
        in_specs=[pl.BlockSpec((ts, K), lambda s: (s, 0))] + [pl.BlockSpec((ts, w), lambda s: (s, 0)) for w in widths],
        out_specs=[pl.BlockSpec((K, total), lambda s: (0, 0)), pl.BlockSpec((SUBLANES, total), lambda s: (0, 0))],
        out_shape=[jax.ShapeDtypeStruct((K, total), BF16), jax.ShapeDtypeStruct((SUBLANES, total), F32)],
        scratch_shapes=[pltpu.VMEM((K, total), F32), pltpu.VMEM((SUBLANES, total), F32)],
        name=name, compiler_params=_cparams(("arbitrary",), vmem_mb))(*_hbm(a, *gs))


DILS = tuple(d for _, d in GROUPS if d > 1)


def _res_spec(d, tm, width):
    return pl.BlockSpec((d, tm // d, width), lambda i: (0, i, 0))


def _lane_scratch(tm, width):
    return [pltpu.VMEM((tm, LANES), F32)] * (width // LANES)


def _to_residue(val, dst_refs, dils, tm, dtype, scr):
    for c, ref in enumerate(scr):
        ref[...] = val[:, c * LANES:(c + 1) * LANES]
    for dst_ref, d in zip(dst_refs, dils):
        for r in range(d):
            cols = [ref[pl.ds(r, tm // d, stride=d), :] for ref in scr]
            dst_ref[r] = jnp.concatenate(cols, axis=1).astype(dtype)


def _from_residue(rows_of, d, tm, scr):
    for r in range(d):
        v = rows_of(r).astype(F32)
        for c, ref in enumerate(scr):
            ref[pl.ds(r, tm // d, stride=d), :] = v[:, c * LANES:(c + 1) * LANES]
    return jnp.concatenate([ref[...] for ref in scr], axis=1)


def _ln0_fwd(x, g, b, after=(), *, tm=512):
    S, Dm = x.shape
    n_after = len(after)

    def body(x_ref, g_ref, b_ref, *rest):
        h_ref, hb_ref, *rest = rest[n_after:]
        xhat, _ = _ln_stats(x_ref[...])
        h = xhat * g_ref[...] + b_ref[...]
        h_ref[...] = h
        hb_ref[...] = h.astype(BF16)
        _to_residue(h, rest[:len(DILS)], DILS, tm, BF16, rest[len(DILS):])

    row = pl.BlockSpec((tm, Dm), lambda i: (i, 0))
    vec = pl.BlockSpec((1, Dm), lambda i: (0, 0))
    return pl.pallas_call(
        body, grid=(S // tm,), in_specs=[row, vec, vec] + [pl.BlockSpec(memory_space=pl.ANY)] * n_after,
        out_specs=[row, row] + [_res_spec(d, tm, Dm) for d in DILS],
        out_shape=[jax.ShapeDtypeStruct((S, Dm), F32), jax.ShapeDtypeStruct((S, Dm), BF16)]
        + [jax.ShapeDtypeStruct((d, S // d, Dm), BF16) for d in DILS],
        scratch_shapes=_lane_scratch(tm, Dm),
        name="ln0_fwd", compiler_params=_cparams(("parallel",), 32))(*_hbm(x, g, b), *after)


def _slab_spec(S, col0):
    return pl.BlockSpec((S, SLAB), lambda j: (0, col0 // SLAB + j))


def _zero_pads(scr, S):
    scr[0:PAD, :] = jnp.zeros((PAD, SLAB), F32)
    scr[S + PAD:S + 2 * PAD, :] = jnp.zeros((PAD, SLAB), F32)


def _shifted(scr, t):
    return (scr[PAD - 1 + t:PAD - 1 + t + CHUNK, :], scr[PAD + t:PAD + t + CHUNK, :],
            scr[PAD + 1 + t:PAD + 1 + t + CHUNK, :])


def _conv_gate_fwd(proj, conv_w):
    S = proj.shape[0]

    def body(b_ref, c_ref, h_ref, w_ref, o_ref, u_scr):
        _zero_pads(u_scr, S)
        for t in range(0, S, CHUNK):
            u_scr[PAD + t:PAD + t + CHUNK, :] = c_ref[t:t + CHUNK, :].astype(F32) * h_ref[t:t + CHUNK, :].astype(F32)
        w0, w1, w2 = w_ref[0:1, :], w_ref[1:2, :], w_ref[2:3, :]
        for t in range(0, S, CHUNK):
            um, u0, up = _shifted(u_scr, t)
            cv = w0 * um + w1 * u0 + w2 * up
            o_ref[t:t + CHUNK, :] = (b_ref[t:t + CHUNK, :].astype(F32) * cv).astype(BF16)

    return pl.pallas_call(
        body, grid=(D_CONV // SLAB,),
        in_specs=[_slab_spec(S, P_B), _slab_spec(S, P_C), _slab_spec(S, P_H),
                  pl.BlockSpec((3, SLAB), lambda j: (0, j))],
        out_specs=pl.BlockSpec((S, SLAB), lambda j: (0, j)),
        out_shape=jax.ShapeDtypeStruct((S, D_CONV), BF16),
        scratch_shapes=[pltpu.VMEM((S + 2 * PAD, SLAB), F32)],
        name="conv_gate_fwd", compiler_params=_cparams(("parallel",), 40))(*_hbm(proj, proj, proj, conv_w))


MASKED_DISTANCE = -1e34


def _attn_bias_table(g):
    dil = GROUPS[g][1]
    j = lax.broadcasted_iota(jnp.int32, (2 * TQ, TQ), 0)
    a = lax.broadcasted_iota(jnp.int32, (2 * TQ, TQ), 1)
    rel = jnp.abs(j - RADIUS - a)
    base = -(rel * dil).astype(F32)
    inside, after_start, before_end = rel <= RADIUS, j >= RADIUS, j < TQ + RADIUS
    variants = []
    for first, last in ((False, False), (True, False), (False, True), (True, True)):
        valid = inside & (after_start if first else True) & (before_end if last else True)
        variants.append(jnp.where(valid, base, MASKED_DISTANCE))
    return jnp.stack(variants)


def _bias_spec(nb):
    def variant(r, i):
        return (jnp.where(i == 0, 1, 0) + jnp.where(i == nb - 1, 2, 0), 0, 0)
    return pl.BlockSpec((None, 2 * TQ, TQ), variant)


def _head_stats(rows):
    pad = jnp.zeros((LANES - len(rows), TQ), F32)
    return jnp.concatenate(list(rows) + [pad], axis=0).T


def _slope(g, h):
    return 2.0 ** (-8.0 * (g * HEADS_PER_GROUP + h + 1) / (N_GROUPS * HEADS_PER_GROUP))


def _window(p_ref, c_ref, n_ref):
    return jnp.concatenate([p_ref[TQ - RADIUS:, :], c_ref[...], n_ref[:RADIUS, :]], axis=0)


def _pair(a, h):
    return a[:, (h // 2) * LANES:(h // 2 + 1) * LANES]


def _own_lanes(a, h):
    lane = lax.broadcasted_iota(jnp.int32, a.shape, 1)
    return jnp.where((lane >= HEAD_DIM) == (h % 2 == 1), a, jnp.zeros_like(a))


def _own_rows(a, h):
    return a[(h % 2) * HEAD_DIM:(h % 2 + 1) * HEAD_DIM, :]


def _qkv_specs(nb, col0):
    def spec(col, shift):
        return pl.BlockSpec((None, TQ, GROUP_W), lambda r, i: (r, jnp.clip(i + shift, 0, nb - 1), col))

    return [spec(col0, 0), spec(col0 + 1, -1), spec(col0 + 1, 0), spec(col0 + 1, 1),
            spec(col0 + 2, -1), spec(col0 + 2, 0), spec(col0 + 2, 1)]


def _attn_fwd(qkv, col0, g):
    dil, sub, _ = qkv.shape
    nb = sub // TQ

    def body(q_ref, kp, kc, kn, vp, vc, vn, bias_ref, o_ref, lse_ref, ot_scr, s_scr, p_scr):
        kwin = _window(kp, kc, kn)
        vwin = _window(vp, vc, vn)
        q = q_ref[...] * ATT_SCALE
        for h in range(HEADS_PER_GROUP):
            s_scr[h] = _dot_nt(_pair(kwin, h), _own_lanes(_pair(q, h), h))
        lse, inv_den = [], []
        for h in range(HEADS_PER_GROUP):
            s = s_scr[h] + _slope(g, h) * bias_ref[...]
            m = jnp.max(s, axis=0, keepdims=True)
            p = jnp.exp(s - m)
            den = jnp.sum(p, axis=0, keepdims=True)
            p_scr[h] = p.astype(BF16)
            inv_den.append(1.0 / den)
            lse.append(m + jnp.log(den))
        for h in range(HEADS_PER_GROUP):
            ot = _dot_tn(_pair(vwin, h), p_scr[h])
            ot_scr[h * HEAD_DIM:(h + 1) * HEAD_DIM, :] = _own_rows(ot, h) * inv_den[h]
        o_ref[...] = ot_scr[...].T
        lse_ref[...] = _head_stats(lse)

    return pl.pallas_call(
        body, grid=(dil, nb), in_specs=_qkv_specs(nb, col0) + [_bias_spec(nb)],
        out_specs=[pl.BlockSpec((None, TQ, GROUP_W), lambda r, i: (r, i, 0)),
                   pl.BlockSpec((None, TQ, LANES), lambda r, i: (r, i, 0))],
        out_shape=[jax.ShapeDtypeStruct((dil, sub, GROUP_W), F32), jax.ShapeDtypeStruct((dil, sub, LANES), F32)],
        scratch_shapes=[pltpu.VMEM((GROUP_W, TQ), F32), pltpu.VMEM((HEADS_PER_GROUP, 2 * TQ, TQ), F32),
                        pltpu.VMEM((HEADS_PER_GROUP, 2 * TQ, TQ), BF16)],
        name=f"attn_fwd_g{g}", compiler_params=_cparams(("parallel", "arbitrary"), 32))(
            *_hbm(*([qkv] * 7), _attn_bias_table(g)))


def _expand_heads():
    h = lax.broadcasted_iota(jnp.int32, (LANES, GROUP_W), 0)
    c = lax.broadcasted_iota(jnp.int32, (LANES, GROUP_W), 1)
    return (c // HEAD_DIM == h).astype(F32)


def _dot_f32(a, b):
    return jnp.dot(a, b, preferred_element_type=F32, precision=lax.Precision.HIGHEST)


def _attn_combine(outs, lses, *, tm=512):
    S = outs[0].shape[1]
    n_col = GROUP_W // LANES

    def body(*refs):
        ins, e_ref = refs[:2 * N_GROUPS], refs[2 * N_GROUPS]
        c_ref, cb_ref, lt_ref = refs[2 * N_GROUPS + 1:2 * N_GROUPS + 4]
        scr = refs[2 * N_GROUPS + 4:]
        o, l = [ins[0][0]], [ins[N_GROUPS][0]]
        for k, d in enumerate(DILS):
            o_ref, l_ref = ins[1 + k], ins[N_GROUPS + 1 + k]
            o.append(_from_residue(lambda r: o_ref[r], d, tm, scr[k * (n_col + 1):k * (n_col + 1) + n_col]))
            l.append(_from_residue(lambda r: l_ref[r], d, tm, scr[k * (n_col + 1) + n_col:(k + 1) * (n_col + 1)]))
        m = jnp.maximum(jnp.maximum(l[0], l[1]), l[2])
        e = [jnp.exp(v - m) for v in l]
        den = e[0] + e[1] + e[2]
        comb = sum(_dot_f32(ev / den, e_ref[...]) * ov for ev, ov in zip(e, o))
        c_ref[...] = comb
        cb_ref[...] = comb.astype(BF16)
        lt_ref[...] = m + jnp.log(den)

    row = pl.BlockSpec((tm, GROUP_W), lambda i: (i, 0))
    dils = [d for _, d in GROUPS]
    return pl.pallas_call(
        body, grid=(S // tm,),
        in_specs=[_res_spec(d, tm, GROUP_W) for d in dils] + [_res_spec(d, tm, LANES) for d in dils]
        + [_resident((LANES, GROUP_W))],
        out_specs=[row, row, pl.BlockSpec((tm, LANES), lambda i: (i, 0))],
        out_shape=[jax.ShapeDtypeStruct((S, GROUP_W), F32), jax.ShapeDtypeStruct((S, GROUP_W), BF16),
                   jax.ShapeDtypeStruct((S, LANES), F32)],
        scratch_shapes=_lane_scratch(tm, GROUP_W + LANES) * len(DILS),
        name="attn_combine", compiler_params=_cparams(("parallel",), 32))(*_hbm(*outs, *lses, _expand_heads()))


def _branch_mix(ya_in, comb_b, w_a, w_b, proj, *, tm=512):
    S = ya_in.shape[0]

    def body(ya_ref, cb_ref, wa_ref, wb_ref, ga_ref, gb_ref, yab_ref, mx_ref):
        y_a = _dot(ya_ref[...], wa_ref[...])
        y_b = _dot(cb_ref[...], wb_ref[...])
        yab_ref[:, 0:D_MODEL] = y_a.astype(BF16)
        yab_ref[:, D_MODEL:2 * D_MODEL] = y_b.astype(BF16)
        mx = jax.nn.sigmoid(ga_ref[...].astype(F32)) * y_a + jax.nn.sigmoid(gb_ref[...].astype(F32)) * y_b
        mx_ref[...] = mx.astype(BF16)

    return pl.pallas_call(
        body, grid=(S // tm,),
        in_specs=[pl.BlockSpec((tm, D_CONV), lambda i: (i, 0)), pl.BlockSpec((tm, GROUP_W), lambda i: (i, 0)),
                  pl.BlockSpec((D_CONV, D_MODEL), lambda i: (0, 0)), pl.BlockSpec((GROUP_W, D_MODEL), lambda i: (0, 0)),
                  pl.BlockSpec((tm, D_MODEL), lambda i: (i, P_GA // D_MODEL)),
                  pl.BlockSpec((tm, D_MODEL), lambda i: (i, P_GB // D_MODEL))],
        out_specs=[pl.BlockSpec((tm, 2 * D_MODEL), lambda i: (i, 0)), pl.BlockSpec((tm, D_MODEL), lambda i: (i, 0))],
        out_shape=[jax.ShapeDtypeStruct((S, 2 * D_MODEL), BF16), jax.ShapeDtypeStruct((S, D_MODEL), BF16)],
        name="branch_mix", compiler_params=_cparams(("parallel",), 40))(*_hbm(ya_in, comb_b, w_a, w_b, proj, proj))


def _mix_ln1(mixin, w_o, b_o, h0, g1, b1, *, tm=512):
    S = mixin.shape[0]

    def body(mx_ref, wo_ref, bo_ref, h0_ref, g_ref, b_ref, xh_ref, rs_ref, h1b_ref):
        z = ALPHA * h0_ref[...] + _dot(mx_ref[...], wo_ref[...]) + bo_ref[...]
        xhat, rstd = _ln_stats(z)
        xh_ref[...] = xhat
        rs_ref[...] = jnp.broadcast_to(rstd, (tm, LANES))
        h1b_ref[...] = (xhat * g_ref[...] + b_ref[...]).astype(BF16)

    row = pl.BlockSpec((tm, D_MODEL), lambda i: (i, 0))
    vec = pl.BlockSpec((1, D_MODEL), lambda i: (0, 0))
    return pl.pallas_call(
        body, grid=(S // tm,),
        in_specs=[row, pl.BlockSpec((D_MODEL, D_MODEL), lambda i: (0, 0)), vec, row, vec, vec],
        out_specs=[row, pl.BlockSpec((tm, LANES), lambda i: (i, 0)), row],
        out_shape=[jax.ShapeDtypeStruct((S, D_MODEL), F32), jax.ShapeDtypeStruct((S, LANES), F32),
                   jax.ShapeDtypeStruct((S, D_MODEL), BF16)],
        name="mix_ln1", compiler_params=_cparams(("parallel",), 40))(*_hbm(mixin, w_o, b_o, h0, g1, b1))


def _gelu_parts(cz):
    cdf = 0.5 * (1.0 + lax.erf(cz * INV_SQRT2))
    return cdf, cz * cdf


def _ffn_conv_fwd(up, cw, cb):
    S = up.shape[0]

    def body(a_ref, g_ref, w_ref, cb_ref, o_ref, a_scr):
        _zero_pads(a_scr, S)
        for t in range(0, S, CHUNK):
            a_scr[PAD + t:PAD + t + CHUNK, :] = a_ref[t:t + CHUNK, :].astype(F32)
        w0, w1, w2 = w_ref[0:1, :], w_ref[1:2, :], w_ref[2:3, :]
        for t in range(0, S, CHUNK):
            am, a0, ap = _shifted(a_scr, t)
            _, gel = _gelu_parts(w0 * am + w1 * a0 + w2 * ap + cb_ref[...])
            o_ref[t:t + CHUNK, :] = (gel * g_ref[t:t + CHUNK, :].astype(F32)).astype(BF16)

    return pl.pallas_call(
        body, grid=(D_FF // SLAB,),
        in_specs=[_slab_spec(S, 0), _slab_spec(S, D_FF), pl.BlockSpec((3, SLAB), lambda j: (0, j)),
                  pl.BlockSpec((1, SLAB), lambda j: (0, j))],
        out_specs=pl.BlockSpec((S, SLAB), lambda j: (0, j)),
        out_shape=jax.ShapeDtypeStruct((S, D_FF), BF16),
        scratch_shapes=[pltpu.VMEM((S + 2 * PAD, SLAB), F32)],
        name="ffn_conv_fwd", compiler_params=_cparams(("parallel",), 40))(*_hbm(up, up, cw, cb))


def _down_ln2_loss(f, w_down, b_down, xhat1, g1, b1, g2, b2, target, *, tm=512):
    S = f.shape[0]

    def body(f_ref, wd_ref, bd_ref, xh1_ref, g1_ref, b1_ref, g2_ref, b2_ref, t_ref, dz_ref, dzb_ref, st_ref):
        h1 = xh1_ref[...] * g1_ref[...] + b1_ref[...]
        z = ALPHA * h1 + _dot(f_ref[...], wd_ref[...]) + bd_ref[...]
        xhat, rstd = _ln_stats(z)
        err = xhat * g2_ref[...] + b2_ref[...] - t_ref[...]
        loss = (0.5 / D_MODEL) * jnp.sum(jnp.sum(err * err, axis=1, keepdims=True), axis=0, keepdims=True)
        dh2 = err * (1.0 / D_MODEL)
        dz = _ln_bwd(dh2, xhat, rstd, g2_ref[...])
        dz_ref[...] = dz
        dzb_ref[...] = dz.astype(BF16)
        upd = _rows8([jnp.sum(dh2 * xhat, axis=0, keepdims=True), jnp.sum(dh2, axis=0, keepdims=True),
                      jnp.broadcast_to(loss, (1, D_MODEL)), jnp.sum(dz, axis=0, keepdims=True)], D_MODEL)

        @pl.when(pl.program_id(0) == 0)
        def _():
            st_ref[...] = upd

        @pl.when(pl.program_id(0) != 0)
        def _():
            st_ref[...] += upd

    row = pl.BlockSpec((tm, D_MODEL), lambda i: (i, 0))
    vec = pl.BlockSpec((1, D_MODEL), lambda i: (0, 0))
    return pl.pallas_call(
        body, grid=(S // tm,),
        in_specs=[pl.BlockSpec((tm, D_FF), lambda i: (i, 0)), _resident((D_FF, D_MODEL)),
                  vec, row, vec, vec, vec, vec, row],
        out_specs=[row, row, pl.BlockSpec((SUBLANES, D_MODEL), lambda i: (0, 0))],
        out_shape=[jax.ShapeDtypeStruct((S, D_MODEL), F32), jax.ShapeDtypeStruct((S, D_MODEL), BF16),
                   jax.ShapeDtypeStruct((SUBLANES, D_MODEL), F32)],
        name="down_ln2_loss", compiler_params=_cparams(("arbitrary",), 56))(
            *_hbm(f, w_down, b_down, xhat1, g1, b1, g2, b2, target))


def _ffn_conv_bwd(up, df, cw, cb):
    S = up.shape[0]

    def body(a_ref, g_ref, df_ref, w_ref, cb_ref, dup_ref, sm_ref, a_scr, d_scr):
        _zero_pads(a_scr, S)
        _zero_pads(d_scr, S)
        for t in range(0, S, CHUNK):
            a_scr[PAD + t:PAD + t + CHUNK, :] = a_ref[t:t + CHUNK, :].astype(F32)
        w0, w1, w2 = w_ref[0:1, :], w_ref[1:2, :], w_ref[2:3, :]
        zero = jnp.zeros((1, SLAB), F32)
        s_dg, s_dcz, s_w0, s_w1, s_w2 = zero, zero, zero, zero, zero
        for t in range(0, S, CHUNK):
            am, a0, ap = _shifted(a_scr, t)
            cz = w0 * am + w1 * a0 + w2 * ap + cb_ref[...]
            cdf, gel = _gelu_parts(cz)
            dfv = df_ref[t:t + CHUNK, :].astype(F32)
            dgte = dfv * gel
            dcz = dfv * g_ref[t:t + CHUNK, :].astype(F32) * (cdf + cz * jnp.exp(-0.5 * cz * cz) * INV_SQRT_2PI)
            dup_ref[1, t:t + CHUNK, :] = dgte.astype(BF16)
            d_scr[PAD + t:PAD + t + CHUNK, :] = dcz
            s_dg = s_dg + jnp.sum(dgte, axis=0, keepdims=True)
            s_dcz = s_dcz + jnp.sum(dcz, axis=0, keepdims=True)
            s_w0 = s_w0 + jnp.sum(dcz * am, axis=0, keepdims=True)
            s_w1 = s_w1 + jnp.sum(dcz * a0, axis=0, keepdims=True)
            s_w2 = s_w2 + jnp.sum(dcz * ap, axis=0, keepdims=True)
        s_da = zero
        for t in range(0, S, CHUNK):
            dm, d0, dp = _shifted(d_scr, t)
            da = w0 * dp + w1 * d0 + w2 * dm
            dup_ref[0, t:t + CHUNK, :] = da.astype(BF16)
            s_da = s_da + jnp.sum(da, axis=0, keepdims=True)
        sm_ref[...] = _rows8([s_da, s_dg, s_dcz, s_w0, s_w1, s_w2], SLAB)

    return pl.pallas_call(
        body, grid=(D_FF // SLAB,),
        in_specs=[_slab_spec(S, 0), _slab_spec(S, D_FF), pl.BlockSpec((S, SLAB), lambda j: (0, j)),
                  pl.BlockSpec((3, SLAB), lambda j: (0, j)), pl.BlockSpec((1, SLAB), lambda j: (0, j))],
        out_specs=[pl.BlockSpec((2, S, SLAB), lambda j: (0, 0, j)), pl.BlockSpec((SUBLANES, SLAB), lambda j: (0, j))],
        out_shape=[jax.ShapeDtypeStruct((2, S, D_FF), BF16), jax.ShapeDtypeStruct((SUBLANES, D_FF), F32)],
        scratch_shapes=[pltpu.VMEM((S + 2 * PAD, SLAB), F32)] * 2,
        name="ffn_conv_bwd", compiler_params=_cparams(("parallel",), 48))(*_hbm(up, up, df, cw, cb))


def _up_bwd_ln1(dup, w_up3, dz2, xhat1, rstd1, g1, *, tm=512):
    S = dz2.shape[0]
    ns, _, tk = w_up3.shape
    per_plane = D_FF // tk

    def body(du_ref, w_ref, dz2_ref, xh_ref, rs_ref, g_ref, dz_ref, dzb_ref, st_ref):
        dh = ALPHA * dz2_ref[...]
        for k in range(ns):
            col = (k % per_plane) * tk
            dh = dh + _dot_nt(du_ref[k // per_plane, :, col:col + tk], w_ref[k])
        xhat = xh_ref[...]
        dz = _ln_bwd(dh, xhat, rs_ref[:, 0:1], g_ref[...])
        dz_ref[...] = dz
        dzb_ref[...] = dz.astype(BF16)
        upd = _rows8([jnp.sum(dh * xhat, axis=0, keepdims=True), jnp.sum(dh, axis=0, keepdims=True),
                      jnp.sum(dz, axis=0, keepdims=True)], D_MODEL)

        @pl.when(pl.program_id(0) == 0)
        def _():
            st_ref[...] = upd

        @pl.when(pl.program_id(0) != 0)
        def _():
            st_ref[...] += upd

    row = pl.BlockSpec((tm, D_MODEL), lambda i: (i, 0))
    return pl.pallas_call(
        body, grid=(S // tm,),
        in_specs=[pl.BlockSpec((dup.shape[0], tm, D_FF), lambda i: (0, i, 0)), _resident(w_up3.shape),
                  row, row, pl.BlockSpec((tm, LANES), lambda i: (i, 0)), pl.BlockSpec((1, D_MODEL), lambda i: (0, 0))],
        out_specs=[row, row, pl.BlockSpec((SUBLANES, D_MODEL), lambda i: (0, 0))],
        out_shape=[jax.ShapeDtypeStruct((S, D_MODEL), F32), jax.ShapeDtypeStruct((S, D_MODEL), BF16),
                   jax.ShapeDtypeStruct((SUBLANES, D_MODEL), F32)],
        name="up_bwd_ln1", compiler_params=_cparams(("arbitrary",), 56))(*_hbm(dup, w_up3, dz2, xhat1, rstd1, g1))


def _mix_bwd(dz1b, w_o, proj, yab, *, tm=512):
    S = dz1b.shape[0]

    def body(dz_ref, wo_ref, ga_ref, gb_ref, y_ref, dy_ref, dg_ref):
        dmx = _dot_nt(dz_ref[...], wo_ref[...])
        for k, gt_ref in enumerate((ga_ref, gb_ref)):
            sl = slice(k * D_MODEL, (k + 1) * D_MODEL)
            sg = jax.nn.sigmoid(gt_ref[...].astype(F32))
            dy_ref[:, sl] = (dmx * sg).astype(BF16)
            dg_ref[k] = (dmx * y_ref[:, sl].astype(F32) * sg * (1.0 - sg)).astype(BF16)

    row = pl.BlockSpec((tm, D_MODEL), lambda i: (i, 0))
    wide = pl.BlockSpec((tm, 2 * D_MODEL), lambda i: (i, 0))
    return pl.pallas_call(
        body, grid=(S // tm,),
        in_specs=[row, _resident(w_o.shape), pl.BlockSpec((tm, D_MODEL), lambda i: (i, P_GA // D_MODEL)),
                  pl.BlockSpec((tm, D_MODEL), lambda i: (i, P_GB // D_MODEL)), wide],
        out_specs=[wide, pl.BlockSpec((2, tm, D_MODEL), lambda i: (0, i, 0))],
        out_shape=[jax.ShapeDtypeStruct((S, 2 * D_MODEL), BF16), jax.ShapeDtypeStruct((2, S, D_MODEL), BF16)],
        name="mix_bwd", compiler_params=_cparams(("parallel",), 40))(*_hbm(dz1b, w_o, proj, proj, yab))


def _conv_gate_bwd(proj, dya_in, conv_w):
    S = proj.shape[0]

    def body(b_ref, c_ref, h_ref, dy_ref, w_ref, o_ref, sm_ref, u_scr, d_scr):
        _zero_pads(u_scr, S)
        _zero_pads(d_scr, S)
        for t in range(0, S, CHUNK):
            u_scr[PAD + t:PAD + t + CHUNK, :] = c_ref[t:t + CHUNK, :].astype(F32) * h_ref[t:t + CHUNK, :].astype(F32)
        w0, w1, w2 = w_ref[0:1, :], w_ref[1:2, :], w_ref[2:3, :]
        zero = jnp.zeros((1, SLAB), F32)
        s_w0, s_w1, s_w2 = zero, zero, zero
        for t in range(0, S, CHUNK):
            um, u0, up = _shifted(u_scr, t)
            dy = dy_ref[t:t + CHUNK, :].astype(F32)
            o_ref[0, t:t + CHUNK, :] = (dy * (w0 * um + w1 * u0 + w2 * up)).astype(BF16)
            dcv = dy * b_ref[t:t + CHUNK, :].astype(F32)
            d_scr[PAD + t:PAD + t + CHUNK, :] = dcv
            s_w0 = s_w0 + jnp.sum(dcv * um, axis=0, keepdims=True)
            s_w1 = s_w1 + jnp.sum(dcv * u0, axis=0, keepdims=True)
            s_w2 = s_w2 + jnp.sum(dcv * up, axis=0, keepdims=True)
        for t in range(0, S, CHUNK):
            dm, d0, dp = _shifted(d_scr, t)
            du = w0 * dp + w1 * d0 + w2 * dm
            o_ref[1, t:t + CHUNK, :] = (du * h_ref[t:t + CHUNK, :].astype(F32)).astype(BF16)
            o_ref[2, t:t + CHUNK, :] = (du * c_ref[t:t + CHUNK, :].astype(F32)).astype(BF16)
        sm_ref[...] = _rows8([s_w0, s_w1, s_w2], SLAB)

    return pl.pallas_call(
        body, grid=(D_CONV // SLAB,),
        in_specs=[_slab_spec(S, P_B), _slab_spec(S, P_C), _slab_spec(S, P_H),
                  pl.BlockSpec((S, SLAB), lambda j: (0, j)), pl.BlockSpec((3, SLAB), lambda j: (0, j))],
        out_specs=[pl.BlockSpec((3, S, SLAB), lambda j: (0, 0, j)), pl.BlockSpec((SUBLANES, SLAB), lambda j: (0, j))],
        out_shape=[jax.ShapeDtypeStruct((3, S, D_CONV), BF16), jax.ShapeDtypeStruct((SUBLANES, D_CONV), F32)],
        scratch_shapes=[pltpu.VMEM((S + 2 * PAD, SLAB), F32)] * 2,
        name="conv_gate_bwd", compiler_params=_cparams(("parallel",), 48))(*_hbm(proj, proj, proj, dya_in, conv_w))


def _comb_bwd(dyab, w_b, comb, lse_tot, *, tm=512):
    S = comb.shape[0]
    widths, dtypes = (GROUP_W, LANES, LANES), (BF16, F32, F32)

    def body(dy_ref, wb_ref, c_ref, lt_ref, e_ref, *rest):
        outs, scr = rest[:3 * N_GROUPS], rest[3 * N_GROUPS:]
        dcb = _dot_nt(dy_ref[...], wb_ref[...]).astype(BF16)
        dc = dcb.astype(F32)
        delta = lax.dot_general(dc * c_ref[...], e_ref[...], (((1,), (1,)), ((), ())),
                                preferred_element_type=F32, precision=lax.Precision.HIGHEST)
        for k, (val, dtype) in enumerate(zip((dc, lt_ref[...], delta), dtypes)):
            outs[k][0] = val.astype(dtype)
            _to_residue(val, [outs[3 * (1 + j) + k] for j in range(len(DILS))], DILS, tm, dtype,
                        scr[:val.shape[1] // LANES])

    out_specs, out_shape = [], []
    for _, d in GROUPS:
        out_specs += [_res_spec(d, tm, w) for w in widths]
        out_shape += [jax.ShapeDtypeStruct((d, S // d, w), t) for w, t in zip(widths, dtypes)]
    res = pl.pallas_call(
        body, grid=(S // tm,),
        in_specs=[pl.BlockSpec((tm, D_MODEL), lambda i: (i, 1)), _resident(w_b.shape),
                  pl.BlockSpec((tm, GROUP_W), lambda i: (i, 0)), pl.BlockSpec((tm, LANES), lambda i: (i, 0)),
                  _resident((LANES, GROUP_W))],
        out_specs=out_specs, out_shape=out_shape, scratch_shapes=_lane_scratch(tm, GROUP_W),
        name="comb_bwd", compiler_params=_cparams(("parallel",), 32))(*_hbm(dyab, w_b, comb, lse_tot, _expand_heads()))
    return [tuple(res[3 * g:3 * g + 3]) for g in range(N_GROUPS)]


def _attn_bwd(qkv, col0, g, dcomb, lse_tot, delta):
    dil, sub, _ = qkv.shape
    nb = sub // TQ

    def body(q_ref, kp, kc, kn, vp, vc, vn, do_ref, lse_ref, dl_ref, bias_ref, dq_ref, dk_ref, dv_ref,
             ak, av, dqt_scr, s_scr, dp_scr, ds_scr, p_scr):
        i = pl.program_id(1)

        @pl.when(i == 0)
        def _():
            ak[...] = jnp.zeros_like(ak)
            av[...] = jnp.zeros_like(av)

        @pl.when(i < nb)
        def _():
            kwin = _window(kp, kc, kn)
            vwin = _window(vp, vc, vn)
            q = q_ref[...] * ATT_SCALE
            do = do_ref[...]
            lse_t, dl_t = lse_ref[...].T, dl_ref[...].T
            for h in range(HEADS_PER_GROUP):
                s_scr[h] = _dot_nt(_pair(kwin, h), _own_lanes(_pair(q, h), h))
                dp_scr[h] = _dot_nt(_pair(vwin, h), _own_lanes(_pair(do, h), h))
            for h in range(HEADS_PER_GROUP):
                p = jnp.exp(s_scr[h] + _slope(g, h) * bias_ref[...] - lse_t[h:h + 1, :])
                ds_scr[h] = (p * (dp_scr[h] - dl_t[h:h + 1, :])).astype(BF16)
                p_scr[h] = p.astype(BF16)
            for h in range(HEADS_PER_GROUP):
                dqt_scr[h * HEAD_DIM:(h + 1) * HEAD_DIM, :] = _own_rows(_dot_tn(_pair(kwin, h), ds_scr[h]), h)
            for h in range(0, HEADS_PER_GROUP, 2):
                cols = slice(h * HEAD_DIM, (h + 2) * HEAD_DIM)
                q2 = jnp.concatenate([_own_lanes(_pair(q, h), h), _own_lanes(_pair(q, h), h + 1)], axis=0)
                do2 = jnp.concatenate([_own_lanes(_pair(do, h), h), _own_lanes(_pair(do, h), h + 1)], axis=0)
                ak[RADIUS:RADIUS + 2 * TQ, cols] += _dot(jnp.concatenate([ds_scr[h], ds_scr[h + 1]], axis=1), q2)
                av[RADIUS:RADIUS + 2 * TQ, cols] += _dot(jnp.concatenate([p_scr[h], p_scr[h + 1]], axis=1), do2)
            dq_ref[...] = (dqt_scr[...].T * ATT_SCALE).astype(BF16)

        dk_ref[...] = ak[0:TQ, :].astype(BF16)
        dv_ref[...] = av[0:TQ, :].astype(BF16)
        ak[0:2 * TQ, :] = ak[TQ:3 * TQ, :]
        av[0:2 * TQ, :] = av[TQ:3 * TQ, :]
        ak[2 * TQ:3 * TQ, :] = jnp.zeros((TQ, GROUP_W), F32)
        av[2 * TQ:3 * TQ, :] = jnp.zeros((TQ, GROUP_W), F32)

    tok = pl.BlockSpec((None, TQ, GROUP_W), lambda r, i: (r, jnp.minimum(i, nb - 1), 0))
    stat = pl.BlockSpec((None, TQ, LANES), lambda r, i: (r, jnp.minimum(i, nb - 1), 0))
    dkv_spec = pl.BlockSpec((None, TQ, GROUP_W), lambda r, i: (r, jnp.maximum(i - 1, 0), 0))
    return pl.pallas_call(
        body, grid=(dil, nb + 1), in_specs=_qkv_specs(nb, col0) + [tok, stat, stat, _bias_spec(nb)],
        out_specs=[tok, dkv_spec, dkv_spec], out_shape=[jax.ShapeDtypeStruct((dil, sub, GROUP_W), BF16)] * 3,
        scratch_shapes=[pltpu.VMEM((3 * TQ, GROUP_W), F32)] * 2 + [pltpu.VMEM((GROUP_W, TQ), F32)]
        + [pltpu.VMEM((HEADS_PER_GROUP, 2 * TQ, TQ), F32)] * 2 + [pltpu.VMEM((HEADS_PER_GROUP, 2 * TQ, TQ), BF16)] * 2,
        name=f"attn_bwd_g{g}", compiler_params=_cparams(("arbitrary", "arbitrary"), 32))(
            *_hbm(*([qkv] * 7), dcomb, lse_tot, delta, _attn_bias_table(g)))


def _in_bwd_ln0(dgated, dqkv, w_nat, w_dil, dz1, x, g0, *, tm=256):
    S = x.shape[0]
    n_gated, n_in = len(dgated), 3 * N_GROUPS

    def body(*refs):
        g_refs, d_refs = refs[:n_gated], refs[n_gated:n_gated + n_in]
        wn_ref, *wd_refs = refs[n_gated + n_in:n_gated + n_in + N_GROUPS]
        dz_ref, x_ref, g_ref, gx_ref, st_ref, *tmp_ref = refs[n_gated + n_in + N_GROUPS:]
        dh = ALPHA * dz_ref[...]
        col = 0
        for ref in g_refs:
            for k in range(ref.shape[0]):
                dh = dh + _dot_nt(ref[k], wn_ref[:, col:col + D_MODEL])
                col += D_MODEL
        for g, (_, d) in enumerate(GROUPS):
            rows = [jnp.concatenate([d_refs[3 * g + k][r] for k in range(3)], axis=1) for r in range(d)]
            w = wn_ref[:, col:col + QKV_W] if d == 1 else wd_refs[g - 1][...]
            res = _dot_nt(jnp.concatenate(rows, axis=0), w)
            if d == 1:
                dh = dh + res
            else:
                n = tm // d
                dh = dh + _from_residue(lambda r: res[r * n:(r + 1) * n, :], d, tm, tmp_ref)
        xhat, rstd = _ln_stats(x_ref[...])
        gx_ref[...] = _ln_bwd(dh, xhat, rstd, g_ref[...])
        upd = _rows8([jnp.sum(dh * xhat, axis=0, keepdims=True), jnp.sum(dh, axis=0, keepdims=True)], D_MODEL)

        @pl.when(pl.program_id(0) == 0)
        def _():
            st_ref[...] = upd

        @pl.when(pl.program_id(0) != 0)
        def _():
            st_ref[...] += upd

    row = pl.BlockSpec((tm, D_MODEL), lambda i: (i, 0))
    g_specs = [pl.BlockSpec((a.shape[0], tm, D_MODEL), lambda i: (0, i, 0)) for a in dgated]
    d_specs = []
    for _, d in GROUPS:
        d_specs += [_res_spec(d, tm, GROUP_W)] * 3
    operands = list(dgated) + [a for grp in dqkv for a in grp] + [w_nat] + list(w_dil) + [dz1, x, g0]
    return pl.pallas_call(
        body, grid=(S // tm,),
        in_specs=g_specs + d_specs + [_resident(w_nat.shape)] + [_resident(w.shape) for w in w_dil]
        + [row, row, pl.BlockSpec((1, D_MODEL), lambda i: (0, 0))],
        out_specs=[row, pl.BlockSpec((SUBLANES, D_MODEL), lambda i: (0, 0))],
        out_shape=[jax.ShapeDtypeStruct((S, D_MODEL), F32), jax.ShapeDtypeStruct((SUBLANES, D_MODEL), F32)],
        scratch_shapes=_lane_scratch(tm, D_MODEL),
        name="in_bwd_ln0", compiler_params=_cparams(("arbitrary",), 52))(*_hbm(*operands))


HBM_SPEC = pl.BlockSpec(memory_space=pltpu.HBM)


def _place():
    x, y, c = lax.axis_index("x"), lax.axis_index("y"), lax.axis_index("c")
    chips = [(1 - x, y), (x, 1 - y), (1 - x, 1 - y)]
    return x, y, c, chips


def _allgather_shards(shards, after, *, name, collective_id):
    n = len(shards)
    per = 6

    def body(*refs):
        ins, outs = refs[:n], refs[n + len(after):2 * n + len(after)]
        send_sems, recv_sems, loc_sems = refs[2 * n + len(after):]
        x, y, c, chips = _place()
        me = 2 * x + y
        sib = (x, y, 1 - c)
        peers = [sib] + [(px, py, c) for px, py in chips]
        barrier = pltpu.get_barrier_semaphore()
        for peer in peers:
            pl.semaphore_signal(barrier, inc=1, device_id=peer, device_id_type=MESH)
        pl.semaphore_wait(barrier, len(peers))

        def rcopy(w, k, src, dst, to):
            return pltpu.make_async_remote_copy(src_ref=src, dst_ref=dst, send_sem=send_sems.at[per * w + k],
                                                recv_sem=recv_sems.at[per * w + k], device_id=to, device_id_type=MESH)

        split = [s.shape[0] == N_CORES for s in shards]
        half = lambda w: c if split[w] else 0
        local, sends = [], []
        for w in range(n):
            cp = pltpu.make_async_copy(ins[w], outs[w].at[me], loc_sems.at[w])
            cp.start()
            local.append(cp)
            for j, (px, py) in enumerate(chips):
                cp = rcopy(w, j, ins[w].at[half(w)], outs[w].at[me, half(w)], (px, py, c))
                cp.start()
                sends.append(cp)
        for w in range(n):
            for j, (px, py) in enumerate(chips):
                slot = outs[w].at[2 * px + py, half(w)]
                rcopy(w, j, slot, slot, (px, py, c)).wait_recv()
                if split[w]:
                    cp = rcopy(w, 3 + j, slot, slot, sib)
                    cp.start()
                    sends.append(cp)
        for w in range(n):
            if split[w]:
                for j, (px, py) in enumerate(chips):
                    slot = outs[w].at[2 * px + py, 1 - c]
                    rcopy(w, 3 + j, slot, slot, sib).wait_recv()
        for cp in sends:
            cp.wait_send()
        for cp in local:
            cp.wait()

    return pl.kernel(
        body, out_type=[jax.ShapeDtypeStruct((N_CHIPS,) + s.shape, s.dtype) for s in shards],
        mesh=plsc.ScalarSubcoreMesh(axis_name="sequencer", num_cores=1),
        scratch_types=[pltpu.SemaphoreType.DMA((per * n,)), pltpu.SemaphoreType.DMA((per * n,)),
                       pltpu.SemaphoreType.DMA((n,))],
        name=name, compiler_params=pltpu.CompilerParams(collective_id=collective_id))(*shards, *after)


def _exchange_grads(grads, *, name, collective_id):
    n = len(grads)
    per = 7

    def body(*refs):
        ins, outs = refs[:n], refs[n:2 * n]
        send_sems, recv_sems, loc_sems = refs[2 * n:]
        x, y, c, chips = _place()
        me = 2 * x + y
        sib = (x, y, 1 - c)
        peers = [sib] + [(px, py, c) for px, py in chips]
        barrier = pltpu.get_barrier_semaphore()
        for peer in peers:
            pl.semaphore_signal(barrier, inc=1, device_id=peer, device_id_type=MESH)
        pl.semaphore_wait(barrier, len(peers))

        def rcopy(w, k, src, dst, to):
            return pltpu.make_async_remote_copy(src_ref=src, dst_ref=dst, send_sem=send_sems.at[per * w + k],
                                                recv_sem=recv_sems.at[per * w + k], device_id=to, device_id_type=MESH)

        local, sends = [], []
        for w in range(n):
            cp = pltpu.make_async_copy(ins[w].at[me], outs[w].at[c, me], loc_sems.at[w])
            cp.start()
            local.append(cp)
            cp = rcopy(w, 0, ins[w].at[me], outs[w].at[c, me], sib)
            cp.start()
            sends.append(cp)
            for j, (px, py) in enumerate(chips):
                cp = rcopy(w, 1 + j, ins[w].at[2 * px + py], outs[w].at[c, me], (px, py, c))
                cp.start()
                sends.append(cp)
        for w in range(n):
            for j, (px, py) in enumerate(chips):
                slot = outs[w].at[c, 2 * px + py]
                rcopy(w, 1 + j, slot, slot, (px, py, c)).wait_recv()
                cp = rcopy(w, 4 + j, slot, slot, sib)
                cp.start()
                sends.append(cp)
        for w in range(n):
            slot = outs[w].at[1 - c, me]
            rcopy(w, 0, slot, slot, sib).wait_recv()
            for j, (px, py) in enumerate(chips):
                slot = outs[w].at[1 - c, 2 * px + py]
                rcopy(w, 4 + j, slot, slot, sib).wait_recv()
        for cp in sends:
            cp.wait_send()
        for cp in local:
            cp.wait()

    return pl.kernel(
        body, out_type=[jax.ShapeDtypeStruct((N_CORES,) + g.shape, g.dtype) for g in grads],
        mesh=plsc.ScalarSubcoreMesh(axis_name="sequencer", num_cores=1),
        scratch_types=[pltpu.SemaphoreType.DMA((per * n,)), pltpu.SemaphoreType.DMA((per * n,)),
                       pltpu.SemaphoreType.DMA((n,))],
        name=name, compiler_params=pltpu.CompilerParams(collective_id=collective_id))(*grads)


def _allgather_small(vec, after):
    def body(v_ref, _, o_ref, send_sems, recv_sems, loc_sem):
        x, y, c = lax.axis_index("x"), lax.axis_index("y"), lax.axis_index("c")
        me = 4 * x + 2 * y + c

        def peer(k):
            flip = lambda v, bit: 1 - v if (k >> bit) & 1 else v
            return flip(x, 2), flip(y, 1), flip(c, 0)

        loc = pltpu.make_async_copy(v_ref, o_ref.at[me], loc_sem)
        loc.start()
        sends = []
        for k in range(1, N_DEV):
            cp = pltpu.make_async_remote_copy(src_ref=v_ref, dst_ref=o_ref.at[me], send_sem=send_sems.at[k - 1],
                                              recv_sem=recv_sems.at[k - 1], device_id=peer(k), device_id_type=MESH)
            cp.start()
            sends.append(cp)
        for k in range(1, N_DEV):
            px, py, pc = peer(k)
            pltpu.make_async_remote_copy(src_ref=v_ref, dst_ref=o_ref.at[4 * px + 2 * py + pc],
                                         send_sem=send_sems.at[k - 1], recv_sem=recv_sems.at[k - 1],
                                         device_id=(px, py, pc), device_id_type=MESH).wait_recv()
        for cp in sends:
            cp.wait_send()
        loc.wait()

    return pl.pallas_call(
        body, in_specs=[HBM_SPEC, HBM_SPEC], out_specs=HBM_SPEC,
        out_shape=jax.ShapeDtypeStruct((N_DEV,) + vec.shape, vec.dtype),
        scratch_shapes=[pltpu.SemaphoreType.DMA((N_DEV - 1,)), pltpu.SemaphoreType.DMA((N_DEV - 1,)),
                        pltpu.SemaphoreType.DMA],
        name="allgather_small")(vec, after)


def _adamw(w, g, m, v):
    m = ADAM_B1 * m + (1.0 - ADAM_B1) * g
    v = ADAM_B2 * v + (1.0 - ADAM_B2) * (g * g)
    m_hat = m / (1.0 - ADAM_B1 ** ADAM_STEP)
    v_hat = v / (1.0 - ADAM_B2 ** ADAM_STEP)
    delta = -ADAM_LR * (m_hat / (jnp.sqrt(v_hat) + ADAM_EPS) + ADAM_WD * w)
    return delta, m, v


def _reduce_adamw(parts, w, m, v, *, tr, name):
    R, C = w.shape

    def body(p_ref, w_ref, m_ref, v_ref, g_ref, d_ref, nm_ref, nv_ref):
        def core_sum(cc):
            s = p_ref[cc, 0].astype(F32)
            for k in range(1, N_CHIPS):
                s = s + p_ref[cc, k].astype(F32)
            return s

        g = core_sum(0) + core_sum(1)
        delta, nm, nv = _adamw(w_ref[...], g, m_ref[...], v_ref[...])
        g_ref[...] = g
        d_ref[...] = delta
        nm_ref[...] = nm
        nv_ref[...] = nv

    blk = pl.BlockSpec((tr, C), lambda i: (i, 0))
    return pl.pallas_call(
        body, grid=(R // tr,),
        in_specs=[pl.BlockSpec((N_CORES, N_CHIPS, tr, C), lambda i: (0, 0, i, 0)), blk, blk, blk],
        out_specs=[blk] * 4, out_shape=[jax.ShapeDtypeStruct((R, C), F32)] * 4,
        name=name, compiler_params=_cparams(("parallel",), 40))(*_hbm(parts, w, m, v))


def _reduce_adamw_vectors(allv, offs, ws, ms, vs):
    n = len(ws)

    def body(a_ref, *refs):
        w_refs, m_refs, v_refs = refs[:n], refs[n:2 * n], refs[2 * n:3 * n]
        tot_ref, outs = refs[3 * n], refs[3 * n + 1:]
        s = a_ref[0]
        for d in range(1, N_DEV):
            s = s + a_ref[d]
        tot_ref[...] = s
        for k in range(n):
            g = s[:, offs[k]:offs[k] + w_refs[k].shape[1]]
            delta, nm, nv = _adamw(w_refs[k][...], g, m_refs[k][...], v_refs[k][...])
            for ref, val in zip(outs[4 * k:4 * k + 4], (g, delta, nm, nv)):
                ref[...] = val

    out_shape = [jax.ShapeDtypeStruct(allv.shape[1:], F32)]
    for w in ws:
        out_shape += [jax.ShapeDtypeStruct(w.shape, F32)] * 4
    res = pl.pallas_call(body, out_shape=out_shape, name="reduce_adamw_vectors",
                         compiler_params=_cparams((), 40))(allv, *ws, *ms, *vs)
    return res[0], [tuple(res[1 + 4 * k:5 + 4 * k]) for k in range(n)]


def _adamw_taps(ws, gs, ms, vs):
    n = len(ws)

    def body(*refs):
        outs = refs[4 * n:]
        for k in range(n):
            res = _adamw(refs[k][...], refs[n + k][...], refs[2 * n + k][...], refs[3 * n + k][...])
            for ref, val in zip(outs[3 * k:3 * k + 3], res):
                ref[...] = val

    out_shape = []
    for w in ws:
        out_shape += [jax.ShapeDtypeStruct(w.shape, F32)] * 3
    res = pl.pallas_call(body, out_shape=out_shape, name="adamw_taps")(*ws, *gs, *ms, *vs)
    return [tuple(res[3 * k:3 * k + 3]) for k in range(n)]


def _pack(pieces):
    flat, offs, n = [], [], 0
    for p in pieces:
        size = -(-p.size // LANES) * LANES
        flat.append(jnp.pad(p.reshape(-1), (0, size - p.size)))
        offs.append(n)
        n += size
    return jnp.concatenate(flat).reshape(1, n), offs


def _local_step(x, target, p, wfull, on_ready=lambda group: None, before_ln0=()):
    S = x.shape[0]
    dils = [d for _, d in GROUPS]

    h0, h0b, *h0_res = _ln0_fwd(x, p["ln0_g"], p["ln0_b"], before_ln0)
    h0_rows = [h0b] + [h.reshape(S, D_MODEL) for h in h0_res]

    if isinstance(wfull, dict):
        w_in3, pending = wfull["w_in"], None
    else:
        w_in3, launch_rest, assemble = wfull
        w_in3, h0b = lax.optimization_barrier((w_in3, h0b))
        pending = launch_rest(h0b)

    runs = _col_runs()
    w_perm = jnp.concatenate([w_in3[s, :, c:c + w] for s, c, _, w in runs], axis=1)
    b_blocks = p["b_in"].reshape(N_BLK, GROUP_W)
    b_perm = jnp.concatenate([b_blocks[b] for b in PERM]).reshape(1, N_IN)
    w_nat, b_nat = w_perm[:, :N_NAT], b_perm[:, :N_NAT]
    qkv_cols = [slice(P_Q0 + g * QKV_W, P_Q0 + (g + 1) * QKV_W) for g in range(N_GROUPS)]
    w_qkv = [w_perm[:, c] for c in qkv_cols]

    proj = _mm_nn(h0b, w_nat, b_nat, tm=512, tn=N_NAT // 2, out_dtype=BF16, name="proj")
    qkv = [proj[None]]
    for g in range(1, N_GROUPS):
        t = _mm_nn(h0_rows[g], w_qkv[g], b_perm[:, qkv_cols[g]], tm=512, tn=QKV_W, out_dtype=BF16, name=f"proj_qkv{g}")
        qkv.append(t.reshape(dils[g], S // dils[g], QKV_W))
    if pending is not None:
        pending, qkv = lax.optimization_barrier((pending, qkv))
        proj = qkv[0][0]
        wfull = assemble(pending)
    w_up3 = wfull["w_up"]
    w_a, w_o, w_down, w_b = wfull["w_a"], wfull["w_o"], wfull["w_down"], wfull["w_b"]
    conv_w, ffn_conv_w = wfull["conv_w"], wfull["ffn_conv_w"]
    col0 = [P_Q0 // GROUP_W] + [0] * (N_GROUPS - 1)
    ya_in = _conv_gate_fwd(proj, conv_w)
    att = [_attn_fwd(qkv[g], col0[g], g) for g in range(N_GROUPS)]
    comb, comb_b, lse_tot = _attn_combine([a[0] for a in att], [a[1] for a in att])
    yab, mixin = _branch_mix(ya_in, comb_b, w_a, w_b, proj)
    xhat1, rstd1, h1b = _mix_ln1(mixin, w_o, p["b_o"], h0, p["ln1_g"], p["ln1_b"])
    up = _mm_nn(h1b, w_up3, p["b_up"], tm=512, tn=w_up3.shape[2], out_dtype=BF16, name="up")
    f = _ffn_conv_fwd(up, ffn_conv_w, p["ffn_conv_b"])
    dz2, dz2b, st2 = _down_ln2_loss(f, w_down, p["b_down"], xhat1, p["ln1_g"], p["ln1_b"],
                                    p["ln2_g"], p["ln2_b"], target)

    gw = {}
    gw["w_down"] = _mm_tn(f, dz2b, n_out=1, tn=D_MODEL, ts=1024, g_block=(1024, D_MODEL),
                          g_map=lambda j, s: (s, 0), name="grad_w_down").reshape(N_CHIPS, D_FF // N_CHIPS, D_MODEL)
    df = _mm_nt(dz2b, w_down, tm=512, name="df")
    dup, sm_ffn = _ffn_conv_bwd(up, df, ffn_conv_w, p["ffn_conv_b"])
    up_tn = w_up3.shape[2]
    up_pp = D_FF // up_tn
    gw["w_up"] = _mm_tn(h1b, dup, n_out=N_CHIPS, tn=up_tn, ts=1024, g_block=(None, 1024, up_tn),
                        g_map=lambda j, s: (j // up_pp, s, j % up_pp), name="grad_w_up")
    exchanged = on_ready({n: gw[n] for n in ("w_down", "w_up")}) or {}
    dz1, dz1b, st1 = _up_bwd_ln1(dup, w_up3, dz2, xhat1, rstd1, p["ln1_g"])

    gw["w_o"] = _mm_tn(mixin, dz1b, n_out=1, tn=D_MODEL, ts=512, g_block=(512, D_MODEL),
                       g_map=lambda j, s: (s, 0), name="grad_w_o").reshape(N_CHIPS, D_MODEL // N_CHIPS, D_MODEL)
    dyab, dgab = _mix_bwd(dz1b, w_o, proj, yab)
    gw["w_a"] =_mm_tn(ya_in, dyab, n_out=1, tn=D_MODEL, ts=512, g_block=(512, D_MODEL),
                       g_map=lambda j, s: (s, 0), name="grad_w_a").reshape(N_CHIPS, D_CONV // N_CHIPS, D_MODEL)
    gw_b = _mm_tn(comb_b, dyab, n_out=1, tn=D_MODEL, ts=1024, g_block=(1024, D_MODEL),
                  g_map=lambda j, s: (s, 1), name="grad_w_b")
    gw["w_b"] = gw_b.reshape(GROUP_W, N_CHIPS, D_MODEL // N_CHIPS).transpose(1, 0, 2)
    exchanged_mix = on_ready({n: gw[n] for n in ("w_o", "w_a", "w_b")}) or {}
    dya_in = _mm_nt(dyab, w_a, tm=512, a_col=0, name="dya_in")
    exchanged, dya_in = lax.optimization_barrier((exchanged, dya_in))
    dbch, sm_conv = _conv_gate_bwd(proj, dya_in, conv_w)
    att_stats = _comb_bwd(dyab, w_b, comb, lse_tot)
    exchanged_mix, att_stats = lax.optimization_barrier((exchanged_mix, att_stats))
    exchanged.update(exchanged_mix)
    dqkv = [_attn_bwd(qkv[g], col0[g], g, *att_stats[g]) for g in range(N_GROUPS)]

    w_pieces, b_pieces = [], []
    for nm, planes in (("bch", dbch), ("gab", dgab)):
        pw, pc = _mm_tn(h0b, planes, n_out=planes.shape[0], tn=D_MODEL, ts=1024, g_block=(None, 1024, D_MODEL),
                        g_map=lambda j, s: (j, s, 0), colsum=True, name="grad_w_in_" + nm)
        w_pieces.extend(pw[k] for k in range(planes.shape[0]))
        b_pieces.append(pc[0])
    for g in range(N_GROUPS):
        pw, pc = _mm_tn_cat(h0_rows[g], [a.reshape(S, GROUP_W) for a in dqkv[g]], ts=1024, name=f"grad_w_in_qkv{g}")
        w_pieces.append(pw)
        b_pieces.append(pc[0])
    dw_perm = jnp.concatenate(w_pieces, axis=1)
    gw["w_in"] = jnp.stack([
        jnp.concatenate([dw_perm[:, pc:pc + w] for s, c, pc, w in sorted(runs, key=lambda r: r[1]) if s == k], axis=1)
        for k in range(N_CHIPS)])
    exchanged.update(on_ready({"w_in": gw["w_in"]}) or {})
    db_blocks = jnp.concatenate(b_pieces).reshape(N_BLK, GROUP_W)
    grad_b_in = jnp.concatenate([db_blocks[b] for b in INV_PERM])

    grad_x, st0 = _in_bwd_ln0([dbch, dgab], dqkv, w_nat, w_qkv[1:], dz1, x, p["ln0_g"])

    small = {
        "loss": st2[2:3, 0:1],
        "ln0_g": st0[0], "ln0_b": st0[1], "b_in": grad_b_in, "conv_w": sm_conv[0:3],
        "b_o": st1[2], "ln1_g": st1[0], "ln1_b": st1[1],
        "b_up": jnp.concatenate([sm_ffn[0], sm_ffn[1]]), "ffn_conv_w": sm_ffn[3:6], "ffn_conv_b": sm_ffn[2],
        "b_down": st2[3], "ln2_g": st2[0], "ln2_b": st2[1],
    }
    return grad_x, exchanged or gw, small


BIG =("w_in", "w_a", "w_b", "w_o", "w_up", "w_down")
CONV = ("conv_w", "ffn_conv_w")
VECS = ("ln0_g", "ln0_b", "b_in", "b_o", "ln1_g", "ln1_b", "b_up", "ffn_conv_b", "b_down", "ln2_g", "ln2_b")
ORDER = ("ln0_g", "ln0_b", "w_in", "b_in", "conv_w", "w_a", "w_b", "w_o", "b_o", "ln1_g", "ln1_b", "w_up", "b_up",
         "ffn_conv_w", "ffn_conv_b", "w_down", "b_down", "ln2_g", "ln2_b")
SMALL_ORDER = ("loss",) + VECS + CONV


def _step(x, target, W, Mo, Vo):
    x2, t2 = x[0], target[0]
    big2 = {n: W[n][0] for n in BIG}
    halves = lambda a: a.astype(BF16).reshape(N_CORES, a.shape[0] // N_CORES, a.shape[1])
    whole = lambda g: g.reshape(N_CHIPS, g.shape[1] * g.shape[2], g.shape[3])
    later = tuple(n for n in BIG if n != "w_in")
    w_in_halves = halves(big2["w_in"])
    first = _allgather_shards([w_in_halves], [], name="allgather_w_in", collective_id=1)

    def launch_rest(h0b):
        return _allgather_shards([halves(big2[n]) for n in later] + [W[n] for n in CONV], [h0b],
                                 name="allgather_rest", collective_id=2)

    def assemble(rest):
        gathered = {n: whole(g) for n, g in zip(later + CONV, rest)}
        return {
            "w_up": gathered["w_up"],
            "w_a": gathered["w_a"].reshape(D_CONV, D_MODEL), "w_o": gathered["w_o"].reshape(D_MODEL, D_MODEL),
            "w_down": gathered["w_down"].reshape(D_FF, D_MODEL),
            "w_b": gathered["w_b"].transpose(1, 0, 2).reshape(GROUP_W, D_MODEL),
            "conv_w": gathered["conv_w"].transpose(1, 0, 2).reshape(3, D_CONV),
            "ffn_conv_w": gathered["ffn_conv_w"].transpose(1, 0, 2).reshape(3, D_FF),
        }

    pvec = {n: W[n].reshape(1, -1) for n in VECS}

    exchange_ids = iter((3, 4, 5))

    def exchange(group):
        names = tuple(group)
        res = _exchange_grads([group[n] for n in names], name="exchange_" + "_".join(names),
                              collective_id=next(exchange_ids))
        return dict(zip(names, res))

    grad_x, parts, small = _local_step(x2, t2, pvec, (whole(first[0]), launch_rest, assemble), exchange,
                                       before_ln0=[w_in_halves])
    out = {}
    for n in BIG:
        tr = {"w_in": 128, "w_up": 128, "w_b": 128}.get(n, big2[n].shape[0] // 4)
        g, d, nm, nv = _reduce_adamw(parts[n], big2[n], Mo[n][0], Vo[n][0], tr=tr, name="adamw_" + n)
        out[n] = tuple(a[None] for a in (g, d, nm, nv))

    vec, offs = _pack([small[n] for n in SMALL_ORDER])
    off = dict(zip(SMALL_ORDER, offs))
    row = lambda a: a.reshape(1, -1)
    allv = _allgather_small(vec, parts["w_in"])
    tot, vec_out = _reduce_adamw_vectors(allv, [off[n] for n in VECS], [row(W[n]) for n in VECS],
                                         [row(Mo[n]) for n in VECS], [row(Vo[n]) for n in VECS])
    for n, res in zip(VECS, vec_out):
        out[n] = tuple(a.reshape(W[n].shape) for a in res)
    loss = tot[0, off["loss"]]
    chip = 2 * lax.axis_index("x") + lax.axis_index("y")
    taps_g = []
    for n in CONV:
        width = W[n].shape[2]
        full = lax.slice(tot, (0, off[n]), (1, off[n] + 3 * N_CHIPS * width)).reshape(3, N_CHIPS * width)
        taps_g.append(lax.dynamic_slice_in_dim(full, chip * width, width, axis=1))
    taps_out = _adamw_taps([W[n][0] for n in CONV], taps_g, [Mo[n][0] for n in CONV], [Vo[n][0] for n in CONV])
    for n, g, res in zip(CONV, taps_g, taps_out):
        out[n] = tuple(a[None] for a in (g,) + res)

    res = [loss, grad_x[None]]
    for k in range(4):
        res += [out[n][k] for n in ORDER]
    return tuple(res)


def kernel(x, ln0_g, ln0_b, w_in, b_in, conv_w, w_a, w_b, w_o, b_o, ln1_g, ln1_b, w_up, b_up, ffn_conv_w, ffn_conv_b, w_down, b_down, ln2_g, ln2_b, loss_target, m_ln0_g, m_ln0_b, m_w_in, m_b_in, m_conv_w, m_w_a, m_w_b, m_w_o, m_b_o, m_ln1_g, m_ln1_b, m_w_up, m_b_up, m_ffn_conv_w, m_ffn_conv_b, m_w_down, m_b_down, m_ln2_g, m_ln2_b, v_ln0_g, v_ln0_b, v_w_in, v_b_in, v_conv_w, v_w_a, v_w_b, v_w_o, v_b_o, v_ln1_g, v_ln1_b, v_w_up, v_b_up, v_ffn_conv_w, v_ffn_conv_b, v_w_down, v_b_down, v_ln2_g, v_ln2_b):
    W = dict(zip(ORDER, (ln0_g, ln0_b, w_in, b_in, conv_w, w_a, w_b, w_o, b_o, ln1_g, ln1_b, w_up, b_up,
                         ffn_conv_w, ffn_conv_b, w_down, b_down, ln2_g, ln2_b)))
    Mo = dict(zip(ORDER, (m_ln0_g, m_ln0_b, m_w_in, m_b_in, m_conv_w, m_w_a, m_w_b, m_w_o, m_b_o, m_ln1_g, m_ln1_b,
                          m_w_up, m_b_up, m_ffn_conv_w, m_ffn_conv_b, m_w_down, m_b_down, m_ln2_g, m_ln2_b)))
    Vo = dict(zip(ORDER, (v_ln0_g, v_ln0_b, v_w_in, v_b_in, v_conv_w, v_w_a, v_w_b, v_w_o, v_b_o, v_ln1_g, v_ln1_b,
                          v_w_up, v_b_up, v_ffn_conv_w, v_ffn_conv_b, v_w_down, v_b_down, v_ln2_g, v_ln2_b)))
    return _step(x, loss_target, W, Mo, Vo)
```

```python
import functools
import math

import jax
import jax.numpy as jnp
from jax import lax
from jax.experimental import pallas as pl
from jax.experimental.pallas import tpu as pltpu
from jax.experimental.pallas import tpu_sc as plsc

F32 = jnp.float32
BF16 = jnp.bfloat16

D_MODEL = 1024
D_CONV = D_MODEL
HEAD_DIM = 64
HEADS_PER_GROUP = 8
GROUPS = ((128, 1), (512, 4), (2048, 16))
N_GROUPS = len(GROUPS)
GROUP_W = HEADS_PER_GROUP * HEAD_DIM
QKV_W = N_GROUPS * GROUP_W
RADIUS = 64
D_FF = 2816
LN_EPS = 1e-5
ALPHA = 2.0 ** 0.25
MASK_VALUE = -1e30
ATT_SCALE = HEAD_DIM ** -0.5
OFF_B = 0
OFF_C = OFF_B + D_CONV
OFF_H = OFF_C + D_CONV
OFF_Q = OFF_H + D_CONV
OFF_K = OFF_Q + QKV_W
OFF_V = OFF_K + QKV_W
OFF_GA = OFF_V + QKV_W
OFF_GB = OFF_GA + D_MODEL
N_IN = OFF_GB + D_MODEL
ADAM_LR = 0.001
ADAM_B1 = 0.9
ADAM_B2 = 0.999
ADAM_EPS = 1e-08
ADAM_WD = 0.01
ADAM_STEP = 10
INV_SQRT2 = 0.7071067811865476
INV_SQRT_2PI = 0.3989422804014327

LANES = 128
SUBLANES = 8
VMEM_BYTES_V7X = 64 * 1024 * 1024
N_CHIPS = 4
N_CORES = 2
N_DEV = N_CHIPS * N_CORES
MESH = pl.DeviceIdType.MESH

N_BLK = N_IN // GROUP_W
PERM = (0, 1, 2, 3, 4, 5, 15, 16, 17, 18, 6, 9, 12, 7, 10, 13, 8, 11, 14)
INV_PERM = tuple(PERM.index(b) for b in range(N_BLK))
P_B, P_C, P_H, P_GA, P_GB, P_Q0 = 0, 1024, 2048, 3072, 4096, 5120
N_NAT = P_Q0 + QKV_W // N_GROUPS * 3
N_GATED = P_Q0

def _col_runs():
    shard_w = N_IN // N_CHIPS
    runs = []
    for pos, blk in enumerate(PERM):
        c, end = blk * GROUP_W, (blk + 1) * GROUP_W
        while c < end:
            stop = min(end, (c // shard_w + 1) * shard_w)
            runs.append((c // shard_w, c % shard_w, pos * GROUP_W + c - blk * GROUP_W, stop - c))
            c = stop
    return runs


SLAB = 128
CHUNK = 256
PAD = SUBLANES
TQ = 128


def _cparams(sem, vmem_mb):
    assert vmem_mb * 1024 * 1024 < VMEM_BYTES_V7X
    return pltpu.CompilerParams(dimension_semantics=sem, vmem_limit_bytes=vmem_mb * 1024 * 1024)


def _resident(shape):
    nd = len(shape)
    return pl.BlockSpec(shape, lambda *_: (0,) * nd, pipeline_mode=pl.Buffered(1))


def _hbm(*arrays):
    return [pltpu.with_memory_space_constraint(a, pltpu.HBM) for a in arrays]


def _dot(a, b):
    return jnp.dot(a, b, preferred_element_type=F32)


def _dot_nt(a, b):
    return lax.dot_general(a, b, (((1,), (1,)), ((), ())), preferred_element_type=F32)


def _dot_tn(a, b):
    return lax.dot_general(a, b, (((0,), (0,)), ((), ())), preferred_element_type=F32)


def _ln_stats(z):
    mu = jnp.mean(z, -1, keepdims=True)
    zc = z - mu
    var = jnp.mean(zc * zc, -1, keepdims=True)
    rstd = lax.rsqrt(var + LN_EPS)
    return zc * rstd, rstd


def _ln_bwd(dh, xhat, rstd, g):
    dxh = dh * g
    m1 = jnp.mean(dxh, -1, keepdims=True)
    m2 = jnp.mean(dxh * xhat, -1, keepdims=True)
    return rstd * (dxh - m1 - xhat * m2)


def _rows8(rows, width):
    pad = [jnp.zeros((1, width), F32)] * (SUBLANES - len(rows))
    return jnp.concatenate(list(rows) + pad, axis=0)


def _mm_nn(a, w, bias, *, tm, tn, out_dtype, name, vmem_mb=40):
    M, K = a.shape
    if w.ndim == 3:
        assert w.shape[2] == tn
        n_tiles = w.shape[0]
        w_spec = pl.BlockSpec((None, K, tn), lambda j, i: (j, 0, 0))
    else:
        n_tiles = w.shape[1] // tn
        w_spec = pl.BlockSpec((K, tn), lambda j, i: (0, j))

    def body(a_ref, w_ref, b_ref, o_ref):
        o_ref[...] = (_dot(a_ref[...], w_ref[...]) + b_ref[...]).astype(o_ref.dtype)

    return pl.pallas_call(
        body, grid=(n_tiles, M // tm),
        in_specs=[pl.BlockSpec((tm, K), lambda j, i: (i, 0)), w_spec, pl.BlockSpec((1, tn), lambda j, i: (0, j))],
        out_specs=pl.BlockSpec((tm, tn), lambda j, i: (i, j)),
        out_shape=jax.ShapeDtypeStruct((M, n_tiles * tn), out_dtype),
        name=name, compiler_params=_cparams(("arbitrary", "parallel"), vmem_mb))(*_hbm(a, w, bias))


def _mm_nt(a, w, *, tm, a_col=0, name, vmem_mb=40):
    M = a.shape[0]
    N, K = w.shape

    def body(a_ref, w_ref, o_ref):
        o_ref[...] = _dot_nt(a_ref[...], w_ref[...]).astype(o_ref.dtype)

    return pl.pallas_call(
        body, grid=(M // tm,),
        in_specs=[pl.BlockSpec((tm, K), lambda i: (i, a_col)),
                  pl.BlockSpec((N, K), lambda i: (0, 0))],
        out_specs=pl.BlockSpec((tm, N), lambda i: (i, 0)),
        out_shape=jax.ShapeDtypeStruct((M, N), BF16),
        name=name, compiler_params=_cparams(("parallel",), vmem_mb))(*_hbm(a, w))


def _mm_tn(a, g, *, n_out, tn, ts, g_block, g_map, colsum=False, name, vmem_mb=48):
    S, K = a.shape
    n_s = S // ts

    def body(a_ref, g_ref, *rest):
        if colsum:
            o_ref, cs_ref, acc_ref, cacc_ref = rest
        else:
            o_ref, acc_ref = rest
        s = pl.program_id(1)

        @pl.when(s == 0)
        def _():
            acc_ref[...] = jnp.zeros_like(acc_ref)
            if colsum:
                cacc_ref[...] = jnp.zeros_like(cacc_ref)

        gv = g_ref[...]
        acc_ref[...] += _dot_tn(a_ref[...], gv)
        if colsum:
            cacc_ref[...] += jnp.broadcast_to(jnp.sum(gv.astype(F32), axis=0, keepdims=True), cacc_ref.shape)

        @pl.when(s == n_s - 1)
        def _():
            o_ref[...] = acc_ref[...].astype(o_ref.dtype)
            if colsum:
                cs_ref[...] = cacc_ref[...]

    out_specs = [pl.BlockSpec((None, K, tn), lambda j, s: (j, 0, 0))]
    out_shape = [jax.ShapeDtypeStruct((n_out, K, tn), BF16)]
    scratch = [pltpu.VMEM((K, tn), F32)]
    if colsum:
        out_specs.append(pl.BlockSpec((SUBLANES, tn), lambda j, s: (0, j)))
        out_shape.append(jax.ShapeDtypeStruct((SUBLANES, n_out * tn), F32))
        scratch.append(pltpu.VMEM((SUBLANES, tn), F32))
    res = pl.pallas_call(
        body, grid=(n_out, n_s),
        in_specs=[pl.BlockSpec((ts, K), lambda j, s: (s, 0)), pl.BlockSpec(g_block, g_map)],
        out_specs=out_specs, out_shape=out_shape, scratch_shapes=scratch,
        name=name, compiler_params=_cparams(("parallel", "arbitrary"), vmem_mb))(*_hbm(a, g))
    return res if colsum else res[0]


def _mm_tn_cat(a, gs, *, ts, name, vmem_mb=40):
    S, K = a.shape
    widths = [g.shape[1] for g in gs]
    n_s, total = S // ts, sum(widths)

    def body(*refs):
        a_ref, g_refs = refs[0], refs[1:1 + len(gs)]
        o_ref, cs_ref, acc_ref, cacc_ref = refs[1 + len(gs):]
        s = pl.program_id(0)

        @pl.when(s == 0)
        def _():
            acc_ref[...] = jnp.zeros_like(acc_ref)
            cacc_ref[...] = jnp.zeros_like(cacc_ref)

        av, col = a_ref[...], 0
        for g_ref, w in zip(g_refs, widths):
            gv = g_ref[...]
            acc_ref[:, col:col + w] += _dot_tn(av, gv)
            cacc_ref[:, col:col + w] += jnp.broadcast_to(jnp.sum(gv.astype(F32), axis=0, keepdims=True), (SUBLANES, w))
            col += w

        @pl.when(s == n_s - 1)
        def _():
            o_ref[...] = acc_ref[...].astype(BF16)
            cs_ref[...] = cacc_ref[...]

    return pl.pallas_call(
        body, grid=(n_s,),
        in_specs=[pl.BlockSpec((ts, K), lambda s: (s, 0))] + [pl.BlockSpec((ts, w), lambda s: (s, 0)) for w in widths],
        out_specs=[pl.BlockSpec((K, total), lambda s: (0, 0)), pl.BlockSpec((SUBLANES, total), lambda s: (0, 0))],
        out_shape=[jax.ShapeDtypeStruct((K, total), BF16), jax.ShapeDtypeStruct((SUBLANES, total), F32)],
        scratch_shapes=[pltpu.VMEM((K, total), F32), pltpu.VMEM((SUBLANES, total), F32)],
        name=name, compiler_params=_cparams(("arbitrary",), vmem_mb))(*_hbm(a, *gs))


DILS = tuple(d for _, d in GROUPS if d > 1)


def _res_spec(d, tm, width):
    return pl.BlockSpec((d, tm // d, width), lambda i: (0, i, 0))


def _lane_scratch(tm, width):
    return [pltpu.VMEM((tm, LANES), F32)] * (width // LANES)


def _to_residue(val, dst_refs, dils, tm, dtype, scr):
    for c, ref in enumerate(scr):
        ref[...] = val[:, c * LANES:(c + 1) * LANES]
    for dst_ref, d in zip(dst_refs, dils):
        for r in range(d):
            cols = [ref[pl.ds(r, tm // d, stride=d), :] for ref in scr]
            dst_ref[r] = jnp.concatenate(cols, axis=1).astype(dtype)


def _from_residue(rows_of, d, tm, scr):
    for r in range(d):
        v = rows_of(r).astype(F32)
        for c, ref in enumerate(scr):
            ref[pl.ds(r, tm // d, stride=d), :] = v[:, c * LANES:(c + 1) * LANES]
    return jnp.concatenate([ref[...] for ref in scr], axis=1)


def _ln0_fwd(x, g, b, after=(), *, tm=512):
    S, Dm = x.shape
    n_after = len(after)

    def body(x_ref, g_ref, b_ref, *rest):
        h_ref, hb_ref, *rest = rest[n_after:]
        xhat, _ = _ln_stats(x_ref[...])
        h = xhat * g_ref[...] + b_ref[...]
        h_ref[...] = h
        hb_ref[...] = h.astype(BF16)
        _to_residue(h, rest[:len(DILS)], DILS, tm, BF16, rest[len(DILS):])

    row = pl.BlockSpec((tm, Dm), lambda i: (i, 0))
    vec = pl.BlockSpec((1, Dm), lambda i: (0, 0))
    return pl.pallas_call(
        body, grid=(S // tm,), in_specs=[row, vec, vec] + [pl.BlockSpec(memory_space=pl.ANY)] * n_after,
        out_specs=[row, row] + [_res_spec(d, tm, Dm) for d in DILS],
        out_shape=[jax.ShapeDtypeStruct((S, Dm), F32), jax.ShapeDtypeStruct((S, Dm), BF16)]
        + [jax.ShapeDtypeStruct((d, S // d, Dm), BF16) for d in DILS],
        scratch_shapes=_lane_scratch(tm, Dm),
        name="ln0_fwd", compiler_params=_cparams(("parallel",), 32))(*_hbm(x, g, b), *after)


def _slab_spec(S, col0):
    return pl.BlockSpec((S, SLAB), lambda j: (0, col0 // SLAB + j))


def _zero_pads(scr, S):
    scr[0:PAD, :] = jnp.zeros((PAD, SLAB), F32)
    scr[S + PAD:S + 2 * PAD, :] = jnp.zeros((PAD, SLAB), F32)


def _shifted(scr, t):
    return (scr[PAD - 1 + t:PAD - 1 + t + CHUNK, :], scr[PAD + t:PAD + t + CHUNK, :],
            scr[PAD + 1 + t:PAD + 1 + t + CHUNK, :])


def _conv_gate_fwd(proj, conv_w):
    S = proj.shape[0]

    def body(b_ref, c_ref, h_ref, w_ref, o_ref, u_scr):
        _zero_pads(u_scr, S)
        for t in range(0, S, CHUNK):
            u_scr[PAD + t:PAD + t + CHUNK, :] = c_ref[t:t + CHUNK, :].astype(F32) * h_ref[t:t + CHUNK, :].astype(F32)
        w0, w1, w2 = w_ref[0:1, :], w_ref[1:2, :], w_ref[2:3, :]
        for t in range(0, S, CHUNK):
            um, u0, up = _shifted(u_scr, t)
            cv = w0 * um + w1 * u0 + w2 * up
            o_ref[t:t + CHUNK, :] = (b_ref[t:t + CHUNK, :].astype(F32) * cv).astype(BF16)

    return pl.pallas_call(
        body, grid=(D_CONV // SLAB,),
        in_specs=[_slab_spec(S, P_B), _slab_spec(S, P_C), _slab_spec(S, P_H),
                  pl.BlockSpec((3, SLAB), lambda j: (0, j))],
        out_specs=pl.BlockSpec((S, SLAB), lambda j: (0, j)),
        out_shape=jax.ShapeDtypeStruct((S, D_CONV), BF16),
        scratch_shapes=[pltpu.VMEM((S + 2 * PAD, SLAB), F32)],
        name="conv_gate_fwd", compiler_params=_cparams(("parallel",), 40))(*_hbm(proj, proj, proj, conv_w))


MASKED_DISTANCE = -1e34


def _attn_bias_table(g):
    dil = GROUPS[g][1]
    j = lax.broadcasted_iota(jnp.int32, (2 * TQ, TQ), 0)
    a = lax.broadcasted_iota(jnp.int32, (2 * TQ, TQ), 1)
    rel = jnp.abs(j - RADIUS - a)
    base = -(rel * dil).astype(F32)
    inside, after_start, before_end = rel <= RADIUS, j >= RADIUS, j < TQ + RADIUS
    variants = []
    for first, last in ((False, False), (True, False), (False, True), (True, True)):
        valid = inside & (after_start if first else True) & (before_end if last else True)
        variants.append(jnp.where(valid, base, MASKED_DISTANCE))
    return jnp.stack(variants)


SUBS = 2
TB = SUBS * TQ


def _ext_window(p_ref, c_ref, n_ref):
    return jnp.concatenate([p_ref[TB - RADIUS:, :], c_ref[...], n_ref[:RADIUS, :]], axis=0)


def _head_stats(rows):
    pad = jnp.zeros((LANES - len(rows), TQ), F32)
    return jnp.concatenate(list(rows) + [pad], axis=0).T


def _slope(g, h):
    return 2.0 ** (-8.0 * (g * HEADS_PER_GROUP + h + 1) / (N_GROUPS * HEADS_PER_GROUP))


def _pair(a, h):
    return a[:, (h // 2) * LANES:(h // 2 + 1) * LANES]


def _own_lanes(a, h):
    lane = lax.broadcasted_iota(jnp.int32, a.shape, 1)
    return jnp.where((lane >= HEAD_DIM) == (h % 2 == 1), a, jnp.zeros_like(a))


def _own_rows(a, h):
    return a[(h % 2) * HEAD_DIM:(h % 2 + 1) * HEAD_DIM, :]


def _attn_fwd(qkv, col0, g):
    dil, sub, _ = qkv.shape
    nb = sub // TB
    heads = HEADS_PER_GROUP

    def body(q_ref, kp, kc, kn, vp, vc, vn, bias_ref, o_ref, lse_ref, ot_scr, s_scr, p_scr):
        i = pl.program_id(1)
        kext = _ext_window(kp, kc, kn)
        vext = _ext_window(vp, vc, vn)
        q = q_ref[...] * ATT_SCALE
        for b in range(SUBS):
            kwin, qb = kext[b * TQ:(b + 2) * TQ, :], q[b * TQ:(b + 1) * TQ, :]
            for h in range(heads):
                s_scr[b * heads + h] = _dot_nt(_pair(kwin, h), _own_lanes(_pair(qb, h), h))
        inv_den = []
        for b in range(SUBS):
            block = i * SUBS + b
            bias = bias_ref[jnp.where(block == 0, 1, 0) + jnp.where(block == nb * SUBS - 1, 2, 0)]
            lse = []
            for h in range(heads):
                s = s_scr[b * heads + h] + _slope(g, h) * bias
                m = jnp.max(s, axis=0, keepdims=True)
                p = jnp.exp(s - m)
                den = jnp.sum(p, axis=0, keepdims=True)
                p_scr[b * heads + h] = p.astype(BF16)
                inv_den.append(1.0 / den)
                lse.append(m + jnp.log(den))
            lse_ref[b * TQ:(b + 1) * TQ, :] = _head_stats(lse)
        for b in range(SUBS):
            vwin = vext[b * TQ:(b + 2) * TQ, :]
            for h in range(heads):
                ot = _dot_tn(_pair(vwin, h), p_scr[b * heads + h])
                ot_scr[h * HEAD_DIM:(h + 1) * HEAD_DIM, b * TQ:(b + 1) * TQ] = _own_rows(ot, h) * inv_den[b * heads + h]
        o_ref[...] = ot_scr[...].T

    def spec(col, shift):
        return pl.BlockSpec((None, TB, GROUP_W), lambda r, i: (r, jnp.clip(i + shift, 0, nb - 1), col))

    return pl.pallas_call(
        body, grid=(dil, nb),
        in_specs=[spec(col0, 0), spec(col0 + 1, -1), spec(col0 + 1, 0), spec(col0 + 1, 1),
                  spec(col0 + 2, -1), spec(col0 + 2, 0), spec(col0 + 2, 1),
                  pl.BlockSpec((4, 2 * TQ, TQ), lambda r, i: (0, 0, 0))],
        out_specs=[pl.BlockSpec((None, TB, GROUP_W), lambda r, i: (r, i, 0)),
                   pl.BlockSpec((None, TB, LANES), lambda r, i: (r, i, 0))],
        out_shape=[jax.ShapeDtypeStruct((dil, sub, GROUP_W), F32), jax.ShapeDtypeStruct((dil, sub, LANES), F32)],
        scratch_shapes=[pltpu.VMEM((GROUP_W, TB), F32), pltpu.VMEM((SUBS * heads, 2 * TQ, TQ), F32),
                        pltpu.VMEM((SUBS * heads, 2 * TQ, TQ), BF16)],
        name=f"attn_fwd_g{g}", compiler_params=_cparams(("parallel", "arbitrary"), 32))(
            *_hbm(*([qkv] * 7), _attn_bias_table(g)))


def _expand_heads():
    h = lax.broadcasted_iota(jnp.int32, (LANES, GROUP_W), 0)
    c = lax.broadcasted_iota(jnp.int32, (LANES, GROUP_W), 1)
    return (c // HEAD_DIM == h).astype(F32)


def _dot_f32(a, b):
    return jnp.dot(a, b, preferred_element_type=F32, precision=lax.Precision.HIGHEST)


def _attn_combine(outs, lses, *, tm=512):
    S = outs[0].shape[1]
    n_col = GROUP_W // LANES

    def body(*refs):
        ins, e_ref = refs[:2 * N_GROUPS], refs[2 * N_GROUPS]
        c_ref, cb_ref, lt_ref = refs[2 * N_GROUPS + 1:2 * N_GROUPS + 4]
        scr = refs[2 * N_GROUPS + 4:]
        o, l = [ins[0][0]], [ins[N_GROUPS][0]]
        for k, d in enumerate(DILS):
            o_ref, l_ref = ins[1 + k], ins[N_GROUPS + 1 + k]
            o.append(_from_residue(lambda r: o_ref[r], d, tm, scr[k * (n_col + 1):k * (n_col + 1) + n_col]))
            l.append(_from_residue(lambda r: l_ref[r], d, tm, scr[k * (n_col + 1) + n_col:(k + 1) * (n_col + 1)]))
        m = jnp.maximum(jnp.maximum(l[0], l[1]), l[2])
        e = [jnp.exp(v - m) for v in l]
        den = e[0] + e[1] + e[2]
        comb = sum(_dot_f32(ev / den, e_ref[...]) * ov for ev, ov in zip(e, o))
        c_ref[...] = comb
        cb_ref[...] = comb.astype(BF16)
        lt_ref[...] = m + jnp.log(den)

    row = pl.BlockSpec((tm, GROUP_W), lambda i: (i, 0))
    dils = [d for _, d in GROUPS]
    return pl.pallas_call(
        body, grid=(S // tm,),
        in_specs=[_res_spec(d, tm, GROUP_W) for d in dils] + [_res_spec(d, tm, LANES) for d in dils]
        + [_resident((LANES, GROUP_W))],
        out_specs=[row, row, pl.BlockSpec((tm, LANES), lambda i: (i, 0))],
        out_shape=[jax.ShapeDtypeStruct((S, GROUP_W), F32), jax.ShapeDtypeStruct((S, GROUP_W), BF16),
                   jax.ShapeDtypeStruct((S, LANES), F32)],
        scratch_shapes=_lane_scratch(tm, GROUP_W + LANES) * len(DILS),
        name="attn_combine", compiler_params=_cparams(("parallel",), 32))(*_hbm(*outs, *lses, _expand_heads()))


def _branch_mix(ya_in, comb_b, w_a, w_b, proj, *, tm=512):
    S = ya_in.shape[0]

    def body(ya_ref, cb_ref, wa_ref, wb_ref, ga_ref, gb_ref, yab_ref, mx_ref):
        y_a = _dot(ya_ref[...], wa_ref[...])
        y_b = _dot(cb_ref[...], wb_ref[...])
        yab_ref[:, 0:D_MODEL] = y_a.astype(BF16)
        yab_ref[:, D_MODEL:2 * D_MODEL] = y_b.astype(BF16)
        mx = jax.nn.sigmoid(ga_ref[...].astype(F32)) * y_a + jax.nn.sigmoid(gb_ref[...].astype(F32)) * y_b
        mx_ref[...] = mx.astype(BF16)

    return pl.pallas_call(
        body, grid=(S // tm,),
        in_specs=[pl.BlockSpec((tm, D_CONV), lambda i: (i, 0)), pl.BlockSpec((tm, GROUP_W), lambda i: (i, 0)),
                  pl.BlockSpec((D_CONV, D_MODEL), lambda i: (0, 0)), pl.BlockSpec((GROUP_W, D_MODEL), lambda i: (0, 0)),
                  pl.BlockSpec((tm, D_MODEL), lambda i: (i, P_GA // D_MODEL)),
                  pl.BlockSpec((tm, D_MODEL), lambda i: (i, P_GB // D_MODEL))],
        out_specs=[pl.BlockSpec((tm, 2 * D_MODEL), lambda i: (i, 0)), pl.BlockSpec((tm, D_MODEL), lambda i: (i, 0))],
        out_shape=[jax.ShapeDtypeStruct((S, 2 * D_MODEL), BF16), jax.ShapeDtypeStruct((S, D_MODEL), BF16)],
        name="branch_mix", compiler_params=_cparams(("parallel",), 40))(*_hbm(ya_in, comb_b, w_a, w_b, proj, proj))


def _mix_ln1(mixin, w_o, b_o, h0, g1, b1, *, tm=512):
    S = mixin.shape[0]

    def body(mx_ref, wo_ref, bo_ref, h0_ref, g_ref, b_ref, xh_ref, rs_ref, h1b_ref):
        z = ALPHA * h0_ref[...] + _dot(mx_ref[...], wo_ref[...]) + bo_ref[...]
        xhat, rstd = _ln_stats(z)
        xh_ref[...] = xhat
        rs_ref[...] = jnp.broadcast_to(rstd, (tm, LANES))
        h1b_ref[...] = (xhat * g_ref[...] + b_ref[...]).astype(BF16)

    row = pl.BlockSpec((tm, D_MODEL), lambda i: (i, 0))
    vec = pl.BlockSpec((1, D_MODEL), lambda i: (0, 0))
    return pl.pallas_call(
        body, grid=(S // tm,),
        in_specs=[row, pl.BlockSpec((D_MODEL, D_MODEL), lambda i: (0, 0)), vec, row, vec, vec],
        out_specs=[row, pl.BlockSpec((tm, LANES), lambda i: (i, 0)), row],
        out_shape=[jax.ShapeDtypeStruct((S, D_MODEL), F32), jax.ShapeDtypeStruct((S, LANES), F32),
                   jax.ShapeDtypeStruct((S, D_MODEL), BF16)],
        name="mix_ln1", compiler_params=_cparams(("parallel",), 40))(*_hbm(mixin, w_o, b_o, h0, g1, b1))


def _gelu_parts(cz):
    cdf = 0.5 * (1.0 + lax.erf(cz * INV_SQRT2))
    return cdf, cz * cdf


def _ffn_conv_fwd(up, cw, cb):
    S = up.shape[0]

    def body(a_ref, g_ref, w_ref, cb_ref, o_ref, a_scr):
        _zero_pads(a_scr, S)
        for t in range(0, S, CHUNK):
            a_scr[PAD + t:PAD + t + CHUNK, :] = a_ref[t:t + CHUNK, :].astype(F32)
        w0, w1, w2 = w_ref[0:1, :], w_ref[1:2, :], w_ref[2:3, :]
        for t in range(0, S, CHUNK):
            am, a0, ap = _shifted(a_scr, t)
            _, gel = _gelu_parts(w0 * am + w1 * a0 + w2 * ap + cb_ref[...])
            o_ref[t:t + CHUNK, :] = (gel * g_ref[t:t + CHUNK, :].astype(F32)).astype(BF16)

    return pl.pallas_call(
        body, grid=(D_FF // SLAB,),
        in_specs=[_slab_spec(S, 0), _slab_spec(S, D_FF), pl.BlockSpec((3, SLAB), lambda j: (0, j)),
                  pl.BlockSpec((1, SLAB), lambda j: (0, j))],
        out_specs=pl.BlockSpec((S, SLAB), lambda j: (0, j)),
        out_shape=jax.ShapeDtypeStruct((S, D_FF), BF16),
        scratch_shapes=[pltpu.VMEM((S + 2 * PAD, SLAB), F32)],
        name="ffn_conv_fwd", compiler_params=_cparams(("parallel",), 40))(*_hbm(up, up, cw, cb))


def _down_ln2_loss(f, w_down, b_down, xhat1, g1, b1, g2, b2, target, *, tm=512):
    S = f.shape[0]

    def body(f_ref, wd_ref, bd_ref, xh1_ref, g1_ref, b1_ref, g2_ref, b2_ref, t_ref, dz_ref, dzb_ref, st_ref):
        h1 = xh1_ref[...] * g1_ref[...] + b1_ref[...]
        z = ALPHA * h1 + _dot(f_ref[...], wd_ref[...]) + bd_ref[...]
        xhat, rstd = _ln_stats(z)
        err = xhat * g2_ref[...] + b2_ref[...] - t_ref[...]
        loss = (0.5 / D_MODEL) * jnp.sum(jnp.sum(err * err, axis=1, keepdims=True), axis=0, keepdims=True)
        dh2 = err * (1.0 / D_MODEL)
        dz = _ln_bwd(dh2, xhat, rstd, g2_ref[...])
        dz_ref[...] = dz
        dzb_ref[...] = dz.astype(BF16)
        upd = _rows8([jnp.sum(dh2 * xhat, axis=0, keepdims=True), jnp.sum(dh2, axis=0, keepdims=True),
                      jnp.broadcast_to(loss, (1, D_MODEL)), jnp.sum(dz, axis=0, keepdims=True)], D_MODEL)

        @pl.when(pl.program_id(0) == 0)
        def _():
            st_ref[...] = upd

        @pl.when(pl.program_id(0) != 0)
        def _():
            st_ref[...] += upd

    row = pl.BlockSpec((tm, D_MODEL), lambda i: (i, 0))
    vec = pl.BlockSpec((1, D_MODEL), lambda i: (0, 0))
    return pl.pallas_call(
        body, grid=(S // tm,),
        in_specs=[pl.BlockSpec((tm, D_FF), lambda i: (i, 0)), _resident((D_FF, D_MODEL)),
                  vec, row, vec, vec, vec, vec, row],
        out_specs=[row, row, pl.BlockSpec((SUBLANES, D_MODEL), lambda i: (0, 0))],
        out_shape=[jax.ShapeDtypeStruct((S, D_MODEL), F32), jax.ShapeDtypeStruct((S, D_MODEL), BF16),
                   jax.ShapeDtypeStruct((SUBLANES, D_MODEL), F32)],
        name="down_ln2_loss", compiler_params=_cparams(("arbitrary",), 56))(
            *_hbm(f, w_down, b_down, xhat1, g1, b1, g2, b2, target))


def _ffn_conv_bwd(up, df, cw, cb):
    S = up.shape[0]

    def body(a_ref, g_ref, df_ref, w_ref, cb_ref, dup_ref, sm_ref, a_scr, d_scr):
        _zero_pads(a_scr, S)
        _zero_pads(d_scr, S)
        for t in range(0, S, CHUNK):
            a_scr[PAD + t:PAD + t + CHUNK, :] = a_ref[t:t + CHUNK, :].astype(F32)
        w0, w1, w2 = w_ref[0:1, :], w_ref[1:2, :], w_ref[2:3, :]
        zero = jnp.zeros((1, SLAB), F32)
        s_dg, s_dcz, s_w0, s_w1, s_w2 = zero, zero, zero, zero, zero
        for t in range(0, S, CHUNK):
            am, a0, ap = _shifted(a_scr, t)
            cz = w0 * am + w1 * a0 + w2 * ap + cb_ref[...]
            cdf, gel = _gelu_parts(cz)
            dfv = df_ref[t:t + CHUNK, :].astype(F32)
            dgte = dfv * gel
            dcz = dfv * g_ref[t:t + CHUNK, :].astype(F32) * (cdf + cz * jnp.exp(-0.5 * cz * cz) * INV_SQRT_2PI)
            dup_ref[1, t:t + CHUNK, :] = dgte.astype(BF16)
            d_scr[PAD + t:PAD + t + CHUNK, :] = dcz
            s_dg = s_dg + jnp.sum(dgte, axis=0, keepdims=True)
            s_dcz = s_dcz + jnp.sum(dcz, axis=0, keepdims=True)
            s_w0 = s_w0 + jnp.sum(dcz * am, axis=0, keepdims=True)
            s_w1 = s_w1 + jnp.sum(dcz * a0, axis=0, keepdims=True)
            s_w2 = s_w2 + jnp.sum(dcz * ap, axis=0, keepdims=True)
        s_da = zero
        for t in range(0, S, CHUNK):
            dm, d0, dp = _shifted(d_scr, t)
            da = w0 * dp + w1 * d0 + w2 * dm
            dup_ref[0, t:t + CHUNK, :] = da.astype(BF16)
            s_da = s_da + jnp.sum(da, axis=0, keepdims=True)
        sm_ref[...] = _rows8([s_da, s_dg, s_dcz, s_w0, s_w1, s_w2], SLAB)

    return pl.pallas_call(
        body, grid=(D_FF // SLAB,),
        in_specs=[_slab_spec(S, 0), _slab_spec(S, D_FF), pl.BlockSpec((S, SLAB), lambda j: (0, j)),
                  pl.BlockSpec((3, SLAB), lambda j: (0, j)), pl.BlockSpec((1, SLAB), lambda j: (0, j))],
        out_specs=[pl.BlockSpec((2, S, SLAB), lambda j: (0, 0, j)), pl.BlockSpec((SUBLANES, SLAB), lambda j: (0, j))],
        out_shape=[jax.ShapeDtypeStruct((2, S, D_FF), BF16), jax.ShapeDtypeStruct((SUBLANES, D_FF), F32)],
        scratch_shapes=[pltpu.VMEM((S + 2 * PAD, SLAB), F32)] * 2,
        name="ffn_conv_bwd", compiler_params=_cparams(("parallel",), 48))(*_hbm(up, up, df, cw, cb))


def _up_bwd_ln1(dup, w_up3, dz2, xhat1, rstd1, g1, *, tm=512):
    S = dz2.shape[0]
    ns, _, tk = w_up3.shape
    per_plane = D_FF // tk

    def body(du_ref, w_ref, dz2_ref, xh_ref, rs_ref, g_ref, dz_ref, dzb_ref, st_ref):
        dh = ALPHA * dz2_ref[...]
        for k in range(ns):
            col = (k % per_plane) * tk
            dh = dh + _dot_nt(du_ref[k // per_plane, :, col:col + tk], w_ref[k])
        xhat = xh_ref[...]
        dz = _ln_bwd(dh, xhat, rs_ref[:, 0:1], g_ref[...])
        dz_ref[...] = dz
        dzb_ref[...] = dz.astype(BF16)
        upd = _rows8([jnp.sum(dh * xhat, axis=0, keepdims=True), jnp.sum(dh, axis=0, keepdims=True),
                      jnp.sum(dz, axis=0, keepdims=True)], D_MODEL)

        @pl.when(pl.program_id(0) == 0)
        def _():
            st_ref[...] = upd

        @pl.when(pl.program_id(0) != 0)
        def _():
            st_ref[...] += upd

    row = pl.BlockSpec((tm, D_MODEL), lambda i: (i, 0))
    return pl.pallas_call(
        body, grid=(S // tm,),
        in_specs=[pl.BlockSpec((dup.shape[0], tm, D_FF), lambda i: (0, i, 0)), _resident(w_up3.shape),
                  row, row, pl.BlockSpec((tm, LANES), lambda i: (i, 0)), pl.BlockSpec((1, D_MODEL), lambda i: (0, 0))],
        out_specs=[row, row, pl.BlockSpec((SUBLANES, D_MODEL), lambda i: (0, 0))],
        out_shape=[jax.ShapeDtypeStruct((S, D_MODEL), F32), jax.ShapeDtypeStruct((S, D_MODEL), BF16),
                   jax.ShapeDtypeStruct((SUBLANES, D_MODEL), F32)],
        name="up_bwd_ln1", compiler_params=_cparams(("arbitrary",), 56))(*_hbm(dup, w_up3, dz2, xhat1, rstd1, g1))


def _mix_bwd(dz1b, w_o, proj, yab, *, tm=512):
    S = dz1b.shape[0]

    def body(dz_ref, wo_ref, ga_ref, gb_ref, y_ref, dy_ref, dg_ref):
        dmx = _dot_nt(dz_ref[...], wo_ref[...])
        for k, gt_ref in enumerate((ga_ref, gb_ref)):
            sl = slice(k * D_MODEL, (k + 1) * D_MODEL)
            sg = jax.nn.sigmoid(gt_ref[...].astype(F32))
            dy_ref[:, sl] = (dmx * sg).astype(BF16)
            dg_ref[k] = (dmx * y_ref[:, sl].astype(F32) * sg * (1.0 - sg)).astype(BF16)

    row = pl.BlockSpec((tm, D_MODEL), lambda i: (i, 0))
    wide = pl.BlockSpec((tm, 2 * D_MODEL), lambda i: (i, 0))
    return pl.pallas_call(
        body, grid=(S // tm,),
        in_specs=[row, _resident(w_o.shape), pl.BlockSpec((tm, D_MODEL), lambda i: (i, P_GA // D_MODEL)),
                  pl.BlockSpec((tm, D_MODEL), lambda i: (i, P_GB // D_MODEL)), wide],
        out_specs=[wide, pl.BlockSpec((2, tm, D_MODEL), lambda i: (0, i, 0))],
        out_shape=[jax.ShapeDtypeStruct((S, 2 * D_MODEL), BF16), jax.ShapeDtypeStruct((2, S, D_MODEL), BF16)],
        name="mix_bwd", compiler_params=_cparams(("parallel",), 40))(*_hbm(dz1b, w_o, proj, proj, yab))


def _conv_gate_bwd(proj, dya_in, conv_w):
    S = proj.shape[0]

    def body(b_ref, c_ref, h_ref, dy_ref, w_ref, o_ref, sm_ref, u_scr, d_scr):
        _zero_pads(u_scr, S)
        _zero_pads(d_scr, S)
        for t in range(0, S, CHUNK):
            u_scr[PAD + t:PAD + t + CHUNK, :] = c_ref[t:t + CHUNK, :].astype(F32) * h_ref[t:t + CHUNK, :].astype(F32)
        w0, w1, w2 = w_ref[0:1, :], w_ref[1:2, :], w_ref[2:3, :]
        zero = jnp.zeros((1, SLAB), F32)
        s_w0, s_w1, s_w2 = zero, zero, zero
        for t in range(0, S, CHUNK):
            um, u0, up = _shifted(u_scr, t)
            dy = dy_ref[t:t + CHUNK, :].astype(F32)
            o_ref[0, t:t + CHUNK, :] = (dy * (w0 * um + w1 * u0 + w2 * up)).astype(BF16)
            dcv = dy * b_ref[t:t + CHUNK, :].astype(F32)
            d_scr[PAD + t:PAD + t + CHUNK, :] = dcv
            s_w0 = s_w0 + jnp.sum(dcv * um, axis=0, keepdims=True)
            s_w1 = s_w1 + jnp.sum(dcv * u0, axis=0, keepdims=True)
            s_w2 = s_w2 + jnp.sum(dcv * up, axis=0, keepdims=True)
        for t in range(0, S, CHUNK):
            dm, d0, dp = _shifted(d_scr, t)
            du = w0 * dp + w1 * d0 + w2 * dm
            o_ref[1, t:t + CHUNK, :] = (du * h_ref[t:t + CHUNK, :].astype(F32)).astype(BF16)
            o_ref[2, t:t + CHUNK, :] = (du * c_ref[t:t + CHUNK, :].astype(F32)).astype(BF16)
        sm_ref[...] = _rows8([s_w0, s_w1, s_w2], SLAB)

    return pl.pallas_call(
        body, grid=(D_CONV // SLAB,),
        in_specs=[_slab_spec(S, P_B), _slab_spec(S, P_C), _slab_spec(S, P_H),
                  pl.BlockSpec((S, SLAB), lambda j: (0, j)), pl.BlockSpec((3, SLAB), lambda j: (0, j))],
        out_specs=[pl.BlockSpec((3, S, SLAB), lambda j: (0, 0, j)), pl.BlockSpec((SUBLANES, SLAB), lambda j: (0, j))],
        out_shape=[jax.ShapeDtypeStruct((3, S, D_CONV), BF16), jax.ShapeDtypeStruct((SUBLANES, D_CONV), F32)],
        scratch_shapes=[pltpu.VMEM((S + 2 * PAD, SLAB), F32)] * 2,
        name="conv_gate_bwd", compiler_params=_cparams(("parallel",), 48))(*_hbm(proj, proj, proj, dya_in, conv_w))


def _comb_bwd(dyab, w_b, comb, lse_tot, *, tm=512):
    S = comb.shape[0]
    widths, dtypes = (GROUP_W, LANES, LANES), (BF16, F32, F32)

    def body(dy_ref, wb_ref, c_ref, lt_ref, e_ref, *rest):
        outs, scr = rest[:3 * N_GROUPS], rest[3 * N_GROUPS:]
        dcb = _dot_nt(dy_ref[...], wb_ref[...]).astype(BF16)
        dc = dcb.astype(F32)
        delta = lax.dot_general(dc * c_ref[...], e_ref[...], (((1,), (1,)), ((), ())),
                                preferred_element_type=F32, precision=lax.Precision.HIGHEST)
        for k, (val, dtype) in enumerate(zip((dc, lt_ref[...], delta), dtypes)):
            outs[k][0] = val.astype(dtype)
            _to_residue(val, [outs[3 * (1 + j) + k] for j in range(len(DILS))], DILS, tm, dtype,
                        scr[:val.shape[1] // LANES])

    out_specs, out_shape = [], []
    for _, d in GROUPS:
        out_specs += [_res_spec(d, tm, w) for w in widths]
        out_shape += [jax.ShapeDtypeStruct((d, S // d, w), t) for w, t in zip(widths, dtypes)]
    res = pl.pallas_call(
        body, grid=(S // tm,),
        in_specs=[pl.BlockSpec((tm, D_MODEL), lambda i: (i, 1)), _resident(w_b.shape),
                  pl.BlockSpec((tm, GROUP_W), lambda i: (i, 0)), pl.BlockSpec((tm, LANES), lambda i: (i, 0)),
                  _resident((LANES, GROUP_W))],
        out_specs=out_specs, out_shape=out_shape, scratch_shapes=_lane_scratch(tm, GROUP_W),
        name="comb_bwd", compiler_params=_cparams(("parallel",), 32))(*_hbm(dyab, w_b, comb, lse_tot, _expand_heads()))
    return [tuple(res[3 * g:3 * g + 3]) for g in range(N_GROUPS)]


def _attn_bwd(qkv, col0, g, dcomb, lse_tot, delta):
    dil, sub, _ = qkv.shape
    nb = sub // TB
    heads = HEADS_PER_GROUP

    def body(q_ref, kp, kc, kn, vp, vc, vn, do_ref, lse_ref, dl_ref, bias_ref, dq_ref, dk_ref, dv_ref,
             ak, av, dqt_scr, s_scr, dp_scr, ds_scr, p_scr):
        i = pl.program_id(1)

        @pl.when(i == 0)
        def _():
            ak[...] = jnp.zeros_like(ak)
            av[...] = jnp.zeros_like(av)

        @pl.when(i < nb)
        def _():
            kext = _ext_window(kp, kc, kn)
            vext = _ext_window(vp, vc, vn)
            q = q_ref[...] * ATT_SCALE
            do = do_ref[...]
            lse_t, dl_t = lse_ref[...].T, dl_ref[...].T
            for b in range(SUBS):
                rows = slice(b * TQ, (b + 1) * TQ)
                kwin, vwin = kext[b * TQ:(b + 2) * TQ, :], vext[b * TQ:(b + 2) * TQ, :]
                for h in range(heads):
                    s_scr[b * heads + h] = _dot_nt(_pair(kwin, h), _own_lanes(_pair(q[rows], h), h))
                    dp_scr[b * heads + h] = _dot_nt(_pair(vwin, h), _own_lanes(_pair(do[rows], h), h))
            for b in range(SUBS):
                cols = slice(b * TQ, (b + 1) * TQ)
                block = i * SUBS + b
                bias = bias_ref[jnp.where(block == 0, 1, 0) + jnp.where(block == nb * SUBS - 1, 2, 0)]
                for h in range(heads):
                    k = b * heads + h
                    p = jnp.exp(s_scr[k] + _slope(g, h) * bias - lse_t[h:h + 1, cols])
                    ds_scr[k] = (p * (dp_scr[k] - dl_t[h:h + 1, cols])).astype(BF16)
                    p_scr[k] = p.astype(BF16)
            for b in range(SUBS):
                kwin = kext[b * TQ:(b + 2) * TQ, :]
                for h in range(heads):
                    dqt_scr[h * HEAD_DIM:(h + 1) * HEAD_DIM, b * TQ:(b + 1) * TQ] = _own_rows(
                        _dot_tn(_pair(kwin, h), ds_scr[b * heads + h]), h)
            for b in range(SUBS):
                rows = slice(b * TQ, (b + 1) * TQ)
                acc_rows = slice(TB - RADIUS + b * TQ, TB - RADIUS + (b + 2) * TQ)
                for h in range(0, heads, 2):
                    cols = slice(h * HEAD_DIM, (h + 2) * HEAD_DIM)
                    k = b * heads + h
                    q2 = jnp.concatenate([_own_lanes(_pair(q[rows], h), h), _own_lanes(_pair(q[rows], h), h + 1)], axis=0)
                    do2 = jnp.concatenate([_own_lanes(_pair(do[rows], h), h), _own_lanes(_pair(do[rows], h), h + 1)],
                                          axis=0)
                    ak[acc_rows, cols] += _dot(jnp.concatenate([ds_scr[k], ds_scr[k + 1]], axis=1), q2)
                    av[acc_rows, cols] += _dot(jnp.concatenate([p_scr[k], p_scr[k + 1]], axis=1), do2)
            dq_ref[...] = (dqt_scr[...].T * ATT_SCALE).astype(BF16)

        dk_ref[...] = ak[0:TB, :].astype(BF16)
        dv_ref[...] = av[0:TB, :].astype(BF16)
        ak[0:2 * TB, :] = ak[TB:3 * TB, :]
        av[0:2 * TB, :] = av[TB:3 * TB, :]
        ak[2 * TB:3 * TB, :] = jnp.zeros((TB, GROUP_W), F32)
        av[2 * TB:3 * TB, :] = jnp.zeros((TB, GROUP_W), F32)

    def spec(col, shift):
        return pl.BlockSpec((None, TB, GROUP_W), lambda r, i: (r, jnp.clip(i + shift, 0, nb - 1), col))

    tok = pl.BlockSpec((None, TB, GROUP_W), lambda r, i: (r, jnp.minimum(i, nb - 1), 0))
    stat = pl.BlockSpec((None, TB, LANES), lambda r, i: (r, jnp.minimum(i, nb - 1), 0))
    dkv_spec = pl.BlockSpec((None, TB, GROUP_W), lambda r, i: (r, jnp.maximum(i - 1, 0), 0))
    return pl.pallas_call(
        body, grid=(dil, nb + 1),
        in_specs=[spec(col0, 0), spec(col0 + 1, -1), spec(col0 + 1, 0), spec(col0 + 1, 1),
                  spec(col0 + 2, -1), spec(col0 + 2, 0), spec(col0 + 2, 1), tok, stat, stat,
                  pl.BlockSpec((4, 2 * TQ, TQ), lambda r, i: (0, 0, 0))],
        out_specs=[tok, dkv_spec, dkv_spec], out_shape=[jax.ShapeDtypeStruct((dil, sub, GROUP_W), BF16)] * 3,
        scratch_shapes=[pltpu.VMEM((3 * TB, GROUP_W), F32)] * 2 + [pltpu.VMEM((GROUP_W, TB), F32)]
        + [pltpu.VMEM((SUBS * heads, 2 * TQ, TQ), F32)] * 2 + [pltpu.VMEM((SUBS * heads, 2 * TQ, TQ), BF16)] * 2,
        name=f"attn_bwd_g{g}", compiler_params=_cparams(("arbitrary", "arbitrary"), 40))(
            *_hbm(*([qkv] * 7), dcomb, lse_tot, delta, _attn_bias_table(g)))


def _in_bwd_ln0(dgated, dqkv, w_nat, w_dil, dz1, x, g0, *, tm=256):
    S = x.shape[0]
    n_gated, n_in = len(dgated), 3 * N_GROUPS

    def body(*refs):
        g_refs, d_refs = refs[:n_gated], refs[n_gated:n_gated + n_in]
        wn_ref, *wd_refs = refs[n_gated + n_in:n_gated + n_in + N_GROUPS]
        dz_ref, x_ref, g_ref, gx_ref, st_ref, *tmp_ref = refs[n_gated + n_in + N_GROUPS:]
        dh = ALPHA * dz_ref[...]
        col = 0
        for ref in g_refs:
            for k in range(ref.shape[0]):
                dh = dh + _dot_nt(ref[k], wn_ref[:, col:col + D_MODEL])
                col += D_MODEL
        for g, (_, d) in enumerate(GROUPS):
            rows = [jnp.concatenate([d_refs[3 * g + k][r] for k in range(3)], axis=1) for r in range(d)]
            w = wn_ref[:, col:col + QKV_W] if d == 1 else wd_refs[g - 1][...]
            res = _dot_nt(jnp.concatenate(rows, axis=0), w)
            if d == 1:
                dh = dh + res
            else:
                n = tm // d
                dh = dh + _from_residue(lambda r: res[r * n:(r + 1) * n, :], d, tm, tmp_ref)
        xhat, rstd = _ln_stats(x_ref[...])
        gx_ref[...] = _ln_bwd(dh, xhat, rstd, g_ref[...])
        upd = _rows8([jnp.sum(dh * xhat, axis=0, keepdims=True), jnp.sum(dh, axis=0, keepdims=True)], D_MODEL)

        @pl.when(pl.program_id(0) == 0)
        def _():
            st_ref[...] = upd

        @pl.when(pl.program_id(0) != 0)
        def _():
            st_ref[...] += upd

    row = pl.BlockSpec((tm, D_MODEL), lambda i: (i, 0))
    g_specs = [pl.BlockSpec((a.shape[0], tm, D_MODEL), lambda i: (0, i, 0)) for a in dgated]
    d_specs = []
    for _, d in GROUPS:
        d_specs += [_res_spec(d, tm, GROUP_W)] * 3
    operands = list(dgated) + [a for grp in dqkv for a in grp] + [w_nat] + list(w_dil) + [dz1, x, g0]
    return pl.pallas_call(
        body, grid=(S // tm,),
        in_specs=g_specs + d_specs + [_resident(w_nat.shape)] + [_resident(w.shape) for w in w_dil]
        + [row, row, pl.BlockSpec((1, D_MODEL), lambda i: (0, 0))],
        out_specs=[row, pl.BlockSpec((SUBLANES, D_MODEL), lambda i: (0, 0))],
        out_shape=[jax.ShapeDtypeStruct((S, D_MODEL), F32), jax.ShapeDtypeStruct((SUBLANES, D_MODEL), F32)],
        scratch_shapes=_lane_scratch(tm, D_MODEL),
        name="in_bwd_ln0", compiler_params=_cparams(("arbitrary",), 52))(*_hbm(*operands))


HBM_SPEC = pl.BlockSpec(memory_space=pltpu.HBM)


def _place():
    x, y, c = lax.axis_index("x"), lax.axis_index("y"), lax.axis_index("c")
    chips = [(1 - x, y), (x, 1 - y), (1 - x, 1 - y)]
    return x, y, c, chips


def _allgather_shards(shards, after, *, name, collective_id):
    n = len(shards)
    per = 6

    def body(*refs):
        ins, outs = refs[:n], refs[n + len(after):2 * n + len(after)]
        send_sems, recv_sems, loc_sems = refs[2 * n + len(after):]
        x, y, c, chips = _place()
        me = 2 * x + y
        sib = (x, y, 1 - c)
        peers = [sib] + [(px, py, c) for px, py in chips]
        barrier = pltpu.get_barrier_semaphore()
        for peer in peers:
            pl.semaphore_signal(barrier, inc=1, device_id=peer, device_id_type=MESH)
        pl.semaphore_wait(barrier, len(peers))

        def rcopy(w, k, src, dst, to):
            return pltpu.make_async_remote_copy(src_ref=src, dst_ref=dst, send_sem=send_sems.at[per * w + k],
                                                recv_sem=recv_sems.at[per * w + k], device_id=to, device_id_type=MESH)

        split = [s.shape[0] == N_CORES for s in shards]
        half = lambda w: c if split[w] else 0
        local, sends = [], []
        for w in range(n):
            cp = pltpu.make_async_copy(ins[w], outs[w].at[me], loc_sems.at[w])
            cp.start()
            local.append(cp)
            for j, (px, py) in enumerate(chips):
                cp = rcopy(w, j, ins[w].at[half(w)], outs[w].at[me, half(w)], (px, py, c))
                cp.start()
                sends.append(cp)
        for w in range(n):
            for j, (px, py) in enumerate(chips):
                slot = outs[w].at[2 * px + py, half(w)]
                rcopy(w, j, slot, slot, (px, py, c)).wait_recv()
                if split[w]:
                    cp = rcopy(w, 3 + j, slot, slot, sib)
                    cp.start()
                    sends.append(cp)
        for w in range(n):
            if split[w]:
                for j, (px, py) in enumerate(chips):
                    slot = outs[w].at[2 * px + py, 1 - c]
                    rcopy(w, 3 + j, slot, slot, sib).wait_recv()
        for cp in sends:
            cp.wait_send()
        for cp in local:
            cp.wait()

    return pl.kernel(
        body, out_type=[jax.ShapeDtypeStruct((N_CHIPS,) + s.shape, s.dtype) for s in shards],
        mesh=plsc.ScalarSubcoreMesh(axis_name="sequencer", num_cores=1),
        scratch_types=[pltpu.SemaphoreType.DMA((per * n,)), pltpu.SemaphoreType.DMA((per * n,)),
                       pltpu.SemaphoreType.DMA((n,))],
        name=name, compiler_params=pltpu.CompilerParams(collective_id=collective_id))(*shards, *after)


def _exchange_grads(grads, *, name, collective_id):
    n = len(grads)
    per = 7

    def body(*refs):
        ins, outs = refs[:n], refs[n:2 * n]
        send_sems, recv_sems, loc_sems = refs[2 * n:]
        x, y, c, chips = _place()
        me = 2 * x + y
        sib = (x, y, 1 - c)
        peers = [sib] + [(px, py, c) for px, py in chips]
        barrier = pltpu.get_barrier_semaphore()
        for peer in peers:
            pl.semaphore_signal(barrier, inc=1, device_id=peer, device_id_type=MESH)
        pl.semaphore_wait(barrier, len(peers))

        def rcopy(w, k, src, dst, to):
            return pltpu.make_async_remote_copy(src_ref=src, dst_ref=dst, send_sem=send_sems.at[per * w + k],
                                                recv_sem=recv_sems.at[per * w + k], device_id=to, device_id_type=MESH)

        local, sends = [], []
        for w in range(n):
            cp = pltpu.make_async_copy(ins[w].at[me], outs[w].at[c, me], loc_sems.at[w])
            cp.start()
            local.append(cp)
            cp = rcopy(w, 0, ins[w].at[me], outs[w].at[c, me], sib)
            cp.start()
            sends.append(cp)
            for j, (px, py) in enumerate(chips):
                cp = rcopy(w, 1 + j, ins[w].at[2 * px + py], outs[w].at[c, me], (px, py, c))
                cp.start()
                sends.append(cp)
        for w in range(n):
            for j, (px, py) in enumerate(chips):
                slot = outs[w].at[c, 2 * px + py]
                rcopy(w, 1 + j, slot, slot, (px, py, c)).wait_recv()
                cp = rcopy(w, 4 + j, slot, slot, sib)
                cp.start()
                sends.append(cp)
        for w in range(n):
            slot = outs[w].at[1 - c, me]
            rcopy(w, 0, slot, slot, sib).wait_recv()
            for j, (px, py) in enumerate(chips):
                slot = outs[w].at[1 - c, 2 * px + py]
                rcopy(w, 4 + j, slot, slot, sib).wait_recv()
        for cp in sends:
            cp.wait_send()
        for cp in local:
            cp.wait()

    return pl.kernel(
        body, out_type=[jax.ShapeDtypeStruct((N_CORES,) + g.shape, g.dtype) for g in grads],
        mesh=plsc.ScalarSubcoreMesh(axis_name="sequencer", num_cores=1),
        scratch_types=[pltpu.SemaphoreType.DMA((per * n,)), pltpu.SemaphoreType.DMA((per * n,)),
                       pltpu.SemaphoreType.DMA((n,))],
        name=name, compiler_params=pltpu.CompilerParams(collective_id=collective_id))(*grads)


def _allgather_small(vec, after):
    def body(v_ref, _, o_ref, send_sems, recv_sems, loc_sem):
        x, y, c = lax.axis_index("x"), lax.axis_index("y"), lax.axis_index("c")
        me = 4 * x + 2 * y + c

        def peer(k):
            flip = lambda v, bit: 1 - v if (k >> bit) & 1 else v
            return flip(x, 2), flip(y, 1), flip(c, 0)

        loc = pltpu.make_async_copy(v_ref, o_ref.at[me], loc_sem)
        loc.start()
        sends = []
        for k in range(1, N_DEV):
            cp = pltpu.make_async_remote_copy(src_ref=v_ref, dst_ref=o_ref.at[me], send_sem=send_sems.at[k - 1],
                                              recv_sem=recv_sems.at[k - 1], device_id=peer(k), device_id_type=MESH)
            cp.start()
            sends.append(cp)
        for k in range(1, N_DEV):
            px, py, pc = peer(k)
            pltpu.make_async_remote_copy(src_ref=v_ref, dst_ref=o_ref.at[4 * px + 2 * py + pc],
                                         send_sem=send_sems.at[k - 1], recv_sem=recv_sems.at[k - 1],
                                         device_id=(px, py, pc), device_id_type=MESH).wait_recv()
        for cp in sends:
            cp.wait_send()
        loc.wait()

    return pl.pallas_call(
        body, in_specs=[HBM_SPEC, HBM_SPEC], out_specs=HBM_SPEC,
        out_shape=jax.ShapeDtypeStruct((N_DEV,) + vec.shape, vec.dtype),
        scratch_shapes=[pltpu.SemaphoreType.DMA((N_DEV - 1,)), pltpu.SemaphoreType.DMA((N_DEV - 1,)),
                        pltpu.SemaphoreType.DMA],
        name="allgather_small")(vec, after)


def _adamw(w, g, m, v):
    m = ADAM_B1 * m + (1.0 - ADAM_B1) * g
    v = ADAM_B2 * v + (1.0 - ADAM_B2) * (g * g)
    m_hat = m / (1.0 - ADAM_B1 ** ADAM_STEP)
    v_hat = v / (1.0 - ADAM_B2 ** ADAM_STEP)
    delta = -ADAM_LR * (m_hat / (jnp.sqrt(v_hat) + ADAM_EPS) + ADAM_WD * w)
    return delta, m, v


def _reduce_adamw(parts, w, m, v, *, tr, name):
    R, C = w.shape

    def body(p_ref, w_ref, m_ref, v_ref, g_ref, d_ref, nm_ref, nv_ref):
        def core_sum(cc):
            s = p_ref[cc, 0].astype(F32)
            for k in range(1, N_CHIPS):
                s = s + p_ref[cc, k].astype(F32)
            return s

        g = core_sum(0) + core_sum(1)
        delta, nm, nv = _adamw(w_ref[...], g, m_ref[...], v_ref[...])
        g_ref[...] = g
        d_ref[...] = delta
        nm_ref[...] = nm
        nv_ref[...] = nv

    blk = pl.BlockSpec((tr, C), lambda i: (i, 0))
    return pl.pallas_call(
        body, grid=(R // tr,),
        in_specs=[pl.BlockSpec((N_CORES, N_CHIPS, tr, C), lambda i: (0, 0, i, 0)), blk, blk, blk],
        out_specs=[blk] * 4, out_shape=[jax.ShapeDtypeStruct((R, C), F32)] * 4,
        name=name, compiler_params=_cparams(("parallel",), 40))(*_hbm(parts, w, m, v))


def _reduce_adamw_vectors(allv, offs, ws, ms, vs):
    n = len(ws)

    def body(a_ref, *refs):
        w_refs, m_refs, v_refs = refs[:n], refs[n:2 * n], refs[2 * n:3 * n]
        tot_ref, outs = refs[3 * n], refs[3 * n + 1:]
        s = a_ref[0]
        for d in range(1, N_DEV):
            s = s + a_ref[d]
        tot_ref[...] = s
        for k in range(n):
            g = s[:, offs[k]:offs[k] + w_refs[k].shape[1]]
            delta, nm, nv = _adamw(w_refs[k][...], g, m_refs[k][...], v_refs[k][...])
            for ref, val in zip(outs[4 * k:4 * k + 4], (g, delta, nm, nv)):
                ref[...] = val

    out_shape = [jax.ShapeDtypeStruct(allv.shape[1:], F32)]
    for w in ws:
        out_shape += [jax.ShapeDtypeStruct(w.shape, F32)] * 4
    res = pl.pallas_call(body, out_shape=out_shape, name="reduce_adamw_vectors",
                         compiler_params=_cparams((), 40))(allv, *ws, *ms, *vs)
    return res[0], [tuple(res[1 + 4 * k:5 + 4 * k]) for k in range(n)]


def _adamw_taps(ws, gs, ms, vs):
    n = len(ws)

    def body(*refs):
        outs = refs[4 * n:]
        for k in range(n):
            res = _adamw(refs[k][...], refs[n + k][...], refs[2 * n + k][...], refs[3 * n + k][...])
            for ref, val in zip(outs[3 * k:3 * k + 3], res):
                ref[...] = val

    out_shape = []
    for w in ws:
        out_shape += [jax.ShapeDtypeStruct(w.shape, F32)] * 3
    res = pl.pallas_call(body, out_shape=out_shape, name="adamw_taps")(*ws, *gs, *ms, *vs)
    return [tuple(res[3 * k:3 * k + 3]) for k in range(n)]


def _pack(pieces):
    flat, offs, n = [], [], 0
    for p in pieces:
        size = -(-p.size // LANES) * LANES
        flat.append(jnp.pad(p.reshape(-1), (0, size - p.size)))
        offs.append(n)
        n += size
    return jnp.concatenate(flat).reshape(1, n), offs


def _local_step(x, target, p, wfull, on_ready=lambda group: None, before_ln0=()):
    S = x.shape[0]
    dils = [d for _, d in GROUPS]

    h0, h0b, *h0_res = _ln0_fwd(x, p["ln0_g"], p["ln0_b"], before_ln0)
    h0_rows = [h0b] + [h.reshape(S, D_MODEL) for h in h0_res]

    if isinstance(wfull, dict):
        w_in3, pending = wfull["w_in"], None
    else:
        w_in3, launch_rest, assemble = wfull
        w_in3, h0b = lax.optimization_barrier((w_in3, h0b))
        pending = launch_rest(h0b)

    runs = _col_runs()
    w_perm = jnp.concatenate([w_in3[s, :, c:c + w] for s, c, _, w in runs], axis=1)
    b_blocks = p["b_in"].reshape(N_BLK, GROUP_W)
    b_perm = jnp.concatenate([b_blocks[b] for b in PERM]).reshape(1, N_IN)
    w_nat, b_nat = w_perm[:, :N_NAT], b_perm[:, :N_NAT]
    qkv_cols = [slice(P_Q0 + g * QKV_W, P_Q0 + (g + 1) * QKV_W) for g in range(N_GROUPS)]
    w_qkv = [w_perm[:, c] for c in qkv_cols]

    proj = _mm_nn(h0b, w_nat, b_nat, tm=512, tn=N_NAT // 2, out_dtype=BF16, name="proj")
    qkv = [proj[None]]
    for g in range(1, N_GROUPS):
        t = _mm_nn(h0_rows[g], w_qkv[g], b_perm[:, qkv_cols[g]], tm=512, tn=QKV_W, out_dtype=BF16, name=f"proj_qkv{g}")
        qkv.append(t.reshape(dils[g], S // dils[g], QKV_W))
    if pending is not None:
        pending, qkv = lax.optimization_barrier((pending, qkv))
        proj = qkv[0][0]
        wfull = assemble(pending)
    w_up3 = wfull["w_up"]
    w_a, w_o, w_down, w_b = wfull["w_a"], wfull["w_o"], wfull["w_down"], wfull["w_b"]
    conv_w, ffn_conv_w = wfull["conv_w"], wfull["ffn_conv_w"]
    col0 = [P_Q0 // GROUP_W] + [0] * (N_GROUPS - 1)
    ya_in = _conv_gate_fwd(proj, conv_w)
    att = [_attn_fwd(qkv[g], col0[g], g) for g in range(N_GROUPS)]
    comb, comb_b, lse_tot = _attn_combine([a[0] for a in att], [a[1] for a in att])
    yab, mixin = _branch_mix(ya_in, comb_b, w_a, w_b, proj)
    xhat1, rstd1, h1b = _mix_ln1(mixin, w_o, p["b_o"], h0, p["ln1_g"], p["ln1_b"])
    up = _mm_nn(h1b, w_up3, p["b_up"], tm=512, tn=w_up3.shape[2], out_dtype=BF16, name="up")
    f = _ffn_conv_fwd(up, ffn_conv_w, p["ffn_conv_b"])
    dz2, dz2b, st2 = _down_ln2_loss(f, w_down, p["b_down"], xhat1, p["ln1_g"], p["ln1_b"],
                                    p["ln2_g"], p["ln2_b"], target)

    gw = {}
    gw["w_down"] = _mm_tn(f, dz2b, n_out=1, tn=D_MODEL, ts=1024, g_block=(1024, D_MODEL),
                          g_map=lambda j, s: (s, 0), name="grad_w_down").reshape(N_CHIPS, D_FF // N_CHIPS, D_MODEL)
    df = _mm_nt(dz2b, w_down, tm=512, name="df")
    dup, sm_ffn = _ffn_conv_bwd(up, df, ffn_conv_w, p["ffn_conv_b"])
    up_tn = w_up3.shape[2]
    up_pp = D_FF // up_tn
    gw["w_up"] = _mm_tn(h1b, dup, n_out=N_CHIPS, tn=up_tn, ts=1024, g_block=(None, 1024, up_tn),
                        g_map=lambda j, s: (j // up_pp, s, j % up_pp), name="grad_w_up")
    exchanged = on_ready({n: gw[n] for n in ("w_down", "w_up")}) or {}
    dz1, dz1b, st1 = _up_bwd_ln1(dup, w_up3, dz2, xhat1, rstd1, p["ln1_g"])

    gw["w_o"] = _mm_tn(mixin, dz1b, n_out=1, tn=D_MODEL, ts=512, g_block=(512, D_MODEL),
                       g_map=lambda j, s: (s, 0), name="grad_w_o").reshape(N_CHIPS, D_MODEL // N_CHIPS, D_MODEL)
    dyab, dgab = _mix_bwd(dz1b, w_o, proj, yab)
    gw["w_a"] =_mm_tn(ya_in, dyab, n_out=1, tn=D_MODEL, ts=512, g_block=(512, D_MODEL),
                       g_map=lambda j, s: (s, 0), name="grad_w_a").reshape(N_CHIPS, D_CONV // N_CHIPS, D_MODEL)
    gw_b = _mm_tn(comb_b, dyab, n_out=1, tn=D_MODEL, ts=1024, g_block=(1024, D_MODEL),
                  g_map=lambda j, s: (s, 1), name="grad_w_b")
    gw["w_b"] = gw_b.reshape(GROUP_W, N_CHIPS, D_MODEL // N_CHIPS).transpose(1, 0, 2)
    exchanged_mix = on_ready({n: gw[n] for n in ("w_o", "w_a", "w_b")}) or {}
    dya_in = _mm_nt(dyab, w_a, tm=512, a_col=0, name="dya_in")
    exchanged, dya_in = lax.optimization_barrier((exchanged, dya_in))
    dbch, sm_conv = _conv_gate_bwd(proj, dya_in, conv_w)
    att_stats = _comb_bwd(dyab, w_b, comb, lse_tot)
    exchanged_mix, att_stats = lax.optimization_barrier((exchanged_mix, att_stats))
    exchanged.update(exchanged_mix)
    dqkv = [_attn_bwd(qkv[g], col0[g], g, *att_stats[g]) for g in range(N_GROUPS)]

    w_pieces, b_pieces = [], []
    for nm, planes in (("bch", dbch), ("gab", dgab)):
        pw, pc = _mm_tn(h0b, planes, n_out=planes.shape[0], tn=D_MODEL, ts=1024, g_block=(None, 1024, D_MODEL),
                        g_map=lambda j, s: (j, s, 0), colsum=True, name="grad_w_in_" + nm)
        w_pieces.extend(pw[k] for k in range(planes.shape[0]))
        b_pieces.append(pc[0])
    for g in range(N_GROUPS):
        pw, pc = _mm_tn_cat(h0_rows[g], [a.reshape(S, GROUP_W) for a in dqkv[g]], ts=1024, name=f"grad_w_in_qkv{g}")
        w_pieces.append(pw)
        b_pieces.append(pc[0])
    dw_perm = jnp.concatenate(w_pieces, axis=1)
    gw["w_in"] = jnp.stack([
        jnp.concatenate([dw_perm[:, pc:pc + w] for s, c, pc, w in sorted(runs, key=lambda r: r[1]) if s == k], axis=1)
        for k in range(N_CHIPS)])
    exchanged.update(on_ready({"w_in": gw["w_in"]}) or {})
    db_blocks = jnp.concatenate(b_pieces).reshape(N_BLK, GROUP_W)
    grad_b_in = jnp.concatenate([db_blocks[b] for b in INV_PERM])

    grad_x, st0 = _in_bwd_ln0([dbch, dgab], dqkv, w_nat, w_qkv[1:], dz1, x, p["ln0_g"])

    small = {
        "loss": st2[2:3, 0:1],
        "ln0_g": st0[0], "ln0_b": st0[1], "b_in": grad_b_in, "conv_w": sm_conv[0:3],
        "b_o": st1[2], "ln1_g": st1[0], "ln1_b": st1[1],
        "b_up": jnp.concatenate([sm_ffn[0], sm_ffn[1]]), "ffn_conv_w": sm_ffn[3:6], "ffn_conv_b": sm_ffn[2],
        "b_down": st2[3], "ln2_g": st2[0], "ln2_b": st2[1],
    }
    return grad_x, exchanged or gw, small


BIG =("w_in", "w_a", "w_b", "w_o", "w_up", "w_down")
CONV = ("conv_w", "ffn_conv_w")
VECS = ("ln0_g", "ln0_b", "b_in", "b_o", "ln1_g", "ln1_b", "b_up", "ffn_conv_b", "b_down", "ln2_g", "ln2_b")
ORDER = ("ln0_g", "ln0_b", "w_in", "b_in", "conv_w", "w_a", "w_b", "w_o", "b_o", "ln1_g", "ln1_b", "w_up", "b_up",
         "ffn_conv_w", "ffn_conv_b", "w_down", "b_down", "ln2_g", "ln2_b")
SMALL_ORDER = ("loss",) + VECS + CONV


def _step(x, target, W, Mo, Vo):
    x2, t2 = x[0], target[0]
    big2 = {n: W[n][0] for n in BIG}
    halves = lambda a: a.astype(BF16).reshape(N_CORES, a.shape[0] // N_CORES, a.shape[1])
    whole = lambda g: g.reshape(N_CHIPS, g.shape[1] * g.shape[2], g.shape[3])
    later = tuple(n for n in BIG if n != "w_in")
    w_in_halves = halves(big2["w_in"])
    first = _allgather_shards([w_in_halves], [], name="allgather_w_in", collective_id=1)

    def launch_rest(h0b):
        return _allgather_shards([halves(big2[n]) for n in later] + [W[n] for n in CONV], [h0b],
                                 name="allgather_rest", collective_id=2)

    def assemble(rest):
        gathered = {n: whole(g) for n, g in zip(later + CONV, rest)}
        return {
            "w_up": gathered["w_up"],
            "w_a": gathered["w_a"].reshape(D_CONV, D_MODEL), "w_o": gathered["w_o"].reshape(D_MODEL, D_MODEL),
            "w_down": gathered["w_down"].reshape(D_FF, D_MODEL),
            "w_b": gathered["w_b"].transpose(1, 0, 2).reshape(GROUP_W, D_MODEL),
            "conv_w": gathered["conv_w"].transpose(1, 0, 2).reshape(3, D_CONV),
            "ffn_conv_w": gathered["ffn_conv_w"].transpose(1, 0, 2).reshape(3, D_FF),
        }

    pvec = {n: W[n].reshape(1, -1) for n in VECS}

    exchange_ids = iter((3, 4, 5))

    def exchange(group):
        names = tuple(group)
        res = _exchange_grads([group[n] for n in names], name="exchange_" + "_".join(names),
                              collective_id=next(exchange_ids))
        return dict(zip(names, res))

    grad_x, parts, small = _local_step(x2, t2, pvec, (whole(first[0]), launch_rest, assemble), exchange,
                                       before_ln0=[w_in_halves])
    out = {}
    for n in BIG:
        tr = {"w_in": 128, "w_up": 128, "w_b": 128}.get(n, big2[n].shape[0] // 4)
        g, d, nm, nv = _reduce_adamw(parts[n], big2[n], Mo[n][0], Vo[n][0], tr=tr, name="adamw_" + n)
        out[n] = tuple(a[None] for a in (g, d, nm, nv))

    vec, offs = _pack([small[n] for n in SMALL_ORDER])
    off = dict(zip(SMALL_ORDER, offs))
    row = lambda a: a.reshape(1, -1)
    allv = _allgather_small(vec, parts["w_in"])
    tot, vec_out = _reduce_adamw_vectors(allv, [off[n] for n in VECS], [row(W[n]) for n in VECS],
                                         [row(Mo[n]) for n in VECS], [row(Vo[n]) for n in VECS])
    for n, res in zip(VECS, vec_out):
        out[n] = tuple(a.reshape(W[n].shape) for a in res)
    loss = tot[0, off["loss"]]
    chip = 2 * lax.axis_index("x") + lax.axis_index("y")
    taps_g = []
    for n in CONV:
        width = W[n].shape[2]
        full = lax.slice(tot, (0, off[n]), (1, off[n] + 3 * N_CHIPS * width)).reshape(3, N_CHIPS * width)
        taps_g.append(lax.dynamic_slice_in_dim(full, chip * width, width, axis=1))
    taps_out = _adamw_taps([W[n][0] for n in CONV], taps_g, [Mo[n][0] for n in CONV], [Vo[n][0] for n in CONV])
    for n, g, res in zip(CONV, taps_g, taps_out):
        out[n] = tuple(a[None] for a in (g,) + res)

    res = [loss, grad_x[None]]
    for k in range(4):
        res += [out[n][k] for n in ORDER]
    return tuple(res)


def kernel(x, ln0_g, ln0_b, w_in, b_in, conv_w, w_a, w_b, w_o, b_o, ln1_g, ln1_b, w_up, b_up, ffn_conv_w, ffn_conv_b, w_down, b_down, ln2_g, ln2_b, loss_target, m_ln0_g, m_ln0_b, m_w_in, m_b_in, m_conv_w, m_w_a, m_w_b, m_w_o, m_b_o, m_ln1_g, m_ln1_b, m_w_up, m_b_up, m_ffn_conv_w, m_ffn_conv_b, m_w_down, m_b_down, m_ln2_g, m_ln2_b, v_ln0_g, v_ln0_b, v_w_in, v_b_in, v_conv_w, v_w_a, v_w_b, v_w_o, v_b_o, v_ln1_g, v_ln1_b, v_w_up, v_b_up, v_ffn_conv_w, v_ffn_conv_b, v_w_down, v_b_down, v_ln2_g, v_ln2_b):
    W = dict(zip(ORDER, (ln0_g, ln0_b, w_in, b_in, conv_w, w_a, w_b, w_o, b_o, ln1_g, ln1_b, w_up, b_up,
                         ffn_conv_w, ffn_conv_b, w_down, b_down, ln2_g, ln2_b)))
    Mo = dict(zip(ORDER, (m_ln0_g, m_ln0_b, m_w_in, m_b_in, m_conv_w, m_w_a, m_w_b, m_w_o, m_b_o, m_ln1_g, m_ln1_b,
                          m_w_up, m_b_up, m_ffn_conv_w, m_ffn_conv_b, m_w_down, m_b_down, m_ln2_g, m_ln2_b)))
    Vo = dict(zip(ORDER, (v_ln0_g, v_ln0_b, v_w_in, v_b_in, v_conv_w, v_w_a, v_w_b, v_w_o, v_b_o, v_ln1_g, v_ln1_b,
                          v_w_up, v_b_up, v_ffn_conv_w, v_ffn_conv_b, v_w_down, v_b_down, v_ln2_g, v_ln2_b)))
    return _step(x, loss_target, W, Mo, Vo)
```

```python
import functools
import math

import jax
import jax.numpy as jnp
from jax import lax
from jax.experimental import pallas as pl
from jax.experimental.pallas import tpu as pltpu
from jax.experimental.pallas import tpu_sc as plsc

F32 = jnp.float32
BF16 = jnp.bfloat16

D_MODEL = 1024
D_CONV = D_MODEL
HEAD_DIM = 64
HEADS_PER_GROUP = 8
GROUPS = ((128, 1), (512, 4), (2048, 16))
N_GROUPS = len(GROUPS)
GROUP_W = HEADS_PER_GROUP * HEAD_DIM
QKV_W = N_GROUPS * GROUP_W
RADIUS = 64
D_FF = 2816
LN_EPS = 1e-5
ALPHA = 2.0 ** 0.25
MASK_VALUE = -1e30
ATT_SCALE = HEAD_DIM ** -0.5
OFF_B = 0
OFF_C = OFF_B + D_CONV
OFF_H = OFF_C + D_CONV
OFF_Q = OFF_H + D_CONV
OFF_K = OFF_Q + QKV_W
OFF_V = OFF_K + QKV_W
OFF_GA = OFF_V + QKV_W
OFF_GB = OFF_GA + D_MODEL
N_IN = OFF_GB + D_MODEL
ADAM_LR = 0.001
ADAM_B1 = 0.9
ADAM_B2 = 0.999
ADAM_EPS = 1e-08
ADAM_WD = 0.01
ADAM_STEP = 10
INV_SQRT2 = 0.7071067811865476
INV_SQRT_2PI = 0.3989422804014327

LANES = 128
SUBLANES = 8
VMEM_BYTES_V7X = 64 * 1024 * 1024
N_CHIPS = 4
N_CORES = 2
N_DEV = N_CHIPS * N_CORES
MESH = pl.DeviceIdType.MESH

N_BLK = N_IN // GROUP_W
PERM = (0, 1, 2, 3, 4, 5, 15, 16, 17, 18, 6, 9, 12, 7, 10, 13, 8, 11, 14)
INV_PERM = tuple(PERM.index(b) for b in range(N_BLK))
P_B, P_C, P_H, P_GA, P_GB, P_Q0 = 0, 1024, 2048, 3072, 4096, 5120
N_NAT = P_Q0 + QKV_W // N_GROUPS * 3
N_GATED = P_Q0

def _col_runs():
    shard_w = N_IN // N_CHIPS
    runs = []
    for pos, blk in enumerate(PERM):
        c, end = blk * GROUP_W, (blk + 1) * GROUP_W
        while c < end:
            stop = min(end, (c // shard_w + 1) * shard_w)
            runs.append((c // shard_w, c % shard_w, pos * GROUP_W + c - blk * GROUP_W, stop - c))
            c = stop
    return runs


SLAB = 128
CHUNK = 256
PAD = SUBLANES
TQ = 128


def _cparams(sem, vmem_mb):
    assert vmem_mb * 1024 * 1024 < VMEM_BYTES_V7X
    return pltpu.CompilerParams(dimension_semantics=sem, vmem_limit_bytes=vmem_mb * 1024 * 1024)


def _resident(shape):
    nd = len(shape)
    return pl.BlockSpec(shape, lambda *_: (0,) * nd, pipeline_mode=pl.Buffered(1))


def _hbm(*arrays):
    return [pltpu.with_memory_space_constraint(a, pltpu.HBM) for a in arrays]


def _dot(a, b):
    return jnp.dot(a, b, preferred_element_type=F32)


def _dot_nt(a, b):
    return lax.dot_general(a, b, (((1,), (1,)), ((), ())), preferred_element_type=F32)


def _dot_tn(a, b):
    return lax.dot_general(a, b, (((0,), (0,)), ((), ())), preferred_element_type=F32)


def _ln_stats(z):
    mu = jnp.mean(z, -1, keepdims=True)
    zc = z - mu
    var = jnp.mean(zc * zc, -1, keepdims=True)
    rstd = lax.rsqrt(var + LN_EPS)
    return zc * rstd, rstd


def _ln_bwd(dh, xhat, rstd, g):
    dxh = dh * g
    m1 = jnp.mean(dxh, -1, keepdims=True)
    m2 = jnp.mean(dxh * xhat, -1, keepdims=True)
    return rstd * (dxh - m1 - xhat * m2)


def _rows8(rows, width):
    pad = [jnp.zeros((1, width), F32)] * (SUBLANES - len(rows))
    return jnp.concatenate(list(rows) + pad, axis=0)


def _mm_nn(a, w, bias, *, tm, tn, out_dtype, name, vmem_mb=40):
    M, K = a.shape
    if w.ndim == 3:
        assert w.shape[2] == tn
        n_tiles = w.shape[0]
        w_spec = pl.BlockSpec((None, K, tn), lambda j, i: (j, 0, 0))
    else:
        n_tiles = w.shape[1] // tn
        w_spec = pl.BlockSpec((K, tn), lambda j, i: (0, j))

    def body(a_ref, w_ref, b_ref, o_ref):
        o_ref[...] = (_dot(a_ref[...], w_ref[...]) + b_ref[...]).astype(o_ref.dtype)

    return pl.pallas_call(
        body, grid=(n_tiles, M // tm),
        in_specs=[pl.BlockSpec((tm, K), lambda j, i: (i, 0)), w_spec, pl.BlockSpec((1, tn), lambda j, i: (0, j))],
        out_specs=pl.BlockSpec((tm, tn), lambda j, i: (i, j)),
        out_shape=jax.ShapeDtypeStruct((M, n_tiles * tn), out_dtype),
        name=name, compiler_params=_cparams(("arbitrary", "parallel"), vmem_mb))(*_hbm(a, w, bias))


def _mm_nt(a, w, *, tm, a_col=0, name, vmem_mb=40):
    M = a.shape[0]
    N, K = w.shape

    def body(a_ref, w_ref, o_ref):
        o_ref[...] = _dot_nt(a_ref[...], w_ref[...]).astype(o_ref.dtype)

    return pl.pallas_call(
        body, grid=(M // tm,),
        in_specs=[pl.BlockSpec((tm, K), lambda i: (i, a_col)),
                  pl.BlockSpec((N, K), lambda i: (0, 0))],
        out_specs=pl.BlockSpec((tm, N), lambda i: (i, 0)),
        out_shape=jax.ShapeDtypeStruct((M, N), BF16),
        name=name, compiler_params=_cparams(("parallel",), vmem_mb))(*_hbm(a, w))


def _mm_tn(a, g, *, n_out, tn, ts, g_block, g_map, colsum=False, name, vmem_mb=48):
    S, K = a.shape
    n_s = S // ts

    def body(a_ref, g_ref, *rest):
        if colsum:
            o_ref, cs_ref, acc_ref, cacc_ref = rest
        else:
            o_ref, acc_ref = rest
        s = pl.program_id(1)

        @pl.when(s == 0)
        def _():
            acc_ref[...] = jnp.zeros_like(acc_ref)
            if colsum:
                cacc_ref[...] = jnp.zeros_like(cacc_ref)

        gv = g_ref[...]
        acc_ref[...] += _dot_tn(a_ref[...], gv)
        if colsum:
            cacc_ref[...] += jnp.broadcast_to(jnp.sum(gv.astype(F32), axis=0, keepdims=True), cacc_ref.shape)

        @pl.when(s == n_s - 1)
        def _():
            o_ref[...] = acc_ref[...].astype(o_ref.dtype)
            if colsum:
                cs_ref[...] = cacc_ref[...]

    out_specs = [pl.BlockSpec((None, K, tn), lambda j, s: (j, 0, 0))]
    out_shape = [jax.ShapeDtypeStruct((n_out, K, tn), BF16)]
    scratch = [pltpu.VMEM((K, tn), F32)]
    if colsum:
        out_specs.append(pl.BlockSpec((SUBLANES, tn), lambda j, s: (0, j)))
        out_shape.append(jax.ShapeDtypeStruct((SUBLANES, n_out * tn), F32))
        scratch.append(pltpu.VMEM((SUBLANES, tn), F32))
    res = pl.pallas_call(
        body, grid=(n_out, n_s),
        in_specs=[pl.BlockSpec((ts, K), lambda j, s: (s, 0)), pl.BlockSpec(g_block, g_map)],
        out_specs=out_specs, out_shape=out_shape, scratch_shapes=scratch,
        name=name, compiler_params=_cparams(("parallel", "arbitrary"), vmem_mb))(*_hbm(a, g))
    return res if colsum else res[0]


def _mm_tn_cat(a, gs, *, ts, name, vmem_mb=40):
    S, K = a.shape
    widths = [g.shape[1] for g in gs]
    n_s, total = S // ts, sum(widths)

    def body(*refs):
        a_ref, g_refs = refs[0], refs[1:1 + len(gs)]
        o_ref, cs_ref, acc_ref, cacc_ref = refs[1 + len(gs):]
        s = pl.program_id(0)

        @pl.when(s == 0)
        def _():
            acc_ref[...] = jnp.zeros_like(acc_ref)
            cacc_ref[...] = jnp.zeros_like(cacc_ref)

        av, col = a_ref[...], 0
        for g_ref, w in zip(g_refs, widths):
            gv = g_ref[...]
            acc_ref[:, col:col + w] += _dot_tn(av, gv)
            cacc_ref[:, col:col + w] += jnp.broadcast_to(jnp.sum(gv.astype(F32), axis=0, keepdims=True), (SUBLANES, w))
            col += w

        @pl.when(s == n_s - 1)
        def _():
            o_ref[...] = acc_ref[...].astype(BF16)
            cs_ref[...] = cacc_ref[...]

    return pl.pallas_call(
        body, grid=(n_s,),
        in_specs=[pl.BlockSpec((ts, K), lambda s: (s, 0))] + [pl.BlockSpec((ts, w), lambda s: (s, 0)) for w in widths],
        out_specs=[pl.BlockSpec((K, total), lambda s: (0, 0)), pl.BlockSpec((SUBLANES, total), lambda s: (0, 0))],
        out_shape=[jax.ShapeDtypeStruct((K, total), BF16), jax.ShapeDtypeStruct((SUBLANES, total), F32)],
        scratch_shapes=[pltpu.VMEM((K, total), F32), pltpu.VMEM((SUBLANES, total), F32)],
        name=name, compiler_params=_cparams(("arbitrary",), vmem_mb))(*_hbm(a, *gs))


DILS = tuple(d for _, d in GROUPS if d > 1)


def _res_spec(d, tm, width):
    return pl.BlockSpec((d, tm // d, width), lambda i: (0, i, 0))


def _lane_scratch(tm, width):
    return [pltpu.VMEM((tm, LANES), F32)] * (width // LANES)


def _to_residue(val, dst_refs, dils, tm, dtype, scr):
    for c, ref in enumerate(scr):
        ref[...] = val[:, c * LANES:(c + 1) * LANES]
    for dst_ref, d in zip(dst_refs, dils):
        for r in range(d):
            cols = [ref[pl.ds(r, tm // d, stride=d), :] for ref in scr]
            dst_ref[r] = jnp.concatenate(cols, axis=1).astype(dtype)


def _from_residue(rows_of, d, tm, scr):
    for r in range(d):
        v = rows_of(r).astype(F32)
        for c, ref in enumerate(scr):
            ref[pl.ds(r, tm // d, stride=d), :] = v[:, c * LANES:(c + 1) * LANES]
    return jnp.concatenate([ref[...] for ref in scr], axis=1)


def _ln0_fwd(x, g, b, after=(), *, tm=512):
    S, Dm = x.shape
    n_after = len(after)

    def body(x_ref, g_ref, b_ref, *rest):
        h_ref, hb_ref, *rest = rest[n_after:]
        xhat, _ = _ln_stats(x_ref[...])
        h = xhat * g_ref[...] + b_ref[...]
        h_ref[...] = h
        hb_ref[...] = h.astype(BF16)
        _to_residue(h, rest[:len(DILS)], DILS, tm, BF16, rest[len(DILS):])

    row = pl.BlockSpec((tm, Dm), lambda i: (i, 0))
    vec = pl.BlockSpec((1, Dm), lambda i: (0, 0))
    return pl.pallas_call(
        body, grid=(S // tm,), in_specs=[row, vec, vec] + [pl.BlockSpec(memory_space=pl.ANY)] * n_after,
        out_specs=[row, row] + [_res_spec(d, tm, Dm) for d in DILS],
        out_shape=[jax.ShapeDtypeStruct((S, Dm), F32), jax.ShapeDtypeStruct((S, Dm), BF16)]
        + [jax.ShapeDtypeStruct((d, S // d, Dm), BF16) for d in DILS],
        scratch_shapes=_lane_scratch(tm, Dm),
        name="ln0_fwd", compiler_params=_cparams(("parallel",), 32))(*_hbm(x, g, b), *after)


def _slab_spec(S, col0):
    return pl.BlockSpec((S, SLAB), lambda j: (0, col0 // SLAB + j))


def _zero_pads(scr, S):
    scr[0:PAD, :] = jnp.zeros((PAD, SLAB), F32)
    scr[S + PAD:S + 2 * PAD, :] = jnp.zeros((PAD, SLAB), F32)


def _shifted(scr, t):
    return (scr[PAD - 1 + t:PAD - 1 + t + CHUNK, :], scr[PAD + t:PAD + t + CHUNK, :],
            scr[PAD + 1 + t:PAD + 1 + t + CHUNK, :])


def _conv_gate_fwd(proj, conv_w):
    S = proj.shape[0]

    def body(b_ref, c_ref, h_ref, w_ref, o_ref, u_scr):
        _zero_pads(u_scr, S)
        for t in range(0, S, CHUNK):
            u_scr[PAD + t:PAD + t + CHUNK, :] = c_ref[t:t + CHUNK, :].astype(F32) * h_ref[t:t + CHUNK, :].astype(F32)
        w0, w1, w2 = w_ref[0:1, :], w_ref[1:2, :], w_ref[2:3, :]
        for t in range(0, S, CHUNK):
            um, u0, up = _shifted(u_scr, t)
            cv = w0 * um + w1 * u0 + w2 * up
            o_ref[t:t + CHUNK, :] = (b_ref[t:t + CHUNK, :].astype(F32) * cv).astype(BF16)

    return pl.pallas_call(
        body, grid=(D_CONV // SLAB,),
        in_specs=[_slab_spec(S, P_B), _slab_spec(S, P_C), _slab_spec(S, P_H),
                  pl.BlockSpec((3, SLAB), lambda j: (0, j))],
        out_specs=pl.BlockSpec((S, SLAB), lambda j: (0, j)),
        out_shape=jax.ShapeDtypeStruct((S, D_CONV), BF16),
        scratch_shapes=[pltpu.VMEM((S + 2 * PAD, SLAB), F32)],
        name="conv_gate_fwd", compiler_params=_cparams(("parallel",), 40))(*_hbm(proj, proj, proj, conv_w))


MASKED_DISTANCE = -1e34


def _attn_bias_table(g):
    dil = GROUPS[g][1]
    j = lax.broadcasted_iota(jnp.int32, (2 * TQ, TQ), 0)
    a = lax.broadcasted_iota(jnp.int32, (2 * TQ, TQ), 1)
    rel = jnp.abs(j - RADIUS - a)
    base = -(rel * dil).astype(F32)
    inside, after_start, before_end = rel <= RADIUS, j >= RADIUS, j < TQ + RADIUS
    variants = []
    for first, last in ((False, False), (True, False), (False, True), (True, True)):
        valid = inside & (after_start if first else True) & (before_end if last else True)
        variants.append(jnp.where(valid, base, MASKED_DISTANCE))
    return jnp.stack(variants)


SUBS = 4
TB = SUBS * TQ


def _ext_window(p_ref, c_ref, n_ref):
    return jnp.concatenate([p_ref[TB - RADIUS:, :], c_ref[...], n_ref[:RADIUS, :]], axis=0)


def _head_stats(rows):
    pad = jnp.zeros((LANES - len(rows), TQ), F32)
    return jnp.concatenate(list(rows) + [pad], axis=0).T


def _slope(g, h):
    return 2.0 ** (-8.0 * (g * HEADS_PER_GROUP + h + 1) / (N_GROUPS * HEADS_PER_GROUP))


def _pair(a, h):
    return a[:, (h // 2) * LANES:(h // 2 + 1) * LANES]


def _own_lanes(a, h):
    lane = lax.broadcasted_iota(jnp.int32, a.shape, 1)
    return jnp.where((lane >= HEAD_DIM) == (h % 2 == 1), a, jnp.zeros_like(a))


def _own_rows(a, h):
    return a[(h % 2) * HEAD_DIM:(h % 2 + 1) * HEAD_DIM, :]


def _attn_fwd(qkv, col0, g):
    dil, sub, _ = qkv.shape
    nb = sub // TB
    heads = HEADS_PER_GROUP

    def body(q_ref, kp, kc, kn, vp, vc, vn, bias_ref, o_ref, lse_ref, ot_scr, s_scr, p_scr):
        i = pl.program_id(1)
        kext = _ext_window(kp, kc, kn)
        vext = _ext_window(vp, vc, vn)
        q = q_ref[...] * ATT_SCALE
        for b in range(SUBS):
            kwin, qb = kext[b * TQ:(b + 2) * TQ, :], q[b * TQ:(b + 1) * TQ, :]
            for h in range(heads):
                s_scr[b * heads + h] = _dot_nt(_pair(kwin, h), _own_lanes(_pair(qb, h), h))
        inv_den = []
        for b in range(SUBS):
            block = i * SUBS + b
            bias = bias_ref[jnp.where(block == 0, 1, 0) + jnp.where(block == nb * SUBS - 1, 2, 0)]
            lse = []
            for h in range(heads):
                s = s_scr[b * heads + h] + _slope(g, h) * bias
                m = jnp.max(s, axis=0, keepdims=True)
                p = jnp.exp(s - m)
                den = jnp.sum(p, axis=0, keepdims=True)
                p_scr[b * heads + h] = p.astype(BF16)
                inv_den.append(1.0 / den)
                lse.append(m + jnp.log(den))
            lse_ref[b * TQ:(b + 1) * TQ, :] = _head_stats(lse)
        for b in range(SUBS):
            vwin = vext[b * TQ:(b + 2) * TQ, :]
            for h in range(heads):
                ot = _dot_tn(_pair(vwin, h), p_scr[b * heads + h])
                ot_scr[h * HEAD_DIM:(h + 1) * HEAD_DIM, b * TQ:(b + 1) * TQ] = _own_rows(ot, h) * inv_den[b * heads + h]
        o_ref[...] = ot_scr[...].T

    def spec(col, shift):
        return pl.BlockSpec((None, TB, GROUP_W), lambda r, i: (r, jnp.clip(i + shift, 0, nb - 1), col))

    return pl.pallas_call(
        body, grid=(dil, nb),
        in_specs=[spec(col0, 0), spec(col0 + 1, -1), spec(col0 + 1, 0), spec(col0 + 1, 1),
                  spec(col0 + 2, -1), spec(col0 + 2, 0), spec(col0 + 2, 1),
                  pl.BlockSpec((4, 2 * TQ, TQ), lambda r, i: (0, 0, 0))],
        out_specs=[pl.BlockSpec((None, TB, GROUP_W), lambda r, i: (r, i, 0)),
                   pl.BlockSpec((None, TB, LANES), lambda r, i: (r, i, 0))],
        out_shape=[jax.ShapeDtypeStruct((dil, sub, GROUP_W), F32), jax.ShapeDtypeStruct((dil, sub, LANES), F32)],
        scratch_shapes=[pltpu.VMEM((GROUP_W, TB), F32), pltpu.VMEM((SUBS * heads, 2 * TQ, TQ), F32),
                        pltpu.VMEM((SUBS * heads, 2 * TQ, TQ), BF16)],
        name=f"attn_fwd_g{g}", compiler_params=_cparams(("parallel", "arbitrary"), 32))(
            *_hbm(*([qkv] * 7), _attn_bias_table(g)))


def _expand_heads():
    h = lax.broadcasted_iota(jnp.int32, (LANES, GROUP_W), 0)
    c = lax.broadcasted_iota(jnp.int32, (LANES, GROUP_W), 1)
    return (c // HEAD_DIM == h).astype(F32)


def _dot_f32(a, b):
    return jnp.dot(a, b, preferred_element_type=F32, precision=lax.Precision.HIGHEST)


def _attn_combine(outs, lses, *, tm=512):
    S = outs[0].shape[1]
    n_col = GROUP_W // LANES

    def body(*refs):
        ins, e_ref = refs[:2 * N_GROUPS], refs[2 * N_GROUPS]
        c_ref, cb_ref, lt_ref = refs[2 * N_GROUPS + 1:2 * N_GROUPS + 4]
        scr = refs[2 * N_GROUPS + 4:]
        o, l = [ins[0][0]], [ins[N_GROUPS][0]]
        for k, d in enumerate(DILS):
            o_ref, l_ref = ins[1 + k], ins[N_GROUPS + 1 + k]
            o.append(_from_residue(lambda r: o_ref[r], d, tm, scr[k * (n_col + 1):k * (n_col + 1) + n_col]))
            l.append(_from_residue(lambda r: l_ref[r], d, tm, scr[k * (n_col + 1) + n_col:(k + 1) * (n_col + 1)]))
        m = jnp.maximum(jnp.maximum(l[0], l[1]), l[2])
        e = [jnp.exp(v - m) for v in l]
        den = e[0] + e[1] + e[2]
        comb = sum(_dot_f32(ev / den, e_ref[...]) * ov for ev, ov in zip(e, o))
        c_ref[...] = comb
        cb_ref[...] = comb.astype(BF16)
        lt_ref[...] = m + jnp.log(den)

    row = pl.BlockSpec((tm, GROUP_W), lambda i: (i, 0))
    dils = [d for _, d in GROUPS]
    return pl.pallas_call(
        body, grid=(S // tm,),
        in_specs=[_res_spec(d, tm, GROUP_W) for d in dils] + [_res_spec(d, tm, LANES) for d in dils]
        + [_resident((LANES, GROUP_W))],
        out_specs=[row, row, pl.BlockSpec((tm, LANES), lambda i: (i, 0))],
        out_shape=[jax.ShapeDtypeStruct((S, GROUP_W), F32), jax.ShapeDtypeStruct((S, GROUP_W), BF16),
                   jax.ShapeDtypeStruct((S, LANES), F32)],
        scratch_shapes=_lane_scratch(tm, GROUP_W + LANES) * len(DILS),
        name="attn_combine", compiler_params=_cparams(("parallel",), 32))(*_hbm(*outs, *lses, _expand_heads()))


def _branch_mix(ya_in, comb_b, w_a, w_b, proj, *, tm=512):
    S = ya_in.shape[0]

    def body(ya_ref, cb_ref, wa_ref, wb_ref, ga_ref, gb_ref, yab_ref, mx_ref):
        y_a = _dot(ya_ref[...], wa_ref[...])
        y_b = _dot(cb_ref[...], wb_ref[...])
        yab_ref[:, 0:D_MODEL] = y_a.astype(BF16)
        yab_ref[:, D_MODEL:2 * D_MODEL] = y_b.astype(BF16)
        mx = jax.nn.sigmoid(ga_ref[...].astype(F32)) * y_a + jax.nn.sigmoid(gb_ref[...].astype(F32)) * y_b
        mx_ref[...] = mx.astype(BF16)

    return pl.pallas_call(
        body, grid=(S // tm,),
        in_specs=[pl.BlockSpec((tm, D_CONV), lambda i: (i, 0)), pl.BlockSpec((tm, GROUP_W), lambda i: (i, 0)),
                  pl.BlockSpec((D_CONV, D_MODEL), lambda i: (0, 0)), pl.BlockSpec((GROUP_W, D_MODEL), lambda i: (0, 0)),
                  pl.BlockSpec((tm, D_MODEL), lambda i: (i, P_GA // D_MODEL)),
                  pl.BlockSpec((tm, D_MODEL), lambda i: (i, P_GB // D_MODEL))],
        out_specs=[pl.BlockSpec((tm, 2 * D_MODEL), lambda i: (i, 0)), pl.BlockSpec((tm, D_MODEL), lambda i: (i, 0))],
        out_shape=[jax.ShapeDtypeStruct((S, 2 * D_MODEL), BF16), jax.ShapeDtypeStruct((S, D_MODEL), BF16)],
        name="branch_mix", compiler_params=_cparams(("parallel",), 40))(*_hbm(ya_in, comb_b, w_a, w_b, proj, proj))


def _mix_ln1(mixin, w_o, b_o, h0, g1, b1, *, tm=512):
    S = mixin.shape[0]

    def body(mx_ref, wo_ref, bo_ref, h0_ref, g_ref, b_ref, xh_ref, rs_ref, h1b_ref):
        z = ALPHA * h0_ref[...] + _dot(mx_ref[...], wo_ref[...]) + bo_ref[...]
        xhat, rstd = _ln_stats(z)
        xh_ref[...] = xhat
        rs_ref[...] = jnp.broadcast_to(rstd, (tm, LANES))
        h1b_ref[...] = (xhat * g_ref[...] + b_ref[...]).astype(BF16)

    row = pl.BlockSpec((tm, D_MODEL), lambda i: (i, 0))
    vec = pl.BlockSpec((1, D_MODEL), lambda i: (0, 0))
    return pl.pallas_call(
        body, grid=(S // tm,),
        in_specs=[row, pl.BlockSpec((D_MODEL, D_MODEL), lambda i: (0, 0)), vec, row, vec, vec],
        out_specs=[row, pl.BlockSpec((tm, LANES), lambda i: (i, 0)), row],
        out_shape=[jax.ShapeDtypeStruct((S, D_MODEL), F32), jax.ShapeDtypeStruct((S, LANES), F32),
                   jax.ShapeDtypeStruct((S, D_MODEL), BF16)],
        name="mix_ln1", compiler_params=_cparams(("parallel",), 40))(*_hbm(mixin, w_o, b_o, h0, g1, b1))


def _gelu_parts(cz):
    cdf = 0.5 * (1.0 + lax.erf(cz * INV_SQRT2))
    return cdf, cz * cdf


def _ffn_conv_fwd(up, cw, cb):
    S = up.shape[0]

    def body(a_ref, g_ref, w_ref, cb_ref, o_ref, a_scr):
        _zero_pads(a_scr, S)
        for t in range(0, S, CHUNK):
            a_scr[PAD + t:PAD + t + CHUNK, :] = a_ref[t:t + CHUNK, :].astype(F32)
        w0, w1, w2 = w_ref[0:1, :], w_ref[1:2, :], w_ref[2:3, :]
        for t in range(0, S, CHUNK):
            am, a0, ap = _shifted(a_scr, t)
            _, gel = _gelu_parts(w0 * am + w1 * a0 + w2 * ap + cb_ref[...])
            o_ref[t:t + CHUNK, :] = (gel * g_ref[t:t + CHUNK, :].astype(F32)).astype(BF16)

    return pl.pallas_call(
        body, grid=(D_FF // SLAB,),
        in_specs=[_slab_spec(S, 0), _slab_spec(S, D_FF), pl.BlockSpec((3, SLAB), lambda j: (0, j)),
                  pl.BlockSpec((1, SLAB), lambda j: (0, j))],
        out_specs=pl.BlockSpec((S, SLAB), lambda j: (0, j)),
        out_shape=jax.ShapeDtypeStruct((S, D_FF), BF16),
        scratch_shapes=[pltpu.VMEM((S + 2 * PAD, SLAB), F32)],
        name="ffn_conv_fwd", compiler_params=_cparams(("parallel",), 40))(*_hbm(up, up, cw, cb))


def _down_ln2_loss(f, w_down, b_down, xhat1, g1, b1, g2, b2, target, *, tm=512):
    S = f.shape[0]

    def body(f_ref, wd_ref, bd_ref, xh1_ref, g1_ref, b1_ref, g2_ref, b2_ref, t_ref, dz_ref, dzb_ref, st_ref):
        h1 = xh1_ref[...] * g1_ref[...] + b1_ref[...]
        z = ALPHA * h1 + _dot(f_ref[...], wd_ref[...]) + bd_ref[...]
        xhat, rstd = _ln_stats(z)
        err = xhat * g2_ref[...] + b2_ref[...] - t_ref[...]
        loss = (0.5 / D_MODEL) * jnp.sum(jnp.sum(err * err, axis=1, keepdims=True), axis=0, keepdims=True)
        dh2 = err * (1.0 / D_MODEL)
        dz = _ln_bwd(dh2, xhat, rstd, g2_ref[...])
        dz_ref[...] = dz
        dzb_ref[...] = dz.astype(BF16)
        upd = _rows8([jnp.sum(dh2 * xhat, axis=0, keepdims=True), jnp.sum(dh2, axis=0, keepdims=True),
                      jnp.broadcast_to(loss, (1, D_MODEL)), jnp.sum(dz, axis=0, keepdims=True)], D_MODEL)

        @pl.when(pl.program_id(0) == 0)
        def _():
            st_ref[...] = upd

        @pl.when(pl.program_id(0) != 0)
        def _():
            st_ref[...] += upd

    row = pl.BlockSpec((tm, D_MODEL), lambda i: (i, 0))
    vec = pl.BlockSpec((1, D_MODEL), lambda i: (0, 0))
    return pl.pallas_call(
        body, grid=(S // tm,),
        in_specs=[pl.BlockSpec((tm, D_FF), lambda i: (i, 0)), _resident((D_FF, D_MODEL)),
                  vec, row, vec, vec, vec, vec, row],
        out_specs=[row, row, pl.BlockSpec((SUBLANES, D_MODEL), lambda i: (0, 0))],
        out_shape=[jax.ShapeDtypeStruct((S, D_MODEL), F32), jax.ShapeDtypeStruct((S, D_MODEL), BF16),
                   jax.ShapeDtypeStruct((SUBLANES, D_MODEL), F32)],
        name="down_ln2_loss", compiler_params=_cparams(("arbitrary",), 56))(
            *_hbm(f, w_down, b_down, xhat1, g1, b1, g2, b2, target))


def _ffn_conv_bwd(up, df, cw, cb):
    S = up.shape[0]

    def body(a_ref, g_ref, df_ref, w_ref, cb_ref, dup_ref, sm_ref, a_scr, d_scr):
        _zero_pads(a_scr, S)
        _zero_pads(d_scr, S)
        for t in range(0, S, CHUNK):
            a_scr[PAD + t:PAD + t + CHUNK, :] = a_ref[t:t + CHUNK, :].astype(F32)
        w0, w1, w2 = w_ref[0:1, :], w_ref[1:2, :], w_ref[2:3, :]
        zero = jnp.zeros((1, SLAB), F32)
        s_dg, s_dcz, s_w0, s_w1, s_w2 = zero, zero, zero, zero, zero
        for t in range(0, S, CHUNK):
            am, a0, ap = _shifted(a_scr, t)
            cz = w0 * am + w1 * a0 + w2 * ap + cb_ref[...]
            cdf, gel = _gelu_parts(cz)
            dfv = df_ref[t:t + CHUNK, :].astype(F32)
            dgte = dfv * gel
            dcz = dfv * g_ref[t:t + CHUNK, :].astype(F32) * (cdf + cz * jnp.exp(-0.5 * cz * cz) * INV_SQRT_2PI)
            dup_ref[1, t:t + CHUNK, :] = dgte.astype(BF16)
            d_scr[PAD + t:PAD + t + CHUNK, :] = dcz
            s_dg = s_dg + jnp.sum(dgte, axis=0, keepdims=True)
            s_dcz = s_dcz + jnp.sum(dcz, axis=0, keepdims=True)
            s_w0 = s_w0 + jnp.sum(dcz * am, axis=0, keepdims=True)
            s_w1 = s_w1 + jnp.sum(dcz * a0, axis=0, keepdims=True)
            s_w2 = s_w2 + jnp.sum(dcz * ap, axis=0, keepdims=True)
        s_da = zero
        for t in range(0, S, CHUNK):
            dm, d0, dp = _shifted(d_scr, t)
            da = w0 * dp + w1 * d0 + w2 * dm
            dup_ref[0, t:t + CHUNK, :] = da.astype(BF16)
            s_da = s_da + jnp.sum(da, axis=0, keepdims=True)
        sm_ref[...] = _rows8([s_da, s_dg, s_dcz, s_w0, s_w1, s_w2], SLAB)

    return pl.pallas_call(
        body, grid=(D_FF // SLAB,),
        in_specs=[_slab_spec(S, 0), _slab_spec(S, D_FF), pl.BlockSpec((S, SLAB), lambda j: (0, j)),
                  pl.BlockSpec((3, SLAB), lambda j: (0, j)), pl.BlockSpec((1, SLAB), lambda j: (0, j))],
        out_specs=[pl.BlockSpec((2, S, SLAB), lambda j: (0, 0, j)), pl.BlockSpec((SUBLANES, SLAB), lambda j: (0, j))],
        out_shape=[jax.ShapeDtypeStruct((2, S, D_FF), BF16), jax.ShapeDtypeStruct((SUBLANES, D_FF), F32)],
        scratch_shapes=[pltpu.VMEM((S + 2 * PAD, SLAB), F32)] * 2,
        name="ffn_conv_bwd", compiler_params=_cparams(("parallel",), 48))(*_hbm(up, up, df, cw, cb))


def _up_bwd_ln1(dup, w_up3, dz2, xhat1, rstd1, g1, *, tm=512):
    S = dz2.shape[0]
    ns, _, tk = w_up3.shape
    per_plane = D_FF // tk

    def body(du_ref, w_ref, dz2_ref, xh_ref, rs_ref, g_ref, dz_ref, dzb_ref, st_ref):
        dh = ALPHA * dz2_ref[...]
        for k in range(ns):
            col = (k % per_plane) * tk
            dh = dh + _dot_nt(du_ref[k // per_plane, :, col:col + tk], w_ref[k])
        xhat = xh_ref[...]
        dz = _ln_bwd(dh, xhat, rs_ref[:, 0:1], g_ref[...])
        dz_ref[...] = dz
        dzb_ref[...] = dz.astype(BF16)
        upd = _rows8([jnp.sum(dh * xhat, axis=0, keepdims=True), jnp.sum(dh, axis=0, keepdims=True),
                      jnp.sum(dz, axis=0, keepdims=True)], D_MODEL)

        @pl.when(pl.program_id(0) == 0)
        def _():
            st_ref[...] = upd

        @pl.when(pl.program_id(0) != 0)
        def _():
            st_ref[...] += upd

    row = pl.BlockSpec((tm, D_MODEL), lambda i: (i, 0))
    return pl.pallas_call(
        body, grid=(S // tm,),
        in_specs=[pl.BlockSpec((dup.shape[0], tm, D_FF), lambda i: (0, i, 0)), _resident(w_up3.shape),
                  row, row, pl.BlockSpec((tm, LANES), lambda i: (i, 0)), pl.BlockSpec((1, D_MODEL), lambda i: (0, 0))],
        out_specs=[row, row, pl.BlockSpec((SUBLANES, D_MODEL), lambda i: (0, 0))],
        out_shape=[jax.ShapeDtypeStruct((S, D_MODEL), F32), jax.ShapeDtypeStruct((S, D_MODEL), BF16),
                   jax.ShapeDtypeStruct((SUBLANES, D_MODEL), F32)],
        name="up_bwd_ln1", compiler_params=_cparams(("arbitrary",), 56))(*_hbm(dup, w_up3, dz2, xhat1, rstd1, g1))


def _mix_bwd(dz1b, w_o, proj, yab, *, tm=512):
    S = dz1b.shape[0]

    def body(dz_ref, wo_ref, ga_ref, gb_ref, y_ref, dy_ref, dg_ref):
        dmx = _dot_nt(dz_ref[...], wo_ref[...])
        for k, gt_ref in enumerate((ga_ref, gb_ref)):
            sl = slice(k * D_MODEL, (k + 1) * D_MODEL)
            sg = jax.nn.sigmoid(gt_ref[...].astype(F32))
            dy_ref[:, sl] = (dmx * sg).astype(BF16)
            dg_ref[k] = (dmx * y_ref[:, sl].astype(F32) * sg * (1.0 - sg)).astype(BF16)

    row = pl.BlockSpec((tm, D_MODEL), lambda i: (i, 0))
    wide = pl.BlockSpec((tm, 2 * D_MODEL), lambda i: (i, 0))
    return pl.pallas_call(
        body, grid=(S // tm,),
        in_specs=[row, _resident(w_o.shape), pl.BlockSpec((tm, D_MODEL), lambda i: (i, P_GA // D_MODEL)),
                  pl.BlockSpec((tm, D_MODEL), lambda i: (i, P_GB // D_MODEL)), wide],
        out_specs=[wide, pl.BlockSpec((2, tm, D_MODEL), lambda i: (0, i, 0))],
        out_shape=[jax.ShapeDtypeStruct((S, 2 * D_MODEL), BF16), jax.ShapeDtypeStruct((2, S, D_MODEL), BF16)],
        name="mix_bwd", compiler_params=_cparams(("parallel",), 40))(*_hbm(dz1b, w_o, proj, proj, yab))


def _conv_gate_bwd(proj, dya_in, conv_w):
    S = proj.shape[0]

    def body(b_ref, c_ref, h_ref, dy_ref, w_ref, o_ref, sm_ref, u_scr, d_scr):
        _zero_pads(u_scr, S)
        _zero_pads(d_scr, S)
        for t in range(0, S, CHUNK):
            u_scr[PAD + t:PAD + t + CHUNK, :] = c_ref[t:t + CHUNK, :].astype(F32) * h_ref[t:t + CHUNK, :].astype(F32)
        w0, w1, w2 = w_ref[0:1, :], w_ref[1:2, :], w_ref[2:3, :]
        zero = jnp.zeros((1, SLAB), F32)
        s_w0, s_w1, s_w2 = zero, zero, zero
        for t in range(0, S, CHUNK):
            um, u0, up = _shifted(u_scr, t)
            dy = dy_ref[t:t + CHUNK, :].astype(F32)
            o_ref[0, t:t + CHUNK, :] = (dy * (w0 * um + w1 * u0 + w2 * up)).astype(BF16)
            dcv = dy * b_ref[t:t + CHUNK, :].astype(F32)
            d_scr[PAD + t:PAD + t + CHUNK, :] = dcv
            s_w0 = s_w0 + jnp.sum(dcv * um, axis=0, keepdims=True)
            s_w1 = s_w1 + jnp.sum(dcv * u0, axis=0, keepdims=True)
            s_w2 = s_w2 + jnp.sum(dcv * up, axis=0, keepdims=True)
        for t in range(0, S, CHUNK):
            dm, d0, dp = _shifted(d_scr, t)
            du = w0 * dp + w1 * d0 + w2 * dm
            o_ref[1, t:t + CHUNK, :] = (du * h_ref[t:t + CHUNK, :].astype(F32)).astype(BF16)
            o_ref[2, t:t + CHUNK, :] = (du * c_ref[t:t + CHUNK, :].astype(F32)).astype(BF16)
        sm_ref[...] = _rows8([s_w0, s_w1, s_w2], SLAB)

    return pl.pallas_call(
        body, grid=(D_CONV // SLAB,),
        in_specs=[_slab_spec(S, P_B), _slab_spec(S, P_C), _slab_spec(S, P_H),
                  pl.BlockSpec((S, SLAB), lambda j: (0, j)), pl.BlockSpec((3, SLAB), lambda j: (0, j))],
        out_specs=[pl.BlockSpec((3, S, SLAB), lambda j: (0, 0, j)), pl.BlockSpec((SUBLANES, SLAB), lambda j: (0, j))],
        out_shape=[jax.ShapeDtypeStruct((3, S, D_CONV), BF16), jax.ShapeDtypeStruct((SUBLANES, D_CONV), F32)],
        scratch_shapes=[pltpu.VMEM((S + 2 * PAD, SLAB), F32)] * 2,
        name="conv_gate_bwd", compiler_params=_cparams(("parallel",), 48))(*_hbm(proj, proj, proj, dya_in, conv_w))


def _comb_bwd(dyab, w_b, comb, lse_tot, *, tm=512):
    S = comb.shape[0]
    widths, dtypes = (GROUP_W, LANES, LANES), (BF16, F32, F32)

    def body(dy_ref, wb_ref, c_ref, lt_ref, e_ref, *rest):
        outs, scr = rest[:3 * N_GROUPS], rest[3 * N_GROUPS:]
        dcb = _dot_nt(dy_ref[...], wb_ref[...]).astype(BF16)
        dc = dcb.astype(F32)
        delta = lax.dot_general(dc * c_ref[...], e_ref[...], (((1,), (1,)), ((), ())),
                                preferred_element_type=F32, precision=lax.Precision.HIGHEST)
        for k, (val, dtype) in enumerate(zip((dc, lt_ref[...], delta), dtypes)):
            outs[k][0] = val.astype(dtype)
            _to_residue(val, [outs[3 * (1 + j) + k] for j in range(len(DILS))], DILS, tm, dtype,
                        scr[:val.shape[1] // LANES])

    out_specs, out_shape = [], []
    for _, d in GROUPS:
        out_specs += [_res_spec(d, tm, w) for w in widths]
        out_shape += [jax.ShapeDtypeStruct((d, S // d, w), t) for w, t in zip(widths, dtypes)]
    res = pl.pallas_call(
        body, grid=(S // tm,),
        in_specs=[pl.BlockSpec((tm, D_MODEL), lambda i: (i, 1)), _resident(w_b.shape),
                  pl.BlockSpec((tm, GROUP_W), lambda i: (i, 0)), pl.BlockSpec((tm, LANES), lambda i: (i, 0)),
                  _resident((LANES, GROUP_W))],
        out_specs=out_specs, out_shape=out_shape, scratch_shapes=_lane_scratch(tm, GROUP_W),
        name="comb_bwd", compiler_params=_cparams(("parallel",), 32))(*_hbm(dyab, w_b, comb, lse_tot, _expand_heads()))
    return [tuple(res[3 * g:3 * g + 3]) for g in range(N_GROUPS)]


def _attn_bwd(qkv, col0, g, dcomb, lse_tot, delta):
    dil, sub, _ = qkv.shape
    nb = sub // TB
    heads = HEADS_PER_GROUP

    def body(q_ref, kp, kc, kn, vp, vc, vn, do_ref, lse_ref, dl_ref, bias_ref, dq_ref, dk_ref, dv_ref,
             ak, av, dqt_scr, s_scr, dp_scr, ds_scr, p_scr):
        i = pl.program_id(1)

        @pl.when(i == 0)
        def _():
            ak[...] = jnp.zeros_like(ak)
            av[...] = jnp.zeros_like(av)

        @pl.when(i < nb)
        def _():
            kext = _ext_window(kp, kc, kn)
            vext = _ext_window(vp, vc, vn)
            q = q_ref[...] * ATT_SCALE
            do = do_ref[...]
            lse_t, dl_t = lse_ref[...].T, dl_ref[...].T
            for b in range(SUBS):
                rows = slice(b * TQ, (b + 1) * TQ)
                kwin, vwin = kext[b * TQ:(b + 2) * TQ, :], vext[b * TQ:(b + 2) * TQ, :]
                for h in range(heads):
                    s_scr[b * heads + h] = _dot_nt(_pair(kwin, h), _own_lanes(_pair(q[rows], h), h))
                    dp_scr[b * heads + h] = _dot_nt(_pair(vwin, h), _own_lanes(_pair(do[rows], h), h))
            for b in range(SUBS):
                cols = slice(b * TQ, (b + 1) * TQ)
                block = i * SUBS + b
                bias = bias_ref[jnp.where(block == 0, 1, 0) + jnp.where(block == nb * SUBS - 1, 2, 0)]
                for h in range(heads):
                    k = b * heads + h
                    p = jnp.exp(s_scr[k] + _slope(g, h) * bias - lse_t[h:h + 1, cols])
                    ds_scr[k] = (p * (dp_scr[k] - dl_t[h:h + 1, cols])).astype(BF16)
                    p_scr[k] = p.astype(BF16)
            for b in range(SUBS):
                kwin = kext[b * TQ:(b + 2) * TQ, :]
                for h in range(heads):
                    dqt_scr[h * HEAD_DIM:(h + 1) * HEAD_DIM, b * TQ:(b + 1) * TQ] = _own_rows(
                        _dot_tn(_pair(kwin, h), ds_scr[b * heads + h]), h)
            for b in range(SUBS):
                rows = slice(b * TQ, (b + 1) * TQ)
                acc_rows = slice(TB - RADIUS + b * TQ, TB - RADIUS + (b + 2) * TQ)
                for h in range(0, heads, 2):
                    cols = slice(h * HEAD_DIM, (h + 2) * HEAD_DIM)
                    k = b * heads + h
                    q2 = jnp.concatenate([_own_lanes(_pair(q[rows], h), h), _own_lanes(_pair(q[rows], h), h + 1)], axis=0)
                    do2 = jnp.concatenate([_own_lanes(_pair(do[rows], h), h), _own_lanes(_pair(do[rows], h), h + 1)],
                                          axis=0)
                    ak[acc_rows, cols] += _dot(jnp.concatenate([ds_scr[k], ds_scr[k + 1]], axis=1), q2)
                    av[acc_rows, cols] += _dot(jnp.concatenate([p_scr[k], p_scr[k + 1]], axis=1), do2)
            dq_ref[...] = (dqt_scr[...].T * ATT_SCALE).astype(BF16)

        dk_ref[...] = ak[0:TB, :].astype(BF16)
        dv_ref[...] = av[0:TB, :].astype(BF16)
        ak[0:2 * TB, :] = ak[TB:3 * TB, :]
        av[0:2 * TB, :] = av[TB:3 * TB, :]
        ak[2 * TB:3 * TB, :] = jnp.zeros((TB, GROUP_W), F32)
        av[2 * TB:3 * TB, :] = jnp.zeros((TB, GROUP_W), F32)

    def spec(col, shift):
        return pl.BlockSpec((None, TB, GROUP_W), lambda r, i: (r, jnp.clip(i + shift, 0, nb - 1), col))

    tok = pl.BlockSpec((None, TB, GROUP_W), lambda r, i: (r, jnp.minimum(i, nb - 1), 0))
    stat = pl.BlockSpec((None, TB, LANES), lambda r, i: (r, jnp.minimum(i, nb - 1), 0))
    dkv_spec = pl.BlockSpec((None, TB, GROUP_W), lambda r, i: (r, jnp.maximum(i - 1, 0), 0))
    return pl.pallas_call(
        body, grid=(dil, nb + 1),
        in_specs=[spec(col0, 0), spec(col0 + 1, -1), spec(col0 + 1, 0), spec(col0 + 1, 1),
                  spec(col0 + 2, -1), spec(col0 + 2, 0), spec(col0 + 2, 1), tok, stat, stat,
                  pl.BlockSpec((4, 2 * TQ, TQ), lambda r, i: (0, 0, 0))],
        out_specs=[tok, dkv_spec, dkv_spec], out_shape=[jax.ShapeDtypeStruct((dil, sub, GROUP_W), BF16)] * 3,
        scratch_shapes=[pltpu.VMEM((3 * TB, GROUP_W), F32)] * 2 + [pltpu.VMEM((GROUP_W, TB), F32)]
        + [pltpu.VMEM((SUBS * heads, 2 * TQ, TQ), F32)] * 2 + [pltpu.VMEM((SUBS * heads, 2 * TQ, TQ), BF16)] * 2,
        name=f"attn_bwd_g{g}", compiler_params=_cparams(("arbitrary", "arbitrary"), 40))(
            *_hbm(*([qkv] * 7), dcomb, lse_tot, delta, _attn_bias_table(g)))


def _in_bwd_ln0(dgated, dqkv, w_nat, w_dil, dz1, x, g0, *, tm=256):
    S = x.shape[0]
    n_gated, n_in = len(dgated), 3 * N_GROUPS

    def body(*refs):
        g_refs, d_refs = refs[:n_gated], refs[n_gated:n_gated + n_in]
        wn_ref, *wd_refs = refs[n_gated + n_in:n_gated + n_in + N_GROUPS]
        dz_ref, x_ref, g_ref, gx_ref, st_ref, *tmp_ref = refs[n_gated + n_in + N_GROUPS:]
        dh = ALPHA * dz_ref[...]
        col = 0
        for ref in g_refs:
            for k in range(ref.shape[0]):
                dh = dh + _dot_nt(ref[k], wn_ref[:, col:col + D_MODEL])
                col += D_MODEL
        for g, (_, d) in enumerate(GROUPS):
            rows = [jnp.concatenate([d_refs[3 * g + k][r] for k in range(3)], axis=1) for r in range(d)]
            w = wn_ref[:, col:col + QKV_W] if d == 1 else wd_refs[g - 1][...]
            res = _dot_nt(jnp.concatenate(rows, axis=0), w)
            if d == 1:
                dh = dh + res
            else:
                n = tm // d
                dh = dh + _from_residue(lambda r: res[r * n:(r + 1) * n, :], d, tm, tmp_ref)
        xhat, rstd = _ln_stats(x_ref[...])
        gx_ref[...] = _ln_bwd(dh, xhat, rstd, g_ref[...])
        upd = _rows8([jnp.sum(dh * xhat, axis=0, keepdims=True), jnp.sum(dh, axis=0, keepdims=True)], D_MODEL)

        @pl.when(pl.program_id(0) == 0)
        def _():
            st_ref[...] = upd

        @pl.when(pl.program_id(0) != 0)
        def _():
            st_ref[...] += upd

    row = pl.BlockSpec((tm, D_MODEL), lambda i: (i, 0))
    g_specs = [pl.BlockSpec((a.shape[0], tm, D_MODEL), lambda i: (0, i, 0)) for a in dgated]
    d_specs = []
    for _, d in GROUPS:
        d_specs += [_res_spec(d, tm, GROUP_W)] * 3
    operands = list(dgated) + [a for grp in dqkv for a in grp] + [w_nat] + list(w_dil) + [dz1, x, g0]
    return pl.pallas_call(
        body, grid=(S // tm,),
        in_specs=g_specs + d_specs + [_resident(w_nat.shape)] + [_resident(w.shape) for w in w_dil]
        + [row, row, pl.BlockSpec((1, D_MODEL), lambda i: (0, 0))],
        out_specs=[row, pl.BlockSpec((SUBLANES, D_MODEL), lambda i: (0, 0))],
        out_shape=[jax.ShapeDtypeStruct((S, D_MODEL), F32), jax.ShapeDtypeStruct((SUBLANES, D_MODEL), F32)],
        scratch_shapes=_lane_scratch(tm, D_MODEL),
        name="in_bwd_ln0", compiler_params=_cparams(("arbitrary",), 52))(*_hbm(*operands))


HBM_SPEC = pl.BlockSpec(memory_space=pltpu.HBM)


def _place():
    x, y, c = lax.axis_index("x"), lax.axis_index("y"), lax.axis_index("c")
    chips = [(1 - x, y), (x, 1 - y), (1 - x, 1 - y)]
    return x, y, c, chips


def _allgather_shards(shards, after, *, name, collective_id):
    n = len(shards)
    per = 6

    def body(*refs):
        ins, outs = refs[:n], refs[n + len(after):2 * n + len(after)]
        send_sems, recv_sems, loc_sems = refs[2 * n + len(after):]
        x, y, c, chips = _place()
        me = 2 * x + y
        sib = (x, y, 1 - c)
        peers = [sib] + [(px, py, c) for px, py in chips]
        barrier = pltpu.get_barrier_semaphore()
        for peer in peers:
            pl.semaphore_signal(barrier, inc=1, device_id=peer, device_id_type=MESH)
        pl.semaphore_wait(barrier, len(peers))

        def rcopy(w, k, src, dst, to):
            return pltpu.make_async_remote_copy(src_ref=src, dst_ref=dst, send_sem=send_sems.at[per * w + k],
                                                recv_sem=recv_sems.at[per * w + k], device_id=to, device_id_type=MESH)

        split = [s.shape[0] == N_CORES for s in shards]
        half = lambda w: c if split[w] else 0
        local, sends = [], []
        for w in range(n):
            cp = pltpu.make_async_copy(ins[w], outs[w].at[me], loc_sems.at[w])
            cp.start()
            local.append(cp)
            for j, (px, py) in enumerate(chips):
                cp = rcopy(w, j, ins[w].at[half(w)], outs[w].at[me, half(w)], (px, py, c))
                cp.start()
                sends.append(cp)
        for w in range(n):
            for j, (px, py) in enumerate(chips):
                slot = outs[w].at[2 * px + py, half(w)]
                rcopy(w, j, slot, slot, (px, py, c)).wait_recv()
                if split[w]:
                    cp = rcopy(w, 3 + j, slot, slot, sib)
                    cp.start()
                    sends.append(cp)
        for w in range(n):
            if split[w]:
                for j, (px, py) in enumerate(chips):
                    slot = outs[w].at[2 * px + py, 1 - c]
                    rcopy(w, 3 + j, slot, slot, sib).wait_recv()
        for cp in sends:
            cp.wait_send()
        for cp in local:
            cp.wait()

    return pl.kernel(
        body, out_type=[jax.ShapeDtypeStruct((N_CHIPS,) + s.shape, s.dtype) for s in shards],
        mesh=plsc.ScalarSubcoreMesh(axis_name="sequencer", num_cores=1),
        scratch_types=[pltpu.SemaphoreType.DMA((per * n,)), pltpu.SemaphoreType.DMA((per * n,)),
                       pltpu.SemaphoreType.DMA((n,))],
        name=name, compiler_params=pltpu.CompilerParams(collective_id=collective_id))(*shards, *after)


def _exchange_grads(grads, *, name, collective_id):
    n = len(grads)
    per = 7

    def body(*refs):
        ins, outs = refs[:n], refs[n:2 * n]
        send_sems, recv_sems, loc_sems = refs[2 * n:]
        x, y, c, chips = _place()
        me = 2 * x + y
        sib = (x, y, 1 - c)
        peers = [sib] + [(px, py, c) for px, py in chips]
        barrier = pltpu.get_barrier_semaphore()
        for peer in peers:
            pl.semaphore_signal(barrier, inc=1, device_id=peer, device_id_type=MESH)
        pl.semaphore_wait(barrier, len(peers))

        def rcopy(w, k, src, dst, to):
            return pltpu.make_async_remote_copy(src_ref=src, dst_ref=dst, send_sem=send_sems.at[per * w + k],
                                                recv_sem=recv_sems.at[per * w + k], device_id=to, device_id_type=MESH)

        local, sends = [], []
        for w in range(n):
            cp = pltpu.make_async_copy(ins[w].at[me], outs[w].at[c, me], loc_sems.at[w])
            cp.start()
            local.append(cp)
            cp = rcopy(w, 0, ins[w].at[me], outs[w].at[c, me], sib)
            cp.start()
            sends.append(cp)
            for j, (px, py) in enumerate(chips):
                cp = rcopy(w, 1 + j, ins[w].at[2 * px + py], outs[w].at[c, me], (px, py, c))
                cp.start()
                sends.append(cp)
        for w in range(n):
            for j, (px, py) in enumerate(chips):
                slot = outs[w].at[c, 2 * px + py]
                rcopy(w, 1 + j, slot, slot, (px, py, c)).wait_recv()
                cp = rcopy(w, 4 + j, slot, slot, sib)
                cp.start()
                sends.append(cp)
        for w in range(n):
            slot = outs[w].at[1 - c, me]
            rcopy(w, 0, slot, slot, sib).wait_recv()
            for j, (px, py) in enumerate(chips):
                slot = outs[w].at[1 - c, 2 * px + py]
                rcopy(w, 4 + j, slot, slot, sib).wait_recv()
        for cp in sends:
            cp.wait_send()
        for cp in local:
            cp.wait()

    return pl.kernel(
        body, out_type=[jax.ShapeDtypeStruct((N_CORES,) + g.shape, g.dtype) for g in grads],
        mesh=plsc.ScalarSubcoreMesh(axis_name="sequencer", num_cores=1),
        scratch_types=[pltpu.SemaphoreType.DMA((per * n,)), pltpu.SemaphoreType.DMA((per * n,)),
                       pltpu.SemaphoreType.DMA((n,))],
        name=name, compiler_params=pltpu.CompilerParams(collective_id=collective_id))(*grads)


def _allgather_small(vec, after):
    def body(v_ref, _, o_ref, send_sems, recv_sems, loc_sem):
        x, y, c = lax.axis_index("x"), lax.axis_index("y"), lax.axis_index("c")
        me = 4 * x + 2 * y + c

        def peer(k):
            flip = lambda v, bit: 1 - v if (k >> bit) & 1 else v
            return flip(x, 2), flip(y, 1), flip(c, 0)

        loc = pltpu.make_async_copy(v_ref, o_ref.at[me], loc_sem)
        loc.start()
        sends = []
        for k in range(1, N_DEV):
            cp = pltpu.make_async_remote_copy(src_ref=v_ref, dst_ref=o_ref.at[me], send_sem=send_sems.at[k - 1],
                                              recv_sem=recv_sems.at[k - 1], device_id=peer(k), device_id_type=MESH)
            cp.start()
            sends.append(cp)
        for k in range(1, N_DEV):
            px, py, pc = peer(k)
            pltpu.make_async_remote_copy(src_ref=v_ref, dst_ref=o_ref.at[4 * px + 2 * py + pc],
                                         send_sem=send_sems.at[k - 1], recv_sem=recv_sems.at[k - 1],
                                         device_id=(px, py, pc), device_id_type=MESH).wait_recv()
        for cp in sends:
            cp.wait_send()
        loc.wait()

    return pl.pallas_call(
        body, in_specs=[HBM_SPEC, HBM_SPEC], out_specs=HBM_SPEC,
        out_shape=jax.ShapeDtypeStruct((N_DEV,) + vec.shape, vec.dtype),
        scratch_shapes=[pltpu.SemaphoreType.DMA((N_DEV - 1,)), pltpu.SemaphoreType.DMA((N_DEV - 1,)),
                        pltpu.SemaphoreType.DMA],
        name="allgather_small")(vec, after)


def _adamw(w, g, m, v):
    m = ADAM_B1 * m + (1.0 - ADAM_B1) * g
    v = ADAM_B2 * v + (1.0 - ADAM_B2) * (g * g)
    m_hat = m / (1.0 - ADAM_B1 ** ADAM_STEP)
    v_hat = v / (1.0 - ADAM_B2 ** ADAM_STEP)
    delta = -ADAM_LR * (m_hat / (jnp.sqrt(v_hat) + ADAM_EPS) + ADAM_WD * w)
    return delta, m, v


def _reduce_adamw(parts, w, m, v, *, tr, name):
    R, C = w.shape

    def body(p_ref, w_ref, m_ref, v_ref, g_ref, d_ref, nm_ref, nv_ref):
        def core_sum(cc):
            s = p_ref[cc, 0].astype(F32)
            for k in range(1, N_CHIPS):
                s = s + p_ref[cc, k].astype(F32)
            return s

        g = core_sum(0) + core_sum(1)
        delta, nm, nv = _adamw(w_ref[...], g, m_ref[...], v_ref[...])
        g_ref[...] = g
        d_ref[...] = delta
        nm_ref[...] = nm
        nv_ref[...] = nv

    blk = pl.BlockSpec((tr, C), lambda i: (i, 0))
    return pl.pallas_call(
        body, grid=(R // tr,),
        in_specs=[pl.BlockSpec((N_CORES, N_CHIPS, tr, C), lambda i: (0, 0, i, 0)), blk, blk, blk],
        out_specs=[blk] * 4, out_shape=[jax.ShapeDtypeStruct((R, C), F32)] * 4,
        name=name, compiler_params=_cparams(("parallel",), 40))(*_hbm(parts, w, m, v))


def _reduce_adamw_vectors(allv, offs, ws, ms, vs):
    n = len(ws)

    def body(a_ref, *refs):
        w_refs, m_refs, v_refs = refs[:n], refs[n:2 * n], refs[2 * n:3 * n]
        tot_ref, outs = refs[3 * n], refs[3 * n + 1:]
        s = a_ref[0]
        for d in range(1, N_DEV):
            s = s + a_ref[d]
        tot_ref[...] = s
        for k in range(n):
            g = s[:, offs[k]:offs[k] + w_refs[k].shape[1]]
            delta, nm, nv = _adamw(w_refs[k][...], g, m_refs[k][...], v_refs[k][...])
            for ref, val in zip(outs[4 * k:4 * k + 4], (g, delta, nm, nv)):
                ref[...] = val

    out_shape = [jax.ShapeDtypeStruct(allv.shape[1:], F32)]
    for w in ws:
        out_shape += [jax.ShapeDtypeStruct(w.shape, F32)] * 4
    res = pl.pallas_call(body, out_shape=out_shape, name="reduce_adamw_vectors",
                         compiler_params=_cparams((), 40))(allv, *ws, *ms, *vs)
    return res[0], [tuple(res[1 + 4 * k:5 + 4 * k]) for k in range(n)]


def _adamw_taps(ws, gs, ms, vs):
    n = len(ws)

    def body(*refs):
        outs = refs[4 * n:]
        for k in range(n):
            res = _adamw(refs[k][...], refs[n + k][...], refs[2 * n + k][...], refs[3 * n + k][...])
            for ref, val in zip(outs[3 * k:3 * k + 3], res):
                ref[...] = val

    out_shape = []
    for w in ws:
        out_shape += [jax.ShapeDtypeStruct(w.shape, F32)] * 3
    res = pl.pallas_call(body, out_shape=out_shape, name="adamw_taps")(*ws, *gs, *ms, *vs)
    return [tuple(res[3 * k:3 * k + 3]) for k in range(n)]


def _pack(pieces):
    flat, offs, n = [], [], 0
    for p in pieces:
        size = -(-p.size // LANES) * LANES
        flat.append(jnp.pad(p.reshape(-1), (0, size - p.size)))
        offs.append(n)
        n += size
    return jnp.concatenate(flat).reshape(1, n), offs


def _local_step(x, target, p, wfull, on_ready=lambda group: None, before_ln0=()):
    S = x.shape[0]
    dils = [d for _, d in GROUPS]

    h0, h0b, *h0_res = _ln0_fwd(x, p["ln0_g"], p["ln0_b"], before_ln0)
    h0_rows = [h0b] + [h.reshape(S, D_MODEL) for h in h0_res]

    if isinstance(wfull, dict):
        w_in3, pending = wfull["w_in"], None
    else:
        w_in3, launch_rest, assemble = wfull
        w_in3, h0b = lax.optimization_barrier((w_in3, h0b))
        pending = launch_rest(h0b)

    runs = _col_runs()
    w_perm = jnp.concatenate([w_in3[s, :, c:c + w] for s, c, _, w in runs], axis=1)
    b_blocks = p["b_in"].reshape(N_BLK, GROUP_W)
    b_perm = jnp.concatenate([b_blocks[b] for b in PERM]).reshape(1, N_IN)
    w_nat, b_nat = w_perm[:, :N_NAT], b_perm[:, :N_NAT]
    qkv_cols = [slice(P_Q0 + g * QKV_W, P_Q0 + (g + 1) * QKV_W) for g in range(N_GROUPS)]
    w_qkv = [w_perm[:, c] for c in qkv_cols]

    proj = _mm_nn(h0b, w_nat, b_nat, tm=512, tn=N_NAT // 2, out_dtype=BF16, name="proj")
    qkv = [proj[None]]
    for g in range(1, N_GROUPS):
        t = _mm_nn(h0_rows[g], w_qkv[g], b_perm[:, qkv_cols[g]], tm=512, tn=QKV_W, out_dtype=BF16, name=f"proj_qkv{g}")
        qkv.append(t.reshape(dils[g], S // dils[g], QKV_W))
    if pending is not None:
        pending, qkv = lax.optimization_barrier((pending, qkv))
        proj = qkv[0][0]
        wfull = assemble(pending)
    w_up3 = wfull["w_up"]
    w_a, w_o, w_down, w_b = wfull["w_a"], wfull["w_o"], wfull["w_down"], wfull["w_b"]
    conv_w, ffn_conv_w = wfull["conv_w"], wfull["ffn_conv_w"]
    col0 = [P_Q0 // GROUP_W] + [0] * (N_GROUPS - 1)
    ya_in = _conv_gate_fwd(proj, conv_w)
    att = [_attn_fwd(qkv[g], col0[g], g) for g in range(N_GROUPS)]
    comb, comb_b, lse_tot = _attn_combine([a[0] for a in att], [a[1] for a in att])
    yab, mixin = _branch_mix(ya_in, comb_b, w_a, w_b, proj)
    xhat1, rstd1, h1b = _mix_ln1(mixin, w_o, p["b_o"], h0, p["ln1_g"], p["ln1_b"])
    up = _mm_nn(h1b, w_up3, p["b_up"], tm=512, tn=w_up3.shape[2], out_dtype=BF16, name="up")
    f = _ffn_conv_fwd(up, ffn_conv_w, p["ffn_conv_b"])
    dz2, dz2b, st2 = _down_ln2_loss(f, w_down, p["b_down"], xhat1, p["ln1_g"], p["ln1_b"],
                                    p["ln2_g"], p["ln2_b"], target)

    gw = {}
    gw["w_down"] = _mm_tn(f, dz2b, n_out=1, tn=D_MODEL, ts=1024, g_block=(1024, D_MODEL),
                          g_map=lambda j, s: (s, 0), name="grad_w_down").reshape(N_CHIPS, D_FF // N_CHIPS, D_MODEL)
    df = _mm_nt(dz2b, w_down, tm=512, name="df")
    dup, sm_ffn = _ffn_conv_bwd(up, df, ffn_conv_w, p["ffn_conv_b"])
    up_tn = w_up3.shape[2]
    up_pp = D_FF // up_tn
    gw["w_up"] = _mm_tn(h1b, dup, n_out=N_CHIPS, tn=up_tn, ts=1024, g_block=(None, 1024, up_tn),
                        g_map=lambda j, s: (j // up_pp, s, j % up_pp), name="grad_w_up")
    exchanged = on_ready({n: gw[n] for n in ("w_down", "w_up")}) or {}
    dz1, dz1b, st1 = _up_bwd_ln1(dup, w_up3, dz2, xhat1, rstd1, p["ln1_g"])

    gw["w_o"] = _mm_tn(mixin, dz1b, n_out=1, tn=D_MODEL, ts=512, g_block=(512, D_MODEL),
                       g_map=lambda j, s: (s, 0), name="grad_w_o").reshape(N_CHIPS, D_MODEL // N_CHIPS, D_MODEL)
    dyab, dgab = _mix_bwd(dz1b, w_o, proj, yab)
    gw["w_a"] =_mm_tn(ya_in, dyab, n_out=1, tn=D_MODEL, ts=512, g_block=(512, D_MODEL),
                       g_map=lambda j, s: (s, 0), name="grad_w_a").reshape(N_CHIPS, D_CONV // N_CHIPS, D_MODEL)
    gw_b = _mm_tn(comb_b, dyab, n_out=1, tn=D_MODEL, ts=1024, g_block=(1024, D_MODEL),
                  g_map=lambda j, s: (s, 1), name="grad_w_b")
    gw["w_b"] = gw_b.reshape(GROUP_W, N_CHIPS, D_MODEL // N_CHIPS).transpose(1, 0, 2)
    exchanged_mix = on_ready({n: gw[n] for n in ("w_o", "w_a", "w_b")}) or {}
    dya_in = _mm_nt(dyab, w_a, tm=512, a_col=0, name="dya_in")
    exchanged, dya_in = lax.optimization_barrier((exchanged, dya_in))
    dbch, sm_conv = _conv_gate_bwd(proj, dya_in, conv_w)
    att_stats = _comb_bwd(dyab, w_b, comb, lse_tot)
    exchanged_mix, att_stats = lax.optimization_barrier((exchanged_mix, att_stats))
    exchanged.update(exchanged_mix)
    dqkv = [_attn_bwd(qkv[g], col0[g], g, *att_stats[g]) for g in range(N_GROUPS)]

    w_pieces, b_pieces = [], []
    for nm, planes in (("bch", dbch), ("gab", dgab)):
        pw, pc = _mm_tn(h0b, planes, n_out=planes.shape[0], tn=D_MODEL, ts=1024, g_block=(None, 1024, D_MODEL),
                        g_map=lambda j, s: (j, s, 0), colsum=True, name="grad_w_in_" + nm)
        w_pieces.extend(pw[k] for k in range(planes.shape[0]))
        b_pieces.append(pc[0])
    for g in range(N_GROUPS):
        pw, pc = _mm_tn_cat(h0_rows[g], [a.reshape(S, GROUP_W) for a in dqkv[g]], ts=1024, name=f"grad_w_in_qkv{g}")
        w_pieces.append(pw)
        b_pieces.append(pc[0])
    dw_perm = jnp.concatenate(w_pieces, axis=1)
    gw["w_in"] = jnp.stack([
        jnp.concatenate([dw_perm[:, pc:pc + w] for s, c, pc, w in sorted(runs, key=lambda r: r[1]) if s == k], axis=1)
        for k in range(N_CHIPS)])
    exchanged.update(on_ready({"w_in": gw["w_in"]}) or {})
    db_blocks = jnp.concatenate(b_pieces).reshape(N_BLK, GROUP_W)
    grad_b_in = jnp.concatenate([db_blocks[b] for b in INV_PERM])

    grad_x, st0 = _in_bwd_ln0([dbch, dgab], dqkv, w_nat, w_qkv[1:], dz1, x, p["ln0_g"])

    small = {
        "loss": st2[2:3, 0:1],
        "ln0_g": st0[0], "ln0_b": st0[1], "b_in": grad_b_in, "conv_w": sm_conv[0:3],
        "b_o": st1[2], "ln1_g": st1[0], "ln1_b": st1[1],
        "b_up": jnp.concatenate([sm_ffn[0], sm_ffn[1]]), "ffn_conv_w": sm_ffn[3:6], "ffn_conv_b": sm_ffn[2],
        "b_down": st2[3], "ln2_g": st2[0], "ln2_b": st2[1],
    }
    return grad_x, exchanged or gw, small


BIG =("w_in", "w_a", "w_b", "w_o", "w_up", "w_down")
CONV = ("conv_w", "ffn_conv_w")
VECS = ("ln0_g", "ln0_b", "b_in", "b_o", "ln1_g", "ln1_b", "b_up", "ffn_conv_b", "b_down", "ln2_g", "ln2_b")
ORDER = ("ln0_g", "ln0_b", "w_in", "b_in", "conv_w", "w_a", "w_b", "w_o", "b_o", "ln1_g", "ln1_b", "w_up", "b_up",
         "ffn_conv_w", "ffn_conv_b", "w_down", "b_down", "ln2_g", "ln2_b")
SMALL_ORDER = ("loss",) + VECS + CONV


def _step(x, target, W, Mo, Vo):
    x2, t2 = x[0], target[0]
    big2 = {n: W[n][0] for n in BIG}
    halves = lambda a: a.astype(BF16).reshape(N_CORES, a.shape[0] // N_CORES, a.shape[1])
    whole = lambda g: g.reshape(N_CHIPS, g.shape[1] * g.shape[2], g.shape[3])
    later = tuple(n for n in BIG if n != "w_in")
    w_in_halves = halves(big2["w_in"])
    first = _allgather_shards([w_in_halves], [], name="allgather_w_in", collective_id=1)

    def launch_rest(h0b):
        return _allgather_shards([halves(big2[n]) for n in later] + [W[n] for n in CONV], [h0b],
                                 name="allgather_rest", collective_id=2)

    def assemble(rest):
        gathered = {n: whole(g) for n, g in zip(later + CONV, rest)}
        return {
            "w_up": gathered["w_up"],
            "w_a": gathered["w_a"].reshape(D_CONV, D_MODEL), "w_o": gathered["w_o"].reshape(D_MODEL, D_MODEL),
            "w_down": gathered["w_down"].reshape(D_FF, D_MODEL),
            "w_b": gathered["w_b"].transpose(1, 0, 2).reshape(GROUP_W, D_MODEL),
            "conv_w": gathered["conv_w"].transpose(1, 0, 2).reshape(3, D_CONV),
            "ffn_conv_w": gathered["ffn_conv_w"].transpose(1, 0, 2).reshape(3, D_FF),
        }

    pvec = {n: W[n].reshape(1, -1) for n in VECS}

    exchange_ids = iter((3, 4, 5))

    def exchange(group):
        names = tuple(group)
        res = _exchange_grads([group[n] for n in names], name="exchange_" + "_".join(names),
                              collective_id=next(exchange_ids))
        return dict(zip(names, res))

    grad_x, parts, small = _local_step(x2, t2, pvec, (whole(first[0]), launch_rest, assemble), exchange,
                                       before_ln0=[w_in_halves])
    out = {}
    for n in BIG:
        tr = {"w_in": 128, "w_up": 128, "w_b": 128}.get(n, big2[n].shape[0] // 4)
        g, d, nm, nv = _reduce_adamw(parts[n], big2[n], Mo[n][0], Vo[n][0], tr=tr, name="adamw_" + n)
        out[n] = tuple(a[None] for a in (g, d, nm, nv))

    vec, offs = _pack([small[n] for n in SMALL_ORDER])
    off = dict(zip(SMALL_ORDER, offs))
    row = lambda a: a.reshape(1, -1)
    allv = _allgather_small(vec, parts["w_in"])
    tot, vec_out = _reduce_adamw_vectors(allv, [off[n] for n in VECS], [row(W[n]) for n in VECS],
                                         [row(Mo[n]) for n in VECS], [row(Vo[n]) for n in VECS])
    for n, res in zip(VECS, vec_out):
        out[n] = tuple(a.reshape(W[n].shape) for a in res)
    loss = tot[0, off["loss"]]
    chip = 2 * lax.axis_index("x") + lax.axis_index("y")
    taps_g = []
    for n in CONV:
        width = W[n].shape[2]
        full = lax.slice(tot, (0, off[n]), (1, off[n] + 3 * N_CHIPS * width)).reshape(3, N_CHIPS * width)
        taps_g.append(lax.dynamic_slice_in_dim(full, chip * width, width, axis=1))
    taps_out = _adamw_taps([W[n][0] for n in CONV], taps_g, [Mo[n][0] for n in CONV], [Vo[n][0] for n in CONV])
    for n, g, res in zip(CONV, taps_g, taps_out):
        out[n] = tuple(a[None] for a in (g,) + res)

    res = [loss, grad_x[None]]
    for k in range(4):
        res += [out[n][k] for n in ORDER]
    return tuple(res)


def kernel(x, ln0_g, ln0_b, w_in, b_in, conv_w, w_a, w_b, w_o, b_o, ln1_g, ln1_b, w_up, b_up, ffn_conv_w, ffn_conv_b, w_down, b_down, ln2_g, ln2_b, loss_target, m_ln0_g, m_ln0_b, m_w_in, m_b_in, m_conv_w, m_w_a, m_w_b, m_w_o, m_b_o, m_ln1_g, m_ln1_b, m_w_up, m_b_up, m_ffn_conv_w, m_ffn_conv_b, m_w_down, m_b_down, m_ln2_g, m_ln2_b, v_ln0_g, v_ln0_b, v_w_in, v_b_in, v_conv_w, v_w_a, v_w_b, v_w_o, v_b_o, v_ln1_g, v_ln1_b, v_w_up, v_b_up, v_ffn_conv_w, v_ffn_conv_b, v_w_down, v_b_down, v_ln2_g, v_ln2_b):
    W = dict(zip(ORDER, (ln0_g, ln0_b, w_in, b_in, conv_w, w_a, w_b, w_o, b_o, ln1_g, ln1_b, w_up, b_up,
                         ffn_conv_w, ffn_conv_b, w_down, b_down, ln2_g, ln2_b)))
    Mo = dict(zip(ORDER, (m_ln0_g, m_ln0_b, m_w_in, m_b_in, m_conv_w, m_w_a, m_w_b, m_w_o, m_b_o, m_ln1_g, m_ln1_b,
                          m_w_up, m_b_up, m_ffn_conv_w, m_ffn_conv_b, m_w_down, m_b_down, m_ln2_g, m_ln2_b)))
    Vo = dict(zip(ORDER, (v_ln0_g, v_ln0_b, v_w_in, v_b_in, v_conv_w, v_w_a, v_w_b, v_w_o, v_b_o, v_ln1_g, v_ln1_b,
                          v_w_up, v_b_up, v_ffn_conv_w, v_ffn_conv_b, v_w_down, v_b_down, v_ln2_g, v_ln2_b)))
    return _step(x, loss_target, W, Mo, Vo)
```

```python
import functools
import math

import jax
import jax.numpy as jnp
from jax import lax
from jax.experimental import pallas as pl
from jax.experimental.pallas import tpu as pltpu
from jax.experimental.pallas import tpu_sc as plsc

F32 = jnp.float32
BF16 = jnp.bfloat16

D_MODEL = 1024
D_CONV = D_MODEL
HEAD_DIM = 64
HEADS_PER_GROUP = 8
GROUPS = ((128, 1), (512, 4), (2048, 16))
N_GROUPS = len(GROUPS)
GROUP_W = HEADS_PER_GROUP * HEAD_DIM
QKV_W = N_GROUPS * GROUP_W
RADIUS = 64
D_FF = 2816
LN_EPS = 1e-5
ALPHA = 2.0 ** 0.25
MASK_VALUE = -1e30
ATT_SCALE = HEAD_DIM ** -0.5
OFF_B = 0
OFF_C = OFF_B + D_CONV
OFF_H = OFF_C + D_CONV
OFF_Q = OFF_H + D_CONV
OFF_K = OFF_Q + QKV_W
OFF_V = OFF_K + QKV_W
OFF_GA = OFF_V + QKV_W
OFF_GB = OFF_GA + D_MODEL
N_IN = OFF_GB + D_MODEL
ADAM_LR = 0.001
ADAM_B1 = 0.9
ADAM_B2 = 0.999
ADAM_EPS = 1e-08
ADAM_WD = 0.01
ADAM_STEP = 10
INV_SQRT2 = 0.7071067811865476
INV_SQRT_2PI = 0.3989422804014327

LANES = 128
SUBLANES = 8
VMEM_BYTES_V7X = 64 * 1024 * 1024
N_CHIPS = 4
N_CORES = 2
N_DEV = N_CHIPS * N_CORES
MESH = pl.DeviceIdType.MESH

N_BLK = N_IN // GROUP_W
PERM = (0, 1, 2, 3, 4, 5, 15, 16, 17, 18, 6, 9, 12, 7, 10, 13, 8, 11, 14)
INV_PERM = tuple(PERM.index(b) for b in range(N_BLK))
P_B, P_C, P_H, P_GA, P_GB, P_Q0 = 0, 1024, 2048, 3072, 4096, 5120
N_NAT = P_Q0 + QKV_W // N_GROUPS * 3
N_GATED = P_Q0

def _col_runs():
    shard_w = N_IN // N_CHIPS
    runs = []
    for pos, blk in enumerate(PERM):
        c, end = blk * GROUP_W, (blk + 1) * GROUP_W
        while c < end:
            stop = min(end, (c // shard_w + 1) * shard_w)
            runs.append((c // shard_w, c % shard_w, pos * GROUP_W + c - blk * GROUP_W, stop - c))
            c = stop
    return runs


SLAB = 128
CHUNK = 256
PAD = SUBLANES
TQ = 128


def _cparams(sem, vmem_mb):
    assert vmem_mb * 1024 * 1024 < VMEM_BYTES_V7X
    return pltpu.CompilerParams(dimension_semantics=sem, vmem_limit_bytes=vmem_mb * 1024 * 1024)


def _resident(shape):
    nd = len(shape)
    return pl.BlockSpec(shape, lambda *_: (0,) * nd, pipeline_mode=pl.Buffered(1))


def _hbm(*arrays):
    return [pltpu.with_memory_space_constraint(a, pltpu.HBM) for a in arrays]


def _dot(a, b):
    return jnp.dot(a, b, preferred_element_type=F32)


def _dot_nt(a, b):
    return lax.dot_general(a, b, (((1,), (1,)), ((), ())), preferred_element_type=F32)


def _dot_tn(a, b):
    return lax.dot_general(a, b, (((0,), (0,)), ((), ())), preferred_element_type=F32)


def _ln_stats(z):
    mu = jnp.mean(z, -1, keepdims=True)
    zc = z - mu
    var = jnp.mean(zc * zc, -1, keepdims=True)
    rstd = lax.rsqrt(var + LN_EPS)
    return zc * rstd, rstd


def _ln_bwd(dh, xhat, rstd, g):
    dxh = dh * g
    m1 = jnp.mean(dxh, -1, keepdims=True)
    m2 = jnp.mean(dxh * xhat, -1, keepdims=True)
    return rstd * (dxh - m1 - xhat * m2)


def _rows8(rows, width):
    pad = [jnp.zeros((1, width), F32)] * (SUBLANES - len(rows))
    return jnp.concatenate(list(rows) + pad, axis=0)


def _mm_nn(a, w, bias, *, tm, tn, out_dtype, name, vmem_mb=40):
    M, K = a.shape
    if w.ndim == 3:
        assert w.shape[2] == tn
        n_tiles = w.shape[0]
        w_spec = pl.BlockSpec((None, K, tn), lambda j, i: (j, 0, 0))
    else:
        n_tiles = w.shape[1] // tn
        w_spec = pl.BlockSpec((K, tn), lambda j, i: (0, j))

    def body(a_ref, w_ref, b_ref, o_ref):
        o_ref[...] = (_dot(a_ref[...], w_ref[...]) + b_ref[...]).astype(o_ref.dtype)

    return pl.pallas_call(
        body, grid=(n_tiles, M // tm),
        in_specs=[pl.BlockSpec((tm, K), lambda j, i: (i, 0)), w_spec, pl.BlockSpec((1, tn), lambda j, i: (0, j))],
        out_specs=pl.BlockSpec((tm, tn), lambda j, i: (i, j)),
        out_shape=jax.ShapeDtypeStruct((M, n_tiles * tn), out_dtype),
        name=name, compiler_params=_cparams(("arbitrary", "parallel"), vmem_mb))(*_hbm(a, w, bias))


def _mm_nt(a, w, *, tm, a_col=0, name, vmem_mb=40):
    M = a.shape[0]
    N, K = w.shape

    def body(a_ref, w_ref, o_ref):
        o_ref[...] = _dot_nt(a_ref[...], w_ref[...]).astype(o_ref.dtype)

    return pl.pallas_call(
        body, grid=(M // tm,),
        in_specs=[pl.BlockSpec((tm, K), lambda i: (i, a_col)),
                  pl.BlockSpec((N, K), lambda i: (0, 0))],
        out_specs=pl.BlockSpec((tm, N), lambda i: (i, 0)),
        out_shape=jax.ShapeDtypeStruct((M, N), BF16),
        name=name, compiler_params=_cparams(("parallel",), vmem_mb))(*_hbm(a, w))


def _mm_tn(a, g, *, n_out, tn, ts, g_block, g_map, colsum=False, name, vmem_mb=48):
    S, K = a.shape
    n_s = S // ts

    def body(a_ref, g_ref, *rest):
        if colsum:
            o_ref, cs_ref, acc_ref, cacc_ref = rest
        else:
            o_ref, acc_ref = rest
        s = pl.program_id(1)

        @pl.when(s == 0)
        def _():
            acc_ref[...] = jnp.zeros_like(acc_ref)
            if colsum:
                cacc_ref[...] = jnp.zeros_like(cacc_ref)

        gv = g_ref[...]
        acc_ref[...] += _dot_tn(a_ref[...], gv)
        if colsum:
            cacc_ref[...] += jnp.broadcast_to(jnp.sum(gv.astype(F32), axis=0, keepdims=True), cacc_ref.shape)

        @pl.when(s == n_s - 1)
        def _():
            o_ref[...] = acc_ref[...].astype(o_ref.dtype)
            if colsum:
                cs_ref[...] = cacc_ref[...]

    out_specs = [pl.BlockSpec((None, K, tn), lambda j, s: (j, 0, 0))]
    out_shape = [jax.ShapeDtypeStruct((n_out, K, tn), BF16)]
    scratch = [pltpu.VMEM((K, tn), F32)]
    if colsum:
        out_specs.append(pl.BlockSpec((SUBLANES, tn), lambda j, s: (0, j)))
        out_shape.append(jax.ShapeDtypeStruct((SUBLANES, n_out * tn), F32))
        scratch.append(pltpu.VMEM((SUBLANES, tn), F32))
    res = pl.pallas_call(
        body, grid=(n_out, n_s),
        in_specs=[pl.BlockSpec((ts, K), lambda j, s: (s, 0)), pl.BlockSpec(g_block, g_map)],
        out_specs=out_specs, out_shape=out_shape, scratch_shapes=scratch,
        name=name, compiler_params=_cparams(("parallel", "arbitrary"), vmem_mb))(*_hbm(a, g))
    return res if colsum else res[0]


def _mm_tn_cat(a, gs, *, ts, name, vmem_mb=40):
    S, K = a.shape
    widths = [g.shape[1] for g in gs]
    n_s, total = S // ts, sum(widths)

    def body(*refs):
        a_ref, g_refs = refs[0], refs[1:1 + len(gs)]
        o_ref, cs_ref, acc_ref, cacc_ref = refs[1 + len(gs):]
        s = pl.program_id(0)

        @pl.when(s == 0)
        def _():
            acc_ref[...] = jnp.zeros_like(acc_ref)
            cacc_ref[...] = jnp.zeros_like(cacc_ref)

        av, col = a_ref[...], 0
        for g_ref, w in zip(g_refs, widths):
            gv = g_ref[...]
            acc_ref[:, col:col + w] += _dot_tn(av, gv)
            cacc_ref[:, col:col + w] += jnp.broadcast_to(jnp.sum(gv.astype(F32), axis=0, keepdims=True), (SUBLANES, w))
            col += w

        @pl.when(s == n_s - 1)
        def _():
            o_ref[...] = acc_ref[...].astype(BF16)
            cs_ref[...] = cacc_ref[...]

    return pl.pallas_call(
        body, grid=(n_s,),
        in_specs=[pl.BlockSpec((ts, K), lambda s: (s, 0))] + [pl.BlockSpec((ts, w), lambda s: (s, 0)) for w in widths],
        out_specs=[pl.BlockSpec((K, total), lambda s: (0, 0)), pl.BlockSpec((SUBLANES, total), lambda s: (0, 0))],
        out_shape=[jax.ShapeDtypeStruct((K, total), BF16), jax.ShapeDtypeStruct((SUBLANES, total), F32)],
        scratch_shapes=[pltpu.VMEM((K, total), F32), pltpu.VMEM((SUBLANES, total), F32)],
        name=name, compiler_params=_cparams(("arbitrary",), vmem_mb))(*_hbm(a, *gs))


DILS = tuple(d for _, d in GROUPS if d > 1)


def _res_spec(d, tm, width):
    return pl.BlockSpec((d, tm // d, width), lambda i: (0, i, 0))


def _lane_scratch(tm, width):
    return [pltpu.VMEM((tm, LANES), F32)] * (width // LANES)


def _to_residue(val, dst_refs, dils, tm, dtype, scr):
    for c, ref in enumerate(scr):
        ref[...] = val[:, c * LANES:(c + 1) * LANES]
    for dst_ref, d in zip(dst_refs, dils):
        for r in range(d):
            cols = [ref[pl.ds(r, tm // d, stride=d), :] for ref in scr]
            dst_ref[r] = jnp.concatenate(cols, axis=1).astype(dtype)


def _from_residue(rows_of, d, tm, scr):
    for r in range(d):
        v = rows_of(r).astype(F32)
        for c, ref in enumerate(scr):
            ref[pl.ds(r, tm // d, stride=d), :] = v[:, c * LANES:(c + 1) * LANES]
    return jnp.concatenate([ref[...] for ref in scr], axis=1)


def _ln0_fwd(x, g, b, after=(), *, tm=512):
    S, Dm = x.shape
    n_after = len(after)

    def body(x_ref, g_ref, b_ref, *rest):
        h_ref, hb_ref, *rest = rest[n_after:]
        xhat, _ = _ln_stats(x_ref[...])
        h = xhat * g_ref[...] + b_ref[...]
        h_ref[...] = h
        hb_ref[...] = h.astype(BF16)
        _to_residue(h, rest[:len(DILS)], DILS, tm, BF16, rest[len(DILS):])

    row = pl.BlockSpec((tm, Dm), lambda i: (i, 0))
    vec = pl.BlockSpec((1, Dm), lambda i: (0, 0))
    return pl.pallas_call(
        body, grid=(S // tm,), in_specs=[row, vec, vec] + [pl.BlockSpec(memory_space=pl.ANY)] * n_after,
        out_specs=[row, row] + [_res_spec(d, tm, Dm) for d in DILS],
        out_shape=[jax.ShapeDtypeStruct((S, Dm), F32), jax.ShapeDtypeStruct((S, Dm), BF16)]
        + [jax.ShapeDtypeStruct((d, S // d, Dm), BF16) for d in DILS],
        scratch_shapes=_lane_scratch(tm, Dm),
        name="ln0_fwd", compiler_params=_cparams(("parallel",), 32))(*_hbm(x, g, b), *after)


def _slab_spec(S, col0):
    return pl.BlockSpec((S, SLAB), lambda j: (0, col0 // SLAB + j))


def _zero_pads(scr, S):
    scr[0:PAD, :] = jnp.zeros((PAD, SLAB), F32)
    scr[S + PAD:S + 2 * PAD, :] = jnp.zeros((PAD, SLAB), F32)


def _shifted(scr, t):
    return (scr[PAD - 1 + t:PAD - 1 + t + CHUNK, :], scr[PAD + t:PAD + t + CHUNK, :],
            scr[PAD + 1 + t:PAD + 1 + t + CHUNK, :])


def _conv_gate_fwd(proj, conv_w):
    S = proj.shape[0]

    def body(b_ref, c_ref, h_ref, w_ref, o_ref, u_scr):
        _zero_pads(u_scr, S)
        for t in range(0, S, CHUNK):
            u_scr[PAD + t:PAD + t + CHUNK, :] = c_ref[t:t + CHUNK, :].astype(F32) * h_ref[t:t + CHUNK, :].astype(F32)
        w0, w1, w2 = w_ref[0:1, :], w_ref[1:2, :], w_ref[2:3, :]
        for t in range(0, S, CHUNK):
            um, u0, up = _shifted(u_scr, t)
            cv = w0 * um + w1 * u0 + w2 * up
            o_ref[t:t + CHUNK, :] = (b_ref[t:t + CHUNK, :].astype(F32) * cv).astype(BF16)

    return pl.pallas_call(
        body, grid=(D_CONV // SLAB,),
        in_specs=[_slab_spec(S, P_B), _slab_spec(S, P_C), _slab_spec(S, P_H),
                  pl.BlockSpec((3, SLAB), lambda j: (0, j))],
        out_specs=pl.BlockSpec((S, SLAB), lambda j: (0, j)),
        out_shape=jax.ShapeDtypeStruct((S, D_CONV), BF16),
        scratch_shapes=[pltpu.VMEM((S + 2 * PAD, SLAB), F32)],
        name="conv_gate_fwd", compiler_params=_cparams(("parallel",), 40))(*_hbm(proj, proj, proj, conv_w))


MASKED_DISTANCE = -1e34


def _attn_bias_table(g):
    dil = GROUPS[g][1]
    j = lax.broadcasted_iota(jnp.int32, (2 * TQ, TQ), 0)
    a = lax.broadcasted_iota(jnp.int32, (2 * TQ, TQ), 1)
    rel = jnp.abs(j - RADIUS - a)
    base = -(rel * dil).astype(F32)
    inside, after_start, before_end = rel <= RADIUS, j >= RADIUS, j < TQ + RADIUS
    variants = []
    for first, last in ((False, False), (True, False), (False, True), (True, True)):
        valid = inside & (after_start if first else True) & (before_end if last else True)
        variants.append(jnp.where(valid, base, MASKED_DISTANCE))
    return jnp.stack(variants)


SUBS = 4
TB = SUBS * TQ


def _ext_window(p_ref, c_ref, n_ref):
    return jnp.concatenate([p_ref[TB - RADIUS:, :], c_ref[...], n_ref[:RADIUS, :]], axis=0)


def _head_stats(rows):
    pad = jnp.zeros((LANES - len(rows), TQ), F32)
    return jnp.concatenate(list(rows) + [pad], axis=0).T


def _slope(g, h):
    return 2.0 ** (-8.0 * (g * HEADS_PER_GROUP + h + 1) / (N_GROUPS * HEADS_PER_GROUP))


def _pair(a, h):
    return a[:, (h // 2) * LANES:(h // 2 + 1) * LANES]


def _own_lanes(a, h):
    lane = lax.broadcasted_iota(jnp.int32, a.shape, 1)
    return jnp.where((lane >= HEAD_DIM) == (h % 2 == 1), a, jnp.zeros_like(a))


def _own_rows(a, h):
    return a[(h % 2) * HEAD_DIM:(h % 2 + 1) * HEAD_DIM, :]


def _attn_fwd(qkv, col0, g):
    dil, sub, _ = qkv.shape
    nb = sub // TB
    heads = HEADS_PER_GROUP

    def body(q_ref, kp, kc, kn, vp, vc, vn, bias_ref, o_ref, lse_ref, ot_scr, s_scr, p_scr):
        i = pl.program_id(1)
        kext = _ext_window(kp, kc, kn)
        vext = _ext_window(vp, vc, vn)
        q = q_ref[...] * ATT_SCALE
        for b in range(SUBS):
            kwin, qb = kext[b * TQ:(b + 2) * TQ, :], q[b * TQ:(b + 1) * TQ, :]
            for h in range(heads):
                s_scr[b * heads + h] = _dot_nt(_pair(kwin, h), _own_lanes(_pair(qb, h), h))
        inv_den = []
        for b in range(SUBS):
            block = i * SUBS + b
            bias = bias_ref[jnp.where(block == 0, 1, 0) + jnp.where(block == nb * SUBS - 1, 2, 0)]
            lse = []
            for h in range(heads):
                s = s_scr[b * heads + h] + _slope(g, h) * bias
                m = jnp.max(s, axis=0, keepdims=True)
                p = jnp.exp(s - m)
                den = jnp.sum(p, axis=0, keepdims=True)
                p_scr[b * heads + h] = p.astype(BF16)
                inv_den.append(1.0 / den)
                lse.append(m + jnp.log(den))
            lse_ref[b * TQ:(b + 1) * TQ, :] = _head_stats(lse)
        for b in range(SUBS):
            vwin = vext[b * TQ:(b + 2) * TQ, :]
            for h in range(heads):
                ot = _dot_tn(_pair(vwin, h), p_scr[b * heads + h])
                ot_scr[h * HEAD_DIM:(h + 1) * HEAD_DIM, b * TQ:(b + 1) * TQ] = _own_rows(ot, h) * inv_den[b * heads + h]
        o_ref[...] = ot_scr[...].T

    def spec(col, shift):
        return pl.BlockSpec((None, TB, GROUP_W), lambda r, i: (r, jnp.clip(i + shift, 0, nb - 1), col))

    return pl.pallas_call(
        body, grid=(dil, nb),
        in_specs=[spec(col0, 0), spec(col0 + 1, -1), spec(col0 + 1, 0), spec(col0 + 1, 1),
                  spec(col0 + 2, -1), spec(col0 + 2, 0), spec(col0 + 2, 1),
                  pl.BlockSpec((4, 2 * TQ, TQ), lambda r, i: (0, 0, 0))],
        out_specs=[pl.BlockSpec((None, TB, GROUP_W), lambda r, i: (r, i, 0)),
                   pl.BlockSpec((None, TB, LANES), lambda r, i: (r, i, 0))],
        out_shape=[jax.ShapeDtypeStruct((dil, sub, GROUP_W), F32), jax.ShapeDtypeStruct((dil, sub, LANES), F32)],
        scratch_shapes=[pltpu.VMEM((GROUP_W, TB), F32), pltpu.VMEM((SUBS * heads, 2 * TQ, TQ), F32),
                        pltpu.VMEM((SUBS * heads, 2 * TQ, TQ), BF16)],
        name=f"attn_fwd_g{g}", compiler_params=_cparams(("parallel", "arbitrary"), 32))(
            *_hbm(*([qkv] * 7), _attn_bias_table(g)))


def _expand_heads():
    h = lax.broadcasted_iota(jnp.int32, (LANES, GROUP_W), 0)
    c = lax.broadcasted_iota(jnp.int32, (LANES, GROUP_W), 1)
    return (c // HEAD_DIM == h).astype(F32)


def _dot_f32(a, b):
    return jnp.dot(a, b, preferred_element_type=F32, precision=lax.Precision.HIGHEST)


def _attn_combine(outs, lses, *, tm=512):
    S = outs[0].shape[1]
    n_col = GROUP_W // LANES

    def body(*refs):
        ins, e_ref = refs[:2 * N_GROUPS], refs[2 * N_GROUPS]
        c_ref, cb_ref, lt_ref = refs[2 * N_GROUPS + 1:2 * N_GROUPS + 4]
        scr = refs[2 * N_GROUPS + 4:]
        o, l = [ins[0][0]], [ins[N_GROUPS][0]]
        for k, d in enumerate(DILS):
            o_ref, l_ref = ins[1 + k], ins[N_GROUPS + 1 + k]
            o.append(_from_residue(lambda r: o_ref[r], d, tm, scr[k * (n_col + 1):k * (n_col + 1) + n_col]))
            l.append(_from_residue(lambda r: l_ref[r], d, tm, scr[k * (n_col + 1) + n_col:(k + 1) * (n_col + 1)]))
        m = jnp.maximum(jnp.maximum(l[0], l[1]), l[2])
        e = [jnp.exp(v - m) for v in l]
        den = e[0] + e[1] + e[2]
        comb = sum(_dot_f32(ev / den, e_ref[...]) * ov for ev, ov in zip(e, o))
        c_ref[...] = comb
        cb_ref[...] = comb.astype(BF16)
        lt_ref[...] = m + jnp.log(den)

    row = pl.BlockSpec((tm, GROUP_W), lambda i: (i, 0))
    dils = [d for _, d in GROUPS]
    return pl.pallas_call(
        body, grid=(S // tm,),
        in_specs=[_res_spec(d, tm, GROUP_W) for d in dils] + [_res_spec(d, tm, LANES) for d in dils]
        + [_resident((LANES, GROUP_W))],
        out_specs=[row, row, pl.BlockSpec((tm, LANES), lambda i: (i, 0))],
        out_shape=[jax.ShapeDtypeStruct((S, GROUP_W), F32), jax.ShapeDtypeStruct((S, GROUP_W), BF16),
                   jax.ShapeDtypeStruct((S, LANES), F32)],
        scratch_shapes=_lane_scratch(tm, GROUP_W + LANES) * len(DILS),
        name="attn_combine", compiler_params=_cparams(("parallel",), 32))(*_hbm(*outs, *lses, _expand_heads()))


def _branch_mix(ya_in, comb_b, w_a, w_b, proj, *, tm=512):
    S = ya_in.shape[0]

    def body(ya_ref, cb_ref, wa_ref, wb_ref, ga_ref, gb_ref, yab_ref, mx_ref):
        y_a = _dot(ya_ref[...], wa_ref[...])
        y_b = _dot(cb_ref[...], wb_ref[...])
        yab_ref[:, 0:D_MODEL] = y_a.astype(BF16)
        yab_ref[:, D_MODEL:2 * D_MODEL] = y_b.astype(BF16)
        mx = jax.nn.sigmoid(ga_ref[...].astype(F32)) * y_a + jax.nn.sigmoid(gb_ref[...].astype(F32)) * y_b
        mx_ref[...] = mx.astype(BF16)

    return pl.pallas_call(
        body, grid=(S // tm,),
        in_specs=[pl.BlockSpec((tm, D_CONV), lambda i: (i, 0)), pl.BlockSpec((tm, GROUP_W), lambda i: (i, 0)),
                  pl.BlockSpec((D_CONV, D_MODEL), lambda i: (0, 0)), pl.BlockSpec((GROUP_W, D_MODEL), lambda i: (0, 0)),
                  pl.BlockSpec((tm, D_MODEL), lambda i: (i, P_GA // D_MODEL)),
                  pl.BlockSpec((tm, D_MODEL), lambda i: (i, P_GB // D_MODEL))],
        out_specs=[pl.BlockSpec((tm, 2 * D_MODEL), lambda i: (i, 0)), pl.BlockSpec((tm, D_MODEL), lambda i: (i, 0))],
        out_shape=[jax.ShapeDtypeStruct((S, 2 * D_MODEL), BF16), jax.ShapeDtypeStruct((S, D_MODEL), BF16)],
        name="branch_mix", compiler_params=_cparams(("parallel",), 40))(*_hbm(ya_in, comb_b, w_a, w_b, proj, proj))


def _mix_ln1(mixin, w_o, b_o, h0, g1, b1, *, tm=512):
    S = mixin.shape[0]

    def body(mx_ref, wo_ref, bo_ref, h0_ref, g_ref, b_ref, xh_ref, rs_ref, h1b_ref):
        z = ALPHA * h0_ref[...] + _dot(mx_ref[...], wo_ref[...]) + bo_ref[...]
        xhat, rstd = _ln_stats(z)
        xh_ref[...] = xhat
        rs_ref[...] = jnp.broadcast_to(rstd, (tm, LANES))
        h1b_ref[...] = (xhat * g_ref[...] + b_ref[...]).astype(BF16)

    row = pl.BlockSpec((tm, D_MODEL), lambda i: (i, 0))
    vec = pl.BlockSpec((1, D_MODEL), lambda i: (0, 0))
    return pl.pallas_call(
        body, grid=(S // tm,),
        in_specs=[row, pl.BlockSpec((D_MODEL, D_MODEL), lambda i: (0, 0)), vec, row, vec, vec],
        out_specs=[row, pl.BlockSpec((tm, LANES), lambda i: (i, 0)), row],
        out_shape=[jax.ShapeDtypeStruct((S, D_MODEL), F32), jax.ShapeDtypeStruct((S, LANES), F32),
                   jax.ShapeDtypeStruct((S, D_MODEL), BF16)],
        name="mix_ln1", compiler_params=_cparams(("parallel",), 40))(*_hbm(mixin, w_o, b_o, h0, g1, b1))


def _gelu_parts(cz):
    cdf = 0.5 * (1.0 + lax.erf(cz * INV_SQRT2))
    return cdf, cz * cdf


def _ffn_conv_fwd(up, cw, cb):
    S = up.shape[0]

    def body(a_ref, g_ref, w_ref, cb_ref, o_ref, a_scr):
        _zero_pads(a_scr, S)
        for t in range(0, S, CHUNK):
            a_scr[PAD + t:PAD + t + CHUNK, :] = a_ref[t:t + CHUNK, :].astype(F32)
        w0, w1, w2 = w_ref[0:1, :], w_ref[1:2, :], w_ref[2:3, :]
        for t in range(0, S, CHUNK):
            am, a0, ap = _shifted(a_scr, t)
            _, gel = _gelu_parts(w0 * am + w1 * a0 + w2 * ap + cb_ref[...])
            o_ref[t:t + CHUNK, :] = (gel * g_ref[t:t + CHUNK, :].astype(F32)).astype(BF16)

    return pl.pallas_call(
        body, grid=(D_FF // SLAB,),
        in_specs=[_slab_spec(S, 0), _slab_spec(S, D_FF), pl.BlockSpec((3, SLAB), lambda j: (0, j)),
                  pl.BlockSpec((1, SLAB), lambda j: (0, j))],
        out_specs=pl.BlockSpec((S, SLAB), lambda j: (0, j)),
        out_shape=jax.ShapeDtypeStruct((S, D_FF), BF16),
        scratch_shapes=[pltpu.VMEM((S + 2 * PAD, SLAB), F32)],
        name="ffn_conv_fwd", compiler_params=_cparams(("parallel",), 40))(*_hbm(up, up, cw, cb))


def _down_ln2_loss(f, w_down, b_down, xhat1, g1, b1, g2, b2, target, *, tm=512):
    S = f.shape[0]

    def body(f_ref, wd_ref, bd_ref, xh1_ref, g1_ref, b1_ref, g2_ref, b2_ref, t_ref, dz_ref, dzb_ref, st_ref):
        h1 = xh1_ref[...] * g1_ref[...] + b1_ref[...]
        z = ALPHA * h1 + _dot(f_ref[...], wd_ref[...]) + bd_ref[...]
        xhat, rstd = _ln_stats(z)
        err = xhat * g2_ref[...] + b2_ref[...] - t_ref[...]
        loss = (0.5 / D_MODEL) * jnp.sum(jnp.sum(err * err, axis=1, keepdims=True), axis=0, keepdims=True)
        dh2 = err * (1.0 / D_MODEL)
        dz = _ln_bwd(dh2, xhat, rstd, g2_ref[...])
        dz_ref[...] = dz
        dzb_ref[...] = dz.astype(BF16)
        upd = _rows8([jnp.sum(dh2 * xhat, axis=0, keepdims=True), jnp.sum(dh2, axis=0, keepdims=True),
                      jnp.broadcast_to(loss, (1, D_MODEL)), jnp.sum(dz, axis=0, keepdims=True)], D_MODEL)

        @pl.when(pl.program_id(0) == 0)
        def _():
            st_ref[...] = upd

        @pl.when(pl.program_id(0) != 0)
        def _():
            st_ref[...] += upd

    row = pl.BlockSpec((tm, D_MODEL), lambda i: (i, 0))
    vec = pl.BlockSpec((1, D_MODEL), lambda i: (0, 0))
    return pl.pallas_call(
        body, grid=(S // tm,),
        in_specs=[pl.BlockSpec((tm, D_FF), lambda i: (i, 0)), _resident((D_FF, D_MODEL)),
                  vec, row, vec, vec, vec, vec, row],
        out_specs=[row, row, pl.BlockSpec((SUBLANES, D_MODEL), lambda i: (0, 0))],
        out_shape=[jax.ShapeDtypeStruct((S, D_MODEL), F32), jax.ShapeDtypeStruct((S, D_MODEL), BF16),
                   jax.ShapeDtypeStruct((SUBLANES, D_MODEL), F32)],
        name="down_ln2_loss", compiler_params=_cparams(("arbitrary",), 56))(
            *_hbm(f, w_down, b_down, xhat1, g1, b1, g2, b2, target))


def _ffn_conv_bwd(up, df, cw, cb):
    S = up.shape[0]

    def body(a_ref, g_ref, df_ref, w_ref, cb_ref, dup_ref, sm_ref, a_scr, d_scr):
        _zero_pads(a_scr, S)
        _zero_pads(d_scr, S)
        for t in range(0, S, CHUNK):
            a_scr[PAD + t:PAD + t + CHUNK, :] = a_ref[t:t + CHUNK, :].astype(F32)
        w0, w1, w2 = w_ref[0:1, :], w_ref[1:2, :], w_ref[2:3, :]
        zero = jnp.zeros((1, SLAB), F32)
        s_dg, s_dcz, s_w0, s_w1, s_w2 = zero, zero, zero, zero, zero
        for t in range(0, S, CHUNK):
            am, a0, ap = _shifted(a_scr, t)
            cz = w0 * am + w1 * a0 + w2 * ap + cb_ref[...]
            cdf, gel = _gelu_parts(cz)
            dfv = df_ref[t:t + CHUNK, :].astype(F32)
            dgte = dfv * gel
            dcz = dfv * g_ref[t:t + CHUNK, :].astype(F32) * (cdf + cz * jnp.exp(-0.5 * cz * cz) * INV_SQRT_2PI)
            dup_ref[1, t:t + CHUNK, :] = dgte.astype(BF16)
            d_scr[PAD + t:PAD + t + CHUNK, :] = dcz
            s_dg = s_dg + jnp.sum(dgte, axis=0, keepdims=True)
            s_dcz = s_dcz + jnp.sum(dcz, axis=0, keepdims=True)
            s_w0 = s_w0 + jnp.sum(dcz * am, axis=0, keepdims=True)
            s_w1 = s_w1 + jnp.sum(dcz * a0, axis=0, keepdims=True)
            s_w2 = s_w2 + jnp.sum(dcz * ap, axis=0, keepdims=True)
        s_da = zero
        for t in range(0, S, CHUNK):
            dm, d0, dp = _shifted(d_scr, t)
            da = w0 * dp + w1 * d0 + w2 * dm
            dup_ref[0, t:t + CHUNK, :] = da.astype(BF16)
            s_da = s_da + jnp.sum(da, axis=0, keepdims=True)
        sm_ref[...] = _rows8([s_da, s_dg, s_dcz, s_w0, s_w1, s_w2], SLAB)

    return pl.pallas_call(
        body, grid=(D_FF // SLAB,),
        in_specs=[_slab_spec(S, 0), _slab_spec(S, D_FF), pl.BlockSpec((S, SLAB), lambda j: (0, j)),
                  pl.BlockSpec((3, SLAB), lambda j: (0, j)), pl.BlockSpec((1, SLAB), lambda j: (0, j))],
        out_specs=[pl.BlockSpec((2, S, SLAB), lambda j: (0, 0, j)), pl.BlockSpec((SUBLANES, SLAB), lambda j: (0, j))],
        out_shape=[jax.ShapeDtypeStruct((2, S, D_FF), BF16), jax.ShapeDtypeStruct((SUBLANES, D_FF), F32)],
        scratch_shapes=[pltpu.VMEM((S + 2 * PAD, SLAB), F32)] * 2,
        name="ffn_conv_bwd", compiler_params=_cparams(("parallel",), 48))(*_hbm(up, up, df, cw, cb))


def _up_bwd_ln1(dup, w_up3, dz2, xhat1, rstd1, g1, *, tm=512):
    S = dz2.shape[0]
    ns, _, tk = w_up3.shape
    per_plane = D_FF // tk

    def body(du_ref, w_ref, dz2_ref, xh_ref, rs_ref, g_ref, dz_ref, dzb_ref, st_ref):
        dh = ALPHA * dz2_ref[...]
        for k in range(ns):
            col = (k % per_plane) * tk
            dh = dh + _dot_nt(du_ref[k // per_plane, :, col:col + tk], w_ref[k])
        xhat = xh_ref[...]
        dz = _ln_bwd(dh, xhat, rs_ref[:, 0:1], g_ref[...])
        dz_ref[...] = dz
        dzb_ref[...] = dz.astype(BF16)
        upd = _rows8([jnp.sum(dh * xhat, axis=0, keepdims=True), jnp.sum(dh, axis=0, keepdims=True),
                      jnp.sum(dz, axis=0, keepdims=True)], D_MODEL)

        @pl.when(pl.program_id(0) == 0)
        def _():
            st_ref[...] = upd

        @pl.when(pl.program_id(0) != 0)
        def _():
            st_ref[...] += upd

    row = pl.BlockSpec((tm, D_MODEL), lambda i: (i, 0))
    return pl.pallas_call(
        body, grid=(S // tm,),
        in_specs=[pl.BlockSpec((dup.shape[0], tm, D_FF), lambda i: (0, i, 0)), _resident(w_up3.shape),
                  row, row, pl.BlockSpec((tm, LANES), lambda i: (i, 0)), pl.BlockSpec((1, D_MODEL), lambda i: (0, 0))],
        out_specs=[row, row, pl.BlockSpec((SUBLANES, D_MODEL), lambda i: (0, 0))],
        out_shape=[jax.ShapeDtypeStruct((S, D_MODEL), F32), jax.ShapeDtypeStruct((S, D_MODEL), BF16),
                   jax.ShapeDtypeStruct((SUBLANES, D_MODEL), F32)],
        name="up_bwd_ln1", compiler_params=_cparams(("arbitrary",), 56))(*_hbm(dup, w_up3, dz2, xhat1, rstd1, g1))


def _mix_bwd(dz1b, w_o, proj, yab, *, tm=512):
    S = dz1b.shape[0]

    def body(dz_ref, wo_ref, ga_ref, gb_ref, y_ref, dy_ref, dg_ref):
        dmx = _dot_nt(dz_ref[...], wo_ref[...])
        for k, gt_ref in enumerate((ga_ref, gb_ref)):
            sl = slice(k * D_MODEL, (k + 1) * D_MODEL)
            sg = jax.nn.sigmoid(gt_ref[...].astype(F32))
            dy_ref[:, sl] = (dmx * sg).astype(BF16)
            dg_ref[k] = (dmx * y_ref[:, sl].astype(F32) * sg * (1.0 - sg)).astype(BF16)

    row = pl.BlockSpec((tm, D_MODEL), lambda i: (i, 0))
    wide = pl.BlockSpec((tm, 2 * D_MODEL), lambda i: (i, 0))
    return pl.pallas_call(
        body, grid=(S // tm,),
        in_specs=[row, _resident(w_o.shape), pl.BlockSpec((tm, D_MODEL), lambda i: (i, P_GA // D_MODEL)),
                  pl.BlockSpec((tm, D_MODEL), lambda i: (i, P_GB // D_MODEL)), wide],
        out_specs=[wide, pl.BlockSpec((2, tm, D_MODEL), lambda i: (0, i, 0))],
        out_shape=[jax.ShapeDtypeStruct((S, 2 * D_MODEL), BF16), jax.ShapeDtypeStruct((2, S, D_MODEL), BF16)],
        name="mix_bwd", compiler_params=_cparams(("parallel",), 40))(*_hbm(dz1b, w_o, proj, proj, yab))


def _conv_gate_bwd(proj, dya_in, conv_w):
    S = proj.shape[0]

    def body(b_ref, c_ref, h_ref, dy_ref, w_ref, o_ref, sm_ref, u_scr, d_scr):
        _zero_pads(u_scr, S)
        _zero_pads(d_scr, S)
        for t in range(0, S, CHUNK):
            u_scr[PAD + t:PAD + t + CHUNK, :] = c_ref[t:t + CHUNK, :].astype(F32) * h_ref[t:t + CHUNK, :].astype(F32)
        w0, w1, w2 = w_ref[0:1, :], w_ref[1:2, :], w_ref[2:3, :]
        zero = jnp.zeros((1, SLAB), F32)
        s_w0, s_w1, s_w2 = zero, zero, zero
        for t in range(0, S, CHUNK):
            um, u0, up = _shifted(u_scr, t)
            dy = dy_ref[t:t + CHUNK, :].astype(F32)
            o_ref[0, t:t + CHUNK, :] = (dy * (w0 * um + w1 * u0 + w2 * up)).astype(BF16)
            dcv = dy * b_ref[t:t + CHUNK, :].astype(F32)
            d_scr[PAD + t:PAD + t + CHUNK, :] = dcv
            s_w0 = s_w0 + jnp.sum(dcv * um, axis=0, keepdims=True)
            s_w1 = s_w1 + jnp.sum(dcv * u0, axis=0, keepdims=True)
            s_w2 = s_w2 + jnp.sum(dcv * up, axis=0, keepdims=True)
        for t in range(0, S, CHUNK):
            dm, d0, dp = _shifted(d_scr, t)
            du = w0 * dp + w1 * d0 + w2 * dm
            o_ref[1, t:t + CHUNK, :] = (du * h_ref[t:t + CHUNK, :].astype(F32)).astype(BF16)
            o_ref[2, t:t + CHUNK, :] = (du * c_ref[t:t + CHUNK, :].astype(F32)).astype(BF16)
        sm_ref[...] = _rows8([s_w0, s_w1, s_w2], SLAB)

    return pl.pallas_call(
        body, grid=(D_CONV // SLAB,),
        in_specs=[_slab_spec(S, P_B), _slab_spec(S, P_C), _slab_spec(S, P_H),
                  pl.BlockSpec((S, SLAB), lambda j: (0, j)), pl.BlockSpec((3, SLAB), lambda j: (0, j))],
        out_specs=[pl.BlockSpec((3, S, SLAB), lambda j: (0, 0, j)), pl.BlockSpec((SUBLANES, SLAB), lambda j: (0, j))],
        out_shape=[jax.ShapeDtypeStruct((3, S, D_CONV), BF16), jax.ShapeDtypeStruct((SUBLANES, D_CONV), F32)],
        scratch_shapes=[pltpu.VMEM((S + 2 * PAD, SLAB), F32)] * 2,
        name="conv_gate_bwd", compiler_params=_cparams(("parallel",), 48))(*_hbm(proj, proj, proj, dya_in, conv_w))


def _comb_bwd(dyab, w_b, comb, lse_tot, *, tm=512):
    S = comb.shape[0]
    widths, dtypes = (GROUP_W, LANES, LANES), (BF16, F32, F32)

    def body(dy_ref, wb_ref, c_ref, lt_ref, e_ref, *rest):
        outs, scr = rest[:3 * N_GROUPS], rest[3 * N_GROUPS:]
        dcb = _dot_nt(dy_ref[...], wb_ref[...]).astype(BF16)
        dc = dcb.astype(F32)
        delta = lax.dot_general(dc * c_ref[...], e_ref[...], (((1,), (1,)), ((), ())),
                                preferred_element_type=F32, precision=lax.Precision.HIGHEST)
        for k, (val, dtype) in enumerate(zip((dc, lt_ref[...], delta), dtypes)):
            outs[k][0] = val.astype(dtype)
            _to_residue(val, [outs[3 * (1 + j) + k] for j in range(len(DILS))], DILS, tm, dtype,
                        scr[:val.shape[1] // LANES])

    out_specs, out_shape = [], []
    for _, d in GROUPS:
        out_specs += [_res_spec(d, tm, w) for w in widths]
        out_shape += [jax.ShapeDtypeStruct((d, S // d, w), t) for w, t in zip(widths, dtypes)]
    res = pl.pallas_call(
        body, grid=(S // tm,),
        in_specs=[pl.BlockSpec((tm, D_MODEL), lambda i: (i, 1)), _resident(w_b.shape),
                  pl.BlockSpec((tm, GROUP_W), lambda i: (i, 0)), pl.BlockSpec((tm, LANES), lambda i: (i, 0)),
                  _resident((LANES, GROUP_W))],
        out_specs=out_specs, out_shape=out_shape, scratch_shapes=_lane_scratch(tm, GROUP_W),
        name="comb_bwd", compiler_params=_cparams(("parallel",), 32))(*_hbm(dyab, w_b, comb, lse_tot, _expand_heads()))
    return [tuple(res[3 * g:3 * g + 3]) for g in range(N_GROUPS)]


def _attn_bwd(qkv, col0, g, dcomb, lse_tot, delta):
    dil, sub, _ = qkv.shape
    nb = sub // TB
    heads = HEADS_PER_GROUP

    def body(q_ref, kp, kc, kn, vp, vc, vn, do_ref, lse_ref, dl_ref, bias_ref, dq_ref, dk_ref, dv_ref,
             ak, av, dqt_scr, s_scr, dp_scr, ds_scr, p_scr):
        i = pl.program_id(1)

        @pl.when(i == 0)
        def _():
            ak[...] = jnp.zeros_like(ak)
            av[...] = jnp.zeros_like(av)

        @pl.when(i < nb)
        def _():
            kext = _ext_window(kp, kc, kn)
            vext = _ext_window(vp, vc, vn)
            q = q_ref[...] * ATT_SCALE
            do = do_ref[...]
            lse_t, dl_t = lse_ref[...].T, dl_ref[...].T
            for b in range(SUBS):
                rows = slice(b * TQ, (b + 1) * TQ)
                kwin, vwin = kext[b * TQ:(b + 2) * TQ, :], vext[b * TQ:(b + 2) * TQ, :]
                for h in range(heads):
                    s_scr[b * heads + h] = _dot_nt(_pair(kwin, h), _own_lanes(_pair(q[rows], h), h))
                    dp_scr[b * heads + h] = _dot_nt(_pair(vwin, h), _own_lanes(_pair(do[rows], h), h))
            for b in range(SUBS):
                cols = slice(b * TQ, (b + 1) * TQ)
                block = i * SUBS + b
                bias = bias_ref[jnp.where(block == 0, 1, 0) + jnp.where(block == nb * SUBS - 1, 2, 0)]
                for h in range(heads):
                    k = b * heads + h
                    p = jnp.exp(s_scr[k] + _slope(g, h) * bias - lse_t[h:h + 1, cols])
                    ds_scr[k] = (p * (dp_scr[k] - dl_t[h:h + 1, cols])).astype(BF16)
                    p_scr[k] = p.astype(BF16)
            for b in range(SUBS):
                kwin = kext[b * TQ:(b + 2) * TQ, :]
                for h in range(heads):
                    dqt_scr[h * HEAD_DIM:(h + 1) * HEAD_DIM, b * TQ:(b + 1) * TQ] = _own_rows(
                        _dot_tn(_pair(kwin, h), ds_scr[b * heads + h]), h)
            for b in range(SUBS):
                rows = slice(b * TQ, (b + 1) * TQ)
                acc_rows = slice(TB - RADIUS + b * TQ, TB - RADIUS + (b + 2) * TQ)
                for h in range(0, heads, 2):
                    cols = slice(h * HEAD_DIM, (h + 2) * HEAD_DIM)
                    k = b * heads + h
                    q2 = jnp.concatenate([_own_lanes(_pair(q[rows], h), h), _own_lanes(_pair(q[rows], h), h + 1)], axis=0)
                    do2 = jnp.concatenate([_own_lanes(_pair(do[rows], h), h), _own_lanes(_pair(do[rows], h), h + 1)],
                                          axis=0)
                    ak[acc_rows, cols] += _dot(jnp.concatenate([ds_scr[k], ds_scr[k + 1]], axis=1), q2)
                    av[acc_rows, cols] += _dot(jnp.concatenate([p_scr[k], p_scr[k + 1]], axis=1), do2)
            dq_ref[...] = (dqt_scr[...].T * ATT_SCALE).astype(BF16)

        if nb == 1:
            dk_ref[...] = ak[TB:2 * TB, :].astype(BF16)
            dv_ref[...] = av[TB:2 * TB, :].astype(BF16)
        else:
            dk_ref[...] = ak[0:TB, :].astype(BF16)
            dv_ref[...] = av[0:TB, :].astype(BF16)
            used = 2 * TB + RADIUS
            for acc in (ak, av):
                acc[0:used - TB, :] = acc[TB:used, :]
                acc[used - TB:used, :] = jnp.zeros((TB, GROUP_W), F32)

    def spec(col, shift):
        return pl.BlockSpec((None, TB, GROUP_W), lambda r, i: (r, jnp.clip(i + shift, 0, nb - 1), col))

    tok = pl.BlockSpec((None, TB, GROUP_W), lambda r, i: (r, jnp.minimum(i, nb - 1), 0))
    stat = pl.BlockSpec((None, TB, LANES), lambda r, i: (r, jnp.minimum(i, nb - 1), 0))
    dkv_spec = tok if nb == 1 else pl.BlockSpec((None, TB, GROUP_W), lambda r, i: (r, jnp.maximum(i - 1, 0), 0))
    return pl.pallas_call(
        body, grid=(dil, nb + (nb > 1)),
        in_specs=[spec(col0, 0), spec(col0 + 1, -1), spec(col0 + 1, 0), spec(col0 + 1, 1),
                  spec(col0 + 2, -1), spec(col0 + 2, 0), spec(col0 + 2, 1), tok, stat, stat,
                  pl.BlockSpec((4, 2 * TQ, TQ), lambda r, i: (0, 0, 0))],
        out_specs=[tok, dkv_spec, dkv_spec], out_shape=[jax.ShapeDtypeStruct((dil, sub, GROUP_W), BF16)] * 3,
        scratch_shapes=[pltpu.VMEM((3 * TB, GROUP_W), F32)] * 2 + [pltpu.VMEM((GROUP_W, TB), F32)]
        + [pltpu.VMEM((SUBS * heads, 2 * TQ, TQ), F32)] * 2 + [pltpu.VMEM((SUBS * heads, 2 * TQ, TQ), BF16)] * 2,
        name=f"attn_bwd_g{g}", compiler_params=_cparams(("arbitrary", "arbitrary"), 40))(
            *_hbm(*([qkv] * 7), dcomb, lse_tot, delta, _attn_bias_table(g)))


def _in_bwd_ln0(dgated, dqkv, w_nat, w_dil, dz1, x, g0, *, tm=256):
    S = x.shape[0]
    n_gated, n_in = len(dgated), 3 * N_GROUPS

    def body(*refs):
        g_refs, d_refs = refs[:n_gated], refs[n_gated:n_gated + n_in]
        wn_ref, *wd_refs = refs[n_gated + n_in:n_gated + n_in + N_GROUPS]
        dz_ref, x_ref, g_ref, gx_ref, st_ref, *tmp_ref = refs[n_gated + n_in + N_GROUPS:]
        dh = ALPHA * dz_ref[...]
        col = 0
        for ref in g_refs:
            for k in range(ref.shape[0]):
                dh = dh + _dot_nt(ref[k], wn_ref[:, col:col + D_MODEL])
                col += D_MODEL
        for g, (_, d) in enumerate(GROUPS):
            rows = [jnp.concatenate([d_refs[3 * g + k][r] for k in range(3)], axis=1) for r in range(d)]
            w = wn_ref[:, col:col + QKV_W] if d == 1 else wd_refs[g - 1][...]
            res = _dot_nt(jnp.concatenate(rows, axis=0), w)
            if d == 1:
                dh = dh + res
            else:
                n = tm // d
                dh = dh + _from_residue(lambda r: res[r * n:(r + 1) * n, :], d, tm, tmp_ref)
        xhat, rstd = _ln_stats(x_ref[...])
        gx_ref[...] = _ln_bwd(dh, xhat, rstd, g_ref[...])
        upd = _rows8([jnp.sum(dh * xhat, axis=0, keepdims=True), jnp.sum(dh, axis=0, keepdims=True)], D_MODEL)

        @pl.when(pl.program_id(0) == 0)
        def _():
            st_ref[...] = upd

        @pl.when(pl.program_id(0) != 0)
        def _():
            st_ref[...] += upd

    row = pl.BlockSpec((tm, D_MODEL), lambda i: (i, 0))
    g_specs = [pl.BlockSpec((a.shape[0], tm, D_MODEL), lambda i: (0, i, 0)) for a in dgated]
    d_specs = []
    for _, d in GROUPS:
        d_specs += [_res_spec(d, tm, GROUP_W)] * 3
    operands = list(dgated) + [a for grp in dqkv for a in grp] + [w_nat] + list(w_dil) + [dz1, x, g0]
    return pl.pallas_call(
        body, grid=(S // tm,),
        in_specs=g_specs + d_specs + [_resident(w_nat.shape)] + [_resident(w.shape) for w in w_dil]
        + [row, row, pl.BlockSpec((1, D_MODEL), lambda i: (0, 0))],
        out_specs=[row, pl.BlockSpec((SUBLANES, D_MODEL), lambda i: (0, 0))],
        out_shape=[jax.ShapeDtypeStruct((S, D_MODEL), F32), jax.ShapeDtypeStruct((SUBLANES, D_MODEL), F32)],
        scratch_shapes=_lane_scratch(tm, D_MODEL),
        name="in_bwd_ln0", compiler_params=_cparams(("arbitrary",), 52))(*_hbm(*operands))


HBM_SPEC = pl.BlockSpec(memory_space=pltpu.HBM)


def _place():
    x, y, c = lax.axis_index("x"), lax.axis_index("y"), lax.axis_index("c")
    chips = [(1 - x, y), (x, 1 - y), (1 - x, 1 - y)]
    return x, y, c, chips


def _allgather_shards(shards, after, *, name, collective_id):
    n = len(shards)
    per = 6

    def body(*refs):
        ins, outs = refs[:n], refs[n + len(after):2 * n + len(after)]
        send_sems, recv_sems, loc_sems = refs[2 * n + len(after):]
        x, y, c, chips = _place()
        me = 2 * x + y
        sib = (x, y, 1 - c)
        peers = [sib] + [(px, py, c) for px, py in chips]
        barrier = pltpu.get_barrier_semaphore()
        for peer in peers:
            pl.semaphore_signal(barrier, inc=1, device_id=peer, device_id_type=MESH)
        pl.semaphore_wait(barrier, len(peers))

        def rcopy(w, k, src, dst, to):
            return pltpu.make_async_remote_copy(src_ref=src, dst_ref=dst, send_sem=send_sems.at[per * w + k],
                                                recv_sem=recv_sems.at[per * w + k], device_id=to, device_id_type=MESH)

        split = [s.shape[0] == N_CORES for s in shards]
        half = lambda w: c if split[w] else 0
        local, sends = [], []
        for w in range(n):
            cp = pltpu.make_async_copy(ins[w], outs[w].at[me], loc_sems.at[w])
            cp.start()
            local.append(cp)
            for j, (px, py) in enumerate(chips):
                cp = rcopy(w, j, ins[w].at[half(w)], outs[w].at[me, half(w)], (px, py, c))
                cp.start()
                sends.append(cp)
        for w in range(n):
            for j, (px, py) in enumerate(chips):
                slot = outs[w].at[2 * px + py, half(w)]
                rcopy(w, j, slot, slot, (px, py, c)).wait_recv()
                if split[w]:
                    cp = rcopy(w, 3 + j, slot, slot, sib)
                    cp.start()
                    sends.append(cp)
        for w in range(n):
            if split[w]:
                for j, (px, py) in enumerate(chips):
                    slot = outs[w].at[2 * px + py, 1 - c]
                    rcopy(w, 3 + j, slot, slot, sib).wait_recv()
        for cp in sends:
            cp.wait_send()
        for cp in local:
            cp.wait()

    return pl.kernel(
        body, out_type=[jax.ShapeDtypeStruct((N_CHIPS,) + s.shape, s.dtype) for s in shards],
        mesh=plsc.ScalarSubcoreMesh(axis_name="sequencer", num_cores=1),
        scratch_types=[pltpu.SemaphoreType.DMA((per * n,)), pltpu.SemaphoreType.DMA((per * n,)),
                       pltpu.SemaphoreType.DMA((n,))],
        name=name, compiler_params=pltpu.CompilerParams(collective_id=collective_id))(*shards, *after)


def _exchange_grads(grads, *, name, collective_id):
    n = len(grads)
    per = 7

    def body(*refs):
        ins, outs = refs[:n], refs[n:2 * n]
        send_sems, recv_sems, loc_sems = refs[2 * n:]
        x, y, c, chips = _place()
        me = 2 * x + y
        sib = (x, y, 1 - c)
        peers = [sib] + [(px, py, c) for px, py in chips]
        barrier = pltpu.get_barrier_semaphore()
        for peer in peers:
            pl.semaphore_signal(barrier, inc=1, device_id=peer, device_id_type=MESH)
        pl.semaphore_wait(barrier, len(peers))

        def rcopy(w, k, src, dst, to):
            return pltpu.make_async_remote_copy(src_ref=src, dst_ref=dst, send_sem=send_sems.at[per * w + k],
                                                recv_sem=recv_sems.at[per * w + k], device_id=to, device_id_type=MESH)

        local, sends = [], []
        for w in range(n):
            cp = pltpu.make_async_copy(ins[w].at[me], outs[w].at[c, me], loc_sems.at[w])
            cp.start()
            local.append(cp)
            cp = rcopy(w, 0, ins[w].at[me], outs[w].at[c, me], sib)
            cp.start()
            sends.append(cp)
            for j, (px, py) in enumerate(chips):
                cp = rcopy(w, 1 + j, ins[w].at[2 * px + py], outs[w].at[c, me], (px, py, c))
                cp.start()
                sends.append(cp)
        for w in range(n):
            for j, (px, py) in enumerate(chips):
                slot = outs[w].at[c, 2 * px + py]
                rcopy(w, 1 + j, slot, slot, (px, py, c)).wait_recv()
                cp = rcopy(w, 4 + j, slot, slot, sib)
                cp.start()
                sends.append(cp)
        for w in range(n):
            slot = outs[w].at[1 - c, me]
            rcopy(w, 0, slot, slot, sib).wait_recv()
            for j, (px, py) in enumerate(chips):
                slot = outs[w].at[1 - c, 2 * px + py]
                rcopy(w, 4 + j, slot, slot, sib).wait_recv()
        for cp in sends:
            cp.wait_send()
        for cp in local:
            cp.wait()

    return pl.kernel(
        body, out_type=[jax.ShapeDtypeStruct((N_CORES,) + g.shape, g.dtype) for g in grads],
        mesh=plsc.ScalarSubcoreMesh(axis_name="sequencer", num_cores=1),
        scratch_types=[pltpu.SemaphoreType.DMA((per * n,)), pltpu.SemaphoreType.DMA((per * n,)),
                       pltpu.SemaphoreType.DMA((n,))],
        name=name, compiler_params=pltpu.CompilerParams(collective_id=collective_id))(*grads)


def _allgather_small(vec, after):
    def body(v_ref, _, o_ref, send_sems, recv_sems, loc_sem):
        x, y, c = lax.axis_index("x"), lax.axis_index("y"), lax.axis_index("c")
        me = 4 * x + 2 * y + c

        def peer(k):
            flip = lambda v, bit: 1 - v if (k >> bit) & 1 else v
            return flip(x, 2), flip(y, 1), flip(c, 0)

        loc = pltpu.make_async_copy(v_ref, o_ref.at[me], loc_sem)
        loc.start()
        sends = []
        for k in range(1, N_DEV):
            cp = pltpu.make_async_remote_copy(src_ref=v_ref, dst_ref=o_ref.at[me], send_sem=send_sems.at[k - 1],
                                              recv_sem=recv_sems.at[k - 1], device_id=peer(k), device_id_type=MESH)
            cp.start()
            sends.append(cp)
        for k in range(1, N_DEV):
            px, py, pc = peer(k)
            pltpu.make_async_remote_copy(src_ref=v_ref, dst_ref=o_ref.at[4 * px + 2 * py + pc],
                                         send_sem=send_sems.at[k - 1], recv_sem=recv_sems.at[k - 1],
                                         device_id=(px, py, pc), device_id_type=MESH).wait_recv()
        for cp in sends:
            cp.wait_send()
        loc.wait()

    return pl.pallas_call(
        body, in_specs=[HBM_SPEC, HBM_SPEC], out_specs=HBM_SPEC,
        out_shape=jax.ShapeDtypeStruct((N_DEV,) + vec.shape, vec.dtype),
        scratch_shapes=[pltpu.SemaphoreType.DMA((N_DEV - 1,)), pltpu.SemaphoreType.DMA((N_DEV - 1,)),
                        pltpu.SemaphoreType.DMA],
        name="allgather_small")(vec, after)


def _adamw(w, g, m, v):
    m = ADAM_B1 * m + (1.0 - ADAM_B1) * g
    v = ADAM_B2 * v + (1.0 - ADAM_B2) * (g * g)
    m_hat = m / (1.0 - ADAM_B1 ** ADAM_STEP)
    v_hat = v / (1.0 - ADAM_B2 ** ADAM_STEP)
    delta = -ADAM_LR * (m_hat / (jnp.sqrt(v_hat) + ADAM_EPS) + ADAM_WD * w)
    return delta, m, v


def _reduce_adamw(parts, w, m, v, *, tr, name):
    R, C = w.shape

    def body(p_ref, w_ref, m_ref, v_ref, g_ref, d_ref, nm_ref, nv_ref):
        def core_sum(cc):
            s = p_ref[cc, 0].astype(F32)
            for k in range(1, N_CHIPS):
                s = s + p_ref[cc, k].astype(F32)
            return s

        g = core_sum(0) + core_sum(1)
        delta, nm, nv = _adamw(w_ref[...], g, m_ref[...], v_ref[...])
        g_ref[...] = g
        d_ref[...] = delta
        nm_ref[...] = nm
        nv_ref[...] = nv

    blk = pl.BlockSpec((tr, C), lambda i: (i, 0))
    return pl.pallas_call(
        body, grid=(R // tr,),
        in_specs=[pl.BlockSpec((N_CORES, N_CHIPS, tr, C), lambda i: (0, 0, i, 0)), blk, blk, blk],
        out_specs=[blk] * 4, out_shape=[jax.ShapeDtypeStruct((R, C), F32)] * 4,
        name=name, compiler_params=_cparams(("parallel",), 40))(*_hbm(parts, w, m, v))


def _reduce_adamw_vectors(allv, offs, ws, ms, vs):
    n = len(ws)

    def body(a_ref, *refs):
        w_refs, m_refs, v_refs = refs[:n], refs[n:2 * n], refs[2 * n:3 * n]
        tot_ref, outs = refs[3 * n], refs[3 * n + 1:]
        s = a_ref[0]
        for d in range(1, N_DEV):
            s = s + a_ref[d]
        tot_ref[...] = s
        for k in range(n):
            g = s[:, offs[k]:offs[k] + w_refs[k].shape[1]]
            delta, nm, nv = _adamw(w_refs[k][...], g, m_refs[k][...], v_refs[k][...])
            for ref, val in zip(outs[4 * k:4 * k + 4], (g, delta, nm, nv)):
                ref[...] = val

    out_shape = [jax.ShapeDtypeStruct(allv.shape[1:], F32)]
    for w in ws:
        out_shape += [jax.ShapeDtypeStruct(w.shape, F32)] * 4
    res = pl.pallas_call(body, out_shape=out_shape, name="reduce_adamw_vectors",
                         compiler_params=_cparams((), 40))(allv, *ws, *ms, *vs)
    return res[0], [tuple(res[1 + 4 * k:5 + 4 * k]) for k in range(n)]


def _adamw_taps(ws, gs, ms, vs):
    n = len(ws)

    def body(*refs):
        outs = refs[4 * n:]
        for k in range(n):
            res = _adamw(refs[k][...], refs[n + k][...], refs[2 * n + k][...], refs[3 * n + k][...])
            for ref, val in zip(outs[3 * k:3 * k + 3], res):
                ref[...] = val

    out_shape = []
    for w in ws:
        out_shape += [jax.ShapeDtypeStruct(w.shape, F32)] * 3
    res = pl.pallas_call(body, out_shape=out_shape, name="adamw_taps")(*ws, *gs, *ms, *vs)
    return [tuple(res[3 * k:3 * k + 3]) for k in range(n)]


def _pack(pieces):
    flat, offs, n = [], [], 0
    for p in pieces:
        size = -(-p.size // LANES) * LANES
        flat.append(jnp.pad(p.reshape(-1), (0, size - p.size)))
        offs.append(n)
        n += size
    return jnp.concatenate(flat).reshape(1, n), offs


def _local_step(x, target, p, wfull, on_ready=lambda group: None, before_ln0=()):
    S = x.shape[0]
    dils = [d for _, d in GROUPS]

    h0, h0b, *h0_res = _ln0_fwd(x, p["ln0_g"], p["ln0_b"], before_ln0)
    h0_rows = [h0b] + [h.reshape(S, D_MODEL) for h in h0_res]

    if isinstance(wfull, dict):
        w_in3, pending = wfull["w_in"], None
    else:
        w_in3, launch_rest, assemble = wfull
        w_in3, h0b = lax.optimization_barrier((w_in3, h0b))
        pending = launch_rest(h0b)

    runs = _col_runs()
    w_perm = jnp.concatenate([w_in3[s, :, c:c + w] for s, c, _, w in runs], axis=1)
    b_blocks = p["b_in"].reshape(N_BLK, GROUP_W)
    b_perm = jnp.concatenate([b_blocks[b] for b in PERM]).reshape(1, N_IN)
    w_nat, b_nat = w_perm[:, :N_NAT], b_perm[:, :N_NAT]
    qkv_cols = [slice(P_Q0 + g * QKV_W, P_Q0 + (g + 1) * QKV_W) for g in range(N_GROUPS)]
    w_qkv = [w_perm[:, c] for c in qkv_cols]

    proj = _mm_nn(h0b, w_nat, b_nat, tm=512, tn=N_NAT // 2, out_dtype=BF16, name="proj")
    qkv = [proj[None]]
    for g in range(1, N_GROUPS):
        t = _mm_nn(h0_rows[g], w_qkv[g], b_perm[:, qkv_cols[g]], tm=512, tn=QKV_W, out_dtype=BF16, name=f"proj_qkv{g}")
        qkv.append(t.reshape(dils[g], S // dils[g], QKV_W))
    if pending is not None:
        pending, qkv = lax.optimization_barrier((pending, qkv))
        proj = qkv[0][0]
        wfull = assemble(pending)
    w_up3 = wfull["w_up"]
    w_a, w_o, w_down, w_b = wfull["w_a"], wfull["w_o"], wfull["w_down"], wfull["w_b"]
    conv_w, ffn_conv_w = wfull["conv_w"], wfull["ffn_conv_w"]
    col0 = [P_Q0 // GROUP_W] + [0] * (N_GROUPS - 1)
    ya_in = _conv_gate_fwd(proj, conv_w)
    att = [_attn_fwd(qkv[g], col0[g], g) for g in range(N_GROUPS)]
    comb, comb_b, lse_tot = _attn_combine([a[0] for a in att], [a[1] for a in att])
    yab, mixin = _branch_mix(ya_in, comb_b, w_a, w_b, proj)
    xhat1, rstd1, h1b = _mix_ln1(mixin, w_o, p["b_o"], h0, p["ln1_g"], p["ln1_b"])
    up = _mm_nn(h1b, w_up3, p["b_up"], tm=512, tn=w_up3.shape[2], out_dtype=BF16, name="up")
    f = _ffn_conv_fwd(up, ffn_conv_w, p["ffn_conv_b"])
    dz2, dz2b, st2 = _down_ln2_loss(f, w_down, p["b_down"], xhat1, p["ln1_g"], p["ln1_b"],
                                    p["ln2_g"], p["ln2_b"], target)

    gw = {}
    gw["w_down"] = _mm_tn(f, dz2b, n_out=1, tn=D_MODEL, ts=1024, g_block=(1024, D_MODEL),
                          g_map=lambda j, s: (s, 0), name="grad_w_down").reshape(N_CHIPS, D_FF // N_CHIPS, D_MODEL)
    df = _mm_nt(dz2b, w_down, tm=512, name="df")
    dup, sm_ffn = _ffn_conv_bwd(up, df, ffn_conv_w, p["ffn_conv_b"])
    up_tn = w_up3.shape[2]
    up_pp = D_FF // up_tn
    gw["w_up"] = _mm_tn(h1b, dup, n_out=N_CHIPS, tn=up_tn, ts=1024, g_block=(None, 1024, up_tn),
                        g_map=lambda j, s: (j // up_pp, s, j % up_pp), name="grad_w_up")
    exchanged = on_ready({n: gw[n] for n in ("w_down", "w_up")}) or {}
    dz1, dz1b, st1 = _up_bwd_ln1(dup, w_up3, dz2, xhat1, rstd1, p["ln1_g"])

    gw["w_o"] = _mm_tn(mixin, dz1b, n_out=1, tn=D_MODEL, ts=512, g_block=(512, D_MODEL),
                       g_map=lambda j, s: (s, 0), name="grad_w_o").reshape(N_CHIPS, D_MODEL // N_CHIPS, D_MODEL)
    dyab, dgab = _mix_bwd(dz1b, w_o, proj, yab)
    gw["w_a"] =_mm_tn(ya_in, dyab, n_out=1, tn=D_MODEL, ts=512, g_block=(512, D_MODEL),
                       g_map=lambda j, s: (s, 0), name="grad_w_a").reshape(N_CHIPS, D_CONV // N_CHIPS, D_MODEL)
    gw_b = _mm_tn(comb_b, dyab, n_out=1, tn=D_MODEL, ts=1024, g_block=(1024, D_MODEL),
                  g_map=lambda j, s: (s, 1), name="grad_w_b")
    gw["w_b"] = gw_b.reshape(GROUP_W, N_CHIPS, D_MODEL // N_CHIPS).transpose(1, 0, 2)
    exchanged_mix = on_ready({n: gw[n] for n in ("w_o", "w_a", "w_b")}) or {}
    dya_in = _mm_nt(dyab, w_a, tm=512, a_col=0, name="dya_in")
    exchanged, dya_in = lax.optimization_barrier((exchanged, dya_in))
    dbch, sm_conv = _conv_gate_bwd(proj, dya_in, conv_w)
    att_stats = _comb_bwd(dyab, w_b, comb, lse_tot)
    exchanged_mix, att_stats = lax.optimization_barrier((exchanged_mix, att_stats))
    exchanged.update(exchanged_mix)
    dqkv = [_attn_bwd(qkv[g], col0[g], g, *att_stats[g]) for g in range(N_GROUPS)]

    w_pieces, b_pieces = [], []
    for nm, planes in (("bch", dbch), ("gab", dgab)):
        pw, pc = _mm_tn(h0b, planes, n_out=planes.shape[0], tn=D_MODEL, ts=1024, g_block=(None, 1024, D_MODEL),
                        g_map=lambda j, s: (j, s, 0), colsum=True, name="grad_w_in_" + nm)
        w_pieces.extend(pw[k] for k in range(planes.shape[0]))
        b_pieces.append(pc[0])
    for g in range(N_GROUPS):
        pw, pc = _mm_tn_cat(h0_rows[g], [a.reshape(S, GROUP_W) for a in dqkv[g]], ts=1024, name=f"grad_w_in_qkv{g}")
        w_pieces.append(pw)
        b_pieces.append(pc[0])
    dw_perm = jnp.concatenate(w_pieces, axis=1)
    gw["w_in"] = jnp.stack([
        jnp.concatenate([dw_perm[:, pc:pc + w] for s, c, pc, w in sorted(runs, key=lambda r: r[1]) if s == k], axis=1)
        for k in range(N_CHIPS)])
    exchanged.update(on_ready({"w_in": gw["w_in"]}) or {})
    db_blocks = jnp.concatenate(b_pieces).reshape(N_BLK, GROUP_W)
    grad_b_in = jnp.concatenate([db_blocks[b] for b in INV_PERM])

    grad_x, st0 = _in_bwd_ln0([dbch, dgab], dqkv, w_nat, w_qkv[1:], dz1, x, p["ln0_g"])

    small = {
        "loss": st2[2:3, 0:1],
        "ln0_g": st0[0], "ln0_b": st0[1], "b_in": grad_b_in, "conv_w": sm_conv[0:3],
        "b_o": st1[2], "ln1_g": st1[0], "ln1_b": st1[1],
        "b_up": jnp.concatenate([sm_ffn[0], sm_ffn[1]]), "ffn_conv_w": sm_ffn[3:6], "ffn_conv_b": sm_ffn[2],
        "b_down": st2[3], "ln2_g": st2[0], "ln2_b": st2[1],
    }
    return grad_x, exchanged or gw, small


BIG =("w_in", "w_a", "w_b", "w_o", "w_up", "w_down")
CONV = ("conv_w", "ffn_conv_w")
VECS = ("ln0_g", "ln0_b", "b_in", "b_o", "ln1_g", "ln1_b", "b_up", "ffn_conv_b", "b_down", "ln2_g", "ln2_b")
ORDER = ("ln0_g", "ln0_b", "w_in", "b_in", "conv_w", "w_a", "w_b", "w_o", "b_o", "ln1_g", "ln1_b", "w_up", "b_up",
         "ffn_conv_w", "ffn_conv_b", "w_down", "b_down", "ln2_g", "ln2_b")
SMALL_ORDER = ("loss",) + VECS + CONV


def _step(x, target, W, Mo, Vo):
    x2, t2 = x[0], target[0]
    big2 = {n: W[n][0] for n in BIG}
    halves = lambda a: a.astype(BF16).reshape(N_CORES, a.shape[0] // N_CORES, a.shape[1])
    whole = lambda g: g.reshape(N_CHIPS, g.shape[1] * g.shape[2], g.shape[3])
    later = tuple(n for n in BIG if n != "w_in")
    w_in_halves = halves(big2["w_in"])
    first = _allgather_shards([w_in_halves], [], name="allgather_w_in", collective_id=1)

    def launch_rest(h0b):
        return _allgather_shards([halves(big2[n]) for n in later] + [W[n] for n in CONV], [h0b],
                                 name="allgather_rest", collective_id=2)

    def assemble(rest):
        gathered = {n: whole(g) for n, g in zip(later + CONV, rest)}
        return {
            "w_up": gathered["w_up"],
            "w_a": gathered["w_a"].reshape(D_CONV, D_MODEL), "w_o": gathered["w_o"].reshape(D_MODEL, D_MODEL),
            "w_down": gathered["w_down"].reshape(D_FF, D_MODEL),
            "w_b": gathered["w_b"].transpose(1, 0, 2).reshape(GROUP_W, D_MODEL),
            "conv_w": gathered["conv_w"].transpose(1, 0, 2).reshape(3, D_CONV),
            "ffn_conv_w": gathered["ffn_conv_w"].transpose(1, 0, 2).reshape(3, D_FF),
        }

    pvec = {n: W[n].reshape(1, -1) for n in VECS}

    exchange_ids = iter((3, 4, 5))

    def exchange(group):
        names = tuple(group)
        res = _exchange_grads([group[n] for n in names], name="exchange_" + "_".join(names),
                              collective_id=next(exchange_ids))
        return dict(zip(names, res))

    grad_x, parts, small = _local_step(x2, t2, pvec, (whole(first[0]), launch_rest, assemble), exchange,
                                       before_ln0=[w_in_halves])
    out = {}
    for n in BIG:
        tr = {"w_in": 128, "w_up": 128, "w_b": 128}.get(n, big2[n].shape[0] // 4)
        g, d, nm, nv = _reduce_adamw(parts[n], big2[n], Mo[n][0], Vo[n][0], tr=tr, name="adamw_" + n)
        out[n] = tuple(a[None] for a in (g, d, nm, nv))

    vec, offs = _pack([small[n] for n in SMALL_ORDER])
    off = dict(zip(SMALL_ORDER, offs))
    row = lambda a: a.reshape(1, -1)
    allv = _allgather_small(vec, parts["w_in"])
    tot, vec_out = _reduce_adamw_vectors(allv, [off[n] for n in VECS], [row(W[n]) for n in VECS],
                                         [row(Mo[n]) for n in VECS], [row(Vo[n]) for n in VECS])
    for n, res in zip(VECS, vec_out):
        out[n] = tuple(a.reshape(W[n].shape) for a in res)
    loss = tot[0, off["loss"]]
    chip = 2 * lax.axis_index("x") + lax.axis_index("y")
    taps_g = []
    for n in CONV:
        width = W[n].shape[2]
        full = lax.slice(tot, (0, off[n]), (1, off[n] + 3 * N_CHIPS * width)).reshape(3, N_CHIPS * width)
        taps_g.append(lax.dynamic_slice_in_dim(full, chip * width, width, axis=1))
    taps_out = _adamw_taps([W[n][0] for n in CONV], taps_g, [Mo[n][0] for n in CONV], [Vo[n][0] for n in CONV])
    for n, g, res in zip(CONV, taps_g, taps_out):
        out[n] = tuple(a[None] for a in (g,) + res)

    res = [loss, grad_x[None]]
    for k in range(4):
        res += [out[n][k] for n in ORDER]
    return tuple(res)


def kernel(x, ln0_g, ln0_b, w_in, b_in, conv_w, w_a, w_b, w_o, b_o, ln1_g, ln1_b, w_up, b_up, ffn_conv_w, ffn_conv_b, w_down, b_down, ln2_g, ln2_b, loss_target, m_ln0_g, m_ln0_b, m_w_in, m_b_in, m_conv_w, m_w_a, m_w_b, m_w_o, m_b_o, m_ln1_g, m_ln1_b, m_w_up, m_b_up, m_ffn_conv_w, m_ffn_conv_b, m_w_down, m_b_down, m_ln2_g, m_ln2_b, v_ln0_g, v_ln0_b, v_w_in, v_b_in, v_conv_w, v_w_a, v_w_b, v_w_o, v_b_o, v_ln1_g, v_ln1_b, v_w_up, v_b_up, v_ffn_conv_w, v_ffn_conv_b, v_w_down, v_b_down, v_ln2_g, v_ln2_b):
    W = dict(zip(ORDER, (ln0_g, ln0_b, w_in, b_in, conv_w, w_a, w_b, w_o, b_o, ln1_g, ln1_b, w_up, b_up,
                         ffn_conv_w, ffn_conv_b, w_down, b_down, ln2_g, ln2_b)))
    Mo = dict(zip(ORDER, (m_ln0_g, m_ln0_b, m_w_in, m_b_in, m_conv_w, m_w_a, m_w_b, m_w_o, m_b_o, m_ln1_g, m_ln1_b,
                          m_w_up, m_b_up, m_ffn_conv_w, m_ffn_conv_b, m_w_down, m_b_down, m_ln2_g, m_ln2_b)))
    Vo = dict(zip(ORDER, (v_ln0_g, v_ln0_b, v_w_in, v_b_in, v_conv_w, v_w_a, v_w_b, v_w_o, v_b_o, v_ln1_g, v_ln1_b,
                          v_w_up, v_b_up, v_ffn_conv_w, v_ffn_conv_b, v_w_down, v_b_down, v_ln2_g, v_ln2_b)))
    return _step(x, loss_target, W, Mo, Vo)
```

```python
import functools
import math

import jax
import jax.numpy as jnp
from jax import lax
from jax.experimental import pallas as pl
from jax.experimental.pallas import tpu as pltpu
from jax.experimental.pallas import tpu_sc as plsc

F32 = jnp.float32
BF16 = jnp.bfloat16

D_MODEL = 1024
D_CONV = D_MODEL
HEAD_DIM = 64
HEADS_PER_GROUP = 8
GROUPS = ((128, 1), (512, 4), (2048, 16))
N_GROUPS = len(GROUPS)
GROUP_W = HEADS_PER_GROUP * HEAD_DIM
QKV_W = N_GROUPS * GROUP_W
RADIUS = 64
D_FF = 2816
LN_EPS = 1e-5
ALPHA = 2.0 ** 0.25
MASK_VALUE = -1e30
ATT_SCALE = HEAD_DIM ** -0.5
OFF_B = 0
OFF_C = OFF_B + D_CONV
OFF_H = OFF_C + D_CONV
OFF_Q = OFF_H + D_CONV
OFF_K = OFF_Q + QKV_W
OFF_V = OFF_K + QKV_W
OFF_GA = OFF_V + QKV_W
OFF_GB = OFF_GA + D_MODEL
N_IN = OFF_GB + D_MODEL
ADAM_LR = 0.001
ADAM_B1 = 0.9
ADAM_B2 = 0.999
ADAM_EPS = 1e-08
ADAM_WD = 0.01
ADAM_STEP = 10
INV_SQRT2 = 0.7071067811865476
INV_SQRT_2PI = 0.3989422804014327

LANES = 128
SUBLANES = 8
VMEM_BYTES_V7X = 64 * 1024 * 1024
N_CHIPS = 4
N_CORES = 2
N_DEV = N_CHIPS * N_CORES
MESH = pl.DeviceIdType.MESH

N_BLK = N_IN // GROUP_W
PERM = (0, 1, 2, 3, 4, 5, 15, 16, 17, 18, 6, 9, 12, 7, 10, 13, 8, 11, 14)
INV_PERM = tuple(PERM.index(b) for b in range(N_BLK))
P_B, P_C, P_H, P_GA, P_GB, P_Q0 = 0, 1024, 2048, 3072, 4096, 5120
N_NAT = P_Q0 + QKV_W // N_GROUPS * 3
N_GATED = P_Q0

def _col_runs():
    shard_w = N_IN // N_CHIPS
    runs = []
    for pos, blk in enumerate(PERM):
        c, end = blk * GROUP_W, (blk + 1) * GROUP_W
        while c < end:
            stop = min(end, (c // shard_w + 1) * shard_w)
            runs.append((c // shard_w, c % shard_w, pos * GROUP_W + c - blk * GROUP_W, stop - c))
            c = stop
    return runs


SLAB = 128
CHUNK = 256
PAD = SUBLANES
TQ = 128


def _cparams(sem, vmem_mb):
    assert vmem_mb * 1024 * 1024 < VMEM_BYTES_V7X
    return pltpu.CompilerParams(dimension_semantics=sem, vmem_limit_bytes=vmem_mb * 1024 * 1024)


def _resident(shape):
    nd = len(shape)
    return pl.BlockSpec(shape, lambda *_: (0,) * nd, pipeline_mode=pl.Buffered(1))


def _hbm(*arrays):
    return [pltpu.with_memory_space_constraint(a, pltpu.HBM) for a in arrays]


def _dot(a, b):
    return jnp.dot(a, b, preferred_element_type=F32)


def _dot_nt(a, b):
    return lax.dot_general(a, b, (((1,), (1,)), ((), ())), preferred_element_type=F32)


def _dot_tn(a, b):
    return lax.dot_general(a, b, (((0,), (0,)), ((), ())), preferred_element_type=F32)


def _ln_stats(z):
    mu = jnp.mean(z, -1, keepdims=True)
    zc = z - mu
    var = jnp.mean(zc * zc, -1, keepdims=True)
    rstd = lax.rsqrt(var + LN_EPS)
    return zc * rstd, rstd


def _ln_bwd(dh, xhat, rstd, g):
    dxh = dh * g
    m1 = jnp.mean(dxh, -1, keepdims=True)
    m2 = jnp.mean(dxh * xhat, -1, keepdims=True)
    return rstd * (dxh - m1 - xhat * m2)


def _rows8(rows, width):
    pad = [jnp.zeros((1, width), F32)] * (SUBLANES - len(rows))
    return jnp.concatenate(list(rows) + pad, axis=0)


def _mm_nn(a, w, bias, *, tm, tn, out_dtype, name, vmem_mb=40):
    M, K = a.shape
    if w.ndim == 3:
        assert w.shape[2] == tn
        n_tiles = w.shape[0]
        w_spec = pl.BlockSpec((None, K, tn), lambda j, i: (j, 0, 0))
    else:
        n_tiles = w.shape[1] // tn
        w_spec = pl.BlockSpec((K, tn), lambda j, i: (0, j))

    def body(a_ref, w_ref, b_ref, o_ref):
        o_ref[...] = (_dot(a_ref[...], w_ref[...]) + b_ref[...]).astype(o_ref.dtype)

    return pl.pallas_call(
        body, grid=(n_tiles, M // tm),
        in_specs=[pl.BlockSpec((tm, K), lambda j, i: (i, 0)), w_spec, pl.BlockSpec((1, tn), lambda j, i: (0, j))],
        out_specs=pl.BlockSpec((tm, tn), lambda j, i: (i, j)),
        out_shape=jax.ShapeDtypeStruct((M, n_tiles * tn), out_dtype),
        name=name, compiler_params=_cparams(("arbitrary", "parallel"), vmem_mb))(*_hbm(a, w, bias))


def _mm_nt(a, w, *, tm, a_col=0, name, vmem_mb=40):
    M = a.shape[0]
    N, K = w.shape

    def body(a_ref, w_ref, o_ref):
        o_ref[...] = _dot_nt(a_ref[...], w_ref[...]).astype(o_ref.dtype)

    return pl.pallas_call(
        body, grid=(M // tm,),
        in_specs=[pl.BlockSpec((tm, K), lambda i: (i, a_col)),
                  pl.BlockSpec((N, K), lambda i: (0, 0))],
        out_specs=pl.BlockSpec((tm, N), lambda i: (i, 0)),
        out_shape=jax.ShapeDtypeStruct((M, N), BF16),
        name=name, compiler_params=_cparams(("parallel",), vmem_mb))(*_hbm(a, w))


def _mm_tn(a, g, *, n_out, tn, ts, g_block, g_map, colsum=False, name, vmem_mb=48):
    S, K = a.shape
    n_s = S // ts

    def body(a_ref, g_ref, *rest):
        if colsum:
            o_ref, cs_ref, acc_ref, cacc_ref = rest
        else:
            o_ref, acc_ref = rest
        s = pl.program_id(1)

        @pl.when(s == 0)
        def _():
            acc_ref[...] = jnp.zeros_like(acc_ref)
            if colsum:
                cacc_ref[...] = jnp.zeros_like(cacc_ref)

        gv = g_ref[...]
        acc_ref[...] += _dot_tn(a_ref[...], gv)
        if colsum:
            cacc_ref[...] += jnp.broadcast_to(jnp.sum(gv.astype(F32), axis=0, keepdims=True), cacc_ref.shape)

        @pl.when(s == n_s - 1)
        def _():
            o_ref[...] = acc_ref[...].astype(o_ref.dtype)
            if colsum:
                cs_ref[...] = cacc_ref[...]

    out_specs = [pl.BlockSpec((None, K, tn), lambda j, s: (j, 0, 0))]
    out_shape = [jax.ShapeDtypeStruct((n_out, K, tn), BF16)]
    scratch = [pltpu.VMEM((K, tn), F32)]
    if colsum:
        out_specs.append(pl.BlockSpec((SUBLANES, tn), lambda j, s: (0, j)))
        out_shape.append(jax.ShapeDtypeStruct((SUBLANES, n_out * tn), F32))
        scratch.append(pltpu.VMEM((SUBLANES, tn), F32))
    res = pl.pallas_call(
        body, grid=(n_out, n_s),
        in_specs=[pl.BlockSpec((ts, K), lambda j, s: (s, 0)), pl.BlockSpec(g_block, g_map)],
        out_specs=out_specs, out_shape=out_shape, scratch_shapes=scratch,
        name=name, compiler_params=_cparams(("parallel", "arbitrary"), vmem_mb))(*_hbm(a, g))
    return res if colsum else res[0]


def _mm_tn_cat(a, gs, *, ts, name, vmem_mb=40):
    S, K = a.shape
    widths = [g.shape[1] for g in gs]
    n_s, total = S // ts, sum(widths)

    def body(*refs):
        a_ref, g_refs = refs[0], refs[1:1 + len(gs)]
        o_ref, cs_ref, acc_ref, cacc_ref = refs[1 + len(gs):]
        s = pl.program_id(0)

        @pl.when(s == 0)
        def _():
            acc_ref[...] = jnp.zeros_like(acc_ref)
            cacc_ref[...] = jnp.zeros_like(cacc_ref)

        av, col = a_ref[...], 0
        for g_ref, w in zip(g_refs, widths):
            gv = g_ref[...]
            acc_ref[:, col:col + w] += _dot_tn(av, gv)
            cacc_ref[:, col:col + w] += jnp.broadcast_to(jnp.sum(gv.astype(F32), axis=0, keepdims=True), (SUBLANES, w))
            col += w

        @pl.when(s == n_s - 1)
        def _():
            o_ref[...] = acc_ref[...].astype(BF16)
            cs_ref[...] = cacc_ref[...]

    return pl.pallas_call(
        body, grid=(n_s,),
        in_specs=[pl.BlockSpec((ts, K), lambda s: (s, 0))] + [pl.BlockSpec((ts, w), lambda s: (s, 0)) for w in widths],
        out_specs=[pl.BlockSpec((K, total), lambda s: (0, 0)), pl.BlockSpec((SUBLANES, total), lambda s: (0, 0))],
        out_shape=[jax.ShapeDtypeStruct((K, total), BF16), jax.ShapeDtypeStruct((SUBLANES, total), F32)],
        scratch_shapes=[pltpu.VMEM((K, total), F32), pltpu.VMEM((SUBLANES, total), F32)],
        name=name, compiler_params=_cparams(("arbitrary",), vmem_mb))(*_hbm(a, *gs))


DILS = tuple(d for _, d in GROUPS if d > 1)


def _res_spec(d, tm, width):
    return pl.BlockSpec((d, tm // d, width), lambda i: (0, i, 0))


def _lane_scratch(tm, width):
    return [pltpu.VMEM((tm, LANES), F32)] * (width // LANES)


def _to_residue(val, dst_refs, dils, tm, dtype, scr):
    for c, ref in enumerate(scr):
        ref[...] = val[:, c * LANES:(c + 1) * LANES]
    for dst_ref, d in zip(dst_refs, dils):
        for r in range(d):
            cols = [ref[pl.ds(r, tm // d, stride=d), :] for ref in scr]
            dst_ref[r] = jnp.concatenate(cols, axis=1).astype(dtype)


def _from_residue(rows_of, d, tm, scr):
    for r in range(d):
        v = rows_of(r).astype(F32)
        for c, ref in enumerate(scr):
            ref[pl.ds(r, tm // d, stride=d), :] = v[:, c * LANES:(c + 1) * LANES]
    return jnp.concatenate([ref[...] for ref in scr], axis=1)


def _ln0_fwd(x, g, b, after=(), *, tm=512):
    S, Dm = x.shape
    n_after = len(after)

    def body(x_ref, g_ref, b_ref, *rest):
        h_ref, hb_ref, *rest = rest[n_after:]
        xhat, _ = _ln_stats(x_ref[...])
        h = xhat * g_ref[...] + b_ref[...]
        h_ref[...] = h
        hb_ref[...] = h.astype(BF16)
        _to_residue(h, rest[:len(DILS)], DILS, tm, BF16, rest[len(DILS):])

    row = pl.BlockSpec((tm, Dm), lambda i: (i, 0))
    vec = pl.BlockSpec((1, Dm), lambda i: (0, 0))
    return pl.pallas_call(
        body, grid=(S // tm,), in_specs=[row, vec, vec] + [pl.BlockSpec(memory_space=pl.ANY)] * n_after,
        out_specs=[row, row] + [_res_spec(d, tm, Dm) for d in DILS],
        out_shape=[jax.ShapeDtypeStruct((S, Dm), F32), jax.ShapeDtypeStruct((S, Dm), BF16)]
        + [jax.ShapeDtypeStruct((d, S // d, Dm), BF16) for d in DILS],
        scratch_shapes=_lane_scratch(tm, Dm),
        name="ln0_fwd", compiler_params=_cparams(("parallel",), 32))(*_hbm(x, g, b), *after)


def _slab_spec(S, col0):
    return pl.BlockSpec((S, SLAB), lambda j: (0, col0 // SLAB + j))


def _zero_pads(scr, S):
    scr[0:PAD, :] = jnp.zeros((PAD, SLAB), F32)
    scr[S + PAD:S + 2 * PAD, :] = jnp.zeros((PAD, SLAB), F32)


def _shifted(scr, t):
    return (scr[PAD - 1 + t:PAD - 1 + t + CHUNK, :], scr[PAD + t:PAD + t + CHUNK, :],
            scr[PAD + 1 + t:PAD + 1 + t + CHUNK, :])


def _tap_sums_add(acc_ref, d, shifted):
    db = d.astype(BF16)
    for j, s in enumerate(shifted):
        acc_ref[j] += _dot_tn(db, s.astype(BF16))


def _tap_sums(acc_ref):
    r = lax.broadcasted_iota(jnp.int32, (SLAB, SLAB), 0)
    c = lax.broadcasted_iota(jnp.int32, (SLAB, SLAB), 1)
    return [jnp.sum(jnp.where(r == c, acc_ref[j], 0.0), axis=0, keepdims=True) for j in range(acc_ref.shape[0])]


def _conv_gate_fwd(proj, conv_w):
    S = proj.shape[0]

    def body(b_ref, c_ref, h_ref, w_ref, o_ref, u_scr):
        _zero_pads(u_scr, S)
        for t in range(0, S, CHUNK):
            u_scr[PAD + t:PAD + t + CHUNK, :] = c_ref[t:t + CHUNK, :].astype(F32) * h_ref[t:t + CHUNK, :].astype(F32)
        w0, w1, w2 = w_ref[0:1, :], w_ref[1:2, :], w_ref[2:3, :]
        for t in range(0, S, CHUNK):
            um, u0, up = _shifted(u_scr, t)
            cv = w0 * um + w1 * u0 + w2 * up
            o_ref[t:t + CHUNK, :] = (b_ref[t:t + CHUNK, :].astype(F32) * cv).astype(BF16)

    return pl.pallas_call(
        body, grid=(D_CONV // SLAB,),
        in_specs=[_slab_spec(S, P_B), _slab_spec(S, P_C), _slab_spec(S, P_H),
                  pl.BlockSpec((3, SLAB), lambda j: (0, j))],
        out_specs=pl.BlockSpec((S, SLAB), lambda j: (0, j)),
        out_shape=jax.ShapeDtypeStruct((S, D_CONV), BF16),
        scratch_shapes=[pltpu.VMEM((S + 2 * PAD, SLAB), F32)],
        name="conv_gate_fwd", compiler_params=_cparams(("parallel",), 40))(*_hbm(proj, proj, proj, conv_w))


MASKED_DISTANCE = -1e34


def _attn_bias_table(g):
    dil = GROUPS[g][1]
    j = lax.broadcasted_iota(jnp.int32, (2 * TQ, TQ), 0)
    a = lax.broadcasted_iota(jnp.int32, (2 * TQ, TQ), 1)
    rel = jnp.abs(j - RADIUS - a)
    base = -(rel * dil).astype(F32)
    inside, after_start, before_end = rel <= RADIUS, j >= RADIUS, j < TQ + RADIUS
    variants = []
    for first, last in ((False, False), (True, False), (False, True), (True, True)):
        valid = inside & (after_start if first else True) & (before_end if last else True)
        variants.append(jnp.where(valid, base, MASKED_DISTANCE))
    return jnp.stack(variants)


SUBS = 4
TB = SUBS * TQ


def _ext_window(p_ref, c_ref, n_ref):
    return jnp.concatenate([p_ref[TB - RADIUS:, :], c_ref[...], n_ref[:RADIUS, :]], axis=0)


def _head_stats(rows):
    pad = jnp.zeros((LANES - len(rows), TQ), F32)
    return jnp.concatenate(list(rows) + [pad], axis=0).T


def _slope(g, h):
    return 2.0 ** (-8.0 * (g * HEADS_PER_GROUP + h + 1) / (N_GROUPS * HEADS_PER_GROUP))


def _pair(a, h):
    return a[:, (h // 2) * LANES:(h // 2 + 1) * LANES]


def _own_lanes(a, h):
    lane = lax.broadcasted_iota(jnp.int32, a.shape, 1)
    return jnp.where((lane >= HEAD_DIM) == (h % 2 == 1), a, jnp.zeros_like(a))


def _own_rows(a, h):
    return a[(h % 2) * HEAD_DIM:(h % 2 + 1) * HEAD_DIM, :]


def _attn_fwd(qkv, col0, g):
    dil, sub, _ = qkv.shape
    nb = sub // TB
    heads = HEADS_PER_GROUP

    def body(q_ref, kp, kc, kn, vp, vc, vn, bias_ref, o_ref, lse_ref, ot_scr, s_scr, p_scr):
        i = pl.program_id(1)
        kext = _ext_window(kp, kc, kn)
        vext = _ext_window(vp, vc, vn)
        q = q_ref[...] * ATT_SCALE
        for b in range(SUBS):
            kwin, qb = kext[b * TQ:(b + 2) * TQ, :], q[b * TQ:(b + 1) * TQ, :]
            for h in range(heads):
                s_scr[b * heads + h] = _dot_nt(_pair(kwin, h), _own_lanes(_pair(qb, h), h))
        inv_den = []
        for b in range(SUBS):
            block = i * SUBS + b
            bias = bias_ref[jnp.where(block == 0, 1, 0) + jnp.where(block == nb * SUBS - 1, 2, 0)]
            lse = []
            for h in range(heads):
                s = s_scr[b * heads + h] + _slope(g, h) * bias
                m = jnp.max(s, axis=0, keepdims=True)
                p = jnp.exp(s - m)
                den = jnp.sum(p, axis=0, keepdims=True)
                p_scr[b * heads + h] = p.astype(BF16)
                inv_den.append(1.0 / den)
                lse.append(m + jnp.log(den))
            lse_ref[b * TQ:(b + 1) * TQ, :] = _head_stats(lse)
        for b in range(SUBS):
            vwin = vext[b * TQ:(b + 2) * TQ, :]
            for h in range(heads):
                ot = _dot_tn(_pair(vwin, h), p_scr[b * heads + h])
                ot_scr[h * HEAD_DIM:(h + 1) * HEAD_DIM, b * TQ:(b + 1) * TQ] = _own_rows(ot, h) * inv_den[b * heads + h]
        o_ref[...] = ot_scr[...].T

    def spec(col, shift):
        return pl.BlockSpec((None, TB, GROUP_W), lambda r, i: (r, jnp.clip(i + shift, 0, nb - 1), col))

    return pl.pallas_call(
        body, grid=(dil, nb),
        in_specs=[spec(col0, 0), spec(col0 + 1, -1), spec(col0 + 1, 0), spec(col0 + 1, 1),
                  spec(col0 + 2, -1), spec(col0 + 2, 0), spec(col0 + 2, 1),
                  pl.BlockSpec((4, 2 * TQ, TQ), lambda r, i: (0, 0, 0))],
        out_specs=[pl.BlockSpec((None, TB, GROUP_W), lambda r, i: (r, i, 0)),
                   pl.BlockSpec((None, TB, LANES), lambda r, i: (r, i, 0))],
        out_shape=[jax.ShapeDtypeStruct((dil, sub, GROUP_W), F32), jax.ShapeDtypeStruct((dil, sub, LANES), F32)],
        scratch_shapes=[pltpu.VMEM((GROUP_W, TB), F32), pltpu.VMEM((SUBS * heads, 2 * TQ, TQ), F32),
                        pltpu.VMEM((SUBS * heads, 2 * TQ, TQ), BF16)],
        name=f"attn_fwd_g{g}", compiler_params=_cparams(("parallel", "arbitrary"), 32))(
            *_hbm(*([qkv] * 7), _attn_bias_table(g)))


def _expand_heads():
    h = lax.broadcasted_iota(jnp.int32, (LANES, GROUP_W), 0)
    c = lax.broadcasted_iota(jnp.int32, (LANES, GROUP_W), 1)
    return (c // HEAD_DIM == h).astype(F32)


def _dot_f32(a, b):
    return jnp.dot(a, b, preferred_element_type=F32, precision=lax.Precision.HIGH)


def _attn_combine(outs, lses, *, tm=512):
    S = outs[0].shape[1]
    n_col = GROUP_W // LANES

    def body(*refs):
        ins, e_ref = refs[:2 * N_GROUPS], refs[2 * N_GROUPS]
        c_ref, cb_ref, lt_ref = refs[2 * N_GROUPS + 1:2 * N_GROUPS + 4]
        scr = refs[2 * N_GROUPS + 4:]
        o, l = [ins[0][0]], [ins[N_GROUPS][0]]
        for k, d in enumerate(DILS):
            o_ref, l_ref = ins[1 + k], ins[N_GROUPS + 1 + k]
            o.append(_from_residue(lambda r: o_ref[r], d, tm, scr[k * (n_col + 1):k * (n_col + 1) + n_col]))
            l.append(_from_residue(lambda r: l_ref[r], d, tm, scr[k * (n_col + 1) + n_col:(k + 1) * (n_col + 1)]))
        m = jnp.maximum(jnp.maximum(l[0], l[1]), l[2])
        e = [jnp.exp(v - m) for v in l]
        den = e[0] + e[1] + e[2]
        comb = sum(_dot_f32(ev / den, e_ref[...]) * ov for ev, ov in zip(e, o))
        c_ref[...] = comb
        cb_ref[...] = comb.astype(BF16)
        lt_ref[...] = m + jnp.log(den)

    row = pl.BlockSpec((tm, GROUP_W), lambda i: (i, 0))
    dils = [d for _, d in GROUPS]
    return pl.pallas_call(
        body, grid=(S // tm,),
        in_specs=[_res_spec(d, tm, GROUP_W) for d in dils] + [_res_spec(d, tm, LANES) for d in dils]
        + [_resident((LANES, GROUP_W))],
        out_specs=[row, row, pl.BlockSpec((tm, LANES), lambda i: (i, 0))],
        out_shape=[jax.ShapeDtypeStruct((S, GROUP_W), F32), jax.ShapeDtypeStruct((S, GROUP_W), BF16),
                   jax.ShapeDtypeStruct((S, LANES), F32)],
        scratch_shapes=_lane_scratch(tm, GROUP_W + LANES) * len(DILS),
        name="attn_combine", compiler_params=_cparams(("parallel",), 32))(*_hbm(*outs, *lses, _expand_heads()))


def _branch_mix(ya_in, comb_b, w_a, w_b, proj, *, tm=512):
    S = ya_in.shape[0]

    def body(ya_ref, cb_ref, wa_ref, wb_ref, ga_ref, gb_ref, yab_ref, mx_ref):
        y_a = _dot(ya_ref[...], wa_ref[...])
        y_b = _dot(cb_ref[...], wb_ref[...])
        yab_ref[:, 0:D_MODEL] = y_a.astype(BF16)
        yab_ref[:, D_MODEL:2 * D_MODEL] = y_b.astype(BF16)
        mx = jax.nn.sigmoid(ga_ref[...].astype(F32)) * y_a + jax.nn.sigmoid(gb_ref[...].astype(F32)) * y_b
        mx_ref[...] = mx.astype(BF16)

    return pl.pallas_call(
        body, grid=(S // tm,),
        in_specs=[pl.BlockSpec((tm, D_CONV), lambda i: (i, 0)), pl.BlockSpec((tm, GROUP_W), lambda i: (i, 0)),
                  pl.BlockSpec((D_CONV, D_MODEL), lambda i: (0, 0)), pl.BlockSpec((GROUP_W, D_MODEL), lambda i: (0, 0)),
                  pl.BlockSpec((tm, D_MODEL), lambda i: (i, P_GA // D_MODEL)),
                  pl.BlockSpec((tm, D_MODEL), lambda i: (i, P_GB // D_MODEL))],
        out_specs=[pl.BlockSpec((tm, 2 * D_MODEL), lambda i: (i, 0)), pl.BlockSpec((tm, D_MODEL), lambda i: (i, 0))],
        out_shape=[jax.ShapeDtypeStruct((S, 2 * D_MODEL), BF16), jax.ShapeDtypeStruct((S, D_MODEL), BF16)],
        name="branch_mix", compiler_params=_cparams(("parallel",), 40))(*_hbm(ya_in, comb_b, w_a, w_b, proj, proj))


def _mix_ln1(mixin, w_o, b_o, h0, g1, b1, *, tm=512):
    S = mixin.shape[0]

    def body(mx_ref, wo_ref, bo_ref, h0_ref, g_ref, b_ref, xh_ref, rs_ref, h1b_ref):
        z = ALPHA * h0_ref[...] + _dot(mx_ref[...], wo_ref[...]) + bo_ref[...]
        xhat, rstd = _ln_stats(z)
        xh_ref[...] = xhat
        rs_ref[...] = jnp.broadcast_to(rstd, (tm, LANES))
        h1b_ref[...] = (xhat * g_ref[...] + b_ref[...]).astype(BF16)

    row = pl.BlockSpec((tm, D_MODEL), lambda i: (i, 0))
    vec = pl.BlockSpec((1, D_MODEL), lambda i: (0, 0))
    return pl.pallas_call(
        body, grid=(S // tm,),
        in_specs=[row, pl.BlockSpec((D_MODEL, D_MODEL), lambda i: (0, 0)), vec, row, vec, vec],
        out_specs=[row, pl.BlockSpec((tm, LANES), lambda i: (i, 0)), row],
        out_shape=[jax.ShapeDtypeStruct((S, D_MODEL), F32), jax.ShapeDtypeStruct((S, LANES), F32),
                   jax.ShapeDtypeStruct((S, D_MODEL), BF16)],
        name="mix_ln1", compiler_params=_cparams(("parallel",), 40))(*_hbm(mixin, w_o, b_o, h0, g1, b1))


def _gelu_parts(cz):
    cdf = 0.5 * (1.0 + lax.erf(cz * INV_SQRT2))
    return cdf, cz * cdf


def _ffn_conv_fwd(up, cw, cb):
    S = up.shape[0]

    def body(a_ref, g_ref, w_ref, cb_ref, o_ref, a_scr):
        _zero_pads(a_scr, S)
        for t in range(0, S, CHUNK):
            a_scr[PAD + t:PAD + t + CHUNK, :] = a_ref[t:t + CHUNK, :].astype(F32)
        w0, w1, w2 = w_ref[0:1, :], w_ref[1:2, :], w_ref[2:3, :]
        for t in range(0, S, CHUNK):
            am, a0, ap = _shifted(a_scr, t)
            _, gel = _gelu_parts(w0 * am + w1 * a0 + w2 * ap + cb_ref[...])
            o_ref[t:t + CHUNK, :] = (gel * g_ref[t:t + CHUNK, :].astype(F32)).astype(BF16)

    return pl.pallas_call(
        body, grid=(D_FF // SLAB,),
        in_specs=[_slab_spec(S, 0), _slab_spec(S, D_FF), pl.BlockSpec((3, SLAB), lambda j: (0, j)),
                  pl.BlockSpec((1, SLAB), lambda j: (0, j))],
        out_specs=pl.BlockSpec((S, SLAB), lambda j: (0, j)),
        out_shape=jax.ShapeDtypeStruct((S, D_FF), BF16),
        scratch_shapes=[pltpu.VMEM((S + 2 * PAD, SLAB), F32)],
        name="ffn_conv_fwd", compiler_params=_cparams(("parallel",), 40))(*_hbm(up, up, cw, cb))


def _down_ln2_loss(f, w_down, b_down, xhat1, g1, b1, g2, b2, target, *, tm=512):
    S = f.shape[0]

    def body(f_ref, wd_ref, bd_ref, xh1_ref, g1_ref, b1_ref, g2_ref, b2_ref, t_ref, dz_ref, dzb_ref, st_ref):
        h1 = xh1_ref[...] * g1_ref[...] + b1_ref[...]
        z = ALPHA * h1 + _dot(f_ref[...], wd_ref[...]) + bd_ref[...]
        xhat, rstd = _ln_stats(z)
        err = xhat * g2_ref[...] + b2_ref[...] - t_ref[...]
        loss = (0.5 / D_MODEL) * jnp.sum(jnp.sum(err * err, axis=1, keepdims=True), axis=0, keepdims=True)
        dh2 = err * (1.0 / D_MODEL)
        dz = _ln_bwd(dh2, xhat, rstd, g2_ref[...])
        dz_ref[...] = dz
        dzb_ref[...] = dz.astype(BF16)
        upd = _rows8([jnp.sum(dh2 * xhat, axis=0, keepdims=True), jnp.sum(dh2, axis=0, keepdims=True),
                      jnp.broadcast_to(loss, (1, D_MODEL)), jnp.sum(dz, axis=0, keepdims=True)], D_MODEL)

        @pl.when(pl.program_id(0) == 0)
        def _():
            st_ref[...] = upd

        @pl.when(pl.program_id(0) != 0)
        def _():
            st_ref[...] += upd

    row = pl.BlockSpec((tm, D_MODEL), lambda i: (i, 0))
    vec = pl.BlockSpec((1, D_MODEL), lambda i: (0, 0))
    return pl.pallas_call(
        body, grid=(S // tm,),
        in_specs=[pl.BlockSpec((tm, D_FF), lambda i: (i, 0)), _resident((D_FF, D_MODEL)),
                  vec, row, vec, vec, vec, vec, row],
        out_specs=[row, row, pl.BlockSpec((SUBLANES, D_MODEL), lambda i: (0, 0))],
        out_shape=[jax.ShapeDtypeStruct((S, D_MODEL), F32), jax.ShapeDtypeStruct((S, D_MODEL), BF16),
                   jax.ShapeDtypeStruct((SUBLANES, D_MODEL), F32)],
        name="down_ln2_loss", compiler_params=_cparams(("arbitrary",), 56))(
            *_hbm(f, w_down, b_down, xhat1, g1, b1, g2, b2, target))


def _ffn_conv_bwd(up, df, cw, cb):
    S = up.shape[0]

    def body(a_ref, g_ref, df_ref, w_ref, cb_ref, dup_ref, sm_ref, a_scr, d_scr, tap_scr):
        _zero_pads(a_scr, S)
        _zero_pads(d_scr, S)
        tap_scr[...] = jnp.zeros_like(tap_scr)
        for t in range(0, S, CHUNK):
            a_scr[PAD + t:PAD + t + CHUNK, :] = a_ref[t:t + CHUNK, :].astype(F32)
        w0, w1, w2 = w_ref[0:1, :], w_ref[1:2, :], w_ref[2:3, :]
        zero = jnp.zeros((1, SLAB), F32)
        s_dg, s_dcz = zero, zero
        for t in range(0, S, CHUNK):
            am, a0, ap = _shifted(a_scr, t)
            cz = w0 * am + w1 * a0 + w2 * ap + cb_ref[...]
            cdf, gel = _gelu_parts(cz)
            dfv = df_ref[t:t + CHUNK, :].astype(F32)
            dgte = dfv * gel
            dcz = dfv * g_ref[t:t + CHUNK, :].astype(F32) * (cdf + cz * jnp.exp(-0.5 * cz * cz) * INV_SQRT_2PI)
            dup_ref[1, t:t + CHUNK, :] = dgte.astype(BF16)
            d_scr[PAD + t:PAD + t + CHUNK, :] = dcz
            s_dg = s_dg + jnp.sum(dgte, axis=0, keepdims=True)
            s_dcz = s_dcz + jnp.sum(dcz, axis=0, keepdims=True)
            _tap_sums_add(tap_scr, dcz, (am, a0, ap))
        s_da = zero
        for t in range(0, S, CHUNK):
            dm, d0, dp = _shifted(d_scr, t)
            da = w0 * dp + w1 * d0 + w2 * dm
            dup_ref[0, t:t + CHUNK, :] = da.astype(BF16)
            s_da = s_da + jnp.sum(da, axis=0, keepdims=True)
        sm_ref[...] = _rows8([s_da, s_dg, s_dcz] + _tap_sums(tap_scr), SLAB)

    return pl.pallas_call(
        body, grid=(D_FF // SLAB,),
        in_specs=[_slab_spec(S, 0), _slab_spec(S, D_FF), pl.BlockSpec((S, SLAB), lambda j: (0, j)),
                  pl.BlockSpec((3, SLAB), lambda j: (0, j)), pl.BlockSpec((1, SLAB), lambda j: (0, j))],
        out_specs=[pl.BlockSpec((2, S, SLAB), lambda j: (0, 0, j)), pl.BlockSpec((SUBLANES, SLAB), lambda j: (0, j))],
        out_shape=[jax.ShapeDtypeStruct((2, S, D_FF), BF16), jax.ShapeDtypeStruct((SUBLANES, D_FF), F32)],
        scratch_shapes=[pltpu.VMEM((S + 2 * PAD, SLAB), F32)] * 2 + [pltpu.VMEM((3, SLAB, SLAB), F32)],
        name="ffn_conv_bwd", compiler_params=_cparams(("parallel",), 48))(*_hbm(up, up, df, cw, cb))


def _up_bwd_ln1(dup, w_up3, dz2, xhat1, rstd1, g1, *, tm=512):
    S = dz2.shape[0]
    ns, _, tk = w_up3.shape
    per_plane = D_FF // tk

    def body(du_ref, w_ref, dz2_ref, xh_ref, rs_ref, g_ref, dz_ref, dzb_ref, st_ref):
        dh = ALPHA * dz2_ref[...]
        for k in range(ns):
            col = (k % per_plane) * tk
            dh = dh + _dot_nt(du_ref[k // per_plane, :, col:col + tk], w_ref[k])
        xhat = xh_ref[...]
        dz = _ln_bwd(dh, xhat, rs_ref[:, 0:1], g_ref[...])
        dz_ref[...] = dz
        dzb_ref[...] = dz.astype(BF16)
        upd = _rows8([jnp.sum(dh * xhat, axis=0, keepdims=True), jnp.sum(dh, axis=0, keepdims=True),
                      jnp.sum(dz, axis=0, keepdims=True)], D_MODEL)

        @pl.when(pl.program_id(0) == 0)
        def _():
            st_ref[...] = upd

        @pl.when(pl.program_id(0) != 0)
        def _():
            st_ref[...] += upd

    row = pl.BlockSpec((tm, D_MODEL), lambda i: (i, 0))
    return pl.pallas_call(
        body, grid=(S // tm,),
        in_specs=[pl.BlockSpec((dup.shape[0], tm, D_FF), lambda i: (0, i, 0)), _resident(w_up3.shape),
                  row, row, pl.BlockSpec((tm, LANES), lambda i: (i, 0)), pl.BlockSpec((1, D_MODEL), lambda i: (0, 0))],
        out_specs=[row, row, pl.BlockSpec((SUBLANES, D_MODEL), lambda i: (0, 0))],
        out_shape=[jax.ShapeDtypeStruct((S, D_MODEL), F32), jax.ShapeDtypeStruct((S, D_MODEL), BF16),
                   jax.ShapeDtypeStruct((SUBLANES, D_MODEL), F32)],
        name="up_bwd_ln1", compiler_params=_cparams(("arbitrary",), 56))(*_hbm(dup, w_up3, dz2, xhat1, rstd1, g1))


def _mix_bwd(dz1b, w_o, proj, yab, *, tm=512):
    S = dz1b.shape[0]

    def body(dz_ref, wo_ref, ga_ref, gb_ref, y_ref, dy_ref, dg_ref):
        dmx = _dot_nt(dz_ref[...], wo_ref[...])
        for k, gt_ref in enumerate((ga_ref, gb_ref)):
            sl = slice(k * D_MODEL, (k + 1) * D_MODEL)
            sg = jax.nn.sigmoid(gt_ref[...].astype(F32))
            dy_ref[:, sl] = (dmx * sg).astype(BF16)
            dg_ref[k] = (dmx * y_ref[:, sl].astype(F32) * sg * (1.0 - sg)).astype(BF16)

    row = pl.BlockSpec((tm, D_MODEL), lambda i: (i, 0))
    wide = pl.BlockSpec((tm, 2 * D_MODEL), lambda i: (i, 0))
    return pl.pallas_call(
        body, grid=(S // tm,),
        in_specs=[row, _resident(w_o.shape), pl.BlockSpec((tm, D_MODEL), lambda i: (i, P_GA // D_MODEL)),
                  pl.BlockSpec((tm, D_MODEL), lambda i: (i, P_GB // D_MODEL)), wide],
        out_specs=[wide, pl.BlockSpec((2, tm, D_MODEL), lambda i: (0, i, 0))],
        out_shape=[jax.ShapeDtypeStruct((S, 2 * D_MODEL), BF16), jax.ShapeDtypeStruct((2, S, D_MODEL), BF16)],
        name="mix_bwd", compiler_params=_cparams(("parallel",), 40))(*_hbm(dz1b, w_o, proj, proj, yab))


def _conv_gate_bwd(proj, dya_in, conv_w):
    S = proj.shape[0]

    def body(b_ref, c_ref, h_ref, dy_ref, w_ref, o_ref, sm_ref, u_scr, d_scr, tap_scr):
        _zero_pads(u_scr, S)
        _zero_pads(d_scr, S)
        tap_scr[...] = jnp.zeros_like(tap_scr)
        for t in range(0, S, CHUNK):
            u_scr[PAD + t:PAD + t + CHUNK, :] = c_ref[t:t + CHUNK, :].astype(F32) * h_ref[t:t + CHUNK, :].astype(F32)
        w0, w1, w2 = w_ref[0:1, :], w_ref[1:2, :], w_ref[2:3, :]
        for t in range(0, S, CHUNK):
            um, u0, up = _shifted(u_scr, t)
            dy = dy_ref[t:t + CHUNK, :].astype(F32)
            o_ref[0, t:t + CHUNK, :] = (dy * (w0 * um + w1 * u0 + w2 * up)).astype(BF16)
            dcv = dy * b_ref[t:t + CHUNK, :].astype(F32)
            d_scr[PAD + t:PAD + t + CHUNK, :] = dcv
            _tap_sums_add(tap_scr, dcv, (um, u0, up))
        for t in range(0, S, CHUNK):
            dm, d0, dp = _shifted(d_scr, t)
            du = w0 * dp + w1 * d0 + w2 * dm
            o_ref[1, t:t + CHUNK, :] = (du * h_ref[t:t + CHUNK, :].astype(F32)).astype(BF16)
            o_ref[2, t:t + CHUNK, :] = (du * c_ref[t:t + CHUNK, :].astype(F32)).astype(BF16)
        sm_ref[...] = _rows8(_tap_sums(tap_scr), SLAB)

    return pl.pallas_call(
        body, grid=(D_CONV // SLAB,),
        in_specs=[_slab_spec(S, P_B), _slab_spec(S, P_C), _slab_spec(S, P_H),
                  pl.BlockSpec((S, SLAB), lambda j: (0, j)), pl.BlockSpec((3, SLAB), lambda j: (0, j))],
        out_specs=[pl.BlockSpec((3, S, SLAB), lambda j: (0, 0, j)), pl.BlockSpec((SUBLANES, SLAB), lambda j: (0, j))],
        out_shape=[jax.ShapeDtypeStruct((3, S, D_CONV), BF16), jax.ShapeDtypeStruct((SUBLANES, D_CONV), F32)],
        scratch_shapes=[pltpu.VMEM((S + 2 * PAD, SLAB), F32)] * 2 + [pltpu.VMEM((3, SLAB, SLAB), F32)],
        name="conv_gate_bwd", compiler_params=_cparams(("parallel",), 48))(*_hbm(proj, proj, proj, dya_in, conv_w))


def _comb_bwd(dyab, w_b, comb, lse_tot, *, tm=512):
    S = comb.shape[0]
    widths, dtypes = (GROUP_W, LANES, LANES), (BF16, F32, F32)

    def body(dy_ref, wb_ref, c_ref, lt_ref, e_ref, *rest):
        outs, scr = rest[:3 * N_GROUPS], rest[3 * N_GROUPS:]
        dcb = _dot_nt(dy_ref[...], wb_ref[...]).astype(BF16)
        dc = dcb.astype(F32)
        delta = lax.dot_general(dc * c_ref[...], e_ref[...], (((1,), (1,)), ((), ())),
                                preferred_element_type=F32, precision=lax.Precision.HIGH)
        for k, (val, dtype) in enumerate(zip((dc, lt_ref[...], delta), dtypes)):
            outs[k][0] = val.astype(dtype)
            _to_residue(val, [outs[3 * (1 + j) + k] for j in range(len(DILS))], DILS, tm, dtype,
                        scr[:val.shape[1] // LANES])

    out_specs, out_shape = [], []
    for _, d in GROUPS:
        out_specs += [_res_spec(d, tm, w) for w in widths]
        out_shape += [jax.ShapeDtypeStruct((d, S // d, w), t) for w, t in zip(widths, dtypes)]
    res = pl.pallas_call(
        body, grid=(S // tm,),
        in_specs=[pl.BlockSpec((tm, D_MODEL), lambda i: (i, 1)), _resident(w_b.shape),
                  pl.BlockSpec((tm, GROUP_W), lambda i: (i, 0)), pl.BlockSpec((tm, LANES), lambda i: (i, 0)),
                  _resident((LANES, GROUP_W))],
        out_specs=out_specs, out_shape=out_shape, scratch_shapes=_lane_scratch(tm, GROUP_W),
        name="comb_bwd", compiler_params=_cparams(("parallel",), 32))(*_hbm(dyab, w_b, comb, lse_tot, _expand_heads()))
    return [tuple(res[3 * g:3 * g + 3]) for g in range(N_GROUPS)]


def _attn_bwd(qkv, col0, g, dcomb, lse_tot, delta):
    dil, sub, _ = qkv.shape
    nb = sub // TB
    heads = HEADS_PER_GROUP

    def body(q_ref, kp, kc, kn, vp, vc, vn, do_ref, lse_ref, dl_ref, bias_ref, dq_ref, dk_ref, dv_ref,
             ak, av, dqt_scr, s_scr, dp_scr, ds_scr, p_scr):
        i = pl.program_id(1)

        @pl.when(i == 0)
        def _():
            ak[...] = jnp.zeros_like(ak)
            av[...] = jnp.zeros_like(av)

        @pl.when(i < nb)
        def _():
            kext = _ext_window(kp, kc, kn)
            vext = _ext_window(vp, vc, vn)
            q = q_ref[...] * ATT_SCALE
            do = do_ref[...]
            lse_t, dl_t = lse_ref[...].T, dl_ref[...].T
            for b in range(SUBS):
                rows = slice(b * TQ, (b + 1) * TQ)
                kwin, vwin = kext[b * TQ:(b + 2) * TQ, :], vext[b * TQ:(b + 2) * TQ, :]
                for h in range(heads):
                    s_scr[b * heads + h] = _dot_nt(_pair(kwin, h), _own_lanes(_pair(q[rows], h), h))
                    dp_scr[b * heads + h] = _dot_nt(_pair(vwin, h), _own_lanes(_pair(do[rows], h), h))
            for b in range(SUBS):
                cols = slice(b * TQ, (b + 1) * TQ)
                block = i * SUBS + b
                bias = bias_ref[jnp.where(block == 0, 1, 0) + jnp.where(block == nb * SUBS - 1, 2, 0)]
                for h in range(heads):
                    k = b * heads + h
                    p = jnp.exp(s_scr[k] + _slope(g, h) * bias - lse_t[h:h + 1, cols])
                    ds_scr[k] = (p * (dp_scr[k] - dl_t[h:h + 1, cols])).astype(BF16)
                    p_scr[k] = p.astype(BF16)
            for b in range(SUBS):
                kwin = kext[b * TQ:(b + 2) * TQ, :]
                for h in range(heads):
                    dqt_scr[h * HEAD_DIM:(h + 1) * HEAD_DIM, b * TQ:(b + 1) * TQ] = _own_rows(
                        _dot_tn(_pair(kwin, h), ds_scr[b * heads + h]), h)
            for b in range(SUBS):
                rows = slice(b * TQ, (b + 1) * TQ)
                acc_rows = slice(TB - RADIUS + b * TQ, TB - RADIUS + (b + 2) * TQ)
                for h in range(0, heads, 2):
                    cols = slice(h * HEAD_DIM, (h + 2) * HEAD_DIM)
                    k = b * heads + h
                    q2 = jnp.concatenate([_own_lanes(_pair(q[rows], h), h), _own_lanes(_pair(q[rows], h), h + 1)], axis=0)
                    do2 = jnp.concatenate([_own_lanes(_pair(do[rows], h), h), _own_lanes(_pair(do[rows], h), h + 1)],
                                          axis=0)
                    ak[acc_rows, cols] += _dot(jnp.concatenate([ds_scr[k], ds_scr[k + 1]], axis=1), q2)
                    av[acc_rows, cols] += _dot(jnp.concatenate([p_scr[k], p_scr[k + 1]], axis=1), do2)
            dq_ref[...] = (dqt_scr[...].T * ATT_SCALE).astype(BF16)

        if nb == 1:
            dk_ref[...] = ak[TB:2 * TB, :].astype(BF16)
            dv_ref[...] = av[TB:2 * TB, :].astype(BF16)
        else:
            dk_ref[...] = ak[0:TB, :].astype(BF16)
            dv_ref[...] = av[0:TB, :].astype(BF16)
            used = 2 * TB + RADIUS
            for acc in (ak, av):
                acc[0:used - TB, :] = acc[TB:used, :]
                acc[used - TB:used, :] = jnp.zeros((TB, GROUP_W), F32)

    def spec(col, shift):
        return pl.BlockSpec((None, TB, GROUP_W), lambda r, i: (r, jnp.clip(i + shift, 0, nb - 1), col))

    tok = pl.BlockSpec((None, TB, GROUP_W), lambda r, i: (r, jnp.minimum(i, nb - 1), 0))
    stat = pl.BlockSpec((None, TB, LANES), lambda r, i: (r, jnp.minimum(i, nb - 1), 0))
    dkv_spec = tok if nb == 1 else pl.BlockSpec((None, TB, GROUP_W), lambda r, i: (r, jnp.maximum(i - 1, 0), 0))
    return pl.pallas_call(
        body, grid=(dil, nb + (nb > 1)),
        in_specs=[spec(col0, 0), spec(col0 + 1, -1), spec(col0 + 1, 0), spec(col0 + 1, 1),
                  spec(col0 + 2, -1), spec(col0 + 2, 0), spec(col0 + 2, 1), tok, stat, stat,
                  pl.BlockSpec((4, 2 * TQ, TQ), lambda r, i: (0, 0, 0))],
        out_specs=[tok, dkv_spec, dkv_spec], out_shape=[jax.ShapeDtypeStruct((dil, sub, GROUP_W), BF16)] * 3,
        scratch_shapes=[pltpu.VMEM((3 * TB, GROUP_W), F32)] * 2 + [pltpu.VMEM((GROUP_W, TB), F32)]
        + [pltpu.VMEM((SUBS * heads, 2 * TQ, TQ), F32)] * 2 + [pltpu.VMEM((SUBS * heads, 2 * TQ, TQ), BF16)] * 2,
        name=f"attn_bwd_g{g}", compiler_params=_cparams(("arbitrary", "arbitrary"), 40))(
            *_hbm(*([qkv] * 7), dcomb, lse_tot, delta, _attn_bias_table(g)))


def _in_bwd_ln0(dgated, dqkv, w_nat, w_dil, dz1, x, g0, *, tm=256):
    S = x.shape[0]
    n_gated, n_in = len(dgated), 3 * N_GROUPS

    def body(*refs):
        g_refs, d_refs = refs[:n_gated], refs[n_gated:n_gated + n_in]
        wn_ref, *wd_refs = refs[n_gated + n_in:n_gated + n_in + N_GROUPS]
        dz_ref, x_ref, g_ref, gx_ref, st_ref, *tmp_ref = refs[n_gated + n_in + N_GROUPS:]
        dh = ALPHA * dz_ref[...]
        col = 0
        for ref in g_refs:
            for k in range(ref.shape[0]):
                dh = dh + _dot_nt(ref[k], wn_ref[:, col:col + D_MODEL])
                col += D_MODEL
        for g, (_, d) in enumerate(GROUPS):
            rows = [jnp.concatenate([d_refs[3 * g + k][r] for k in range(3)], axis=1) for r in range(d)]
            w = wn_ref[:, col:col + QKV_W] if d == 1 else wd_refs[g - 1][...]
            res = _dot_nt(jnp.concatenate(rows, axis=0), w)
            if d == 1:
                dh = dh + res
            else:
                n = tm // d
                dh = dh + _from_residue(lambda r: res[r * n:(r + 1) * n, :], d, tm, tmp_ref)
        xhat, rstd = _ln_stats(x_ref[...])
        gx_ref[...] = _ln_bwd(dh, xhat, rstd, g_ref[...])
        upd = _rows8([jnp.sum(dh * xhat, axis=0, keepdims=True), jnp.sum(dh, axis=0, keepdims=True)], D_MODEL)

        @pl.when(pl.program_id(0) == 0)
        def _():
            st_ref[...] = upd

        @pl.when(pl.program_id(0) != 0)
        def _():
            st_ref[...] += upd

    row = pl.BlockSpec((tm, D_MODEL), lambda i: (i, 0))
    g_specs = [pl.BlockSpec((a.shape[0], tm, D_MODEL), lambda i: (0, i, 0)) for a in dgated]
    d_specs = []
    for _, d in GROUPS:
        d_specs += [_res_spec(d, tm, GROUP_W)] * 3
    operands = list(dgated) + [a for grp in dqkv for a in grp] + [w_nat] + list(w_dil) + [dz1, x, g0]
    return pl.pallas_call(
        body, grid=(S // tm,),
        in_specs=g_specs + d_specs + [_resident(w_nat.shape)] + [_resident(w.shape) for w in w_dil]
        + [row, row, pl.BlockSpec((1, D_MODEL), lambda i: (0, 0))],
        out_specs=[row, pl.BlockSpec((SUBLANES, D_MODEL), lambda i: (0, 0))],
        out_shape=[jax.ShapeDtypeStruct((S, D_MODEL), F32), jax.ShapeDtypeStruct((SUBLANES, D_MODEL), F32)],
        scratch_shapes=_lane_scratch(tm, D_MODEL),
        name="in_bwd_ln0", compiler_params=_cparams(("arbitrary",), 52))(*_hbm(*operands))


HBM_SPEC = pl.BlockSpec(memory_space=pltpu.HBM)


def _place():
    x, y, c = lax.axis_index("x"), lax.axis_index("y"), lax.axis_index("c")
    chips = [(1 - x, y), (x, 1 - y), (1 - x, 1 - y)]
    return x, y, c, chips


def _allgather_shards(shards, after, *, name, collective_id):
    n = len(shards)
    per = 6

    def body(*refs):
        ins, outs = refs[:n], refs[n + len(after):2 * n + len(after)]
        send_sems, recv_sems, loc_sems = refs[2 * n + len(after):]
        x, y, c, chips = _place()
        me = 2 * x + y
        sib = (x, y, 1 - c)
        peers = [sib] + [(px, py, c) for px, py in chips]
        barrier = pltpu.get_barrier_semaphore()
        for peer in peers:
            pl.semaphore_signal(barrier, inc=1, device_id=peer, device_id_type=MESH)
        pl.semaphore_wait(barrier, len(peers))

        def rcopy(w, k, src, dst, to):
            return pltpu.make_async_remote_copy(src_ref=src, dst_ref=dst, send_sem=send_sems.at[per * w + k],
                                                recv_sem=recv_sems.at[per * w + k], device_id=to, device_id_type=MESH)

        split = [s.shape[0] == N_CORES for s in shards]
        half = lambda w: c if split[w] else 0
        local, sends = [], []
        for w in range(n):
            cp = pltpu.make_async_copy(ins[w], outs[w].at[me], loc_sems.at[w])
            cp.start()
            local.append(cp)
            for j, (px, py) in enumerate(chips):
                cp = rcopy(w, j, ins[w].at[half(w)], outs[w].at[me, half(w)], (px, py, c))
                cp.start()
                sends.append(cp)
        for w in range(n):
            for j, (px, py) in enumerate(chips):
                slot = outs[w].at[2 * px + py, half(w)]
                rcopy(w, j, slot, slot, (px, py, c)).wait_recv()
                if split[w]:
                    cp = rcopy(w, 3 + j, slot, slot, sib)
                    cp.start()
                    sends.append(cp)
        for w in range(n):
            if split[w]:
                for j, (px, py) in enumerate(chips):
                    slot = outs[w].at[2 * px + py, 1 - c]
                    rcopy(w, 3 + j, slot, slot, sib).wait_recv()
        for cp in sends:
            cp.wait_send()
        for cp in local:
            cp.wait()

    return pl.kernel(
        body, out_type=[jax.ShapeDtypeStruct((N_CHIPS,) + s.shape, s.dtype) for s in shards],
        mesh=plsc.ScalarSubcoreMesh(axis_name="sequencer", num_cores=1),
        scratch_types=[pltpu.SemaphoreType.DMA((per * n,)), pltpu.SemaphoreType.DMA((per * n,)),
                       pltpu.SemaphoreType.DMA((n,))],
        name=name, compiler_params=pltpu.CompilerParams(collective_id=collective_id))(*shards, *after)


def _exchange_grads(grads, *, name, collective_id):
    n = len(grads)
    per = 7

    def body(*refs):
        ins, outs = refs[:n], refs[n:2 * n]
        send_sems, recv_sems, loc_sems = refs[2 * n:]
        x, y, c, chips = _place()
        me = 2 * x + y
        sib = (x, y, 1 - c)
        peers = [sib] + [(px, py, c) for px, py in chips]
        barrier = pltpu.get_barrier_semaphore()
        for peer in peers:
            pl.semaphore_signal(barrier, inc=1, device_id=peer, device_id_type=MESH)
        pl.semaphore_wait(barrier, len(peers))

        def rcopy(w, k, src, dst, to):
            return pltpu.make_async_remote_copy(src_ref=src, dst_ref=dst, send_sem=send_sems.at[per * w + k],
                                                recv_sem=recv_sems.at[per * w + k], device_id=to, device_id_type=MESH)

        local, sends = [], []
        for w in range(n):
            cp = pltpu.make_async_copy(ins[w].at[me], outs[w].at[c, me], loc_sems.at[w])
            cp.start()
            local.append(cp)
            cp = rcopy(w, 0, ins[w].at[me], outs[w].at[c, me], sib)
            cp.start()
            sends.append(cp)
            for j, (px, py) in enumerate(chips):
                cp = rcopy(w, 1 + j, ins[w].at[2 * px + py], outs[w].at[c, me], (px, py, c))
                cp.start()
                sends.append(cp)
        for w in range(n):
            for j, (px, py) in enumerate(chips):
                slot = outs[w].at[c, 2 * px + py]
                rcopy(w, 1 + j, slot, slot, (px, py, c)).wait_recv()
                cp = rcopy(w, 4 + j, slot, slot, sib)
                cp.start()
                sends.append(cp)
        for w in range(n):
            slot = outs[w].at[1 - c, me]
            rcopy(w, 0, slot, slot, sib).wait_recv()
            for j, (px, py) in enumerate(chips):
                slot = outs[w].at[1 - c, 2 * px + py]
                rcopy(w, 4 + j, slot, slot, sib).wait_recv()
        for cp in sends:
            cp.wait_send()
        for cp in local:
            cp.wait()

    return pl.kernel(
        body, out_type=[jax.ShapeDtypeStruct((N_CORES,) + g.shape, g.dtype) for g in grads],
        mesh=plsc.ScalarSubcoreMesh(axis_name="sequencer", num_cores=1),
        scratch_types=[pltpu.SemaphoreType.DMA((per * n,)), pltpu.SemaphoreType.DMA((per * n,)),
                       pltpu.SemaphoreType.DMA((n,))],
        name=name, compiler_params=pltpu.CompilerParams(collective_id=collective_id))(*grads)


def _allgather_small(vec, after):
    def body(v_ref, _, o_ref, send_sems, recv_sems, loc_sem):
        x, y, c = lax.axis_index("x"), lax.axis_index("y"), lax.axis_index("c")
        me = 4 * x + 2 * y + c

        def peer(k):
            flip = lambda v, bit: 1 - v if (k >> bit) & 1 else v
            return flip(x, 2), flip(y, 1), flip(c, 0)

        loc = pltpu.make_async_copy(v_ref, o_ref.at[me], loc_sem)
        loc.start()
        sends = []
        for k in range(1, N_DEV):
            cp = pltpu.make_async_remote_copy(src_ref=v_ref, dst_ref=o_ref.at[me], send_sem=send_sems.at[k - 1],
                                              recv_sem=recv_sems.at[k - 1], device_id=peer(k), device_id_type=MESH)
            cp.start()
            sends.append(cp)
        for k in range(1, N_DEV):
            px, py, pc = peer(k)
            pltpu.make_async_remote_copy(src_ref=v_ref, dst_ref=o_ref.at[4 * px + 2 * py + pc],
                                         send_sem=send_sems.at[k - 1], recv_sem=recv_sems.at[k - 1],
                                         device_id=(px, py, pc), device_id_type=MESH).wait_recv()
        for cp in sends:
            cp.wait_send()
        loc.wait()

    return pl.pallas_call(
        body, in_specs=[HBM_SPEC, HBM_SPEC], out_specs=HBM_SPEC,
        out_shape=jax.ShapeDtypeStruct((N_DEV,) + vec.shape, vec.dtype),
        scratch_shapes=[pltpu.SemaphoreType.DMA((N_DEV - 1,)), pltpu.SemaphoreType.DMA((N_DEV - 1,)),
                        pltpu.SemaphoreType.DMA],
        name="allgather_small")(vec, after)


def _adamw(w, g, m, v):
    m = ADAM_B1 * m + (1.0 - ADAM_B1) * g
    v = ADAM_B2 * v + (1.0 - ADAM_B2) * (g * g)
    m_hat = m / (1.0 - ADAM_B1 ** ADAM_STEP)
    v_hat = v / (1.0 - ADAM_B2 ** ADAM_STEP)
    delta = -ADAM_LR * (m_hat / (jnp.sqrt(v_hat) + ADAM_EPS) + ADAM_WD * w)
    return delta, m, v


def _reduce_adamw(parts, w, m, v, *, tr, name):
    R, C = w.shape

    def body(p_ref, w_ref, m_ref, v_ref, g_ref, d_ref, nm_ref, nv_ref):
        def core_sum(cc):
            s = p_ref[cc, 0].astype(F32)
            for k in range(1, N_CHIPS):
                s = s + p_ref[cc, k].astype(F32)
            return s

        g = core_sum(0) + core_sum(1)
        delta, nm, nv = _adamw(w_ref[...], g, m_ref[...], v_ref[...])
        g_ref[...] = g
        d_ref[...] = delta
        nm_ref[...] = nm
        nv_ref[...] = nv

    blk = pl.BlockSpec((tr, C), lambda i: (i, 0))
    return pl.pallas_call(
        body, grid=(R // tr,),
        in_specs=[pl.BlockSpec((N_CORES, N_CHIPS, tr, C), lambda i: (0, 0, i, 0)), blk, blk, blk],
        out_specs=[blk] * 4, out_shape=[jax.ShapeDtypeStruct((R, C), F32)] * 4,
        name=name, compiler_params=_cparams(("parallel",), 40))(*_hbm(parts, w, m, v))


def _reduce_adamw_vectors(allv, offs, ws, ms, vs):
    n = len(ws)

    def body(a_ref, *refs):
        w_refs, m_refs, v_refs = refs[:n], refs[n:2 * n], refs[2 * n:3 * n]
        tot_ref, outs = refs[3 * n], refs[3 * n + 1:]
        s = a_ref[0]
        for d in range(1, N_DEV):
            s = s + a_ref[d]
        tot_ref[...] = s
        for k in range(n):
            g = s[:, offs[k]:offs[k] + w_refs[k].shape[1]]
            delta, nm, nv = _adamw(w_refs[k][...], g, m_refs[k][...], v_refs[k][...])
            for ref, val in zip(outs[4 * k:4 * k + 4], (g, delta, nm, nv)):
                ref[...] = val

    out_shape = [jax.ShapeDtypeStruct(allv.shape[1:], F32)]
    for w in ws:
        out_shape += [jax.ShapeDtypeStruct(w.shape, F32)] * 4
    res = pl.pallas_call(body, out_shape=out_shape, name="reduce_adamw_vectors",
                         compiler_params=_cparams((), 40))(allv, *ws, *ms, *vs)
    return res[0], [tuple(res[1 + 4 * k:5 + 4 * k]) for k in range(n)]


def _adamw_taps(ws, gs, ms, vs):
    n = len(ws)

    def body(*refs):
        outs = refs[4 * n:]
        for k in range(n):
            res = _adamw(refs[k][...], refs[n + k][...], refs[2 * n + k][...], refs[3 * n + k][...])
            for ref, val in zip(outs[3 * k:3 * k + 3], res):
                ref[...] = val

    out_shape = []
    for w in ws:
        out_shape += [jax.ShapeDtypeStruct(w.shape, F32)] * 3
    res = pl.pallas_call(body, out_shape=out_shape, name="adamw_taps")(*ws, *gs, *ms, *vs)
    return [tuple(res[3 * k:3 * k + 3]) for k in range(n)]


def _pack(pieces):
    flat, offs, n = [], [], 0
    for p in pieces:
        size = -(-p.size // LANES) * LANES
        flat.append(jnp.pad(p.reshape(-1), (0, size - p.size)))
        offs.append(n)
        n += size
    return jnp.concatenate(flat).reshape(1, n), offs


def _local_step(x, target, p, wfull, on_ready=lambda group: None, before_ln0=()):
    S = x.shape[0]
    dils = [d for _, d in GROUPS]

    h0, h0b, *h0_res = _ln0_fwd(x, p["ln0_g"], p["ln0_b"], before_ln0)
    h0_rows = [h0b] + [h.reshape(S, D_MODEL) for h in h0_res]

    if isinstance(wfull, dict):
        w_in3, pending = wfull["w_in"], None
    else:
        w_in3, launch_rest, assemble = wfull
        w_in3, h0b = lax.optimization_barrier((w_in3, h0b))
        pending = launch_rest(h0b)

    runs = _col_runs()
    w_perm = jnp.concatenate([w_in3[s, :, c:c + w] for s, c, _, w in runs], axis=1)
    b_blocks = p["b_in"].reshape(N_BLK, GROUP_W)
    b_perm = jnp.concatenate([b_blocks[b] for b in PERM]).reshape(1, N_IN)
    w_nat, b_nat = w_perm[:, :N_NAT], b_perm[:, :N_NAT]
    qkv_cols = [slice(P_Q0 + g * QKV_W, P_Q0 + (g + 1) * QKV_W) for g in range(N_GROUPS)]
    w_qkv = [w_perm[:, c] for c in qkv_cols]

    proj = _mm_nn(h0b, w_nat, b_nat, tm=512, tn=N_NAT // 2, out_dtype=BF16, name="proj")
    qkv = [proj[None]]
    for g in range(1, N_GROUPS):
        t = _mm_nn(h0_rows[g], w_qkv[g], b_perm[:, qkv_cols[g]], tm=512, tn=QKV_W, out_dtype=BF16, name=f"proj_qkv{g}")
        qkv.append(t.reshape(dils[g], S // dils[g], QKV_W))
    if pending is not None:
        pending, qkv = lax.optimization_barrier((pending, qkv))
        proj = qkv[0][0]
        wfull = assemble(pending)
    w_up3 = wfull["w_up"]
    w_a, w_o, w_down, w_b = wfull["w_a"], wfull["w_o"], wfull["w_down"], wfull["w_b"]
    conv_w, ffn_conv_w = wfull["conv_w"], wfull["ffn_conv_w"]
    col0 = [P_Q0 // GROUP_W] + [0] * (N_GROUPS - 1)
    ya_in = _conv_gate_fwd(proj, conv_w)
    att = [_attn_fwd(qkv[g], col0[g], g) for g in range(N_GROUPS)]
    comb, comb_b, lse_tot = _attn_combine([a[0] for a in att], [a[1] for a in att])
    yab, mixin = _branch_mix(ya_in, comb_b, w_a, w_b, proj)
    xhat1, rstd1, h1b = _mix_ln1(mixin, w_o, p["b_o"], h0, p["ln1_g"], p["ln1_b"])
    up = _mm_nn(h1b, w_up3, p["b_up"], tm=512, tn=w_up3.shape[2], out_dtype=BF16, name="up")
    f = _ffn_conv_fwd(up, ffn_conv_w, p["ffn_conv_b"])
    dz2, dz2b, st2 = _down_ln2_loss(f, w_down, p["b_down"], xhat1, p["ln1_g"], p["ln1_b"],
                                    p["ln2_g"], p["ln2_b"], target)

    gw = {}
    gw["w_down"] = _mm_tn(f, dz2b, n_out=1, tn=D_MODEL, ts=1024, g_block=(1024, D_MODEL),
                          g_map=lambda j, s: (s, 0), name="grad_w_down").reshape(N_CHIPS, D_FF // N_CHIPS, D_MODEL)
    df = _mm_nt(dz2b, w_down, tm=512, name="df")
    dup, sm_ffn = _ffn_conv_bwd(up, df, ffn_conv_w, p["ffn_conv_b"])
    up_tn = w_up3.shape[2]
    up_pp = D_FF // up_tn
    gw["w_up"] = _mm_tn(h1b, dup, n_out=N_CHIPS, tn=up_tn, ts=1024, g_block=(None, 1024, up_tn),
                        g_map=lambda j, s: (j // up_pp, s, j % up_pp), name="grad_w_up")
    exchanged = on_ready({n: gw[n] for n in ("w_down", "w_up")}) or {}
    dz1, dz1b, st1 = _up_bwd_ln1(dup, w_up3, dz2, xhat1, rstd1, p["ln1_g"])

    gw["w_o"] = _mm_tn(mixin, dz1b, n_out=1, tn=D_MODEL, ts=512, g_block=(512, D_MODEL),
                       g_map=lambda j, s: (s, 0), name="grad_w_o").reshape(N_CHIPS, D_MODEL // N_CHIPS, D_MODEL)
    dyab, dgab = _mix_bwd(dz1b, w_o, proj, yab)
    gw["w_a"] =_mm_tn(ya_in, dyab, n_out=1, tn=D_MODEL, ts=512, g_block=(512, D_MODEL),
                       g_map=lambda j, s: (s, 0), name="grad_w_a").reshape(N_CHIPS, D_CONV // N_CHIPS, D_MODEL)
    gw_b = _mm_tn(comb_b, dyab, n_out=1, tn=D_MODEL, ts=1024, g_block=(1024, D_MODEL),
                  g_map=lambda j, s: (s, 1), name="grad_w_b")
    gw["w_b"] = gw_b.reshape(GROUP_W, N_CHIPS, D_MODEL // N_CHIPS).transpose(1, 0, 2)
    exchanged_mix = on_ready({n: gw[n] for n in ("w_o", "w_a", "w_b")}) or {}
    dya_in = _mm_nt(dyab, w_a, tm=512, a_col=0, name="dya_in")
    exchanged, dya_in = lax.optimization_barrier((exchanged, dya_in))
    dbch, sm_conv = _conv_gate_bwd(proj, dya_in, conv_w)
    att_stats = _comb_bwd(dyab, w_b, comb, lse_tot)
    exchanged_mix, att_stats = lax.optimization_barrier((exchanged_mix, att_stats))
    exchanged.update(exchanged_mix)
    dqkv = [_attn_bwd(qkv[g], col0[g], g, *att_stats[g]) for g in range(N_GROUPS)]

    w_pieces, b_pieces = [], []
    for nm, planes in (("bch", dbch), ("gab", dgab)):
        pw, pc = _mm_tn(h0b, planes, n_out=planes.shape[0], tn=D_MODEL, ts=1024, g_block=(None, 1024, D_MODEL),
                        g_map=lambda j, s: (j, s, 0), colsum=True, name="grad_w_in_" + nm)
        w_pieces.extend(pw[k] for k in range(planes.shape[0]))
        b_pieces.append(pc[0])
    for g in range(N_GROUPS):
        pw, pc = _mm_tn_cat(h0_rows[g], [a.reshape(S, GROUP_W) for a in dqkv[g]], ts=1024, name=f"grad_w_in_qkv{g}")
        w_pieces.append(pw)
        b_pieces.append(pc[0])
    dw_perm = jnp.concatenate(w_pieces, axis=1)
    gw["w_in"] = jnp.stack([
        jnp.concatenate([dw_perm[:, pc:pc + w] for s, c, pc, w in sorted(runs, key=lambda r: r[1]) if s == k], axis=1)
        for k in range(N_CHIPS)])
    exchanged.update(on_ready({"w_in": gw["w_in"]}) or {})
    db_blocks = jnp.concatenate(b_pieces).reshape(N_BLK, GROUP_W)
    grad_b_in = jnp.concatenate([db_blocks[b] for b in INV_PERM])

    grad_x, st0 = _in_bwd_ln0([dbch, dgab], dqkv, w_nat, w_qkv[1:], dz1, x, p["ln0_g"])

    small = {
        "loss": st2[2:3, 0:1],
        "ln0_g": st0[0], "ln0_b": st0[1], "b_in": grad_b_in, "conv_w": sm_conv[0:3],
        "b_o": st1[2], "ln1_g": st1[0], "ln1_b": st1[1],
        "b_up": jnp.concatenate([sm_ffn[0], sm_ffn[1]]), "ffn_conv_w": sm_ffn[3:6], "ffn_conv_b": sm_ffn[2],
        "b_down": st2[3], "ln2_g": st2[0], "ln2_b": st2[1],
    }
    return grad_x, exchanged or gw, small


BIG =("w_in", "w_a", "w_b", "w_o", "w_up", "w_down")
CONV = ("conv_w", "ffn_conv_w")
VECS = ("ln0_g", "ln0_b", "b_in", "b_o", "ln1_g", "ln1_b", "b_up", "ffn_conv_b", "b_down", "ln2_g", "ln2_b")
ORDER = ("ln0_g", "ln0_b", "w_in", "b_in", "conv_w", "w_a", "w_b", "w_o", "b_o", "ln1_g", "ln1_b", "w_up", "b_up",
         "ffn_conv_w", "ffn_conv_b", "w_down", "b_down", "ln2_g", "ln2_b")
SMALL_ORDER = ("loss",) + VECS + CONV


def _step(x, target, W, Mo, Vo):
    x2, t2 = x[0], target[0]
    big2 = {n: W[n][0] for n in BIG}
    halves = lambda a: a.astype(BF16).reshape(N_CORES, a.shape[0] // N_CORES, a.shape[1])
    whole = lambda g: g.reshape(N_CHIPS, g.shape[1] * g.shape[2], g.shape[3])
    later = tuple(n for n in BIG if n != "w_in")
    w_in_halves = halves(big2["w_in"])
    first = _allgather_shards([w_in_halves], [], name="allgather_w_in", collective_id=1)

    def launch_rest(h0b):
        return _allgather_shards([halves(big2[n]) for n in later] + [W[n] for n in CONV], [h0b],
                                 name="allgather_rest", collective_id=2)

    def assemble(rest):
        gathered = {n: whole(g) for n, g in zip(later + CONV, rest)}
        return {
            "w_up": gathered["w_up"],
            "w_a": gathered["w_a"].reshape(D_CONV, D_MODEL), "w_o": gathered["w_o"].reshape(D_MODEL, D_MODEL),
            "w_down": gathered["w_down"].reshape(D_FF, D_MODEL),
            "w_b": gathered["w_b"].transpose(1, 0, 2).reshape(GROUP_W, D_MODEL),
            "conv_w": gathered["conv_w"].transpose(1, 0, 2).reshape(3, D_CONV),
            "ffn_conv_w": gathered["ffn_conv_w"].transpose(1, 0, 2).reshape(3, D_FF),
        }

    pvec = {n: W[n].reshape(1, -1) for n in VECS}

    exchange_ids = iter((3, 4, 5))

    def exchange(group):
        names = tuple(group)
        res = _exchange_grads([group[n] for n in names], name="exchange_" + "_".join(names),
                              collective_id=next(exchange_ids))
        return dict(zip(names, res))

    grad_x, parts, small = _local_step(x2, t2, pvec, (whole(first[0]), launch_rest, assemble), exchange,
                                       before_ln0=[w_in_halves])
    out = {}
    for n in BIG:
        tr = {"w_in": 128, "w_up": 128, "w_b": 128}.get(n, big2[n].shape[0] // 4)
        g, d, nm, nv = _reduce_adamw(parts[n], big2[n], Mo[n][0], Vo[n][0], tr=tr, name="adamw_" + n)
        out[n] = tuple(a[None] for a in (g, d, nm, nv))

    vec, offs = _pack([small[n] for n in SMALL_ORDER])
    off = dict(zip(SMALL_ORDER, offs))
    row = lambda a: a.reshape(1, -1)
    allv = _allgather_small(vec, parts["w_in"])
    tot, vec_out = _reduce_adamw_vectors(allv, [off[n] for n in VECS], [row(W[n]) for n in VECS],
                                         [row(Mo[n]) for n in VECS], [row(Vo[n]) for n in VECS])
    for n, res in zip(VECS, vec_out):
        out[n] = tuple(a.reshape(W[n].shape) for a in res)
    loss = tot[0, off["loss"]]
    chip = 2 * lax.axis_index("x") + lax.axis_index("y")
    taps_g = []
    for n in CONV:
        width = W[n].shape[2]
        full = lax.slice(tot, (0, off[n]), (1, off[n] + 3 * N_CHIPS * width)).reshape(3, N_CHIPS * width)
        taps_g.append(lax.dynamic_slice_in_dim(full, chip * width, width, axis=1))
    taps_out = _adamw_taps([W[n][0] for n in CONV], taps_g, [Mo[n][0] for n in CONV], [Vo[n][0] for n in CONV])
    for n, g, res in zip(CONV, taps_g, taps_out):
        out[n] = tuple(a[None] for a in (g,) + res)

    res = [loss, grad_x[None]]
    for k in range(4):
        res += [out[n][k] for n in ORDER]
    return tuple(res)


def kernel(x, ln0_g, ln0_b, w_in, b_in, conv_w, w_a, w_b, w_o, b_o, ln1_g, ln1_b, w_up, b_up, ffn_conv_w, ffn_conv_b, w_down, b_down, ln2_g, ln2_b, loss_target, m_ln0_g, m_ln0_b, m_w_in, m_b_in, m_conv_w, m_w_a, m_w_b, m_w_o, m_b_o, m_ln1_g, m_ln1_b, m_w_up, m_b_up, m_ffn_conv_w, m_ffn_conv_b, m_w_down, m_b_down, m_ln2_g, m_ln2_b, v_ln0_g, v_ln0_b, v_w_in, v_b_in, v_conv_w, v_w_a, v_w_b, v_w_o, v_b_o, v_ln1_g, v_ln1_b, v_w_up, v_b_up, v_ffn_conv_w, v_ffn_conv_b, v_w_down, v_b_down, v_ln2_g, v_ln2_b):
    W = dict(zip(ORDER, (ln0_g, ln0_b, w_in, b_in, conv_w, w_a, w_b, w_o, b_o, ln1_g, ln1_b, w_up, b_up,
                         ffn_conv_w, ffn_conv_b, w_down, b_down, ln2_g, ln2_b)))
    Mo = dict(zip(ORDER, (m_ln0_g, m_ln0_b, m_w_in, m_b_in, m_conv_w, m_w_a, m_w_b, m_w_o, m_b_o, m_ln1_g, m_ln1_b,
                          m_w_up, m_b_up, m_ffn_conv_w, m_ffn_conv_b, m_w_down, m_b_down, m_ln2_g, m_ln2_b)))
    Vo = dict(zip(ORDER, (v_ln0_g, v_ln0_b, v_w_in, v_b_in, v_conv_w, v_w_a, v_w_b, v_w_o, v_b_o, v_ln1_g, v_ln1_b,
                          v_w_up, v_b_up, v_ffn_conv_w, v_ffn_conv_b, v_w_down, v_b_down, v_ln2_g, v_ln2_b)))
    return _step(x, loss_target, W, Mo, Vo)
```

```python
import functools
import math

import jax
import jax.numpy as jnp
from jax import lax
from jax.experimental import pallas as pl
from jax.experimental.pallas import tpu as pltpu
from jax.experimental.pallas import tpu_sc as plsc

F32 = jnp.float32
BF16 = jnp.bfloat16

D_MODEL = 1024
D_CONV = D_MODEL
HEAD_DIM = 64
HEADS_PER_GROUP = 8
GROUPS = ((128, 1), (512, 4), (2048, 16))
N_GROUPS = len(GROUPS)
GROUP_W = HEADS_PER_GROUP * HEAD_DIM
QKV_W = N_GROUPS * GROUP_W
RADIUS = 64
D_FF = 2816
LN_EPS = 1e-5
ALPHA = 2.0 ** 0.25
MASK_VALUE = -1e30
ATT_SCALE = HEAD_DIM ** -0.5
OFF_B = 0
OFF_C = OFF_B + D_CONV
OFF_H = OFF_C + D_CONV
OFF_Q = OFF_H + D_CONV
OFF_K = OFF_Q + QKV_W
OFF_V = OFF_K + QKV_W
OFF_GA = OFF_V + QKV_W
OFF_GB = OFF_GA + D_MODEL
N_IN = OFF_GB + D_MODEL
ADAM_LR = 0.001
ADAM_B1 = 0.9
ADAM_B2 = 0.999
ADAM_EPS = 1e-08
ADAM_WD = 0.01
ADAM_STEP = 10
INV_SQRT2 = 0.7071067811865476
INV_SQRT_2PI = 0.3989422804014327

LANES = 128
SUBLANES = 8
VMEM_BYTES_V7X = 64 * 1024 * 1024
N_CHIPS = 4
N_CORES = 2
N_DEV = N_CHIPS * N_CORES
MESH = pl.DeviceIdType.MESH

N_BLK = N_IN // GROUP_W
PERM = (0, 1, 2, 3, 4, 5, 15, 16, 17, 18, 6, 9, 12, 7, 10, 13, 8, 11, 14)
INV_PERM = tuple(PERM.index(b) for b in range(N_BLK))
P_B, P_C, P_H, P_GA, P_GB, P_Q0 = 0, 1024, 2048, 3072, 4096, 5120
N_NAT = P_Q0 + QKV_W // N_GROUPS * 3
N_GATED = P_Q0

def _col_runs():
    shard_w = N_IN // N_CHIPS
    runs = []
    for pos, blk in enumerate(PERM):
        c, end = blk * GROUP_W, (blk + 1) * GROUP_W
        while c < end:
            stop = min(end, (c // shard_w + 1) * shard_w)
            runs.append((c // shard_w, c % shard_w, pos * GROUP_W + c - blk * GROUP_W, stop - c))
            c = stop
    return runs


SLAB = 128
CHUNK = 256
PAD = SUBLANES
TQ = 128


def _cparams(sem, vmem_mb):
    assert vmem_mb * 1024 * 1024 < VMEM_BYTES_V7X
    return pltpu.CompilerParams(dimension_semantics=sem, vmem_limit_bytes=vmem_mb * 1024 * 1024)


def _resident(shape):
    nd = len(shape)
    return pl.BlockSpec(shape, lambda *_: (0,) * nd, pipeline_mode=pl.Buffered(1))


def _hbm(*arrays):
    return [pltpu.with_memory_space_constraint(a, pltpu.HBM) for a in arrays]


def _dot(a, b):
    return jnp.dot(a, b, preferred_element_type=F32)


def _dot_nt(a, b):
    return lax.dot_general(a, b, (((1,), (1,)), ((), ())), preferred_element_type=F32)


def _dot_tn(a, b):
    return lax.dot_general(a, b, (((0,), (0,)), ((), ())), preferred_element_type=F32)


def _ln_stats(z):
    mu = jnp.mean(z, -1, keepdims=True)
    zc = z - mu
    var = jnp.mean(zc * zc, -1, keepdims=True)
    rstd = lax.rsqrt(var + LN_EPS)
    return zc * rstd, rstd


def _ln_bwd(dh, xhat, rstd, g):
    dxh = dh * g
    m1 = jnp.mean(dxh, -1, keepdims=True)
    m2 = jnp.mean(dxh * xhat, -1, keepdims=True)
    return rstd * (dxh - m1 - xhat * m2)


def _rows8(rows, width):
    pad = [jnp.zeros((1, width), F32)] * (SUBLANES - len(rows))
    return jnp.concatenate(list(rows) + pad, axis=0)


def _mm_nn(a, w, bias, *, tm, tn, out_dtype, name, vmem_mb=40):
    M, K = a.shape
    if w.ndim == 3:
        assert w.shape[2] == tn
        n_tiles = w.shape[0]
        w_spec = pl.BlockSpec((None, K, tn), lambda j, i: (j, 0, 0))
    else:
        n_tiles = w.shape[1] // tn
        w_spec = pl.BlockSpec((K, tn), lambda j, i: (0, j))

    def body(a_ref, w_ref, b_ref, o_ref):
        o_ref[...] = (_dot(a_ref[...], w_ref[...]) + b_ref[...]).astype(o_ref.dtype)

    return pl.pallas_call(
        body, grid=(n_tiles, M // tm),
        in_specs=[pl.BlockSpec((tm, K), lambda j, i: (i, 0)), w_spec, pl.BlockSpec((1, tn), lambda j, i: (0, j))],
        out_specs=pl.BlockSpec((tm, tn), lambda j, i: (i, j)),
        out_shape=jax.ShapeDtypeStruct((M, n_tiles * tn), out_dtype),
        name=name, compiler_params=_cparams(("arbitrary", "parallel"), vmem_mb))(*_hbm(a, w, bias))


def _mm_nt(a, w, *, tm, a_col=0, name, vmem_mb=40):
    M = a.shape[0]
    N, K = w.shape

    def body(a_ref, w_ref, o_ref):
        o_ref[...] = _dot_nt(a_ref[...], w_ref[...]).astype(o_ref.dtype)

    return pl.pallas_call(
        body, grid=(M // tm,),
        in_specs=[pl.BlockSpec((tm, K), lambda i: (i, a_col)),
                  pl.BlockSpec((N, K), lambda i: (0, 0))],
        out_specs=pl.BlockSpec((tm, N), lambda i: (i, 0)),
        out_shape=jax.ShapeDtypeStruct((M, N), BF16),
        name=name, compiler_params=_cparams(("parallel",), vmem_mb))(*_hbm(a, w))


def _mm_tn(a, g, *, n_out, tn, ts, g_block, g_map, colsum=False, name, vmem_mb=48):
    S, K = a.shape
    n_s = S // ts

    def body(a_ref, g_ref, *rest):
        if colsum:
            o_ref, cs_ref, acc_ref, cacc_ref = rest
        else:
            o_ref, acc_ref = rest
        s = pl.program_id(1)

        @pl.when(s == 0)
        def _():
            acc_ref[...] = jnp.zeros_like(acc_ref)
            if colsum:
                cacc_ref[...] = jnp.zeros_like(cacc_ref)

        gv = g_ref[...]
        acc_ref[...] += _dot_tn(a_ref[...], gv)
        if colsum:
            cacc_ref[...] += jnp.broadcast_to(jnp.sum(gv.astype(F32), axis=0, keepdims=True), cacc_ref.shape)

        @pl.when(s == n_s - 1)
        def _():
            o_ref[...] = acc_ref[...].astype(o_ref.dtype)
            if colsum:
                cs_ref[...] = cacc_ref[...]

    out_specs = [pl.BlockSpec((None, K, tn), lambda j, s: (j, 0, 0))]
    out_shape = [jax.ShapeDtypeStruct((n_out, K, tn), BF16)]
    scratch = [pltpu.VMEM((K, tn), F32)]
    if colsum:
        out_specs.append(pl.BlockSpec((SUBLANES, tn), lambda j, s: (0, j)))
        out_shape.append(jax.ShapeDtypeStruct((SUBLANES, n_out * tn), F32))
        scratch.append(pltpu.VMEM((SUBLANES, tn), F32))
    res = pl.pallas_call(
        body, grid=(n_out, n_s),
        in_specs=[pl.BlockSpec((ts, K), lambda j, s: (s, 0)), pl.BlockSpec(g_block, g_map)],
        out_specs=out_specs, out_shape=out_shape, scratch_shapes=scratch,
        name=name, compiler_params=_cparams(("parallel", "arbitrary"), vmem_mb))(*_hbm(a, g))
    return res if colsum else res[0]


def _mm_tn_cat(a, gs, *, ts, name, vmem_mb=40):
    S, K = a.shape
    widths = [g.shape[1] for g in gs]
    n_s, total = S // ts, sum(widths)

    def body(*refs):
        a_ref, g_refs = refs[0], refs[1:1 + len(gs)]
        o_ref, cs_ref, acc_ref, cacc_ref = refs[1 + len(gs):]
        s = pl.program_id(0)

        @pl.when(s == 0)
        def _():
            acc_ref[...] = jnp.zeros_like(acc_ref)
            cacc_ref[...] = jnp.zeros_like(cacc_ref)

        av, col = a_ref[...], 0
        for g_ref, w in zip(g_refs, widths):
            gv = g_ref[...]
            acc_ref[:, col:col + w] += _dot_tn(av, gv)
            cacc_ref[:, col:col + w] += jnp.broadcast_to(jnp.sum(gv.astype(F32), axis=0, keepdims=True), (SUBLANES, w))
            col += w

        @pl.when(s == n_s - 1)
        def _():
            o_ref[...] = acc_ref[...].astype(BF16)
            cs_ref[...] = cacc_ref[...]

    return pl.pallas_call(
        body, grid=(n_s,),
        in_specs=[pl.BlockSpec((ts, K), lambda s: (s, 0))] + [pl.BlockSpec((ts, w), lambda s: (s, 0)) for w in widths],
        out_specs=[pl.BlockSpec((K, total), lambda s: (0, 0)), pl.BlockSpec((SUBLANES, total), lambda s: (0, 0))],
        out_shape=[jax.ShapeDtypeStruct((K, total), BF16), jax.ShapeDtypeStruct((SUBLANES, total), F32)],
        scratch_shapes=[pltpu.VMEM((K, total), F32), pltpu.VMEM((SUBLANES, total), F32)],
        name=name, compiler_params=_cparams(("arbitrary",), vmem_mb))(*_hbm(a, *gs))


DILS = tuple(d for _, d in GROUPS if d > 1)


def _res_spec(d, tm, width):
    return pl.BlockSpec((d, tm // d, width), lambda i: (0, i, 0))


def _lane_scratch(tm, width):
    return [pltpu.VMEM((tm, LANES), F32)] * (width // LANES)


def _to_residue(val, dst_refs, dils, tm, dtype, scr):
    for c, ref in enumerate(scr):
        ref[...] = val[:, c * LANES:(c + 1) * LANES]
    for dst_ref, d in zip(dst_refs, dils):
        for r in range(d):
            cols = [ref[pl.ds(r, tm // d, stride=d), :] for ref in scr]
            dst_ref[r] = jnp.concatenate(cols, axis=1).astype(dtype)


def _from_residue(rows_of, d, tm, scr):
    for r in range(d):
        v = rows_of(r).astype(F32)
        for c, ref in enumerate(scr):
            ref[pl.ds(r, tm // d, stride=d), :] = v[:, c * LANES:(c + 1) * LANES]
    return jnp.concatenate([ref[...] for ref in scr], axis=1)


def _ln0_fwd(x, g, b, after=(), *, tm=512):
    S, Dm = x.shape
    n_after = len(after)

    def body(x_ref, g_ref, b_ref, *rest):
        h_ref, hb_ref, *rest = rest[n_after:]
        xhat, _ = _ln_stats(x_ref[...])
        h = xhat * g_ref[...] + b_ref[...]
        h_ref[...] = h
        hb_ref[...] = h.astype(BF16)
        _to_residue(h, rest[:len(DILS)], DILS, tm, BF16, rest[len(DILS):])

    row = pl.BlockSpec((tm, Dm), lambda i: (i, 0))
    vec = pl.BlockSpec((1, Dm), lambda i: (0, 0))
    return pl.pallas_call(
        body, grid=(S // tm,), in_specs=[row, vec, vec] + [pl.BlockSpec(memory_space=pl.ANY)] * n_after,
        out_specs=[row, row] + [_res_spec(d, tm, Dm) for d in DILS],
        out_shape=[jax.ShapeDtypeStruct((S, Dm), F32), jax.ShapeDtypeStruct((S, Dm), BF16)]
        + [jax.ShapeDtypeStruct((d, S // d, Dm), BF16) for d in DILS],
        scratch_shapes=_lane_scratch(tm, Dm),
        name="ln0_fwd", compiler_params=_cparams(("parallel",), 32))(*_hbm(x, g, b), *after)


def _slab_spec(S, col0):
    return pl.BlockSpec((S, SLAB), lambda j: (0, col0 // SLAB + j))


def _zero_pads(scr, S):
    scr[0:PAD, :] = jnp.zeros((PAD, SLAB), F32)
    scr[S + PAD:S + 2 * PAD, :] = jnp.zeros((PAD, SLAB), F32)


def _shifted(scr, t):
    return (scr[PAD - 1 + t:PAD - 1 + t + CHUNK, :], scr[PAD + t:PAD + t + CHUNK, :],
            scr[PAD + 1 + t:PAD + 1 + t + CHUNK, :])


def _conv_gate_fwd(proj, conv_w):
    S = proj.shape[0]

    def body(b_ref, c_ref, h_ref, w_ref, o_ref, u_scr):
        _zero_pads(u_scr, S)
        for t in range(0, S, CHUNK):
            u_scr[PAD + t:PAD + t + CHUNK, :] = c_ref[t:t + CHUNK, :].astype(F32) * h_ref[t:t + CHUNK, :].astype(F32)
        w0, w1, w2 = w_ref[0:1, :], w_ref[1:2, :], w_ref[2:3, :]
        for t in range(0, S, CHUNK):
            um, u0, up = _shifted(u_scr, t)
            cv = w0 * um + w1 * u0 + w2 * up
            o_ref[t:t + CHUNK, :] = (b_ref[t:t + CHUNK, :].astype(F32) * cv).astype(BF16)

    return pl.pallas_call(
        body, grid=(D_CONV // SLAB,),
        in_specs=[_slab_spec(S, P_B), _slab_spec(S, P_C), _slab_spec(S, P_H),
                  pl.BlockSpec((3, SLAB), lambda j: (0, j))],
        out_specs=pl.BlockSpec((S, SLAB), lambda j: (0, j)),
        out_shape=jax.ShapeDtypeStruct((S, D_CONV), BF16),
        scratch_shapes=[pltpu.VMEM((S + 2 * PAD, SLAB), F32)],
        name="conv_gate_fwd", compiler_params=_cparams(("parallel",), 40))(*_hbm(proj, proj, proj, conv_w))


MASKED_DISTANCE = -1e34


def _attn_bias_table(g):
    dil = GROUPS[g][1]
    j = lax.broadcasted_iota(jnp.int32, (2 * TQ, TQ), 0)
    a = lax.broadcasted_iota(jnp.int32, (2 * TQ, TQ), 1)
    rel = jnp.abs(j - RADIUS - a)
    base = -(rel * dil).astype(F32)
    inside, after_start, before_end = rel <= RADIUS, j >= RADIUS, j < TQ + RADIUS
    variants = []
    for first, last in ((False, False), (True, False), (False, True), (True, True)):
        valid = inside & (after_start if first else True) & (before_end if last else True)
        variants.append(jnp.where(valid, base, MASKED_DISTANCE))
    return jnp.stack(variants)


SUBS = 4
TB = SUBS * TQ


def _ext_window(p_ref, c_ref, n_ref):
    return jnp.concatenate([p_ref[TB - RADIUS:, :], c_ref[...], n_ref[:RADIUS, :]], axis=0)


def _head_stats(rows):
    pad = jnp.zeros((LANES - len(rows), TQ), F32)
    return jnp.concatenate(list(rows) + [pad], axis=0).T


def _slope(g, h):
    return 2.0 ** (-8.0 * (g * HEADS_PER_GROUP + h + 1) / (N_GROUPS * HEADS_PER_GROUP))


def _pair(a, h):
    return a[:, (h // 2) * LANES:(h // 2 + 1) * LANES]


def _own_lanes(a, h):
    lane = lax.broadcasted_iota(jnp.int32, a.shape, 1)
    return jnp.where((lane >= HEAD_DIM) == (h % 2 == 1), a, jnp.zeros_like(a))


def _own_rows(a, h):
    return a[(h % 2) * HEAD_DIM:(h % 2 + 1) * HEAD_DIM, :]


def _attn_fwd(qkv, col0, g):
    dil, sub, _ = qkv.shape
    nb = sub // TB
    heads = HEADS_PER_GROUP

    def body(q_ref, kp, kc, kn, vp, vc, vn, bias_ref, o_ref, lse_ref, ot_scr, s_scr, p_scr):
        i = pl.program_id(1)
        kext = _ext_window(kp, kc, kn)
        vext = _ext_window(vp, vc, vn)
        q = q_ref[...] * ATT_SCALE
        for b in range(SUBS):
            kwin, qb = kext[b * TQ:(b + 2) * TQ, :], q[b * TQ:(b + 1) * TQ, :]
            for h in range(heads):
                s_scr[b * heads + h] = _dot_nt(_pair(kwin, h), _own_lanes(_pair(qb, h), h))
        inv_den = []
        for b in range(SUBS):
            block = i * SUBS + b
            bias = bias_ref[jnp.where(block == 0, 1, 0) + jnp.where(block == nb * SUBS - 1, 2, 0)]
            lse = []
            for h in range(heads):
                s = s_scr[b * heads + h] + _slope(g, h) * bias
                m = jnp.max(s, axis=0, keepdims=True)
                p = jnp.exp(s - m)
                den = jnp.sum(p, axis=0, keepdims=True)
                p_scr[b * heads + h] = p.astype(BF16)
                inv_den.append(1.0 / den)
                lse.append(m + jnp.log(den))
            lse_ref[b * TQ:(b + 1) * TQ, :] = _head_stats(lse)
        for b in range(SUBS):
            vwin = vext[b * TQ:(b + 2) * TQ, :]
            for h in range(heads):
                ot = _dot_tn(_pair(vwin, h), p_scr[b * heads + h])
                ot_scr[h * HEAD_DIM:(h + 1) * HEAD_DIM, b * TQ:(b + 1) * TQ] = _own_rows(ot, h) * inv_den[b * heads + h]
        o_ref[...] = ot_scr[...].T

    def spec(col, shift):
        return pl.BlockSpec((None, TB, GROUP_W), lambda r, i: (r, jnp.clip(i + shift, 0, nb - 1), col))

    return pl.pallas_call(
        body, grid=(dil, nb),
        in_specs=[spec(col0, 0), spec(col0 + 1, -1), spec(col0 + 1, 0), spec(col0 + 1, 1),
                  spec(col0 + 2, -1), spec(col0 + 2, 0), spec(col0 + 2, 1),
                  pl.BlockSpec((4, 2 * TQ, TQ), lambda r, i: (0, 0, 0))],
        out_specs=[pl.BlockSpec((None, TB, GROUP_W), lambda r, i: (r, i, 0)),
                   pl.BlockSpec((None, TB, LANES), lambda r, i: (r, i, 0))],
        out_shape=[jax.ShapeDtypeStruct((dil, sub, GROUP_W), F32), jax.ShapeDtypeStruct((dil, sub, LANES), F32)],
        scratch_shapes=[pltpu.VMEM((GROUP_W, TB), F32), pltpu.VMEM((SUBS * heads, 2 * TQ, TQ), F32),
                        pltpu.VMEM((SUBS * heads, 2 * TQ, TQ), BF16)],
        name=f"attn_fwd_g{g}", compiler_params=_cparams(("parallel", "arbitrary"), 32))(
            *_hbm(*([qkv] * 7), _attn_bias_table(g)))


def _expand_heads():
    h = lax.broadcasted_iota(jnp.int32, (LANES, GROUP_W), 0)
    c = lax.broadcasted_iota(jnp.int32, (LANES, GROUP_W), 1)
    return (c // HEAD_DIM == h).astype(F32)


def _dot_f32(a, b):
    return jnp.dot(a, b, preferred_element_type=F32, precision=lax.Precision.HIGH)


def _attn_combine(outs, lses, *, tm=512):
    S = outs[0].shape[1]
    n_col = GROUP_W // LANES

    def body(*refs):
        ins, e_ref = refs[:2 * N_GROUPS], refs[2 * N_GROUPS]
        c_ref, cb_ref, lt_ref = refs[2 * N_GROUPS + 1:2 * N_GROUPS + 4]
        scr = refs[2 * N_GROUPS + 4:]
        o, l = [ins[0][0]], [ins[N_GROUPS][0]]
        for k, d in enumerate(DILS):
            o_ref, l_ref = ins[1 + k], ins[N_GROUPS + 1 + k]
            o.append(_from_residue(lambda r: o_ref[r], d, tm, scr[k * (n_col + 1):k * (n_col + 1) + n_col]))
            l.append(_from_residue(lambda r: l_ref[r], d, tm, scr[k * (n_col + 1) + n_col:(k + 1) * (n_col + 1)]))
        m = jnp.maximum(jnp.maximum(l[0], l[1]), l[2])
        e = [jnp.exp(v - m) for v in l]
        den = e[0] + e[1] + e[2]
        comb = sum(_dot_f32(ev / den, e_ref[...]) * ov for ev, ov in zip(e, o))
        c_ref[...] = comb
        cb_ref[...] = comb.astype(BF16)
        lt_ref[...] = m + jnp.log(den)

    row = pl.BlockSpec((tm, GROUP_W), lambda i: (i, 0))
    dils = [d for _, d in GROUPS]
    return pl.pallas_call(
        body, grid=(S // tm,),
        in_specs=[_res_spec(d, tm, GROUP_W) for d in dils] + [_res_spec(d, tm, LANES) for d in dils]
        + [_resident((LANES, GROUP_W))],
        out_specs=[row, row, pl.BlockSpec((tm, LANES), lambda i: (i, 0))],
        out_shape=[jax.ShapeDtypeStruct((S, GROUP_W), F32), jax.ShapeDtypeStruct((S, GROUP_W), BF16),
                   jax.ShapeDtypeStruct((S, LANES), F32)],
        scratch_shapes=_lane_scratch(tm, GROUP_W + LANES) * len(DILS),
        name="attn_combine", compiler_params=_cparams(("parallel",), 32))(*_hbm(*outs, *lses, _expand_heads()))


def _branch_mix(ya_in, comb_b, w_a, w_b, proj, *, tm=512):
    S = ya_in.shape[0]

    def body(ya_ref, cb_ref, wa_ref, wb_ref, ga_ref, gb_ref, yab_ref, mx_ref):
        y_a = _dot(ya_ref[...], wa_ref[...])
        y_b = _dot(cb_ref[...], wb_ref[...])
        yab_ref[:, 0:D_MODEL] = y_a.astype(BF16)
        yab_ref[:, D_MODEL:2 * D_MODEL] = y_b.astype(BF16)
        mx = jax.nn.sigmoid(ga_ref[...].astype(F32)) * y_a + jax.nn.sigmoid(gb_ref[...].astype(F32)) * y_b
        mx_ref[...] = mx.astype(BF16)

    return pl.pallas_call(
        body, grid=(S // tm,),
        in_specs=[pl.BlockSpec((tm, D_CONV), lambda i: (i, 0)), pl.BlockSpec((tm, GROUP_W), lambda i: (i, 0)),
                  pl.BlockSpec((D_CONV, D_MODEL), lambda i: (0, 0)), pl.BlockSpec((GROUP_W, D_MODEL), lambda i: (0, 0)),
                  pl.BlockSpec((tm, D_MODEL), lambda i: (i, P_GA // D_MODEL)),
                  pl.BlockSpec((tm, D_MODEL), lambda i: (i, P_GB // D_MODEL))],
        out_specs=[pl.BlockSpec((tm, 2 * D_MODEL), lambda i: (i, 0)), pl.BlockSpec((tm, D_MODEL), lambda i: (i, 0))],
        out_shape=[jax.ShapeDtypeStruct((S, 2 * D_MODEL), BF16), jax.ShapeDtypeStruct((S, D_MODEL), BF16)],
        name="branch_mix", compiler_params=_cparams(("parallel",), 40))(*_hbm(ya_in, comb_b, w_a, w_b, proj, proj))


def _mix_ln1(mixin, w_o, b_o, h0, g1, b1, *, tm=512):
    S = mixin.shape[0]

    def body(mx_ref, wo_ref, bo_ref, h0_ref, g_ref, b_ref, xh_ref, rs_ref, h1b_ref):
        z = ALPHA * h0_ref[...] + _dot(mx_ref[...], wo_ref[...]) + bo_ref[...]
        xhat, rstd = _ln_stats(z)
        xh_ref[...] = xhat
        rs_ref[...] = jnp.broadcast_to(rstd, (tm, LANES))
        h1b_ref[...] = (xhat * g_ref[...] + b_ref[...]).astype(BF16)

    row = pl.BlockSpec((tm, D_MODEL), lambda i: (i, 0))
    vec = pl.BlockSpec((1, D_MODEL), lambda i: (0, 0))
    return pl.pallas_call(
        body, grid=(S // tm,),
        in_specs=[row, pl.BlockSpec((D_MODEL, D_MODEL), lambda i: (0, 0)), vec, row, vec, vec],
        out_specs=[row, pl.BlockSpec((tm, LANES), lambda i: (i, 0)), row],
        out_shape=[jax.ShapeDtypeStruct((S, D_MODEL), F32), jax.ShapeDtypeStruct((S, LANES), F32),
                   jax.ShapeDtypeStruct((S, D_MODEL), BF16)],
        name="mix_ln1", compiler_params=_cparams(("parallel",), 40))(*_hbm(mixin, w_o, b_o, h0, g1, b1))


def _gelu_parts(cz):
    cdf = 0.5 * (1.0 + lax.erf(cz * INV_SQRT2))
    return cdf, cz * cdf


def _ffn_conv_fwd(up, cw, cb):
    S = up.shape[0]

    def body(a_ref, g_ref, w_ref, cb_ref, o_ref, a_scr):
        _zero_pads(a_scr, S)
        for t in range(0, S, CHUNK):
            a_scr[PAD + t:PAD + t + CHUNK, :] = a_ref[t:t + CHUNK, :].astype(F32)
        w0, w1, w2 = w_ref[0:1, :], w_ref[1:2, :], w_ref[2:3, :]
        for t in range(0, S, CHUNK):
            am, a0, ap = _shifted(a_scr, t)
            _, gel = _gelu_parts(w0 * am + w1 * a0 + w2 * ap + cb_ref[...])
            o_ref[t:t + CHUNK, :] = (gel * g_ref[t:t + CHUNK, :].astype(F32)).astype(BF16)

    return pl.pallas_call(
        body, grid=(D_FF // SLAB,),
        in_specs=[_slab_spec(S, 0), _slab_spec(S, D_FF), pl.BlockSpec((3, SLAB), lambda j: (0, j)),
                  pl.BlockSpec((1, SLAB), lambda j: (0, j))],
        out_specs=pl.BlockSpec((S, SLAB), lambda j: (0, j)),
        out_shape=jax.ShapeDtypeStruct((S, D_FF), BF16),
        scratch_shapes=[pltpu.VMEM((S + 2 * PAD, SLAB), F32)],
        name="ffn_conv_fwd", compiler_params=_cparams(("parallel",), 40))(*_hbm(up, up, cw, cb))


def _down_ln2_loss(f, w_down, b_down, xhat1, g1, b1, g2, b2, target, *, tm=512):
    S = f.shape[0]

    def body(f_ref, wd_ref, bd_ref, xh1_ref, g1_ref, b1_ref, g2_ref, b2_ref, t_ref, dz_ref, dzb_ref, st_ref):
        h1 = xh1_ref[...] * g1_ref[...] + b1_ref[...]
        z = ALPHA * h1 + _dot(f_ref[...], wd_ref[...]) + bd_ref[...]
        xhat, rstd = _ln_stats(z)
        err = xhat * g2_ref[...] + b2_ref[...] - t_ref[...]
        loss = (0.5 / D_MODEL) * jnp.sum(jnp.sum(err * err, axis=1, keepdims=True), axis=0, keepdims=True)
        dh2 = err * (1.0 / D_MODEL)
        dz = _ln_bwd(dh2, xhat, rstd, g2_ref[...])
        dz_ref[...] = dz
        dzb_ref[...] = dz.astype(BF16)
        upd = _rows8([jnp.sum(dh2 * xhat, axis=0, keepdims=True), jnp.sum(dh2, axis=0, keepdims=True),
                      jnp.broadcast_to(loss, (1, D_MODEL)), jnp.sum(dz, axis=0, keepdims=True)], D_MODEL)

        @pl.when(pl.program_id(0) == 0)
        def _():
            st_ref[...] = upd

        @pl.when(pl.program_id(0) != 0)
        def _():
            st_ref[...] += upd

    row = pl.BlockSpec((tm, D_MODEL), lambda i: (i, 0))
    vec = pl.BlockSpec((1, D_MODEL), lambda i: (0, 0))
    return pl.pallas_call(
        body, grid=(S // tm,),
        in_specs=[pl.BlockSpec((tm, D_FF), lambda i: (i, 0)), _resident((D_FF, D_MODEL)),
                  vec, row, vec, vec, vec, vec, row],
        out_specs=[row, row, pl.BlockSpec((SUBLANES, D_MODEL), lambda i: (0, 0))],
        out_shape=[jax.ShapeDtypeStruct((S, D_MODEL), F32), jax.ShapeDtypeStruct((S, D_MODEL), BF16),
                   jax.ShapeDtypeStruct((SUBLANES, D_MODEL), F32)],
        name="down_ln2_loss", compiler_params=_cparams(("arbitrary",), 56))(
            *_hbm(f, w_down, b_down, xhat1, g1, b1, g2, b2, target))


def _ffn_conv_bwd(up, df, cw, cb):
    S = up.shape[0]

    def body(a_ref, g_ref, df_ref, w_ref, cb_ref, dup_ref, sm_ref, a_scr, d_scr):
        _zero_pads(a_scr, S)
        _zero_pads(d_scr, S)
        for t in range(0, S, CHUNK):
            a_scr[PAD + t:PAD + t + CHUNK, :] = a_ref[t:t + CHUNK, :].astype(F32)
        w0, w1, w2 = w_ref[0:1, :], w_ref[1:2, :], w_ref[2:3, :]
        zero = jnp.zeros((1, SLAB), F32)
        s_dg, s_dcz, s_w0, s_w1, s_w2 = zero, zero, zero, zero, zero
        for t in range(0, S, CHUNK):
            am, a0, ap = _shifted(a_scr, t)
            cz = w0 * am + w1 * a0 + w2 * ap + cb_ref[...]
            cdf, gel = _gelu_parts(cz)
            dfv = df_ref[t:t + CHUNK, :].astype(F32)
            dgte = dfv * gel
            dcz = dfv * g_ref[t:t + CHUNK, :].astype(F32) * (cdf + cz * jnp.exp(-0.5 * cz * cz) * INV_SQRT_2PI)
            dup_ref[1, t:t + CHUNK, :] = dgte.astype(BF16)
            d_scr[PAD + t:PAD + t + CHUNK, :] = dcz
            s_dg = s_dg + jnp.sum(dgte, axis=0, keepdims=True)
            s_dcz = s_dcz + jnp.sum(dcz, axis=0, keepdims=True)
            s_w0 = s_w0 + jnp.sum(dcz * am, axis=0, keepdims=True)
            s_w1 = s_w1 + jnp.sum(dcz * a0, axis=0, keepdims=True)
            s_w2 = s_w2 + jnp.sum(dcz * ap, axis=0, keepdims=True)
        s_da = zero
        for t in range(0, S, CHUNK):
            dm, d0, dp = _shifted(d_scr, t)
            da = w0 * dp + w1 * d0 + w2 * dm
            dup_ref[0, t:t + CHUNK, :] = da.astype(BF16)
            s_da = s_da + jnp.sum(da, axis=0, keepdims=True)
        sm_ref[...] = _rows8([s_da, s_dg, s_dcz, s_w0, s_w1, s_w2], SLAB)

    return pl.pallas_call(
        body, grid=(D_FF // SLAB,),
        in_specs=[_slab_spec(S, 0), _slab_spec(S, D_FF), pl.BlockSpec((S, SLAB), lambda j: (0, j)),
                  pl.BlockSpec((3, SLAB), lambda j: (0, j)), pl.BlockSpec((1, SLAB), lambda j: (0, j))],
        out_specs=[pl.BlockSpec((2, S, SLAB), lambda j: (0, 0, j)), pl.BlockSpec((SUBLANES, SLAB), lambda j: (0, j))],
        out_shape=[jax.ShapeDtypeStruct((2, S, D_FF), BF16), jax.ShapeDtypeStruct((SUBLANES, D_FF), F32)],
        scratch_shapes=[pltpu.VMEM((S + 2 * PAD, SLAB), F32)] * 2,
        name="ffn_conv_bwd", compiler_params=_cparams(("parallel",), 48))(*_hbm(up, up, df, cw, cb))


def _up_bwd_ln1(dup, w_up3, dz2, xhat1, rstd1, g1, *, tm=512):
    S = dz2.shape[0]
    ns, _, tk = w_up3.shape
    per_plane = D_FF // tk

    def body(du_ref, w_ref, dz2_ref, xh_ref, rs_ref, g_ref, dz_ref, dzb_ref, st_ref):
        dh = ALPHA * dz2_ref[...]
        for k in range(ns):
            col = (k % per_plane) * tk
            dh = dh + _dot_nt(du_ref[k // per_plane, :, col:col + tk], w_ref[k])
        xhat = xh_ref[...]
        dz = _ln_bwd(dh, xhat, rs_ref[:, 0:1], g_ref[...])
        dz_ref[...] = dz
        dzb_ref[...] = dz.astype(BF16)
        upd = _rows8([jnp.sum(dh * xhat, axis=0, keepdims=True), jnp.sum(dh, axis=0, keepdims=True),
                      jnp.sum(dz, axis=0, keepdims=True)], D_MODEL)

        @pl.when(pl.program_id(0) == 0)
        def _():
            st_ref[...] = upd

        @pl.when(pl.program_id(0) != 0)
        def _():
            st_ref[...] += upd

    row = pl.BlockSpec((tm, D_MODEL), lambda i: (i, 0))
    return pl.pallas_call(
        body, grid=(S // tm,),
        in_specs=[pl.BlockSpec((dup.shape[0], tm, D_FF), lambda i: (0, i, 0)), _resident(w_up3.shape),
                  row, row, pl.BlockSpec((tm, LANES), lambda i: (i, 0)), pl.BlockSpec((1, D_MODEL), lambda i: (0, 0))],
        out_specs=[row, row, pl.BlockSpec((SUBLANES, D_MODEL), lambda i: (0, 0))],
        out_shape=[jax.ShapeDtypeStruct((S, D_MODEL), F32), jax.ShapeDtypeStruct((S, D_MODEL), BF16),
                   jax.ShapeDtypeStruct((SUBLANES, D_MODEL), F32)],
        name="up_bwd_ln1", compiler_params=_cparams(("arbitrary",), 56))(*_hbm(dup, w_up3, dz2, xhat1, rstd1, g1))


def _mix_bwd(dz1b, w_o, proj, yab, *, tm=512):
    S = dz1b.shape[0]

    def body(dz_ref, wo_ref, ga_ref, gb_ref, y_ref, dy_ref, dg_ref):
        dmx = _dot_nt(dz_ref[...], wo_ref[...])
        for k, gt_ref in enumerate((ga_ref, gb_ref)):
            sl = slice(k * D_MODEL, (k + 1) * D_MODEL)
            sg = jax.nn.sigmoid(gt_ref[...].astype(F32))
            dy_ref[:, sl] = (dmx * sg).astype(BF16)
            dg_ref[k] = (dmx * y_ref[:, sl].astype(F32) * sg * (1.0 - sg)).astype(BF16)

    row = pl.BlockSpec((tm, D_MODEL), lambda i: (i, 0))
    wide = pl.BlockSpec((tm, 2 * D_MODEL), lambda i: (i, 0))
    return pl.pallas_call(
        body, grid=(S // tm,),
        in_specs=[row, _resident(w_o.shape), pl.BlockSpec((tm, D_MODEL), lambda i: (i, P_GA // D_MODEL)),
                  pl.BlockSpec((tm, D_MODEL), lambda i: (i, P_GB // D_MODEL)), wide],
        out_specs=[wide, pl.BlockSpec((2, tm, D_MODEL), lambda i: (0, i, 0))],
        out_shape=[jax.ShapeDtypeStruct((S, 2 * D_MODEL), BF16), jax.ShapeDtypeStruct((2, S, D_MODEL), BF16)],
        name="mix_bwd", compiler_params=_cparams(("parallel",), 40))(*_hbm(dz1b, w_o, proj, proj, yab))


def _conv_gate_bwd(proj, dya_in, conv_w):
    S = proj.shape[0]

    def body(b_ref, c_ref, h_ref, dy_ref, w_ref, o_ref, sm_ref, u_scr, d_scr):
        _zero_pads(u_scr, S)
        _zero_pads(d_scr, S)
        for t in range(0, S, CHUNK):
            u_scr[PAD + t:PAD + t + CHUNK, :] = c_ref[t:t + CHUNK, :].astype(F32) * h_ref[t:t + CHUNK, :].astype(F32)
        w0, w1, w2 = w_ref[0:1, :], w_ref[1:2, :], w_ref[2:3, :]
        zero = jnp.zeros((1, SLAB), F32)
        s_w0, s_w1, s_w2 = zero, zero, zero
        for t in range(0, S, CHUNK):
            um, u0, up = _shifted(u_scr, t)
            dy = dy_ref[t:t + CHUNK, :].astype(F32)
            o_ref[0, t:t + CHUNK, :] = (dy * (w0 * um + w1 * u0 + w2 * up)).astype(BF16)
            dcv = dy * b_ref[t:t + CHUNK, :].astype(F32)
            d_scr[PAD + t:PAD + t + CHUNK, :] = dcv
            s_w0 = s_w0 + jnp.sum(dcv * um, axis=0, keepdims=True)
            s_w1 = s_w1 + jnp.sum(dcv * u0, axis=0, keepdims=True)
            s_w2 = s_w2 + jnp.sum(dcv * up, axis=0, keepdims=True)
        for t in range(0, S, CHUNK):
            dm, d0, dp = _shifted(d_scr, t)
            du = w0 * dp + w1 * d0 + w2 * dm
            o_ref[1, t:t + CHUNK, :] = (du * h_ref[t:t + CHUNK, :].astype(F32)).astype(BF16)
            o_ref[2, t:t + CHUNK, :] = (du * c_ref[t:t + CHUNK, :].astype(F32)).astype(BF16)
        sm_ref[...] = _rows8([s_w0, s_w1, s_w2], SLAB)

    return pl.pallas_call(
        body, grid=(D_CONV // SLAB,),
        in_specs=[_slab_spec(S, P_B), _slab_spec(S, P_C), _slab_spec(S, P_H),
                  pl.BlockSpec((S, SLAB), lambda j: (0, j)), pl.BlockSpec((3, SLAB), lambda j: (0, j))],
        out_specs=[pl.BlockSpec((3, S, SLAB), lambda j: (0, 0, j)), pl.BlockSpec((SUBLANES, SLAB), lambda j: (0, j))],
        out_shape=[jax.ShapeDtypeStruct((3, S, D_CONV), BF16), jax.ShapeDtypeStruct((SUBLANES, D_CONV), F32)],
        scratch_shapes=[pltpu.VMEM((S + 2 * PAD, SLAB), F32)] * 2,
        name="conv_gate_bwd", compiler_params=_cparams(("parallel",), 48))(*_hbm(proj, proj, proj, dya_in, conv_w))


def _comb_bwd(dyab, w_b, comb, lse_tot, *, tm=512):
    S = comb.shape[0]
    widths, dtypes = (GROUP_W, LANES, LANES), (BF16, F32, F32)

    def body(dy_ref, wb_ref, c_ref, lt_ref, e_ref, *rest):
        outs, scr = rest[:3 * N_GROUPS], rest[3 * N_GROUPS:]
        dcb = _dot_nt(dy_ref[...], wb_ref[...]).astype(BF16)
        dc = dcb.astype(F32)
        delta = lax.dot_general(dc * c_ref[...], e_ref[...], (((1,), (1,)), ((), ())),
                                preferred_element_type=F32, precision=lax.Precision.HIGH)
        for k, (val, dtype) in enumerate(zip((dc, lt_ref[...], delta), dtypes)):
            outs[k][0] = val.astype(dtype)
            _to_residue(val, [outs[3 * (1 + j) + k] for j in range(len(DILS))], DILS, tm, dtype,
                        scr[:val.shape[1] // LANES])

    out_specs, out_shape = [], []
    for _, d in GROUPS:
        out_specs += [_res_spec(d, tm, w) for w in widths]
        out_shape += [jax.ShapeDtypeStruct((d, S // d, w), t) for w, t in zip(widths, dtypes)]
    res = pl.pallas_call(
        body, grid=(S // tm,),
        in_specs=[pl.BlockSpec((tm, D_MODEL), lambda i: (i, 1)), _resident(w_b.shape),
                  pl.BlockSpec((tm, GROUP_W), lambda i: (i, 0)), pl.BlockSpec((tm, LANES), lambda i: (i, 0)),
                  _resident((LANES, GROUP_W))],
        out_specs=out_specs, out_shape=out_shape, scratch_shapes=_lane_scratch(tm, GROUP_W),
        name="comb_bwd", compiler_params=_cparams(("parallel",), 32))(*_hbm(dyab, w_b, comb, lse_tot, _expand_heads()))
    return [tuple(res[3 * g:3 * g + 3]) for g in range(N_GROUPS)]


def _attn_bwd(qkv, col0, g, dcomb, lse_tot, delta):
    dil, sub, _ = qkv.shape
    nb = sub // TB
    heads = HEADS_PER_GROUP

    def body(q_ref, kp, kc, kn, vp, vc, vn, do_ref, lse_ref, dl_ref, bias_ref, dq_ref, dk_ref, dv_ref,
             ak, av, dqt_scr, s_scr, dp_scr, ds_scr, p_scr):
        i = pl.program_id(1)

        @pl.when(i == 0)
        def _():
            ak[...] = jnp.zeros_like(ak)
            av[...] = jnp.zeros_like(av)

        @pl.when(i < nb)
        def _():
            kext = _ext_window(kp, kc, kn)
            vext = _ext_window(vp, vc, vn)
            q = q_ref[...] * ATT_SCALE
            do = do_ref[...]
            lse_t, dl_t = lse_ref[...].T, dl_ref[...].T
            for b in range(SUBS):
                rows = slice(b * TQ, (b + 1) * TQ)
                kwin, vwin = kext[b * TQ:(b + 2) * TQ, :], vext[b * TQ:(b + 2) * TQ, :]
                for h in range(heads):
                    s_scr[b * heads + h] = _dot_nt(_pair(kwin, h), _own_lanes(_pair(q[rows], h), h))
                    dp_scr[b * heads + h] = _dot_nt(_pair(vwin, h), _own_lanes(_pair(do[rows], h), h))
            for b in range(SUBS):
                cols = slice(b * TQ, (b + 1) * TQ)
                block = i * SUBS + b
                bias = bias_ref[jnp.where(block == 0, 1, 0) + jnp.where(block == nb * SUBS - 1, 2, 0)]
                for h in range(heads):
                    k = b * heads + h
                    p = jnp.exp(s_scr[k] + _slope(g, h) * bias - lse_t[h:h + 1, cols])
                    ds_scr[k] = (p * (dp_scr[k] - dl_t[h:h + 1, cols])).astype(BF16)
                    p_scr[k] = p.astype(BF16)
            for b in range(SUBS):
                kwin = kext[b * TQ:(b + 2) * TQ, :]
                for h in range(heads):
                    dqt_scr[h * HEAD_DIM:(h + 1) * HEAD_DIM, b * TQ:(b + 1) * TQ] = _own_rows(
                        _dot_tn(_pair(kwin, h), ds_scr[b * heads + h]), h)
            for b in range(SUBS):
                rows = slice(b * TQ, (b + 1) * TQ)
                acc_rows = slice(TB - RADIUS + b * TQ, TB - RADIUS + (b + 2) * TQ)
                for h in range(0, heads, 2):
                    cols = slice(h * HEAD_DIM, (h + 2) * HEAD_DIM)
                    k = b * heads + h
                    q2 = jnp.concatenate([_own_lanes(_pair(q[rows], h), h), _own_lanes(_pair(q[rows], h), h + 1)], axis=0)
                    do2 = jnp.concatenate([_own_lanes(_pair(do[rows], h), h), _own_lanes(_pair(do[rows], h), h + 1)],
                                          axis=0)
                    ak[acc_rows, cols] += _dot(jnp.concatenate([ds_scr[k], ds_scr[k + 1]], axis=1), q2)
                    av[acc_rows, cols] += _dot(jnp.concatenate([p_scr[k], p_scr[k + 1]], axis=1), do2)
            dq_ref[...] = (dqt_scr[...].T * ATT_SCALE).astype(BF16)

        if nb == 1:
            dk_ref[...] = ak[TB:2 * TB, :].astype(BF16)
            dv_ref[...] = av[TB:2 * TB, :].astype(BF16)
        else:
            dk_ref[...] = ak[0:TB, :].astype(BF16)
            dv_ref[...] = av[0:TB, :].astype(BF16)
            used = 2 * TB + RADIUS
            for acc in (ak, av):
                acc[0:used - TB, :] = acc[TB:used, :]
                acc[used - TB:used, :] = jnp.zeros((TB, GROUP_W), F32)

    def spec(col, shift):
        return pl.BlockSpec((None, TB, GROUP_W), lambda r, i: (r, jnp.clip(i + shift, 0, nb - 1), col))

    tok = pl.BlockSpec((None, TB, GROUP_W), lambda r, i: (r, jnp.minimum(i, nb - 1), 0))
    stat = pl.BlockSpec((None, TB, LANES), lambda r, i: (r, jnp.minimum(i, nb - 1), 0))
    dkv_spec = tok if nb == 1 else pl.BlockSpec((None, TB, GROUP_W), lambda r, i: (r, jnp.maximum(i - 1, 0), 0))
    return pl.pallas_call(
        body, grid=(dil, nb + (nb > 1)),
        in_specs=[spec(col0, 0), spec(col0 + 1, -1), spec(col0 + 1, 0), spec(col0 + 1, 1),
                  spec(col0 + 2, -1), spec(col0 + 2, 0), spec(col0 + 2, 1), tok, stat, stat,
                  pl.BlockSpec((4, 2 * TQ, TQ), lambda r, i: (0, 0, 0))],
        out_specs=[tok, dkv_spec, dkv_spec], out_shape=[jax.ShapeDtypeStruct((dil, sub, GROUP_W), BF16)] * 3,
        scratch_shapes=[pltpu.VMEM((3 * TB, GROUP_W), F32)] * 2 + [pltpu.VMEM((GROUP_W, TB), F32)]
        + [pltpu.VMEM((SUBS * heads, 2 * TQ, TQ), F32)] * 2 + [pltpu.VMEM((SUBS * heads, 2 * TQ, TQ), BF16)] * 2,
        name=f"attn_bwd_g{g}", compiler_params=_cparams(("arbitrary", "arbitrary"), 40))(
            *_hbm(*([qkv] * 7), dcomb, lse_tot, delta, _attn_bias_table(g)))


def _in_bwd_ln0(dgated, dqkv, w_nat, w_dil, dz1, x, g0, *, tm=256):
    S = x.shape[0]
    n_gated, n_in = len(dgated), 3 * N_GROUPS

    def body(*refs):
        g_refs, d_refs = refs[:n_gated], refs[n_gated:n_gated + n_in]
        wn_ref, *wd_refs = refs[n_gated + n_in:n_gated + n_in + N_GROUPS]
        dz_ref, x_ref, g_ref, gx_ref, st_ref, *tmp_ref = refs[n_gated + n_in + N_GROUPS:]
        dh = ALPHA * dz_ref[...]
        col = 0
        for ref in g_refs:
            for k in range(ref.shape[0]):
                dh = dh + _dot_nt(ref[k], wn_ref[:, col:col + D_MODEL])
                col += D_MODEL
        for g, (_, d) in enumerate(GROUPS):
            rows = [jnp.concatenate([d_refs[3 * g + k][r] for k in range(3)], axis=1) for r in range(d)]
            w = wn_ref[:, col:col + QKV_W] if d == 1 else wd_refs[g - 1][...]
            res = _dot_nt(jnp.concatenate(rows, axis=0), w)
            if d == 1:
                dh = dh + res
            else:
                n = tm // d
                dh = dh + _from_residue(lambda r: res[r * n:(r + 1) * n, :], d, tm, tmp_ref)
        xhat, rstd = _ln_stats(x_ref[...])
        gx_ref[...] = _ln_bwd(dh, xhat, rstd, g_ref[...])
        upd = _rows8([jnp.sum(dh * xhat, axis=0, keepdims=True), jnp.sum(dh, axis=0, keepdims=True)], D_MODEL)

        @pl.when(pl.program_id(0) == 0)
        def _():
            st_ref[...] = upd

        @pl.when(pl.program_id(0) != 0)
        def _():
            st_ref[...] += upd

    row = pl.BlockSpec((tm, D_MODEL), lambda i: (i, 0))
    g_specs = [pl.BlockSpec((a.shape[0], tm, D_MODEL), lambda i: (0, i, 0)) for a in dgated]
    d_specs = []
    for _, d in GROUPS:
        d_specs += [_res_spec(d, tm, GROUP_W)] * 3
    operands = list(dgated) + [a for grp in dqkv for a in grp] + [w_nat] + list(w_dil) + [dz1, x, g0]
    return pl.pallas_call(
        body, grid=(S // tm,),
        in_specs=g_specs + d_specs + [_resident(w_nat.shape)] + [_resident(w.shape) for w in w_dil]
        + [row, row, pl.BlockSpec((1, D_MODEL), lambda i: (0, 0))],
        out_specs=[row, pl.BlockSpec((SUBLANES, D_MODEL), lambda i: (0, 0))],
        out_shape=[jax.ShapeDtypeStruct((S, D_MODEL), F32), jax.ShapeDtypeStruct((SUBLANES, D_MODEL), F32)],
        scratch_shapes=_lane_scratch(tm, D_MODEL),
        name="in_bwd_ln0", compiler_params=_cparams(("arbitrary",), 52))(*_hbm(*operands))


HBM_SPEC = pl.BlockSpec(memory_space=pltpu.HBM)


def _place():
    x, y, c = lax.axis_index("x"), lax.axis_index("y"), lax.axis_index("c")
    chips = [(1 - x, y), (x, 1 - y), (1 - x, 1 - y)]
    return x, y, c, chips


def _allgather_shards(shards, after, *, name, collective_id):
    n = len(shards)
    per = 6

    def body(*refs):
        ins, outs = refs[:n], refs[n + len(after):2 * n + len(after)]
        send_sems, recv_sems, loc_sems = refs[2 * n + len(after):]
        x, y, c, chips = _place()
        me = 2 * x + y
        sib = (x, y, 1 - c)
        peers = [sib] + [(px, py, c) for px, py in chips]
        barrier = pltpu.get_barrier_semaphore()
        for peer in peers:
            pl.semaphore_signal(barrier, inc=1, device_id=peer, device_id_type=MESH)
        pl.semaphore_wait(barrier, len(peers))

        def rcopy(w, k, src, dst, to):
            return pltpu.make_async_remote_copy(src_ref=src, dst_ref=dst, send_sem=send_sems.at[per * w + k],
                                                recv_sem=recv_sems.at[per * w + k], device_id=to, device_id_type=MESH)

        split = [s.shape[0] == N_CORES for s in shards]
        half = lambda w: c if split[w] else 0
        local, sends = [], []
        for w in range(n):
            cp = pltpu.make_async_copy(ins[w], outs[w].at[me], loc_sems.at[w])
            cp.start()
            local.append(cp)
            for j, (px, py) in enumerate(chips):
                cp = rcopy(w, j, ins[w].at[half(w)], outs[w].at[me, half(w)], (px, py, c))
                cp.start()
                sends.append(cp)
        for w in range(n):
            for j, (px, py) in enumerate(chips):
                slot = outs[w].at[2 * px + py, half(w)]
                rcopy(w, j, slot, slot, (px, py, c)).wait_recv()
                if split[w]:
                    cp = rcopy(w, 3 + j, slot, slot, sib)
                    cp.start()
                    sends.append(cp)
        for w in range(n):
            if split[w]:
                for j, (px, py) in enumerate(chips):
                    slot = outs[w].at[2 * px + py, 1 - c]
                    rcopy(w, 3 + j, slot, slot, sib).wait_recv()
        for cp in sends:
            cp.wait_send()
        for cp in local:
            cp.wait()

    return pl.kernel(
        body, out_type=[jax.ShapeDtypeStruct((N_CHIPS,) + s.shape, s.dtype) for s in shards],
        mesh=plsc.ScalarSubcoreMesh(axis_name="sequencer", num_cores=1),
        scratch_types=[pltpu.SemaphoreType.DMA((per * n,)), pltpu.SemaphoreType.DMA((per * n,)),
                       pltpu.SemaphoreType.DMA((n,))],
        name=name, compiler_params=pltpu.CompilerParams(collective_id=collective_id))(*shards, *after)


def _exchange_grads(grads, *, name, collective_id):
    n = len(grads)
    per = 7

    def body(*refs):
        ins, outs = refs[:n], refs[n:2 * n]
        send_sems, recv_sems, loc_sems = refs[2 * n:]
        x, y, c, chips = _place()
        me = 2 * x + y
        sib = (x, y, 1 - c)
        peers = [sib] + [(px, py, c) for px, py in chips]
        barrier = pltpu.get_barrier_semaphore()
        for peer in peers:
            pl.semaphore_signal(barrier, inc=1, device_id=peer, device_id_type=MESH)
        pl.semaphore_wait(barrier, len(peers))

        def rcopy(w, k, src, dst, to):
            return pltpu.make_async_remote_copy(src_ref=src, dst_ref=dst, send_sem=send_sems.at[per * w + k],
                                                recv_sem=recv_sems.at[per * w + k], device_id=to, device_id_type=MESH)

        local, sends = [], []
        for w in range(n):
            cp = pltpu.make_async_copy(ins[w].at[me], outs[w].at[c, me], loc_sems.at[w])
            cp.start()
            local.append(cp)
            cp = rcopy(w, 0, ins[w].at[me], outs[w].at[c, me], sib)
            cp.start()
            sends.append(cp)
            for j, (px, py) in enumerate(chips):
                cp = rcopy(w, 1 + j, ins[w].at[2 * px + py], outs[w].at[c, me], (px, py, c))
                cp.start()
                sends.append(cp)
        for w in range(n):
            for j, (px, py) in enumerate(chips):
                slot = outs[w].at[c, 2 * px + py]
                rcopy(w, 1 + j, slot, slot, (px, py, c)).wait_recv()
                cp = rcopy(w, 4 + j, slot, slot, sib)
                cp.start()
                sends.append(cp)
        for w in range(n):
            slot = outs[w].at[1 - c, me]
            rcopy(w, 0, slot, slot, sib).wait_recv()
            for j, (px, py) in enumerate(chips):
                slot = outs[w].at[1 - c, 2 * px + py]
                rcopy(w, 4 + j, slot, slot, sib).wait_recv()
        for cp in sends:
            cp.wait_send()
        for cp in local:
            cp.wait()

    return pl.kernel(
        body, out_type=[jax.ShapeDtypeStruct((N_CORES,) + g.shape, g.dtype) for g in grads],
        mesh=plsc.ScalarSubcoreMesh(axis_name="sequencer", num_cores=1),
        scratch_types=[pltpu.SemaphoreType.DMA((per * n,)), pltpu.SemaphoreType.DMA((per * n,)),
                       pltpu.SemaphoreType.DMA((n,))],
        name=name, compiler_params=pltpu.CompilerParams(collective_id=collective_id))(*grads)


def _allgather_small(vec, after):
    def body(v_ref, _, o_ref, send_sems, recv_sems, loc_sem):
        x, y, c = lax.axis_index("x"), lax.axis_index("y"), lax.axis_index("c")
        me = 4 * x + 2 * y + c

        def peer(k):
            flip = lambda v, bit: 1 - v if (k >> bit) & 1 else v
            return flip(x, 2), flip(y, 1), flip(c, 0)

        loc = pltpu.make_async_copy(v_ref, o_ref.at[me], loc_sem)
        loc.start()
        sends = []
        for k in range(1, N_DEV):
            cp = pltpu.make_async_remote_copy(src_ref=v_ref, dst_ref=o_ref.at[me], send_sem=send_sems.at[k - 1],
                                              recv_sem=recv_sems.at[k - 1], device_id=peer(k), device_id_type=MESH)
            cp.start()
            sends.append(cp)
        for k in range(1, N_DEV):
            px, py, pc = peer(k)
            pltpu.make_async_remote_copy(src_ref=v_ref, dst_ref=o_ref.at[4 * px + 2 * py + pc],
                                         send_sem=send_sems.at[k - 1], recv_sem=recv_sems.at[k - 1],
                                         device_id=(px, py, pc), device_id_type=MESH).wait_recv()
        for cp in sends:
            cp.wait_send()
        loc.wait()

    return pl.pallas_call(
        body, in_specs=[HBM_SPEC, HBM_SPEC], out_specs=HBM_SPEC,
        out_shape=jax.ShapeDtypeStruct((N_DEV,) + vec.shape, vec.dtype),
        scratch_shapes=[pltpu.SemaphoreType.DMA((N_DEV - 1,)), pltpu.SemaphoreType.DMA((N_DEV - 1,)),
                        pltpu.SemaphoreType.DMA],
        name="allgather_small")(vec, after)


def _adamw(w, g, m, v):
    m = ADAM_B1 * m + (1.0 - ADAM_B1) * g
    v = ADAM_B2 * v + (1.0 - ADAM_B2) * (g * g)
    m_hat = m / (1.0 - ADAM_B1 ** ADAM_STEP)
    v_hat = v / (1.0 - ADAM_B2 ** ADAM_STEP)
    delta = -ADAM_LR * (m_hat / (jnp.sqrt(v_hat) + ADAM_EPS) + ADAM_WD * w)
    return delta, m, v


def _reduce_adamw(parts, w, m, v, *, tr, name):
    R, C = w.shape

    def body(p_ref, w_ref, m_ref, v_ref, g_ref, d_ref, nm_ref, nv_ref):
        def core_sum(cc):
            s = p_ref[cc, 0].astype(F32)
            for k in range(1, N_CHIPS):
                s = s + p_ref[cc, k].astype(F32)
            return s

        g = core_sum(0) + core_sum(1)
        delta, nm, nv = _adamw(w_ref[...], g, m_ref[...], v_ref[...])
        g_ref[...] = g
        d_ref[...] = delta
        nm_ref[...] = nm
        nv_ref[...] = nv

    blk = pl.BlockSpec((tr, C), lambda i: (i, 0))
    return pl.pallas_call(
        body, grid=(R // tr,),
        in_specs=[pl.BlockSpec((N_CORES, N_CHIPS, tr, C), lambda i: (0, 0, i, 0)), blk, blk, blk],
        out_specs=[blk] * 4, out_shape=[jax.ShapeDtypeStruct((R, C), F32)] * 4,
        name=name, compiler_params=_cparams(("parallel",), 40))(*_hbm(parts, w, m, v))


def _reduce_adamw_vectors(allv, offs, ws, ms, vs):
    n = len(ws)

    def body(a_ref, *refs):
        w_refs, m_refs, v_refs = refs[:n], refs[n:2 * n], refs[2 * n:3 * n]
        tot_ref, outs = refs[3 * n], refs[3 * n + 1:]
        s = a_ref[0]
        for d in range(1, N_DEV):
            s = s + a_ref[d]
        tot_ref[...] = s
        for k in range(n):
            g = s[:, offs[k]:offs[k] + w_refs[k].shape[1]]
            delta, nm, nv = _adamw(w_refs[k][...], g, m_refs[k][...], v_refs[k][...])
            for ref, val in zip(outs[4 * k:4 * k + 4], (g, delta, nm, nv)):
                ref[...] = val

    out_shape = [jax.ShapeDtypeStruct(allv.shape[1:], F32)]
    for w in ws:
        out_shape += [jax.ShapeDtypeStruct(w.shape, F32)] * 4
    res = pl.pallas_call(body, out_shape=out_shape, name="reduce_adamw_vectors",
                         compiler_params=_cparams((), 40))(allv, *ws, *ms, *vs)
    return res[0], [tuple(res[1 + 4 * k:5 + 4 * k]) for k in range(n)]


def _adamw_taps(ws, gs, ms, vs):
    n = len(ws)

    def body(*refs):
        outs = refs[4 * n:]
        for k in range(n):
            res = _adamw(refs[k][...], refs[n + k][...], refs[2 * n + k][...], refs[3 * n + k][...])
            for ref, val in zip(outs[3 * k:3 * k + 3], res):
                ref[...] = val

    out_shape = []
    for w in ws:
        out_shape += [jax.ShapeDtypeStruct(w.shape, F32)] * 3
    res = pl.pallas_call(body, out_shape=out_shape, name="adamw_taps")(*ws, *gs, *ms, *vs)
    return [tuple(res[3 * k:3 * k + 3]) for k in range(n)]


def _pack(pieces):
    flat, offs, n = [], [], 0
    for p in pieces:
        size = -(-p.size // LANES) * LANES
        flat.append(jnp.pad(p.reshape(-1), (0, size - p.size)))
        offs.append(n)
        n += size
    return jnp.concatenate(flat).reshape(1, n), offs


def _local_step(x, target, p, wfull, on_ready=lambda group: None, before_ln0=()):
    S = x.shape[0]
    dils = [d for _, d in GROUPS]

    h0, h0b, *h0_res = _ln0_fwd(x, p["ln0_g"], p["ln0_b"], before_ln0)
    h0_rows = [h0b] + [h.reshape(S, D_MODEL) for h in h0_res]

    if isinstance(wfull, dict):
        w_in3, pending = wfull["w_in"], None
    else:
        w_in3, launch_rest, assemble = wfull
        w_in3, h0b = lax.optimization_barrier((w_in3, h0b))
        pending = launch_rest(h0b)

    runs = _col_runs()
    w_perm = jnp.concatenate([w_in3[s, :, c:c + w] for s, c, _, w in runs], axis=1)
    b_blocks = p["b_in"].reshape(N_BLK, GROUP_W)
    b_perm = jnp.concatenate([b_blocks[b] for b in PERM]).reshape(1, N_IN)
    w_nat, b_nat = w_perm[:, :N_NAT], b_perm[:, :N_NAT]
    qkv_cols = [slice(P_Q0 + g * QKV_W, P_Q0 + (g + 1) * QKV_W) for g in range(N_GROUPS)]
    w_qkv = [w_perm[:, c] for c in qkv_cols]

    proj = _mm_nn(h0b, w_nat, b_nat, tm=512, tn=N_NAT // 2, out_dtype=BF16, name="proj")
    qkv = [proj[None]]
    for g in range(1, N_GROUPS):
        t = _mm_nn(h0_rows[g], w_qkv[g], b_perm[:, qkv_cols[g]], tm=512, tn=QKV_W, out_dtype=BF16, name=f"proj_qkv{g}")
        qkv.append(t.reshape(dils[g], S // dils[g], QKV_W))
    if pending is not None:
        pending, qkv = lax.optimization_barrier((pending, qkv))
        proj = qkv[0][0]
        wfull = assemble(pending)
    w_up3 = wfull["w_up"]
    w_a, w_o, w_down, w_b = wfull["w_a"], wfull["w_o"], wfull["w_down"], wfull["w_b"]
    conv_w, ffn_conv_w = wfull["conv_w"], wfull["ffn_conv_w"]
    col0 = [P_Q0 // GROUP_W] + [0] * (N_GROUPS - 1)
    ya_in = _conv_gate_fwd(proj, conv_w)
    att = [_attn_fwd(qkv[g], col0[g], g) for g in range(N_GROUPS)]
    comb, comb_b, lse_tot = _attn_combine([a[0] for a in att], [a[1] for a in att])
    yab, mixin = _branch_mix(ya_in, comb_b, w_a, w_b, proj)
    xhat1, rstd1, h1b = _mix_ln1(mixin, w_o, p["b_o"], h0, p["ln1_g"], p["ln1_b"])
    up = _mm_nn(h1b, w_up3, p["b_up"], tm=512, tn=w_up3.shape[2], out_dtype=BF16, name="up")
    f = _ffn_conv_fwd(up, ffn_conv_w, p["ffn_conv_b"])
    dz2, dz2b, st2 = _down_ln2_loss(f, w_down, p["b_down"], xhat1, p["ln1_g"], p["ln1_b"],
                                    p["ln2_g"], p["ln2_b"], target)

    gw = {}
    gw["w_down"] = _mm_tn(f, dz2b, n_out=1, tn=D_MODEL, ts=1024, g_block=(1024, D_MODEL),
                          g_map=lambda j, s: (s, 0), name="grad_w_down").reshape(N_CHIPS, D_FF // N_CHIPS, D_MODEL)
    df = _mm_nt(dz2b, w_down, tm=512, name="df")
    dup, sm_ffn = _ffn_conv_bwd(up, df, ffn_conv_w, p["ffn_conv_b"])
    up_tn = w_up3.shape[2]
    up_pp = D_FF // up_tn
    gw["w_up"] = _mm_tn(h1b, dup, n_out=N_CHIPS, tn=up_tn, ts=1024, g_block=(None, 1024, up_tn),
                        g_map=lambda j, s: (j // up_pp, s, j % up_pp), name="grad_w_up")
    exchanged = on_ready({n: gw[n] for n in ("w_down", "w_up")}) or {}
    dz1, dz1b, st1 = _up_bwd_ln1(dup, w_up3, dz2, xhat1, rstd1, p["ln1_g"])

    gw["w_o"] = _mm_tn(mixin, dz1b, n_out=1, tn=D_MODEL, ts=512, g_block=(512, D_MODEL),
                       g_map=lambda j, s: (s, 0), name="grad_w_o").reshape(N_CHIPS, D_MODEL // N_CHIPS, D_MODEL)
    dyab, dgab = _mix_bwd(dz1b, w_o, proj, yab)
    gw["w_a"] =_mm_tn(ya_in, dyab, n_out=1, tn=D_MODEL, ts=512, g_block=(512, D_MODEL),
                       g_map=lambda j, s: (s, 0), name="grad_w_a").reshape(N_CHIPS, D_CONV // N_CHIPS, D_MODEL)
    gw_b = _mm_tn(comb_b, dyab, n_out=1, tn=D_MODEL, ts=1024, g_block=(1024, D_MODEL),
                  g_map=lambda j, s: (s, 1), name="grad_w_b")
    gw["w_b"] = gw_b.reshape(GROUP_W, N_CHIPS, D_MODEL // N_CHIPS).transpose(1, 0, 2)
    exchanged_mix = on_ready({n: gw[n] for n in ("w_o", "w_a", "w_b")}) or {}
    dya_in = _mm_nt(dyab, w_a, tm=512, a_col=0, name="dya_in")
    exchanged, dya_in = lax.optimization_barrier((exchanged, dya_in))
    dbch, sm_conv = _conv_gate_bwd(proj, dya_in, conv_w)
    att_stats = _comb_bwd(dyab, w_b, comb, lse_tot)
    exchanged_mix, att_stats = lax.optimization_barrier((exchanged_mix, att_stats))
    exchanged.update(exchanged_mix)
    dqkv = [_attn_bwd(qkv[g], col0[g], g, *att_stats[g]) for g in range(N_GROUPS)]

    w_pieces, b_pieces = [], []
    for nm, planes in (("bch", dbch), ("gab", dgab)):
        pw, pc = _mm_tn(h0b, planes, n_out=planes.shape[0], tn=D_MODEL, ts=1024, g_block=(None, 1024, D_MODEL),
                        g_map=lambda j, s: (j, s, 0), colsum=True, name="grad_w_in_" + nm)
        w_pieces.extend(pw[k] for k in range(planes.shape[0]))
        b_pieces.append(pc[0])
    for g in range(N_GROUPS):
        pw, pc = _mm_tn_cat(h0_rows[g], [a.reshape(S, GROUP_W) for a in dqkv[g]], ts=1024, name=f"grad_w_in_qkv{g}")
        w_pieces.append(pw)
        b_pieces.append(pc[0])
    dw_perm = jnp.concatenate(w_pieces, axis=1)
    gw["w_in"] = jnp.stack([
        jnp.concatenate([dw_perm[:, pc:pc + w] for s, c, pc, w in sorted(runs, key=lambda r: r[1]) if s == k], axis=1)
        for k in range(N_CHIPS)])
    exchanged.update(on_ready({"w_in": gw["w_in"]}) or {})
    db_blocks = jnp.concatenate(b_pieces).reshape(N_BLK, GROUP_W)
    grad_b_in = jnp.concatenate([db_blocks[b] for b in INV_PERM])

    grad_x, st0 = _in_bwd_ln0([dbch, dgab], dqkv, w_nat, w_qkv[1:], dz1, x, p["ln0_g"])

    small = {
        "loss": st2[2:3, 0:1],
        "ln0_g": st0[0], "ln0_b": st0[1], "b_in": grad_b_in, "conv_w": sm_conv[0:3],
        "b_o": st1[2], "ln1_g": st1[0], "ln1_b": st1[1],
        "b_up": jnp.concatenate([sm_ffn[0], sm_ffn[1]]), "ffn_conv_w": sm_ffn[3:6], "ffn_conv_b": sm_ffn[2],
        "b_down": st2[3], "ln2_g": st2[0], "ln2_b": st2[1],
    }
    return grad_x, exchanged or gw, small


BIG =("w_in", "w_a", "w_b", "w_o", "w_up", "w_down")
CONV = ("conv_w", "ffn_conv_w")
VECS = ("ln0_g", "ln0_b", "b_in", "b_o", "ln1_g", "ln1_b", "b_up", "ffn_conv_b", "b_down", "ln2_g", "ln2_b")
ORDER = ("ln0_g", "ln0_b", "w_in", "b_in", "conv_w", "w_a", "w_b", "w_o", "b_o", "ln1_g", "ln1_b", "w_up", "b_up",
         "ffn_conv_w", "ffn_conv_b", "w_down", "b_down", "ln2_g", "ln2_b")
SMALL_ORDER = ("loss",) + VECS + CONV


def _step(x, target, W, Mo, Vo):
    x2, t2 = x[0], target[0]
    big2 = {n: W[n][0] for n in BIG}
    halves = lambda a: a.astype(BF16).reshape(N_CORES, a.shape[0] // N_CORES, a.shape[1])
    whole = lambda g: g.reshape(N_CHIPS, g.shape[1] * g.shape[2], g.shape[3])
    later = tuple(n for n in BIG if n != "w_in")
    w_in_halves = halves(big2["w_in"])
    first = _allgather_shards([w_in_halves], [], name="allgather_w_in", collective_id=1)

    def launch_rest(h0b):
        return _allgather_shards([halves(big2[n]) for n in later] + [W[n] for n in CONV], [h0b],
                                 name="allgather_rest", collective_id=2)

    def assemble(rest):
        gathered = {n: whole(g) for n, g in zip(later + CONV, rest)}
        return {
            "w_up": gathered["w_up"],
            "w_a": gathered["w_a"].reshape(D_CONV, D_MODEL), "w_o": gathered["w_o"].reshape(D_MODEL, D_MODEL),
            "w_down": gathered["w_down"].reshape(D_FF, D_MODEL),
            "w_b": gathered["w_b"].transpose(1, 0, 2).reshape(GROUP_W, D_MODEL),
            "conv_w": gathered["conv_w"].transpose(1, 0, 2).reshape(3, D_CONV),
            "ffn_conv_w": gathered["ffn_conv_w"].transpose(1, 0, 2).reshape(3, D_FF),
        }

    pvec = {n: W[n].reshape(1, -1) for n in VECS}

    exchange_ids = iter((3, 4, 5))

    def exchange(group):
        names = tuple(group)
        res = _exchange_grads([group[n] for n in names], name="exchange_" + "_".join(names),
                              collective_id=next(exchange_ids))
        return dict(zip(names, res))

    grad_x, parts, small = _local_step(x2, t2, pvec, (whole(first[0]), launch_rest, assemble), exchange,
                                       before_ln0=[w_in_halves])
    out = {}
    for n in BIG:
        tr = {"w_in": 128, "w_up": 128, "w_b": 128}.get(n, big2[n].shape[0] // 4)
        g, d, nm, nv = _reduce_adamw(parts[n], big2[n], Mo[n][0], Vo[n][0], tr=tr, name="adamw_" + n)
        out[n] = tuple(a[None] for a in (g, d, nm, nv))

    vec, offs = _pack([small[n] for n in SMALL_ORDER])
    off = dict(zip(SMALL_ORDER, offs))
    row = lambda a: a.reshape(1, -1)
    allv = _allgather_small(vec, parts["w_in"])
    tot, vec_out = _reduce_adamw_vectors(allv, [off[n] for n in VECS], [row(W[n]) for n in VECS],
                                         [row(Mo[n]) for n in VECS], [row(Vo[n]) for n in VECS])
    for n, res in zip(VECS, vec_out):
        out[n] = tuple(a.reshape(W[n].shape) for a in res)
    loss = tot[0, off["loss"]]
    chip = 2 * lax.axis_index("x") + lax.axis_index("y")
    taps_g = []
    for n in CONV:
        width = W[n].shape[2]
        full = lax.slice(tot, (0, off[n]), (1, off[n] + 3 * N_CHIPS * width)).reshape(3, N_CHIPS * width)
        taps_g.append(lax.dynamic_slice_in_dim(full, chip * width, width, axis=1))
    taps_out = _adamw_taps([W[n][0] for n in CONV], taps_g, [Mo[n][0] for n in CONV], [Vo[n][0] for n in CONV])
    for n, g, res in zip(CONV, taps_g, taps_out):
        out[n] = tuple(a[None] for a in (g,) + res)

    res = [loss, grad_x[None]]
    for k in range(4):
        res += [out[n][k] for n in ORDER]
    return tuple(res)


def kernel(x, ln0_g, ln0_b, w_in, b_in, conv_w, w_a, w_b, w_o, b_o, ln1_g, ln1_b, w_up, b_up, ffn_conv_w, ffn_conv_b, w_down, b_down, ln2_g, ln2_b, loss_target, m_ln0_g, m_ln0_b, m_w_in, m_b_in, m_conv_w, m_w_a, m_w_b, m_w_o, m_b_o, m_ln1_g, m_ln1_b, m_w_up, m_b_up, m_ffn_conv_w, m_ffn_conv_b, m_w_down, m_b_down, m_ln2_g, m_ln2_b, v_ln0_g, v_ln0_b, v_w_in, v_b_in, v_conv_w, v_w_a, v_w_b, v_w_o, v_b_o, v_ln1_g, v_ln1_b, v_w_up, v_b_up, v_ffn_conv_w, v_ffn_conv_b, v_w_down, v_b_down, v_ln2_g, v_ln2_b):
    W = dict(zip(ORDER, (ln0_g, ln0_b, w_in, b_in, conv_w, w_a, w_b, w_o, b_o, ln1_g, ln1_b, w_up, b_up,
                         ffn_conv_w, ffn_conv_b, w_down, b_down, ln2_g, ln2_b)))
    Mo = dict(zip(ORDER, (m_ln0_g, m_ln0_b, m_w_in, m_b_in, m_conv_w, m_w_a, m_w_b, m_w_o, m_b_o, m_ln1_g, m_ln1_b,
                          m_w_up, m_b_up, m_ffn_conv_w, m_ffn_conv_b, m_w_down, m_b_down, m_ln2_g, m_ln2_b)))
    Vo = dict(zip(ORDER, (v_ln0_g, v_ln0_b, v_w_in, v_b_in, v_conv_w, v_w_a, v_w_b, v_w_o, v_b_o, v_ln1_g, v_ln1_b,
                          v_w_up, v_b_up, v_ffn_conv_w, v_ffn_conv_b, v_w_down, v_b_down, v_ln2_g, v_ln2_b)))
    return _step(x, loss_target, W, Mo, Vo)
```

```python
import functools
import math

import jax
import jax.numpy as jnp
from jax import lax
from jax.experimental import pallas as pl
from jax.experimental.pallas import tpu as pltpu
from jax.experimental.pallas import tpu_sc as plsc

F32 = jnp.float32
BF16 = jnp.bfloat16

D_MODEL = 1024
D_CONV = D_MODEL
HEAD_DIM = 64
HEADS_PER_GROUP = 8
GROUPS = ((128, 1), (512, 4), (2048, 16))
N_GROUPS = len(GROUPS)
GROUP_W = HEADS_PER_GROUP * HEAD_DIM
QKV_W = N_GROUPS * GROUP_W
RADIUS = 64
D_FF = 2816
LN_EPS = 1e-5
ALPHA = 2.0 ** 0.25
MASK_VALUE = -1e30
ATT_SCALE = HEAD_DIM ** -0.5
OFF_B = 0
OFF_C = OFF_B + D_CONV
OFF_H = OFF_C + D_CONV
OFF_Q = OFF_H + D_CONV
OFF_K = OFF_Q + QKV_W
OFF_V = OFF_K + QKV_W
OFF_GA = OFF_V + QKV_W
OFF_GB = OFF_GA + D_MODEL
N_IN = OFF_GB + D_MODEL
ADAM_LR = 0.001
ADAM_B1 = 0.9
ADAM_B2 = 0.999
ADAM_EPS = 1e-08
ADAM_WD = 0.01
ADAM_STEP = 10
INV_SQRT2 = 0.7071067811865476
INV_SQRT_2PI = 0.3989422804014327

LANES = 128
SUBLANES = 8
VMEM_BYTES_V7X = 64 * 1024 * 1024
N_CHIPS = 4
N_CORES = 2
N_DEV = N_CHIPS * N_CORES
MESH = pl.DeviceIdType.MESH

N_BLK = N_IN // GROUP_W
PERM = (0, 1, 2, 3, 4, 5, 15, 16, 17, 18, 6, 9, 12, 7, 10, 13, 8, 11, 14)
INV_PERM = tuple(PERM.index(b) for b in range(N_BLK))
P_B, P_C, P_H, P_GA, P_GB, P_Q0 = 0, 1024, 2048, 3072, 4096, 5120
N_NAT = P_Q0 + QKV_W // N_GROUPS * 3
N_GATED = P_Q0

def _col_runs():
    shard_w = N_IN // N_CHIPS
    runs = []
    for pos, blk in enumerate(PERM):
        c, end = blk * GROUP_W, (blk + 1) * GROUP_W
        while c < end:
            stop = min(end, (c // shard_w + 1) * shard_w)
            runs.append((c // shard_w, c % shard_w, pos * GROUP_W + c - blk * GROUP_W, stop - c))
            c = stop
    return runs


SLAB = 128
CHUNK = 256
PAD = SUBLANES
TQ = 128


def _cparams(sem, vmem_mb):
    assert vmem_mb * 1024 * 1024 < VMEM_BYTES_V7X
    return pltpu.CompilerParams(dimension_semantics=sem, vmem_limit_bytes=vmem_mb * 1024 * 1024)


def _resident(shape):
    nd = len(shape)
    return pl.BlockSpec(shape, lambda *_: (0,) * nd, pipeline_mode=pl.Buffered(1))


def _hbm(*arrays):
    return [pltpu.with_memory_space_constraint(a, pltpu.HBM) for a in arrays]


def _dot(a, b):
    return jnp.dot(a, b, preferred_element_type=F32)


def _dot_nt(a, b):
    return lax.dot_general(a, b, (((1,), (1,)), ((), ())), preferred_element_type=F32)


def _dot_tn(a, b):
    return lax.dot_general(a, b, (((0,), (0,)), ((), ())), preferred_element_type=F32)


def _ln_stats(z):
    mu = jnp.mean(z, -1, keepdims=True)
    zc = z - mu
    var = jnp.mean(zc * zc, -1, keepdims=True)
    rstd = lax.rsqrt(var + LN_EPS)
    return zc * rstd, rstd


def _ln_bwd(dh, xhat, rstd, g):
    dxh = dh * g
    m1 = jnp.mean(dxh, -1, keepdims=True)
    m2 = jnp.mean(dxh * xhat, -1, keepdims=True)
    return rstd * (dxh - m1 - xhat * m2)


def _rows8(rows, width):
    pad = [jnp.zeros((1, width), F32)] * (SUBLANES - len(rows))
    return jnp.concatenate(list(rows) + pad, axis=0)


def _mm_nn(a, w, bias, *, tm, tn, out_dtype, name, vmem_mb=40):
    M, K = a.shape
    if w.ndim == 3:
        per = tn // w.shape[2]
        assert per * w.shape[2] == tn and w.shape[0] % per == 0
        n_tiles = w.shape[0] // per
        w_spec = pl.BlockSpec((per, K, w.shape[2]), lambda j, i: (j, 0, 0))
    else:
        per = 0
        n_tiles = w.shape[1] // tn
        w_spec = pl.BlockSpec((K, tn), lambda j, i: (0, j))

    def body(a_ref, w_ref, b_ref, o_ref):
        wv = jnp.concatenate([w_ref[k] for k in range(per)], axis=1) if per else w_ref[...]
        o_ref[...] = (_dot(a_ref[...], wv) + b_ref[...]).astype(o_ref.dtype)

    return pl.pallas_call(
        body, grid=(n_tiles, M // tm),
        in_specs=[pl.BlockSpec((tm, K), lambda j, i: (i, 0)), w_spec, pl.BlockSpec((1, tn), lambda j, i: (0, j))],
        out_specs=pl.BlockSpec((tm, tn), lambda j, i: (i, j)),
        out_shape=jax.ShapeDtypeStruct((M, n_tiles * tn), out_dtype),
        name=name, compiler_params=_cparams(("arbitrary", "parallel"), vmem_mb))(*_hbm(a, w, bias))


def _mm_nt(a, w, *, tm, a_col=0, name, vmem_mb=40):
    M = a.shape[0]
    N, K = w.shape

    def body(a_ref, w_ref, o_ref):
        o_ref[...] = _dot_nt(a_ref[...], w_ref[...]).astype(o_ref.dtype)

    return pl.pallas_call(
        body, grid=(M // tm,),
        in_specs=[pl.BlockSpec((tm, K), lambda i: (i, a_col)),
                  pl.BlockSpec((N, K), lambda i: (0, 0))],
        out_specs=pl.BlockSpec((tm, N), lambda i: (i, 0)),
        out_shape=jax.ShapeDtypeStruct((M, N), BF16),
        name=name, compiler_params=_cparams(("parallel",), vmem_mb))(*_hbm(a, w))


def _mm_tn(a, g, *, n_out, tn, ts, g_block, g_map, colsum=False, split=1, name, vmem_mb=48):
    S, K = a.shape
    n_s = S // ts
    shard_w = tn // split

    def body(a_ref, g_ref, *rest):
        if colsum:
            o_ref, cs_ref, acc_ref, cacc_ref = rest
        else:
            o_ref, acc_ref = rest
        s = pl.program_id(1)

        @pl.when(s == 0)
        def _():
            acc_ref[...] = jnp.zeros_like(acc_ref)
            if colsum:
                cacc_ref[...] = jnp.zeros_like(cacc_ref)

        gv = g_ref[...]
        acc_ref[...] += _dot_tn(a_ref[...], gv)
        if colsum:
            cacc_ref[...] += jnp.broadcast_to(jnp.sum(gv.astype(F32), axis=0, keepdims=True), cacc_ref.shape)

        @pl.when(s == n_s - 1)
        def _():
            for k in range(split):
                o_ref[k] = acc_ref[:, k * shard_w:(k + 1) * shard_w].astype(o_ref.dtype)
            if colsum:
                cs_ref[...] = cacc_ref[...]

    out_specs = [pl.BlockSpec((split, K, shard_w), lambda j, s: (j, 0, 0))]
    out_shape = [jax.ShapeDtypeStruct((n_out * split, K, shard_w), BF16)]
    scratch = [pltpu.VMEM((K, tn), F32)]
    if colsum:
        out_specs.append(pl.BlockSpec((SUBLANES, tn), lambda j, s: (0, j)))
        out_shape.append(jax.ShapeDtypeStruct((SUBLANES, n_out * tn), F32))
        scratch.append(pltpu.VMEM((SUBLANES, tn), F32))
    res = pl.pallas_call(
        body, grid=(n_out, n_s),
        in_specs=[pl.BlockSpec((ts, K), lambda j, s: (s, 0)), pl.BlockSpec(g_block, g_map)],
        out_specs=out_specs, out_shape=out_shape, scratch_shapes=scratch,
        name=name, compiler_params=_cparams(("parallel", "arbitrary"), vmem_mb))(*_hbm(a, g))
    return res if colsum else res[0]


def _mm_tn_cat(a, gs, *, ts, name, vmem_mb=40):
    S, K = a.shape
    widths = [g.shape[1] for g in gs]
    n_s, total = S // ts, sum(widths)

    def body(*refs):
        a_ref, g_refs = refs[0], refs[1:1 + len(gs)]
        o_ref, cs_ref, acc_ref, cacc_ref = refs[1 + len(gs):]
        s = pl.program_id(0)

        @pl.when(s == 0)
        def _():
            acc_ref[...] = jnp.zeros_like(acc_ref)
            cacc_ref[...] = jnp.zeros_like(cacc_ref)

        av, col = a_ref[...], 0
        for g_ref, w in zip(g_refs, widths):
            gv = g_ref[...]
            acc_ref[:, col:col + w] += _dot_tn(av, gv)
            cacc_ref[:, col:col + w] += jnp.broadcast_to(jnp.sum(gv.astype(F32), axis=0, keepdims=True), (SUBLANES, w))
            col += w

        @pl.when(s == n_s - 1)
        def _():
            o_ref[...] = acc_ref[...].astype(BF16)
            cs_ref[...] = cacc_ref[...]

    return pl.pallas_call(
        body, grid=(n_s,),
        in_specs=[pl.BlockSpec((ts, K), lambda s: (s, 0))] + [pl.BlockSpec((ts, w), lambda s: (s, 0)) for w in widths],
        out_specs=[pl.BlockSpec((K, total), lambda s: (0, 0)), pl.BlockSpec((SUBLANES, total), lambda s: (0, 0))],
        out_shape=[jax.ShapeDtypeStruct((K, total), BF16), jax.ShapeDtypeStruct((SUBLANES, total), F32)],
        scratch_shapes=[pltpu.VMEM((K, total), F32), pltpu.VMEM((SUBLANES, total), F32)],
        name=name, compiler_params=_cparams(("arbitrary",), vmem_mb))(*_hbm(a, *gs))


DILS = tuple(d for _, d in GROUPS if d > 1)


def _res_spec(d, tm, width):
    return pl.BlockSpec((d, tm // d, width), lambda i: (0, i, 0))


def _lane_scratch(tm, width):
    return [pltpu.VMEM((tm, LANES), F32)] * (width // LANES)


def _to_residue(val, dst_refs, dils, tm, dtype, scr):
    for c, ref in enumerate(scr):
        ref[...] = val[:, c * LANES:(c + 1) * LANES]
    for dst_ref, d in zip(dst_refs, dils):
        for r in range(d):
            cols = [ref[pl.ds(r, tm // d, stride=d), :] for ref in scr]
            dst_ref[r] = jnp.concatenate(cols, axis=1).astype(dtype)


def _from_residue(rows_of, d, tm, scr):
    for r in range(d):
        v = rows_of(r).astype(F32)
        for c, ref in enumerate(scr):
            ref[pl.ds(r, tm // d, stride=d), :] = v[:, c * LANES:(c + 1) * LANES]
    return jnp.concatenate([ref[...] for ref in scr], axis=1)


def _ln0_fwd(x, g, b, after=(), *, tm=512):
    S, Dm = x.shape
    n_after = len(after)

    def body(x_ref, g_ref, b_ref, *rest):
        h_ref, hb_ref, *rest = rest[n_after:]
        xhat, _ = _ln_stats(x_ref[...])
        h = xhat * g_ref[...] + b_ref[...]
        h_ref[...] = h
        hb_ref[...] = h.astype(BF16)
        _to_residue(h, rest[:len(DILS)], DILS, tm, BF16, rest[len(DILS):])

    row = pl.BlockSpec((tm, Dm), lambda i: (i, 0))
    vec = pl.BlockSpec((1, Dm), lambda i: (0, 0))
    return pl.pallas_call(
        body, grid=(S // tm,), in_specs=[row, vec, vec] + [pl.BlockSpec(memory_space=pl.ANY)] * n_after,
        out_specs=[row, row] + [_res_spec(d, tm, Dm) for d in DILS],
        out_shape=[jax.ShapeDtypeStruct((S, Dm), F32), jax.ShapeDtypeStruct((S, Dm), BF16)]
        + [jax.ShapeDtypeStruct((d, S // d, Dm), BF16) for d in DILS],
        scratch_shapes=_lane_scratch(tm, Dm),
        name="ln0_fwd", compiler_params=_cparams(("parallel",), 32))(*_hbm(x, g, b), *after)


def _slab_spec(S, col0):
    return pl.BlockSpec((S, SLAB), lambda j: (0, col0 // SLAB + j))


def _zero_pads(scr, S):
    scr[0:PAD, :] = jnp.zeros((PAD, SLAB), F32)
    scr[S + PAD:S + 2 * PAD, :] = jnp.zeros((PAD, SLAB), F32)


def _shifted(scr, t):
    return (scr[PAD - 1 + t:PAD - 1 + t + CHUNK, :], scr[PAD + t:PAD + t + CHUNK, :],
            scr[PAD + 1 + t:PAD + 1 + t + CHUNK, :])


def _conv_gate_fwd(proj, conv_w):
    S = proj.shape[0]

    def body(b_ref, c_ref, h_ref, w_ref, o_ref, u_scr):
        _zero_pads(u_scr, S)
        for t in range(0, S, CHUNK):
            u_scr[PAD + t:PAD + t + CHUNK, :] = c_ref[t:t + CHUNK, :].astype(F32) * h_ref[t:t + CHUNK, :].astype(F32)
        w0, w1, w2 = w_ref[0:1, :], w_ref[1:2, :], w_ref[2:3, :]
        for t in range(0, S, CHUNK):
            um, u0, up = _shifted(u_scr, t)
            cv = w0 * um + w1 * u0 + w2 * up
            o_ref[t:t + CHUNK, :] = (b_ref[t:t + CHUNK, :].astype(F32) * cv).astype(BF16)

    return pl.pallas_call(
        body, grid=(D_CONV // SLAB,),
        in_specs=[_slab_spec(S, P_B), _slab_spec(S, P_C), _slab_spec(S, P_H),
                  pl.BlockSpec((3, SLAB), lambda j: (0, j))],
        out_specs=pl.BlockSpec((S, SLAB), lambda j: (0, j)),
        out_shape=jax.ShapeDtypeStruct((S, D_CONV), BF16),
        scratch_shapes=[pltpu.VMEM((S + 2 * PAD, SLAB), F32)],
        name="conv_gate_fwd", compiler_params=_cparams(("parallel",), 40))(*_hbm(proj, proj, proj, conv_w))


MASKED_DISTANCE = -1e34


def _attn_bias_table(g):
    dil = GROUPS[g][1]
    j = lax.broadcasted_iota(jnp.int32, (2 * TQ, TQ), 0)
    a = lax.broadcasted_iota(jnp.int32, (2 * TQ, TQ), 1)
    rel = jnp.abs(j - RADIUS - a)
    base = -(rel * dil).astype(F32)
    inside, after_start, before_end = rel <= RADIUS, j >= RADIUS, j < TQ + RADIUS
    variants = []
    for first, last in ((False, False), (True, False), (False, True), (True, True)):
        valid = inside & (after_start if first else True) & (before_end if last else True)
        variants.append(jnp.where(valid, base, MASKED_DISTANCE))
    return jnp.stack(variants)


SUBS = 4
TB = SUBS * TQ


def _ext_window(p_ref, c_ref, n_ref):
    return jnp.concatenate([p_ref[TB - RADIUS:, :], c_ref[...], n_ref[:RADIUS, :]], axis=0)


def _head_stats(rows):
    pad = jnp.zeros((LANES - len(rows), TQ), F32)
    return jnp.concatenate(list(rows) + [pad], axis=0).T


def _slope(g, h):
    return 2.0 ** (-8.0 * (g * HEADS_PER_GROUP + h + 1) / (N_GROUPS * HEADS_PER_GROUP))


def _pair(a, h):
    return a[:, (h // 2) * LANES:(h // 2 + 1) * LANES]


def _own_lanes(a, h):
    lane = lax.broadcasted_iota(jnp.int32, a.shape, 1)
    return jnp.where((lane >= HEAD_DIM) == (h % 2 == 1), a, jnp.zeros_like(a))


def _own_rows(a, h):
    return a[(h % 2) * HEAD_DIM:(h % 2 + 1) * HEAD_DIM, :]


def _attn_fwd(qkv, col0, g):
    dil, sub, _ = qkv.shape
    nb = sub // TB
    heads = HEADS_PER_GROUP

    def body(q_ref, kp, kc, kn, vp, vc, vn, bias_ref, o_ref, lse_ref, ot_scr, s_scr, p_scr):
        i = pl.program_id(1)
        kext = _ext_window(kp, kc, kn)
        vext = _ext_window(vp, vc, vn)
        q = q_ref[...] * ATT_SCALE
        for b in range(SUBS):
            kwin, qb = kext[b * TQ:(b + 2) * TQ, :], q[b * TQ:(b + 1) * TQ, :]
            for h in range(heads):
                s_scr[b * heads + h] = _dot_nt(_pair(kwin, h), _own_lanes(_pair(qb, h), h))
        inv_den = []
        for b in range(SUBS):
            block = i * SUBS + b
            bias = bias_ref[jnp.where(block == 0, 1, 0) + jnp.where(block == nb * SUBS - 1, 2, 0)]
            lse = []
            for h in range(heads):
                s = s_scr[b * heads + h] + _slope(g, h) * bias
                m = jnp.max(s, axis=0, keepdims=True)
                p = jnp.exp(s - m)
                den = jnp.sum(p, axis=0, keepdims=True)
                p_scr[b * heads + h] = p.astype(BF16)
                inv_den.append(1.0 / den)
                lse.append(m + jnp.log(den))
            lse_ref[b * TQ:(b + 1) * TQ, :] = _head_stats(lse)
        for b in range(SUBS):
            vwin = vext[b * TQ:(b + 2) * TQ, :]
            for h in range(heads):
                ot = _dot_tn(_pair(vwin, h), p_scr[b * heads + h])
                ot_scr[h * HEAD_DIM:(h + 1) * HEAD_DIM, b * TQ:(b + 1) * TQ] = _own_rows(ot, h) * inv_den[b * heads + h]
        o_ref[...] = ot_scr[...].T

    def spec(col, shift):
        return pl.BlockSpec((None, TB, GROUP_W), lambda r, i: (r, jnp.clip(i + shift, 0, nb - 1), col))

    return pl.pallas_call(
        body, grid=(dil, nb),
        in_specs=[spec(col0, 0), spec(col0 + 1, -1), spec(col0 + 1, 0), spec(col0 + 1, 1),
                  spec(col0 + 2, -1), spec(col0 + 2, 0), spec(col0 + 2, 1),
                  pl.BlockSpec((4, 2 * TQ, TQ), lambda r, i: (0, 0, 0))],
        out_specs=[pl.BlockSpec((None, TB, GROUP_W), lambda r, i: (r, i, 0)),
                   pl.BlockSpec((None, TB, LANES), lambda r, i: (r, i, 0))],
        out_shape=[jax.ShapeDtypeStruct((dil, sub, GROUP_W), F32), jax.ShapeDtypeStruct((dil, sub, LANES), F32)],
        scratch_shapes=[pltpu.VMEM((GROUP_W, TB), F32), pltpu.VMEM((SUBS * heads, 2 * TQ, TQ), F32),
                        pltpu.VMEM((SUBS * heads, 2 * TQ, TQ), BF16)],
        name=f"attn_fwd_g{g}", compiler_params=_cparams(("parallel", "arbitrary"), 32))(
            *_hbm(*([qkv] * 7), _attn_bias_table(g)))


def _expand_heads():
    h = lax.broadcasted_iota(jnp.int32, (LANES, GROUP_W), 0)
    c = lax.broadcasted_iota(jnp.int32, (LANES, GROUP_W), 1)
    return (c // HEAD_DIM == h).astype(F32)


def _dot_f32(a, b):
    return jnp.dot(a, b, preferred_element_type=F32, precision=lax.Precision.HIGH)


def _attn_combine(outs, lses, *, tm=512):
    S = outs[0].shape[1]
    n_col = GROUP_W // LANES

    def body(*refs):
        ins, e_ref = refs[:2 * N_GROUPS], refs[2 * N_GROUPS]
        c_ref, cb_ref, lt_ref = refs[2 * N_GROUPS + 1:2 * N_GROUPS + 4]
        scr = refs[2 * N_GROUPS + 4:]
        o, l = [ins[0][0]], [ins[N_GROUPS][0]]
        for k, d in enumerate(DILS):
            o_ref, l_ref = ins[1 + k], ins[N_GROUPS + 1 + k]
            o.append(_from_residue(lambda r: o_ref[r], d, tm, scr[k * (n_col + 1):k * (n_col + 1) + n_col]))
            l.append(_from_residue(lambda r: l_ref[r], d, tm, scr[k * (n_col + 1) + n_col:(k + 1) * (n_col + 1)]))
        m = jnp.maximum(jnp.maximum(l[0], l[1]), l[2])
        e = [jnp.exp(v - m) for v in l]
        den = e[0] + e[1] + e[2]
        comb = sum(_dot_f32(ev / den, e_ref[...]) * ov for ev, ov in zip(e, o))
        c_ref[...] = comb
        cb_ref[...] = comb.astype(BF16)
        lt_ref[...] = m + jnp.log(den)

    row = pl.BlockSpec((tm, GROUP_W), lambda i: (i, 0))
    dils = [d for _, d in GROUPS]
    return pl.pallas_call(
        body, grid=(S // tm,),
        in_specs=[_res_spec(d, tm, GROUP_W) for d in dils] + [_res_spec(d, tm, LANES) for d in dils]
        + [_resident((LANES, GROUP_W))],
        out_specs=[row, row, pl.BlockSpec((tm, LANES), lambda i: (i, 0))],
        out_shape=[jax.ShapeDtypeStruct((S, GROUP_W), F32), jax.ShapeDtypeStruct((S, GROUP_W), BF16),
                   jax.ShapeDtypeStruct((S, LANES), F32)],
        scratch_shapes=_lane_scratch(tm, GROUP_W + LANES) * len(DILS),
        name="attn_combine", compiler_params=_cparams(("parallel",), 32))(*_hbm(*outs, *lses, _expand_heads()))


def _branch_mix(ya_in, comb_b, w_a, w_b, proj, *, tm=512):
    S = ya_in.shape[0]

    def body(ya_ref, cb_ref, wa_ref, wb_ref, ga_ref, gb_ref, yab_ref, mx_ref):
        y_a = _dot(ya_ref[...], wa_ref[...])
        y_b = _dot(cb_ref[...], wb_ref[...])
        yab_ref[:, 0:D_MODEL] = y_a.astype(BF16)
        yab_ref[:, D_MODEL:2 * D_MODEL] = y_b.astype(BF16)
        mx = jax.nn.sigmoid(ga_ref[...].astype(F32)) * y_a + jax.nn.sigmoid(gb_ref[...].astype(F32)) * y_b
        mx_ref[...] = mx.astype(BF16)

    return pl.pallas_call(
        body, grid=(S // tm,),
        in_specs=[pl.BlockSpec((tm, D_CONV), lambda i: (i, 0)), pl.BlockSpec((tm, GROUP_W), lambda i: (i, 0)),
                  pl.BlockSpec((D_CONV, D_MODEL), lambda i: (0, 0)), pl.BlockSpec((GROUP_W, D_MODEL), lambda i: (0, 0)),
                  pl.BlockSpec((tm, D_MODEL), lambda i: (i, P_GA // D_MODEL)),
                  pl.BlockSpec((tm, D_MODEL), lambda i: (i, P_GB // D_MODEL))],
        out_specs=[pl.BlockSpec((tm, 2 * D_MODEL), lambda i: (i, 0)), pl.BlockSpec((tm, D_MODEL), lambda i: (i, 0))],
        out_shape=[jax.ShapeDtypeStruct((S, 2 * D_MODEL), BF16), jax.ShapeDtypeStruct((S, D_MODEL), BF16)],
        name="branch_mix", compiler_params=_cparams(("parallel",), 40))(*_hbm(ya_in, comb_b, w_a, w_b, proj, proj))


def _mix_ln1(mixin, w_o, b_o, h0, g1, b1, *, tm=512):
    S = mixin.shape[0]

    def body(mx_ref, wo_ref, bo_ref, h0_ref, g_ref, b_ref, xh_ref, rs_ref, h1b_ref):
        z = ALPHA * h0_ref[...] + _dot(mx_ref[...], wo_ref[...]) + bo_ref[...]
        xhat, rstd = _ln_stats(z)
        xh_ref[...] = xhat
        rs_ref[...] = jnp.broadcast_to(rstd, (tm, LANES))
        h1b_ref[...] = (xhat * g_ref[...] + b_ref[...]).astype(BF16)

    row = pl.BlockSpec((tm, D_MODEL), lambda i: (i, 0))
    vec = pl.BlockSpec((1, D_MODEL), lambda i: (0, 0))
    return pl.pallas_call(
        body, grid=(S // tm,),
        in_specs=[row, pl.BlockSpec((D_MODEL, D_MODEL), lambda i: (0, 0)), vec, row, vec, vec],
        out_specs=[row, pl.BlockSpec((tm, LANES), lambda i: (i, 0)), row],
        out_shape=[jax.ShapeDtypeStruct((S, D_MODEL), F32), jax.ShapeDtypeStruct((S, LANES), F32),
                   jax.ShapeDtypeStruct((S, D_MODEL), BF16)],
        name="mix_ln1", compiler_params=_cparams(("parallel",), 40))(*_hbm(mixin, w_o, b_o, h0, g1, b1))


def _gelu_parts(cz):
    cdf = 0.5 * (1.0 + lax.erf(cz * INV_SQRT2))
    return cdf, cz * cdf


def _ffn_conv_fwd(up, cw, cb):
    S = up.shape[0]

    def body(a_ref, g_ref, w_ref, cb_ref, o_ref, a_scr):
        _zero_pads(a_scr, S)
        for t in range(0, S, CHUNK):
            a_scr[PAD + t:PAD + t + CHUNK, :] = a_ref[t:t + CHUNK, :].astype(F32)
        w0, w1, w2 = w_ref[0:1, :], w_ref[1:2, :], w_ref[2:3, :]
        for t in range(0, S, CHUNK):
            am, a0, ap = _shifted(a_scr, t)
            _, gel = _gelu_parts(w0 * am + w1 * a0 + w2 * ap + cb_ref[...])
            o_ref[t:t + CHUNK, :] = (gel * g_ref[t:t + CHUNK, :].astype(F32)).astype(BF16)

    return pl.pallas_call(
        body, grid=(D_FF // SLAB,),
        in_specs=[_slab_spec(S, 0), _slab_spec(S, D_FF), pl.BlockSpec((3, SLAB), lambda j: (0, j)),
                  pl.BlockSpec((1, SLAB), lambda j: (0, j))],
        out_specs=pl.BlockSpec((S, SLAB), lambda j: (0, j)),
        out_shape=jax.ShapeDtypeStruct((S, D_FF), BF16),
        scratch_shapes=[pltpu.VMEM((S + 2 * PAD, SLAB), F32)],
        name="ffn_conv_fwd", compiler_params=_cparams(("parallel",), 40))(*_hbm(up, up, cw, cb))


def _down_ln2_loss(f, w_down, b_down, xhat1, g1, b1, g2, b2, target, *, tm=512):
    S = f.shape[0]

    def body(f_ref, wd_ref, bd_ref, xh1_ref, g1_ref, b1_ref, g2_ref, b2_ref, t_ref, dz_ref, dzb_ref, st_ref):
        h1 = xh1_ref[...] * g1_ref[...] + b1_ref[...]
        z = ALPHA * h1 + _dot(f_ref[...], wd_ref[...]) + bd_ref[...]
        xhat, rstd = _ln_stats(z)
        err = xhat * g2_ref[...] + b2_ref[...] - t_ref[...]
        loss = (0.5 / D_MODEL) * jnp.sum(jnp.sum(err * err, axis=1, keepdims=True), axis=0, keepdims=True)
        dh2 = err * (1.0 / D_MODEL)
        dz = _ln_bwd(dh2, xhat, rstd, g2_ref[...])
        dz_ref[...] = dz
        dzb_ref[...] = dz.astype(BF16)
        upd = _rows8([jnp.sum(dh2 * xhat, axis=0, keepdims=True), jnp.sum(dh2, axis=0, keepdims=True),
                      jnp.broadcast_to(loss, (1, D_MODEL)), jnp.sum(dz, axis=0, keepdims=True)], D_MODEL)

        @pl.when(pl.program_id(0) == 0)
        def _():
            st_ref[...] = upd

        @pl.when(pl.program_id(0) != 0)
        def _():
            st_ref[...] += upd

    row = pl.BlockSpec((tm, D_MODEL), lambda i: (i, 0))
    vec = pl.BlockSpec((1, D_MODEL), lambda i: (0, 0))
    return pl.pallas_call(
        body, grid=(S // tm,),
        in_specs=[pl.BlockSpec((tm, D_FF), lambda i: (i, 0)), _resident((D_FF, D_MODEL)),
                  vec, row, vec, vec, vec, vec, row],
        out_specs=[row, row, pl.BlockSpec((SUBLANES, D_MODEL), lambda i: (0, 0))],
        out_shape=[jax.ShapeDtypeStruct((S, D_MODEL), F32), jax.ShapeDtypeStruct((S, D_MODEL), BF16),
                   jax.ShapeDtypeStruct((SUBLANES, D_MODEL), F32)],
        name="down_ln2_loss", compiler_params=_cparams(("arbitrary",), 56))(
            *_hbm(f, w_down, b_down, xhat1, g1, b1, g2, b2, target))


def _ffn_conv_bwd(up, df, cw, cb):
    S = up.shape[0]

    def body(a_ref, g_ref, df_ref, w_ref, cb_ref, dup_ref, sm_ref, a_scr, d_scr):
        _zero_pads(a_scr, S)
        _zero_pads(d_scr, S)
        for t in range(0, S, CHUNK):
            a_scr[PAD + t:PAD + t + CHUNK, :] = a_ref[t:t + CHUNK, :].astype(F32)
        w0, w1, w2 = w_ref[0:1, :], w_ref[1:2, :], w_ref[2:3, :]
        zero = jnp.zeros((1, SLAB), F32)
        s_dg, s_dcz, s_w0, s_w1, s_w2 = zero, zero, zero, zero, zero
        for t in range(0, S, CHUNK):
            am, a0, ap = _shifted(a_scr, t)
            cz = w0 * am + w1 * a0 + w2 * ap + cb_ref[...]
            cdf, gel = _gelu_parts(cz)
            dfv = df_ref[t:t + CHUNK, :].astype(F32)
            dgte = dfv * gel
            dcz = dfv * g_ref[t:t + CHUNK, :].astype(F32) * (cdf + cz * jnp.exp(-0.5 * cz * cz) * INV_SQRT_2PI)
            dup_ref[1, t:t + CHUNK, :] = dgte.astype(BF16)
            d_scr[PAD + t:PAD + t + CHUNK, :] = dcz
            s_dg = s_dg + jnp.sum(dgte, axis=0, keepdims=True)
            s_dcz = s_dcz + jnp.sum(dcz, axis=0, keepdims=True)
            s_w0 = s_w0 + jnp.sum(dcz * am, axis=0, keepdims=True)
            s_w1 = s_w1 + jnp.sum(dcz * a0, axis=0, keepdims=True)
            s_w2 = s_w2 + jnp.sum(dcz * ap, axis=0, keepdims=True)
        s_da = zero
        for t in range(0, S, CHUNK):
            dm, d0, dp = _shifted(d_scr, t)
            da = w0 * dp + w1 * d0 + w2 * dm
            dup_ref[0, t:t + CHUNK, :] = da.astype(BF16)
            s_da = s_da + jnp.sum(da, axis=0, keepdims=True)
        sm_ref[...] = _rows8([s_da, s_dg, s_dcz, s_w0, s_w1, s_w2], SLAB)

    return pl.pallas_call(
        body, grid=(D_FF // SLAB,),
        in_specs=[_slab_spec(S, 0), _slab_spec(S, D_FF), pl.BlockSpec((S, SLAB), lambda j: (0, j)),
                  pl.BlockSpec((3, SLAB), lambda j: (0, j)), pl.BlockSpec((1, SLAB), lambda j: (0, j))],
        out_specs=[pl.BlockSpec((2, S, SLAB), lambda j: (0, 0, j)), pl.BlockSpec((SUBLANES, SLAB), lambda j: (0, j))],
        out_shape=[jax.ShapeDtypeStruct((2, S, D_FF), BF16), jax.ShapeDtypeStruct((SUBLANES, D_FF), F32)],
        scratch_shapes=[pltpu.VMEM((S + 2 * PAD, SLAB), F32)] * 2,
        name="ffn_conv_bwd", compiler_params=_cparams(("parallel",), 48))(*_hbm(up, up, df, cw, cb))


def _up_bwd_ln1(dup, w_up3, dz2, xhat1, rstd1, g1, *, tm=512):
    S = dz2.shape[0]
    ns, _, tk = w_up3.shape
    per_plane = D_FF // tk

    def body(du_ref, w_ref, dz2_ref, xh_ref, rs_ref, g_ref, dz_ref, dzb_ref, st_ref):
        dh = ALPHA * dz2_ref[...]
        for plane in range(ns // per_plane):
            w = jnp.concatenate([w_ref[plane * per_plane + k] for k in range(per_plane)], axis=1)
            dh = dh + _dot_nt(du_ref[plane], w)
        xhat = xh_ref[...]
        dz = _ln_bwd(dh, xhat, rs_ref[:, 0:1], g_ref[...])
        dz_ref[...] = dz
        dzb_ref[...] = dz.astype(BF16)
        upd = _rows8([jnp.sum(dh * xhat, axis=0, keepdims=True), jnp.sum(dh, axis=0, keepdims=True),
                      jnp.sum(dz, axis=0, keepdims=True)], D_MODEL)

        @pl.when(pl.program_id(0) == 0)
        def _():
            st_ref[...] = upd

        @pl.when(pl.program_id(0) != 0)
        def _():
            st_ref[...] += upd

    row = pl.BlockSpec((tm, D_MODEL), lambda i: (i, 0))
    return pl.pallas_call(
        body, grid=(S // tm,),
        in_specs=[pl.BlockSpec((dup.shape[0], tm, D_FF), lambda i: (0, i, 0)), _resident(w_up3.shape),
                  row, row, pl.BlockSpec((tm, LANES), lambda i: (i, 0)), pl.BlockSpec((1, D_MODEL), lambda i: (0, 0))],
        out_specs=[row, row, pl.BlockSpec((SUBLANES, D_MODEL), lambda i: (0, 0))],
        out_shape=[jax.ShapeDtypeStruct((S, D_MODEL), F32), jax.ShapeDtypeStruct((S, D_MODEL), BF16),
                   jax.ShapeDtypeStruct((SUBLANES, D_MODEL), F32)],
        name="up_bwd_ln1", compiler_params=_cparams(("arbitrary",), 56))(*_hbm(dup, w_up3, dz2, xhat1, rstd1, g1))


def _mix_bwd(dz1b, w_o, proj, yab, *, tm=512):
    S = dz1b.shape[0]

    def body(dz_ref, wo_ref, ga_ref, gb_ref, y_ref, dy_ref, dg_ref):
        dmx = _dot_nt(dz_ref[...], wo_ref[...])
        for k, gt_ref in enumerate((ga_ref, gb_ref)):
            sl = slice(k * D_MODEL, (k + 1) * D_MODEL)
            sg = jax.nn.sigmoid(gt_ref[...].astype(F32))
            dy_ref[:, sl] = (dmx * sg).astype(BF16)
            dg_ref[k] = (dmx * y_ref[:, sl].astype(F32) * sg * (1.0 - sg)).astype(BF16)

    row = pl.BlockSpec((tm, D_MODEL), lambda i: (i, 0))
    wide = pl.BlockSpec((tm, 2 * D_MODEL), lambda i: (i, 0))
    return pl.pallas_call(
        body, grid=(S // tm,),
        in_specs=[row, _resident(w_o.shape), pl.BlockSpec((tm, D_MODEL), lambda i: (i, P_GA // D_MODEL)),
                  pl.BlockSpec((tm, D_MODEL), lambda i: (i, P_GB // D_MODEL)), wide],
        out_specs=[wide, pl.BlockSpec((2, tm, D_MODEL), lambda i: (0, i, 0))],
        out_shape=[jax.ShapeDtypeStruct((S, 2 * D_MODEL), BF16), jax.ShapeDtypeStruct((2, S, D_MODEL), BF16)],
        name="mix_bwd", compiler_params=_cparams(("parallel",), 40))(*_hbm(dz1b, w_o, proj, proj, yab))


def _conv_gate_bwd(proj, dya_in, conv_w):
    S = proj.shape[0]

    def body(b_ref, c_ref, h_ref, dy_ref, w_ref, o_ref, sm_ref, u_scr, d_scr):
        _zero_pads(u_scr, S)
        _zero_pads(d_scr, S)
        for t in range(0, S, CHUNK):
            u_scr[PAD + t:PAD + t + CHUNK, :] = c_ref[t:t + CHUNK, :].astype(F32) * h_ref[t:t + CHUNK, :].astype(F32)
        w0, w1, w2 = w_ref[0:1, :], w_ref[1:2, :], w_ref[2:3, :]
        zero = jnp.zeros((1, SLAB), F32)
        s_w0, s_w1, s_w2 = zero, zero, zero
        for t in range(0, S, CHUNK):
            um, u0, up = _shifted(u_scr, t)
            dy = dy_ref[t:t + CHUNK, :].astype(F32)
            o_ref[0, t:t + CHUNK, :] = (dy * (w0 * um + w1 * u0 + w2 * up)).astype(BF16)
            dcv = dy * b_ref[t:t + CHUNK, :].astype(F32)
            d_scr[PAD + t:PAD + t + CHUNK, :] = dcv
            s_w0 = s_w0 + jnp.sum(dcv * um, axis=0, keepdims=True)
            s_w1 = s_w1 + jnp.sum(dcv * u0, axis=0, keepdims=True)
            s_w2 = s_w2 + jnp.sum(dcv * up, axis=0, keepdims=True)
        for t in range(0, S, CHUNK):
            dm, d0, dp = _shifted(d_scr, t)
            du = w0 * dp + w1 * d0 + w2 * dm
            o_ref[1, t:t + CHUNK, :] = (du * h_ref[t:t + CHUNK, :].astype(F32)).astype(BF16)
            o_ref[2, t:t + CHUNK, :] = (du * c_ref[t:t + CHUNK, :].astype(F32)).astype(BF16)
        sm_ref[...] = _rows8([s_w0, s_w1, s_w2], SLAB)

    return pl.pallas_call(
        body, grid=(D_CONV // SLAB,),
        in_specs=[_slab_spec(S, P_B), _slab_spec(S, P_C), _slab_spec(S, P_H),
                  pl.BlockSpec((S, SLAB), lambda j: (0, j)), pl.BlockSpec((3, SLAB), lambda j: (0, j))],
        out_specs=[pl.BlockSpec((3, S, SLAB), lambda j: (0, 0, j)), pl.BlockSpec((SUBLANES, SLAB), lambda j: (0, j))],
        out_shape=[jax.ShapeDtypeStruct((3, S, D_CONV), BF16), jax.ShapeDtypeStruct((SUBLANES, D_CONV), F32)],
        scratch_shapes=[pltpu.VMEM((S + 2 * PAD, SLAB), F32)] * 2,
        name="conv_gate_bwd", compiler_params=_cparams(("parallel",), 48))(*_hbm(proj, proj, proj, dya_in, conv_w))


def _comb_bwd(dyab, w_b, comb, lse_tot, *, tm=512):
    S = comb.shape[0]
    widths, dtypes = (GROUP_W, LANES, LANES), (BF16, F32, F32)

    def body(dy_ref, wb_ref, c_ref, lt_ref, e_ref, *rest):
        outs, scr = rest[:3 * N_GROUPS], rest[3 * N_GROUPS:]
        dcb = _dot_nt(dy_ref[...], wb_ref[...]).astype(BF16)
        dc = dcb.astype(F32)
        delta = lax.dot_general(dc * c_ref[...], e_ref[...], (((1,), (1,)), ((), ())),
                                preferred_element_type=F32, precision=lax.Precision.HIGH)
        for k, (val, dtype) in enumerate(zip((dc, lt_ref[...], delta), dtypes)):
            outs[k][0] = val.astype(dtype)
            _to_residue(val, [outs[3 * (1 + j) + k] for j in range(len(DILS))], DILS, tm, dtype,
                        scr[:val.shape[1] // LANES])

    out_specs, out_shape = [], []
    for _, d in GROUPS:
        out_specs += [_res_spec(d, tm, w) for w in widths]
        out_shape += [jax.ShapeDtypeStruct((d, S // d, w), t) for w, t in zip(widths, dtypes)]
    res = pl.pallas_call(
        body, grid=(S // tm,),
        in_specs=[pl.BlockSpec((tm, D_MODEL), lambda i: (i, 1)), _resident(w_b.shape),
                  pl.BlockSpec((tm, GROUP_W), lambda i: (i, 0)), pl.BlockSpec((tm, LANES), lambda i: (i, 0)),
                  _resident((LANES, GROUP_W))],
        out_specs=out_specs, out_shape=out_shape, scratch_shapes=_lane_scratch(tm, GROUP_W),
        name="comb_bwd", compiler_params=_cparams(("parallel",), 32))(*_hbm(dyab, w_b, comb, lse_tot, _expand_heads()))
    return [tuple(res[3 * g:3 * g + 3]) for g in range(N_GROUPS)]


def _attn_bwd(qkv, col0, g, dcomb, lse_tot, delta):
    dil, sub, _ = qkv.shape
    nb = sub // TB
    heads = HEADS_PER_GROUP

    def body(q_ref, kp, kc, kn, vp, vc, vn, do_ref, lse_ref, dl_ref, bias_ref, dq_ref, dk_ref, dv_ref,
             ak, av, dqt_scr, s_scr, dp_scr, ds_scr, p_scr):
        i = pl.program_id(1)

        @pl.when(i == 0)
        def _():
            ak[...] = jnp.zeros_like(ak)
            av[...] = jnp.zeros_like(av)

        @pl.when(i < nb)
        def _():
            kext = _ext_window(kp, kc, kn)
            vext = _ext_window(vp, vc, vn)
            q = q_ref[...] * ATT_SCALE
            do = do_ref[...]
            lse_t, dl_t = lse_ref[...].T, dl_ref[...].T
            for b in range(SUBS):
                rows = slice(b * TQ, (b + 1) * TQ)
                kwin, vwin = kext[b * TQ:(b + 2) * TQ, :], vext[b * TQ:(b + 2) * TQ, :]
                for h in range(heads):
                    s_scr[b * heads + h] = _dot_nt(_pair(kwin, h), _own_lanes(_pair(q[rows], h), h))
                    dp_scr[b * heads + h] = _dot_nt(_pair(vwin, h), _own_lanes(_pair(do[rows], h), h))
            for b in range(SUBS):
                cols = slice(b * TQ, (b + 1) * TQ)
                block = i * SUBS + b
                bias = bias_ref[jnp.where(block == 0, 1, 0) + jnp.where(block == nb * SUBS - 1, 2, 0)]
                for h in range(heads):
                    k = b * heads + h
                    p = jnp.exp(s_scr[k] + _slope(g, h) * bias - lse_t[h:h + 1, cols])
                    ds_scr[k] = (p * (dp_scr[k] - dl_t[h:h + 1, cols])).astype(BF16)
                    p_scr[k] = p.astype(BF16)
            for b in range(SUBS):
                kwin = kext[b * TQ:(b + 2) * TQ, :]
                for h in range(heads):
                    dqt_scr[h * HEAD_DIM:(h + 1) * HEAD_DIM, b * TQ:(b + 1) * TQ] = _own_rows(
                        _dot_tn(_pair(kwin, h), ds_scr[b * heads + h]), h)
            for b in range(SUBS):
                rows = slice(b * TQ, (b + 1) * TQ)
                acc_rows = slice(TB - RADIUS + b * TQ, TB - RADIUS + (b + 2) * TQ)
                for h in range(0, heads, 2):
                    cols = slice(h * HEAD_DIM, (h + 2) * HEAD_DIM)
                    k = b * heads + h
                    q2 = jnp.concatenate([_own_lanes(_pair(q[rows], h), h), _own_lanes(_pair(q[rows], h), h + 1)], axis=0)
                    do2 = jnp.concatenate([_own_lanes(_pair(do[rows], h), h), _own_lanes(_pair(do[rows], h), h + 1)],
                                          axis=0)
                    ak[acc_rows, cols] += _dot(jnp.concatenate([ds_scr[k], ds_scr[k + 1]], axis=1), q2)
                    av[acc_rows, cols] += _dot(jnp.concatenate([p_scr[k], p_scr[k + 1]], axis=1), do2)
            dq_ref[...] = (dqt_scr[...].T * ATT_SCALE).astype(BF16)

        if nb == 1:
            dk_ref[...] = ak[TB:2 * TB, :].astype(BF16)
            dv_ref[...] = av[TB:2 * TB, :].astype(BF16)
        else:
            dk_ref[...] = ak[0:TB, :].astype(BF16)
            dv_ref[...] = av[0:TB, :].astype(BF16)
            used = 2 * TB + RADIUS
            for acc in (ak, av):
                acc[0:used - TB, :] = acc[TB:used, :]
                acc[used - TB:used, :] = jnp.zeros((TB, GROUP_W), F32)

    def spec(col, shift):
        return pl.BlockSpec((None, TB, GROUP_W), lambda r, i: (r, jnp.clip(i + shift, 0, nb - 1), col))

    tok = pl.BlockSpec((None, TB, GROUP_W), lambda r, i: (r, jnp.minimum(i, nb - 1), 0))
    stat = pl.BlockSpec((None, TB, LANES), lambda r, i: (r, jnp.minimum(i, nb - 1), 0))
    dkv_spec = tok if nb == 1 else pl.BlockSpec((None, TB, GROUP_W), lambda r, i: (r, jnp.maximum(i - 1, 0), 0))
    return pl.pallas_call(
        body, grid=(dil, nb + (nb > 1)),
        in_specs=[spec(col0, 0), spec(col0 + 1, -1), spec(col0 + 1, 0), spec(col0 + 1, 1),
                  spec(col0 + 2, -1), spec(col0 + 2, 0), spec(col0 + 2, 1), tok, stat, stat,
                  pl.BlockSpec((4, 2 * TQ, TQ), lambda r, i: (0, 0, 0))],
        out_specs=[tok, dkv_spec, dkv_spec], out_shape=[jax.ShapeDtypeStruct((dil, sub, GROUP_W), BF16)] * 3,
        scratch_shapes=[pltpu.VMEM((3 * TB, GROUP_W), F32)] * 2 + [pltpu.VMEM((GROUP_W, TB), F32)]
        + [pltpu.VMEM((SUBS * heads, 2 * TQ, TQ), F32)] * 2 + [pltpu.VMEM((SUBS * heads, 2 * TQ, TQ), BF16)] * 2,
        name=f"attn_bwd_g{g}", compiler_params=_cparams(("arbitrary", "arbitrary"), 40))(
            *_hbm(*([qkv] * 7), dcomb, lse_tot, delta, _attn_bias_table(g)))


def _in_bwd_ln0(dgated, dqkv, w_nat, w_dil, dz1, x, g0, *, tm=256):
    S = x.shape[0]
    n_gated, n_in = len(dgated), 3 * N_GROUPS

    def body(*refs):
        g_refs, d_refs = refs[:n_gated], refs[n_gated:n_gated + n_in]
        wn_ref, *wd_refs = refs[n_gated + n_in:n_gated + n_in + N_GROUPS]
        dz_ref, x_ref, g_ref, gx_ref, st_ref, *tmp_ref = refs[n_gated + n_in + N_GROUPS:]
        dh = ALPHA * dz_ref[...]
        col = 0
        for ref in g_refs:
            for k in range(ref.shape[0]):
                dh = dh + _dot_nt(ref[k], wn_ref[:, col:col + D_MODEL])
                col += D_MODEL
        for g, (_, d) in enumerate(GROUPS):
            rows = [jnp.concatenate([d_refs[3 * g + k][r] for k in range(3)], axis=1) for r in range(d)]
            w = wn_ref[:, col:col + QKV_W] if d == 1 else wd_refs[g - 1][...]
            res = _dot_nt(jnp.concatenate(rows, axis=0), w)
            if d == 1:
                dh = dh + res
            else:
                n = tm // d
                dh = dh + _from_residue(lambda r: res[r * n:(r + 1) * n, :], d, tm, tmp_ref)
        xhat, rstd = _ln_stats(x_ref[...])
        gx_ref[...] = _ln_bwd(dh, xhat, rstd, g_ref[...])
        upd = _rows8([jnp.sum(dh * xhat, axis=0, keepdims=True), jnp.sum(dh, axis=0, keepdims=True)], D_MODEL)

        @pl.when(pl.program_id(0) == 0)
        def _():
            st_ref[...] = upd

        @pl.when(pl.program_id(0) != 0)
        def _():
            st_ref[...] += upd

    row = pl.BlockSpec((tm, D_MODEL), lambda i: (i, 0))
    g_specs = [pl.BlockSpec((a.shape[0], tm, D_MODEL), lambda i: (0, i, 0)) for a in dgated]
    d_specs = []
    for _, d in GROUPS:
        d_specs += [_res_spec(d, tm, GROUP_W)] * 3
    operands = list(dgated) + [a for grp in dqkv for a in grp] + [w_nat] + list(w_dil) + [dz1, x, g0]
    return pl.pallas_call(
        body, grid=(S // tm,),
        in_specs=g_specs + d_specs + [_resident(w_nat.shape)] + [_resident(w.shape) for w in w_dil]
        + [row, row, pl.BlockSpec((1, D_MODEL), lambda i: (0, 0))],
        out_specs=[row, pl.BlockSpec((SUBLANES, D_MODEL), lambda i: (0, 0))],
        out_shape=[jax.ShapeDtypeStruct((S, D_MODEL), F32), jax.ShapeDtypeStruct((SUBLANES, D_MODEL), F32)],
        scratch_shapes=_lane_scratch(tm, D_MODEL),
        name="in_bwd_ln0", compiler_params=_cparams(("arbitrary",), 52))(*_hbm(*operands))


HBM_SPEC = pl.BlockSpec(memory_space=pltpu.HBM)


def _place():
    x, y, c = lax.axis_index("x"), lax.axis_index("y"), lax.axis_index("c")
    chips = [(1 - x, y), (x, 1 - y), (1 - x, 1 - y)]
    return x, y, c, chips


def _allgather_shards(shards, after, *, name, collective_id):
    n = len(shards)
    per = 6

    def body(*refs):
        ins, outs = refs[:n], refs[n + len(after):2 * n + len(after)]
        send_sems, recv_sems, loc_sems = refs[2 * n + len(after):]
        x, y, c, chips = _place()
        me = 2 * x + y
        sib = (x, y, 1 - c)
        peers = [sib] + [(px, py, c) for px, py in chips]
        barrier = pltpu.get_barrier_semaphore()
        for peer in peers:
            pl.semaphore_signal(barrier, inc=1, device_id=peer, device_id_type=MESH)
        pl.semaphore_wait(barrier, len(peers))

        def rcopy(w, k, src, dst, to):
            return pltpu.make_async_remote_copy(src_ref=src, dst_ref=dst, send_sem=send_sems.at[per * w + k],
                                                recv_sem=recv_sems.at[per * w + k], device_id=to, device_id_type=MESH)

        split = [s.shape[0] == N_CORES for s in shards]
        half = lambda w: c if split[w] else 0
        local, sends = [], []
        for w in range(n):
            cp = pltpu.make_async_copy(ins[w], outs[w].at[me], loc_sems.at[w])
            cp.start()
            local.append(cp)
            for j, (px, py) in enumerate(chips):
                cp = rcopy(w, j, ins[w].at[half(w)], outs[w].at[me, half(w)], (px, py, c))
                cp.start()
                sends.append(cp)
        for w in range(n):
            for j, (px, py) in enumerate(chips):
                slot = outs[w].at[2 * px + py, half(w)]
                rcopy(w, j, slot, slot, (px, py, c)).wait_recv()
                if split[w]:
                    cp = rcopy(w, 3 + j, slot, slot, sib)
                    cp.start()
                    sends.append(cp)
        for w in range(n):
            if split[w]:
                for j, (px, py) in enumerate(chips):
                    slot = outs[w].at[2 * px + py, 1 - c]
                    rcopy(w, 3 + j, slot, slot, sib).wait_recv()
        for cp in sends:
            cp.wait_send()
        for cp in local:
            cp.wait()

    return pl.kernel(
        body, out_type=[jax.ShapeDtypeStruct((N_CHIPS,) + s.shape, s.dtype) for s in shards],
        mesh=plsc.ScalarSubcoreMesh(axis_name="sequencer", num_cores=1),
        scratch_types=[pltpu.SemaphoreType.DMA((per * n,)), pltpu.SemaphoreType.DMA((per * n,)),
                       pltpu.SemaphoreType.DMA((n,))],
        name=name, compiler_params=pltpu.CompilerParams(collective_id=collective_id))(*shards, *after)


def _exchange_grads(grads, *, name, collective_id):
    n = len(grads)
    per = 7

    def body(*refs):
        ins, outs = refs[:n], refs[n:2 * n]
        send_sems, recv_sems, loc_sems = refs[2 * n:]
        x, y, c, chips = _place()
        me = 2 * x + y
        sib = (x, y, 1 - c)
        peers = [sib] + [(px, py, c) for px, py in chips]
        barrier = pltpu.get_barrier_semaphore()
        for peer in peers:
            pl.semaphore_signal(barrier, inc=1, device_id=peer, device_id_type=MESH)
        pl.semaphore_wait(barrier, len(peers))

        def rcopy(w, k, src, dst, to):
            return pltpu.make_async_remote_copy(src_ref=src, dst_ref=dst, send_sem=send_sems.at[per * w + k],
                                                recv_sem=recv_sems.at[per * w + k], device_id=to, device_id_type=MESH)

        local, sends = [], []
        for w in range(n):
            cp = pltpu.make_async_copy(ins[w].at[me], outs[w].at[c, me], loc_sems.at[w])
            cp.start()
            local.append(cp)
            cp = rcopy(w, 0, ins[w].at[me], outs[w].at[c, me], sib)
            cp.start()
            sends.append(cp)
            for j, (px, py) in enumerate(chips):
                cp = rcopy(w, 1 + j, ins[w].at[2 * px + py], outs[w].at[c, me], (px, py, c))
                cp.start()
                sends.append(cp)
        for w in range(n):
            for j, (px, py) in enumerate(chips):
                slot = outs[w].at[c, 2 * px + py]
                rcopy(w, 1 + j, slot, slot, (px, py, c)).wait_recv()
                cp = rcopy(w, 4 + j, slot, slot, sib)
                cp.start()
                sends.append(cp)
        for w in range(n):
            slot = outs[w].at[1 - c, me]
            rcopy(w, 0, slot, slot, sib).wait_recv()
            for j, (px, py) in enumerate(chips):
                slot = outs[w].at[1 - c, 2 * px + py]
                rcopy(w, 4 + j, slot, slot, sib).wait_recv()
        for cp in sends:
            cp.wait_send()
        for cp in local:
            cp.wait()

    return pl.kernel(
        body, out_type=[jax.ShapeDtypeStruct((N_CORES,) + g.shape, g.dtype) for g in grads],
        mesh=plsc.ScalarSubcoreMesh(axis_name="sequencer", num_cores=1),
        scratch_types=[pltpu.SemaphoreType.DMA((per * n,)), pltpu.SemaphoreType.DMA((per * n,)),
                       pltpu.SemaphoreType.DMA((n,))],
        name=name, compiler_params=pltpu.CompilerParams(collective_id=collective_id))(*grads)


def _allgather_small(vec, after):
    def body(v_ref, _, o_ref, send_sems, recv_sems, loc_sem):
        x, y, c = lax.axis_index("x"), lax.axis_index("y"), lax.axis_index("c")
        me = 4 * x + 2 * y + c

        def peer(k):
            flip = lambda v, bit: 1 - v if (k >> bit) & 1 else v
            return flip(x, 2), flip(y, 1), flip(c, 0)

        loc = pltpu.make_async_copy(v_ref, o_ref.at[me], loc_sem)
        loc.start()
        sends = []
        for k in range(1, N_DEV):
            cp = pltpu.make_async_remote_copy(src_ref=v_ref, dst_ref=o_ref.at[me], send_sem=send_sems.at[k - 1],
                                              recv_sem=recv_sems.at[k - 1], device_id=peer(k), device_id_type=MESH)
            cp.start()
            sends.append(cp)
        for k in range(1, N_DEV):
            px, py, pc = peer(k)
            pltpu.make_async_remote_copy(src_ref=v_ref, dst_ref=o_ref.at[4 * px + 2 * py + pc],
                                         send_sem=send_sems.at[k - 1], recv_sem=recv_sems.at[k - 1],
                                         device_id=(px, py, pc), device_id_type=MESH).wait_recv()
        for cp in sends:
            cp.wait_send()
        loc.wait()

    return pl.pallas_call(
        body, in_specs=[HBM_SPEC, HBM_SPEC], out_specs=HBM_SPEC,
        out_shape=jax.ShapeDtypeStruct((N_DEV,) + vec.shape, vec.dtype),
        scratch_shapes=[pltpu.SemaphoreType.DMA((N_DEV - 1,)), pltpu.SemaphoreType.DMA((N_DEV - 1,)),
                        pltpu.SemaphoreType.DMA],
        name="allgather_small")(vec, after)


def _adamw(w, g, m, v):
    m = ADAM_B1 * m + (1.0 - ADAM_B1) * g
    v = ADAM_B2 * v + (1.0 - ADAM_B2) * (g * g)
    m_hat = m / (1.0 - ADAM_B1 ** ADAM_STEP)
    v_hat = v / (1.0 - ADAM_B2 ** ADAM_STEP)
    delta = -ADAM_LR * (m_hat / (jnp.sqrt(v_hat) + ADAM_EPS) + ADAM_WD * w)
    return delta, m, v


def _reduce_adamw(parts, w, m, v, *, tr, name):
    R, C = w.shape

    def body(p_ref, w_ref, m_ref, v_ref, g_ref, d_ref, nm_ref, nv_ref):
        def core_sum(cc):
            s = p_ref[cc, 0].astype(F32)
            for k in range(1, N_CHIPS):
                s = s + p_ref[cc, k].astype(F32)
            return s

        g = core_sum(0) + core_sum(1)
        delta, nm, nv = _adamw(w_ref[...], g, m_ref[...], v_ref[...])
        g_ref[...] = g
        d_ref[...] = delta
        nm_ref[...] = nm
        nv_ref[...] = nv

    blk = pl.BlockSpec((tr, C), lambda i: (i, 0))
    return pl.pallas_call(
        body, grid=(R // tr,),
        in_specs=[pl.BlockSpec((N_CORES, N_CHIPS, tr, C), lambda i: (0, 0, i, 0)), blk, blk, blk],
        out_specs=[blk] * 4, out_shape=[jax.ShapeDtypeStruct((R, C), F32)] * 4,
        name=name, compiler_params=_cparams(("parallel",), 40))(*_hbm(parts, w, m, v))


def _reduce_adamw_vectors(allv, offs, ws, ms, vs):
    n = len(ws)

    def body(a_ref, *refs):
        w_refs, m_refs, v_refs = refs[:n], refs[n:2 * n], refs[2 * n:3 * n]
        tot_ref, outs = refs[3 * n], refs[3 * n + 1:]
        s = a_ref[0]
        for d in range(1, N_DEV):
            s = s + a_ref[d]
        tot_ref[...] = s
        for k in range(n):
            g = s[:, offs[k]:offs[k] + w_refs[k].shape[1]]
            delta, nm, nv = _adamw(w_refs[k][...], g, m_refs[k][...], v_refs[k][...])
            for ref, val in zip(outs[4 * k:4 * k + 4], (g, delta, nm, nv)):
                ref[...] = val

    out_shape = [jax.ShapeDtypeStruct(allv.shape[1:], F32)]
    for w in ws:
        out_shape += [jax.ShapeDtypeStruct(w.shape, F32)] * 4
    res = pl.pallas_call(body, out_shape=out_shape, name="reduce_adamw_vectors",
                         compiler_params=_cparams((), 40))(allv, *ws, *ms, *vs)
    return res[0], [tuple(res[1 + 4 * k:5 + 4 * k]) for k in range(n)]


def _adamw_taps(ws, gs, ms, vs):
    n = len(ws)

    def body(*refs):
        outs = refs[4 * n:]
        for k in range(n):
            res = _adamw(refs[k][...], refs[n + k][...], refs[2 * n + k][...], refs[3 * n + k][...])
            for ref, val in zip(outs[3 * k:3 * k + 3], res):
                ref[...] = val

    out_shape = []
    for w in ws:
        out_shape += [jax.ShapeDtypeStruct(w.shape, F32)] * 3
    res = pl.pallas_call(body, out_shape=out_shape, name="adamw_taps")(*ws, *gs, *ms, *vs)
    return [tuple(res[3 * k:3 * k + 3]) for k in range(n)]


def _pack(pieces):
    flat, offs, n = [], [], 0
    for p in pieces:
        size = -(-p.size // LANES) * LANES
        flat.append(jnp.pad(p.reshape(-1), (0, size - p.size)))
        offs.append(n)
        n += size
    return jnp.concatenate(flat).reshape(1, n), offs


def _local_step(x, target, p, wfull, on_ready=lambda group: None, before_ln0=()):
    S = x.shape[0]
    dils = [d for _, d in GROUPS]

    h0, h0b, *h0_res = _ln0_fwd(x, p["ln0_g"], p["ln0_b"], before_ln0)
    h0_rows = [h0b] + [h.reshape(S, D_MODEL) for h in h0_res]

    if isinstance(wfull, dict):
        w_in3, pending = wfull["w_in"], None
    else:
        w_in3, launch_rest, assemble = wfull
        w_in3, h0b = lax.optimization_barrier((w_in3, h0b))
        pending = launch_rest(h0b)

    runs = _col_runs()
    w_perm = jnp.concatenate([w_in3[s, :, c:c + w] for s, c, _, w in runs], axis=1)
    b_blocks = p["b_in"].reshape(N_BLK, GROUP_W)
    b_perm = jnp.concatenate([b_blocks[b] for b in PERM]).reshape(1, N_IN)
    w_nat, b_nat = w_perm[:, :N_NAT], b_perm[:, :N_NAT]
    qkv_cols = [slice(P_Q0 + g * QKV_W, P_Q0 + (g + 1) * QKV_W) for g in range(N_GROUPS)]
    w_qkv = [w_perm[:, c] for c in qkv_cols]

    proj = _mm_nn(h0b, w_nat, b_nat, tm=512, tn=N_NAT // 2, out_dtype=BF16, name="proj")
    qkv = [proj[None]]
    for g in range(1, N_GROUPS):
        t = _mm_nn(h0_rows[g], w_qkv[g], b_perm[:, qkv_cols[g]], tm=512, tn=QKV_W, out_dtype=BF16, name=f"proj_qkv{g}")
        qkv.append(t.reshape(dils[g], S // dils[g], QKV_W))
    if pending is not None:
        pending, qkv = lax.optimization_barrier((pending, qkv))
        proj = qkv[0][0]
        wfull = assemble(pending)
    w_up3 = wfull["w_up"]
    w_a, w_o, w_down, w_b = wfull["w_a"], wfull["w_o"], wfull["w_down"], wfull["w_b"]
    conv_w, ffn_conv_w = wfull["conv_w"], wfull["ffn_conv_w"]
    col0 = [P_Q0 // GROUP_W] + [0] * (N_GROUPS - 1)
    ya_in = _conv_gate_fwd(proj, conv_w)
    att = [_attn_fwd(qkv[g], col0[g], g) for g in range(N_GROUPS)]
    comb, comb_b, lse_tot = _attn_combine([a[0] for a in att], [a[1] for a in att])
    yab, mixin = _branch_mix(ya_in, comb_b, w_a, w_b, proj)
    xhat1, rstd1, h1b = _mix_ln1(mixin, w_o, p["b_o"], h0, p["ln1_g"], p["ln1_b"])
    up = _mm_nn(h1b, w_up3, p["b_up"], tm=512, tn=2 * w_up3.shape[2], out_dtype=BF16, name="up")
    f = _ffn_conv_fwd(up, ffn_conv_w, p["ffn_conv_b"])
    dz2, dz2b, st2 = _down_ln2_loss(f, w_down, p["b_down"], xhat1, p["ln1_g"], p["ln1_b"],
                                    p["ln2_g"], p["ln2_b"], target)

    gw = {}
    gw["w_down"] = _mm_tn(f, dz2b, n_out=1, tn=D_MODEL, ts=1024, g_block=(1024, D_MODEL),
                          g_map=lambda j, s: (s, 0), name="grad_w_down").reshape(N_CHIPS, D_FF // N_CHIPS, D_MODEL)
    df = _mm_nt(dz2b, w_down, tm=512, name="df")
    dup, sm_ffn = _ffn_conv_bwd(up, df, ffn_conv_w, p["ffn_conv_b"])
    gw["w_up"] = _mm_tn(h1b, dup, n_out=dup.shape[0], tn=D_FF, ts=1024, g_block=(None, 1024, D_FF),
                        g_map=lambda j, s: (j, s, 0), split=D_FF // w_up3.shape[2], name="grad_w_up")
    exchanged = on_ready({n: gw[n] for n in ("w_down", "w_up")}) or {}
    dz1, dz1b, st1 = _up_bwd_ln1(dup, w_up3, dz2, xhat1, rstd1, p["ln1_g"])

    gw["w_o"] = _mm_tn(mixin, dz1b, n_out=1, tn=D_MODEL, ts=512, g_block=(512, D_MODEL),
                       g_map=lambda j, s: (s, 0), name="grad_w_o").reshape(N_CHIPS, D_MODEL // N_CHIPS, D_MODEL)
    dyab, dgab = _mix_bwd(dz1b, w_o, proj, yab)
    gw["w_a"] =_mm_tn(ya_in, dyab, n_out=1, tn=D_MODEL, ts=512, g_block=(512, D_MODEL),
                       g_map=lambda j, s: (s, 0), name="grad_w_a").reshape(N_CHIPS, D_CONV // N_CHIPS, D_MODEL)
    gw_b = _mm_tn(comb_b, dyab, n_out=1, tn=D_MODEL, ts=1024, g_block=(1024, D_MODEL),
                  g_map=lambda j, s: (s, 1), name="grad_w_b")
    gw["w_b"] = gw_b.reshape(GROUP_W, N_CHIPS, D_MODEL // N_CHIPS).transpose(1, 0, 2)
    exchanged_mix = on_ready({n: gw[n] for n in ("w_o", "w_a", "w_b")}) or {}
    dya_in = _mm_nt(dyab, w_a, tm=512, a_col=0, name="dya_in")
    exchanged, dya_in = lax.optimization_barrier((exchanged, dya_in))
    dbch, sm_conv = _conv_gate_bwd(proj, dya_in, conv_w)
    att_stats = _comb_bwd(dyab, w_b, comb, lse_tot)
    exchanged_mix, att_stats = lax.optimization_barrier((exchanged_mix, att_stats))
    exchanged.update(exchanged_mix)
    dqkv = [_attn_bwd(qkv[g], col0[g], g, *att_stats[g]) for g in range(N_GROUPS)]

    w_pieces, b_pieces = [], []
    for nm, planes in (("bch", dbch), ("gab", dgab)):
        pw, pc = _mm_tn(h0b, planes, n_out=planes.shape[0], tn=D_MODEL, ts=1024, g_block=(None, 1024, D_MODEL),
                        g_map=lambda j, s: (j, s, 0), colsum=True, name="grad_w_in_" + nm)
        w_pieces.extend(pw[k] for k in range(planes.shape[0]))
        b_pieces.append(pc[0])
    for g in range(N_GROUPS):
        pw, pc = _mm_tn_cat(h0_rows[g], [a.reshape(S, GROUP_W) for a in dqkv[g]], ts=1024, name=f"grad_w_in_qkv{g}")
        w_pieces.append(pw)
        b_pieces.append(pc[0])
    dw_perm = jnp.concatenate(w_pieces, axis=1)
    gw["w_in"] = jnp.stack([
        jnp.concatenate([dw_perm[:, pc:pc + w] for s, c, pc, w in sorted(runs, key=lambda r: r[1]) if s == k], axis=1)
        for k in range(N_CHIPS)])
    exchanged.update(on_ready({"w_in": gw["w_in"]}) or {})
    db_blocks = jnp.concatenate(b_pieces).reshape(N_BLK, GROUP_W)
    grad_b_in = jnp.concatenate([db_blocks[b] for b in INV_PERM])

    grad_x, st0 = _in_bwd_ln0([dbch, dgab], dqkv, w_nat, w_qkv[1:], dz1, x, p["ln0_g"])

    small = {
        "loss": st2[2:3, 0:1],
        "ln0_g": st0[0], "ln0_b": st0[1], "b_in": grad_b_in, "conv_w": sm_conv[0:3],
        "b_o": st1[2], "ln1_g": st1[0], "ln1_b": st1[1],
        "b_up": jnp.concatenate([sm_ffn[0], sm_ffn[1]]), "ffn_conv_w": sm_ffn[3:6], "ffn_conv_b": sm_ffn[2],
        "b_down": st2[3], "ln2_g": st2[0], "ln2_b": st2[1],
    }
    return grad_x, exchanged or gw, small


BIG =("w_in", "w_a", "w_b", "w_o", "w_up", "w_down")
CONV = ("conv_w", "ffn_conv_w")
VECS = ("ln0_g", "ln0_b", "b_in", "b_o", "ln1_g", "ln1_b", "b_up", "ffn_conv_b", "b_down", "ln2_g", "ln2_b")
ORDER = ("ln0_g", "ln0_b", "w_in", "b_in", "conv_w", "w_a", "w_b", "w_o", "b_o", "ln1_g", "ln1_b", "w_up", "b_up",
         "ffn_conv_w", "ffn_conv_b", "w_down", "b_down", "ln2_g", "ln2_b")
SMALL_ORDER = ("loss",) + VECS + CONV


def _step(x, target, W, Mo, Vo):
    x2, t2 = x[0], target[0]
    big2 = {n: W[n][0] for n in BIG}
    halves = lambda a: a.astype(BF16).reshape(N_CORES, a.shape[0] // N_CORES, a.shape[1])
    whole = lambda g: g.reshape(N_CHIPS, g.shape[1] * g.shape[2], g.shape[3])
    later = tuple(n for n in BIG if n != "w_in")
    w_in_halves = halves(big2["w_in"])
    first = _allgather_shards([w_in_halves], [], name="allgather_w_in", collective_id=1)

    def launch_rest(h0b):
        return _allgather_shards([halves(big2[n]) for n in later] + [W[n] for n in CONV], [h0b],
                                 name="allgather_rest", collective_id=2)

    def assemble(rest):
        gathered = {n: whole(g) for n, g in zip(later + CONV, rest)}
        return {
            "w_up": gathered["w_up"],
            "w_a": gathered["w_a"].reshape(D_CONV, D_MODEL), "w_o": gathered["w_o"].reshape(D_MODEL, D_MODEL),
            "w_down": gathered["w_down"].reshape(D_FF, D_MODEL),
            "w_b": gathered["w_b"].transpose(1, 0, 2).reshape(GROUP_W, D_MODEL),
            "conv_w": gathered["conv_w"].transpose(1, 0, 2).reshape(3, D_CONV),
            "ffn_conv_w": gathered["ffn_conv_w"].transpose(1, 0, 2).reshape(3, D_FF),
        }

    pvec = {n: W[n].reshape(1, -1) for n in VECS}

    exchange_ids = iter((3, 4, 5))

    def exchange(group):
        names = tuple(group)
        res = _exchange_grads([group[n] for n in names], name="exchange_" + "_".join(names),
                              collective_id=next(exchange_ids))
        return dict(zip(names, res))

    grad_x, parts, small = _local_step(x2, t2, pvec, (whole(first[0]), launch_rest, assemble), exchange,
                                       before_ln0=[w_in_halves])
    out = {}
    for n in BIG:
        tr = {"w_in": 128, "w_up": 128, "w_b": 128}.get(n, big2[n].shape[0] // 4)
        g, d, nm, nv = _reduce_adamw(parts[n], big2[n], Mo[n][0], Vo[n][0], tr=tr, name="adamw_" + n)
        out[n] = tuple(a[None] for a in (g, d, nm, nv))

    vec, offs = _pack([small[n] for n in SMALL_ORDER])
    off = dict(zip(SMALL_ORDER, offs))
    row = lambda a: a.reshape(1, -1)
    allv = _allgather_small(vec, parts["w_in"])
    tot, vec_out = _reduce_adamw_vectors(allv, [off[n] for n in VECS], [row(W[n]) for n in VECS],
                                         [row(Mo[n]) for n in VECS], [row(Vo[n]) for n in VECS])
    for n, res in zip(VECS, vec_out):
        out[n] = tuple(a.reshape(W[n].shape) for a in res)
    loss = tot[0, off["loss"]]
    chip = 2 * lax.axis_index("x") + lax.axis_index("y")
    taps_g = []
    for n in CONV:
        width = W[n].shape[2]
        full = lax.slice(tot, (0, off[n]), (1, off[n] + 3 * N_CHIPS * width)).reshape(3, N_CHIPS * width)
        taps_g.append(lax.dynamic_slice_in_dim(full, chip * width, width, axis=1))
    taps_out = _adamw_taps([W[n][0] for n in CONV], taps_g, [Mo[n][0] for n in CONV], [Vo[n][0] for n in CONV])
    for n, g, res in zip(CONV, taps_g, taps_out):
        out[n] = tuple(a[None] for a in (g,) + res)

    res = [loss, grad_x[None]]
    for k in range(4):
        res += [out[n][k] for n in ORDER]
    return tuple(res)


def kernel(x, ln0_g, ln0_b, w_in, b_in, conv_w, w_a, w_b, w_o, b_o, ln1_g, ln1_b, w_up, b_up, ffn_conv_w, ffn_conv_b, w_down, b_down, ln2_g, ln2_b, loss_target, m_ln0_g, m_ln0_b, m_w_in, m_b_in, m_conv_w, m_w_a, m_w_b, m_w_o, m_b_o, m_ln1_g, m_ln1_b, m_w_up, m_b_up, m_ffn_conv_w, m_ffn_conv_b, m_w_down, m_b_down, m_ln2_g, m_ln2_b, v_ln0_g, v_ln0_b, v_w_in, v_b_in, v_conv_w, v_w_a, v_w_b, v_w_o, v_b_o, v_ln1_g, v_ln1_b, v_w_up, v_b_up, v_ffn_conv_w, v_ffn_conv_b, v_w_down, v_b_down, v_ln2_g, v_ln2_b):
    W = dict(zip(ORDER, (ln0_g, ln0_b, w_in, b_in, conv_w, w_a, w_b, w_o, b_o, ln1_g, ln1_b, w_up, b_up,
                         ffn_conv_w, ffn_conv_b, w_down, b_down, ln2_g, ln2_b)))
    Mo = dict(zip(ORDER, (m_ln0_g, m_ln0_b, m_w_in, m_b_in, m_conv_w, m_w_a, m_w_b, m_w_o, m_b_o, m_ln1_g, m_ln1_b,
                          m_w_up, m_b_up, m_ffn_conv_w, m_ffn_conv_b, m_w_down, m_b_down, m_ln2_g, m_ln2_b)))
    Vo = dict(zip(ORDER, (v_ln0_g, v_ln0_b, v_w_in, v_b_in, v_conv_w, v_w_a, v_w_b, v_w_o, v_b_o, v_ln1_g, v_ln1_b,
                          v_w_up, v_b_up, v_ffn_conv_w, v_ffn_conv_b, v_w_down, v_b_down, v_ln2_g, v_ln2_b)))
    return _step(x, loss_target, W, Mo, Vo)
```

```python
import functools
import math

import jax
import jax.numpy as jnp
from jax import lax
from jax.experimental import pallas as pl
from jax.experimental.pallas import tpu as pltpu
from jax.experimental.pallas import tpu_sc as plsc

F32 = jnp.float32
BF16 = jnp.bfloat16

D_MODEL = 1024
D_CONV = D_MODEL
HEAD_DIM = 64
HEADS_PER_GROUP = 8
GROUPS = ((128, 1), (512, 4), (2048, 16))
N_GROUPS = len(GROUPS)
GROUP_W = HEADS_PER_GROUP * HEAD_DIM
QKV_W = N_GROUPS * GROUP_W
RADIUS = 64
D_FF = 2816
LN_EPS = 1e-5
ALPHA = 2.0 ** 0.25
MASK_VALUE = -1e30
ATT_SCALE = HEAD_DIM ** -0.5
OFF_B = 0
OFF_C = OFF_B + D_CONV
OFF_H = OFF_C + D_CONV
OFF_Q = OFF_H + D_CONV
OFF_K = OFF_Q + QKV_W
OFF_V = OFF_K + QKV_W
OFF_GA = OFF_V + QKV_W
OFF_GB = OFF_GA + D_MODEL
N_IN = OFF_GB + D_MODEL
ADAM_LR = 0.001
ADAM_B1 = 0.9
ADAM_B2 = 0.999
ADAM_EPS = 1e-08
ADAM_WD = 0.01
ADAM_STEP = 10
INV_SQRT2 = 0.7071067811865476
INV_SQRT_2PI = 0.3989422804014327
LOG2_E = 1.4426950408889634

LANES = 128
SUBLANES = 8
VMEM_BYTES_V7X = 64 * 1024 * 1024
N_CHIPS = 4
N_CORES = 2
N_DEV = N_CHIPS * N_CORES
MESH = pl.DeviceIdType.MESH

N_BLK = N_IN // GROUP_W
PERM = (0, 1, 2, 3, 4, 5, 15, 16, 17, 18, 6, 9, 12, 7, 10, 13, 8, 11, 14)
INV_PERM = tuple(PERM.index(b) for b in range(N_BLK))
P_B, P_C, P_H, P_GA, P_GB, P_Q0 = 0, 1024, 2048, 3072, 4096, 5120
N_NAT = P_Q0 + QKV_W // N_GROUPS * 3
N_GATED = P_Q0

def _col_runs():
    shard_w = N_IN // N_CHIPS
    runs = []
    for pos, blk in enumerate(PERM):
        c, end = blk * GROUP_W, (blk + 1) * GROUP_W
        while c < end:
            stop = min(end, (c // shard_w + 1) * shard_w)
            run = (c // shard_w, c % shard_w, pos * GROUP_W + c - blk * GROUP_W, stop - c)
            if runs and runs[-1][0] == run[0] and runs[-1][1] + runs[-1][3] == run[1]:
                runs[-1] = runs[-1][:3] + (runs[-1][3] + run[3],)
            else:
                runs.append(run)
            c = stop
    return runs


SLAB = 128
CHUNK = 256
PAD = SUBLANES
TQ = 128


def _cparams(sem, vmem_mb):
    assert vmem_mb * 1024 * 1024 < VMEM_BYTES_V7X
    return pltpu.CompilerParams(dimension_semantics=sem, vmem_limit_bytes=vmem_mb * 1024 * 1024)


def _resident(shape):
    nd = len(shape)
    return pl.BlockSpec(shape, lambda *_: (0,) * nd, pipeline_mode=pl.Buffered(1))


def _hbm(*arrays):
    return [pltpu.with_memory_space_constraint(a, pltpu.HBM) for a in arrays]


def _dot(a, b):
    return jnp.dot(a, b, preferred_element_type=F32)


def _dot_nt(a, b):
    return lax.dot_general(a, b, (((1,), (1,)), ((), ())), preferred_element_type=F32)


def _dot_tn(a, b):
    return lax.dot_general(a, b, (((0,), (0,)), ((), ())), preferred_element_type=F32)


def _ln_stats(z):
    mu = jnp.mean(z, -1, keepdims=True)
    zc = z - mu
    var = jnp.mean(zc * zc, -1, keepdims=True)
    rstd = lax.rsqrt(var + LN_EPS)
    return zc * rstd, rstd


def _ln_bwd(dh, xhat, rstd, g):
    dxh = dh * g
    m1 = jnp.mean(dxh, -1, keepdims=True)
    m2 = jnp.mean(dxh * xhat, -1, keepdims=True)
    return rstd * (dxh - m1 - xhat * m2)


def _rows8(rows, width):
    pad = [jnp.zeros((1, width), F32)] * (SUBLANES - len(rows))
    return jnp.concatenate(list(rows) + pad, axis=0)


def _mm_nn(a, w, bias, *, tm, tn, out_dtype, name, vmem_mb=40):
    M, K = a.shape
    if w.ndim == 3:
        per = tn // w.shape[2]
        assert per * w.shape[2] == tn and w.shape[0] % per == 0
        n_tiles = w.shape[0] // per
        w_spec = pl.BlockSpec((per, K, w.shape[2]), lambda j, i: (j, 0, 0))
    else:
        per = 0
        n_tiles = w.shape[1] // tn
        w_spec = pl.BlockSpec((K, tn), lambda j, i: (0, j))

    def body(a_ref, w_ref, b_ref, o_ref):
        wv = jnp.concatenate([w_ref[k] for k in range(per)], axis=1) if per else w_ref[...]
        o_ref[...] = (_dot(a_ref[...], wv) + b_ref[...]).astype(o_ref.dtype)

    return pl.pallas_call(
        body, grid=(n_tiles, M // tm),
        in_specs=[pl.BlockSpec((tm, K), lambda j, i: (i, 0)), w_spec, pl.BlockSpec((1, tn), lambda j, i: (0, j))],
        out_specs=pl.BlockSpec((tm, tn), lambda j, i: (i, j)),
        out_shape=jax.ShapeDtypeStruct((M, n_tiles * tn), out_dtype),
        name=name, compiler_params=_cparams(("arbitrary", "parallel"), vmem_mb))(*_hbm(a, w, bias))


def _mm_nt(a, w, *, tm, a_col=0, name, vmem_mb=40):
    M = a.shape[0]
    N, K = w.shape

    def body(a_ref, w_ref, o_ref):
        o_ref[...] = _dot_nt(a_ref[...], w_ref[...]).astype(o_ref.dtype)

    return pl.pallas_call(
        body, grid=(M // tm,),
        in_specs=[pl.BlockSpec((tm, K), lambda i: (i, a_col)),
                  pl.BlockSpec((N, K), lambda i: (0, 0))],
        out_specs=pl.BlockSpec((tm, N), lambda i: (i, 0)),
        out_shape=jax.ShapeDtypeStruct((M, N), BF16),
        name=name, compiler_params=_cparams(("parallel",), vmem_mb))(*_hbm(a, w))


def _mm_tn(a, g, *, n_out, tn, ts, g_block, g_map, colsum=False, split=1, name, vmem_mb=48):
    S, K = a.shape
    n_s = S // ts
    shard_w = tn // split

    def body(a_ref, g_ref, *rest):
        if colsum:
            o_ref, cs_ref, acc_ref, cacc_ref = rest
        else:
            o_ref, acc_ref = rest
        s = pl.program_id(1)

        @pl.when(s == 0)
        def _():
            acc_ref[...] = jnp.zeros_like(acc_ref)
            if colsum:
                cacc_ref[...] = jnp.zeros_like(cacc_ref)

        gv = g_ref[...]
        acc_ref[...] += _dot_tn(a_ref[...], gv)
        if colsum:
            cacc_ref[...] += jnp.broadcast_to(jnp.sum(gv.astype(F32), axis=0, keepdims=True), cacc_ref.shape)

        @pl.when(s == n_s - 1)
        def _():
            for k in range(split):
                o_ref[k] = acc_ref[:, k * shard_w:(k + 1) * shard_w].astype(o_ref.dtype)
            if colsum:
                cs_ref[...] = cacc_ref[...]

    out_specs = [pl.BlockSpec((split, K, shard_w), lambda j, s: (j, 0, 0))]
    out_shape = [jax.ShapeDtypeStruct((n_out * split, K, shard_w), BF16)]
    scratch = [pltpu.VMEM((K, tn), F32)]
    if colsum:
        out_specs.append(pl.BlockSpec((SUBLANES, tn), lambda j, s: (0, j)))
        out_shape.append(jax.ShapeDtypeStruct((SUBLANES, n_out * tn), F32))
        scratch.append(pltpu.VMEM((SUBLANES, tn), F32))
    res = pl.pallas_call(
        body, grid=(n_out, n_s),
        in_specs=[pl.BlockSpec((ts, K), lambda j, s: (s, 0)), pl.BlockSpec(g_block, g_map)],
        out_specs=out_specs, out_shape=out_shape, scratch_shapes=scratch,
        name=name, compiler_params=_cparams(("parallel", "arbitrary"), vmem_mb))(*_hbm(a, g))
    return res if colsum else res[0]


def _mm_tn_cat(a, gs, *, ts, name, vmem_mb=40):
    S, K = a.shape
    widths = [g.shape[1] for g in gs]
    n_s, total = S // ts, sum(widths)

    def body(*refs):
        a_ref, g_refs = refs[0], refs[1:1 + len(gs)]
        o_ref, cs_ref, acc_ref, cacc_ref = refs[1 + len(gs):]
        s = pl.program_id(0)

        @pl.when(s == 0)
        def _():
            acc_ref[...] = jnp.zeros_like(acc_ref)
            cacc_ref[...] = jnp.zeros_like(cacc_ref)

        av, col = a_ref[...], 0
        for g_ref, w in zip(g_refs, widths):
            gv = g_ref[...]
            acc_ref[:, col:col + w] += _dot_tn(av, gv)
            cacc_ref[:, col:col + w] += jnp.broadcast_to(jnp.sum(gv.astype(F32), axis=0, keepdims=True), (SUBLANES, w))
            col += w

        @pl.when(s == n_s - 1)
        def _():
            o_ref[...] = acc_ref[...].astype(BF16)
            cs_ref[...] = cacc_ref[...]

    return pl.pallas_call(
        body, grid=(n_s,),
        in_specs=[pl.BlockSpec((ts, K), lambda s: (s, 0))] + [pl.BlockSpec((ts, w), lambda s: (s, 0)) for w in widths],
        out_specs=[pl.BlockSpec((K, total), lambda s: (0, 0)), pl.BlockSpec((SUBLANES, total), lambda s: (0, 0))],
        out_shape=[jax.ShapeDtypeStruct((K, total), BF16), jax.ShapeDtypeStruct((SUBLANES, total), F32)],
        scratch_shapes=[pltpu.VMEM((K, total), F32), pltpu.VMEM((SUBLANES, total), F32)],
        name=name, compiler_params=_cparams(("arbitrary",), vmem_mb))(*_hbm(a, *gs))


DILS = tuple(d for _, d in GROUPS if d > 1)


def _res_spec(d, tm, width):
    return pl.BlockSpec((d, tm // d, width), lambda i: (0, i, 0))


def _lane_scratch(tm, width):
    return [pltpu.VMEM((tm, LANES), F32)] * (width // LANES)


def _to_residue(val, dst_refs, dils, tm, dtype, scr):
    for c, ref in enumerate(scr):
        ref[...] = val[:, c * LANES:(c + 1) * LANES]
    for dst_ref, d in zip(dst_refs, dils):
        for r in range(d):
            cols = [ref[pl.ds(r, tm // d, stride=d), :] for ref in scr]
            dst_ref[r] = jnp.concatenate(cols, axis=1).astype(dtype)


def _from_residue(rows_of, d, tm, scr):
    for r in range(d):
        v = rows_of(r).astype(F32)
        for c, ref in enumerate(scr):
            ref[pl.ds(r, tm // d, stride=d), :] = v[:, c * LANES:(c + 1) * LANES]
    return jnp.concatenate([ref[...] for ref in scr], axis=1)


def _ln0_fwd(x, g, b, after=(), *, tm=512):
    S, Dm = x.shape
    n_after = len(after)

    def body(x_ref, g_ref, b_ref, *rest):
        h_ref, hb_ref, *rest = rest[n_after:]
        xhat, _ = _ln_stats(x_ref[...])
        h = xhat * g_ref[...] + b_ref[...]
        h_ref[...] = h
        hb_ref[...] = h.astype(BF16)
        _to_residue(h, rest[:len(DILS)], DILS, tm, BF16, rest[len(DILS):])

    row = pl.BlockSpec((tm, Dm), lambda i: (i, 0))
    vec = pl.BlockSpec((1, Dm), lambda i: (0, 0))
    return pl.pallas_call(
        body, grid=(S // tm,), in_specs=[row, vec, vec] + [pl.BlockSpec(memory_space=pl.ANY)] * n_after,
        out_specs=[row, row] + [_res_spec(d, tm, Dm) for d in DILS],
        out_shape=[jax.ShapeDtypeStruct((S, Dm), F32), jax.ShapeDtypeStruct((S, Dm), BF16)]
        + [jax.ShapeDtypeStruct((d, S // d, Dm), BF16) for d in DILS],
        scratch_shapes=_lane_scratch(tm, Dm),
        name="ln0_fwd", compiler_params=_cparams(("parallel",), 32))(*_hbm(x, g, b), *after)


def _slab_spec(S, col0):
    return pl.BlockSpec((S, SLAB), lambda j: (0, col0 // SLAB + j))


def _zero_pads(scr, S):
    scr[0:PAD, :] = jnp.zeros((PAD, SLAB), F32)
    scr[S + PAD:S + 2 * PAD, :] = jnp.zeros((PAD, SLAB), F32)


def _shifted(scr, t):
    return (scr[PAD - 1 + t:PAD - 1 + t + CHUNK, :], scr[PAD + t:PAD + t + CHUNK, :],
            scr[PAD + 1 + t:PAD + 1 + t + CHUNK, :])


def _conv_gate_fwd(proj, conv_w):
    S = proj.shape[0]

    def body(b_ref, c_ref, h_ref, w_ref, o_ref, u_scr):
        _zero_pads(u_scr, S)
        for t in range(0, S, CHUNK):
            u_scr[PAD + t:PAD + t + CHUNK, :] = c_ref[t:t + CHUNK, :].astype(F32) * h_ref[t:t + CHUNK, :].astype(F32)
        w0, w1, w2 = w_ref[0:1, :], w_ref[1:2, :], w_ref[2:3, :]
        for t in range(0, S, CHUNK):
            um, u0, up = _shifted(u_scr, t)
            cv = w0 * um + w1 * u0 + w2 * up
            o_ref[t:t + CHUNK, :] = (b_ref[t:t + CHUNK, :].astype(F32) * cv).astype(BF16)

    return pl.pallas_call(
        body, grid=(D_CONV // SLAB,),
        in_specs=[_slab_spec(S, P_B), _slab_spec(S, P_C), _slab_spec(S, P_H),
                  pl.BlockSpec((3, SLAB), lambda j: (0, j))],
        out_specs=pl.BlockSpec((S, SLAB), lambda j: (0, j)),
        out_shape=jax.ShapeDtypeStruct((S, D_CONV), BF16),
        scratch_shapes=[pltpu.VMEM((S + 2 * PAD, SLAB), F32)],
        name="conv_gate_fwd", compiler_params=_cparams(("parallel",), 40))(*_hbm(proj, proj, proj, conv_w))


MASKED_DISTANCE = -1e34


def _attn_bias_table(g):
    dil = GROUPS[g][1]
    j = lax.broadcasted_iota(jnp.int32, (2 * TQ, TQ), 0)
    a = lax.broadcasted_iota(jnp.int32, (2 * TQ, TQ), 1)
    rel = jnp.abs(j - RADIUS - a)
    base = -(rel * dil).astype(F32)
    inside, after_start, before_end = rel <= RADIUS, j >= RADIUS, j < TQ + RADIUS
    variants = []
    for first, last in ((False, False), (True, False), (False, True), (True, True)):
        valid = inside & (after_start if first else True) & (before_end if last else True)
        variants.append(jnp.where(valid, base, MASKED_DISTANCE))
    return jnp.stack(variants)


SUBS = 4
TB = SUBS * TQ


def _ext_window(p_ref, c_ref, n_ref):
    return jnp.concatenate([p_ref[TB - RADIUS:, :], c_ref[...], n_ref[:RADIUS, :]], axis=0)


def _head_stats(rows):
    pad = jnp.zeros((LANES - len(rows), TQ), F32)
    return jnp.concatenate(list(rows) + [pad], axis=0).T


def _slope(g, h):
    return 2.0 ** (-8.0 * (g * HEADS_PER_GROUP + h + 1) / (N_GROUPS * HEADS_PER_GROUP))


def _pair(a, h):
    return a[:, (h // 2) * LANES:(h // 2 + 1) * LANES]


def _own_lanes(a, h):
    lane = lax.broadcasted_iota(jnp.int32, a.shape, 1)
    return jnp.where((lane >= HEAD_DIM) == (h % 2 == 1), a, jnp.zeros_like(a))


def _own_rows(a, h):
    return a[(h % 2) * HEAD_DIM:(h % 2 + 1) * HEAD_DIM, :]


def _attn_fwd(qkv, col0, g):
    dil, sub, _ = qkv.shape
    nb = sub // TB
    heads = HEADS_PER_GROUP

    def body(q_ref, kp, kc, kn, vp, vc, vn, bias_ref, o_ref, lse_ref, ot_scr, s_scr, p_scr):
        i = pl.program_id(1)
        kext = _ext_window(kp, kc, kn)
        vext = _ext_window(vp, vc, vn)
        q = q_ref[...] * ATT_SCALE
        for b in range(SUBS):
            kwin, qb = kext[b * TQ:(b + 2) * TQ, :], q[b * TQ:(b + 1) * TQ, :]
            for h in range(heads):
                s_scr[b * heads + h] = _dot_nt(_pair(kwin, h), _own_lanes(_pair(qb, h), h))
        inv_den = []
        for b in range(SUBS):
            block = i * SUBS + b
            bias = bias_ref[jnp.where(block == 0, 1, 0) + jnp.where(block == nb * SUBS - 1, 2, 0)]
            lse = []
            for h in range(heads):
                s = s_scr[b * heads + h] + _slope(g, h) * bias
                m = jnp.max(s, axis=0, keepdims=True)
                p = jnp.exp(s - m)
                den = jnp.sum(p, axis=0, keepdims=True)
                p_scr[b * heads + h] = p.astype(BF16)
                inv_den.append(1.0 / den)
                lse.append(m + jnp.log(den))
            lse_ref[b * TQ:(b + 1) * TQ, :] = _head_stats(lse)
        for b in range(SUBS):
            vwin = vext[b * TQ:(b + 2) * TQ, :]
            for h in range(heads):
                ot = _dot_tn(_pair(vwin, h), p_scr[b * heads + h])
                ot_scr[h * HEAD_DIM:(h + 1) * HEAD_DIM, b * TQ:(b + 1) * TQ] = _own_rows(ot, h) * inv_den[b * heads + h]
        o_ref[...] = ot_scr[...].T

    def spec(col, shift):
        return pl.BlockSpec((None, TB, GROUP_W), lambda r, i: (r, jnp.clip(i + shift, 0, nb - 1), col))

    return pl.pallas_call(
        body, grid=(dil, nb),
        in_specs=[spec(col0, 0), spec(col0 + 1, -1), spec(col0 + 1, 0), spec(col0 + 1, 1),
                  spec(col0 + 2, -1), spec(col0 + 2, 0), spec(col0 + 2, 1),
                  pl.BlockSpec((4, 2 * TQ, TQ), lambda r, i: (0, 0, 0))],
        out_specs=[pl.BlockSpec((None, TB, GROUP_W), lambda r, i: (r, i, 0)),
                   pl.BlockSpec((None, TB, LANES), lambda r, i: (r, i, 0))],
        out_shape=[jax.ShapeDtypeStruct((dil, sub, GROUP_W), F32), jax.ShapeDtypeStruct((dil, sub, LANES), F32)],
        scratch_shapes=[pltpu.VMEM((GROUP_W, TB), F32), pltpu.VMEM((SUBS * heads, 2 * TQ, TQ), F32),
                        pltpu.VMEM((SUBS * heads, 2 * TQ, TQ), BF16)],
        name=f"attn_fwd_g{g}", compiler_params=_cparams(("parallel", "arbitrary"), 32))(
            *_hbm(*([qkv] * 7), _attn_bias_table(g)))


def _expand_heads():
    h = lax.broadcasted_iota(jnp.int32, (LANES, GROUP_W), 0)
    c = lax.broadcasted_iota(jnp.int32, (LANES, GROUP_W), 1)
    return (c // HEAD_DIM == h).astype(F32)


def _dot_f32(a, b):
    return jnp.dot(a, b, preferred_element_type=F32, precision=lax.Precision.HIGH)


def _attn_combine(outs, lses, *, tm=512):
    S = outs[0].shape[1]
    n_col = GROUP_W // LANES

    def body(*refs):
        ins, e_ref = refs[:2 * N_GROUPS], refs[2 * N_GROUPS]
        c_ref, cb_ref, lt_ref = refs[2 * N_GROUPS + 1:2 * N_GROUPS + 4]
        scr = refs[2 * N_GROUPS + 4:]
        o, l = [ins[0][0]], [ins[N_GROUPS][0]]
        for k, d in enumerate(DILS):
            o_ref, l_ref = ins[1 + k], ins[N_GROUPS + 1 + k]
            o.append(_from_residue(lambda r: o_ref[r], d, tm, scr[k * (n_col + 1):k * (n_col + 1) + n_col]))
            l.append(_from_residue(lambda r: l_ref[r], d, tm, scr[k * (n_col + 1) + n_col:(k + 1) * (n_col + 1)]))
        m = jnp.maximum(jnp.maximum(l[0], l[1]), l[2])
        e = [jnp.exp(v - m) for v in l]
        den = e[0] + e[1] + e[2]
        comb = sum(_dot_f32(ev / den, e_ref[...]) * ov for ev, ov in zip(e, o))
        c_ref[...] = comb
        cb_ref[...] = comb.astype(BF16)
        lt_ref[...] = m + jnp.log(den)

    row = pl.BlockSpec((tm, GROUP_W), lambda i: (i, 0))
    dils = [d for _, d in GROUPS]
    return pl.pallas_call(
        body, grid=(S // tm,),
        in_specs=[_res_spec(d, tm, GROUP_W) for d in dils] + [_res_spec(d, tm, LANES) for d in dils]
        + [_resident((LANES, GROUP_W))],
        out_specs=[row, row, pl.BlockSpec((tm, LANES), lambda i: (i, 0))],
        out_shape=[jax.ShapeDtypeStruct((S, GROUP_W), F32), jax.ShapeDtypeStruct((S, GROUP_W), BF16),
                   jax.ShapeDtypeStruct((S, LANES), F32)],
        scratch_shapes=_lane_scratch(tm, GROUP_W + LANES) * len(DILS),
        name="attn_combine", compiler_params=_cparams(("parallel",), 32))(*_hbm(*outs, *lses, _expand_heads()))


def _branch_mix(ya_in, comb_b, w_a, w_b, proj, *, tm=512):
    S = ya_in.shape[0]

    def body(ya_ref, cb_ref, wa_ref, wb_ref, ga_ref, gb_ref, yab_ref, mx_ref):
        y_a = _dot(ya_ref[...], wa_ref[...])
        y_b = _dot(cb_ref[...], wb_ref[...])
        yab_ref[:, 0:D_MODEL] = y_a.astype(BF16)
        yab_ref[:, D_MODEL:2 * D_MODEL] = y_b.astype(BF16)
        mx = jax.nn.sigmoid(ga_ref[...].astype(F32)) * y_a + jax.nn.sigmoid(gb_ref[...].astype(F32)) * y_b
        mx_ref[...] = mx.astype(BF16)

    return pl.pallas_call(
        body, grid=(S // tm,),
        in_specs=[pl.BlockSpec((tm, D_CONV), lambda i: (i, 0)), pl.BlockSpec((tm, GROUP_W), lambda i: (i, 0)),
                  pl.BlockSpec((D_CONV, D_MODEL), lambda i: (0, 0)), pl.BlockSpec((GROUP_W, D_MODEL), lambda i: (0, 0)),
                  pl.BlockSpec((tm, D_MODEL), lambda i: (i, P_GA // D_MODEL)),
                  pl.BlockSpec((tm, D_MODEL), lambda i: (i, P_GB // D_MODEL))],
        out_specs=[pl.BlockSpec((tm, 2 * D_MODEL), lambda i: (i, 0)), pl.BlockSpec((tm, D_MODEL), lambda i: (i, 0))],
        out_shape=[jax.ShapeDtypeStruct((S, 2 * D_MODEL), BF16), jax.ShapeDtypeStruct((S, D_MODEL), BF16)],
        name="branch_mix", compiler_params=_cparams(("parallel",), 40))(*_hbm(ya_in, comb_b, w_a, w_b, proj, proj))


def _mix_ln1(mixin, w_o, b_o, h0, g1, b1, *, tm=512):
    S = mixin.shape[0]

    def body(mx_ref, wo_ref, bo_ref, h0_ref, g_ref, b_ref, xh_ref, rs_ref, h1b_ref):
        z = ALPHA * h0_ref[...] + _dot(mx_ref[...], wo_ref[...]) + bo_ref[...]
        xhat, rstd = _ln_stats(z)
        xh_ref[...] = xhat
        rs_ref[...] = jnp.broadcast_to(rstd, (tm, LANES))
        h1b_ref[...] = (xhat * g_ref[...] + b_ref[...]).astype(BF16)

    row = pl.BlockSpec((tm, D_MODEL), lambda i: (i, 0))
    vec = pl.BlockSpec((1, D_MODEL), lambda i: (0, 0))
    return pl.pallas_call(
        body, grid=(S // tm,),
        in_specs=[row, pl.BlockSpec((D_MODEL, D_MODEL), lambda i: (0, 0)), vec, row, vec, vec],
        out_specs=[row, pl.BlockSpec((tm, LANES), lambda i: (i, 0)), row],
        out_shape=[jax.ShapeDtypeStruct((S, D_MODEL), F32), jax.ShapeDtypeStruct((S, LANES), F32),
                   jax.ShapeDtypeStruct((S, D_MODEL), BF16)],
        name="mix_ln1", compiler_params=_cparams(("parallel",), 40))(*_hbm(mixin, w_o, b_o, h0, g1, b1))


def _gelu_parts(cz):
    cdf = 0.5 * (1.0 + lax.erf(cz * INV_SQRT2))
    return cdf, cz * cdf


def _ffn_conv_fwd(up, cw, cb):
    S = up.shape[0]

    def body(a_ref, g_ref, w_ref, cb_ref, o_ref, a_scr):
        _zero_pads(a_scr, S)
        for t in range(0, S, CHUNK):
            a_scr[PAD + t:PAD + t + CHUNK, :] = a_ref[t:t + CHUNK, :].astype(F32)
        w0, w1, w2 = w_ref[0:1, :], w_ref[1:2, :], w_ref[2:3, :]
        for t in range(0, S, CHUNK):
            am, a0, ap = _shifted(a_scr, t)
            _, gel = _gelu_parts(w0 * am + w1 * a0 + w2 * ap + cb_ref[...])
            o_ref[t:t + CHUNK, :] = (gel * g_ref[t:t + CHUNK, :].astype(F32)).astype(BF16)

    return pl.pallas_call(
        body, grid=(D_FF // SLAB,),
        in_specs=[_slab_spec(S, 0), _slab_spec(S, D_FF), pl.BlockSpec((3, SLAB), lambda j: (0, j)),
                  pl.BlockSpec((1, SLAB), lambda j: (0, j))],
        out_specs=pl.BlockSpec((S, SLAB), lambda j: (0, j)),
        out_shape=jax.ShapeDtypeStruct((S, D_FF), BF16),
        scratch_shapes=[pltpu.VMEM((S + 2 * PAD, SLAB), F32)],
        name="ffn_conv_fwd", compiler_params=_cparams(("parallel",), 40))(*_hbm(up, up, cw, cb))


def _down_ln2_loss(f, w_down, b_down, xhat1, g1, b1, g2, b2, target, *, tm=512):
    S = f.shape[0]

    def body(f_ref, wd_ref, bd_ref, xh1_ref, g1_ref, b1_ref, g2_ref, b2_ref, t_ref, dz_ref, dzb_ref, st_ref):
        h1 = xh1_ref[...] * g1_ref[...] + b1_ref[...]
        z = ALPHA * h1 + _dot(f_ref[...], wd_ref[...]) + bd_ref[...]
        xhat, rstd = _ln_stats(z)
        err = xhat * g2_ref[...] + b2_ref[...] - t_ref[...]
        loss = (0.5 / D_MODEL) * jnp.sum(jnp.sum(err * err, axis=1, keepdims=True), axis=0, keepdims=True)
        dh2 = err * (1.0 / D_MODEL)
        dz = _ln_bwd(dh2, xhat, rstd, g2_ref[...])
        dz_ref[...] = dz
        dzb_ref[...] = dz.astype(BF16)
        upd = _rows8([jnp.sum(dh2 * xhat, axis=0, keepdims=True), jnp.sum(dh2, axis=0, keepdims=True),
                      jnp.broadcast_to(loss, (1, D_MODEL)), jnp.sum(dz, axis=0, keepdims=True)], D_MODEL)

        @pl.when(pl.program_id(0) == 0)
        def _():
            st_ref[...] = upd

        @pl.when(pl.program_id(0) != 0)
        def _():
            st_ref[...] += upd

    row = pl.BlockSpec((tm, D_MODEL), lambda i: (i, 0))
    vec = pl.BlockSpec((1, D_MODEL), lambda i: (0, 0))
    return pl.pallas_call(
        body, grid=(S // tm,),
        in_specs=[pl.BlockSpec((tm, D_FF), lambda i: (i, 0)), _resident((D_FF, D_MODEL)),
                  vec, row, vec, vec, vec, vec, row],
        out_specs=[row, row, pl.BlockSpec((SUBLANES, D_MODEL), lambda i: (0, 0))],
        out_shape=[jax.ShapeDtypeStruct((S, D_MODEL), F32), jax.ShapeDtypeStruct((S, D_MODEL), BF16),
                   jax.ShapeDtypeStruct((SUBLANES, D_MODEL), F32)],
        name="down_ln2_loss", compiler_params=_cparams(("arbitrary",), 56))(
            *_hbm(f, w_down, b_down, xhat1, g1, b1, g2, b2, target))


def _ffn_conv_bwd(up, df, cw, cb):
    S = up.shape[0]

    def body(a_ref, g_ref, df_ref, w_ref, cb_ref, dup_ref, sm_ref, a_scr, d_scr):
        _zero_pads(a_scr, S)
        _zero_pads(d_scr, S)
        for t in range(0, S, CHUNK):
            a_scr[PAD + t:PAD + t + CHUNK, :] = a_ref[t:t + CHUNK, :].astype(F32)
        w0, w1, w2 = w_ref[0:1, :], w_ref[1:2, :], w_ref[2:3, :]
        zero = jnp.zeros((1, SLAB), F32)
        s_dg, s_dcz, s_w0, s_w1, s_w2 = zero, zero, zero, zero, zero
        for t in range(0, S, CHUNK):
            am, a0, ap = _shifted(a_scr, t)
            cz = w0 * am + w1 * a0 + w2 * ap + cb_ref[...]
            cdf, gel = _gelu_parts(cz)
            dfv = df_ref[t:t + CHUNK, :].astype(F32)
            dgte = dfv * gel
            dcz = dfv * g_ref[t:t + CHUNK, :].astype(F32) * (cdf + (cz * INV_SQRT_2PI) * jnp.exp2(cz * cz * (-0.5 * LOG2_E)))
            dup_ref[1, t:t + CHUNK, :] = dgte.astype(BF16)
            d_scr[PAD + t:PAD + t + CHUNK, :] = dcz
            s_dg = s_dg + jnp.sum(dgte, axis=0, keepdims=True)
            s_dcz = s_dcz + jnp.sum(dcz, axis=0, keepdims=True)
            s_w0 = s_w0 + jnp.sum(dcz * am, axis=0, keepdims=True)
            s_w1 = s_w1 + jnp.sum(dcz * a0, axis=0, keepdims=True)
            s_w2 = s_w2 + jnp.sum(dcz * ap, axis=0, keepdims=True)
        s_da = zero
        for t in range(0, S, CHUNK):
            dm, d0, dp = _shifted(d_scr, t)
            da = w0 * dp + w1 * d0 + w2 * dm
            dup_ref[0, t:t + CHUNK, :] = da.astype(BF16)
            s_da = s_da + jnp.sum(da, axis=0, keepdims=True)
        sm_ref[...] = _rows8([s_da, s_dg, s_dcz, s_w0, s_w1, s_w2], SLAB)

    return pl.pallas_call(
        body, grid=(D_FF // SLAB,),
        in_specs=[_slab_spec(S, 0), _slab_spec(S, D_FF), pl.BlockSpec((S, SLAB), lambda j: (0, j)),
                  pl.BlockSpec((3, SLAB), lambda j: (0, j)), pl.BlockSpec((1, SLAB), lambda j: (0, j))],
        out_specs=[pl.BlockSpec((2, S, SLAB), lambda j: (0, 0, j)), pl.BlockSpec((SUBLANES, SLAB), lambda j: (0, j))],
        out_shape=[jax.ShapeDtypeStruct((2, S, D_FF), BF16), jax.ShapeDtypeStruct((SUBLANES, D_FF), F32)],
        scratch_shapes=[pltpu.VMEM((S + 2 * PAD, SLAB), F32)] * 2,
        name="ffn_conv_bwd", compiler_params=_cparams(("parallel",), 48))(*_hbm(up, up, df, cw, cb))


def _up_bwd_ln1(dup, w_up3, dz2, xhat1, rstd1, g1, *, tm=512):
    S = dz2.shape[0]
    ns, _, tk = w_up3.shape
    per_plane = D_FF // tk

    def body(du_ref, w_ref, dz2_ref, xh_ref, rs_ref, g_ref, dz_ref, dzb_ref, st_ref):
        dh = ALPHA * dz2_ref[...]
        for plane in range(ns // per_plane):
            w = jnp.concatenate([w_ref[plane * per_plane + k] for k in range(per_plane)], axis=1)
            dh = dh + _dot_nt(du_ref[plane], w)
        xhat = xh_ref[...]
        dz = _ln_bwd(dh, xhat, rs_ref[:, 0:1], g_ref[...])
        dz_ref[...] = dz
        dzb_ref[...] = dz.astype(BF16)
        upd = _rows8([jnp.sum(dh * xhat, axis=0, keepdims=True), jnp.sum(dh, axis=0, keepdims=True),
                      jnp.sum(dz, axis=0, keepdims=True)], D_MODEL)

        @pl.when(pl.program_id(0) == 0)
        def _():
            st_ref[...] = upd

        @pl.when(pl.program_id(0) != 0)
        def _():
            st_ref[...] += upd

    row = pl.BlockSpec((tm, D_MODEL), lambda i: (i, 0))
    return pl.pallas_call(
        body, grid=(S // tm,),
        in_specs=[pl.BlockSpec((dup.shape[0], tm, D_FF), lambda i: (0, i, 0)), _resident(w_up3.shape),
                  row, row, pl.BlockSpec((tm, LANES), lambda i: (i, 0)), pl.BlockSpec((1, D_MODEL), lambda i: (0, 0))],
        out_specs=[row, row, pl.BlockSpec((SUBLANES, D_MODEL), lambda i: (0, 0))],
        out_shape=[jax.ShapeDtypeStruct((S, D_MODEL), F32), jax.ShapeDtypeStruct((S, D_MODEL), BF16),
                   jax.ShapeDtypeStruct((SUBLANES, D_MODEL), F32)],
        name="up_bwd_ln1", compiler_params=_cparams(("arbitrary",), 56))(*_hbm(dup, w_up3, dz2, xhat1, rstd1, g1))


def _mix_bwd(dz1b, w_o, proj, yab, *, tm=512):
    S = dz1b.shape[0]

    def body(dz_ref, wo_ref, ga_ref, gb_ref, y_ref, dy_ref, dg_ref):
        dmx = _dot_nt(dz_ref[...], wo_ref[...])
        for k, gt_ref in enumerate((ga_ref, gb_ref)):
            sl = slice(k * D_MODEL, (k + 1) * D_MODEL)
            sg = jax.nn.sigmoid(gt_ref[...].astype(F32))
            dy_ref[:, sl] = (dmx * sg).astype(BF16)
            dg_ref[k] = (dmx * y_ref[:, sl].astype(F32) * sg * (1.0 - sg)).astype(BF16)

    row = pl.BlockSpec((tm, D_MODEL), lambda i: (i, 0))
    wide = pl.BlockSpec((tm, 2 * D_MODEL), lambda i: (i, 0))
    return pl.pallas_call(
        body, grid=(S // tm,),
        in_specs=[row, _resident(w_o.shape), pl.BlockSpec((tm, D_MODEL), lambda i: (i, P_GA // D_MODEL)),
                  pl.BlockSpec((tm, D_MODEL), lambda i: (i, P_GB // D_MODEL)), wide],
        out_specs=[wide, pl.BlockSpec((2, tm, D_MODEL), lambda i: (0, i, 0))],
        out_shape=[jax.ShapeDtypeStruct((S, 2 * D_MODEL), BF16), jax.ShapeDtypeStruct((2, S, D_MODEL), BF16)],
        name="mix_bwd", compiler_params=_cparams(("parallel",), 40))(*_hbm(dz1b, w_o, proj, proj, yab))


def _conv_gate_bwd(proj, dya_in, conv_w):
    S = proj.shape[0]

    def body(b_ref, c_ref, h_ref, dy_ref, w_ref, o_ref, sm_ref, u_scr, d_scr):
        _zero_pads(u_scr, S)
        _zero_pads(d_scr, S)
        for t in range(0, S, CHUNK):
            u_scr[PAD + t:PAD + t + CHUNK, :] = c_ref[t:t + CHUNK, :].astype(F32) * h_ref[t:t + CHUNK, :].astype(F32)
        w0, w1, w2 = w_ref[0:1, :], w_ref[1:2, :], w_ref[2:3, :]
        zero = jnp.zeros((1, SLAB), F32)
        s_w0, s_w1, s_w2 = zero, zero, zero
        for t in range(0, S, CHUNK):
            um, u0, up = _shifted(u_scr, t)
            dy = dy_ref[t:t + CHUNK, :].astype(F32)
            o_ref[0, t:t + CHUNK, :] = (dy * (w0 * um + w1 * u0 + w2 * up)).astype(BF16)
            dcv = dy * b_ref[t:t + CHUNK, :].astype(F32)
            d_scr[PAD + t:PAD + t + CHUNK, :] = dcv
            s_w0 = s_w0 + jnp.sum(dcv * um, axis=0, keepdims=True)
            s_w1 = s_w1 + jnp.sum(dcv * u0, axis=0, keepdims=True)
            s_w2 = s_w2 + jnp.sum(dcv * up, axis=0, keepdims=True)
        for t in range(0, S, CHUNK):
            dm, d0, dp = _shifted(d_scr, t)
            du = w0 * dp + w1 * d0 + w2 * dm
            o_ref[1, t:t + CHUNK, :] = (du * h_ref[t:t + CHUNK, :].astype(F32)).astype(BF16)
            o_ref[2, t:t + CHUNK, :] = (du * c_ref[t:t + CHUNK, :].astype(F32)).astype(BF16)
        sm_ref[...] = _rows8([s_w0, s_w1, s_w2], SLAB)

    return pl.pallas_call(
        body, grid=(D_CONV // SLAB,),
        in_specs=[_slab_spec(S, P_B), _slab_spec(S, P_C), _slab_spec(S, P_H),
                  pl.BlockSpec((S, SLAB), lambda j: (0, j)), pl.BlockSpec((3, SLAB), lambda j: (0, j))],
        out_specs=[pl.BlockSpec((3, S, SLAB), lambda j: (0, 0, j)), pl.BlockSpec((SUBLANES, SLAB), lambda j: (0, j))],
        out_shape=[jax.ShapeDtypeStruct((3, S, D_CONV), BF16), jax.ShapeDtypeStruct((SUBLANES, D_CONV), F32)],
        scratch_shapes=[pltpu.VMEM((S + 2 * PAD, SLAB), F32)] * 2,
        name="conv_gate_bwd", compiler_params=_cparams(("parallel",), 48))(*_hbm(proj, proj, proj, dya_in, conv_w))


def _comb_bwd(dyab, w_b, comb, lse_tot, *, tm=512):
    S = comb.shape[0]
    widths, dtypes = (GROUP_W, LANES, LANES), (BF16, F32, F32)

    def body(dy_ref, wb_ref, c_ref, lt_ref, e_ref, *rest):
        outs, scr = rest[:3 * N_GROUPS], rest[3 * N_GROUPS:]
        dcb = _dot_nt(dy_ref[...], wb_ref[...]).astype(BF16)
        dc = dcb.astype(F32)
        delta = lax.dot_general(dc * c_ref[...], e_ref[...], (((1,), (1,)), ((), ())),
                                preferred_element_type=F32, precision=lax.Precision.HIGH)
        for k, (val, dtype) in enumerate(zip((dc, lt_ref[...], delta), dtypes)):
            outs[k][0] = val.astype(dtype)
            _to_residue(val, [outs[3 * (1 + j) + k] for j in range(len(DILS))], DILS, tm, dtype,
                        scr[:val.shape[1] // LANES])

    out_specs, out_shape = [], []
    for _, d in GROUPS:
        out_specs += [_res_spec(d, tm, w) for w in widths]
        out_shape += [jax.ShapeDtypeStruct((d, S // d, w), t) for w, t in zip(widths, dtypes)]
    res = pl.pallas_call(
        body, grid=(S // tm,),
        in_specs=[pl.BlockSpec((tm, D_MODEL), lambda i: (i, 1)), _resident(w_b.shape),
                  pl.BlockSpec((tm, GROUP_W), lambda i: (i, 0)), pl.BlockSpec((tm, LANES), lambda i: (i, 0)),
                  _resident((LANES, GROUP_W))],
        out_specs=out_specs, out_shape=out_shape, scratch_shapes=_lane_scratch(tm, GROUP_W),
        name="comb_bwd", compiler_params=_cparams(("parallel",), 32))(*_hbm(dyab, w_b, comb, lse_tot, _expand_heads()))
    return [tuple(res[3 * g:3 * g + 3]) for g in range(N_GROUPS)]


def _attn_bwd(qkv, col0, g, dcomb, lse_tot, delta):
    dil, sub, _ = qkv.shape
    nb = sub // TB
    heads = HEADS_PER_GROUP

    def body(q_ref, kp, kc, kn, vp, vc, vn, do_ref, lse_ref, dl_ref, bias_ref, dq_ref, dk_ref, dv_ref,
             ak, av, dqt_scr, s_scr, dp_scr, ds_scr, p_scr):
        i = pl.program_id(1)

        @pl.when(i == 0)
        def _():
            ak[...] = jnp.zeros_like(ak)
            av[...] = jnp.zeros_like(av)

        @pl.when(i < nb)
        def _():
            kext = _ext_window(kp, kc, kn)
            vext = _ext_window(vp, vc, vn)
            q = q_ref[...] * ATT_SCALE
            do = do_ref[...]
            lse_t, dl_t = lse_ref[...].T, dl_ref[...].T
            for b in range(SUBS):
                rows = slice(b * TQ, (b + 1) * TQ)
                kwin, vwin = kext[b * TQ:(b + 2) * TQ, :], vext[b * TQ:(b + 2) * TQ, :]
                for h in range(heads):
                    s_scr[b * heads + h] = _dot_nt(_pair(kwin, h), _own_lanes(_pair(q[rows], h), h))
                    dp_scr[b * heads + h] = _dot_nt(_pair(vwin, h), _own_lanes(_pair(do[rows], h), h))
            for b in range(SUBS):
                cols = slice(b * TQ, (b + 1) * TQ)
                block = i * SUBS + b
                bias = bias_ref[jnp.where(block == 0, 1, 0) + jnp.where(block == nb * SUBS - 1, 2, 0)]
                for h in range(heads):
                    k = b * heads + h
                    p = jnp.exp(s_scr[k] + _slope(g, h) * bias - lse_t[h:h + 1, cols])
                    ds_scr[k] = (p * (dp_scr[k] - dl_t[h:h + 1, cols])).astype(BF16)
                    p_scr[k] = p.astype(BF16)
            for b in range(SUBS):
                kwin = kext[b * TQ:(b + 2) * TQ, :]
                for h in range(heads):
                    dqt_scr[h * HEAD_DIM:(h + 1) * HEAD_DIM, b * TQ:(b + 1) * TQ] = _own_rows(
                        _dot_tn(_pair(kwin, h), ds_scr[b * heads + h]), h)
            for b in range(SUBS):
                rows = slice(b * TQ, (b + 1) * TQ)
                acc_rows = slice(TB - RADIUS + b * TQ, TB - RADIUS + (b + 2) * TQ)
                for h in range(0, heads, 2):
                    cols = slice(h * HEAD_DIM, (h + 2) * HEAD_DIM)
                    k = b * heads + h
                    q2 = jnp.concatenate([_own_lanes(_pair(q[rows], h), h), _own_lanes(_pair(q[rows], h), h + 1)], axis=0)
                    do2 = jnp.concatenate([_own_lanes(_pair(do[rows], h), h), _own_lanes(_pair(do[rows], h), h + 1)],
                                          axis=0)
                    ak[acc_rows, cols] += _dot(jnp.concatenate([ds_scr[k], ds_scr[k + 1]], axis=1), q2)
                    av[acc_rows, cols] += _dot(jnp.concatenate([p_scr[k], p_scr[k + 1]], axis=1), do2)
            dq_ref[...] = (dqt_scr[...].T * ATT_SCALE).astype(BF16)

        if nb == 1:
            dk_ref[...] = ak[TB:2 * TB, :].astype(BF16)
            dv_ref[...] = av[TB:2 * TB, :].astype(BF16)
        else:
            dk_ref[...] = ak[0:TB, :].astype(BF16)
            dv_ref[...] = av[0:TB, :].astype(BF16)
            used = 2 * TB + RADIUS
            for acc in (ak, av):
                acc[0:used - TB, :] = acc[TB:used, :]
                acc[used - TB:used, :] = jnp.zeros((TB, GROUP_W), F32)

    def spec(col, shift):
        return pl.BlockSpec((None, TB, GROUP_W), lambda r, i: (r, jnp.clip(i + shift, 0, nb - 1), col))

    tok = pl.BlockSpec((None, TB, GROUP_W), lambda r, i: (r, jnp.minimum(i, nb - 1), 0))
    stat = pl.BlockSpec((None, TB, LANES), lambda r, i: (r, jnp.minimum(i, nb - 1), 0))
    dkv_spec = tok if nb == 1 else pl.BlockSpec((None, TB, GROUP_W), lambda r, i: (r, jnp.maximum(i - 1, 0), 0))
    return pl.pallas_call(
        body, grid=(dil, nb + (nb > 1)),
        in_specs=[spec(col0, 0), spec(col0 + 1, -1), spec(col0 + 1, 0), spec(col0 + 1, 1),
                  spec(col0 + 2, -1), spec(col0 + 2, 0), spec(col0 + 2, 1), tok, stat, stat,
                  pl.BlockSpec((4, 2 * TQ, TQ), lambda r, i: (0, 0, 0))],
        out_specs=[tok, dkv_spec, dkv_spec], out_shape=[jax.ShapeDtypeStruct((dil, sub, GROUP_W), BF16)] * 3,
        scratch_shapes=[pltpu.VMEM((3 * TB, GROUP_W), F32)] * 2 + [pltpu.VMEM((GROUP_W, TB), F32)]
        + [pltpu.VMEM((SUBS * heads, 2 * TQ, TQ), F32)] * 2 + [pltpu.VMEM((SUBS * heads, 2 * TQ, TQ), BF16)] * 2,
        name=f"attn_bwd_g{g}", compiler_params=_cparams(("arbitrary", "arbitrary"), 40))(
            *_hbm(*([qkv] * 7), dcomb, lse_tot, delta, _attn_bias_table(g)))


def _in_bwd_ln0(dgated, dqkv, w_nat, w_dil, dz1, x, g0, *, tm=256):
    S = x.shape[0]
    n_gated, n_in = len(dgated), 3 * N_GROUPS

    def body(*refs):
        g_refs, d_refs = refs[:n_gated], refs[n_gated:n_gated + n_in]
        wn_ref, *wd_refs = refs[n_gated + n_in:n_gated + n_in + N_GROUPS]
        dz_ref, x_ref, g_ref, gx_ref, st_ref, *tmp_ref = refs[n_gated + n_in + N_GROUPS:]
        dh = ALPHA * dz_ref[...]
        col = 0
        for ref in g_refs:
            for k in range(ref.shape[0]):
                dh = dh + _dot_nt(ref[k], wn_ref[:, col:col + D_MODEL])
                col += D_MODEL
        for g, (_, d) in enumerate(GROUPS):
            rows = [jnp.concatenate([d_refs[3 * g + k][r] for k in range(3)], axis=1) for r in range(d)]
            w = wn_ref[:, col:col + QKV_W] if d == 1 else wd_refs[g - 1][...]
            res = _dot_nt(jnp.concatenate(rows, axis=0), w)
            if d == 1:
                dh = dh + res
            else:
                n = tm // d
                dh = dh + _from_residue(lambda r: res[r * n:(r + 1) * n, :], d, tm, tmp_ref)
        xhat, rstd = _ln_stats(x_ref[...])
        gx_ref[...] = _ln_bwd(dh, xhat, rstd, g_ref[...])
        upd = _rows8([jnp.sum(dh * xhat, axis=0, keepdims=True), jnp.sum(dh, axis=0, keepdims=True)], D_MODEL)

        @pl.when(pl.program_id(0) == 0)
        def _():
            st_ref[...] = upd

        @pl.when(pl.program_id(0) != 0)
        def _():
            st_ref[...] += upd

    row = pl.BlockSpec((tm, D_MODEL), lambda i: (i, 0))
    g_specs = [pl.BlockSpec((a.shape[0], tm, D_MODEL), lambda i: (0, i, 0)) for a in dgated]
    d_specs = []
    for _, d in GROUPS:
        d_specs += [_res_spec(d, tm, GROUP_W)] * 3
    operands = list(dgated) + [a for grp in dqkv for a in grp] + [w_nat] + list(w_dil) + [dz1, x, g0]
    return pl.pallas_call(
        body, grid=(S // tm,),
        in_specs=g_specs + d_specs + [_resident(w_nat.shape)] + [_resident(w.shape) for w in w_dil]
        + [row, row, pl.BlockSpec((1, D_MODEL), lambda i: (0, 0))],
        out_specs=[row, pl.BlockSpec((SUBLANES, D_MODEL), lambda i: (0, 0))],
        out_shape=[jax.ShapeDtypeStruct((S, D_MODEL), F32), jax.ShapeDtypeStruct((SUBLANES, D_MODEL), F32)],
        scratch_shapes=_lane_scratch(tm, D_MODEL),
        name="in_bwd_ln0", compiler_params=_cparams(("arbitrary",), 52))(*_hbm(*operands))


HBM_SPEC = pl.BlockSpec(memory_space=pltpu.HBM)


def _place():
    x, y, c = lax.axis_index("x"), lax.axis_index("y"), lax.axis_index("c")
    chips = [(1 - x, y), (x, 1 - y), (1 - x, 1 - y)]
    return x, y, c, chips


def _allgather_shards(shards, after, *, name, collective_id):
    n = len(shards)
    per = 6

    def body(*refs):
        ins, outs = refs[:n], refs[n + len(after):2 * n + len(after)]
        send_sems, recv_sems, loc_sems = refs[2 * n + len(after):]
        x, y, c, chips = _place()
        me = 2 * x + y
        sib = (x, y, 1 - c)
        peers = [sib] + [(px, py, c) for px, py in chips]
        barrier = pltpu.get_barrier_semaphore()
        for peer in peers:
            pl.semaphore_signal(barrier, inc=1, device_id=peer, device_id_type=MESH)
        pl.semaphore_wait(barrier, len(peers))

        def rcopy(w, k, src, dst, to):
            return pltpu.make_async_remote_copy(src_ref=src, dst_ref=dst, send_sem=send_sems.at[per * w + k],
                                                recv_sem=recv_sems.at[per * w + k], device_id=to, device_id_type=MESH)

        split = [s.shape[0] == N_CORES for s in shards]
        half = lambda w: c if split[w] else 0
        local, sends = [], []
        for w in range(n):
            cp = pltpu.make_async_copy(ins[w], outs[w].at[me], loc_sems.at[w])
            cp.start()
            local.append(cp)
            for j, (px, py) in enumerate(chips):
                cp = rcopy(w, j, ins[w].at[half(w)], outs[w].at[me, half(w)], (px, py, c))
                cp.start()
                sends.append(cp)
        for w in range(n):
            for j, (px, py) in enumerate(chips):
                slot = outs[w].at[2 * px + py, half(w)]
                rcopy(w, j, slot, slot, (px, py, c)).wait_recv()
                if split[w]:
                    cp = rcopy(w, 3 + j, slot, slot, sib)
                    cp.start()
                    sends.append(cp)
        for w in range(n):
            if split[w]:
                for j, (px, py) in enumerate(chips):
                    slot = outs[w].at[2 * px + py, 1 - c]
                    rcopy(w, 3 + j, slot, slot, sib).wait_recv()
        for cp in sends:
            cp.wait_send()
        for cp in local:
            cp.wait()

    return pl.kernel(
        body, out_type=[jax.ShapeDtypeStruct((N_CHIPS,) + s.shape, s.dtype) for s in shards],
        mesh=plsc.ScalarSubcoreMesh(axis_name="sequencer", num_cores=1),
        scratch_types=[pltpu.SemaphoreType.DMA((per * n,)), pltpu.SemaphoreType.DMA((per * n,)),
                       pltpu.SemaphoreType.DMA((n,))],
        name=name, compiler_params=pltpu.CompilerParams(collective_id=collective_id))(*shards, *after)


def _exchange_grads(grads, *, name, collective_id):
    n = len(grads)
    per = 7

    def body(*refs):
        ins, outs = refs[:n], refs[n:2 * n]
        send_sems, recv_sems, loc_sems = refs[2 * n:]
        x, y, c, chips = _place()
        me = 2 * x + y
        sib = (x, y, 1 - c)
        peers = [sib] + [(px, py, c) for px, py in chips]
        barrier = pltpu.get_barrier_semaphore()
        for peer in peers:
            pl.semaphore_signal(barrier, inc=1, device_id=peer, device_id_type=MESH)
        pl.semaphore_wait(barrier, len(peers))

        def rcopy(w, k, src, dst, to):
            return pltpu.make_async_remote_copy(src_ref=src, dst_ref=dst, send_sem=send_sems.at[per * w + k],
                                                recv_sem=recv_sems.at[per * w + k], device_id=to, device_id_type=MESH)

        local, sends = [], []
        for w in range(n):
            cp = pltpu.make_async_copy(ins[w].at[me], outs[w].at[c, me], loc_sems.at[w])
            cp.start()
            local.append(cp)
            cp = rcopy(w, 0, ins[w].at[me], outs[w].at[c, me], sib)
            cp.start()
            sends.append(cp)
            for j, (px, py) in enumerate(chips):
                cp = rcopy(w, 1 + j, ins[w].at[2 * px + py], outs[w].at[c, me], (px, py, c))
                cp.start()
                sends.append(cp)
        for w in range(n):
            for j, (px, py) in enumerate(chips):
                slot = outs[w].at[c, 2 * px + py]
                rcopy(w, 1 + j, slot, slot, (px, py, c)).wait_recv()
                cp = rcopy(w, 4 + j, slot, slot, sib)
                cp.start()
                sends.append(cp)
        for w in range(n):
            slot = outs[w].at[1 - c, me]
            rcopy(w, 0, slot, slot, sib).wait_recv()
            for j, (px, py) in enumerate(chips):
                slot = outs[w].at[1 - c, 2 * px + py]
                rcopy(w, 4 + j, slot, slot, sib).wait_recv()
        for cp in sends:
            cp.wait_send()
        for cp in local:
            cp.wait()

    return pl.kernel(
        body, out_type=[jax.ShapeDtypeStruct((N_CORES,) + g.shape, g.dtype) for g in grads],
        mesh=plsc.ScalarSubcoreMesh(axis_name="sequencer", num_cores=1),
        scratch_types=[pltpu.SemaphoreType.DMA((per * n,)), pltpu.SemaphoreType.DMA((per * n,)),
                       pltpu.SemaphoreType.DMA((n,))],
        name=name, compiler_params=pltpu.CompilerParams(collective_id=collective_id))(*grads)


def _allgather_small(vec, after):
    def body(v_ref, _, o_ref, send_sems, recv_sems, loc_sem):
        x, y, c = lax.axis_index("x"), lax.axis_index("y"), lax.axis_index("c")
        me = 4 * x + 2 * y + c

        def peer(k):
            flip = lambda v, bit: 1 - v if (k >> bit) & 1 else v
            return flip(x, 2), flip(y, 1), flip(c, 0)

        loc = pltpu.make_async_copy(v_ref, o_ref.at[me], loc_sem)
        loc.start()
        sends = []
        for k in range(1, N_DEV):
            cp = pltpu.make_async_remote_copy(src_ref=v_ref, dst_ref=o_ref.at[me], send_sem=send_sems.at[k - 1],
                                              recv_sem=recv_sems.at[k - 1], device_id=peer(k), device_id_type=MESH)
            cp.start()
            sends.append(cp)
        for k in range(1, N_DEV):
            px, py, pc = peer(k)
            pltpu.make_async_remote_copy(src_ref=v_ref, dst_ref=o_ref.at[4 * px + 2 * py + pc],
                                         send_sem=send_sems.at[k - 1], recv_sem=recv_sems.at[k - 1],
                                         device_id=(px, py, pc), device_id_type=MESH).wait_recv()
        for cp in sends:
            cp.wait_send()
        loc.wait()

    return pl.pallas_call(
        body, in_specs=[HBM_SPEC, HBM_SPEC], out_specs=HBM_SPEC,
        out_shape=jax.ShapeDtypeStruct((N_DEV,) + vec.shape, vec.dtype),
        scratch_shapes=[pltpu.SemaphoreType.DMA((N_DEV - 1,)), pltpu.SemaphoreType.DMA((N_DEV - 1,)),
                        pltpu.SemaphoreType.DMA],
        name="allgather_small")(vec, after)


def _adamw(w, g, m, v):
    m = ADAM_B1 * m + (1.0 - ADAM_B1) * g
    v = ADAM_B2 * v + (1.0 - ADAM_B2) * (g * g)
    m_hat = m / (1.0 - ADAM_B1 ** ADAM_STEP)
    v_hat = v / (1.0 - ADAM_B2 ** ADAM_STEP)
    delta = -ADAM_LR * (m_hat / (jnp.sqrt(v_hat) + ADAM_EPS) + ADAM_WD * w)
    return delta, m, v


def _reduce_adamw(parts, w, m, v, *, tr, name):
    R, C = w.shape

    def body(p_ref, w_ref, m_ref, v_ref, g_ref, d_ref, nm_ref, nv_ref):
        def core_sum(cc):
            s = p_ref[cc, 0].astype(F32)
            for k in range(1, N_CHIPS):
                s = s + p_ref[cc, k].astype(F32)
            return s

        g = core_sum(0) + core_sum(1)
        delta, nm, nv = _adamw(w_ref[...], g, m_ref[...], v_ref[...])
        g_ref[...] = g
        d_ref[...] = delta
        nm_ref[...] = nm
        nv_ref[...] = nv

    blk = pl.BlockSpec((tr, C), lambda i: (i, 0))
    return pl.pallas_call(
        body, grid=(R // tr,),
        in_specs=[pl.BlockSpec((N_CORES, N_CHIPS, tr, C), lambda i: (0, 0, i, 0)), blk, blk, blk],
        out_specs=[blk] * 4, out_shape=[jax.ShapeDtypeStruct((R, C), F32)] * 4,
        name=name, compiler_params=_cparams(("parallel",), 40))(*_hbm(parts, w, m, v))


def _reduce_adamw_vectors(allv, offs, ws, ms, vs):
    n = len(ws)

    def body(a_ref, *refs):
        w_refs, m_refs, v_refs = refs[:n], refs[n:2 * n], refs[2 * n:3 * n]
        tot_ref, outs = refs[3 * n], refs[3 * n + 1:]
        s = a_ref[0]
        for d in range(1, N_DEV):
            s = s + a_ref[d]
        tot_ref[...] = s
        for k in range(n):
            g = s[:, offs[k]:offs[k] + w_refs[k].shape[1]]
            delta, nm, nv = _adamw(w_refs[k][...], g, m_refs[k][...], v_refs[k][...])
            for ref, val in zip(outs[4 * k:4 * k + 4], (g, delta, nm, nv)):
                ref[...] = val

    out_shape = [jax.ShapeDtypeStruct(allv.shape[1:], F32)]
    for w in ws:
        out_shape += [jax.ShapeDtypeStruct(w.shape, F32)] * 4
    res = pl.pallas_call(body, out_shape=out_shape, name="reduce_adamw_vectors",
                         compiler_params=_cparams((), 40))(allv, *ws, *ms, *vs)
    return res[0], [tuple(res[1 + 4 * k:5 + 4 * k]) for k in range(n)]


def _adamw_taps(ws, gs, ms, vs):
    n = len(ws)

    def body(*refs):
        outs = refs[4 * n:]
        for k in range(n):
            res = _adamw(refs[k][...], refs[n + k][...], refs[2 * n + k][...], refs[3 * n + k][...])
            for ref, val in zip(outs[3 * k:3 * k + 3], res):
                ref[...] = val

    out_shape = []
    for w in ws:
        out_shape += [jax.ShapeDtypeStruct(w.shape, F32)] * 3
    res = pl.pallas_call(body, out_shape=out_shape, name="adamw_taps")(*ws, *gs, *ms, *vs)
    return [tuple(res[3 * k:3 * k + 3]) for k in range(n)]


def _pack(pieces):
    flat, offs, n = [], [], 0
    for p in pieces:
        size = -(-p.size // LANES) * LANES
        flat.append(jnp.pad(p.reshape(-1), (0, size - p.size)))
        offs.append(n)
        n += size
    return jnp.concatenate(flat).reshape(1, n), offs


def _local_step(x, target, p, wfull, on_ready=lambda group: None, before_ln0=()):
    S = x.shape[0]
    dils = [d for _, d in GROUPS]

    h0, h0b, *h0_res = _ln0_fwd(x, p["ln0_g"], p["ln0_b"], before_ln0)
    h0_rows = [h0b] + [h.reshape(S, D_MODEL) for h in h0_res]

    if isinstance(wfull, dict):
        w_in3, pending = wfull["w_in"], None
    else:
        w_in3, launch_rest, assemble = wfull
        w_in3, h0b = lax.optimization_barrier((w_in3, h0b))
        pending = launch_rest(h0b)

    runs = _col_runs()
    def w_part(lo, hi):
        return jnp.concatenate([w_in3[s, :, c:c + w] for s, c, pc, w in runs if lo <= pc < hi], axis=1)

    b_blocks = p["b_in"].reshape(N_BLK, GROUP_W)
    b_perm = jnp.concatenate([b_blocks[b] for b in PERM]).reshape(1, N_IN)
    w_nat, b_nat = w_part(0, N_NAT), b_perm[:, :N_NAT]
    qkv_cols = [slice(P_Q0 + g * QKV_W, P_Q0 + (g + 1) * QKV_W) for g in range(N_GROUPS)]
    w_qkv = [None] + [w_part(c.start, c.stop) for c in qkv_cols[1:]]

    proj = _mm_nn(h0b, w_nat, b_nat, tm=512, tn=N_NAT // 2, out_dtype=BF16, name="proj")
    qkv = [proj[None]]
    for g in range(1, N_GROUPS):
        t = _mm_nn(h0_rows[g], w_qkv[g], b_perm[:, qkv_cols[g]], tm=512, tn=QKV_W, out_dtype=BF16, name=f"proj_qkv{g}")
        qkv.append(t.reshape(dils[g], S // dils[g], QKV_W))
    if pending is not None:
        pending, qkv = lax.optimization_barrier((pending, qkv))
        proj = qkv[0][0]
        wfull = assemble(pending)
    w_up3 = wfull["w_up"]
    w_a, w_o, w_down, w_b = wfull["w_a"], wfull["w_o"], wfull["w_down"], wfull["w_b"]
    conv_w, ffn_conv_w = wfull["conv_w"], wfull["ffn_conv_w"]
    col0 = [P_Q0 // GROUP_W] + [0] * (N_GROUPS - 1)
    ya_in = _conv_gate_fwd(proj, conv_w)
    att = [_attn_fwd(qkv[g], col0[g], g) for g in range(N_GROUPS)]
    comb, comb_b, lse_tot = _attn_combine([a[0] for a in att], [a[1] for a in att])
    yab, mixin = _branch_mix(ya_in, comb_b, w_a, w_b, proj)
    xhat1, rstd1, h1b = _mix_ln1(mixin, w_o, p["b_o"], h0, p["ln1_g"], p["ln1_b"])
    up = _mm_nn(h1b, w_up3, p["b_up"], tm=512, tn=2 * w_up3.shape[2], out_dtype=BF16, name="up")
    f = _ffn_conv_fwd(up, ffn_conv_w, p["ffn_conv_b"])
    dz2, dz2b, st2 = _down_ln2_loss(f, w_down, p["b_down"], xhat1, p["ln1_g"], p["ln1_b"],
                                    p["ln2_g"], p["ln2_b"], target)

    gw = {}
    gw["w_down"] = _mm_tn(f, dz2b, n_out=1, tn=D_MODEL, ts=1024, g_block=(1024, D_MODEL),
                          g_map=lambda j, s: (s, 0), name="grad_w_down").reshape(N_CHIPS, D_FF // N_CHIPS, D_MODEL)
    df = _mm_nt(dz2b, w_down, tm=512, name="df")
    dup, sm_ffn = _ffn_conv_bwd(up, df, ffn_conv_w, p["ffn_conv_b"])
    gw["w_up"] = _mm_tn(h1b, dup, n_out=dup.shape[0], tn=D_FF, ts=1024, g_block=(None, 1024, D_FF),
                        g_map=lambda j, s: (j, s, 0), split=D_FF // w_up3.shape[2], name="grad_w_up")
    exchanged = on_ready({n: gw[n] for n in ("w_down", "w_up")}) or {}
    dz1, dz1b, st1 = _up_bwd_ln1(dup, w_up3, dz2, xhat1, rstd1, p["ln1_g"])

    gw["w_o"] = _mm_tn(mixin, dz1b, n_out=1, tn=D_MODEL, ts=512, g_block=(512, D_MODEL),
                       g_map=lambda j, s: (s, 0), name="grad_w_o").reshape(N_CHIPS, D_MODEL // N_CHIPS, D_MODEL)
    dyab, dgab = _mix_bwd(dz1b, w_o, proj, yab)
    gw["w_a"] =_mm_tn(ya_in, dyab, n_out=1, tn=D_MODEL, ts=512, g_block=(512, D_MODEL),
                       g_map=lambda j, s: (s, 0), name="grad_w_a").reshape(N_CHIPS, D_CONV // N_CHIPS, D_MODEL)
    gw_b = _mm_tn(comb_b, dyab, n_out=1, tn=D_MODEL, ts=1024, g_block=(1024, D_MODEL),
                  g_map=lambda j, s: (s, 1), name="grad_w_b")
    gw["w_b"] = gw_b.reshape(GROUP_W, N_CHIPS, D_MODEL // N_CHIPS).transpose(1, 0, 2)
    exchanged_mix = on_ready({n: gw[n] for n in ("w_o", "w_a", "w_b")}) or {}
    dya_in = _mm_nt(dyab, w_a, tm=512, a_col=0, name="dya_in")
    exchanged, dya_in = lax.optimization_barrier((exchanged, dya_in))
    dbch, sm_conv = _conv_gate_bwd(proj, dya_in, conv_w)
    att_stats = _comb_bwd(dyab, w_b, comb, lse_tot)
    exchanged_mix, att_stats = lax.optimization_barrier((exchanged_mix, att_stats))
    exchanged.update(exchanged_mix)
    dqkv = [_attn_bwd(qkv[g], col0[g], g, *att_stats[g]) for g in range(N_GROUPS)]

    w_pieces, b_pieces = [], []
    for nm, planes in (("bch", dbch), ("gab", dgab)):
        pw, pc = _mm_tn(h0b, planes, n_out=planes.shape[0], tn=D_MODEL, ts=1024, g_block=(None, 1024, D_MODEL),
                        g_map=lambda j, s: (j, s, 0), colsum=True, name="grad_w_in_" + nm)
        w_pieces.extend(pw[k] for k in range(planes.shape[0]))
        b_pieces.append(pc[0])
    for g in range(N_GROUPS):
        pw, pc = _mm_tn_cat(h0_rows[g], [a.reshape(S, GROUP_W) for a in dqkv[g]], ts=1024, name=f"grad_w_in_qkv{g}")
        w_pieces.append(pw)
        b_pieces.append(pc[0])
    dw_perm = jnp.concatenate(w_pieces, axis=1)
    gw["w_in"] = jnp.stack([
        jnp.concatenate([dw_perm[:, pc:pc + w] for s, c, pc, w in sorted(runs, key=lambda r: r[1]) if s == k], axis=1)
        for k in range(N_CHIPS)])
    exchanged.update(on_ready({"w_in": gw["w_in"]}) or {})
    db_blocks = jnp.concatenate(b_pieces).reshape(N_BLK, GROUP_W)
    grad_b_in = jnp.concatenate([db_blocks[b] for b in INV_PERM])

    grad_x, st0 = _in_bwd_ln0([dbch, dgab], dqkv, w_nat, w_qkv[1:], dz1, x, p["ln0_g"])

    small = {
        "loss": st2[2:3, 0:1],
        "ln0_g": st0[0], "ln0_b": st0[1], "b_in": grad_b_in, "conv_w": sm_conv[0:3],
        "b_o": st1[2], "ln1_g": st1[0], "ln1_b": st1[1],
        "b_up": jnp.concatenate([sm_ffn[0], sm_ffn[1]]), "ffn_conv_w": sm_ffn[3:6], "ffn_conv_b": sm_ffn[2],
        "b_down": st2[3], "ln2_g": st2[0], "ln2_b": st2[1],
    }
    return grad_x, exchanged or gw, small


BIG =("w_in", "w_a", "w_b", "w_o", "w_up", "w_down")
CONV = ("conv_w", "ffn_conv_w")
VECS = ("ln0_g", "ln0_b", "b_in", "b_o", "ln1_g", "ln1_b", "b_up", "ffn_conv_b", "b_down", "ln2_g", "ln2_b")
ORDER = ("ln0_g", "ln0_b", "w_in", "b_in", "conv_w", "w_a", "w_b", "w_o", "b_o", "ln1_g", "ln1_b", "w_up", "b_up",
         "ffn_conv_w", "ffn_conv_b", "w_down", "b_down", "ln2_g", "ln2_b")
SMALL_ORDER = ("loss",) + VECS + CONV


def _step(x, target, W, Mo, Vo):
    x2, t2 = x[0], target[0]
    big2 = {n: W[n][0] for n in BIG}
    halves = lambda a: a.astype(BF16).reshape(N_CORES, a.shape[0] // N_CORES, a.shape[1])
    whole = lambda g: g.reshape(N_CHIPS, g.shape[1] * g.shape[2], g.shape[3])
    later = tuple(n for n in BIG if n != "w_in")
    w_in_halves = halves(big2["w_in"])
    first = _allgather_shards([w_in_halves], [], name="allgather_w_in", collective_id=1)

    def launch_rest(h0b):
        return _allgather_shards([halves(big2[n]) for n in later] + [W[n] for n in CONV], [h0b],
                                 name="allgather_rest", collective_id=2)

    def assemble(rest):
        gathered = {n: whole(g) for n, g in zip(later + CONV, rest)}
        return {
            "w_up": gathered["w_up"],
            "w_a": gathered["w_a"].reshape(D_CONV, D_MODEL), "w_o": gathered["w_o"].reshape(D_MODEL, D_MODEL),
            "w_down": gathered["w_down"].reshape(D_FF, D_MODEL),
            "w_b": gathered["w_b"].transpose(1, 0, 2).reshape(GROUP_W, D_MODEL),
            "conv_w": gathered["conv_w"].transpose(1, 0, 2).reshape(3, D_CONV),
            "ffn_conv_w": gathered["ffn_conv_w"].transpose(1, 0, 2).reshape(3, D_FF),
        }

    pvec = {n: W[n].reshape(1, -1) for n in VECS}

    exchange_ids = iter((3, 4, 5))

    def exchange(group):
        names = tuple(group)
        res = _exchange_grads([group[n] for n in names], name="exchange_" + "_".join(names),
                              collective_id=next(exchange_ids))
        return dict(zip(names, res))

    grad_x, parts, small = _local_step(x2, t2, pvec, (whole(first[0]), launch_rest, assemble), exchange,
                                       before_ln0=[w_in_halves])
    out = {}
    for n in BIG:
        tr = {"w_in": 128, "w_up": 128, "w_b": 128}.get(n, big2[n].shape[0] // 4)
        g, d, nm, nv = _reduce_adamw(parts[n], big2[n], Mo[n][0], Vo[n][0], tr=tr, name="adamw_" + n)
        out[n] = tuple(a[None] for a in (g, d, nm, nv))

    vec, offs = _pack([small[n] for n in SMALL_ORDER])
    off = dict(zip(SMALL_ORDER, offs))
    row = lambda a: a.reshape(1, -1)
    allv = _allgather_small(vec, parts["w_in"])
    tot, vec_out = _reduce_adamw_vectors(allv, [off[n] for n in VECS], [row(W[n]) for n in VECS],
                                         [row(Mo[n]) for n in VECS], [row(Vo[n]) for n in VECS])
    for n, res in zip(VECS, vec_out):
        out[n] = tuple(a.reshape(W[n].shape) for a in res)
    loss = tot[0, off["loss"]]
    chip = 2 * lax.axis_index("x") + lax.axis_index("y")
    taps_g = []
    for n in CONV:
        width = W[n].shape[2]
        full = lax.slice(tot, (0, off[n]), (1, off[n] + 3 * N_CHIPS * width)).reshape(3, N_CHIPS * width)
        taps_g.append(lax.dynamic_slice_in_dim(full, chip * width, width, axis=1))
    taps_out = _adamw_taps([W[n][0] for n in CONV], taps_g, [Mo[n][0] for n in CONV], [Vo[n][0] for n in CONV])
    for n, g, res in zip(CONV, taps_g, taps_out):
        out[n] = tuple(a[None] for a in (g,) + res)

    res = [loss, grad_x[None]]
    for k in range(4):
        res += [out[n][k] for n in ORDER]
    return tuple(res)


def kernel(x, ln0_g, ln0_b, w_in, b_in, conv_w, w_a, w_b, w_o, b_o, ln1_g, ln1_b, w_up, b_up, ffn_conv_w, ffn_conv_b, w_down, b_down, ln2_g, ln2_b, loss_target, m_ln0_g, m_ln0_b, m_w_in, m_b_in, m_conv_w, m_w_a, m_w_b, m_w_o, m_b_o, m_ln1_g, m_ln1_b, m_w_up, m_b_up, m_ffn_conv_w, m_ffn_conv_b, m_w_down, m_b_down, m_ln2_g, m_ln2_b, v_ln0_g, v_ln0_b, v_w_in, v_b_in, v_conv_w, v_w_a, v_w_b, v_w_o, v_b_o, v_ln1_g, v_ln1_b, v_w_up, v_b_up, v_ffn_conv_w, v_ffn_conv_b, v_w_down, v_b_down, v_ln2_g, v_ln2_b):
    W = dict(zip(ORDER, (ln0_g, ln0_b, w_in, b_in, conv_w, w_a, w_b, w_o, b_o, ln1_g, ln1_b, w_up, b_up,
                         ffn_conv_w, ffn_conv_b, w_down, b_down, ln2_g, ln2_b)))
    Mo = dict(zip(ORDER, (m_ln0_g, m_ln0_b, m_w_in, m_b_in, m_conv_w, m_w_a, m_w_b, m_w_o, m_b_o, m_ln1_g, m_ln1_b,
                          m_w_up, m_b_up, m_ffn_conv_w, m_ffn_conv_b, m_w_down, m_b_down, m_ln2_g, m_ln2_b)))
    Vo = dict(zip(ORDER, (v_ln0_g, v_ln0_b, v_w_in, v_b_in, v_conv_w, v_w_a, v_w_b, v_w_o, v_b_o, v_ln1_g, v_ln1_b,
                          v_w_up, v_b_up, v_ffn_conv_w, v_ffn_conv_b, v_w_down, v_b_down, v_ln2_g, v_ln2_b)))
    return _step(x, loss_target, W, Mo, Vo)
```

```python
import functools
import math

import jax
import jax.numpy as jnp
from jax import lax
from jax.experimental import pallas as pl
from jax.experimental.pallas import tpu as pltpu
from jax.experimental.pallas import tpu_sc as plsc

F32 = jnp.float32
BF16 = jnp.bfloat16

D_MODEL = 1024
D_CONV = D_MODEL
HEAD_DIM = 64
HEADS_PER_GROUP = 8
GROUPS = ((128, 1), (512, 4), (2048, 16))
N_GROUPS = len(GROUPS)
GROUP_W = HEADS_PER_GROUP * HEAD_DIM
QKV_W = N_GROUPS * GROUP_W
RADIUS = 64
D_FF = 2816
LN_EPS = 1e-5
ALPHA = 2.0 ** 0.25
MASK_VALUE = -1e30
ATT_SCALE = HEAD_DIM ** -0.5
OFF_B = 0
OFF_C = OFF_B + D_CONV
OFF_H = OFF_C + D_CONV
OFF_Q = OFF_H + D_CONV
OFF_K = OFF_Q + QKV_W
OFF_V = OFF_K + QKV_W
OFF_GA = OFF_V + QKV_W
OFF_GB = OFF_GA + D_MODEL
N_IN = OFF_GB + D_MODEL
ADAM_LR = 0.001
ADAM_B1 = 0.9
ADAM_B2 = 0.999
ADAM_EPS = 1e-08
ADAM_WD = 0.01
ADAM_STEP = 10
INV_SQRT2 = 0.7071067811865476
INV_SQRT_2PI = 0.3989422804014327
LOG2_E = 1.4426950408889634

LANES = 128
SUBLANES = 8
VMEM_BYTES_V7X = 64 * 1024 * 1024
N_CHIPS = 4
N_CORES = 2
N_DEV = N_CHIPS * N_CORES
MESH = pl.DeviceIdType.MESH

N_BLK = N_IN // GROUP_W
PERM = (0, 1, 2, 3, 4, 5, 15, 16, 17, 18, 6, 9, 12, 7, 10, 13, 8, 11, 14)
INV_PERM = tuple(PERM.index(b) for b in range(N_BLK))
P_B, P_C, P_H, P_GA, P_GB, P_Q0 = 0, 1024, 2048, 3072, 4096, 5120
N_NAT = P_Q0 + QKV_W // N_GROUPS * 3
N_GATED = P_Q0

def _col_runs():
    shard_w = N_IN // N_CHIPS
    runs = []
    for pos, blk in enumerate(PERM):
        c, end = blk * GROUP_W, (blk + 1) * GROUP_W
        while c < end:
            stop = min(end, (c // shard_w + 1) * shard_w)
            run = (c // shard_w, c % shard_w, pos * GROUP_W + c - blk * GROUP_W, stop - c)
            if runs and runs[-1][0] == run[0] and runs[-1][1] + runs[-1][3] == run[1]:
                runs[-1] = runs[-1][:3] + (runs[-1][3] + run[3],)
            else:
                runs.append(run)
            c = stop
    return runs


SLAB = 128
CHUNK = 256
PAD = SUBLANES
TQ = 128


def _cparams(sem, vmem_mb):
    assert vmem_mb * 1024 * 1024 < VMEM_BYTES_V7X
    return pltpu.CompilerParams(dimension_semantics=sem, vmem_limit_bytes=vmem_mb * 1024 * 1024)


def _resident(shape):
    nd = len(shape)
    return pl.BlockSpec(shape, lambda *_: (0,) * nd, pipeline_mode=pl.Buffered(1))


def _hbm(*arrays):
    return [pltpu.with_memory_space_constraint(a, pltpu.HBM) for a in arrays]


def _dot(a, b):
    return jnp.dot(a, b, preferred_element_type=F32)


def _dot_nt(a, b):
    return lax.dot_general(a, b, (((1,), (1,)), ((), ())), preferred_element_type=F32)


def _dot_tn(a, b):
    return lax.dot_general(a, b, (((0,), (0,)), ((), ())), preferred_element_type=F32)


def _ln_stats(z):
    mu = jnp.mean(z, -1, keepdims=True)
    zc = z - mu
    var = jnp.mean(zc * zc, -1, keepdims=True)
    rstd = lax.rsqrt(var + LN_EPS)
    return zc * rstd, rstd


def _ln_bwd(dh, xhat, rstd, g):
    dxh = dh * g
    m1 = jnp.mean(dxh, -1, keepdims=True)
    m2 = jnp.mean(dxh * xhat, -1, keepdims=True)
    return rstd * (dxh - m1 - xhat * m2)


def _rows8(rows, width):
    pad = [jnp.zeros((1, width), F32)] * (SUBLANES - len(rows))
    return jnp.concatenate(list(rows) + pad, axis=0)


def _mm_nn(a, w, bias, *, tm, tn, out_dtype, name, vmem_mb=40):
    M, K = a.shape
    if w.ndim == 3:
        per = tn // w.shape[2]
        assert per * w.shape[2] == tn and w.shape[0] % per == 0
        n_tiles = w.shape[0] // per
        w_spec = pl.BlockSpec((per, K, w.shape[2]), lambda j, i: (j, 0, 0))
    else:
        per = 0
        n_tiles = w.shape[1] // tn
        w_spec = pl.BlockSpec((K, tn), lambda j, i: (0, j))

    def body(a_ref, w_ref, b_ref, o_ref):
        wv = jnp.concatenate([w_ref[k] for k in range(per)], axis=1) if per else w_ref[...]
        o_ref[...] = (_dot(a_ref[...], wv) + b_ref[...]).astype(o_ref.dtype)

    return pl.pallas_call(
        body, grid=(n_tiles, M // tm),
        in_specs=[pl.BlockSpec((tm, K), lambda j, i: (i, 0)), w_spec, pl.BlockSpec((1, tn), lambda j, i: (0, j))],
        out_specs=pl.BlockSpec((tm, tn), lambda j, i: (i, j)),
        out_shape=jax.ShapeDtypeStruct((M, n_tiles * tn), out_dtype),
        name=name, compiler_params=_cparams(("arbitrary", "parallel"), vmem_mb))(*_hbm(a, w, bias))


def _mm_nt(a, w, *, tm, a_col=0, name, vmem_mb=40):
    M = a.shape[0]
    N, K = w.shape

    def body(a_ref, w_ref, o_ref):
        o_ref[...] = _dot_nt(a_ref[...], w_ref[...]).astype(o_ref.dtype)

    return pl.pallas_call(
        body, grid=(M // tm,),
        in_specs=[pl.BlockSpec((tm, K), lambda i: (i, a_col)),
                  pl.BlockSpec((N, K), lambda i: (0, 0))],
        out_specs=pl.BlockSpec((tm, N), lambda i: (i, 0)),
        out_shape=jax.ShapeDtypeStruct((M, N), BF16),
        name=name, compiler_params=_cparams(("parallel",), vmem_mb))(*_hbm(a, w))


def _mm_tn(a, g, *, n_out, tn, ts, g_block, g_map, colsum=False, split=1, name, vmem_mb=48):
    S, K = a.shape
    n_s = S // ts
    shard_w = tn // split

    def body(a_ref, g_ref, *rest):
        if colsum:
            o_ref, cs_ref, acc_ref, cacc_ref = rest
        else:
            o_ref, acc_ref = rest
        s = pl.program_id(1)

        @pl.when(s == 0)
        def _():
            acc_ref[...] = jnp.zeros_like(acc_ref)
            if colsum:
                cacc_ref[...] = jnp.zeros_like(cacc_ref)

        gv = g_ref[...]
        acc_ref[...] += _dot_tn(a_ref[...], gv)
        if colsum:
            cacc_ref[...] += jnp.broadcast_to(jnp.sum(gv.astype(F32), axis=0, keepdims=True), cacc_ref.shape)

        @pl.when(s == n_s - 1)
        def _():
            for k in range(split):
                o_ref[k] = acc_ref[:, k * shard_w:(k + 1) * shard_w].astype(o_ref.dtype)
            if colsum:
                cs_ref[...] = cacc_ref[...]

    out_specs = [pl.BlockSpec((split, K, shard_w), lambda j, s: (j, 0, 0))]
    out_shape = [jax.ShapeDtypeStruct((n_out * split, K, shard_w), BF16)]
    scratch = [pltpu.VMEM((K, tn), F32)]
    if colsum:
        out_specs.append(pl.BlockSpec((SUBLANES, tn), lambda j, s: (0, j)))
        out_shape.append(jax.ShapeDtypeStruct((SUBLANES, n_out * tn), F32))
        scratch.append(pltpu.VMEM((SUBLANES, tn), F32))
    res = pl.pallas_call(
        body, grid=(n_out, n_s),
        in_specs=[pl.BlockSpec((ts, K), lambda j, s: (s, 0)), pl.BlockSpec(g_block, g_map)],
        out_specs=out_specs, out_shape=out_shape, scratch_shapes=scratch,
        name=name, compiler_params=_cparams(("parallel", "arbitrary"), vmem_mb))(*_hbm(a, g))
    return res if colsum else res[0]


def _mm_tn_cat(a, gs, *, ts, name, vmem_mb=40):
    S, K = a.shape
    widths = [g.shape[1] for g in gs]
    n_s, total = S // ts, sum(widths)

    def body(*refs):
        a_ref, g_refs = refs[0], refs[1:1 + len(gs)]
        o_ref, cs_ref, acc_ref, cacc_ref = refs[1 + len(gs):]
        s = pl.program_id(0)

        @pl.when(s == 0)
        def _():
            acc_ref[...] = jnp.zeros_like(acc_ref)
            cacc_ref[...] = jnp.zeros_like(cacc_ref)

        av, col = a_ref[...], 0
        for g_ref, w in zip(g_refs, widths):
            gv = g_ref[...]
            acc_ref[:, col:col + w] += _dot_tn(av, gv)
            cacc_ref[:, col:col + w] += jnp.broadcast_to(jnp.sum(gv.astype(F32), axis=0, keepdims=True), (SUBLANES, w))
            col += w

        @pl.when(s == n_s - 1)
        def _():
            o_ref[...] = acc_ref[...].astype(BF16)
            cs_ref[...] = cacc_ref[...]

    return pl.pallas_call(
        body, grid=(n_s,),
        in_specs=[pl.BlockSpec((ts, K), lambda s: (s, 0))] + [pl.BlockSpec((ts, w), lambda s: (s, 0)) for w in widths],
        out_specs=[pl.BlockSpec((K, total), lambda s: (0, 0)), pl.BlockSpec((SUBLANES, total), lambda s: (0, 0))],
        out_shape=[jax.ShapeDtypeStruct((K, total), BF16), jax.ShapeDtypeStruct((SUBLANES, total), F32)],
        scratch_shapes=[pltpu.VMEM((K, total), F32), pltpu.VMEM((SUBLANES, total), F32)],
        name=name, compiler_params=_cparams(("arbitrary",), vmem_mb))(*_hbm(a, *gs))


DILS = tuple(d for _, d in GROUPS if d > 1)


def _res_spec(d, tm, width):
    return pl.BlockSpec((d, tm // d, width), lambda i: (0, i, 0))


def _lane_scratch(tm, width):
    return [pltpu.VMEM((tm, LANES), F32)] * (width // LANES)


def _to_residue(val, dst_refs, dils, tm, dtype, scr):
    for c, ref in enumerate(scr):
        ref[...] = val[:, c * LANES:(c + 1) * LANES]
    for dst_ref, d in zip(dst_refs, dils):
        for r in range(d):
            cols = [ref[pl.ds(r, tm // d, stride=d), :] for ref in scr]
            dst_ref[r] = jnp.concatenate(cols, axis=1).astype(dtype)


def _from_residue(rows_of, d, tm, scr):
    for r in range(d):
        v = rows_of(r).astype(F32)
        for c, ref in enumerate(scr):
            ref[pl.ds(r, tm // d, stride=d), :] = v[:, c * LANES:(c + 1) * LANES]
    return jnp.concatenate([ref[...] for ref in scr], axis=1)


def _ln0_fwd(x, g, b, after=(), *, tm=512):
    S, Dm = x.shape
    n_after = len(after)

    def body(x_ref, g_ref, b_ref, *rest):
        h_ref, hb_ref, *rest = rest[n_after:]
        xhat, _ = _ln_stats(x_ref[...])
        h = xhat * g_ref[...] + b_ref[...]
        h_ref[...] = h
        hb_ref[...] = h.astype(BF16)
        _to_residue(h, rest[:len(DILS)], DILS, tm, BF16, rest[len(DILS):])

    row = pl.BlockSpec((tm, Dm), lambda i: (i, 0))
    vec = pl.BlockSpec((1, Dm), lambda i: (0, 0))
    return pl.pallas_call(
        body, grid=(S // tm,), in_specs=[row, vec, vec] + [pl.BlockSpec(memory_space=pl.ANY)] * n_after,
        out_specs=[row, row] + [_res_spec(d, tm, Dm) for d in DILS],
        out_shape=[jax.ShapeDtypeStruct((S, Dm), F32), jax.ShapeDtypeStruct((S, Dm), BF16)]
        + [jax.ShapeDtypeStruct((d, S // d, Dm), BF16) for d in DILS],
        scratch_shapes=_lane_scratch(tm, Dm),
        name="ln0_fwd", compiler_params=_cparams(("parallel",), 32))(*_hbm(x, g, b), *after)


def _slab_spec(S, col0):
    return pl.BlockSpec((S, SLAB), lambda j: (0, col0 // SLAB + j))


def _zero_pads(scr, S):
    scr[0:PAD, :] = jnp.zeros((PAD, SLAB), F32)
    scr[S + PAD:S + 2 * PAD, :] = jnp.zeros((PAD, SLAB), F32)


def _shifted(scr, t):
    return (scr[PAD - 1 + t:PAD - 1 + t + CHUNK, :], scr[PAD + t:PAD + t + CHUNK, :],
            scr[PAD + 1 + t:PAD + 1 + t + CHUNK, :])


def _conv_gate_fwd(proj, conv_w):
    S = proj.shape[0]

    def body(b_ref, c_ref, h_ref, w_ref, o_ref, u_scr):
        _zero_pads(u_scr, S)
        for t in range(0, S, CHUNK):
            u_scr[PAD + t:PAD + t + CHUNK, :] = c_ref[t:t + CHUNK, :].astype(F32) * h_ref[t:t + CHUNK, :].astype(F32)
        w0, w1, w2 = w_ref[0:1, :], w_ref[1:2, :], w_ref[2:3, :]
        for t in range(0, S, CHUNK):
            um, u0, up = _shifted(u_scr, t)
            cv = w0 * um + w1 * u0 + w2 * up
            o_ref[t:t + CHUNK, :] = (b_ref[t:t + CHUNK, :].astype(F32) * cv).astype(BF16)

    return pl.pallas_call(
        body, grid=(D_CONV // SLAB,),
        in_specs=[_slab_spec(S, P_B), _slab_spec(S, P_C), _slab_spec(S, P_H),
                  pl.BlockSpec((3, SLAB), lambda j: (0, j))],
        out_specs=pl.BlockSpec((S, SLAB), lambda j: (0, j)),
        out_shape=jax.ShapeDtypeStruct((S, D_CONV), BF16),
        scratch_shapes=[pltpu.VMEM((S + 2 * PAD, SLAB), F32)],
        name="conv_gate_fwd", compiler_params=_cparams(("parallel",), 40))(*_hbm(proj, proj, proj, conv_w))


MASKED_DISTANCE = -1e34


def _attn_bias_table(g):
    dil = GROUPS[g][1]
    j = lax.broadcasted_iota(jnp.int32, (2 * TQ, TQ), 0)
    a = lax.broadcasted_iota(jnp.int32, (2 * TQ, TQ), 1)
    rel = jnp.abs(j - RADIUS - a)
    base = -(rel * dil).astype(F32)
    inside, after_start, before_end = rel <= RADIUS, j >= RADIUS, j < TQ + RADIUS
    variants = []
    for first, last in ((False, False), (True, False), (False, True), (True, True)):
        valid = inside & (after_start if first else True) & (before_end if last else True)
        variants.append(jnp.where(valid, base, MASKED_DISTANCE))
    return jnp.stack(variants)


SUBS = 4
TB = SUBS * TQ


def _ext_window(p_ref, c_ref, n_ref):
    return jnp.concatenate([p_ref[TB - RADIUS:, :], c_ref[...], n_ref[:RADIUS, :]], axis=0)


def _head_stats(rows):
    pad = jnp.zeros((LANES - len(rows), TQ), F32)
    return jnp.concatenate(list(rows) + [pad], axis=0).T


def _slope(g, h):
    return 2.0 ** (-8.0 * (g * HEADS_PER_GROUP + h + 1) / (N_GROUPS * HEADS_PER_GROUP))


def _pair(a, h):
    return a[:, (h // 2) * LANES:(h // 2 + 1) * LANES]


def _own_lanes(a, h):
    lane = lax.broadcasted_iota(jnp.int32, a.shape, 1)
    return jnp.where((lane >= HEAD_DIM) == (h % 2 == 1), a, jnp.zeros_like(a))


def _own_rows(a, h):
    return a[(h % 2) * HEAD_DIM:(h % 2 + 1) * HEAD_DIM, :]


def _attn_fwd(qkv, col0, g):
    dil, sub, _ = qkv.shape
    nb = sub // TB
    heads = HEADS_PER_GROUP

    def body(q_ref, kp, kc, kn, vp, vc, vn, bias_ref, o_ref, lse_ref, ot_scr, s_scr, p_scr):
        i = pl.program_id(1)
        kext = _ext_window(kp, kc, kn)
        vext = _ext_window(vp, vc, vn)
        q = q_ref[...] * ATT_SCALE
        for b in range(SUBS):
            kwin, qb = kext[b * TQ:(b + 2) * TQ, :], q[b * TQ:(b + 1) * TQ, :]
            for h in range(heads):
                s_scr[b * heads + h] = _dot_nt(_pair(kwin, h), _own_lanes(_pair(qb, h), h))
        inv_den = []
        for b in range(SUBS):
            block = i * SUBS + b
            bias = bias_ref[jnp.where(block == 0, 1, 0) + jnp.where(block == nb * SUBS - 1, 2, 0)]
            lse = []
            for h in range(heads):
                s = s_scr[b * heads + h] + _slope(g, h) * bias
                m = jnp.max(s, axis=0, keepdims=True)
                p = jnp.exp(s - m)
                den = jnp.sum(p, axis=0, keepdims=True)
                p_scr[b * heads + h] = p.astype(BF16)
                inv_den.append(1.0 / den)
                lse.append(m + jnp.log(den))
            lse_ref[b * TQ:(b + 1) * TQ, :] = _head_stats(lse)
        for b in range(SUBS):
            vwin = vext[b * TQ:(b + 2) * TQ, :]
            for h in range(heads):
                ot = _dot_tn(_pair(vwin, h), p_scr[b * heads + h])
                ot_scr[h * HEAD_DIM:(h + 1) * HEAD_DIM, b * TQ:(b + 1) * TQ] = _own_rows(ot, h) * inv_den[b * heads + h]
        o_ref[...] = ot_scr[...].T

    def spec(col, shift):
        return pl.BlockSpec((None, TB, GROUP_W), lambda r, i: (r, jnp.clip(i + shift, 0, nb - 1), col))

    return pl.pallas_call(
        body, grid=(dil, nb),
        in_specs=[spec(col0, 0), spec(col0 + 1, -1), spec(col0 + 1, 0), spec(col0 + 1, 1),
                  spec(col0 + 2, -1), spec(col0 + 2, 0), spec(col0 + 2, 1),
                  pl.BlockSpec((4, 2 * TQ, TQ), lambda r, i: (0, 0, 0))],
        out_specs=[pl.BlockSpec((None, TB, GROUP_W), lambda r, i: (r, i, 0)),
                   pl.BlockSpec((None, TB, LANES), lambda r, i: (r, i, 0))],
        out_shape=[jax.ShapeDtypeStruct((dil, sub, GROUP_W), F32), jax.ShapeDtypeStruct((dil, sub, LANES), F32)],
        scratch_shapes=[pltpu.VMEM((GROUP_W, TB), F32), pltpu.VMEM((SUBS * heads, 2 * TQ, TQ), F32),
                        pltpu.VMEM((SUBS * heads, 2 * TQ, TQ), BF16)],
        name=f"attn_fwd_g{g}", compiler_params=_cparams(("parallel", "arbitrary"), 32))(
            *_hbm(*([qkv] * 7), _attn_bias_table(g)))


def _expand_heads():
    h = lax.broadcasted_iota(jnp.int32, (LANES, GROUP_W), 0)
    c = lax.broadcasted_iota(jnp.int32, (LANES, GROUP_W), 1)
    return (c // HEAD_DIM == h).astype(F32)


def _dot_f32(a, b):
    return jnp.dot(a, b, preferred_element_type=F32, precision=lax.Precision.HIGH)


def _attn_combine(outs, lses, *, tm=512):
    S = outs[0].shape[1]
    n_col = GROUP_W // LANES

    def body(*refs):
        ins, e_ref = refs[:2 * N_GROUPS], refs[2 * N_GROUPS]
        c_ref, cb_ref, lt_ref = refs[2 * N_GROUPS + 1:2 * N_GROUPS + 4]
        scr = refs[2 * N_GROUPS + 4:]
        o, l = [ins[0][0]], [ins[N_GROUPS][0]]
        for k, d in enumerate(DILS):
            o_ref, l_ref = ins[1 + k], ins[N_GROUPS + 1 + k]
            o.append(_from_residue(lambda r: o_ref[r], d, tm, scr[k * (n_col + 1):k * (n_col + 1) + n_col]))
            l.append(_from_residue(lambda r: l_ref[r], d, tm, scr[k * (n_col + 1) + n_col:(k + 1) * (n_col + 1)]))
        m = jnp.maximum(jnp.maximum(l[0], l[1]), l[2])
        e = [jnp.exp(v - m) for v in l]
        den = e[0] + e[1] + e[2]
        comb = sum(_dot_f32(ev / den, e_ref[...]) * ov for ev, ov in zip(e, o))
        c_ref[...] = comb
        cb_ref[...] = comb.astype(BF16)
        lt_ref[...] = m + jnp.log(den)

    row = pl.BlockSpec((tm, GROUP_W), lambda i: (i, 0))
    dils = [d for _, d in GROUPS]
    return pl.pallas_call(
        body, grid=(S // tm,),
        in_specs=[_res_spec(d, tm, GROUP_W) for d in dils] + [_res_spec(d, tm, LANES) for d in dils]
        + [_resident((LANES, GROUP_W))],
        out_specs=[row, row, pl.BlockSpec((tm, LANES), lambda i: (i, 0))],
        out_shape=[jax.ShapeDtypeStruct((S, GROUP_W), F32), jax.ShapeDtypeStruct((S, GROUP_W), BF16),
                   jax.ShapeDtypeStruct((S, LANES), F32)],
        scratch_shapes=_lane_scratch(tm, GROUP_W + LANES) * len(DILS),
        name="attn_combine", compiler_params=_cparams(("parallel",), 32))(*_hbm(*outs, *lses, _expand_heads()))


def _branch_mix(ya_in, comb_b, w_a, w_b, proj, *, tm=512):
    S = ya_in.shape[0]

    def body(ya_ref, cb_ref, wa_ref, wb_ref, ga_ref, gb_ref, yab_ref, mx_ref):
        y_a = _dot(ya_ref[...], wa_ref[...])
        y_b = _dot(cb_ref[...], wb_ref[...])
        yab_ref[:, 0:D_MODEL] = y_a.astype(BF16)
        yab_ref[:, D_MODEL:2 * D_MODEL] = y_b.astype(BF16)
        mx = jax.nn.sigmoid(ga_ref[...].astype(F32)) * y_a + jax.nn.sigmoid(gb_ref[...].astype(F32)) * y_b
        mx_ref[...] = mx.astype(BF16)

    return pl.pallas_call(
        body, grid=(S // tm,),
        in_specs=[pl.BlockSpec((tm, D_CONV), lambda i: (i, 0)), pl.BlockSpec((tm, GROUP_W), lambda i: (i, 0)),
                  pl.BlockSpec((D_CONV, D_MODEL), lambda i: (0, 0)), pl.BlockSpec((GROUP_W, D_MODEL), lambda i: (0, 0)),
                  pl.BlockSpec((tm, D_MODEL), lambda i: (i, P_GA // D_MODEL)),
                  pl.BlockSpec((tm, D_MODEL), lambda i: (i, P_GB // D_MODEL))],
        out_specs=[pl.BlockSpec((tm, 2 * D_MODEL), lambda i: (i, 0)), pl.BlockSpec((tm, D_MODEL), lambda i: (i, 0))],
        out_shape=[jax.ShapeDtypeStruct((S, 2 * D_MODEL), BF16), jax.ShapeDtypeStruct((S, D_MODEL), BF16)],
        name="branch_mix", compiler_params=_cparams(("parallel",), 40))(*_hbm(ya_in, comb_b, w_a, w_b, proj, proj))


def _mix_ln1(mixin, w_o, b_o, h0, g1, b1, *, tm=512):
    S = mixin.shape[0]

    def body(mx_ref, wo_ref, bo_ref, h0_ref, g_ref, b_ref, xh_ref, rs_ref, h1b_ref):
        z = ALPHA * h0_ref[...] + _dot(mx_ref[...], wo_ref[...]) + bo_ref[...]
        xhat, rstd = _ln_stats(z)
        xh_ref[...] = xhat
        rs_ref[...] = jnp.broadcast_to(rstd, (tm, LANES))
        h1b_ref[...] = (xhat * g_ref[...] + b_ref[...]).astype(BF16)

    row = pl.BlockSpec((tm, D_MODEL), lambda i: (i, 0))
    vec = pl.BlockSpec((1, D_MODEL), lambda i: (0, 0))
    return pl.pallas_call(
        body, grid=(S // tm,),
        in_specs=[row, pl.BlockSpec((D_MODEL, D_MODEL), lambda i: (0, 0)), vec, row, vec, vec],
        out_specs=[row, pl.BlockSpec((tm, LANES), lambda i: (i, 0)), row],
        out_shape=[jax.ShapeDtypeStruct((S, D_MODEL), F32), jax.ShapeDtypeStruct((S, LANES), F32),
                   jax.ShapeDtypeStruct((S, D_MODEL), BF16)],
        name="mix_ln1", compiler_params=_cparams(("parallel",), 40))(*_hbm(mixin, w_o, b_o, h0, g1, b1))


def _gelu_parts(cz):
    cdf = 0.5 * (1.0 + lax.erf(cz * INV_SQRT2))
    return cdf, cz * cdf


def _ffn_conv_fwd(up, cw, cb):
    S = up.shape[0]

    def body(a_ref, g_ref, w_ref, cb_ref, o_ref, a_scr):
        _zero_pads(a_scr, S)
        for t in range(0, S, CHUNK):
            a_scr[PAD + t:PAD + t + CHUNK, :] = a_ref[t:t + CHUNK, :].astype(F32)
        w0, w1, w2 = w_ref[0:1, :], w_ref[1:2, :], w_ref[2:3, :]
        for t in range(0, S, CHUNK):
            am, a0, ap = _shifted(a_scr, t)
            _, gel = _gelu_parts(w0 * am + w1 * a0 + w2 * ap + cb_ref[...])
            o_ref[t:t + CHUNK, :] = (gel * g_ref[t:t + CHUNK, :].astype(F32)).astype(BF16)

    return pl.pallas_call(
        body, grid=(D_FF // SLAB,),
        in_specs=[_slab_spec(S, 0), _slab_spec(S, D_FF), pl.BlockSpec((3, SLAB), lambda j: (0, j)),
                  pl.BlockSpec((1, SLAB), lambda j: (0, j))],
        out_specs=pl.BlockSpec((S, SLAB), lambda j: (0, j)),
        out_shape=jax.ShapeDtypeStruct((S, D_FF), BF16),
        scratch_shapes=[pltpu.VMEM((S + 2 * PAD, SLAB), F32)],
        name="ffn_conv_fwd", compiler_params=_cparams(("parallel",), 40))(*_hbm(up, up, cw, cb))


def _down_ln2_loss(f, w_down, b_down, xhat1, g1, b1, g2, b2, target, *, tm=512):
    S = f.shape[0]

    def body(f_ref, wd_ref, bd_ref, xh1_ref, g1_ref, b1_ref, g2_ref, b2_ref, t_ref, dz_ref, dzb_ref, st_ref):
        h1 = xh1_ref[...] * g1_ref[...] + b1_ref[...]
        z = ALPHA * h1 + _dot(f_ref[...], wd_ref[...]) + bd_ref[...]
        xhat, rstd = _ln_stats(z)
        err = xhat * g2_ref[...] + b2_ref[...] - t_ref[...]
        loss = (0.5 / D_MODEL) * jnp.sum(jnp.sum(err * err, axis=1, keepdims=True), axis=0, keepdims=True)
        dh2 = err * (1.0 / D_MODEL)
        dz = _ln_bwd(dh2, xhat, rstd, g2_ref[...])
        dz_ref[...] = dz
        dzb_ref[...] = dz.astype(BF16)
        upd = _rows8([jnp.sum(dh2 * xhat, axis=0, keepdims=True), jnp.sum(dh2, axis=0, keepdims=True),
                      jnp.broadcast_to(loss, (1, D_MODEL)), jnp.sum(dz, axis=0, keepdims=True)], D_MODEL)

        @pl.when(pl.program_id(0) == 0)
        def _():
            st_ref[...] = upd

        @pl.when(pl.program_id(0) != 0)
        def _():
            st_ref[...] += upd

    row = pl.BlockSpec((tm, D_MODEL), lambda i: (i, 0))
    vec = pl.BlockSpec((1, D_MODEL), lambda i: (0, 0))
    return pl.pallas_call(
        body, grid=(S // tm,),
        in_specs=[pl.BlockSpec((tm, D_FF), lambda i: (i, 0)), _resident((D_FF, D_MODEL)),
                  vec, row, vec, vec, vec, vec, row],
        out_specs=[row, row, pl.BlockSpec((SUBLANES, D_MODEL), lambda i: (0, 0))],
        out_shape=[jax.ShapeDtypeStruct((S, D_MODEL), F32), jax.ShapeDtypeStruct((S, D_MODEL), BF16),
                   jax.ShapeDtypeStruct((SUBLANES, D_MODEL), F32)],
        name="down_ln2_loss", compiler_params=_cparams(("arbitrary",), 56))(
            *_hbm(f, w_down, b_down, xhat1, g1, b1, g2, b2, target))


def _ffn_conv_bwd(up, df, cw, cb):
    S = up.shape[0]

    def body(a_ref, g_ref, df_ref, w_ref, cb_ref, dup_ref, sm_ref, a_scr, d_scr):
        _zero_pads(a_scr, S)
        _zero_pads(d_scr, S)
        for t in range(0, S, CHUNK):
            a_scr[PAD + t:PAD + t + CHUNK, :] = a_ref[t:t + CHUNK, :].astype(F32)
        w0, w1, w2 = w_ref[0:1, :], w_ref[1:2, :], w_ref[2:3, :]
        zero = jnp.zeros((1, SLAB), F32)
        s_dg, s_dcz, s_w0, s_w1, s_w2 = zero, zero, zero, zero, zero
        for t in range(0, S, CHUNK):
            am, a0, ap = _shifted(a_scr, t)
            cz = w0 * am + w1 * a0 + w2 * ap + cb_ref[...]
            cdf, gel = _gelu_parts(cz)
            dfv = df_ref[t:t + CHUNK, :].astype(F32)
            dgte = dfv * gel
            dcz = dfv * g_ref[t:t + CHUNK, :].astype(F32) * (cdf + (cz * INV_SQRT_2PI) * jnp.exp2(cz * cz * (-0.5 * LOG2_E)))
            dup_ref[1, t:t + CHUNK, :] = dgte.astype(BF16)
            d_scr[PAD + t:PAD + t + CHUNK, :] = dcz
            s_dg = s_dg + jnp.sum(dgte, axis=0, keepdims=True)
            s_dcz = s_dcz + jnp.sum(dcz, axis=0, keepdims=True)
            s_w0 = s_w0 + jnp.sum(dcz * am, axis=0, keepdims=True)
            s_w1 = s_w1 + jnp.sum(dcz * a0, axis=0, keepdims=True)
            s_w2 = s_w2 + jnp.sum(dcz * ap, axis=0, keepdims=True)
        s_da = zero
        for t in range(0, S, CHUNK):
            dm, d0, dp = _shifted(d_scr, t)
            da = w0 * dp + w1 * d0 + w2 * dm
            dup_ref[0, t:t + CHUNK, :] = da.astype(BF16)
            s_da = s_da + jnp.sum(da, axis=0, keepdims=True)
        sm_ref[...] = _rows8([s_da, s_dg, s_dcz, s_w0, s_w1, s_w2], SLAB)

    return pl.pallas_call(
        body, grid=(D_FF // SLAB,),
        in_specs=[_slab_spec(S, 0), _slab_spec(S, D_FF), pl.BlockSpec((S, SLAB), lambda j: (0, j)),
                  pl.BlockSpec((3, SLAB), lambda j: (0, j)), pl.BlockSpec((1, SLAB), lambda j: (0, j))],
        out_specs=[pl.BlockSpec((2, S, SLAB), lambda j: (0, 0, j)), pl.BlockSpec((SUBLANES, SLAB), lambda j: (0, j))],
        out_shape=[jax.ShapeDtypeStruct((2, S, D_FF), BF16), jax.ShapeDtypeStruct((SUBLANES, D_FF), F32)],
        scratch_shapes=[pltpu.VMEM((S + 2 * PAD, SLAB), F32)] * 2,
        name="ffn_conv_bwd", compiler_params=_cparams(("parallel",), 48))(*_hbm(up, up, df, cw, cb))


def _up_bwd_ln1(dup, w_up3, dz2, xhat1, rstd1, g1, *, tm=512):
    S = dz2.shape[0]
    ns, _, tk = w_up3.shape
    per_plane = D_FF // tk

    def body(du_ref, w_ref, dz2_ref, xh_ref, rs_ref, g_ref, dz_ref, dzb_ref, st_ref):
        dh = ALPHA * dz2_ref[...]
        for plane in range(ns // per_plane):
            w = jnp.concatenate([w_ref[plane * per_plane + k] for k in range(per_plane)], axis=1)
            dh = dh + _dot_nt(du_ref[plane], w)
        xhat = xh_ref[...]
        dz = _ln_bwd(dh, xhat, rs_ref[:, 0:1], g_ref[...])
        dz_ref[...] = dz
        dzb_ref[...] = dz.astype(BF16)
        upd = _rows8([jnp.sum(dh * xhat, axis=0, keepdims=True), jnp.sum(dh, axis=0, keepdims=True),
                      jnp.sum(dz, axis=0, keepdims=True)], D_MODEL)

        @pl.when(pl.program_id(0) == 0)
        def _():
            st_ref[...] = upd

        @pl.when(pl.program_id(0) != 0)
        def _():
            st_ref[...] += upd

    row = pl.BlockSpec((tm, D_MODEL), lambda i: (i, 0))
    return pl.pallas_call(
        body, grid=(S // tm,),
        in_specs=[pl.BlockSpec((dup.shape[0], tm, D_FF), lambda i: (0, i, 0)), _resident(w_up3.shape),
                  row, row, pl.BlockSpec((tm, LANES), lambda i: (i, 0)), pl.BlockSpec((1, D_MODEL), lambda i: (0, 0))],
        out_specs=[row, row, pl.BlockSpec((SUBLANES, D_MODEL), lambda i: (0, 0))],
        out_shape=[jax.ShapeDtypeStruct((S, D_MODEL), F32), jax.ShapeDtypeStruct((S, D_MODEL), BF16),
                   jax.ShapeDtypeStruct((SUBLANES, D_MODEL), F32)],
        name="up_bwd_ln1", compiler_params=_cparams(("arbitrary",), 56))(*_hbm(dup, w_up3, dz2, xhat1, rstd1, g1))


def _mix_bwd(dz1b, w_o, proj, yab, *, tm=512):
    S = dz1b.shape[0]

    def body(dz_ref, wo_ref, ga_ref, gb_ref, y_ref, dy_ref, dg_ref):
        dmx = _dot_nt(dz_ref[...], wo_ref[...])
        for k, gt_ref in enumerate((ga_ref, gb_ref)):
            sl = slice(k * D_MODEL, (k + 1) * D_MODEL)
            sg = jax.nn.sigmoid(gt_ref[...].astype(F32))
            dy_ref[:, sl] = (dmx * sg).astype(BF16)
            dg_ref[k] = (dmx * y_ref[:, sl].astype(F32) * sg * (1.0 - sg)).astype(BF16)

    row = pl.BlockSpec((tm, D_MODEL), lambda i: (i, 0))
    wide = pl.BlockSpec((tm, 2 * D_MODEL), lambda i: (i, 0))
    return pl.pallas_call(
        body, grid=(S // tm,),
        in_specs=[row, _resident(w_o.shape), pl.BlockSpec((tm, D_MODEL), lambda i: (i, P_GA // D_MODEL)),
                  pl.BlockSpec((tm, D_MODEL), lambda i: (i, P_GB // D_MODEL)), wide],
        out_specs=[wide, pl.BlockSpec((2, tm, D_MODEL), lambda i: (0, i, 0))],
        out_shape=[jax.ShapeDtypeStruct((S, 2 * D_MODEL), BF16), jax.ShapeDtypeStruct((2, S, D_MODEL), BF16)],
        name="mix_bwd", compiler_params=_cparams(("parallel",), 40))(*_hbm(dz1b, w_o, proj, proj, yab))


def _conv_gate_bwd(proj, dya_in, conv_w):
    S = proj.shape[0]

    def body(b_ref, c_ref, h_ref, dy_ref, w_ref, o_ref, sm_ref, u_scr, d_scr):
        _zero_pads(u_scr, S)
        _zero_pads(d_scr, S)
        for t in range(0, S, CHUNK):
            u_scr[PAD + t:PAD + t + CHUNK, :] = c_ref[t:t + CHUNK, :].astype(F32) * h_ref[t:t + CHUNK, :].astype(F32)
        w0, w1, w2 = w_ref[0:1, :], w_ref[1:2, :], w_ref[2:3, :]
        zero = jnp.zeros((1, SLAB), F32)
        s_w0, s_w1, s_w2 = zero, zero, zero
        for t in range(0, S, CHUNK):
            um, u0, up = _shifted(u_scr, t)
            dy = dy_ref[t:t + CHUNK, :].astype(F32)
            o_ref[0, t:t + CHUNK, :] = (dy * (w0 * um + w1 * u0 + w2 * up)).astype(BF16)
            dcv = dy * b_ref[t:t + CHUNK, :].astype(F32)
            d_scr[PAD + t:PAD + t + CHUNK, :] = dcv
            s_w0 = s_w0 + jnp.sum(dcv * um, axis=0, keepdims=True)
            s_w1 = s_w1 + jnp.sum(dcv * u0, axis=0, keepdims=True)
            s_w2 = s_w2 + jnp.sum(dcv * up, axis=0, keepdims=True)
        for t in range(0, S, CHUNK):
            dm, d0, dp = _shifted(d_scr, t)
            du = w0 * dp + w1 * d0 + w2 * dm
            o_ref[1, t:t + CHUNK, :] = (du * h_ref[t:t + CHUNK, :].astype(F32)).astype(BF16)
            o_ref[2, t:t + CHUNK, :] = (du * c_ref[t:t + CHUNK, :].astype(F32)).astype(BF16)
        sm_ref[...] = _rows8([s_w0, s_w1, s_w2], SLAB)

    return pl.pallas_call(
        body, grid=(D_CONV // SLAB,),
        in_specs=[_slab_spec(S, P_B), _slab_spec(S, P_C), _slab_spec(S, P_H),
                  pl.BlockSpec((S, SLAB), lambda j: (0, j)), pl.BlockSpec((3, SLAB), lambda j: (0, j))],
        out_specs=[pl.BlockSpec((3, S, SLAB), lambda j: (0, 0, j)), pl.BlockSpec((SUBLANES, SLAB), lambda j: (0, j))],
        out_shape=[jax.ShapeDtypeStruct((3, S, D_CONV), BF16), jax.ShapeDtypeStruct((SUBLANES, D_CONV), F32)],
        scratch_shapes=[pltpu.VMEM((S + 2 * PAD, SLAB), F32)] * 2,
        name="conv_gate_bwd", compiler_params=_cparams(("parallel",), 48))(*_hbm(proj, proj, proj, dya_in, conv_w))


def _comb_bwd(dyab, w_b, comb, lse_tot, *, tm=512):
    S = comb.shape[0]
    widths, dtypes = (GROUP_W, LANES, LANES), (BF16, F32, F32)

    def body(dy_ref, wb_ref, c_ref, lt_ref, e_ref, *rest):
        outs, scr = rest[:3 * N_GROUPS], rest[3 * N_GROUPS:]
        dcb = _dot_nt(dy_ref[...], wb_ref[...]).astype(BF16)
        dc = dcb.astype(F32)
        delta = lax.dot_general(dc * c_ref[...], e_ref[...], (((1,), (1,)), ((), ())),
                                preferred_element_type=F32, precision=lax.Precision.HIGH)
        for k, (val, dtype) in enumerate(zip((dc, lt_ref[...], delta), dtypes)):
            outs[k][0] = val.astype(dtype)
            _to_residue(val, [outs[3 * (1 + j) + k] for j in range(len(DILS))], DILS, tm, dtype,
                        scr[:val.shape[1] // LANES])

    out_specs, out_shape = [], []
    for _, d in GROUPS:
        out_specs += [_res_spec(d, tm, w) for w in widths]
        out_shape += [jax.ShapeDtypeStruct((d, S // d, w), t) for w, t in zip(widths, dtypes)]
    res = pl.pallas_call(
        body, grid=(S // tm,),
        in_specs=[pl.BlockSpec((tm, D_MODEL), lambda i: (i, 1)), _resident(w_b.shape),
                  pl.BlockSpec((tm, GROUP_W), lambda i: (i, 0)), pl.BlockSpec((tm, LANES), lambda i: (i, 0)),
                  _resident((LANES, GROUP_W))],
        out_specs=out_specs, out_shape=out_shape, scratch_shapes=_lane_scratch(tm, GROUP_W),
        name="comb_bwd", compiler_params=_cparams(("parallel",), 32))(*_hbm(dyab, w_b, comb, lse_tot, _expand_heads()))
    return [tuple(res[3 * g:3 * g + 3]) for g in range(N_GROUPS)]


def _attn_bwd(qkv, col0, g, dcomb, lse_tot, delta):
    dil, sub, _ = qkv.shape
    nb = sub // TB
    heads = HEADS_PER_GROUP

    def body(q_ref, kp, kc, kn, vp, vc, vn, do_ref, lse_ref, dl_ref, bias_ref, dq_ref, dk_ref, dv_ref,
             ak, av, dqt_scr, s_scr, dp_scr, ds_scr, p_scr):
        i = pl.program_id(1)

        @pl.when(i == 0)
        def _():
            ak[...] = jnp.zeros_like(ak)
            av[...] = jnp.zeros_like(av)

        @pl.when(i < nb)
        def _():
            kext = _ext_window(kp, kc, kn)
            vext = _ext_window(vp, vc, vn)
            q = q_ref[...] * ATT_SCALE
            do = do_ref[...]
            lse_t, dl_t = lse_ref[...].T, dl_ref[...].T
            for b in range(SUBS):
                rows = slice(b * TQ, (b + 1) * TQ)
                kwin, vwin = kext[b * TQ:(b + 2) * TQ, :], vext[b * TQ:(b + 2) * TQ, :]
                for h in range(heads):
                    s_scr[b * heads + h] = _dot_nt(_pair(kwin, h), _own_lanes(_pair(q[rows], h), h))
                    dp_scr[b * heads + h] = _dot_nt(_pair(vwin, h), _own_lanes(_pair(do[rows], h), h))
            for b in range(SUBS):
                cols = slice(b * TQ, (b + 1) * TQ)
                block = i * SUBS + b
                bias = bias_ref[jnp.where(block == 0, 1, 0) + jnp.where(block == nb * SUBS - 1, 2, 0)]
                for h in range(heads):
                    k = b * heads + h
                    p = jnp.exp(s_scr[k] + _slope(g, h) * bias - lse_t[h:h + 1, cols])
                    ds_scr[k] = (p * (dp_scr[k] - dl_t[h:h + 1, cols])).astype(BF16)
                    p_scr[k] = p.astype(BF16)
            for b in range(SUBS):
                kwin = kext[b * TQ:(b + 2) * TQ, :]
                for h in range(heads):
                    dqt_scr[h * HEAD_DIM:(h + 1) * HEAD_DIM, b * TQ:(b + 1) * TQ] = _own_rows(
                        _dot_tn(_pair(kwin, h), ds_scr[b * heads + h]), h)
            for b in range(SUBS):
                rows = slice(b * TQ, (b + 1) * TQ)
                acc_rows = slice(TB - RADIUS + b * TQ, TB - RADIUS + (b + 2) * TQ)
                for h in range(0, heads, 2):
                    cols = slice(h * HEAD_DIM, (h + 2) * HEAD_DIM)
                    k = b * heads + h
                    q2 = jnp.concatenate([_own_lanes(_pair(q[rows], h), h), _own_lanes(_pair(q[rows], h), h + 1)], axis=0)
                    do2 = jnp.concatenate([_own_lanes(_pair(do[rows], h), h), _own_lanes(_pair(do[rows], h), h + 1)],
                                          axis=0)
                    ak[acc_rows, cols] += _dot(jnp.concatenate([ds_scr[k], ds_scr[k + 1]], axis=1), q2)
                    av[acc_rows, cols] += _dot(jnp.concatenate([p_scr[k], p_scr[k + 1]], axis=1), do2)
            dq_ref[...] = (dqt_scr[...].T * ATT_SCALE).astype(BF16)

        if nb == 1:
            dk_ref[...] = ak[TB:2 * TB, :].astype(BF16)
            dv_ref[...] = av[TB:2 * TB, :].astype(BF16)
        else:
            dk_ref[...] = ak[0:TB, :].astype(BF16)
            dv_ref[...] = av[0:TB, :].astype(BF16)
            used = 2 * TB + RADIUS
            for acc in (ak, av):
                acc[0:used - TB, :] = acc[TB:used, :]
                acc[used - TB:used, :] = jnp.zeros((TB, GROUP_W), F32)

    def spec(col, shift):
        return pl.BlockSpec((None, TB, GROUP_W), lambda r, i: (r, jnp.clip(i + shift, 0, nb - 1), col))

    tok = pl.BlockSpec((None, TB, GROUP_W), lambda r, i: (r, jnp.minimum(i, nb - 1), 0))
    stat = pl.BlockSpec((None, TB, LANES), lambda r, i: (r, jnp.minimum(i, nb - 1), 0))
    dkv_spec = tok if nb == 1 else pl.BlockSpec((None, TB, GROUP_W), lambda r, i: (r, jnp.maximum(i - 1, 0), 0))
    return pl.pallas_call(
        body, grid=(dil, nb + (nb > 1)),
        in_specs=[spec(col0, 0), spec(col0 + 1, -1), spec(col0 + 1, 0), spec(col0 + 1, 1),
                  spec(col0 + 2, -1), spec(col0 + 2, 0), spec(col0 + 2, 1), tok, stat, stat,
                  pl.BlockSpec((4, 2 * TQ, TQ), lambda r, i: (0, 0, 0))],
        out_specs=[tok, dkv_spec, dkv_spec], out_shape=[jax.ShapeDtypeStruct((dil, sub, GROUP_W), BF16)] * 3,
        scratch_shapes=[pltpu.VMEM((3 * TB, GROUP_W), F32)] * 2 + [pltpu.VMEM((GROUP_W, TB), F32)]
        + [pltpu.VMEM((SUBS * heads, 2 * TQ, TQ), F32)] * 2 + [pltpu.VMEM((SUBS * heads, 2 * TQ, TQ), BF16)] * 2,
        name=f"attn_bwd_g{g}", compiler_params=_cparams(("arbitrary", "arbitrary"), 40))(
            *_hbm(*([qkv] * 7), dcomb, lse_tot, delta, _attn_bias_table(g)))


def _in_bwd_ln0(dgated, dqkv, w_nat, w_dil, dz1, x, g0, *, tm=256):
    S = x.shape[0]
    n_gated, n_in = len(dgated), 3 * N_GROUPS

    def body(*refs):
        g_refs, d_refs = refs[:n_gated], refs[n_gated:n_gated + n_in]
        wn_ref, *wd_refs = refs[n_gated + n_in:n_gated + n_in + N_GROUPS]
        dz_ref, x_ref, g_ref, gx_ref, st_ref, *tmp_ref = refs[n_gated + n_in + N_GROUPS:]
        dh = ALPHA * dz_ref[...]
        col = 0
        for ref in g_refs:
            for k in range(ref.shape[0]):
                dh = dh + _dot_nt(ref[k], wn_ref[:, col:col + D_MODEL])
                col += D_MODEL
        for g, (_, d) in enumerate(GROUPS):
            rows = [jnp.concatenate([d_refs[3 * g + k][r] for k in range(3)], axis=1) for r in range(d)]
            w = wn_ref[:, col:col + QKV_W] if d == 1 else wd_refs[g - 1][...]
            res = _dot_nt(jnp.concatenate(rows, axis=0), w)
            if d == 1:
                dh = dh + res
            else:
                n = tm // d
                dh = dh + _from_residue(lambda r: res[r * n:(r + 1) * n, :], d, tm, tmp_ref)
        xhat, rstd = _ln_stats(x_ref[...])
        gx_ref[...] = _ln_bwd(dh, xhat, rstd, g_ref[...])
        upd = _rows8([jnp.sum(dh * xhat, axis=0, keepdims=True), jnp.sum(dh, axis=0, keepdims=True)], D_MODEL)

        @pl.when(pl.program_id(0) == 0)
        def _():
            st_ref[...] = upd

        @pl.when(pl.program_id(0) != 0)
        def _():
            st_ref[...] += upd

    row = pl.BlockSpec((tm, D_MODEL), lambda i: (i, 0))
    g_specs = [pl.BlockSpec((a.shape[0], tm, D_MODEL), lambda i: (0, i, 0)) for a in dgated]
    d_specs = []
    for _, d in GROUPS:
        d_specs += [_res_spec(d, tm, GROUP_W)] * 3
    operands = list(dgated) + [a for grp in dqkv for a in grp] + [w_nat] + list(w_dil) + [dz1, x, g0]
    return pl.pallas_call(
        body, grid=(S // tm,),
        in_specs=g_specs + d_specs + [_resident(w_nat.shape)] + [_resident(w.shape) for w in w_dil]
        + [row, row, pl.BlockSpec((1, D_MODEL), lambda i: (0, 0))],
        out_specs=[row, pl.BlockSpec((SUBLANES, D_MODEL), lambda i: (0, 0))],
        out_shape=[jax.ShapeDtypeStruct((S, D_MODEL), F32), jax.ShapeDtypeStruct((SUBLANES, D_MODEL), F32)],
        scratch_shapes=_lane_scratch(tm, D_MODEL),
        name="in_bwd_ln0", compiler_params=_cparams(("arbitrary",), 52))(*_hbm(*operands))


HBM_SPEC = pl.BlockSpec(memory_space=pltpu.HBM)


def _place():
    x, y, c = lax.axis_index("x"), lax.axis_index("y"), lax.axis_index("c")
    chips = [(1 - x, y), (x, 1 - y), (1 - x, 1 - y)]
    return x, y, c, chips


def _allgather_shards(shards, after, *, name, collective_id):
    n = len(shards)
    per = 6

    def body(*refs):
        ins, outs = refs[:n], refs[n + len(after):2 * n + len(after)]
        send_sems, recv_sems, loc_sems = refs[2 * n + len(after):]
        x, y, c, chips = _place()
        me = 2 * x + y
        sib = (x, y, 1 - c)
        peers = [sib] + [(px, py, c) for px, py in chips]
        barrier = pltpu.get_barrier_semaphore()
        for peer in peers:
            pl.semaphore_signal(barrier, inc=1, device_id=peer, device_id_type=MESH)
        pl.semaphore_wait(barrier, len(peers))

        def rcopy(w, k, src, dst, to):
            return pltpu.make_async_remote_copy(src_ref=src, dst_ref=dst, send_sem=send_sems.at[per * w + k],
                                                recv_sem=recv_sems.at[per * w + k], device_id=to, device_id_type=MESH)

        split = [s.shape[0] == N_CORES for s in shards]
        half = lambda w: c if split[w] else 0
        local, sends = [], []
        for w in range(n):
            cp = pltpu.make_async_copy(ins[w], outs[w].at[me], loc_sems.at[w])
            cp.start()
            local.append(cp)
            for j, (px, py) in enumerate(chips):
                cp = rcopy(w, j, ins[w].at[half(w)], outs[w].at[me, half(w)], (px, py, c))
                cp.start()
                sends.append(cp)
        for w in range(n):
            for j, (px, py) in enumerate(chips):
                slot = outs[w].at[2 * px + py, half(w)]
                rcopy(w, j, slot, slot, (px, py, c)).wait_recv()
                if split[w]:
                    cp = rcopy(w, 3 + j, slot, slot, sib)
                    cp.start()
                    sends.append(cp)
        for w in range(n):
            if split[w]:
                for j, (px, py) in enumerate(chips):
                    slot = outs[w].at[2 * px + py, 1 - c]
                    rcopy(w, 3 + j, slot, slot, sib).wait_recv()
        for cp in sends:
            cp.wait_send()
        for cp in local:
            cp.wait()

    return pl.kernel(
        body, out_type=[jax.ShapeDtypeStruct((N_CHIPS,) + s.shape, s.dtype) for s in shards],
        mesh=plsc.ScalarSubcoreMesh(axis_name="sequencer", num_cores=1),
        scratch_types=[pltpu.SemaphoreType.DMA((per * n,)), pltpu.SemaphoreType.DMA((per * n,)),
                       pltpu.SemaphoreType.DMA((n,))],
        name=name, compiler_params=pltpu.CompilerParams(collective_id=collective_id))(*shards, *after)


def _exchange_grads(grads, *, name, collective_id):
    n = len(grads)
    per = 7

    def body(*refs):
        ins, outs = refs[:n], refs[n:2 * n]
        send_sems, recv_sems, loc_sems = refs[2 * n:]
        x, y, c, chips = _place()
        me = 2 * x + y
        sib = (x, y, 1 - c)
        peers = [sib] + [(px, py, c) for px, py in chips]
        barrier = pltpu.get_barrier_semaphore()
        for peer in peers:
            pl.semaphore_signal(barrier, inc=1, device_id=peer, device_id_type=MESH)
        pl.semaphore_wait(barrier, len(peers))

        def rcopy(w, k, src, dst, to):
            return pltpu.make_async_remote_copy(src_ref=src, dst_ref=dst, send_sem=send_sems.at[per * w + k],
                                                recv_sem=recv_sems.at[per * w + k], device_id=to, device_id_type=MESH)

        local, sends = [], []
        for w in range(n):
            cp = pltpu.make_async_copy(ins[w].at[me], outs[w].at[c, me], loc_sems.at[w])
            cp.start()
            local.append(cp)
            cp = rcopy(w, 0, ins[w].at[me], outs[w].at[c, me], sib)
            cp.start()
            sends.append(cp)
            for j, (px, py) in enumerate(chips):
                cp = rcopy(w, 1 + j, ins[w].at[2 * px + py], outs[w].at[c, me], (px, py, c))
                cp.start()
                sends.append(cp)
        for w in range(n):
            for j, (px, py) in enumerate(chips):
                slot = outs[w].at[c, 2 * px + py]
                rcopy(w, 1 + j, slot, slot, (px, py, c)).wait_recv()
                cp = rcopy(w, 4 + j, slot, slot, sib)
                cp.start()
                sends.append(cp)
        for w in range(n):
            slot = outs[w].at[1 - c, me]
            rcopy(w, 0, slot, slot, sib).wait_recv()
            for j, (px, py) in enumerate(chips):
                slot = outs[w].at[1 - c, 2 * px + py]
                rcopy(w, 4 + j, slot, slot, sib).wait_recv()
        for cp in sends:
            cp.wait_send()
        for cp in local:
            cp.wait()

    return pl.kernel(
        body, out_type=[jax.ShapeDtypeStruct((N_CORES,) + g.shape, g.dtype) for g in grads],
        mesh=plsc.ScalarSubcoreMesh(axis_name="sequencer", num_cores=1),
        scratch_types=[pltpu.SemaphoreType.DMA((per * n,)), pltpu.SemaphoreType.DMA((per * n,)),
                       pltpu.SemaphoreType.DMA((n,))],
        name=name, compiler_params=pltpu.CompilerParams(collective_id=collective_id))(*grads)


def _allgather_small(vec, after):
    def body(v_ref, _, o_ref, send_sems, recv_sems, loc_sem):
        x, y, c = lax.axis_index("x"), lax.axis_index("y"), lax.axis_index("c")
        me = 4 * x + 2 * y + c

        def peer(k):
            flip = lambda v, bit: 1 - v if (k >> bit) & 1 else v
            return flip(x, 2), flip(y, 1), flip(c, 0)

        loc = pltpu.make_async_copy(v_ref, o_ref.at[me], loc_sem)
        loc.start()
        sends = []
        for k in range(1, N_DEV):
            cp = pltpu.make_async_remote_copy(src_ref=v_ref, dst_ref=o_ref.at[me], send_sem=send_sems.at[k - 1],
                                              recv_sem=recv_sems.at[k - 1], device_id=peer(k), device_id_type=MESH)
            cp.start()
            sends.append(cp)
        for k in range(1, N_DEV):
            px, py, pc = peer(k)
            pltpu.make_async_remote_copy(src_ref=v_ref, dst_ref=o_ref.at[4 * px + 2 * py + pc],
                                         send_sem=send_sems.at[k - 1], recv_sem=recv_sems.at[k - 1],
                                         device_id=(px, py, pc), device_id_type=MESH).wait_recv()
        for cp in sends:
            cp.wait_send()
        loc.wait()

    return pl.pallas_call(
        body, in_specs=[HBM_SPEC, HBM_SPEC], out_specs=HBM_SPEC,
        out_shape=jax.ShapeDtypeStruct((N_DEV,) + vec.shape, vec.dtype),
        scratch_shapes=[pltpu.SemaphoreType.DMA((N_DEV - 1,)), pltpu.SemaphoreType.DMA((N_DEV - 1,)),
                        pltpu.SemaphoreType.DMA],
        name="allgather_small")(vec, after)


def _adamw(w, g, m, v):
    m = ADAM_B1 * m + (1.0 - ADAM_B1) * g
    v = ADAM_B2 * v + (1.0 - ADAM_B2) * (g * g)
    m_hat = m / (1.0 - ADAM_B1 ** ADAM_STEP)
    v_hat = v / (1.0 - ADAM_B2 ** ADAM_STEP)
    delta = -ADAM_LR * (m_hat / (jnp.sqrt(v_hat) + ADAM_EPS) + ADAM_WD * w)
    return delta, m, v


def _reduce_adamw(parts, w, m, v, *, tr, name):
    R, C = w.shape

    def body(p_ref, w_ref, m_ref, v_ref, g_ref, d_ref, nm_ref, nv_ref):
        def core_sum(cc):
            s = p_ref[cc, 0].astype(F32)
            for k in range(1, N_CHIPS):
                s = s + p_ref[cc, k].astype(F32)
            return s

        g = core_sum(0) + core_sum(1)
        delta, nm, nv = _adamw(w_ref[...], g, m_ref[...], v_ref[...])
        g_ref[...] = g
        d_ref[...] = delta
        nm_ref[...] = nm
        nv_ref[...] = nv

    blk = pl.BlockSpec((tr, C), lambda i: (i, 0))
    return pl.pallas_call(
        body, grid=(R // tr,),
        in_specs=[pl.BlockSpec((N_CORES, N_CHIPS, tr, C), lambda i: (0, 0, i, 0)), blk, blk, blk],
        out_specs=[blk] * 4, out_shape=[jax.ShapeDtypeStruct((R, C), F32)] * 4,
        name=name, compiler_params=_cparams(("parallel",), 40))(*_hbm(parts, w, m, v))


def _reduce_adamw_vectors(allv, offs, ws, ms, vs):
    n = len(ws)

    def body(a_ref, *refs):
        w_refs, m_refs, v_refs = refs[:n], refs[n:2 * n], refs[2 * n:3 * n]
        tot_ref, outs = refs[3 * n], refs[3 * n + 1:]
        s = a_ref[0]
        for d in range(1, N_DEV):
            s = s + a_ref[d]
        tot_ref[...] = s
        for k in range(n):
            g = s[:, offs[k]:offs[k] + w_refs[k].shape[1]]
            delta, nm, nv = _adamw(w_refs[k][...], g, m_refs[k][...], v_refs[k][...])
            for ref, val in zip(outs[4 * k:4 * k + 4], (g, delta, nm, nv)):
                ref[...] = val

    out_shape = [jax.ShapeDtypeStruct(allv.shape[1:], F32)]
    for w in ws:
        out_shape += [jax.ShapeDtypeStruct(w.shape, F32)] * 4
    res = pl.pallas_call(body, out_shape=out_shape, name="reduce_adamw_vectors",
                         compiler_params=_cparams((), 40))(allv, *ws, *ms, *vs)
    return res[0], [tuple(res[1 + 4 * k:5 + 4 * k]) for k in range(n)]


def _adamw_taps(ws, gs, ms, vs):
    n = len(ws)

    def body(*refs):
        outs = refs[4 * n:]
        for k in range(n):
            res = _adamw(refs[k][...], refs[n + k][...], refs[2 * n + k][...], refs[3 * n + k][...])
            for ref, val in zip(outs[3 * k:3 * k + 3], res):
                ref[...] = val

    out_shape = []
    for w in ws:
        out_shape += [jax.ShapeDtypeStruct(w.shape, F32)] * 3
    res = pl.pallas_call(body, out_shape=out_shape, name="adamw_taps")(*ws, *gs, *ms, *vs)
    return [tuple(res[3 * k:3 * k + 3]) for k in range(n)]


def _pack(pieces):
    arrays, copies, offs, n = [], [], [], 0
    for runs in pieces:
        offs.append(n)
        for arr, r, c, w in runs:
            assert arr.ndim == 2 and arr.dtype == F32 and w % LANES == 0 and c % LANES == 0
            k = next((i for i, a in enumerate(arrays) if a is arr), None)
            if k is None:
                arrays.append(arr)
                k = len(arrays) - 1
            copies.append((k, r, c, w, n))
            n += w

    def body(*refs):
        o_ref = refs[-1]
        for k, r, c, w, dst in copies:
            o_ref[:, dst:dst + w] = refs[k][r:r + 1, c:c + w]

    return pl.pallas_call(body, out_shape=jax.ShapeDtypeStruct((1, n), F32), name="pack_small")(*arrays), offs


def _local_step(x, target, p, wfull, on_ready=lambda group: None, before_ln0=()):
    S = x.shape[0]
    dils = [d for _, d in GROUPS]

    h0, h0b, *h0_res = _ln0_fwd(x, p["ln0_g"], p["ln0_b"], before_ln0)
    h0_rows = [h0b] + [h.reshape(S, D_MODEL) for h in h0_res]

    if isinstance(wfull, dict):
        w_in3, pending = wfull["w_in"], None
    else:
        w_in3, launch_rest, assemble = wfull
        w_in3, h0b = lax.optimization_barrier((w_in3, h0b))
        pending = launch_rest(h0b)

    runs = _col_runs()
    def w_part(lo, hi):
        return jnp.concatenate([w_in3[s, :, c:c + w] for s, c, pc, w in runs if lo <= pc < hi], axis=1)

    b_blocks = p["b_in"].reshape(N_BLK, GROUP_W)
    n_tok = PERM.index(N_BLK - 1) + 1
    q0 = PERM[n_tok]
    assert PERM[n_tok:] == tuple(q0 + N_GROUPS * t + g for g in range(N_GROUPS) for t in range(3))
    assert PERM[:n_tok] == tuple(range(q0)) + tuple(range(q0 + 3 * N_GROUPS, N_BLK))
    b_qkv = b_blocks[q0:q0 + 3 * N_GROUPS].reshape(3, N_GROUPS, GROUP_W).transpose(1, 0, 2)
    b_perm = jnp.concatenate([b_blocks[:q0].reshape(-1), b_blocks[q0 + 3 * N_GROUPS:].reshape(-1),
                              b_qkv.reshape(-1)]).reshape(1, N_IN)
    w_nat, b_nat = w_part(0, N_NAT), b_perm[:, :N_NAT]
    qkv_cols = [slice(P_Q0 + g * QKV_W, P_Q0 + (g + 1) * QKV_W) for g in range(N_GROUPS)]
    w_qkv = [None] + [w_part(c.start, c.stop) for c in qkv_cols[1:]]

    proj = _mm_nn(h0b, w_nat, b_nat, tm=512, tn=N_NAT // 2, out_dtype=BF16, name="proj")
    qkv = [proj[None]]
    for g in range(1, N_GROUPS):
        t = _mm_nn(h0_rows[g], w_qkv[g], b_perm[:, qkv_cols[g]], tm=512, tn=QKV_W, out_dtype=BF16, name=f"proj_qkv{g}")
        qkv.append(t.reshape(dils[g], S // dils[g], QKV_W))
    if pending is not None:
        pending, qkv = lax.optimization_barrier((pending, qkv))
        proj = qkv[0][0]
        wfull = assemble(pending)
    w_up3 = wfull["w_up"]
    w_a, w_o, w_down, w_b = wfull["w_a"], wfull["w_o"], wfull["w_down"], wfull["w_b"]
    conv_w, ffn_conv_w = wfull["conv_w"], wfull["ffn_conv_w"]
    col0 = [P_Q0 // GROUP_W] + [0] * (N_GROUPS - 1)
    ya_in = _conv_gate_fwd(proj, conv_w)
    att = [_attn_fwd(qkv[g], col0[g], g) for g in range(N_GROUPS)]
    comb, comb_b, lse_tot = _attn_combine([a[0] for a in att], [a[1] for a in att])
    yab, mixin = _branch_mix(ya_in, comb_b, w_a, w_b, proj)
    xhat1, rstd1, h1b = _mix_ln1(mixin, w_o, p["b_o"], h0, p["ln1_g"], p["ln1_b"])
    up = _mm_nn(h1b, w_up3, p["b_up"], tm=512, tn=2 * w_up3.shape[2], out_dtype=BF16, name="up")
    f = _ffn_conv_fwd(up, ffn_conv_w, p["ffn_conv_b"])
    dz2, dz2b, st2 = _down_ln2_loss(f, w_down, p["b_down"], xhat1, p["ln1_g"], p["ln1_b"],
                                    p["ln2_g"], p["ln2_b"], target)

    gw = {}
    gw["w_down"] = _mm_tn(f, dz2b, n_out=1, tn=D_MODEL, ts=1024, g_block=(1024, D_MODEL),
                          g_map=lambda j, s: (s, 0), name="grad_w_down").reshape(N_CHIPS, D_FF // N_CHIPS, D_MODEL)
    df = _mm_nt(dz2b, w_down, tm=512, name="df")
    dup, sm_ffn = _ffn_conv_bwd(up, df, ffn_conv_w, p["ffn_conv_b"])
    gw["w_up"] = _mm_tn(h1b, dup, n_out=dup.shape[0], tn=D_FF, ts=1024, g_block=(None, 1024, D_FF),
                        g_map=lambda j, s: (j, s, 0), split=D_FF // w_up3.shape[2], name="grad_w_up")
    exchanged = on_ready({n: gw[n] for n in ("w_down", "w_up")}) or {}
    dz1, dz1b, st1 = _up_bwd_ln1(dup, w_up3, dz2, xhat1, rstd1, p["ln1_g"])

    gw["w_o"] = _mm_tn(mixin, dz1b, n_out=1, tn=D_MODEL, ts=512, g_block=(512, D_MODEL),
                       g_map=lambda j, s: (s, 0), name="grad_w_o").reshape(N_CHIPS, D_MODEL // N_CHIPS, D_MODEL)
    dyab, dgab = _mix_bwd(dz1b, w_o, proj, yab)
    gw["w_a"] =_mm_tn(ya_in, dyab, n_out=1, tn=D_MODEL, ts=512, g_block=(512, D_MODEL),
                       g_map=lambda j, s: (s, 0), name="grad_w_a").reshape(N_CHIPS, D_CONV // N_CHIPS, D_MODEL)
    gw_b = _mm_tn(comb_b, dyab, n_out=1, tn=D_MODEL, ts=1024, g_block=(1024, D_MODEL),
                  g_map=lambda j, s: (s, 1), name="grad_w_b")
    gw["w_b"] = gw_b.reshape(GROUP_W, N_CHIPS, D_MODEL // N_CHIPS).transpose(1, 0, 2)
    exchanged_mix = on_ready({n: gw[n] for n in ("w_o", "w_a", "w_b")}) or {}
    dya_in = _mm_nt(dyab, w_a, tm=512, a_col=0, name="dya_in")
    exchanged, dya_in = lax.optimization_barrier((exchanged, dya_in))
    dbch, sm_conv = _conv_gate_bwd(proj, dya_in, conv_w)
    att_stats = _comb_bwd(dyab, w_b, comb, lse_tot)
    exchanged_mix, att_stats = lax.optimization_barrier((exchanged_mix, att_stats))
    exchanged.update(exchanged_mix)
    dqkv = [_attn_bwd(qkv[g], col0[g], g, *att_stats[g]) for g in range(N_GROUPS)]

    w_pieces, b_pieces, b_rows = [], [], []
    for nm, planes in (("bch", dbch), ("gab", dgab)):
        pw, pc = _mm_tn(h0b, planes, n_out=planes.shape[0], tn=D_MODEL, ts=1024, g_block=(None, 1024, D_MODEL),
                        g_map=lambda j, s: (j, s, 0), colsum=True, name="grad_w_in_" + nm)
        w_pieces.extend(pw[k] for k in range(planes.shape[0]))
        b_pieces.append(pc[0])
        b_rows.append(pc)
    for g in range(N_GROUPS):
        pw, pc = _mm_tn_cat(h0_rows[g], [a.reshape(S, GROUP_W) for a in dqkv[g]], ts=1024, name=f"grad_w_in_qkv{g}")
        w_pieces.append(pw)
        b_pieces.append(pc[0])
        b_rows.append(pc)
    dw_perm = jnp.concatenate(w_pieces, axis=1)
    gw["w_in"] = jnp.stack([
        jnp.concatenate([dw_perm[:, pc:pc + w] for s, c, pc, w in sorted(runs, key=lambda r: r[1]) if s == k], axis=1)
        for k in range(N_CHIPS)])
    exchanged.update(on_ready({"w_in": gw["w_in"]}) or {})
    db_blocks = jnp.concatenate(b_pieces).reshape(N_BLK, GROUP_W)
    grad_b_in = jnp.concatenate([db_blocks[b] for b in INV_PERM])

    grad_x, st0 = _in_bwd_ln0([dbch, dgab], dqkv, w_nat, w_qkv[1:], dz1, x, p["ln0_g"])

    small = {
        "loss": st2[2:3, 0:1],
        "ln0_g": st0[0], "ln0_b": st0[1], "b_in": grad_b_in, "conv_w": sm_conv[0:3],
        "b_o": st1[2], "ln1_g": st1[0], "ln1_b": st1[1],
        "b_up": jnp.concatenate([sm_ffn[0], sm_ffn[1]]), "ffn_conv_w": sm_ffn[3:6], "ffn_conv_b": sm_ffn[2],
        "b_down": st2[3], "ln2_g": st2[0], "ln2_b": st2[1],
    }
    row = lambda a, r: [(a, r, 0, a.shape[1])]
    starts = [sum(a.shape[1] for a in b_rows[:k]) for k in range(len(b_rows))]
    b_in_src = []
    for b in range(N_BLK):
        col = INV_PERM[b] * GROUP_W
        k = max(i for i, s0 in enumerate(starts) if s0 <= col)
        b_in_src.append((b_rows[k], 0, col - starts[k], GROUP_W))
    small["_src"] = {
        "loss": [(st2, 2, 0, LANES)],
        "ln0_g": row(st0, 0), "ln0_b": row(st0, 1), "b_in": b_in_src, "conv_w": sum((row(sm_conv, r) for r in range(3)), []),
        "b_o": row(st1, 2), "ln1_g": row(st1, 0), "ln1_b": row(st1, 1),
        "b_up": row(sm_ffn, 0) + row(sm_ffn, 1), "ffn_conv_w": sum((row(sm_ffn, r) for r in (3, 4, 5)), []),
        "ffn_conv_b": row(sm_ffn, 2), "b_down": row(st2, 3), "ln2_g": row(st2, 0), "ln2_b": row(st2, 1),
    }
    return grad_x, exchanged or gw, small


BIG =("w_in", "w_a", "w_b", "w_o", "w_up", "w_down")
CONV = ("conv_w", "ffn_conv_w")
VECS = ("ln0_g", "ln0_b", "b_in", "b_o", "ln1_g", "ln1_b", "b_up", "ffn_conv_b", "b_down", "ln2_g", "ln2_b")
ORDER = ("ln0_g", "ln0_b", "w_in", "b_in", "conv_w", "w_a", "w_b", "w_o", "b_o", "ln1_g", "ln1_b", "w_up", "b_up",
         "ffn_conv_w", "ffn_conv_b", "w_down", "b_down", "ln2_g", "ln2_b")
SMALL_ORDER = ("loss",) + VECS + CONV


def _step(x, target, W, Mo, Vo):
    x2, t2 = x[0], target[0]
    big2 = {n: W[n][0] for n in BIG}
    halves = lambda a: a.astype(BF16).reshape(N_CORES, a.shape[0] // N_CORES, a.shape[1])
    whole = lambda g: g.reshape(N_CHIPS, g.shape[1] * g.shape[2], g.shape[3])
    later = tuple(n for n in BIG if n != "w_in")
    w_in_halves = halves(big2["w_in"])
    first = _allgather_shards([w_in_halves], [], name="allgather_w_in", collective_id=1)

    def launch_rest(h0b):
        return _allgather_shards([halves(big2[n]) for n in later] + [W[n] for n in CONV], [h0b],
                                 name="allgather_rest", collective_id=2)

    def assemble(rest):
        gathered = {n: whole(g) for n, g in zip(later + CONV, rest)}
        return {
            "w_up": gathered["w_up"],
            "w_a": gathered["w_a"].reshape(D_CONV, D_MODEL), "w_o": gathered["w_o"].reshape(D_MODEL, D_MODEL),
            "w_down": gathered["w_down"].reshape(D_FF, D_MODEL),
            "w_b": gathered["w_b"].transpose(1, 0, 2).reshape(GROUP_W, D_MODEL),
            "conv_w": gathered["conv_w"].transpose(1, 0, 2).reshape(3, D_CONV),
            "ffn_conv_w": gathered["ffn_conv_w"].transpose(1, 0, 2).reshape(3, D_FF),
        }

    pvec = {n: W[n].reshape(1, -1) for n in VECS}

    exchange_ids = iter((3, 4, 5))

    def exchange(group):
        names = tuple(group)
        res = _exchange_grads([group[n] for n in names], name="exchange_" + "_".join(names),
                              collective_id=next(exchange_ids))
        return dict(zip(names, res))

    grad_x, parts, small = _local_step(x2, t2, pvec, (whole(first[0]), launch_rest, assemble), exchange,
                                       before_ln0=[w_in_halves])
    out = {}
    for n in BIG:
        tr = {"w_in": 128, "w_up": 128, "w_b": 128}.get(n, big2[n].shape[0] // 4)
        g, d, nm, nv = _reduce_adamw(parts[n], big2[n], Mo[n][0], Vo[n][0], tr=tr, name="adamw_" + n)
        out[n] = tuple(a[None] for a in (g, d, nm, nv))

    vec, offs = _pack([small["_src"][n] for n in SMALL_ORDER])
    off = dict(zip(SMALL_ORDER, offs))
    row = lambda a: a.reshape(1, -1)
    allv = _allgather_small(vec, parts["w_in"])
    tot, vec_out = _reduce_adamw_vectors(allv, [off[n] for n in VECS], [row(W[n]) for n in VECS],
                                         [row(Mo[n]) for n in VECS], [row(Vo[n]) for n in VECS])
    for n, res in zip(VECS, vec_out):
        out[n] = tuple(a.reshape(W[n].shape) for a in res)
    loss = tot[0, off["loss"]]
    chip = 2 * lax.axis_index("x") + lax.axis_index("y")
    taps_g = []
    for n in CONV:
        width = W[n].shape[2]
        full = lax.slice(tot, (0, off[n]), (1, off[n] + 3 * N_CHIPS * width)).reshape(3, N_CHIPS * width)
        taps_g.append(lax.dynamic_slice_in_dim(full, chip * width, width, axis=1))
    taps_out = _adamw_taps([W[n][0] for n in CONV], taps_g, [Mo[n][0] for n in CONV], [Vo[n][0] for n in CONV])
    for n, g, res in zip(CONV, taps_g, taps_out):
        out[n] = tuple(a[None] for a in (g,) + res)

    res = [loss, grad_x[None]]
    for k in range(4):
        res += [out[n][k] for n in ORDER]
    return tuple(res)


def kernel(x, ln0_g, ln0_b, w_in, b_in, conv_w, w_a, w_b, w_o, b_o, ln1_g, ln1_b, w_up, b_up, ffn_conv_w, ffn_conv_b, w_down, b_down, ln2_g, ln2_b, loss_target, m_ln0_g, m_ln0_b, m_w_in, m_b_in, m_conv_w, m_w_a, m_w_b, m_w_o, m_b_o, m_ln1_g, m_ln1_b, m_w_up, m_b_up, m_ffn_conv_w, m_ffn_conv_b, m_w_down, m_b_down, m_ln2_g, m_ln2_b, v_ln0_g, v_ln0_b, v_w_in, v_b_in, v_conv_w, v_w_a, v_w_b, v_w_o, v_b_o, v_ln1_g, v_ln1_b, v_w_up, v_b_up, v_ffn_conv_w, v_ffn_conv_b, v_w_down, v_b_down, v_ln2_g, v_ln2_b):
    W = dict(zip(ORDER, (ln0_g, ln0_b, w_in, b_in, conv_w, w_a, w_b, w_o, b_o, ln1_g, ln1_b, w_up, b_up,
                         ffn_conv_w, ffn_conv_b, w_down, b_down, ln2_g, ln2_b)))
    Mo = dict(zip(ORDER, (m_ln0_g, m_ln0_b, m_w_in, m_b_in, m_conv_w, m_w_a, m_w_b, m_w_o, m_b_o, m_ln1_g, m_ln1_b,
                          m_w_up, m_b_up, m_ffn_conv_w, m_ffn_conv_b, m_w_down, m_b_down, m_ln2_g, m_ln2_b)))
    Vo = dict(zip(ORDER, (v_ln0_g, v_ln0_b, v_w_in, v_b_in, v_conv_w, v_w_a, v_w_b, v_w_o, v_b_o, v_ln1_g, v_ln1_b,
                          v_w_up, v_b_up, v_ffn_conv_w, v_ffn_conv_b, v_w_down, v_b_down, v_ln2_g, v_ln2_b)))
    return _step(x, loss_target, W, Mo, Vo)
```

```python
import functools
import math

import jax
import jax.numpy as jnp
from jax import lax
from jax.experimental import pallas as pl
from jax.experimental.pallas import tpu as pltpu
from jax.experimental.pallas import tpu_sc as plsc

F32 = jnp.float32
BF16 = jnp.bfloat16

D_MODEL = 1024
D_CONV = D_MODEL
HEAD_DIM = 64
HEADS_PER_GROUP = 8
GROUPS = ((128, 1), (512, 4), (2048, 16))
N_GROUPS = len(GROUPS)
GROUP_W = HEADS_PER_GROUP * HEAD_DIM
QKV_W = N_GROUPS * GROUP_W
RADIUS = 64
D_FF = 2816
LN_EPS = 1e-5
ALPHA = 2.0 ** 0.25
MASK_VALUE = -1e30
ATT_SCALE = HEAD_DIM ** -0.5
OFF_B = 0
OFF_C = OFF_B + D_CONV
OFF_H = OFF_C + D_CONV
OFF_Q = OFF_H + D_CONV
OFF_K = OFF_Q + QKV_W
OFF_V = OFF_K + QKV_W
OFF_GA = OFF_V + QKV_W
OFF_GB = OFF_GA + D_MODEL
N_IN = OFF_GB + D_MODEL
ADAM_LR = 0.001
ADAM_B1 = 0.9
ADAM_B2 = 0.999
ADAM_EPS = 1e-08
ADAM_WD = 0.01
ADAM_STEP = 10
INV_SQRT2 = 0.7071067811865476
INV_SQRT_2PI = 0.3989422804014327
LOG2_E = 1.4426950408889634

LANES = 128
SUBLANES = 8
VMEM_BYTES_V7X = 64 * 1024 * 1024
N_CHIPS = 4
N_CORES = 2
N_DEV = N_CHIPS * N_CORES
MESH = pl.DeviceIdType.MESH

N_BLK = N_IN // GROUP_W
PERM = (0, 1, 2, 3, 4, 5, 15, 16, 17, 18, 6, 9, 12, 7, 10, 13, 8, 11, 14)
INV_PERM = tuple(PERM.index(b) for b in range(N_BLK))
P_B, P_C, P_H, P_GA, P_GB, P_Q0 = 0, 1024, 2048, 3072, 4096, 5120
N_NAT = P_Q0 + QKV_W // N_GROUPS * 3
N_GATED = P_Q0

def _col_runs():
    shard_w = N_IN // N_CHIPS
    runs = []
    for pos, blk in enumerate(PERM):
        c, end = blk * GROUP_W, (blk + 1) * GROUP_W
        while c < end:
            stop = min(end, (c // shard_w + 1) * shard_w)
            run = (c // shard_w, c % shard_w, pos * GROUP_W + c - blk * GROUP_W, stop - c)
            if runs and runs[-1][0] == run[0] and runs[-1][1] + runs[-1][3] == run[1]:
                runs[-1] = runs[-1][:3] + (runs[-1][3] + run[3],)
            else:
                runs.append(run)
            c = stop
    return runs


SLAB = 128
CHUNK = 256
PAD = SUBLANES
TQ = 128


def _cparams(sem, vmem_mb):
    assert vmem_mb * 1024 * 1024 < VMEM_BYTES_V7X
    return pltpu.CompilerParams(dimension_semantics=sem, vmem_limit_bytes=vmem_mb * 1024 * 1024)


def _resident(shape):
    nd = len(shape)
    return pl.BlockSpec(shape, lambda *_: (0,) * nd, pipeline_mode=pl.Buffered(1))


def _hbm(*arrays):
    return [pltpu.with_memory_space_constraint(a, pltpu.HBM) for a in arrays]


def _dot(a, b):
    return jnp.dot(a, b, preferred_element_type=F32)


def _dot_nt(a, b):
    return lax.dot_general(a, b, (((1,), (1,)), ((), ())), preferred_element_type=F32)


def _dot_tn(a, b):
    return lax.dot_general(a, b, (((0,), (0,)), ((), ())), preferred_element_type=F32)


def _ln_stats(z):
    mu = jnp.mean(z, -1, keepdims=True)
    zc = z - mu
    var = jnp.mean(zc * zc, -1, keepdims=True)
    rstd = lax.rsqrt(var + LN_EPS)
    return zc * rstd, rstd


def _ln_bwd(dh, xhat, rstd, g):
    dxh = dh * g
    m1 = jnp.mean(dxh, -1, keepdims=True)
    m2 = jnp.mean(dxh * xhat, -1, keepdims=True)
    return rstd * (dxh - m1 - xhat * m2)


def _rows8(rows, width):
    pad = [jnp.zeros((1, width), F32)] * (SUBLANES - len(rows))
    return jnp.concatenate(list(rows) + pad, axis=0)


def _mm_nn(a, w, bias, *, tm, tn, out_dtype, name, vmem_mb=40):
    M, K = a.shape
    if w.ndim == 3:
        per = tn // w.shape[2]
        assert per * w.shape[2] == tn and w.shape[0] % per == 0
        n_tiles = w.shape[0] // per
        w_spec = pl.BlockSpec((per, K, w.shape[2]), lambda j, i: (j, 0, 0))
    else:
        per = 0
        n_tiles = w.shape[1] // tn
        w_spec = pl.BlockSpec((K, tn), lambda j, i: (0, j))

    def body(a_ref, w_ref, b_ref, o_ref):
        wv = jnp.concatenate([w_ref[k] for k in range(per)], axis=1) if per else w_ref[...]
        o_ref[...] = (_dot(a_ref[...], wv) + b_ref[...]).astype(o_ref.dtype)

    return pl.pallas_call(
        body, grid=(n_tiles, M // tm),
        in_specs=[pl.BlockSpec((tm, K), lambda j, i: (i, 0)), w_spec, pl.BlockSpec((1, tn), lambda j, i: (0, j))],
        out_specs=pl.BlockSpec((tm, tn), lambda j, i: (i, j)),
        out_shape=jax.ShapeDtypeStruct((M, n_tiles * tn), out_dtype),
        name=name, compiler_params=_cparams(("arbitrary", "parallel"), vmem_mb))(*_hbm(a, w, bias))


def _mm_nt(a, w, *, tm, a_col=0, name, vmem_mb=40):
    M = a.shape[0]
    N, K = w.shape

    def body(a_ref, w_ref, o_ref):
        o_ref[...] = _dot_nt(a_ref[...], w_ref[...]).astype(o_ref.dtype)

    return pl.pallas_call(
        body, grid=(M // tm,),
        in_specs=[pl.BlockSpec((tm, K), lambda i: (i, a_col)),
                  pl.BlockSpec((N, K), lambda i: (0, 0))],
        out_specs=pl.BlockSpec((tm, N), lambda i: (i, 0)),
        out_shape=jax.ShapeDtypeStruct((M, N), BF16),
        name=name, compiler_params=_cparams(("parallel",), vmem_mb))(*_hbm(a, w))


def _mm_tn(a, g, *, n_out, tn, ts, g_block, g_map, colsum=False, split=1, name, vmem_mb=48):
    S, K = a.shape
    n_s = S // ts
    shard_w = tn // split

    def body(a_ref, g_ref, *rest):
        if colsum:
            o_ref, cs_ref, acc_ref, cacc_ref = rest
        else:
            o_ref, acc_ref = rest
        s = pl.program_id(1)

        @pl.when(s == 0)
        def _():
            acc_ref[...] = jnp.zeros_like(acc_ref)
            if colsum:
                cacc_ref[...] = jnp.zeros_like(cacc_ref)

        gv = g_ref[...]
        acc_ref[...] += _dot_tn(a_ref[...], gv)
        if colsum:
            cacc_ref[...] += jnp.broadcast_to(jnp.sum(gv.astype(F32), axis=0, keepdims=True), cacc_ref.shape)

        @pl.when(s == n_s - 1)
        def _():
            for k in range(split):
                o_ref[k] = acc_ref[:, k * shard_w:(k + 1) * shard_w].astype(o_ref.dtype)
            if colsum:
                cs_ref[...] = cacc_ref[...]

    out_specs = [pl.BlockSpec((split, K, shard_w), lambda j, s: (j, 0, 0))]
    out_shape = [jax.ShapeDtypeStruct((n_out * split, K, shard_w), BF16)]
    scratch = [pltpu.VMEM((K, tn), F32)]
    if colsum:
        out_specs.append(pl.BlockSpec((SUBLANES, tn), lambda j, s: (0, j)))
        out_shape.append(jax.ShapeDtypeStruct((SUBLANES, n_out * tn), F32))
        scratch.append(pltpu.VMEM((SUBLANES, tn), F32))
    res = pl.pallas_call(
        body, grid=(n_out, n_s),
        in_specs=[pl.BlockSpec((ts, K), lambda j, s: (s, 0)), pl.BlockSpec(g_block, g_map)],
        out_specs=out_specs, out_shape=out_shape, scratch_shapes=scratch,
        name=name, compiler_params=_cparams(("parallel", "arbitrary"), vmem_mb))(*_hbm(a, g))
    return res if colsum else res[0]


def _mm_tn_cat(a, gs, *, ts, name, vmem_mb=40):
    S, K = a.shape
    widths = [g.shape[1] for g in gs]
    n_s, total = S // ts, sum(widths)

    def body(*refs):
        a_ref, g_refs = refs[0], refs[1:1 + len(gs)]
        o_ref, cs_ref, acc_ref, cacc_ref = refs[1 + len(gs):]
        s = pl.program_id(0)

        @pl.when(s == 0)
        def _():
            acc_ref[...] = jnp.zeros_like(acc_ref)
            cacc_ref[...] = jnp.zeros_like(cacc_ref)

        av, col = a_ref[...], 0
        for g_ref, w in zip(g_refs, widths):
            gv = g_ref[...]
            acc_ref[:, col:col + w] += _dot_tn(av, gv)
            cacc_ref[:, col:col + w] += jnp.broadcast_to(jnp.sum(gv.astype(F32), axis=0, keepdims=True), (SUBLANES, w))
            col += w

        @pl.when(s == n_s - 1)
        def _():
            o_ref[...] = acc_ref[...].astype(BF16)
            cs_ref[...] = cacc_ref[...]

    return pl.pallas_call(
        body, grid=(n_s,),
        in_specs=[pl.BlockSpec((ts, K), lambda s: (s, 0))] + [pl.BlockSpec((ts, w), lambda s: (s, 0)) for w in widths],
        out_specs=[pl.BlockSpec((K, total), lambda s: (0, 0)), pl.BlockSpec((SUBLANES, total), lambda s: (0, 0))],
        out_shape=[jax.ShapeDtypeStruct((K, total), BF16), jax.ShapeDtypeStruct((SUBLANES, total), F32)],
        scratch_shapes=[pltpu.VMEM((K, total), F32), pltpu.VMEM((SUBLANES, total), F32)],
        name=name, compiler_params=_cparams(("arbitrary",), vmem_mb))(*_hbm(a, *gs))


DILS = tuple(d for _, d in GROUPS if d > 1)


def _res_spec(d, tm, width):
    return pl.BlockSpec((d, tm // d, width), lambda i: (0, i, 0))


def _lane_scratch(tm, width):
    return [pltpu.VMEM((tm, LANES), F32)] * (width // LANES)


def _to_residue(val, dst_refs, dils, tm, dtype, scr):
    for c, ref in enumerate(scr):
        ref[...] = val[:, c * LANES:(c + 1) * LANES]
    for dst_ref, d in zip(dst_refs, dils):
        for r in range(d):
            cols = [ref[pl.ds(r, tm // d, stride=d), :] for ref in scr]
            dst_ref[r] = jnp.concatenate(cols, axis=1).astype(dtype)


def _from_residue(rows_of, d, tm, scr):
    for r in range(d):
        v = rows_of(r).astype(F32)
        for c, ref in enumerate(scr):
            ref[pl.ds(r, tm // d, stride=d), :] = v[:, c * LANES:(c + 1) * LANES]
    return jnp.concatenate([ref[...] for ref in scr], axis=1)


def _ln0_fwd(x, g, b, after=(), *, tm=512):
    S, Dm = x.shape
    n_after = len(after)

    def body(x_ref, g_ref, b_ref, *rest):
        h_ref, hb_ref, *rest = rest[n_after:]
        xhat, _ = _ln_stats(x_ref[...])
        h = xhat * g_ref[...] + b_ref[...]
        h_ref[...] = h
        hb_ref[...] = h.astype(BF16)
        _to_residue(h, rest[:len(DILS)], DILS, tm, BF16, rest[len(DILS):])

    row = pl.BlockSpec((tm, Dm), lambda i: (i, 0))
    vec = pl.BlockSpec((1, Dm), lambda i: (0, 0))
    return pl.pallas_call(
        body, grid=(S // tm,), in_specs=[row, vec, vec] + [pl.BlockSpec(memory_space=pl.ANY)] * n_after,
        out_specs=[row, row] + [_res_spec(d, tm, Dm) for d in DILS],
        out_shape=[jax.ShapeDtypeStruct((S, Dm), F32), jax.ShapeDtypeStruct((S, Dm), BF16)]
        + [jax.ShapeDtypeStruct((d, S // d, Dm), BF16) for d in DILS],
        scratch_shapes=_lane_scratch(tm, Dm),
        name="ln0_fwd", compiler_params=_cparams(("parallel",), 32))(*_hbm(x, g, b), *after)


def _slab_spec(S, col0):
    return pl.BlockSpec((S, SLAB), lambda j: (0, col0 // SLAB + j))


def _zero_pads(scr, S):
    scr[0:PAD, :] = jnp.zeros((PAD, SLAB), F32)
    scr[S + PAD:S + 2 * PAD, :] = jnp.zeros((PAD, SLAB), F32)


def _shifted(scr, t):
    return (scr[PAD - 1 + t:PAD - 1 + t + CHUNK, :], scr[PAD + t:PAD + t + CHUNK, :],
            scr[PAD + 1 + t:PAD + 1 + t + CHUNK, :])


def _conv_gate_fwd(proj, conv_w):
    S = proj.shape[0]

    def body(b_ref, c_ref, h_ref, w_ref, o_ref, u_scr):
        _zero_pads(u_scr, S)
        for t in range(0, S, CHUNK):
            u_scr[PAD + t:PAD + t + CHUNK, :] = c_ref[t:t + CHUNK, :].astype(F32) * h_ref[t:t + CHUNK, :].astype(F32)
        w0, w1, w2 = w_ref[0:1, :], w_ref[1:2, :], w_ref[2:3, :]
        for t in range(0, S, CHUNK):
            um, u0, up = _shifted(u_scr, t)
            cv = w0 * um + w1 * u0 + w2 * up
            o_ref[t:t + CHUNK, :] = (b_ref[t:t + CHUNK, :].astype(F32) * cv).astype(BF16)

    return pl.pallas_call(
        body, grid=(D_CONV // SLAB,),
        in_specs=[_slab_spec(S, P_B), _slab_spec(S, P_C), _slab_spec(S, P_H),
                  pl.BlockSpec((3, SLAB), lambda j: (0, j))],
        out_specs=pl.BlockSpec((S, SLAB), lambda j: (0, j)),
        out_shape=jax.ShapeDtypeStruct((S, D_CONV), BF16),
        scratch_shapes=[pltpu.VMEM((S + 2 * PAD, SLAB), F32)],
        name="conv_gate_fwd", compiler_params=_cparams(("parallel",), 40))(*_hbm(proj, proj, proj, conv_w))


MASKED_DISTANCE = -1e34


def _attn_bias_table(g):
    dil = GROUPS[g][1]
    j = lax.broadcasted_iota(jnp.int32, (2 * TQ, TQ), 0)
    a = lax.broadcasted_iota(jnp.int32, (2 * TQ, TQ), 1)
    rel = jnp.abs(j - RADIUS - a)
    base = -(rel * dil).astype(F32)
    inside, after_start, before_end = rel <= RADIUS, j >= RADIUS, j < TQ + RADIUS
    variants = []
    for first, last in ((False, False), (True, False), (False, True), (True, True)):
        valid = inside & (after_start if first else True) & (before_end if last else True)
        variants.append(jnp.where(valid, base, MASKED_DISTANCE))
    return jnp.stack(variants)


SUBS = 4
TB = SUBS * TQ


def _ext_window(p_ref, c_ref, n_ref):
    return jnp.concatenate([p_ref[TB - RADIUS:, :], c_ref[...], n_ref[:RADIUS, :]], axis=0)


def _head_stats(rows):
    pad = jnp.zeros((LANES - len(rows), TQ), F32)
    return jnp.concatenate(list(rows) + [pad], axis=0).T


def _slope(g, h):
    return 2.0 ** (-8.0 * (g * HEADS_PER_GROUP + h + 1) / (N_GROUPS * HEADS_PER_GROUP))


def _pair(a, h):
    return a[:, (h // 2) * LANES:(h // 2 + 1) * LANES]


def _own_lanes(a, h):
    lane = lax.broadcasted_iota(jnp.int32, a.shape, 1)
    return jnp.where((lane >= HEAD_DIM) == (h % 2 == 1), a, jnp.zeros_like(a))


def _own_rows(a, h):
    return a[(h % 2) * HEAD_DIM:(h % 2 + 1) * HEAD_DIM, :]


def _attn_fwd(qkv, col0, g):
    dil, sub, _ = qkv.shape
    nb = sub // TB
    heads = HEADS_PER_GROUP

    def body(q_ref, kp, kc, kn, vp, vc, vn, bias_ref, o_ref, lse_ref, ot_scr, s_scr, p_scr):
        i = pl.program_id(1)
        kext = _ext_window(kp, kc, kn)
        vext = _ext_window(vp, vc, vn)
        q = q_ref[...] * ATT_SCALE
        for b in range(SUBS):
            kwin, qb = kext[b * TQ:(b + 2) * TQ, :], q[b * TQ:(b + 1) * TQ, :]
            for h in range(heads):
                s_scr[b * heads + h] = _dot_nt(_pair(kwin, h), _own_lanes(_pair(qb, h), h))
        inv_den = []
        for b in range(SUBS):
            block = i * SUBS + b
            bias = bias_ref[jnp.where(block == 0, 1, 0) + jnp.where(block == nb * SUBS - 1, 2, 0)]
            lse = []
            for h in range(heads):
                s = s_scr[b * heads + h] + _slope(g, h) * bias
                m = jnp.max(s, axis=0, keepdims=True)
                p = jnp.exp(s - m)
                den = jnp.sum(p, axis=0, keepdims=True)
                p_scr[b * heads + h] = p.astype(BF16)
                inv_den.append(1.0 / den)
                lse.append(m + jnp.log(den))
            lse_ref[b * TQ:(b + 1) * TQ, :] = _head_stats(lse)
        for b in range(SUBS):
            vwin = vext[b * TQ:(b + 2) * TQ, :]
            for h in range(heads):
                ot = _dot_tn(_pair(vwin, h), p_scr[b * heads + h])
                ot_scr[h * HEAD_DIM:(h + 1) * HEAD_DIM, b * TQ:(b + 1) * TQ] = _own_rows(ot, h) * inv_den[b * heads + h]
        o_ref[...] = ot_scr[...].T

    def spec(col, shift):
        return pl.BlockSpec((None, TB, GROUP_W), lambda r, i: (r, jnp.clip(i + shift, 0, nb - 1), col))

    return pl.pallas_call(
        body, grid=(dil, nb),
        in_specs=[spec(col0, 0), spec(col0 + 1, -1), spec(col0 + 1, 0), spec(col0 + 1, 1),
                  spec(col0 + 2, -1), spec(col0 + 2, 0), spec(col0 + 2, 1),
                  pl.BlockSpec((4, 2 * TQ, TQ), lambda r, i: (0, 0, 0))],
        out_specs=[pl.BlockSpec((None, TB, GROUP_W), lambda r, i: (r, i, 0)),
                   pl.BlockSpec((None, TB, LANES), lambda r, i: (r, i, 0))],
        out_shape=[jax.ShapeDtypeStruct((dil, sub, GROUP_W), F32), jax.ShapeDtypeStruct((dil, sub, LANES), F32)],
        scratch_shapes=[pltpu.VMEM((GROUP_W, TB), F32), pltpu.VMEM((SUBS * heads, 2 * TQ, TQ), F32),
                        pltpu.VMEM((SUBS * heads, 2 * TQ, TQ), BF16)],
        name=f"attn_fwd_g{g}", compiler_params=_cparams(("parallel", "arbitrary"), 32))(
            *_hbm(*([qkv] * 7), _attn_bias_table(g)))


def _expand_heads():
    h = lax.broadcasted_iota(jnp.int32, (LANES, GROUP_W), 0)
    c = lax.broadcasted_iota(jnp.int32, (LANES, GROUP_W), 1)
    return (c // HEAD_DIM == h).astype(F32)


def _dot_f32(a, b):
    return jnp.dot(a, b, preferred_element_type=F32, precision=lax.Precision.HIGH)


def _attn_combine(outs, lses, *, tm=512):
    S = outs[0].shape[1]
    n_col = GROUP_W // LANES

    def body(*refs):
        ins, e_ref = refs[:2 * N_GROUPS], refs[2 * N_GROUPS]
        c_ref, cb_ref, lt_ref = refs[2 * N_GROUPS + 1:2 * N_GROUPS + 4]
        scr = refs[2 * N_GROUPS + 4:]
        o, l = [ins[0][0]], [ins[N_GROUPS][0]]
        for k, d in enumerate(DILS):
            o_ref, l_ref = ins[1 + k], ins[N_GROUPS + 1 + k]
            o.append(_from_residue(lambda r: o_ref[r], d, tm, scr[k * (n_col + 1):k * (n_col + 1) + n_col]))
            l.append(_from_residue(lambda r: l_ref[r], d, tm, scr[k * (n_col + 1) + n_col:(k + 1) * (n_col + 1)]))
        m = jnp.maximum(jnp.maximum(l[0], l[1]), l[2])
        e = [jnp.exp(v - m) for v in l]
        den = e[0] + e[1] + e[2]
        comb = sum(_dot_f32(ev / den, e_ref[...]) * ov for ev, ov in zip(e, o))
        c_ref[...] = comb
        cb_ref[...] = comb.astype(BF16)
        lt_ref[...] = m + jnp.log(den)

    row = pl.BlockSpec((tm, GROUP_W), lambda i: (i, 0))
    dils = [d for _, d in GROUPS]
    return pl.pallas_call(
        body, grid=(S // tm,),
        in_specs=[_res_spec(d, tm, GROUP_W) for d in dils] + [_res_spec(d, tm, LANES) for d in dils]
        + [_resident((LANES, GROUP_W))],
        out_specs=[row, row, pl.BlockSpec((tm, LANES), lambda i: (i, 0))],
        out_shape=[jax.ShapeDtypeStruct((S, GROUP_W), F32), jax.ShapeDtypeStruct((S, GROUP_W), BF16),
                   jax.ShapeDtypeStruct((S, LANES), F32)],
        scratch_shapes=_lane_scratch(tm, GROUP_W + LANES) * len(DILS),
        name="attn_combine", compiler_params=_cparams(("parallel",), 32))(*_hbm(*outs, *lses, _expand_heads()))


def _branch_mix(ya_in, comb_b, w_a, w_b, proj, *, tm=512):
    S = ya_in.shape[0]

    def body(ya_ref, cb_ref, wa_ref, wb_ref, ga_ref, gb_ref, yab_ref, mx_ref):
        y_a = _dot(ya_ref[...], wa_ref[...])
        y_b = _dot(cb_ref[...], wb_ref[...])
        yab_ref[:, 0:D_MODEL] = y_a.astype(BF16)
        yab_ref[:, D_MODEL:2 * D_MODEL] = y_b.astype(BF16)
        mx = jax.nn.sigmoid(ga_ref[...].astype(F32)) * y_a + jax.nn.sigmoid(gb_ref[...].astype(F32)) * y_b
        mx_ref[...] = mx.astype(BF16)

    return pl.pallas_call(
        body, grid=(S // tm,),
        in_specs=[pl.BlockSpec((tm, D_CONV), lambda i: (i, 0)), pl.BlockSpec((tm, GROUP_W), lambda i: (i, 0)),
                  pl.BlockSpec((D_CONV, D_MODEL), lambda i: (0, 0)), pl.BlockSpec((GROUP_W, D_MODEL), lambda i: (0, 0)),
                  pl.BlockSpec((tm, D_MODEL), lambda i: (i, P_GA // D_MODEL)),
                  pl.BlockSpec((tm, D_MODEL), lambda i: (i, P_GB // D_MODEL))],
        out_specs=[pl.BlockSpec((tm, 2 * D_MODEL), lambda i: (i, 0)), pl.BlockSpec((tm, D_MODEL), lambda i: (i, 0))],
        out_shape=[jax.ShapeDtypeStruct((S, 2 * D_MODEL), BF16), jax.ShapeDtypeStruct((S, D_MODEL), BF16)],
        name="branch_mix", compiler_params=_cparams(("parallel",), 40))(*_hbm(ya_in, comb_b, w_a, w_b, proj, proj))


def _mix_ln1(mixin, w_o, b_o, h0, g1, b1, *, tm=512):
    S = mixin.shape[0]

    def body(mx_ref, wo_ref, bo_ref, h0_ref, g_ref, b_ref, xh_ref, rs_ref, h1b_ref):
        z = ALPHA * h0_ref[...] + _dot(mx_ref[...], wo_ref[...]) + bo_ref[...]
        xhat, rstd = _ln_stats(z)
        xh_ref[...] = xhat
        rs_ref[...] = jnp.broadcast_to(rstd, (tm, LANES))
        h1b_ref[...] = (xhat * g_ref[...] + b_ref[...]).astype(BF16)

    row = pl.BlockSpec((tm, D_MODEL), lambda i: (i, 0))
    vec = pl.BlockSpec((1, D_MODEL), lambda i: (0, 0))
    return pl.pallas_call(
        body, grid=(S // tm,),
        in_specs=[row, pl.BlockSpec((D_MODEL, D_MODEL), lambda i: (0, 0)), vec, row, vec, vec],
        out_specs=[row, pl.BlockSpec((tm, LANES), lambda i: (i, 0)), row],
        out_shape=[jax.ShapeDtypeStruct((S, D_MODEL), F32), jax.ShapeDtypeStruct((S, LANES), F32),
                   jax.ShapeDtypeStruct((S, D_MODEL), BF16)],
        name="mix_ln1", compiler_params=_cparams(("parallel",), 40))(*_hbm(mixin, w_o, b_o, h0, g1, b1))


def _gelu_parts(cz):
    cdf = 0.5 * (1.0 + lax.erf(cz * INV_SQRT2))
    return cdf, cz * cdf


def _ffn_conv_fwd(up, cw, cb):
    S = up.shape[0]

    def body(a_ref, g_ref, w_ref, cb_ref, o_ref, a_scr):
        _zero_pads(a_scr, S)
        for t in range(0, S, CHUNK):
            a_scr[PAD + t:PAD + t + CHUNK, :] = a_ref[t:t + CHUNK, :].astype(F32)
        w0, w1, w2 = w_ref[0:1, :], w_ref[1:2, :], w_ref[2:3, :]
        for t in range(0, S, CHUNK):
            am, a0, ap = _shifted(a_scr, t)
            _, gel = _gelu_parts(w0 * am + w1 * a0 + w2 * ap + cb_ref[...])
            o_ref[t:t + CHUNK, :] = (gel * g_ref[t:t + CHUNK, :].astype(F32)).astype(BF16)

    return pl.pallas_call(
        body, grid=(D_FF // SLAB,),
        in_specs=[_slab_spec(S, 0), _slab_spec(S, D_FF), pl.BlockSpec((3, SLAB), lambda j: (0, j)),
                  pl.BlockSpec((1, SLAB), lambda j: (0, j))],
        out_specs=pl.BlockSpec((S, SLAB), lambda j: (0, j)),
        out_shape=jax.ShapeDtypeStruct((S, D_FF), BF16),
        scratch_shapes=[pltpu.VMEM((S + 2 * PAD, SLAB), F32)],
        name="ffn_conv_fwd", compiler_params=_cparams(("parallel",), 40))(*_hbm(up, up, cw, cb))


def _down_ln2_loss(f, w_down, b_down, xhat1, g1, b1, g2, b2, target, *, tm=512):
    S = f.shape[0]

    def body(f_ref, wd_ref, bd_ref, xh1_ref, g1_ref, b1_ref, g2_ref, b2_ref, t_ref, dz_ref, dzb_ref, st_ref):
        h1 = xh1_ref[...] * g1_ref[...] + b1_ref[...]
        z = ALPHA * h1 + _dot(f_ref[...], wd_ref[...]) + bd_ref[...]
        xhat, rstd = _ln_stats(z)
        err = xhat * g2_ref[...] + b2_ref[...] - t_ref[...]
        loss = (0.5 / D_MODEL) * jnp.sum(jnp.sum(err * err, axis=1, keepdims=True), axis=0, keepdims=True)
        dh2 = err * (1.0 / D_MODEL)
        dz = _ln_bwd(dh2, xhat, rstd, g2_ref[...])
        dz_ref[...] = dz
        dzb_ref[...] = dz.astype(BF16)
        upd = _rows8([jnp.sum(dh2 * xhat, axis=0, keepdims=True), jnp.sum(dh2, axis=0, keepdims=True),
                      jnp.broadcast_to(loss, (1, D_MODEL)), jnp.sum(dz, axis=0, keepdims=True)], D_MODEL)

        @pl.when(pl.program_id(0) == 0)
        def _():
            st_ref[...] = upd

        @pl.when(pl.program_id(0) != 0)
        def _():
            st_ref[...] += upd

    row = pl.BlockSpec((tm, D_MODEL), lambda i: (i, 0))
    vec = pl.BlockSpec((1, D_MODEL), lambda i: (0, 0))
    return pl.pallas_call(
        body, grid=(S // tm,),
        in_specs=[pl.BlockSpec((tm, D_FF), lambda i: (i, 0)), _resident((D_FF, D_MODEL)),
                  vec, row, vec, vec, vec, vec, row],
        out_specs=[row, row, pl.BlockSpec((SUBLANES, D_MODEL), lambda i: (0, 0))],
        out_shape=[jax.ShapeDtypeStruct((S, D_MODEL), F32), jax.ShapeDtypeStruct((S, D_MODEL), BF16),
                   jax.ShapeDtypeStruct((SUBLANES, D_MODEL), F32)],
        name="down_ln2_loss", compiler_params=_cparams(("arbitrary",), 56))(
            *_hbm(f, w_down, b_down, xhat1, g1, b1, g2, b2, target))


def _ffn_conv_bwd(up, df, cw, cb):
    S = up.shape[0]

    def body(a_ref, g_ref, df_ref, w_ref, cb_ref, dup_ref, sm_ref, a_scr, d_scr):
        _zero_pads(a_scr, S)
        _zero_pads(d_scr, S)
        for t in range(0, S, CHUNK):
            a_scr[PAD + t:PAD + t + CHUNK, :] = a_ref[t:t + CHUNK, :].astype(F32)
        w0, w1, w2 = w_ref[0:1, :], w_ref[1:2, :], w_ref[2:3, :]
        zero = jnp.zeros((1, SLAB), F32)
        s_dg, s_dcz, s_w0, s_w1, s_w2 = zero, zero, zero, zero, zero
        for t in range(0, S, CHUNK):
            am, a0, ap = _shifted(a_scr, t)
            cz = w0 * am + w1 * a0 + w2 * ap + cb_ref[...]
            cdf, gel = _gelu_parts(cz)
            dfv = df_ref[t:t + CHUNK, :].astype(F32)
            dgte = dfv * gel
            dcz = dfv * g_ref[t:t + CHUNK, :].astype(F32) * (cdf + (cz * INV_SQRT_2PI) * jnp.exp2(cz * cz * (-0.5 * LOG2_E)))
            dup_ref[1, t:t + CHUNK, :] = dgte.astype(BF16)
            d_scr[PAD + t:PAD + t + CHUNK, :] = dcz
            s_dg = s_dg + jnp.sum(dgte, axis=0, keepdims=True)
            s_dcz = s_dcz + jnp.sum(dcz, axis=0, keepdims=True)
            s_w0 = s_w0 + jnp.sum(dcz * am, axis=0, keepdims=True)
            s_w1 = s_w1 + jnp.sum(dcz * a0, axis=0, keepdims=True)
            s_w2 = s_w2 + jnp.sum(dcz * ap, axis=0, keepdims=True)
        s_da = zero
        for t in range(0, S, CHUNK):
            dm, d0, dp = _shifted(d_scr, t)
            da = w0 * dp + w1 * d0 + w2 * dm
            dup_ref[0, t:t + CHUNK, :] = da.astype(BF16)
            s_da = s_da + jnp.sum(da, axis=0, keepdims=True)
        sm_ref[...] = _rows8([s_da, s_dg, s_dcz, s_w0, s_w1, s_w2], SLAB)

    return pl.pallas_call(
        body, grid=(D_FF // SLAB,),
        in_specs=[_slab_spec(S, 0), _slab_spec(S, D_FF), pl.BlockSpec((S, SLAB), lambda j: (0, j)),
                  pl.BlockSpec((3, SLAB), lambda j: (0, j)), pl.BlockSpec((1, SLAB), lambda j: (0, j))],
        out_specs=[pl.BlockSpec((2, S, SLAB), lambda j: (0, 0, j)), pl.BlockSpec((SUBLANES, SLAB), lambda j: (0, j))],
        out_shape=[jax.ShapeDtypeStruct((2, S, D_FF), BF16), jax.ShapeDtypeStruct((SUBLANES, D_FF), F32)],
        scratch_shapes=[pltpu.VMEM((S + 2 * PAD, SLAB), F32)] * 2,
        name="ffn_conv_bwd", compiler_params=_cparams(("parallel",), 48))(*_hbm(up, up, df, cw, cb))


def _up_bwd_ln1(dup, w_up3, dz2, xhat1, rstd1, g1, *, tm=512):
    S = dz2.shape[0]
    ns, _, tk = w_up3.shape
    per_plane = D_FF // tk

    def body(du_ref, w_ref, dz2_ref, xh_ref, rs_ref, g_ref, dz_ref, dzb_ref, st_ref):
        dh = ALPHA * dz2_ref[...]
        for plane in range(ns // per_plane):
            w = jnp.concatenate([w_ref[plane * per_plane + k] for k in range(per_plane)], axis=1)
            dh = dh + _dot_nt(du_ref[plane], w)
        xhat = xh_ref[...]
        dz = _ln_bwd(dh, xhat, rs_ref[:, 0:1], g_ref[...])
        dz_ref[...] = dz
        dzb_ref[...] = dz.astype(BF16)
        upd = _rows8([jnp.sum(dh * xhat, axis=0, keepdims=True), jnp.sum(dh, axis=0, keepdims=True),
                      jnp.sum(dz, axis=0, keepdims=True)], D_MODEL)

        @pl.when(pl.program_id(0) == 0)
        def _():
            st_ref[...] = upd

        @pl.when(pl.program_id(0) != 0)
        def _():
            st_ref[...] += upd

    row = pl.BlockSpec((tm, D_MODEL), lambda i: (i, 0))
    return pl.pallas_call(
        body, grid=(S // tm,),
        in_specs=[pl.BlockSpec((dup.shape[0], tm, D_FF), lambda i: (0, i, 0)), _resident(w_up3.shape),
                  row, row, pl.BlockSpec((tm, LANES), lambda i: (i, 0)), pl.BlockSpec((1, D_MODEL), lambda i: (0, 0))],
        out_specs=[row, row, pl.BlockSpec((SUBLANES, D_MODEL), lambda i: (0, 0))],
        out_shape=[jax.ShapeDtypeStruct((S, D_MODEL), F32), jax.ShapeDtypeStruct((S, D_MODEL), BF16),
                   jax.ShapeDtypeStruct((SUBLANES, D_MODEL), F32)],
        name="up_bwd_ln1", compiler_params=_cparams(("arbitrary",), 56))(*_hbm(dup, w_up3, dz2, xhat1, rstd1, g1))


def _mix_bwd(dz1b, w_o, proj, yab, *, tm=512):
    S = dz1b.shape[0]

    def body(dz_ref, wo_ref, ga_ref, gb_ref, y_ref, dy_ref, dg_ref):
        dmx = _dot_nt(dz_ref[...], wo_ref[...])
        for k, gt_ref in enumerate((ga_ref, gb_ref)):
            sl = slice(k * D_MODEL, (k + 1) * D_MODEL)
            sg = jax.nn.sigmoid(gt_ref[...].astype(F32))
            dy_ref[:, sl] = (dmx * sg).astype(BF16)
            dg_ref[k] = (dmx * y_ref[:, sl].astype(F32) * sg * (1.0 - sg)).astype(BF16)

    row = pl.BlockSpec((tm, D_MODEL), lambda i: (i, 0))
    wide = pl.BlockSpec((tm, 2 * D_MODEL), lambda i: (i, 0))
    return pl.pallas_call(
        body, grid=(S // tm,),
        in_specs=[row, _resident(w_o.shape), pl.BlockSpec((tm, D_MODEL), lambda i: (i, P_GA // D_MODEL)),
                  pl.BlockSpec((tm, D_MODEL), lambda i: (i, P_GB // D_MODEL)), wide],
        out_specs=[wide, pl.BlockSpec((2, tm, D_MODEL), lambda i: (0, i, 0))],
        out_shape=[jax.ShapeDtypeStruct((S, 2 * D_MODEL), BF16), jax.ShapeDtypeStruct((2, S, D_MODEL), BF16)],
        name="mix_bwd", compiler_params=_cparams(("parallel",), 40))(*_hbm(dz1b, w_o, proj, proj, yab))


def _conv_gate_bwd(proj, dya_in, conv_w):
    S = proj.shape[0]

    def body(b_ref, c_ref, h_ref, dy_ref, w_ref, o_ref, sm_ref, u_scr, d_scr):
        _zero_pads(u_scr, S)
        _zero_pads(d_scr, S)
        for t in range(0, S, CHUNK):
            u_scr[PAD + t:PAD + t + CHUNK, :] = c_ref[t:t + CHUNK, :].astype(F32) * h_ref[t:t + CHUNK, :].astype(F32)
        w0, w1, w2 = w_ref[0:1, :], w_ref[1:2, :], w_ref[2:3, :]
        zero = jnp.zeros((1, SLAB), F32)
        s_w0, s_w1, s_w2 = zero, zero, zero
        for t in range(0, S, CHUNK):
            um, u0, up = _shifted(u_scr, t)
            dy = dy_ref[t:t + CHUNK, :].astype(F32)
            o_ref[0, t:t + CHUNK, :] = (dy * (w0 * um + w1 * u0 + w2 * up)).astype(BF16)
            dcv = dy * b_ref[t:t + CHUNK, :].astype(F32)
            d_scr[PAD + t:PAD + t + CHUNK, :] = dcv
            s_w0 = s_w0 + jnp.sum(dcv * um, axis=0, keepdims=True)
            s_w1 = s_w1 + jnp.sum(dcv * u0, axis=0, keepdims=True)
            s_w2 = s_w2 + jnp.sum(dcv * up, axis=0, keepdims=True)
        for t in range(0, S, CHUNK):
            dm, d0, dp = _shifted(d_scr, t)
            du = w0 * dp + w1 * d0 + w2 * dm
            o_ref[1, t:t + CHUNK, :] = (du * h_ref[t:t + CHUNK, :].astype(F32)).astype(BF16)
            o_ref[2, t:t + CHUNK, :] = (du * c_ref[t:t + CHUNK, :].astype(F32)).astype(BF16)
        sm_ref[...] = _rows8([s_w0, s_w1, s_w2], SLAB)

    return pl.pallas_call(
        body, grid=(D_CONV // SLAB,),
        in_specs=[_slab_spec(S, P_B), _slab_spec(S, P_C), _slab_spec(S, P_H),
                  pl.BlockSpec((S, SLAB), lambda j: (0, j)), pl.BlockSpec((3, SLAB), lambda j: (0, j))],
        out_specs=[pl.BlockSpec((3, S, SLAB), lambda j: (0, 0, j)), pl.BlockSpec((SUBLANES, SLAB), lambda j: (0, j))],
        out_shape=[jax.ShapeDtypeStruct((3, S, D_CONV), BF16), jax.ShapeDtypeStruct((SUBLANES, D_CONV), F32)],
        scratch_shapes=[pltpu.VMEM((S + 2 * PAD, SLAB), F32)] * 2,
        name="conv_gate_bwd", compiler_params=_cparams(("parallel",), 48))(*_hbm(proj, proj, proj, dya_in, conv_w))


def _comb_bwd(dyab, w_b, comb, lse_tot, *, tm=512):
    S = comb.shape[0]
    widths, dtypes = (GROUP_W, LANES, LANES), (BF16, F32, F32)

    def body(dy_ref, wb_ref, c_ref, lt_ref, e_ref, *rest):
        outs, scr = rest[:3 * N_GROUPS], rest[3 * N_GROUPS:]
        dcb = _dot_nt(dy_ref[...], wb_ref[...]).astype(BF16)
        dc = dcb.astype(F32)
        delta = lax.dot_general(dc * c_ref[...], e_ref[...], (((1,), (1,)), ((), ())),
                                preferred_element_type=F32, precision=lax.Precision.HIGH)
        for k, (val, dtype) in enumerate(zip((dc, lt_ref[...], delta), dtypes)):
            outs[k][0] = val.astype(dtype)
            _to_residue(val, [outs[3 * (1 + j) + k] for j in range(len(DILS))], DILS, tm, dtype,
                        scr[:val.shape[1] // LANES])

    out_specs, out_shape = [], []
    for _, d in GROUPS:
        out_specs += [_res_spec(d, tm, w) for w in widths]
        out_shape += [jax.ShapeDtypeStruct((d, S // d, w), t) for w, t in zip(widths, dtypes)]
    res = pl.pallas_call(
        body, grid=(S // tm,),
        in_specs=[pl.BlockSpec((tm, D_MODEL), lambda i: (i, 1)), _resident(w_b.shape),
                  pl.BlockSpec((tm, GROUP_W), lambda i: (i, 0)), pl.BlockSpec((tm, LANES), lambda i: (i, 0)),
                  _resident((LANES, GROUP_W))],
        out_specs=out_specs, out_shape=out_shape, scratch_shapes=_lane_scratch(tm, GROUP_W),
        name="comb_bwd", compiler_params=_cparams(("parallel",), 32))(*_hbm(dyab, w_b, comb, lse_tot, _expand_heads()))
    return [tuple(res[3 * g:3 * g + 3]) for g in range(N_GROUPS)]


def _attn_bwd(qkv, col0, g, dcomb, lse_tot, delta):
    dil, sub, _ = qkv.shape
    nb = sub // TB
    heads = HEADS_PER_GROUP

    def body(q_ref, kp, kc, kn, vp, vc, vn, do_ref, lse_ref, dl_ref, bias_ref, dq_ref, dk_ref, dv_ref,
             ak, av, dqt_scr, s_scr, dp_scr, ds_scr, p_scr):
        i = pl.program_id(1)

        @pl.when(i == 0)
        def _():
            ak[...] = jnp.zeros_like(ak)
            av[...] = jnp.zeros_like(av)

        @pl.when(i < nb)
        def _():
            kext = _ext_window(kp, kc, kn)
            vext = _ext_window(vp, vc, vn)
            q = q_ref[...] * ATT_SCALE
            do = do_ref[...]
            lse_t, dl_t = lse_ref[...].T, dl_ref[...].T
            for b in range(SUBS):
                rows = slice(b * TQ, (b + 1) * TQ)
                kwin, vwin = kext[b * TQ:(b + 2) * TQ, :], vext[b * TQ:(b + 2) * TQ, :]
                for h in range(heads):
                    s_scr[b * heads + h] = _dot_nt(_pair(kwin, h), _own_lanes(_pair(q[rows], h), h))
                    dp_scr[b * heads + h] = _dot_nt(_pair(vwin, h), _own_lanes(_pair(do[rows], h), h))
            for b in range(SUBS):
                cols = slice(b * TQ, (b + 1) * TQ)
                block = i * SUBS + b
                bias = bias_ref[jnp.where(block == 0, 1, 0) + jnp.where(block == nb * SUBS - 1, 2, 0)]
                for h in range(heads):
                    k = b * heads + h
                    p = jnp.exp(s_scr[k] + _slope(g, h) * bias - lse_t[h:h + 1, cols])
                    ds_scr[k] = (p * (dp_scr[k] - dl_t[h:h + 1, cols])).astype(BF16)
                    p_scr[k] = p.astype(BF16)
            for b in range(SUBS):
                kwin = kext[b * TQ:(b + 2) * TQ, :]
                for h in range(heads):
                    dqt_scr[h * HEAD_DIM:(h + 1) * HEAD_DIM, b * TQ:(b + 1) * TQ] = _own_rows(
                        _dot_tn(_pair(kwin, h), ds_scr[b * heads + h]), h)
            for b in range(SUBS):
                rows = slice(b * TQ, (b + 1) * TQ)
                acc_rows = slice(TB - RADIUS + b * TQ, TB - RADIUS + (b + 2) * TQ)
                for h in range(0, heads, 2):
                    cols = slice(h * HEAD_DIM, (h + 2) * HEAD_DIM)
                    k = b * heads + h
                    q2 = jnp.concatenate([_own_lanes(_pair(q[rows], h), h), _own_lanes(_pair(q[rows], h), h + 1)], axis=0)
                    do2 = jnp.concatenate([_own_lanes(_pair(do[rows], h), h), _own_lanes(_pair(do[rows], h), h + 1)],
                                          axis=0)
                    ak[acc_rows, cols] += _dot(jnp.concatenate([ds_scr[k], ds_scr[k + 1]], axis=1), q2)
                    av[acc_rows, cols] += _dot(jnp.concatenate([p_scr[k], p_scr[k + 1]], axis=1), do2)
            dq_ref[...] = (dqt_scr[...].T * ATT_SCALE).astype(BF16)

        if nb == 1:
            dk_ref[...] = ak[TB:2 * TB, :].astype(BF16)
            dv_ref[...] = av[TB:2 * TB, :].astype(BF16)
        else:
            dk_ref[...] = ak[0:TB, :].astype(BF16)
            dv_ref[...] = av[0:TB, :].astype(BF16)
            used = 2 * TB + RADIUS
            for acc in (ak, av):
                acc[0:used - TB, :] = acc[TB:used, :]
                acc[used - TB:used, :] = jnp.zeros((TB, GROUP_W), F32)

    def spec(col, shift):
        return pl.BlockSpec((None, TB, GROUP_W), lambda r, i: (r, jnp.clip(i + shift, 0, nb - 1), col))

    tok = pl.BlockSpec((None, TB, GROUP_W), lambda r, i: (r, jnp.minimum(i, nb - 1), 0))
    stat = pl.BlockSpec((None, TB, LANES), lambda r, i: (r, jnp.minimum(i, nb - 1), 0))
    dkv_spec = tok if nb == 1 else pl.BlockSpec((None, TB, GROUP_W), lambda r, i: (r, jnp.maximum(i - 1, 0), 0))
    return pl.pallas_call(
        body, grid=(dil, nb + (nb > 1)),
        in_specs=[spec(col0, 0), spec(col0 + 1, -1), spec(col0 + 1, 0), spec(col0 + 1, 1),
                  spec(col0 + 2, -1), spec(col0 + 2, 0), spec(col0 + 2, 1), tok, stat, stat,
                  pl.BlockSpec((4, 2 * TQ, TQ), lambda r, i: (0, 0, 0))],
        out_specs=[tok, dkv_spec, dkv_spec], out_shape=[jax.ShapeDtypeStruct((dil, sub, GROUP_W), BF16)] * 3,
        scratch_shapes=[pltpu.VMEM((3 * TB, GROUP_W), F32)] * 2 + [pltpu.VMEM((GROUP_W, TB), F32)]
        + [pltpu.VMEM((SUBS * heads, 2 * TQ, TQ), F32)] * 2 + [pltpu.VMEM((SUBS * heads, 2 * TQ, TQ), BF16)] * 2,
        name=f"attn_bwd_g{g}", compiler_params=_cparams(("arbitrary", "arbitrary"), 40))(
            *_hbm(*([qkv] * 7), dcomb, lse_tot, delta, _attn_bias_table(g)))


def _in_bwd_ln0(dgated, dqkv, w_nat, w_dil, dz1, x, g0, *, tm=256):
    S = x.shape[0]
    n_gated, n_in = len(dgated), 3 * N_GROUPS

    def body(*refs):
        g_refs, d_refs = refs[:n_gated], refs[n_gated:n_gated + n_in]
        wn_ref, *wd_refs = refs[n_gated + n_in:n_gated + n_in + N_GROUPS]
        dz_ref, x_ref, g_ref, gx_ref, st_ref, *tmp_ref = refs[n_gated + n_in + N_GROUPS:]
        dh = ALPHA * dz_ref[...]
        col = 0
        for ref in g_refs:
            for k in range(ref.shape[0]):
                dh = dh + _dot_nt(ref[k], wn_ref[:, col:col + D_MODEL])
                col += D_MODEL
        for g, (_, d) in enumerate(GROUPS):
            rows = [jnp.concatenate([d_refs[3 * g + k][r] for k in range(3)], axis=1) for r in range(d)]
            w = wn_ref[:, col:col + QKV_W] if d == 1 else wd_refs[g - 1][...]
            res = _dot_nt(jnp.concatenate(rows, axis=0), w)
            if d == 1:
                dh = dh + res
            else:
                n = tm // d
                dh = dh + _from_residue(lambda r: res[r * n:(r + 1) * n, :], d, tm, tmp_ref)
        xhat, rstd = _ln_stats(x_ref[...])
        gx_ref[...] = _ln_bwd(dh, xhat, rstd, g_ref[...])
        upd = _rows8([jnp.sum(dh * xhat, axis=0, keepdims=True), jnp.sum(dh, axis=0, keepdims=True)], D_MODEL)

        @pl.when(pl.program_id(0) == 0)
        def _():
            st_ref[...] = upd

        @pl.when(pl.program_id(0) != 0)
        def _():
            st_ref[...] += upd

    row = pl.BlockSpec((tm, D_MODEL), lambda i: (i, 0))
    g_specs = [pl.BlockSpec((a.shape[0], tm, D_MODEL), lambda i: (0, i, 0)) for a in dgated]
    d_specs = []
    for _, d in GROUPS:
        d_specs += [_res_spec(d, tm, GROUP_W)] * 3
    operands = list(dgated) + [a for grp in dqkv for a in grp] + [w_nat] + list(w_dil) + [dz1, x, g0]
    return pl.pallas_call(
        body, grid=(S // tm,),
        in_specs=g_specs + d_specs + [_resident(w_nat.shape)] + [_resident(w.shape) for w in w_dil]
        + [row, row, pl.BlockSpec((1, D_MODEL), lambda i: (0, 0))],
        out_specs=[row, pl.BlockSpec((SUBLANES, D_MODEL), lambda i: (0, 0))],
        out_shape=[jax.ShapeDtypeStruct((S, D_MODEL), F32), jax.ShapeDtypeStruct((SUBLANES, D_MODEL), F32)],
        scratch_shapes=_lane_scratch(tm, D_MODEL),
        name="in_bwd_ln0", compiler_params=_cparams(("arbitrary",), 52))(*_hbm(*operands))


HBM_SPEC = pl.BlockSpec(memory_space=pltpu.HBM)


def _place():
    x, y, c = lax.axis_index("x"), lax.axis_index("y"), lax.axis_index("c")
    chips = [(1 - x, y), (x, 1 - y), (1 - x, 1 - y)]
    return x, y, c, chips


def _allgather_shards(shards, after, *, name, collective_id):
    n = len(shards)
    per = 6

    def body(*refs):
        ins, outs = refs[:n], refs[n + len(after):2 * n + len(after)]
        send_sems, recv_sems, loc_sems = refs[2 * n + len(after):]
        x, y, c, chips = _place()
        me = 2 * x + y
        sib = (x, y, 1 - c)
        peers = [sib] + [(px, py, c) for px, py in chips]
        barrier = pltpu.get_barrier_semaphore()
        for peer in peers:
            pl.semaphore_signal(barrier, inc=1, device_id=peer, device_id_type=MESH)
        pl.semaphore_wait(barrier, len(peers))

        def rcopy(w, k, src, dst, to):
            return pltpu.make_async_remote_copy(src_ref=src, dst_ref=dst, send_sem=send_sems.at[per * w + k],
                                                recv_sem=recv_sems.at[per * w + k], device_id=to, device_id_type=MESH)

        split = [s.shape[0] == N_CORES for s in shards]
        half = lambda w: c if split[w] else 0
        local, sends = [], []
        for w in range(n):
            cp = pltpu.make_async_copy(ins[w], outs[w].at[me], loc_sems.at[w])
            cp.start()
            local.append(cp)
            for j, (px, py) in enumerate(chips):
                cp = rcopy(w, j, ins[w].at[half(w)], outs[w].at[me, half(w)], (px, py, c))
                cp.start()
                sends.append(cp)
        for w in range(n):
            for j, (px, py) in enumerate(chips):
                slot = outs[w].at[2 * px + py, half(w)]
                rcopy(w, j, slot, slot, (px, py, c)).wait_recv()
                if split[w]:
                    cp = rcopy(w, 3 + j, slot, slot, sib)
                    cp.start()
                    sends.append(cp)
        for w in range(n):
            if split[w]:
                for j, (px, py) in enumerate(chips):
                    slot = outs[w].at[2 * px + py, 1 - c]
                    rcopy(w, 3 + j, slot, slot, sib).wait_recv()
        for cp in sends:
            cp.wait_send()
        for cp in local:
            cp.wait()

    return pl.kernel(
        body, out_type=[jax.ShapeDtypeStruct((N_CHIPS,) + s.shape, s.dtype) for s in shards],
        mesh=plsc.ScalarSubcoreMesh(axis_name="sequencer", num_cores=1),
        scratch_types=[pltpu.SemaphoreType.DMA((per * n,)), pltpu.SemaphoreType.DMA((per * n,)),
                       pltpu.SemaphoreType.DMA((n,))],
        name=name, compiler_params=pltpu.CompilerParams(collective_id=collective_id))(*shards, *after)


def _exchange_grads(grads, *, name, collective_id):
    n = len(grads)
    per = 7

    def body(*refs):
        ins, outs = refs[:n], refs[n:2 * n]
        send_sems, recv_sems, loc_sems = refs[2 * n:]
        x, y, c, chips = _place()
        me = 2 * x + y
        sib = (x, y, 1 - c)
        peers = [sib] + [(px, py, c) for px, py in chips]
        barrier = pltpu.get_barrier_semaphore()
        for peer in peers:
            pl.semaphore_signal(barrier, inc=1, device_id=peer, device_id_type=MESH)
        pl.semaphore_wait(barrier, len(peers))

        def rcopy(w, k, src, dst, to):
            return pltpu.make_async_remote_copy(src_ref=src, dst_ref=dst, send_sem=send_sems.at[per * w + k],
                                                recv_sem=recv_sems.at[per * w + k], device_id=to, device_id_type=MESH)

        local, sends = [], []
        for w in range(n):
            cp = pltpu.make_async_copy(ins[w].at[me], outs[w].at[c, me], loc_sems.at[w])
            cp.start()
            local.append(cp)
            cp = rcopy(w, 0, ins[w].at[me], outs[w].at[c, me], sib)
            cp.start()
            sends.append(cp)
            for j, (px, py) in enumerate(chips):
                cp = rcopy(w, 1 + j, ins[w].at[2 * px + py], outs[w].at[c, me], (px, py, c))
                cp.start()
                sends.append(cp)
        for w in range(n):
            for j, (px, py) in enumerate(chips):
                slot = outs[w].at[c, 2 * px + py]
                rcopy(w, 1 + j, slot, slot, (px, py, c)).wait_recv()
                cp = rcopy(w, 4 + j, slot, slot, sib)
                cp.start()
                sends.append(cp)
        for w in range(n):
            slot = outs[w].at[1 - c, me]
            rcopy(w, 0, slot, slot, sib).wait_recv()
            for j, (px, py) in enumerate(chips):
                slot = outs[w].at[1 - c, 2 * px + py]
                rcopy(w, 4 + j, slot, slot, sib).wait_recv()
        for cp in sends:
            cp.wait_send()
        for cp in local:
            cp.wait()

    return pl.kernel(
        body, out_type=[jax.ShapeDtypeStruct((N_CORES,) + g.shape, g.dtype) for g in grads],
        mesh=plsc.ScalarSubcoreMesh(axis_name="sequencer", num_cores=1),
        scratch_types=[pltpu.SemaphoreType.DMA((per * n,)), pltpu.SemaphoreType.DMA((per * n,)),
                       pltpu.SemaphoreType.DMA((n,))],
        name=name, compiler_params=pltpu.CompilerParams(collective_id=collective_id))(*grads)


def _allgather_small(vec, after):
    def body(v_ref, _, o_ref, send_sems, recv_sems, loc_sem):
        x, y, c = lax.axis_index("x"), lax.axis_index("y"), lax.axis_index("c")
        me = 4 * x + 2 * y + c

        def peer(k):
            flip = lambda v, bit: 1 - v if (k >> bit) & 1 else v
            return flip(x, 2), flip(y, 1), flip(c, 0)

        loc = pltpu.make_async_copy(v_ref, o_ref.at[me], loc_sem)
        loc.start()
        sends = []
        for k in range(1, N_DEV):
            cp = pltpu.make_async_remote_copy(src_ref=v_ref, dst_ref=o_ref.at[me], send_sem=send_sems.at[k - 1],
                                              recv_sem=recv_sems.at[k - 1], device_id=peer(k), device_id_type=MESH)
            cp.start()
            sends.append(cp)
        for k in range(1, N_DEV):
            px, py, pc = peer(k)
            pltpu.make_async_remote_copy(src_ref=v_ref, dst_ref=o_ref.at[4 * px + 2 * py + pc],
                                         send_sem=send_sems.at[k - 1], recv_sem=recv_sems.at[k - 1],
                                         device_id=(px, py, pc), device_id_type=MESH).wait_recv()
        for cp in sends:
            cp.wait_send()
        loc.wait()

    return pl.pallas_call(
        body, in_specs=[HBM_SPEC, HBM_SPEC], out_specs=HBM_SPEC,
        out_shape=jax.ShapeDtypeStruct((N_DEV,) + vec.shape, vec.dtype),
        scratch_shapes=[pltpu.SemaphoreType.DMA((N_DEV - 1,)), pltpu.SemaphoreType.DMA((N_DEV - 1,)),
                        pltpu.SemaphoreType.DMA],
        name="allgather_small")(vec, after)


def _adamw(w, g, m, v):
    m = ADAM_B1 * m + (1.0 - ADAM_B1) * g
    v = ADAM_B2 * v + (1.0 - ADAM_B2) * (g * g)
    m_hat = m / (1.0 - ADAM_B1 ** ADAM_STEP)
    v_hat = v / (1.0 - ADAM_B2 ** ADAM_STEP)
    delta = -ADAM_LR * (m_hat / (jnp.sqrt(v_hat) + ADAM_EPS) + ADAM_WD * w)
    return delta, m, v


def _reduce_adamw(parts, w, m, v, *, tr, name):
    R, C = w.shape

    def body(p_ref, w_ref, m_ref, v_ref, g_ref, d_ref, nm_ref, nv_ref):
        def core_sum(cc):
            s = p_ref[cc, 0].astype(F32)
            for k in range(1, N_CHIPS):
                s = s + p_ref[cc, k].astype(F32)
            return s

        g = core_sum(0) + core_sum(1)
        delta, nm, nv = _adamw(w_ref[...], g, m_ref[...], v_ref[...])
        g_ref[...] = g
        d_ref[...] = delta
        nm_ref[...] = nm
        nv_ref[...] = nv

    blk = pl.BlockSpec((tr, C), lambda i: (i, 0))
    return pl.pallas_call(
        body, grid=(R // tr,),
        in_specs=[pl.BlockSpec((N_CORES, N_CHIPS, tr, C), lambda i: (0, 0, i, 0)), blk, blk, blk],
        out_specs=[blk] * 4, out_shape=[jax.ShapeDtypeStruct((R, C), F32)] * 4,
        name=name, compiler_params=_cparams(("parallel",), 40))(*_hbm(parts, w, m, v))


def _reduce_adamw_vectors(allv, offs, ws, ms, vs):
    n = len(ws)

    def body(a_ref, *refs):
        w_refs, m_refs, v_refs = refs[:n], refs[n:2 * n], refs[2 * n:3 * n]
        tot_ref, outs = refs[3 * n], refs[3 * n + 1:]
        s = a_ref[0]
        for d in range(1, N_DEV):
            s = s + a_ref[d]
        tot_ref[...] = s
        for k in range(n):
            g = s[:, offs[k]:offs[k] + w_refs[k].shape[1]]
            delta, nm, nv = _adamw(w_refs[k][...], g, m_refs[k][...], v_refs[k][...])
            for ref, val in zip(outs[4 * k:4 * k + 4], (g, delta, nm, nv)):
                ref[...] = val

    out_shape = [jax.ShapeDtypeStruct(allv.shape[1:], F32)]
    for w in ws:
        out_shape += [jax.ShapeDtypeStruct(w.shape, F32)] * 4
    res = pl.pallas_call(body, out_shape=out_shape, name="reduce_adamw_vectors",
                         compiler_params=_cparams((), 40))(allv, *ws, *ms, *vs)
    return res[0], [tuple(res[1 + 4 * k:5 + 4 * k]) for k in range(n)]


def _adamw_taps(ws, gs, ms, vs):
    n = len(ws)

    def body(*refs):
        outs = refs[4 * n:]
        for k in range(n):
            res = _adamw(refs[k][...], refs[n + k][...], refs[2 * n + k][...], refs[3 * n + k][...])
            for ref, val in zip(outs[3 * k:3 * k + 3], res):
                ref[...] = val

    out_shape = []
    for w in ws:
        out_shape += [jax.ShapeDtypeStruct(w.shape, F32)] * 3
    res = pl.pallas_call(body, out_shape=out_shape, name="adamw_taps")(*ws, *gs, *ms, *vs)
    return [tuple(res[3 * k:3 * k + 3]) for k in range(n)]


def _segments(widths, col, w):
    out, start = [], 0
    for k, wk in enumerate(widths):
        lo, hi = max(col, start), min(col + w, start + wk)
        if lo < hi:
            out.append((k, lo - start, hi - lo))
        start += wk
    return out


def _permute_w_in(w3, widths, tr=128):
    copies = [(s, c + done, k, off, wd) for s, c, pc, w in _col_runs()
              for done, (k, off, wd) in _offsets(_segments(widths, pc, w))]
    rows = w3.shape[1]

    def body(w_ref, *o_refs):
        for s, c, k, off, wd in copies:
            o_refs[k][:, off:off + wd] = w_ref[s, :, c:c + wd]

    return pl.pallas_call(
        body, grid=(rows // tr,), in_specs=[pl.BlockSpec((N_CHIPS, tr, w3.shape[2]), lambda i: (0, i, 0))],
        out_specs=[pl.BlockSpec((tr, wk), lambda i: (i, 0)) for wk in widths],
        out_shape=[jax.ShapeDtypeStruct((rows, wk), w3.dtype) for wk in widths],
        compiler_params=_cparams(("arbitrary",), 32), name="permute_w_in")(w3)


def _unpermute_w_in(pieces, tr=128):
    widths = [a.shape[2] for a, _ in pieces]
    copies = [(s, c + done, k, off, wd) for s, c, pc, w in _col_runs()
              for done, (k, off, wd) in _offsets(_segments(widths, pc, w))]
    rows, n = pieces[0][0].shape[1], N_IN // N_CHIPS

    def body(*refs):
        o_ref = refs[-1]
        for s, c, k, off, wd in copies:
            o_ref[s, :, c:c + wd] = refs[k][:, off:off + wd]

    return pl.pallas_call(
        body, grid=(rows // tr,),
        in_specs=[pl.BlockSpec((None, tr, a.shape[2]), lambda i, plane=plane: (plane, i, 0)) for a, plane in pieces],
        out_specs=pl.BlockSpec((N_CHIPS, tr, n), lambda i: (0, i, 0)),
        out_shape=jax.ShapeDtypeStruct((N_CHIPS, rows, n), pieces[0][0].dtype),
        compiler_params=_cparams(("arbitrary",), 32), name="unpermute_grad_w_in")(*[a for a, _ in pieces])


def _offsets(segments):
    out, done = [], 0
    for seg in segments:
        out.append((done, seg))
        done += seg[2]
    return out


def _pack(pieces):
    arrays, copies, offs, n = [], [], [], 0
    for runs in pieces:
        offs.append(n)
        for arr, r, c, w in runs:
            assert arr.ndim == 2 and arr.dtype == F32 and w % LANES == 0 and c % LANES == 0
            k = next((i for i, a in enumerate(arrays) if a is arr), None)
            if k is None:
                arrays.append(arr)
                k = len(arrays) - 1
            copies.append((k, r, c, w, n))
            n += w

    def body(*refs):
        o_ref = refs[-1]
        for k, r, c, w, dst in copies:
            o_ref[:, dst:dst + w] = refs[k][r:r + 1, c:c + w]

    return pl.pallas_call(body, out_shape=jax.ShapeDtypeStruct((1, n), F32), name="pack_small")(*arrays), offs


def _local_step(x, target, p, wfull, on_ready=lambda group: None, before_ln0=()):
    S = x.shape[0]
    dils = [d for _, d in GROUPS]

    h0, h0b, *h0_res = _ln0_fwd(x, p["ln0_g"], p["ln0_b"], before_ln0)
    h0_rows = [h0b] + [h.reshape(S, D_MODEL) for h in h0_res]

    if isinstance(wfull, dict):
        w_in3, pending = wfull["w_in"], None
    else:
        w_in3, launch_rest, assemble = wfull
        w_in3, h0b = lax.optimization_barrier((w_in3, h0b))
        pending = launch_rest(h0b)

    runs = _col_runs()
    b_blocks = p["b_in"].reshape(N_BLK, GROUP_W)
    n_tok = PERM.index(N_BLK - 1) + 1
    q0 = PERM[n_tok]
    assert PERM[n_tok:] == tuple(q0 + N_GROUPS * t + g for g in range(N_GROUPS) for t in range(3))
    assert PERM[:n_tok] == tuple(range(q0)) + tuple(range(q0 + 3 * N_GROUPS, N_BLK))
    b_qkv = b_blocks[q0:q0 + 3 * N_GROUPS].reshape(3, N_GROUPS, GROUP_W).transpose(1, 0, 2)
    b_perm = jnp.concatenate([b_blocks[:q0].reshape(-1), b_blocks[q0 + 3 * N_GROUPS:].reshape(-1),
                              b_qkv.reshape(-1)]).reshape(1, N_IN)
    b_nat = b_perm[:, :N_NAT]
    qkv_cols = [slice(P_Q0 + g * QKV_W, P_Q0 + (g + 1) * QKV_W) for g in range(N_GROUPS)]
    w_nat, *w_dil = _permute_w_in(w_in3, [N_NAT] + [QKV_W] * (N_GROUPS - 1))
    w_qkv = [None] + w_dil

    proj = _mm_nn(h0b, w_nat, b_nat, tm=512, tn=N_NAT // 2, out_dtype=BF16, name="proj")
    qkv = [proj[None]]
    for g in range(1, N_GROUPS):
        t = _mm_nn(h0_rows[g], w_qkv[g], b_perm[:, qkv_cols[g]], tm=512, tn=QKV_W, out_dtype=BF16, name=f"proj_qkv{g}")
        qkv.append(t.reshape(dils[g], S // dils[g], QKV_W))
    if pending is not None:
        pending, qkv = lax.optimization_barrier((pending, qkv))
        proj = qkv[0][0]
        wfull = assemble(pending)
    w_up3 = wfull["w_up"]
    w_a, w_o, w_down, w_b = wfull["w_a"], wfull["w_o"], wfull["w_down"], wfull["w_b"]
    conv_w, ffn_conv_w = wfull["conv_w"], wfull["ffn_conv_w"]
    col0 = [P_Q0 // GROUP_W] + [0] * (N_GROUPS - 1)
    ya_in = _conv_gate_fwd(proj, conv_w)
    att = [_attn_fwd(qkv[g], col0[g], g) for g in range(N_GROUPS)]
    comb, comb_b, lse_tot = _attn_combine([a[0] for a in att], [a[1] for a in att])
    yab, mixin = _branch_mix(ya_in, comb_b, w_a, w_b, proj)
    xhat1, rstd1, h1b = _mix_ln1(mixin, w_o, p["b_o"], h0, p["ln1_g"], p["ln1_b"])
    up = _mm_nn(h1b, w_up3, p["b_up"], tm=512, tn=2 * w_up3.shape[2], out_dtype=BF16, name="up")
    f = _ffn_conv_fwd(up, ffn_conv_w, p["ffn_conv_b"])
    dz2, dz2b, st2 = _down_ln2_loss(f, w_down, p["b_down"], xhat1, p["ln1_g"], p["ln1_b"],
                                    p["ln2_g"], p["ln2_b"], target)

    gw = {}
    gw["w_down"] = _mm_tn(f, dz2b, n_out=1, tn=D_MODEL, ts=1024, g_block=(1024, D_MODEL),
                          g_map=lambda j, s: (s, 0), name="grad_w_down").reshape(N_CHIPS, D_FF // N_CHIPS, D_MODEL)
    df = _mm_nt(dz2b, w_down, tm=512, name="df")
    dup, sm_ffn = _ffn_conv_bwd(up, df, ffn_conv_w, p["ffn_conv_b"])
    gw["w_up"] = _mm_tn(h1b, dup, n_out=dup.shape[0], tn=D_FF, ts=1024, g_block=(None, 1024, D_FF),
                        g_map=lambda j, s: (j, s, 0), split=D_FF // w_up3.shape[2], name="grad_w_up")
    exchanged = on_ready({n: gw[n] for n in ("w_down", "w_up")}) or {}
    dz1, dz1b, st1 = _up_bwd_ln1(dup, w_up3, dz2, xhat1, rstd1, p["ln1_g"])

    gw["w_o"] = _mm_tn(mixin, dz1b, n_out=1, tn=D_MODEL, ts=512, g_block=(512, D_MODEL),
                       g_map=lambda j, s: (s, 0), name="grad_w_o").reshape(N_CHIPS, D_MODEL // N_CHIPS, D_MODEL)
    dyab, dgab = _mix_bwd(dz1b, w_o, proj, yab)
    gw["w_a"] =_mm_tn(ya_in, dyab, n_out=1, tn=D_MODEL, ts=512, g_block=(512, D_MODEL),
                       g_map=lambda j, s: (s, 0), name="grad_w_a").reshape(N_CHIPS, D_CONV // N_CHIPS, D_MODEL)
    gw_b = _mm_tn(comb_b, dyab, n_out=1, tn=D_MODEL, ts=1024, g_block=(1024, D_MODEL),
                  g_map=lambda j, s: (s, 1), name="grad_w_b")
    gw["w_b"] = gw_b.reshape(GROUP_W, N_CHIPS, D_MODEL // N_CHIPS).transpose(1, 0, 2)
    exchanged_mix = on_ready({n: gw[n] for n in ("w_o", "w_a", "w_b")}) or {}
    dya_in = _mm_nt(dyab, w_a, tm=512, a_col=0, name="dya_in")
    exchanged, dya_in = lax.optimization_barrier((exchanged, dya_in))
    dbch, sm_conv = _conv_gate_bwd(proj, dya_in, conv_w)
    att_stats = _comb_bwd(dyab, w_b, comb, lse_tot)
    exchanged_mix, att_stats = lax.optimization_barrier((exchanged_mix, att_stats))
    exchanged.update(exchanged_mix)
    dqkv = [_attn_bwd(qkv[g], col0[g], g, *att_stats[g]) for g in range(N_GROUPS)]

    w_pieces, b_pieces, b_rows = [], [], []
    for nm, planes in (("bch", dbch), ("gab", dgab)):
        pw, pc = _mm_tn(h0b, planes, n_out=planes.shape[0], tn=D_MODEL, ts=1024, g_block=(None, 1024, D_MODEL),
                        g_map=lambda j, s: (j, s, 0), colsum=True, name="grad_w_in_" + nm)
        w_pieces.extend((pw, k) for k in range(planes.shape[0]))
        b_pieces.append(pc[0])
        b_rows.append(pc)
    for g in range(N_GROUPS):
        pw, pc = _mm_tn_cat(h0_rows[g], [a.reshape(S, GROUP_W) for a in dqkv[g]], ts=1024, name=f"grad_w_in_qkv{g}")
        w_pieces.append((pw[None], 0))
        b_pieces.append(pc[0])
        b_rows.append(pc)
    gw["w_in"] = _unpermute_w_in(w_pieces)
    exchanged.update(on_ready({"w_in": gw["w_in"]}) or {})
    db_blocks = jnp.concatenate(b_pieces).reshape(N_BLK, GROUP_W)
    grad_b_in = jnp.concatenate([db_blocks[b] for b in INV_PERM])

    grad_x, st0 = _in_bwd_ln0([dbch, dgab], dqkv, w_nat, w_qkv[1:], dz1, x, p["ln0_g"])

    small = {
        "loss": st2[2:3, 0:1],
        "ln0_g": st0[0], "ln0_b": st0[1], "b_in": grad_b_in, "conv_w": sm_conv[0:3],
        "b_o": st1[2], "ln1_g": st1[0], "ln1_b": st1[1],
        "b_up": jnp.concatenate([sm_ffn[0], sm_ffn[1]]), "ffn_conv_w": sm_ffn[3:6], "ffn_conv_b": sm_ffn[2],
        "b_down": st2[3], "ln2_g": st2[0], "ln2_b": st2[1],
    }
    row = lambda a, r: [(a, r, 0, a.shape[1])]
    starts = [sum(a.shape[1] for a in b_rows[:k]) for k in range(len(b_rows))]
    b_in_src = []
    for b in range(N_BLK):
        col = INV_PERM[b] * GROUP_W
        k = max(i for i, s0 in enumerate(starts) if s0 <= col)
        b_in_src.append((b_rows[k], 0, col - starts[k], GROUP_W))
    small["_src"] = {
        "loss": [(st2, 2, 0, LANES)],
        "ln0_g": row(st0, 0), "ln0_b": row(st0, 1), "b_in": b_in_src, "conv_w": sum((row(sm_conv, r) for r in range(3)), []),
        "b_o": row(st1, 2), "ln1_g": row(st1, 0), "ln1_b": row(st1, 1),
        "b_up": row(sm_ffn, 0) + row(sm_ffn, 1), "ffn_conv_w": sum((row(sm_ffn, r) for r in (3, 4, 5)), []),
        "ffn_conv_b": row(sm_ffn, 2), "b_down": row(st2, 3), "ln2_g": row(st2, 0), "ln2_b": row(st2, 1),
    }
    return grad_x, exchanged or gw, small


BIG =("w_in", "w_a", "w_b", "w_o", "w_up", "w_down")
CONV = ("conv_w", "ffn_conv_w")
VECS = ("ln0_g", "ln0_b", "b_in", "b_o", "ln1_g", "ln1_b", "b_up", "ffn_conv_b", "b_down", "ln2_g", "ln2_b")
ORDER = ("ln0_g", "ln0_b", "w_in", "b_in", "conv_w", "w_a", "w_b", "w_o", "b_o", "ln1_g", "ln1_b", "w_up", "b_up",
         "ffn_conv_w", "ffn_conv_b", "w_down", "b_down", "ln2_g", "ln2_b")
SMALL_ORDER = ("loss",) + VECS + CONV


def _step(x, target, W, Mo, Vo):
    x2, t2 = x[0], target[0]
    big2 = {n: W[n][0] for n in BIG}
    halves = lambda a: a.astype(BF16).reshape(N_CORES, a.shape[0] // N_CORES, a.shape[1])
    whole = lambda g: g.reshape(N_CHIPS, g.shape[1] * g.shape[2], g.shape[3])
    later = tuple(n for n in BIG if n != "w_in")
    w_in_halves = halves(big2["w_in"])
    first = _allgather_shards([w_in_halves], [], name="allgather_w_in", collective_id=1)

    def launch_rest(h0b):
        return _allgather_shards([halves(big2[n]) for n in later] + [W[n] for n in CONV], [h0b],
                                 name="allgather_rest", collective_id=2)

    def assemble(rest):
        gathered = {n: whole(g) for n, g in zip(later + CONV, rest)}
        return {
            "w_up": gathered["w_up"],
            "w_a": gathered["w_a"].reshape(D_CONV, D_MODEL), "w_o": gathered["w_o"].reshape(D_MODEL, D_MODEL),
            "w_down": gathered["w_down"].reshape(D_FF, D_MODEL),
            "w_b": gathered["w_b"].transpose(1, 0, 2).reshape(GROUP_W, D_MODEL),
            "conv_w": gathered["conv_w"].transpose(1, 0, 2).reshape(3, D_CONV),
            "ffn_conv_w": gathered["ffn_conv_w"].transpose(1, 0, 2).reshape(3, D_FF),
        }

    pvec = {n: W[n].reshape(1, -1) for n in VECS}

    exchange_ids = iter((3, 4, 5))

    def exchange(group):
        names = tuple(group)
        res = _exchange_grads([group[n] for n in names], name="exchange_" + "_".join(names),
                              collective_id=next(exchange_ids))
        return dict(zip(names, res))

    grad_x, parts, small = _local_step(x2, t2, pvec, (whole(first[0]), launch_rest, assemble), exchange,
                                       before_ln0=[w_in_halves])
    out = {}
    for n in BIG:
        tr = {"w_in": 128, "w_up": 128, "w_b": 128}.get(n, big2[n].shape[0] // 4)
        g, d, nm, nv = _reduce_adamw(parts[n], big2[n], Mo[n][0], Vo[n][0], tr=tr, name="adamw_" + n)
        out[n] = tuple(a[None] for a in (g, d, nm, nv))

    vec, offs = _pack([small["_src"][n] for n in SMALL_ORDER])
    off = dict(zip(SMALL_ORDER, offs))
    row = lambda a: a.reshape(1, -1)
    allv = _allgather_small(vec, parts["w_in"])
    tot, vec_out = _reduce_adamw_vectors(allv, [off[n] for n in VECS], [row(W[n]) for n in VECS],
                                         [row(Mo[n]) for n in VECS], [row(Vo[n]) for n in VECS])
    for n, res in zip(VECS, vec_out):
        out[n] = tuple(a.reshape(W[n].shape) for a in res)
    loss = tot[0, off["loss"]]
    chip = 2 * lax.axis_index("x") + lax.axis_index("y")
    taps_g = []
    for n in CONV:
        width = W[n].shape[2]
        full = lax.slice(tot, (0, off[n]), (1, off[n] + 3 * N_CHIPS * width)).reshape(3, N_CHIPS * width)
        taps_g.append(lax.dynamic_slice_in_dim(full, chip * width, width, axis=1))
    taps_out = _adamw_taps([W[n][0] for n in CONV], taps_g, [Mo[n][0] for n in CONV], [Vo[n][0] for n in CONV])
    for n, g, res in zip(CONV, taps_g, taps_out):
        out[n] = tuple(a[None] for a in (g,) + res)

    res = [loss, grad_x[None]]
    for k in range(4):
        res += [out[n][k] for n in ORDER]
    return tuple(res)


def kernel(x, ln0_g, ln0_b, w_in, b_in, conv_w, w_a, w_b, w_o, b_o, ln1_g, ln1_b, w_up, b_up, ffn_conv_w, ffn_conv_b, w_down, b_down, ln2_g, ln2_b, loss_target, m_ln0_g, m_ln0_b, m_w_in, m_b_in, m_conv_w, m_w_a, m_w_b, m_w_o, m_b_o, m_ln1_g, m_ln1_b, m_w_up, m_b_up, m_ffn_conv_w, m_ffn_conv_b, m_w_down, m_b_down, m_ln2_g, m_ln2_b, v_ln0_g, v_ln0_b, v_w_in, v_b_in, v_conv_w, v_w_a, v_w_b, v_w_o, v_b_o, v_ln1_g, v_ln1_b, v_w_up, v_b_up, v_ffn_conv_w, v_ffn_conv_b, v_w_down, v_b_down, v_ln2_g, v_ln2_b):
    W = dict(zip(ORDER, (ln0_g, ln0_b, w_in, b_in, conv_w, w_a, w_b, w_o, b_o, ln1_g, ln1_b, w_up, b_up,
                         ffn_conv_w, ffn_conv_b, w_down, b_down, ln2_g, ln2_b)))
    Mo = dict(zip(ORDER, (m_ln0_g, m_ln0_b, m_w_in, m_b_in, m_conv_w, m_w_a, m_w_b, m_w_o, m_b_o, m_ln1_g, m_ln1_b,
                          m_w_up, m_b_up, m_ffn_conv_w, m_ffn_conv_b, m_w_down, m_b_down, m_ln2_g, m_ln2_b)))
    Vo = dict(zip(ORDER, (v_ln0_g, v_ln0_b, v_w_in, v_b_in, v_conv_w, v_w_a, v_w_b, v_w_o, v_b_o, v_ln1_g, v_ln1_b,
                          v_w_up, v_b_up, v_ffn_conv_w, v_ffn_conv_b, v_w_down, v_b_down, v_ln2_g, v_ln2_b)))
    return _step(x, loss_target, W, Mo, Vo)
```

```python
import functools
import math

import jax
import jax.numpy as jnp
from jax import lax
from jax.experimental import pallas as pl
from jax.experimental.pallas import tpu as pltpu
from jax.experimental.pallas import tpu_sc as plsc

F32 = jnp.float32
BF16 = jnp.bfloat16

D_MODEL = 1024
D_CONV = D_MODEL
HEAD_DIM = 64
HEADS_PER_GROUP = 8
GROUPS = ((128, 1), (512, 4), (2048, 16))
N_GROUPS = len(GROUPS)
GROUP_W = HEADS_PER_GROUP * HEAD_DIM
QKV_W = N_GROUPS * GROUP_W
RADIUS = 64
D_FF = 2816
LN_EPS = 1e-5
ALPHA = 2.0 ** 0.25
MASK_VALUE = -1e30
ATT_SCALE = HEAD_DIM ** -0.5
OFF_B = 0
OFF_C = OFF_B + D_CONV
OFF_H = OFF_C + D_CONV
OFF_Q = OFF_H + D_CONV
OFF_K = OFF_Q + QKV_W
OFF_V = OFF_K + QKV_W
OFF_GA = OFF_V + QKV_W
OFF_GB = OFF_GA + D_MODEL
N_IN = OFF_GB + D_MODEL
ADAM_LR = 0.001
ADAM_B1 = 0.9
ADAM_B2 = 0.999
ADAM_EPS = 1e-08
ADAM_WD = 0.01
ADAM_STEP = 10
INV_SQRT2 = 0.7071067811865476
INV_SQRT_2PI = 0.3989422804014327
LOG2_E = 1.4426950408889634

LANES = 128
SUBLANES = 8
VMEM_BYTES_V7X = 64 * 1024 * 1024
N_CHIPS = 4
N_CORES = 2
N_DEV = N_CHIPS * N_CORES
MESH = pl.DeviceIdType.MESH

N_BLK = N_IN // GROUP_W
PERM = (0, 1, 2, 3, 4, 5, 15, 16, 17, 18, 6, 9, 12, 7, 10, 13, 8, 11, 14)
INV_PERM = tuple(PERM.index(b) for b in range(N_BLK))
P_B, P_C, P_H, P_GA, P_GB, P_Q0 = 0, 1024, 2048, 3072, 4096, 5120
N_NAT = P_Q0 + QKV_W // N_GROUPS * 3
N_GATED = P_Q0

def _col_runs():
    shard_w = N_IN // N_CHIPS
    runs = []
    for pos, blk in enumerate(PERM):
        c, end = blk * GROUP_W, (blk + 1) * GROUP_W
        while c < end:
            stop = min(end, (c // shard_w + 1) * shard_w)
            run = (c // shard_w, c % shard_w, pos * GROUP_W + c - blk * GROUP_W, stop - c)
            if runs and runs[-1][0] == run[0] and runs[-1][1] + runs[-1][3] == run[1]:
                runs[-1] = runs[-1][:3] + (runs[-1][3] + run[3],)
            else:
                runs.append(run)
            c = stop
    return runs


SLAB = 128
CHUNK = 256
PAD = SUBLANES
TQ = 128


def _cparams(sem, vmem_mb):
    assert vmem_mb * 1024 * 1024 < VMEM_BYTES_V7X
    return pltpu.CompilerParams(dimension_semantics=sem, vmem_limit_bytes=vmem_mb * 1024 * 1024)


def _resident(shape):
    nd = len(shape)
    return pl.BlockSpec(shape, lambda *_: (0,) * nd, pipeline_mode=pl.Buffered(1))


def _hbm(*arrays):
    return [pltpu.with_memory_space_constraint(a, pltpu.HBM) for a in arrays]


def _dot(a, b):
    return jnp.dot(a, b, preferred_element_type=F32)


def _dot_nt(a, b):
    return lax.dot_general(a, b, (((1,), (1,)), ((), ())), preferred_element_type=F32)


def _dot_tn(a, b):
    return lax.dot_general(a, b, (((0,), (0,)), ((), ())), preferred_element_type=F32)


def _ln_stats(z):
    mu = jnp.mean(z, -1, keepdims=True)
    zc = z - mu
    var = jnp.mean(zc * zc, -1, keepdims=True)
    rstd = lax.rsqrt(var + LN_EPS)
    return zc * rstd, rstd


def _ln_bwd(dh, xhat, rstd, g):
    dxh = dh * g
    m1 = jnp.mean(dxh, -1, keepdims=True)
    m2 = jnp.mean(dxh * xhat, -1, keepdims=True)
    return rstd * (dxh - m1 - xhat * m2)


def _rows8(rows, width):
    pad = [jnp.zeros((1, width), F32)] * (SUBLANES - len(rows))
    return jnp.concatenate(list(rows) + pad, axis=0)


def _mm_nn(a, w, bias, *, tm, tn, out_dtype, name, vmem_mb=40):
    M, K = a.shape
    if w.ndim == 3:
        per = tn // w.shape[2]
        assert per * w.shape[2] == tn and w.shape[0] % per == 0
        n_tiles = w.shape[0] // per
        w_spec = pl.BlockSpec((per, K, w.shape[2]), lambda j, i: (j, 0, 0))
    else:
        per = 0
        n_tiles = w.shape[1] // tn
        w_spec = pl.BlockSpec((K, tn), lambda j, i: (0, j))

    def body(a_ref, w_ref, b_ref, o_ref):
        wv = jnp.concatenate([w_ref[k] for k in range(per)], axis=1) if per else w_ref[...]
        o_ref[...] = (_dot(a_ref[...], wv) + b_ref[...]).astype(o_ref.dtype)

    return pl.pallas_call(
        body, grid=(n_tiles, M // tm),
        in_specs=[pl.BlockSpec((tm, K), lambda j, i: (i, 0)), w_spec, pl.BlockSpec((1, tn), lambda j, i: (0, j))],
        out_specs=pl.BlockSpec((tm, tn), lambda j, i: (i, j)),
        out_shape=jax.ShapeDtypeStruct((M, n_tiles * tn), out_dtype),
        name=name, compiler_params=_cparams(("arbitrary", "parallel"), vmem_mb))(*_hbm(a, w, bias))


def _mm_nt(a, w, *, tm, a_col=0, name, vmem_mb=40):
    M = a.shape[0]
    N, K = w.shape

    def body(a_ref, w_ref, o_ref):
        o_ref[...] = _dot_nt(a_ref[...], w_ref[...]).astype(o_ref.dtype)

    return pl.pallas_call(
        body, grid=(M // tm,),
        in_specs=[pl.BlockSpec((tm, K), lambda i: (i, a_col)),
                  pl.BlockSpec((N, K), lambda i: (0, 0))],
        out_specs=pl.BlockSpec((tm, N), lambda i: (i, 0)),
        out_shape=jax.ShapeDtypeStruct((M, N), BF16),
        name=name, compiler_params=_cparams(("parallel",), vmem_mb))(*_hbm(a, w))


def _mm_tn(a, g, *, n_out, tn, ts, g_block, g_map, colsum=False, split=1, name, vmem_mb=48):
    S, K = a.shape
    n_s = S // ts
    shard_w = tn // split

    def body(a_ref, g_ref, *rest):
        if colsum:
            o_ref, cs_ref, acc_ref, cacc_ref = rest
        else:
            o_ref, acc_ref = rest
        s = pl.program_id(1)

        @pl.when(s == 0)
        def _():
            acc_ref[...] = jnp.zeros_like(acc_ref)
            if colsum:
                cacc_ref[...] = jnp.zeros_like(cacc_ref)

        gv = g_ref[...]
        acc_ref[...] += _dot_tn(a_ref[...], gv)
        if colsum:
            cacc_ref[...] += jnp.broadcast_to(jnp.sum(gv.astype(F32), axis=0, keepdims=True), cacc_ref.shape)

        @pl.when(s == n_s - 1)
        def _():
            for k in range(split):
                o_ref[k] = acc_ref[:, k * shard_w:(k + 1) * shard_w].astype(o_ref.dtype)
            if colsum:
                cs_ref[...] = cacc_ref[...]

    out_specs = [pl.BlockSpec((split, K, shard_w), lambda j, s: (j, 0, 0))]
    out_shape = [jax.ShapeDtypeStruct((n_out * split, K, shard_w), BF16)]
    scratch = [pltpu.VMEM((K, tn), F32)]
    if colsum:
        out_specs.append(pl.BlockSpec((SUBLANES, tn), lambda j, s: (0, j)))
        out_shape.append(jax.ShapeDtypeStruct((SUBLANES, n_out * tn), F32))
        scratch.append(pltpu.VMEM((SUBLANES, tn), F32))
    res = pl.pallas_call(
        body, grid=(n_out, n_s),
        in_specs=[pl.BlockSpec((ts, K), lambda j, s: (s, 0)), pl.BlockSpec(g_block, g_map)],
        out_specs=out_specs, out_shape=out_shape, scratch_shapes=scratch,
        name=name, compiler_params=_cparams(("parallel", "arbitrary"), vmem_mb))(*_hbm(a, g))
    return res if colsum else res[0]


def _mm_tn_cat(a, gs, *, ts, name, vmem_mb=40):
    S, K = a.shape
    widths = [g.shape[1] for g in gs]
    n_s, total = S // ts, sum(widths)

    def body(*refs):
        a_ref, g_refs = refs[0], refs[1:1 + len(gs)]
        o_ref, cs_ref, acc_ref, cacc_ref = refs[1 + len(gs):]
        s = pl.program_id(0)

        @pl.when(s == 0)
        def _():
            acc_ref[...] = jnp.zeros_like(acc_ref)
            cacc_ref[...] = jnp.zeros_like(cacc_ref)

        av, col = a_ref[...], 0
        for g_ref, w in zip(g_refs, widths):
            gv = g_ref[...]
            acc_ref[:, col:col + w] += _dot_tn(av, gv)
            cacc_ref[:, col:col + w] += jnp.broadcast_to(jnp.sum(gv.astype(F32), axis=0, keepdims=True), (SUBLANES, w))
            col += w

        @pl.when(s == n_s - 1)
        def _():
            o_ref[...] = acc_ref[...].astype(BF16)
            cs_ref[...] = cacc_ref[...]

    return pl.pallas_call(
        body, grid=(n_s,),
        in_specs=[pl.BlockSpec((ts, K), lambda s: (s, 0))] + [pl.BlockSpec((ts, w), lambda s: (s, 0)) for w in widths],
        out_specs=[pl.BlockSpec((K, total), lambda s: (0, 0)), pl.BlockSpec((SUBLANES, total), lambda s: (0, 0))],
        out_shape=[jax.ShapeDtypeStruct((K, total), BF16), jax.ShapeDtypeStruct((SUBLANES, total), F32)],
        scratch_shapes=[pltpu.VMEM((K, total), F32), pltpu.VMEM((SUBLANES, total), F32)],
        name=name, compiler_params=_cparams(("arbitrary",), vmem_mb))(*_hbm(a, *gs))


DILS = tuple(d for _, d in GROUPS if d > 1)


def _res_spec(d, tm, width):
    return pl.BlockSpec((d, tm // d, width), lambda i: (0, i, 0))


def _lane_scratch(tm, width):
    return [pltpu.VMEM((tm, LANES), F32)] * (width // LANES)


def _to_residue(val, dst_refs, dils, tm, dtype, scr):
    for c, ref in enumerate(scr):
        ref[...] = val[:, c * LANES:(c + 1) * LANES]
    for dst_ref, d in zip(dst_refs, dils):
        for r in range(d):
            cols = [ref[pl.ds(r, tm // d, stride=d), :] for ref in scr]
            dst_ref[r] = jnp.concatenate(cols, axis=1).astype(dtype)


def _from_residue(rows_of, d, tm, scr):
    for r in range(d):
        v = rows_of(r).astype(F32)
        for c, ref in enumerate(scr):
            ref[pl.ds(r, tm // d, stride=d), :] = v[:, c * LANES:(c + 1) * LANES]
    return jnp.concatenate([ref[...] for ref in scr], axis=1)


def _ln0_fwd(x, g, b, after=(), *, tm=512):
    S, Dm = x.shape
    n_after = len(after)

    def body(x_ref, g_ref, b_ref, *rest):
        h_ref, hb_ref, *rest = rest[n_after:]
        xhat, _ = _ln_stats(x_ref[...])
        h = xhat * g_ref[...] + b_ref[...]
        h_ref[...] = h
        hb_ref[...] = h.astype(BF16)
        _to_residue(h, rest[:len(DILS)], DILS, tm, BF16, rest[len(DILS):])

    row = pl.BlockSpec((tm, Dm), lambda i: (i, 0))
    vec = pl.BlockSpec((1, Dm), lambda i: (0, 0))
    return pl.pallas_call(
        body, grid=(S // tm,), in_specs=[row, vec, vec] + [pl.BlockSpec(memory_space=pl.ANY)] * n_after,
        out_specs=[row, row] + [_res_spec(d, tm, Dm) for d in DILS],
        out_shape=[jax.ShapeDtypeStruct((S, Dm), F32), jax.ShapeDtypeStruct((S, Dm), BF16)]
        + [jax.ShapeDtypeStruct((d, S // d, Dm), BF16) for d in DILS],
        scratch_shapes=_lane_scratch(tm, Dm),
        name="ln0_fwd", compiler_params=_cparams(("parallel",), 32))(*_hbm(x, g, b), *after)


def _slab_spec(S, col0):
    return pl.BlockSpec((S, SLAB), lambda j: (0, col0 // SLAB + j))


def _zero_pads(scr, S):
    scr[0:PAD, :] = jnp.zeros((PAD, SLAB), F32)
    scr[S + PAD:S + 2 * PAD, :] = jnp.zeros((PAD, SLAB), F32)


def _shifted(scr, t):
    return (scr[PAD - 1 + t:PAD - 1 + t + CHUNK, :], scr[PAD + t:PAD + t + CHUNK, :],
            scr[PAD + 1 + t:PAD + 1 + t + CHUNK, :])


def _conv_gate_fwd(proj, conv_w):
    S = proj.shape[0]

    def body(b_ref, c_ref, h_ref, w_ref, o_ref, u_scr):
        _zero_pads(u_scr, S)
        for t in range(0, S, CHUNK):
            u_scr[PAD + t:PAD + t + CHUNK, :] = c_ref[t:t + CHUNK, :].astype(F32) * h_ref[t:t + CHUNK, :].astype(F32)
        w0, w1, w2 = w_ref[0:1, :], w_ref[1:2, :], w_ref[2:3, :]
        for t in range(0, S, CHUNK):
            um, u0, up = _shifted(u_scr, t)
            cv = w0 * um + w1 * u0 + w2 * up
            o_ref[t:t + CHUNK, :] = (b_ref[t:t + CHUNK, :].astype(F32) * cv).astype(BF16)

    return pl.pallas_call(
        body, grid=(D_CONV // SLAB,),
        in_specs=[_slab_spec(S, P_B), _slab_spec(S, P_C), _slab_spec(S, P_H),
                  pl.BlockSpec((3, SLAB), lambda j: (0, j))],
        out_specs=pl.BlockSpec((S, SLAB), lambda j: (0, j)),
        out_shape=jax.ShapeDtypeStruct((S, D_CONV), BF16),
        scratch_shapes=[pltpu.VMEM((S + 2 * PAD, SLAB), F32)],
        name="conv_gate_fwd", compiler_params=_cparams(("parallel",), 40))(*_hbm(proj, proj, proj, conv_w))


MASKED_DISTANCE = -1e34


def _attn_bias_table(g):
    dil = GROUPS[g][1]
    j = lax.broadcasted_iota(jnp.int32, (2 * TQ, TQ), 0)
    a = lax.broadcasted_iota(jnp.int32, (2 * TQ, TQ), 1)
    rel = jnp.abs(j - RADIUS - a)
    base = -(rel * dil).astype(F32)
    inside, after_start, before_end = rel <= RADIUS, j >= RADIUS, j < TQ + RADIUS
    variants = []
    for first, last in ((False, False), (True, False), (False, True), (True, True)):
        valid = inside & (after_start if first else True) & (before_end if last else True)
        variants.append(jnp.where(valid, base, MASKED_DISTANCE))
    return jnp.stack(variants)


SUBS = 4
TB = SUBS * TQ


def _ext_window(p_ref, c_ref, n_ref):
    return jnp.concatenate([p_ref[TB - RADIUS:, :], c_ref[...], n_ref[:RADIUS, :]], axis=0)


def _head_stats(rows):
    pad = jnp.zeros((LANES - len(rows), TQ), F32)
    return jnp.concatenate(list(rows) + [pad], axis=0).T


def _slope(g, h):
    return 2.0 ** (-8.0 * (g * HEADS_PER_GROUP + h + 1) / (N_GROUPS * HEADS_PER_GROUP))


def _pair(a, h):
    return a[:, (h // 2) * LANES:(h // 2 + 1) * LANES]


def _own_lanes(a, h):
    lane = lax.broadcasted_iota(jnp.int32, a.shape, 1)
    return jnp.where((lane >= HEAD_DIM) == (h % 2 == 1), a, jnp.zeros_like(a))


def _own_rows(a, h):
    return a[(h % 2) * HEAD_DIM:(h % 2 + 1) * HEAD_DIM, :]


def _attn_fwd(qkv, col0, g):
    dil, sub, _ = qkv.shape
    nb = sub // TB
    heads = HEADS_PER_GROUP

    def body(q_ref, kp, kc, kn, vp, vc, vn, bias_ref, o_ref, lse_ref, ot_scr, s_scr, p_scr):
        i = pl.program_id(1)
        kext = _ext_window(kp, kc, kn)
        vext = _ext_window(vp, vc, vn)
        q = q_ref[...] * ATT_SCALE
        for b in range(SUBS):
            kwin, qb = kext[b * TQ:(b + 2) * TQ, :], q[b * TQ:(b + 1) * TQ, :]
            for h in range(heads):
                s_scr[b * heads + h] = _dot_nt(_pair(kwin, h), _own_lanes(_pair(qb, h), h))
        inv_den = []
        for b in range(SUBS):
            block = i * SUBS + b
            bias = bias_ref[jnp.where(block == 0, 1, 0) + jnp.where(block == nb * SUBS - 1, 2, 0)]
            lse = []
            for h in range(heads):
                s = s_scr[b * heads + h] + _slope(g, h) * bias
                m = jnp.max(s, axis=0, keepdims=True)
                p = jnp.exp(s - m)
                den = jnp.sum(p, axis=0, keepdims=True)
                p_scr[b * heads + h] = p.astype(BF16)
                inv_den.append(1.0 / den)
                lse.append(m + jnp.log(den))
            lse_ref[b * TQ:(b + 1) * TQ, :] = _head_stats(lse)
        for b in range(SUBS):
            vwin = vext[b * TQ:(b + 2) * TQ, :]
            for h in range(heads):
                ot = _dot_tn(_pair(vwin, h), p_scr[b * heads + h])
                ot_scr[h * HEAD_DIM:(h + 1) * HEAD_DIM, b * TQ:(b + 1) * TQ] = _own_rows(ot, h) * inv_den[b * heads + h]
        o_ref[...] = ot_scr[...].T

    def spec(col, shift):
        return pl.BlockSpec((None, TB, GROUP_W), lambda r, i: (r, jnp.clip(i + shift, 0, nb - 1), col))

    return pl.pallas_call(
        body, grid=(dil, nb),
        in_specs=[spec(col0, 0), spec(col0 + 1, -1), spec(col0 + 1, 0), spec(col0 + 1, 1),
                  spec(col0 + 2, -1), spec(col0 + 2, 0), spec(col0 + 2, 1),
                  pl.BlockSpec((4, 2 * TQ, TQ), lambda r, i: (0, 0, 0))],
        out_specs=[pl.BlockSpec((None, TB, GROUP_W), lambda r, i: (r, i, 0)),
                   pl.BlockSpec((None, TB, LANES), lambda r, i: (r, i, 0))],
        out_shape=[jax.ShapeDtypeStruct((dil, sub, GROUP_W), F32), jax.ShapeDtypeStruct((dil, sub, LANES), F32)],
        scratch_shapes=[pltpu.VMEM((GROUP_W, TB), F32), pltpu.VMEM((SUBS * heads, 2 * TQ, TQ), F32),
                        pltpu.VMEM((SUBS * heads, 2 * TQ, TQ), BF16)],
        name=f"attn_fwd_g{g}", compiler_params=_cparams(("parallel", "arbitrary"), 32))(
            *_hbm(*([qkv] * 7), _attn_bias_table(g)))


def _expand_heads():
    h = lax.broadcasted_iota(jnp.int32, (LANES, GROUP_W), 0)
    c = lax.broadcasted_iota(jnp.int32, (LANES, GROUP_W), 1)
    return (c // HEAD_DIM == h).astype(F32)


def _dot_f32(a, b):
    return jnp.dot(a, b, preferred_element_type=F32, precision=lax.Precision.HIGH)


def _attn_combine(outs, lses, *, tm=512):
    S = outs[0].shape[1]
    n_col = GROUP_W // LANES

    def body(*refs):
        ins, e_ref = refs[:2 * N_GROUPS], refs[2 * N_GROUPS]
        c_ref, cb_ref, lt_ref = refs[2 * N_GROUPS + 1:2 * N_GROUPS + 4]
        scr = refs[2 * N_GROUPS + 4:]
        o, l = [ins[0][0]], [ins[N_GROUPS][0]]
        for k, d in enumerate(DILS):
            o_ref, l_ref = ins[1 + k], ins[N_GROUPS + 1 + k]
            o.append(_from_residue(lambda r: o_ref[r], d, tm, scr[k * (n_col + 1):k * (n_col + 1) + n_col]))
            l.append(_from_residue(lambda r: l_ref[r], d, tm, scr[k * (n_col + 1) + n_col:(k + 1) * (n_col + 1)]))
        m = jnp.maximum(jnp.maximum(l[0], l[1]), l[2])
        e = [jnp.exp(v - m) for v in l]
        den = e[0] + e[1] + e[2]
        comb = sum(_dot_f32(ev / den, e_ref[...]) * ov for ev, ov in zip(e, o))
        c_ref[...] = comb
        cb_ref[...] = comb.astype(BF16)
        lt_ref[...] = m + jnp.log(den)

    row = pl.BlockSpec((tm, GROUP_W), lambda i: (i, 0))
    dils = [d for _, d in GROUPS]
    return pl.pallas_call(
        body, grid=(S // tm,),
        in_specs=[_res_spec(d, tm, GROUP_W) for d in dils] + [_res_spec(d, tm, LANES) for d in dils]
        + [_resident((LANES, GROUP_W))],
        out_specs=[row, row, pl.BlockSpec((tm, LANES), lambda i: (i, 0))],
        out_shape=[jax.ShapeDtypeStruct((S, GROUP_W), F32), jax.ShapeDtypeStruct((S, GROUP_W), BF16),
                   jax.ShapeDtypeStruct((S, LANES), F32)],
        scratch_shapes=_lane_scratch(tm, GROUP_W + LANES) * len(DILS),
        name="attn_combine", compiler_params=_cparams(("parallel",), 32))(*_hbm(*outs, *lses, _expand_heads()))


def _branch_mix(ya_in, comb_b, w_a, w_b, proj, *, tm=512):
    S = ya_in.shape[0]

    def body(ya_ref, cb_ref, wa_ref, wb_ref, ga_ref, gb_ref, yab_ref, mx_ref):
        y_a = _dot(ya_ref[...], wa_ref[...])
        y_b = _dot(cb_ref[...], wb_ref[...])
        yab_ref[:, 0:D_MODEL] = y_a.astype(BF16)
        yab_ref[:, D_MODEL:2 * D_MODEL] = y_b.astype(BF16)
        mx = jax.nn.sigmoid(ga_ref[...].astype(F32)) * y_a + jax.nn.sigmoid(gb_ref[...].astype(F32)) * y_b
        mx_ref[...] = mx.astype(BF16)

    return pl.pallas_call(
        body, grid=(S // tm,),
        in_specs=[pl.BlockSpec((tm, D_CONV), lambda i: (i, 0)), pl.BlockSpec((tm, GROUP_W), lambda i: (i, 0)),
                  pl.BlockSpec((D_CONV, D_MODEL), lambda i: (0, 0)), pl.BlockSpec((GROUP_W, D_MODEL), lambda i: (0, 0)),
                  pl.BlockSpec((tm, D_MODEL), lambda i: (i, P_GA // D_MODEL)),
                  pl.BlockSpec((tm, D_MODEL), lambda i: (i, P_GB // D_MODEL))],
        out_specs=[pl.BlockSpec((tm, 2 * D_MODEL), lambda i: (i, 0)), pl.BlockSpec((tm, D_MODEL), lambda i: (i, 0))],
        out_shape=[jax.ShapeDtypeStruct((S, 2 * D_MODEL), BF16), jax.ShapeDtypeStruct((S, D_MODEL), BF16)],
        name="branch_mix", compiler_params=_cparams(("parallel",), 40))(*_hbm(ya_in, comb_b, w_a, w_b, proj, proj))


def _mix_ln1(mixin, w_o, b_o, h0, g1, b1, *, tm=512):
    S = mixin.shape[0]

    def body(mx_ref, wo_ref, bo_ref, h0_ref, g_ref, b_ref, xh_ref, rs_ref, h1b_ref):
        z = ALPHA * h0_ref[...] + _dot(mx_ref[...], wo_ref[...]) + bo_ref[...]
        xhat, rstd = _ln_stats(z)
        xh_ref[...] = xhat
        rs_ref[...] = jnp.broadcast_to(rstd, (tm, LANES))
        h1b_ref[...] = (xhat * g_ref[...] + b_ref[...]).astype(BF16)

    row = pl.BlockSpec((tm, D_MODEL), lambda i: (i, 0))
    vec = pl.BlockSpec((1, D_MODEL), lambda i: (0, 0))
    return pl.pallas_call(
        body, grid=(S // tm,),
        in_specs=[row, pl.BlockSpec((D_MODEL, D_MODEL), lambda i: (0, 0)), vec, row, vec, vec],
        out_specs=[row, pl.BlockSpec((tm, LANES), lambda i: (i, 0)), row],
        out_shape=[jax.ShapeDtypeStruct((S, D_MODEL), F32), jax.ShapeDtypeStruct((S, LANES), F32),
                   jax.ShapeDtypeStruct((S, D_MODEL), BF16)],
        name="mix_ln1", compiler_params=_cparams(("parallel",), 40))(*_hbm(mixin, w_o, b_o, h0, g1, b1))


def _gelu_parts(cz):
    cdf = 0.5 * (1.0 + lax.erf(cz * INV_SQRT2))
    return cdf, cz * cdf


def _ffn_conv_fwd(up, cw, cb):
    S = up.shape[0]

    def body(a_ref, g_ref, w_ref, cb_ref, o_ref, a_scr):
        _zero_pads(a_scr, S)
        for t in range(0, S, CHUNK):
            a_scr[PAD + t:PAD + t + CHUNK, :] = a_ref[t:t + CHUNK, :].astype(F32)
        w0, w1, w2 = w_ref[0:1, :], w_ref[1:2, :], w_ref[2:3, :]
        for t in range(0, S, CHUNK):
            am, a0, ap = _shifted(a_scr, t)
            _, gel = _gelu_parts(w0 * am + w1 * a0 + w2 * ap + cb_ref[...])
            o_ref[t:t + CHUNK, :] = (gel * g_ref[t:t + CHUNK, :].astype(F32)).astype(BF16)

    return pl.pallas_call(
        body, grid=(D_FF // SLAB,),
        in_specs=[_slab_spec(S, 0), _slab_spec(S, D_FF), pl.BlockSpec((3, SLAB), lambda j: (0, j)),
                  pl.BlockSpec((1, SLAB), lambda j: (0, j))],
        out_specs=pl.BlockSpec((S, SLAB), lambda j: (0, j)),
        out_shape=jax.ShapeDtypeStruct((S, D_FF), BF16),
        scratch_shapes=[pltpu.VMEM((S + 2 * PAD, SLAB), F32)],
        name="ffn_conv_fwd", compiler_params=_cparams(("parallel",), 40))(*_hbm(up, up, cw, cb))


def _down_ln2_loss(f, w_down, b_down, xhat1, g1, b1, g2, b2, target, *, tm=512):
    S = f.shape[0]

    def body(f_ref, wd_ref, bd_ref, xh1_ref, g1_ref, b1_ref, g2_ref, b2_ref, t_ref, dz_ref, dzb_ref, st_ref):
        h1 = xh1_ref[...] * g1_ref[...] + b1_ref[...]
        z = ALPHA * h1 + _dot(f_ref[...], wd_ref[...]) + bd_ref[...]
        xhat, rstd = _ln_stats(z)
        err = xhat * g2_ref[...] + b2_ref[...] - t_ref[...]
        loss = (0.5 / D_MODEL) * jnp.sum(jnp.sum(err * err, axis=1, keepdims=True), axis=0, keepdims=True)
        dh2 = err * (1.0 / D_MODEL)
        dz = _ln_bwd(dh2, xhat, rstd, g2_ref[...])
        dz_ref[...] = dz
        dzb_ref[...] = dz.astype(BF16)
        upd = _rows8([jnp.sum(dh2 * xhat, axis=0, keepdims=True), jnp.sum(dh2, axis=0, keepdims=True),
                      jnp.broadcast_to(loss, (1, D_MODEL)), jnp.sum(dz, axis=0, keepdims=True)], D_MODEL)

        @pl.when(pl.program_id(0) == 0)
        def _():
            st_ref[...] = upd

        @pl.when(pl.program_id(0) != 0)
        def _():
            st_ref[...] += upd

    row = pl.BlockSpec((tm, D_MODEL), lambda i: (i, 0))
    vec = pl.BlockSpec((1, D_MODEL), lambda i: (0, 0))
    return pl.pallas_call(
        body, grid=(S // tm,),
        in_specs=[pl.BlockSpec((tm, D_FF), lambda i: (i, 0)), _resident((D_FF, D_MODEL)),
                  vec, row, vec, vec, vec, vec, row],
        out_specs=[row, row, pl.BlockSpec((SUBLANES, D_MODEL), lambda i: (0, 0))],
        out_shape=[jax.ShapeDtypeStruct((S, D_MODEL), F32), jax.ShapeDtypeStruct((S, D_MODEL), BF16),
                   jax.ShapeDtypeStruct((SUBLANES, D_MODEL), F32)],
        name="down_ln2_loss", compiler_params=_cparams(("arbitrary",), 56))(
            *_hbm(f, w_down, b_down, xhat1, g1, b1, g2, b2, target))


def _ffn_conv_bwd(up, df, cw, cb):
    S = up.shape[0]

    def body(a_ref, g_ref, df_ref, w_ref, cb_ref, dup_ref, sm_ref, a_scr, d_scr):
        _zero_pads(a_scr, S)
        _zero_pads(d_scr, S)
        for t in range(0, S, CHUNK):
            a_scr[PAD + t:PAD + t + CHUNK, :] = a_ref[t:t + CHUNK, :].astype(F32)
        w0, w1, w2 = w_ref[0:1, :], w_ref[1:2, :], w_ref[2:3, :]
        zero = jnp.zeros((1, SLAB), F32)
        s_dg, s_dcz, s_w0, s_w1, s_w2 = zero, zero, zero, zero, zero
        for t in range(0, S, CHUNK):
            am, a0, ap = _shifted(a_scr, t)
            cz = w0 * am + w1 * a0 + w2 * ap + cb_ref[...]
            cdf, gel = _gelu_parts(cz)
            dfv = df_ref[t:t + CHUNK, :].astype(F32)
            dgte = dfv * gel
            dcz = dfv * g_ref[t:t + CHUNK, :].astype(F32) * (cdf + (cz * INV_SQRT_2PI) * jnp.exp2(cz * cz * (-0.5 * LOG2_E)))
            dup_ref[1, t:t + CHUNK, :] = dgte.astype(BF16)
            d_scr[PAD + t:PAD + t + CHUNK, :] = dcz
            s_dg = s_dg + jnp.sum(dgte, axis=0, keepdims=True)
            s_dcz = s_dcz + jnp.sum(dcz, axis=0, keepdims=True)
            s_w0 = s_w0 + jnp.sum(dcz * am, axis=0, keepdims=True)
            s_w1 = s_w1 + jnp.sum(dcz * a0, axis=0, keepdims=True)
            s_w2 = s_w2 + jnp.sum(dcz * ap, axis=0, keepdims=True)
        s_da = zero
        for t in range(0, S, CHUNK):
            dm, d0, dp = _shifted(d_scr, t)
            da = w0 * dp + w1 * d0 + w2 * dm
            dup_ref[0, t:t + CHUNK, :] = da.astype(BF16)
            s_da = s_da + jnp.sum(da, axis=0, keepdims=True)
        sm_ref[...] = _rows8([s_da, s_dg, s_dcz, s_w0, s_w1, s_w2], SLAB)

    return pl.pallas_call(
        body, grid=(D_FF // SLAB,),
        in_specs=[_slab_spec(S, 0), _slab_spec(S, D_FF), pl.BlockSpec((S, SLAB), lambda j: (0, j)),
                  pl.BlockSpec((3, SLAB), lambda j: (0, j)), pl.BlockSpec((1, SLAB), lambda j: (0, j))],
        out_specs=[pl.BlockSpec((2, S, SLAB), lambda j: (0, 0, j)), pl.BlockSpec((SUBLANES, SLAB), lambda j: (0, j))],
        out_shape=[jax.ShapeDtypeStruct((2, S, D_FF), BF16), jax.ShapeDtypeStruct((SUBLANES, D_FF), F32)],
        scratch_shapes=[pltpu.VMEM((S + 2 * PAD, SLAB), F32)] * 2,
        name="ffn_conv_bwd", compiler_params=_cparams(("parallel",), 48))(*_hbm(up, up, df, cw, cb))


def _up_bwd_ln1(dup, w_up3, dz2, xhat1, rstd1, g1, *, tm=512):
    S = dz2.shape[0]
    ns, _, tk = w_up3.shape
    per_plane = D_FF // tk

    def body(du_ref, w_ref, dz2_ref, xh_ref, rs_ref, g_ref, dz_ref, dzb_ref, st_ref):
        dh = ALPHA * dz2_ref[...]
        for plane in range(ns // per_plane):
            w = jnp.concatenate([w_ref[plane * per_plane + k] for k in range(per_plane)], axis=1)
            dh = dh + _dot_nt(du_ref[plane], w)
        xhat = xh_ref[...]
        dz = _ln_bwd(dh, xhat, rs_ref[:, 0:1], g_ref[...])
        dz_ref[...] = dz
        dzb_ref[...] = dz.astype(BF16)
        upd = _rows8([jnp.sum(dh * xhat, axis=0, keepdims=True), jnp.sum(dh, axis=0, keepdims=True),
                      jnp.sum(dz, axis=0, keepdims=True)], D_MODEL)

        @pl.when(pl.program_id(0) == 0)
        def _():
            st_ref[...] = upd

        @pl.when(pl.program_id(0) != 0)
        def _():
            st_ref[...] += upd

    row = pl.BlockSpec((tm, D_MODEL), lambda i: (i, 0))
    return pl.pallas_call(
        body, grid=(S // tm,),
        in_specs=[pl.BlockSpec((dup.shape[0], tm, D_FF), lambda i: (0, i, 0)), _resident(w_up3.shape),
                  row, row, pl.BlockSpec((tm, LANES), lambda i: (i, 0)), pl.BlockSpec((1, D_MODEL), lambda i: (0, 0))],
        out_specs=[row, row, pl.BlockSpec((SUBLANES, D_MODEL), lambda i: (0, 0))],
        out_shape=[jax.ShapeDtypeStruct((S, D_MODEL), F32), jax.ShapeDtypeStruct((S, D_MODEL), BF16),
                   jax.ShapeDtypeStruct((SUBLANES, D_MODEL), F32)],
        name="up_bwd_ln1", compiler_params=_cparams(("arbitrary",), 56))(*_hbm(dup, w_up3, dz2, xhat1, rstd1, g1))


def _mix_bwd(dz1b, w_o, proj, yab, *, tm=512):
    S = dz1b.shape[0]

    def body(dz_ref, wo_ref, ga_ref, gb_ref, y_ref, dy_ref, dg_ref):
        dmx = _dot_nt(dz_ref[...], wo_ref[...])
        for k, gt_ref in enumerate((ga_ref, gb_ref)):
            sl = slice(k * D_MODEL, (k + 1) * D_MODEL)
            sg = jax.nn.sigmoid(gt_ref[...].astype(F32))
            dy_ref[:, sl] = (dmx * sg).astype(BF16)
            dg_ref[k] = (dmx * y_ref[:, sl].astype(F32) * sg * (1.0 - sg)).astype(BF16)

    row = pl.BlockSpec((tm, D_MODEL), lambda i: (i, 0))
    wide = pl.BlockSpec((tm, 2 * D_MODEL), lambda i: (i, 0))
    return pl.pallas_call(
        body, grid=(S // tm,),
        in_specs=[row, _resident(w_o.shape), pl.BlockSpec((tm, D_MODEL), lambda i: (i, P_GA // D_MODEL)),
                  pl.BlockSpec((tm, D_MODEL), lambda i: (i, P_GB // D_MODEL)), wide],
        out_specs=[wide, pl.BlockSpec((2, tm, D_MODEL), lambda i: (0, i, 0))],
        out_shape=[jax.ShapeDtypeStruct((S, 2 * D_MODEL), BF16), jax.ShapeDtypeStruct((2, S, D_MODEL), BF16)],
        name="mix_bwd", compiler_params=_cparams(("parallel",), 40))(*_hbm(dz1b, w_o, proj, proj, yab))


def _conv_gate_bwd(proj, dya_in, conv_w):
    S = proj.shape[0]

    def body(b_ref, c_ref, h_ref, dy_ref, w_ref, o_ref, sm_ref, u_scr, d_scr):
        _zero_pads(u_scr, S)
        _zero_pads(d_scr, S)
        for t in range(0, S, CHUNK):
            u_scr[PAD + t:PAD + t + CHUNK, :] = c_ref[t:t + CHUNK, :].astype(F32) * h_ref[t:t + CHUNK, :].astype(F32)
        w0, w1, w2 = w_ref[0:1, :], w_ref[1:2, :], w_ref[2:3, :]
        zero = jnp.zeros((1, SLAB), F32)
        s_w0, s_w1, s_w2 = zero, zero, zero
        for t in range(0, S, CHUNK):
            um, u0, up = _shifted(u_scr, t)
            dy = dy_ref[t:t + CHUNK, :].astype(F32)
            o_ref[0, t:t + CHUNK, :] = (dy * (w0 * um + w1 * u0 + w2 * up)).astype(BF16)
            dcv = dy * b_ref[t:t + CHUNK, :].astype(F32)
            d_scr[PAD + t:PAD + t + CHUNK, :] = dcv
            s_w0 = s_w0 + jnp.sum(dcv * um, axis=0, keepdims=True)
            s_w1 = s_w1 + jnp.sum(dcv * u0, axis=0, keepdims=True)
            s_w2 = s_w2 + jnp.sum(dcv * up, axis=0, keepdims=True)
        for t in range(0, S, CHUNK):
            dm, d0, dp = _shifted(d_scr, t)
            du = w0 * dp + w1 * d0 + w2 * dm
            o_ref[1, t:t + CHUNK, :] = (du * h_ref[t:t + CHUNK, :].astype(F32)).astype(BF16)
            o_ref[2, t:t + CHUNK, :] = (du * c_ref[t:t + CHUNK, :].astype(F32)).astype(BF16)
        sm_ref[...] = _rows8([s_w0, s_w1, s_w2], SLAB)

    return pl.pallas_call(
        body, grid=(D_CONV // SLAB,),
        in_specs=[_slab_spec(S, P_B), _slab_spec(S, P_C), _slab_spec(S, P_H),
                  pl.BlockSpec((S, SLAB), lambda j: (0, j)), pl.BlockSpec((3, SLAB), lambda j: (0, j))],
        out_specs=[pl.BlockSpec((3, S, SLAB), lambda j: (0, 0, j)), pl.BlockSpec((SUBLANES, SLAB), lambda j: (0, j))],
        out_shape=[jax.ShapeDtypeStruct((3, S, D_CONV), BF16), jax.ShapeDtypeStruct((SUBLANES, D_CONV), F32)],
        scratch_shapes=[pltpu.VMEM((S + 2 * PAD, SLAB), F32)] * 2,
        name="conv_gate_bwd", compiler_params=_cparams(("parallel",), 48))(*_hbm(proj, proj, proj, dya_in, conv_w))


def _comb_bwd(dyab, w_b, comb, lse_tot, *, tm=512):
    S = comb.shape[0]
    widths, dtypes = (GROUP_W, LANES, LANES), (BF16, F32, F32)

    def body(dy_ref, wb_ref, c_ref, lt_ref, e_ref, *rest):
        outs, scr = rest[:3 * N_GROUPS], rest[3 * N_GROUPS:]
        dcb = _dot_nt(dy_ref[...], wb_ref[...]).astype(BF16)
        dc = dcb.astype(F32)
        delta = lax.dot_general(dc * c_ref[...], e_ref[...], (((1,), (1,)), ((), ())),
                                preferred_element_type=F32, precision=lax.Precision.HIGH)
        for k, (val, dtype) in enumerate(zip((dc, lt_ref[...], delta), dtypes)):
            outs[k][0] = val.astype(dtype)
            _to_residue(val, [outs[3 * (1 + j) + k] for j in range(len(DILS))], DILS, tm, dtype,
                        scr[:val.shape[1] // LANES])

    out_specs, out_shape = [], []
    for _, d in GROUPS:
        out_specs += [_res_spec(d, tm, w) for w in widths]
        out_shape += [jax.ShapeDtypeStruct((d, S // d, w), t) for w, t in zip(widths, dtypes)]
    res = pl.pallas_call(
        body, grid=(S // tm,),
        in_specs=[pl.BlockSpec((tm, D_MODEL), lambda i: (i, 1)), _resident(w_b.shape),
                  pl.BlockSpec((tm, GROUP_W), lambda i: (i, 0)), pl.BlockSpec((tm, LANES), lambda i: (i, 0)),
                  _resident((LANES, GROUP_W))],
        out_specs=out_specs, out_shape=out_shape, scratch_shapes=_lane_scratch(tm, GROUP_W),
        name="comb_bwd", compiler_params=_cparams(("parallel",), 32))(*_hbm(dyab, w_b, comb, lse_tot, _expand_heads()))
    return [tuple(res[3 * g:3 * g + 3]) for g in range(N_GROUPS)]


def _attn_bwd(qkv, col0, g, dcomb, lse_tot, delta):
    dil, sub, _ = qkv.shape
    nb = sub // TB
    heads = HEADS_PER_GROUP

    def body(q_ref, kp, kc, kn, vp, vc, vn, do_ref, lse_ref, dl_ref, bias_ref, dq_ref, dk_ref, dv_ref,
             ak, av, dqt_scr, s_scr, dp_scr, ds_scr, p_scr):
        i = pl.program_id(1)

        @pl.when(i == 0)
        def _():
            ak[...] = jnp.zeros_like(ak)
            av[...] = jnp.zeros_like(av)

        @pl.when(i < nb)
        def _():
            kext = _ext_window(kp, kc, kn)
            vext = _ext_window(vp, vc, vn)
            q = q_ref[...] * ATT_SCALE
            do = do_ref[...]
            lse_t, dl_t = lse_ref[...].T, dl_ref[...].T
            for b in range(SUBS):
                rows = slice(b * TQ, (b + 1) * TQ)
                kwin, vwin = kext[b * TQ:(b + 2) * TQ, :], vext[b * TQ:(b + 2) * TQ, :]
                for h in range(heads):
                    s_scr[b * heads + h] = _dot_nt(_pair(kwin, h), _own_lanes(_pair(q[rows], h), h))
                    dp_scr[b * heads + h] = _dot_nt(_pair(vwin, h), _own_lanes(_pair(do[rows], h), h))
            for b in range(SUBS):
                cols = slice(b * TQ, (b + 1) * TQ)
                block = i * SUBS + b
                bias = bias_ref[jnp.where(block == 0, 1, 0) + jnp.where(block == nb * SUBS - 1, 2, 0)]
                for h in range(heads):
                    k = b * heads + h
                    p = jnp.exp(s_scr[k] + _slope(g, h) * bias - lse_t[h:h + 1, cols])
                    ds_scr[k] = (p * (dp_scr[k] - dl_t[h:h + 1, cols])).astype(BF16)
                    p_scr[k] = p.astype(BF16)
            for b in range(SUBS):
                kwin = kext[b * TQ:(b + 2) * TQ, :]
                for h in range(heads):
                    dqt_scr[h * HEAD_DIM:(h + 1) * HEAD_DIM, b * TQ:(b + 1) * TQ] = _own_rows(
                        _dot_tn(_pair(kwin, h), ds_scr[b * heads + h]), h)
            for b in range(SUBS):
                rows = slice(b * TQ, (b + 1) * TQ)
                acc_rows = slice(TB - RADIUS + b * TQ, TB - RADIUS + (b + 2) * TQ)
                for h in range(0, heads, 2):
                    cols = slice(h * HEAD_DIM, (h + 2) * HEAD_DIM)
                    k = b * heads + h
                    q2 = jnp.concatenate([_own_lanes(_pair(q[rows], h), h), _own_lanes(_pair(q[rows], h), h + 1)], axis=0)
                    do2 = jnp.concatenate([_own_lanes(_pair(do[rows], h), h), _own_lanes(_pair(do[rows], h), h + 1)],
                                          axis=0)
                    ak[acc_rows, cols] += _dot(jnp.concatenate([ds_scr[k], ds_scr[k + 1]], axis=1), q2)
                    av[acc_rows, cols] += _dot(jnp.concatenate([p_scr[k], p_scr[k + 1]], axis=1), do2)
            dq_ref[...] = (dqt_scr[...].T * ATT_SCALE).astype(BF16)

        if nb == 1:
            dk_ref[...] = ak[TB:2 * TB, :].astype(BF16)
            dv_ref[...] = av[TB:2 * TB, :].astype(BF16)
        else:
            dk_ref[...] = ak[0:TB, :].astype(BF16)
            dv_ref[...] = av[0:TB, :].astype(BF16)
            used = 2 * TB + RADIUS
            for acc in (ak, av):
                acc[0:used - TB, :] = acc[TB:used, :]
                acc[used - TB:used, :] = jnp.zeros((TB, GROUP_W), F32)

    def spec(col, shift):
        return pl.BlockSpec((None, TB, GROUP_W), lambda r, i: (r, jnp.clip(i + shift, 0, nb - 1), col))

    tok = pl.BlockSpec((None, TB, GROUP_W), lambda r, i: (r, jnp.minimum(i, nb - 1), 0))
    stat = pl.BlockSpec((None, TB, LANES), lambda r, i: (r, jnp.minimum(i, nb - 1), 0))
    dkv_spec = tok if nb == 1 else pl.BlockSpec((None, TB, GROUP_W), lambda r, i: (r, jnp.maximum(i - 1, 0), 0))
    return pl.pallas_call(
        body, grid=(dil, nb + (nb > 1)),
        in_specs=[spec(col0, 0), spec(col0 + 1, -1), spec(col0 + 1, 0), spec(col0 + 1, 1),
                  spec(col0 + 2, -1), spec(col0 + 2, 0), spec(col0 + 2, 1), tok, stat, stat,
                  pl.BlockSpec((4, 2 * TQ, TQ), lambda r, i: (0, 0, 0))],
        out_specs=[tok, dkv_spec, dkv_spec], out_shape=[jax.ShapeDtypeStruct((dil, sub, GROUP_W), BF16)] * 3,
        scratch_shapes=[pltpu.VMEM((3 * TB, GROUP_W), F32)] * 2 + [pltpu.VMEM((GROUP_W, TB), F32)]
        + [pltpu.VMEM((SUBS * heads, 2 * TQ, TQ), F32)] * 2 + [pltpu.VMEM((SUBS * heads, 2 * TQ, TQ), BF16)] * 2,
        name=f"attn_bwd_g{g}", compiler_params=_cparams(("arbitrary", "arbitrary"), 40))(
            *_hbm(*([qkv] * 7), dcomb, lse_tot, delta, _attn_bias_table(g)))


def _in_bwd_ln0(dgated, dqkv, w_nat, w_dil, dz1, x, g0, *, tm=256):
    S = x.shape[0]
    n_gated, n_in = len(dgated), 3 * N_GROUPS

    def body(*refs):
        g_refs, d_refs = refs[:n_gated], refs[n_gated:n_gated + n_in]
        wn_ref, *wd_refs = refs[n_gated + n_in:n_gated + n_in + N_GROUPS]
        dz_ref, x_ref, g_ref, gx_ref, st_ref, *tmp_ref = refs[n_gated + n_in + N_GROUPS:]
        dh = ALPHA * dz_ref[...]
        col = 0
        for ref in g_refs:
            for k in range(ref.shape[0]):
                dh = dh + _dot_nt(ref[k], wn_ref[:, col:col + D_MODEL])
                col += D_MODEL
        for g, (_, d) in enumerate(GROUPS):
            rows = [jnp.concatenate([d_refs[3 * g + k][r] for k in range(3)], axis=1) for r in range(d)]
            w = wn_ref[:, col:col + QKV_W] if d == 1 else wd_refs[g - 1][...]
            res = _dot_nt(jnp.concatenate(rows, axis=0), w)
            if d == 1:
                dh = dh + res
            else:
                n = tm // d
                dh = dh + _from_residue(lambda r: res[r * n:(r + 1) * n, :], d, tm, tmp_ref)
        xhat, rstd = _ln_stats(x_ref[...])
        gx_ref[...] = _ln_bwd(dh, xhat, rstd, g_ref[...])
        upd = _rows8([jnp.sum(dh * xhat, axis=0, keepdims=True), jnp.sum(dh, axis=0, keepdims=True)], D_MODEL)

        @pl.when(pl.program_id(0) == 0)
        def _():
            st_ref[...] = upd

        @pl.when(pl.program_id(0) != 0)
        def _():
            st_ref[...] += upd

    row = pl.BlockSpec((tm, D_MODEL), lambda i: (i, 0))
    g_specs = [pl.BlockSpec((a.shape[0], tm, D_MODEL), lambda i: (0, i, 0)) for a in dgated]
    d_specs = []
    for _, d in GROUPS:
        d_specs += [_res_spec(d, tm, GROUP_W)] * 3
    operands = list(dgated) + [a for grp in dqkv for a in grp] + [w_nat] + list(w_dil) + [dz1, x, g0]
    return pl.pallas_call(
        body, grid=(S // tm,),
        in_specs=g_specs + d_specs + [_resident(w_nat.shape)] + [_resident(w.shape) for w in w_dil]
        + [row, row, pl.BlockSpec((1, D_MODEL), lambda i: (0, 0))],
        out_specs=[row, pl.BlockSpec((SUBLANES, D_MODEL), lambda i: (0, 0))],
        out_shape=[jax.ShapeDtypeStruct((S, D_MODEL), F32), jax.ShapeDtypeStruct((SUBLANES, D_MODEL), F32)],
        scratch_shapes=_lane_scratch(tm, D_MODEL),
        name="in_bwd_ln0", compiler_params=_cparams(("arbitrary",), 52))(*_hbm(*operands))


HBM_SPEC = pl.BlockSpec(memory_space=pltpu.HBM)


def _place():
    x, y, c = lax.axis_index("x"), lax.axis_index("y"), lax.axis_index("c")
    chips = [(1 - x, y), (x, 1 - y), (1 - x, 1 - y)]
    return x, y, c, chips


def _allgather_shards(shards, after, *, name, collective_id):
    n = len(shards)
    per = 6

    def body(*refs):
        ins, outs = refs[:n], refs[n + len(after):2 * n + len(after)]
        send_sems, recv_sems, loc_sems = refs[2 * n + len(after):]
        x, y, c, chips = _place()
        me = 2 * x + y
        sib = (x, y, 1 - c)
        peers = [sib] + [(px, py, c) for px, py in chips]
        barrier = pltpu.get_barrier_semaphore()
        for peer in peers:
            pl.semaphore_signal(barrier, inc=1, device_id=peer, device_id_type=MESH)
        pl.semaphore_wait(barrier, len(peers))

        def rcopy(w, k, src, dst, to):
            return pltpu.make_async_remote_copy(src_ref=src, dst_ref=dst, send_sem=send_sems.at[per * w + k],
                                                recv_sem=recv_sems.at[per * w + k], device_id=to, device_id_type=MESH)

        split = [s.shape[0] == N_CORES for s in shards]
        half = lambda w: c if split[w] else 0
        local, sends = [], []
        for w in range(n):
            cp = pltpu.make_async_copy(ins[w], outs[w].at[me], loc_sems.at[w])
            cp.start()
            local.append(cp)
            for j, (px, py) in enumerate(chips):
                cp = rcopy(w, j, ins[w].at[half(w)], outs[w].at[me, half(w)], (px, py, c))
                cp.start()
                sends.append(cp)
        for w in range(n):
            for j, (px, py) in enumerate(chips):
                slot = outs[w].at[2 * px + py, half(w)]
                rcopy(w, j, slot, slot, (px, py, c)).wait_recv()
                if split[w]:
                    cp = rcopy(w, 3 + j, slot, slot, sib)
                    cp.start()
                    sends.append(cp)
        for w in range(n):
            if split[w]:
                for j, (px, py) in enumerate(chips):
                    slot = outs[w].at[2 * px + py, 1 - c]
                    rcopy(w, 3 + j, slot, slot, sib).wait_recv()
        for cp in sends:
            cp.wait_send()
        for cp in local:
            cp.wait()

    return pl.kernel(
        body, out_type=[jax.ShapeDtypeStruct((N_CHIPS,) + s.shape, s.dtype) for s in shards],
        mesh=plsc.ScalarSubcoreMesh(axis_name="sequencer", num_cores=1),
        scratch_types=[pltpu.SemaphoreType.DMA((per * n,)), pltpu.SemaphoreType.DMA((per * n,)),
                       pltpu.SemaphoreType.DMA((n,))],
        name=name, compiler_params=pltpu.CompilerParams(collective_id=collective_id))(*shards, *after)


def _exchange_grads(grads, *, name, collective_id):
    n = len(grads)
    per = 7

    def body(*refs):
        ins, outs = refs[:n], refs[n:2 * n]
        send_sems, recv_sems, loc_sems = refs[2 * n:]
        x, y, c, chips = _place()
        me = 2 * x + y
        sib = (x, y, 1 - c)
        peers = [sib] + [(px, py, c) for px, py in chips]
        barrier = pltpu.get_barrier_semaphore()
        for peer in peers:
            pl.semaphore_signal(barrier, inc=1, device_id=peer, device_id_type=MESH)
        pl.semaphore_wait(barrier, len(peers))

        def rcopy(w, k, src, dst, to):
            return pltpu.make_async_remote_copy(src_ref=src, dst_ref=dst, send_sem=send_sems.at[per * w + k],
                                                recv_sem=recv_sems.at[per * w + k], device_id=to, device_id_type=MESH)

        local, sends = [], []
        for w in range(n):
            cp = pltpu.make_async_copy(ins[w].at[me], outs[w].at[c, me], loc_sems.at[w])
            cp.start()
            local.append(cp)
            cp = rcopy(w, 0, ins[w].at[me], outs[w].at[c, me], sib)
            cp.start()
            sends.append(cp)
            for j, (px, py) in enumerate(chips):
                cp = rcopy(w, 1 + j, ins[w].at[2 * px + py], outs[w].at[c, me], (px, py, c))
                cp.start()
                sends.append(cp)
        for w in range(n):
            for j, (px, py) in enumerate(chips):
                slot = outs[w].at[c, 2 * px + py]
                rcopy(w, 1 + j, slot, slot, (px, py, c)).wait_recv()
                cp = rcopy(w, 4 + j, slot, slot, sib)
                cp.start()
                sends.append(cp)
        for w in range(n):
            slot = outs[w].at[1 - c, me]
            rcopy(w, 0, slot, slot, sib).wait_recv()
            for j, (px, py) in enumerate(chips):
                slot = outs[w].at[1 - c, 2 * px + py]
                rcopy(w, 4 + j, slot, slot, sib).wait_recv()
        for cp in sends:
            cp.wait_send()
        for cp in local:
            cp.wait()

    return pl.kernel(
        body, out_type=[jax.ShapeDtypeStruct((N_CORES,) + g.shape, g.dtype) for g in grads],
        mesh=plsc.ScalarSubcoreMesh(axis_name="sequencer", num_cores=1),
        scratch_types=[pltpu.SemaphoreType.DMA((per * n,)), pltpu.SemaphoreType.DMA((per * n,)),
                       pltpu.SemaphoreType.DMA((n,))],
        name=name, compiler_params=pltpu.CompilerParams(collective_id=collective_id))(*grads)


def _allgather_small(vec, after):
    def body(v_ref, _, o_ref, send_sems, recv_sems, loc_sem):
        x, y, c = lax.axis_index("x"), lax.axis_index("y"), lax.axis_index("c")
        me = 4 * x + 2 * y + c

        def peer(k):
            flip = lambda v, bit: 1 - v if (k >> bit) & 1 else v
            return flip(x, 2), flip(y, 1), flip(c, 0)

        loc = pltpu.make_async_copy(v_ref, o_ref.at[me], loc_sem)
        loc.start()
        sends = []
        for k in range(1, N_DEV):
            cp = pltpu.make_async_remote_copy(src_ref=v_ref, dst_ref=o_ref.at[me], send_sem=send_sems.at[k - 1],
                                              recv_sem=recv_sems.at[k - 1], device_id=peer(k), device_id_type=MESH)
            cp.start()
            sends.append(cp)
        for k in range(1, N_DEV):
            px, py, pc = peer(k)
            pltpu.make_async_remote_copy(src_ref=v_ref, dst_ref=o_ref.at[4 * px + 2 * py + pc],
                                         send_sem=send_sems.at[k - 1], recv_sem=recv_sems.at[k - 1],
                                         device_id=(px, py, pc), device_id_type=MESH).wait_recv()
        for cp in sends:
            cp.wait_send()
        loc.wait()

    return pl.pallas_call(
        body, in_specs=[HBM_SPEC, HBM_SPEC], out_specs=HBM_SPEC,
        out_shape=jax.ShapeDtypeStruct((N_DEV,) + vec.shape, vec.dtype),
        scratch_shapes=[pltpu.SemaphoreType.DMA((N_DEV - 1,)), pltpu.SemaphoreType.DMA((N_DEV - 1,)),
                        pltpu.SemaphoreType.DMA],
        name="allgather_small")(vec, after)


def _adamw(w, g, m, v):
    m = ADAM_B1 * m + (1.0 - ADAM_B1) * g
    v = ADAM_B2 * v + (1.0 - ADAM_B2) * (g * g)
    m_hat = m / (1.0 - ADAM_B1 ** ADAM_STEP)
    v_hat = v / (1.0 - ADAM_B2 ** ADAM_STEP)
    delta = -ADAM_LR * (m_hat / (jnp.sqrt(v_hat) + ADAM_EPS) + ADAM_WD * w)
    return delta, m, v


def _reduce_adamw(parts, w, m, v, *, tr, name):
    R, C = w.shape

    def body(p_ref, w_ref, m_ref, v_ref, g_ref, d_ref, nm_ref, nv_ref):
        def core_sum(cc):
            s = p_ref[cc, 0].astype(F32)
            for k in range(1, N_CHIPS):
                s = s + p_ref[cc, k].astype(F32)
            return s

        g = core_sum(0) + core_sum(1)
        delta, nm, nv = _adamw(w_ref[...], g, m_ref[...], v_ref[...])
        g_ref[...] = g
        d_ref[...] = delta
        nm_ref[...] = nm
        nv_ref[...] = nv

    blk = pl.BlockSpec((tr, C), lambda i: (i, 0))
    return pl.pallas_call(
        body, grid=(R // tr,),
        in_specs=[pl.BlockSpec((N_CORES, N_CHIPS, tr, C), lambda i: (0, 0, i, 0)), blk, blk, blk],
        out_specs=[blk] * 4, out_shape=[jax.ShapeDtypeStruct((R, C), F32)] * 4,
        name=name, compiler_params=_cparams(("parallel",), 40))(*_hbm(parts, w, m, v))


def _reduce_adamw_vectors(allv, offs, ws, ms, vs):
    n = len(ws)

    def body(a_ref, *refs):
        w_refs, m_refs, v_refs = refs[:n], refs[n:2 * n], refs[2 * n:3 * n]
        tot_ref, outs = refs[3 * n], refs[3 * n + 1:]
        s = a_ref[0]
        for d in range(1, N_DEV):
            s = s + a_ref[d]
        tot_ref[...] = s
        for k in range(n):
            g = s[:, offs[k]:offs[k] + w_refs[k].shape[1]]
            delta, nm, nv = _adamw(w_refs[k][...], g, m_refs[k][...], v_refs[k][...])
            for ref, val in zip(outs[4 * k:4 * k + 4], (g, delta, nm, nv)):
                ref[...] = val

    out_shape = [jax.ShapeDtypeStruct(allv.shape[1:], F32)]
    for w in ws:
        out_shape += [jax.ShapeDtypeStruct(w.shape, F32)] * 4
    res = pl.pallas_call(body, out_shape=out_shape, name="reduce_adamw_vectors",
                         compiler_params=_cparams((), 40))(allv, *ws, *ms, *vs)
    return res[0], [tuple(res[1 + 4 * k:5 + 4 * k]) for k in range(n)]


def _adamw_taps(ws, gs, ms, vs):
    n = len(ws)

    def body(*refs):
        outs = refs[4 * n:]
        for k in range(n):
            res = _adamw(refs[k][...], refs[n + k][...], refs[2 * n + k][...], refs[3 * n + k][...])
            for ref, val in zip(outs[3 * k:3 * k + 3], res):
                ref[...] = val

    out_shape = []
    for w in ws:
        out_shape += [jax.ShapeDtypeStruct(w.shape, F32)] * 3
    res = pl.pallas_call(body, out_shape=out_shape, name="adamw_taps")(*ws, *gs, *ms, *vs)
    return [tuple(res[3 * k:3 * k + 3]) for k in range(n)]


def _segments(widths, col, w):
    out, start = [], 0
    for k, wk in enumerate(widths):
        lo, hi = max(col, start), min(col + w, start + wk)
        if lo < hi:
            out.append((k, lo - start, hi - lo))
        start += wk
    return out


def _permute_w_in(w3, widths, tr=256):
    copies = [(s, c + done, k, off, wd) for s, c, pc, w in _col_runs()
              for done, (k, off, wd) in _offsets(_segments(widths, pc, w))]
    rows = w3.shape[1]

    def body(w_ref, *o_refs):
        for s, c, k, off, wd in copies:
            o_refs[k][:, off:off + wd] = w_ref[s, :, c:c + wd]

    return pl.pallas_call(
        body, grid=(rows // tr,), in_specs=[pl.BlockSpec((N_CHIPS, tr, w3.shape[2]), lambda i: (0, i, 0))],
        out_specs=[pl.BlockSpec((tr, wk), lambda i: (i, 0)) for wk in widths],
        out_shape=[jax.ShapeDtypeStruct((rows, wk), w3.dtype) for wk in widths],
        compiler_params=_cparams(("arbitrary",), 32), name="permute_w_in")(w3)


def _unpermute_w_in(pieces, tr=256):
    widths = [a.shape[2] for a, _ in pieces]
    copies = [(s, c + done, k, off, wd) for s, c, pc, w in _col_runs()
              for done, (k, off, wd) in _offsets(_segments(widths, pc, w))]
    rows, n = pieces[0][0].shape[1], N_IN // N_CHIPS

    def body(*refs):
        o_ref = refs[-1]
        for s, c, k, off, wd in copies:
            o_ref[s, :, c:c + wd] = refs[k][:, off:off + wd]

    return pl.pallas_call(
        body, grid=(rows // tr,),
        in_specs=[pl.BlockSpec((None, tr, a.shape[2]), lambda i, plane=plane: (plane, i, 0)) for a, plane in pieces],
        out_specs=pl.BlockSpec((N_CHIPS, tr, n), lambda i: (0, i, 0)),
        out_shape=jax.ShapeDtypeStruct((N_CHIPS, rows, n), pieces[0][0].dtype),
        compiler_params=_cparams(("arbitrary",), 32), name="unpermute_grad_w_in")(*[a for a, _ in pieces])


def _offsets(segments):
    out, done = [], 0
    for seg in segments:
        out.append((done, seg))
        done += seg[2]
    return out


def _pack(pieces):
    arrays, copies, offs, n = [], [], [], 0
    for runs in pieces:
        offs.append(n)
        for arr, r, c, w in runs:
            assert arr.ndim == 2 and arr.dtype == F32 and w % LANES == 0 and c % LANES == 0
            k = next((i for i, a in enumerate(arrays) if a is arr), None)
            if k is None:
                arrays.append(arr)
                k = len(arrays) - 1
            copies.append((k, r, c, w, n))
            n += w

    def body(*refs):
        o_ref = refs[-1]
        for k, r, c, w, dst in copies:
            o_ref[:, dst:dst + w] = refs[k][r:r + 1, c:c + w]

    return pl.pallas_call(body, out_shape=jax.ShapeDtypeStruct((1, n), F32), name="pack_small")(*arrays), offs


def _local_step(x, target, p, wfull, on_ready=lambda group: None, before_ln0=()):
    S = x.shape[0]
    dils = [d for _, d in GROUPS]

    h0, h0b, *h0_res = _ln0_fwd(x, p["ln0_g"], p["ln0_b"], before_ln0)
    h0_rows = [h0b] + [h.reshape(S, D_MODEL) for h in h0_res]

    if isinstance(wfull, dict):
        w_in3, pending = wfull["w_in"], None
    else:
        w_in3, launch_rest, assemble = wfull
        w_in3, h0b = lax.optimization_barrier((w_in3, h0b))
        pending = launch_rest(h0b)

    runs = _col_runs()
    b_blocks = p["b_in"].reshape(N_BLK, GROUP_W)
    n_tok = PERM.index(N_BLK - 1) + 1
    q0 = PERM[n_tok]
    assert PERM[n_tok:] == tuple(q0 + N_GROUPS * t + g for g in range(N_GROUPS) for t in range(3))
    assert PERM[:n_tok] == tuple(range(q0)) + tuple(range(q0 + 3 * N_GROUPS, N_BLK))
    b_qkv = b_blocks[q0:q0 + 3 * N_GROUPS].reshape(3, N_GROUPS, GROUP_W).transpose(1, 0, 2)
    b_perm = jnp.concatenate([b_blocks[:q0].reshape(-1), b_blocks[q0 + 3 * N_GROUPS:].reshape(-1),
                              b_qkv.reshape(-1)]).reshape(1, N_IN)
    b_nat = b_perm[:, :N_NAT]
    qkv_cols = [slice(P_Q0 + g * QKV_W, P_Q0 + (g + 1) * QKV_W) for g in range(N_GROUPS)]
    w_nat, *w_dil = _permute_w_in(w_in3, [N_NAT] + [QKV_W] * (N_GROUPS - 1))
    w_qkv = [None] + w_dil

    proj = _mm_nn(h0b, w_nat, b_nat, tm=512, tn=N_NAT // 2, out_dtype=BF16, name="proj")
    qkv = [proj[None]]
    for g in range(1, N_GROUPS):
        t = _mm_nn(h0_rows[g], w_qkv[g], b_perm[:, qkv_cols[g]], tm=512, tn=QKV_W, out_dtype=BF16, name=f"proj_qkv{g}")
        qkv.append(t.reshape(dils[g], S // dils[g], QKV_W))
    if pending is not None:
        pending, qkv = lax.optimization_barrier((pending, qkv))
        proj = qkv[0][0]
        wfull = assemble(pending)
    w_up3 = wfull["w_up"]
    w_a, w_o, w_down, w_b = wfull["w_a"], wfull["w_o"], wfull["w_down"], wfull["w_b"]
    conv_w, ffn_conv_w = wfull["conv_w"], wfull["ffn_conv_w"]
    col0 = [P_Q0 // GROUP_W] + [0] * (N_GROUPS - 1)
    ya_in = _conv_gate_fwd(proj, conv_w)
    att = [_attn_fwd(qkv[g], col0[g], g) for g in range(N_GROUPS)]
    comb, comb_b, lse_tot = _attn_combine([a[0] for a in att], [a[1] for a in att])
    yab, mixin = _branch_mix(ya_in, comb_b, w_a, w_b, proj)
    xhat1, rstd1, h1b = _mix_ln1(mixin, w_o, p["b_o"], h0, p["ln1_g"], p["ln1_b"])
    up = _mm_nn(h1b, w_up3, p["b_up"], tm=512, tn=2 * w_up3.shape[2], out_dtype=BF16, name="up")
    f = _ffn_conv_fwd(up, ffn_conv_w, p["ffn_conv_b"])
    dz2, dz2b, st2 = _down_ln2_loss(f, w_down, p["b_down"], xhat1, p["ln1_g"], p["ln1_b"],
                                    p["ln2_g"], p["ln2_b"], target)

    gw = {}
    gw["w_down"] = _mm_tn(f, dz2b, n_out=1, tn=D_MODEL, ts=1024, g_block=(1024, D_MODEL),
                          g_map=lambda j, s: (s, 0), name="grad_w_down").reshape(N_CHIPS, D_FF // N_CHIPS, D_MODEL)
    df = _mm_nt(dz2b, w_down, tm=512, name="df")
    dup, sm_ffn = _ffn_conv_bwd(up, df, ffn_conv_w, p["ffn_conv_b"])
    gw["w_up"] = _mm_tn(h1b, dup, n_out=dup.shape[0], tn=D_FF, ts=1024, g_block=(None, 1024, D_FF),
                        g_map=lambda j, s: (j, s, 0), split=D_FF // w_up3.shape[2], name="grad_w_up")
    exchanged = on_ready({n: gw[n] for n in ("w_down", "w_up")}) or {}
    dz1, dz1b, st1 = _up_bwd_ln1(dup, w_up3, dz2, xhat1, rstd1, p["ln1_g"])

    gw["w_o"] = _mm_tn(mixin, dz1b, n_out=1, tn=D_MODEL, ts=512, g_block=(512, D_MODEL),
                       g_map=lambda j, s: (s, 0), name="grad_w_o").reshape(N_CHIPS, D_MODEL // N_CHIPS, D_MODEL)
    dyab, dgab = _mix_bwd(dz1b, w_o, proj, yab)
    gw["w_a"] =_mm_tn(ya_in, dyab, n_out=1, tn=D_MODEL, ts=512, g_block=(512, D_MODEL),
                       g_map=lambda j, s: (s, 0), name="grad_w_a").reshape(N_CHIPS, D_CONV // N_CHIPS, D_MODEL)
    gw_b = _mm_tn(comb_b, dyab, n_out=1, tn=D_MODEL, ts=1024, g_block=(1024, D_MODEL),
                  g_map=lambda j, s: (s, 1), name="grad_w_b")
    gw["w_b"] = gw_b.reshape(GROUP_W, N_CHIPS, D_MODEL // N_CHIPS).transpose(1, 0, 2)
    exchanged_mix = on_ready({n: gw[n] for n in ("w_o", "w_a", "w_b")}) or {}
    dya_in = _mm_nt(dyab, w_a, tm=512, a_col=0, name="dya_in")
    exchanged, dya_in = lax.optimization_barrier((exchanged, dya_in))
    dbch, sm_conv = _conv_gate_bwd(proj, dya_in, conv_w)
    att_stats = _comb_bwd(dyab, w_b, comb, lse_tot)
    exchanged_mix, att_stats = lax.optimization_barrier((exchanged_mix, att_stats))
    exchanged.update(exchanged_mix)
    dqkv = [_attn_bwd(qkv[g], col0[g], g, *att_stats[g]) for g in range(N_GROUPS)]

    w_pieces, b_pieces, b_rows = [], [], []
    for nm, planes in (("bch", dbch), ("gab", dgab)):
        pw, pc = _mm_tn(h0b, planes, n_out=planes.shape[0], tn=D_MODEL, ts=1024, g_block=(None, 1024, D_MODEL),
                        g_map=lambda j, s: (j, s, 0), colsum=True, name="grad_w_in_" + nm)
        w_pieces.extend((pw, k) for k in range(planes.shape[0]))
        b_pieces.append(pc[0])
        b_rows.append(pc)
    for g in range(N_GROUPS):
        pw, pc = _mm_tn_cat(h0_rows[g], [a.reshape(S, GROUP_W) for a in dqkv[g]], ts=1024, name=f"grad_w_in_qkv{g}")
        w_pieces.append((pw[None], 0))
        b_pieces.append(pc[0])
        b_rows.append(pc)
    gw["w_in"] = _unpermute_w_in(w_pieces)
    exchanged.update(on_ready({"w_in": gw["w_in"]}) or {})
    db_blocks = jnp.concatenate(b_pieces).reshape(N_BLK, GROUP_W)
    grad_b_in = jnp.concatenate([db_blocks[b] for b in INV_PERM])

    grad_x, st0 = _in_bwd_ln0([dbch, dgab], dqkv, w_nat, w_qkv[1:], dz1, x, p["ln0_g"])

    small = {
        "loss": st2[2:3, 0:1],
        "ln0_g": st0[0], "ln0_b": st0[1], "b_in": grad_b_in, "conv_w": sm_conv[0:3],
        "b_o": st1[2], "ln1_g": st1[0], "ln1_b": st1[1],
        "b_up": jnp.concatenate([sm_ffn[0], sm_ffn[1]]), "ffn_conv_w": sm_ffn[3:6], "ffn_conv_b": sm_ffn[2],
        "b_down": st2[3], "ln2_g": st2[0], "ln2_b": st2[1],
    }
    row = lambda a, r: [(a, r, 0, a.shape[1])]
    starts = [sum(a.shape[1] for a in b_rows[:k]) for k in range(len(b_rows))]
    b_in_src = []
    for b in range(N_BLK):
        col = INV_PERM[b] * GROUP_W
        k = max(i for i, s0 in enumerate(starts) if s0 <= col)
        b_in_src.append((b_rows[k], 0, col - starts[k], GROUP_W))
    small["_src"] = {
        "loss": [(st2, 2, 0, LANES)],
        "ln0_g": row(st0, 0), "ln0_b": row(st0, 1), "b_in": b_in_src, "conv_w": sum((row(sm_conv, r) for r in range(3)), []),
        "b_o": row(st1, 2), "ln1_g": row(st1, 0), "ln1_b": row(st1, 1),
        "b_up": row(sm_ffn, 0) + row(sm_ffn, 1), "ffn_conv_w": sum((row(sm_ffn, r) for r in (3, 4, 5)), []),
        "ffn_conv_b": row(sm_ffn, 2), "b_down": row(st2, 3), "ln2_g": row(st2, 0), "ln2_b": row(st2, 1),
    }
    return grad_x, exchanged or gw, small


BIG =("w_in", "w_a", "w_b", "w_o", "w_up", "w_down")
CONV = ("conv_w", "ffn_conv_w")
VECS = ("ln0_g", "ln0_b", "b_in", "b_o", "ln1_g", "ln1_b", "b_up", "ffn_conv_b", "b_down", "ln2_g", "ln2_b")
ORDER = ("ln0_g", "ln0_b", "w_in", "b_in", "conv_w", "w_a", "w_b", "w_o", "b_o", "ln1_g", "ln1_b", "w_up", "b_up",
         "ffn_conv_w", "ffn_conv_b", "w_down", "b_down", "ln2_g", "ln2_b")
SMALL_ORDER = ("loss",) + VECS + CONV


def _step(x, target, W, Mo, Vo):
    x2, t2 = x[0], target[0]
    big2 = {n: W[n][0] for n in BIG}
    halves = lambda a: a.astype(BF16).reshape(N_CORES, a.shape[0] // N_CORES, a.shape[1])
    whole = lambda g: g.reshape(N_CHIPS, g.shape[1] * g.shape[2], g.shape[3])
    later = tuple(n for n in BIG if n != "w_in")
    w_in_halves = halves(big2["w_in"])
    first = _allgather_shards([w_in_halves], [], name="allgather_w_in", collective_id=1)

    def launch_rest(h0b):
        return _allgather_shards([halves(big2[n]) for n in later] + [W[n] for n in CONV], [h0b],
                                 name="allgather_rest", collective_id=2)

    def assemble(rest):
        gathered = {n: whole(g) for n, g in zip(later + CONV, rest)}
        return {
            "w_up": gathered["w_up"],
            "w_a": gathered["w_a"].reshape(D_CONV, D_MODEL), "w_o": gathered["w_o"].reshape(D_MODEL, D_MODEL),
            "w_down": gathered["w_down"].reshape(D_FF, D_MODEL),
            "w_b": gathered["w_b"].transpose(1, 0, 2).reshape(GROUP_W, D_MODEL),
            "conv_w": gathered["conv_w"].transpose(1, 0, 2).reshape(3, D_CONV),
            "ffn_conv_w": gathered["ffn_conv_w"].transpose(1, 0, 2).reshape(3, D_FF),
        }

    pvec = {n: W[n].reshape(1, -1) for n in VECS}

    exchange_ids = iter((3, 4, 5))

    def exchange(group):
        names = tuple(group)
        res = _exchange_grads([group[n] for n in names], name="exchange_" + "_".join(names),
                              collective_id=next(exchange_ids))
        return dict(zip(names, res))

    grad_x, parts, small = _local_step(x2, t2, pvec, (whole(first[0]), launch_rest, assemble), exchange,
                                       before_ln0=[w_in_halves])
    out = {}
    for n in BIG:
        tr = {"w_in": 128, "w_up": 128, "w_b": 128}.get(n, big2[n].shape[0] // 4)
        g, d, nm, nv = _reduce_adamw(parts[n], big2[n], Mo[n][0], Vo[n][0], tr=tr, name="adamw_" + n)
        out[n] = tuple(a[None] for a in (g, d, nm, nv))

    vec, offs = _pack([small["_src"][n] for n in SMALL_ORDER])
    off = dict(zip(SMALL_ORDER, offs))
    row = lambda a: a.reshape(1, -1)
    allv = _allgather_small(vec, parts["w_in"])
    tot, vec_out = _reduce_adamw_vectors(allv, [off[n] for n in VECS], [row(W[n]) for n in VECS],
                                         [row(Mo[n]) for n in VECS], [row(Vo[n]) for n in VECS])
    for n, res in zip(VECS, vec_out):
        out[n] = tuple(a.reshape(W[n].shape) for a in res)
    loss = tot[0, off["loss"]]
    chip = 2 * lax.axis_index("x") + lax.axis_index("y")
    taps_g = []
    for n in CONV:
        width = W[n].shape[2]
        full = lax.slice(tot, (0, off[n]), (1, off[n] + 3 * N_CHIPS * width)).reshape(3, N_CHIPS * width)
        taps_g.append(lax.dynamic_slice_in_dim(full, chip * width, width, axis=1))
    taps_out = _adamw_taps([W[n][0] for n in CONV], taps_g, [Mo[n][0] for n in CONV], [Vo[n][0] for n in CONV])
    for n, g, res in zip(CONV, taps_g, taps_out):
        out[n] = tuple(a[None] for a in (g,) + res)

    res = [loss, grad_x[None]]
    for k in range(4):
        res += [out[n][k] for n in ORDER]
    return tuple(res)


def kernel(x, ln0_g, ln0_b, w_in, b_in, conv_w, w_a, w_b, w_o, b_o, ln1_g, ln1_b, w_up, b_up, ffn_conv_w, ffn_conv_b, w_down, b_down, ln2_g, ln2_b, loss_target, m_ln0_g, m_ln0_b, m_w_in, m_b_in, m_conv_w, m_w_a, m_w_b, m_w_o, m_b_o, m_ln1_g, m_ln1_b, m_w_up, m_b_up, m_ffn_conv_w, m_ffn_conv_b, m_w_down, m_b_down, m_ln2_g, m_ln2_b, v_ln0_g, v_ln0_b, v_w_in, v_b_in, v_conv_w, v_w_a, v_w_b, v_w_o, v_b_o, v_ln1_g, v_ln1_b, v_w_up, v_b_up, v_ffn_conv_w, v_ffn_conv_b, v_w_down, v_b_down, v_ln2_g, v_ln2_b):
    W = dict(zip(ORDER, (ln0_g, ln0_b, w_in, b_in, conv_w, w_a, w_b, w_o, b_o, ln1_g, ln1_b, w_up, b_up,
                         ffn_conv_w, ffn_conv_b, w_down, b_down, ln2_g, ln2_b)))
    Mo = dict(zip(ORDER, (m_ln0_g, m_ln0_b, m_w_in, m_b_in, m_conv_w, m_w_a, m_w_b, m_w_o, m_b_o, m_ln1_g, m_ln1_b,
                          m_w_up, m_b_up, m_ffn_conv_w, m_ffn_conv_b, m_w_down, m_b_down, m_ln2_g, m_ln2_b)))
    Vo = dict(zip(ORDER, (v_ln0_g, v_ln0_b, v_w_in, v_b_in, v_conv_w, v_w_a, v_w_b, v_w_o, v_b_o, v_ln1_g, v_ln1_b,
                          v_w_up, v_b_up, v_ffn_conv_w, v_ffn_conv_b, v_w_down, v_b_down, v_ln2_g, v_ln2_b)))
    return _step(x, loss_target, W, Mo, Vo)
```
